```python
import math
import jax, jax.numpy as jnp
from jax import lax
import numpy as np

D_MODEL = 1024
BATCH = 16
SEQ = 2048
DEPTH = 2

EPS = 1e-6
PLE_DIM = 256
D_FF = 2816
CONV_W = 4
HEAD_DIM = 64
LRU_WIDTH = D_MODEL // 4
LRU_BLOCKS = LRU_WIDTH // HEAD_DIM
LRU_BLOCK = LRU_WIDTH // LRU_BLOCKS
LRU_C = 8.0
ATT_WIDTH = D_MODEL // 2
ATT_HEADS = ATT_WIDTH // HEAD_DIM
ATT_KV_HEADS = 2
ATT_GROUP = ATT_HEADS // ATT_KV_HEADS
KV_WIDTH = ATT_KV_HEADS * HEAD_DIM
WINDOW = 128
BLOCK_Q = 128
REL_BUCKETS = 32
REL_MAX_DIST = 128
DN_WIDTH = D_MODEL // 4
DN_HEADS = DN_WIDTH // HEAD_DIM
DN_DK = HEAD_DIM
DN_DV = HEAD_DIM
DN_QK = DN_HEADS * DN_DK
DN_CHUNK = 64
D_MIX = LRU_WIDTH + ATT_WIDTH + DN_WIDTH
IN_SPLITS = (LRU_WIDTH, LRU_WIDTH,
             ATT_WIDTH, KV_WIDTH, KV_WIDTH,
             DN_QK, DN_QK, DN_WIDTH, DN_WIDTH,
             DN_HEADS, DN_HEADS)
D_IN = sum(IN_SPLITS)

kernel_name = "hymba_style_lru_swa_deltanet_macaron"


def rms_norm(x, g):
    xf = x.astype(jnp.float32)
    y = xf * lax.rsqrt(jnp.mean(xf * xf, axis=-1, keepdims=True) + EPS)
    return (y * g.astype(jnp.float32)).astype(x.dtype)


def swiglu(x, w_gate, w_up, w_down):
    return (jax.nn.silu(x @ w_gate) * (x @ w_up)) @ w_down


def causal_dwconv(x, w, b=None):
    K = w.shape[0]
    S = x.shape[1]
    xp = jnp.pad(x, ((0, 0), (K - 1, 0), (0, 0)))
    y = xp[:, 0:S] * w[0]
    for k in range(1, K):
        y = y + xp[:, k:k + S] * w[k]
    if b is not None:
        y = y + b
    return y


def split_points():
    return np.cumsum(np.array(IN_SPLITS))[:-1].tolist()


def rg_lru(x, w_a, b_a, w_x, b_x, lam):
    B, S, _ = x.shape
    xb = x.reshape(B, S, LRU_BLOCKS, LRU_BLOCK)
    r = jax.nn.sigmoid(jnp.einsum('bshi,hij->bshj', xb, w_a).reshape(B, S, LRU_WIDTH) + b_a)
    i = jax.nn.sigmoid(jnp.einsum('bshi,hij->bshj', xb, w_x).reshape(B, S, LRU_WIDTH) + b_x)
    log_a = -LRU_C * r.astype(jnp.float32) * jax.nn.softplus(-lam.astype(jnp.float32))
    a = jnp.exp(log_a)
    u = jnp.sqrt(-jnp.expm1(2.0 * log_a)) * (i * x).astype(jnp.float32)

    def combine(left, right):
        a1, b1 = left
        a2, b2 = right
        return a1 * a2, a2 * b1 + b2

    _, h = lax.associative_scan(combine, (a, u), axis=1)
    return h.astype(x.dtype)


def rel_bucket(dist):
    max_exact = REL_BUCKETS // 2
    large = max_exact + (jnp.log(jnp.maximum(dist, 1).astype(jnp.float32) / max_exact)
                         / math.log(REL_MAX_DIST / max_exact)
                         * (REL_BUCKETS - max_exact)).astype(jnp.int32)
    large = jnp.minimum(large, REL_BUCKETS - 1)
    return jnp.where(dist < max_exact, dist, large)


def swa_attention(q, k, v, sinks, rel_bias):
    B, S = q.shape[:2]
    NB = S // BLOCK_Q
    qb = q.reshape(B, NB, BLOCK_Q, ATT_KV_HEADS, ATT_GROUP, HEAD_DIM)

    def with_prev(t):
        tb = t.reshape(B, NB, BLOCK_Q, ATT_KV_HEADS, HEAD_DIM)
        prev = jnp.pad(tb, ((0, 0), (1, 0), (0, 0), (0, 0), (0, 0)))[:, :-1]
        return jnp.concatenate([prev, tb], axis=2)

    kb, vb = with_prev(k), with_prev(v)
    qi = jnp.arange(BLOCK_Q)[:, None]
    kj = jnp.arange(2 * BLOCK_Q)[None, :]
    dist = BLOCK_Q + qi - kj
    band = (dist >= 0) & (dist < WINDOW)
    blk = jnp.arange(NB)[:, None, None]
    valid = band[None] & ((blk > 0) | (kj[None] >= BLOCK_Q))
    bias = rel_bias.astype(jnp.float32)[rel_bucket(jnp.maximum(dist, 0))]
    bias = bias.transpose(2, 0, 1).reshape(ATT_KV_HEADS, ATT_GROUP, BLOCK_Q, 2 * BLOCK_Q)
    s = jnp.einsum('bnikgd,bnjkd->bnkgij', qb, kb).astype(jnp.float32) * (HEAD_DIM ** -0.5) + bias
    s = jnp.where(valid[None, :, None, None], s, -jnp.inf)
    sink = sinks.astype(jnp.float32).reshape(ATT_KV_HEADS, ATT_GROUP)[:, :, None, None]
    m = jnp.maximum(jnp.max(s, axis=-1, keepdims=True), sink)
    e = jnp.exp(s - m)
    probs = e / (jnp.sum(e, axis=-1, keepdims=True) + jnp.exp(sink - m))
    o = jnp.einsum('bnkgij,bnjkd->bnikgd', probs.astype(v.dtype), vb)
    return o.reshape(B, S, ATT_WIDTH)


def l2norm(t):
    return t * lax.rsqrt(jnp.sum(t * t, axis=-1, keepdims=True) + EPS)


def gated_delta_rule(q, k, v, g, beta):
    B, S, H, DK = k.shape
    DV = v.shape[-1]
    C = DN_CHUNK
    NC = S // C
    f32 = jnp.float32
    q = l2norm(q.astype(f32)) * (DK ** -0.5)
    k = l2norm(k.astype(f32))

    def chunks(t):
        return t.reshape(B, NC, C, H, -1).transpose(1, 0, 3, 2, 4)

    qc, kc, vc = chunks(q), chunks(k), chunks(v.astype(f32))
    gc = g.astype(f32).reshape(B, NC, C, H).transpose(1, 0, 3, 2)
    bc = beta.astype(f32).reshape(B, NC, C, H).transpose(1, 0, 3, 2)
    gcum = jnp.cumsum(gc, axis=-1)
    tril = jnp.tril(jnp.ones((C, C), dtype=bool))
    strict = jnp.tril(jnp.ones((C, C), dtype=bool), -1)
    decay = jnp.exp(jnp.where(tril, gcum[..., :, None] - gcum[..., None, :], -jnp.inf))
    k_beta = kc * bc[..., None]
    v_beta = vc * bc[..., None]
    Lmat = jnp.where(strict, jnp.einsum('...id,...jd->...ij', k_beta, kc) * decay, 0.0)
    eye = jnp.eye(C, dtype=f32)
    T = lax.linalg.triangular_solve(Lmat + eye, jnp.broadcast_to(eye, Lmat.shape),
                                    left_side=True, lower=True, unit_diagonal=True)
    u = jnp.einsum('...ij,...jd->...id', T, v_beta)
    w = jnp.einsum('...ij,...jd->...id', T, k_beta * jnp.exp(gcum)[..., None])

    def step(state, xs):
        q_i, k_i, u_i, w_i, g_i, dec_i = xs
        attn = jnp.einsum('bhid,bhjd->bhij', q_i, k_i) * dec_i
        v_new = u_i - jnp.einsum('bhcd,bhde->bhce', w_i, state)
        o = (jnp.einsum('bhcd,bhde->bhce', q_i * jnp.exp(g_i)[..., None], state)
             + jnp.einsum('bhij,bhje->bhie', attn, v_new))
        g_last = g_i[..., -1]
        k_dec = k_i * jnp.exp(g_last[..., None] - g_i)[..., None]
        state = state * jnp.exp(g_last)[..., None, None] + jnp.einsum('bhcd,bhce->bhde', k_dec, v_new)
        return state, o

    state0 = jnp.zeros((B, H, DK, DV), f32)
    _, o = lax.scan(step, state0, (qc, kc, u, w, gcum, decay))
    return o.transpose(1, 0, 3, 2, 4).reshape(B, S, H, DV)


def hybrid_mixer(xn, w_in, lru_conv_w, lru_conv_b, lru_w_a, lru_b_a, lru_w_x, lru_b_x, lru_lambda,
                 attn_sinks, rel_bias, dn_conv_w, dn_a_log, dn_dt_bias, dn_norm, w_out):
    B, S, _ = xn.shape
    u = xn @ w_in
    (lru_x, lru_gate, att_q, att_k, att_v, dn_q, dn_k, dn_v, dn_z, dn_b, dn_a) = jnp.split(
        u, split_points(), axis=-1)
    xr = causal_dwconv(lru_x, lru_conv_w, lru_conv_b)
    y_lru = jax.nn.gelu(lru_gate) * rg_lru(xr, lru_w_a, lru_b_a, lru_w_x, lru_b_x, lru_lambda)
    y_att = swa_attention(att_q.reshape(B, S, ATT_HEADS, HEAD_DIM),
                          att_k.reshape(B, S, ATT_KV_HEADS, HEAD_DIM),
                          att_v.reshape(B, S, ATT_KV_HEADS, HEAD_DIM),
                          attn_sinks, rel_bias)
    qkv = jax.nn.silu(causal_dwconv(jnp.concatenate([dn_q, dn_k, dn_v], axis=-1), dn_conv_w))
    q, k, v = jnp.split(qkv, [DN_QK, 2 * DN_QK], axis=-1)
    beta = jax.nn.sigmoid(dn_b.astype(jnp.float32))
    g = -jnp.exp(dn_a_log.astype(jnp.float32)) * jax.nn.softplus(
        dn_a.astype(jnp.float32) + dn_dt_bias.astype(jnp.float32))
    o = gated_delta_rule(q.reshape(B, S, DN_HEADS, DN_DK), k.reshape(B, S, DN_HEADS, DN_DK),
                         v.reshape(B, S, DN_HEADS, DN_DV), g, beta)
    z = dn_z.reshape(B, S, DN_HEADS, DN_DV).astype(jnp.float32)
    o = (o * lax.rsqrt(jnp.mean(o * o, axis=-1, keepdims=True) + EPS)
         * dn_norm.astype(jnp.float32) * jax.nn.silu(z))
    y_dn = o.reshape(B, S, DN_WIDTH).astype(xn.dtype)
    return jnp.concatenate([y_lru, y_att, y_dn], axis=-1) @ w_out


def _fwd_setup_inputs(seed: int = 0) -> dict:
    key = jax.random.key(seed)
    ks = list(jax.random.split(key, 48))

    def nrm(shape, scale):
        return scale * jax.random.normal(ks.pop(), shape, jnp.float32)

    def gain(shape):
        return 1.0 + 0.02 * jax.random.normal(ks.pop(), shape, jnp.float32)

    L, D = DEPTH, D_MODEL
    x = nrm((BATCH, SEQ, D), 1.0)
    p = nrm((DEPTH, BATCH, SEQ, PLE_DIM), 1.0)
    a_c = jax.random.uniform(ks.pop(), (L, LRU_WIDTH), jnp.float32, 0.9, 0.999)
    s = a_c ** (1.0 / LRU_C)
    lru_lambda = jnp.log(s) - jnp.log1p(-s)
    dn_a_log = jnp.log(jax.random.uniform(ks.pop(), (L, DN_HEADS), jnp.float32, 1.0, 16.0))
    dt = jnp.exp(jax.random.uniform(ks.pop(), (L, DN_HEADS), jnp.float32,
                                    math.log(1e-3), math.log(1e-1)))
    dn_dt_bias = dt + jnp.log(-jnp.expm1(-dt))
    return {
        "x": x,
        "p": p,
        "ffn1_norm": gain((L, D)),
        "ffn1_w_gate": nrm((L, D, D_FF), D ** -0.5),
        "ffn1_w_up": nrm((L, D, D_FF), D ** -0.5),
        "ffn1_w_down": nrm((L, D_FF, D), D_FF ** -0.5),
        "mix_norm": gain((L, D)),
        "w_in": nrm((L, D, D_IN), D ** -0.5),
        "lru_conv_w": nrm((L, CONV_W, LRU_WIDTH), CONV_W ** -0.5),
        "lru_conv_b": nrm((L, LRU_WIDTH), 0.01),
        "lru_w_a": nrm((L, LRU_BLOCKS, LRU_BLOCK, LRU_BLOCK), LRU_BLOCK ** -0.5),
        "lru_b_a": nrm((L, LRU_WIDTH), 0.01),
        "lru_w_x": nrm((L, LRU_BLOCKS, LRU_BLOCK, LRU_BLOCK), LRU_BLOCK ** -0.5),
        "lru_b_x": nrm((L, LRU_WIDTH), 0.01),
        "lru_lambda": lru_lambda,
        "attn_sinks": nrm((L, ATT_HEADS), 0.5),
        "rel_bias": nrm((REL_BUCKETS, ATT_HEADS), 0.5),
        "dn_conv_w": nrm((L, CONV_W, 2 * DN_QK + DN_WIDTH), CONV_W ** -0.5),
        "dn_a_log": dn_a_log,
        "dn_dt_bias": dn_dt_bias,
        "dn_norm": gain((L, DN_DV)),
        "w_out": nrm((L, D_MIX, D), D_MIX ** -0.5),
        "ffn2_norm": gain((L, D)),
        "ffn2_w_gate": nrm((L, D, D_FF), D ** -0.5),
        "ffn2_w_up": nrm((L, D, D_FF), D ** -0.5),
        "ffn2_w_down": nrm((L, D_FF, D), D_FF ** -0.5),
        "ple_norm": gain((L, D)),
        "ple_w_gate": nrm((L, D, D), D ** -0.5),
        "ple_w_proj": nrm((L, PLE_DIM, D), PLE_DIM ** -0.5),
        "final_norm": gain((D,)),
    }


def _fwd_reference(x, p, ffn1_norm, ffn1_w_gate, ffn1_w_up, ffn1_w_down, mix_norm, w_in,
              lru_conv_w, lru_conv_b, lru_w_a, lru_b_a, lru_w_x, lru_b_x, lru_lambda,
              attn_sinks, rel_bias, dn_conv_w, dn_a_log, dn_dt_bias, dn_norm, w_out,
              ffn2_norm, ffn2_w_gate, ffn2_w_up, ffn2_w_down, ple_norm, ple_w_gate, ple_w_proj,
              final_norm):
    h = x
    for l in range(DEPTH):
        h = h + 0.5 * swiglu(rms_norm(h, ffn1_norm[l]), ffn1_w_gate[l], ffn1_w_up[l], ffn1_w_down[l])
        h = h + hybrid_mixer(rms_norm(h, mix_norm[l]), w_in[l],
                             lru_conv_w[l], lru_conv_b[l], lru_w_a[l], lru_b_a[l],
                             lru_w_x[l], lru_b_x[l], lru_lambda[l],
                             attn_sinks[l], rel_bias,
                             dn_conv_w[l], dn_a_log[l], dn_dt_bias[l], dn_norm[l], w_out[l])
        h = h + 0.5 * swiglu(rms_norm(h, ffn2_norm[l]), ffn2_w_gate[l], ffn2_w_up[l], ffn2_w_down[l])
        gate = jax.nn.sigmoid(rms_norm(h, ple_norm[l]) @ ple_w_gate[l])
        h = h + gate * (p[l] @ ple_w_proj[l])
    return rms_norm(h, final_norm)


import jax as _jax
import jax.numpy as _jnp

TWIN_FORMAT = 'train_step'
FWD_PARAMS = ['x', 'p', 'ffn1_norm', 'ffn1_w_gate', 'ffn1_w_up', 'ffn1_w_down', 'mix_norm', 'w_in', 'lru_conv_w', 'lru_conv_b', 'lru_w_a', 'lru_b_a', 'lru_w_x', 'lru_b_x', 'lru_lambda', 'attn_sinks', 'rel_bias', 'dn_conv_w', 'dn_a_log', 'dn_dt_bias', 'dn_norm', 'w_out', 'ffn2_norm', 'ffn2_w_gate', 'ffn2_w_up', 'ffn2_w_down', 'ple_norm', 'ple_w_gate', 'ple_w_proj', 'final_norm']
TWIN_WEIGHTS = ['ffn1_norm', 'ffn1_w_gate', 'ffn1_w_up', 'ffn1_w_down', 'mix_norm', 'w_in', 'lru_conv_w', 'lru_conv_b', 'lru_w_a', 'lru_b_a', 'lru_w_x', 'lru_b_x', 'lru_lambda', 'attn_sinks', 'rel_bias', 'dn_conv_w', 'dn_a_log', 'dn_dt_bias', 'dn_norm', 'w_out', 'ffn2_norm', 'ffn2_w_gate', 'ffn2_w_up', 'ffn2_w_down', 'ple_norm', 'ple_w_gate', 'ple_w_proj', 'final_norm']
TWIN_DIFF_INPUT = 'x'
TWIN_INPUTS = ['x', 'p', 'ffn1_norm', 'ffn1_w_gate', 'ffn1_w_up', 'ffn1_w_down', 'mix_norm', 'w_in', 'lru_conv_w', 'lru_conv_b', 'lru_w_a', 'lru_b_a', 'lru_w_x', 'lru_b_x', 'lru_lambda', 'attn_sinks', 'rel_bias', 'dn_conv_w', 'dn_a_log', 'dn_dt_bias', 'dn_norm', 'w_out', 'ffn2_norm', 'ffn2_w_gate', 'ffn2_w_up', 'ffn2_w_down', 'ple_norm', 'ple_w_gate', 'ple_w_proj', 'final_norm', 'loss_target', 'm_ffn1_norm', 'm_ffn1_w_gate', 'm_ffn1_w_up', 'm_ffn1_w_down', 'm_mix_norm', 'm_w_in', 'm_lru_conv_w', 'm_lru_conv_b', 'm_lru_w_a', 'm_lru_b_a', 'm_lru_w_x', 'm_lru_b_x', 'm_lru_lambda', 'm_attn_sinks', 'm_rel_bias', 'm_dn_conv_w', 'm_dn_a_log', 'm_dn_dt_bias', 'm_dn_norm', 'm_w_out', 'm_ffn2_norm', 'm_ffn2_w_gate', 'm_ffn2_w_up', 'm_ffn2_w_down', 'm_ple_norm', 'm_ple_w_gate', 'm_ple_w_proj', 'm_final_norm', 'v_ffn1_norm', 'v_ffn1_w_gate', 'v_ffn1_w_up', 'v_ffn1_w_down', 'v_mix_norm', 'v_w_in', 'v_lru_conv_w', 'v_lru_conv_b', 'v_lru_w_a', 'v_lru_b_a', 'v_lru_w_x', 'v_lru_b_x', 'v_lru_lambda', 'v_attn_sinks', 'v_rel_bias', 'v_dn_conv_w', 'v_dn_a_log', 'v_dn_dt_bias', 'v_dn_norm', 'v_w_out', 'v_ffn2_norm', 'v_ffn2_w_gate', 'v_ffn2_w_up', 'v_ffn2_w_down', 'v_ple_norm', 'v_ple_w_gate', 'v_ple_w_proj', 'v_final_norm']
TWIN_OUTPUTS = ['loss', 'grad_x', 'grad_ffn1_norm', 'grad_ffn1_w_gate', 'grad_ffn1_w_up', 'grad_ffn1_w_down', 'grad_mix_norm', 'grad_w_in', 'grad_lru_conv_w', 'grad_lru_conv_b', 'grad_lru_w_a', 'grad_lru_b_a', 'grad_lru_w_x', 'grad_lru_b_x', 'grad_lru_lambda', 'grad_attn_sinks', 'grad_rel_bias', 'grad_dn_conv_w', 'grad_dn_a_log', 'grad_dn_dt_bias', 'grad_dn_norm', 'grad_w_out', 'grad_ffn2_norm', 'grad_ffn2_w_gate', 'grad_ffn2_w_up', 'grad_ffn2_w_down', 'grad_ple_norm', 'grad_ple_w_gate', 'grad_ple_w_proj', 'grad_final_norm', 'delta_ffn1_norm', 'delta_ffn1_w_gate', 'delta_ffn1_w_up', 'delta_ffn1_w_down', 'delta_mix_norm', 'delta_w_in', 'delta_lru_conv_w', 'delta_lru_conv_b', 'delta_lru_w_a', 'delta_lru_b_a', 'delta_lru_w_x', 'delta_lru_b_x', 'delta_lru_lambda', 'delta_attn_sinks', 'delta_rel_bias', 'delta_dn_conv_w', 'delta_dn_a_log', 'delta_dn_dt_bias', 'delta_dn_norm', 'delta_w_out', 'delta_ffn2_norm', 'delta_ffn2_w_gate', 'delta_ffn2_w_up', 'delta_ffn2_w_down', 'delta_ple_norm', 'delta_ple_w_gate', 'delta_ple_w_proj', 'delta_final_norm', 'new_m_ffn1_norm', 'new_m_ffn1_w_gate', 'new_m_ffn1_w_up', 'new_m_ffn1_w_down', 'new_m_mix_norm', 'new_m_w_in', 'new_m_lru_conv_w', 'new_m_lru_conv_b', 'new_m_lru_w_a', 'new_m_lru_b_a', 'new_m_lru_w_x', 'new_m_lru_b_x', 'new_m_lru_lambda', 'new_m_attn_sinks', 'new_m_rel_bias', 'new_m_dn_conv_w', 'new_m_dn_a_log', 'new_m_dn_dt_bias', 'new_m_dn_norm', 'new_m_w_out', 'new_m_ffn2_norm', 'new_m_ffn2_w_gate', 'new_m_ffn2_w_up', 'new_m_ffn2_w_down', 'new_m_ple_norm', 'new_m_ple_w_gate', 'new_m_ple_w_proj', 'new_m_final_norm', 'new_v_ffn1_norm', 'new_v_ffn1_w_gate', 'new_v_ffn1_w_up', 'new_v_ffn1_w_down', 'new_v_mix_norm', 'new_v_w_in', 'new_v_lru_conv_w', 'new_v_lru_conv_b', 'new_v_lru_w_a', 'new_v_lru_b_a', 'new_v_lru_w_x', 'new_v_lru_b_x', 'new_v_lru_lambda', 'new_v_attn_sinks', 'new_v_rel_bias', 'new_v_dn_conv_w', 'new_v_dn_a_log', 'new_v_dn_dt_bias', 'new_v_dn_norm', 'new_v_w_out', 'new_v_ffn2_norm', 'new_v_ffn2_w_gate', 'new_v_ffn2_w_up', 'new_v_ffn2_w_down', 'new_v_ple_norm', 'new_v_ple_w_gate', 'new_v_ple_w_proj', 'new_v_final_norm']
TWIN_LEAF_KINDS = {'loss': 'loss', 'grad_x': 'grad_x', 'grad_ffn1_norm': 'grad_w', 'grad_ffn1_w_gate': 'grad_w', 'grad_ffn1_w_up': 'grad_w', 'grad_ffn1_w_down': 'grad_w', 'grad_mix_norm': 'grad_w', 'grad_w_in': 'grad_w', 'grad_lru_conv_w': 'grad_w', 'grad_lru_conv_b': 'grad_w', 'grad_lru_w_a': 'grad_w', 'grad_lru_b_a': 'grad_w', 'grad_lru_w_x': 'grad_w', 'grad_lru_b_x': 'grad_w', 'grad_lru_lambda': 'grad_w', 'grad_attn_sinks': 'grad_w', 'grad_rel_bias': 'grad_w', 'grad_dn_conv_w': 'grad_w', 'grad_dn_a_log': 'grad_w', 'grad_dn_dt_bias': 'grad_w', 'grad_dn_norm': 'grad_w', 'grad_w_out': 'grad_w', 'grad_ffn2_norm': 'grad_w', 'grad_ffn2_w_gate': 'grad_w', 'grad_ffn2_w_up': 'grad_w', 'grad_ffn2_w_down': 'grad_w', 'grad_ple_norm': 'grad_w', 'grad_ple_w_gate': 'grad_w', 'grad_ple_w_proj': 'grad_w', 'grad_final_norm': 'grad_w', 'delta_ffn1_norm': 'delta_w', 'delta_ffn1_w_gate': 'delta_w', 'delta_ffn1_w_up': 'delta_w', 'delta_ffn1_w_down': 'delta_w', 'delta_mix_norm': 'delta_w', 'delta_w_in': 'delta_w', 'delta_lru_conv_w': 'delta_w', 'delta_lru_conv_b': 'delta_w', 'delta_lru_w_a': 'delta_w', 'delta_lru_b_a': 'delta_w', 'delta_lru_w_x': 'delta_w', 'delta_lru_b_x': 'delta_w', 'delta_lru_lambda': 'delta_w', 'delta_attn_sinks': 'delta_w', 'delta_rel_bias': 'delta_w', 'delta_dn_conv_w': 'delta_w', 'delta_dn_a_log': 'delta_w', 'delta_dn_dt_bias': 'delta_w', 'delta_dn_norm': 'delta_w', 'delta_w_out': 'delta_w', 'delta_ffn2_norm': 'delta_w', 'delta_ffn2_w_gate': 'delta_w', 'delta_ffn2_w_up': 'delta_w', 'delta_ffn2_w_down': 'delta_w', 'delta_ple_norm': 'delta_w', 'delta_ple_w_gate': 'delta_w', 'delta_ple_w_proj': 'delta_w', 'delta_final_norm': 'delta_w', 'new_m_ffn1_norm': 'new_m', 'new_m_ffn1_w_gate': 'new_m', 'new_m_ffn1_w_up': 'new_m', 'new_m_ffn1_w_down': 'new_m', 'new_m_mix_norm': 'new_m', 'new_m_w_in': 'new_m', 'new_m_lru_conv_w': 'new_m', 'new_m_lru_conv_b': 'new_m', 'new_m_lru_w_a': 'new_m', 'new_m_lru_b_a': 'new_m', 'new_m_lru_w_x': 'new_m', 'new_m_lru_b_x': 'new_m', 'new_m_lru_lambda': 'new_m', 'new_m_attn_sinks': 'new_m', 'new_m_rel_bias': 'new_m', 'new_m_dn_conv_w': 'new_m', 'new_m_dn_a_log': 'new_m', 'new_m_dn_dt_bias': 'new_m', 'new_m_dn_norm': 'new_m', 'new_m_w_out': 'new_m', 'new_m_ffn2_norm': 'new_m', 'new_m_ffn2_w_gate': 'new_m', 'new_m_ffn2_w_up': 'new_m', 'new_m_ffn2_w_down': 'new_m', 'new_m_ple_norm': 'new_m', 'new_m_ple_w_gate': 'new_m', 'new_m_ple_w_proj': 'new_m', 'new_m_final_norm': 'new_m', 'new_v_ffn1_norm': 'new_v', 'new_v_ffn1_w_gate': 'new_v', 'new_v_ffn1_w_up': 'new_v', 'new_v_ffn1_w_down': 'new_v', 'new_v_mix_norm': 'new_v', 'new_v_w_in': 'new_v', 'new_v_lru_conv_w': 'new_v', 'new_v_lru_conv_b': 'new_v', 'new_v_lru_w_a': 'new_v', 'new_v_lru_b_a': 'new_v', 'new_v_lru_w_x': 'new_v', 'new_v_lru_b_x': 'new_v', 'new_v_lru_lambda': 'new_v', 'new_v_attn_sinks': 'new_v', 'new_v_rel_bias': 'new_v', 'new_v_dn_conv_w': 'new_v', 'new_v_dn_a_log': 'new_v', 'new_v_dn_dt_bias': 'new_v', 'new_v_dn_norm': 'new_v', 'new_v_w_out': 'new_v', 'new_v_ffn2_norm': 'new_v', 'new_v_ffn2_w_gate': 'new_v', 'new_v_ffn2_w_up': 'new_v', 'new_v_ffn2_w_down': 'new_v', 'new_v_ple_norm': 'new_v', 'new_v_ple_w_gate': 'new_v', 'new_v_ple_w_proj': 'new_v', 'new_v_final_norm': 'new_v'}


def _forward(args):
    return _fwd_reference(*[args[k] for k in FWD_PARAMS])


def _output_shape():
    out = _jax.eval_shape(lambda: _forward(_fwd_setup_inputs(0)))
    return out.shape, out.dtype

N_MICROBATCH = 1
ADAM_LR = 0.001
ADAM_B1 = 0.9
ADAM_B2 = 0.999
ADAM_EPS = 1e-08
ADAM_WD = 0.01
ADAM_STEP = 10
PER_EXAMPLE_BATCH_AXIS = {'x': 0, 'p': 1, 'loss_target': 0}
SHARED_INPUTS = []
_WEIGHT_DTYPES = {'ffn1_norm': _jnp.float32, 'ffn1_w_gate': _jnp.float32, 'ffn1_w_up': _jnp.float32, 'ffn1_w_down': _jnp.float32, 'mix_norm': _jnp.float32, 'w_in': _jnp.float32, 'lru_conv_w': _jnp.float32, 'lru_conv_b': _jnp.float32, 'lru_w_a': _jnp.float32, 'lru_b_a': _jnp.float32, 'lru_w_x': _jnp.float32, 'lru_b_x': _jnp.float32, 'lru_lambda': _jnp.float32, 'attn_sinks': _jnp.float32, 'rel_bias': _jnp.float32, 'dn_conv_w': _jnp.float32, 'dn_a_log': _jnp.float32, 'dn_dt_bias': _jnp.float32, 'dn_norm': _jnp.float32, 'w_out': _jnp.float32, 'ffn2_norm': _jnp.float32, 'ffn2_w_gate': _jnp.float32, 'ffn2_w_up': _jnp.float32, 'ffn2_w_down': _jnp.float32, 'ple_norm': _jnp.float32, 'ple_w_gate': _jnp.float32, 'ple_w_proj': _jnp.float32, 'final_norm': _jnp.float32}
MOMENT_SCALE = {'ffn1_norm': 7.406570e-02, 'ffn1_w_gate': 3.175897e-02, 'ffn1_w_up': 3.071869e-02, 'ffn1_w_down': 5.095428e-02, 'mix_norm': 1.000675e-01, 'w_in': 6.714543e-02, 'lru_conv_w': 8.299577e-02, 'lru_conv_b': 8.356510e-01, 'lru_w_a': 2.985011e-02, 'lru_b_a': 2.079460e-02, 'lru_w_x': 5.296160e-02, 'lru_b_x': 3.839966e-02, 'lru_lambda': 4.390070e-02, 'attn_sinks': 2.671501e-02, 'rel_bias': 5.653413e-02, 'dn_conv_w': 8.308936e-02, 'dn_a_log': 6.872039e-01, 'dn_dt_bias': 6.510312e-01, 'dn_norm': 1.724802e-01, 'w_out': 6.101913e-02, 'ffn2_norm': 5.837109e-02, 'ffn2_w_gate': 2.529739e-02, 'ffn2_w_up': 2.457052e-02, 'ffn2_w_down': 4.071785e-02, 'ple_norm': 2.793247e-02, 'ple_w_gate': 2.844663e-02, 'ple_w_proj': 7.275663e-02, 'final_norm': 3.203705e+01}


def _to_microbatches(a, axis):
    t = _jnp.moveaxis(a, axis, 0)
    t = t.reshape((N_MICROBATCH, t.shape[0] // N_MICROBATCH) + t.shape[1:])
    return _jnp.moveaxis(t, 1, axis + 1)


def setup_inputs(seed: int = 0) -> dict:
    inp = _fwd_setup_inputs(seed)
    key = _jax.random.fold_in(_jax.random.key(seed), 7919)
    shape, _ = _output_shape()
    out = dict(inp)
    out["loss_target"] = _jax.random.normal(_jax.random.fold_in(key, 0), shape, _jnp.float32)
    for i, name in enumerate(TWIN_WEIGHTS):
        w = inp[name].astype(_jnp.float32)
        if MOMENT_SCALE is None:
            s = _jnp.sqrt(_jnp.mean(_jnp.square(w)) + 1e-30)
        else:
            s = MOMENT_SCALE[name]
        km, kv = _jax.random.split(_jax.random.fold_in(key, i + 1))
        out[name] = w
        out["m_" + name] = s * _jax.random.normal(km, w.shape, _jnp.float32)
        out["v_" + name] = (s * s) * _jax.random.uniform(kv, w.shape, _jnp.float32, 0.5, 1.5)
    if N_MICROBATCH > 1:
        for name, axis in PER_EXAMPLE_BATCH_AXIS.items():
            out[name] = _to_microbatches(out[name], axis)
    return {'x': out['x'], 'p': out['p'], 'ffn1_norm': out['ffn1_norm'], 'ffn1_w_gate': out['ffn1_w_gate'], 'ffn1_w_up': out['ffn1_w_up'], 'ffn1_w_down': out['ffn1_w_down'], 'mix_norm': out['mix_norm'], 'w_in': out['w_in'], 'lru_conv_w': out['lru_conv_w'], 'lru_conv_b': out['lru_conv_b'], 'lru_w_a': out['lru_w_a'], 'lru_b_a': out['lru_b_a'], 'lru_w_x': out['lru_w_x'], 'lru_b_x': out['lru_b_x'], 'lru_lambda': out['lru_lambda'], 'attn_sinks': out['attn_sinks'], 'rel_bias': out['rel_bias'], 'dn_conv_w': out['dn_conv_w'], 'dn_a_log': out['dn_a_log'], 'dn_dt_bias': out['dn_dt_bias'], 'dn_norm': out['dn_norm'], 'w_out': out['w_out'], 'ffn2_norm': out['ffn2_norm'], 'ffn2_w_gate': out['ffn2_w_gate'], 'ffn2_w_up': out['ffn2_w_up'], 'ffn2_w_down': out['ffn2_w_down'], 'ple_norm': out['ple_norm'], 'ple_w_gate': out['ple_w_gate'], 'ple_w_proj': out['ple_w_proj'], 'final_norm': out['final_norm'], 'loss_target': out['loss_target'], 'm_ffn1_norm': out['m_ffn1_norm'], 'm_ffn1_w_gate': out['m_ffn1_w_gate'], 'm_ffn1_w_up': out['m_ffn1_w_up'], 'm_ffn1_w_down': out['m_ffn1_w_down'], 'm_mix_norm': out['m_mix_norm'], 'm_w_in': out['m_w_in'], 'm_lru_conv_w': out['m_lru_conv_w'], 'm_lru_conv_b': out['m_lru_conv_b'], 'm_lru_w_a': out['m_lru_w_a'], 'm_lru_b_a': out['m_lru_b_a'], 'm_lru_w_x': out['m_lru_w_x'], 'm_lru_b_x': out['m_lru_b_x'], 'm_lru_lambda': out['m_lru_lambda'], 'm_attn_sinks': out['m_attn_sinks'], 'm_rel_bias': out['m_rel_bias'], 'm_dn_conv_w': out['m_dn_conv_w'], 'm_dn_a_log': out['m_dn_a_log'], 'm_dn_dt_bias': out['m_dn_dt_bias'], 'm_dn_norm': out['m_dn_norm'], 'm_w_out': out['m_w_out'], 'm_ffn2_norm': out['m_ffn2_norm'], 'm_ffn2_w_gate': out['m_ffn2_w_gate'], 'm_ffn2_w_up': out['m_ffn2_w_up'], 'm_ffn2_w_down': out['m_ffn2_w_down'], 'm_ple_norm': out['m_ple_norm'], 'm_ple_w_gate': out['m_ple_w_gate'], 'm_ple_w_proj': out['m_ple_w_proj'], 'm_final_norm': out['m_final_norm'], 'v_ffn1_norm': out['v_ffn1_norm'], 'v_ffn1_w_gate': out['v_ffn1_w_gate'], 'v_ffn1_w_up': out['v_ffn1_w_up'], 'v_ffn1_w_down': out['v_ffn1_w_down'], 'v_mix_norm': out['v_mix_norm'], 'v_w_in': out['v_w_in'], 'v_lru_conv_w': out['v_lru_conv_w'], 'v_lru_conv_b': out['v_lru_conv_b'], 'v_lru_w_a': out['v_lru_w_a'], 'v_lru_b_a': out['v_lru_b_a'], 'v_lru_w_x': out['v_lru_w_x'], 'v_lru_b_x': out['v_lru_b_x'], 'v_lru_lambda': out['v_lru_lambda'], 'v_attn_sinks': out['v_attn_sinks'], 'v_rel_bias': out['v_rel_bias'], 'v_dn_conv_w': out['v_dn_conv_w'], 'v_dn_a_log': out['v_dn_a_log'], 'v_dn_dt_bias': out['v_dn_dt_bias'], 'v_dn_norm': out['v_dn_norm'], 'v_w_out': out['v_w_out'], 'v_ffn2_norm': out['v_ffn2_norm'], 'v_ffn2_w_gate': out['v_ffn2_w_gate'], 'v_ffn2_w_up': out['v_ffn2_w_up'], 'v_ffn2_w_down': out['v_ffn2_w_down'], 'v_ple_norm': out['v_ple_norm'], 'v_ple_w_gate': out['v_ple_w_gate'], 'v_ple_w_proj': out['v_ple_w_proj'], 'v_final_norm': out['v_final_norm']}


def _loss(weights, diff, rest, loss_target):
    with _jax.named_scope("forward"):
        args = {**rest, TWIN_DIFF_INPUT: diff, **{k: w.astype(_WEIGHT_DTYPES[k]) for k, w in weights.items()}}
        y = _forward(args)
    with _jax.named_scope("loss_head"):
        err = _jnp.square(y.astype(_jnp.float32) - loss_target)
        return 0.5 * _jnp.sum(_jnp.mean(err, axis=-1)) if err.ndim else 0.5 * err


def _adamw(w, g, m, v):
    m = ADAM_B1 * m + (1.0 - ADAM_B1) * g
    v = ADAM_B2 * v + (1.0 - ADAM_B2) * _jnp.square(g)
    m_hat = m / (1.0 - ADAM_B1 ** ADAM_STEP)
    v_hat = v / (1.0 - ADAM_B2 ** ADAM_STEP)
    delta = -ADAM_LR * (m_hat / (_jnp.sqrt(v_hat) + ADAM_EPS) + ADAM_WD * w)
    return delta, m, v


def reference(x, p, ffn1_norm, ffn1_w_gate, ffn1_w_up, ffn1_w_down, mix_norm, w_in, lru_conv_w, lru_conv_b, lru_w_a, lru_b_a, lru_w_x, lru_b_x, lru_lambda, attn_sinks, rel_bias, dn_conv_w, dn_a_log, dn_dt_bias, dn_norm, w_out, ffn2_norm, ffn2_w_gate, ffn2_w_up, ffn2_w_down, ple_norm, ple_w_gate, ple_w_proj, final_norm, loss_target, m_ffn1_norm, m_ffn1_w_gate, m_ffn1_w_up, m_ffn1_w_down, m_mix_norm, m_w_in, m_lru_conv_w, m_lru_conv_b, m_lru_w_a, m_lru_b_a, m_lru_w_x, m_lru_b_x, m_lru_lambda, m_attn_sinks, m_rel_bias, m_dn_conv_w, m_dn_a_log, m_dn_dt_bias, m_dn_norm, m_w_out, m_ffn2_norm, m_ffn2_w_gate, m_ffn2_w_up, m_ffn2_w_down, m_ple_norm, m_ple_w_gate, m_ple_w_proj, m_final_norm, v_ffn1_norm, v_ffn1_w_gate, v_ffn1_w_up, v_ffn1_w_down, v_mix_norm, v_w_in, v_lru_conv_w, v_lru_conv_b, v_lru_w_a, v_lru_b_a, v_lru_w_x, v_lru_b_x, v_lru_lambda, v_attn_sinks, v_rel_bias, v_dn_conv_w, v_dn_a_log, v_dn_dt_bias, v_dn_norm, v_w_out, v_ffn2_norm, v_ffn2_w_gate, v_ffn2_w_up, v_ffn2_w_down, v_ple_norm, v_ple_w_gate, v_ple_w_proj, v_final_norm):
    given = dict(x=x, p=p, ffn1_norm=ffn1_norm, ffn1_w_gate=ffn1_w_gate, ffn1_w_up=ffn1_w_up, ffn1_w_down=ffn1_w_down, mix_norm=mix_norm, w_in=w_in, lru_conv_w=lru_conv_w, lru_conv_b=lru_conv_b, lru_w_a=lru_w_a, lru_b_a=lru_b_a, lru_w_x=lru_w_x, lru_b_x=lru_b_x, lru_lambda=lru_lambda, attn_sinks=attn_sinks, rel_bias=rel_bias, dn_conv_w=dn_conv_w, dn_a_log=dn_a_log, dn_dt_bias=dn_dt_bias, dn_norm=dn_norm, w_out=w_out, ffn2_norm=ffn2_norm, ffn2_w_gate=ffn2_w_gate, ffn2_w_up=ffn2_w_up, ffn2_w_down=ffn2_w_down, ple_norm=ple_norm, ple_w_gate=ple_w_gate, ple_w_proj=ple_w_proj, final_norm=final_norm, loss_target=loss_target, m_ffn1_norm=m_ffn1_norm, m_ffn1_w_gate=m_ffn1_w_gate, m_ffn1_w_up=m_ffn1_w_up, m_ffn1_w_down=m_ffn1_w_down, m_mix_norm=m_mix_norm, m_w_in=m_w_in, m_lru_conv_w=m_lru_conv_w, m_lru_conv_b=m_lru_conv_b, m_lru_w_a=m_lru_w_a, m_lru_b_a=m_lru_b_a, m_lru_w_x=m_lru_w_x, m_lru_b_x=m_lru_b_x, m_lru_lambda=m_lru_lambda, m_attn_sinks=m_attn_sinks, m_rel_bias=m_rel_bias, m_dn_conv_w=m_dn_conv_w, m_dn_a_log=m_dn_a_log, m_dn_dt_bias=m_dn_dt_bias, m_dn_norm=m_dn_norm, m_w_out=m_w_out, m_ffn2_norm=m_ffn2_norm, m_ffn2_w_gate=m_ffn2_w_gate, m_ffn2_w_up=m_ffn2_w_up, m_ffn2_w_down=m_ffn2_w_down, m_ple_norm=m_ple_norm, m_ple_w_gate=m_ple_w_gate, m_ple_w_proj=m_ple_w_proj, m_final_norm=m_final_norm, v_ffn1_norm=v_ffn1_norm, v_ffn1_w_gate=v_ffn1_w_gate, v_ffn1_w_up=v_ffn1_w_up, v_ffn1_w_down=v_ffn1_w_down, v_mix_norm=v_mix_norm, v_w_in=v_w_in, v_lru_conv_w=v_lru_conv_w, v_lru_conv_b=v_lru_conv_b, v_lru_w_a=v_lru_w_a, v_lru_b_a=v_lru_b_a, v_lru_w_x=v_lru_w_x, v_lru_b_x=v_lru_b_x, v_lru_lambda=v_lru_lambda, v_attn_sinks=v_attn_sinks, v_rel_bias=v_rel_bias, v_dn_conv_w=v_dn_conv_w, v_dn_a_log=v_dn_a_log, v_dn_dt_bias=v_dn_dt_bias, v_dn_norm=v_dn_norm, v_w_out=v_w_out, v_ffn2_norm=v_ffn2_norm, v_ffn2_w_gate=v_ffn2_w_gate, v_ffn2_w_up=v_ffn2_w_up, v_ffn2_w_down=v_ffn2_w_down, v_ple_norm=v_ple_norm, v_ple_w_gate=v_ple_w_gate, v_ple_w_proj=v_ple_w_proj, v_final_norm=v_final_norm)
    weights = {n: given[n] for n in TWIN_WEIGHTS}
    shared = {n: given[n] for n in SHARED_INPUTS}
    per_example = {n: given[n] for n in ['x', 'p']}
    grad_fn = _jax.value_and_grad(_loss, argnums=(0, 1))

    def one_microbatch(ex, loss_target):
        ex = dict(ex)
        diff = ex.pop(TWIN_DIFF_INPUT)
        return grad_fn(weights, diff, {**shared, **ex}, loss_target)

    if N_MICROBATCH == 1:
        loss, (grad_w, grad_x) = one_microbatch(per_example, given["loss_target"])
    else:
        def body(carry, xs):
            loss_sum, grad_sum = carry
            l_k, (gw_k, gx_k) = one_microbatch(xs[0], xs[1])
            with _jax.named_scope("update"):
                return (loss_sum + l_k, _jax.tree.map(_jnp.add, grad_sum, gw_k)), gx_k

        init = (_jnp.zeros((), _jnp.float32), _jax.tree.map(_jnp.zeros_like, weights))
        (loss, grad_w), grad_x = _jax.lax.scan(body, init, (per_example, given["loss_target"]))
    with _jax.named_scope("update"):
        delta_w, new_m, new_v = {}, {}, {}
        for n in TWIN_WEIGHTS:
            delta_w[n], new_m[n], new_v[n] = _adamw(weights[n], grad_w[n], given["m_" + n], given["v_" + n])
    return (loss, grad_x, *[grad_w[n] for n in TWIN_WEIGHTS], *[delta_w[n] for n in TWIN_WEIGHTS],
            *[new_m[n] for n in TWIN_WEIGHTS], *[new_v[n] for n in TWIN_WEIGHTS])
```

```python
import functools
import math

import numpy as np
import jax
import jax.numpy as jnp
from jax import lax
from jax.experimental import pallas as pl
from jax.experimental.pallas import tpu as pltpu

F32 = jnp.float32
BF16 = jnp.bfloat16
HI = lax.Precision.HIGHEST

D = 1024
DEPTH = 2
EPS = 1e-6
PLE = 256
FF = 2816
HD = 64
LRU_W = 256
LRU_C = 8.0
ATT_W = 512
ATT_H = 8
ATT_KV = 2
ATT_G = 4
KV_W = 128
WINDOW = 128
BQ = 128
REL_BUCKETS = 32
REL_MAX_DIST = 128
DN_W = 256
DN_H = 4
CHUNK = 64
D_IN = 2312
D_IN_PAD = 2432
N_DEV = 8

ADAM_LR = 0.001
ADAM_B1 = 0.9
ADAM_B2 = 0.999
ADAM_EPS = 1e-08
ADAM_WD = 0.01
ADAM_STEP = 10

LANE = 128
VMEM_LIMIT = 56 * 1024 * 1024
FF_TILE = 1408
TOK_TILE = 512
NEG = -1e30


def _cp(*sem):
    return pltpu.CompilerParams(dimension_semantics=tuple(sem), vmem_limit_bytes=VMEM_LIMIT)


def _dg(a, b, ca, cb, exact):
    if exact:
        return lax.dot_general(a.astype(F32), b.astype(F32), (((ca,), (cb,)), ((), ())),
                               precision=HI, preferred_element_type=F32)
    return lax.dot_general(a.astype(BF16), b.astype(BF16), (((ca,), (cb,)), ((), ())),
                           preferred_element_type=F32)


def _make_mm(exact):
    @jax.custom_vjp
    def mm(a, b):
        return _dg(a, b, 1, 0, exact)

    @jax.custom_vjp
    def mm_nt(a, b):
        return _dg(a, b, 1, 1, exact)

    @jax.custom_vjp
    def mm_tn(a, b):
        return _dg(a, b, 0, 0, exact)

    mm.defvjp(lambda a, b: (mm(a, b), (a, b)),
              lambda r, d: (mm_nt(d, r[1]), mm_tn(r[0], d)))
    mm_nt.defvjp(lambda a, b: (mm_nt(a, b), (a, b)),
                 lambda r, d: (mm(d, r[1]), mm_tn(d, r[0])))
    mm_tn.defvjp(lambda a, b: (mm_tn(a, b), (a, b)),
                 lambda r, d: (mm_nt(r[1], d), mm(r[0], d)))
    return mm, mm_nt, mm_tn


_mm, _mm_nt, _mm_tn = _make_mm(False)
_mmx, _mmx_nt, _mmx_tn = _make_mm(True)


def _iota(shape, dim):
    return lax.broadcasted_iota(jnp.int32, shape, dim)


def _sigmoid(x):
    return 1.0 / (1.0 + jnp.exp(-x))


def _rms(h, g):
    rstd = lax.rsqrt(jnp.mean(h * h, axis=-1, keepdims=True) + EPS)
    xhat = h * rstd
    return xhat * g, xhat, rstd


def _rms_bwd(dxn, xhat, rstd, g):
    dxhat = dxn * g
    dh = rstd * (dxhat - xhat * jnp.mean(dxhat * xhat, axis=-1, keepdims=True))
    dg = jnp.sum(dxn * xhat, axis=0, keepdims=True)
    return dh, dg


def _row_spec(tm, n):
    return pl.BlockSpec((tm, n), lambda i, *_: (i, 0))


def _full_spec(shape):
    nd = len(shape)
    return pl.BlockSpec(shape, lambda *_: (0,) * nd)


def ffn_fwd(h, g, wg, wu, wd, name):
    T = h.shape[0]
    tm = min(TOK_TILE, T)
    nj = FF // FF_TILE

    def body(h_ref, g_ref, wg_ref, wu_ref, wd_ref, o_ref, xn_s):
        j = pl.program_id(1)

        @pl.when(j == 0)
        def _():
            hh = h_ref[...]
            xn_s[...] = _rms(hh, g_ref[...])[0].astype(BF16)
            o_ref[...] = hh

        xn = xn_s[...]
        gt = _mm(xn, wg_ref[...])
        up = _mm(xn, wu_ref[...])
        act = gt * _sigmoid(gt) * up
        o_ref[...] += 0.5 * _mm(act, wd_ref[...])

    return pl.pallas_call(
        body, name=name, grid=(T // tm, nj),
        in_specs=[pl.BlockSpec((tm, D), lambda i, j: (i, 0)),
                  pl.BlockSpec((1, D), lambda i, j: (0, 0)),
                  pl.BlockSpec((D, FF_TILE), lambda i, j: (0, j)),
                  pl.BlockSpec((D, FF_TILE), lambda i, j: (0, j)),
                  pl.BlockSpec((FF_TILE, D), lambda i, j: (j, 0))],
        out_specs=pl.BlockSpec((tm, D), lambda i, j: (i, 0)),
        out_shape=jax.ShapeDtypeStruct((T, D), F32),
        scratch_shapes=[pltpu.VMEM((tm, D), BF16)],
        compiler_params=_cp("parallel", "arbitrary"),
    )(h, g, wg, wu, wd)


def ffn_bwd(h, dy, g, wg, wu, wd, name):
    T = h.shape[0]
    tm = min(TOK_TILE // 2, T)
    nj = FF // FF_TILE

    def body(h_ref, dy_ref, g_ref, wg_ref, wu_ref, wd_ref,
             dh_ref, dg_ref, du_ref, a_ref, xn_ref, dn_ref, xn_s, dxn_s):
        i = pl.program_id(0)
        j = pl.program_id(1)

        @pl.when(j == 0)
        def _():
            xn = _rms(h_ref[...], g_ref[...])[0].astype(BF16)
            xn_s[...] = xn
            xn_ref[...] = xn
            dxn_s[...] = jnp.zeros_like(dxn_s)

        @pl.when((i == 0) & (j == 0))
        def _():
            dn_ref[...] = jnp.zeros_like(dn_ref)

        xn = xn_s[...]
        gt = _mm(xn, wg_ref[...])
        up = _mm(xn, wu_ref[...])
        sg = _sigmoid(gt)
        si = gt * sg
        da = _mm_nt(0.5 * dy_ref[...], wd_ref[...])
        dup = da * si
        dgt = da * up * (sg * (1.0 + gt * (1.0 - sg)))
        dg_ref[...] = dgt.astype(BF16)
        du_ref[...] = dup.astype(BF16)
        a_ref[...] = (si * up).astype(BF16)
        dxn_s[...] += _mm_nt(dgt, wg_ref[...]) + _mm_nt(dup, wu_ref[...])

        @pl.when(j == nj - 1)
        def _():
            gg = g_ref[...]
            _, xhat, rstd = _rms(h_ref[...], gg)
            dh, dn = _rms_bwd(dxn_s[...], xhat, rstd, gg)
            dh_ref[...] = dy_ref[...] + dh
            dn_ref[...] += dn

    return pl.pallas_call(
        body, name=name, grid=(T // tm, nj),
        in_specs=[pl.BlockSpec((tm, D), lambda i, j: (i, 0)),
                  pl.BlockSpec((tm, D), lambda i, j: (i, 0)),
                  pl.BlockSpec((1, D), lambda i, j: (0, 0)),
                  pl.BlockSpec((D, FF_TILE), lambda i, j: (0, j)),
                  pl.BlockSpec((D, FF_TILE), lambda i, j: (0, j)),
                  pl.BlockSpec((FF_TILE, D), lambda i, j: (j, 0))],
        out_specs=[pl.BlockSpec((tm, D), lambda i, j: (i, 0)),
                   pl.BlockSpec((tm, FF_TILE), lambda i, j: (i, j)),
                   pl.BlockSpec((tm, FF_TILE), lambda i, j: (i, j)),
                   pl.BlockSpec((tm, FF_TILE), lambda i, j: (i, j)),
                   pl.BlockSpec((tm, D), lambda i, j: (i, 0)),
                   pl.BlockSpec((1, D), lambda i, j: (0, 0))],
        out_shape=[jax.ShapeDtypeStruct((T, D), F32),
                   jax.ShapeDtypeStruct((T, FF), BF16),
                   jax.ShapeDtypeStruct((T, FF), BF16),
                   jax.ShapeDtypeStruct((T, FF), BF16),
                   jax.ShapeDtypeStruct((T, D), BF16),
                   jax.ShapeDtypeStruct((1, D), F32)],
        scratch_shapes=[pltpu.VMEM((tm, D), BF16), pltpu.VMEM((tm, D), F32)],
        compiler_params=_cp("arbitrary", "arbitrary"),
    )(h, dy, g, wg, wu, wd)


def _pick(n, prefs):
    for t in prefs:
        if n % t == 0:
            return t
    return n


def matmul_tn(a, b, name, scale=1.0, out_dtype=BF16):
    T, M = a.shape
    N = b.shape[1]
    tmm = _pick(M, (512, 1408, 256))
    tnn = _pick(N, (1408, 1024, 2432))
    tk = min(TOK_TILE, T)
    nk = T // tk

    def body(a_ref, b_ref, o_ref, acc):
        k = pl.program_id(2)

        @pl.when(k == 0)
        def _():
            acc[...] = jnp.zeros_like(acc)

        acc[...] += _mm_tn(a_ref[...], b_ref[...])

        @pl.when(k == nk - 1)
        def _():
            o_ref[...] = (scale * acc[...]).astype(out_dtype)

    return pl.pallas_call(
        body, name=name, grid=(M // tmm, N // tnn, nk),
        in_specs=[pl.BlockSpec((tk, tmm), lambda i, j, k: (k, i)),
                  pl.BlockSpec((tk, tnn), lambda i, j, k: (k, j))],
        out_specs=pl.BlockSpec((tmm, tnn), lambda i, j, k: (i, j)),
        out_shape=jax.ShapeDtypeStruct((M, N), out_dtype),
        scratch_shapes=[pltpu.VMEM((tmm, tnn), F32)],
        compiler_params=_cp("parallel", "parallel", "arbitrary"),
    )(a, b)


U_SPLITS = (512, 768, 1024, 128)
U_OFFS = (0, 512, 1280, 2304)


def mixin_fwd(h, g, w_in, name):
    T = h.shape[0]
    tm = min(TOK_TILE, T)

    def body(h_ref, g_ref, w_ref, u0, u1, u2, u3, xn_ref):
        xn = _rms(h_ref[...], g_ref[...])[0].astype(BF16)
        xn_ref[...] = xn
        u = _mm(xn, w_ref[...])
        for ref, off, n in zip((u0, u1, u2, u3), U_OFFS, U_SPLITS):
            ref[...] = u[:, off:off + n]

    return pl.pallas_call(
        body, name=name, grid=(T // tm,),
        in_specs=[_row_spec(tm, D), _full_spec((1, D)), _full_spec((D, D_IN_PAD))],
        out_specs=[_row_spec(tm, n) for n in U_SPLITS] + [_row_spec(tm, D)],
        out_shape=[jax.ShapeDtypeStruct((T, n), F32) for n in U_SPLITS]
        + [jax.ShapeDtypeStruct((T, D), BF16)],
        compiler_params=_cp("parallel"),
    )(h, g, w_in)


def mixin_bwd(h, dh_in, g, w_in, dus, name):
    T = h.shape[0]
    tm = min(TOK_TILE, T)

    def body(h_ref, dhi_ref, g_ref, w_ref, d0, d1, d2, d3, dh_ref, du_ref, dn_ref):
        @pl.when(pl.program_id(0) == 0)
        def _():
            dn_ref[...] = jnp.zeros_like(dn_ref)

        dxn = jnp.zeros((tm, D), F32)
        for ref, off, n in zip((d0, d1, d2, d3), U_OFFS, U_SPLITS):
            du = ref[...]
            du_ref[:, off:off + n] = du.astype(BF16)
            dxn += _mm_nt(du, w_ref[:, off:off + n])
        gg = g_ref[...]
        _, xhat, rstd = _rms(h_ref[...], gg)
        dh, dn = _rms_bwd(dxn, xhat, rstd, gg)
        dh_ref[...] = dhi_ref[...] + dh
        dn_ref[...] += dn

    return pl.pallas_call(
        body, name=name, grid=(T // tm,),
        in_specs=[_row_spec(tm, D), _row_spec(tm, D), _full_spec((1, D)), _full_spec((D, D_IN_PAD))]
        + [_row_spec(tm, n) for n in U_SPLITS],
        out_specs=[_row_spec(tm, D), _row_spec(tm, D_IN_PAD), _full_spec((1, D))],
        out_shape=[jax.ShapeDtypeStruct((T, D), F32), jax.ShapeDtypeStruct((T, D_IN_PAD), BF16),
                   jax.ShapeDtypeStruct((1, D), F32)],
        compiler_params=_cp("arbitrary"),
    )(h, dh_in, g, w_in, *dus)


def _shift_down(x, s, row):
    if s == 0:
        return x
    return jnp.where(row >= s, pltpu.roll(x, s, 0), 0.0)


def _shift_up(x, s, row):
    if s == 0:
        return x
    n = x.shape[0]
    return jnp.where(row < n - s, pltpu.roll(x, n - s, 0), 0.0)


def conv_fwd(x, w, b, S, col0, C, name):
    T = x.shape[0]
    cb0 = col0 // LANE

    def body(x_ref, w_ref, b_ref, y_ref):
        xx = x_ref[...]
        row = _iota(xx.shape, 0)
        y = xx * w_ref[3:4, :] + b_ref[...]
        for k in range(3):
            y += _shift_down(xx, 3 - k, row) * w_ref[k:k + 1, :]
        y_ref[...] = y

    return pl.pallas_call(
        body, name=name, grid=(T // S, C // LANE),
        in_specs=[pl.BlockSpec((S, LANE), lambda s, c: (s, cb0 + c)),
                  pl.BlockSpec((4, LANE), lambda s, c: (0, c)),
                  pl.BlockSpec((1, LANE), lambda s, c: (0, c))],
        out_specs=pl.BlockSpec((S, LANE), lambda s, c: (s, c)),
        out_shape=jax.ShapeDtypeStruct((T, C), F32),
        compiler_params=_cp("parallel", "parallel"),
    )(x, w, b)


def conv_bwd(x, dy, w, S, col0, C, name):
    T = x.shape[0]
    cb0 = col0 // LANE

    def body(x_ref, dy_ref, w_ref, dx_ref, dwb_ref):
        @pl.when(pl.program_id(1) == 0)
        def _():
            dwb_ref[...] = jnp.zeros_like(dwb_ref)

        xx = x_ref[...]
        dd = dy_ref[...]
        row = _iota(xx.shape, 0)
        dx = dd * w_ref[3:4, :]
        for k in range(3):
            dx += _shift_up(dd, 3 - k, row) * w_ref[k:k + 1, :]
        dx_ref[...] = dx
        for k in range(4):
            dwb_ref[k:k + 1, :] += jnp.sum(dd * _shift_down(xx, 3 - k, row), axis=0, keepdims=True)
        dwb_ref[4:5, :] += jnp.sum(dd, axis=0, keepdims=True)

    return pl.pallas_call(
        body, name=name, grid=(C // LANE, T // S),
        in_specs=[pl.BlockSpec((S, LANE), lambda c, s: (s, cb0 + c)),
                  pl.BlockSpec((S, LANE), lambda c, s: (s, c)),
                  pl.BlockSpec((4, LANE), lambda c, s: (0, c))],
        out_specs=[pl.BlockSpec((S, LANE), lambda c, s: (s, c)),
                   pl.BlockSpec((8, LANE), lambda c, s: (0, c))],
        out_shape=[jax.ShapeDtypeStruct((T, C), F32), jax.ShapeDtypeStruct((8, C), F32)],
        compiler_params=_cp("parallel", "arbitrary"),
    )(x, dy, w)


def _scan(a, b, row):
    n = a.shape[0]
    d = 1
    while d < n:
        keep = row >= d
        b = a * jnp.where(keep, pltpu.roll(b, d, 0), 0.0) + b
        a = a * jnp.where(keep, pltpu.roll(a, d, 0), 1.0)
        d *= 2
    return b


def _rscan(a, b, row):
    n = a.shape[0]
    d = 1
    while d < n:
        keep = row < n - d
        b = a * jnp.where(keep, pltpu.roll(b, n - d, 0), 0.0) + b
        a = a * jnp.where(keep, pltpu.roll(a, n - d, 0), 1.0)
        d *= 2
    return b


GELU_C = math.sqrt(2.0 / math.pi)


def _gelu(x):
    t = jnp.tanh(GELU_C * (x + 0.044715 * (x * x * x)))
    return 0.5 * x * (1.0 + t), t


def _lru_gates(xr, wa, ba, wx, bx, lam):
    r = _sigmoid(_mm(xr, wa) + ba)
    i = _sigmoid(_mm(xr, wx) + bx)
    sp = jnp.maximum(-lam, 0.0) + jnp.log(1.0 + jnp.exp(-jnp.abs(lam)))
    la = -LRU_C * r * sp
    a = jnp.exp(la)
    e2 = a * a
    m = jnp.sqrt(-jnp.tanh(la) * (e2 + 1.0))
    return r, i, sp, a, e2, m


def lru_fwd(xr, u_lru, wa, wx, vec, S, name):
    T = xr.shape[0]

    def body(xr_ref, gt_ref, wa_ref, wx_ref, vec_ref, y_ref):
        x = xr_ref[...]
        row = _iota(x.shape, 0)
        r, i, sp, a, e2, m = _lru_gates(x, wa_ref[...], vec_ref[0:1, :], wx_ref[...], vec_ref[1:2, :],
                                        vec_ref[2:3, :])
        hh = _scan(a, m * (i * x), row)
        y_ref[...] = _gelu(gt_ref[...])[0] * hh

    return pl.pallas_call(
        body, name=name, grid=(T // S, LRU_W // LANE),
        in_specs=[pl.BlockSpec((S, LANE), lambda s, c: (s, c)),
                  pl.BlockSpec((S, LANE), lambda s, c: (s, 2 + c)),
                  pl.BlockSpec((LANE, LANE), lambda s, c: (c, c)),
                  pl.BlockSpec((LANE, LANE), lambda s, c: (c, c)),
                  pl.BlockSpec((8, LANE), lambda s, c: (0, c))],
        out_specs=pl.BlockSpec((S, LANE), lambda s, c: (s, c)),
        out_shape=jax.ShapeDtypeStruct((T, LRU_W), F32),
        compiler_params=_cp("parallel", "parallel"),
    )(xr, u_lru, wa, wx, vec)


def lru_bwd(xr, u_lru, dy, wa, wx, vec, S, name):
    T = xr.shape[0]

    def body(xr_ref, gt_ref, dy_ref, wa_ref, wx_ref, vec_ref,
             dxr_ref, dgt_ref, dwa_ref, dwx_ref, dvec_ref):
        @pl.when(pl.program_id(1) == 0)
        def _():
            dwa_ref[...] = jnp.zeros_like(dwa_ref)
            dwx_ref[...] = jnp.zeros_like(dwx_ref)
            dvec_ref[...] = jnp.zeros_like(dvec_ref)

        x = xr_ref[...]
        n = x.shape[0]
        row = _iota(x.shape, 0)
        lam = vec_ref[2:3, :]
        r, i, sp, a, e2, m = _lru_gates(x, wa_ref[...], vec_ref[0:1, :], wx_ref[...], vec_ref[1:2, :], lam)
        v = i * x
        hh = _scan(a, m * v, row)
        gt = gt_ref[...]
        dy = dy_ref[...]
        ge, t = _gelu(gt)
        dgt_ref[...] = dy * hh * (0.5 * (1.0 + t) + 0.5 * gt * (1.0 - t * t) * GELU_C
                                  * (1.0 + 3.0 * 0.044715 * gt * gt))
        a_next = jnp.where(row < n - 1, pltpu.roll(a, n - 1, 0), 0.0)
        G = _rscan(a_next, dy * ge, row)
        da = G * _shift_down(hh, 1, row)
        dv = G * m
        dla = da * a - (G * v) * e2 / m
        dr = dla * (-LRU_C * sp)
        dsp = jnp.sum(dla * (-LRU_C * r), axis=0, keepdims=True)
        dra = dr * r * (1.0 - r)
        dia = (dv * x) * i * (1.0 - i)
        dxr_ref[...] = dv * i + _mm_nt(dra, wa_ref[...]) + _mm_nt(dia, wx_ref[...])
        dwa_ref[0] += _mm_tn(x, dra)
        dwx_ref[0] += _mm_tn(x, dia)
        dvec_ref[0:1, :] += jnp.sum(dra, axis=0, keepdims=True)
        dvec_ref[1:2, :] += jnp.sum(dia, axis=0, keepdims=True)
        dvec_ref[2:3, :] += dsp * (-_sigmoid(-lam))

    return pl.pallas_call(
        body, name=name, grid=(LRU_W // LANE, T // S),
        in_specs=[pl.BlockSpec((S, LANE), lambda c, s: (s, c)),
                  pl.BlockSpec((S, LANE), lambda c, s: (s, 2 + c)),
                  pl.BlockSpec((S, LANE), lambda c, s: (s, c)),
                  pl.BlockSpec((LANE, LANE), lambda c, s: (c, c)),
                  pl.BlockSpec((LANE, LANE), lambda c, s: (c, c)),
                  pl.BlockSpec((8, LANE), lambda c, s: (0, c))],
        out_specs=[pl.BlockSpec((S, LANE), lambda c, s: (s, c)),
                   pl.BlockSpec((S, LANE), lambda c, s: (s, c)),
                   pl.BlockSpec((1, LANE, LANE), lambda c, s: (c, 0, 0)),
                   pl.BlockSpec((1, LANE, LANE), lambda c, s: (c, 0, 0)),
                   pl.BlockSpec((8, LANE), lambda c, s: (0, c))],
        out_shape=[jax.ShapeDtypeStruct((T, LRU_W), F32), jax.ShapeDtypeStruct((T, LRU_W), F32),
                   jax.ShapeDtypeStruct((2, LANE, LANE), F32), jax.ShapeDtypeStruct((2, LANE, LANE), F32),
                   jax.ShapeDtypeStruct((8, LRU_W), F32)],
        compiler_params=_cp("parallel", "arbitrary"),
    )(xr, u_lru, dy, wa, wx, vec)


def _bucket_table():
    qi = np.arange(BQ)[:, None]
    kj = np.arange(2 * BQ)[None, :]
    dist = BQ + qi - kj
    band = (dist >= 0) & (dist < WINDOW)
    dd = np.maximum(dist, 0)
    max_exact = REL_BUCKETS // 2
    large = max_exact + (np.log(np.maximum(dd, 1).astype(np.float32) / np.float32(max_exact))
                         / np.float32(math.log(REL_MAX_DIST / max_exact))
                         * np.float32(REL_BUCKETS - max_exact)).astype(np.int32)
    large = np.minimum(large, REL_BUCKETS - 1)
    bucket = np.where(dd < max_exact, dd, large)
    return np.where(band, bucket, -1).astype(np.int32)


def _att_specs(S):
    nb = S // BQ
    qc = ATT_W // LANE
    return [pl.BlockSpec((BQ, ATT_W), lambda b, n: (b * nb + n, 0)),
            pl.BlockSpec((BQ, KV_W), lambda b, n: (b * nb + jnp.maximum(n - 1, 0), qc)),
            pl.BlockSpec((BQ, KV_W), lambda b, n: (b * nb + n, qc)),
            pl.BlockSpec((BQ, KV_W), lambda b, n: (b * nb + jnp.maximum(n - 1, 0), qc + 1)),
            pl.BlockSpec((BQ, KV_W), lambda b, n: (b * nb + n, qc + 1))]


def _att_bias(bk, rb_ref, bias_s):
    for h in range(ATT_H):
        acc = jnp.zeros(bk.shape, F32)
        for bb in range(REL_BUCKETS):
            acc = jnp.where(bk == bb, rb_ref[bb * ATT_H + h], acc)
        bias_s[h] = acc


def _att_probs(qh, kg, bias, valid, sink):
    s = _mm_nt(qh, kg) * (HD ** -0.5) + bias
    s = jnp.where(valid, s, NEG)
    m = jnp.maximum(jnp.max(s, axis=-1, keepdims=True), sink)
    e = jnp.exp(s - m)
    es = jnp.exp(sink - m)
    den = jnp.sum(e, axis=-1, keepdims=True) + es
    return e / den, es / den


def attn_fwd(u_att, sinks, rel_bias, S, name):
    T = u_att.shape[0]
    nb = S // BQ
    table = jnp.asarray(_bucket_table())

    def body(sk_ref, rb_ref, bk_ref, q_ref, kp_ref, kc_ref, vp_ref, vc_ref, o_ref, bias_s):
        b = pl.program_id(0)
        n = pl.program_id(1)
        bk = bk_ref[...]

        @pl.when((b == 0) & (n == 0))
        def _():
            _att_bias(bk, rb_ref, bias_s)

        valid = (bk >= 0) & ((n > 0) | (_iota(bk.shape, 1) >= BQ))
        for h in range(ATT_H):
            gs = slice(HD * (h // ATT_G), HD * (h // ATT_G + 1))
            kg = jnp.concatenate([kp_ref[:, gs], kc_ref[:, gs]], axis=0)
            vg = jnp.concatenate([vp_ref[:, gs], vc_ref[:, gs]], axis=0)
            p, _ = _att_probs(q_ref[:, HD * h:HD * (h + 1)], kg, bias_s[h], valid, sk_ref[h])
            o_ref[:, HD * h:HD * (h + 1)] = _mm(p, vg)

    smem = pl.BlockSpec(memory_space=pltpu.SMEM)
    return pl.pallas_call(
        body, name=name, grid=(T // S, nb),
        in_specs=[smem, smem, _full_spec((BQ, 2 * BQ))] + _att_specs(S),
        out_specs=pl.BlockSpec((BQ, ATT_W), lambda b, n: (b * nb + n, 0)),
        out_shape=jax.ShapeDtypeStruct((T, ATT_W), F32),
        scratch_shapes=[pltpu.VMEM((ATT_H, BQ, 2 * BQ), F32)],
        compiler_params=_cp("arbitrary", "arbitrary"),
    )(sinks, rel_bias, table, u_att, u_att, u_att, u_att, u_att)


def attn_bwd(u_att, dy, sinks, rel_bias, S, name):
    T = u_att.shape[0]
    nb = S // BQ
    nB = T // S
    table = jnp.asarray(_bucket_table())
    scale = HD ** -0.5

    def body(sk_ref, rb_ref, bk_ref, q_ref, kp_ref, kc_ref, vp_ref, vc_ref, dy_ref,
             du_ref, drel_ref, dsk_ref, bias_s, dbias_s):
        b = pl.program_id(0)
        n = pl.program_id(1)
        bk = bk_ref[...]

        @pl.when((b == 0) & (n == 0))
        def _():
            _att_bias(bk, rb_ref, bias_s)
            dbias_s[...] = jnp.zeros_like(dbias_s)
            dsk_ref[...] = jnp.zeros_like(dsk_ref)
            drel_ref[...] = jnp.zeros_like(drel_ref)

        @pl.when(n == 0)
        def _():
            du_ref[...] = jnp.zeros_like(du_ref)

        valid = (bk >= 0) & ((n > 0) | (_iota(bk.shape, 1) >= BQ))
        r_cur = pl.multiple_of(n * BQ, BQ)
        r_prev = pl.multiple_of(jnp.maximum(n - 1, 0) * BQ, BQ)
        for g in range(ATT_KV):
            gs = slice(HD * g, HD * (g + 1))
            kg = jnp.concatenate([kp_ref[:, gs], kc_ref[:, gs]], axis=0)
            vg = jnp.concatenate([vp_ref[:, gs], vc_ref[:, gs]], axis=0)
            dk = jnp.zeros((2 * BQ, HD), F32)
            dv = jnp.zeros((2 * BQ, HD), F32)
            for e in range(ATT_G):
                h = g * ATT_G + e
                qh = q_ref[:, HD * h:HD * (h + 1)]
                do = dy_ref[:, HD * h:HD * (h + 1)]
                p, ps = _att_probs(qh, kg, bias_s[h], valid, sk_ref[h])
                dp = _mm_nt(do, vg)
                delta = jnp.sum(p * dp, axis=-1, keepdims=True)
                ds = p * (dp - delta)
                dbias_s[h] += ds
                dsk_ref[h:h + 1, :] += jnp.broadcast_to(
                    jnp.sum(-ps * delta, axis=0, keepdims=True), (1, LANE))
                dss = ds * scale
                du_ref[pl.ds(r_cur, BQ), HD * h:HD * (h + 1)] = _mm(dss, kg)
                dk += _mm_tn(dss, qh)
                dv += _mm_tn(p, do)
            ck = ATT_W + HD * g
            cv = ATT_W + KV_W + HD * g
            du_ref[pl.ds(r_prev, BQ), ck:ck + HD] += dk[0:BQ]
            du_ref[pl.ds(r_cur, BQ), ck:ck + HD] += dk[BQ:]
            du_ref[pl.ds(r_prev, BQ), cv:cv + HD] += dv[0:BQ]
            du_ref[pl.ds(r_cur, BQ), cv:cv + HD] += dv[BQ:]

        @pl.when((b == nB - 1) & (n == nb - 1))
        def _():
            lane = _iota((1, LANE), 1)
            for h in range(ATT_H):
                db = dbias_s[h]
                acc = jnp.zeros((1, LANE), F32)
                for bb in range(REL_BUCKETS):
                    val = jnp.sum(jnp.sum(jnp.where(bk == bb, db, 0.0), axis=1, keepdims=True),
                                  axis=0, keepdims=True)
                    acc = jnp.where(lane == bb, val, acc)
                drel_ref[h:h + 1, :] = acc

    smem = pl.BlockSpec(memory_space=pltpu.SMEM)
    return pl.pallas_call(
        body, name=name, grid=(nB, nb),
        in_specs=[smem, smem, _full_spec((BQ, 2 * BQ))] + _att_specs(S)
        + [pl.BlockSpec((BQ, ATT_W), lambda b, n: (b * nb + n, 0))],
        out_specs=[pl.BlockSpec((S, ATT_W + 2 * KV_W), lambda b, n: (b, 0)),
                   _full_spec((8, LANE)), _full_spec((8, LANE))],
        out_shape=[jax.ShapeDtypeStruct((T, ATT_W + 2 * KV_W), F32),
                   jax.ShapeDtypeStruct((8, LANE), F32), jax.ShapeDtypeStruct((8, LANE), F32)],
        scratch_shapes=[pltpu.VMEM((ATT_H, BQ, 2 * BQ), F32), pltpu.VMEM((ATT_H, BQ, 2 * BQ), F32)],
        compiler_params=_cp("arbitrary", "arbitrary"),
    )(sinks, rel_bias, table, u_att, u_att, u_att, u_att, u_att, dy)


def _head_of(i):
    return lax.shift_right_logical(i, 6)


def _head_mask(shape):
    return (_head_of(_iota(shape, 0)) == _head_of(_iota(shape, 1))).astype(F32)


def _dn_point(c, uba, alog, dtb):
    s = c * _sigmoid(c)
    qt, kt, vt = s[:, 0:256], s[:, 256:512], s[:, 512:768]
    ones_bd = _head_mask((DN_W, DN_W))
    q = qt * lax.rsqrt(_mmx(qt * qt, ones_bd) + EPS) * (HD ** -0.5)
    k = kt * lax.rsqrt(_mmx(kt * kt, ones_bd) + EPS)
    sel = _head_of(_iota((LANE, DN_W), 1))
    row = _iota((LANE, DN_W), 0)
    braw = _mmx(uba, (row == sel).astype(F32))
    araw = _mmx(uba, (row == sel + DN_H).astype(F32)) + dtb
    beta = _sigmoid(braw)
    g = -jnp.exp(alog) * (jnp.maximum(araw, 0.0) + jnp.log(1.0 + jnp.exp(-jnp.abs(araw))))
    return q, k, vt, g, beta


def dn_point_fwd(c, uba, alog, dtb, name):
    T = c.shape[0]
    tm = min(TOK_TILE, T)

    def body(c_ref, u_ref, al_ref, dt_ref, *outs):
        for ref, val in zip(outs, _dn_point(c_ref[...], u_ref[...], al_ref[...], dt_ref[...])):
            ref[...] = val

    return pl.pallas_call(
        body, name=name, grid=(T // tm,),
        in_specs=[_row_spec(tm, 768), _row_spec(tm, LANE), _full_spec((1, DN_W)), _full_spec((1, DN_W))],
        out_specs=[_row_spec(tm, DN_W)] * 5,
        out_shape=[jax.ShapeDtypeStruct((T, DN_W), F32)] * 5,
        compiler_params=_cp("parallel"),
    )(c, uba, alog, dtb)


def dn_point_bwd(c, uba, alog, dtb, douts, name):
    T = c.shape[0]
    tm = min(TOK_TILE, T)

    def body(c_ref, u_ref, al_ref, dt_ref, dq, dk, dv, dg, db, dc_ref, du_ref, dvec_ref):
        @pl.when(pl.program_id(0) == 0)
        def _():
            dvec_ref[...] = jnp.zeros_like(dvec_ref)

        _, vjp = jax.vjp(_dn_point, c_ref[...], u_ref[...], al_ref[...], dt_ref[...])
        dc, du, dal, ddt = vjp((dq[...], dk[...], dv[...], dg[...], db[...]))
        dc_ref[...] = dc
        du_ref[...] = du
        fold = (_iota((LANE, DN_W), 0) == _head_of(_iota((LANE, DN_W), 1))).astype(F32)
        both = jnp.concatenate([dal, ddt, jnp.zeros((6, DN_W), F32)], axis=0)
        dvec_ref[...] += _mmx_nt(both, fold)

    return pl.pallas_call(
        body, name=name, grid=(T // tm,),
        in_specs=[_row_spec(tm, 768), _row_spec(tm, LANE), _full_spec((1, DN_W)), _full_spec((1, DN_W))]
        + [_row_spec(tm, DN_W)] * 5,
        out_specs=[_row_spec(tm, 768), _row_spec(tm, LANE), _full_spec((8, LANE))],
        out_shape=[jax.ShapeDtypeStruct((T, 768), F32), jax.ShapeDtypeStruct((T, LANE), F32),
                   jax.ShapeDtypeStruct((8, LANE), F32)],
        compiler_params=_cp("arbitrary"),
    )(c, uba, alog, dtb, *douts)


def _dn_chunk(state, q, k, v, g, beta):
    hm = _head_mask((DN_W, DN_W))
    ri = _iota((DN_W, DN_W), 0) & (CHUNK - 1)
    ci = _iota((DN_W, DN_W), 1) & (CHUNK - 1)
    tril = hm * (ri >= ci).astype(F32)
    strict = hm * (ri > ci).astype(F32)
    eye = (_iota((DN_W, DN_W), 0) == _iota((DN_W, DN_W), 1)).astype(F32)
    tri64 = (_iota((CHUNK, CHUNK), 0) >= _iota((CHUNK, CHUNK), 1)).astype(F32)

    def stack(x):
        return jnp.concatenate([x, x, x, x], axis=0) * hm

    gc = _mmx(tri64, g)
    glast = jnp.sum(g, axis=0, keepdims=True)
    eg = jnp.exp(gc)
    kb = k * beta
    qs, ks = stack(q), stack(k)
    gcol = jnp.sum(stack(gc), axis=1, keepdims=True) * (1.0 / HD)
    grow = _mmx(jnp.ones((DN_W, DN_W), F32), eye * gcol)
    decay = jnp.exp(jnp.minimum(gcol - grow, 0.0))
    lmat = _mm_nt(stack(kb), ks) * decay * strict
    tinv = eye - lmat
    pw = lmat
    for _ in range(5):
        pw = _mmx(pw, pw)
        tinv = tinv + _mmx(tinv, pw)
    u = _mm(tinv, stack(v * beta))
    w = _mm(tinv, stack(kb * eg))
    vn = u - _mm(w, state)
    att = _mm_nt(qs, ks) * decay * tril
    o4 = _mm(stack(q * eg), state) + _mm(att, vn)
    o = o4[0:64] + o4[64:128] + o4[128:192] + o4[192:256]
    new_state = state * jnp.exp(glast) + _mm_tn(stack(k * jnp.exp(glast - gc)), vn)
    return o, new_state


def dn_scan_fwd(q, k, v, g, beta, S, name):
    T = q.shape[0]
    nc = S // CHUNK

    def body(q_ref, k_ref, v_ref, g_ref, b_ref, o_ref, st_ref, s_s):
        @pl.when(pl.program_id(1) == 0)
        def _():
            s_s[...] = jnp.zeros_like(s_s)

        st = s_s[...]
        st_ref[0] = st
        o, new = _dn_chunk(st, q_ref[...], k_ref[...], v_ref[...], g_ref[...], b_ref[...])
        o_ref[...] = o
        s_s[...] = new

    spec = pl.BlockSpec((CHUNK, DN_W), lambda b, t: (b * nc + t, 0))
    return pl.pallas_call(
        body, name=name, grid=(T // S, nc),
        in_specs=[spec] * 5,
        out_specs=[spec, pl.BlockSpec((1, DN_W, DN_W), lambda b, t: (b * nc + t, 0, 0))],
        out_shape=[jax.ShapeDtypeStruct((T, DN_W), F32),
                   jax.ShapeDtypeStruct((T // CHUNK, DN_W, DN_W), F32)],
        scratch_shapes=[pltpu.VMEM((DN_W, DN_W), F32)],
        compiler_params=_cp("parallel", "arbitrary"),
    )(q, k, v, g, beta)


def dn_scan_bwd(q, k, v, g, beta, states, do, S, name):
    T = q.shape[0]
    nc = S // CHUNK

    def body(q_ref, k_ref, v_ref, g_ref, b_ref, st_ref, do_ref, dq, dk, dv, dg, db, ds_s):
        @pl.when(pl.program_id(1) == 0)
        def _():
            ds_s[...] = jnp.zeros_like(ds_s)

        _, vjp = jax.vjp(_dn_chunk, st_ref[0], q_ref[...], k_ref[...], v_ref[...], g_ref[...], b_ref[...])
        grads = vjp((do_ref[...], ds_s[...]))
        ds_s[...] = grads[0]
        for ref, val in zip((dq, dk, dv, dg, db), grads[1:]):
            ref[...] = val

    spec = pl.BlockSpec((CHUNK, DN_W), lambda b, t: (b * nc + nc - 1 - t, 0))
    return pl.pallas_call(
        body, name=name, grid=(T // S, nc),
        in_specs=[spec] * 5 + [pl.BlockSpec((1, DN_W, DN_W), lambda b, t: (b * nc + nc - 1 - t, 0, 0)), spec],
        out_specs=[spec] * 5,
        out_shape=[jax.ShapeDtypeStruct((T, DN_W), F32)] * 5,
        scratch_shapes=[pltpu.VMEM((DN_W, DN_W), F32)],
        compiler_params=_cp("parallel", "arbitrary"),
    )(q, k, v, g, beta, states, do)


def _dn_gate(o, z, nl):
    ms = _mmx(o * o, _head_mask((DN_W, DN_W))) * (1.0 / HD)
    return o * lax.rsqrt(ms + EPS) * nl * (z * _sigmoid(z))


def dn_gate_fwd(o, u_dn, nl, name):
    T = o.shape[0]
    tm = min(TOK_TILE, T)

    def body(o_ref, z_ref, n_ref, y_ref):
        y_ref[...] = _dn_gate(o_ref[...], z_ref[...], n_ref[...])

    return pl.pallas_call(
        body, name=name, grid=(T // tm,),
        in_specs=[_row_spec(tm, DN_W), pl.BlockSpec((tm, DN_W), lambda i: (i, 3)), _full_spec((1, DN_W))],
        out_specs=_row_spec(tm, DN_W),
        out_shape=jax.ShapeDtypeStruct((T, DN_W), F32),
        compiler_params=_cp("parallel"),
    )(o, u_dn, nl)


def dn_gate_bwd(o, u_dn, nl, dy, name):
    T = o.shape[0]
    tm = min(TOK_TILE, T)

    def body(o_ref, z_ref, n_ref, dy_ref, do_ref, dz_ref, dn_ref):
        @pl.when(pl.program_id(0) == 0)
        def _():
            dn_ref[...] = jnp.zeros_like(dn_ref)

        _, vjp = jax.vjp(_dn_gate, o_ref[...], z_ref[...], n_ref[...])
        do, dz, dn = vjp(dy_ref[...])
        do_ref[...] = do
        dz_ref[...] = dz
        fold = (_iota((LANE, DN_W), 0) == (_iota((LANE, DN_W), 1) & (HD - 1))).astype(F32)
        dn_ref[...] += _mmx_nt(jnp.concatenate([dn, jnp.zeros((7, DN_W), F32)], axis=0), fold)

    return pl.pallas_call(
        body, name=name, grid=(T // tm,),
        in_specs=[_row_spec(tm, DN_W), pl.BlockSpec((tm, DN_W), lambda i: (i, 3)), _full_spec((1, DN_W)),
                  _row_spec(tm, DN_W)],
        out_specs=[_row_spec(tm, DN_W), _row_spec(tm, DN_W), _full_spec((8, LANE))],
        out_shape=[jax.ShapeDtypeStruct((T, DN_W), F32), jax.ShapeDtypeStruct((T, DN_W), F32),
                   jax.ShapeDtypeStruct((8, LANE), F32)],
        compiler_params=_cp("arbitrary"),
    )(o, u_dn, nl, dy)


Y_SPLITS = (LRU_W, ATT_W, DN_W)
Y_OFFS = (0, LRU_W, LRU_W + ATT_W)


def wout_fwd(h, ys, w_out, name):
    T = h.shape[0]
    tm = min(TOK_TILE, T)

    def body(h_ref, y0, y1, y2, w_ref, o_ref, yc_ref):
        acc = h_ref[...]
        for ref, off, n in zip((y0, y1, y2), Y_OFFS, Y_SPLITS):
            y = ref[...].astype(BF16)
            yc_ref[:, off:off + n] = y
            acc += _mm(y, w_ref[off:off + n, :])
        o_ref[...] = acc

    return pl.pallas_call(
        body, name=name, grid=(T // tm,),
        in_specs=[_row_spec(tm, D)] + [_row_spec(tm, n) for n in Y_SPLITS] + [_full_spec((D, D))],
        out_specs=[_row_spec(tm, D), _row_spec(tm, D)],
        out_shape=[jax.ShapeDtypeStruct((T, D), F32), jax.ShapeDtypeStruct((T, D), BF16)],
        compiler_params=_cp("parallel"),
    )(h, *ys, w_out)


def wout_bwd(dy, w_out, name):
    T = dy.shape[0]
    tm = min(TOK_TILE, T)

    def body(dy_ref, w_ref, d0, d1, d2):
        dd = dy_ref[...].astype(BF16)
        for ref, off, n in zip((d0, d1, d2), Y_OFFS, Y_SPLITS):
            ref[...] = _mm_nt(dd, w_ref[off:off + n, :])

    return pl.pallas_call(
        body, name=name, grid=(T // tm,),
        in_specs=[_row_spec(tm, D), _full_spec((D, D))],
        out_specs=[_row_spec(tm, n) for n in Y_SPLITS],
        out_shape=[jax.ShapeDtypeStruct((T, n), F32) for n in Y_SPLITS],
        compiler_params=_cp("parallel"),
    )(dy, w_out)


def ple_fwd(h, g, pe, wg, wp, name):
    T = h.shape[0]
    tm = min(TOK_TILE, T)

    def body(h_ref, g_ref, p_ref, wg_ref, wp_ref, o_ref):
        hh = h_ref[...]
        xn = _rms(hh, g_ref[...])[0]
        o_ref[...] = hh + _sigmoid(_mm(xn, wg_ref[...])) * _mm(p_ref[...], wp_ref[...])

    return pl.pallas_call(
        body, name=name, grid=(T // tm,),
        in_specs=[_row_spec(tm, D), _full_spec((1, D)), _row_spec(tm, PLE), _full_spec((D, D)),
                  _full_spec((PLE, D))],
        out_specs=_row_spec(tm, D),
        out_shape=jax.ShapeDtypeStruct((T, D), F32),
        compiler_params=_cp("parallel"),
    )(h, g, pe, wg, wp)


def ple_bwd(h, dy, g, pe, wg, wp, name):
    T = h.shape[0]
    tm = min(TOK_TILE, T)

    def body(h_ref, dy_ref, g_ref, p_ref, wg_ref, wp_ref, dh_ref, dz_ref, dpp_ref, xn_ref, dn_ref):
        @pl.when(pl.program_id(0) == 0)
        def _():
            dn_ref[...] = jnp.zeros_like(dn_ref)

        gg = g_ref[...]
        dy = dy_ref[...]
        xn, xhat, rstd = _rms(h_ref[...], gg)
        gate = _sigmoid(_mm(xn, wg_ref[...]))
        pp = _mm(p_ref[...], wp_ref[...])
        dz = dy * pp * gate * (1.0 - gate)
        dz_ref[...] = dz.astype(BF16)
        dpp_ref[...] = (dy * gate).astype(BF16)
        xn_ref[...] = xn.astype(BF16)
        dh, dn = _rms_bwd(_mm_nt(dz, wg_ref[...]), xhat, rstd, gg)
        dh_ref[...] = dy + dh
        dn_ref[...] += dn

    return pl.pallas_call(
        body, name=name, grid=(T // tm,),
        in_specs=[_row_spec(tm, D), _row_spec(tm, D), _full_spec((1, D)), _row_spec(tm, PLE),
                  _full_spec((D, D)), _full_spec((PLE, D))],
        out_specs=[_row_spec(tm, D), _row_spec(tm, D), _row_spec(tm, D), _row_spec(tm, D), _full_spec((1, D))],
        out_shape=[jax.ShapeDtypeStruct((T, D), F32), jax.ShapeDtypeStruct((T, D), BF16),
                   jax.ShapeDtypeStruct((T, D), BF16), jax.ShapeDtypeStruct((T, D), BF16),
                   jax.ShapeDtypeStruct((1, D), F32)],
        compiler_params=_cp("arbitrary"),
    )(h, dy, g, pe, wg, wp)


def loss_head(h, g, target, name):
    T = h.shape[0]
    tm = min(TOK_TILE, T)

    def body(h_ref, g_ref, t_ref, loss_ref, dh_ref, dn_ref):
        @pl.when(pl.program_id(0) == 0)
        def _():
            dn_ref[...] = jnp.zeros_like(dn_ref)
            loss_ref[...] = jnp.zeros_like(loss_ref)

        gg = g_ref[...]
        y, xhat, rstd = _rms(h_ref[...], gg)
        err = y - t_ref[...]
        per_tok = jnp.mean(err * err, axis=-1, keepdims=True)
        loss_ref[...] += 0.5 * jnp.sum(per_tok, axis=0, keepdims=True)
        dh, dn = _rms_bwd(err * (1.0 / D), xhat, rstd, gg)
        dh_ref[...] = dh
        dn_ref[...] += dn

    return pl.pallas_call(
        body, name=name, grid=(T // tm,),
        in_specs=[_row_spec(tm, D), _full_spec((1, D)), _row_spec(tm, D)],
        out_specs=[_full_spec((8, LANE)), _row_spec(tm, D), _full_spec((1, D))],
        out_shape=[jax.ShapeDtypeStruct((8, LANE), F32), jax.ShapeDtypeStruct((T, D), F32),
                   jax.ShapeDtypeStruct((1, D), F32)],
        compiler_params=_cp("arbitrary"),
    )(h, g, target)


def _block_diag(w):
    return jnp.einsum('hij,hk->hikj', w, jnp.eye(4, dtype=w.dtype)).reshape(LRU_W, LRU_W)


def _layer_consts(W, l):
    row = lambda v: v.reshape(1, -1)
    zeros = jnp.zeros((5, LRU_W), F32)
    return dict(
        wa=_block_diag(W["lru_w_a"][l]), wx=_block_diag(W["lru_w_x"][l]),
        lru_vec=jnp.concatenate([row(W["lru_b_a"][l]), row(W["lru_b_x"][l]), row(W["lru_lambda"][l]), zeros], 0),
        lru_cb=row(W["lru_conv_b"][l]),
        sinks=W["attn_sinks"][l], rel=W["rel_bias"].reshape(-1),
        dn_cb=jnp.zeros((1, 3 * DN_W), F32),
        alog=row(jnp.repeat(W["dn_a_log"][l], HD)), dtb=row(jnp.repeat(W["dn_dt_bias"][l], HD)),
        dn_nl=row(jnp.tile(W["dn_norm"][l], DN_H)),
    )


def _layer_fwd(h0, pe, W, l, S):
    n = f"l{l}_"
    c_ = _layer_consts(W, l)
    row = lambda v: v.reshape(1, -1)
    h1 = ffn_fwd(h0, row(W["ffn1_norm"][l]), W["ffn1_w_gate"][l], W["ffn1_w_up"][l], W["ffn1_w_down"][l],
                 n + "ffn1_fwd")
    u_lru, u_att, u_dn, u_ba, xn_mix = mixin_fwd(h1, row(W["mix_norm"][l]), W["w_in"][l], n + "mixin_fwd")
    xr = conv_fwd(u_lru, W["lru_conv_w"][l], c_["lru_cb"], S, 0, LRU_W, n + "lru_conv_fwd")
    y_lru = lru_fwd(xr, u_lru, c_["wa"], c_["wx"], c_["lru_vec"], S, n + "lru_fwd")
    y_att = attn_fwd(u_att, c_["sinks"], c_["rel"], S, n + "attn_fwd")
    cc = conv_fwd(u_dn, W["dn_conv_w"][l], c_["dn_cb"], S, 0, 3 * DN_W, n + "dn_conv_fwd")
    q, k, v, g, beta = dn_point_fwd(cc, u_ba, c_["alog"], c_["dtb"], n + "dn_point_fwd")
    o, states = dn_scan_fwd(q, k, v, g, beta, S, n + "dn_scan_fwd")
    y_dn = dn_gate_fwd(o, u_dn, c_["dn_nl"], n + "dn_gate_fwd")
    h2, ycat = wout_fwd(h1, (y_lru, y_att, y_dn), W["w_out"][l], n + "wout_fwd")
    h3 = ffn_fwd(h2, row(W["ffn2_norm"][l]), W["ffn2_w_gate"][l], W["ffn2_w_up"][l], W["ffn2_w_down"][l],
                 n + "ffn2_fwd")
    h4 = ple_fwd(h3, row(W["ple_norm"][l]), pe, W["ple_w_gate"][l], W["ple_w_proj"][l], n + "ple_fwd")
    saved = dict(h0=h0, h1=h1, h2=h2, h3=h3, u_lru=u_lru, u_att=u_att, u_dn=u_dn, u_ba=u_ba, xn_mix=xn_mix,
                 xr=xr, cc=cc, q=q, k=k, v=v, g=g, beta=beta, o=o, states=states, ycat=ycat)
    return h4, saved


def _layer_bwd(dh4, sv, pe, W, l, S):
    n = f"l{l}_"
    c_ = _layer_consts(W, l)
    row = lambda v: v.reshape(1, -1)
    G = {}
    dh3, dz, dpp, xn_p, dn = ple_bwd(sv["h3"], dh4, row(W["ple_norm"][l]), pe, W["ple_w_gate"][l],
                                     W["ple_w_proj"][l], n + "ple_bwd")
    G["ple_norm"] = dn[0]
    G["ple_w_gate"] = matmul_tn(xn_p, dz, n + "d_ple_w_gate")
    G["ple_w_proj"] = matmul_tn(pe, dpp, n + "d_ple_w_proj")

    def ffn_back(which, h_in, dy):
        dh, dgt, dup, act, xn, dn_ = ffn_bwd(h_in, dy, row(W[which + "_norm"][l]), W[which + "_w_gate"][l],
                                             W[which + "_w_up"][l], W[which + "_w_down"][l], n + which + "_bwd")
        G[which + "_norm"] = dn_[0]
        G[which + "_w_gate"] = matmul_tn(xn, dgt, n + "d_" + which + "_w_gate")
        G[which + "_w_up"] = matmul_tn(xn, dup, n + "d_" + which + "_w_up")
        G[which + "_w_down"] = matmul_tn(act, dy, n + "d_" + which + "_w_down", scale=0.5)
        return dh

    dh2 = ffn_back("ffn2", sv["h2"], dh3)
    dy_lru, dy_att, dy_dn = wout_bwd(dh2, W["w_out"][l], n + "wout_bwd")
    G["w_out"] = matmul_tn(sv["ycat"], dh2, n + "d_w_out")
    do, dz_dn, dnn = dn_gate_bwd(sv["o"], sv["u_dn"], c_["dn_nl"], dy_dn, n + "dn_gate_bwd")
    dqkvgb = dn_scan_bwd(sv["q"], sv["k"], sv["v"], sv["g"], sv["beta"], sv["states"], do, S, n + "dn_scan_bwd")
    dcc, du_ba, dvec_dn = dn_point_bwd(sv["cc"], sv["u_ba"], c_["alog"], c_["dtb"], dqkvgb, n + "dn_point_bwd")
    dqkv, dwb_dn = conv_bwd(sv["u_dn"], dcc, W["dn_conv_w"][l], S, 0, 3 * DN_W, n + "dn_conv_bwd")
    du_dn = jnp.concatenate([dqkv, dz_dn], axis=1)
    G["dn_norm"] = dnn[0, 0:HD]
    G["dn_a_log"] = dvec_dn[0, 0:DN_H]
    G["dn_dt_bias"] = dvec_dn[1, 0:DN_H]
    G["dn_conv_w"] = dwb_dn[0:4]
    du_att, drel, dsk = attn_bwd(sv["u_att"], dy_att, c_["sinks"], c_["rel"], S, n + "attn_bwd")
    G["attn_sinks"] = dsk[:, 0]
    G["rel_bias"] = drel[:, 0:REL_BUCKETS].T
    dxr, dgt_lru, dwa, dwx, dvec = lru_bwd(sv["xr"], sv["u_lru"], dy_lru, c_["wa"], c_["wx"], c_["lru_vec"], S,
                                           n + "lru_bwd")
    dx_lru, dwb_lru = conv_bwd(sv["u_lru"], dxr, W["lru_conv_w"][l], S, 0, LRU_W, n + "lru_conv_bwd")
    du_lru = jnp.concatenate([dx_lru, dgt_lru], axis=1)
    diag = lambda m: jnp.stack([m[c, HD * e:HD * (e + 1), HD * e:HD * (e + 1)] for c in range(2) for e in range(2)])
    G["lru_w_a"], G["lru_w_x"] = diag(dwa), diag(dwx)
    G["lru_b_a"], G["lru_b_x"], G["lru_lambda"] = dvec[0], dvec[1], dvec[2]
    G["lru_conv_w"], G["lru_conv_b"] = dwb_lru[0:4], dwb_lru[4]
    dh1, du_cat, dn = mixin_bwd(sv["h1"], dh2, row(W["mix_norm"][l]), W["w_in"][l],
                                (du_lru, du_att, du_dn, du_ba), n + "mixin_bwd")
    G["mix_norm"] = dn[0]
    G["w_in"] = matmul_tn(sv["xn_mix"], du_cat, n + "d_w_in")
    dh0 = ffn_back("ffn1", sv["h0"], dh1)
    return dh0, G


def _core(x, pe, W, target, S):
    h = x
    saved = []
    for l in range(DEPTH):
        h, sv = _layer_fwd(h, pe[l], W, l, S)
        saved.append(sv)
    loss_tile, dh, dfn = loss_head(h, W["final_norm"].reshape(1, -1), target, "loss_head")
    grads = [None] * DEPTH
    for l in reversed(range(DEPTH)):
        dh, grads[l] = _layer_bwd(dh, saved[l], pe[l], W, l, S)
    return loss_tile[0, 0], dh, grads, dfn[0]


MESH_ID = pl.DeviceIdType.MESH
ANY_SPEC = pl.BlockSpec(memory_space=pl.ANY)
AXES = ("x", "y", "c")


def _my_pos():
    return lax.axis_index("x"), lax.axis_index("y"), lax.axis_index("c")


def _slot_of(px, py, pc):
    return 4 * px + 2 * py + pc


def all_gather(x, name):
    R, C = x.shape

    def body(x_ref, out_ref, send_sems, recv_sems, local_sem):
        mx, my, mc = _my_pos()
        me, sibling = (mx, my, mc), (mx, my, 1 - mc)
        chips = [(1 - mx, my), (mx, 1 - my), (1 - mx, 1 - my)]

        def copy(k, block, to, src=None):
            dst = out_ref.at[_slot_of(*block)]
            return pltpu.make_async_remote_copy(
                src_ref=dst if src is None else src, dst_ref=dst,
                send_sem=send_sems.at[k], recv_sem=recv_sems.at[k],
                device_id=to, device_id_type=MESH_ID)

        mine = pltpu.make_async_copy(x_ref, out_ref.at[_slot_of(*me)], local_sem)
        mine.start()
        first = [copy(0, me, sibling, src=x_ref)]
        first += [copy(1 + j, me, (*chip, mc), src=x_ref) for j, chip in enumerate(chips)]
        for cp in first:
            cp.start()
        passed = [copy(4 + j, (*chip, mc), sibling) for j, chip in enumerate(chips)]
        for j, chip in enumerate(chips):
            copy(1 + j, (*chip, mc), me).wait_recv()
            passed[j].start()
        copy(0, sibling, me).wait_recv()
        for j, chip in enumerate(chips):
            copy(4 + j, (*chip, 1 - mc), me).wait_recv()
        for cp in first + passed:
            cp.wait_send()
        mine.wait()

    return pl.pallas_call(
        body, name=name,
        out_shape=jax.ShapeDtypeStruct((N_DEV, R, C), x.dtype),
        in_specs=[ANY_SPEC], out_specs=ANY_SPEC,
        scratch_shapes=[pltpu.SemaphoreType.DMA((7,)), pltpu.SemaphoreType.DMA((7,)), pltpu.SemaphoreType.DMA],
    )(x)


def exchange(x, name):
    _, R, C = x.shape

    def body(x_ref, out_ref, send_sems, recv_sems, local_sem):
        mx, my, mc = _my_pos()
        mine = _slot_of(mx, my, mc)
        local = pltpu.make_async_copy(x_ref.at[mine], out_ref.at[mine], local_sem)
        local.start()
        sent = []
        for r in range(1, N_DEV):
            peer = (1 - mx if r & 4 else mx, 1 - my if r & 2 else my, 1 - mc if r & 1 else mc)
            ps = _slot_of(*peer)
            cp = pltpu.make_async_remote_copy(
                src_ref=x_ref.at[ps], dst_ref=out_ref.at[mine],
                send_sem=send_sems.at[r - 1], recv_sem=recv_sems.at[r - 1],
                device_id=peer, device_id_type=MESH_ID)
            cp.start()
            sent.append((cp, r, peer, ps))
        for cp, r, peer, ps in sent:
            pltpu.make_async_remote_copy(
                src_ref=x_ref.at[ps], dst_ref=out_ref.at[ps],
                send_sem=send_sems.at[r - 1], recv_sem=recv_sems.at[r - 1],
                device_id=peer, device_id_type=MESH_ID).wait_recv()
        for cp, _, _, _ in sent:
            cp.wait_send()
        local.wait()

    return pl.pallas_call(
        body, name=name,
        out_shape=jax.ShapeDtypeStruct((N_DEV, R, C), x.dtype),
        in_specs=[ANY_SPEC], out_specs=ANY_SPEC,
        scratch_shapes=[pltpu.SemaphoreType.DMA((7,)), pltpu.SemaphoreType.DMA((7,)), pltpu.SemaphoreType.DMA],
    )(x)


def sum_parts(parts, name):
    _, R, C = parts.shape
    tr = _pick(R, (320, 336, 256, 128, 64, 32, 16, 8))

    def body(p_ref, o_ref):
        acc = p_ref[0].astype(F32)
        for k in range(1, N_DEV):
            acc += p_ref[k].astype(F32)
        o_ref[...] = acc

    return pl.pallas_call(
        body, name=name, grid=(R // tr,),
        in_specs=[pl.BlockSpec((N_DEV, tr, C), lambda i: (0, i, 0))],
        out_specs=pl.BlockSpec((tr, C), lambda i: (i, 0)),
        out_shape=jax.ShapeDtypeStruct((R, C), F32),
        compiler_params=_cp("parallel"),
    )(parts)


def adamw(g, w, m, v, name):
    R, C = g.shape
    tr = _pick(R, (512, 352, 256, 128, 64, 32, 16, 8))
    c1 = 1.0 - ADAM_B1 ** ADAM_STEP
    c2 = 1.0 - ADAM_B2 ** ADAM_STEP

    def body(g_ref, w_ref, m_ref, v_ref, d_ref, nm_ref, nv_ref):
        gg = g_ref[...]
        mm = ADAM_B1 * m_ref[...] + (1.0 - ADAM_B1) * gg
        vv = ADAM_B2 * v_ref[...] + (1.0 - ADAM_B2) * (gg * gg)
        nm_ref[...] = mm
        nv_ref[...] = vv
        d_ref[...] = -ADAM_LR * ((mm / c1) / (jnp.sqrt(vv / c2) + ADAM_EPS) + ADAM_WD * w_ref[...])

    spec = pl.BlockSpec((tr, C), lambda i: (i, 0))
    return pl.pallas_call(
        body, name=name, grid=(R // tr,),
        in_specs=[spec] * 4, out_specs=[spec] * 3,
        out_shape=[jax.ShapeDtypeStruct((R, C), F32)] * 3,
        compiler_params=_cp("parallel"),
    )(g, w, m, v)


PACK_W = 1024
BIG = (("ffn1_w_gate", 1, D, FF), ("ffn1_w_up", 1, D, FF), ("ffn1_w_down", 0, FF, D),
       ("w_in", 1, D, D_IN), ("w_out", 0, D, D),
       ("ffn2_w_gate", 1, D, FF), ("ffn2_w_up", 1, D, FF), ("ffn2_w_down", 0, FF, D),
       ("ple_w_gate", 0, D, D), ("ple_w_proj", 1, PLE, D))
BIG_ROWS = sum(r * c // N_DEV // PACK_W for _, _, r, c in BIG) * DEPTH
BIG_ROWS_PAD = 5440
SMALL = (("ffn1_norm", (D,), None), ("mix_norm", (D,), None), ("lru_conv_w", (4, LRU_W), LRU_W // N_DEV),
         ("lru_conv_b", (LRU_W,), None), ("lru_w_a", (4, HD, HD), None), ("lru_b_a", (LRU_W,), None),
         ("lru_w_x", (4, HD, HD), None), ("lru_b_x", (LRU_W,), None), ("lru_lambda", (LRU_W,), None),
         ("attn_sinks", (ATT_H,), None), ("dn_conv_w", (4, 3 * DN_W), 3 * DN_W // N_DEV),
         ("dn_a_log", (DN_H,), None), ("dn_dt_bias", (DN_H,), None), ("dn_norm", (HD,), None),
         ("ffn2_norm", (D,), None), ("ple_norm", (D,), None))
SINGLE = (("rel_bias", (REL_BUCKETS, ATT_H)), ("final_norm", (D,)))


def _shard_shape(axis, r, c):
    return (r // N_DEV, c) if axis == 0 else (r, c // N_DEV)


def _pack_rows(arrs, width, mult):
    flat = jnp.concatenate([a.reshape(-1) for a in arrs])
    rows = -(-flat.shape[0] // (width * mult)) * mult
    return jnp.pad(flat, (0, rows * width - flat.shape[0])).reshape(rows, width)


def _unpack_rows(packed, shapes):
    flat = packed.reshape(-1)
    out, off = [], 0
    for s in shapes:
        n = int(np.prod(s))
        out.append(flat[off:off + n].reshape(s))
        off += n
    return out


def _pack_big_shards(a):
    parts = [a[name][l].reshape(-1, PACK_W) for l in range(DEPTH) for name, _, _, _ in BIG]
    packed = jnp.concatenate(parts, axis=0).astype(BF16)
    return jnp.pad(packed, ((0, BIG_ROWS_PAD - BIG_ROWS), (0, 0)))


def _unpack_big_full(gathered):
    out = {name: [] for name, _, _, _ in BIG}
    off = 0
    for l in range(DEPTH):
        for name, axis, r, c in BIG:
            n = r * c // N_DEV // PACK_W
            seg = gathered[:, off:off + n, :]
            off += n
            if axis == 0:
                full = seg.reshape(r, c)
            else:
                full = seg.reshape(N_DEV, r, c // N_DEV).transpose(1, 0, 2).reshape(r, c)
            if name == "w_in":
                full = jnp.pad(full, ((0, 0), (0, D_IN_PAD - D_IN)))
            out[name].append(full)
    return {k: jnp.stack(v) for k, v in out.items()}


def _pack_big_grads(grads):
    parts = []
    for l in range(DEPTH):
        for name, axis, r, c in BIG:
            g = grads[l][name]
            if name == "w_in":
                g = g[:, :D_IN]
            n = r * c // N_DEV // PACK_W
            if axis == 0:
                parts.append(g.reshape(N_DEV, n, PACK_W))
            else:
                parts.append(g.reshape(r, N_DEV, c // N_DEV).transpose(1, 0, 2).reshape(N_DEV, n, PACK_W))
    packed = jnp.concatenate(parts, axis=1)
    return jnp.pad(packed, ((0, 0), (0, BIG_ROWS_PAD - BIG_ROWS), (0, 0)))


def _unpack_big_shards(summed):
    out = {name: [] for name, _, _, _ in BIG}
    off = 0
    for l in range(DEPTH):
        for name, axis, r, c in BIG:
            n = r * c // N_DEV // PACK_W
            out[name].append(summed[off:off + n].reshape(_shard_shape(axis, r, c)))
            off += n
    return {k: jnp.stack(v) for k, v in out.items()}


def kernel(x, p, ffn1_norm, ffn1_w_gate, ffn1_w_up, ffn1_w_down, mix_norm, w_in, lru_conv_w, lru_conv_b, lru_w_a, lru_b_a, lru_w_x, lru_b_x, lru_lambda, attn_sinks, rel_bias, dn_conv_w, dn_a_log, dn_dt_bias, dn_norm, w_out, ffn2_norm, ffn2_w_gate, ffn2_w_up, ffn2_w_down, ple_norm, ple_w_gate, ple_w_proj, final_norm, loss_target, m_ffn1_norm, m_ffn1_w_gate, m_ffn1_w_up, m_ffn1_w_down, m_mix_norm, m_w_in, m_lru_conv_w, m_lru_conv_b, m_lru_w_a, m_lru_b_a, m_lru_w_x, m_lru_b_x, m_lru_lambda, m_attn_sinks, m_rel_bias, m_dn_conv_w, m_dn_a_log, m_dn_dt_bias, m_dn_norm, m_w_out, m_ffn2_norm, m_ffn2_w_gate, m_ffn2_w_up, m_ffn2_w_down, m_ple_norm, m_ple_w_gate, m_ple_w_proj, m_final_norm, v_ffn1_norm, v_ffn1_w_gate, v_ffn1_w_up, v_ffn1_w_down, v_mix_norm, v_w_in, v_lru_conv_w, v_lru_conv_b, v_lru_w_a, v_lru_b_a, v_lru_w_x, v_lru_b_x, v_lru_lambda, v_attn_sinks, v_rel_bias, v_dn_conv_w, v_dn_a_log, v_dn_dt_bias, v_dn_norm, v_w_out, v_ffn2_norm, v_ffn2_w_gate, v_ffn2_w_up, v_ffn2_w_down, v_ple_norm, v_ple_w_gate, v_ple_w_proj, v_final_norm):
    a = dict(locals())
    nb, S, _ = x.shape
    T = nb * S
    my_slot = _slot_of(*_my_pos())

    W = _unpack_big_full(all_gather(_pack_big_shards(a), "gather_weights"))
    taps = all_gather(_pack_rows([lru_conv_w, dn_conv_w], LANE, 8), "gather_conv_taps")
    tap_shapes = [lru_conv_w.shape, dn_conv_w.shape]
    lcw, dcw = zip(*[_unpack_rows(taps[k], tap_shapes) for k in range(N_DEV)])
    W["lru_conv_w"] = jnp.concatenate(lcw, axis=-1)
    W["dn_conv_w"] = jnp.concatenate(dcw, axis=-1)
    for name, _, cols in SMALL:
        if cols is None:
            W[name] = a[name]
    W["rel_bias"], W["final_norm"] = rel_bias, final_norm

    loss_local, dx, grads, d_final = _core(x.reshape(T, D), p.reshape(DEPTH, T, PLE), W,
                                           loss_target.reshape(T, D), S)
    loss = lax.psum(loss_local, AXES)

    g_big = _unpack_big_shards(
        sum_parts(exchange(_pack_big_grads(grads), "exchange_grads"), "sum_grads"))

    small_full = [jnp.stack([grads[l][name] for l in range(DEPTH)]) for name, _, _ in SMALL]
    small_full += [grads[0]["rel_bias"] + grads[1]["rel_bias"], d_final]
    small_sum = sum_parts(all_gather(_pack_rows(small_full, LANE, 8), "gather_small_grads"), "sum_small_grads")
    g_small = dict(zip([n for n, _, _ in SMALL] + [n for n, _ in SINGLE],
                       _unpack_rows(small_sum, [s.shape for s in small_full])))
    for name, _, cols in SMALL:
        if cols is not None:
            g_small[name] = lax.dynamic_slice_in_dim(g_small[name], my_slot * cols, cols, axis=2)

    out = {}
    for name, _, _, _ in BIG:
        shape = a[name].shape
        two_d = lambda t: t.reshape(-1, shape[-1])
        res = adamw(two_d(g_big[name]), two_d(a[name]), two_d(a["m_" + name]), two_d(a["v_" + name]),
                    "adamw_" + name)
        out[name] = (g_big[name],) + tuple(r.reshape(shape) for r in res)
    small_names = [n for n, _, _ in SMALL] + [n for n, _ in SINGLE]
    shapes = [a[n].shape for n in small_names]
    packed = [_pack_rows([a[pre + n] if pre is not None else g_small[n] for n in small_names], LANE, 8)
              for pre in (None, "", "m_", "v_")]
    res = adamw(*packed, "adamw_small")
    unpacked = [_unpack_rows(r, shapes) for r in res]
    for i, n in enumerate(small_names):
        out[n] = (g_small[n].reshape(shapes[i]),) + tuple(u[i] for u in unpacked)

    order = ['ffn1_norm', 'ffn1_w_gate', 'ffn1_w_up', 'ffn1_w_down', 'mix_norm', 'w_in', 'lru_conv_w', 'lru_conv_b',
             'lru_w_a', 'lru_b_a', 'lru_w_x', 'lru_b_x', 'lru_lambda', 'attn_sinks', 'rel_bias', 'dn_conv_w',
             'dn_a_log', 'dn_dt_bias', 'dn_norm', 'w_out', 'ffn2_norm', 'ffn2_w_gate', 'ffn2_w_up', 'ffn2_w_down',
             'ple_norm', 'ple_w_gate', 'ple_w_proj', 'final_norm']
    return (loss, dx.reshape(x.shape)) + tuple(out[n][k] for k in range(4) for n in order)
```

```python
import functools
import math

import numpy as np
import jax
import jax.numpy as jnp
from jax import lax
from jax.experimental import pallas as pl
from jax.experimental.pallas import tpu as pltpu

F32 = jnp.float32
BF16 = jnp.bfloat16
HI = lax.Precision.HIGHEST

D = 1024
DEPTH = 2
EPS = 1e-6
PLE = 256
FF = 2816
HD = 64
LRU_W = 256
LRU_C = 8.0
ATT_W = 512
ATT_H = 8
ATT_KV = 2
ATT_G = 4
KV_W = 128
WINDOW = 128
BQ = 128
REL_BUCKETS = 32
REL_MAX_DIST = 128
DN_W = 256
DN_H = 4
CHUNK = 64
D_IN = 2312
D_IN_PAD = 2432
N_DEV = 8

ADAM_LR = 0.001
ADAM_B1 = 0.9
ADAM_B2 = 0.999
ADAM_EPS = 1e-08
ADAM_WD = 0.01
ADAM_STEP = 10

LANE = 128
VMEM_LIMIT = 56 * 1024 * 1024
FF_TILE = 1408
TOK_TILE = 512
NEG = -1e30


def _cp(*sem):
    return pltpu.CompilerParams(dimension_semantics=tuple(sem), vmem_limit_bytes=VMEM_LIMIT)


def _dg(a, b, ca, cb, exact):
    dims = (((ca,), (cb,)), ((), ()))
    if exact == "f32":
        return lax.dot_general(a.astype(F32), b.astype(F32), dims, precision=HI, preferred_element_type=F32)
    if exact == "split":
        a_hi, b_hi = a.astype(BF16), b.astype(BF16)
        a_lo = (a - a_hi.astype(F32)).astype(BF16)
        b_lo = (b - b_hi.astype(F32)).astype(BF16)
        dot = lambda u, v: lax.dot_general(u, v, dims, preferred_element_type=F32)
        return dot(a_hi, b_hi) + (dot(a_hi, b_lo) + dot(a_lo, b_hi))
    return lax.dot_general(a.astype(BF16), b.astype(BF16), dims, preferred_element_type=F32)


def _make_mm(exact):
    @jax.custom_vjp
    def mm(a, b):
        return _dg(a, b, 1, 0, exact)

    @jax.custom_vjp
    def mm_nt(a, b):
        return _dg(a, b, 1, 1, exact)

    @jax.custom_vjp
    def mm_tn(a, b):
        return _dg(a, b, 0, 0, exact)

    mm.defvjp(lambda a, b: (mm(a, b), (a, b)),
              lambda r, d: (mm_nt(d, r[1]), mm_tn(r[0], d)))
    mm_nt.defvjp(lambda a, b: (mm_nt(a, b), (a, b)),
                 lambda r, d: (mm(d, r[1]), mm_tn(d, r[0])))
    mm_tn.defvjp(lambda a, b: (mm_tn(a, b), (a, b)),
                 lambda r, d: (mm_nt(r[1], d), mm(r[0], d)))
    return mm, mm_nt, mm_tn


_mm, _mm_nt, _mm_tn = _make_mm("bf16")
_mmx, _mmx_nt, _mmx_tn = _make_mm("f32")
_mm3, _mm3_nt, _mm3_tn = _make_mm("split")


def _iota(shape, dim):
    return lax.broadcasted_iota(jnp.int32, shape, dim)


def _sigmoid(x):
    return 1.0 / (1.0 + jnp.exp(-x))


def _rms(h, g):
    rstd = lax.rsqrt(jnp.mean(h * h, axis=-1, keepdims=True) + EPS)
    xhat = h * rstd
    return xhat * g, xhat, rstd


def _rms_bwd(dxn, xhat, rstd, g):
    dxhat = dxn * g
    dh = rstd * (dxhat - xhat * jnp.mean(dxhat * xhat, axis=-1, keepdims=True))
    dg = jnp.sum(dxn * xhat, axis=0, keepdims=True)
    return dh, dg


def _row_spec(tm, n):
    return pl.BlockSpec((tm, n), lambda i, *_: (i, 0))


def _full_spec(shape):
    nd = len(shape)
    return pl.BlockSpec(shape, lambda *_: (0,) * nd)


def ffn_fwd(h, g, wg, wu, wd, name):
    T = h.shape[0]
    tm = min(TOK_TILE, T)
    nj = FF // FF_TILE

    def body(h_ref, g_ref, wg_ref, wu_ref, wd_ref, o_ref, xn_s):
        j = pl.program_id(1)

        @pl.when(j == 0)
        def _():
            hh = h_ref[...]
            xn_s[...] = _rms(hh, g_ref[...])[0].astype(BF16)
            o_ref[...] = hh

        xn = xn_s[...]
        gt = _mm(xn, wg_ref[...])
        up = _mm(xn, wu_ref[...])
        act = gt * _sigmoid(gt) * up
        o_ref[...] += 0.5 * _mm(act, wd_ref[...])

    return pl.pallas_call(
        body, name=name, grid=(T // tm, nj),
        in_specs=[pl.BlockSpec((tm, D), lambda i, j: (i, 0)),
                  pl.BlockSpec((1, D), lambda i, j: (0, 0)),
                  pl.BlockSpec((D, FF_TILE), lambda i, j: (0, j)),
                  pl.BlockSpec((D, FF_TILE), lambda i, j: (0, j)),
                  pl.BlockSpec((FF_TILE, D), lambda i, j: (j, 0))],
        out_specs=pl.BlockSpec((tm, D), lambda i, j: (i, 0)),
        out_shape=jax.ShapeDtypeStruct((T, D), F32),
        scratch_shapes=[pltpu.VMEM((tm, D), BF16)],
        compiler_params=_cp("parallel", "arbitrary"),
    )(h, g, wg, wu, wd)


def ffn_bwd(h, dy, g, wg, wu, wd, name):
    T = h.shape[0]
    tm = min(TOK_TILE // 2, T)
    nj = FF // FF_TILE

    def body(h_ref, dy_ref, g_ref, wg_ref, wu_ref, wd_ref,
             dh_ref, dg_ref, du_ref, a_ref, xn_ref, dn_ref, xn_s, dxn_s):
        i = pl.program_id(0)
        j = pl.program_id(1)

        @pl.when(j == 0)
        def _():
            xn = _rms(h_ref[...], g_ref[...])[0].astype(BF16)
            xn_s[...] = xn
            xn_ref[...] = xn
            dxn_s[...] = jnp.zeros_like(dxn_s)

        @pl.when((i == 0) & (j == 0))
        def _():
            dn_ref[...] = jnp.zeros_like(dn_ref)

        xn = xn_s[...]
        gt = _mm(xn, wg_ref[...])
        up = _mm(xn, wu_ref[...])
        sg = _sigmoid(gt)
        si = gt * sg
        da = _mm_nt(0.5 * dy_ref[...], wd_ref[...])
        dup = da * si
        dgt = da * up * (sg * (1.0 + gt * (1.0 - sg)))
        dg_ref[...] = dgt.astype(BF16)
        du_ref[...] = dup.astype(BF16)
        a_ref[...] = (si * up).astype(BF16)
        dxn_s[...] += _mm_nt(dgt, wg_ref[...]) + _mm_nt(dup, wu_ref[...])

        @pl.when(j == nj - 1)
        def _():
            gg = g_ref[...]
            _, xhat, rstd = _rms(h_ref[...], gg)
            dh, dn = _rms_bwd(dxn_s[...], xhat, rstd, gg)
            dh_ref[...] = dy_ref[...] + dh
            dn_ref[...] += dn

    return pl.pallas_call(
        body, name=name, grid=(T // tm, nj),
        in_specs=[pl.BlockSpec((tm, D), lambda i, j: (i, 0)),
                  pl.BlockSpec((tm, D), lambda i, j: (i, 0)),
                  pl.BlockSpec((1, D), lambda i, j: (0, 0)),
                  pl.BlockSpec((D, FF_TILE), lambda i, j: (0, j)),
                  pl.BlockSpec((D, FF_TILE), lambda i, j: (0, j)),
                  pl.BlockSpec((FF_TILE, D), lambda i, j: (j, 0))],
        out_specs=[pl.BlockSpec((tm, D), lambda i, j: (i, 0)),
                   pl.BlockSpec((tm, FF_TILE), lambda i, j: (i, j)),
                   pl.BlockSpec((tm, FF_TILE), lambda i, j: (i, j)),
                   pl.BlockSpec((tm, FF_TILE), lambda i, j: (i, j)),
                   pl.BlockSpec((tm, D), lambda i, j: (i, 0)),
                   pl.BlockSpec((1, D), lambda i, j: (0, 0))],
        out_shape=[jax.ShapeDtypeStruct((T, D), F32),
                   jax.ShapeDtypeStruct((T, FF), BF16),
                   jax.ShapeDtypeStruct((T, FF), BF16),
                   jax.ShapeDtypeStruct((T, FF), BF16),
                   jax.ShapeDtypeStruct((T, D), BF16),
                   jax.ShapeDtypeStruct((1, D), F32)],
        scratch_shapes=[pltpu.VMEM((tm, D), BF16), pltpu.VMEM((tm, D), F32)],
        compiler_params=_cp("arbitrary", "arbitrary"),
    )(h, dy, g, wg, wu, wd)


def _pick(n, prefs):
    for t in prefs:
        if n % t == 0:
            return t
    return n


def matmul_tn(a, b, name, scale=1.0, out_dtype=BF16):
    T, M = a.shape
    N = b.shape[1]
    tmm = _pick(M, (512, 1408, 256))
    tnn = _pick(N, (1408, 1024, 2432))
    tk = min(TOK_TILE, T)
    nk = T // tk

    def body(a_ref, b_ref, o_ref, acc):
        k = pl.program_id(2)

        @pl.when(k == 0)
        def _():
            acc[...] = jnp.zeros_like(acc)

        acc[...] += _mm_tn(a_ref[...], b_ref[...])

        @pl.when(k == nk - 1)
        def _():
            o_ref[...] = (scale * acc[...]).astype(out_dtype)

    return pl.pallas_call(
        body, name=name, grid=(M // tmm, N // tnn, nk),
        in_specs=[pl.BlockSpec((tk, tmm), lambda i, j, k: (k, i)),
                  pl.BlockSpec((tk, tnn), lambda i, j, k: (k, j))],
        out_specs=pl.BlockSpec((tmm, tnn), lambda i, j, k: (i, j)),
        out_shape=jax.ShapeDtypeStruct((M, N), out_dtype),
        scratch_shapes=[pltpu.VMEM((tmm, tnn), F32)],
        compiler_params=_cp("parallel", "parallel", "arbitrary"),
    )(a, b)


U_SPLITS = (512, 768, 1024, 128)
U_OFFS = (0, 512, 1280, 2304)


def mixin_fwd(h, g, w_in, name):
    T = h.shape[0]
    tm = min(TOK_TILE, T)

    def body(h_ref, g_ref, w_ref, u0, u1, u2, u3, xn_ref):
        xn = _rms(h_ref[...], g_ref[...])[0].astype(BF16)
        xn_ref[...] = xn
        u = _mm(xn, w_ref[...])
        for ref, off, n in zip((u0, u1, u2, u3), U_OFFS, U_SPLITS):
            ref[...] = u[:, off:off + n]

    return pl.pallas_call(
        body, name=name, grid=(T // tm,),
        in_specs=[_row_spec(tm, D), _full_spec((1, D)), _full_spec((D, D_IN_PAD))],
        out_specs=[_row_spec(tm, n) for n in U_SPLITS] + [_row_spec(tm, D)],
        out_shape=[jax.ShapeDtypeStruct((T, n), F32) for n in U_SPLITS]
        + [jax.ShapeDtypeStruct((T, D), BF16)],
        compiler_params=_cp("parallel"),
    )(h, g, w_in)


def mixin_bwd(h, dh_in, g, w_in, dus, name):
    T = h.shape[0]
    tm = min(TOK_TILE, T)

    def body(h_ref, dhi_ref, g_ref, w_ref, d0, d1, d2, d3, dh_ref, du_ref, dn_ref):
        @pl.when(pl.program_id(0) == 0)
        def _():
            dn_ref[...] = jnp.zeros_like(dn_ref)

        dxn = jnp.zeros((tm, D), F32)
        for ref, off, n in zip((d0, d1, d2, d3), U_OFFS, U_SPLITS):
            du = ref[...]
            du_ref[:, off:off + n] = du.astype(BF16)
            dxn += _mm_nt(du, w_ref[:, off:off + n])
        gg = g_ref[...]
        _, xhat, rstd = _rms(h_ref[...], gg)
        dh, dn = _rms_bwd(dxn, xhat, rstd, gg)
        dh_ref[...] = dhi_ref[...] + dh
        dn_ref[...] += dn

    return pl.pallas_call(
        body, name=name, grid=(T // tm,),
        in_specs=[_row_spec(tm, D), _row_spec(tm, D), _full_spec((1, D)), _full_spec((D, D_IN_PAD))]
        + [_row_spec(tm, n) for n in U_SPLITS],
        out_specs=[_row_spec(tm, D), _row_spec(tm, D_IN_PAD), _full_spec((1, D))],
        out_shape=[jax.ShapeDtypeStruct((T, D), F32), jax.ShapeDtypeStruct((T, D_IN_PAD), BF16),
                   jax.ShapeDtypeStruct((1, D), F32)],
        compiler_params=_cp("arbitrary"),
    )(h, dh_in, g, w_in, *dus)


def _shift_down(x, s, row):
    if s == 0:
        return x
    return jnp.where(row >= s, pltpu.roll(x, s, 0), 0.0)


def _shift_up(x, s, row):
    if s == 0:
        return x
    n = x.shape[0]
    return jnp.where(row < n - s, pltpu.roll(x, n - s, 0), 0.0)


def conv_fwd(x, w, b, S, col0, C, name):
    T = x.shape[0]
    cb0 = col0 // LANE

    def body(x_ref, w_ref, b_ref, y_ref):
        xx = x_ref[...]
        row = _iota(xx.shape, 0)
        y = xx * w_ref[3:4, :] + b_ref[...]
        for k in range(3):
            y += _shift_down(xx, 3 - k, row) * w_ref[k:k + 1, :]
        y_ref[...] = y

    return pl.pallas_call(
        body, name=name, grid=(T // S, C // LANE),
        in_specs=[pl.BlockSpec((S, LANE), lambda s, c: (s, cb0 + c)),
                  pl.BlockSpec((4, LANE), lambda s, c: (0, c)),
                  pl.BlockSpec((1, LANE), lambda s, c: (0, c))],
        out_specs=pl.BlockSpec((S, LANE), lambda s, c: (s, c)),
        out_shape=jax.ShapeDtypeStruct((T, C), F32),
        compiler_params=_cp("parallel", "parallel"),
    )(x, w, b)


def conv_bwd(x, dy, w, S, col0, C, name):
    T = x.shape[0]
    cb0 = col0 // LANE

    def body(x_ref, dy_ref, w_ref, dx_ref, dwb_ref):
        @pl.when(pl.program_id(1) == 0)
        def _():
            dwb_ref[...] = jnp.zeros_like(dwb_ref)

        xx = x_ref[...]
        dd = dy_ref[...]
        row = _iota(xx.shape, 0)
        dx = dd * w_ref[3:4, :]
        for k in range(3):
            dx += _shift_up(dd, 3 - k, row) * w_ref[k:k + 1, :]
        dx_ref[...] = dx
        for k in range(4):
            dwb_ref[k:k + 1, :] += jnp.sum(dd * _shift_down(xx, 3 - k, row), axis=0, keepdims=True)
        dwb_ref[4:5, :] += jnp.sum(dd, axis=0, keepdims=True)

    return pl.pallas_call(
        body, name=name, grid=(C // LANE, T // S),
        in_specs=[pl.BlockSpec((S, LANE), lambda c, s: (s, cb0 + c)),
                  pl.BlockSpec((S, LANE), lambda c, s: (s, c)),
                  pl.BlockSpec((4, LANE), lambda c, s: (0, c))],
        out_specs=[pl.BlockSpec((S, LANE), lambda c, s: (s, c)),
                   pl.BlockSpec((8, LANE), lambda c, s: (0, c))],
        out_shape=[jax.ShapeDtypeStruct((T, C), F32), jax.ShapeDtypeStruct((8, C), F32)],
        compiler_params=_cp("parallel", "arbitrary"),
    )(x, dy, w)


def _scan(a, b, row):
    n = a.shape[0]
    d = 1
    while d < n:
        keep = row >= d
        b = a * jnp.where(keep, pltpu.roll(b, d, 0), 0.0) + b
        a = a * jnp.where(keep, pltpu.roll(a, d, 0), 1.0)
        d *= 2
    return b


def _rscan(a, b, row):
    n = a.shape[0]
    d = 1
    while d < n:
        keep = row < n - d
        b = a * jnp.where(keep, pltpu.roll(b, n - d, 0), 0.0) + b
        a = a * jnp.where(keep, pltpu.roll(a, n - d, 0), 1.0)
        d *= 2
    return b


GELU_C = math.sqrt(2.0 / math.pi)


def _gelu(x):
    t = jnp.tanh(GELU_C * (x + 0.044715 * (x * x * x)))
    return 0.5 * x * (1.0 + t), t


def _lru_gates(xr, wa, ba, wx, bx, lam):
    r = _sigmoid(_mm(xr, wa) + ba)
    i = _sigmoid(_mm(xr, wx) + bx)
    sp = jnp.maximum(-lam, 0.0) + jnp.log(1.0 + jnp.exp(-jnp.abs(lam)))
    la = -LRU_C * r * sp
    a = jnp.exp(la)
    e2 = a * a
    m = jnp.sqrt(-jnp.tanh(la) * (e2 + 1.0))
    return r, i, sp, a, e2, m


def lru_fwd(xr, u_lru, wa, wx, vec, S, name):
    T = xr.shape[0]

    def body(xr_ref, gt_ref, wa_ref, wx_ref, vec_ref, y_ref):
        x = xr_ref[...]
        row = _iota(x.shape, 0)
        r, i, sp, a, e2, m = _lru_gates(x, wa_ref[...], vec_ref[0:1, :], wx_ref[...], vec_ref[1:2, :],
                                        vec_ref[2:3, :])
        hh = _scan(a, m * (i * x), row)
        y_ref[...] = _gelu(gt_ref[...])[0] * hh

    return pl.pallas_call(
        body, name=name, grid=(T // S, LRU_W // LANE),
        in_specs=[pl.BlockSpec((S, LANE), lambda s, c: (s, c)),
                  pl.BlockSpec((S, LANE), lambda s, c: (s, 2 + c)),
                  pl.BlockSpec((LANE, LANE), lambda s, c: (c, c)),
                  pl.BlockSpec((LANE, LANE), lambda s, c: (c, c)),
                  pl.BlockSpec((8, LANE), lambda s, c: (0, c))],
        out_specs=pl.BlockSpec((S, LANE), lambda s, c: (s, c)),
        out_shape=jax.ShapeDtypeStruct((T, LRU_W), F32),
        compiler_params=_cp("parallel", "parallel"),
    )(xr, u_lru, wa, wx, vec)


def lru_bwd(xr, u_lru, dy, wa, wx, vec, S, name):
    T = xr.shape[0]

    def body(xr_ref, gt_ref, dy_ref, wa_ref, wx_ref, vec_ref,
             dxr_ref, dgt_ref, dwa_ref, dwx_ref, dvec_ref):
        @pl.when(pl.program_id(1) == 0)
        def _():
            dwa_ref[...] = jnp.zeros_like(dwa_ref)
            dwx_ref[...] = jnp.zeros_like(dwx_ref)
            dvec_ref[...] = jnp.zeros_like(dvec_ref)

        x = xr_ref[...]
        n = x.shape[0]
        row = _iota(x.shape, 0)
        lam = vec_ref[2:3, :]
        r, i, sp, a, e2, m = _lru_gates(x, wa_ref[...], vec_ref[0:1, :], wx_ref[...], vec_ref[1:2, :], lam)
        v = i * x
        hh = _scan(a, m * v, row)
        gt = gt_ref[...]
        dy = dy_ref[...]
        ge, t = _gelu(gt)
        dgt_ref[...] = dy * hh * (0.5 * (1.0 + t) + 0.5 * gt * (1.0 - t * t) * GELU_C
                                  * (1.0 + 3.0 * 0.044715 * gt * gt))
        a_next = jnp.where(row < n - 1, pltpu.roll(a, n - 1, 0), 0.0)
        G = _rscan(a_next, dy * ge, row)
        da = G * _shift_down(hh, 1, row)
        dv = G * m
        dla = da * a - (G * v) * e2 / m
        dr = dla * (-LRU_C * sp)
        dsp = jnp.sum(dla * (-LRU_C * r), axis=0, keepdims=True)
        dra = dr * r * (1.0 - r)
        dia = (dv * x) * i * (1.0 - i)
        dxr_ref[...] = dv * i + _mm_nt(dra, wa_ref[...]) + _mm_nt(dia, wx_ref[...])
        dwa_ref[0] += _mm_tn(x, dra)
        dwx_ref[0] += _mm_tn(x, dia)
        dvec_ref[0:1, :] += jnp.sum(dra, axis=0, keepdims=True)
        dvec_ref[1:2, :] += jnp.sum(dia, axis=0, keepdims=True)
        dvec_ref[2:3, :] += dsp * (-_sigmoid(-lam))

    return pl.pallas_call(
        body, name=name, grid=(LRU_W // LANE, T // S),
        in_specs=[pl.BlockSpec((S, LANE), lambda c, s: (s, c)),
                  pl.BlockSpec((S, LANE), lambda c, s: (s, 2 + c)),
                  pl.BlockSpec((S, LANE), lambda c, s: (s, c)),
                  pl.BlockSpec((LANE, LANE), lambda c, s: (c, c)),
                  pl.BlockSpec((LANE, LANE), lambda c, s: (c, c)),
                  pl.BlockSpec((8, LANE), lambda c, s: (0, c))],
        out_specs=[pl.BlockSpec((S, LANE), lambda c, s: (s, c)),
                   pl.BlockSpec((S, LANE), lambda c, s: (s, c)),
                   pl.BlockSpec((1, LANE, LANE), lambda c, s: (c, 0, 0)),
                   pl.BlockSpec((1, LANE, LANE), lambda c, s: (c, 0, 0)),
                   pl.BlockSpec((8, LANE), lambda c, s: (0, c))],
        out_shape=[jax.ShapeDtypeStruct((T, LRU_W), F32), jax.ShapeDtypeStruct((T, LRU_W), F32),
                   jax.ShapeDtypeStruct((2, LANE, LANE), F32), jax.ShapeDtypeStruct((2, LANE, LANE), F32),
                   jax.ShapeDtypeStruct((8, LRU_W), F32)],
        compiler_params=_cp("parallel", "arbitrary"),
    )(xr, u_lru, dy, wa, wx, vec)


def _bucket_table():
    qi = np.arange(BQ)[:, None]
    kj = np.arange(2 * BQ)[None, :]
    dist = BQ + qi - kj
    band = (dist >= 0) & (dist < WINDOW)
    dd = np.maximum(dist, 0)
    max_exact = REL_BUCKETS // 2
    large = max_exact + (np.log(np.maximum(dd, 1).astype(np.float32) / np.float32(max_exact))
                         / np.float32(math.log(REL_MAX_DIST / max_exact))
                         * np.float32(REL_BUCKETS - max_exact)).astype(np.int32)
    large = np.minimum(large, REL_BUCKETS - 1)
    bucket = np.where(dd < max_exact, dd, large)
    return np.where(band, bucket, -1).astype(np.int32)


def _att_specs(S):
    nb = S // BQ
    qc = ATT_W // LANE
    return [pl.BlockSpec((BQ, ATT_W), lambda b, n: (b * nb + n, 0)),
            pl.BlockSpec((BQ, KV_W), lambda b, n: (b * nb + jnp.maximum(n - 1, 0), qc)),
            pl.BlockSpec((BQ, KV_W), lambda b, n: (b * nb + n, qc)),
            pl.BlockSpec((BQ, KV_W), lambda b, n: (b * nb + jnp.maximum(n - 1, 0), qc + 1)),
            pl.BlockSpec((BQ, KV_W), lambda b, n: (b * nb + n, qc + 1))]


def _att_bias(bk, rb_ref, bias_s):
    for h in range(ATT_H):
        acc = jnp.zeros(bk.shape, F32)
        for bb in range(REL_BUCKETS):
            acc = jnp.where(bk == bb, rb_ref[bb * ATT_H + h], acc)
        bias_s[h] = acc


def _att_probs(qh, kg, bias, valid, sink):
    s = _mm_nt(qh, kg) * (HD ** -0.5) + bias
    s = jnp.where(valid, s, NEG)
    m = jnp.maximum(jnp.max(s, axis=-1, keepdims=True), sink)
    e = jnp.exp(s - m)
    es = jnp.exp(sink - m)
    den = jnp.sum(e, axis=-1, keepdims=True) + es
    return e / den, es / den


def attn_fwd(u_att, sinks, rel_bias, S, name):
    T = u_att.shape[0]
    nb = S // BQ
    table = jnp.asarray(_bucket_table())

    def body(sk_ref, rb_ref, bk_ref, q_ref, kp_ref, kc_ref, vp_ref, vc_ref, o_ref, bias_s):
        b = pl.program_id(0)
        n = pl.program_id(1)
        bk = bk_ref[...]

        @pl.when((b == 0) & (n == 0))
        def _():
            _att_bias(bk, rb_ref, bias_s)

        valid = (bk >= 0) & ((n > 0) | (_iota(bk.shape, 1) >= BQ))
        for h in range(ATT_H):
            gs = slice(HD * (h // ATT_G), HD * (h // ATT_G + 1))
            kg = jnp.concatenate([kp_ref[:, gs], kc_ref[:, gs]], axis=0)
            vg = jnp.concatenate([vp_ref[:, gs], vc_ref[:, gs]], axis=0)
            p, _ = _att_probs(q_ref[:, HD * h:HD * (h + 1)], kg, bias_s[h], valid, sk_ref[h])
            o_ref[:, HD * h:HD * (h + 1)] = _mm(p, vg)

    smem = pl.BlockSpec(memory_space=pltpu.SMEM)
    return pl.pallas_call(
        body, name=name, grid=(T // S, nb),
        in_specs=[smem, smem, _full_spec((BQ, 2 * BQ))] + _att_specs(S),
        out_specs=pl.BlockSpec((BQ, ATT_W), lambda b, n: (b * nb + n, 0)),
        out_shape=jax.ShapeDtypeStruct((T, ATT_W), F32),
        scratch_shapes=[pltpu.VMEM((ATT_H, BQ, 2 * BQ), F32)],
        compiler_params=_cp("arbitrary", "arbitrary"),
    )(sinks, rel_bias, table, u_att, u_att, u_att, u_att, u_att)


def attn_bwd(u_att, dy, sinks, rel_bias, S, name):
    T = u_att.shape[0]
    nb = S // BQ
    nB = T // S
    table = jnp.asarray(_bucket_table())
    scale = HD ** -0.5

    def body(sk_ref, rb_ref, bk_ref, q_ref, kp_ref, kc_ref, vp_ref, vc_ref, dy_ref,
             du_ref, drel_ref, dsk_ref, bias_s, dbias_s):
        b = pl.program_id(0)
        n = pl.program_id(1)
        bk = bk_ref[...]

        @pl.when((b == 0) & (n == 0))
        def _():
            _att_bias(bk, rb_ref, bias_s)
            dbias_s[...] = jnp.zeros_like(dbias_s)
            dsk_ref[...] = jnp.zeros_like(dsk_ref)
            drel_ref[...] = jnp.zeros_like(drel_ref)

        @pl.when(n == 0)
        def _():
            du_ref[...] = jnp.zeros_like(du_ref)

        valid = (bk >= 0) & ((n > 0) | (_iota(bk.shape, 1) >= BQ))
        r_cur = pl.multiple_of(n * BQ, BQ)
        r_prev = pl.multiple_of(jnp.maximum(n - 1, 0) * BQ, BQ)
        for g in range(ATT_KV):
            gs = slice(HD * g, HD * (g + 1))
            kg = jnp.concatenate([kp_ref[:, gs], kc_ref[:, gs]], axis=0)
            vg = jnp.concatenate([vp_ref[:, gs], vc_ref[:, gs]], axis=0)
            dk = jnp.zeros((2 * BQ, HD), F32)
            dv = jnp.zeros((2 * BQ, HD), F32)
            for e in range(ATT_G):
                h = g * ATT_G + e
                qh = q_ref[:, HD * h:HD * (h + 1)]
                do = dy_ref[:, HD * h:HD * (h + 1)]
                p, ps = _att_probs(qh, kg, bias_s[h], valid, sk_ref[h])
                dp = _mm_nt(do, vg)
                delta = jnp.sum(p * dp, axis=-1, keepdims=True)
                ds = p * (dp - delta)
                dbias_s[h] += ds
                dsk_ref[h:h + 1, :] += jnp.broadcast_to(
                    jnp.sum(-ps * delta, axis=0, keepdims=True), (1, LANE))
                dss = ds * scale
                du_ref[pl.ds(r_cur, BQ), HD * h:HD * (h + 1)] = _mm(dss, kg)
                dk += _mm_tn(dss, qh)
                dv += _mm_tn(p, do)
            ck = ATT_W + HD * g
            cv = ATT_W + KV_W + HD * g
            du_ref[pl.ds(r_prev, BQ), ck:ck + HD] += dk[0:BQ]
            du_ref[pl.ds(r_cur, BQ), ck:ck + HD] += dk[BQ:]
            du_ref[pl.ds(r_prev, BQ), cv:cv + HD] += dv[0:BQ]
            du_ref[pl.ds(r_cur, BQ), cv:cv + HD] += dv[BQ:]

        @pl.when((b == nB - 1) & (n == nb - 1))
        def _():
            lane = _iota((1, LANE), 1)
            for h in range(ATT_H):
                db = dbias_s[h]
                acc = jnp.zeros((1, LANE), F32)
                for bb in range(REL_BUCKETS):
                    val = jnp.sum(jnp.sum(jnp.where(bk == bb, db, 0.0), axis=1, keepdims=True),
                                  axis=0, keepdims=True)
                    acc = jnp.where(lane == bb, val, acc)
                drel_ref[h:h + 1, :] = acc

    smem = pl.BlockSpec(memory_space=pltpu.SMEM)
    return pl.pallas_call(
        body, name=name, grid=(nB, nb),
        in_specs=[smem, smem, _full_spec((BQ, 2 * BQ))] + _att_specs(S)
        + [pl.BlockSpec((BQ, ATT_W), lambda b, n: (b * nb + n, 0))],
        out_specs=[pl.BlockSpec((S, ATT_W + 2 * KV_W), lambda b, n: (b, 0)),
                   _full_spec((8, LANE)), _full_spec((8, LANE))],
        out_shape=[jax.ShapeDtypeStruct((T, ATT_W + 2 * KV_W), F32),
                   jax.ShapeDtypeStruct((8, LANE), F32), jax.ShapeDtypeStruct((8, LANE), F32)],
        scratch_shapes=[pltpu.VMEM((ATT_H, BQ, 2 * BQ), F32), pltpu.VMEM((ATT_H, BQ, 2 * BQ), F32)],
        compiler_params=_cp("arbitrary", "arbitrary"),
    )(sinks, rel_bias, table, u_att, u_att, u_att, u_att, u_att, dy)


def _head_of(i):
    return lax.shift_right_logical(i, 6)


def _head_mask(shape):
    return (_head_of(_iota(shape, 0)) == _head_of(_iota(shape, 1))).astype(F32)


def _dn_point(c, uba, alog, dtb):
    s = c * _sigmoid(c)
    qt, kt, vt = s[:, 0:256], s[:, 256:512], s[:, 512:768]
    ones_bd = _head_mask((DN_W, DN_W))
    q = qt * lax.rsqrt(_mmx(qt * qt, ones_bd) + EPS) * (HD ** -0.5)
    k = kt * lax.rsqrt(_mmx(kt * kt, ones_bd) + EPS)
    sel = _head_of(_iota((LANE, DN_W), 1))
    row = _iota((LANE, DN_W), 0)
    braw = _mmx(uba, (row == sel).astype(F32))
    araw = _mmx(uba, (row == sel + DN_H).astype(F32)) + dtb
    beta = _sigmoid(braw)
    g = -jnp.exp(alog) * (jnp.maximum(araw, 0.0) + jnp.log(1.0 + jnp.exp(-jnp.abs(araw))))
    return q, k, vt, g, beta


def dn_point_fwd(c, uba, alog, dtb, name):
    T = c.shape[0]
    tm = min(TOK_TILE, T)

    def body(c_ref, u_ref, al_ref, dt_ref, *outs):
        for ref, val in zip(outs, _dn_point(c_ref[...], u_ref[...], al_ref[...], dt_ref[...])):
            ref[...] = val

    return pl.pallas_call(
        body, name=name, grid=(T // tm,),
        in_specs=[_row_spec(tm, 768), _row_spec(tm, LANE), _full_spec((1, DN_W)), _full_spec((1, DN_W))],
        out_specs=[_row_spec(tm, DN_W)] * 5,
        out_shape=[jax.ShapeDtypeStruct((T, DN_W), F32)] * 5,
        compiler_params=_cp("parallel"),
    )(c, uba, alog, dtb)


def dn_point_bwd(c, uba, alog, dtb, douts, name):
    T = c.shape[0]
    tm = min(TOK_TILE, T)

    def body(c_ref, u_ref, al_ref, dt_ref, dq, dk, dv, dg, db, dc_ref, du_ref, dvec_ref):
        @pl.when(pl.program_id(0) == 0)
        def _():
            dvec_ref[...] = jnp.zeros_like(dvec_ref)

        _, vjp = jax.vjp(_dn_point, c_ref[...], u_ref[...], al_ref[...], dt_ref[...])
        dc, du, dal, ddt = vjp((dq[...], dk[...], dv[...], dg[...], db[...]))
        dc_ref[...] = dc
        du_ref[...] = du
        fold = (_iota((LANE, DN_W), 0) == _head_of(_iota((LANE, DN_W), 1))).astype(F32)
        both = jnp.concatenate([dal, ddt, jnp.zeros((6, DN_W), F32)], axis=0)
        dvec_ref[...] += _mmx_nt(both, fold)

    return pl.pallas_call(
        body, name=name, grid=(T // tm,),
        in_specs=[_row_spec(tm, 768), _row_spec(tm, LANE), _full_spec((1, DN_W)), _full_spec((1, DN_W))]
        + [_row_spec(tm, DN_W)] * 5,
        out_specs=[_row_spec(tm, 768), _row_spec(tm, LANE), _full_spec((8, LANE))],
        out_shape=[jax.ShapeDtypeStruct((T, 768), F32), jax.ShapeDtypeStruct((T, LANE), F32),
                   jax.ShapeDtypeStruct((8, LANE), F32)],
        compiler_params=_cp("arbitrary"),
    )(c, uba, alog, dtb, *douts)


def _unit_lower_inverse(lmat):
    eye = (_iota(lmat.shape, 0) == _iota(lmat.shape, 1)).astype(F32)
    tinv = eye - lmat
    pw = lmat
    for _ in range(5):
        pw = _mm3(pw, pw)
        tinv = tinv + _mm3(tinv, pw)
    return tinv


def _inverse_bwd(tinv, d):
    return -_mm3_nt(_mm3_tn(tinv, d), tinv)


@jax.custom_vjp
def _tri_inv(lmat):
    return _unit_lower_inverse(lmat)


def _tri_inv_fwd(lmat):
    tinv = _unit_lower_inverse(lmat)
    return tinv, tinv


_tri_inv.defvjp(_tri_inv_fwd, lambda tinv, d: (_inverse_bwd(tinv, d),))


@jax.custom_vjp
def _tri_inv_known(lmat, tinv):
    return tinv


_tri_inv_known.defvjp(lambda lmat, tinv: (tinv, tinv),
                      lambda tinv, d: (_inverse_bwd(tinv, d), jnp.zeros_like(tinv)))


DN_SUB = 2


def _dn_prep(q, k, v, g, beta, known=None):
    hm = _head_mask((DN_W, DN_W))
    ri = _iota((DN_W, DN_W), 0) & (CHUNK - 1)
    ci = _iota((DN_W, DN_W), 1) & (CHUNK - 1)
    tril = hm * (ri >= ci).astype(F32)
    strict = hm * (ri > ci).astype(F32)
    tri64 = (_iota((CHUNK, CHUNK), 0) >= _iota((CHUNK, CHUNK), 1)).astype(F32)

    def stack(x):
        return jnp.concatenate([x, x, x, x], axis=0) * hm

    gc = _mm3(tri64, g)
    glast = jnp.sum(g, axis=0, keepdims=True)
    eg = jnp.exp(gc)
    kb = k * beta
    qs, ks = stack(q), stack(k)
    gcol = jnp.sum(stack(gc), axis=1, keepdims=True) * (1.0 / HD)
    gmat = jnp.broadcast_to(gcol, (DN_W, DN_W))
    decay = jnp.exp(jnp.minimum(gmat - gmat.T, 0.0))
    lmat = _mm_nt(stack(kb), ks) * decay * strict
    tinv = _tri_inv(lmat) if known is None else _tri_inv_known(lmat, known)
    u = _mm(tinv, stack(v * beta))
    w = _mm(tinv, stack(kb * eg))
    att = _mm_nt(qs, ks) * decay * tril
    return u, w, att, stack(q * eg), stack(k * jnp.exp(glast - gc)), jnp.exp(glast), tinv


def _dn_apply(state, prep):
    u, w, att, qe, kd, eglast, _ = prep
    vn = u - _mm(w, state)
    o4 = _mm(qe, state) + _mm(att, vn)
    o = o4[0:64] + o4[64:128] + o4[128:192] + o4[192:256]
    return o, state * eglast + _mm_tn(kd, vn)


def _dn_chunks(state, q, k, v, g, beta, knowns=None):
    n = q.shape[0] // CHUNK
    rows = lambda x, c: x[c * CHUNK:(c + 1) * CHUNK]
    preps = [_dn_prep(*(rows(x, c) for x in (q, k, v, g, beta)),
                      known=None if knowns is None else knowns[c]) for c in range(n)]
    outs = []
    for prep in preps:
        o, state = _dn_apply(state, prep)
        outs.append(o)
    return jnp.concatenate(outs, axis=0), state, [prep[-1] for prep in preps]


def dn_scan_fwd(q, k, v, g, beta, S, name):
    T = q.shape[0]
    rows = DN_SUB * CHUNK
    ns = S // rows

    def body(q_ref, k_ref, v_ref, g_ref, b_ref, o_ref, st_ref, ti_ref, s_s):
        @pl.when(pl.program_id(1) == 0)
        def _():
            s_s[...] = jnp.zeros_like(s_s)

        st = s_s[...]
        st_ref[0] = st
        o, new, tinvs = _dn_chunks(st, q_ref[...], k_ref[...], v_ref[...], g_ref[...], b_ref[...])
        o_ref[...] = o
        for c, tinv in enumerate(tinvs):
            ti_ref[c] = tinv
        s_s[...] = new

    spec = pl.BlockSpec((rows, DN_W), lambda b, t: (b * ns + t, 0))
    return pl.pallas_call(
        body, name=name, grid=(T // S, ns),
        in_specs=[spec] * 5,
        out_specs=[spec, pl.BlockSpec((1, DN_W, DN_W), lambda b, t: (b * ns + t, 0, 0)),
                   pl.BlockSpec((DN_SUB, DN_W, DN_W), lambda b, t: (b * ns + t, 0, 0))],
        out_shape=[jax.ShapeDtypeStruct((T, DN_W), F32),
                   jax.ShapeDtypeStruct((T // rows, DN_W, DN_W), F32),
                   jax.ShapeDtypeStruct((T // CHUNK, DN_W, DN_W), F32)],
        scratch_shapes=[pltpu.VMEM((DN_W, DN_W), F32)],
        compiler_params=_cp("parallel", "arbitrary"),
    )(q, k, v, g, beta)


def dn_scan_bwd(q, k, v, g, beta, states, tinvs, do, S, name):
    T = q.shape[0]
    rows = DN_SUB * CHUNK
    ns = S // rows

    def body(q_ref, k_ref, v_ref, g_ref, b_ref, st_ref, ti_ref, do_ref, dq, dk, dv, dg, db, ds_s):
        @pl.when(pl.program_id(1) == 0)
        def _():
            ds_s[...] = jnp.zeros_like(ds_s)

        knowns = [ti_ref[c] for c in range(DN_SUB)]
        _, vjp = jax.vjp(lambda *args: _dn_chunks(*args, knowns=knowns)[:2],
                         st_ref[0], q_ref[...], k_ref[...], v_ref[...], g_ref[...], b_ref[...])
        grads = vjp((do_ref[...], ds_s[...]))
        ds_s[...] = grads[0]
        for ref, val in zip((dq, dk, dv, dg, db), grads[1:]):
            ref[...] = val

    spec = pl.BlockSpec((rows, DN_W), lambda b, t: (b * ns + ns - 1 - t, 0))
    return pl.pallas_call(
        body, name=name, grid=(T // S, ns),
        in_specs=[spec] * 5 + [pl.BlockSpec((1, DN_W, DN_W), lambda b, t: (b * ns + ns - 1 - t, 0, 0)),
                               pl.BlockSpec((DN_SUB, DN_W, DN_W), lambda b, t: (b * ns + ns - 1 - t, 0, 0)),
                               spec],
        out_specs=[spec] * 5,
        out_shape=[jax.ShapeDtypeStruct((T, DN_W), F32)] * 5,
        scratch_shapes=[pltpu.VMEM((DN_W, DN_W), F32)],
        compiler_params=_cp("parallel", "arbitrary"),
    )(q, k, v, g, beta, states, tinvs, do)


def _dn_gate(o, z, nl):
    ms = _mmx(o * o, _head_mask((DN_W, DN_W))) * (1.0 / HD)
    return o * lax.rsqrt(ms + EPS) * nl * (z * _sigmoid(z))


def dn_gate_fwd(o, u_dn, nl, name):
    T = o.shape[0]
    tm = min(TOK_TILE, T)

    def body(o_ref, z_ref, n_ref, y_ref):
        y_ref[...] = _dn_gate(o_ref[...], z_ref[...], n_ref[...])

    return pl.pallas_call(
        body, name=name, grid=(T // tm,),
        in_specs=[_row_spec(tm, DN_W), pl.BlockSpec((tm, DN_W), lambda i: (i, 3)), _full_spec((1, DN_W))],
        out_specs=_row_spec(tm, DN_W),
        out_shape=jax.ShapeDtypeStruct((T, DN_W), F32),
        compiler_params=_cp("parallel"),
    )(o, u_dn, nl)


def dn_gate_bwd(o, u_dn, nl, dy, name):
    T = o.shape[0]
    tm = min(TOK_TILE, T)

    def body(o_ref, z_ref, n_ref, dy_ref, do_ref, dz_ref, dn_ref):
        @pl.when(pl.program_id(0) == 0)
        def _():
            dn_ref[...] = jnp.zeros_like(dn_ref)

        _, vjp = jax.vjp(_dn_gate, o_ref[...], z_ref[...], n_ref[...])
        do, dz, dn = vjp(dy_ref[...])
        do_ref[...] = do
        dz_ref[...] = dz
        fold = (_iota((LANE, DN_W), 0) == (_iota((LANE, DN_W), 1) & (HD - 1))).astype(F32)
        dn_ref[...] += _mmx_nt(jnp.concatenate([dn, jnp.zeros((7, DN_W), F32)], axis=0), fold)

    return pl.pallas_call(
        body, name=name, grid=(T // tm,),
        in_specs=[_row_spec(tm, DN_W), pl.BlockSpec((tm, DN_W), lambda i: (i, 3)), _full_spec((1, DN_W)),
                  _row_spec(tm, DN_W)],
        out_specs=[_row_spec(tm, DN_W), _row_spec(tm, DN_W), _full_spec((8, LANE))],
        out_shape=[jax.ShapeDtypeStruct((T, DN_W), F32), jax.ShapeDtypeStruct((T, DN_W), F32),
                   jax.ShapeDtypeStruct((8, LANE), F32)],
        compiler_params=_cp("arbitrary"),
    )(o, u_dn, nl, dy)


Y_SPLITS = (LRU_W, ATT_W, DN_W)
Y_OFFS = (0, LRU_W, LRU_W + ATT_W)


def wout_fwd(h, ys, w_out, name):
    T = h.shape[0]
    tm = min(TOK_TILE, T)

    def body(h_ref, y0, y1, y2, w_ref, o_ref, yc_ref):
        acc = h_ref[...]
        for ref, off, n in zip((y0, y1, y2), Y_OFFS, Y_SPLITS):
            y = ref[...].astype(BF16)
            yc_ref[:, off:off + n] = y
            acc += _mm(y, w_ref[off:off + n, :])
        o_ref[...] = acc

    return pl.pallas_call(
        body, name=name, grid=(T // tm,),
        in_specs=[_row_spec(tm, D)] + [_row_spec(tm, n) for n in Y_SPLITS] + [_full_spec((D, D))],
        out_specs=[_row_spec(tm, D), _row_spec(tm, D)],
        out_shape=[jax.ShapeDtypeStruct((T, D), F32), jax.ShapeDtypeStruct((T, D), BF16)],
        compiler_params=_cp("parallel"),
    )(h, *ys, w_out)


def wout_bwd(dy, w_out, name):
    T = dy.shape[0]
    tm = min(TOK_TILE, T)

    def body(dy_ref, w_ref, d0, d1, d2):
        dd = dy_ref[...].astype(BF16)
        for ref, off, n in zip((d0, d1, d2), Y_OFFS, Y_SPLITS):
            ref[...] = _mm_nt(dd, w_ref[off:off + n, :])

    return pl.pallas_call(
        body, name=name, grid=(T // tm,),
        in_specs=[_row_spec(tm, D), _full_spec((D, D))],
        out_specs=[_row_spec(tm, n) for n in Y_SPLITS],
        out_shape=[jax.ShapeDtypeStruct((T, n), F32) for n in Y_SPLITS],
        compiler_params=_cp("parallel"),
    )(dy, w_out)


def ple_fwd(h, g, pe, wg, wp, name):
    T = h.shape[0]
    tm = min(TOK_TILE, T)

    def body(h_ref, g_ref, p_ref, wg_ref, wp_ref, o_ref):
        hh = h_ref[...]
        xn = _rms(hh, g_ref[...])[0]
        o_ref[...] = hh + _sigmoid(_mm(xn, wg_ref[...])) * _mm(p_ref[...], wp_ref[...])

    return pl.pallas_call(
        body, name=name, grid=(T // tm,),
        in_specs=[_row_spec(tm, D), _full_spec((1, D)), _row_spec(tm, PLE), _full_spec((D, D)),
                  _full_spec((PLE, D))],
        out_specs=_row_spec(tm, D),
        out_shape=jax.ShapeDtypeStruct((T, D), F32),
        compiler_params=_cp("parallel"),
    )(h, g, pe, wg, wp)


def ple_bwd(h, dy, g, pe, wg, wp, name):
    T = h.shape[0]
    tm = min(TOK_TILE, T)

    def body(h_ref, dy_ref, g_ref, p_ref, wg_ref, wp_ref, dh_ref, dz_ref, dpp_ref, xn_ref, dn_ref):
        @pl.when(pl.program_id(0) == 0)
        def _():
            dn_ref[...] = jnp.zeros_like(dn_ref)

        gg = g_ref[...]
        dy = dy_ref[...]
        xn, xhat, rstd = _rms(h_ref[...], gg)
        gate = _sigmoid(_mm(xn, wg_ref[...]))
        pp = _mm(p_ref[...], wp_ref[...])
        dz = dy * pp * gate * (1.0 - gate)
        dz_ref[...] = dz.astype(BF16)
        dpp_ref[...] = (dy * gate).astype(BF16)
        xn_ref[...] = xn.astype(BF16)
        dh, dn = _rms_bwd(_mm_nt(dz, wg_ref[...]), xhat, rstd, gg)
        dh_ref[...] = dy + dh
        dn_ref[...] += dn

    return pl.pallas_call(
        body, name=name, grid=(T // tm,),
        in_specs=[_row_spec(tm, D), _row_spec(tm, D), _full_spec((1, D)), _row_spec(tm, PLE),
                  _full_spec((D, D)), _full_spec((PLE, D))],
        out_specs=[_row_spec(tm, D), _row_spec(tm, D), _row_spec(tm, D), _row_spec(tm, D), _full_spec((1, D))],
        out_shape=[jax.ShapeDtypeStruct((T, D), F32), jax.ShapeDtypeStruct((T, D), BF16),
                   jax.ShapeDtypeStruct((T, D), BF16), jax.ShapeDtypeStruct((T, D), BF16),
                   jax.ShapeDtypeStruct((1, D), F32)],
        compiler_params=_cp("arbitrary"),
    )(h, dy, g, pe, wg, wp)


def loss_head(h, g, target, name):
    T = h.shape[0]
    tm = min(TOK_TILE, T)

    def body(h_ref, g_ref, t_ref, loss_ref, dh_ref, dn_ref):
        @pl.when(pl.program_id(0) == 0)
        def _():
            dn_ref[...] = jnp.zeros_like(dn_ref)
            loss_ref[...] = jnp.zeros_like(loss_ref)

        gg = g_ref[...]
        y, xhat, rstd = _rms(h_ref[...], gg)
        err = y - t_ref[...]
        per_tok = jnp.mean(err * err, axis=-1, keepdims=True)
        loss_ref[...] += 0.5 * jnp.sum(per_tok, axis=0, keepdims=True)
        dh, dn = _rms_bwd(err * (1.0 / D), xhat, rstd, gg)
        dh_ref[...] = dh
        dn_ref[...] += dn

    return pl.pallas_call(
        body, name=name, grid=(T // tm,),
        in_specs=[_row_spec(tm, D), _full_spec((1, D)), _row_spec(tm, D)],
        out_specs=[_full_spec((8, LANE)), _row_spec(tm, D), _full_spec((1, D))],
        out_shape=[jax.ShapeDtypeStruct((8, LANE), F32), jax.ShapeDtypeStruct((T, D), F32),
                   jax.ShapeDtypeStruct((1, D), F32)],
        compiler_params=_cp("arbitrary"),
    )(h, g, target)


def _block_diag(w):
    return jnp.einsum('hij,hk->hikj', w, jnp.eye(4, dtype=w.dtype)).reshape(LRU_W, LRU_W)


def _layer_consts(W, l):
    row = lambda v: v.reshape(1, -1)
    zeros = jnp.zeros((5, LRU_W), F32)
    return dict(
        wa=_block_diag(W["lru_w_a"][l]), wx=_block_diag(W["lru_w_x"][l]),
        lru_vec=jnp.concatenate([row(W["lru_b_a"][l]), row(W["lru_b_x"][l]), row(W["lru_lambda"][l]), zeros], 0),
        lru_cb=row(W["lru_conv_b"][l]),
        sinks=W["attn_sinks"][l], rel=W["rel_bias"].reshape(-1),
        dn_cb=jnp.zeros((1, 3 * DN_W), F32),
        alog=row(jnp.repeat(W["dn_a_log"][l], HD)), dtb=row(jnp.repeat(W["dn_dt_bias"][l], HD)),
        dn_nl=row(jnp.tile(W["dn_norm"][l], DN_H)),
    )


def _layer_fwd(h0, pe, W, l, S):
    n = f"l{l}_"
    c_ = _layer_consts(W, l)
    row = lambda v: v.reshape(1, -1)
    h1 = ffn_fwd(h0, row(W["ffn1_norm"][l]), W["ffn1_w_gate"][l], W["ffn1_w_up"][l], W["ffn1_w_down"][l],
                 n + "ffn1_fwd")
    u_lru, u_att, u_dn, u_ba, xn_mix = mixin_fwd(h1, row(W["mix_norm"][l]), W["w_in"][l], n + "mixin_fwd")
    xr = conv_fwd(u_lru, W["lru_conv_w"][l], c_["lru_cb"], S, 0, LRU_W, n + "lru_conv_fwd")
    y_lru = lru_fwd(xr, u_lru, c_["wa"], c_["wx"], c_["lru_vec"], S, n + "lru_fwd")
    y_att = attn_fwd(u_att, c_["sinks"], c_["rel"], S, n + "attn_fwd")
    cc = conv_fwd(u_dn, W["dn_conv_w"][l], c_["dn_cb"], S, 0, 3 * DN_W, n + "dn_conv_fwd")
    q, k, v, g, beta = dn_point_fwd(cc, u_ba, c_["alog"], c_["dtb"], n + "dn_point_fwd")
    o, states, tinvs = dn_scan_fwd(q, k, v, g, beta, S, n + "dn_scan_fwd")
    y_dn = dn_gate_fwd(o, u_dn, c_["dn_nl"], n + "dn_gate_fwd")
    h2, ycat = wout_fwd(h1, (y_lru, y_att, y_dn), W["w_out"][l], n + "wout_fwd")
    h3 = ffn_fwd(h2, row(W["ffn2_norm"][l]), W["ffn2_w_gate"][l], W["ffn2_w_up"][l], W["ffn2_w_down"][l],
                 n + "ffn2_fwd")
    h4 = ple_fwd(h3, row(W["ple_norm"][l]), pe, W["ple_w_gate"][l], W["ple_w_proj"][l], n + "ple_fwd")
    saved = dict(h0=h0, h1=h1, h2=h2, h3=h3, u_lru=u_lru, u_att=u_att, u_dn=u_dn, u_ba=u_ba, xn_mix=xn_mix,
                 xr=xr, cc=cc, q=q, k=k, v=v, g=g, beta=beta, o=o, states=states, tinvs=tinvs, ycat=ycat)
    return h4, saved


def _layer_bwd(dh4, sv, pe, W, l, S):
    n = f"l{l}_"
    c_ = _layer_consts(W, l)
    row = lambda v: v.reshape(1, -1)
    G = {}
    dh3, dz, dpp, xn_p, dn = ple_bwd(sv["h3"], dh4, row(W["ple_norm"][l]), pe, W["ple_w_gate"][l],
                                     W["ple_w_proj"][l], n + "ple_bwd")
    G["ple_norm"] = dn[0]
    G["ple_w_gate"] = matmul_tn(xn_p, dz, n + "d_ple_w_gate")
    G["ple_w_proj"] = matmul_tn(pe, dpp, n + "d_ple_w_proj")

    def ffn_back(which, h_in, dy):
        dh, dgt, dup, act, xn, dn_ = ffn_bwd(h_in, dy, row(W[which + "_norm"][l]), W[which + "_w_gate"][l],
                                             W[which + "_w_up"][l], W[which + "_w_down"][l], n + which + "_bwd")
        G[which + "_norm"] = dn_[0]
        G[which + "_w_gate"] = matmul_tn(xn, dgt, n + "d_" + which + "_w_gate")
        G[which + "_w_up"] = matmul_tn(xn, dup, n + "d_" + which + "_w_up")
        G[which + "_w_down"] = matmul_tn(act, dy, n + "d_" + which + "_w_down", scale=0.5)
        return dh

    dh2 = ffn_back("ffn2", sv["h2"], dh3)
    dy_lru, dy_att, dy_dn = wout_bwd(dh2, W["w_out"][l], n + "wout_bwd")
    G["w_out"] = matmul_tn(sv["ycat"], dh2, n + "d_w_out")
    do, dz_dn, dnn = dn_gate_bwd(sv["o"], sv["u_dn"], c_["dn_nl"], dy_dn, n + "dn_gate_bwd")
    dqkvgb = dn_scan_bwd(sv["q"], sv["k"], sv["v"], sv["g"], sv["beta"], sv["states"], sv["tinvs"], do, S,
                         n + "dn_scan_bwd")
    dcc, du_ba, dvec_dn = dn_point_bwd(sv["cc"], sv["u_ba"], c_["alog"], c_["dtb"], dqkvgb, n + "dn_point_bwd")
    dqkv, dwb_dn = conv_bwd(sv["u_dn"], dcc, W["dn_conv_w"][l], S, 0, 3 * DN_W, n + "dn_conv_bwd")
    du_dn = jnp.concatenate([dqkv, dz_dn], axis=1)
    G["dn_norm"] = dnn[0, 0:HD]
    G["dn_a_log"] = dvec_dn[0, 0:DN_H]
    G["dn_dt_bias"] = dvec_dn[1, 0:DN_H]
    G["dn_conv_w"] = dwb_dn[0:4]
    du_att, drel, dsk = attn_bwd(sv["u_att"], dy_att, c_["sinks"], c_["rel"], S, n + "attn_bwd")
    G["attn_sinks"] = dsk[:, 0]
    G["rel_bias"] = drel[:, 0:REL_BUCKETS].T
    dxr, dgt_lru, dwa, dwx, dvec = lru_bwd(sv["xr"], sv["u_lru"], dy_lru, c_["wa"], c_["wx"], c_["lru_vec"], S,
                                           n + "lru_bwd")
    dx_lru, dwb_lru = conv_bwd(sv["u_lru"], dxr, W["lru_conv_w"][l], S, 0, LRU_W, n + "lru_conv_bwd")
    du_lru = jnp.concatenate([dx_lru, dgt_lru], axis=1)
    diag = lambda m: jnp.stack([m[c, HD * e:HD * (e + 1), HD * e:HD * (e + 1)] for c in range(2) for e in range(2)])
    G["lru_w_a"], G["lru_w_x"] = diag(dwa), diag(dwx)
    G["lru_b_a"], G["lru_b_x"], G["lru_lambda"] = dvec[0], dvec[1], dvec[2]
    G["lru_conv_w"], G["lru_conv_b"] = dwb_lru[0:4], dwb_lru[4]
    dh1, du_cat, dn = mixin_bwd(sv["h1"], dh2, row(W["mix_norm"][l]), W["w_in"][l],
                                (du_lru, du_att, du_dn, du_ba), n + "mixin_bwd")
    G["mix_norm"] = dn[0]
    G["w_in"] = matmul_tn(sv["xn_mix"], du_cat, n + "d_w_in")
    dh0 = ffn_back("ffn1", sv["h0"], dh1)
    return dh0, G


def _core(x, pe, W, target, S):
    h = x
    saved = []
    for l in range(DEPTH):
        h, sv = _layer_fwd(h, pe[l], W, l, S)
        saved.append(sv)
    loss_tile, dh, dfn = loss_head(h, W["final_norm"].reshape(1, -1), target, "loss_head")
    grads = [None] * DEPTH
    for l in reversed(range(DEPTH)):
        dh, grads[l] = _layer_bwd(dh, saved[l], pe[l], W, l, S)
    return loss_tile[0, 0], dh, grads, dfn[0]


MESH_ID = pl.DeviceIdType.MESH
ANY_SPEC = pl.BlockSpec(memory_space=pl.ANY)
AXES = ("x", "y", "c")


def _my_pos():
    return lax.axis_index("x"), lax.axis_index("y"), lax.axis_index("c")


def _slot_of(px, py, pc):
    return 4 * px + 2 * py + pc


def all_gather(x, name):
    R, C = x.shape

    def body(x_ref, out_ref, send_sems, recv_sems, local_sem):
        mx, my, mc = _my_pos()
        me, sibling = (mx, my, mc), (mx, my, 1 - mc)
        chips = [(1 - mx, my), (mx, 1 - my), (1 - mx, 1 - my)]

        def copy(k, block, to, src=None):
            dst = out_ref.at[_slot_of(*block)]
            return pltpu.make_async_remote_copy(
                src_ref=dst if src is None else src, dst_ref=dst,
                send_sem=send_sems.at[k], recv_sem=recv_sems.at[k],
                device_id=to, device_id_type=MESH_ID)

        mine = pltpu.make_async_copy(x_ref, out_ref.at[_slot_of(*me)], local_sem)
        mine.start()
        first = [copy(0, me, sibling, src=x_ref)]
        first += [copy(1 + j, me, (*chip, mc), src=x_ref) for j, chip in enumerate(chips)]
        for cp in first:
            cp.start()
        passed = [copy(4 + j, (*chip, mc), sibling) for j, chip in enumerate(chips)]
        for j, chip in enumerate(chips):
            copy(1 + j, (*chip, mc), me).wait_recv()
            passed[j].start()
        copy(0, sibling, me).wait_recv()
        for j, chip in enumerate(chips):
            copy(4 + j, (*chip, 1 - mc), me).wait_recv()
        for cp in first + passed:
            cp.wait_send()
        mine.wait()

    return pl.pallas_call(
        body, name=name,
        out_shape=jax.ShapeDtypeStruct((N_DEV, R, C), x.dtype),
        in_specs=[ANY_SPEC], out_specs=ANY_SPEC,
        scratch_shapes=[pltpu.SemaphoreType.DMA((7,)), pltpu.SemaphoreType.DMA((7,)), pltpu.SemaphoreType.DMA],
    )(x)


def exchange(x, name):
    _, R, C = x.shape

    def body(x_ref, out_ref, send_sems, recv_sems, local_sem):
        mx, my, mc = _my_pos()
        mine = _slot_of(mx, my, mc)
        local = pltpu.make_async_copy(x_ref.at[mine], out_ref.at[mine], local_sem)
        local.start()
        sent = []
        for r in range(1, N_DEV):
            peer = (1 - mx if r & 4 else mx, 1 - my if r & 2 else my, 1 - mc if r & 1 else mc)
            ps = _slot_of(*peer)
            cp = pltpu.make_async_remote_copy(
                src_ref=x_ref.at[ps], dst_ref=out_ref.at[mine],
                send_sem=send_sems.at[r - 1], recv_sem=recv_sems.at[r - 1],
                device_id=peer, device_id_type=MESH_ID)
            cp.start()
            sent.append((cp, r, peer, ps))
        for cp, r, peer, ps in sent:
            pltpu.make_async_remote_copy(
                src_ref=x_ref.at[ps], dst_ref=out_ref.at[ps],
                send_sem=send_sems.at[r - 1], recv_sem=recv_sems.at[r - 1],
                device_id=peer, device_id_type=MESH_ID).wait_recv()
        for cp, _, _, _ in sent:
            cp.wait_send()
        local.wait()

    return pl.pallas_call(
        body, name=name,
        out_shape=jax.ShapeDtypeStruct((N_DEV, R, C), x.dtype),
        in_specs=[ANY_SPEC], out_specs=ANY_SPEC,
        scratch_shapes=[pltpu.SemaphoreType.DMA((7,)), pltpu.SemaphoreType.DMA((7,)), pltpu.SemaphoreType.DMA],
    )(x)


def sum_parts(parts, name):
    _, R, C = parts.shape
    tr = _pick(R, (320, 336, 256, 128, 64, 32, 16, 8))

    def body(p_ref, o_ref):
        acc = p_ref[0].astype(F32)
        for k in range(1, N_DEV):
            acc += p_ref[k].astype(F32)
        o_ref[...] = acc

    return pl.pallas_call(
        body, name=name, grid=(R // tr,),
        in_specs=[pl.BlockSpec((N_DEV, tr, C), lambda i: (0, i, 0))],
        out_specs=pl.BlockSpec((tr, C), lambda i: (i, 0)),
        out_shape=jax.ShapeDtypeStruct((R, C), F32),
        compiler_params=_cp("parallel"),
    )(parts)


def adamw(g, w, m, v, name):
    R, C = g.shape
    tr = _pick(R, (512, 352, 256, 128, 64, 32, 16, 8))
    c1 = 1.0 - ADAM_B1 ** ADAM_STEP
    c2 = 1.0 - ADAM_B2 ** ADAM_STEP

    def body(g_ref, w_ref, m_ref, v_ref, d_ref, nm_ref, nv_ref):
        gg = g_ref[...]
        mm = ADAM_B1 * m_ref[...] + (1.0 - ADAM_B1) * gg
        vv = ADAM_B2 * v_ref[...] + (1.0 - ADAM_B2) * (gg * gg)
        nm_ref[...] = mm
        nv_ref[...] = vv
        d_ref[...] = -ADAM_LR * ((mm / c1) / (jnp.sqrt(vv / c2) + ADAM_EPS) + ADAM_WD * w_ref[...])

    spec = pl.BlockSpec((tr, C), lambda i: (i, 0))
    return pl.pallas_call(
        body, name=name, grid=(R // tr,),
        in_specs=[spec] * 4, out_specs=[spec] * 3,
        out_shape=[jax.ShapeDtypeStruct((R, C), F32)] * 3,
        compiler_params=_cp("parallel"),
    )(g, w, m, v)


PACK_W = 1024
BIG = (("ffn1_w_gate", 1, D, FF), ("ffn1_w_up", 1, D, FF), ("ffn1_w_down", 0, FF, D),
       ("w_in", 1, D, D_IN), ("w_out", 0, D, D),
       ("ffn2_w_gate", 1, D, FF), ("ffn2_w_up", 1, D, FF), ("ffn2_w_down", 0, FF, D),
       ("ple_w_gate", 0, D, D), ("ple_w_proj", 1, PLE, D))
BIG_ROWS = sum(r * c // N_DEV // PACK_W for _, _, r, c in BIG) * DEPTH
BIG_ROWS_PAD = 5440
SMALL = (("ffn1_norm", (D,), None), ("mix_norm", (D,), None), ("lru_conv_w", (4, LRU_W), LRU_W // N_DEV),
         ("lru_conv_b", (LRU_W,), None), ("lru_w_a", (4, HD, HD), None), ("lru_b_a", (LRU_W,), None),
         ("lru_w_x", (4, HD, HD), None), ("lru_b_x", (LRU_W,), None), ("lru_lambda", (LRU_W,), None),
         ("attn_sinks", (ATT_H,), None), ("dn_conv_w", (4, 3 * DN_W), 3 * DN_W // N_DEV),
         ("dn_a_log", (DN_H,), None), ("dn_dt_bias", (DN_H,), None), ("dn_norm", (HD,), None),
         ("ffn2_norm", (D,), None), ("ple_norm", (D,), None))
SINGLE = (("rel_bias", (REL_BUCKETS, ATT_H)), ("final_norm", (D,)))


def _shard_shape(axis, r, c):
    return (r // N_DEV, c) if axis == 0 else (r, c // N_DEV)


def _pack_rows(arrs, width, mult):
    flat = jnp.concatenate([a.reshape(-1) for a in arrs])
    rows = -(-flat.shape[0] // (width * mult)) * mult
    return jnp.pad(flat, (0, rows * width - flat.shape[0])).reshape(rows, width)


def _unpack_rows(packed, shapes):
    flat = packed.reshape(-1)
    out, off = [], 0
    for s in shapes:
        n = int(np.prod(s))
        out.append(flat[off:off + n].reshape(s))
        off += n
    return out


def _pack_big_shards(a):
    parts = [a[name][l].reshape(-1, PACK_W) for l in range(DEPTH) for name, _, _, _ in BIG]
    packed = jnp.concatenate(parts, axis=0).astype(BF16)
    return jnp.pad(packed, ((0, BIG_ROWS_PAD - BIG_ROWS), (0, 0)))


def _unpack_big_full(gathered):
    out = {name: [] for name, _, _, _ in BIG}
    off = 0
    for l in range(DEPTH):
        for name, axis, r, c in BIG:
            n = r * c // N_DEV // PACK_W
            seg = gathered[:, off:off + n, :]
            off += n
            if axis == 0:
                full = seg.reshape(r, c)
            else:
                full = seg.reshape(N_DEV, r, c // N_DEV).transpose(1, 0, 2).reshape(r, c)
            if name == "w_in":
                full = jnp.pad(full, ((0, 0), (0, D_IN_PAD - D_IN)))
            out[name].append(full)
    return {k: jnp.stack(v) for k, v in out.items()}


def _pack_big_grads(grads):
    parts = []
    for l in range(DEPTH):
        for name, axis, r, c in BIG:
            g = grads[l][name]
            if name == "w_in":
                g = g[:, :D_IN]
            n = r * c // N_DEV // PACK_W
            if axis == 0:
                parts.append(g.reshape(N_DEV, n, PACK_W))
            else:
                parts.append(g.reshape(r, N_DEV, c // N_DEV).transpose(1, 0, 2).reshape(N_DEV, n, PACK_W))
    packed = jnp.concatenate(parts, axis=1)
    return jnp.pad(packed, ((0, 0), (0, BIG_ROWS_PAD - BIG_ROWS), (0, 0)))


def _unpack_big_shards(summed):
    out = {name: [] for name, _, _, _ in BIG}
    off = 0
    for l in range(DEPTH):
        for name, axis, r, c in BIG:
            n = r * c // N_DEV // PACK_W
            out[name].append(summed[off:off + n].reshape(_shard_shape(axis, r, c)))
            off += n
    return {k: jnp.stack(v) for k, v in out.items()}


def kernel(x, p, ffn1_norm, ffn1_w_gate, ffn1_w_up, ffn1_w_down, mix_norm, w_in, lru_conv_w, lru_conv_b, lru_w_a, lru_b_a, lru_w_x, lru_b_x, lru_lambda, attn_sinks, rel_bias, dn_conv_w, dn_a_log, dn_dt_bias, dn_norm, w_out, ffn2_norm, ffn2_w_gate, ffn2_w_up, ffn2_w_down, ple_norm, ple_w_gate, ple_w_proj, final_norm, loss_target, m_ffn1_norm, m_ffn1_w_gate, m_ffn1_w_up, m_ffn1_w_down, m_mix_norm, m_w_in, m_lru_conv_w, m_lru_conv_b, m_lru_w_a, m_lru_b_a, m_lru_w_x, m_lru_b_x, m_lru_lambda, m_attn_sinks, m_rel_bias, m_dn_conv_w, m_dn_a_log, m_dn_dt_bias, m_dn_norm, m_w_out, m_ffn2_norm, m_ffn2_w_gate, m_ffn2_w_up, m_ffn2_w_down, m_ple_norm, m_ple_w_gate, m_ple_w_proj, m_final_norm, v_ffn1_norm, v_ffn1_w_gate, v_ffn1_w_up, v_ffn1_w_down, v_mix_norm, v_w_in, v_lru_conv_w, v_lru_conv_b, v_lru_w_a, v_lru_b_a, v_lru_w_x, v_lru_b_x, v_lru_lambda, v_attn_sinks, v_rel_bias, v_dn_conv_w, v_dn_a_log, v_dn_dt_bias, v_dn_norm, v_w_out, v_ffn2_norm, v_ffn2_w_gate, v_ffn2_w_up, v_ffn2_w_down, v_ple_norm, v_ple_w_gate, v_ple_w_proj, v_final_norm):
    a = dict(locals())
    nb, S, _ = x.shape
    T = nb * S
    my_slot = _slot_of(*_my_pos())

    W = _unpack_big_full(all_gather(_pack_big_shards(a), "gather_weights"))
    taps = all_gather(_pack_rows([lru_conv_w, dn_conv_w], LANE, 8), "gather_conv_taps")
    tap_shapes = [lru_conv_w.shape, dn_conv_w.shape]
    lcw, dcw = zip(*[_unpack_rows(taps[k], tap_shapes) for k in range(N_DEV)])
    W["lru_conv_w"] = jnp.concatenate(lcw, axis=-1)
    W["dn_conv_w"] = jnp.concatenate(dcw, axis=-1)
    for name, _, cols in SMALL:
        if cols is None:
            W[name] = a[name]
    W["rel_bias"], W["final_norm"] = rel_bias, final_norm

    loss_local, dx, grads, d_final = _core(x.reshape(T, D), p.reshape(DEPTH, T, PLE), W,
                                           loss_target.reshape(T, D), S)
    loss = lax.psum(loss_local, AXES)

    g_big = _unpack_big_shards(
        sum_parts(exchange(_pack_big_grads(grads), "exchange_grads"), "sum_grads"))

    small_full = [jnp.stack([grads[l][name] for l in range(DEPTH)]) for name, _, _ in SMALL]
    small_full += [grads[0]["rel_bias"] + grads[1]["rel_bias"], d_final]
    small_sum = sum_parts(all_gather(_pack_rows(small_full, LANE, 8), "gather_small_grads"), "sum_small_grads")
    g_small = dict(zip([n for n, _, _ in SMALL] + [n for n, _ in SINGLE],
                       _unpack_rows(small_sum, [s.shape for s in small_full])))
    for name, _, cols in SMALL:
        if cols is not None:
            g_small[name] = lax.dynamic_slice_in_dim(g_small[name], my_slot * cols, cols, axis=2)

    out = {}
    for name, _, _, _ in BIG:
        shape = a[name].shape
        two_d = lambda t: t.reshape(-1, shape[-1])
        res = adamw(two_d(g_big[name]), two_d(a[name]), two_d(a["m_" + name]), two_d(a["v_" + name]),
                    "adamw_" + name)
        out[name] = (g_big[name],) + tuple(r.reshape(shape) for r in res)
    small_names = [n for n, _, _ in SMALL] + [n for n, _ in SINGLE]
    shapes = [a[n].shape for n in small_names]
    packed = [_pack_rows([a[pre + n] if pre is not None else g_small[n] for n in small_names], LANE, 8)
              for pre in (None, "", "m_", "v_")]
    res = adamw(*packed, "adamw_small")
    unpacked = [_unpack_rows(r, shapes) for r in res]
    for i, n in enumerate(small_names):
        out[n] = (g_small[n].reshape(shapes[i]),) + tuple(u[i] for u in unpacked)

    order = ['ffn1_norm', 'ffn1_w_gate', 'ffn1_w_up', 'ffn1_w_down', 'mix_norm', 'w_in', 'lru_conv_w', 'lru_conv_b',
             'lru_w_a', 'lru_b_a', 'lru_w_x', 'lru_b_x', 'lru_lambda', 'attn_sinks', 'rel_bias', 'dn_conv_w',
             'dn_a_log', 'dn_dt_bias', 'dn_norm', 'w_out', 'ffn2_norm', 'ffn2_w_gate', 'ffn2_w_up', 'ffn2_w_down',
             'ple_norm', 'ple_w_gate', 'ple_w_proj', 'final_norm']
    return (loss, dx.reshape(x.shape)) + tuple(out[n][k] for k in range(4) for n in order)
```

```python
import functools
import math

import numpy as np
import jax
import jax.numpy as jnp
from jax import lax
from jax.experimental import pallas as pl
from jax.experimental.pallas import tpu as pltpu

F32 = jnp.float32
BF16 = jnp.bfloat16
HI = lax.Precision.HIGHEST

D = 1024
DEPTH = 2
EPS = 1e-6
PLE = 256
FF = 2816
HD = 64
LRU_W = 256
LRU_C = 8.0
ATT_W = 512
ATT_H = 8
ATT_KV = 2
ATT_G = 4
KV_W = 128
WINDOW = 128
BQ = 128
REL_BUCKETS = 32
REL_MAX_DIST = 128
DN_W = 256
DN_H = 4
CHUNK = 64
D_IN = 2312
D_IN_PAD = 2432
N_DEV = 8

ADAM_LR = 0.001
ADAM_B1 = 0.9
ADAM_B2 = 0.999
ADAM_EPS = 1e-08
ADAM_WD = 0.01
ADAM_STEP = 10

LANE = 128
VMEM_LIMIT = 56 * 1024 * 1024
SH = FF // N_DEV
SHP = 384
FFP = N_DEV * SHP
FF_TILE = 2 * SHP
TOK_TILE = 512
B_DOWN1, B_DOWN2, B_WOUT, B_PGATE, B_WIN, B_PPROJ, B_ROWS = 0, 384, 768, 896, 1024, 1328, 1360
WIN_ROWS = D * D_IN // N_DEV // 1024
NEG = -1e30


def _cp(*sem):
    return pltpu.CompilerParams(dimension_semantics=tuple(sem), vmem_limit_bytes=VMEM_LIMIT)


def _dg(a, b, ca, cb, exact):
    dims = (((ca,), (cb,)), ((), ()))
    if exact == "f32":
        return lax.dot_general(a.astype(F32), b.astype(F32), dims, precision=HI, preferred_element_type=F32)
    if exact == "split":
        a_hi, b_hi = a.astype(BF16), b.astype(BF16)
        a_lo = (a - a_hi.astype(F32)).astype(BF16)
        b_lo = (b - b_hi.astype(F32)).astype(BF16)
        dot = lambda u, v: lax.dot_general(u, v, dims, preferred_element_type=F32)
        return dot(a_hi, b_hi) + (dot(a_hi, b_lo) + dot(a_lo, b_hi))
    return lax.dot_general(a.astype(BF16), b.astype(BF16), dims, preferred_element_type=F32)


def _make_mm(exact):
    @jax.custom_vjp
    def mm(a, b):
        return _dg(a, b, 1, 0, exact)

    @jax.custom_vjp
    def mm_nt(a, b):
        return _dg(a, b, 1, 1, exact)

    @jax.custom_vjp
    def mm_tn(a, b):
        return _dg(a, b, 0, 0, exact)

    mm.defvjp(lambda a, b: (mm(a, b), (a, b)),
              lambda r, d: (mm_nt(d, r[1]), mm_tn(r[0], d)))
    mm_nt.defvjp(lambda a, b: (mm_nt(a, b), (a, b)),
                 lambda r, d: (mm(d, r[1]), mm_tn(d, r[0])))
    mm_tn.defvjp(lambda a, b: (mm_tn(a, b), (a, b)),
                 lambda r, d: (mm_nt(r[1], d), mm(r[0], d)))
    return mm, mm_nt, mm_tn


_mm, _mm_nt, _mm_tn = _make_mm("bf16")
_mmx, _mmx_nt, _mmx_tn = _make_mm("f32")
_mm3, _mm3_nt, _mm3_tn = _make_mm("split")


def _iota(shape, dim):
    return lax.broadcasted_iota(jnp.int32, shape, dim)


def _sigmoid(x):
    return 1.0 / (1.0 + jnp.exp(-x))


def _rms(h, g):
    rstd = lax.rsqrt(jnp.mean(h * h, axis=-1, keepdims=True) + EPS)
    xhat = h * rstd
    return xhat * g, xhat, rstd


def _rms_bwd(dxn, xhat, rstd, g):
    dxhat = dxn * g
    dh = rstd * (dxhat - xhat * jnp.mean(dxhat * xhat, axis=-1, keepdims=True))
    dg = jnp.sum(dxn * xhat, axis=0, keepdims=True)
    return dh, dg


def _row_spec(tm, n):
    return pl.BlockSpec((tm, n), lambda i, *_: (i, 0))


def _full_spec(shape):
    nd = len(shape)
    return pl.BlockSpec(shape, lambda *_: (0,) * nd)


def _ffn_weight_specs(fidx):
    return [pl.BlockSpec((D, FF_TILE), lambda i, j: (2 * fidx, j)),
            pl.BlockSpec((D, FF_TILE), lambda i, j: (2 * fidx + 1, j)),
            pl.BlockSpec((2, SHP, D), lambda i, j: (j, fidx, 0))]


def ffn_fwd(h, g, wa, wb, fidx, name):
    T = h.shape[0]
    tm = min(TOK_TILE, T)
    nj = FFP // FF_TILE

    def body(h_ref, g_ref, wg_ref, wu_ref, wd_ref, o_ref, xn_s):
        j = pl.program_id(1)

        @pl.when(j == 0)
        def _():
            hh = h_ref[...]
            xn_s[...] = _rms(hh, g_ref[...])[0].astype(BF16)
            o_ref[...] = hh

        xn = xn_s[...]
        gt = _mm(xn, wg_ref[...])
        up = _mm(xn, wu_ref[...])
        act = gt * _sigmoid(gt) * up
        o_ref[...] += 0.5 * _mm(act, wd_ref[...].reshape(FF_TILE, D))

    return pl.pallas_call(
        body, name=name, grid=(T // tm, nj),
        in_specs=[pl.BlockSpec((tm, D), lambda i, j: (i, 0)),
                  pl.BlockSpec((1, D), lambda i, j: (0, 0))] + _ffn_weight_specs(fidx),
        out_specs=pl.BlockSpec((tm, D), lambda i, j: (i, 0)),
        out_shape=jax.ShapeDtypeStruct((T, D), F32),
        scratch_shapes=[pltpu.VMEM((tm, D), BF16)],
        compiler_params=_cp("parallel", "arbitrary"),
    )(h, g, wa, wa, wb)


def ffn_bwd(h, dy, g, wa, wb, fidx, name):
    T = h.shape[0]
    tm = min(TOK_TILE, T)
    nj = FFP // FF_TILE

    def body(h_ref, dy_ref, g_ref, wg_ref, wu_ref, wd_ref,
             dh_ref, dg_ref, du_ref, a_ref, xn_ref, dn_ref, xn_s, dxn_s):
        i = pl.program_id(0)
        j = pl.program_id(1)

        @pl.when(j == 0)
        def _():
            xn = _rms(h_ref[...], g_ref[...])[0].astype(BF16)
            xn_s[...] = xn
            xn_ref[...] = xn
            dxn_s[...] = jnp.zeros_like(dxn_s)

        @pl.when((i == 0) & (j == 0))
        def _():
            dn_ref[...] = jnp.zeros_like(dn_ref)

        xn = xn_s[...]
        gt = _mm(xn, wg_ref[...])
        up = _mm(xn, wu_ref[...])
        sg = _sigmoid(gt)
        si = gt * sg
        da = _mm_nt(0.5 * dy_ref[...], wd_ref[...].reshape(FF_TILE, D))
        dup = da * si
        dgt = da * up * (sg * (1.0 + gt * (1.0 - sg)))
        dg_ref[...] = dgt.astype(BF16)
        du_ref[...] = dup.astype(BF16)
        a_ref[...] = (si * up).astype(BF16)
        dxn_s[...] += _mm_nt(dgt, wg_ref[...]) + _mm_nt(dup, wu_ref[...])

        @pl.when(j == nj - 1)
        def _():
            gg = g_ref[...]
            _, xhat, rstd = _rms(h_ref[...], gg)
            dh, dn = _rms_bwd(dxn_s[...], xhat, rstd, gg)
            dh_ref[...] = dy_ref[...] + dh
            dn_ref[...] += dn

    tile = pl.BlockSpec((tm, FF_TILE), lambda i, j: (i, j))
    return pl.pallas_call(
        body, name=name, grid=(T // tm, nj),
        in_specs=[pl.BlockSpec((tm, D), lambda i, j: (i, 0)),
                  pl.BlockSpec((tm, D), lambda i, j: (i, 0)),
                  pl.BlockSpec((1, D), lambda i, j: (0, 0))] + _ffn_weight_specs(fidx),
        out_specs=[pl.BlockSpec((tm, D), lambda i, j: (i, 0)), tile, tile, tile,
                   pl.BlockSpec((tm, D), lambda i, j: (i, 0)),
                   pl.BlockSpec((1, D), lambda i, j: (0, 0))],
        out_shape=[jax.ShapeDtypeStruct((T, D), F32)] + [jax.ShapeDtypeStruct((T, FFP), BF16)] * 3
        + [jax.ShapeDtypeStruct((T, D), BF16), jax.ShapeDtypeStruct((1, D), F32)],
        scratch_shapes=[pltpu.VMEM((tm, D), BF16), pltpu.VMEM((tm, D), F32)],
        compiler_params=_cp("arbitrary", "arbitrary"),
    )(h, dy, g, wa, wa, wb)


def _pick(n, prefs):
    for t in prefs:
        if n % t == 0:
            return t
    return n


def _tn_body(nk, scale, out_dtype, squeeze):
    def body(a_ref, b_ref, *rest):
        o_ref, acc = rest[-2], rest[-1]
        k = pl.program_id(2)

        @pl.when(k == 0)
        def _():
            acc[...] = jnp.zeros_like(acc)

        acc[...] += _mm_tn(a_ref[...], b_ref[...])

        @pl.when(k == nk - 1)
        def _():
            res = (scale * acc[...]).astype(out_dtype)
            if squeeze:
                o_ref[0] = res
            else:
                o_ref[...] = res

    return body


def matmul_tn(a, b, name, scale=1.0, out_dtype=BF16):
    T, M = a.shape
    N = b.shape[1]
    tmm = _pick(M, (512, 256))
    tnn = _pick(N, (1024, 2432))
    tk = min(TOK_TILE, T)
    nk = T // tk
    return pl.pallas_call(
        _tn_body(nk, scale, out_dtype, False), name=name, grid=(M // tmm, N // tnn, nk),
        in_specs=[pl.BlockSpec((tk, tmm), lambda i, j, k: (k, i)),
                  pl.BlockSpec((tk, tnn), lambda i, j, k: (k, j))],
        out_specs=pl.BlockSpec((tmm, tnn), lambda i, j, k: (i, j)),
        out_shape=jax.ShapeDtypeStruct((M, N), out_dtype),
        scratch_shapes=[pltpu.VMEM((tmm, tnn), F32)],
        compiler_params=_cp("parallel", "parallel", "arbitrary"),
    )(a, b)


def grad_cols(a, b, dst, slot, name):
    T = a.shape[0]
    tmm, tnn = 512, FFP // 2
    tk = min(TOK_TILE, T)
    nk = T // tk
    return pl.pallas_call(
        _tn_body(nk, 1.0, BF16, True), name=name, grid=(D // tmm, FFP // tnn, nk),
        in_specs=[pl.BlockSpec((tk, tmm), lambda i, j, k: (k, i)),
                  pl.BlockSpec((tk, tnn), lambda i, j, k: (k, j)),
                  pl.BlockSpec(memory_space=pl.ANY)],
        out_specs=pl.BlockSpec((1, tmm, tnn), lambda i, j, k: (slot, i, j)),
        out_shape=jax.ShapeDtypeStruct(dst.shape, dst.dtype),
        scratch_shapes=[pltpu.VMEM((tmm, tnn), F32)],
        input_output_aliases={2: 0},
        compiler_params=_cp("parallel", "parallel", "arbitrary"),
    )(a, b, dst)


def grad_rows(a, b, dst, row0, rows, name, scale=1.0):
    T = a.shape[0]
    tk = min(TOK_TILE, T)
    nk = T // tk
    blk = row0 // rows
    return pl.pallas_call(
        _tn_body(nk, scale, BF16, True), name=name, grid=(N_DEV, 1, nk),
        in_specs=[pl.BlockSpec((tk, rows), lambda i, j, k: (k, i)),
                  pl.BlockSpec((tk, D), lambda i, j, k: (k, 0)),
                  pl.BlockSpec(memory_space=pl.ANY)],
        out_specs=pl.BlockSpec((1, rows, D), lambda i, j, k: (i, blk, 0)),
        out_shape=jax.ShapeDtypeStruct(dst.shape, dst.dtype),
        scratch_shapes=[pltpu.VMEM((rows, D), F32)],
        input_output_aliases={2: 0},
        compiler_params=_cp("parallel", "parallel", "arbitrary"),
    )(a, b, dst)


U_SPLITS = (512, 768, 1024, 128)
U_OFFS = (0, 512, 1280, 2304)


def mixin_fwd(h, g, w_in, name):
    T = h.shape[0]
    tm = min(TOK_TILE, T)

    def body(h_ref, g_ref, w_ref, u0, u1, u2, u3, xn_ref):
        xn = _rms(h_ref[...], g_ref[...])[0].astype(BF16)
        xn_ref[...] = xn
        u = _mm(xn, w_ref[...])
        for ref, off, n in zip((u0, u1, u2, u3), U_OFFS, U_SPLITS):
            ref[...] = u[:, off:off + n]

    return pl.pallas_call(
        body, name=name, grid=(T // tm,),
        in_specs=[_row_spec(tm, D), _full_spec((1, D)), _full_spec((D, D_IN_PAD))],
        out_specs=[_row_spec(tm, n) for n in U_SPLITS] + [_row_spec(tm, D)],
        out_shape=[jax.ShapeDtypeStruct((T, n), F32) for n in U_SPLITS]
        + [jax.ShapeDtypeStruct((T, D), BF16)],
        compiler_params=_cp("parallel"),
    )(h, g, w_in)


def mixin_bwd(h, dh_in, g, w_in, dus, name):
    T = h.shape[0]
    tm = min(TOK_TILE, T)

    def body(h_ref, dhi_ref, g_ref, w_ref, d0, d1, d2, d3, dh_ref, du_ref, dn_ref):
        @pl.when(pl.program_id(0) == 0)
        def _():
            dn_ref[...] = jnp.zeros_like(dn_ref)

        dxn = jnp.zeros((tm, D), F32)
        for ref, off, n in zip((d0, d1, d2, d3), U_OFFS, U_SPLITS):
            du = ref[...]
            du_ref[:, off:off + n] = du.astype(BF16)
            dxn += _mm_nt(du, w_ref[:, off:off + n])
        gg = g_ref[...]
        _, xhat, rstd = _rms(h_ref[...], gg)
        dh, dn = _rms_bwd(dxn, xhat, rstd, gg)
        dh_ref[...] = dhi_ref[...] + dh
        dn_ref[...] += dn

    return pl.pallas_call(
        body, name=name, grid=(T // tm,),
        in_specs=[_row_spec(tm, D), _row_spec(tm, D), _full_spec((1, D)), _full_spec((D, D_IN_PAD))]
        + [_row_spec(tm, n) for n in U_SPLITS],
        out_specs=[_row_spec(tm, D), _row_spec(tm, D_IN_PAD), _full_spec((1, D))],
        out_shape=[jax.ShapeDtypeStruct((T, D), F32), jax.ShapeDtypeStruct((T, D_IN_PAD), BF16),
                   jax.ShapeDtypeStruct((1, D), F32)],
        compiler_params=_cp("arbitrary"),
    )(h, dh_in, g, w_in, *dus)


def _shift_down(x, s, row):
    if s == 0:
        return x
    return jnp.where(row >= s, pltpu.roll(x, s, 0), 0.0)


def _shift_up(x, s, row):
    if s == 0:
        return x
    n = x.shape[0]
    return jnp.where(row < n - s, pltpu.roll(x, n - s, 0), 0.0)


def conv_fwd(x, w, b, S, col0, C, name):
    T = x.shape[0]
    cb0 = col0 // LANE

    def body(x_ref, w_ref, b_ref, y_ref):
        xx = x_ref[...]
        row = _iota(xx.shape, 0)
        y = xx * w_ref[3:4, :] + b_ref[...]
        for k in range(3):
            y += _shift_down(xx, 3 - k, row) * w_ref[k:k + 1, :]
        y_ref[...] = y

    return pl.pallas_call(
        body, name=name, grid=(T // S, C // LANE),
        in_specs=[pl.BlockSpec((S, LANE), lambda s, c: (s, cb0 + c)),
                  pl.BlockSpec((4, LANE), lambda s, c: (0, c)),
                  pl.BlockSpec((1, LANE), lambda s, c: (0, c))],
        out_specs=pl.BlockSpec((S, LANE), lambda s, c: (s, c)),
        out_shape=jax.ShapeDtypeStruct((T, C), F32),
        compiler_params=_cp("parallel", "parallel"),
    )(x, w, b)


def conv_bwd(x, dy, w, S, col0, C, name):
    T = x.shape[0]
    cb0 = col0 // LANE

    def body(x_ref, dy_ref, w_ref, dx_ref, dwb_ref):
        @pl.when(pl.program_id(1) == 0)
        def _():
            dwb_ref[...] = jnp.zeros_like(dwb_ref)

        xx = x_ref[...]
        dd = dy_ref[...]
        row = _iota(xx.shape, 0)
        dx = dd * w_ref[3:4, :]
        for k in range(3):
            dx += _shift_up(dd, 3 - k, row) * w_ref[k:k + 1, :]
        dx_ref[...] = dx
        for k in range(4):
            dwb_ref[k:k + 1, :] += jnp.sum(dd * _shift_down(xx, 3 - k, row), axis=0, keepdims=True)
        dwb_ref[4:5, :] += jnp.sum(dd, axis=0, keepdims=True)

    return pl.pallas_call(
        body, name=name, grid=(C // LANE, T // S),
        in_specs=[pl.BlockSpec((S, LANE), lambda c, s: (s, cb0 + c)),
                  pl.BlockSpec((S, LANE), lambda c, s: (s, c)),
                  pl.BlockSpec((4, LANE), lambda c, s: (0, c))],
        out_specs=[pl.BlockSpec((S, LANE), lambda c, s: (s, c)),
                   pl.BlockSpec((8, LANE), lambda c, s: (0, c))],
        out_shape=[jax.ShapeDtypeStruct((T, C), F32), jax.ShapeDtypeStruct((8, C), F32)],
        compiler_params=_cp("parallel", "arbitrary"),
    )(x, dy, w)


def _scan(a, b, row):
    n = a.shape[0]
    d = 1
    while d < n:
        keep = row >= d
        b = a * jnp.where(keep, pltpu.roll(b, d, 0), 0.0) + b
        a = a * jnp.where(keep, pltpu.roll(a, d, 0), 1.0)
        d *= 2
    return b


def _rscan(a, b, row):
    n = a.shape[0]
    d = 1
    while d < n:
        keep = row < n - d
        b = a * jnp.where(keep, pltpu.roll(b, n - d, 0), 0.0) + b
        a = a * jnp.where(keep, pltpu.roll(a, n - d, 0), 1.0)
        d *= 2
    return b


GELU_C = math.sqrt(2.0 / math.pi)


def _gelu(x):
    t = jnp.tanh(GELU_C * (x + 0.044715 * (x * x * x)))
    return 0.5 * x * (1.0 + t), t


def _lru_gates(xr, wa, ba, wx, bx, lam):
    r = _sigmoid(_mm(xr, wa) + ba)
    i = _sigmoid(_mm(xr, wx) + bx)
    sp = jnp.maximum(-lam, 0.0) + jnp.log(1.0 + jnp.exp(-jnp.abs(lam)))
    la = -LRU_C * r * sp
    a = jnp.exp(la)
    e2 = a * a
    m = jnp.sqrt(-jnp.tanh(la) * (e2 + 1.0))
    return r, i, sp, a, e2, m


def lru_fwd(xr, u_lru, wa, wx, vec, S, name):
    T = xr.shape[0]

    def body(xr_ref, gt_ref, wa_ref, wx_ref, vec_ref, y_ref):
        x = xr_ref[...]
        row = _iota(x.shape, 0)
        r, i, sp, a, e2, m = _lru_gates(x, wa_ref[...], vec_ref[0:1, :], wx_ref[...], vec_ref[1:2, :],
                                        vec_ref[2:3, :])
        hh = _scan(a, m * (i * x), row)
        y_ref[...] = _gelu(gt_ref[...])[0] * hh

    return pl.pallas_call(
        body, name=name, grid=(T // S, LRU_W // LANE),
        in_specs=[pl.BlockSpec((S, LANE), lambda s, c: (s, c)),
                  pl.BlockSpec((S, LANE), lambda s, c: (s, 2 + c)),
                  pl.BlockSpec((LANE, LANE), lambda s, c: (c, c)),
                  pl.BlockSpec((LANE, LANE), lambda s, c: (c, c)),
                  pl.BlockSpec((8, LANE), lambda s, c: (0, c))],
        out_specs=pl.BlockSpec((S, LANE), lambda s, c: (s, c)),
        out_shape=jax.ShapeDtypeStruct((T, LRU_W), F32),
        compiler_params=_cp("parallel", "parallel"),
    )(xr, u_lru, wa, wx, vec)


def lru_bwd(xr, u_lru, dy, wa, wx, vec, S, name):
    T = xr.shape[0]

    def body(xr_ref, gt_ref, dy_ref, wa_ref, wx_ref, vec_ref,
             dxr_ref, dgt_ref, dwa_ref, dwx_ref, dvec_ref):
        @pl.when(pl.program_id(1) == 0)
        def _():
            dwa_ref[...] = jnp.zeros_like(dwa_ref)
            dwx_ref[...] = jnp.zeros_like(dwx_ref)
            dvec_ref[...] = jnp.zeros_like(dvec_ref)

        x = xr_ref[...]
        n = x.shape[0]
        row = _iota(x.shape, 0)
        lam = vec_ref[2:3, :]
        r, i, sp, a, e2, m = _lru_gates(x, wa_ref[...], vec_ref[0:1, :], wx_ref[...], vec_ref[1:2, :], lam)
        v = i * x
        hh = _scan(a, m * v, row)
        gt = gt_ref[...]
        dy = dy_ref[...]
        ge, t = _gelu(gt)
        dgt_ref[...] = dy * hh * (0.5 * (1.0 + t) + 0.5 * gt * (1.0 - t * t) * GELU_C
                                  * (1.0 + 3.0 * 0.044715 * gt * gt))
        a_next = jnp.where(row < n - 1, pltpu.roll(a, n - 1, 0), 0.0)
        G = _rscan(a_next, dy * ge, row)
        da = G * _shift_down(hh, 1, row)
        dv = G * m
        dla = da * a - (G * v) * e2 / m
        dr = dla * (-LRU_C * sp)
        dsp = jnp.sum(dla * (-LRU_C * r), axis=0, keepdims=True)
        dra = dr * r * (1.0 - r)
        dia = (dv * x) * i * (1.0 - i)
        dxr_ref[...] = dv * i + _mm_nt(dra, wa_ref[...]) + _mm_nt(dia, wx_ref[...])
        dwa_ref[0] += _mm_tn(x, dra)
        dwx_ref[0] += _mm_tn(x, dia)
        dvec_ref[0:1, :] += jnp.sum(dra, axis=0, keepdims=True)
        dvec_ref[1:2, :] += jnp.sum(dia, axis=0, keepdims=True)
        dvec_ref[2:3, :] += dsp * (-_sigmoid(-lam))

    return pl.pallas_call(
        body, name=name, grid=(LRU_W // LANE, T // S),
        in_specs=[pl.BlockSpec((S, LANE), lambda c, s: (s, c)),
                  pl.BlockSpec((S, LANE), lambda c, s: (s, 2 + c)),
                  pl.BlockSpec((S, LANE), lambda c, s: (s, c)),
                  pl.BlockSpec((LANE, LANE), lambda c, s: (c, c)),
                  pl.BlockSpec((LANE, LANE), lambda c, s: (c, c)),
                  pl.BlockSpec((8, LANE), lambda c, s: (0, c))],
        out_specs=[pl.BlockSpec((S, LANE), lambda c, s: (s, c)),
                   pl.BlockSpec((S, LANE), lambda c, s: (s, c)),
                   pl.BlockSpec((1, LANE, LANE), lambda c, s: (c, 0, 0)),
                   pl.BlockSpec((1, LANE, LANE), lambda c, s: (c, 0, 0)),
                   pl.BlockSpec((8, LANE), lambda c, s: (0, c))],
        out_shape=[jax.ShapeDtypeStruct((T, LRU_W), F32), jax.ShapeDtypeStruct((T, LRU_W), F32),
                   jax.ShapeDtypeStruct((2, LANE, LANE), F32), jax.ShapeDtypeStruct((2, LANE, LANE), F32),
                   jax.ShapeDtypeStruct((8, LRU_W), F32)],
        compiler_params=_cp("parallel", "arbitrary"),
    )(xr, u_lru, dy, wa, wx, vec)


def _bucket_table():
    qi = np.arange(BQ)[:, None]
    kj = np.arange(2 * BQ)[None, :]
    dist = BQ + qi - kj
    band = (dist >= 0) & (dist < WINDOW)
    dd = np.maximum(dist, 0)
    max_exact = REL_BUCKETS // 2
    large = max_exact + (np.log(np.maximum(dd, 1).astype(np.float32) / np.float32(max_exact))
                         / np.float32(math.log(REL_MAX_DIST / max_exact))
                         * np.float32(REL_BUCKETS - max_exact)).astype(np.int32)
    large = np.minimum(large, REL_BUCKETS - 1)
    bucket = np.where(dd < max_exact, dd, large)
    return np.where(band, bucket, -1).astype(np.int32)


def _att_specs(S):
    nb = S // BQ
    qc = ATT_W // LANE
    return [pl.BlockSpec((BQ, ATT_W), lambda b, n: (b * nb + n, 0)),
            pl.BlockSpec((BQ, KV_W), lambda b, n: (b * nb + jnp.maximum(n - 1, 0), qc)),
            pl.BlockSpec((BQ, KV_W), lambda b, n: (b * nb + n, qc)),
            pl.BlockSpec((BQ, KV_W), lambda b, n: (b * nb + jnp.maximum(n - 1, 0), qc + 1)),
            pl.BlockSpec((BQ, KV_W), lambda b, n: (b * nb + n, qc + 1))]


def _att_bias(bk, rb_ref, bias_s):
    for h in range(ATT_H):
        acc = jnp.zeros(bk.shape, F32)
        for bb in range(REL_BUCKETS):
            acc = jnp.where(bk == bb, rb_ref[bb * ATT_H + h], acc)
        bias_s[h] = acc


def _att_probs(qh, kg, bias, valid, sink):
    s = _mm_nt(qh, kg) * (HD ** -0.5) + bias
    s = jnp.where(valid, s, NEG)
    m = jnp.maximum(jnp.max(s, axis=-1, keepdims=True), sink)
    e = jnp.exp(s - m)
    es = jnp.exp(sink - m)
    den = jnp.sum(e, axis=-1, keepdims=True) + es
    return e / den, es / den


def attn_fwd(u_att, sinks, rel_bias, S, name):
    T = u_att.shape[0]
    nb = S // BQ
    table = jnp.asarray(_bucket_table())

    def body(sk_ref, rb_ref, bk_ref, q_ref, kp_ref, kc_ref, vp_ref, vc_ref, o_ref, bias_s):
        b = pl.program_id(0)
        n = pl.program_id(1)
        bk = bk_ref[...]

        @pl.when((b == 0) & (n == 0))
        def _():
            _att_bias(bk, rb_ref, bias_s)

        valid = (bk >= 0) & ((n > 0) | (_iota(bk.shape, 1) >= BQ))
        for h in range(ATT_H):
            gs = slice(HD * (h // ATT_G), HD * (h // ATT_G + 1))
            kg = jnp.concatenate([kp_ref[:, gs], kc_ref[:, gs]], axis=0)
            vg = jnp.concatenate([vp_ref[:, gs], vc_ref[:, gs]], axis=0)
            p, _ = _att_probs(q_ref[:, HD * h:HD * (h + 1)], kg, bias_s[h], valid, sk_ref[h])
            o_ref[:, HD * h:HD * (h + 1)] = _mm(p, vg)

    smem = pl.BlockSpec(memory_space=pltpu.SMEM)
    return pl.pallas_call(
        body, name=name, grid=(T // S, nb),
        in_specs=[smem, smem, _full_spec((BQ, 2 * BQ))] + _att_specs(S),
        out_specs=pl.BlockSpec((BQ, ATT_W), lambda b, n: (b * nb + n, 0)),
        out_shape=jax.ShapeDtypeStruct((T, ATT_W), F32),
        scratch_shapes=[pltpu.VMEM((ATT_H, BQ, 2 * BQ), F32)],
        compiler_params=_cp("arbitrary", "arbitrary"),
    )(sinks, rel_bias, table, u_att, u_att, u_att, u_att, u_att)


def attn_bwd(u_att, dy, sinks, rel_bias, S, name):
    T = u_att.shape[0]
    nb = S // BQ
    nB = T // S
    table = jnp.asarray(_bucket_table())
    scale = HD ** -0.5

    def body(sk_ref, rb_ref, bk_ref, q_ref, kp_ref, kc_ref, vp_ref, vc_ref, dy_ref,
             du_ref, drel_ref, dsk_ref, bias_s, dbias_s):
        b = pl.program_id(0)
        n = pl.program_id(1)
        bk = bk_ref[...]

        @pl.when((b == 0) & (n == 0))
        def _():
            _att_bias(bk, rb_ref, bias_s)
            dbias_s[...] = jnp.zeros_like(dbias_s)
            dsk_ref[...] = jnp.zeros_like(dsk_ref)
            drel_ref[...] = jnp.zeros_like(drel_ref)

        @pl.when(n == 0)
        def _():
            du_ref[...] = jnp.zeros_like(du_ref)

        valid = (bk >= 0) & ((n > 0) | (_iota(bk.shape, 1) >= BQ))
        r_cur = pl.multiple_of(n * BQ, BQ)
        r_prev = pl.multiple_of(jnp.maximum(n - 1, 0) * BQ, BQ)
        for g in range(ATT_KV):
            gs = slice(HD * g, HD * (g + 1))
            kg = jnp.concatenate([kp_ref[:, gs], kc_ref[:, gs]], axis=0)
            vg = jnp.concatenate([vp_ref[:, gs], vc_ref[:, gs]], axis=0)
            dk = jnp.zeros((2 * BQ, HD), F32)
            dv = jnp.zeros((2 * BQ, HD), F32)
            for e in range(ATT_G):
                h = g * ATT_G + e
                qh = q_ref[:, HD * h:HD * (h + 1)]
                do = dy_ref[:, HD * h:HD * (h + 1)]
                p, ps = _att_probs(qh, kg, bias_s[h], valid, sk_ref[h])
                dp = _mm_nt(do, vg)
                delta = jnp.sum(p * dp, axis=-1, keepdims=True)
                ds = p * (dp - delta)
                dbias_s[h] += ds
                dsk_ref[h:h + 1, :] += jnp.broadcast_to(
                    jnp.sum(-ps * delta, axis=0, keepdims=True), (1, LANE))
                dss = ds * scale
                du_ref[pl.ds(r_cur, BQ), HD * h:HD * (h + 1)] = _mm(dss, kg)
                dk += _mm_tn(dss, qh)
                dv += _mm_tn(p, do)
            ck = ATT_W + HD * g
            cv = ATT_W + KV_W + HD * g
            du_ref[pl.ds(r_prev, BQ), ck:ck + HD] += dk[0:BQ]
            du_ref[pl.ds(r_cur, BQ), ck:ck + HD] += dk[BQ:]
            du_ref[pl.ds(r_prev, BQ), cv:cv + HD] += dv[0:BQ]
            du_ref[pl.ds(r_cur, BQ), cv:cv + HD] += dv[BQ:]

        @pl.when((b == nB - 1) & (n == nb - 1))
        def _():
            lane = _iota((1, LANE), 1)
            for h in range(ATT_H):
                db = dbias_s[h]
                acc = jnp.zeros((1, LANE), F32)
                for bb in range(REL_BUCKETS):
                    val = jnp.sum(jnp.sum(jnp.where(bk == bb, db, 0.0), axis=1, keepdims=True),
                                  axis=0, keepdims=True)
                    acc = jnp.where(lane == bb, val, acc)
                drel_ref[h:h + 1, :] = acc

    smem = pl.BlockSpec(memory_space=pltpu.SMEM)
    return pl.pallas_call(
        body, name=name, grid=(nB, nb),
        in_specs=[smem, smem, _full_spec((BQ, 2 * BQ))] + _att_specs(S)
        + [pl.BlockSpec((BQ, ATT_W), lambda b, n: (b * nb + n, 0))],
        out_specs=[pl.BlockSpec((S, ATT_W + 2 * KV_W), lambda b, n: (b, 0)),
                   _full_spec((8, LANE)), _full_spec((8, LANE))],
        out_shape=[jax.ShapeDtypeStruct((T, ATT_W + 2 * KV_W), F32),
                   jax.ShapeDtypeStruct((8, LANE), F32), jax.ShapeDtypeStruct((8, LANE), F32)],
        scratch_shapes=[pltpu.VMEM((ATT_H, BQ, 2 * BQ), F32), pltpu.VMEM((ATT_H, BQ, 2 * BQ), F32)],
        compiler_params=_cp("arbitrary", "arbitrary"),
    )(sinks, rel_bias, table, u_att, u_att, u_att, u_att, u_att, dy)


def _head_of(i):
    return lax.shift_right_logical(i, 6)


def _head_mask(shape):
    return (_head_of(_iota(shape, 0)) == _head_of(_iota(shape, 1))).astype(F32)


def _dn_point(c, uba, alog, dtb):
    s = c * _sigmoid(c)
    qt, kt, vt = s[:, 0:256], s[:, 256:512], s[:, 512:768]
    ones_bd = _head_mask((DN_W, DN_W))
    q = qt * lax.rsqrt(_mmx(qt * qt, ones_bd) + EPS) * (HD ** -0.5)
    k = kt * lax.rsqrt(_mmx(kt * kt, ones_bd) + EPS)
    sel = _head_of(_iota((LANE, DN_W), 1))
    row = _iota((LANE, DN_W), 0)
    braw = _mmx(uba, (row == sel).astype(F32))
    araw = _mmx(uba, (row == sel + DN_H).astype(F32)) + dtb
    beta = _sigmoid(braw)
    g = -jnp.exp(alog) * (jnp.maximum(araw, 0.0) + jnp.log(1.0 + jnp.exp(-jnp.abs(araw))))
    return q, k, vt, g, beta


def dn_point_fwd(c, uba, alog, dtb, name):
    T = c.shape[0]
    tm = min(TOK_TILE, T)

    def body(c_ref, u_ref, al_ref, dt_ref, *outs):
        for ref, val in zip(outs, _dn_point(c_ref[...], u_ref[...], al_ref[...], dt_ref[...])):
            ref[...] = val

    return pl.pallas_call(
        body, name=name, grid=(T // tm,),
        in_specs=[_row_spec(tm, 768), _row_spec(tm, LANE), _full_spec((1, DN_W)), _full_spec((1, DN_W))],
        out_specs=[_row_spec(tm, DN_W)] * 5,
        out_shape=[jax.ShapeDtypeStruct((T, DN_W), F32)] * 5,
        compiler_params=_cp("parallel"),
    )(c, uba, alog, dtb)


def dn_point_bwd(c, uba, alog, dtb, douts, name):
    T = c.shape[0]
    tm = min(TOK_TILE, T)

    def body(c_ref, u_ref, al_ref, dt_ref, dq, dk, dv, dg, db, dc_ref, du_ref, dvec_ref):
        @pl.when(pl.program_id(0) == 0)
        def _():
            dvec_ref[...] = jnp.zeros_like(dvec_ref)

        _, vjp = jax.vjp(_dn_point, c_ref[...], u_ref[...], al_ref[...], dt_ref[...])
        dc, du, dal, ddt = vjp((dq[...], dk[...], dv[...], dg[...], db[...]))
        dc_ref[...] = dc
        du_ref[...] = du
        fold = (_iota((LANE, DN_W), 0) == _head_of(_iota((LANE, DN_W), 1))).astype(F32)
        both = jnp.concatenate([dal, ddt, jnp.zeros((6, DN_W), F32)], axis=0)
        dvec_ref[...] += _mmx_nt(both, fold)

    return pl.pallas_call(
        body, name=name, grid=(T // tm,),
        in_specs=[_row_spec(tm, 768), _row_spec(tm, LANE), _full_spec((1, DN_W)), _full_spec((1, DN_W))]
        + [_row_spec(tm, DN_W)] * 5,
        out_specs=[_row_spec(tm, 768), _row_spec(tm, LANE), _full_spec((8, LANE))],
        out_shape=[jax.ShapeDtypeStruct((T, 768), F32), jax.ShapeDtypeStruct((T, LANE), F32),
                   jax.ShapeDtypeStruct((8, LANE), F32)],
        compiler_params=_cp("arbitrary"),
    )(c, uba, alog, dtb, *douts)


def _unit_lower_inverse(lmat):
    eye = (_iota(lmat.shape, 0) == _iota(lmat.shape, 1)).astype(F32)
    tinv = eye - lmat
    pw = lmat
    for _ in range(5):
        pw = _mm3(pw, pw)
        tinv = tinv + _mm3(tinv, pw)
    return tinv


def _inverse_bwd(tinv, d):
    return -_mm3_nt(_mm3_tn(tinv, d), tinv)


@jax.custom_vjp
def _tri_inv(lmat):
    return _unit_lower_inverse(lmat)


def _tri_inv_fwd(lmat):
    tinv = _unit_lower_inverse(lmat)
    return tinv, tinv


_tri_inv.defvjp(_tri_inv_fwd, lambda tinv, d: (_inverse_bwd(tinv, d),))


@jax.custom_vjp
def _tri_inv_known(lmat, tinv):
    return tinv


_tri_inv_known.defvjp(lambda lmat, tinv: (tinv, tinv),
                      lambda tinv, d: (_inverse_bwd(tinv, d), jnp.zeros_like(tinv)))


DN_SUB = 2


def _dn_prep(q, k, v, g, beta, known=None):
    hm = _head_mask((DN_W, DN_W))
    ri = _iota((DN_W, DN_W), 0) & (CHUNK - 1)
    ci = _iota((DN_W, DN_W), 1) & (CHUNK - 1)
    tril = hm * (ri >= ci).astype(F32)
    strict = hm * (ri > ci).astype(F32)
    tri64 = (_iota((CHUNK, CHUNK), 0) >= _iota((CHUNK, CHUNK), 1)).astype(F32)

    def stack(x):
        return jnp.concatenate([x, x, x, x], axis=0) * hm

    gc = _mm3(tri64, g)
    glast = jnp.sum(g, axis=0, keepdims=True)
    eg = jnp.exp(gc)
    kb = k * beta
    qs, ks = stack(q), stack(k)
    gcol = jnp.sum(stack(gc), axis=1, keepdims=True) * (1.0 / HD)
    gmat = jnp.broadcast_to(gcol, (DN_W, DN_W))
    decay = jnp.exp(jnp.minimum(gmat - gmat.T, 0.0))
    lmat = _mm_nt(stack(kb), ks) * decay * strict
    tinv = _tri_inv(lmat) if known is None else _tri_inv_known(lmat, known)
    u = _mm(tinv, stack(v * beta))
    w = _mm(tinv, stack(kb * eg))
    att = _mm_nt(qs, ks) * decay * tril
    return u, w, att, stack(q * eg), stack(k * jnp.exp(glast - gc)), jnp.exp(glast), tinv


def _dn_apply(state, prep):
    u, w, att, qe, kd, eglast, _ = prep
    vn = u - _mm(w, state)
    o4 = _mm(qe, state) + _mm(att, vn)
    o = o4[0:64] + o4[64:128] + o4[128:192] + o4[192:256]
    return o, state * eglast + _mm_tn(kd, vn)


def _dn_chunks(state, q, k, v, g, beta, knowns=None):
    n = q.shape[0] // CHUNK
    rows = lambda x, c: x[c * CHUNK:(c + 1) * CHUNK]
    preps = [_dn_prep(*(rows(x, c) for x in (q, k, v, g, beta)),
                      known=None if knowns is None else knowns[c]) for c in range(n)]
    outs = []
    for prep in preps:
        o, state = _dn_apply(state, prep)
        outs.append(o)
    return jnp.concatenate(outs, axis=0), state, [prep[-1] for prep in preps]


def dn_scan_fwd(q, k, v, g, beta, S, name):
    T = q.shape[0]
    rows = DN_SUB * CHUNK
    ns = S // rows

    def body(q_ref, k_ref, v_ref, g_ref, b_ref, o_ref, st_ref, ti_ref, s_s):
        @pl.when(pl.program_id(1) == 0)
        def _():
            s_s[...] = jnp.zeros_like(s_s)

        st = s_s[...]
        st_ref[0] = st
        o, new, tinvs = _dn_chunks(st, q_ref[...], k_ref[...], v_ref[...], g_ref[...], b_ref[...])
        o_ref[...] = o
        for c, tinv in enumerate(tinvs):
            ti_ref[c] = tinv
        s_s[...] = new

    spec = pl.BlockSpec((rows, DN_W), lambda b, t: (b * ns + t, 0))
    return pl.pallas_call(
        body, name=name, grid=(T // S, ns),
        in_specs=[spec] * 5,
        out_specs=[spec, pl.BlockSpec((1, DN_W, DN_W), lambda b, t: (b * ns + t, 0, 0)),
                   pl.BlockSpec((DN_SUB, DN_W, DN_W), lambda b, t: (b * ns + t, 0, 0))],
        out_shape=[jax.ShapeDtypeStruct((T, DN_W), F32),
                   jax.ShapeDtypeStruct((T // rows, DN_W, DN_W), F32),
                   jax.ShapeDtypeStruct((T // CHUNK, DN_W, DN_W), F32)],
        scratch_shapes=[pltpu.VMEM((DN_W, DN_W), F32)],
        compiler_params=_cp("parallel", "arbitrary"),
    )(q, k, v, g, beta)


def dn_scan_bwd(q, k, v, g, beta, states, tinvs, do, S, name):
    T = q.shape[0]
    rows = DN_SUB * CHUNK
    ns = S // rows

    def body(q_ref, k_ref, v_ref, g_ref, b_ref, st_ref, ti_ref, do_ref, dq, dk, dv, dg, db, ds_s):
        @pl.when(pl.program_id(1) == 0)
        def _():
            ds_s[...] = jnp.zeros_like(ds_s)

        knowns = [ti_ref[c] for c in range(DN_SUB)]
        _, vjp = jax.vjp(lambda *args: _dn_chunks(*args, knowns=knowns)[:2],
                         st_ref[0], q_ref[...], k_ref[...], v_ref[...], g_ref[...], b_ref[...])
        grads = vjp((do_ref[...], ds_s[...]))
        ds_s[...] = grads[0]
        for ref, val in zip((dq, dk, dv, dg, db), grads[1:]):
            ref[...] = val

    spec = pl.BlockSpec((rows, DN_W), lambda b, t: (b * ns + ns - 1 - t, 0))
    return pl.pallas_call(
        body, name=name, grid=(T // S, ns),
        in_specs=[spec] * 5 + [pl.BlockSpec((1, DN_W, DN_W), lambda b, t: (b * ns + ns - 1 - t, 0, 0)),
                               pl.BlockSpec((DN_SUB, DN_W, DN_W), lambda b, t: (b * ns + ns - 1 - t, 0, 0)),
                               spec],
        out_specs=[spec] * 5,
        out_shape=[jax.ShapeDtypeStruct((T, DN_W), F32)] * 5,
        scratch_shapes=[pltpu.VMEM((DN_W, DN_W), F32)],
        compiler_params=_cp("parallel", "arbitrary"),
    )(q, k, v, g, beta, states, tinvs, do)


def _dn_gate(o, z, nl):
    ms = _mmx(o * o, _head_mask((DN_W, DN_W))) * (1.0 / HD)
    return o * lax.rsqrt(ms + EPS) * nl * (z * _sigmoid(z))


def dn_gate_fwd(o, u_dn, nl, name):
    T = o.shape[0]
    tm = min(TOK_TILE, T)

    def body(o_ref, z_ref, n_ref, y_ref):
        y_ref[...] = _dn_gate(o_ref[...], z_ref[...], n_ref[...])

    return pl.pallas_call(
        body, name=name, grid=(T // tm,),
        in_specs=[_row_spec(tm, DN_W), pl.BlockSpec((tm, DN_W), lambda i: (i, 3)), _full_spec((1, DN_W))],
        out_specs=_row_spec(tm, DN_W),
        out_shape=jax.ShapeDtypeStruct((T, DN_W), F32),
        compiler_params=_cp("parallel"),
    )(o, u_dn, nl)


def dn_gate_bwd(o, u_dn, nl, dy, name):
    T = o.shape[0]
    tm = min(TOK_TILE, T)

    def body(o_ref, z_ref, n_ref, dy_ref, do_ref, dz_ref, dn_ref):
        @pl.when(pl.program_id(0) == 0)
        def _():
            dn_ref[...] = jnp.zeros_like(dn_ref)

        _, vjp = jax.vjp(_dn_gate, o_ref[...], z_ref[...], n_ref[...])
        do, dz, dn = vjp(dy_ref[...])
        do_ref[...] = do
        dz_ref[...] = dz
        fold = (_iota((LANE, DN_W), 0) == (_iota((LANE, DN_W), 1) & (HD - 1))).astype(F32)
        dn_ref[...] += _mmx_nt(jnp.concatenate([dn, jnp.zeros((7, DN_W), F32)], axis=0), fold)

    return pl.pallas_call(
        body, name=name, grid=(T // tm,),
        in_specs=[_row_spec(tm, DN_W), pl.BlockSpec((tm, DN_W), lambda i: (i, 3)), _full_spec((1, DN_W)),
                  _row_spec(tm, DN_W)],
        out_specs=[_row_spec(tm, DN_W), _row_spec(tm, DN_W), _full_spec((8, LANE))],
        out_shape=[jax.ShapeDtypeStruct((T, DN_W), F32), jax.ShapeDtypeStruct((T, DN_W), F32),
                   jax.ShapeDtypeStruct((8, LANE), F32)],
        compiler_params=_cp("arbitrary"),
    )(o, u_dn, nl, dy)


Y_SPLITS = (LRU_W, ATT_W, DN_W)
Y_OFFS = (0, LRU_W, LRU_W + ATT_W)


ROWS_DEV = D // N_DEV


def _dev_rows_spec(row0):
    return pl.BlockSpec((N_DEV, ROWS_DEV, D), lambda *_: (0, row0 // ROWS_DEV, 0))


def _dev_rows(w_ref, off, n):
    return w_ref[off // ROWS_DEV:(off + n) // ROWS_DEV].reshape(n, D)


def wout_fwd(h, ys, wb, name):
    T = h.shape[0]
    tm = min(TOK_TILE, T)

    def body(h_ref, y0, y1, y2, w_ref, o_ref, yc_ref):
        acc = h_ref[...]
        for ref, off, n in zip((y0, y1, y2), Y_OFFS, Y_SPLITS):
            y = ref[...].astype(BF16)
            yc_ref[:, off:off + n] = y
            acc += _mm(y, _dev_rows(w_ref, off, n))
        o_ref[...] = acc

    return pl.pallas_call(
        body, name=name, grid=(T // tm,),
        in_specs=[_row_spec(tm, D)] + [_row_spec(tm, n) for n in Y_SPLITS] + [_dev_rows_spec(B_WOUT)],
        out_specs=[_row_spec(tm, D), _row_spec(tm, D)],
        out_shape=[jax.ShapeDtypeStruct((T, D), F32), jax.ShapeDtypeStruct((T, D), BF16)],
        compiler_params=_cp("parallel"),
    )(h, *ys, wb)


def wout_bwd(dy, wb, name):
    T = dy.shape[0]
    tm = min(TOK_TILE, T)

    def body(dy_ref, w_ref, d0, d1, d2):
        dd = dy_ref[...].astype(BF16)
        for ref, off, n in zip((d0, d1, d2), Y_OFFS, Y_SPLITS):
            ref[...] = _mm_nt(dd, _dev_rows(w_ref, off, n))

    return pl.pallas_call(
        body, name=name, grid=(T // tm,),
        in_specs=[_row_spec(tm, D), _dev_rows_spec(B_WOUT)],
        out_specs=[_row_spec(tm, n) for n in Y_SPLITS],
        out_shape=[jax.ShapeDtypeStruct((T, n), F32) for n in Y_SPLITS],
        compiler_params=_cp("parallel"),
    )(dy, wb)


def ple_fwd(h, g, pe, wg, wp, name):
    T = h.shape[0]
    tm = min(TOK_TILE, T)

    def body(h_ref, g_ref, p_ref, wg_ref, wp_ref, o_ref):
        hh = h_ref[...]
        xn = _rms(hh, g_ref[...])[0]
        o_ref[...] = hh + _sigmoid(_mm(xn, _dev_rows(wg_ref, 0, D))) * _mm(p_ref[...], wp_ref[...])

    return pl.pallas_call(
        body, name=name, grid=(T // tm,),
        in_specs=[_row_spec(tm, D), _full_spec((1, D)), _row_spec(tm, PLE), _dev_rows_spec(B_PGATE),
                  _full_spec((PLE, D))],
        out_specs=_row_spec(tm, D),
        out_shape=jax.ShapeDtypeStruct((T, D), F32),
        compiler_params=_cp("parallel"),
    )(h, g, pe, wg, wp)


def ple_bwd(h, dy, g, pe, wg, wp, name):
    T = h.shape[0]
    tm = min(TOK_TILE, T)

    def body(h_ref, dy_ref, g_ref, p_ref, wg_ref, wp_ref, dh_ref, dz_ref, dpp_ref, xn_ref, dn_ref):
        @pl.when(pl.program_id(0) == 0)
        def _():
            dn_ref[...] = jnp.zeros_like(dn_ref)

        gg = g_ref[...]
        dy = dy_ref[...]
        xn, xhat, rstd = _rms(h_ref[...], gg)
        wg = _dev_rows(wg_ref, 0, D)
        gate = _sigmoid(_mm(xn, wg))
        pp = _mm(p_ref[...], wp_ref[...])
        dz = dy * pp * gate * (1.0 - gate)
        dz_ref[...] = dz.astype(BF16)
        dpp_ref[...] = (dy * gate).astype(BF16)
        xn_ref[...] = xn.astype(BF16)
        dh, dn = _rms_bwd(_mm_nt(dz, wg), xhat, rstd, gg)
        dh_ref[...] = dy + dh
        dn_ref[...] += dn

    return pl.pallas_call(
        body, name=name, grid=(T // tm,),
        in_specs=[_row_spec(tm, D), _row_spec(tm, D), _full_spec((1, D)), _row_spec(tm, PLE),
                  _dev_rows_spec(B_PGATE), _full_spec((PLE, D))],
        out_specs=[_row_spec(tm, D), _row_spec(tm, D), _row_spec(tm, D), _row_spec(tm, D), _full_spec((1, D))],
        out_shape=[jax.ShapeDtypeStruct((T, D), F32), jax.ShapeDtypeStruct((T, D), BF16),
                   jax.ShapeDtypeStruct((T, D), BF16), jax.ShapeDtypeStruct((T, D), BF16),
                   jax.ShapeDtypeStruct((1, D), F32)],
        compiler_params=_cp("arbitrary"),
    )(h, dy, g, pe, wg, wp)


def loss_head(h, g, target, name):
    T = h.shape[0]
    tm = min(TOK_TILE, T)

    def body(h_ref, g_ref, t_ref, loss_ref, dh_ref, dn_ref):
        @pl.when(pl.program_id(0) == 0)
        def _():
            dn_ref[...] = jnp.zeros_like(dn_ref)
            loss_ref[...] = jnp.zeros_like(loss_ref)

        gg = g_ref[...]
        y, xhat, rstd = _rms(h_ref[...], gg)
        err = y - t_ref[...]
        per_tok = jnp.mean(err * err, axis=-1, keepdims=True)
        loss_ref[...] += 0.5 * jnp.sum(per_tok, axis=0, keepdims=True)
        dh, dn = _rms_bwd(err * (1.0 / D), xhat, rstd, gg)
        dh_ref[...] = dh
        dn_ref[...] += dn

    return pl.pallas_call(
        body, name=name, grid=(T // tm,),
        in_specs=[_row_spec(tm, D), _full_spec((1, D)), _row_spec(tm, D)],
        out_specs=[_full_spec((8, LANE)), _row_spec(tm, D), _full_spec((1, D))],
        out_shape=[jax.ShapeDtypeStruct((8, LANE), F32), jax.ShapeDtypeStruct((T, D), F32),
                   jax.ShapeDtypeStruct((1, D), F32)],
        compiler_params=_cp("arbitrary"),
    )(h, g, target)


def _block_diag(w):
    return jnp.einsum('hij,hk->hikj', w, jnp.eye(4, dtype=w.dtype)).reshape(LRU_W, LRU_W)


def _layer_consts(W, l):
    row = lambda v: v.reshape(1, -1)
    zeros = jnp.zeros((5, LRU_W), F32)
    return dict(
        wa=_block_diag(W["lru_w_a"][l]), wx=_block_diag(W["lru_w_x"][l]),
        lru_vec=jnp.concatenate([row(W["lru_b_a"][l]), row(W["lru_b_x"][l]), row(W["lru_lambda"][l]), zeros], 0),
        lru_cb=row(W["lru_conv_b"][l]),
        sinks=W["attn_sinks"][l], rel=W["rel_bias"].reshape(-1),
        dn_cb=jnp.zeros((1, 3 * DN_W), F32),
        alog=row(jnp.repeat(W["dn_a_log"][l], HD)), dtb=row(jnp.repeat(W["dn_dt_bias"][l], HD)),
        dn_nl=row(jnp.tile(W["dn_norm"][l], DN_H)),
    )


def _layer_fwd(h0, pe, W, l, S):
    n = f"l{l}_"
    c_ = _layer_consts(W, l)
    row = lambda v: v.reshape(1, -1)
    wa, wb = W["wa"][l], W["wb"][l]
    h1 = ffn_fwd(h0, row(W["ffn1_norm"][l]), wa, wb, 0, n + "ffn1_fwd")
    u_lru, u_att, u_dn, u_ba, xn_mix = mixin_fwd(h1, row(W["mix_norm"][l]), W["w_in"][l], n + "mixin_fwd")
    xr = conv_fwd(u_lru, W["lru_conv_w"][l], c_["lru_cb"], S, 0, LRU_W, n + "lru_conv_fwd")
    y_lru = lru_fwd(xr, u_lru, c_["wa"], c_["wx"], c_["lru_vec"], S, n + "lru_fwd")
    y_att = attn_fwd(u_att, c_["sinks"], c_["rel"], S, n + "attn_fwd")
    cc = conv_fwd(u_dn, W["dn_conv_w"][l], c_["dn_cb"], S, 0, 3 * DN_W, n + "dn_conv_fwd")
    q, k, v, g, beta = dn_point_fwd(cc, u_ba, c_["alog"], c_["dtb"], n + "dn_point_fwd")
    o, states, tinvs = dn_scan_fwd(q, k, v, g, beta, S, n + "dn_scan_fwd")
    y_dn = dn_gate_fwd(o, u_dn, c_["dn_nl"], n + "dn_gate_fwd")
    h2, ycat = wout_fwd(h1, (y_lru, y_att, y_dn), wb, n + "wout_fwd")
    h3 = ffn_fwd(h2, row(W["ffn2_norm"][l]), wa, wb, 1, n + "ffn2_fwd")
    h4 = ple_fwd(h3, row(W["ple_norm"][l]), pe, wb, W["ple_w_proj"][l], n + "ple_fwd")
    saved = dict(h0=h0, h1=h1, h2=h2, h3=h3, u_lru=u_lru, u_att=u_att, u_dn=u_dn, u_ba=u_ba, xn_mix=xn_mix,
                 xr=xr, cc=cc, q=q, k=k, v=v, g=g, beta=beta, o=o, states=states, tinvs=tinvs, ycat=ycat)
    return h4, saved


def _layer_bwd(dh4, sv, pe, W, l, S):
    n = f"l{l}_"
    c_ = _layer_consts(W, l)
    row = lambda v: v.reshape(1, -1)
    wa, wb = W["wa"][l], W["wb"][l]
    G = {"ga": jnp.zeros((4, D, FFP), BF16), "gb": jnp.zeros((N_DEV, B_ROWS, D), BF16)}
    dh3, dz, dpp, xn_p, dn = ple_bwd(sv["h3"], dh4, row(W["ple_norm"][l]), pe, wb, W["ple_w_proj"][l],
                                     n + "ple_bwd")
    G["ple_norm"] = dn[0]
    G["gb"] = grad_rows(xn_p, dz, G["gb"], B_PGATE, ROWS_DEV, n + "d_ple_w_gate")
    d_proj = matmul_tn(pe, dpp, n + "d_ple_w_proj")
    d_proj = d_proj.reshape(PLE, N_DEV, D // N_DEV).transpose(1, 0, 2).reshape(N_DEV, B_ROWS - B_PPROJ, D)
    G["gb"] = lax.dynamic_update_slice(G["gb"], d_proj, (0, B_PPROJ, 0))

    def ffn_back(which, fidx, h_in, dy):
        dh, dgt, dup, act, xn, dn_ = ffn_bwd(h_in, dy, row(W[which + "_norm"][l]), wa, wb, fidx,
                                             n + which + "_bwd")
        G[which + "_norm"] = dn_[0]
        G["ga"] = grad_cols(xn, dgt, G["ga"], 2 * fidx, n + "d_" + which + "_w_gate")
        G["ga"] = grad_cols(xn, dup, G["ga"], 2 * fidx + 1, n + "d_" + which + "_w_up")
        G["gb"] = grad_rows(act, dy, G["gb"], (B_DOWN1, B_DOWN2)[fidx], SHP, n + "d_" + which + "_w_down",
                            scale=0.5)
        return dh

    dh2 = ffn_back("ffn2", 1, sv["h2"], dh3)
    dy_lru, dy_att, dy_dn = wout_bwd(dh2, wb, n + "wout_bwd")
    G["gb"] = grad_rows(sv["ycat"], dh2, G["gb"], B_WOUT, ROWS_DEV, n + "d_w_out")
    do, dz_dn, dnn = dn_gate_bwd(sv["o"], sv["u_dn"], c_["dn_nl"], dy_dn, n + "dn_gate_bwd")
    dqkvgb = dn_scan_bwd(sv["q"], sv["k"], sv["v"], sv["g"], sv["beta"], sv["states"], sv["tinvs"], do, S,
                         n + "dn_scan_bwd")
    dcc, du_ba, dvec_dn = dn_point_bwd(sv["cc"], sv["u_ba"], c_["alog"], c_["dtb"], dqkvgb, n + "dn_point_bwd")
    dqkv, dwb_dn = conv_bwd(sv["u_dn"], dcc, W["dn_conv_w"][l], S, 0, 3 * DN_W, n + "dn_conv_bwd")
    du_dn = jnp.concatenate([dqkv, dz_dn], axis=1)
    G["dn_norm"] = dnn[0, 0:HD]
    G["dn_a_log"] = dvec_dn[0, 0:DN_H]
    G["dn_dt_bias"] = dvec_dn[1, 0:DN_H]
    G["dn_conv_w"] = dwb_dn[0:4]
    du_att, drel, dsk = attn_bwd(sv["u_att"], dy_att, c_["sinks"], c_["rel"], S, n + "attn_bwd")
    G["attn_sinks"] = dsk[:, 0]
    G["rel_bias"] = drel[:, 0:REL_BUCKETS].T
    dxr, dgt_lru, dwa, dwx, dvec = lru_bwd(sv["xr"], sv["u_lru"], dy_lru, c_["wa"], c_["wx"], c_["lru_vec"], S,
                                           n + "lru_bwd")
    dx_lru, dwb_lru = conv_bwd(sv["u_lru"], dxr, W["lru_conv_w"][l], S, 0, LRU_W, n + "lru_conv_bwd")
    du_lru = jnp.concatenate([dx_lru, dgt_lru], axis=1)
    diag = lambda m: jnp.stack([m[c, HD * e:HD * (e + 1), HD * e:HD * (e + 1)] for c in range(2) for e in range(2)])
    G["lru_w_a"], G["lru_w_x"] = diag(dwa), diag(dwx)
    G["lru_b_a"], G["lru_b_x"], G["lru_lambda"] = dvec[0], dvec[1], dvec[2]
    G["lru_conv_w"], G["lru_conv_b"] = dwb_lru[0:4], dwb_lru[4]
    dh1, du_cat, dn = mixin_bwd(sv["h1"], dh2, row(W["mix_norm"][l]), W["w_in"][l],
                                (du_lru, du_att, du_dn, du_ba), n + "mixin_bwd")
    G["mix_norm"] = dn[0]
    d_in = matmul_tn(sv["xn_mix"], du_cat, n + "d_w_in")[:, :D_IN]
    d_in = d_in.reshape(D, N_DEV, D_IN // N_DEV).transpose(1, 0, 2).reshape(N_DEV, WIN_ROWS, D)
    d_in = jnp.pad(d_in, ((0, 0), (0, B_PPROJ - B_WIN - WIN_ROWS), (0, 0)))
    G["gb"] = lax.dynamic_update_slice(G["gb"], d_in, (0, B_WIN, 0))
    dh0 = ffn_back("ffn1", 0, sv["h0"], dh1)
    return dh0, G


def _core(x, pe, W, target, S):
    h = x
    saved = []
    for l in range(DEPTH):
        h, sv = _layer_fwd(h, pe[l], W, l, S)
        saved.append(sv)
    loss_tile, dh, dfn = loss_head(h, W["final_norm"].reshape(1, -1), target, "loss_head")
    grads = [None] * DEPTH
    for l in reversed(range(DEPTH)):
        dh, grads[l] = _layer_bwd(dh, saved[l], pe[l], W, l, S)
    return loss_tile[0, 0], dh, grads, dfn[0]


MESH_ID = pl.DeviceIdType.MESH
ANY_SPEC = pl.BlockSpec(memory_space=pl.ANY)
AXES = ("x", "y", "c")


def _my_pos():
    return lax.axis_index("x"), lax.axis_index("y"), lax.axis_index("c")


def _slot_of(px, py, pc):
    return 4 * px + 2 * py + pc


def all_gather(x, name):
    R, C = x.shape

    def body(x_ref, out_ref, send_sems, recv_sems, local_sem):
        mx, my, mc = _my_pos()
        me, sibling = (mx, my, mc), (mx, my, 1 - mc)
        chips = [(1 - mx, my), (mx, 1 - my), (1 - mx, 1 - my)]

        def copy(k, block, to, src=None):
            dst = out_ref.at[_slot_of(*block)]
            return pltpu.make_async_remote_copy(
                src_ref=dst if src is None else src, dst_ref=dst,
                send_sem=send_sems.at[k], recv_sem=recv_sems.at[k],
                device_id=to, device_id_type=MESH_ID)

        mine = pltpu.make_async_copy(x_ref, out_ref.at[_slot_of(*me)], local_sem)
        mine.start()
        first = [copy(0, me, sibling, src=x_ref)]
        first += [copy(1 + j, me, (*chip, mc), src=x_ref) for j, chip in enumerate(chips)]
        for cp in first:
            cp.start()
        passed = [copy(4 + j, (*chip, mc), sibling) for j, chip in enumerate(chips)]
        for j, chip in enumerate(chips):
            copy(1 + j, (*chip, mc), me).wait_recv()
            passed[j].start()
        copy(0, sibling, me).wait_recv()
        for j, chip in enumerate(chips):
            copy(4 + j, (*chip, 1 - mc), me).wait_recv()
        for cp in first + passed:
            cp.wait_send()
        mine.wait()

    return pl.pallas_call(
        body, name=name,
        out_shape=jax.ShapeDtypeStruct((N_DEV, R, C), x.dtype),
        in_specs=[ANY_SPEC], out_specs=ANY_SPEC,
        scratch_shapes=[pltpu.SemaphoreType.DMA((7,)), pltpu.SemaphoreType.DMA((7,)), pltpu.SemaphoreType.DMA],
    )(x)


def _col_window(ref, slot):
    return ref.at[:, pl.ds(pl.multiple_of(slot * SHP, LANE), SHP)]


def gather_layer(a_sh, b_sh, name):
    def body(a_ref, b_ref, ao_ref, bo_ref, send_sems, recv_sems, local_sems):
        mx, my, mc = _my_pos()
        me, sibling = (mx, my, mc), (mx, my, 1 - mc)
        chips = [(1 - mx, my), (mx, 1 - my), (1 - mx, 1 - my)]

        def copies(k, block, to, own=False):
            slot = _slot_of(*block)
            dsts = (_col_window(ao_ref, slot), bo_ref.at[slot])
            srcs = (a_ref, b_ref) if own else dsts
            return [pltpu.make_async_remote_copy(
                src_ref=s, dst_ref=d, send_sem=send_sems.at[2 * k + i], recv_sem=recv_sems.at[2 * k + i],
                device_id=to, device_id_type=MESH_ID) for i, (s, d) in enumerate(zip(srcs, dsts))]

        mine = [pltpu.make_async_copy(a_ref, _col_window(ao_ref, _slot_of(*me)), local_sems.at[0]),
                pltpu.make_async_copy(b_ref, bo_ref.at[_slot_of(*me)], local_sems.at[1])]
        for cp in mine:
            cp.start()
        first = copies(0, me, sibling, own=True)
        for j, chip in enumerate(chips):
            first += copies(1 + j, me, (*chip, mc), own=True)
        for cp in first:
            cp.start()
        passed = []
        for j, chip in enumerate(chips):
            for cp in copies(1 + j, (*chip, mc), me):
                cp.wait_recv()
            fwd = copies(4 + j, (*chip, mc), sibling)
            for cp in fwd:
                cp.start()
            passed += fwd
        for cp in copies(0, sibling, me):
            cp.wait_recv()
        for j, chip in enumerate(chips):
            for cp in copies(4 + j, (*chip, 1 - mc), me):
                cp.wait_recv()
        for cp in first + passed:
            cp.wait_send()
        for cp in mine:
            cp.wait()

    return pl.pallas_call(
        body, name=name,
        out_shape=[jax.ShapeDtypeStruct((a_sh.shape[0], FFP), a_sh.dtype),
                   jax.ShapeDtypeStruct((N_DEV,) + b_sh.shape, b_sh.dtype)],
        in_specs=[ANY_SPEC, ANY_SPEC], out_specs=[ANY_SPEC, ANY_SPEC],
        scratch_shapes=[pltpu.SemaphoreType.DMA((14,)), pltpu.SemaphoreType.DMA((14,)),
                        pltpu.SemaphoreType.DMA((2,))],
    )(a_sh, b_sh)


def exchange_layer(ga, gb, name):
    def body(a_ref, b_ref, ao_ref, bo_ref, send_sems, recv_sems, local_sems):
        mx, my, mc = _my_pos()
        mine = _slot_of(mx, my, mc)
        local = [pltpu.make_async_copy(_col_window(a_ref, mine), ao_ref.at[mine], local_sems.at[0]),
                 pltpu.make_async_copy(b_ref.at[mine], bo_ref.at[mine], local_sems.at[1])]
        for cp in local:
            cp.start()

        def copies(r, src_slot, dst_slot, peer):
            pairs = ((_col_window(a_ref, src_slot), ao_ref.at[dst_slot]), (b_ref.at[src_slot], bo_ref.at[dst_slot]))
            return [pltpu.make_async_remote_copy(
                src_ref=s, dst_ref=d, send_sem=send_sems.at[2 * (r - 1) + i], recv_sem=recv_sems.at[2 * (r - 1) + i],
                device_id=peer, device_id_type=MESH_ID) for i, (s, d) in enumerate(pairs)]

        sent, peers = [], []
        for r in range(1, N_DEV):
            peer = (1 - mx if r & 4 else mx, 1 - my if r & 2 else my, 1 - mc if r & 1 else mc)
            peers.append((r, peer, _slot_of(*peer)))
            sent += copies(r, peers[-1][2], mine, peer)
        for cp in sent:
            cp.start()
        for r, peer, ps in peers:
            for cp in copies(r, ps, ps, peer):
                cp.wait_recv()
        for cp in sent:
            cp.wait_send()
        for cp in local:
            cp.wait()

    return pl.pallas_call(
        body, name=name,
        out_shape=[jax.ShapeDtypeStruct((N_DEV, ga.shape[0], SHP), ga.dtype),
                   jax.ShapeDtypeStruct(gb.shape, gb.dtype)],
        in_specs=[ANY_SPEC, ANY_SPEC], out_specs=[ANY_SPEC, ANY_SPEC],
        scratch_shapes=[pltpu.SemaphoreType.DMA((14,)), pltpu.SemaphoreType.DMA((14,)),
                        pltpu.SemaphoreType.DMA((2,))],
    )(ga, gb)


def sum_parts(parts, name):
    _, R, C = parts.shape
    tr = _pick(R, (512, 336, 272, 256, 128, 64, 32, 16, 8))

    def body(p_ref, o_ref):
        acc = p_ref[0].astype(F32)
        for k in range(1, N_DEV):
            acc += p_ref[k].astype(F32)
        o_ref[...] = acc

    return pl.pallas_call(
        body, name=name, grid=(R // tr,),
        in_specs=[pl.BlockSpec((N_DEV, tr, C), lambda i: (0, i, 0))],
        out_specs=pl.BlockSpec((tr, C), lambda i: (i, 0)),
        out_shape=jax.ShapeDtypeStruct((R, C), F32),
        compiler_params=_cp("parallel"),
    )(parts)


def adamw(g, w, m, v, name):
    R, C = g.shape
    tr = _pick(R, (512, 352, 256, 128, 64, 32, 16, 8))
    c1 = 1.0 - ADAM_B1 ** ADAM_STEP
    c2 = 1.0 - ADAM_B2 ** ADAM_STEP

    def body(g_ref, w_ref, m_ref, v_ref, d_ref, nm_ref, nv_ref):
        gg = g_ref[...]
        mm = ADAM_B1 * m_ref[...] + (1.0 - ADAM_B1) * gg
        vv = ADAM_B2 * v_ref[...] + (1.0 - ADAM_B2) * (gg * gg)
        nm_ref[...] = mm
        nv_ref[...] = vv
        d_ref[...] = -ADAM_LR * ((mm / c1) / (jnp.sqrt(vv / c2) + ADAM_EPS) + ADAM_WD * w_ref[...])

    spec = pl.BlockSpec((tr, C), lambda i: (i, 0))
    return pl.pallas_call(
        body, name=name, grid=(R // tr,),
        in_specs=[spec] * 4, out_specs=[spec] * 3,
        out_shape=[jax.ShapeDtypeStruct((R, C), F32)] * 3,
        compiler_params=_cp("parallel"),
    )(g, w, m, v)


BIG = (("ffn1_w_gate", 1, D, FF), ("ffn1_w_up", 1, D, FF), ("ffn1_w_down", 0, FF, D),
       ("w_in", 1, D, D_IN), ("w_out", 0, D, D),
       ("ffn2_w_gate", 1, D, FF), ("ffn2_w_up", 1, D, FF), ("ffn2_w_down", 0, FF, D),
       ("ple_w_gate", 0, D, D), ("ple_w_proj", 1, PLE, D))
SMALL = (("ffn1_norm", (D,), None), ("mix_norm", (D,), None), ("lru_conv_w", (4, LRU_W), LRU_W // N_DEV),
         ("lru_conv_b", (LRU_W,), None), ("lru_w_a", (4, HD, HD), None), ("lru_b_a", (LRU_W,), None),
         ("lru_w_x", (4, HD, HD), None), ("lru_b_x", (LRU_W,), None), ("lru_lambda", (LRU_W,), None),
         ("attn_sinks", (ATT_H,), None), ("dn_conv_w", (4, 3 * DN_W), 3 * DN_W // N_DEV),
         ("dn_a_log", (DN_H,), None), ("dn_dt_bias", (DN_H,), None), ("dn_norm", (HD,), None),
         ("ffn2_norm", (D,), None), ("ple_norm", (D,), None))
SINGLE = (("rel_bias", (REL_BUCKETS, ATT_H)), ("final_norm", (D,)))


def _pack_rows(arrs, width, mult):
    flat = jnp.concatenate([a.reshape(-1) for a in arrs])
    rows = -(-flat.shape[0] // (width * mult)) * mult
    return jnp.pad(flat, (0, rows * width - flat.shape[0])).reshape(rows, width)


def _unpack_rows(packed, shapes):
    flat = packed.reshape(-1)
    out, off = [], 0
    for s in shapes:
        n = int(np.prod(s))
        out.append(flat[off:off + n].reshape(s))
        off += n
    return out


COL_NAMES = ("ffn1_w_gate", "ffn1_w_up", "ffn2_w_gate", "ffn2_w_up")


def _shard_cols(a, l):
    blk = jnp.concatenate([a[n][l] for n in COL_NAMES], axis=0)
    return jnp.pad(blk, ((0, 0), (0, SHP - SH))).astype(BF16)


def _shard_rows(a, l):
    to = lambda w, r: jnp.pad(w, ((0, r - w.shape[0]), (0, 0)))
    parts = [to(a["ffn1_w_down"][l], SHP), to(a["ffn2_w_down"][l], SHP), a["w_out"][l], a["ple_w_gate"][l],
             to(a["w_in"][l].reshape(WIN_ROWS, D), B_PPROJ - B_WIN), a["ple_w_proj"][l].reshape(-1, D)]
    return jnp.concatenate(parts, axis=0).astype(BF16)


def _full_w_in(wb):
    sh = wb[:, B_WIN:B_WIN + WIN_ROWS, :].reshape(N_DEV, D, D_IN // N_DEV)
    return jnp.pad(sh.transpose(1, 0, 2).reshape(D, D_IN), ((0, 0), (0, D_IN_PAD - D_IN)))


def _full_ple_proj(wb):
    sh = wb[:, B_PPROJ:B_ROWS, :].reshape(N_DEV, PLE, D // N_DEV)
    return sh.transpose(1, 0, 2).reshape(PLE, D)


def _shard_grads(sa, sb):
    g = {n: sa[i * D:(i + 1) * D, :SH] for i, n in enumerate(COL_NAMES)}
    g["ffn1_w_down"] = sb[B_DOWN1:B_DOWN1 + SH]
    g["ffn2_w_down"] = sb[B_DOWN2:B_DOWN2 + SH]
    g["w_out"] = sb[B_WOUT:B_WOUT + ROWS_DEV]
    g["ple_w_gate"] = sb[B_PGATE:B_PGATE + ROWS_DEV]
    g["w_in"] = sb[B_WIN:B_WIN + WIN_ROWS].reshape(D, D_IN // N_DEV)
    g["ple_w_proj"] = sb[B_PPROJ:B_ROWS].reshape(PLE, D // N_DEV)
    return g


def kernel(x, p, ffn1_norm, ffn1_w_gate, ffn1_w_up, ffn1_w_down, mix_norm, w_in, lru_conv_w, lru_conv_b, lru_w_a, lru_b_a, lru_w_x, lru_b_x, lru_lambda, attn_sinks, rel_bias, dn_conv_w, dn_a_log, dn_dt_bias, dn_norm, w_out, ffn2_norm, ffn2_w_gate, ffn2_w_up, ffn2_w_down, ple_norm, ple_w_gate, ple_w_proj, final_norm, loss_target, m_ffn1_norm, m_ffn1_w_gate, m_ffn1_w_up, m_ffn1_w_down, m_mix_norm, m_w_in, m_lru_conv_w, m_lru_conv_b, m_lru_w_a, m_lru_b_a, m_lru_w_x, m_lru_b_x, m_lru_lambda, m_attn_sinks, m_rel_bias, m_dn_conv_w, m_dn_a_log, m_dn_dt_bias, m_dn_norm, m_w_out, m_ffn2_norm, m_ffn2_w_gate, m_ffn2_w_up, m_ffn2_w_down, m_ple_norm, m_ple_w_gate, m_ple_w_proj, m_final_norm, v_ffn1_norm, v_ffn1_w_gate, v_ffn1_w_up, v_ffn1_w_down, v_mix_norm, v_w_in, v_lru_conv_w, v_lru_conv_b, v_lru_w_a, v_lru_b_a, v_lru_w_x, v_lru_b_x, v_lru_lambda, v_attn_sinks, v_rel_bias, v_dn_conv_w, v_dn_a_log, v_dn_dt_bias, v_dn_norm, v_w_out, v_ffn2_norm, v_ffn2_w_gate, v_ffn2_w_up, v_ffn2_w_down, v_ple_norm, v_ple_w_gate, v_ple_w_proj, v_final_norm):
    a = dict(locals())
    nb, S, _ = x.shape
    T = nb * S
    my_slot = _slot_of(*_my_pos())

    W = {"wa": [], "wb": [], "w_in": [], "ple_w_proj": []}
    for l in range(DEPTH):
        wa, wb = gather_layer(_shard_cols(a, l), _shard_rows(a, l), f"gather_weights_l{l}")
        W["wa"].append(wa)
        W["wb"].append(wb)
        W["w_in"].append(_full_w_in(wb))
        W["ple_w_proj"].append(_full_ple_proj(wb))
    taps = all_gather(_pack_rows([lru_conv_w, dn_conv_w], LANE, 8), "gather_conv_taps")
    tap_shapes = [lru_conv_w.shape, dn_conv_w.shape]
    lcw, dcw = zip(*[_unpack_rows(taps[k], tap_shapes) for k in range(N_DEV)])
    W["lru_conv_w"] = jnp.concatenate(lcw, axis=-1)
    W["dn_conv_w"] = jnp.concatenate(dcw, axis=-1)
    for name, _, cols in SMALL:
        if cols is None:
            W[name] = a[name]
    W["rel_bias"], W["final_norm"] = rel_bias, final_norm

    loss_local, dx, grads, d_final = _core(x.reshape(T, D), p.reshape(DEPTH, T, PLE), W,
                                           loss_target.reshape(T, D), S)
    loss = lax.psum(loss_local, AXES)

    per_layer = []
    for l in range(DEPTH):
        ra, rb = exchange_layer(grads[l]["ga"].reshape(4 * D, FFP), grads[l]["gb"], f"exchange_grads_l{l}")
        per_layer.append(_shard_grads(sum_parts(ra, f"sum_col_grads_l{l}"), sum_parts(rb, f"sum_row_grads_l{l}")))
    g_big = {name: jnp.stack([per_layer[l][name] for l in range(DEPTH)]) for name, _, _, _ in BIG}

    small_full = [jnp.stack([grads[l][name] for l in range(DEPTH)]) for name, _, _ in SMALL]
    small_full += [grads[0]["rel_bias"] + grads[1]["rel_bias"], d_final]
    small_sum = sum_parts(all_gather(_pack_rows(small_full, LANE, 8), "gather_small_grads"), "sum_small_grads")
    g_small = dict(zip([n for n, _, _ in SMALL] + [n for n, _ in SINGLE],
                       _unpack_rows(small_sum, [s.shape for s in small_full])))
    for name, _, cols in SMALL:
        if cols is not None:
            g_small[name] = lax.dynamic_slice_in_dim(g_small[name], my_slot * cols, cols, axis=2)

    out = {}
    for name, _, _, _ in BIG:
        shape = a[name].shape
        two_d = lambda t: t.reshape(-1, shape[-1])
        res = adamw(two_d(g_big[name]), two_d(a[name]), two_d(a["m_" + name]), two_d(a["v_" + name]),
                    "adamw_" + name)
        out[name] = (g_big[name],) + tuple(r.reshape(shape) for r in res)
    small_names = [n for n, _, _ in SMALL] + [n for n, _ in SINGLE]
    shapes = [a[n].shape for n in small_names]
    packed = [_pack_rows([a[pre + n] if pre is not None else g_small[n] for n in small_names], LANE, 8)
              for pre in (None, "", "m_", "v_")]
    res = adamw(*packed, "adamw_small")
    unpacked = [_unpack_rows(r, shapes) for r in res]
    for i, n in enumerate(small_names):
        out[n] = (g_small[n].reshape(shapes[i]),) + tuple(u[i] for u in unpacked)

    order = ['ffn1_norm', 'ffn1_w_gate', 'ffn1_w_up', 'ffn1_w_down', 'mix_norm', 'w_in', 'lru_conv_w', 'lru_conv_b',
             'lru_w_a', 'lru_b_a', 'lru_w_x', 'lru_b_x', 'lru_lambda', 'attn_sinks', 'rel_bias', 'dn_conv_w',
             'dn_a_log', 'dn_dt_bias', 'dn_norm', 'w_out', 'ffn2_norm', 'ffn2_w_gate', 'ffn2_w_up', 'ffn2_w_down',
             'ple_norm', 'ple_w_gate', 'ple_w_proj', 'final_norm']
    return (loss, dx.reshape(x.shape)) + tuple(out[n][k] for k in range(4) for n in order)
```

```python
import functools
import math

import numpy as np
import jax
import jax.numpy as jnp
from jax import lax
from jax.experimental import pallas as pl
from jax.experimental.pallas import tpu as pltpu

F32 = jnp.float32
BF16 = jnp.bfloat16
HI = lax.Precision.HIGHEST

D = 1024
DEPTH = 2
EPS = 1e-6
PLE = 256
FF = 2816
HD = 64
LRU_W = 256
LRU_C = 8.0
ATT_W = 512
ATT_H = 8
ATT_KV = 2
ATT_G = 4
KV_W = 128
WINDOW = 128
BQ = 128
REL_BUCKETS = 32
REL_MAX_DIST = 128
DN_W = 256
DN_H = 4
CHUNK = 64
D_IN = 2312
D_IN_PAD = 2432
N_DEV = 8

ADAM_LR = 0.001
ADAM_B1 = 0.9
ADAM_B2 = 0.999
ADAM_EPS = 1e-08
ADAM_WD = 0.01
ADAM_STEP = 10

LANE = 128
VMEM_LIMIT = 56 * 1024 * 1024
SH = FF // N_DEV
SHP = 384
FFP = N_DEV * SHP
FF_TILE = 2 * SHP
TOK_TILE = 512
B_DOWN1, B_DOWN2, B_WOUT, B_PGATE, B_WIN, B_PPROJ, B_ROWS = 0, 384, 768, 896, 1024, 1328, 1360
WIN_ROWS = D * D_IN // N_DEV // 1024
NEG = -1e30


def _cp(*sem):
    return pltpu.CompilerParams(dimension_semantics=tuple(sem), vmem_limit_bytes=VMEM_LIMIT)


def _dg(a, b, ca, cb, exact):
    dims = (((ca,), (cb,)), ((), ()))
    if exact == "f32":
        return lax.dot_general(a.astype(F32), b.astype(F32), dims, precision=HI, preferred_element_type=F32)
    if exact == "split":
        a_hi, b_hi = a.astype(BF16), b.astype(BF16)
        a_lo = (a - a_hi.astype(F32)).astype(BF16)
        b_lo = (b - b_hi.astype(F32)).astype(BF16)
        dot = lambda u, v: lax.dot_general(u, v, dims, preferred_element_type=F32)
        return dot(a_hi, b_hi) + (dot(a_hi, b_lo) + dot(a_lo, b_hi))
    return lax.dot_general(a.astype(BF16), b.astype(BF16), dims, preferred_element_type=F32)


def _make_mm(exact):
    @jax.custom_vjp
    def mm(a, b):
        return _dg(a, b, 1, 0, exact)

    @jax.custom_vjp
    def mm_nt(a, b):
        return _dg(a, b, 1, 1, exact)

    @jax.custom_vjp
    def mm_tn(a, b):
        return _dg(a, b, 0, 0, exact)

    mm.defvjp(lambda a, b: (mm(a, b), (a, b)),
              lambda r, d: (mm_nt(d, r[1]), mm_tn(r[0], d)))
    mm_nt.defvjp(lambda a, b: (mm_nt(a, b), (a, b)),
                 lambda r, d: (mm(d, r[1]), mm_tn(d, r[0])))
    mm_tn.defvjp(lambda a, b: (mm_tn(a, b), (a, b)),
                 lambda r, d: (mm_nt(r[1], d), mm(r[0], d)))
    return mm, mm_nt, mm_tn


_mm, _mm_nt, _mm_tn = _make_mm("bf16")
_mmx, _mmx_nt, _mmx_tn = _make_mm("f32")
_mm3, _mm3_nt, _mm3_tn = _make_mm("split")


def _iota(shape, dim):
    return lax.broadcasted_iota(jnp.int32, shape, dim)


def _sigmoid(x):
    return 1.0 / (1.0 + jnp.exp(-x))


def _rms(h, g):
    rstd = lax.rsqrt(jnp.mean(h * h, axis=-1, keepdims=True) + EPS)
    xhat = h * rstd
    return xhat * g, xhat, rstd


def _rms_bwd(dxn, xhat, rstd, g):
    dxhat = dxn * g
    dh = rstd * (dxhat - xhat * jnp.mean(dxhat * xhat, axis=-1, keepdims=True))
    dg = jnp.sum(dxn * xhat, axis=0, keepdims=True)
    return dh, dg


def _row_spec(tm, n):
    return pl.BlockSpec((tm, n), lambda i, *_: (i, 0))


def _full_spec(shape):
    nd = len(shape)
    return pl.BlockSpec(shape, lambda *_: (0,) * nd)


def _ffn_weight_specs(fidx):
    return [pl.BlockSpec((D, FF_TILE), lambda i, j: (2 * fidx, j)),
            pl.BlockSpec((D, FF_TILE), lambda i, j: (2 * fidx + 1, j)),
            pl.BlockSpec((2, SHP, D), lambda i, j: (j, fidx, 0))]


def ffn_fwd(h, g, wa, wb, fidx, name):
    T = h.shape[0]
    tm = min(TOK_TILE, T)
    nj = FFP // FF_TILE

    def body(h_ref, g_ref, wg_ref, wu_ref, wd_ref, o_ref, xn_s):
        j = pl.program_id(1)

        @pl.when(j == 0)
        def _():
            hh = h_ref[...]
            xn_s[...] = _rms(hh, g_ref[...])[0].astype(BF16)
            o_ref[...] = hh

        xn = xn_s[...]
        gt = _mm(xn, wg_ref[...])
        up = _mm(xn, wu_ref[...])
        act = gt * _sigmoid(gt) * up
        o_ref[...] += 0.5 * _mm(act, wd_ref[...].reshape(FF_TILE, D))

    return pl.pallas_call(
        body, name=name, grid=(T // tm, nj),
        in_specs=[pl.BlockSpec((tm, D), lambda i, j: (i, 0)),
                  pl.BlockSpec((1, D), lambda i, j: (0, 0))] + _ffn_weight_specs(fidx),
        out_specs=pl.BlockSpec((tm, D), lambda i, j: (i, 0)),
        out_shape=jax.ShapeDtypeStruct((T, D), F32),
        scratch_shapes=[pltpu.VMEM((tm, D), BF16)],
        compiler_params=_cp("parallel", "arbitrary"),
    )(h, g, wa, wa, wb)


def ffn_bwd(h, dy, g, wa, wb, fidx, name):
    T = h.shape[0]
    tm = min(TOK_TILE, T)
    nj = FFP // FF_TILE

    def body(h_ref, dy_ref, g_ref, wg_ref, wu_ref, wd_ref,
             dh_ref, dg_ref, du_ref, a_ref, xn_ref, dn_ref, xn_s, dxn_s):
        i = pl.program_id(0)
        j = pl.program_id(1)

        @pl.when(j == 0)
        def _():
            xn = _rms(h_ref[...], g_ref[...])[0].astype(BF16)
            xn_s[...] = xn
            xn_ref[...] = xn
            dxn_s[...] = jnp.zeros_like(dxn_s)

        @pl.when((i == 0) & (j == 0))
        def _():
            dn_ref[...] = jnp.zeros_like(dn_ref)

        xn = xn_s[...]
        gt = _mm(xn, wg_ref[...])
        up = _mm(xn, wu_ref[...])
        sg = _sigmoid(gt)
        si = gt * sg
        da = _mm_nt(0.5 * dy_ref[...], wd_ref[...].reshape(FF_TILE, D))
        dup = da * si
        dgt = da * up * (sg * (1.0 + gt * (1.0 - sg)))
        dg_ref[...] = dgt.astype(BF16)
        du_ref[...] = dup.astype(BF16)
        a_ref[...] = (si * up).astype(BF16)
        dxn_s[...] += _mm_nt(dgt, wg_ref[...]) + _mm_nt(dup, wu_ref[...])

        @pl.when(j == nj - 1)
        def _():
            gg = g_ref[...]
            _, xhat, rstd = _rms(h_ref[...], gg)
            dh, dn = _rms_bwd(dxn_s[...], xhat, rstd, gg)
            dh_ref[...] = dy_ref[...] + dh
            dn_ref[...] += dn

    tile = pl.BlockSpec((tm, FF_TILE), lambda i, j: (i, j))
    return pl.pallas_call(
        body, name=name, grid=(T // tm, nj),
        in_specs=[pl.BlockSpec((tm, D), lambda i, j: (i, 0)),
                  pl.BlockSpec((tm, D), lambda i, j: (i, 0)),
                  pl.BlockSpec((1, D), lambda i, j: (0, 0))] + _ffn_weight_specs(fidx),
        out_specs=[pl.BlockSpec((tm, D), lambda i, j: (i, 0)), tile, tile, tile,
                   pl.BlockSpec((tm, D), lambda i, j: (i, 0)),
                   pl.BlockSpec((1, D), lambda i, j: (0, 0))],
        out_shape=[jax.ShapeDtypeStruct((T, D), F32)] + [jax.ShapeDtypeStruct((T, FFP), BF16)] * 3
        + [jax.ShapeDtypeStruct((T, D), BF16), jax.ShapeDtypeStruct((1, D), F32)],
        scratch_shapes=[pltpu.VMEM((tm, D), BF16), pltpu.VMEM((tm, D), F32)],
        compiler_params=_cp("arbitrary", "arbitrary"),
    )(h, dy, g, wa, wa, wb)


def _pick(n, prefs):
    for t in prefs:
        if n % t == 0:
            return t
    return n


def _tn_body(nk, scale, out_dtype, squeeze):
    def body(a_ref, b_ref, *rest):
        o_ref, acc = rest[-2], rest[-1]
        k = pl.program_id(2)

        @pl.when(k == 0)
        def _():
            acc[...] = jnp.zeros_like(acc)

        acc[...] += _mm_tn(a_ref[...], b_ref[...])

        @pl.when(k == nk - 1)
        def _():
            res = (scale * acc[...]).astype(out_dtype)
            if squeeze:
                o_ref[0] = res
            else:
                o_ref[...] = res

    return body


def matmul_tn(a, b, name, scale=1.0, out_dtype=BF16):
    T, M = a.shape
    N = b.shape[1]
    tmm = _pick(M, (512, 256))
    tnn = _pick(N, (1024, 2432))
    tk = min(TOK_TILE, T)
    nk = T // tk
    return pl.pallas_call(
        _tn_body(nk, scale, out_dtype, False), name=name, grid=(M // tmm, N // tnn, nk),
        in_specs=[pl.BlockSpec((tk, tmm), lambda i, j, k: (k, i)),
                  pl.BlockSpec((tk, tnn), lambda i, j, k: (k, j))],
        out_specs=pl.BlockSpec((tmm, tnn), lambda i, j, k: (i, j)),
        out_shape=jax.ShapeDtypeStruct((M, N), out_dtype),
        scratch_shapes=[pltpu.VMEM((tmm, tnn), F32)],
        compiler_params=_cp("parallel", "parallel", "arbitrary"),
    )(a, b)


def grad_cols(a, b, dst, slot, name):
    T = a.shape[0]
    tmm, tnn = 512, FFP // 2
    tk = min(TOK_TILE, T)
    nk = T // tk
    return pl.pallas_call(
        _tn_body(nk, 1.0, BF16, True), name=name, grid=(D // tmm, FFP // tnn, nk),
        in_specs=[pl.BlockSpec((tk, tmm), lambda i, j, k: (k, i)),
                  pl.BlockSpec((tk, tnn), lambda i, j, k: (k, j)),
                  pl.BlockSpec(memory_space=pl.ANY)],
        out_specs=pl.BlockSpec((1, tmm, tnn), lambda i, j, k: (slot, i, j)),
        out_shape=jax.ShapeDtypeStruct(dst.shape, dst.dtype),
        scratch_shapes=[pltpu.VMEM((tmm, tnn), F32)],
        input_output_aliases={2: 0},
        compiler_params=_cp("parallel", "parallel", "arbitrary"),
    )(a, b, dst)


def grad_rows(a, b, dst, row0, rows, name, scale=1.0):
    T = a.shape[0]
    tk = min(TOK_TILE, T)
    nk = T // tk
    blk = row0 // rows
    return pl.pallas_call(
        _tn_body(nk, scale, BF16, True), name=name, grid=(N_DEV, 1, nk),
        in_specs=[pl.BlockSpec((tk, rows), lambda i, j, k: (k, i)),
                  pl.BlockSpec((tk, D), lambda i, j, k: (k, 0)),
                  pl.BlockSpec(memory_space=pl.ANY)],
        out_specs=pl.BlockSpec((1, rows, D), lambda i, j, k: (i, blk, 0)),
        out_shape=jax.ShapeDtypeStruct(dst.shape, dst.dtype),
        scratch_shapes=[pltpu.VMEM((rows, D), F32)],
        input_output_aliases={2: 0},
        compiler_params=_cp("parallel", "parallel", "arbitrary"),
    )(a, b, dst)


U_SPLITS = (512, 768, 1024, 128)
U_OFFS = (0, 512, 1280, 2304)


def mixin_fwd(h, g, w_in, name):
    T = h.shape[0]
    tm = min(TOK_TILE, T)

    def body(h_ref, g_ref, w_ref, u0, u1, u2, u3, xn_ref):
        xn = _rms(h_ref[...], g_ref[...])[0].astype(BF16)
        xn_ref[...] = xn
        u = _mm(xn, w_ref[...])
        for ref, off, n in zip((u0, u1, u2, u3), U_OFFS, U_SPLITS):
            ref[...] = u[:, off:off + n]

    return pl.pallas_call(
        body, name=name, grid=(T // tm,),
        in_specs=[_row_spec(tm, D), _full_spec((1, D)), _full_spec((D, D_IN_PAD))],
        out_specs=[_row_spec(tm, n) for n in U_SPLITS] + [_row_spec(tm, D)],
        out_shape=[jax.ShapeDtypeStruct((T, n), F32) for n in U_SPLITS]
        + [jax.ShapeDtypeStruct((T, D), BF16)],
        compiler_params=_cp("parallel"),
    )(h, g, w_in)


def mixin_bwd(h, dh_in, g, w_in, dus, name):
    T = h.shape[0]
    tm = min(TOK_TILE, T)

    def body(h_ref, dhi_ref, g_ref, w_ref, d0, d1, d2, d3, dh_ref, du_ref, dn_ref):
        @pl.when(pl.program_id(0) == 0)
        def _():
            dn_ref[...] = jnp.zeros_like(dn_ref)

        dxn = jnp.zeros((tm, D), F32)
        for ref, off, n in zip((d0, d1, d2, d3), U_OFFS, U_SPLITS):
            du = ref[...]
            du_ref[:, off:off + n] = du.astype(BF16)
            dxn += _mm_nt(du, w_ref[:, off:off + n])
        gg = g_ref[...]
        _, xhat, rstd = _rms(h_ref[...], gg)
        dh, dn = _rms_bwd(dxn, xhat, rstd, gg)
        dh_ref[...] = dhi_ref[...] + dh
        dn_ref[...] += dn

    return pl.pallas_call(
        body, name=name, grid=(T // tm,),
        in_specs=[_row_spec(tm, D), _row_spec(tm, D), _full_spec((1, D)), _full_spec((D, D_IN_PAD))]
        + [_row_spec(tm, n) for n in U_SPLITS],
        out_specs=[_row_spec(tm, D), _row_spec(tm, D_IN_PAD), _full_spec((1, D))],
        out_shape=[jax.ShapeDtypeStruct((T, D), F32), jax.ShapeDtypeStruct((T, D_IN_PAD), BF16),
                   jax.ShapeDtypeStruct((1, D), F32)],
        compiler_params=_cp("arbitrary"),
    )(h, dh_in, g, w_in, *dus)


def _shift_down(x, s, row):
    if s == 0:
        return x
    return jnp.where(row >= s, pltpu.roll(x, s, 0), 0.0)


def _shift_up(x, s, row):
    if s == 0:
        return x
    n = x.shape[0]
    return jnp.where(row < n - s, pltpu.roll(x, n - s, 0), 0.0)


def conv_fwd(x, w, b, S, col0, C, name):
    T = x.shape[0]
    cb0 = col0 // LANE

    def body(x_ref, w_ref, b_ref, y_ref):
        xx = x_ref[...]
        row = _iota(xx.shape, 0)
        y = xx * w_ref[3:4, :] + b_ref[...]
        for k in range(3):
            y += _shift_down(xx, 3 - k, row) * w_ref[k:k + 1, :]
        y_ref[...] = y

    return pl.pallas_call(
        body, name=name, grid=(T // S, C // LANE),
        in_specs=[pl.BlockSpec((S, LANE), lambda s, c: (s, cb0 + c)),
                  pl.BlockSpec((4, LANE), lambda s, c: (0, c)),
                  pl.BlockSpec((1, LANE), lambda s, c: (0, c))],
        out_specs=pl.BlockSpec((S, LANE), lambda s, c: (s, c)),
        out_shape=jax.ShapeDtypeStruct((T, C), F32),
        compiler_params=_cp("parallel", "parallel"),
    )(x, w, b)


def conv_bwd(x, dy, w, S, col0, C, name):
    T = x.shape[0]
    cb0 = col0 // LANE

    def body(x_ref, dy_ref, w_ref, dx_ref, dwb_ref):
        @pl.when(pl.program_id(1) == 0)
        def _():
            dwb_ref[...] = jnp.zeros_like(dwb_ref)

        xx = x_ref[...]
        dd = dy_ref[...]
        row = _iota(xx.shape, 0)
        dx = dd * w_ref[3:4, :]
        for k in range(3):
            dx += _shift_up(dd, 3 - k, row) * w_ref[k:k + 1, :]
        dx_ref[...] = dx
        for k in range(4):
            dwb_ref[k:k + 1, :] += jnp.sum(dd * _shift_down(xx, 3 - k, row), axis=0, keepdims=True)
        dwb_ref[4:5, :] += jnp.sum(dd, axis=0, keepdims=True)

    return pl.pallas_call(
        body, name=name, grid=(C // LANE, T // S),
        in_specs=[pl.BlockSpec((S, LANE), lambda c, s: (s, cb0 + c)),
                  pl.BlockSpec((S, LANE), lambda c, s: (s, c)),
                  pl.BlockSpec((4, LANE), lambda c, s: (0, c))],
        out_specs=[pl.BlockSpec((S, LANE), lambda c, s: (s, c)),
                   pl.BlockSpec((8, LANE), lambda c, s: (0, c))],
        out_shape=[jax.ShapeDtypeStruct((T, C), F32), jax.ShapeDtypeStruct((8, C), F32)],
        compiler_params=_cp("parallel", "arbitrary"),
    )(x, dy, w)


def _scan(a, b, row):
    n = a.shape[0]
    d = 1
    while d < n:
        keep = row >= d
        b = a * jnp.where(keep, pltpu.roll(b, d, 0), 0.0) + b
        a = a * jnp.where(keep, pltpu.roll(a, d, 0), 1.0)
        d *= 2
    return b


def _rscan(a, b, row):
    n = a.shape[0]
    d = 1
    while d < n:
        keep = row < n - d
        b = a * jnp.where(keep, pltpu.roll(b, n - d, 0), 0.0) + b
        a = a * jnp.where(keep, pltpu.roll(a, n - d, 0), 1.0)
        d *= 2
    return b


GELU_C = math.sqrt(2.0 / math.pi)


def _gelu(x):
    t = jnp.tanh(GELU_C * (x + 0.044715 * (x * x * x)))
    return 0.5 * x * (1.0 + t), t


def _lru_gates(xr, wa, ba, wx, bx, lam):
    r = _sigmoid(_mm(xr, wa) + ba)
    i = _sigmoid(_mm(xr, wx) + bx)
    sp = jnp.maximum(-lam, 0.0) + jnp.log(1.0 + jnp.exp(-jnp.abs(lam)))
    la = -LRU_C * r * sp
    a = jnp.exp(la)
    e2 = a * a
    m = jnp.sqrt(-jnp.tanh(la) * (e2 + 1.0))
    return r, i, sp, a, e2, m


def lru_fwd(xr, u_lru, wa, wx, vec, S, name):
    T = xr.shape[0]

    def body(xr_ref, gt_ref, wa_ref, wx_ref, vec_ref, y_ref):
        x = xr_ref[...]
        row = _iota(x.shape, 0)
        r, i, sp, a, e2, m = _lru_gates(x, wa_ref[...], vec_ref[0:1, :], wx_ref[...], vec_ref[1:2, :],
                                        vec_ref[2:3, :])
        hh = _scan(a, m * (i * x), row)
        y_ref[...] = _gelu(gt_ref[...])[0] * hh

    return pl.pallas_call(
        body, name=name, grid=(T // S, LRU_W // LANE),
        in_specs=[pl.BlockSpec((S, LANE), lambda s, c: (s, c)),
                  pl.BlockSpec((S, LANE), lambda s, c: (s, 2 + c)),
                  pl.BlockSpec((LANE, LANE), lambda s, c: (c, c)),
                  pl.BlockSpec((LANE, LANE), lambda s, c: (c, c)),
                  pl.BlockSpec((8, LANE), lambda s, c: (0, c))],
        out_specs=pl.BlockSpec((S, LANE), lambda s, c: (s, c)),
        out_shape=jax.ShapeDtypeStruct((T, LRU_W), F32),
        compiler_params=_cp("parallel", "parallel"),
    )(xr, u_lru, wa, wx, vec)


def lru_bwd(xr, u_lru, dy, wa, wx, vec, S, name):
    T = xr.shape[0]

    def body(xr_ref, gt_ref, dy_ref, wa_ref, wx_ref, vec_ref,
             dxr_ref, dgt_ref, dwa_ref, dwx_ref, dvec_ref):
        @pl.when(pl.program_id(1) == 0)
        def _():
            dwa_ref[...] = jnp.zeros_like(dwa_ref)
            dwx_ref[...] = jnp.zeros_like(dwx_ref)
            dvec_ref[...] = jnp.zeros_like(dvec_ref)

        x = xr_ref[...]
        n = x.shape[0]
        row = _iota(x.shape, 0)
        lam = vec_ref[2:3, :]
        r, i, sp, a, e2, m = _lru_gates(x, wa_ref[...], vec_ref[0:1, :], wx_ref[...], vec_ref[1:2, :], lam)
        v = i * x
        hh = _scan(a, m * v, row)
        gt = gt_ref[...]
        dy = dy_ref[...]
        ge, t = _gelu(gt)
        dgt_ref[...] = dy * hh * (0.5 * (1.0 + t) + 0.5 * gt * (1.0 - t * t) * GELU_C
                                  * (1.0 + 3.0 * 0.044715 * gt * gt))
        a_next = jnp.where(row < n - 1, pltpu.roll(a, n - 1, 0), 0.0)
        G = _rscan(a_next, dy * ge, row)
        da = G * _shift_down(hh, 1, row)
        dv = G * m
        dla = da * a - (G * v) * e2 / m
        dr = dla * (-LRU_C * sp)
        dsp = jnp.sum(dla * (-LRU_C * r), axis=0, keepdims=True)
        dra = dr * r * (1.0 - r)
        dia = (dv * x) * i * (1.0 - i)
        dxr_ref[...] = dv * i + _mm_nt(dra, wa_ref[...]) + _mm_nt(dia, wx_ref[...])
        dwa_ref[0] += _mm_tn(x, dra)
        dwx_ref[0] += _mm_tn(x, dia)
        dvec_ref[0:1, :] += jnp.sum(dra, axis=0, keepdims=True)
        dvec_ref[1:2, :] += jnp.sum(dia, axis=0, keepdims=True)
        dvec_ref[2:3, :] += dsp * (-_sigmoid(-lam))

    return pl.pallas_call(
        body, name=name, grid=(LRU_W // LANE, T // S),
        in_specs=[pl.BlockSpec((S, LANE), lambda c, s: (s, c)),
                  pl.BlockSpec((S, LANE), lambda c, s: (s, 2 + c)),
                  pl.BlockSpec((S, LANE), lambda c, s: (s, c)),
                  pl.BlockSpec((LANE, LANE), lambda c, s: (c, c)),
                  pl.BlockSpec((LANE, LANE), lambda c, s: (c, c)),
                  pl.BlockSpec((8, LANE), lambda c, s: (0, c))],
        out_specs=[pl.BlockSpec((S, LANE), lambda c, s: (s, c)),
                   pl.BlockSpec((S, LANE), lambda c, s: (s, c)),
                   pl.BlockSpec((1, LANE, LANE), lambda c, s: (c, 0, 0)),
                   pl.BlockSpec((1, LANE, LANE), lambda c, s: (c, 0, 0)),
                   pl.BlockSpec((8, LANE), lambda c, s: (0, c))],
        out_shape=[jax.ShapeDtypeStruct((T, LRU_W), F32), jax.ShapeDtypeStruct((T, LRU_W), F32),
                   jax.ShapeDtypeStruct((2, LANE, LANE), F32), jax.ShapeDtypeStruct((2, LANE, LANE), F32),
                   jax.ShapeDtypeStruct((8, LRU_W), F32)],
        compiler_params=_cp("parallel", "arbitrary"),
    )(xr, u_lru, dy, wa, wx, vec)


def _bucket_table():
    qi = np.arange(BQ)[:, None]
    kj = np.arange(2 * BQ)[None, :]
    dist = BQ + qi - kj
    band = (dist >= 0) & (dist < WINDOW)
    dd = np.maximum(dist, 0)
    max_exact = REL_BUCKETS // 2
    large = max_exact + (np.log(np.maximum(dd, 1).astype(np.float32) / np.float32(max_exact))
                         / np.float32(math.log(REL_MAX_DIST / max_exact))
                         * np.float32(REL_BUCKETS - max_exact)).astype(np.int32)
    large = np.minimum(large, REL_BUCKETS - 1)
    bucket = np.where(dd < max_exact, dd, large)
    return np.where(band, bucket, -1).astype(np.int32)


def _att_specs(S):
    nb = S // BQ
    qc = ATT_W // LANE
    return [pl.BlockSpec((BQ, ATT_W), lambda b, n: (b * nb + n, 0)),
            pl.BlockSpec((BQ, KV_W), lambda b, n: (b * nb + jnp.maximum(n - 1, 0), qc)),
            pl.BlockSpec((BQ, KV_W), lambda b, n: (b * nb + n, qc)),
            pl.BlockSpec((BQ, KV_W), lambda b, n: (b * nb + jnp.maximum(n - 1, 0), qc + 1)),
            pl.BlockSpec((BQ, KV_W), lambda b, n: (b * nb + n, qc + 1))]


def _att_bias(bk, rb_ref, bias_s):
    for h in range(ATT_H):
        acc = jnp.zeros(bk.shape, F32)
        for bb in range(REL_BUCKETS):
            acc = jnp.where(bk == bb, rb_ref[bb * ATT_H + h], acc)
        bias_s[h] = acc


def _att_probs(qh, kg, bias, valid, sink):
    s = _mm_nt(qh, kg) * (HD ** -0.5) + bias
    s = jnp.where(valid, s, NEG)
    m = jnp.maximum(jnp.max(s, axis=-1, keepdims=True), sink)
    e = jnp.exp(s - m)
    es = jnp.exp(sink - m)
    den = jnp.sum(e, axis=-1, keepdims=True) + es
    return e / den, es / den


def attn_fwd(u_att, sinks, rel_bias, S, name):
    T = u_att.shape[0]
    nb = S // BQ
    table = jnp.asarray(_bucket_table())

    def body(sk_ref, rb_ref, bk_ref, q_ref, kp_ref, kc_ref, vp_ref, vc_ref, o_ref, bias_s):
        b = pl.program_id(0)
        n = pl.program_id(1)
        bk = bk_ref[...]

        @pl.when((b == 0) & (n == 0))
        def _():
            _att_bias(bk, rb_ref, bias_s)

        valid = (bk >= 0) & ((n > 0) | (_iota(bk.shape, 1) >= BQ))
        for h in range(ATT_H):
            gs = slice(HD * (h // ATT_G), HD * (h // ATT_G + 1))
            kg = jnp.concatenate([kp_ref[:, gs], kc_ref[:, gs]], axis=0)
            vg = jnp.concatenate([vp_ref[:, gs], vc_ref[:, gs]], axis=0)
            p, _ = _att_probs(q_ref[:, HD * h:HD * (h + 1)], kg, bias_s[h], valid, sk_ref[h])
            o_ref[:, HD * h:HD * (h + 1)] = _mm(p, vg)

    smem = pl.BlockSpec(memory_space=pltpu.SMEM)
    return pl.pallas_call(
        body, name=name, grid=(T // S, nb),
        in_specs=[smem, smem, _full_spec((BQ, 2 * BQ))] + _att_specs(S),
        out_specs=pl.BlockSpec((BQ, ATT_W), lambda b, n: (b * nb + n, 0)),
        out_shape=jax.ShapeDtypeStruct((T, ATT_W), F32),
        scratch_shapes=[pltpu.VMEM((ATT_H, BQ, 2 * BQ), F32)],
        compiler_params=_cp("arbitrary", "arbitrary"),
    )(sinks, rel_bias, table, u_att, u_att, u_att, u_att, u_att)


def attn_bwd(u_att, dy, sinks, rel_bias, S, name):
    T = u_att.shape[0]
    nb = S // BQ
    nB = T // S
    table = jnp.asarray(_bucket_table())
    scale = HD ** -0.5

    def body(sk_ref, rb_ref, bk_ref, q_ref, kp_ref, kc_ref, vp_ref, vc_ref, dy_ref,
             du_ref, drel_ref, dsk_ref, bias_s, dbias_s):
        b = pl.program_id(0)
        n = pl.program_id(1)
        bk = bk_ref[...]

        @pl.when((b == 0) & (n == 0))
        def _():
            _att_bias(bk, rb_ref, bias_s)
            dbias_s[...] = jnp.zeros_like(dbias_s)
            dsk_ref[...] = jnp.zeros_like(dsk_ref)
            drel_ref[...] = jnp.zeros_like(drel_ref)

        @pl.when(n == 0)
        def _():
            du_ref[...] = jnp.zeros_like(du_ref)

        valid = (bk >= 0) & ((n > 0) | (_iota(bk.shape, 1) >= BQ))
        r_cur = pl.multiple_of(n * BQ, BQ)
        r_prev = pl.multiple_of(jnp.maximum(n - 1, 0) * BQ, BQ)
        for g in range(ATT_KV):
            gs = slice(HD * g, HD * (g + 1))
            kg = jnp.concatenate([kp_ref[:, gs], kc_ref[:, gs]], axis=0)
            vg = jnp.concatenate([vp_ref[:, gs], vc_ref[:, gs]], axis=0)
            dk = jnp.zeros((2 * BQ, HD), F32)
            dv = jnp.zeros((2 * BQ, HD), F32)
            for e in range(ATT_G):
                h = g * ATT_G + e
                qh = q_ref[:, HD * h:HD * (h + 1)]
                do = dy_ref[:, HD * h:HD * (h + 1)]
                p, ps = _att_probs(qh, kg, bias_s[h], valid, sk_ref[h])
                dp = _mm_nt(do, vg)
                delta = jnp.sum(p * dp, axis=-1, keepdims=True)
                ds = p * (dp - delta)
                dbias_s[h] += ds
                dsk_ref[h:h + 1, :] += jnp.broadcast_to(
                    jnp.sum(-ps * delta, axis=0, keepdims=True), (1, LANE))
                dss = ds * scale
                du_ref[pl.ds(r_cur, BQ), HD * h:HD * (h + 1)] = _mm(dss, kg)
                dk += _mm_tn(dss, qh)
                dv += _mm_tn(p, do)
            ck = ATT_W + HD * g
            cv = ATT_W + KV_W + HD * g
            du_ref[pl.ds(r_prev, BQ), ck:ck + HD] += dk[0:BQ]
            du_ref[pl.ds(r_cur, BQ), ck:ck + HD] += dk[BQ:]
            du_ref[pl.ds(r_prev, BQ), cv:cv + HD] += dv[0:BQ]
            du_ref[pl.ds(r_cur, BQ), cv:cv + HD] += dv[BQ:]

        @pl.when((b == nB - 1) & (n == nb - 1))
        def _():
            lane = _iota((1, LANE), 1)
            for h in range(ATT_H):
                db = dbias_s[h]
                acc = jnp.zeros((1, LANE), F32)
                for bb in range(REL_BUCKETS):
                    val = jnp.sum(jnp.sum(jnp.where(bk == bb, db, 0.0), axis=1, keepdims=True),
                                  axis=0, keepdims=True)
                    acc = jnp.where(lane == bb, val, acc)
                drel_ref[h:h + 1, :] = acc

    smem = pl.BlockSpec(memory_space=pltpu.SMEM)
    return pl.pallas_call(
        body, name=name, grid=(nB, nb),
        in_specs=[smem, smem, _full_spec((BQ, 2 * BQ))] + _att_specs(S)
        + [pl.BlockSpec((BQ, ATT_W), lambda b, n: (b * nb + n, 0))],
        out_specs=[pl.BlockSpec((S, ATT_W + 2 * KV_W), lambda b, n: (b, 0)),
                   _full_spec((8, LANE)), _full_spec((8, LANE))],
        out_shape=[jax.ShapeDtypeStruct((T, ATT_W + 2 * KV_W), F32),
                   jax.ShapeDtypeStruct((8, LANE), F32), jax.ShapeDtypeStruct((8, LANE), F32)],
        scratch_shapes=[pltpu.VMEM((ATT_H, BQ, 2 * BQ), F32), pltpu.VMEM((ATT_H, BQ, 2 * BQ), F32)],
        compiler_params=_cp("arbitrary", "arbitrary"),
    )(sinks, rel_bias, table, u_att, u_att, u_att, u_att, u_att, dy)


def _head_of(i):
    return lax.shift_right_logical(i, 6)


def _head_mask(shape):
    return (_head_of(_iota(shape, 0)) == _head_of(_iota(shape, 1))).astype(F32)


def _dn_point(c, uba, alog, dtb):
    s = c * _sigmoid(c)
    qt, kt, vt = s[:, 0:256], s[:, 256:512], s[:, 512:768]
    ones_bd = _head_mask((DN_W, DN_W))
    q = qt * lax.rsqrt(_mmx(qt * qt, ones_bd) + EPS) * (HD ** -0.5)
    k = kt * lax.rsqrt(_mmx(kt * kt, ones_bd) + EPS)
    sel = _head_of(_iota((LANE, DN_W), 1))
    row = _iota((LANE, DN_W), 0)
    braw = _mmx(uba, (row == sel).astype(F32))
    araw = _mmx(uba, (row == sel + DN_H).astype(F32)) + dtb
    beta = _sigmoid(braw)
    g = -jnp.exp(alog) * (jnp.maximum(araw, 0.0) + jnp.log(1.0 + jnp.exp(-jnp.abs(araw))))
    return q, k, vt, g, beta


def dn_point_fwd(c, uba, alog, dtb, name):
    T = c.shape[0]
    tm = min(TOK_TILE, T)

    def body(c_ref, u_ref, al_ref, dt_ref, *outs):
        for ref, val in zip(outs, _dn_point(c_ref[...], u_ref[...], al_ref[...], dt_ref[...])):
            ref[...] = val

    return pl.pallas_call(
        body, name=name, grid=(T // tm,),
        in_specs=[_row_spec(tm, 768), _row_spec(tm, LANE), _full_spec((1, DN_W)), _full_spec((1, DN_W))],
        out_specs=[_row_spec(tm, DN_W)] * 5,
        out_shape=[jax.ShapeDtypeStruct((T, DN_W), F32)] * 5,
        compiler_params=_cp("parallel"),
    )(c, uba, alog, dtb)


def dn_point_bwd(c, uba, alog, dtb, douts, name):
    T = c.shape[0]
    tm = min(TOK_TILE, T)

    def body(c_ref, u_ref, al_ref, dt_ref, dq, dk, dv, dg, db, dc_ref, du_ref, dvec_ref):
        @pl.when(pl.program_id(0) == 0)
        def _():
            dvec_ref[...] = jnp.zeros_like(dvec_ref)

        _, vjp = jax.vjp(_dn_point, c_ref[...], u_ref[...], al_ref[...], dt_ref[...])
        dc, du, dal, ddt = vjp((dq[...], dk[...], dv[...], dg[...], db[...]))
        dc_ref[...] = dc
        du_ref[...] = du
        fold = (_iota((LANE, DN_W), 0) == _head_of(_iota((LANE, DN_W), 1))).astype(F32)
        both = jnp.concatenate([dal, ddt, jnp.zeros((6, DN_W), F32)], axis=0)
        dvec_ref[...] += _mmx_nt(both, fold)

    return pl.pallas_call(
        body, name=name, grid=(T // tm,),
        in_specs=[_row_spec(tm, 768), _row_spec(tm, LANE), _full_spec((1, DN_W)), _full_spec((1, DN_W))]
        + [_row_spec(tm, DN_W)] * 5,
        out_specs=[_row_spec(tm, 768), _row_spec(tm, LANE), _full_spec((8, LANE))],
        out_shape=[jax.ShapeDtypeStruct((T, 768), F32), jax.ShapeDtypeStruct((T, LANE), F32),
                   jax.ShapeDtypeStruct((8, LANE), F32)],
        compiler_params=_cp("arbitrary"),
    )(c, uba, alog, dtb, *douts)


def _unit_lower_inverse(lmat):
    eye = (_iota(lmat.shape, 0) == _iota(lmat.shape, 1)).astype(F32)
    tinv = eye - lmat
    pw = lmat
    for _ in range(5):
        pw = _mm3(pw, pw)
        tinv = tinv + _mm3(tinv, pw)
    return tinv


def _inverse_bwd(tinv, d):
    return -_mm3_nt(_mm3_tn(tinv, d), tinv)


@jax.custom_vjp
def _tri_inv(lmat):
    return _unit_lower_inverse(lmat)


def _tri_inv_fwd(lmat):
    tinv = _unit_lower_inverse(lmat)
    return tinv, tinv


_tri_inv.defvjp(_tri_inv_fwd, lambda tinv, d: (_inverse_bwd(tinv, d),))


@jax.custom_vjp
def _tri_inv_known(lmat, tinv):
    return tinv


_tri_inv_known.defvjp(lambda lmat, tinv: (tinv, tinv),
                      lambda tinv, d: (_inverse_bwd(tinv, d), jnp.zeros_like(tinv)))


DN_SUB = 2


def _dn_prep(q, k, v, g, beta, known=None):
    hm = _head_mask((DN_W, DN_W))
    ri = _iota((DN_W, DN_W), 0) & (CHUNK - 1)
    ci = _iota((DN_W, DN_W), 1) & (CHUNK - 1)
    tril = hm * (ri >= ci).astype(F32)
    strict = hm * (ri > ci).astype(F32)
    tri64 = (_iota((CHUNK, CHUNK), 0) >= _iota((CHUNK, CHUNK), 1)).astype(F32)

    def stack(x):
        return jnp.concatenate([x, x, x, x], axis=0) * hm

    gc = _mm3(tri64, g)
    glast = jnp.sum(g, axis=0, keepdims=True)
    eg = jnp.exp(gc)
    kb = k * beta
    qs, ks = stack(q), stack(k)
    gcol = jnp.sum(stack(gc), axis=1, keepdims=True) * (1.0 / HD)
    gmat = jnp.broadcast_to(gcol, (DN_W, DN_W))
    decay = jnp.exp(jnp.minimum(gmat - gmat.T, 0.0))
    lmat = _mm_nt(stack(kb), ks) * decay * strict
    tinv = _tri_inv(lmat) if known is None else _tri_inv_known(lmat, known)
    u = _mm(tinv, stack(v * beta))
    w = _mm(tinv, stack(kb * eg))
    att = _mm_nt(qs, ks) * decay * tril
    return u, w, att, stack(q * eg), stack(k * jnp.exp(glast - gc)), jnp.exp(glast), tinv


def _dn_apply(state, prep):
    u, w, att, qe, kd, eglast, _ = prep
    vn = u - _mm(w, state)
    o4 = _mm(qe, state) + _mm(att, vn)
    o = o4[0:64] + o4[64:128] + o4[128:192] + o4[192:256]
    return o, state * eglast + _mm_tn(kd, vn)


def _dn_chunks(state, q, k, v, g, beta, knowns=None):
    n = q.shape[0] // CHUNK
    rows = lambda x, c: x[c * CHUNK:(c + 1) * CHUNK]
    preps = [_dn_prep(*(rows(x, c) for x in (q, k, v, g, beta)),
                      known=None if knowns is None else knowns[c]) for c in range(n)]
    outs = []
    for prep in preps:
        o, state = _dn_apply(state, prep)
        outs.append(o)
    return jnp.concatenate(outs, axis=0), state, [prep[-1] for prep in preps]


def dn_scan_fwd(q, k, v, g, beta, S, name):
    T = q.shape[0]
    rows = DN_SUB * CHUNK
    ns = S // rows

    def body(q_ref, k_ref, v_ref, g_ref, b_ref, o_ref, st_ref, ti_ref, s_s):
        @pl.when(pl.program_id(1) == 0)
        def _():
            s_s[...] = jnp.zeros_like(s_s)

        st = s_s[...]
        st_ref[0] = st
        o, new, tinvs = _dn_chunks(st, q_ref[...], k_ref[...], v_ref[...], g_ref[...], b_ref[...])
        o_ref[...] = o
        for c, tinv in enumerate(tinvs):
            ti_ref[c] = tinv
        s_s[...] = new

    spec = pl.BlockSpec((rows, DN_W), lambda b, t: (b * ns + t, 0))
    return pl.pallas_call(
        body, name=name, grid=(T // S, ns),
        in_specs=[spec] * 5,
        out_specs=[spec, pl.BlockSpec((1, DN_W, DN_W), lambda b, t: (b * ns + t, 0, 0)),
                   pl.BlockSpec((DN_SUB, DN_W, DN_W), lambda b, t: (b * ns + t, 0, 0))],
        out_shape=[jax.ShapeDtypeStruct((T, DN_W), F32),
                   jax.ShapeDtypeStruct((T // rows, DN_W, DN_W), F32),
                   jax.ShapeDtypeStruct((T // CHUNK, DN_W, DN_W), F32)],
        scratch_shapes=[pltpu.VMEM((DN_W, DN_W), F32)],
        compiler_params=_cp("parallel", "arbitrary"),
    )(q, k, v, g, beta)


def dn_scan_bwd(q, k, v, g, beta, states, tinvs, do, S, name):
    T = q.shape[0]
    rows = DN_SUB * CHUNK
    ns = S // rows

    def body(q_ref, k_ref, v_ref, g_ref, b_ref, st_ref, ti_ref, do_ref, dq, dk, dv, dg, db, ds_s):
        @pl.when(pl.program_id(1) == 0)
        def _():
            ds_s[...] = jnp.zeros_like(ds_s)

        knowns = [ti_ref[c] for c in range(DN_SUB)]
        _, vjp = jax.vjp(lambda *args: _dn_chunks(*args, knowns=knowns)[:2],
                         st_ref[0], q_ref[...], k_ref[...], v_ref[...], g_ref[...], b_ref[...])
        grads = vjp((do_ref[...], ds_s[...]))
        ds_s[...] = grads[0]
        for ref, val in zip((dq, dk, dv, dg, db), grads[1:]):
            ref[...] = val

    spec = pl.BlockSpec((rows, DN_W), lambda b, t: (b * ns + ns - 1 - t, 0))
    return pl.pallas_call(
        body, name=name, grid=(T // S, ns),
        in_specs=[spec] * 5 + [pl.BlockSpec((1, DN_W, DN_W), lambda b, t: (b * ns + ns - 1 - t, 0, 0)),
                               pl.BlockSpec((DN_SUB, DN_W, DN_W), lambda b, t: (b * ns + ns - 1 - t, 0, 0)),
                               spec],
        out_specs=[spec] * 5,
        out_shape=[jax.ShapeDtypeStruct((T, DN_W), F32)] * 5,
        scratch_shapes=[pltpu.VMEM((DN_W, DN_W), F32)],
        compiler_params=_cp("parallel", "arbitrary"),
    )(q, k, v, g, beta, states, tinvs, do)


def _dn_gate(o, z, nl):
    ms = _mmx(o * o, _head_mask((DN_W, DN_W))) * (1.0 / HD)
    return o * lax.rsqrt(ms + EPS) * nl * (z * _sigmoid(z))


def dn_gate_fwd(o, u_dn, nl, name):
    T = o.shape[0]
    tm = min(TOK_TILE, T)

    def body(o_ref, z_ref, n_ref, y_ref):
        y_ref[...] = _dn_gate(o_ref[...], z_ref[...], n_ref[...])

    return pl.pallas_call(
        body, name=name, grid=(T // tm,),
        in_specs=[_row_spec(tm, DN_W), pl.BlockSpec((tm, DN_W), lambda i: (i, 3)), _full_spec((1, DN_W))],
        out_specs=_row_spec(tm, DN_W),
        out_shape=jax.ShapeDtypeStruct((T, DN_W), F32),
        compiler_params=_cp("parallel"),
    )(o, u_dn, nl)


def dn_gate_bwd(o, u_dn, nl, dy, name):
    T = o.shape[0]
    tm = min(TOK_TILE, T)

    def body(o_ref, z_ref, n_ref, dy_ref, do_ref, dz_ref, dn_ref):
        @pl.when(pl.program_id(0) == 0)
        def _():
            dn_ref[...] = jnp.zeros_like(dn_ref)

        _, vjp = jax.vjp(_dn_gate, o_ref[...], z_ref[...], n_ref[...])
        do, dz, dn = vjp(dy_ref[...])
        do_ref[...] = do
        dz_ref[...] = dz
        fold = (_iota((LANE, DN_W), 0) == (_iota((LANE, DN_W), 1) & (HD - 1))).astype(F32)
        dn_ref[...] += _mmx_nt(jnp.concatenate([dn, jnp.zeros((7, DN_W), F32)], axis=0), fold)

    return pl.pallas_call(
        body, name=name, grid=(T // tm,),
        in_specs=[_row_spec(tm, DN_W), pl.BlockSpec((tm, DN_W), lambda i: (i, 3)), _full_spec((1, DN_W)),
                  _row_spec(tm, DN_W)],
        out_specs=[_row_spec(tm, DN_W), _row_spec(tm, DN_W), _full_spec((8, LANE))],
        out_shape=[jax.ShapeDtypeStruct((T, DN_W), F32), jax.ShapeDtypeStruct((T, DN_W), F32),
                   jax.ShapeDtypeStruct((8, LANE), F32)],
        compiler_params=_cp("arbitrary"),
    )(o, u_dn, nl, dy)


Y_SPLITS = (LRU_W, ATT_W, DN_W)
Y_OFFS = (0, LRU_W, LRU_W + ATT_W)


ROWS_DEV = D // N_DEV


def _dev_rows_spec(row0):
    return pl.BlockSpec((N_DEV, ROWS_DEV, D), lambda *_: (0, row0 // ROWS_DEV, 0))


def _dev_rows(w_ref, off, n):
    return w_ref[off // ROWS_DEV:(off + n) // ROWS_DEV].reshape(n, D)


def wout_fwd(h, ys, wb, name):
    T = h.shape[0]
    tm = min(TOK_TILE, T)

    def body(h_ref, y0, y1, y2, w_ref, o_ref, yc_ref):
        acc = h_ref[...]
        for ref, off, n in zip((y0, y1, y2), Y_OFFS, Y_SPLITS):
            y = ref[...].astype(BF16)
            yc_ref[:, off:off + n] = y
            acc += _mm(y, _dev_rows(w_ref, off, n))
        o_ref[...] = acc

    return pl.pallas_call(
        body, name=name, grid=(T // tm,),
        in_specs=[_row_spec(tm, D)] + [_row_spec(tm, n) for n in Y_SPLITS] + [_dev_rows_spec(B_WOUT)],
        out_specs=[_row_spec(tm, D), _row_spec(tm, D)],
        out_shape=[jax.ShapeDtypeStruct((T, D), F32), jax.ShapeDtypeStruct((T, D), BF16)],
        compiler_params=_cp("parallel"),
    )(h, *ys, wb)


def wout_bwd(dy, wb, name):
    T = dy.shape[0]
    tm = min(TOK_TILE, T)

    def body(dy_ref, w_ref, d0, d1, d2):
        dd = dy_ref[...].astype(BF16)
        for ref, off, n in zip((d0, d1, d2), Y_OFFS, Y_SPLITS):
            ref[...] = _mm_nt(dd, _dev_rows(w_ref, off, n))

    return pl.pallas_call(
        body, name=name, grid=(T // tm,),
        in_specs=[_row_spec(tm, D), _dev_rows_spec(B_WOUT)],
        out_specs=[_row_spec(tm, n) for n in Y_SPLITS],
        out_shape=[jax.ShapeDtypeStruct((T, n), F32) for n in Y_SPLITS],
        compiler_params=_cp("parallel"),
    )(dy, wb)


def ple_fwd(h, g, pe, wg, wp, name):
    T = h.shape[0]
    tm = min(TOK_TILE, T)

    def body(h_ref, g_ref, p_ref, wg_ref, wp_ref, o_ref):
        hh = h_ref[...]
        xn = _rms(hh, g_ref[...])[0]
        o_ref[...] = hh + _sigmoid(_mm(xn, _dev_rows(wg_ref, 0, D))) * _mm(p_ref[...], wp_ref[...])

    return pl.pallas_call(
        body, name=name, grid=(T // tm,),
        in_specs=[_row_spec(tm, D), _full_spec((1, D)), _row_spec(tm, PLE), _dev_rows_spec(B_PGATE),
                  _full_spec((PLE, D))],
        out_specs=_row_spec(tm, D),
        out_shape=jax.ShapeDtypeStruct((T, D), F32),
        compiler_params=_cp("parallel"),
    )(h, g, pe, wg, wp)


def ple_bwd(h, dy, g, pe, wg, wp, name):
    T = h.shape[0]
    tm = min(TOK_TILE, T)

    def body(h_ref, dy_ref, g_ref, p_ref, wg_ref, wp_ref, dh_ref, dz_ref, dpp_ref, xn_ref, dn_ref):
        @pl.when(pl.program_id(0) == 0)
        def _():
            dn_ref[...] = jnp.zeros_like(dn_ref)

        gg = g_ref[...]
        dy = dy_ref[...]
        xn, xhat, rstd = _rms(h_ref[...], gg)
        wg = _dev_rows(wg_ref, 0, D)
        gate = _sigmoid(_mm(xn, wg))
        pp = _mm(p_ref[...], wp_ref[...])
        dz = dy * pp * gate * (1.0 - gate)
        dz_ref[...] = dz.astype(BF16)
        dpp_ref[...] = (dy * gate).astype(BF16)
        xn_ref[...] = xn.astype(BF16)
        dh, dn = _rms_bwd(_mm_nt(dz, wg), xhat, rstd, gg)
        dh_ref[...] = dy + dh
        dn_ref[...] += dn

    return pl.pallas_call(
        body, name=name, grid=(T // tm,),
        in_specs=[_row_spec(tm, D), _row_spec(tm, D), _full_spec((1, D)), _row_spec(tm, PLE),
                  _dev_rows_spec(B_PGATE), _full_spec((PLE, D))],
        out_specs=[_row_spec(tm, D), _row_spec(tm, D), _row_spec(tm, D), _row_spec(tm, D), _full_spec((1, D))],
        out_shape=[jax.ShapeDtypeStruct((T, D), F32), jax.ShapeDtypeStruct((T, D), BF16),
                   jax.ShapeDtypeStruct((T, D), BF16), jax.ShapeDtypeStruct((T, D), BF16),
                   jax.ShapeDtypeStruct((1, D), F32)],
        compiler_params=_cp("arbitrary"),
    )(h, dy, g, pe, wg, wp)


def loss_head(h, g, target, name):
    T = h.shape[0]
    tm = min(TOK_TILE, T)

    def body(h_ref, g_ref, t_ref, loss_ref, dh_ref, dn_ref):
        @pl.when(pl.program_id(0) == 0)
        def _():
            dn_ref[...] = jnp.zeros_like(dn_ref)
            loss_ref[...] = jnp.zeros_like(loss_ref)

        gg = g_ref[...]
        y, xhat, rstd = _rms(h_ref[...], gg)
        err = y - t_ref[...]
        per_tok = jnp.mean(err * err, axis=-1, keepdims=True)
        loss_ref[...] += 0.5 * jnp.sum(per_tok, axis=0, keepdims=True)
        dh, dn = _rms_bwd(err * (1.0 / D), xhat, rstd, gg)
        dh_ref[...] = dh
        dn_ref[...] += dn

    return pl.pallas_call(
        body, name=name, grid=(T // tm,),
        in_specs=[_row_spec(tm, D), _full_spec((1, D)), _row_spec(tm, D)],
        out_specs=[_full_spec((8, LANE)), _row_spec(tm, D), _full_spec((1, D))],
        out_shape=[jax.ShapeDtypeStruct((8, LANE), F32), jax.ShapeDtypeStruct((T, D), F32),
                   jax.ShapeDtypeStruct((1, D), F32)],
        compiler_params=_cp("arbitrary"),
    )(h, g, target)


def _block_diag(w):
    return jnp.einsum('hij,hk->hikj', w, jnp.eye(4, dtype=w.dtype)).reshape(LRU_W, LRU_W)


def _layer_consts(W, l):
    row = lambda v: v.reshape(1, -1)
    zeros = jnp.zeros((5, LRU_W), F32)
    return dict(
        wa=_block_diag(W["lru_w_a"][l]), wx=_block_diag(W["lru_w_x"][l]),
        lru_vec=jnp.concatenate([row(W["lru_b_a"][l]), row(W["lru_b_x"][l]), row(W["lru_lambda"][l]), zeros], 0),
        lru_cb=row(W["lru_conv_b"][l]),
        sinks=W["attn_sinks"][l], rel=W["rel_bias"].reshape(-1),
        dn_cb=jnp.zeros((1, 3 * DN_W), F32),
        alog=row(jnp.repeat(W["dn_a_log"][l], HD)), dtb=row(jnp.repeat(W["dn_dt_bias"][l], HD)),
        dn_nl=row(jnp.tile(W["dn_norm"][l], DN_H)),
    )


def _layer_fwd(h0, pe, W, l, S):
    n = f"l{l}_"
    c_ = _layer_consts(W, l)
    row = lambda v: v.reshape(1, -1)
    wa, wb = W["wa"][l], W["wb"][l]
    h1 = ffn_fwd(h0, row(W["ffn1_norm"][l]), wa, wb, 0, n + "ffn1_fwd")
    u_lru, u_att, u_dn, u_ba, xn_mix = mixin_fwd(h1, row(W["mix_norm"][l]), W["w_in"][l], n + "mixin_fwd")
    xr = conv_fwd(u_lru, W["lru_conv_w"][l], c_["lru_cb"], S, 0, LRU_W, n + "lru_conv_fwd")
    y_lru = lru_fwd(xr, u_lru, c_["wa"], c_["wx"], c_["lru_vec"], S, n + "lru_fwd")
    y_att = attn_fwd(u_att, c_["sinks"], c_["rel"], S, n + "attn_fwd")
    cc = conv_fwd(u_dn, W["dn_conv_w"][l], c_["dn_cb"], S, 0, 3 * DN_W, n + "dn_conv_fwd")
    q, k, v, g, beta = dn_point_fwd(cc, u_ba, c_["alog"], c_["dtb"], n + "dn_point_fwd")
    o, states, tinvs = dn_scan_fwd(q, k, v, g, beta, S, n + "dn_scan_fwd")
    y_dn = dn_gate_fwd(o, u_dn, c_["dn_nl"], n + "dn_gate_fwd")
    h2, ycat = wout_fwd(h1, (y_lru, y_att, y_dn), wb, n + "wout_fwd")
    h3 = ffn_fwd(h2, row(W["ffn2_norm"][l]), wa, wb, 1, n + "ffn2_fwd")
    h4 = ple_fwd(h3, row(W["ple_norm"][l]), pe, wb, W["ple_w_proj"][l], n + "ple_fwd")
    saved = dict(h0=h0, h1=h1, h2=h2, h3=h3, u_lru=u_lru, u_att=u_att, u_dn=u_dn, u_ba=u_ba, xn_mix=xn_mix,
                 xr=xr, cc=cc, q=q, k=k, v=v, g=g, beta=beta, o=o, states=states, tinvs=tinvs, ycat=ycat)
    return h4, saved


def _layer_bwd(dh4, sv, pe, W, l, S):
    n = f"l{l}_"
    c_ = _layer_consts(W, l)
    row = lambda v: v.reshape(1, -1)
    wa, wb = W["wa"][l], W["wb"][l]
    G = {"ga": jnp.zeros((4, D, FFP), BF16), "gb": jnp.zeros((N_DEV, B_ROWS, D), BF16)}
    dh3, dz, dpp, xn_p, dn = ple_bwd(sv["h3"], dh4, row(W["ple_norm"][l]), pe, wb, W["ple_w_proj"][l],
                                     n + "ple_bwd")
    G["ple_norm"] = dn[0]
    G["gb"] = grad_rows(xn_p, dz, G["gb"], B_PGATE, ROWS_DEV, n + "d_ple_w_gate")
    d_proj = matmul_tn(pe, dpp, n + "d_ple_w_proj")
    d_proj = d_proj.reshape(PLE, N_DEV, D // N_DEV).transpose(1, 0, 2).reshape(N_DEV, B_ROWS - B_PPROJ, D)
    G["gb"] = lax.dynamic_update_slice(G["gb"], d_proj, (0, B_PPROJ, 0))

    def ffn_back(which, fidx, h_in, dy):
        dh, dgt, dup, act, xn, dn_ = ffn_bwd(h_in, dy, row(W[which + "_norm"][l]), wa, wb, fidx,
                                             n + which + "_bwd")
        G[which + "_norm"] = dn_[0]
        G["ga"] = grad_cols(xn, dgt, G["ga"], 2 * fidx, n + "d_" + which + "_w_gate")
        G["ga"] = grad_cols(xn, dup, G["ga"], 2 * fidx + 1, n + "d_" + which + "_w_up")
        G["gb"] = grad_rows(act, dy, G["gb"], (B_DOWN1, B_DOWN2)[fidx], SHP, n + "d_" + which + "_w_down",
                            scale=0.5)
        return dh

    dh2 = ffn_back("ffn2", 1, sv["h2"], dh3)
    dy_lru, dy_att, dy_dn = wout_bwd(dh2, wb, n + "wout_bwd")
    G["gb"] = grad_rows(sv["ycat"], dh2, G["gb"], B_WOUT, ROWS_DEV, n + "d_w_out")
    do, dz_dn, dnn = dn_gate_bwd(sv["o"], sv["u_dn"], c_["dn_nl"], dy_dn, n + "dn_gate_bwd")
    dqkvgb = dn_scan_bwd(sv["q"], sv["k"], sv["v"], sv["g"], sv["beta"], sv["states"], sv["tinvs"], do, S,
                         n + "dn_scan_bwd")
    dcc, du_ba, dvec_dn = dn_point_bwd(sv["cc"], sv["u_ba"], c_["alog"], c_["dtb"], dqkvgb, n + "dn_point_bwd")
    dqkv, dwb_dn = conv_bwd(sv["u_dn"], dcc, W["dn_conv_w"][l], S, 0, 3 * DN_W, n + "dn_conv_bwd")
    du_dn = jnp.concatenate([dqkv, dz_dn], axis=1)
    G["dn_norm"] = dnn[0, 0:HD]
    G["dn_a_log"] = dvec_dn[0, 0:DN_H]
    G["dn_dt_bias"] = dvec_dn[1, 0:DN_H]
    G["dn_conv_w"] = dwb_dn[0:4]
    du_att, drel, dsk = attn_bwd(sv["u_att"], dy_att, c_["sinks"], c_["rel"], S, n + "attn_bwd")
    G["attn_sinks"] = dsk[:, 0]
    G["rel_bias"] = drel[:, 0:REL_BUCKETS].T
    dxr, dgt_lru, dwa, dwx, dvec = lru_bwd(sv["xr"], sv["u_lru"], dy_lru, c_["wa"], c_["wx"], c_["lru_vec"], S,
                                           n + "lru_bwd")
    dx_lru, dwb_lru = conv_bwd(sv["u_lru"], dxr, W["lru_conv_w"][l], S, 0, LRU_W, n + "lru_conv_bwd")
    du_lru = jnp.concatenate([dx_lru, dgt_lru], axis=1)
    diag = lambda m: jnp.stack([m[c, HD * e:HD * (e + 1), HD * e:HD * (e + 1)] for c in range(2) for e in range(2)])
    G["lru_w_a"], G["lru_w_x"] = diag(dwa), diag(dwx)
    G["lru_b_a"], G["lru_b_x"], G["lru_lambda"] = dvec[0], dvec[1], dvec[2]
    G["lru_conv_w"], G["lru_conv_b"] = dwb_lru[0:4], dwb_lru[4]
    dh1, du_cat, dn = mixin_bwd(sv["h1"], dh2, row(W["mix_norm"][l]), W["w_in"][l],
                                (du_lru, du_att, du_dn, du_ba), n + "mixin_bwd")
    G["mix_norm"] = dn[0]
    d_in = matmul_tn(sv["xn_mix"], du_cat, n + "d_w_in")[:, :D_IN]
    d_in = d_in.reshape(D, N_DEV, D_IN // N_DEV).transpose(1, 0, 2).reshape(N_DEV, WIN_ROWS, D)
    d_in = jnp.pad(d_in, ((0, 0), (0, B_PPROJ - B_WIN - WIN_ROWS), (0, 0)))
    G["gb"] = lax.dynamic_update_slice(G["gb"], d_in, (0, B_WIN, 0))
    dh0 = ffn_back("ffn1", 0, sv["h0"], dh1)
    return dh0, G


def _core(x, pe, W, target, S, weights_at=None, on_grads=None):
    weights_at = weights_at or (lambda l, backward, h: W)
    h = x
    saved = []
    for l in range(DEPTH):
        h, sv = _layer_fwd(h, pe[l], weights_at(l, False, h), l, S)
        saved.append(sv)
    loss_tile, dh, dfn = loss_head(h, W["final_norm"].reshape(1, -1), target, "loss_head")
    grads = [None] * DEPTH
    for l in reversed(range(DEPTH)):
        dh, grads[l] = _layer_bwd(dh, saved[l], pe[l], weights_at(l, True, dh), l, S)
        if on_grads is not None:
            on_grads(l, grads[l])
    return loss_tile[0, 0], dh, grads, dfn[0]


MESH_ID = pl.DeviceIdType.MESH
ANY_SPEC = pl.BlockSpec(memory_space=pl.ANY)
AXES = ("x", "y", "c")


def _my_pos():
    return lax.axis_index("x"), lax.axis_index("y"), lax.axis_index("c")


def _slot_of(px, py, pc):
    return 4 * px + 2 * py + pc


def all_gather(x, name):
    R, C = x.shape

    def body(x_ref, out_ref, send_sems, recv_sems, local_sem):
        mx, my, mc = _my_pos()
        me, sibling = (mx, my, mc), (mx, my, 1 - mc)
        chips = [(1 - mx, my), (mx, 1 - my), (1 - mx, 1 - my)]

        def copy(k, block, to, src=None):
            dst = out_ref.at[_slot_of(*block)]
            return pltpu.make_async_remote_copy(
                src_ref=dst if src is None else src, dst_ref=dst,
                send_sem=send_sems.at[k], recv_sem=recv_sems.at[k],
                device_id=to, device_id_type=MESH_ID)

        mine = pltpu.make_async_copy(x_ref, out_ref.at[_slot_of(*me)], local_sem)
        mine.start()
        first = [copy(0, me, sibling, src=x_ref)]
        first += [copy(1 + j, me, (*chip, mc), src=x_ref) for j, chip in enumerate(chips)]
        for cp in first:
            cp.start()
        passed = [copy(4 + j, (*chip, mc), sibling) for j, chip in enumerate(chips)]
        for j, chip in enumerate(chips):
            copy(1 + j, (*chip, mc), me).wait_recv()
            passed[j].start()
        copy(0, sibling, me).wait_recv()
        for j, chip in enumerate(chips):
            copy(4 + j, (*chip, 1 - mc), me).wait_recv()
        for cp in first + passed:
            cp.wait_send()
        mine.wait()

    return pl.pallas_call(
        body, name=name,
        out_shape=jax.ShapeDtypeStruct((N_DEV, R, C), x.dtype),
        in_specs=[ANY_SPEC], out_specs=ANY_SPEC,
        scratch_shapes=[pltpu.SemaphoreType.DMA((7,)), pltpu.SemaphoreType.DMA((7,)), pltpu.SemaphoreType.DMA],
    )(x)


def _col_window(ref, slot):
    return ref.at[:, pl.ds(pl.multiple_of(slot * SHP, LANE), SHP)]


def gather_layer(a_sh, b_sh, name):
    def body(a_ref, b_ref, ao_ref, bo_ref, send_sems, recv_sems, local_sems):
        mx, my, mc = _my_pos()
        me, sibling = (mx, my, mc), (mx, my, 1 - mc)
        chips = [(1 - mx, my), (mx, 1 - my), (1 - mx, 1 - my)]

        def copies(k, block, to, own=False):
            slot = _slot_of(*block)
            dsts = (_col_window(ao_ref, slot), bo_ref.at[slot])
            srcs = (a_ref, b_ref) if own else dsts
            return [pltpu.make_async_remote_copy(
                src_ref=s, dst_ref=d, send_sem=send_sems.at[2 * k + i], recv_sem=recv_sems.at[2 * k + i],
                device_id=to, device_id_type=MESH_ID) for i, (s, d) in enumerate(zip(srcs, dsts))]

        mine = [pltpu.make_async_copy(a_ref, _col_window(ao_ref, _slot_of(*me)), local_sems.at[0]),
                pltpu.make_async_copy(b_ref, bo_ref.at[_slot_of(*me)], local_sems.at[1])]
        for cp in mine:
            cp.start()
        first = copies(0, me, sibling, own=True)
        for j, chip in enumerate(chips):
            first += copies(1 + j, me, (*chip, mc), own=True)
        for cp in first:
            cp.start()
        passed = []
        for j, chip in enumerate(chips):
            for cp in copies(1 + j, (*chip, mc), me):
                cp.wait_recv()
            fwd = copies(4 + j, (*chip, mc), sibling)
            for cp in fwd:
                cp.start()
            passed += fwd
        for cp in copies(0, sibling, me):
            cp.wait_recv()
        for j, chip in enumerate(chips):
            for cp in copies(4 + j, (*chip, 1 - mc), me):
                cp.wait_recv()
        for cp in first + passed:
            cp.wait_send()
        for cp in mine:
            cp.wait()

    return pl.pallas_call(
        body, name=name,
        out_shape=[jax.ShapeDtypeStruct((a_sh.shape[0], FFP), a_sh.dtype),
                   jax.ShapeDtypeStruct((N_DEV,) + b_sh.shape, b_sh.dtype)],
        in_specs=[ANY_SPEC, ANY_SPEC], out_specs=[ANY_SPEC, ANY_SPEC],
        scratch_shapes=[pltpu.SemaphoreType.DMA((14,)), pltpu.SemaphoreType.DMA((14,)),
                        pltpu.SemaphoreType.DMA((2,))],
    )(a_sh, b_sh)


def exchange_layer(ga, gb, name):
    def body(a_ref, b_ref, ao_ref, bo_ref, send_sems, recv_sems, local_sems):
        mx, my, mc = _my_pos()
        mine = _slot_of(mx, my, mc)
        local = [pltpu.make_async_copy(_col_window(a_ref, mine), ao_ref.at[mine], local_sems.at[0]),
                 pltpu.make_async_copy(b_ref.at[mine], bo_ref.at[mine], local_sems.at[1])]
        for cp in local:
            cp.start()

        def copies(r, src_slot, dst_slot, peer):
            pairs = ((_col_window(a_ref, src_slot), ao_ref.at[dst_slot]), (b_ref.at[src_slot], bo_ref.at[dst_slot]))
            return [pltpu.make_async_remote_copy(
                src_ref=s, dst_ref=d, send_sem=send_sems.at[2 * (r - 1) + i], recv_sem=recv_sems.at[2 * (r - 1) + i],
                device_id=peer, device_id_type=MESH_ID) for i, (s, d) in enumerate(pairs)]

        sent, peers = [], []
        for r in range(1, N_DEV):
            peer = (1 - mx if r & 4 else mx, 1 - my if r & 2 else my, 1 - mc if r & 1 else mc)
            peers.append((r, peer, _slot_of(*peer)))
            sent += copies(r, peers[-1][2], mine, peer)
        for cp in sent:
            cp.start()
        for r, peer, ps in peers:
            for cp in copies(r, ps, ps, peer):
                cp.wait_recv()
        for cp in sent:
            cp.wait_send()
        for cp in local:
            cp.wait()

    return pl.pallas_call(
        body, name=name,
        out_shape=[jax.ShapeDtypeStruct((N_DEV, ga.shape[0], SHP), ga.dtype),
                   jax.ShapeDtypeStruct(gb.shape, gb.dtype)],
        in_specs=[ANY_SPEC, ANY_SPEC], out_specs=[ANY_SPEC, ANY_SPEC],
        scratch_shapes=[pltpu.SemaphoreType.DMA((14,)), pltpu.SemaphoreType.DMA((14,)),
                        pltpu.SemaphoreType.DMA((2,))],
    )(ga, gb)


HBM_SPEC = pl.BlockSpec(memory_space=pltpu.HBM)
SEM_SPEC = pl.BlockSpec(memory_space=pltpu.SEMAPHORE)
N_SPLIT = 2 * (N_DEV - 1)


def _split_views(gathering, a_ref, b_ref, ao_ref, bo_ref, src_slot, dst_slot):
    if gathering:
        return ((a_ref, _col_window(ao_ref, dst_slot)), (b_ref, bo_ref.at[dst_slot]))
    return ((_col_window(a_ref, src_slot), ao_ref.at[dst_slot]), (b_ref.at[src_slot], bo_ref.at[dst_slot]))


def _split_peers():
    mx, my, mc = _my_pos()
    for r in range(1, N_DEV):
        peer = (1 - mx if r & 4 else mx, 1 - my if r & 2 else my, 1 - mc if r & 1 else mc)
        yield r - 1, peer, _slot_of(*peer)


def split_start(gathering, a, b, ao, bo, name):
    def body(a_ref, b_ref, ao_ref, bo_ref, send_sems, recv_sems, a_thru, b_thru, ao_thru, bo_thru, token):
        mine = _slot_of(*_my_pos())
        for k, peer, ps in _split_peers():
            for i, (src, dst) in enumerate(_split_views(gathering, a_ref, b_ref, ao_ref, bo_ref, ps, mine)):
                pltpu.make_async_remote_copy(
                    src_ref=src, dst_ref=dst, send_sem=send_sems.at[2 * k + i], recv_sem=recv_sems.at[2 * k + i],
                    device_id=peer, device_id_type=MESH_ID).start()
        token[...] = jnp.zeros_like(token)

    bufs = (a, b, ao, bo)
    return pl.pallas_call(
        body, name=name,
        out_shape=(pltpu.SemaphoreType.DMA((N_SPLIT,)), pltpu.SemaphoreType.DMA((N_SPLIT,)))
        + tuple(pltpu.HBM(t.shape, t.dtype) for t in bufs) + (jax.ShapeDtypeStruct((8, LANE), F32),),
        in_specs=[HBM_SPEC] * 4,
        out_specs=(SEM_SPEC, SEM_SPEC) + (HBM_SPEC,) * 4 + (pl.BlockSpec(memory_space=pltpu.VMEM),),
        input_output_aliases={0: 2, 1: 3, 2: 4, 3: 5},
        compiler_params=pltpu.CompilerParams(has_side_effects=pltpu.SideEffectType.DATAFLOW_SIDE_EFFECTING),
    )(*(pltpu.with_memory_space_constraint(t, pltpu.HBM) for t in bufs))


def split_wait(gathering, started, after, name):
    send_sems, recv_sems, a, b, ao, bo = started

    def body(a_ref, b_ref, ao_ref, bo_ref, send_sems, recv_sems, after_ref, a_dead, b_dead, ao_out, bo_out):
        mine = _slot_of(*_my_pos())
        for k, peer, ps in _split_peers():
            sends = _split_views(gathering, a_ref, b_ref, ao_ref, bo_ref, ps, mine)
            lands = _split_views(gathering, a_ref, b_ref, ao_ref, bo_ref, mine, ps)
            for i in range(2):
                cp = pltpu.make_async_remote_copy(
                    src_ref=sends[i][0], dst_ref=lands[i][1], send_sem=send_sems.at[2 * k + i],
                    recv_sem=recv_sems.at[2 * k + i], device_id=peer, device_id_type=MESH_ID)
                cp.wait_send()
                cp.wait_recv()

    res = pl.pallas_call(
        body, name=name,
        out_shape=tuple(pltpu.HBM(t.shape, t.dtype) for t in (a, b, ao, bo)),
        in_specs=[HBM_SPEC] * 4 + [SEM_SPEC, SEM_SPEC, pl.BlockSpec(memory_space=pl.ANY)],
        out_specs=(HBM_SPEC,) * 4,
        input_output_aliases={0: 0, 1: 1, 2: 2, 3: 3},
        compiler_params=pltpu.CompilerParams(has_side_effects=pltpu.SideEffectType.DATAFLOW_SIDE_EFFECTING),
    )(a, b, ao, bo, send_sems, recv_sems, after)
    return res[2], res[3]


def sum_parts(parts, name):
    _, R, C = parts.shape
    tr = _pick(R, (512, 336, 272, 256, 128, 64, 32, 16, 8))

    def body(p_ref, o_ref):
        acc = p_ref[0].astype(F32)
        for k in range(1, N_DEV):
            acc += p_ref[k].astype(F32)
        o_ref[...] = acc

    return pl.pallas_call(
        body, name=name, grid=(R // tr,),
        in_specs=[pl.BlockSpec((N_DEV, tr, C), lambda i: (0, i, 0))],
        out_specs=pl.BlockSpec((tr, C), lambda i: (i, 0)),
        out_shape=jax.ShapeDtypeStruct((R, C), F32),
        compiler_params=_cp("parallel"),
    )(parts)


def adamw(g, w, m, v, name):
    R, C = g.shape
    tr = _pick(R, (512, 352, 256, 128, 64, 32, 16, 8))
    c1 = 1.0 - ADAM_B1 ** ADAM_STEP
    c2 = 1.0 - ADAM_B2 ** ADAM_STEP

    def body(g_ref, w_ref, m_ref, v_ref, d_ref, nm_ref, nv_ref):
        gg = g_ref[...]
        mm = ADAM_B1 * m_ref[...] + (1.0 - ADAM_B1) * gg
        vv = ADAM_B2 * v_ref[...] + (1.0 - ADAM_B2) * (gg * gg)
        nm_ref[...] = mm
        nv_ref[...] = vv
        d_ref[...] = -ADAM_LR * ((mm / c1) / (jnp.sqrt(vv / c2) + ADAM_EPS) + ADAM_WD * w_ref[...])

    spec = pl.BlockSpec((tr, C), lambda i: (i, 0))
    return pl.pallas_call(
        body, name=name, grid=(R // tr,),
        in_specs=[spec] * 4, out_specs=[spec] * 3,
        out_shape=[jax.ShapeDtypeStruct((R, C), F32)] * 3,
        compiler_params=_cp("parallel"),
    )(g, w, m, v)


BIG = (("ffn1_w_gate", 1, D, FF), ("ffn1_w_up", 1, D, FF), ("ffn1_w_down", 0, FF, D),
       ("w_in", 1, D, D_IN), ("w_out", 0, D, D),
       ("ffn2_w_gate", 1, D, FF), ("ffn2_w_up", 1, D, FF), ("ffn2_w_down", 0, FF, D),
       ("ple_w_gate", 0, D, D), ("ple_w_proj", 1, PLE, D))
SMALL = (("ffn1_norm", (D,), None), ("mix_norm", (D,), None), ("lru_conv_w", (4, LRU_W), LRU_W // N_DEV),
         ("lru_conv_b", (LRU_W,), None), ("lru_w_a", (4, HD, HD), None), ("lru_b_a", (LRU_W,), None),
         ("lru_w_x", (4, HD, HD), None), ("lru_b_x", (LRU_W,), None), ("lru_lambda", (LRU_W,), None),
         ("attn_sinks", (ATT_H,), None), ("dn_conv_w", (4, 3 * DN_W), 3 * DN_W // N_DEV),
         ("dn_a_log", (DN_H,), None), ("dn_dt_bias", (DN_H,), None), ("dn_norm", (HD,), None),
         ("ffn2_norm", (D,), None), ("ple_norm", (D,), None))
SINGLE = (("rel_bias", (REL_BUCKETS, ATT_H)), ("final_norm", (D,)))


def _pack_rows(arrs, width, mult):
    flat = jnp.concatenate([a.reshape(-1) for a in arrs])
    rows = -(-flat.shape[0] // (width * mult)) * mult
    return jnp.pad(flat, (0, rows * width - flat.shape[0])).reshape(rows, width)


def _unpack_rows(packed, shapes):
    flat = packed.reshape(-1)
    out, off = [], 0
    for s in shapes:
        n = int(np.prod(s))
        out.append(flat[off:off + n].reshape(s))
        off += n
    return out


COL_NAMES = ("ffn1_w_gate", "ffn1_w_up", "ffn2_w_gate", "ffn2_w_up")


def _shard_cols(a, l):
    blk = jnp.concatenate([a[n][l] for n in COL_NAMES], axis=0)
    return jnp.pad(blk, ((0, 0), (0, SHP - SH))).astype(BF16)


def _shard_rows(a, l):
    to = lambda w, r: jnp.pad(w, ((0, r - w.shape[0]), (0, 0)))
    parts = [to(a["ffn1_w_down"][l], SHP), to(a["ffn2_w_down"][l], SHP), a["w_out"][l], a["ple_w_gate"][l],
             to(a["w_in"][l].reshape(WIN_ROWS, D), B_PPROJ - B_WIN), a["ple_w_proj"][l].reshape(-1, D)]
    return jnp.concatenate(parts, axis=0).astype(BF16)


def _full_w_in(wb):
    sh = wb[:, B_WIN:B_WIN + WIN_ROWS, :].reshape(N_DEV, D, D_IN // N_DEV)
    return jnp.pad(sh.transpose(1, 0, 2).reshape(D, D_IN), ((0, 0), (0, D_IN_PAD - D_IN)))


def _full_ple_proj(wb):
    sh = wb[:, B_PPROJ:B_ROWS, :].reshape(N_DEV, PLE, D // N_DEV)
    return sh.transpose(1, 0, 2).reshape(PLE, D)


def _shard_grads(sa, sb):
    g = {n: sa[i * D:(i + 1) * D, :SH] for i, n in enumerate(COL_NAMES)}
    g["ffn1_w_down"] = sb[B_DOWN1:B_DOWN1 + SH]
    g["ffn2_w_down"] = sb[B_DOWN2:B_DOWN2 + SH]
    g["w_out"] = sb[B_WOUT:B_WOUT + ROWS_DEV]
    g["ple_w_gate"] = sb[B_PGATE:B_PGATE + ROWS_DEV]
    g["w_in"] = sb[B_WIN:B_WIN + WIN_ROWS].reshape(D, D_IN // N_DEV)
    g["ple_w_proj"] = sb[B_PPROJ:B_ROWS].reshape(PLE, D // N_DEV)
    return g


def kernel(x, p, ffn1_norm, ffn1_w_gate, ffn1_w_up, ffn1_w_down, mix_norm, w_in, lru_conv_w, lru_conv_b, lru_w_a, lru_b_a, lru_w_x, lru_b_x, lru_lambda, attn_sinks, rel_bias, dn_conv_w, dn_a_log, dn_dt_bias, dn_norm, w_out, ffn2_norm, ffn2_w_gate, ffn2_w_up, ffn2_w_down, ple_norm, ple_w_gate, ple_w_proj, final_norm, loss_target, m_ffn1_norm, m_ffn1_w_gate, m_ffn1_w_up, m_ffn1_w_down, m_mix_norm, m_w_in, m_lru_conv_w, m_lru_conv_b, m_lru_w_a, m_lru_b_a, m_lru_w_x, m_lru_b_x, m_lru_lambda, m_attn_sinks, m_rel_bias, m_dn_conv_w, m_dn_a_log, m_dn_dt_bias, m_dn_norm, m_w_out, m_ffn2_norm, m_ffn2_w_gate, m_ffn2_w_up, m_ffn2_w_down, m_ple_norm, m_ple_w_gate, m_ple_w_proj, m_final_norm, v_ffn1_norm, v_ffn1_w_gate, v_ffn1_w_up, v_ffn1_w_down, v_mix_norm, v_w_in, v_lru_conv_w, v_lru_conv_b, v_lru_w_a, v_lru_b_a, v_lru_w_x, v_lru_b_x, v_lru_lambda, v_attn_sinks, v_rel_bias, v_dn_conv_w, v_dn_a_log, v_dn_dt_bias, v_dn_norm, v_w_out, v_ffn2_norm, v_ffn2_w_gate, v_ffn2_w_up, v_ffn2_w_down, v_ple_norm, v_ple_w_gate, v_ple_w_proj, v_final_norm):
    a = dict(locals())
    nb, S, _ = x.shape
    T = nb * S
    my_slot = _slot_of(*_my_pos())

    W = {"wa": [None] * DEPTH, "wb": [None] * DEPTH, "w_in": [None] * DEPTH, "ple_w_proj": [None] * DEPTH}

    def set_layer_weights(l, wa, wb):
        W["wa"][l], W["wb"][l] = wa, wb
        W["w_in"][l], W["ple_w_proj"][l] = _full_w_in(wb), _full_ple_proj(wb)

    def own_part_in_place(cols, rows, n_cols_slots):
        if n_cols_slots is None:
            ao = lax.dynamic_update_slice(jnp.zeros((cols.shape[0], FFP), BF16), cols, (0, my_slot * SHP))
        else:
            ao = lax.dynamic_update_slice(jnp.zeros((N_DEV,) + cols.shape, BF16), cols[None], (my_slot, 0, 0))
        bo = lax.dynamic_update_slice(jnp.zeros((N_DEV,) + rows.shape, BF16), rows[None], (my_slot, 0, 0))
        return ao, bo

    set_layer_weights(0, *gather_layer(_shard_cols(a, 0), _shard_rows(a, 0), "gather_weights_l0"))
    taps = all_gather(_pack_rows([lru_conv_w, dn_conv_w], LANE, 8), "gather_conv_taps")
    tap_shapes = [lru_conv_w.shape, dn_conv_w.shape]
    lcw, dcw = zip(*[_unpack_rows(taps[k], tap_shapes) for k in range(N_DEV)])
    W["lru_conv_w"] = jnp.concatenate(lcw, axis=-1)
    W["dn_conv_w"] = jnp.concatenate(dcw, axis=-1)
    for name, _, cols in SMALL:
        if cols is None:
            W[name] = a[name]
    W["rel_bias"], W["final_norm"] = rel_bias, final_norm

    cols1, rows1 = _shard_cols(a, 1), _shard_rows(a, 1)
    gather1 = split_start(True, cols1, rows1, *own_part_in_place(cols1, rows1, None), "gather_start_l1")
    W["ffn1_norm"] = ffn1_norm + gather1[6][0, 0]
    flight = {}

    def weights_at(l, backward, h):
        if l == 1 and not backward:
            set_layer_weights(1, *split_wait(True, gather1[:6], h, "gather_wait_l1"))
        if l == 0 and backward:
            return dict(W, ple_norm=ple_norm + flight["grads1"][6][0, 0])
        return W

    def on_grads(l, G):
        if l == 1:
            ga, gb = G["ga"].reshape(4 * D, FFP), G["gb"]
            mine = (lax.dynamic_slice(ga, (0, my_slot * SHP), (4 * D, SHP)),
                    lax.dynamic_slice(gb, (my_slot, 0, 0), (1, B_ROWS, D))[0])
            flight["grads1"] = split_start(False, ga, gb, *own_part_in_place(*mine, N_DEV), "exchange_start_l1")

    loss_local, dx, grads, d_final = _core(x.reshape(T, D), p.reshape(DEPTH, T, PLE), W,
                                           loss_target.reshape(T, D), S, weights_at, on_grads)
    loss = lax.psum(loss_local, AXES)

    received1 = split_wait(False, flight["grads1"][:6], dx, "exchange_wait_l1")
    received = [exchange_layer(grads[0]["ga"].reshape(4 * D, FFP), grads[0]["gb"], "exchange_grads_l0"), received1]
    per_layer = [_shard_grads(sum_parts(ra, f"sum_col_grads_l{l}"), sum_parts(rb, f"sum_row_grads_l{l}"))
                 for l, (ra, rb) in enumerate(received)]
    g_big = {name: jnp.stack([per_layer[l][name] for l in range(DEPTH)]) for name, _, _, _ in BIG}

    small_full = [jnp.stack([grads[l][name] for l in range(DEPTH)]) for name, _, _ in SMALL]
    small_full += [grads[0]["rel_bias"] + grads[1]["rel_bias"], d_final]
    small_sum = sum_parts(all_gather(_pack_rows(small_full, LANE, 8), "gather_small_grads"), "sum_small_grads")
    g_small = dict(zip([n for n, _, _ in SMALL] + [n for n, _ in SINGLE],
                       _unpack_rows(small_sum, [s.shape for s in small_full])))
    for name, _, cols in SMALL:
        if cols is not None:
            g_small[name] = lax.dynamic_slice_in_dim(g_small[name], my_slot * cols, cols, axis=2)

    out = {}
    for name, _, _, _ in BIG:
        shape = a[name].shape
        two_d = lambda t: t.reshape(-1, shape[-1])
        res = adamw(two_d(g_big[name]), two_d(a[name]), two_d(a["m_" + name]), two_d(a["v_" + name]),
                    "adamw_" + name)
        out[name] = (g_big[name],) + tuple(r.reshape(shape) for r in res)
    small_names = [n for n, _, _ in SMALL] + [n for n, _ in SINGLE]
    shapes = [a[n].shape for n in small_names]
    packed = [_pack_rows([a[pre + n] if pre is not None else g_small[n] for n in small_names], LANE, 8)
              for pre in (None, "", "m_", "v_")]
    res = adamw(*packed, "adamw_small")
    unpacked = [_unpack_rows(r, shapes) for r in res]
    for i, n in enumerate(small_names):
        out[n] = (g_small[n].reshape(shapes[i]),) + tuple(u[i] for u in unpacked)

    order = ['ffn1_norm', 'ffn1_w_gate', 'ffn1_w_up', 'ffn1_w_down', 'mix_norm', 'w_in', 'lru_conv_w', 'lru_conv_b',
             'lru_w_a', 'lru_b_a', 'lru_w_x', 'lru_b_x', 'lru_lambda', 'attn_sinks', 'rel_bias', 'dn_conv_w',
             'dn_a_log', 'dn_dt_bias', 'dn_norm', 'w_out', 'ffn2_norm', 'ffn2_w_gate', 'ffn2_w_up', 'ffn2_w_down',
             'ple_norm', 'ple_w_gate', 'ple_w_proj', 'final_norm']
    return (loss, dx.reshape(x.shape)) + tuple(out[n][k] for k in range(4) for n in order)
```

```python
import functools
import math

import numpy as np
import jax
import jax.numpy as jnp
from jax import lax
from jax.experimental import pallas as pl
from jax.experimental.pallas import tpu as pltpu

F32 = jnp.float32
BF16 = jnp.bfloat16
HI = lax.Precision.HIGHEST

D = 1024
DEPTH = 2
EPS = 1e-6
PLE = 256
FF = 2816
HD = 64
LRU_W = 256
LRU_C = 8.0
ATT_W = 512
ATT_H = 8
ATT_KV = 2
ATT_G = 4
KV_W = 128
WINDOW = 128
BQ = 128
REL_BUCKETS = 32
REL_MAX_DIST = 128
DN_W = 256
DN_H = 4
CHUNK = 64
D_IN = 2312
D_IN_PAD = 2432
N_DEV = 8

ADAM_LR = 0.001
ADAM_B1 = 0.9
ADAM_B2 = 0.999
ADAM_EPS = 1e-08
ADAM_WD = 0.01
ADAM_STEP = 10

LANE = 128
VMEM_LIMIT = 56 * 1024 * 1024
SH = FF // N_DEV
SHP = 384
FFP = N_DEV * SHP
FF_TILE = 2 * SHP
TOK_TILE = 512
B_DOWN1, B_DOWN2, B_WOUT, B_PGATE, B_WIN, B_PPROJ, B_ROWS = 0, 384, 768, 896, 1024, 1328, 1360
WIN_ROWS = D * D_IN // N_DEV // 1024
NEG = -1e30


def _cp(*sem):
    return pltpu.CompilerParams(dimension_semantics=tuple(sem), vmem_limit_bytes=VMEM_LIMIT)


def _dg(a, b, ca, cb, exact):
    dims = (((ca,), (cb,)), ((), ()))
    if exact == "f32":
        return lax.dot_general(a.astype(F32), b.astype(F32), dims, precision=HI, preferred_element_type=F32)
    if exact == "split":
        a_hi, b_hi = a.astype(BF16), b.astype(BF16)
        a_lo = (a - a_hi.astype(F32)).astype(BF16)
        b_lo = (b - b_hi.astype(F32)).astype(BF16)
        dot = lambda u, v: lax.dot_general(u, v, dims, preferred_element_type=F32)
        return dot(a_hi, b_hi) + (dot(a_hi, b_lo) + dot(a_lo, b_hi))
    return lax.dot_general(a.astype(BF16), b.astype(BF16), dims, preferred_element_type=F32)


def _make_mm(exact):
    @jax.custom_vjp
    def mm(a, b):
        return _dg(a, b, 1, 0, exact)

    @jax.custom_vjp
    def mm_nt(a, b):
        return _dg(a, b, 1, 1, exact)

    @jax.custom_vjp
    def mm_tn(a, b):
        return _dg(a, b, 0, 0, exact)

    mm.defvjp(lambda a, b: (mm(a, b), (a, b)),
              lambda r, d: (mm_nt(d, r[1]), mm_tn(r[0], d)))
    mm_nt.defvjp(lambda a, b: (mm_nt(a, b), (a, b)),
                 lambda r, d: (mm(d, r[1]), mm_tn(d, r[0])))
    mm_tn.defvjp(lambda a, b: (mm_tn(a, b), (a, b)),
                 lambda r, d: (mm_nt(r[1], d), mm(r[0], d)))
    return mm, mm_nt, mm_tn


_mm, _mm_nt, _mm_tn = _make_mm("bf16")
_mmx, _mmx_nt, _mmx_tn = _make_mm("f32")
_mm3, _mm3_nt, _mm3_tn = _make_mm("split")


def _iota(shape, dim):
    return lax.broadcasted_iota(jnp.int32, shape, dim)


def _sigmoid(x):
    return 1.0 / (1.0 + jnp.exp(-x))


def _rms(h, g):
    rstd = lax.rsqrt(jnp.mean(h * h, axis=-1, keepdims=True) + EPS)
    xhat = h * rstd
    return xhat * g, xhat, rstd


def _rms_bwd(dxn, xhat, rstd, g):
    dxhat = dxn * g
    dh = rstd * (dxhat - xhat * jnp.mean(dxhat * xhat, axis=-1, keepdims=True))
    dg = jnp.sum(dxn * xhat, axis=0, keepdims=True)
    return dh, dg


def _row_spec(tm, n):
    return pl.BlockSpec((tm, n), lambda i, *_: (i, 0))


def _full_spec(shape):
    nd = len(shape)
    return pl.BlockSpec(shape, lambda *_: (0,) * nd)


def _ffn_weight_specs(fidx):
    return [pl.BlockSpec((D, FF_TILE), lambda i, j: (2 * fidx, j)),
            pl.BlockSpec((D, FF_TILE), lambda i, j: (2 * fidx + 1, j)),
            pl.BlockSpec((2, SHP, D), lambda i, j: (j, fidx, 0))]


def ffn_fwd(h, g, wa, wb, fidx, name):
    T = h.shape[0]
    tm = min(TOK_TILE, T)
    nj = FFP // FF_TILE

    def body(h_ref, g_ref, wg_ref, wu_ref, wd_ref, o_ref, xn_s):
        j = pl.program_id(1)

        @pl.when(j == 0)
        def _():
            hh = h_ref[...]
            xn_s[...] = _rms(hh, g_ref[...])[0].astype(BF16)
            o_ref[...] = hh

        xn = xn_s[...]
        gt = _mm(xn, wg_ref[...])
        up = _mm(xn, wu_ref[...])
        act = gt * _sigmoid(gt) * up
        o_ref[...] += 0.5 * _mm(act, wd_ref[...].reshape(FF_TILE, D))

    return pl.pallas_call(
        body, name=name, grid=(T // tm, nj),
        in_specs=[pl.BlockSpec((tm, D), lambda i, j: (i, 0)),
                  pl.BlockSpec((1, D), lambda i, j: (0, 0))] + _ffn_weight_specs(fidx),
        out_specs=pl.BlockSpec((tm, D), lambda i, j: (i, 0)),
        out_shape=jax.ShapeDtypeStruct((T, D), F32),
        scratch_shapes=[pltpu.VMEM((tm, D), BF16)],
        compiler_params=_cp("parallel", "arbitrary"),
    )(h, g, wa, wa, wb)


def ffn_bwd(h, dy, g, wa, wb, fidx, name):
    T = h.shape[0]
    tm = min(TOK_TILE, T)
    nj = FFP // FF_TILE

    def body(h_ref, dy_ref, g_ref, wg_ref, wu_ref, wd_ref,
             dh_ref, dg_ref, du_ref, a_ref, xn_ref, dn_ref, xn_s, dxn_s):
        i = pl.program_id(0)
        j = pl.program_id(1)

        @pl.when(j == 0)
        def _():
            xn = _rms(h_ref[...], g_ref[...])[0].astype(BF16)
            xn_s[...] = xn
            xn_ref[...] = xn
            dxn_s[...] = jnp.zeros_like(dxn_s)

        @pl.when((i == 0) & (j == 0))
        def _():
            dn_ref[...] = jnp.zeros_like(dn_ref)

        xn = xn_s[...]
        gt = _mm(xn, wg_ref[...])
        up = _mm(xn, wu_ref[...])
        sg = _sigmoid(gt)
        si = gt * sg
        da = _mm_nt(0.5 * dy_ref[...], wd_ref[...].reshape(FF_TILE, D))
        dup = da * si
        dgt = da * up * (sg * (1.0 + gt * (1.0 - sg)))
        dg_ref[...] = dgt.astype(BF16)
        du_ref[...] = dup.astype(BF16)
        a_ref[...] = (si * up).astype(BF16)
        dxn_s[...] += _mm_nt(dgt, wg_ref[...]) + _mm_nt(dup, wu_ref[...])

        @pl.when(j == nj - 1)
        def _():
            gg = g_ref[...]
            _, xhat, rstd = _rms(h_ref[...], gg)
            dh, dn = _rms_bwd(dxn_s[...], xhat, rstd, gg)
            dh_ref[...] = dy_ref[...] + dh
            dn_ref[...] += dn

    tile = pl.BlockSpec((tm, FF_TILE), lambda i, j: (i, j))
    return pl.pallas_call(
        body, name=name, grid=(T // tm, nj),
        in_specs=[pl.BlockSpec((tm, D), lambda i, j: (i, 0)),
                  pl.BlockSpec((tm, D), lambda i, j: (i, 0)),
                  pl.BlockSpec((1, D), lambda i, j: (0, 0))] + _ffn_weight_specs(fidx),
        out_specs=[pl.BlockSpec((tm, D), lambda i, j: (i, 0)), tile, tile, tile,
                   pl.BlockSpec((tm, D), lambda i, j: (i, 0)),
                   pl.BlockSpec((1, D), lambda i, j: (0, 0))],
        out_shape=[jax.ShapeDtypeStruct((T, D), F32)] + [jax.ShapeDtypeStruct((T, FFP), BF16)] * 3
        + [jax.ShapeDtypeStruct((T, D), BF16), jax.ShapeDtypeStruct((1, D), F32)],
        scratch_shapes=[pltpu.VMEM((tm, D), BF16), pltpu.VMEM((tm, D), F32)],
        compiler_params=_cp("arbitrary", "arbitrary"),
    )(h, dy, g, wa, wa, wb)


def _pick(n, prefs):
    for t in prefs:
        if n % t == 0:
            return t
    return n


def _tn_body(nk, scale, out_dtype, squeeze):
    def body(a_ref, b_ref, *rest):
        o_ref, acc = rest[-2], rest[-1]
        k = pl.program_id(2)

        @pl.when(k == 0)
        def _():
            acc[...] = jnp.zeros_like(acc)

        acc[...] += _mm_tn(a_ref[...], b_ref[...])

        @pl.when(k == nk - 1)
        def _():
            res = (scale * acc[...]).astype(out_dtype)
            if squeeze:
                o_ref[0] = res
            else:
                o_ref[...] = res

    return body


def matmul_tn(a, b, name, scale=1.0, out_dtype=BF16):
    T, M = a.shape
    N = b.shape[1]
    tmm = _pick(M, (512, 256))
    tnn = _pick(N, (1024, 2432))
    tk = min(TOK_TILE, T)
    nk = T // tk
    return pl.pallas_call(
        _tn_body(nk, scale, out_dtype, False), name=name, grid=(M // tmm, N // tnn, nk),
        in_specs=[pl.BlockSpec((tk, tmm), lambda i, j, k: (k, i)),
                  pl.BlockSpec((tk, tnn), lambda i, j, k: (k, j))],
        out_specs=pl.BlockSpec((tmm, tnn), lambda i, j, k: (i, j)),
        out_shape=jax.ShapeDtypeStruct((M, N), out_dtype),
        scratch_shapes=[pltpu.VMEM((tmm, tnn), F32)],
        compiler_params=_cp("parallel", "parallel", "arbitrary"),
    )(a, b)


def grad_cols(a, b, dst, slot, name):
    T = a.shape[0]
    tmm, tnn = D, FFP // 2
    tk = min(TOK_TILE, T)
    nk = T // tk
    return pl.pallas_call(
        _tn_body(nk, 1.0, BF16, True), name=name, grid=(D // tmm, FFP // tnn, nk),
        in_specs=[pl.BlockSpec((tk, tmm), lambda i, j, k: (k, i)),
                  pl.BlockSpec((tk, tnn), lambda i, j, k: (k, j)),
                  pl.BlockSpec(memory_space=pl.ANY)],
        out_specs=pl.BlockSpec((1, tmm, tnn), lambda i, j, k: (slot, i, j)),
        out_shape=jax.ShapeDtypeStruct(dst.shape, dst.dtype),
        scratch_shapes=[pltpu.VMEM((tmm, tnn), F32)],
        input_output_aliases={2: 0},
        compiler_params=_cp("parallel", "parallel", "arbitrary"),
    )(a, b, dst)


def grad_rows(a, b, dst, row0, rows, name, scale=1.0):
    T = a.shape[0]
    tk = min(TOK_TILE, T)
    nk = T // tk
    blk = row0 // rows

    def body(a_ref, b_ref, dst_ref, o_ref, acc):
        k = pl.program_id(0)

        @pl.when(k == 0)
        def _():
            acc[...] = jnp.zeros_like(acc)

        acc[...] += _mm_tn(a_ref[...], b_ref[...])

        @pl.when(k == nk - 1)
        def _():
            o_ref[...] = (scale * acc[...]).astype(BF16).reshape(N_DEV, rows, D)

    return pl.pallas_call(
        body, name=name, grid=(nk,),
        in_specs=[pl.BlockSpec((tk, N_DEV * rows), lambda k: (k, 0)),
                  pl.BlockSpec((tk, D), lambda k: (k, 0)),
                  pl.BlockSpec(memory_space=pl.ANY)],
        out_specs=pl.BlockSpec((N_DEV, rows, D), lambda k: (0, blk, 0)),
        out_shape=jax.ShapeDtypeStruct(dst.shape, dst.dtype),
        scratch_shapes=[pltpu.VMEM((N_DEV * rows, D), F32)],
        input_output_aliases={2: 0},
        compiler_params=_cp("arbitrary"),
    )(a, b, dst)


U_SPLITS = (512, 768, 1024, 128)
U_OFFS = (0, 512, 1280, 2304)


def mixin_fwd(h, g, w_in, name):
    T = h.shape[0]
    tm = min(TOK_TILE, T)

    def body(h_ref, g_ref, w_ref, u0, u1, u2, u3, xn_ref):
        xn = _rms(h_ref[...], g_ref[...])[0].astype(BF16)
        xn_ref[...] = xn
        u = _mm(xn, w_ref[...])
        for ref, off, n in zip((u0, u1, u2, u3), U_OFFS, U_SPLITS):
            ref[...] = u[:, off:off + n]

    return pl.pallas_call(
        body, name=name, grid=(T // tm,),
        in_specs=[_row_spec(tm, D), _full_spec((1, D)), _full_spec((D, D_IN_PAD))],
        out_specs=[_row_spec(tm, n) for n in U_SPLITS] + [_row_spec(tm, D)],
        out_shape=[jax.ShapeDtypeStruct((T, n), F32) for n in U_SPLITS]
        + [jax.ShapeDtypeStruct((T, D), BF16)],
        compiler_params=_cp("parallel"),
    )(h, g, w_in)


def mixin_bwd(h, dh_in, g, w_in, dus, name):
    T = h.shape[0]
    tm = min(TOK_TILE, T)

    def body(h_ref, dhi_ref, g_ref, w_ref, d0, d1, d2, d3, dh_ref, du_ref, dn_ref):
        @pl.when(pl.program_id(0) == 0)
        def _():
            dn_ref[...] = jnp.zeros_like(dn_ref)

        dxn = jnp.zeros((tm, D), F32)
        for ref, off, n in zip((d0, d1, d2, d3), U_OFFS, U_SPLITS):
            du = ref[...]
            du_ref[:, off:off + n] = du.astype(BF16)
            dxn += _mm_nt(du, w_ref[:, off:off + n])
        gg = g_ref[...]
        _, xhat, rstd = _rms(h_ref[...], gg)
        dh, dn = _rms_bwd(dxn, xhat, rstd, gg)
        dh_ref[...] = dhi_ref[...] + dh
        dn_ref[...] += dn

    return pl.pallas_call(
        body, name=name, grid=(T // tm,),
        in_specs=[_row_spec(tm, D), _row_spec(tm, D), _full_spec((1, D)), _full_spec((D, D_IN_PAD))]
        + [_row_spec(tm, n) for n in U_SPLITS],
        out_specs=[_row_spec(tm, D), _row_spec(tm, D_IN_PAD), _full_spec((1, D))],
        out_shape=[jax.ShapeDtypeStruct((T, D), F32), jax.ShapeDtypeStruct((T, D_IN_PAD), BF16),
                   jax.ShapeDtypeStruct((1, D), F32)],
        compiler_params=_cp("arbitrary"),
    )(h, dh_in, g, w_in, *dus)


def _shift_down(x, s, row):
    if s == 0:
        return x
    return jnp.where(row >= s, pltpu.roll(x, s, 0), 0.0)


def _shift_up(x, s, row):
    if s == 0:
        return x
    n = x.shape[0]
    return jnp.where(row < n - s, pltpu.roll(x, n - s, 0), 0.0)


def conv_fwd(x, w, b, S, col0, C, name):
    T = x.shape[0]
    cb0 = col0 // LANE

    def body(x_ref, w_ref, b_ref, y_ref):
        xx = x_ref[...]
        row = _iota(xx.shape, 0)
        y = xx * w_ref[3:4, :] + b_ref[...]
        for k in range(3):
            y += _shift_down(xx, 3 - k, row) * w_ref[k:k + 1, :]
        y_ref[...] = y

    return pl.pallas_call(
        body, name=name, grid=(T // S, C // LANE),
        in_specs=[pl.BlockSpec((S, LANE), lambda s, c: (s, cb0 + c)),
                  pl.BlockSpec((4, LANE), lambda s, c: (0, c)),
                  pl.BlockSpec((1, LANE), lambda s, c: (0, c))],
        out_specs=pl.BlockSpec((S, LANE), lambda s, c: (s, c)),
        out_shape=jax.ShapeDtypeStruct((T, C), F32),
        compiler_params=_cp("parallel", "parallel"),
    )(x, w, b)


def conv_bwd(x, dy, w, S, col0, C, name):
    T = x.shape[0]
    cb0 = col0 // LANE

    def body(x_ref, dy_ref, w_ref, dx_ref, dwb_ref):
        @pl.when(pl.program_id(1) == 0)
        def _():
            dwb_ref[...] = jnp.zeros_like(dwb_ref)

        xx = x_ref[...]
        dd = dy_ref[...]
        row = _iota(xx.shape, 0)
        dx = dd * w_ref[3:4, :]
        for k in range(3):
            dx += _shift_up(dd, 3 - k, row) * w_ref[k:k + 1, :]
        dx_ref[...] = dx
        for k in range(4):
            dwb_ref[k:k + 1, :] += jnp.sum(dd * _shift_down(xx, 3 - k, row), axis=0, keepdims=True)
        dwb_ref[4:5, :] += jnp.sum(dd, axis=0, keepdims=True)

    return pl.pallas_call(
        body, name=name, grid=(C // LANE, T // S),
        in_specs=[pl.BlockSpec((S, LANE), lambda c, s: (s, cb0 + c)),
                  pl.BlockSpec((S, LANE), lambda c, s: (s, c)),
                  pl.BlockSpec((4, LANE), lambda c, s: (0, c))],
        out_specs=[pl.BlockSpec((S, LANE), lambda c, s: (s, c)),
                   pl.BlockSpec((8, LANE), lambda c, s: (0, c))],
        out_shape=[jax.ShapeDtypeStruct((T, C), F32), jax.ShapeDtypeStruct((8, C), F32)],
        compiler_params=_cp("parallel", "arbitrary"),
    )(x, dy, w)


def _scan(a, b, row):
    n = a.shape[0]
    d = 1
    while d < n:
        keep = row >= d
        b = a * jnp.where(keep, pltpu.roll(b, d, 0), 0.0) + b
        a = a * jnp.where(keep, pltpu.roll(a, d, 0), 1.0)
        d *= 2
    return b


def _rscan(a, b, row):
    n = a.shape[0]
    d = 1
    while d < n:
        keep = row < n - d
        b = a * jnp.where(keep, pltpu.roll(b, n - d, 0), 0.0) + b
        a = a * jnp.where(keep, pltpu.roll(a, n - d, 0), 1.0)
        d *= 2
    return b


GELU_C = math.sqrt(2.0 / math.pi)


def _gelu(x):
    t = jnp.tanh(GELU_C * (x + 0.044715 * (x * x * x)))
    return 0.5 * x * (1.0 + t), t


def _lru_gates(xr, wa, ba, wx, bx, lam):
    r = _sigmoid(_mm(xr, wa) + ba)
    i = _sigmoid(_mm(xr, wx) + bx)
    sp = jnp.maximum(-lam, 0.0) + jnp.log(1.0 + jnp.exp(-jnp.abs(lam)))
    la = -LRU_C * r * sp
    a = jnp.exp(la)
    e2 = a * a
    m = jnp.sqrt(-jnp.tanh(la) * (e2 + 1.0))
    return r, i, sp, a, e2, m


def lru_fwd(xr, u_lru, wa, wx, vec, S, name):
    T = xr.shape[0]

    def body(xr_ref, gt_ref, wa_ref, wx_ref, vec_ref, y_ref):
        x = xr_ref[...]
        row = _iota(x.shape, 0)
        r, i, sp, a, e2, m = _lru_gates(x, wa_ref[...], vec_ref[0:1, :], wx_ref[...], vec_ref[1:2, :],
                                        vec_ref[2:3, :])
        hh = _scan(a, m * (i * x), row)
        y_ref[...] = _gelu(gt_ref[...])[0] * hh

    return pl.pallas_call(
        body, name=name, grid=(T // S, LRU_W // LANE),
        in_specs=[pl.BlockSpec((S, LANE), lambda s, c: (s, c)),
                  pl.BlockSpec((S, LANE), lambda s, c: (s, 2 + c)),
                  pl.BlockSpec((LANE, LANE), lambda s, c: (c, c)),
                  pl.BlockSpec((LANE, LANE), lambda s, c: (c, c)),
                  pl.BlockSpec((8, LANE), lambda s, c: (0, c))],
        out_specs=pl.BlockSpec((S, LANE), lambda s, c: (s, c)),
        out_shape=jax.ShapeDtypeStruct((T, LRU_W), F32),
        compiler_params=_cp("parallel", "parallel"),
    )(xr, u_lru, wa, wx, vec)


def lru_bwd(xr, u_lru, dy, wa, wx, vec, S, name):
    T = xr.shape[0]

    def body(xr_ref, gt_ref, dy_ref, wa_ref, wx_ref, vec_ref,
             dxr_ref, dgt_ref, dwa_ref, dwx_ref, dvec_ref):
        @pl.when(pl.program_id(1) == 0)
        def _():
            dwa_ref[...] = jnp.zeros_like(dwa_ref)
            dwx_ref[...] = jnp.zeros_like(dwx_ref)
            dvec_ref[...] = jnp.zeros_like(dvec_ref)

        x = xr_ref[...]
        n = x.shape[0]
        row = _iota(x.shape, 0)
        lam = vec_ref[2:3, :]
        r, i, sp, a, e2, m = _lru_gates(x, wa_ref[...], vec_ref[0:1, :], wx_ref[...], vec_ref[1:2, :], lam)
        v = i * x
        hh = _scan(a, m * v, row)
        gt = gt_ref[...]
        dy = dy_ref[...]
        ge, t = _gelu(gt)
        dgt_ref[...] = dy * hh * (0.5 * (1.0 + t) + 0.5 * gt * (1.0 - t * t) * GELU_C
                                  * (1.0 + 3.0 * 0.044715 * gt * gt))
        a_next = jnp.where(row < n - 1, pltpu.roll(a, n - 1, 0), 0.0)
        G = _rscan(a_next, dy * ge, row)
        da = G * _shift_down(hh, 1, row)
        dv = G * m
        dla = da * a - (G * v) * e2 / m
        dr = dla * (-LRU_C * sp)
        dsp = jnp.sum(dla * (-LRU_C * r), axis=0, keepdims=True)
        dra = dr * r * (1.0 - r)
        dia = (dv * x) * i * (1.0 - i)
        dxr_ref[...] = dv * i + _mm_nt(dra, wa_ref[...]) + _mm_nt(dia, wx_ref[...])
        dwa_ref[0] += _mm_tn(x, dra)
        dwx_ref[0] += _mm_tn(x, dia)
        dvec_ref[0:1, :] += jnp.sum(dra, axis=0, keepdims=True)
        dvec_ref[1:2, :] += jnp.sum(dia, axis=0, keepdims=True)
        dvec_ref[2:3, :] += dsp * (-_sigmoid(-lam))

    return pl.pallas_call(
        body, name=name, grid=(LRU_W // LANE, T // S),
        in_specs=[pl.BlockSpec((S, LANE), lambda c, s: (s, c)),
                  pl.BlockSpec((S, LANE), lambda c, s: (s, 2 + c)),
                  pl.BlockSpec((S, LANE), lambda c, s: (s, c)),
                  pl.BlockSpec((LANE, LANE), lambda c, s: (c, c)),
                  pl.BlockSpec((LANE, LANE), lambda c, s: (c, c)),
                  pl.BlockSpec((8, LANE), lambda c, s: (0, c))],
        out_specs=[pl.BlockSpec((S, LANE), lambda c, s: (s, c)),
                   pl.BlockSpec((S, LANE), lambda c, s: (s, c)),
                   pl.BlockSpec((1, LANE, LANE), lambda c, s: (c, 0, 0)),
                   pl.BlockSpec((1, LANE, LANE), lambda c, s: (c, 0, 0)),
                   pl.BlockSpec((8, LANE), lambda c, s: (0, c))],
        out_shape=[jax.ShapeDtypeStruct((T, LRU_W), F32), jax.ShapeDtypeStruct((T, LRU_W), F32),
                   jax.ShapeDtypeStruct((2, LANE, LANE), F32), jax.ShapeDtypeStruct((2, LANE, LANE), F32),
                   jax.ShapeDtypeStruct((8, LRU_W), F32)],
        compiler_params=_cp("parallel", "arbitrary"),
    )(xr, u_lru, dy, wa, wx, vec)


def _bucket_table():
    qi = np.arange(BQ)[:, None]
    kj = np.arange(2 * BQ)[None, :]
    dist = BQ + qi - kj
    band = (dist >= 0) & (dist < WINDOW)
    dd = np.maximum(dist, 0)
    max_exact = REL_BUCKETS // 2
    large = max_exact + (np.log(np.maximum(dd, 1).astype(np.float32) / np.float32(max_exact))
                         / np.float32(math.log(REL_MAX_DIST / max_exact))
                         * np.float32(REL_BUCKETS - max_exact)).astype(np.int32)
    large = np.minimum(large, REL_BUCKETS - 1)
    bucket = np.where(dd < max_exact, dd, large)
    return np.where(band, bucket, -1).astype(np.int32)


def _att_specs(S):
    nb = S // BQ
    qc = ATT_W // LANE
    return [pl.BlockSpec((BQ, ATT_W), lambda b, n: (b * nb + n, 0)),
            pl.BlockSpec((BQ, KV_W), lambda b, n: (b * nb + jnp.maximum(n - 1, 0), qc)),
            pl.BlockSpec((BQ, KV_W), lambda b, n: (b * nb + n, qc)),
            pl.BlockSpec((BQ, KV_W), lambda b, n: (b * nb + jnp.maximum(n - 1, 0), qc + 1)),
            pl.BlockSpec((BQ, KV_W), lambda b, n: (b * nb + n, qc + 1))]


def _att_bias(bk, rb_ref, bias_s):
    for h in range(ATT_H):
        acc = jnp.zeros(bk.shape, F32)
        for bb in range(REL_BUCKETS):
            acc = jnp.where(bk == bb, rb_ref[bb * ATT_H + h], acc)
        bias_s[h] = acc


def _att_probs(qh, kg, bias, valid, sink):
    s = _mm_nt(qh, kg) * (HD ** -0.5) + bias
    s = jnp.where(valid, s, NEG)
    m = jnp.maximum(jnp.max(s, axis=-1, keepdims=True), sink)
    e = jnp.exp(s - m)
    es = jnp.exp(sink - m)
    den = jnp.sum(e, axis=-1, keepdims=True) + es
    return e / den, es / den


def attn_fwd(u_att, sinks, rel_bias, S, name):
    T = u_att.shape[0]
    nb = S // BQ
    table = jnp.asarray(_bucket_table())

    def body(sk_ref, rb_ref, bk_ref, q_ref, kp_ref, kc_ref, vp_ref, vc_ref, o_ref, bias_s):
        b = pl.program_id(0)
        n = pl.program_id(1)
        bk = bk_ref[...]

        @pl.when((b == 0) & (n == 0))
        def _():
            _att_bias(bk, rb_ref, bias_s)

        valid = (bk >= 0) & ((n > 0) | (_iota(bk.shape, 1) >= BQ))
        for h in range(ATT_H):
            gs = slice(HD * (h // ATT_G), HD * (h // ATT_G + 1))
            kg = jnp.concatenate([kp_ref[:, gs], kc_ref[:, gs]], axis=0)
            vg = jnp.concatenate([vp_ref[:, gs], vc_ref[:, gs]], axis=0)
            p, _ = _att_probs(q_ref[:, HD * h:HD * (h + 1)], kg, bias_s[h], valid, sk_ref[h])
            o_ref[:, HD * h:HD * (h + 1)] = _mm(p, vg)

    smem = pl.BlockSpec(memory_space=pltpu.SMEM)
    return pl.pallas_call(
        body, name=name, grid=(T // S, nb),
        in_specs=[smem, smem, _full_spec((BQ, 2 * BQ))] + _att_specs(S),
        out_specs=pl.BlockSpec((BQ, ATT_W), lambda b, n: (b * nb + n, 0)),
        out_shape=jax.ShapeDtypeStruct((T, ATT_W), F32),
        scratch_shapes=[pltpu.VMEM((ATT_H, BQ, 2 * BQ), F32)],
        compiler_params=_cp("arbitrary", "arbitrary"),
    )(sinks, rel_bias, table, u_att, u_att, u_att, u_att, u_att)


def attn_bwd(u_att, dy, sinks, rel_bias, S, name):
    T = u_att.shape[0]
    nb = S // BQ
    nB = T // S
    table = jnp.asarray(_bucket_table())
    scale = HD ** -0.5

    def body(sk_ref, rb_ref, bk_ref, q_ref, kp_ref, kc_ref, vp_ref, vc_ref, dy_ref,
             du_ref, drel_ref, dsk_ref, bias_s, dbias_s):
        b = pl.program_id(0)
        n = pl.program_id(1)
        bk = bk_ref[...]

        @pl.when((b == 0) & (n == 0))
        def _():
            _att_bias(bk, rb_ref, bias_s)
            dbias_s[...] = jnp.zeros_like(dbias_s)
            dsk_ref[...] = jnp.zeros_like(dsk_ref)
            drel_ref[...] = jnp.zeros_like(drel_ref)

        @pl.when(n == 0)
        def _():
            du_ref[...] = jnp.zeros_like(du_ref)

        valid = (bk >= 0) & ((n > 0) | (_iota(bk.shape, 1) >= BQ))
        r_cur = pl.multiple_of(n * BQ, BQ)
        r_prev = pl.multiple_of(jnp.maximum(n - 1, 0) * BQ, BQ)
        for g in range(ATT_KV):
            gs = slice(HD * g, HD * (g + 1))
            kg = jnp.concatenate([kp_ref[:, gs], kc_ref[:, gs]], axis=0)
            vg = jnp.concatenate([vp_ref[:, gs], vc_ref[:, gs]], axis=0)
            dk = jnp.zeros((2 * BQ, HD), F32)
            dv = jnp.zeros((2 * BQ, HD), F32)
            for e in range(ATT_G):
                h = g * ATT_G + e
                qh = q_ref[:, HD * h:HD * (h + 1)]
                do = dy_ref[:, HD * h:HD * (h + 1)]
                p, ps = _att_probs(qh, kg, bias_s[h], valid, sk_ref[h])
                dp = _mm_nt(do, vg)
                delta = jnp.sum(p * dp, axis=-1, keepdims=True)
                ds = p * (dp - delta)
                dbias_s[h] += ds
                dsk_ref[h:h + 1, :] += jnp.broadcast_to(
                    jnp.sum(-ps * delta, axis=0, keepdims=True), (1, LANE))
                dss = ds * scale
                du_ref[pl.ds(r_cur, BQ), HD * h:HD * (h + 1)] = _mm(dss, kg)
                dk += _mm_tn(dss, qh)
                dv += _mm_tn(p, do)
            ck = ATT_W + HD * g
            cv = ATT_W + KV_W + HD * g
            du_ref[pl.ds(r_prev, BQ), ck:ck + HD] += dk[0:BQ]
            du_ref[pl.ds(r_cur, BQ), ck:ck + HD] += dk[BQ:]
            du_ref[pl.ds(r_prev, BQ), cv:cv + HD] += dv[0:BQ]
            du_ref[pl.ds(r_cur, BQ), cv:cv + HD] += dv[BQ:]

        @pl.when((b == nB - 1) & (n == nb - 1))
        def _():
            lane = _iota((1, LANE), 1)
            for h in range(ATT_H):
                db = dbias_s[h]
                acc = jnp.zeros((1, LANE), F32)
                for bb in range(REL_BUCKETS):
                    val = jnp.sum(jnp.sum(jnp.where(bk == bb, db, 0.0), axis=1, keepdims=True),
                                  axis=0, keepdims=True)
                    acc = jnp.where(lane == bb, val, acc)
                drel_ref[h:h + 1, :] = acc

    smem = pl.BlockSpec(memory_space=pltpu.SMEM)
    return pl.pallas_call(
        body, name=name, grid=(nB, nb),
        in_specs=[smem, smem, _full_spec((BQ, 2 * BQ))] + _att_specs(S)
        + [pl.BlockSpec((BQ, ATT_W), lambda b, n: (b * nb + n, 0))],
        out_specs=[pl.BlockSpec((S, ATT_W + 2 * KV_W), lambda b, n: (b, 0)),
                   _full_spec((8, LANE)), _full_spec((8, LANE))],
        out_shape=[jax.ShapeDtypeStruct((T, ATT_W + 2 * KV_W), F32),
                   jax.ShapeDtypeStruct((8, LANE), F32), jax.ShapeDtypeStruct((8, LANE), F32)],
        scratch_shapes=[pltpu.VMEM((ATT_H, BQ, 2 * BQ), F32), pltpu.VMEM((ATT_H, BQ, 2 * BQ), F32)],
        compiler_params=_cp("arbitrary", "arbitrary"),
    )(sinks, rel_bias, table, u_att, u_att, u_att, u_att, u_att, dy)


def _head_of(i):
    return lax.shift_right_logical(i, 6)


def _head_mask(shape):
    return (_head_of(_iota(shape, 0)) == _head_of(_iota(shape, 1))).astype(F32)


def _dn_point(c, uba, alog, dtb):
    s = c * _sigmoid(c)
    qt, kt, vt = s[:, 0:256], s[:, 256:512], s[:, 512:768]
    ones_bd = _head_mask((DN_W, DN_W))
    q = qt * lax.rsqrt(_mmx(qt * qt, ones_bd) + EPS) * (HD ** -0.5)
    k = kt * lax.rsqrt(_mmx(kt * kt, ones_bd) + EPS)
    sel = _head_of(_iota((LANE, DN_W), 1))
    row = _iota((LANE, DN_W), 0)
    braw = _mmx(uba, (row == sel).astype(F32))
    araw = _mmx(uba, (row == sel + DN_H).astype(F32)) + dtb
    beta = _sigmoid(braw)
    g = -jnp.exp(alog) * (jnp.maximum(araw, 0.0) + jnp.log(1.0 + jnp.exp(-jnp.abs(araw))))
    return q, k, vt, g, beta


def dn_point_fwd(c, uba, alog, dtb, name):
    T = c.shape[0]
    tm = min(TOK_TILE, T)

    def body(c_ref, u_ref, al_ref, dt_ref, *outs):
        for ref, val in zip(outs, _dn_point(c_ref[...], u_ref[...], al_ref[...], dt_ref[...])):
            ref[...] = val

    return pl.pallas_call(
        body, name=name, grid=(T // tm,),
        in_specs=[_row_spec(tm, 768), _row_spec(tm, LANE), _full_spec((1, DN_W)), _full_spec((1, DN_W))],
        out_specs=[_row_spec(tm, DN_W)] * 5,
        out_shape=[jax.ShapeDtypeStruct((T, DN_W), F32)] * 5,
        compiler_params=_cp("parallel"),
    )(c, uba, alog, dtb)


def dn_point_bwd(c, uba, alog, dtb, douts, name):
    T = c.shape[0]
    tm = min(TOK_TILE, T)

    def body(c_ref, u_ref, al_ref, dt_ref, dq, dk, dv, dg, db, dc_ref, du_ref, dvec_ref):
        @pl.when(pl.program_id(0) == 0)
        def _():
            dvec_ref[...] = jnp.zeros_like(dvec_ref)

        _, vjp = jax.vjp(_dn_point, c_ref[...], u_ref[...], al_ref[...], dt_ref[...])
        dc, du, dal, ddt = vjp((dq[...], dk[...], dv[...], dg[...], db[...]))
        dc_ref[...] = dc
        du_ref[...] = du
        fold = (_iota((LANE, DN_W), 0) == _head_of(_iota((LANE, DN_W), 1))).astype(F32)
        both = jnp.concatenate([dal, ddt, jnp.zeros((6, DN_W), F32)], axis=0)
        dvec_ref[...] += _mmx_nt(both, fold)

    return pl.pallas_call(
        body, name=name, grid=(T // tm,),
        in_specs=[_row_spec(tm, 768), _row_spec(tm, LANE), _full_spec((1, DN_W)), _full_spec((1, DN_W))]
        + [_row_spec(tm, DN_W)] * 5,
        out_specs=[_row_spec(tm, 768), _row_spec(tm, LANE), _full_spec((8, LANE))],
        out_shape=[jax.ShapeDtypeStruct((T, 768), F32), jax.ShapeDtypeStruct((T, LANE), F32),
                   jax.ShapeDtypeStruct((8, LANE), F32)],
        compiler_params=_cp("arbitrary"),
    )(c, uba, alog, dtb, *douts)


def _unit_lower_inverse(lmat):
    eye = (_iota(lmat.shape, 0) == _iota(lmat.shape, 1)).astype(F32)
    tinv = eye - lmat
    pw = lmat
    for _ in range(5):
        pw = _mm3(pw, pw)
        tinv = tinv + _mm3(tinv, pw)
    return tinv


def _inverse_bwd(tinv, d):
    return -_mm3_nt(_mm3_tn(tinv, d), tinv)


@jax.custom_vjp
def _tri_inv(lmat):
    return _unit_lower_inverse(lmat)


def _tri_inv_fwd(lmat):
    tinv = _unit_lower_inverse(lmat)
    return tinv, tinv


_tri_inv.defvjp(_tri_inv_fwd, lambda tinv, d: (_inverse_bwd(tinv, d),))


@jax.custom_vjp
def _tri_inv_known(lmat, tinv):
    return tinv


_tri_inv_known.defvjp(lambda lmat, tinv: (tinv, tinv),
                      lambda tinv, d: (_inverse_bwd(tinv, d), jnp.zeros_like(tinv)))


DN_SUB = 2


def _dn_prep(q, k, v, g, beta, known=None):
    hm = _head_mask((DN_W, DN_W))
    ri = _iota((DN_W, DN_W), 0) & (CHUNK - 1)
    ci = _iota((DN_W, DN_W), 1) & (CHUNK - 1)
    tril = hm * (ri >= ci).astype(F32)
    strict = hm * (ri > ci).astype(F32)
    tri64 = (_iota((CHUNK, CHUNK), 0) >= _iota((CHUNK, CHUNK), 1)).astype(F32)

    def stack(x):
        return jnp.concatenate([x, x, x, x], axis=0) * hm

    gc = _mm3(tri64, g)
    glast = jnp.sum(g, axis=0, keepdims=True)
    eg = jnp.exp(gc)
    kb = k * beta
    qs, ks = stack(q), stack(k)
    gcol = jnp.sum(stack(gc), axis=1, keepdims=True) * (1.0 / HD)
    gmat = jnp.broadcast_to(gcol, (DN_W, DN_W))
    decay = jnp.exp(jnp.minimum(gmat - gmat.T, 0.0))
    lmat = _mm_nt(stack(kb), ks) * decay * strict
    tinv = _tri_inv(lmat) if known is None else _tri_inv_known(lmat, known)
    u = _mm(tinv, stack(v * beta))
    w = _mm(tinv, stack(kb * eg))
    att = _mm_nt(qs, ks) * decay * tril
    return u, w, att, stack(q * eg), stack(k * jnp.exp(glast - gc)), jnp.exp(glast), tinv


def _dn_apply(state, prep):
    u, w, att, qe, kd, eglast, _ = prep
    vn = u - _mm(w, state)
    o4 = _mm(qe, state) + _mm(att, vn)
    o = o4[0:64] + o4[64:128] + o4[128:192] + o4[192:256]
    return o, state * eglast + _mm_tn(kd, vn)


def _dn_chunks(state, q, k, v, g, beta, knowns=None):
    n = q.shape[0] // CHUNK
    rows = lambda x, c: x[c * CHUNK:(c + 1) * CHUNK]
    preps = [_dn_prep(*(rows(x, c) for x in (q, k, v, g, beta)),
                      known=None if knowns is None else knowns[c]) for c in range(n)]
    outs = []
    for prep in preps:
        o, state = _dn_apply(state, prep)
        outs.append(o)
    return jnp.concatenate(outs, axis=0), state, [prep[-1] for prep in preps]


def dn_scan_fwd(q, k, v, g, beta, S, name):
    T = q.shape[0]
    rows = DN_SUB * CHUNK
    ns = S // rows

    def body(q_ref, k_ref, v_ref, g_ref, b_ref, o_ref, st_ref, ti_ref, s_s):
        @pl.when(pl.program_id(1) == 0)
        def _():
            s_s[...] = jnp.zeros_like(s_s)

        st = s_s[...]
        st_ref[0] = st
        o, new, tinvs = _dn_chunks(st, q_ref[...], k_ref[...], v_ref[...], g_ref[...], b_ref[...])
        o_ref[...] = o
        for c, tinv in enumerate(tinvs):
            ti_ref[c] = tinv
        s_s[...] = new

    spec = pl.BlockSpec((rows, DN_W), lambda b, t: (b * ns + t, 0))
    return pl.pallas_call(
        body, name=name, grid=(T // S, ns),
        in_specs=[spec] * 5,
        out_specs=[spec, pl.BlockSpec((1, DN_W, DN_W), lambda b, t: (b * ns + t, 0, 0)),
                   pl.BlockSpec((DN_SUB, DN_W, DN_W), lambda b, t: (b * ns + t, 0, 0))],
        out_shape=[jax.ShapeDtypeStruct((T, DN_W), F32),
                   jax.ShapeDtypeStruct((T // rows, DN_W, DN_W), F32),
                   jax.ShapeDtypeStruct((T // CHUNK, DN_W, DN_W), F32)],
        scratch_shapes=[pltpu.VMEM((DN_W, DN_W), F32)],
        compiler_params=_cp("parallel", "arbitrary"),
    )(q, k, v, g, beta)


def dn_scan_bwd(q, k, v, g, beta, states, tinvs, do, S, name):
    T = q.shape[0]
    rows = DN_SUB * CHUNK
    ns = S // rows

    def body(q_ref, k_ref, v_ref, g_ref, b_ref, st_ref, ti_ref, do_ref, dq, dk, dv, dg, db, ds_s):
        @pl.when(pl.program_id(1) == 0)
        def _():
            ds_s[...] = jnp.zeros_like(ds_s)

        knowns = [ti_ref[c] for c in range(DN_SUB)]
        _, vjp = jax.vjp(lambda *args: _dn_chunks(*args, knowns=knowns)[:2],
                         st_ref[0], q_ref[...], k_ref[...], v_ref[...], g_ref[...], b_ref[...])
        grads = vjp((do_ref[...], ds_s[...]))
        ds_s[...] = grads[0]
        for ref, val in zip((dq, dk, dv, dg, db), grads[1:]):
            ref[...] = val

    spec = pl.BlockSpec((rows, DN_W), lambda b, t: (b * ns + ns - 1 - t, 0))
    return pl.pallas_call(
        body, name=name, grid=(T // S, ns),
        in_specs=[spec] * 5 + [pl.BlockSpec((1, DN_W, DN_W), lambda b, t: (b * ns + ns - 1 - t, 0, 0)),
                               pl.BlockSpec((DN_SUB, DN_W, DN_W), lambda b, t: (b * ns + ns - 1 - t, 0, 0)),
                               spec],
        out_specs=[spec] * 5,
        out_shape=[jax.ShapeDtypeStruct((T, DN_W), F32)] * 5,
        scratch_shapes=[pltpu.VMEM((DN_W, DN_W), F32)],
        compiler_params=_cp("parallel", "arbitrary"),
    )(q, k, v, g, beta, states, tinvs, do)


def _dn_gate(o, z, nl):
    ms = _mmx(o * o, _head_mask((DN_W, DN_W))) * (1.0 / HD)
    return o * lax.rsqrt(ms + EPS) * nl * (z * _sigmoid(z))


def dn_gate_fwd(o, u_dn, nl, name):
    T = o.shape[0]
    tm = min(TOK_TILE, T)

    def body(o_ref, z_ref, n_ref, y_ref):
        y_ref[...] = _dn_gate(o_ref[...], z_ref[...], n_ref[...])

    return pl.pallas_call(
        body, name=name, grid=(T // tm,),
        in_specs=[_row_spec(tm, DN_W), pl.BlockSpec((tm, DN_W), lambda i: (i, 3)), _full_spec((1, DN_W))],
        out_specs=_row_spec(tm, DN_W),
        out_shape=jax.ShapeDtypeStruct((T, DN_W), F32),
        compiler_params=_cp("parallel"),
    )(o, u_dn, nl)


def dn_gate_bwd(o, u_dn, nl, dy, name):
    T = o.shape[0]
    tm = min(TOK_TILE, T)

    def body(o_ref, z_ref, n_ref, dy_ref, do_ref, dz_ref, dn_ref):
        @pl.when(pl.program_id(0) == 0)
        def _():
            dn_ref[...] = jnp.zeros_like(dn_ref)

        _, vjp = jax.vjp(_dn_gate, o_ref[...], z_ref[...], n_ref[...])
        do, dz, dn = vjp(dy_ref[...])
        do_ref[...] = do
        dz_ref[...] = dz
        fold = (_iota((LANE, DN_W), 0) == (_iota((LANE, DN_W), 1) & (HD - 1))).astype(F32)
        dn_ref[...] += _mmx_nt(jnp.concatenate([dn, jnp.zeros((7, DN_W), F32)], axis=0), fold)

    return pl.pallas_call(
        body, name=name, grid=(T // tm,),
        in_specs=[_row_spec(tm, DN_W), pl.BlockSpec((tm, DN_W), lambda i: (i, 3)), _full_spec((1, DN_W)),
                  _row_spec(tm, DN_W)],
        out_specs=[_row_spec(tm, DN_W), _row_spec(tm, DN_W), _full_spec((8, LANE))],
        out_shape=[jax.ShapeDtypeStruct((T, DN_W), F32), jax.ShapeDtypeStruct((T, DN_W), F32),
                   jax.ShapeDtypeStruct((8, LANE), F32)],
        compiler_params=_cp("arbitrary"),
    )(o, u_dn, nl, dy)


Y_SPLITS = (LRU_W, ATT_W, DN_W)
Y_OFFS = (0, LRU_W, LRU_W + ATT_W)


ROWS_DEV = D // N_DEV


def _dev_rows_spec(row0):
    return pl.BlockSpec((N_DEV, ROWS_DEV, D), lambda *_: (0, row0 // ROWS_DEV, 0))


def _dev_rows(w_ref, off, n):
    return w_ref[off // ROWS_DEV:(off + n) // ROWS_DEV].reshape(n, D)


def wout_fwd(h, ys, wb, name):
    T = h.shape[0]
    tm = min(TOK_TILE, T)

    def body(h_ref, y0, y1, y2, w_ref, o_ref, yc_ref):
        acc = h_ref[...]
        for ref, off, n in zip((y0, y1, y2), Y_OFFS, Y_SPLITS):
            y = ref[...].astype(BF16)
            yc_ref[:, off:off + n] = y
            acc += _mm(y, _dev_rows(w_ref, off, n))
        o_ref[...] = acc

    return pl.pallas_call(
        body, name=name, grid=(T // tm,),
        in_specs=[_row_spec(tm, D)] + [_row_spec(tm, n) for n in Y_SPLITS] + [_dev_rows_spec(B_WOUT)],
        out_specs=[_row_spec(tm, D), _row_spec(tm, D)],
        out_shape=[jax.ShapeDtypeStruct((T, D), F32), jax.ShapeDtypeStruct((T, D), BF16)],
        compiler_params=_cp("parallel"),
    )(h, *ys, wb)


def wout_bwd(dy, wb, name):
    T = dy.shape[0]
    tm = min(TOK_TILE, T)

    def body(dy_ref, w_ref, d0, d1, d2):
        dd = dy_ref[...].astype(BF16)
        for ref, off, n in zip((d0, d1, d2), Y_OFFS, Y_SPLITS):
            ref[...] = _mm_nt(dd, _dev_rows(w_ref, off, n))

    return pl.pallas_call(
        body, name=name, grid=(T // tm,),
        in_specs=[_row_spec(tm, D), _dev_rows_spec(B_WOUT)],
        out_specs=[_row_spec(tm, n) for n in Y_SPLITS],
        out_shape=[jax.ShapeDtypeStruct((T, n), F32) for n in Y_SPLITS],
        compiler_params=_cp("parallel"),
    )(dy, wb)


def ple_fwd(h, g, pe, wg, wp, name):
    T = h.shape[0]
    tm = min(TOK_TILE, T)

    def body(h_ref, g_ref, p_ref, wg_ref, wp_ref, o_ref):
        hh = h_ref[...]
        xn = _rms(hh, g_ref[...])[0]
        o_ref[...] = hh + _sigmoid(_mm(xn, _dev_rows(wg_ref, 0, D))) * _mm(p_ref[...], wp_ref[...])

    return pl.pallas_call(
        body, name=name, grid=(T // tm,),
        in_specs=[_row_spec(tm, D), _full_spec((1, D)), _row_spec(tm, PLE), _dev_rows_spec(B_PGATE),
                  _full_spec((PLE, D))],
        out_specs=_row_spec(tm, D),
        out_shape=jax.ShapeDtypeStruct((T, D), F32),
        compiler_params=_cp("parallel"),
    )(h, g, pe, wg, wp)


def ple_bwd(h, dy, g, pe, wg, wp, name):
    T = h.shape[0]
    tm = min(TOK_TILE, T)

    def body(h_ref, dy_ref, g_ref, p_ref, wg_ref, wp_ref, dh_ref, dz_ref, dpp_ref, xn_ref, dn_ref):
        @pl.when(pl.program_id(0) == 0)
        def _():
            dn_ref[...] = jnp.zeros_like(dn_ref)

        gg = g_ref[...]
        dy = dy_ref[...]
        xn, xhat, rstd = _rms(h_ref[...], gg)
        wg = _dev_rows(wg_ref, 0, D)
        gate = _sigmoid(_mm(xn, wg))
        pp = _mm(p_ref[...], wp_ref[...])
        dz = dy * pp * gate * (1.0 - gate)
        dz_ref[...] = dz.astype(BF16)
        dpp_ref[...] = (dy * gate).astype(BF16)
        xn_ref[...] = xn.astype(BF16)
        dh, dn = _rms_bwd(_mm_nt(dz, wg), xhat, rstd, gg)
        dh_ref[...] = dy + dh
        dn_ref[...] += dn

    return pl.pallas_call(
        body, name=name, grid=(T // tm,),
        in_specs=[_row_spec(tm, D), _row_spec(tm, D), _full_spec((1, D)), _row_spec(tm, PLE),
                  _dev_rows_spec(B_PGATE), _full_spec((PLE, D))],
        out_specs=[_row_spec(tm, D), _row_spec(tm, D), _row_spec(tm, D), _row_spec(tm, D), _full_spec((1, D))],
        out_shape=[jax.ShapeDtypeStruct((T, D), F32), jax.ShapeDtypeStruct((T, D), BF16),
                   jax.ShapeDtypeStruct((T, D), BF16), jax.ShapeDtypeStruct((T, D), BF16),
                   jax.ShapeDtypeStruct((1, D), F32)],
        compiler_params=_cp("arbitrary"),
    )(h, dy, g, pe, wg, wp)


def loss_head(h, g, target, name):
    T = h.shape[0]
    tm = min(TOK_TILE, T)

    def body(h_ref, g_ref, t_ref, loss_ref, dh_ref, dn_ref):
        @pl.when(pl.program_id(0) == 0)
        def _():
            dn_ref[...] = jnp.zeros_like(dn_ref)
            loss_ref[...] = jnp.zeros_like(loss_ref)

        gg = g_ref[...]
        y, xhat, rstd = _rms(h_ref[...], gg)
        err = y - t_ref[...]
        per_tok = jnp.mean(err * err, axis=-1, keepdims=True)
        loss_ref[...] += 0.5 * jnp.sum(per_tok, axis=0, keepdims=True)
        dh, dn = _rms_bwd(err * (1.0 / D), xhat, rstd, gg)
        dh_ref[...] = dh
        dn_ref[...] += dn

    return pl.pallas_call(
        body, name=name, grid=(T // tm,),
        in_specs=[_row_spec(tm, D), _full_spec((1, D)), _row_spec(tm, D)],
        out_specs=[_full_spec((8, LANE)), _row_spec(tm, D), _full_spec((1, D))],
        out_shape=[jax.ShapeDtypeStruct((8, LANE), F32), jax.ShapeDtypeStruct((T, D), F32),
                   jax.ShapeDtypeStruct((1, D), F32)],
        compiler_params=_cp("arbitrary"),
    )(h, g, target)


def _block_diag(w):
    return jnp.einsum('hij,hk->hikj', w, jnp.eye(4, dtype=w.dtype)).reshape(LRU_W, LRU_W)


def _layer_consts(W, l):
    row = lambda v: v.reshape(1, -1)
    zeros = jnp.zeros((5, LRU_W), F32)
    return dict(
        wa=_block_diag(W["lru_w_a"][l]), wx=_block_diag(W["lru_w_x"][l]),
        lru_vec=jnp.concatenate([row(W["lru_b_a"][l]), row(W["lru_b_x"][l]), row(W["lru_lambda"][l]), zeros], 0),
        lru_cb=row(W["lru_conv_b"][l]),
        sinks=W["attn_sinks"][l], rel=W["rel_bias"].reshape(-1),
        dn_cb=jnp.zeros((1, 3 * DN_W), F32),
        alog=row(jnp.repeat(W["dn_a_log"][l], HD)), dtb=row(jnp.repeat(W["dn_dt_bias"][l], HD)),
        dn_nl=row(jnp.tile(W["dn_norm"][l], DN_H)),
    )


def _layer_fwd(h0, pe, W, l, S):
    n = f"l{l}_"
    c_ = _layer_consts(W, l)
    row = lambda v: v.reshape(1, -1)
    wa, wb = W["wa"][l], W["wb"][l]
    h1 = ffn_fwd(h0, row(W["ffn1_norm"][l]), wa, wb, 0, n + "ffn1_fwd")
    u_lru, u_att, u_dn, u_ba, xn_mix = mixin_fwd(h1, row(W["mix_norm"][l]), W["w_in"][l], n + "mixin_fwd")
    xr = conv_fwd(u_lru, W["lru_conv_w"][l], c_["lru_cb"], S, 0, LRU_W, n + "lru_conv_fwd")
    y_lru = lru_fwd(xr, u_lru, c_["wa"], c_["wx"], c_["lru_vec"], S, n + "lru_fwd")
    y_att = attn_fwd(u_att, c_["sinks"], c_["rel"], S, n + "attn_fwd")
    cc = conv_fwd(u_dn, W["dn_conv_w"][l], c_["dn_cb"], S, 0, 3 * DN_W, n + "dn_conv_fwd")
    q, k, v, g, beta = dn_point_fwd(cc, u_ba, c_["alog"], c_["dtb"], n + "dn_point_fwd")
    o, states, tinvs = dn_scan_fwd(q, k, v, g, beta, S, n + "dn_scan_fwd")
    y_dn = dn_gate_fwd(o, u_dn, c_["dn_nl"], n + "dn_gate_fwd")
    h2, ycat = wout_fwd(h1, (y_lru, y_att, y_dn), wb, n + "wout_fwd")
    h3 = ffn_fwd(h2, row(W["ffn2_norm"][l]), wa, wb, 1, n + "ffn2_fwd")
    h4 = ple_fwd(h3, row(W["ple_norm"][l]), pe, wb, W["ple_w_proj"][l], n + "ple_fwd")
    saved = dict(h0=h0, h1=h1, h2=h2, h3=h3, u_lru=u_lru, u_att=u_att, u_dn=u_dn, u_ba=u_ba, xn_mix=xn_mix,
                 xr=xr, cc=cc, q=q, k=k, v=v, g=g, beta=beta, o=o, states=states, tinvs=tinvs, ycat=ycat)
    return h4, saved


GE_DOWN2, GE_WOUT, GE_PGATE, GE_WIN, GE_PPROJ, GE_ROWS = 0, 384, 512, 640, 944, 976


def _layer_bwd(dh4, sv, pe, W, l, S, token=None, on_piece=None):
    n = f"l{l}_"
    c_ = _layer_consts(W, l)
    row = lambda v: v.reshape(1, -1)
    behind = lambda v, tok: v if tok is None else v + tok
    on_piece = on_piece or (lambda *_: None)
    wa, wb = W["wa"][l], W["wb"][l]
    G = {"early_cols": jnp.zeros((2, D, FFP), BF16), "early_rows": jnp.zeros((N_DEV, GE_ROWS, D), BF16),
         "late_cols": jnp.zeros((2, D, FFP), BF16), "late_rows": jnp.zeros((N_DEV, SHP, D), BF16)}
    dh3, dz, dpp, xn_p, dn = ple_bwd(sv["h3"], dh4, behind(row(W["ple_norm"][l]), token), pe, wb,
                                     W["ple_w_proj"][l], n + "ple_bwd")
    G["ple_norm"] = dn[0]
    G["early_rows"] = grad_rows(xn_p, dz, G["early_rows"], GE_PGATE, ROWS_DEV, n + "d_ple_w_gate")
    d_proj = matmul_tn(pe, dpp, n + "d_ple_w_proj")
    d_proj = d_proj.reshape(PLE, N_DEV, D // N_DEV).transpose(1, 0, 2).reshape(N_DEV, GE_ROWS - GE_PPROJ, D)
    G["early_rows"] = lax.dynamic_update_slice(G["early_rows"], d_proj, (0, GE_PPROJ, 0))

    def ffn_back(which, fidx, h_in, dy, tok=None):
        piece = ("late", "early")[fidx]
        dh, dgt, dup, act, xn, dn_ = ffn_bwd(h_in, dy, behind(row(W[which + "_norm"][l]), tok), wa, wb, fidx,
                                             n + which + "_bwd")
        G[which + "_norm"] = dn_[0]
        G[piece + "_cols"] = grad_cols(xn, dgt, G[piece + "_cols"], 0, n + "d_" + which + "_w_gate")
        G[piece + "_cols"] = grad_cols(xn, dup, G[piece + "_cols"], 1, n + "d_" + which + "_w_up")
        G[piece + "_rows"] = grad_rows(act, dy, G[piece + "_rows"], 0, SHP, n + "d_" + which + "_w_down",
                                       scale=0.5)
        return dh

    dh2 = ffn_back("ffn2", 1, sv["h2"], dh3)
    dy_lru, dy_att, dy_dn = wout_bwd(dh2, wb, n + "wout_bwd")
    G["early_rows"] = grad_rows(sv["ycat"], dh2, G["early_rows"], GE_WOUT, ROWS_DEV, n + "d_w_out")
    do, dz_dn, dnn = dn_gate_bwd(sv["o"], sv["u_dn"], c_["dn_nl"], dy_dn, n + "dn_gate_bwd")
    dqkvgb = dn_scan_bwd(sv["q"], sv["k"], sv["v"], sv["g"], sv["beta"], sv["states"], sv["tinvs"], do, S,
                         n + "dn_scan_bwd")
    dcc, du_ba, dvec_dn = dn_point_bwd(sv["cc"], sv["u_ba"], c_["alog"], c_["dtb"], dqkvgb, n + "dn_point_bwd")
    dqkv, dwb_dn = conv_bwd(sv["u_dn"], dcc, W["dn_conv_w"][l], S, 0, 3 * DN_W, n + "dn_conv_bwd")
    du_dn = jnp.concatenate([dqkv, dz_dn], axis=1)
    G["dn_norm"] = dnn[0, 0:HD]
    G["dn_a_log"] = dvec_dn[0, 0:DN_H]
    G["dn_dt_bias"] = dvec_dn[1, 0:DN_H]
    G["dn_conv_w"] = dwb_dn[0:4]
    du_att, drel, dsk = attn_bwd(sv["u_att"], dy_att, c_["sinks"], c_["rel"], S, n + "attn_bwd")
    G["attn_sinks"] = dsk[:, 0]
    G["rel_bias"] = drel[:, 0:REL_BUCKETS].T
    dxr, dgt_lru, dwa, dwx, dvec = lru_bwd(sv["xr"], sv["u_lru"], dy_lru, c_["wa"], c_["wx"], c_["lru_vec"], S,
                                           n + "lru_bwd")
    dx_lru, dwb_lru = conv_bwd(sv["u_lru"], dxr, W["lru_conv_w"][l], S, 0, LRU_W, n + "lru_conv_bwd")
    du_lru = jnp.concatenate([dx_lru, dgt_lru], axis=1)
    diag = lambda m: jnp.stack([m[c, HD * e:HD * (e + 1), HD * e:HD * (e + 1)] for c in range(2) for e in range(2)])
    G["lru_w_a"], G["lru_w_x"] = diag(dwa), diag(dwx)
    G["lru_b_a"], G["lru_b_x"], G["lru_lambda"] = dvec[0], dvec[1], dvec[2]
    G["lru_conv_w"], G["lru_conv_b"] = dwb_lru[0:4], dwb_lru[4]
    dh1, du_cat, dn = mixin_bwd(sv["h1"], dh2, row(W["mix_norm"][l]), W["w_in"][l],
                                (du_lru, du_att, du_dn, du_ba), n + "mixin_bwd")
    G["mix_norm"] = dn[0]
    d_in = matmul_tn(sv["xn_mix"], du_cat, n + "d_w_in")[:, :D_IN]
    d_in = d_in.reshape(D, N_DEV, D_IN // N_DEV).transpose(1, 0, 2).reshape(N_DEV, WIN_ROWS, D)
    d_in = jnp.pad(d_in, ((0, 0), (0, GE_PPROJ - GE_WIN - WIN_ROWS), (0, 0)))
    G["early_rows"] = lax.dynamic_update_slice(G["early_rows"], d_in, (0, GE_WIN, 0))
    tok = on_piece(l, "early", G["early_cols"], G["early_rows"])
    dh0 = ffn_back("ffn1", 0, sv["h0"], dh1, tok)
    return dh0, G, on_piece(l, "late", G["late_cols"], G["late_rows"])


def _core(x, pe, W, target, S, weights_at=None, on_piece=None):
    weights_at = weights_at or (lambda l, h: W)
    h = x
    saved = []
    for l in range(DEPTH):
        h, sv = _layer_fwd(h, pe[l], weights_at(l, h), l, S)
        saved.append(sv)
    loss_tile, dh, dfn = loss_head(h, W["final_norm"].reshape(1, -1), target, "loss_head")
    grads = [None] * DEPTH
    token = None
    for l in reversed(range(DEPTH)):
        dh, grads[l], token = _layer_bwd(dh, saved[l], pe[l], W, l, S, token, on_piece)
    return loss_tile[0, 0], dh, grads, dfn[0]


MESH_ID = pl.DeviceIdType.MESH
ANY_SPEC = pl.BlockSpec(memory_space=pl.ANY)
AXES = ("x", "y", "c")


def _my_pos():
    return lax.axis_index("x"), lax.axis_index("y"), lax.axis_index("c")


def _slot_of(px, py, pc):
    return 4 * px + 2 * py + pc


def all_gather(x, name):
    R, C = x.shape

    def body(x_ref, out_ref, send_sems, recv_sems, local_sem):
        mx, my, mc = _my_pos()
        me, sibling = (mx, my, mc), (mx, my, 1 - mc)
        chips = [(1 - mx, my), (mx, 1 - my), (1 - mx, 1 - my)]

        def copy(k, block, to, src=None):
            dst = out_ref.at[_slot_of(*block)]
            return pltpu.make_async_remote_copy(
                src_ref=dst if src is None else src, dst_ref=dst,
                send_sem=send_sems.at[k], recv_sem=recv_sems.at[k],
                device_id=to, device_id_type=MESH_ID)

        mine = pltpu.make_async_copy(x_ref, out_ref.at[_slot_of(*me)], local_sem)
        mine.start()
        first = [copy(0, me, sibling, src=x_ref)]
        first += [copy(1 + j, me, (*chip, mc), src=x_ref) for j, chip in enumerate(chips)]
        for cp in first:
            cp.start()
        passed = [copy(4 + j, (*chip, mc), sibling) for j, chip in enumerate(chips)]
        for j, chip in enumerate(chips):
            copy(1 + j, (*chip, mc), me).wait_recv()
            passed[j].start()
        copy(0, sibling, me).wait_recv()
        for j, chip in enumerate(chips):
            copy(4 + j, (*chip, 1 - mc), me).wait_recv()
        for cp in first + passed:
            cp.wait_send()
        mine.wait()

    return pl.pallas_call(
        body, name=name,
        out_shape=jax.ShapeDtypeStruct((N_DEV, R, C), x.dtype),
        in_specs=[ANY_SPEC], out_specs=ANY_SPEC,
        scratch_shapes=[pltpu.SemaphoreType.DMA((7,)), pltpu.SemaphoreType.DMA((7,)), pltpu.SemaphoreType.DMA],
    )(x)


def _col_window(ref, slot):
    return ref.at[:, pl.ds(pl.multiple_of(slot * SHP, LANE), SHP)]


def gather_layer(a_sh, b_sh, name):
    def body(a_ref, b_ref, ao_ref, bo_ref, send_sems, recv_sems, local_sems):
        mx, my, mc = _my_pos()
        me, sibling = (mx, my, mc), (mx, my, 1 - mc)
        chips = [(1 - mx, my), (mx, 1 - my), (1 - mx, 1 - my)]

        def copies(k, block, to, own=False):
            slot = _slot_of(*block)
            dsts = (_col_window(ao_ref, slot), bo_ref.at[slot])
            srcs = (a_ref, b_ref) if own else dsts
            return [pltpu.make_async_remote_copy(
                src_ref=s, dst_ref=d, send_sem=send_sems.at[2 * k + i], recv_sem=recv_sems.at[2 * k + i],
                device_id=to, device_id_type=MESH_ID) for i, (s, d) in enumerate(zip(srcs, dsts))]

        mine = [pltpu.make_async_copy(a_ref, _col_window(ao_ref, _slot_of(*me)), local_sems.at[0]),
                pltpu.make_async_copy(b_ref, bo_ref.at[_slot_of(*me)], local_sems.at[1])]
        for cp in mine:
            cp.start()
        first = copies(0, me, sibling, own=True)
        for j, chip in enumerate(chips):
            first += copies(1 + j, me, (*chip, mc), own=True)
        for cp in first:
            cp.start()
        passed = []
        for j, chip in enumerate(chips):
            for cp in copies(1 + j, (*chip, mc), me):
                cp.wait_recv()
            fwd = copies(4 + j, (*chip, mc), sibling)
            for cp in fwd:
                cp.start()
            passed += fwd
        for cp in copies(0, sibling, me):
            cp.wait_recv()
        for j, chip in enumerate(chips):
            for cp in copies(4 + j, (*chip, 1 - mc), me):
                cp.wait_recv()
        for cp in first + passed:
            cp.wait_send()
        for cp in mine:
            cp.wait()

    return pl.pallas_call(
        body, name=name,
        out_shape=[jax.ShapeDtypeStruct((a_sh.shape[0], FFP), a_sh.dtype),
                   jax.ShapeDtypeStruct((N_DEV,) + b_sh.shape, b_sh.dtype)],
        in_specs=[ANY_SPEC, ANY_SPEC], out_specs=[ANY_SPEC, ANY_SPEC],
        scratch_shapes=[pltpu.SemaphoreType.DMA((14,)), pltpu.SemaphoreType.DMA((14,)),
                        pltpu.SemaphoreType.DMA((2,))],
    )(a_sh, b_sh)


HBM_SPEC = pl.BlockSpec(memory_space=pltpu.HBM)
SEM_SPEC = pl.BlockSpec(memory_space=pltpu.SEMAPHORE)
N_SPLIT = 2 * (N_DEV - 1)


def _split_views(gathering, a_ref, b_ref, ao_ref, bo_ref, src_slot, dst_slot):
    if gathering:
        return ((a_ref, _col_window(ao_ref, dst_slot)), (b_ref, bo_ref.at[dst_slot]))
    return ((_col_window(a_ref, src_slot), ao_ref.at[dst_slot]), (b_ref.at[src_slot], bo_ref.at[dst_slot]))


def _split_peers():
    mx, my, mc = _my_pos()
    for r in range(1, N_DEV):
        peer = (1 - mx if r & 4 else mx, 1 - my if r & 2 else my, 1 - mc if r & 1 else mc)
        yield r - 1, peer, _slot_of(*peer)


def split_start(gathering, a, b, ao, bo, name):
    def body(a_ref, b_ref, ao_ref, bo_ref, send_sems, recv_sems, a_thru, b_thru, ao_thru, bo_thru, token):
        mine = _slot_of(*_my_pos())
        for k, peer, ps in _split_peers():
            for i, (src, dst) in enumerate(_split_views(gathering, a_ref, b_ref, ao_ref, bo_ref, ps, mine)):
                pltpu.make_async_remote_copy(
                    src_ref=src, dst_ref=dst, send_sem=send_sems.at[2 * k + i], recv_sem=recv_sems.at[2 * k + i],
                    device_id=peer, device_id_type=MESH_ID).start()
        token[...] = jnp.zeros_like(token)

    bufs = (a, b, ao, bo)
    return pl.pallas_call(
        body, name=name,
        out_shape=(pltpu.SemaphoreType.DMA((N_SPLIT,)), pltpu.SemaphoreType.DMA((N_SPLIT,)))
        + tuple(pltpu.HBM(t.shape, t.dtype) for t in bufs) + (jax.ShapeDtypeStruct((8, LANE), F32),),
        in_specs=[HBM_SPEC] * 4,
        out_specs=(SEM_SPEC, SEM_SPEC) + (HBM_SPEC,) * 4 + (pl.BlockSpec(memory_space=pltpu.VMEM),),
        input_output_aliases={0: 2, 1: 3, 2: 4, 3: 5},
        compiler_params=pltpu.CompilerParams(has_side_effects=pltpu.SideEffectType.DATAFLOW_SIDE_EFFECTING),
    )(*(pltpu.with_memory_space_constraint(t, pltpu.HBM) for t in bufs))


def split_wait(gathering, started, after, name):
    send_sems, recv_sems, a, b, ao, bo = started

    def body(a_ref, b_ref, ao_ref, bo_ref, send_sems, recv_sems, after_ref, a_dead, b_dead, ao_out, bo_out):
        mine = _slot_of(*_my_pos())
        for k, peer, ps in _split_peers():
            sends = _split_views(gathering, a_ref, b_ref, ao_ref, bo_ref, ps, mine)
            lands = _split_views(gathering, a_ref, b_ref, ao_ref, bo_ref, mine, ps)
            for i in range(2):
                cp = pltpu.make_async_remote_copy(
                    src_ref=sends[i][0], dst_ref=lands[i][1], send_sem=send_sems.at[2 * k + i],
                    recv_sem=recv_sems.at[2 * k + i], device_id=peer, device_id_type=MESH_ID)
                cp.wait_send()
                cp.wait_recv()

    res = pl.pallas_call(
        body, name=name,
        out_shape=tuple(pltpu.HBM(t.shape, t.dtype) for t in (a, b, ao, bo)),
        in_specs=[HBM_SPEC] * 4 + [SEM_SPEC, SEM_SPEC, pl.BlockSpec(memory_space=pl.ANY)],
        out_specs=(HBM_SPEC,) * 4,
        input_output_aliases={0: 0, 1: 1, 2: 2, 3: 3},
        compiler_params=pltpu.CompilerParams(has_side_effects=pltpu.SideEffectType.DATAFLOW_SIDE_EFFECTING),
    )(a, b, ao, bo, send_sems, recv_sems, after)
    return res[2], res[3]


def sum_parts(parts, name):
    _, R, C = parts.shape
    tr = _pick(R, (512, 336, 272, 256, 128, 64, 32, 16, 8))

    def body(p_ref, o_ref):
        acc = p_ref[0].astype(F32)
        for k in range(1, N_DEV):
            acc += p_ref[k].astype(F32)
        o_ref[...] = acc

    return pl.pallas_call(
        body, name=name, grid=(R // tr,),
        in_specs=[pl.BlockSpec((N_DEV, tr, C), lambda i: (0, i, 0))],
        out_specs=pl.BlockSpec((tr, C), lambda i: (i, 0)),
        out_shape=jax.ShapeDtypeStruct((R, C), F32),
        compiler_params=_cp("parallel"),
    )(parts)


def adamw(g, w, m, v, name):
    R, C = g.shape
    tr = _pick(R, (512, 352, 256, 128, 64, 32, 16, 8))
    c1 = 1.0 - ADAM_B1 ** ADAM_STEP
    c2 = 1.0 - ADAM_B2 ** ADAM_STEP

    def body(g_ref, w_ref, m_ref, v_ref, d_ref, nm_ref, nv_ref):
        gg = g_ref[...]
        mm = ADAM_B1 * m_ref[...] + (1.0 - ADAM_B1) * gg
        vv = ADAM_B2 * v_ref[...] + (1.0 - ADAM_B2) * (gg * gg)
        nm_ref[...] = mm
        nv_ref[...] = vv
        d_ref[...] = -ADAM_LR * ((mm / c1) / (jnp.sqrt(vv / c2) + ADAM_EPS) + ADAM_WD * w_ref[...])

    spec = pl.BlockSpec((tr, C), lambda i: (i, 0))
    return pl.pallas_call(
        body, name=name, grid=(R // tr,),
        in_specs=[spec] * 4, out_specs=[spec] * 3,
        out_shape=[jax.ShapeDtypeStruct((R, C), F32)] * 3,
        compiler_params=_cp("parallel"),
    )(g, w, m, v)


BIG = (("ffn1_w_gate", 1, D, FF), ("ffn1_w_up", 1, D, FF), ("ffn1_w_down", 0, FF, D),
       ("w_in", 1, D, D_IN), ("w_out", 0, D, D),
       ("ffn2_w_gate", 1, D, FF), ("ffn2_w_up", 1, D, FF), ("ffn2_w_down", 0, FF, D),
       ("ple_w_gate", 0, D, D), ("ple_w_proj", 1, PLE, D))
SMALL = (("ffn1_norm", (D,), None), ("mix_norm", (D,), None), ("lru_conv_w", (4, LRU_W), LRU_W // N_DEV),
         ("lru_conv_b", (LRU_W,), None), ("lru_w_a", (4, HD, HD), None), ("lru_b_a", (LRU_W,), None),
         ("lru_w_x", (4, HD, HD), None), ("lru_b_x", (LRU_W,), None), ("lru_lambda", (LRU_W,), None),
         ("attn_sinks", (ATT_H,), None), ("dn_conv_w", (4, 3 * DN_W), 3 * DN_W // N_DEV),
         ("dn_a_log", (DN_H,), None), ("dn_dt_bias", (DN_H,), None), ("dn_norm", (HD,), None),
         ("ffn2_norm", (D,), None), ("ple_norm", (D,), None))
SINGLE = (("rel_bias", (REL_BUCKETS, ATT_H)), ("final_norm", (D,)))


def _pack_rows(arrs, width, mult):
    flat = jnp.concatenate([a.reshape(-1) for a in arrs])
    rows = -(-flat.shape[0] // (width * mult)) * mult
    return jnp.pad(flat, (0, rows * width - flat.shape[0])).reshape(rows, width)


def _unpack_rows(packed, shapes):
    flat = packed.reshape(-1)
    out, off = [], 0
    for s in shapes:
        n = int(np.prod(s))
        out.append(flat[off:off + n].reshape(s))
        off += n
    return out


COL_NAMES = ("ffn1_w_gate", "ffn1_w_up", "ffn2_w_gate", "ffn2_w_up")


def _shard_cols(a, l):
    blk = jnp.concatenate([a[n][l] for n in COL_NAMES], axis=0)
    return jnp.pad(blk, ((0, 0), (0, SHP - SH))).astype(BF16)


def _shard_rows(a, l):
    to = lambda w, r: jnp.pad(w, ((0, r - w.shape[0]), (0, 0)))
    parts = [to(a["ffn1_w_down"][l], SHP), to(a["ffn2_w_down"][l], SHP), a["w_out"][l], a["ple_w_gate"][l],
             to(a["w_in"][l].reshape(WIN_ROWS, D), B_PPROJ - B_WIN), a["ple_w_proj"][l].reshape(-1, D)]
    return jnp.concatenate(parts, axis=0).astype(BF16)


def _full_w_in(wb):
    sh = wb[:, B_WIN:B_WIN + WIN_ROWS, :].reshape(N_DEV, D, D_IN // N_DEV)
    return jnp.pad(sh.transpose(1, 0, 2).reshape(D, D_IN), ((0, 0), (0, D_IN_PAD - D_IN)))


def _full_ple_proj(wb):
    sh = wb[:, B_PPROJ:B_ROWS, :].reshape(N_DEV, PLE, D // N_DEV)
    return sh.transpose(1, 0, 2).reshape(PLE, D)


PIECE_NAMES = {"late": ("ffn1_w_gate", "ffn1_w_up", "ffn1_w_down"),
               "early": ("ffn2_w_gate", "ffn2_w_up", "ffn2_w_down", "w_out", "ple_w_gate", "w_in", "ple_w_proj")}


def _shard_grads(piece, cols, rows):
    ffn = "ffn1" if piece == "late" else "ffn2"
    g = {ffn + "_w_gate": cols[:D, :SH], ffn + "_w_up": cols[D:, :SH], ffn + "_w_down": rows[:SH]}
    if piece == "early":
        g["w_out"] = rows[GE_WOUT:GE_WOUT + ROWS_DEV]
        g["ple_w_gate"] = rows[GE_PGATE:GE_PGATE + ROWS_DEV]
        g["w_in"] = rows[GE_WIN:GE_WIN + WIN_ROWS].reshape(D, D_IN // N_DEV)
        g["ple_w_proj"] = rows[GE_PPROJ:GE_ROWS].reshape(PLE, D // N_DEV)
    return g


def kernel(x, p, ffn1_norm, ffn1_w_gate, ffn1_w_up, ffn1_w_down, mix_norm, w_in, lru_conv_w, lru_conv_b, lru_w_a, lru_b_a, lru_w_x, lru_b_x, lru_lambda, attn_sinks, rel_bias, dn_conv_w, dn_a_log, dn_dt_bias, dn_norm, w_out, ffn2_norm, ffn2_w_gate, ffn2_w_up, ffn2_w_down, ple_norm, ple_w_gate, ple_w_proj, final_norm, loss_target, m_ffn1_norm, m_ffn1_w_gate, m_ffn1_w_up, m_ffn1_w_down, m_mix_norm, m_w_in, m_lru_conv_w, m_lru_conv_b, m_lru_w_a, m_lru_b_a, m_lru_w_x, m_lru_b_x, m_lru_lambda, m_attn_sinks, m_rel_bias, m_dn_conv_w, m_dn_a_log, m_dn_dt_bias, m_dn_norm, m_w_out, m_ffn2_norm, m_ffn2_w_gate, m_ffn2_w_up, m_ffn2_w_down, m_ple_norm, m_ple_w_gate, m_ple_w_proj, m_final_norm, v_ffn1_norm, v_ffn1_w_gate, v_ffn1_w_up, v_ffn1_w_down, v_mix_norm, v_w_in, v_lru_conv_w, v_lru_conv_b, v_lru_w_a, v_lru_b_a, v_lru_w_x, v_lru_b_x, v_lru_lambda, v_attn_sinks, v_rel_bias, v_dn_conv_w, v_dn_a_log, v_dn_dt_bias, v_dn_norm, v_w_out, v_ffn2_norm, v_ffn2_w_gate, v_ffn2_w_up, v_ffn2_w_down, v_ple_norm, v_ple_w_gate, v_ple_w_proj, v_final_norm):
    a = dict(locals())
    nb, S, _ = x.shape
    T = nb * S
    my_slot = _slot_of(*_my_pos())

    W = {"wa": [None] * DEPTH, "wb": [None] * DEPTH, "w_in": [None] * DEPTH, "ple_w_proj": [None] * DEPTH}

    def set_layer_weights(l, wa, wb):
        W["wa"][l], W["wb"][l] = wa, wb
        W["w_in"][l], W["ple_w_proj"][l] = _full_w_in(wb), _full_ple_proj(wb)

    def own_part_in_place(cols, rows, n_cols_slots):
        if n_cols_slots is None:
            ao = lax.dynamic_update_slice(jnp.zeros((cols.shape[0], FFP), BF16), cols, (0, my_slot * SHP))
        else:
            ao = lax.dynamic_update_slice(jnp.zeros((N_DEV,) + cols.shape, BF16), cols[None], (my_slot, 0, 0))
        bo = lax.dynamic_update_slice(jnp.zeros((N_DEV,) + rows.shape, BF16), rows[None], (my_slot, 0, 0))
        return ao, bo

    set_layer_weights(0, *gather_layer(_shard_cols(a, 0), _shard_rows(a, 0), "gather_weights_l0"))
    taps = all_gather(_pack_rows([lru_conv_w, dn_conv_w], LANE, 8), "gather_conv_taps")
    tap_shapes = [lru_conv_w.shape, dn_conv_w.shape]
    lcw, dcw = zip(*[_unpack_rows(taps[k], tap_shapes) for k in range(N_DEV)])
    W["lru_conv_w"] = jnp.concatenate(lcw, axis=-1)
    W["dn_conv_w"] = jnp.concatenate(dcw, axis=-1)
    for name, _, cols in SMALL:
        if cols is None:
            W[name] = a[name]
    W["rel_bias"], W["final_norm"] = rel_bias, final_norm

    cols1, rows1, _ = lax.optimization_barrier((_shard_cols(a, 1), _shard_rows(a, 1), W["wb"][0]))
    gather1 = split_start(True, cols1, rows1, *own_part_in_place(cols1, rows1, None), "gather_start_l1")
    W["ffn1_norm"] = ffn1_norm + gather1[6][0, 0]
    flight = {}

    def weights_at(l, h):
        if l == 1:
            set_layer_weights(1, *split_wait(True, gather1[:6], h, "gather_wait_l1"))
        return W

    def on_piece(l, piece, cols, rows):
        cols = cols.reshape(2 * D, FFP)
        mine = (lax.dynamic_slice(cols, (0, my_slot * SHP), (2 * D, SHP)),
                lax.dynamic_slice(rows, (my_slot, 0, 0), (1,) + rows.shape[1:])[0])
        flight[l, piece] = split_start(False, cols, rows, *own_part_in_place(*mine, N_DEV),
                                       f"exchange_start_l{l}_{piece}")
        return flight[l, piece][6][0, 0]

    loss_local, dx, grads, d_final = _core(x.reshape(T, D), p.reshape(DEPTH, T, PLE), W,
                                           loss_target.reshape(T, D), S, weights_at, on_piece)
    loss = lax.psum(loss_local, AXES)

    out = {}

    def update(piece, received):
        shards = [_shard_grads(piece, sum_parts(ra, f"sum_col_grads_l{l}_{piece}"),
                               sum_parts(rb, f"sum_row_grads_l{l}_{piece}")) for l, (ra, rb) in enumerate(received)]
        for name in PIECE_NAMES[piece]:
            g = jnp.stack([shards[l][name] for l in range(DEPTH)])
            shape = a[name].shape
            two_d = lambda t: t.reshape(-1, shape[-1])
            res = adamw(two_d(g), two_d(a[name]), two_d(a["m_" + name]), two_d(a["v_" + name]), "adamw_" + name)
            out[name] = (g,) + tuple(r.reshape(shape) for r in res)

    landed = {key: split_wait(False, flight[key][:6], dx, f"exchange_wait_l{key[0]}_{key[1]}")
              for key in ((1, "early"), (1, "late"), (0, "early"))}
    update("early", [landed[0, "early"], landed[1, "early"]])
    landed[0, "late"] = split_wait(False, flight[0, "late"][:6], out["ffn2_w_gate"][1], "exchange_wait_l0_late")
    update("late", [landed[0, "late"], landed[1, "late"]])

    small_full = [jnp.stack([grads[l][name] for l in range(DEPTH)]) for name, _, _ in SMALL]
    small_full += [grads[0]["rel_bias"] + grads[1]["rel_bias"], d_final]
    small_sum = sum_parts(all_gather(_pack_rows(small_full, LANE, 8), "gather_small_grads"), "sum_small_grads")
    g_small = dict(zip([n for n, _, _ in SMALL] + [n for n, _ in SINGLE],
                       _unpack_rows(small_sum, [s.shape for s in small_full])))
    for name, _, cols in SMALL:
        if cols is not None:
            g_small[name] = lax.dynamic_slice_in_dim(g_small[name], my_slot * cols, cols, axis=2)

    small_names = [n for n, _, _ in SMALL] + [n for n, _ in SINGLE]
    shapes = [a[n].shape for n in small_names]
    packed = [_pack_rows([a[pre + n] if pre is not None else g_small[n] for n in small_names], LANE, 8)
              for pre in (None, "", "m_", "v_")]
    res = adamw(*packed, "adamw_small")
    unpacked = [_unpack_rows(r, shapes) for r in res]
    for i, n in enumerate(small_names):
        out[n] = (g_small[n].reshape(shapes[i]),) + tuple(u[i] for u in unpacked)

    order = ['ffn1_norm', 'ffn1_w_gate', 'ffn1_w_up', 'ffn1_w_down', 'mix_norm', 'w_in', 'lru_conv_w', 'lru_conv_b',
             'lru_w_a', 'lru_b_a', 'lru_w_x', 'lru_b_x', 'lru_lambda', 'attn_sinks', 'rel_bias', 'dn_conv_w',
             'dn_a_log', 'dn_dt_bias', 'dn_norm', 'w_out', 'ffn2_norm', 'ffn2_w_gate', 'ffn2_w_up', 'ffn2_w_down',
             'ple_norm', 'ple_w_gate', 'ple_w_proj', 'final_norm']
    return (loss, dx.reshape(x.shape)) + tuple(out[n][k] for k in range(4) for n in order)
```

```python
import functools
import math

import numpy as np
import jax
import jax.numpy as jnp
from jax import lax
from jax.experimental import pallas as pl
from jax.experimental.pallas import tpu as pltpu

F32 = jnp.float32
BF16 = jnp.bfloat16
HI = lax.Precision.HIGHEST

D = 1024
DEPTH = 2
EPS = 1e-6
PLE = 256
FF = 2816
HD = 64
LRU_W = 256
LRU_C = 8.0
ATT_W = 512
ATT_H = 8
ATT_KV = 2
ATT_G = 4
KV_W = 128
WINDOW = 128
BQ = 128
REL_BUCKETS = 32
REL_MAX_DIST = 128
DN_W = 256
DN_H = 4
CHUNK = 64
D_IN = 2312
D_IN_PAD = 2432
N_DEV = 8

ADAM_LR = 0.001
ADAM_B1 = 0.9
ADAM_B2 = 0.999
ADAM_EPS = 1e-08
ADAM_WD = 0.01
ADAM_STEP = 10

LANE = 128
VMEM_LIMIT = 56 * 1024 * 1024
SH = FF // N_DEV
SHP = 384
FFP = N_DEV * SHP
FF_TILE = 2 * SHP
TOK_TILE = 512
B_DOWN1, B_DOWN2, B_WOUT, B_PGATE, B_WIN, B_PPROJ, B_ROWS = 0, 384, 768, 896, 1024, 1328, 1360
WIN_ROWS = D * D_IN // N_DEV // 1024
NEG = -1e30


def _cp(*sem):
    return pltpu.CompilerParams(dimension_semantics=tuple(sem), vmem_limit_bytes=VMEM_LIMIT)


def _dg(a, b, ca, cb, exact):
    dims = (((ca,), (cb,)), ((), ()))
    if exact == "f32":
        return lax.dot_general(a.astype(F32), b.astype(F32), dims, precision=HI, preferred_element_type=F32)
    if exact == "split":
        a_hi, b_hi = a.astype(BF16), b.astype(BF16)
        a_lo = (a - a_hi.astype(F32)).astype(BF16)
        b_lo = (b - b_hi.astype(F32)).astype(BF16)
        dot = lambda u, v: lax.dot_general(u, v, dims, preferred_element_type=F32)
        return dot(a_hi, b_hi) + (dot(a_hi, b_lo) + dot(a_lo, b_hi))
    return lax.dot_general(a.astype(BF16), b.astype(BF16), dims, preferred_element_type=F32)


def _make_mm(exact):
    @jax.custom_vjp
    def mm(a, b):
        return _dg(a, b, 1, 0, exact)

    @jax.custom_vjp
    def mm_nt(a, b):
        return _dg(a, b, 1, 1, exact)

    @jax.custom_vjp
    def mm_tn(a, b):
        return _dg(a, b, 0, 0, exact)

    mm.defvjp(lambda a, b: (mm(a, b), (a, b)),
              lambda r, d: (mm_nt(d, r[1]), mm_tn(r[0], d)))
    mm_nt.defvjp(lambda a, b: (mm_nt(a, b), (a, b)),
                 lambda r, d: (mm(d, r[1]), mm_tn(d, r[0])))
    mm_tn.defvjp(lambda a, b: (mm_tn(a, b), (a, b)),
                 lambda r, d: (mm_nt(r[1], d), mm(r[0], d)))
    return mm, mm_nt, mm_tn


_mm, _mm_nt, _mm_tn = _make_mm("bf16")
_mmx, _mmx_nt, _mmx_tn = _make_mm("f32")
_mm3, _mm3_nt, _mm3_tn = _make_mm("split")


def _iota(shape, dim):
    return lax.broadcasted_iota(jnp.int32, shape, dim)


def _sigmoid(x):
    return 1.0 / (1.0 + jnp.exp(-x))


def _rms(h, g):
    rstd = lax.rsqrt(jnp.mean(h * h, axis=-1, keepdims=True) + EPS)
    xhat = h * rstd
    return xhat * g, xhat, rstd


def _rms_bwd(dxn, xhat, rstd, g):
    dxhat = dxn * g
    dh = rstd * (dxhat - xhat * jnp.mean(dxhat * xhat, axis=-1, keepdims=True))
    dg = jnp.sum(dxn * xhat, axis=0, keepdims=True)
    return dh, dg


def _row_spec(tm, n):
    return pl.BlockSpec((tm, n), lambda i, *_: (i, 0))


def _full_spec(shape):
    nd = len(shape)
    return pl.BlockSpec(shape, lambda *_: (0,) * nd)


def _ffn_weight_specs(fidx):
    return [pl.BlockSpec((D, FF_TILE), lambda i, j: (2 * fidx, j)),
            pl.BlockSpec((D, FF_TILE), lambda i, j: (2 * fidx + 1, j)),
            pl.BlockSpec((2, SHP, D), lambda i, j: (j, fidx, 0))]


def ffn_fwd(h, g, wa, wb, fidx, name):
    T = h.shape[0]
    tm = min(TOK_TILE, T)
    nj = FFP // FF_TILE

    def body(h_ref, g_ref, wg_ref, wu_ref, wd_ref, o_ref, xn_s):
        j = pl.program_id(1)

        @pl.when(j == 0)
        def _():
            hh = h_ref[...]
            xn_s[...] = _rms(hh, g_ref[...])[0].astype(BF16)
            o_ref[...] = hh

        xn = xn_s[...]
        gt = _mm(xn, wg_ref[...])
        up = _mm(xn, wu_ref[...])
        act = gt * _sigmoid(gt) * up
        o_ref[...] += 0.5 * _mm(act, wd_ref[...].reshape(FF_TILE, D))

    return pl.pallas_call(
        body, name=name, grid=(T // tm, nj),
        in_specs=[pl.BlockSpec((tm, D), lambda i, j: (i, 0)),
                  pl.BlockSpec((1, D), lambda i, j: (0, 0))] + _ffn_weight_specs(fidx),
        out_specs=pl.BlockSpec((tm, D), lambda i, j: (i, 0)),
        out_shape=jax.ShapeDtypeStruct((T, D), F32),
        scratch_shapes=[pltpu.VMEM((tm, D), BF16)],
        compiler_params=_cp("parallel", "arbitrary"),
    )(h, g, wa, wa, wb)


def ffn_bwd(h, dy, g, wa, wb, fidx, name):
    T = h.shape[0]
    tm = min(TOK_TILE, T)
    nj = FFP // FF_TILE

    def body(h_ref, dy_ref, g_ref, wg_ref, wu_ref, wd_ref,
             dh_ref, dg_ref, du_ref, a_ref, xn_ref, dn_ref, xn_s, dxn_s):
        i = pl.program_id(0)
        j = pl.program_id(1)

        @pl.when(j == 0)
        def _():
            xn = _rms(h_ref[...], g_ref[...])[0].astype(BF16)
            xn_s[...] = xn
            xn_ref[...] = xn
            dxn_s[...] = jnp.zeros_like(dxn_s)

        @pl.when((i == 0) & (j == 0))
        def _():
            dn_ref[...] = jnp.zeros_like(dn_ref)

        xn = xn_s[...]
        gt = _mm(xn, wg_ref[...])
        up = _mm(xn, wu_ref[...])
        sg = _sigmoid(gt)
        si = gt * sg
        da = _mm_nt(0.5 * dy_ref[...], wd_ref[...].reshape(FF_TILE, D))
        dup = da * si
        dgt = da * up * (sg * (1.0 + gt * (1.0 - sg)))
        dg_ref[...] = dgt.astype(BF16)
        du_ref[...] = dup.astype(BF16)
        a_ref[...] = (si * up).astype(BF16)
        dxn_s[...] += _mm_nt(dgt, wg_ref[...]) + _mm_nt(dup, wu_ref[...])

        @pl.when(j == nj - 1)
        def _():
            gg = g_ref[...]
            _, xhat, rstd = _rms(h_ref[...], gg)
            dh, dn = _rms_bwd(dxn_s[...], xhat, rstd, gg)
            dh_ref[...] = dy_ref[...] + dh
            dn_ref[...] += dn

    tile = pl.BlockSpec((tm, FF_TILE), lambda i, j: (i, j))
    return pl.pallas_call(
        body, name=name, grid=(T // tm, nj),
        in_specs=[pl.BlockSpec((tm, D), lambda i, j: (i, 0)),
                  pl.BlockSpec((tm, D), lambda i, j: (i, 0)),
                  pl.BlockSpec((1, D), lambda i, j: (0, 0))] + _ffn_weight_specs(fidx),
        out_specs=[pl.BlockSpec((tm, D), lambda i, j: (i, 0)), tile, tile, tile,
                   pl.BlockSpec((tm, D), lambda i, j: (i, 0)),
                   pl.BlockSpec((1, D), lambda i, j: (0, 0))],
        out_shape=[jax.ShapeDtypeStruct((T, D), F32)] + [jax.ShapeDtypeStruct((T, FFP), BF16)] * 3
        + [jax.ShapeDtypeStruct((T, D), BF16), jax.ShapeDtypeStruct((1, D), F32)],
        scratch_shapes=[pltpu.VMEM((tm, D), BF16), pltpu.VMEM((tm, D), F32)],
        compiler_params=_cp("arbitrary", "arbitrary"),
    )(h, dy, g, wa, wa, wb)


def _pick(n, prefs):
    for t in prefs:
        if n % t == 0:
            return t
    return n


def _tn_body(nk, scale, out_dtype, squeeze):
    def body(a_ref, b_ref, *rest):
        o_ref, acc = rest[-2], rest[-1]
        k = pl.program_id(2)

        @pl.when(k == 0)
        def _():
            acc[...] = jnp.zeros_like(acc)

        acc[...] += _mm_tn(a_ref[...], b_ref[...])

        @pl.when(k == nk - 1)
        def _():
            res = (scale * acc[...]).astype(out_dtype)
            if squeeze:
                o_ref[0] = res
            else:
                o_ref[...] = res

    return body


def matmul_tn(a, b, name, scale=1.0, out_dtype=BF16):
    T, M = a.shape
    N = b.shape[1]
    tmm = _pick(M, (512, 256))
    tnn = _pick(N, (1024, 2432))
    tk = min(TOK_TILE, T)
    nk = T // tk
    return pl.pallas_call(
        _tn_body(nk, scale, out_dtype, False), name=name, grid=(M // tmm, N // tnn, nk),
        in_specs=[pl.BlockSpec((tk, tmm), lambda i, j, k: (k, i)),
                  pl.BlockSpec((tk, tnn), lambda i, j, k: (k, j))],
        out_specs=pl.BlockSpec((tmm, tnn), lambda i, j, k: (i, j)),
        out_shape=jax.ShapeDtypeStruct((M, N), out_dtype),
        scratch_shapes=[pltpu.VMEM((tmm, tnn), F32)],
        compiler_params=_cp("parallel", "parallel", "arbitrary"),
    )(a, b)


def grad_cols(a, b, dst, slot, name):
    T = a.shape[0]
    tmm, tnn = D, FFP // 2
    tk = min(TOK_TILE, T)
    nk = T // tk
    return pl.pallas_call(
        _tn_body(nk, 1.0, BF16, True), name=name, grid=(D // tmm, FFP // tnn, nk),
        in_specs=[pl.BlockSpec((tk, tmm), lambda i, j, k: (k, i)),
                  pl.BlockSpec((tk, tnn), lambda i, j, k: (k, j)),
                  pl.BlockSpec(memory_space=pl.ANY)],
        out_specs=pl.BlockSpec((1, tmm, tnn), lambda i, j, k: (slot, i, j)),
        out_shape=jax.ShapeDtypeStruct(dst.shape, dst.dtype),
        scratch_shapes=[pltpu.VMEM((tmm, tnn), F32)],
        input_output_aliases={2: 0},
        compiler_params=_cp("parallel", "parallel", "arbitrary"),
    )(a, b, dst)


def grad_rows(a, b, dst, row0, rows, name, scale=1.0):
    T = a.shape[0]
    tk = min(TOK_TILE, T)
    nk = T // tk
    blk = row0 // rows

    def body(a_ref, b_ref, dst_ref, o_ref, acc):
        k = pl.program_id(0)

        @pl.when(k == 0)
        def _():
            acc[...] = jnp.zeros_like(acc)

        acc[...] += _mm_tn(a_ref[...], b_ref[...])

        @pl.when(k == nk - 1)
        def _():
            o_ref[...] = (scale * acc[...]).astype(BF16).reshape(N_DEV, rows, D)

    return pl.pallas_call(
        body, name=name, grid=(nk,),
        in_specs=[pl.BlockSpec((tk, N_DEV * rows), lambda k: (k, 0)),
                  pl.BlockSpec((tk, D), lambda k: (k, 0)),
                  pl.BlockSpec(memory_space=pl.ANY)],
        out_specs=pl.BlockSpec((N_DEV, rows, D), lambda k: (0, blk, 0)),
        out_shape=jax.ShapeDtypeStruct(dst.shape, dst.dtype),
        scratch_shapes=[pltpu.VMEM((N_DEV * rows, D), F32)],
        input_output_aliases={2: 0},
        compiler_params=_cp("arbitrary"),
    )(a, b, dst)


U_SPLITS = (512, 768, 1024, 128)
U_OFFS = (0, 512, 1280, 2304)


def mixin_fwd(h, g, w_in, name):
    T = h.shape[0]
    tm = min(TOK_TILE, T)

    def body(h_ref, g_ref, w_ref, u0, u1, u2, u3, xn_ref):
        xn = _rms(h_ref[...], g_ref[...])[0].astype(BF16)
        xn_ref[...] = xn
        u = _mm(xn, w_ref[...])
        for ref, off, n in zip((u0, u1, u2, u3), U_OFFS, U_SPLITS):
            ref[...] = u[:, off:off + n]

    return pl.pallas_call(
        body, name=name, grid=(T // tm,),
        in_specs=[_row_spec(tm, D), _full_spec((1, D)), _full_spec((D, D_IN_PAD))],
        out_specs=[_row_spec(tm, n) for n in U_SPLITS] + [_row_spec(tm, D)],
        out_shape=[jax.ShapeDtypeStruct((T, n), F32) for n in U_SPLITS]
        + [jax.ShapeDtypeStruct((T, D), BF16)],
        compiler_params=_cp("parallel"),
    )(h, g, w_in)


def mixin_bwd(h, dh_in, g, w_in, dus, name):
    T = h.shape[0]
    tm = min(TOK_TILE, T)

    def body(h_ref, dhi_ref, g_ref, w_ref, d0, d1, d2, d3, dh_ref, du_ref, dn_ref):
        @pl.when(pl.program_id(0) == 0)
        def _():
            dn_ref[...] = jnp.zeros_like(dn_ref)

        dxn = jnp.zeros((tm, D), F32)
        for ref, off, n in zip((d0, d1, d2, d3), U_OFFS, U_SPLITS):
            du = ref[...]
            du_ref[:, off:off + n] = du.astype(BF16)
            dxn += _mm_nt(du, w_ref[:, off:off + n])
        gg = g_ref[...]
        _, xhat, rstd = _rms(h_ref[...], gg)
        dh, dn = _rms_bwd(dxn, xhat, rstd, gg)
        dh_ref[...] = dhi_ref[...] + dh
        dn_ref[...] += dn

    return pl.pallas_call(
        body, name=name, grid=(T // tm,),
        in_specs=[_row_spec(tm, D), _row_spec(tm, D), _full_spec((1, D)), _full_spec((D, D_IN_PAD))]
        + [_row_spec(tm, n) for n in U_SPLITS],
        out_specs=[_row_spec(tm, D), _row_spec(tm, D_IN_PAD), _full_spec((1, D))],
        out_shape=[jax.ShapeDtypeStruct((T, D), F32), jax.ShapeDtypeStruct((T, D_IN_PAD), BF16),
                   jax.ShapeDtypeStruct((1, D), F32)],
        compiler_params=_cp("arbitrary"),
    )(h, dh_in, g, w_in, *dus)


def _shift_down(x, s, row):
    if s == 0:
        return x
    return jnp.where(row >= s, pltpu.roll(x, s, 0), 0.0)


def _shift_up(x, s, row):
    if s == 0:
        return x
    n = x.shape[0]
    return jnp.where(row < n - s, pltpu.roll(x, n - s, 0), 0.0)


def conv_fwd(x, w, b, S, col0, C, name):
    T = x.shape[0]
    cb0 = col0 // LANE

    def body(x_ref, w_ref, b_ref, y_ref):
        xx = x_ref[...]
        row = _iota(xx.shape, 0)
        y = xx * w_ref[3:4, :] + b_ref[...]
        for k in range(3):
            y += _shift_down(xx, 3 - k, row) * w_ref[k:k + 1, :]
        y_ref[...] = y

    return pl.pallas_call(
        body, name=name, grid=(T // S, C // LANE),
        in_specs=[pl.BlockSpec((S, LANE), lambda s, c: (s, cb0 + c)),
                  pl.BlockSpec((4, LANE), lambda s, c: (0, c)),
                  pl.BlockSpec((1, LANE), lambda s, c: (0, c))],
        out_specs=pl.BlockSpec((S, LANE), lambda s, c: (s, c)),
        out_shape=jax.ShapeDtypeStruct((T, C), F32),
        compiler_params=_cp("parallel", "parallel"),
    )(x, w, b)


def conv_bwd(x, dy, w, S, col0, C, name):
    T = x.shape[0]
    cb0 = col0 // LANE

    def body(x_ref, dy_ref, w_ref, dx_ref, dwb_ref):
        @pl.when(pl.program_id(1) == 0)
        def _():
            dwb_ref[...] = jnp.zeros_like(dwb_ref)

        xx = x_ref[...]
        dd = dy_ref[...]
        row = _iota(xx.shape, 0)
        dx = dd * w_ref[3:4, :]
        for k in range(3):
            dx += _shift_up(dd, 3 - k, row) * w_ref[k:k + 1, :]
        dx_ref[...] = dx
        for k in range(4):
            dwb_ref[k:k + 1, :] += jnp.sum(dd * _shift_down(xx, 3 - k, row), axis=0, keepdims=True)
        dwb_ref[4:5, :] += jnp.sum(dd, axis=0, keepdims=True)

    return pl.pallas_call(
        body, name=name, grid=(C // LANE, T // S),
        in_specs=[pl.BlockSpec((S, LANE), lambda c, s: (s, cb0 + c)),
                  pl.BlockSpec((S, LANE), lambda c, s: (s, c)),
                  pl.BlockSpec((4, LANE), lambda c, s: (0, c))],
        out_specs=[pl.BlockSpec((S, LANE), lambda c, s: (s, c)),
                   pl.BlockSpec((8, LANE), lambda c, s: (0, c))],
        out_shape=[jax.ShapeDtypeStruct((T, C), F32), jax.ShapeDtypeStruct((8, C), F32)],
        compiler_params=_cp("parallel", "arbitrary"),
    )(x, dy, w)


def _scan(a, b, row):
    n = a.shape[0]
    d = 1
    while d < n:
        keep = row >= d
        b = a * jnp.where(keep, pltpu.roll(b, d, 0), 0.0) + b
        a = a * jnp.where(keep, pltpu.roll(a, d, 0), 1.0)
        d *= 2
    return b


def _rscan(a, b, row):
    n = a.shape[0]
    d = 1
    while d < n:
        keep = row < n - d
        b = a * jnp.where(keep, pltpu.roll(b, n - d, 0), 0.0) + b
        a = a * jnp.where(keep, pltpu.roll(a, n - d, 0), 1.0)
        d *= 2
    return b


GELU_C = math.sqrt(2.0 / math.pi)


def _gelu(x):
    t = jnp.tanh(GELU_C * (x + 0.044715 * (x * x * x)))
    return 0.5 * x * (1.0 + t), t


def _lru_gates(xr, wa, ba, wx, bx, lam):
    r = _sigmoid(_mm(xr, wa) + ba)
    i = _sigmoid(_mm(xr, wx) + bx)
    sp = jnp.maximum(-lam, 0.0) + jnp.log(1.0 + jnp.exp(-jnp.abs(lam)))
    la = -LRU_C * r * sp
    a = jnp.exp(la)
    e2 = a * a
    m = jnp.sqrt(-jnp.tanh(la) * (e2 + 1.0))
    return r, i, sp, a, e2, m


def lru_fwd(xr, u_lru, wa, wx, vec, S, name):
    T = xr.shape[0]

    def body(xr_ref, gt_ref, wa_ref, wx_ref, vec_ref, y_ref):
        x = xr_ref[...]
        row = _iota(x.shape, 0)
        r, i, sp, a, e2, m = _lru_gates(x, wa_ref[...], vec_ref[0:1, :], wx_ref[...], vec_ref[1:2, :],
                                        vec_ref[2:3, :])
        hh = _scan(a, m * (i * x), row)
        y_ref[...] = _gelu(gt_ref[...])[0] * hh

    return pl.pallas_call(
        body, name=name, grid=(T // S, LRU_W // LANE),
        in_specs=[pl.BlockSpec((S, LANE), lambda s, c: (s, c)),
                  pl.BlockSpec((S, LANE), lambda s, c: (s, 2 + c)),
                  pl.BlockSpec((LANE, LANE), lambda s, c: (c, c)),
                  pl.BlockSpec((LANE, LANE), lambda s, c: (c, c)),
                  pl.BlockSpec((8, LANE), lambda s, c: (0, c))],
        out_specs=pl.BlockSpec((S, LANE), lambda s, c: (s, c)),
        out_shape=jax.ShapeDtypeStruct((T, LRU_W), F32),
        compiler_params=_cp("parallel", "parallel"),
    )(xr, u_lru, wa, wx, vec)


def lru_bwd(xr, u_lru, dy, wa, wx, vec, S, name):
    T = xr.shape[0]

    def body(xr_ref, gt_ref, dy_ref, wa_ref, wx_ref, vec_ref,
             dxr_ref, dgt_ref, dwa_ref, dwx_ref, dvec_ref):
        @pl.when(pl.program_id(1) == 0)
        def _():
            dwa_ref[...] = jnp.zeros_like(dwa_ref)
            dwx_ref[...] = jnp.zeros_like(dwx_ref)
            dvec_ref[...] = jnp.zeros_like(dvec_ref)

        x = xr_ref[...]
        n = x.shape[0]
        row = _iota(x.shape, 0)
        lam = vec_ref[2:3, :]
        r, i, sp, a, e2, m = _lru_gates(x, wa_ref[...], vec_ref[0:1, :], wx_ref[...], vec_ref[1:2, :], lam)
        v = i * x
        hh = _scan(a, m * v, row)
        gt = gt_ref[...]
        dy = dy_ref[...]
        ge, t = _gelu(gt)
        dgt_ref[...] = dy * hh * (0.5 * (1.0 + t) + 0.5 * gt * (1.0 - t * t) * GELU_C
                                  * (1.0 + 3.0 * 0.044715 * gt * gt))
        a_next = jnp.where(row < n - 1, pltpu.roll(a, n - 1, 0), 0.0)
        G = _rscan(a_next, dy * ge, row)
        da = G * _shift_down(hh, 1, row)
        dv = G * m
        dla = da * a - (G * v) * e2 / m
        dr = dla * (-LRU_C * sp)
        dsp = jnp.sum(dla * (-LRU_C * r), axis=0, keepdims=True)
        dra = dr * r * (1.0 - r)
        dia = (dv * x) * i * (1.0 - i)
        dxr_ref[...] = dv * i + _mm_nt(dra, wa_ref[...]) + _mm_nt(dia, wx_ref[...])
        dwa_ref[0] += _mm_tn(x, dra)
        dwx_ref[0] += _mm_tn(x, dia)
        dvec_ref[0:1, :] += jnp.sum(dra, axis=0, keepdims=True)
        dvec_ref[1:2, :] += jnp.sum(dia, axis=0, keepdims=True)
        dvec_ref[2:3, :] += dsp * (-_sigmoid(-lam))

    return pl.pallas_call(
        body, name=name, grid=(LRU_W // LANE, T // S),
        in_specs=[pl.BlockSpec((S, LANE), lambda c, s: (s, c)),
                  pl.BlockSpec((S, LANE), lambda c, s: (s, 2 + c)),
                  pl.BlockSpec((S, LANE), lambda c, s: (s, c)),
                  pl.BlockSpec((LANE, LANE), lambda c, s: (c, c)),
                  pl.BlockSpec((LANE, LANE), lambda c, s: (c, c)),
                  pl.BlockSpec((8, LANE), lambda c, s: (0, c))],
        out_specs=[pl.BlockSpec((S, LANE), lambda c, s: (s, c)),
                   pl.BlockSpec((S, LANE), lambda c, s: (s, c)),
                   pl.BlockSpec((1, LANE, LANE), lambda c, s: (c, 0, 0)),
                   pl.BlockSpec((1, LANE, LANE), lambda c, s: (c, 0, 0)),
                   pl.BlockSpec((8, LANE), lambda c, s: (0, c))],
        out_shape=[jax.ShapeDtypeStruct((T, LRU_W), F32), jax.ShapeDtypeStruct((T, LRU_W), F32),
                   jax.ShapeDtypeStruct((2, LANE, LANE), F32), jax.ShapeDtypeStruct((2, LANE, LANE), F32),
                   jax.ShapeDtypeStruct((8, LRU_W), F32)],
        compiler_params=_cp("parallel", "arbitrary"),
    )(xr, u_lru, dy, wa, wx, vec)


def _bucket_table():
    qi = np.arange(BQ)[:, None]
    kj = np.arange(2 * BQ)[None, :]
    dist = BQ + qi - kj
    band = (dist >= 0) & (dist < WINDOW)
    dd = np.maximum(dist, 0)
    max_exact = REL_BUCKETS // 2
    large = max_exact + (np.log(np.maximum(dd, 1).astype(np.float32) / np.float32(max_exact))
                         / np.float32(math.log(REL_MAX_DIST / max_exact))
                         * np.float32(REL_BUCKETS - max_exact)).astype(np.int32)
    large = np.minimum(large, REL_BUCKETS - 1)
    bucket = np.where(dd < max_exact, dd, large)
    return np.where(band, bucket, -1).astype(np.int32)


def _att_specs(S):
    nb = S // BQ
    qc = ATT_W // LANE
    return [pl.BlockSpec((BQ, ATT_W), lambda b, n: (b * nb + n, 0)),
            pl.BlockSpec((BQ, KV_W), lambda b, n: (b * nb + jnp.maximum(n - 1, 0), qc)),
            pl.BlockSpec((BQ, KV_W), lambda b, n: (b * nb + n, qc)),
            pl.BlockSpec((BQ, KV_W), lambda b, n: (b * nb + jnp.maximum(n - 1, 0), qc + 1)),
            pl.BlockSpec((BQ, KV_W), lambda b, n: (b * nb + n, qc + 1))]


def _att_bias(bk, rb_ref, bias_s):
    for h in range(ATT_H):
        acc = jnp.zeros(bk.shape, F32)
        for bb in range(REL_BUCKETS):
            acc = jnp.where(bk == bb, rb_ref[bb * ATT_H + h], acc)
        bias_s[h] = acc


def _att_probs(qh, kg, bias, valid, sink):
    s = _mm_nt(qh, kg) * (HD ** -0.5) + bias
    s = jnp.where(valid, s, NEG)
    m = jnp.maximum(jnp.max(s, axis=-1, keepdims=True), sink)
    e = jnp.exp(s - m)
    es = jnp.exp(sink - m)
    den = jnp.sum(e, axis=-1, keepdims=True) + es
    return e / den, es / den


def attn_fwd(u_att, sinks, rel_bias, S, name):
    T = u_att.shape[0]
    nb = S // BQ
    table = jnp.asarray(_bucket_table())

    def body(sk_ref, rb_ref, bk_ref, q_ref, kp_ref, kc_ref, vp_ref, vc_ref, o_ref, bias_s):
        b = pl.program_id(0)
        n = pl.program_id(1)
        bk = bk_ref[...]

        @pl.when((b == 0) & (n == 0))
        def _():
            _att_bias(bk, rb_ref, bias_s)

        valid = (bk >= 0) & ((n > 0) | (_iota(bk.shape, 1) >= BQ))
        for h in range(ATT_H):
            gs = slice(HD * (h // ATT_G), HD * (h // ATT_G + 1))
            kg = jnp.concatenate([kp_ref[:, gs], kc_ref[:, gs]], axis=0)
            vg = jnp.concatenate([vp_ref[:, gs], vc_ref[:, gs]], axis=0)
            p, _ = _att_probs(q_ref[:, HD * h:HD * (h + 1)], kg, bias_s[h], valid, sk_ref[h])
            o_ref[:, HD * h:HD * (h + 1)] = _mm(p, vg)

    smem = pl.BlockSpec(memory_space=pltpu.SMEM)
    return pl.pallas_call(
        body, name=name, grid=(T // S, nb),
        in_specs=[smem, smem, _full_spec((BQ, 2 * BQ))] + _att_specs(S),
        out_specs=pl.BlockSpec((BQ, ATT_W), lambda b, n: (b * nb + n, 0)),
        out_shape=jax.ShapeDtypeStruct((T, ATT_W), F32),
        scratch_shapes=[pltpu.VMEM((ATT_H, BQ, 2 * BQ), F32)],
        compiler_params=_cp("arbitrary", "arbitrary"),
    )(sinks, rel_bias, table, u_att, u_att, u_att, u_att, u_att)


def attn_bwd(u_att, dy, sinks, rel_bias, S, name):
    T = u_att.shape[0]
    nb = S // BQ
    nB = T // S
    table = jnp.asarray(_bucket_table())
    scale = HD ** -0.5

    def body(sk_ref, rb_ref, bk_ref, q_ref, kp_ref, kc_ref, vp_ref, vc_ref, dy_ref,
             du_ref, drel_ref, dsk_ref, bias_s, dbias_s):
        b = pl.program_id(0)
        n = pl.program_id(1)
        bk = bk_ref[...]

        @pl.when((b == 0) & (n == 0))
        def _():
            _att_bias(bk, rb_ref, bias_s)
            dbias_s[...] = jnp.zeros_like(dbias_s)
            dsk_ref[...] = jnp.zeros_like(dsk_ref)
            drel_ref[...] = jnp.zeros_like(drel_ref)

        @pl.when(n == 0)
        def _():
            du_ref[...] = jnp.zeros_like(du_ref)

        valid = (bk >= 0) & ((n > 0) | (_iota(bk.shape, 1) >= BQ))
        r_cur = pl.multiple_of(n * BQ, BQ)
        r_prev = pl.multiple_of(jnp.maximum(n - 1, 0) * BQ, BQ)
        for g in range(ATT_KV):
            gs = slice(HD * g, HD * (g + 1))
            kg = jnp.concatenate([kp_ref[:, gs], kc_ref[:, gs]], axis=0)
            vg = jnp.concatenate([vp_ref[:, gs], vc_ref[:, gs]], axis=0)
            dk = jnp.zeros((2 * BQ, HD), F32)
            dv = jnp.zeros((2 * BQ, HD), F32)
            for e in range(ATT_G):
                h = g * ATT_G + e
                qh = q_ref[:, HD * h:HD * (h + 1)]
                do = dy_ref[:, HD * h:HD * (h + 1)]
                p, ps = _att_probs(qh, kg, bias_s[h], valid, sk_ref[h])
                dp = _mm_nt(do, vg)
                delta = jnp.sum(p * dp, axis=-1, keepdims=True)
                ds = p * (dp - delta)
                dbias_s[h] += ds
                dsk_ref[h:h + 1, :] += jnp.broadcast_to(
                    jnp.sum(-ps * delta, axis=0, keepdims=True), (1, LANE))
                dss = ds * scale
                du_ref[pl.ds(r_cur, BQ), HD * h:HD * (h + 1)] = _mm(dss, kg)
                dk += _mm_tn(dss, qh)
                dv += _mm_tn(p, do)
            ck = ATT_W + HD * g
            cv = ATT_W + KV_W + HD * g
            du_ref[pl.ds(r_prev, BQ), ck:ck + HD] += dk[0:BQ]
            du_ref[pl.ds(r_cur, BQ), ck:ck + HD] += dk[BQ:]
            du_ref[pl.ds(r_prev, BQ), cv:cv + HD] += dv[0:BQ]
            du_ref[pl.ds(r_cur, BQ), cv:cv + HD] += dv[BQ:]

        @pl.when((b == nB - 1) & (n == nb - 1))
        def _():
            lane = _iota((1, LANE), 1)
            for h in range(ATT_H):
                db = dbias_s[h]
                acc = jnp.zeros((1, LANE), F32)
                for bb in range(REL_BUCKETS):
                    val = jnp.sum(jnp.sum(jnp.where(bk == bb, db, 0.0), axis=1, keepdims=True),
                                  axis=0, keepdims=True)
                    acc = jnp.where(lane == bb, val, acc)
                drel_ref[h:h + 1, :] = acc

    smem = pl.BlockSpec(memory_space=pltpu.SMEM)
    return pl.pallas_call(
        body, name=name, grid=(nB, nb),
        in_specs=[smem, smem, _full_spec((BQ, 2 * BQ))] + _att_specs(S)
        + [pl.BlockSpec((BQ, ATT_W), lambda b, n: (b * nb + n, 0))],
        out_specs=[pl.BlockSpec((S, ATT_W + 2 * KV_W), lambda b, n: (b, 0)),
                   _full_spec((8, LANE)), _full_spec((8, LANE))],
        out_shape=[jax.ShapeDtypeStruct((T, ATT_W + 2 * KV_W), F32),
                   jax.ShapeDtypeStruct((8, LANE), F32), jax.ShapeDtypeStruct((8, LANE), F32)],
        scratch_shapes=[pltpu.VMEM((ATT_H, BQ, 2 * BQ), F32), pltpu.VMEM((ATT_H, BQ, 2 * BQ), F32)],
        compiler_params=_cp("arbitrary", "arbitrary"),
    )(sinks, rel_bias, table, u_att, u_att, u_att, u_att, u_att, dy)


def _head_of(i):
    return lax.shift_right_logical(i, 6)


def _head_mask(shape):
    return (_head_of(_iota(shape, 0)) == _head_of(_iota(shape, 1))).astype(F32)


def _dn_point(c, uba, alog, dtb):
    s = c * _sigmoid(c)
    qt, kt, vt = s[:, 0:256], s[:, 256:512], s[:, 512:768]
    ones_bd = _head_mask((DN_W, DN_W))
    q = qt * lax.rsqrt(_mmx(qt * qt, ones_bd) + EPS) * (HD ** -0.5)
    k = kt * lax.rsqrt(_mmx(kt * kt, ones_bd) + EPS)
    sel = _head_of(_iota((LANE, DN_W), 1))
    row = _iota((LANE, DN_W), 0)
    braw = _mmx(uba, (row == sel).astype(F32))
    araw = _mmx(uba, (row == sel + DN_H).astype(F32)) + dtb
    beta = _sigmoid(braw)
    g = -jnp.exp(alog) * (jnp.maximum(araw, 0.0) + jnp.log(1.0 + jnp.exp(-jnp.abs(araw))))
    return q, k, vt, g, beta


def dn_point_fwd(c, uba, alog, dtb, name):
    T = c.shape[0]
    tm = min(TOK_TILE, T)

    def body(c_ref, u_ref, al_ref, dt_ref, *outs):
        for ref, val in zip(outs, _dn_point(c_ref[...], u_ref[...], al_ref[...], dt_ref[...])):
            ref[...] = val

    return pl.pallas_call(
        body, name=name, grid=(T // tm,),
        in_specs=[_row_spec(tm, 768), _row_spec(tm, LANE), _full_spec((1, DN_W)), _full_spec((1, DN_W))],
        out_specs=[_row_spec(tm, DN_W)] * 5,
        out_shape=[jax.ShapeDtypeStruct((T, DN_W), F32)] * 5,
        compiler_params=_cp("parallel"),
    )(c, uba, alog, dtb)


def dn_point_bwd(c, uba, alog, dtb, douts, name):
    T = c.shape[0]
    tm = min(TOK_TILE, T)

    def body(c_ref, u_ref, al_ref, dt_ref, dq, dk, dv, dg, db, dc_ref, du_ref, dvec_ref):
        @pl.when(pl.program_id(0) == 0)
        def _():
            dvec_ref[...] = jnp.zeros_like(dvec_ref)

        _, vjp = jax.vjp(_dn_point, c_ref[...], u_ref[...], al_ref[...], dt_ref[...])
        dc, du, dal, ddt = vjp((dq[...], dk[...], dv[...], dg[...], db[...]))
        dc_ref[...] = dc
        du_ref[...] = du
        fold = (_iota((LANE, DN_W), 0) == _head_of(_iota((LANE, DN_W), 1))).astype(F32)
        both = jnp.concatenate([dal, ddt, jnp.zeros((6, DN_W), F32)], axis=0)
        dvec_ref[...] += _mmx_nt(both, fold)

    return pl.pallas_call(
        body, name=name, grid=(T // tm,),
        in_specs=[_row_spec(tm, 768), _row_spec(tm, LANE), _full_spec((1, DN_W)), _full_spec((1, DN_W))]
        + [_row_spec(tm, DN_W)] * 5,
        out_specs=[_row_spec(tm, 768), _row_spec(tm, LANE), _full_spec((8, LANE))],
        out_shape=[jax.ShapeDtypeStruct((T, 768), F32), jax.ShapeDtypeStruct((T, LANE), F32),
                   jax.ShapeDtypeStruct((8, LANE), F32)],
        compiler_params=_cp("arbitrary"),
    )(c, uba, alog, dtb, *douts)


def _unit_lower_inverse(lmat):
    eye = (_iota(lmat.shape, 0) == _iota(lmat.shape, 1)).astype(F32)
    tinv = eye - lmat
    pw = lmat
    for _ in range(5):
        pw = _mm3(pw, pw)
        tinv = tinv + _mm3(tinv, pw)
    return tinv


def _inverse_bwd(tinv, d):
    return -_mm3_nt(_mm3_tn(tinv, d), tinv)


@jax.custom_vjp
def _tri_inv(lmat):
    return _unit_lower_inverse(lmat)


def _tri_inv_fwd(lmat):
    tinv = _unit_lower_inverse(lmat)
    return tinv, tinv


_tri_inv.defvjp(_tri_inv_fwd, lambda tinv, d: (_inverse_bwd(tinv, d),))


@jax.custom_vjp
def _tri_inv_known(lmat, tinv):
    return tinv


_tri_inv_known.defvjp(lambda lmat, tinv: (tinv, tinv),
                      lambda tinv, d: (_inverse_bwd(tinv, d), jnp.zeros_like(tinv)))


DN_SUB = 2


def _dn_prep(q, k, v, g, beta, known=None):
    hm = _head_mask((DN_W, DN_W))
    ri = _iota((DN_W, DN_W), 0) & (CHUNK - 1)
    ci = _iota((DN_W, DN_W), 1) & (CHUNK - 1)
    tril = hm * (ri >= ci).astype(F32)
    strict = hm * (ri > ci).astype(F32)
    tri64 = (_iota((CHUNK, CHUNK), 0) >= _iota((CHUNK, CHUNK), 1)).astype(F32)

    def stack(x):
        return jnp.concatenate([x, x, x, x], axis=0) * hm

    gc = _mm3(tri64, g)
    glast = jnp.sum(g, axis=0, keepdims=True)
    eg = jnp.exp(gc)
    kb = k * beta
    qs, ks = stack(q), stack(k)
    gcol = jnp.sum(stack(gc), axis=1, keepdims=True) * (1.0 / HD)
    gmat = jnp.broadcast_to(gcol, (DN_W, DN_W))
    decay = jnp.exp(jnp.minimum(gmat - gmat.T, 0.0))
    lmat = _mm_nt(stack(kb), ks) * decay * strict
    tinv = _tri_inv(lmat) if known is None else _tri_inv_known(lmat, known)
    u = _mm(tinv, stack(v * beta))
    w = _mm(tinv, stack(kb * eg))
    att = _mm_nt(qs, ks) * decay * tril
    return u, w, att, stack(q * eg), stack(k * jnp.exp(glast - gc)), jnp.exp(glast), tinv


def _dn_apply(state, prep):
    u, w, att, qe, kd, eglast, _ = prep
    vn = u - _mm(w, state)
    o4 = _mm(qe, state) + _mm(att, vn)
    o = o4[0:64] + o4[64:128] + o4[128:192] + o4[192:256]
    return o, state * eglast + _mm_tn(kd, vn)


def _dn_chunks(state, q, k, v, g, beta, knowns=None):
    n = q.shape[0] // CHUNK
    rows = lambda x, c: x[c * CHUNK:(c + 1) * CHUNK]
    preps = [_dn_prep(*(rows(x, c) for x in (q, k, v, g, beta)),
                      known=None if knowns is None else knowns[c]) for c in range(n)]
    outs = []
    for prep in preps:
        o, state = _dn_apply(state, prep)
        outs.append(o)
    return jnp.concatenate(outs, axis=0), state, [prep[-1] for prep in preps]


def dn_scan_fwd(q, k, v, g, beta, S, name):
    T = q.shape[0]
    rows = DN_SUB * CHUNK
    ns = S // rows

    def body(q_ref, k_ref, v_ref, g_ref, b_ref, o_ref, st_ref, ti_ref, s_s):
        @pl.when(pl.program_id(1) == 0)
        def _():
            s_s[...] = jnp.zeros_like(s_s)

        st = s_s[...]
        st_ref[0] = st
        o, new, tinvs = _dn_chunks(st, q_ref[...], k_ref[...], v_ref[...], g_ref[...], b_ref[...])
        o_ref[...] = o
        for c, tinv in enumerate(tinvs):
            ti_ref[c] = tinv
        s_s[...] = new

    spec = pl.BlockSpec((rows, DN_W), lambda b, t: (b * ns + t, 0))
    return pl.pallas_call(
        body, name=name, grid=(T // S, ns),
        in_specs=[spec] * 5,
        out_specs=[spec, pl.BlockSpec((1, DN_W, DN_W), lambda b, t: (b * ns + t, 0, 0)),
                   pl.BlockSpec((DN_SUB, DN_W, DN_W), lambda b, t: (b * ns + t, 0, 0))],
        out_shape=[jax.ShapeDtypeStruct((T, DN_W), F32),
                   jax.ShapeDtypeStruct((T // rows, DN_W, DN_W), F32),
                   jax.ShapeDtypeStruct((T // CHUNK, DN_W, DN_W), F32)],
        scratch_shapes=[pltpu.VMEM((DN_W, DN_W), F32)],
        compiler_params=_cp("parallel", "arbitrary"),
    )(q, k, v, g, beta)


def dn_scan_bwd(q, k, v, g, beta, states, tinvs, do, S, name):
    T = q.shape[0]
    rows = DN_SUB * CHUNK
    ns = S // rows

    def body(q_ref, k_ref, v_ref, g_ref, b_ref, st_ref, ti_ref, do_ref, dq, dk, dv, dg, db, ds_s):
        @pl.when(pl.program_id(1) == 0)
        def _():
            ds_s[...] = jnp.zeros_like(ds_s)

        knowns = [ti_ref[c] for c in range(DN_SUB)]
        _, vjp = jax.vjp(lambda *args: _dn_chunks(*args, knowns=knowns)[:2],
                         st_ref[0], q_ref[...], k_ref[...], v_ref[...], g_ref[...], b_ref[...])
        grads = vjp((do_ref[...], ds_s[...]))
        ds_s[...] = grads[0]
        for ref, val in zip((dq, dk, dv, dg, db), grads[1:]):
            ref[...] = val

    spec = pl.BlockSpec((rows, DN_W), lambda b, t: (b * ns + ns - 1 - t, 0))
    return pl.pallas_call(
        body, name=name, grid=(T // S, ns),
        in_specs=[spec] * 5 + [pl.BlockSpec((1, DN_W, DN_W), lambda b, t: (b * ns + ns - 1 - t, 0, 0)),
                               pl.BlockSpec((DN_SUB, DN_W, DN_W), lambda b, t: (b * ns + ns - 1 - t, 0, 0)),
                               spec],
        out_specs=[spec] * 5,
        out_shape=[jax.ShapeDtypeStruct((T, DN_W), F32)] * 5,
        scratch_shapes=[pltpu.VMEM((DN_W, DN_W), F32)],
        compiler_params=_cp("parallel", "arbitrary"),
    )(q, k, v, g, beta, states, tinvs, do)


def _dn_gate(o, z, nl):
    ms = _mmx(o * o, _head_mask((DN_W, DN_W))) * (1.0 / HD)
    return o * lax.rsqrt(ms + EPS) * nl * (z * _sigmoid(z))


def dn_gate_fwd(o, u_dn, nl, name):
    T = o.shape[0]
    tm = min(TOK_TILE, T)

    def body(o_ref, z_ref, n_ref, y_ref):
        y_ref[...] = _dn_gate(o_ref[...], z_ref[...], n_ref[...])

    return pl.pallas_call(
        body, name=name, grid=(T // tm,),
        in_specs=[_row_spec(tm, DN_W), pl.BlockSpec((tm, DN_W), lambda i: (i, 3)), _full_spec((1, DN_W))],
        out_specs=_row_spec(tm, DN_W),
        out_shape=jax.ShapeDtypeStruct((T, DN_W), F32),
        compiler_params=_cp("parallel"),
    )(o, u_dn, nl)


def dn_gate_bwd(o, u_dn, nl, dy, name):
    T = o.shape[0]
    tm = min(TOK_TILE, T)

    def body(o_ref, z_ref, n_ref, dy_ref, do_ref, dz_ref, dn_ref):
        @pl.when(pl.program_id(0) == 0)
        def _():
            dn_ref[...] = jnp.zeros_like(dn_ref)

        _, vjp = jax.vjp(_dn_gate, o_ref[...], z_ref[...], n_ref[...])
        do, dz, dn = vjp(dy_ref[...])
        do_ref[...] = do
        dz_ref[...] = dz
        fold = (_iota((LANE, DN_W), 0) == (_iota((LANE, DN_W), 1) & (HD - 1))).astype(F32)
        dn_ref[...] += _mmx_nt(jnp.concatenate([dn, jnp.zeros((7, DN_W), F32)], axis=0), fold)

    return pl.pallas_call(
        body, name=name, grid=(T // tm,),
        in_specs=[_row_spec(tm, DN_W), pl.BlockSpec((tm, DN_W), lambda i: (i, 3)), _full_spec((1, DN_W)),
                  _row_spec(tm, DN_W)],
        out_specs=[_row_spec(tm, DN_W), _row_spec(tm, DN_W), _full_spec((8, LANE))],
        out_shape=[jax.ShapeDtypeStruct((T, DN_W), F32), jax.ShapeDtypeStruct((T, DN_W), F32),
                   jax.ShapeDtypeStruct((8, LANE), F32)],
        compiler_params=_cp("arbitrary"),
    )(o, u_dn, nl, dy)


Y_SPLITS = (LRU_W, ATT_W, DN_W)
Y_OFFS = (0, LRU_W, LRU_W + ATT_W)


ROWS_DEV = D // N_DEV


def _dev_rows_spec(row0):
    return pl.BlockSpec((N_DEV, ROWS_DEV, D), lambda *_: (0, row0 // ROWS_DEV, 0))


def _dev_rows(w_ref, off, n):
    return w_ref[off // ROWS_DEV:(off + n) // ROWS_DEV].reshape(n, D)


def wout_fwd(h, ys, wb, name):
    T = h.shape[0]
    tm = min(TOK_TILE, T)

    def body(h_ref, y0, y1, y2, w_ref, o_ref, yc_ref):
        acc = h_ref[...]
        for ref, off, n in zip((y0, y1, y2), Y_OFFS, Y_SPLITS):
            y = ref[...].astype(BF16)
            yc_ref[:, off:off + n] = y
            acc += _mm(y, _dev_rows(w_ref, off, n))
        o_ref[...] = acc

    return pl.pallas_call(
        body, name=name, grid=(T // tm,),
        in_specs=[_row_spec(tm, D)] + [_row_spec(tm, n) for n in Y_SPLITS] + [_dev_rows_spec(B_WOUT)],
        out_specs=[_row_spec(tm, D), _row_spec(tm, D)],
        out_shape=[jax.ShapeDtypeStruct((T, D), F32), jax.ShapeDtypeStruct((T, D), BF16)],
        compiler_params=_cp("parallel"),
    )(h, *ys, wb)


def wout_bwd(dy, wb, name):
    T = dy.shape[0]
    tm = min(TOK_TILE, T)

    def body(dy_ref, w_ref, d0, d1, d2):
        dd = dy_ref[...].astype(BF16)
        for ref, off, n in zip((d0, d1, d2), Y_OFFS, Y_SPLITS):
            ref[...] = _mm_nt(dd, _dev_rows(w_ref, off, n))

    return pl.pallas_call(
        body, name=name, grid=(T // tm,),
        in_specs=[_row_spec(tm, D), _dev_rows_spec(B_WOUT)],
        out_specs=[_row_spec(tm, n) for n in Y_SPLITS],
        out_shape=[jax.ShapeDtypeStruct((T, n), F32) for n in Y_SPLITS],
        compiler_params=_cp("parallel"),
    )(dy, wb)


def ple_fwd(h, g, pe, wg, wp, name):
    T = h.shape[0]
    tm = min(TOK_TILE, T)

    def body(h_ref, g_ref, p_ref, wg_ref, wp_ref, o_ref):
        hh = h_ref[...]
        xn = _rms(hh, g_ref[...])[0]
        o_ref[...] = hh + _sigmoid(_mm(xn, _dev_rows(wg_ref, 0, D))) * _mm(p_ref[...], wp_ref[...])

    return pl.pallas_call(
        body, name=name, grid=(T // tm,),
        in_specs=[_row_spec(tm, D), _full_spec((1, D)), _row_spec(tm, PLE), _dev_rows_spec(B_PGATE),
                  _full_spec((PLE, D))],
        out_specs=_row_spec(tm, D),
        out_shape=jax.ShapeDtypeStruct((T, D), F32),
        compiler_params=_cp("parallel"),
    )(h, g, pe, wg, wp)


def ple_bwd(h, dy, g, pe, wg, wp, name):
    T = h.shape[0]
    tm = min(TOK_TILE, T)

    def body(h_ref, dy_ref, g_ref, p_ref, wg_ref, wp_ref, dh_ref, dz_ref, dpp_ref, xn_ref, dn_ref):
        @pl.when(pl.program_id(0) == 0)
        def _():
            dn_ref[...] = jnp.zeros_like(dn_ref)

        gg = g_ref[...]
        dy = dy_ref[...]
        xn, xhat, rstd = _rms(h_ref[...], gg)
        wg = _dev_rows(wg_ref, 0, D)
        gate = _sigmoid(_mm(xn, wg))
        pp = _mm(p_ref[...], wp_ref[...])
        dz = dy * pp * gate * (1.0 - gate)
        dz_ref[...] = dz.astype(BF16)
        dpp_ref[...] = (dy * gate).astype(BF16)
        xn_ref[...] = xn.astype(BF16)
        dh, dn = _rms_bwd(_mm_nt(dz, wg), xhat, rstd, gg)
        dh_ref[...] = dy + dh
        dn_ref[...] += dn

    return pl.pallas_call(
        body, name=name, grid=(T // tm,),
        in_specs=[_row_spec(tm, D), _row_spec(tm, D), _full_spec((1, D)), _row_spec(tm, PLE),
                  _dev_rows_spec(B_PGATE), _full_spec((PLE, D))],
        out_specs=[_row_spec(tm, D), _row_spec(tm, D), _row_spec(tm, D), _row_spec(tm, D), _full_spec((1, D))],
        out_shape=[jax.ShapeDtypeStruct((T, D), F32), jax.ShapeDtypeStruct((T, D), BF16),
                   jax.ShapeDtypeStruct((T, D), BF16), jax.ShapeDtypeStruct((T, D), BF16),
                   jax.ShapeDtypeStruct((1, D), F32)],
        compiler_params=_cp("arbitrary"),
    )(h, dy, g, pe, wg, wp)


def loss_head(h, g, target, name):
    T = h.shape[0]
    tm = min(TOK_TILE, T)

    def body(h_ref, g_ref, t_ref, loss_ref, dh_ref, dn_ref):
        @pl.when(pl.program_id(0) == 0)
        def _():
            dn_ref[...] = jnp.zeros_like(dn_ref)
            loss_ref[...] = jnp.zeros_like(loss_ref)

        gg = g_ref[...]
        y, xhat, rstd = _rms(h_ref[...], gg)
        err = y - t_ref[...]
        per_tok = jnp.mean(err * err, axis=-1, keepdims=True)
        loss_ref[...] += 0.5 * jnp.sum(per_tok, axis=0, keepdims=True)
        dh, dn = _rms_bwd(err * (1.0 / D), xhat, rstd, gg)
        dh_ref[...] = dh
        dn_ref[...] += dn

    return pl.pallas_call(
        body, name=name, grid=(T // tm,),
        in_specs=[_row_spec(tm, D), _full_spec((1, D)), _row_spec(tm, D)],
        out_specs=[_full_spec((8, LANE)), _row_spec(tm, D), _full_spec((1, D))],
        out_shape=[jax.ShapeDtypeStruct((8, LANE), F32), jax.ShapeDtypeStruct((T, D), F32),
                   jax.ShapeDtypeStruct((1, D), F32)],
        compiler_params=_cp("arbitrary"),
    )(h, g, target)


def _block_diag(w):
    return jnp.einsum('hij,hk->hikj', w, jnp.eye(4, dtype=w.dtype)).reshape(LRU_W, LRU_W)


def _layer_consts(W, l):
    row = lambda v: v.reshape(1, -1)
    zeros = jnp.zeros((5, LRU_W), F32)
    return dict(
        wa=_block_diag(W["lru_w_a"][l]), wx=_block_diag(W["lru_w_x"][l]),
        lru_vec=jnp.concatenate([row(W["lru_b_a"][l]), row(W["lru_b_x"][l]), row(W["lru_lambda"][l]), zeros], 0),
        lru_cb=row(W["lru_conv_b"][l]),
        sinks=W["attn_sinks"][l], rel=W["rel_bias"].reshape(-1),
        dn_cb=jnp.zeros((1, 3 * DN_W), F32),
        alog=row(jnp.repeat(W["dn_a_log"][l], HD)), dtb=row(jnp.repeat(W["dn_dt_bias"][l], HD)),
        dn_nl=row(jnp.tile(W["dn_norm"][l], DN_H)),
    )


def _layer_fwd(h0, pe, W, l, S):
    n = f"l{l}_"
    c_ = _layer_consts(W, l)
    row = lambda v: v.reshape(1, -1)
    wa, wb = W["wa"][l], W["wb"][l]
    h1 = ffn_fwd(h0, row(W["ffn1_norm"][l]), wa, wb, 0, n + "ffn1_fwd")
    u_lru, u_att, u_dn, u_ba, xn_mix = mixin_fwd(h1, row(W["mix_norm"][l]), W["w_in"][l], n + "mixin_fwd")
    xr = conv_fwd(u_lru, W["lru_conv_w"][l], c_["lru_cb"], S, 0, LRU_W, n + "lru_conv_fwd")
    y_lru = lru_fwd(xr, u_lru, c_["wa"], c_["wx"], c_["lru_vec"], S, n + "lru_fwd")
    y_att = attn_fwd(u_att, c_["sinks"], c_["rel"], S, n + "attn_fwd")
    cc = conv_fwd(u_dn, W["dn_conv_w"][l], c_["dn_cb"], S, 0, 3 * DN_W, n + "dn_conv_fwd")
    q, k, v, g, beta = dn_point_fwd(cc, u_ba, c_["alog"], c_["dtb"], n + "dn_point_fwd")
    o, states, tinvs = dn_scan_fwd(q, k, v, g, beta, S, n + "dn_scan_fwd")
    y_dn = dn_gate_fwd(o, u_dn, c_["dn_nl"], n + "dn_gate_fwd")
    h2, ycat = wout_fwd(h1, (y_lru, y_att, y_dn), wb, n + "wout_fwd")
    h3 = ffn_fwd(h2, row(W["ffn2_norm"][l]), wa, wb, 1, n + "ffn2_fwd")
    h4 = ple_fwd(h3, row(W["ple_norm"][l]), pe, wb, W["ple_w_proj"][l], n + "ple_fwd")
    saved = dict(h0=h0, h1=h1, h2=h2, h3=h3, u_lru=u_lru, u_att=u_att, u_dn=u_dn, u_ba=u_ba, xn_mix=xn_mix,
                 xr=xr, cc=cc, q=q, k=k, v=v, g=g, beta=beta, o=o, states=states, tinvs=tinvs, ycat=ycat)
    return h4, saved


GE_DOWN2, GE_WOUT, GE_PGATE, GE_WIN, GE_PPROJ, GE_END, GE_ROWS = 0, 384, 512, 640, 944, 976, 1024


def _layer_bwd(dh4, sv, pe, W, l, S, token=None, on_piece=None):
    n = f"l{l}_"
    c_ = _layer_consts(W, l)
    row = lambda v: v.reshape(1, -1)
    behind = lambda v, tok: v if tok is None else v + tok
    on_piece = on_piece or (lambda *_: None)
    wa, wb = W["wa"][l], W["wb"][l]
    G = {"early_cols": jnp.zeros((2, D, FFP), BF16), "early_rows": jnp.zeros((N_DEV, GE_ROWS, D), BF16),
         "late_cols": jnp.zeros((2, D, FFP), BF16), "late_rows": jnp.zeros((N_DEV, SHP, D), BF16)}
    dh3, dz, dpp, xn_p, dn = ple_bwd(sv["h3"], dh4, behind(row(W["ple_norm"][l]), token), pe, wb,
                                     W["ple_w_proj"][l], n + "ple_bwd")
    G["ple_norm"] = dn[0]
    G["early_rows"] = grad_rows(xn_p, dz, G["early_rows"], GE_PGATE, ROWS_DEV, n + "d_ple_w_gate")
    d_proj = matmul_tn(pe, dpp, n + "d_ple_w_proj")
    d_proj = d_proj.reshape(PLE, N_DEV, D // N_DEV).transpose(1, 0, 2).reshape(N_DEV, GE_END - GE_PPROJ, D)
    G["early_rows"] = lax.dynamic_update_slice(G["early_rows"], d_proj, (0, GE_PPROJ, 0))

    def ffn_back(which, fidx, h_in, dy, tok=None):
        piece = ("late", "early")[fidx]
        dh, dgt, dup, act, xn, dn_ = ffn_bwd(h_in, dy, behind(row(W[which + "_norm"][l]), tok), wa, wb, fidx,
                                             n + which + "_bwd")
        G[which + "_norm"] = dn_[0]
        G[piece + "_cols"] = grad_cols(xn, dgt, G[piece + "_cols"], 0, n + "d_" + which + "_w_gate")
        G[piece + "_cols"] = grad_cols(xn, dup, G[piece + "_cols"], 1, n + "d_" + which + "_w_up")
        G[piece + "_rows"] = grad_rows(act, dy, G[piece + "_rows"], 0, SHP, n + "d_" + which + "_w_down",
                                       scale=0.5)
        return dh

    dh2 = ffn_back("ffn2", 1, sv["h2"], dh3)
    dy_lru, dy_att, dy_dn = wout_bwd(dh2, wb, n + "wout_bwd")
    G["early_rows"] = grad_rows(sv["ycat"], dh2, G["early_rows"], GE_WOUT, ROWS_DEV, n + "d_w_out")
    do, dz_dn, dnn = dn_gate_bwd(sv["o"], sv["u_dn"], c_["dn_nl"], dy_dn, n + "dn_gate_bwd")
    dqkvgb = dn_scan_bwd(sv["q"], sv["k"], sv["v"], sv["g"], sv["beta"], sv["states"], sv["tinvs"], do, S,
                         n + "dn_scan_bwd")
    dcc, du_ba, dvec_dn = dn_point_bwd(sv["cc"], sv["u_ba"], c_["alog"], c_["dtb"], dqkvgb, n + "dn_point_bwd")
    dqkv, dwb_dn = conv_bwd(sv["u_dn"], dcc, W["dn_conv_w"][l], S, 0, 3 * DN_W, n + "dn_conv_bwd")
    du_dn = jnp.concatenate([dqkv, dz_dn], axis=1)
    G["dn_norm"] = dnn[0, 0:HD]
    G["dn_a_log"] = dvec_dn[0, 0:DN_H]
    G["dn_dt_bias"] = dvec_dn[1, 0:DN_H]
    G["dn_conv_w"] = dwb_dn[0:4]
    du_att, drel, dsk = attn_bwd(sv["u_att"], dy_att, c_["sinks"], c_["rel"], S, n + "attn_bwd")
    G["attn_sinks"] = dsk[:, 0]
    G["rel_bias"] = drel[:, 0:REL_BUCKETS].T
    dxr, dgt_lru, dwa, dwx, dvec = lru_bwd(sv["xr"], sv["u_lru"], dy_lru, c_["wa"], c_["wx"], c_["lru_vec"], S,
                                           n + "lru_bwd")
    dx_lru, dwb_lru = conv_bwd(sv["u_lru"], dxr, W["lru_conv_w"][l], S, 0, LRU_W, n + "lru_conv_bwd")
    du_lru = jnp.concatenate([dx_lru, dgt_lru], axis=1)
    diag = lambda m: jnp.stack([m[c, HD * e:HD * (e + 1), HD * e:HD * (e + 1)] for c in range(2) for e in range(2)])
    G["lru_w_a"], G["lru_w_x"] = diag(dwa), diag(dwx)
    G["lru_b_a"], G["lru_b_x"], G["lru_lambda"] = dvec[0], dvec[1], dvec[2]
    G["lru_conv_w"], G["lru_conv_b"] = dwb_lru[0:4], dwb_lru[4]
    dh1, du_cat, dn = mixin_bwd(sv["h1"], dh2, row(W["mix_norm"][l]), W["w_in"][l],
                                (du_lru, du_att, du_dn, du_ba), n + "mixin_bwd")
    G["mix_norm"] = dn[0]
    d_in = matmul_tn(sv["xn_mix"], du_cat, n + "d_w_in")[:, :D_IN]
    d_in = d_in.reshape(D, N_DEV, D_IN // N_DEV).transpose(1, 0, 2).reshape(N_DEV, WIN_ROWS, D)
    d_in = jnp.pad(d_in, ((0, 0), (0, GE_PPROJ - GE_WIN - WIN_ROWS), (0, 0)))
    G["early_rows"] = lax.dynamic_update_slice(G["early_rows"], d_in, (0, GE_WIN, 0))
    tok = on_piece(l, "early", G["early_cols"], G["early_rows"])
    dh0 = ffn_back("ffn1", 0, sv["h0"], dh1, tok)
    return dh0, G, on_piece(l, "late", G["late_cols"], G["late_rows"])


def _core(x, pe, W, target, S, weights_at=None, on_piece=None):
    weights_at = weights_at or (lambda l, h: W)
    h = x
    saved = []
    for l in range(DEPTH):
        h, sv = _layer_fwd(h, pe[l], weights_at(l, h), l, S)
        saved.append(sv)
    loss_tile, dh, dfn = loss_head(h, W["final_norm"].reshape(1, -1), target, "loss_head")
    grads = [None] * DEPTH
    token = None
    for l in reversed(range(DEPTH)):
        dh, grads[l], token = _layer_bwd(dh, saved[l], pe[l], W, l, S, token, on_piece)
    return loss_tile[0, 0], dh, grads, dfn[0]


MESH_ID = pl.DeviceIdType.MESH
ANY_SPEC = pl.BlockSpec(memory_space=pl.ANY)
AXES = ("x", "y", "c")


def _my_pos():
    return lax.axis_index("x"), lax.axis_index("y"), lax.axis_index("c")


def _slot_of(px, py, pc):
    return 4 * px + 2 * py + pc


def all_gather(x, name):
    R, C = x.shape

    def body(x_ref, out_ref, send_sems, recv_sems, local_sem):
        mx, my, mc = _my_pos()
        me, sibling = (mx, my, mc), (mx, my, 1 - mc)
        chips = [(1 - mx, my), (mx, 1 - my), (1 - mx, 1 - my)]

        def copy(k, block, to, src=None):
            dst = out_ref.at[_slot_of(*block)]
            return pltpu.make_async_remote_copy(
                src_ref=dst if src is None else src, dst_ref=dst,
                send_sem=send_sems.at[k], recv_sem=recv_sems.at[k],
                device_id=to, device_id_type=MESH_ID)

        mine = pltpu.make_async_copy(x_ref, out_ref.at[_slot_of(*me)], local_sem)
        mine.start()
        first = [copy(0, me, sibling, src=x_ref)]
        first += [copy(1 + j, me, (*chip, mc), src=x_ref) for j, chip in enumerate(chips)]
        for cp in first:
            cp.start()
        passed = [copy(4 + j, (*chip, mc), sibling) for j, chip in enumerate(chips)]
        for j, chip in enumerate(chips):
            copy(1 + j, (*chip, mc), me).wait_recv()
            passed[j].start()
        copy(0, sibling, me).wait_recv()
        for j, chip in enumerate(chips):
            copy(4 + j, (*chip, 1 - mc), me).wait_recv()
        for cp in first + passed:
            cp.wait_send()
        mine.wait()

    return pl.pallas_call(
        body, name=name,
        out_shape=jax.ShapeDtypeStruct((N_DEV, R, C), x.dtype),
        in_specs=[ANY_SPEC], out_specs=ANY_SPEC,
        scratch_shapes=[pltpu.SemaphoreType.DMA((7,)), pltpu.SemaphoreType.DMA((7,)), pltpu.SemaphoreType.DMA],
    )(x)


def _col_window(ref, slot):
    return ref.at[:, pl.ds(pl.multiple_of(slot * SHP, LANE), SHP)]


def gather_layer(a_sh, b_sh, name):
    def body(a_ref, b_ref, ao_ref, bo_ref, send_sems, recv_sems, local_sems):
        mx, my, mc = _my_pos()
        me, sibling = (mx, my, mc), (mx, my, 1 - mc)
        chips = [(1 - mx, my), (mx, 1 - my), (1 - mx, 1 - my)]

        def copies(k, block, to, own=False):
            slot = _slot_of(*block)
            dsts = (_col_window(ao_ref, slot), bo_ref.at[slot])
            srcs = (a_ref, b_ref) if own else dsts
            return [pltpu.make_async_remote_copy(
                src_ref=s, dst_ref=d, send_sem=send_sems.at[2 * k + i], recv_sem=recv_sems.at[2 * k + i],
                device_id=to, device_id_type=MESH_ID) for i, (s, d) in enumerate(zip(srcs, dsts))]

        mine = [pltpu.make_async_copy(a_ref, _col_window(ao_ref, _slot_of(*me)), local_sems.at[0]),
                pltpu.make_async_copy(b_ref, bo_ref.at[_slot_of(*me)], local_sems.at[1])]
        for cp in mine:
            cp.start()
        first = copies(0, me, sibling, own=True)
        for j, chip in enumerate(chips):
            first += copies(1 + j, me, (*chip, mc), own=True)
        for cp in first:
            cp.start()
        passed = []
        for j, chip in enumerate(chips):
            for cp in copies(1 + j, (*chip, mc), me):
                cp.wait_recv()
            fwd = copies(4 + j, (*chip, mc), sibling)
            for cp in fwd:
                cp.start()
            passed += fwd
        for cp in copies(0, sibling, me):
            cp.wait_recv()
        for j, chip in enumerate(chips):
            for cp in copies(4 + j, (*chip, 1 - mc), me):
                cp.wait_recv()
        for cp in first + passed:
            cp.wait_send()
        for cp in mine:
            cp.wait()

    return pl.pallas_call(
        body, name=name,
        out_shape=[jax.ShapeDtypeStruct((a_sh.shape[0], FFP), a_sh.dtype),
                   jax.ShapeDtypeStruct((N_DEV,) + b_sh.shape, b_sh.dtype)],
        in_specs=[ANY_SPEC, ANY_SPEC], out_specs=[ANY_SPEC, ANY_SPEC],
        scratch_shapes=[pltpu.SemaphoreType.DMA((14,)), pltpu.SemaphoreType.DMA((14,)),
                        pltpu.SemaphoreType.DMA((2,))],
    )(a_sh, b_sh)


HBM_SPEC = pl.BlockSpec(memory_space=pltpu.HBM)
SEM_SPEC = pl.BlockSpec(memory_space=pltpu.SEMAPHORE)
SPLIT_EFFECT = pltpu.CompilerParams(has_side_effects=pltpu.SideEffectType.DATAFLOW_SIDE_EFFECTING)


def _split_ends(mode, src_ref, dst_ref, src_slot, dst_slot):
    cols = mode.endswith("cols")
    if mode.startswith("gather"):
        return src_ref, (_col_window(dst_ref, dst_slot) if cols else dst_ref.at[dst_slot])
    return (_col_window(src_ref, src_slot) if cols else src_ref.at[src_slot]), dst_ref.at[dst_slot]


def _split_peers():
    mx, my, mc = _my_pos()
    for r in range(1, N_DEV):
        peer = (1 - mx if r & 4 else mx, 1 - my if r & 2 else my, 1 - mc if r & 1 else mc)
        yield r - 1, peer, _slot_of(*peer)


def split_start(modes, srcs, dsts, name):
    n = len(modes)

    def body(*refs):
        send_sems, recv_sems, token = refs[2 * n], refs[2 * n + 1], refs[-1]
        mine = _slot_of(*_my_pos())
        for k, peer, ps in _split_peers():
            for i in range(n):
                src, dst = _split_ends(modes[i], refs[i], refs[n + i], ps, mine)
                pltpu.make_async_remote_copy(
                    src_ref=src, dst_ref=dst, send_sem=send_sems.at[n * k + i], recv_sem=recv_sems.at[n * k + i],
                    device_id=peer, device_id_type=MESH_ID).start()
        token[...] = jnp.zeros_like(token)

    bufs = tuple(srcs) + tuple(dsts)
    sems = pltpu.SemaphoreType.DMA((n * (N_DEV - 1),))
    res = pl.pallas_call(
        body, name=name,
        out_shape=(sems, sems) + tuple(pltpu.HBM(t.shape, t.dtype) for t in bufs)
        + (jax.ShapeDtypeStruct((8, LANE), F32),),
        in_specs=[HBM_SPEC] * (2 * n),
        out_specs=(SEM_SPEC, SEM_SPEC) + (HBM_SPEC,) * (2 * n) + (pl.BlockSpec(memory_space=pltpu.VMEM),),
        input_output_aliases={i: 2 + i for i in range(2 * n)},
        compiler_params=SPLIT_EFFECT,
    )(*(pltpu.with_memory_space_constraint(t, pltpu.HBM) for t in bufs))
    return list(res[:-1]), res[-1][0, 0]


def split_wait(modes, started, after, name):
    n = len(modes)
    send_sems, recv_sems, bufs = started[0], started[1], started[2:]

    def body(*refs):
        send_sems, recv_sems = refs[2 * n], refs[2 * n + 1]
        mine = _slot_of(*_my_pos())
        for k, peer, ps in _split_peers():
            for i in range(n):
                sent = _split_ends(modes[i], refs[i], refs[n + i], ps, mine)[0]
                landed = _split_ends(modes[i], refs[i], refs[n + i], mine, ps)[1]
                cp = pltpu.make_async_remote_copy(
                    src_ref=sent, dst_ref=landed, send_sem=send_sems.at[n * k + i],
                    recv_sem=recv_sems.at[n * k + i], device_id=peer, device_id_type=MESH_ID)
                cp.wait_send()
                cp.wait_recv()

    res = pl.pallas_call(
        body, name=name,
        out_shape=tuple(pltpu.HBM(t.shape, t.dtype) for t in bufs),
        in_specs=[HBM_SPEC] * (2 * n) + [SEM_SPEC, SEM_SPEC, pl.BlockSpec(memory_space=pl.ANY)],
        out_specs=(HBM_SPEC,) * (2 * n),
        input_output_aliases={i: i for i in range(2 * n)},
        compiler_params=SPLIT_EFFECT,
    )(*bufs, send_sems, recv_sems, after)
    return list(res[n:])


def sum_parts(parts, name):
    _, R, C = parts.shape
    tr = _pick(R, (512, 336, 272, 256, 128, 64, 32, 16, 8))

    def body(p_ref, o_ref):
        acc = p_ref[0].astype(F32)
        for k in range(1, N_DEV):
            acc += p_ref[k].astype(F32)
        o_ref[...] = acc

    return pl.pallas_call(
        body, name=name, grid=(R // tr,),
        in_specs=[pl.BlockSpec((N_DEV, tr, C), lambda i: (0, i, 0))],
        out_specs=pl.BlockSpec((tr, C), lambda i: (i, 0)),
        out_shape=jax.ShapeDtypeStruct((R, C), F32),
        compiler_params=_cp("parallel"),
    )(parts)


def adamw(g, w, m, v, name):
    R, C = g.shape
    tr = _pick(R, (512, 352, 256, 128, 64, 32, 16, 8))
    c1 = 1.0 - ADAM_B1 ** ADAM_STEP
    c2 = 1.0 - ADAM_B2 ** ADAM_STEP

    def body(g_ref, w_ref, m_ref, v_ref, d_ref, nm_ref, nv_ref):
        gg = g_ref[...]
        mm = ADAM_B1 * m_ref[...] + (1.0 - ADAM_B1) * gg
        vv = ADAM_B2 * v_ref[...] + (1.0 - ADAM_B2) * (gg * gg)
        nm_ref[...] = mm
        nv_ref[...] = vv
        d_ref[...] = -ADAM_LR * ((mm / c1) / (jnp.sqrt(vv / c2) + ADAM_EPS) + ADAM_WD * w_ref[...])

    spec = pl.BlockSpec((tr, C), lambda i: (i, 0))
    return pl.pallas_call(
        body, name=name, grid=(R // tr,),
        in_specs=[spec] * 4, out_specs=[spec] * 3,
        out_shape=[jax.ShapeDtypeStruct((R, C), F32)] * 3,
        compiler_params=_cp("parallel"),
    )(g, w, m, v)


BIG = (("ffn1_w_gate", 1, D, FF), ("ffn1_w_up", 1, D, FF), ("ffn1_w_down", 0, FF, D),
       ("w_in", 1, D, D_IN), ("w_out", 0, D, D),
       ("ffn2_w_gate", 1, D, FF), ("ffn2_w_up", 1, D, FF), ("ffn2_w_down", 0, FF, D),
       ("ple_w_gate", 0, D, D), ("ple_w_proj", 1, PLE, D))
SMALL = (("ffn1_norm", (D,), None), ("mix_norm", (D,), None), ("lru_conv_w", (4, LRU_W), LRU_W // N_DEV),
         ("lru_conv_b", (LRU_W,), None), ("lru_w_a", (4, HD, HD), None), ("lru_b_a", (LRU_W,), None),
         ("lru_w_x", (4, HD, HD), None), ("lru_b_x", (LRU_W,), None), ("lru_lambda", (LRU_W,), None),
         ("attn_sinks", (ATT_H,), None), ("dn_conv_w", (4, 3 * DN_W), 3 * DN_W // N_DEV),
         ("dn_a_log", (DN_H,), None), ("dn_dt_bias", (DN_H,), None), ("dn_norm", (HD,), None),
         ("ffn2_norm", (D,), None), ("ple_norm", (D,), None))
SINGLE = (("rel_bias", (REL_BUCKETS, ATT_H)), ("final_norm", (D,)))


def _pack_rows(arrs, width, mult):
    flat = jnp.concatenate([a.reshape(-1) for a in arrs])
    rows = -(-flat.shape[0] // (width * mult)) * mult
    return jnp.pad(flat, (0, rows * width - flat.shape[0])).reshape(rows, width)


def _unpack_rows(packed, shapes):
    flat = packed.reshape(-1)
    out, off = [], 0
    for s in shapes:
        n = int(np.prod(s))
        out.append(flat[off:off + n].reshape(s))
        off += n
    return out


COL_NAMES = ("ffn1_w_gate", "ffn1_w_up", "ffn2_w_gate", "ffn2_w_up")


def _shard_cols(a, l):
    blk = jnp.concatenate([a[n][l] for n in COL_NAMES], axis=0)
    return jnp.pad(blk, ((0, 0), (0, SHP - SH))).astype(BF16)


def _shard_rows(a, l):
    to = lambda w, r: jnp.pad(w, ((0, r - w.shape[0]), (0, 0)))
    parts = [to(a["ffn1_w_down"][l], SHP), to(a["ffn2_w_down"][l], SHP), a["w_out"][l], a["ple_w_gate"][l],
             to(a["w_in"][l].reshape(WIN_ROWS, D), B_PPROJ - B_WIN), a["ple_w_proj"][l].reshape(-1, D)]
    return jnp.concatenate(parts, axis=0).astype(BF16)


def _full_w_in(wb):
    sh = wb[:, B_WIN:B_WIN + WIN_ROWS, :].reshape(N_DEV, D, D_IN // N_DEV)
    return jnp.pad(sh.transpose(1, 0, 2).reshape(D, D_IN), ((0, 0), (0, D_IN_PAD - D_IN)))


def _full_ple_proj(wb):
    sh = wb[:, B_PPROJ:B_ROWS, :].reshape(N_DEV, PLE, D // N_DEV)
    return sh.transpose(1, 0, 2).reshape(PLE, D)


PIECE_NAMES = {"late": ("ffn1_w_gate", "ffn1_w_up", "ffn1_w_down"),
               "early": ("ffn2_w_gate", "ffn2_w_up", "ffn2_w_down", "w_out", "ple_w_gate", "w_in", "ple_w_proj")}


def _shard_grads(piece, cols, rows):
    ffn = "ffn1" if piece == "late" else "ffn2"
    g = {ffn + "_w_gate": cols[:D, :SH], ffn + "_w_up": cols[D:, :SH], ffn + "_w_down": rows[:SH]}
    if piece == "early":
        g["w_out"] = rows[GE_WOUT:GE_WOUT + ROWS_DEV]
        g["ple_w_gate"] = rows[GE_PGATE:GE_PGATE + ROWS_DEV]
        g["w_in"] = rows[GE_WIN:GE_WIN + WIN_ROWS].reshape(D, D_IN // N_DEV)
        g["ple_w_proj"] = rows[GE_PPROJ:GE_END].reshape(PLE, D // N_DEV)
    return g


def kernel(x, p, ffn1_norm, ffn1_w_gate, ffn1_w_up, ffn1_w_down, mix_norm, w_in, lru_conv_w, lru_conv_b, lru_w_a, lru_b_a, lru_w_x, lru_b_x, lru_lambda, attn_sinks, rel_bias, dn_conv_w, dn_a_log, dn_dt_bias, dn_norm, w_out, ffn2_norm, ffn2_w_gate, ffn2_w_up, ffn2_w_down, ple_norm, ple_w_gate, ple_w_proj, final_norm, loss_target, m_ffn1_norm, m_ffn1_w_gate, m_ffn1_w_up, m_ffn1_w_down, m_mix_norm, m_w_in, m_lru_conv_w, m_lru_conv_b, m_lru_w_a, m_lru_b_a, m_lru_w_x, m_lru_b_x, m_lru_lambda, m_attn_sinks, m_rel_bias, m_dn_conv_w, m_dn_a_log, m_dn_dt_bias, m_dn_norm, m_w_out, m_ffn2_norm, m_ffn2_w_gate, m_ffn2_w_up, m_ffn2_w_down, m_ple_norm, m_ple_w_gate, m_ple_w_proj, m_final_norm, v_ffn1_norm, v_ffn1_w_gate, v_ffn1_w_up, v_ffn1_w_down, v_mix_norm, v_w_in, v_lru_conv_w, v_lru_conv_b, v_lru_w_a, v_lru_b_a, v_lru_w_x, v_lru_b_x, v_lru_lambda, v_attn_sinks, v_rel_bias, v_dn_conv_w, v_dn_a_log, v_dn_dt_bias, v_dn_norm, v_w_out, v_ffn2_norm, v_ffn2_w_gate, v_ffn2_w_up, v_ffn2_w_down, v_ple_norm, v_ple_w_gate, v_ple_w_proj, v_final_norm):
    a = dict(locals())
    nb, S, _ = x.shape
    T = nb * S
    my_slot = _slot_of(*_my_pos())

    W = {"wa": [None] * DEPTH, "wb": [None] * DEPTH, "w_in": [None] * DEPTH, "ple_w_proj": [None] * DEPTH}

    def set_layer_weights(l, wa, wb):
        W["wa"][l], W["wb"][l] = wa, wb
        W["w_in"][l], W["ple_w_proj"][l] = _full_w_in(wb), _full_ple_proj(wb)

    def landing(mode, src):
        if mode == "gather_cols":
            return lax.dynamic_update_slice(lax.empty((src.shape[0], FFP), src.dtype), src, (0, my_slot * SHP))
        if mode == "scatter_cols":
            src = lax.dynamic_slice(src, (0, my_slot * SHP), (src.shape[0], SHP))
        elif mode == "scatter_block":
            src = lax.dynamic_slice(src, (my_slot, 0, 0), (1,) + src.shape[1:])[0]
        return lax.dynamic_update_slice(lax.empty((N_DEV,) + src.shape, src.dtype), src[None], (my_slot, 0, 0))

    def start(modes, srcs, name):
        return split_start(modes, srcs, [landing(m, s) for m, s in zip(modes, srcs)], name)

    set_layer_weights(0, *gather_layer(_shard_cols(a, 0), _shard_rows(a, 0), "gather_weights_l0"))
    taps = all_gather(_pack_rows([lru_conv_w, dn_conv_w], LANE, 8), "gather_conv_taps")
    tap_shapes = [lru_conv_w.shape, dn_conv_w.shape]
    lcw, dcw = zip(*[_unpack_rows(taps[k], tap_shapes) for k in range(N_DEV)])
    W["lru_conv_w"] = jnp.concatenate(lcw, axis=-1)
    W["dn_conv_w"] = jnp.concatenate(dcw, axis=-1)
    for name, _, cols in SMALL:
        if cols is None:
            W[name] = a[name]
    W["rel_bias"], W["final_norm"] = rel_bias, final_norm

    GATHER, SCATTER = ("gather_cols", "gather_block"), ("scatter_cols", "scatter_block")
    cols1, rows1, _, _ = lax.optimization_barrier((_shard_cols(a, 1), _shard_rows(a, 1), W["wb"][0], taps))
    gather1, token = start(GATHER, (cols1, rows1), "gather_start_l1")
    W["ffn1_norm"] = ffn1_norm + token
    flight = {}

    def weights_at(l, h):
        if l == 1:
            set_layer_weights(1, *split_wait(GATHER, gather1, h, "gather_wait_l1"))
        return W

    def on_piece(l, piece, cols, rows):
        flight[l, piece], token = start(SCATTER, (cols.reshape(2 * D, FFP), rows), f"exchange_start_l{l}_{piece}")
        return token

    loss_local, dx, grads, d_final = _core(x.reshape(T, D), p.reshape(DEPTH, T, PLE), W,
                                           loss_target.reshape(T, D), S, weights_at, on_piece)
    loss = lax.psum(loss_local, AXES)

    small_full = [jnp.stack([grads[l][name] for l in range(DEPTH)]) for name, _, _ in SMALL]
    small_full += [grads[0]["rel_bias"] + grads[1]["rel_bias"], d_final]
    small_flight, _ = start(("gather_block",), (_pack_rows(small_full, LANE, 8),), "gather_start_small_grads")

    out = {}

    def update(piece, received):
        shards = [_shard_grads(piece, sum_parts(ra, f"sum_col_grads_l{l}_{piece}"),
                               sum_parts(rb, f"sum_row_grads_l{l}_{piece}")) for l, (ra, rb) in enumerate(received)]
        for name in PIECE_NAMES[piece]:
            g = jnp.stack([shards[l][name] for l in range(DEPTH)])
            shape = a[name].shape
            two_d = lambda t: t.reshape(-1, shape[-1])
            res = adamw(two_d(g), two_d(a[name]), two_d(a["m_" + name]), two_d(a["v_" + name]), "adamw_" + name)
            out[name] = (g,) + tuple(r.reshape(shape) for r in res)

    landed = {key: split_wait(SCATTER, flight[key], dx, f"exchange_wait_l{key[0]}_{key[1]}")
              for key in ((1, "early"), (1, "late"), (0, "early"))}
    update("early", [landed[0, "early"], landed[1, "early"]])
    done_early = lax.optimization_barrier(tuple(out[n][1] for n in PIECE_NAMES["early"]))
    small_parts, = split_wait(("gather_block",), small_flight, done_early[0], "gather_wait_small_grads")
    small_sum = sum_parts(small_parts, "sum_small_grads")
    g_small = dict(zip([n for n, _, _ in SMALL] + [n for n, _ in SINGLE],
                       _unpack_rows(small_sum, [s.shape for s in small_full])))
    for name, _, cols in SMALL:
        if cols is not None:
            g_small[name] = lax.dynamic_slice_in_dim(g_small[name], my_slot * cols, cols, axis=2)

    small_names = [n for n, _, _ in SMALL] + [n for n, _ in SINGLE]
    shapes = [a[n].shape for n in small_names]
    packed = [_pack_rows([a[pre + n] if pre is not None else g_small[n] for n in small_names], LANE, 8)
              for pre in (None, "", "m_", "v_")]
    res = adamw(*packed, "adamw_small")
    unpacked = [_unpack_rows(r, shapes) for r in res]
    for i, n in enumerate(small_names):
        out[n] = (g_small[n].reshape(shapes[i]),) + tuple(u[i] for u in unpacked)

    landed[0, "late"] = split_wait(SCATTER, flight[0, "late"], lax.optimization_barrier((res[0], done_early[1]))[0],
                                   "exchange_wait_l0_late")
    update("late", [landed[0, "late"], landed[1, "late"]])

    order = ['ffn1_norm', 'ffn1_w_gate', 'ffn1_w_up', 'ffn1_w_down', 'mix_norm', 'w_in', 'lru_conv_w', 'lru_conv_b',
             'lru_w_a', 'lru_b_a', 'lru_w_x', 'lru_b_x', 'lru_lambda', 'attn_sinks', 'rel_bias', 'dn_conv_w',
             'dn_a_log', 'dn_dt_bias', 'dn_norm', 'w_out', 'ffn2_norm', 'ffn2_w_gate', 'ffn2_w_up', 'ffn2_w_down',
             'ple_norm', 'ple_w_gate', 'ple_w_proj', 'final_norm']
    return (loss, dx.reshape(x.shape)) + tuple(out[n][k] for k in range(4) for n in order)
```

```python
import functools
import math

import numpy as np
import jax
import jax.numpy as jnp
from jax import lax
from jax.experimental import pallas as pl
from jax.experimental.pallas import tpu as pltpu

F32 = jnp.float32
BF16 = jnp.bfloat16
HI = lax.Precision.HIGHEST

D = 1024
DEPTH = 2
EPS = 1e-6
PLE = 256
FF = 2816
HD = 64
LRU_W = 256
LRU_C = 8.0
ATT_W = 512
ATT_H = 8
ATT_KV = 2
ATT_G = 4
KV_W = 128
WINDOW = 128
BQ = 128
REL_BUCKETS = 32
REL_MAX_DIST = 128
DN_W = 256
DN_H = 4
CHUNK = 64
D_IN = 2312
D_IN_PAD = 2432
N_DEV = 8

ADAM_LR = 0.001
ADAM_B1 = 0.9
ADAM_B2 = 0.999
ADAM_EPS = 1e-08
ADAM_WD = 0.01
ADAM_STEP = 10

LANE = 128
VMEM_LIMIT = 56 * 1024 * 1024
SH = FF // N_DEV
SHP = 384
FFP = N_DEV * SHP
FF_TILE = 2 * SHP
TOK_TILE = 512
B_DOWN1, B_DOWN2, B_WOUT, B_PGATE, B_WIN, B_PPROJ, B_ROWS = 0, 384, 768, 896, 1024, 1328, 1360
WIN_ROWS = D * D_IN // N_DEV // 1024
NEG = -1e30


def _cp(*sem):
    return pltpu.CompilerParams(dimension_semantics=tuple(sem), vmem_limit_bytes=VMEM_LIMIT)


def _dg(a, b, ca, cb, exact):
    dims = (((ca,), (cb,)), ((), ()))
    if exact == "f32":
        return lax.dot_general(a.astype(F32), b.astype(F32), dims, precision=HI, preferred_element_type=F32)
    if exact == "split":
        a_hi, b_hi = a.astype(BF16), b.astype(BF16)
        a_lo = (a - a_hi.astype(F32)).astype(BF16)
        b_lo = (b - b_hi.astype(F32)).astype(BF16)
        dot = lambda u, v: lax.dot_general(u, v, dims, preferred_element_type=F32)
        return dot(a_hi, b_hi) + (dot(a_hi, b_lo) + dot(a_lo, b_hi))
    return lax.dot_general(a.astype(BF16), b.astype(BF16), dims, preferred_element_type=F32)


def _make_mm(exact):
    @jax.custom_vjp
    def mm(a, b):
        return _dg(a, b, 1, 0, exact)

    @jax.custom_vjp
    def mm_nt(a, b):
        return _dg(a, b, 1, 1, exact)

    @jax.custom_vjp
    def mm_tn(a, b):
        return _dg(a, b, 0, 0, exact)

    mm.defvjp(lambda a, b: (mm(a, b), (a, b)),
              lambda r, d: (mm_nt(d, r[1]), mm_tn(r[0], d)))
    mm_nt.defvjp(lambda a, b: (mm_nt(a, b), (a, b)),
                 lambda r, d: (mm(d, r[1]), mm_tn(d, r[0])))
    mm_tn.defvjp(lambda a, b: (mm_tn(a, b), (a, b)),
                 lambda r, d: (mm_nt(r[1], d), mm(r[0], d)))
    return mm, mm_nt, mm_tn


_mm, _mm_nt, _mm_tn = _make_mm("bf16")
_mmx, _mmx_nt, _mmx_tn = _make_mm("f32")
_mm3, _mm3_nt, _mm3_tn = _make_mm("split")


def _iota(shape, dim):
    return lax.broadcasted_iota(jnp.int32, shape, dim)


def _sigmoid(x):
    return 1.0 / (1.0 + jnp.exp(-x))


def _rms(h, g):
    rstd = lax.rsqrt(jnp.mean(h * h, axis=-1, keepdims=True) + EPS)
    xhat = h * rstd
    return xhat * g, xhat, rstd


def _rms_bwd(dxn, xhat, rstd, g):
    dxhat = dxn * g
    dh = rstd * (dxhat - xhat * jnp.mean(dxhat * xhat, axis=-1, keepdims=True))
    dg = jnp.sum(dxn * xhat, axis=0, keepdims=True)
    return dh, dg


def _row_spec(tm, n):
    return pl.BlockSpec((tm, n), lambda i, *_: (i, 0))


def _full_spec(shape):
    nd = len(shape)
    return pl.BlockSpec(shape, lambda *_: (0,) * nd)


def _ffn_weight_specs(fidx):
    return [pl.BlockSpec((D, FF_TILE), lambda i, j: (2 * fidx, j)),
            pl.BlockSpec((D, FF_TILE), lambda i, j: (2 * fidx + 1, j)),
            pl.BlockSpec((2, SHP, D), lambda i, j: (j, fidx, 0))]


def ffn_fwd(h, g, wa, wb, fidx, name):
    T = h.shape[0]
    tm = min(TOK_TILE, T)
    nj = FFP // FF_TILE

    def body(h_ref, g_ref, wg_ref, wu_ref, wd_ref, o_ref, gt_ref, up_ref, xn_ref):
        j = pl.program_id(1)

        @pl.when(j == 0)
        def _():
            hh = h_ref[...]
            xn_ref[...] = _rms(hh, g_ref[...])[0].astype(BF16)
            o_ref[...] = hh

        xn = xn_ref[...]
        gt = _mm(xn, wg_ref[...])
        up = _mm(xn, wu_ref[...])
        gt_ref[...] = gt.astype(BF16)
        up_ref[...] = up.astype(BF16)
        act = gt * _sigmoid(gt) * up
        o_ref[...] += 0.5 * _mm(act, wd_ref[...].reshape(FF_TILE, D))

    tile = pl.BlockSpec((tm, FF_TILE), lambda i, j: (i, j))
    return pl.pallas_call(
        body, name=name, grid=(T // tm, nj),
        in_specs=[pl.BlockSpec((tm, D), lambda i, j: (i, 0)),
                  pl.BlockSpec((1, D), lambda i, j: (0, 0))] + _ffn_weight_specs(fidx),
        out_specs=[pl.BlockSpec((tm, D), lambda i, j: (i, 0)), tile, tile,
                   pl.BlockSpec((tm, D), lambda i, j: (i, 0))],
        out_shape=[jax.ShapeDtypeStruct((T, D), F32), jax.ShapeDtypeStruct((T, FFP), BF16),
                   jax.ShapeDtypeStruct((T, FFP), BF16), jax.ShapeDtypeStruct((T, D), BF16)],
        compiler_params=_cp("parallel", "arbitrary"),
    )(h, g, wa, wa, wb)


def ffn_bwd(h, dy, g, gt_saved, up_saved, wa, wb, fidx, name):
    T = h.shape[0]
    tm = min(TOK_TILE, T)
    nj = FFP // FF_TILE

    def body(h_ref, dy_ref, g_ref, gt_ref, up_ref, wg_ref, wu_ref, wd_ref,
             dh_ref, dg_ref, du_ref, a_ref, dn_ref, dxn_s):
        i = pl.program_id(0)
        j = pl.program_id(1)

        @pl.when(j == 0)
        def _():
            dxn_s[...] = jnp.zeros_like(dxn_s)

        @pl.when((i == 0) & (j == 0))
        def _():
            dn_ref[...] = jnp.zeros_like(dn_ref)

        gt = gt_ref[...].astype(F32)
        up = up_ref[...].astype(F32)
        sg = _sigmoid(gt)
        si = gt * sg
        da = _mm_nt(0.5 * dy_ref[...], wd_ref[...].reshape(FF_TILE, D))
        dup = da * si
        dgt = da * up * (sg * (1.0 + gt * (1.0 - sg)))
        dg_ref[...] = dgt.astype(BF16)
        du_ref[...] = dup.astype(BF16)
        a_ref[...] = (si * up).astype(BF16)
        dxn_s[...] += _mm_nt(dgt, wg_ref[...]) + _mm_nt(dup, wu_ref[...])

        @pl.when(j == nj - 1)
        def _():
            gg = g_ref[...]
            _, xhat, rstd = _rms(h_ref[...], gg)
            dh, dn = _rms_bwd(dxn_s[...], xhat, rstd, gg)
            dh_ref[...] = dy_ref[...] + dh
            dn_ref[...] += dn

    tile = pl.BlockSpec((tm, FF_TILE), lambda i, j: (i, j))
    return pl.pallas_call(
        body, name=name, grid=(T // tm, nj),
        in_specs=[pl.BlockSpec((tm, D), lambda i, j: (i, 0)),
                  pl.BlockSpec((tm, D), lambda i, j: (i, 0)),
                  pl.BlockSpec((1, D), lambda i, j: (0, 0)), tile, tile] + _ffn_weight_specs(fidx),
        out_specs=[pl.BlockSpec((tm, D), lambda i, j: (i, 0)), tile, tile, tile,
                   pl.BlockSpec((1, D), lambda i, j: (0, 0))],
        out_shape=[jax.ShapeDtypeStruct((T, D), F32)] + [jax.ShapeDtypeStruct((T, FFP), BF16)] * 3
        + [jax.ShapeDtypeStruct((1, D), F32)],
        scratch_shapes=[pltpu.VMEM((tm, D), F32)],
        compiler_params=_cp("arbitrary", "arbitrary"),
    )(h, dy, g, gt_saved, up_saved, wa, wa, wb)


def _pick(n, prefs):
    for t in prefs:
        if n % t == 0:
            return t
    return n


def _tn_body(nk, scale, out_dtype, squeeze):
    def body(a_ref, b_ref, *rest):
        o_ref, acc = rest[-2], rest[-1]
        k = pl.program_id(2)

        @pl.when(k == 0)
        def _():
            acc[...] = jnp.zeros_like(acc)

        acc[...] += _mm_tn(a_ref[...], b_ref[...])

        @pl.when(k == nk - 1)
        def _():
            res = (scale * acc[...]).astype(out_dtype)
            if squeeze:
                o_ref[0] = res
            else:
                o_ref[...] = res

    return body


def matmul_tn(a, b, name, scale=1.0, out_dtype=BF16):
    T, M = a.shape
    N = b.shape[1]
    tmm = _pick(M, (512, 256))
    tnn = _pick(N, (1024, 2432))
    tk = min(TOK_TILE, T)
    nk = T // tk
    return pl.pallas_call(
        _tn_body(nk, scale, out_dtype, False), name=name, grid=(M // tmm, N // tnn, nk),
        in_specs=[pl.BlockSpec((tk, tmm), lambda i, j, k: (k, i)),
                  pl.BlockSpec((tk, tnn), lambda i, j, k: (k, j))],
        out_specs=pl.BlockSpec((tmm, tnn), lambda i, j, k: (i, j)),
        out_shape=jax.ShapeDtypeStruct((M, N), out_dtype),
        scratch_shapes=[pltpu.VMEM((tmm, tnn), F32)],
        compiler_params=_cp("parallel", "parallel", "arbitrary"),
    )(a, b)


def grad_cols(a, b, dst, slot, name):
    T = a.shape[0]
    tmm, tnn = D, FFP // 2
    tk = min(TOK_TILE, T)
    nk = T // tk
    return pl.pallas_call(
        _tn_body(nk, 1.0, BF16, True), name=name, grid=(D // tmm, FFP // tnn, nk),
        in_specs=[pl.BlockSpec((tk, tmm), lambda i, j, k: (k, i)),
                  pl.BlockSpec((tk, tnn), lambda i, j, k: (k, j)),
                  pl.BlockSpec(memory_space=pl.ANY)],
        out_specs=pl.BlockSpec((1, tmm, tnn), lambda i, j, k: (slot, i, j)),
        out_shape=jax.ShapeDtypeStruct(dst.shape, dst.dtype),
        scratch_shapes=[pltpu.VMEM((tmm, tnn), F32)],
        input_output_aliases={2: 0},
        compiler_params=_cp("parallel", "parallel", "arbitrary"),
    )(a, b, dst)


def grad_rows(a, b, dst, row0, rows, name, scale=1.0):
    T = a.shape[0]
    tk = min(TOK_TILE, T)
    nk = T // tk
    blk = row0 // rows

    def body(a_ref, b_ref, dst_ref, o_ref, acc):
        k = pl.program_id(0)

        @pl.when(k == 0)
        def _():
            acc[...] = jnp.zeros_like(acc)

        acc[...] += _mm_tn(a_ref[...], b_ref[...])

        @pl.when(k == nk - 1)
        def _():
            o_ref[...] = (scale * acc[...]).astype(BF16).reshape(N_DEV, rows, D)

    return pl.pallas_call(
        body, name=name, grid=(nk,),
        in_specs=[pl.BlockSpec((tk, N_DEV * rows), lambda k: (k, 0)),
                  pl.BlockSpec((tk, D), lambda k: (k, 0)),
                  pl.BlockSpec(memory_space=pl.ANY)],
        out_specs=pl.BlockSpec((N_DEV, rows, D), lambda k: (0, blk, 0)),
        out_shape=jax.ShapeDtypeStruct(dst.shape, dst.dtype),
        scratch_shapes=[pltpu.VMEM((N_DEV * rows, D), F32)],
        input_output_aliases={2: 0},
        compiler_params=_cp("arbitrary"),
    )(a, b, dst)


U_SPLITS = (512, 768, 1024, 128)
U_OFFS = (0, 512, 1280, 2304)


def mixin_fwd(h, g, w_in, name):
    T = h.shape[0]
    tm = min(TOK_TILE, T)

    def body(h_ref, g_ref, w_ref, u0, u1, u2, u3, xn_ref):
        xn = _rms(h_ref[...], g_ref[...])[0].astype(BF16)
        xn_ref[...] = xn
        u = _mm(xn, w_ref[...])
        for ref, off, n in zip((u0, u1, u2, u3), U_OFFS, U_SPLITS):
            ref[...] = u[:, off:off + n]

    return pl.pallas_call(
        body, name=name, grid=(T // tm,),
        in_specs=[_row_spec(tm, D), _full_spec((1, D)), _full_spec((D, D_IN_PAD))],
        out_specs=[_row_spec(tm, n) for n in U_SPLITS] + [_row_spec(tm, D)],
        out_shape=[jax.ShapeDtypeStruct((T, n), F32) for n in U_SPLITS]
        + [jax.ShapeDtypeStruct((T, D), BF16)],
        compiler_params=_cp("parallel"),
    )(h, g, w_in)


def mixin_bwd(h, dh_in, g, w_in, dus, name):
    T = h.shape[0]
    tm = min(TOK_TILE, T)

    def body(h_ref, dhi_ref, g_ref, w_ref, d0, d1, d2, d3, dh_ref, du_ref, dn_ref):
        @pl.when(pl.program_id(0) == 0)
        def _():
            dn_ref[...] = jnp.zeros_like(dn_ref)

        dxn = jnp.zeros((tm, D), F32)
        for ref, off, n in zip((d0, d1, d2, d3), U_OFFS, U_SPLITS):
            du = ref[...]
            du_ref[:, off:off + n] = du.astype(BF16)
            dxn += _mm_nt(du, w_ref[:, off:off + n])
        gg = g_ref[...]
        _, xhat, rstd = _rms(h_ref[...], gg)
        dh, dn = _rms_bwd(dxn, xhat, rstd, gg)
        dh_ref[...] = dhi_ref[...] + dh
        dn_ref[...] += dn

    return pl.pallas_call(
        body, name=name, grid=(T // tm,),
        in_specs=[_row_spec(tm, D), _row_spec(tm, D), _full_spec((1, D)), _full_spec((D, D_IN_PAD))]
        + [_row_spec(tm, n) for n in U_SPLITS],
        out_specs=[_row_spec(tm, D), _row_spec(tm, D_IN_PAD), _full_spec((1, D))],
        out_shape=[jax.ShapeDtypeStruct((T, D), F32), jax.ShapeDtypeStruct((T, D_IN_PAD), BF16),
                   jax.ShapeDtypeStruct((1, D), F32)],
        compiler_params=_cp("arbitrary"),
    )(h, dh_in, g, w_in, *dus)


def _shift_down(x, s, row):
    if s == 0:
        return x
    return jnp.where(row >= s, pltpu.roll(x, s, 0), 0.0)


def _shift_up(x, s, row):
    if s == 0:
        return x
    n = x.shape[0]
    return jnp.where(row < n - s, pltpu.roll(x, n - s, 0), 0.0)


def conv_fwd(x, w, b, S, col0, C, name):
    T = x.shape[0]
    cb0 = col0 // LANE

    def body(x_ref, w_ref, b_ref, y_ref):
        xx = x_ref[...]
        row = _iota(xx.shape, 0)
        y = xx * w_ref[3:4, :] + b_ref[...]
        for k in range(3):
            y += _shift_down(xx, 3 - k, row) * w_ref[k:k + 1, :]
        y_ref[...] = y

    return pl.pallas_call(
        body, name=name, grid=(T // S, C // LANE),
        in_specs=[pl.BlockSpec((S, LANE), lambda s, c: (s, cb0 + c)),
                  pl.BlockSpec((4, LANE), lambda s, c: (0, c)),
                  pl.BlockSpec((1, LANE), lambda s, c: (0, c))],
        out_specs=pl.BlockSpec((S, LANE), lambda s, c: (s, c)),
        out_shape=jax.ShapeDtypeStruct((T, C), F32),
        compiler_params=_cp("parallel", "parallel"),
    )(x, w, b)


def conv_bwd(x, dy, w, S, col0, C, name):
    T = x.shape[0]
    cb0 = col0 // LANE

    def body(x_ref, dy_ref, w_ref, dx_ref, dwb_ref):
        @pl.when(pl.program_id(1) == 0)
        def _():
            dwb_ref[...] = jnp.zeros_like(dwb_ref)

        xx = x_ref[...]
        dd = dy_ref[...]
        row = _iota(xx.shape, 0)
        dx = dd * w_ref[3:4, :]
        for k in range(3):
            dx += _shift_up(dd, 3 - k, row) * w_ref[k:k + 1, :]
        dx_ref[...] = dx
        for k in range(4):
            dwb_ref[k:k + 1, :] += jnp.sum(dd * _shift_down(xx, 3 - k, row), axis=0, keepdims=True)
        dwb_ref[4:5, :] += jnp.sum(dd, axis=0, keepdims=True)

    return pl.pallas_call(
        body, name=name, grid=(C // LANE, T // S),
        in_specs=[pl.BlockSpec((S, LANE), lambda c, s: (s, cb0 + c)),
                  pl.BlockSpec((S, LANE), lambda c, s: (s, c)),
                  pl.BlockSpec((4, LANE), lambda c, s: (0, c))],
        out_specs=[pl.BlockSpec((S, LANE), lambda c, s: (s, c)),
                   pl.BlockSpec((8, LANE), lambda c, s: (0, c))],
        out_shape=[jax.ShapeDtypeStruct((T, C), F32), jax.ShapeDtypeStruct((8, C), F32)],
        compiler_params=_cp("parallel", "arbitrary"),
    )(x, dy, w)


def _scan(a, b, row):
    n = a.shape[0]
    d = 1
    while d < n:
        keep = row >= d
        b = a * jnp.where(keep, pltpu.roll(b, d, 0), 0.0) + b
        a = a * jnp.where(keep, pltpu.roll(a, d, 0), 1.0)
        d *= 2
    return b


def _rscan(a, b, row):
    n = a.shape[0]
    d = 1
    while d < n:
        keep = row < n - d
        b = a * jnp.where(keep, pltpu.roll(b, n - d, 0), 0.0) + b
        a = a * jnp.where(keep, pltpu.roll(a, n - d, 0), 1.0)
        d *= 2
    return b


GELU_C = math.sqrt(2.0 / math.pi)


def _gelu(x):
    t = jnp.tanh(GELU_C * (x + 0.044715 * (x * x * x)))
    return 0.5 * x * (1.0 + t), t


def _lru_gates(xr, wa, ba, wx, bx, lam):
    r = _sigmoid(_mm(xr, wa) + ba)
    i = _sigmoid(_mm(xr, wx) + bx)
    sp = jnp.maximum(-lam, 0.0) + jnp.log(1.0 + jnp.exp(-jnp.abs(lam)))
    la = -LRU_C * r * sp
    a = jnp.exp(la)
    e2 = a * a
    m = jnp.sqrt(-jnp.tanh(la) * (e2 + 1.0))
    return r, i, sp, a, e2, m


def lru_fwd(xr, u_lru, wa, wx, vec, S, name):
    T = xr.shape[0]

    def body(xr_ref, gt_ref, wa_ref, wx_ref, vec_ref, y_ref):
        x = xr_ref[...]
        row = _iota(x.shape, 0)
        r, i, sp, a, e2, m = _lru_gates(x, wa_ref[...], vec_ref[0:1, :], wx_ref[...], vec_ref[1:2, :],
                                        vec_ref[2:3, :])
        hh = _scan(a, m * (i * x), row)
        y_ref[...] = _gelu(gt_ref[...])[0] * hh

    return pl.pallas_call(
        body, name=name, grid=(T // S, LRU_W // LANE),
        in_specs=[pl.BlockSpec((S, LANE), lambda s, c: (s, c)),
                  pl.BlockSpec((S, LANE), lambda s, c: (s, 2 + c)),
                  pl.BlockSpec((LANE, LANE), lambda s, c: (c, c)),
                  pl.BlockSpec((LANE, LANE), lambda s, c: (c, c)),
                  pl.BlockSpec((8, LANE), lambda s, c: (0, c))],
        out_specs=pl.BlockSpec((S, LANE), lambda s, c: (s, c)),
        out_shape=jax.ShapeDtypeStruct((T, LRU_W), F32),
        compiler_params=_cp("parallel", "parallel"),
    )(xr, u_lru, wa, wx, vec)


def lru_bwd(xr, u_lru, dy, wa, wx, vec, S, name):
    T = xr.shape[0]

    def body(xr_ref, gt_ref, dy_ref, wa_ref, wx_ref, vec_ref,
             dxr_ref, dgt_ref, dwa_ref, dwx_ref, dvec_ref):
        @pl.when(pl.program_id(1) == 0)
        def _():
            dwa_ref[...] = jnp.zeros_like(dwa_ref)
            dwx_ref[...] = jnp.zeros_like(dwx_ref)
            dvec_ref[...] = jnp.zeros_like(dvec_ref)

        x = xr_ref[...]
        n = x.shape[0]
        row = _iota(x.shape, 0)
        lam = vec_ref[2:3, :]
        r, i, sp, a, e2, m = _lru_gates(x, wa_ref[...], vec_ref[0:1, :], wx_ref[...], vec_ref[1:2, :], lam)
        v = i * x
        hh = _scan(a, m * v, row)
        gt = gt_ref[...]
        dy = dy_ref[...]
        ge, t = _gelu(gt)
        dgt_ref[...] = dy * hh * (0.5 * (1.0 + t) + 0.5 * gt * (1.0 - t * t) * GELU_C
                                  * (1.0 + 3.0 * 0.044715 * gt * gt))
        a_next = jnp.where(row < n - 1, pltpu.roll(a, n - 1, 0), 0.0)
        G = _rscan(a_next, dy * ge, row)
        da = G * _shift_down(hh, 1, row)
        dv = G * m
        dla = da * a - (G * v) * e2 / m
        dr = dla * (-LRU_C * sp)
        dsp = jnp.sum(dla * (-LRU_C * r), axis=0, keepdims=True)
        dra = dr * r * (1.0 - r)
        dia = (dv * x) * i * (1.0 - i)
        dxr_ref[...] = dv * i + _mm_nt(dra, wa_ref[...]) + _mm_nt(dia, wx_ref[...])
        dwa_ref[0] += _mm_tn(x, dra)
        dwx_ref[0] += _mm_tn(x, dia)
        dvec_ref[0:1, :] += jnp.sum(dra, axis=0, keepdims=True)
        dvec_ref[1:2, :] += jnp.sum(dia, axis=0, keepdims=True)
        dvec_ref[2:3, :] += dsp * (-_sigmoid(-lam))

    return pl.pallas_call(
        body, name=name, grid=(LRU_W // LANE, T // S),
        in_specs=[pl.BlockSpec((S, LANE), lambda c, s: (s, c)),
                  pl.BlockSpec((S, LANE), lambda c, s: (s, 2 + c)),
                  pl.BlockSpec((S, LANE), lambda c, s: (s, c)),
                  pl.BlockSpec((LANE, LANE), lambda c, s: (c, c)),
                  pl.BlockSpec((LANE, LANE), lambda c, s: (c, c)),
                  pl.BlockSpec((8, LANE), lambda c, s: (0, c))],
        out_specs=[pl.BlockSpec((S, LANE), lambda c, s: (s, c)),
                   pl.BlockSpec((S, LANE), lambda c, s: (s, c)),
                   pl.BlockSpec((1, LANE, LANE), lambda c, s: (c, 0, 0)),
                   pl.BlockSpec((1, LANE, LANE), lambda c, s: (c, 0, 0)),
                   pl.BlockSpec((8, LANE), lambda c, s: (0, c))],
        out_shape=[jax.ShapeDtypeStruct((T, LRU_W), F32), jax.ShapeDtypeStruct((T, LRU_W), F32),
                   jax.ShapeDtypeStruct((2, LANE, LANE), F32), jax.ShapeDtypeStruct((2, LANE, LANE), F32),
                   jax.ShapeDtypeStruct((8, LRU_W), F32)],
        compiler_params=_cp("parallel", "arbitrary"),
    )(xr, u_lru, dy, wa, wx, vec)


def _bucket_table():
    qi = np.arange(BQ)[:, None]
    kj = np.arange(2 * BQ)[None, :]
    dist = BQ + qi - kj
    band = (dist >= 0) & (dist < WINDOW)
    dd = np.maximum(dist, 0)
    max_exact = REL_BUCKETS // 2
    large = max_exact + (np.log(np.maximum(dd, 1).astype(np.float32) / np.float32(max_exact))
                         / np.float32(math.log(REL_MAX_DIST / max_exact))
                         * np.float32(REL_BUCKETS - max_exact)).astype(np.int32)
    large = np.minimum(large, REL_BUCKETS - 1)
    bucket = np.where(dd < max_exact, dd, large)
    return np.where(band, bucket, -1).astype(np.int32)


def _att_specs(S):
    nb = S // BQ
    qc = ATT_W // LANE
    return [pl.BlockSpec((BQ, ATT_W), lambda b, n: (b * nb + n, 0)),
            pl.BlockSpec((BQ, KV_W), lambda b, n: (b * nb + jnp.maximum(n - 1, 0), qc)),
            pl.BlockSpec((BQ, KV_W), lambda b, n: (b * nb + n, qc)),
            pl.BlockSpec((BQ, KV_W), lambda b, n: (b * nb + jnp.maximum(n - 1, 0), qc + 1)),
            pl.BlockSpec((BQ, KV_W), lambda b, n: (b * nb + n, qc + 1))]


def _att_bias(bk, rb_ref, bias_s):
    for h in range(ATT_H):
        acc = jnp.zeros(bk.shape, F32)
        for bb in range(REL_BUCKETS):
            acc = jnp.where(bk == bb, rb_ref[bb * ATT_H + h], acc)
        bias_s[h] = acc


def _att_probs(qh, kg, bias, valid, sink):
    s = _mm_nt(qh, kg) * (HD ** -0.5) + bias
    s = jnp.where(valid, s, NEG)
    m = jnp.maximum(jnp.max(s, axis=-1, keepdims=True), sink)
    e = jnp.exp(s - m)
    es = jnp.exp(sink - m)
    den = jnp.sum(e, axis=-1, keepdims=True) + es
    return e / den, es / den


def attn_fwd(u_att, sinks, rel_bias, S, name):
    T = u_att.shape[0]
    nb = S // BQ
    table = jnp.asarray(_bucket_table())

    def body(sk_ref, rb_ref, bk_ref, q_ref, kp_ref, kc_ref, vp_ref, vc_ref, o_ref, bias_s):
        b = pl.program_id(0)
        n = pl.program_id(1)
        bk = bk_ref[...]

        @pl.when((b == 0) & (n == 0))
        def _():
            _att_bias(bk, rb_ref, bias_s)

        valid = (bk >= 0) & ((n > 0) | (_iota(bk.shape, 1) >= BQ))
        for h in range(ATT_H):
            gs = slice(HD * (h // ATT_G), HD * (h // ATT_G + 1))
            kg = jnp.concatenate([kp_ref[:, gs], kc_ref[:, gs]], axis=0)
            vg = jnp.concatenate([vp_ref[:, gs], vc_ref[:, gs]], axis=0)
            p, _ = _att_probs(q_ref[:, HD * h:HD * (h + 1)], kg, bias_s[h], valid, sk_ref[h])
            o_ref[:, HD * h:HD * (h + 1)] = _mm(p, vg)

    smem = pl.BlockSpec(memory_space=pltpu.SMEM)
    return pl.pallas_call(
        body, name=name, grid=(T // S, nb),
        in_specs=[smem, smem, _full_spec((BQ, 2 * BQ))] + _att_specs(S),
        out_specs=pl.BlockSpec((BQ, ATT_W), lambda b, n: (b * nb + n, 0)),
        out_shape=jax.ShapeDtypeStruct((T, ATT_W), F32),
        scratch_shapes=[pltpu.VMEM((ATT_H, BQ, 2 * BQ), F32)],
        compiler_params=_cp("arbitrary", "arbitrary"),
    )(sinks, rel_bias, table, u_att, u_att, u_att, u_att, u_att)


def attn_bwd(u_att, dy, sinks, rel_bias, S, name):
    T = u_att.shape[0]
    nb = S // BQ
    nB = T // S
    table = jnp.asarray(_bucket_table())
    scale = HD ** -0.5

    def body(sk_ref, rb_ref, bk_ref, q_ref, kp_ref, kc_ref, vp_ref, vc_ref, dy_ref,
             du_ref, drel_ref, dsk_ref, bias_s, dbias_s):
        b = pl.program_id(0)
        n = pl.program_id(1)
        bk = bk_ref[...]

        @pl.when((b == 0) & (n == 0))
        def _():
            _att_bias(bk, rb_ref, bias_s)
            dbias_s[...] = jnp.zeros_like(dbias_s)
            dsk_ref[...] = jnp.zeros_like(dsk_ref)
            drel_ref[...] = jnp.zeros_like(drel_ref)

        @pl.when(n == 0)
        def _():
            du_ref[...] = jnp.zeros_like(du_ref)

        valid = (bk >= 0) & ((n > 0) | (_iota(bk.shape, 1) >= BQ))
        r_cur = pl.multiple_of(n * BQ, BQ)
        r_prev = pl.multiple_of(jnp.maximum(n - 1, 0) * BQ, BQ)
        for g in range(ATT_KV):
            gs = slice(HD * g, HD * (g + 1))
            kg = jnp.concatenate([kp_ref[:, gs], kc_ref[:, gs]], axis=0)
            vg = jnp.concatenate([vp_ref[:, gs], vc_ref[:, gs]], axis=0)
            dk = jnp.zeros((2 * BQ, HD), F32)
            dv = jnp.zeros((2 * BQ, HD), F32)
            for e in range(ATT_G):
                h = g * ATT_G + e
                qh = q_ref[:, HD * h:HD * (h + 1)]
                do = dy_ref[:, HD * h:HD * (h + 1)]
                p, ps = _att_probs(qh, kg, bias_s[h], valid, sk_ref[h])
                dp = _mm_nt(do, vg)
                delta = jnp.sum(p * dp, axis=-1, keepdims=True)
                ds = p * (dp - delta)
                dbias_s[h] += ds
                dsk_ref[h:h + 1, :] += jnp.broadcast_to(
                    jnp.sum(-ps * delta, axis=0, keepdims=True), (1, LANE))
                dss = ds * scale
                du_ref[pl.ds(r_cur, BQ), HD * h:HD * (h + 1)] = _mm(dss, kg)
                dk += _mm_tn(dss, qh)
                dv += _mm_tn(p, do)
            ck = ATT_W + HD * g
            cv = ATT_W + KV_W + HD * g
            du_ref[pl.ds(r_prev, BQ), ck:ck + HD] += dk[0:BQ]
            du_ref[pl.ds(r_cur, BQ), ck:ck + HD] += dk[BQ:]
            du_ref[pl.ds(r_prev, BQ), cv:cv + HD] += dv[0:BQ]
            du_ref[pl.ds(r_cur, BQ), cv:cv + HD] += dv[BQ:]

        @pl.when((b == nB - 1) & (n == nb - 1))
        def _():
            lane = _iota((1, LANE), 1)
            for h in range(ATT_H):
                db = dbias_s[h]
                acc = jnp.zeros((1, LANE), F32)
                for bb in range(REL_BUCKETS):
                    val = jnp.sum(jnp.sum(jnp.where(bk == bb, db, 0.0), axis=1, keepdims=True),
                                  axis=0, keepdims=True)
                    acc = jnp.where(lane == bb, val, acc)
                drel_ref[h:h + 1, :] = acc

    smem = pl.BlockSpec(memory_space=pltpu.SMEM)
    return pl.pallas_call(
        body, name=name, grid=(nB, nb),
        in_specs=[smem, smem, _full_spec((BQ, 2 * BQ))] + _att_specs(S)
        + [pl.BlockSpec((BQ, ATT_W), lambda b, n: (b * nb + n, 0))],
        out_specs=[pl.BlockSpec((S, ATT_W + 2 * KV_W), lambda b, n: (b, 0)),
                   _full_spec((8, LANE)), _full_spec((8, LANE))],
        out_shape=[jax.ShapeDtypeStruct((T, ATT_W + 2 * KV_W), F32),
                   jax.ShapeDtypeStruct((8, LANE), F32), jax.ShapeDtypeStruct((8, LANE), F32)],
        scratch_shapes=[pltpu.VMEM((ATT_H, BQ, 2 * BQ), F32), pltpu.VMEM((ATT_H, BQ, 2 * BQ), F32)],
        compiler_params=_cp("arbitrary", "arbitrary"),
    )(sinks, rel_bias, table, u_att, u_att, u_att, u_att, u_att, dy)


def _head_of(i):
    return lax.shift_right_logical(i, 6)


def _head_mask(shape):
    return (_head_of(_iota(shape, 0)) == _head_of(_iota(shape, 1))).astype(F32)


def _dn_point(c, uba, alog, dtb):
    s = c * _sigmoid(c)
    qt, kt, vt = s[:, 0:256], s[:, 256:512], s[:, 512:768]
    ones_bd = _head_mask((DN_W, DN_W))
    q = qt * lax.rsqrt(_mmx(qt * qt, ones_bd) + EPS) * (HD ** -0.5)
    k = kt * lax.rsqrt(_mmx(kt * kt, ones_bd) + EPS)
    sel = _head_of(_iota((LANE, DN_W), 1))
    row = _iota((LANE, DN_W), 0)
    braw = _mmx(uba, (row == sel).astype(F32))
    araw = _mmx(uba, (row == sel + DN_H).astype(F32)) + dtb
    beta = _sigmoid(braw)
    g = -jnp.exp(alog) * (jnp.maximum(araw, 0.0) + jnp.log(1.0 + jnp.exp(-jnp.abs(araw))))
    return q, k, vt, g, beta


def dn_point_fwd(c, uba, alog, dtb, name):
    T = c.shape[0]
    tm = min(TOK_TILE, T)

    def body(c_ref, u_ref, al_ref, dt_ref, *outs):
        for ref, val in zip(outs, _dn_point(c_ref[...], u_ref[...], al_ref[...], dt_ref[...])):
            ref[...] = val

    return pl.pallas_call(
        body, name=name, grid=(T // tm,),
        in_specs=[_row_spec(tm, 768), _row_spec(tm, LANE), _full_spec((1, DN_W)), _full_spec((1, DN_W))],
        out_specs=[_row_spec(tm, DN_W)] * 5,
        out_shape=[jax.ShapeDtypeStruct((T, DN_W), F32)] * 5,
        compiler_params=_cp("parallel"),
    )(c, uba, alog, dtb)


def dn_point_bwd(c, uba, alog, dtb, douts, name):
    T = c.shape[0]
    tm = min(TOK_TILE, T)

    def body(c_ref, u_ref, al_ref, dt_ref, dq, dk, dv, dg, db, dc_ref, du_ref, dvec_ref):
        @pl.when(pl.program_id(0) == 0)
        def _():
            dvec_ref[...] = jnp.zeros_like(dvec_ref)

        _, vjp = jax.vjp(_dn_point, c_ref[...], u_ref[...], al_ref[...], dt_ref[...])
        dc, du, dal, ddt = vjp((dq[...], dk[...], dv[...], dg[...], db[...]))
        dc_ref[...] = dc
        du_ref[...] = du
        fold = (_iota((LANE, DN_W), 0) == _head_of(_iota((LANE, DN_W), 1))).astype(F32)
        both = jnp.concatenate([dal, ddt, jnp.zeros((6, DN_W), F32)], axis=0)
        dvec_ref[...] += _mmx_nt(both, fold)

    return pl.pallas_call(
        body, name=name, grid=(T // tm,),
        in_specs=[_row_spec(tm, 768), _row_spec(tm, LANE), _full_spec((1, DN_W)), _full_spec((1, DN_W))]
        + [_row_spec(tm, DN_W)] * 5,
        out_specs=[_row_spec(tm, 768), _row_spec(tm, LANE), _full_spec((8, LANE))],
        out_shape=[jax.ShapeDtypeStruct((T, 768), F32), jax.ShapeDtypeStruct((T, LANE), F32),
                   jax.ShapeDtypeStruct((8, LANE), F32)],
        compiler_params=_cp("arbitrary"),
    )(c, uba, alog, dtb, *douts)


def _unit_lower_inverses(lmats):
    eye = (_iota(lmats[0].shape, 0) == _iota(lmats[0].shape, 1)).astype(F32)
    tinvs = [eye - lm for lm in lmats]
    pws = list(lmats)
    for _ in range(5):
        pws = [_mm3(pw, pw) for pw in pws]
        tinvs = [t + _mm3(t, pw) for t, pw in zip(tinvs, pws)]
    return tuple(tinvs)


def _inverse_bwd(tinv, d):
    return -_mm3_nt(_mm3_tn(tinv, d), tinv)


@jax.custom_vjp
def _tri_invs(lmats):
    return _unit_lower_inverses(lmats)


def _tri_invs_fwd(lmats):
    tinvs = _unit_lower_inverses(lmats)
    return tinvs, tinvs


_tri_invs.defvjp(_tri_invs_fwd, lambda tinvs, ds: (tuple(_inverse_bwd(t, d) for t, d in zip(tinvs, ds)),))


@jax.custom_vjp
def _tri_inv_known(lmat, tinv):
    return tinv


_tri_inv_known.defvjp(lambda lmat, tinv: (tinv, tinv),
                      lambda tinv, d: (_inverse_bwd(tinv, d), jnp.zeros_like(tinv)))


DN_SUB = 4


def _dn_stack(x):
    return jnp.concatenate([x, x, x, x], axis=0) * _head_mask((DN_W, DN_W))


def _dn_pre_inverse(q, k, v, g, beta):
    hm = _head_mask((DN_W, DN_W))
    ri = _iota((DN_W, DN_W), 0) & (CHUNK - 1)
    ci = _iota((DN_W, DN_W), 1) & (CHUNK - 1)
    tri64 = (_iota((CHUNK, CHUNK), 0) >= _iota((CHUNK, CHUNK), 1)).astype(F32)
    gc = _mm3(tri64, g)
    ks = _dn_stack(k)
    gcol = jnp.sum(_dn_stack(gc), axis=1, keepdims=True) * (1.0 / HD)
    gmat = jnp.broadcast_to(gcol, (DN_W, DN_W))
    decay = jnp.exp(jnp.minimum(gmat - gmat.T, 0.0))
    lmat = _mm_nt(_dn_stack(k * beta), ks) * decay * (hm * (ri > ci).astype(F32))
    att = _mm_nt(_dn_stack(q), ks) * decay * (hm * (ri >= ci).astype(F32))
    return lmat, att, gc


def _dn_post_inverse(q, k, v, g, beta, tinv, att, gc):
    glast = jnp.sum(g, axis=0, keepdims=True)
    eg = jnp.exp(gc)
    u = _mm(tinv, _dn_stack(v * beta))
    w = _mm(tinv, _dn_stack(k * beta * eg))
    return u, w, att, _dn_stack(q * eg), _dn_stack(k * jnp.exp(glast - gc)), jnp.exp(glast), tinv


def _dn_apply(state, prep):
    u, w, att, qe, kd, eglast, _ = prep
    vn = u - _mm(w, state)
    o4 = _mm(qe, state) + _mm(att, vn)
    o = o4[0:64] + o4[64:128] + o4[128:192] + o4[192:256]
    return o, state * eglast + _mm_tn(kd, vn)


def _dn_chunks(state, q, k, v, g, beta, knowns=None):
    n = q.shape[0] // CHUNK
    chunks = [tuple(x[c * CHUNK:(c + 1) * CHUNK] for x in (q, k, v, g, beta)) for c in range(n)]
    pre = [_dn_pre_inverse(*ch) for ch in chunks]
    if knowns is None:
        tinvs = _tri_invs(tuple(p[0] for p in pre))
    else:
        tinvs = [_tri_inv_known(p[0], known) for p, known in zip(pre, knowns)]
    preps = [_dn_post_inverse(*ch, tinv, p[1], p[2]) for ch, tinv, p in zip(chunks, tinvs, pre)]
    outs = []
    for prep in preps:
        o, state = _dn_apply(state, prep)
        outs.append(o)
    return jnp.concatenate(outs, axis=0), state, [prep[-1] for prep in preps]


def dn_scan_fwd(q, k, v, g, beta, S, name):
    T = q.shape[0]
    rows = DN_SUB * CHUNK
    ns = S // rows

    def body(q_ref, k_ref, v_ref, g_ref, b_ref, o_ref, st_ref, ti_ref, s_s):
        @pl.when(pl.program_id(1) == 0)
        def _():
            s_s[...] = jnp.zeros_like(s_s)

        st = s_s[...]
        st_ref[0] = st
        o, new, tinvs = _dn_chunks(st, q_ref[...], k_ref[...], v_ref[...], g_ref[...], b_ref[...])
        o_ref[...] = o
        for c, tinv in enumerate(tinvs):
            ti_ref[c] = tinv
        s_s[...] = new

    spec = pl.BlockSpec((rows, DN_W), lambda b, t: (b * ns + t, 0))
    return pl.pallas_call(
        body, name=name, grid=(T // S, ns),
        in_specs=[spec] * 5,
        out_specs=[spec, pl.BlockSpec((1, DN_W, DN_W), lambda b, t: (b * ns + t, 0, 0)),
                   pl.BlockSpec((DN_SUB, DN_W, DN_W), lambda b, t: (b * ns + t, 0, 0))],
        out_shape=[jax.ShapeDtypeStruct((T, DN_W), F32),
                   jax.ShapeDtypeStruct((T // rows, DN_W, DN_W), F32),
                   jax.ShapeDtypeStruct((T // CHUNK, DN_W, DN_W), F32)],
        scratch_shapes=[pltpu.VMEM((DN_W, DN_W), F32)],
        compiler_params=_cp("parallel", "arbitrary"),
    )(q, k, v, g, beta)


def dn_scan_bwd(q, k, v, g, beta, states, tinvs, do, S, name):
    T = q.shape[0]
    rows = DN_SUB * CHUNK
    ns = S // rows

    def body(q_ref, k_ref, v_ref, g_ref, b_ref, st_ref, ti_ref, do_ref, dq, dk, dv, dg, db, ds_s):
        @pl.when(pl.program_id(1) == 0)
        def _():
            ds_s[...] = jnp.zeros_like(ds_s)

        knowns = [ti_ref[c] for c in range(DN_SUB)]
        _, vjp = jax.vjp(lambda *args: _dn_chunks(*args, knowns=knowns)[:2],
                         st_ref[0], q_ref[...], k_ref[...], v_ref[...], g_ref[...], b_ref[...])
        grads = vjp((do_ref[...], ds_s[...]))
        ds_s[...] = grads[0]
        for ref, val in zip((dq, dk, dv, dg, db), grads[1:]):
            ref[...] = val

    spec = pl.BlockSpec((rows, DN_W), lambda b, t: (b * ns + ns - 1 - t, 0))
    return pl.pallas_call(
        body, name=name, grid=(T // S, ns),
        in_specs=[spec] * 5 + [pl.BlockSpec((1, DN_W, DN_W), lambda b, t: (b * ns + ns - 1 - t, 0, 0)),
                               pl.BlockSpec((DN_SUB, DN_W, DN_W), lambda b, t: (b * ns + ns - 1 - t, 0, 0)),
                               spec],
        out_specs=[spec] * 5,
        out_shape=[jax.ShapeDtypeStruct((T, DN_W), F32)] * 5,
        scratch_shapes=[pltpu.VMEM((DN_W, DN_W), F32)],
        compiler_params=_cp("parallel", "arbitrary"),
    )(q, k, v, g, beta, states, tinvs, do)


def _dn_gate(o, z, nl):
    ms = _mmx(o * o, _head_mask((DN_W, DN_W))) * (1.0 / HD)
    return o * lax.rsqrt(ms + EPS) * nl * (z * _sigmoid(z))


def dn_gate_fwd(o, u_dn, nl, name):
    T = o.shape[0]
    tm = min(TOK_TILE, T)

    def body(o_ref, z_ref, n_ref, y_ref):
        y_ref[...] = _dn_gate(o_ref[...], z_ref[...], n_ref[...])

    return pl.pallas_call(
        body, name=name, grid=(T // tm,),
        in_specs=[_row_spec(tm, DN_W), pl.BlockSpec((tm, DN_W), lambda i: (i, 3)), _full_spec((1, DN_W))],
        out_specs=_row_spec(tm, DN_W),
        out_shape=jax.ShapeDtypeStruct((T, DN_W), F32),
        compiler_params=_cp("parallel"),
    )(o, u_dn, nl)


def dn_gate_bwd(o, u_dn, nl, dy, name):
    T = o.shape[0]
    tm = min(TOK_TILE, T)

    def body(o_ref, z_ref, n_ref, dy_ref, do_ref, dz_ref, dn_ref):
        @pl.when(pl.program_id(0) == 0)
        def _():
            dn_ref[...] = jnp.zeros_like(dn_ref)

        _, vjp = jax.vjp(_dn_gate, o_ref[...], z_ref[...], n_ref[...])
        do, dz, dn = vjp(dy_ref[...])
        do_ref[...] = do
        dz_ref[...] = dz
        fold = (_iota((LANE, DN_W), 0) == (_iota((LANE, DN_W), 1) & (HD - 1))).astype(F32)
        dn_ref[...] += _mmx_nt(jnp.concatenate([dn, jnp.zeros((7, DN_W), F32)], axis=0), fold)

    return pl.pallas_call(
        body, name=name, grid=(T // tm,),
        in_specs=[_row_spec(tm, DN_W), pl.BlockSpec((tm, DN_W), lambda i: (i, 3)), _full_spec((1, DN_W)),
                  _row_spec(tm, DN_W)],
        out_specs=[_row_spec(tm, DN_W), _row_spec(tm, DN_W), _full_spec((8, LANE))],
        out_shape=[jax.ShapeDtypeStruct((T, DN_W), F32), jax.ShapeDtypeStruct((T, DN_W), F32),
                   jax.ShapeDtypeStruct((8, LANE), F32)],
        compiler_params=_cp("arbitrary"),
    )(o, u_dn, nl, dy)


Y_SPLITS = (LRU_W, ATT_W, DN_W)
Y_OFFS = (0, LRU_W, LRU_W + ATT_W)


ROWS_DEV = D // N_DEV


def _dev_rows_spec(row0):
    return pl.BlockSpec((N_DEV, ROWS_DEV, D), lambda *_: (0, row0 // ROWS_DEV, 0))


def _dev_rows(w_ref, off, n):
    return w_ref[off // ROWS_DEV:(off + n) // ROWS_DEV].reshape(n, D)


def wout_fwd(h, ys, wb, name):
    T = h.shape[0]
    tm = min(TOK_TILE, T)

    def body(h_ref, y0, y1, y2, w_ref, o_ref, yc_ref):
        acc = h_ref[...]
        for ref, off, n in zip((y0, y1, y2), Y_OFFS, Y_SPLITS):
            y = ref[...].astype(BF16)
            yc_ref[:, off:off + n] = y
            acc += _mm(y, _dev_rows(w_ref, off, n))
        o_ref[...] = acc

    return pl.pallas_call(
        body, name=name, grid=(T // tm,),
        in_specs=[_row_spec(tm, D)] + [_row_spec(tm, n) for n in Y_SPLITS] + [_dev_rows_spec(B_WOUT)],
        out_specs=[_row_spec(tm, D), _row_spec(tm, D)],
        out_shape=[jax.ShapeDtypeStruct((T, D), F32), jax.ShapeDtypeStruct((T, D), BF16)],
        compiler_params=_cp("parallel"),
    )(h, *ys, wb)


def wout_bwd(dy, wb, name):
    T = dy.shape[0]
    tm = min(TOK_TILE, T)

    def body(dy_ref, w_ref, d0, d1, d2):
        dd = dy_ref[...].astype(BF16)
        for ref, off, n in zip((d0, d1, d2), Y_OFFS, Y_SPLITS):
            ref[...] = _mm_nt(dd, _dev_rows(w_ref, off, n))

    return pl.pallas_call(
        body, name=name, grid=(T // tm,),
        in_specs=[_row_spec(tm, D), _dev_rows_spec(B_WOUT)],
        out_specs=[_row_spec(tm, n) for n in Y_SPLITS],
        out_shape=[jax.ShapeDtypeStruct((T, n), F32) for n in Y_SPLITS],
        compiler_params=_cp("parallel"),
    )(dy, wb)


def ple_fwd(h, g, pe, wg, wp, name):
    T = h.shape[0]
    tm = min(TOK_TILE, T)

    def body(h_ref, g_ref, p_ref, wg_ref, wp_ref, o_ref):
        hh = h_ref[...]
        xn = _rms(hh, g_ref[...])[0]
        o_ref[...] = hh + _sigmoid(_mm(xn, _dev_rows(wg_ref, 0, D))) * _mm(p_ref[...], wp_ref[...])

    return pl.pallas_call(
        body, name=name, grid=(T // tm,),
        in_specs=[_row_spec(tm, D), _full_spec((1, D)), _row_spec(tm, PLE), _dev_rows_spec(B_PGATE),
                  _full_spec((PLE, D))],
        out_specs=_row_spec(tm, D),
        out_shape=jax.ShapeDtypeStruct((T, D), F32),
        compiler_params=_cp("parallel"),
    )(h, g, pe, wg, wp)


def ple_bwd(h, dy, g, pe, wg, wp, name):
    T = h.shape[0]
    tm = min(TOK_TILE, T)

    def body(h_ref, dy_ref, g_ref, p_ref, wg_ref, wp_ref, dh_ref, dz_ref, dpp_ref, xn_ref, dn_ref):
        @pl.when(pl.program_id(0) == 0)
        def _():
            dn_ref[...] = jnp.zeros_like(dn_ref)

        gg = g_ref[...]
        dy = dy_ref[...]
        xn, xhat, rstd = _rms(h_ref[...], gg)
        wg = _dev_rows(wg_ref, 0, D)
        gate = _sigmoid(_mm(xn, wg))
        pp = _mm(p_ref[...], wp_ref[...])
        dz = dy * pp * gate * (1.0 - gate)
        dz_ref[...] = dz.astype(BF16)
        dpp_ref[...] = (dy * gate).astype(BF16)
        xn_ref[...] = xn.astype(BF16)
        dh, dn = _rms_bwd(_mm_nt(dz, wg), xhat, rstd, gg)
        dh_ref[...] = dy + dh
        dn_ref[...] += dn

    return pl.pallas_call(
        body, name=name, grid=(T // tm,),
        in_specs=[_row_spec(tm, D), _row_spec(tm, D), _full_spec((1, D)), _row_spec(tm, PLE),
                  _dev_rows_spec(B_PGATE), _full_spec((PLE, D))],
        out_specs=[_row_spec(tm, D), _row_spec(tm, D), _row_spec(tm, D), _row_spec(tm, D), _full_spec((1, D))],
        out_shape=[jax.ShapeDtypeStruct((T, D), F32), jax.ShapeDtypeStruct((T, D), BF16),
                   jax.ShapeDtypeStruct((T, D), BF16), jax.ShapeDtypeStruct((T, D), BF16),
                   jax.ShapeDtypeStruct((1, D), F32)],
        compiler_params=_cp("arbitrary"),
    )(h, dy, g, pe, wg, wp)


def loss_head(h, g, target, name):
    T = h.shape[0]
    tm = min(TOK_TILE, T)

    def body(h_ref, g_ref, t_ref, loss_ref, dh_ref, dn_ref):
        @pl.when(pl.program_id(0) == 0)
        def _():
            dn_ref[...] = jnp.zeros_like(dn_ref)
            loss_ref[...] = jnp.zeros_like(loss_ref)

        gg = g_ref[...]
        y, xhat, rstd = _rms(h_ref[...], gg)
        err = y - t_ref[...]
        per_tok = jnp.mean(err * err, axis=-1, keepdims=True)
        loss_ref[...] += 0.5 * jnp.sum(per_tok, axis=0, keepdims=True)
        dh, dn = _rms_bwd(err * (1.0 / D), xhat, rstd, gg)
        dh_ref[...] = dh
        dn_ref[...] += dn

    return pl.pallas_call(
        body, name=name, grid=(T // tm,),
        in_specs=[_row_spec(tm, D), _full_spec((1, D)), _row_spec(tm, D)],
        out_specs=[_full_spec((8, LANE)), _row_spec(tm, D), _full_spec((1, D))],
        out_shape=[jax.ShapeDtypeStruct((8, LANE), F32), jax.ShapeDtypeStruct((T, D), F32),
                   jax.ShapeDtypeStruct((1, D), F32)],
        compiler_params=_cp("arbitrary"),
    )(h, g, target)


def _block_diag(w):
    return jnp.einsum('hij,hk->hikj', w, jnp.eye(4, dtype=w.dtype)).reshape(LRU_W, LRU_W)


def _layer_consts(W, l):
    row = lambda v: v.reshape(1, -1)
    zeros = jnp.zeros((5, LRU_W), F32)
    return dict(
        wa=_block_diag(W["lru_w_a"][l]), wx=_block_diag(W["lru_w_x"][l]),
        lru_vec=jnp.concatenate([row(W["lru_b_a"][l]), row(W["lru_b_x"][l]), row(W["lru_lambda"][l]), zeros], 0),
        lru_cb=row(W["lru_conv_b"][l]),
        sinks=W["attn_sinks"][l], rel=W["rel_bias"].reshape(-1),
        dn_cb=jnp.zeros((1, 3 * DN_W), F32),
        alog=row(jnp.repeat(W["dn_a_log"][l], HD)), dtb=row(jnp.repeat(W["dn_dt_bias"][l], HD)),
        dn_nl=row(jnp.tile(W["dn_norm"][l], DN_H)),
    )


def _layer_fwd(h0, pe, W, l, S):
    n = f"l{l}_"
    c_ = _layer_consts(W, l)
    row = lambda v: v.reshape(1, -1)
    wa, wb = W["wa"][l], W["wb"][l]
    h1, *ffn1_kept = ffn_fwd(h0, row(W["ffn1_norm"][l]), wa, wb, 0, n + "ffn1_fwd")
    u_lru, u_att, u_dn, u_ba, xn_mix = mixin_fwd(h1, row(W["mix_norm"][l]), W["w_in"][l], n + "mixin_fwd")
    xr = conv_fwd(u_lru, W["lru_conv_w"][l], c_["lru_cb"], S, 0, LRU_W, n + "lru_conv_fwd")
    y_lru = lru_fwd(xr, u_lru, c_["wa"], c_["wx"], c_["lru_vec"], S, n + "lru_fwd")
    y_att = attn_fwd(u_att, c_["sinks"], c_["rel"], S, n + "attn_fwd")
    cc = conv_fwd(u_dn, W["dn_conv_w"][l], c_["dn_cb"], S, 0, 3 * DN_W, n + "dn_conv_fwd")
    q, k, v, g, beta = dn_point_fwd(cc, u_ba, c_["alog"], c_["dtb"], n + "dn_point_fwd")
    o, states, tinvs = dn_scan_fwd(q, k, v, g, beta, S, n + "dn_scan_fwd")
    y_dn = dn_gate_fwd(o, u_dn, c_["dn_nl"], n + "dn_gate_fwd")
    h2, ycat = wout_fwd(h1, (y_lru, y_att, y_dn), wb, n + "wout_fwd")
    h3, *ffn2_kept = ffn_fwd(h2, row(W["ffn2_norm"][l]), wa, wb, 1, n + "ffn2_fwd")
    h4 = ple_fwd(h3, row(W["ple_norm"][l]), pe, wb, W["ple_w_proj"][l], n + "ple_fwd")
    saved = dict(ffn1=ffn1_kept, ffn2=ffn2_kept, h0=h0, h1=h1, h2=h2, h3=h3, u_lru=u_lru, u_att=u_att, u_dn=u_dn, u_ba=u_ba, xn_mix=xn_mix,
                 xr=xr, cc=cc, q=q, k=k, v=v, g=g, beta=beta, o=o, states=states, tinvs=tinvs, ycat=ycat)
    return h4, saved


GE_DOWN2, GE_WOUT, GE_PGATE, GE_WIN, GE_PPROJ, GE_END, GE_ROWS = 0, 384, 512, 640, 944, 976, 1024


def _layer_bwd(dh4, sv, pe, W, l, S, token=None, on_piece=None):
    n = f"l{l}_"
    c_ = _layer_consts(W, l)
    row = lambda v: v.reshape(1, -1)
    behind = lambda v, tok: v if tok is None else v + tok
    on_piece = on_piece or (lambda *_: None)
    wa, wb = W["wa"][l], W["wb"][l]
    G = {"early_cols": jnp.zeros((2, D, FFP), BF16), "early_rows": jnp.zeros((N_DEV, GE_ROWS, D), BF16),
         "late_cols": jnp.zeros((2, D, FFP), BF16), "late_rows": jnp.zeros((N_DEV, SHP, D), BF16)}
    dh3, dz, dpp, xn_p, dn = ple_bwd(sv["h3"], dh4, behind(row(W["ple_norm"][l]), token), pe, wb,
                                     W["ple_w_proj"][l], n + "ple_bwd")
    G["ple_norm"] = dn[0]
    G["early_rows"] = grad_rows(xn_p, dz, G["early_rows"], GE_PGATE, ROWS_DEV, n + "d_ple_w_gate")
    d_proj = matmul_tn(pe, dpp, n + "d_ple_w_proj")
    d_proj = d_proj.reshape(PLE, N_DEV, D // N_DEV).transpose(1, 0, 2).reshape(N_DEV, GE_END - GE_PPROJ, D)
    G["early_rows"] = lax.dynamic_update_slice(G["early_rows"], d_proj, (0, GE_PPROJ, 0))

    def ffn_back(which, fidx, h_in, dy, tok=None):
        piece = ("late", "early")[fidx]
        gt, up, xn = sv[which]
        dh, dgt, dup, act, dn_ = ffn_bwd(h_in, dy, behind(row(W[which + "_norm"][l]), tok), gt, up, wa, wb, fidx,
                                         n + which + "_bwd")
        G[which + "_norm"] = dn_[0]
        G[piece + "_cols"] = grad_cols(xn, dgt, G[piece + "_cols"], 0, n + "d_" + which + "_w_gate")
        G[piece + "_cols"] = grad_cols(xn, dup, G[piece + "_cols"], 1, n + "d_" + which + "_w_up")
        G[piece + "_rows"] = grad_rows(act, dy, G[piece + "_rows"], 0, SHP, n + "d_" + which + "_w_down",
                                       scale=0.5)
        return dh

    dh2 = ffn_back("ffn2", 1, sv["h2"], dh3)
    dy_lru, dy_att, dy_dn = wout_bwd(dh2, wb, n + "wout_bwd")
    G["early_rows"] = grad_rows(sv["ycat"], dh2, G["early_rows"], GE_WOUT, ROWS_DEV, n + "d_w_out")
    do, dz_dn, dnn = dn_gate_bwd(sv["o"], sv["u_dn"], c_["dn_nl"], dy_dn, n + "dn_gate_bwd")
    dqkvgb = dn_scan_bwd(sv["q"], sv["k"], sv["v"], sv["g"], sv["beta"], sv["states"], sv["tinvs"], do, S,
                         n + "dn_scan_bwd")
    dcc, du_ba, dvec_dn = dn_point_bwd(sv["cc"], sv["u_ba"], c_["alog"], c_["dtb"], dqkvgb, n + "dn_point_bwd")
    dqkv, dwb_dn = conv_bwd(sv["u_dn"], dcc, W["dn_conv_w"][l], S, 0, 3 * DN_W, n + "dn_conv_bwd")
    du_dn = jnp.concatenate([dqkv, dz_dn], axis=1)
    G["dn_norm"] = dnn[0, 0:HD]
    G["dn_a_log"] = dvec_dn[0, 0:DN_H]
    G["dn_dt_bias"] = dvec_dn[1, 0:DN_H]
    G["dn_conv_w"] = dwb_dn[0:4]
    du_att, drel, dsk = attn_bwd(sv["u_att"], dy_att, c_["sinks"], c_["rel"], S, n + "attn_bwd")
    G["attn_sinks"] = dsk[:, 0]
    G["rel_bias"] = drel[:, 0:REL_BUCKETS].T
    dxr, dgt_lru, dwa, dwx, dvec = lru_bwd(sv["xr"], sv["u_lru"], dy_lru, c_["wa"], c_["wx"], c_["lru_vec"], S,
                                           n + "lru_bwd")
    dx_lru, dwb_lru = conv_bwd(sv["u_lru"], dxr, W["lru_conv_w"][l], S, 0, LRU_W, n + "lru_conv_bwd")
    du_lru = jnp.concatenate([dx_lru, dgt_lru], axis=1)
    diag = lambda m: jnp.stack([m[c, HD * e:HD * (e + 1), HD * e:HD * (e + 1)] for c in range(2) for e in range(2)])
    G["lru_w_a"], G["lru_w_x"] = diag(dwa), diag(dwx)
    G["lru_b_a"], G["lru_b_x"], G["lru_lambda"] = dvec[0], dvec[1], dvec[2]
    G["lru_conv_w"], G["lru_conv_b"] = dwb_lru[0:4], dwb_lru[4]
    dh1, du_cat, dn = mixin_bwd(sv["h1"], dh2, row(W["mix_norm"][l]), W["w_in"][l],
                                (du_lru, du_att, du_dn, du_ba), n + "mixin_bwd")
    G["mix_norm"] = dn[0]
    d_in = matmul_tn(sv["xn_mix"], du_cat, n + "d_w_in")[:, :D_IN]
    d_in = d_in.reshape(D, N_DEV, D_IN // N_DEV).transpose(1, 0, 2).reshape(N_DEV, WIN_ROWS, D)
    d_in = jnp.pad(d_in, ((0, 0), (0, GE_PPROJ - GE_WIN - WIN_ROWS), (0, 0)))
    G["early_rows"] = lax.dynamic_update_slice(G["early_rows"], d_in, (0, GE_WIN, 0))
    tok = on_piece(l, "early", G["early_cols"], G["early_rows"])
    dh0 = ffn_back("ffn1", 0, sv["h0"], dh1, tok)
    return dh0, G, on_piece(l, "late", G["late_cols"], G["late_rows"])


def _core(x, pe, W, target, S, weights_at=None, on_piece=None):
    weights_at = weights_at or (lambda l, h: W)
    h = x
    saved = []
    for l in range(DEPTH):
        h, sv = _layer_fwd(h, pe[l], weights_at(l, h), l, S)
        saved.append(sv)
    loss_tile, dh, dfn = loss_head(h, W["final_norm"].reshape(1, -1), target, "loss_head")
    grads = [None] * DEPTH
    token = None
    for l in reversed(range(DEPTH)):
        dh, grads[l], token = _layer_bwd(dh, saved[l], pe[l], W, l, S, token, on_piece)
    return loss_tile[0, 0], dh, grads, dfn[0]


MESH_ID = pl.DeviceIdType.MESH
ANY_SPEC = pl.BlockSpec(memory_space=pl.ANY)
AXES = ("x", "y", "c")


def _my_pos():
    return lax.axis_index("x"), lax.axis_index("y"), lax.axis_index("c")


def _slot_of(px, py, pc):
    return 4 * px + 2 * py + pc


def all_gather(x, name):
    R, C = x.shape

    def body(x_ref, out_ref, send_sems, recv_sems, local_sem):
        mx, my, mc = _my_pos()
        me, sibling = (mx, my, mc), (mx, my, 1 - mc)
        chips = [(1 - mx, my), (mx, 1 - my), (1 - mx, 1 - my)]

        def copy(k, block, to, src=None):
            dst = out_ref.at[_slot_of(*block)]
            return pltpu.make_async_remote_copy(
                src_ref=dst if src is None else src, dst_ref=dst,
                send_sem=send_sems.at[k], recv_sem=recv_sems.at[k],
                device_id=to, device_id_type=MESH_ID)

        mine = pltpu.make_async_copy(x_ref, out_ref.at[_slot_of(*me)], local_sem)
        mine.start()
        first = [copy(0, me, sibling, src=x_ref)]
        first += [copy(1 + j, me, (*chip, mc), src=x_ref) for j, chip in enumerate(chips)]
        for cp in first:
            cp.start()
        passed = [copy(4 + j, (*chip, mc), sibling) for j, chip in enumerate(chips)]
        for j, chip in enumerate(chips):
            copy(1 + j, (*chip, mc), me).wait_recv()
            passed[j].start()
        copy(0, sibling, me).wait_recv()
        for j, chip in enumerate(chips):
            copy(4 + j, (*chip, 1 - mc), me).wait_recv()
        for cp in first + passed:
            cp.wait_send()
        mine.wait()

    return pl.pallas_call(
        body, name=name,
        out_shape=jax.ShapeDtypeStruct((N_DEV, R, C), x.dtype),
        in_specs=[ANY_SPEC], out_specs=ANY_SPEC,
        scratch_shapes=[pltpu.SemaphoreType.DMA((7,)), pltpu.SemaphoreType.DMA((7,)), pltpu.SemaphoreType.DMA],
    )(x)


def _col_window(ref, slot):
    return ref.at[:, pl.ds(pl.multiple_of(slot * SHP, LANE), SHP)]


def gather_layer(a_sh, b_sh, name):
    def body(a_ref, b_ref, ao_ref, bo_ref, send_sems, recv_sems, local_sems):
        mx, my, mc = _my_pos()
        me, sibling = (mx, my, mc), (mx, my, 1 - mc)
        chips = [(1 - mx, my), (mx, 1 - my), (1 - mx, 1 - my)]

        def copies(k, block, to, own=False):
            slot = _slot_of(*block)
            dsts = (_col_window(ao_ref, slot), bo_ref.at[slot])
            srcs = (a_ref, b_ref) if own else dsts
            return [pltpu.make_async_remote_copy(
                src_ref=s, dst_ref=d, send_sem=send_sems.at[2 * k + i], recv_sem=recv_sems.at[2 * k + i],
                device_id=to, device_id_type=MESH_ID) for i, (s, d) in enumerate(zip(srcs, dsts))]

        mine = [pltpu.make_async_copy(a_ref, _col_window(ao_ref, _slot_of(*me)), local_sems.at[0]),
                pltpu.make_async_copy(b_ref, bo_ref.at[_slot_of(*me)], local_sems.at[1])]
        for cp in mine:
            cp.start()
        first = copies(0, me, sibling, own=True)
        for j, chip in enumerate(chips):
            first += copies(1 + j, me, (*chip, mc), own=True)
        for cp in first:
            cp.start()
        passed = []
        for j, chip in enumerate(chips):
            for cp in copies(1 + j, (*chip, mc), me):
                cp.wait_recv()
            fwd = copies(4 + j, (*chip, mc), sibling)
            for cp in fwd:
                cp.start()
            passed += fwd
        for cp in copies(0, sibling, me):
            cp.wait_recv()
        for j, chip in enumerate(chips):
            for cp in copies(4 + j, (*chip, 1 - mc), me):
                cp.wait_recv()
        for cp in first + passed:
            cp.wait_send()
        for cp in mine:
            cp.wait()

    return pl.pallas_call(
        body, name=name,
        out_shape=[jax.ShapeDtypeStruct((a_sh.shape[0], FFP), a_sh.dtype),
                   jax.ShapeDtypeStruct((N_DEV,) + b_sh.shape, b_sh.dtype)],
        in_specs=[ANY_SPEC, ANY_SPEC], out_specs=[ANY_SPEC, ANY_SPEC],
        scratch_shapes=[pltpu.SemaphoreType.DMA((14,)), pltpu.SemaphoreType.DMA((14,)),
                        pltpu.SemaphoreType.DMA((2,))],
    )(a_sh, b_sh)


HBM_SPEC = pl.BlockSpec(memory_space=pltpu.HBM)
SEM_SPEC = pl.BlockSpec(memory_space=pltpu.SEMAPHORE)
SPLIT_EFFECT = pltpu.CompilerParams(has_side_effects=pltpu.SideEffectType.DATAFLOW_SIDE_EFFECTING)


def _split_ends(mode, src_ref, dst_ref, src_slot, dst_slot):
    cols = mode.endswith("cols")
    if mode.startswith("gather"):
        return src_ref, (_col_window(dst_ref, dst_slot) if cols else dst_ref.at[dst_slot])
    return (_col_window(src_ref, src_slot) if cols else src_ref.at[src_slot]), dst_ref.at[dst_slot]


def _split_peers():
    mx, my, mc = _my_pos()
    for r in range(1, N_DEV):
        peer = (1 - mx if r & 4 else mx, 1 - my if r & 2 else my, 1 - mc if r & 1 else mc)
        yield r - 1, peer, _slot_of(*peer)


def split_start(modes, srcs, dsts, name):
    n = len(modes)

    def body(*refs):
        send_sems, recv_sems, token = refs[2 * n], refs[2 * n + 1], refs[-1]
        mine = _slot_of(*_my_pos())
        for k, peer, ps in _split_peers():
            for i in range(n):
                src, dst = _split_ends(modes[i], refs[i], refs[n + i], ps, mine)
                pltpu.make_async_remote_copy(
                    src_ref=src, dst_ref=dst, send_sem=send_sems.at[n * k + i], recv_sem=recv_sems.at[n * k + i],
                    device_id=peer, device_id_type=MESH_ID).start()
        for i in range(n):
            src, dst = _split_ends(modes[i], refs[i], refs[n + i], mine, mine)
            pltpu.make_async_copy(src, dst, recv_sems.at[n * (N_DEV - 1) + i]).start()
        token[...] = jnp.zeros_like(token)

    bufs = tuple(srcs) + tuple(dsts)
    sems = pltpu.SemaphoreType.DMA((n * N_DEV,))
    res = pl.pallas_call(
        body, name=name,
        out_shape=(sems, sems) + tuple(pltpu.HBM(t.shape, t.dtype) for t in bufs)
        + (jax.ShapeDtypeStruct((8, LANE), F32),),
        in_specs=[HBM_SPEC] * (2 * n),
        out_specs=(SEM_SPEC, SEM_SPEC) + (HBM_SPEC,) * (2 * n) + (pl.BlockSpec(memory_space=pltpu.VMEM),),
        input_output_aliases={i: 2 + i for i in range(2 * n)},
        compiler_params=SPLIT_EFFECT,
    )(*(pltpu.with_memory_space_constraint(t, pltpu.HBM) for t in bufs))
    return list(res[:-1]), res[-1][0, 0]


def split_wait(modes, started, after, name):
    n = len(modes)
    send_sems, recv_sems, bufs = started[0], started[1], started[2:]

    def body(*refs):
        send_sems, recv_sems = refs[2 * n], refs[2 * n + 1]
        mine = _slot_of(*_my_pos())
        for k, peer, ps in _split_peers():
            for i in range(n):
                sent = _split_ends(modes[i], refs[i], refs[n + i], ps, mine)[0]
                landed = _split_ends(modes[i], refs[i], refs[n + i], mine, ps)[1]
                cp = pltpu.make_async_remote_copy(
                    src_ref=sent, dst_ref=landed, send_sem=send_sems.at[n * k + i],
                    recv_sem=recv_sems.at[n * k + i], device_id=peer, device_id_type=MESH_ID)
                cp.wait_send()
                cp.wait_recv()
        for i in range(n):
            src, dst = _split_ends(modes[i], refs[i], refs[n + i], mine, mine)
            pltpu.make_async_copy(src, dst, recv_sems.at[n * (N_DEV - 1) + i]).wait()

    res = pl.pallas_call(
        body, name=name,
        out_shape=tuple(pltpu.HBM(t.shape, t.dtype) for t in bufs),
        in_specs=[HBM_SPEC] * (2 * n) + [SEM_SPEC, SEM_SPEC, pl.BlockSpec(memory_space=pl.ANY)],
        out_specs=(HBM_SPEC,) * (2 * n),
        input_output_aliases={i: i for i in range(2 * n)},
        compiler_params=SPLIT_EFFECT,
    )(*bufs, send_sems, recv_sems, after)
    return list(res[n:])


def sum_parts(parts, name):
    _, R, C = parts.shape
    tr = _pick(R, (512, 336, 272, 256, 128, 64, 32, 16, 8))

    def body(p_ref, o_ref):
        acc = p_ref[0].astype(F32)
        for k in range(1, N_DEV):
            acc += p_ref[k].astype(F32)
        o_ref[...] = acc

    return pl.pallas_call(
        body, name=name, grid=(R // tr,),
        in_specs=[pl.BlockSpec((N_DEV, tr, C), lambda i: (0, i, 0))],
        out_specs=pl.BlockSpec((tr, C), lambda i: (i, 0)),
        out_shape=jax.ShapeDtypeStruct((R, C), F32),
        compiler_params=_cp("parallel"),
    )(parts)


def adamw(g, w, m, v, name):
    R, C = g.shape
    tr = _pick(R, (512, 352, 256, 128, 64, 32, 16, 8))
    c1 = 1.0 - ADAM_B1 ** ADAM_STEP
    c2 = 1.0 - ADAM_B2 ** ADAM_STEP

    def body(g_ref, w_ref, m_ref, v_ref, d_ref, nm_ref, nv_ref):
        gg = g_ref[...]
        mm = ADAM_B1 * m_ref[...] + (1.0 - ADAM_B1) * gg
        vv = ADAM_B2 * v_ref[...] + (1.0 - ADAM_B2) * (gg * gg)
        nm_ref[...] = mm
        nv_ref[...] = vv
        d_ref[...] = -ADAM_LR * ((mm / c1) / (jnp.sqrt(vv / c2) + ADAM_EPS) + ADAM_WD * w_ref[...])

    spec = pl.BlockSpec((tr, C), lambda i: (i, 0))
    return pl.pallas_call(
        body, name=name, grid=(R // tr,),
        in_specs=[spec] * 4, out_specs=[spec] * 3,
        out_shape=[jax.ShapeDtypeStruct((R, C), F32)] * 3,
        compiler_params=_cp("parallel"),
    )(g, w, m, v)


BIG = (("ffn1_w_gate", 1, D, FF), ("ffn1_w_up", 1, D, FF), ("ffn1_w_down", 0, FF, D),
       ("w_in", 1, D, D_IN), ("w_out", 0, D, D),
       ("ffn2_w_gate", 1, D, FF), ("ffn2_w_up", 1, D, FF), ("ffn2_w_down", 0, FF, D),
       ("ple_w_gate", 0, D, D), ("ple_w_proj", 1, PLE, D))
SMALL = (("ffn1_norm", (D,), None), ("mix_norm", (D,), None), ("lru_conv_w", (4, LRU_W), LRU_W // N_DEV),
         ("lru_conv_b", (LRU_W,), None), ("lru_w_a", (4, HD, HD), None), ("lru_b_a", (LRU_W,), None),
         ("lru_w_x", (4, HD, HD), None), ("lru_b_x", (LRU_W,), None), ("lru_lambda", (LRU_W,), None),
         ("attn_sinks", (ATT_H,), None), ("dn_conv_w", (4, 3 * DN_W), 3 * DN_W // N_DEV),
         ("dn_a_log", (DN_H,), None), ("dn_dt_bias", (DN_H,), None), ("dn_norm", (HD,), None),
         ("ffn2_norm", (D,), None), ("ple_norm", (D,), None))
SINGLE = (("rel_bias", (REL_BUCKETS, ATT_H)), ("final_norm", (D,)))


def _pack_rows(arrs, width, mult):
    flat = jnp.concatenate([a.reshape(-1) for a in arrs])
    rows = -(-flat.shape[0] // (width * mult)) * mult
    return jnp.pad(flat, (0, rows * width - flat.shape[0])).reshape(rows, width)


def _unpack_rows(packed, shapes):
    flat = packed.reshape(-1)
    out, off = [], 0
    for s in shapes:
        n = int(np.prod(s))
        out.append(flat[off:off + n].reshape(s))
        off += n
    return out


COL_NAMES = ("ffn1_w_gate", "ffn1_w_up", "ffn2_w_gate", "ffn2_w_up")


def _shard_cols(a, l):
    blk = jnp.concatenate([a[n][l] for n in COL_NAMES], axis=0)
    return jnp.pad(blk, ((0, 0), (0, SHP - SH))).astype(BF16)


def _shard_rows(a, l):
    to = lambda w, r: jnp.pad(w, ((0, r - w.shape[0]), (0, 0)))
    parts = [to(a["ffn1_w_down"][l], SHP), to(a["ffn2_w_down"][l], SHP), a["w_out"][l], a["ple_w_gate"][l],
             to(a["w_in"][l].reshape(WIN_ROWS, D), B_PPROJ - B_WIN), a["ple_w_proj"][l].reshape(-1, D)]
    return jnp.concatenate(parts, axis=0).astype(BF16)


def _full_w_in(wb):
    sh = wb[:, B_WIN:B_WIN + WIN_ROWS, :].reshape(N_DEV, D, D_IN // N_DEV)
    return jnp.pad(sh.transpose(1, 0, 2).reshape(D, D_IN), ((0, 0), (0, D_IN_PAD - D_IN)))


def _full_ple_proj(wb):
    sh = wb[:, B_PPROJ:B_ROWS, :].reshape(N_DEV, PLE, D // N_DEV)
    return sh.transpose(1, 0, 2).reshape(PLE, D)


PIECE_NAMES = {"late": ("ffn1_w_gate", "ffn1_w_up", "ffn1_w_down"),
               "early": ("ffn2_w_gate", "ffn2_w_up", "ffn2_w_down", "w_out", "ple_w_gate", "w_in", "ple_w_proj")}


def _shard_grads(piece, cols, rows):
    ffn = "ffn1" if piece == "late" else "ffn2"
    g = {ffn + "_w_gate": cols[:D, :SH], ffn + "_w_up": cols[D:, :SH], ffn + "_w_down": rows[:SH]}
    if piece == "early":
        g["w_out"] = rows[GE_WOUT:GE_WOUT + ROWS_DEV]
        g["ple_w_gate"] = rows[GE_PGATE:GE_PGATE + ROWS_DEV]
        g["w_in"] = rows[GE_WIN:GE_WIN + WIN_ROWS].reshape(D, D_IN // N_DEV)
        g["ple_w_proj"] = rows[GE_PPROJ:GE_END].reshape(PLE, D // N_DEV)
    return g


def kernel(x, p, ffn1_norm, ffn1_w_gate, ffn1_w_up, ffn1_w_down, mix_norm, w_in, lru_conv_w, lru_conv_b, lru_w_a, lru_b_a, lru_w_x, lru_b_x, lru_lambda, attn_sinks, rel_bias, dn_conv_w, dn_a_log, dn_dt_bias, dn_norm, w_out, ffn2_norm, ffn2_w_gate, ffn2_w_up, ffn2_w_down, ple_norm, ple_w_gate, ple_w_proj, final_norm, loss_target, m_ffn1_norm, m_ffn1_w_gate, m_ffn1_w_up, m_ffn1_w_down, m_mix_norm, m_w_in, m_lru_conv_w, m_lru_conv_b, m_lru_w_a, m_lru_b_a, m_lru_w_x, m_lru_b_x, m_lru_lambda, m_attn_sinks, m_rel_bias, m_dn_conv_w, m_dn_a_log, m_dn_dt_bias, m_dn_norm, m_w_out, m_ffn2_norm, m_ffn2_w_gate, m_ffn2_w_up, m_ffn2_w_down, m_ple_norm, m_ple_w_gate, m_ple_w_proj, m_final_norm, v_ffn1_norm, v_ffn1_w_gate, v_ffn1_w_up, v_ffn1_w_down, v_mix_norm, v_w_in, v_lru_conv_w, v_lru_conv_b, v_lru_w_a, v_lru_b_a, v_lru_w_x, v_lru_b_x, v_lru_lambda, v_attn_sinks, v_rel_bias, v_dn_conv_w, v_dn_a_log, v_dn_dt_bias, v_dn_norm, v_w_out, v_ffn2_norm, v_ffn2_w_gate, v_ffn2_w_up, v_ffn2_w_down, v_ple_norm, v_ple_w_gate, v_ple_w_proj, v_final_norm):
    a = dict(locals())
    nb, S, _ = x.shape
    T = nb * S
    my_slot = _slot_of(*_my_pos())

    W = {"wa": [None] * DEPTH, "wb": [None] * DEPTH, "w_in": [None] * DEPTH, "ple_w_proj": [None] * DEPTH}

    def set_layer_weights(l, wa, wb):
        W["wa"][l], W["wb"][l] = wa, wb
        W["w_in"][l], W["ple_w_proj"][l] = _full_w_in(wb), _full_ple_proj(wb)

    def landing(mode, src):
        if mode == "gather_cols":
            return lax.empty((src.shape[0], FFP), src.dtype)
        if mode == "scatter_cols":
            return lax.empty((N_DEV, src.shape[0], SHP), src.dtype)
        return lax.empty((N_DEV,) + src.shape[mode == "scatter_block":], src.dtype)

    def start(modes, srcs, name):
        return split_start(modes, srcs, [landing(m, s) for m, s in zip(modes, srcs)], name)

    set_layer_weights(0, *gather_layer(_shard_cols(a, 0), _shard_rows(a, 0), "gather_weights_l0"))
    taps = all_gather(_pack_rows([lru_conv_w, dn_conv_w], LANE, 8), "gather_conv_taps")
    tap_shapes = [lru_conv_w.shape, dn_conv_w.shape]
    lcw, dcw = zip(*[_unpack_rows(taps[k], tap_shapes) for k in range(N_DEV)])
    W["lru_conv_w"] = jnp.concatenate(lcw, axis=-1)
    W["dn_conv_w"] = jnp.concatenate(dcw, axis=-1)
    for name, _, cols in SMALL:
        if cols is None:
            W[name] = a[name]
    W["rel_bias"], W["final_norm"] = rel_bias, final_norm

    GATHER, SCATTER = ("gather_cols", "gather_block"), ("scatter_cols", "scatter_block")
    cols1, rows1, _, _ = lax.optimization_barrier((_shard_cols(a, 1), _shard_rows(a, 1), W["wb"][0], taps))
    gather1, token = start(GATHER, (cols1, rows1), "gather_start_l1")
    W["ffn1_norm"] = ffn1_norm + token
    flight = {}

    def weights_at(l, h):
        if l == 1:
            set_layer_weights(1, *split_wait(GATHER, gather1, h, "gather_wait_l1"))
        return W

    def on_piece(l, piece, cols, rows):
        flight[l, piece], token = start(SCATTER, (cols.reshape(2 * D, FFP), rows), f"exchange_start_l{l}_{piece}")
        return token

    loss_local, dx, grads, d_final = _core(x.reshape(T, D), p.reshape(DEPTH, T, PLE), W,
                                           loss_target.reshape(T, D), S, weights_at, on_piece)
    loss = lax.psum(loss_local, AXES)

    small_full = [jnp.stack([grads[l][name] for l in range(DEPTH)]) for name, _, _ in SMALL]
    small_full += [grads[0]["rel_bias"] + grads[1]["rel_bias"], d_final]
    small_flight, _ = start(("gather_block",), (_pack_rows(small_full, LANE, 8),), "gather_start_small_grads")

    out = {}

    def update(piece, received):
        shards = [_shard_grads(piece, sum_parts(ra, f"sum_col_grads_l{l}_{piece}"),
                               sum_parts(rb, f"sum_row_grads_l{l}_{piece}")) for l, (ra, rb) in enumerate(received)]
        for name in PIECE_NAMES[piece]:
            g = jnp.stack([shards[l][name] for l in range(DEPTH)])
            shape = a[name].shape
            two_d = lambda t: t.reshape(-1, shape[-1])
            res = adamw(two_d(g), two_d(a[name]), two_d(a["m_" + name]), two_d(a["v_" + name]), "adamw_" + name)
            out[name] = (g,) + tuple(r.reshape(shape) for r in res)

    landed = {key: split_wait(SCATTER, flight[key], dx, f"exchange_wait_l{key[0]}_{key[1]}")
              for key in ((1, "early"), (1, "late"), (0, "early"))}
    update("early", [landed[0, "early"], landed[1, "early"]])
    done_early = lax.optimization_barrier(tuple(out[n][1] for n in PIECE_NAMES["early"]))
    small_parts, = split_wait(("gather_block",), small_flight, done_early[0], "gather_wait_small_grads")
    small_sum = sum_parts(small_parts, "sum_small_grads")
    g_small = dict(zip([n for n, _, _ in SMALL] + [n for n, _ in SINGLE],
                       _unpack_rows(small_sum, [s.shape for s in small_full])))
    for name, _, cols in SMALL:
        if cols is not None:
            g_small[name] = lax.dynamic_slice_in_dim(g_small[name], my_slot * cols, cols, axis=2)

    small_names = [n for n, _, _ in SMALL] + [n for n, _ in SINGLE]
    shapes = [a[n].shape for n in small_names]
    packed = [_pack_rows([a[pre + n] if pre is not None else g_small[n] for n in small_names], LANE, 8)
              for pre in (None, "", "m_", "v_")]
    res = adamw(*packed, "adamw_small")
    unpacked = [_unpack_rows(r, shapes) for r in res]
    for i, n in enumerate(small_names):
        out[n] = (g_small[n].reshape(shapes[i]),) + tuple(u[i] for u in unpacked)

    landed[0, "late"] = split_wait(SCATTER, flight[0, "late"], lax.optimization_barrier((res[0], done_early[1]))[0],
                                   "exchange_wait_l0_late")
    update("late", [landed[0, "late"], landed[1, "late"]])

    order = ['ffn1_norm', 'ffn1_w_gate', 'ffn1_w_up', 'ffn1_w_down', 'mix_norm', 'w_in', 'lru_conv_w', 'lru_conv_b',
             'lru_w_a', 'lru_b_a', 'lru_w_x', 'lru_b_x', 'lru_lambda', 'attn_sinks', 'rel_bias', 'dn_conv_w',
             'dn_a_log', 'dn_dt_bias', 'dn_norm', 'w_out', 'ffn2_norm', 'ffn2_w_gate', 'ffn2_w_up', 'ffn2_w_down',
             'ple_norm', 'ple_w_gate', 'ple_w_proj', 'final_norm']
    return (loss, dx.reshape(x.shape)) + tuple(out[n][k] for k in range(4) for n in order)
```

```python
import functools
import math

import numpy as np
import jax
import jax.numpy as jnp
from jax import lax
from jax.experimental import pallas as pl
from jax.experimental.pallas import tpu as pltpu

F32 = jnp.float32
BF16 = jnp.bfloat16
HI = lax.Precision.HIGHEST

D = 1024
DEPTH = 2
EPS = 1e-6
PLE = 256
FF = 2816
HD = 64
LRU_W = 256
LRU_C = 8.0
ATT_W = 512
ATT_H = 8
ATT_KV = 2
ATT_G = 4
KV_W = 128
WINDOW = 128
BQ = 128
REL_BUCKETS = 32
REL_MAX_DIST = 128
DN_W = 256
DN_H = 4
CHUNK = 64
D_IN = 2312
D_IN_PAD = 2432
N_DEV = 8

ADAM_LR = 0.001
ADAM_B1 = 0.9
ADAM_B2 = 0.999
ADAM_EPS = 1e-08
ADAM_WD = 0.01
ADAM_STEP = 10

LANE = 128
VMEM_LIMIT = 56 * 1024 * 1024
SH = FF // N_DEV
SHP = 384
FFP = N_DEV * SHP
FF_TILE = 2 * SHP
TOK_TILE = 512
B_DOWN1, B_DOWN2, B_WOUT, B_PGATE, B_WIN, B_PPROJ, B_ROWS = 0, 384, 768, 896, 1024, 1328, 1360
WIN_ROWS = D * D_IN // N_DEV // 1024
NEG = -1e30


def _cp(*sem):
    return pltpu.CompilerParams(dimension_semantics=tuple(sem), vmem_limit_bytes=VMEM_LIMIT)


def _dg(a, b, ca, cb, exact):
    dims = (((ca,), (cb,)), ((), ()))
    if exact == "f32":
        return lax.dot_general(a.astype(F32), b.astype(F32), dims, precision=HI, preferred_element_type=F32)
    if exact == "split":
        a_hi, b_hi = a.astype(BF16), b.astype(BF16)
        a_lo = (a - a_hi.astype(F32)).astype(BF16)
        b_lo = (b - b_hi.astype(F32)).astype(BF16)
        dot = lambda u, v: lax.dot_general(u, v, dims, preferred_element_type=F32)
        return dot(a_hi, b_hi) + (dot(a_hi, b_lo) + dot(a_lo, b_hi))
    return lax.dot_general(a.astype(BF16), b.astype(BF16), dims, preferred_element_type=F32)


def _make_mm(exact):
    @jax.custom_vjp
    def mm(a, b):
        return _dg(a, b, 1, 0, exact)

    @jax.custom_vjp
    def mm_nt(a, b):
        return _dg(a, b, 1, 1, exact)

    @jax.custom_vjp
    def mm_tn(a, b):
        return _dg(a, b, 0, 0, exact)

    mm.defvjp(lambda a, b: (mm(a, b), (a, b)),
              lambda r, d: (mm_nt(d, r[1]), mm_tn(r[0], d)))
    mm_nt.defvjp(lambda a, b: (mm_nt(a, b), (a, b)),
                 lambda r, d: (mm(d, r[1]), mm_tn(d, r[0])))
    mm_tn.defvjp(lambda a, b: (mm_tn(a, b), (a, b)),
                 lambda r, d: (mm_nt(r[1], d), mm(r[0], d)))
    return mm, mm_nt, mm_tn


_mm, _mm_nt, _mm_tn = _make_mm("bf16")
_mmx, _mmx_nt, _mmx_tn = _make_mm("f32")
_mm3, _mm3_nt, _mm3_tn = _make_mm("split")


def _iota(shape, dim):
    return lax.broadcasted_iota(jnp.int32, shape, dim)


def _sigmoid(x):
    return 1.0 / (1.0 + jnp.exp(-x))


def _rms(h, g):
    rstd = lax.rsqrt(jnp.mean(h * h, axis=-1, keepdims=True) + EPS)
    xhat = h * rstd
    return xhat * g, xhat, rstd


def _rms_bwd(dxn, xhat, rstd, g):
    dxhat = dxn * g
    dh = rstd * (dxhat - xhat * jnp.mean(dxhat * xhat, axis=-1, keepdims=True))
    dg = jnp.sum(dxn * xhat, axis=0, keepdims=True)
    return dh, dg


def _row_spec(tm, n):
    return pl.BlockSpec((tm, n), lambda i, *_: (i, 0))


def _full_spec(shape):
    nd = len(shape)
    return pl.BlockSpec(shape, lambda *_: (0,) * nd)


def _ffn_weight_specs(fidx):
    return [pl.BlockSpec((D, FF_TILE), lambda i, j: (2 * fidx, j)),
            pl.BlockSpec((D, FF_TILE), lambda i, j: (2 * fidx + 1, j)),
            pl.BlockSpec((2, SHP, D), lambda i, j: (j, fidx, 0))]


def ffn_fwd(h, g, wa, wb, fidx, name):
    T = h.shape[0]
    tm = min(TOK_TILE, T)
    nj = FFP // FF_TILE

    def body(h_ref, g_ref, wg_ref, wu_ref, wd_ref, o_ref, gt_ref, up_ref, xn_ref):
        j = pl.program_id(1)

        @pl.when(j == 0)
        def _():
            hh = h_ref[...]
            xn_ref[...] = _rms(hh, g_ref[...])[0].astype(BF16)
            o_ref[...] = hh

        xn = xn_ref[...]
        gt = _mm(xn, wg_ref[...])
        up = _mm(xn, wu_ref[...])
        gt_ref[...] = gt.astype(BF16)
        up_ref[...] = up.astype(BF16)
        act = gt * _sigmoid(gt) * up
        o_ref[...] += 0.5 * _mm(act, wd_ref[...].reshape(FF_TILE, D))

    tile = pl.BlockSpec((tm, FF_TILE), lambda i, j: (i, j))
    return pl.pallas_call(
        body, name=name, grid=(T // tm, nj),
        in_specs=[pl.BlockSpec((tm, D), lambda i, j: (i, 0)),
                  pl.BlockSpec((1, D), lambda i, j: (0, 0))] + _ffn_weight_specs(fidx),
        out_specs=[pl.BlockSpec((tm, D), lambda i, j: (i, 0)), tile, tile,
                   pl.BlockSpec((tm, D), lambda i, j: (i, 0))],
        out_shape=[jax.ShapeDtypeStruct((T, D), F32), jax.ShapeDtypeStruct((T, FFP), BF16),
                   jax.ShapeDtypeStruct((T, FFP), BF16), jax.ShapeDtypeStruct((T, D), BF16)],
        compiler_params=_cp("parallel", "arbitrary"),
    )(h, g, wa, wa, wb)


def ffn_bwd(h, dy, g, gt_saved, up_saved, wa, wb, fidx, name):
    T = h.shape[0]
    tm = min(TOK_TILE, T)
    nj = FFP // FF_TILE

    def body(h_ref, dy_ref, g_ref, gt_ref, up_ref, wg_ref, wu_ref, wd_ref,
             dh_ref, dg_ref, du_ref, a_ref, dn_ref, dxn_s):
        i = pl.program_id(0)
        j = pl.program_id(1)

        @pl.when(j == 0)
        def _():
            dxn_s[...] = jnp.zeros_like(dxn_s)

        @pl.when((i == 0) & (j == 0))
        def _():
            dn_ref[...] = jnp.zeros_like(dn_ref)

        gt = gt_ref[...].astype(F32)
        up = up_ref[...].astype(F32)
        sg = _sigmoid(gt)
        si = gt * sg
        da = _mm_nt(0.5 * dy_ref[...], wd_ref[...].reshape(FF_TILE, D))
        dup = da * si
        dgt = da * up * (sg * (1.0 + gt * (1.0 - sg)))
        dg_ref[...] = dgt.astype(BF16)
        du_ref[...] = dup.astype(BF16)
        a_ref[...] = (si * up).astype(BF16)
        dxn_s[...] += _mm_nt(dgt, wg_ref[...]) + _mm_nt(dup, wu_ref[...])

        @pl.when(j == nj - 1)
        def _():
            gg = g_ref[...]
            _, xhat, rstd = _rms(h_ref[...], gg)
            dh, dn = _rms_bwd(dxn_s[...], xhat, rstd, gg)
            dh_ref[...] = dy_ref[...] + dh
            dn_ref[...] += dn

    tile = pl.BlockSpec((tm, FF_TILE), lambda i, j: (i, j))
    return pl.pallas_call(
        body, name=name, grid=(T // tm, nj),
        in_specs=[pl.BlockSpec((tm, D), lambda i, j: (i, 0)),
                  pl.BlockSpec((tm, D), lambda i, j: (i, 0)),
                  pl.BlockSpec((1, D), lambda i, j: (0, 0)), tile, tile] + _ffn_weight_specs(fidx),
        out_specs=[pl.BlockSpec((tm, D), lambda i, j: (i, 0)), tile, tile, tile,
                   pl.BlockSpec((1, D), lambda i, j: (0, 0))],
        out_shape=[jax.ShapeDtypeStruct((T, D), F32)] + [jax.ShapeDtypeStruct((T, FFP), BF16)] * 3
        + [jax.ShapeDtypeStruct((1, D), F32)],
        scratch_shapes=[pltpu.VMEM((tm, D), F32)],
        compiler_params=_cp("arbitrary", "arbitrary"),
    )(h, dy, g, gt_saved, up_saved, wa, wa, wb)


def _pick(n, prefs):
    for t in prefs:
        if n % t == 0:
            return t
    return n


def _tn_body(nk, scale, out_dtype, squeeze):
    def body(a_ref, b_ref, *rest):
        o_ref, acc = rest[-2], rest[-1]
        k = pl.program_id(2)

        @pl.when(k == 0)
        def _():
            acc[...] = jnp.zeros_like(acc)

        acc[...] += _mm_tn(a_ref[...], b_ref[...])

        @pl.when(k == nk - 1)
        def _():
            res = (scale * acc[...]).astype(out_dtype)
            if squeeze:
                o_ref[0] = res
            else:
                o_ref[...] = res

    return body


def matmul_tn(a, b, name, scale=1.0, out_dtype=BF16):
    T, M = a.shape
    N = b.shape[1]
    tmm = _pick(M, (512, 256))
    tnn = _pick(N, (1024, 2432))
    tk = min(TOK_TILE, T)
    nk = T // tk
    return pl.pallas_call(
        _tn_body(nk, scale, out_dtype, False), name=name, grid=(M // tmm, N // tnn, nk),
        in_specs=[pl.BlockSpec((tk, tmm), lambda i, j, k: (k, i)),
                  pl.BlockSpec((tk, tnn), lambda i, j, k: (k, j))],
        out_specs=pl.BlockSpec((tmm, tnn), lambda i, j, k: (i, j)),
        out_shape=jax.ShapeDtypeStruct((M, N), out_dtype),
        scratch_shapes=[pltpu.VMEM((tmm, tnn), F32)],
        compiler_params=_cp("parallel", "parallel", "arbitrary"),
    )(a, b)


def grad_cols(a, b, dst, slot, name):
    T = a.shape[0]
    tmm, tnn = D, FFP // 2
    tk = min(TOK_TILE, T)
    nk = T // tk
    return pl.pallas_call(
        _tn_body(nk, 1.0, BF16, True), name=name, grid=(D // tmm, FFP // tnn, nk),
        in_specs=[pl.BlockSpec((tk, tmm), lambda i, j, k: (k, i)),
                  pl.BlockSpec((tk, tnn), lambda i, j, k: (k, j)),
                  pl.BlockSpec(memory_space=pl.ANY)],
        out_specs=pl.BlockSpec((1, tmm, tnn), lambda i, j, k: (slot, i, j)),
        out_shape=jax.ShapeDtypeStruct(dst.shape, dst.dtype),
        scratch_shapes=[pltpu.VMEM((tmm, tnn), F32)],
        input_output_aliases={2: 0},
        compiler_params=_cp("parallel", "parallel", "arbitrary"),
    )(a, b, dst)


def grad_rows(a, b, dst, row0, rows, name, scale=1.0):
    T = a.shape[0]
    tk = min(TOK_TILE, T)
    nk = T // tk
    blk = row0 // rows

    def body(a_ref, b_ref, dst_ref, o_ref, acc):
        k = pl.program_id(0)

        @pl.when(k == 0)
        def _():
            acc[...] = jnp.zeros_like(acc)

        acc[...] += _mm_tn(a_ref[...], b_ref[...])

        @pl.when(k == nk - 1)
        def _():
            o_ref[...] = (scale * acc[...]).astype(BF16).reshape(N_DEV, rows, D)

    return pl.pallas_call(
        body, name=name, grid=(nk,),
        in_specs=[pl.BlockSpec((tk, N_DEV * rows), lambda k: (k, 0)),
                  pl.BlockSpec((tk, D), lambda k: (k, 0)),
                  pl.BlockSpec(memory_space=pl.ANY)],
        out_specs=pl.BlockSpec((N_DEV, rows, D), lambda k: (0, blk, 0)),
        out_shape=jax.ShapeDtypeStruct(dst.shape, dst.dtype),
        scratch_shapes=[pltpu.VMEM((N_DEV * rows, D), F32)],
        input_output_aliases={2: 0},
        compiler_params=_cp("arbitrary"),
    )(a, b, dst)


U_SPLITS = (512, 768, 1024, 128)
U_OFFS = (0, 512, 1280, 2304)


def mixin_fwd(h, g, w_in, name):
    T = h.shape[0]
    tm = min(TOK_TILE, T)

    def body(h_ref, g_ref, w_ref, u0, u1, u2, u3, xn_ref):
        xn = _rms(h_ref[...], g_ref[...])[0].astype(BF16)
        xn_ref[...] = xn
        u = _mm(xn, w_ref[...])
        for ref, off, n in zip((u0, u1, u2, u3), U_OFFS, U_SPLITS):
            ref[...] = u[:, off:off + n]

    return pl.pallas_call(
        body, name=name, grid=(T // tm,),
        in_specs=[_row_spec(tm, D), _full_spec((1, D)), _full_spec((D, D_IN_PAD))],
        out_specs=[_row_spec(tm, n) for n in U_SPLITS] + [_row_spec(tm, D)],
        out_shape=[jax.ShapeDtypeStruct((T, n), F32) for n in U_SPLITS]
        + [jax.ShapeDtypeStruct((T, D), BF16)],
        compiler_params=_cp("parallel"),
    )(h, g, w_in)


def mixin_bwd(h, dh_in, g, w_in, dus, name):
    T = h.shape[0]
    tm = min(TOK_TILE, T)

    def body(h_ref, dhi_ref, g_ref, w_ref, d0, d1, d2, d3, dh_ref, du_ref, dn_ref):
        @pl.when(pl.program_id(0) == 0)
        def _():
            dn_ref[...] = jnp.zeros_like(dn_ref)

        dxn = jnp.zeros((tm, D), F32)
        for ref, off, n in zip((d0, d1, d2, d3), U_OFFS, U_SPLITS):
            du = ref[...]
            du_ref[:, off:off + n] = du.astype(BF16)
            dxn += _mm_nt(du, w_ref[:, off:off + n])
        gg = g_ref[...]
        _, xhat, rstd = _rms(h_ref[...], gg)
        dh, dn = _rms_bwd(dxn, xhat, rstd, gg)
        dh_ref[...] = dhi_ref[...] + dh
        dn_ref[...] += dn

    return pl.pallas_call(
        body, name=name, grid=(T // tm,),
        in_specs=[_row_spec(tm, D), _row_spec(tm, D), _full_spec((1, D)), _full_spec((D, D_IN_PAD))]
        + [_row_spec(tm, n) for n in U_SPLITS],
        out_specs=[_row_spec(tm, D), _row_spec(tm, D_IN_PAD), _full_spec((1, D))],
        out_shape=[jax.ShapeDtypeStruct((T, D), F32), jax.ShapeDtypeStruct((T, D_IN_PAD), BF16),
                   jax.ShapeDtypeStruct((1, D), F32)],
        compiler_params=_cp("arbitrary"),
    )(h, dh_in, g, w_in, *dus)


def _shift_down(x, s, row):
    if s == 0:
        return x
    return jnp.where(row >= s, pltpu.roll(x, s, 0), 0.0)


def _shift_up(x, s, row):
    if s == 0:
        return x
    n = x.shape[0]
    return jnp.where(row < n - s, pltpu.roll(x, n - s, 0), 0.0)


def conv_fwd(x, w, b, S, col0, C, name):
    T = x.shape[0]
    cb0 = col0 // LANE

    def body(x_ref, w_ref, b_ref, y_ref):
        xx = x_ref[...]
        row = _iota(xx.shape, 0)
        y = xx * w_ref[3:4, :] + b_ref[...]
        for k in range(3):
            y += _shift_down(xx, 3 - k, row) * w_ref[k:k + 1, :]
        y_ref[...] = y

    return pl.pallas_call(
        body, name=name, grid=(T // S, C // LANE),
        in_specs=[pl.BlockSpec((S, LANE), lambda s, c: (s, cb0 + c)),
                  pl.BlockSpec((4, LANE), lambda s, c: (0, c)),
                  pl.BlockSpec((1, LANE), lambda s, c: (0, c))],
        out_specs=pl.BlockSpec((S, LANE), lambda s, c: (s, c)),
        out_shape=jax.ShapeDtypeStruct((T, C), F32),
        compiler_params=_cp("parallel", "parallel"),
    )(x, w, b)


def conv_bwd(x, dy, w, S, col0, C, name):
    T = x.shape[0]
    cb0 = col0 // LANE

    def body(x_ref, dy_ref, w_ref, dx_ref, dwb_ref):
        @pl.when(pl.program_id(1) == 0)
        def _():
            dwb_ref[...] = jnp.zeros_like(dwb_ref)

        xx = x_ref[...]
        dd = dy_ref[...]
        row = _iota(xx.shape, 0)
        dx = dd * w_ref[3:4, :]
        for k in range(3):
            dx += _shift_up(dd, 3 - k, row) * w_ref[k:k + 1, :]
        dx_ref[...] = dx
        for k in range(4):
            dwb_ref[k:k + 1, :] += jnp.sum(dd * _shift_down(xx, 3 - k, row), axis=0, keepdims=True)
        dwb_ref[4:5, :] += jnp.sum(dd, axis=0, keepdims=True)

    return pl.pallas_call(
        body, name=name, grid=(C // LANE, T // S),
        in_specs=[pl.BlockSpec((S, LANE), lambda c, s: (s, cb0 + c)),
                  pl.BlockSpec((S, LANE), lambda c, s: (s, c)),
                  pl.BlockSpec((4, LANE), lambda c, s: (0, c))],
        out_specs=[pl.BlockSpec((S, LANE), lambda c, s: (s, c)),
                   pl.BlockSpec((8, LANE), lambda c, s: (0, c))],
        out_shape=[jax.ShapeDtypeStruct((T, C), F32), jax.ShapeDtypeStruct((8, C), F32)],
        compiler_params=_cp("parallel", "arbitrary"),
    )(x, dy, w)


def _scan(a, b, row):
    n = a.shape[0]
    d = 1
    while d < n:
        keep = row >= d
        b = a * jnp.where(keep, pltpu.roll(b, d, 0), 0.0) + b
        a = a * jnp.where(keep, pltpu.roll(a, d, 0), 1.0)
        d *= 2
    return b


def _rscan(a, b, row):
    n = a.shape[0]
    d = 1
    while d < n:
        keep = row < n - d
        b = a * jnp.where(keep, pltpu.roll(b, n - d, 0), 0.0) + b
        a = a * jnp.where(keep, pltpu.roll(a, n - d, 0), 1.0)
        d *= 2
    return b


GELU_C = math.sqrt(2.0 / math.pi)


def _gelu(x):
    t = jnp.tanh(GELU_C * (x + 0.044715 * (x * x * x)))
    return 0.5 * x * (1.0 + t), t


def _lru_gates(xr, wa, ba, wx, bx, lam):
    r = _sigmoid(_mm(xr, wa) + ba)
    i = _sigmoid(_mm(xr, wx) + bx)
    sp = jnp.maximum(-lam, 0.0) + jnp.log(1.0 + jnp.exp(-jnp.abs(lam)))
    la = -LRU_C * r * sp
    a = jnp.exp(la)
    e2 = a * a
    m = jnp.sqrt(-jnp.tanh(la) * (e2 + 1.0))
    return r, i, sp, a, e2, m


def lru_fwd(xr, u_lru, wa, wx, vec, S, name):
    T = xr.shape[0]

    def body(xr_ref, gt_ref, wa_ref, wx_ref, vec_ref, y_ref):
        x = xr_ref[...]
        row = _iota(x.shape, 0)
        r, i, sp, a, e2, m = _lru_gates(x, wa_ref[...], vec_ref[0:1, :], wx_ref[...], vec_ref[1:2, :],
                                        vec_ref[2:3, :])
        hh = _scan(a, m * (i * x), row)
        y_ref[...] = _gelu(gt_ref[...])[0] * hh

    return pl.pallas_call(
        body, name=name, grid=(T // S, LRU_W // LANE),
        in_specs=[pl.BlockSpec((S, LANE), lambda s, c: (s, c)),
                  pl.BlockSpec((S, LANE), lambda s, c: (s, 2 + c)),
                  pl.BlockSpec((LANE, LANE), lambda s, c: (c, c)),
                  pl.BlockSpec((LANE, LANE), lambda s, c: (c, c)),
                  pl.BlockSpec((8, LANE), lambda s, c: (0, c))],
        out_specs=pl.BlockSpec((S, LANE), lambda s, c: (s, c)),
        out_shape=jax.ShapeDtypeStruct((T, LRU_W), F32),
        compiler_params=_cp("parallel", "parallel"),
    )(xr, u_lru, wa, wx, vec)


def lru_bwd(xr, u_lru, dy, wa, wx, vec, S, name):
    T = xr.shape[0]

    def body(xr_ref, gt_ref, dy_ref, wa_ref, wx_ref, vec_ref,
             dxr_ref, dgt_ref, dwa_ref, dwx_ref, dvec_ref):
        @pl.when(pl.program_id(1) == 0)
        def _():
            dwa_ref[...] = jnp.zeros_like(dwa_ref)
            dwx_ref[...] = jnp.zeros_like(dwx_ref)
            dvec_ref[...] = jnp.zeros_like(dvec_ref)

        x = xr_ref[...]
        n = x.shape[0]
        row = _iota(x.shape, 0)
        lam = vec_ref[2:3, :]
        r, i, sp, a, e2, m = _lru_gates(x, wa_ref[...], vec_ref[0:1, :], wx_ref[...], vec_ref[1:2, :], lam)
        v = i * x
        hh = _scan(a, m * v, row)
        gt = gt_ref[...]
        dy = dy_ref[...]
        ge, t = _gelu(gt)
        dgt_ref[...] = dy * hh * (0.5 * (1.0 + t) + 0.5 * gt * (1.0 - t * t) * GELU_C
                                  * (1.0 + 3.0 * 0.044715 * gt * gt))
        a_next = jnp.where(row < n - 1, pltpu.roll(a, n - 1, 0), 0.0)
        G = _rscan(a_next, dy * ge, row)
        da = G * _shift_down(hh, 1, row)
        dv = G * m
        dla = da * a - (G * v) * e2 / m
        dr = dla * (-LRU_C * sp)
        dsp = jnp.sum(dla * (-LRU_C * r), axis=0, keepdims=True)
        dra = dr * r * (1.0 - r)
        dia = (dv * x) * i * (1.0 - i)
        dxr_ref[...] = dv * i + _mm_nt(dra, wa_ref[...]) + _mm_nt(dia, wx_ref[...])
        dwa_ref[0] += _mm_tn(x, dra)
        dwx_ref[0] += _mm_tn(x, dia)
        dvec_ref[0:1, :] += jnp.sum(dra, axis=0, keepdims=True)
        dvec_ref[1:2, :] += jnp.sum(dia, axis=0, keepdims=True)
        dvec_ref[2:3, :] += dsp * (-_sigmoid(-lam))

    return pl.pallas_call(
        body, name=name, grid=(LRU_W // LANE, T // S),
        in_specs=[pl.BlockSpec((S, LANE), lambda c, s: (s, c)),
                  pl.BlockSpec((S, LANE), lambda c, s: (s, 2 + c)),
                  pl.BlockSpec((S, LANE), lambda c, s: (s, c)),
                  pl.BlockSpec((LANE, LANE), lambda c, s: (c, c)),
                  pl.BlockSpec((LANE, LANE), lambda c, s: (c, c)),
                  pl.BlockSpec((8, LANE), lambda c, s: (0, c))],
        out_specs=[pl.BlockSpec((S, LANE), lambda c, s: (s, c)),
                   pl.BlockSpec((S, LANE), lambda c, s: (s, c)),
                   pl.BlockSpec((1, LANE, LANE), lambda c, s: (c, 0, 0)),
                   pl.BlockSpec((1, LANE, LANE), lambda c, s: (c, 0, 0)),
                   pl.BlockSpec((8, LANE), lambda c, s: (0, c))],
        out_shape=[jax.ShapeDtypeStruct((T, LRU_W), F32), jax.ShapeDtypeStruct((T, LRU_W), F32),
                   jax.ShapeDtypeStruct((2, LANE, LANE), F32), jax.ShapeDtypeStruct((2, LANE, LANE), F32),
                   jax.ShapeDtypeStruct((8, LRU_W), F32)],
        compiler_params=_cp("parallel", "arbitrary"),
    )(xr, u_lru, dy, wa, wx, vec)


def _bucket_table():
    qi = np.arange(BQ)[:, None]
    kj = np.arange(2 * BQ)[None, :]
    dist = BQ + qi - kj
    band = (dist >= 0) & (dist < WINDOW)
    dd = np.maximum(dist, 0)
    max_exact = REL_BUCKETS // 2
    large = max_exact + (np.log(np.maximum(dd, 1).astype(np.float32) / np.float32(max_exact))
                         / np.float32(math.log(REL_MAX_DIST / max_exact))
                         * np.float32(REL_BUCKETS - max_exact)).astype(np.int32)
    large = np.minimum(large, REL_BUCKETS - 1)
    bucket = np.where(dd < max_exact, dd, large)
    return np.where(band, bucket, -1).astype(np.int32)


def _att_specs(S):
    nb = S // BQ
    qc = ATT_W // LANE
    return [pl.BlockSpec((BQ, ATT_W), lambda b, n: (b * nb + n, 0)),
            pl.BlockSpec((BQ, KV_W), lambda b, n: (b * nb + jnp.maximum(n - 1, 0), qc)),
            pl.BlockSpec((BQ, KV_W), lambda b, n: (b * nb + n, qc)),
            pl.BlockSpec((BQ, KV_W), lambda b, n: (b * nb + jnp.maximum(n - 1, 0), qc + 1)),
            pl.BlockSpec((BQ, KV_W), lambda b, n: (b * nb + n, qc + 1))]


def _att_bias(bk, rb_ref, bias_s):
    for h in range(ATT_H):
        acc = jnp.zeros(bk.shape, F32)
        for bb in range(REL_BUCKETS):
            acc = jnp.where(bk == bb, rb_ref[bb * ATT_H + h], acc)
        bias_s[h] = acc


def _att_probs(qh, kg, bias, valid, sink):
    s = _mm_nt(qh, kg) * (HD ** -0.5) + bias
    s = jnp.where(valid, s, NEG)
    m = jnp.maximum(jnp.max(s, axis=-1, keepdims=True), sink)
    e = jnp.exp(s - m)
    es = jnp.exp(sink - m)
    den = jnp.sum(e, axis=-1, keepdims=True) + es
    return e / den, es / den


def attn_fwd(u_att, sinks, rel_bias, S, name):
    T = u_att.shape[0]
    nb = S // BQ
    table = jnp.asarray(_bucket_table())

    def body(sk_ref, rb_ref, bk_ref, q_ref, kp_ref, kc_ref, vp_ref, vc_ref, o_ref, bias_s):
        b = pl.program_id(0)
        n = pl.program_id(1)
        bk = bk_ref[...]

        @pl.when((b == 0) & (n == 0))
        def _():
            _att_bias(bk, rb_ref, bias_s)

        valid = (bk >= 0) & ((n > 0) | (_iota(bk.shape, 1) >= BQ))
        for h in range(ATT_H):
            gs = slice(HD * (h // ATT_G), HD * (h // ATT_G + 1))
            kg = jnp.concatenate([kp_ref[:, gs], kc_ref[:, gs]], axis=0)
            vg = jnp.concatenate([vp_ref[:, gs], vc_ref[:, gs]], axis=0)
            p, _ = _att_probs(q_ref[:, HD * h:HD * (h + 1)], kg, bias_s[h], valid, sk_ref[h])
            o_ref[:, HD * h:HD * (h + 1)] = _mm(p, vg)

    smem = pl.BlockSpec(memory_space=pltpu.SMEM)
    return pl.pallas_call(
        body, name=name, grid=(T // S, nb),
        in_specs=[smem, smem, _full_spec((BQ, 2 * BQ))] + _att_specs(S),
        out_specs=pl.BlockSpec((BQ, ATT_W), lambda b, n: (b * nb + n, 0)),
        out_shape=jax.ShapeDtypeStruct((T, ATT_W), F32),
        scratch_shapes=[pltpu.VMEM((ATT_H, BQ, 2 * BQ), F32)],
        compiler_params=_cp("arbitrary", "arbitrary"),
    )(sinks, rel_bias, table, u_att, u_att, u_att, u_att, u_att)


def attn_bwd(u_att, dy, sinks, rel_bias, S, name):
    T = u_att.shape[0]
    nb = S // BQ
    nB = T // S
    table = jnp.asarray(_bucket_table())
    scale = HD ** -0.5

    def body(sk_ref, rb_ref, bk_ref, q_ref, kp_ref, kc_ref, vp_ref, vc_ref, dy_ref,
             du_ref, drel_ref, dsk_ref, bias_s, dbias_s):
        b = pl.program_id(0)
        n = pl.program_id(1)
        bk = bk_ref[...]

        @pl.when((b == 0) & (n == 0))
        def _():
            _att_bias(bk, rb_ref, bias_s)
            dbias_s[...] = jnp.zeros_like(dbias_s)
            dsk_ref[...] = jnp.zeros_like(dsk_ref)
            drel_ref[...] = jnp.zeros_like(drel_ref)

        @pl.when(n == 0)
        def _():
            du_ref[...] = jnp.zeros_like(du_ref)

        valid = (bk >= 0) & ((n > 0) | (_iota(bk.shape, 1) >= BQ))
        r_cur = pl.multiple_of(n * BQ, BQ)
        r_prev = pl.multiple_of(jnp.maximum(n - 1, 0) * BQ, BQ)
        for g in range(ATT_KV):
            gs = slice(HD * g, HD * (g + 1))
            kg = jnp.concatenate([kp_ref[:, gs], kc_ref[:, gs]], axis=0)
            vg = jnp.concatenate([vp_ref[:, gs], vc_ref[:, gs]], axis=0)
            dk = jnp.zeros((2 * BQ, HD), F32)
            dv = jnp.zeros((2 * BQ, HD), F32)
            for e in range(ATT_G):
                h = g * ATT_G + e
                qh = q_ref[:, HD * h:HD * (h + 1)]
                do = dy_ref[:, HD * h:HD * (h + 1)]
                p, ps = _att_probs(qh, kg, bias_s[h], valid, sk_ref[h])
                dp = _mm_nt(do, vg)
                delta = jnp.sum(p * dp, axis=-1, keepdims=True)
                ds = p * (dp - delta)
                dbias_s[h] += ds
                dsk_ref[h:h + 1, :] += jnp.broadcast_to(
                    jnp.sum(-ps * delta, axis=0, keepdims=True), (1, LANE))
                dss = ds * scale
                du_ref[pl.ds(r_cur, BQ), HD * h:HD * (h + 1)] = _mm(dss, kg)
                dk += _mm_tn(dss, qh)
                dv += _mm_tn(p, do)
            ck = ATT_W + HD * g
            cv = ATT_W + KV_W + HD * g
            du_ref[pl.ds(r_prev, BQ), ck:ck + HD] += dk[0:BQ]
            du_ref[pl.ds(r_cur, BQ), ck:ck + HD] += dk[BQ:]
            du_ref[pl.ds(r_prev, BQ), cv:cv + HD] += dv[0:BQ]
            du_ref[pl.ds(r_cur, BQ), cv:cv + HD] += dv[BQ:]

        @pl.when((b == nB - 1) & (n == nb - 1))
        def _():
            lane = _iota((1, LANE), 1)
            for h in range(ATT_H):
                db = dbias_s[h]
                acc = jnp.zeros((1, LANE), F32)
                for bb in range(REL_BUCKETS):
                    val = jnp.sum(jnp.sum(jnp.where(bk == bb, db, 0.0), axis=1, keepdims=True),
                                  axis=0, keepdims=True)
                    acc = jnp.where(lane == bb, val, acc)
                drel_ref[h:h + 1, :] = acc

    smem = pl.BlockSpec(memory_space=pltpu.SMEM)
    return pl.pallas_call(
        body, name=name, grid=(nB, nb),
        in_specs=[smem, smem, _full_spec((BQ, 2 * BQ))] + _att_specs(S)
        + [pl.BlockSpec((BQ, ATT_W), lambda b, n: (b * nb + n, 0))],
        out_specs=[pl.BlockSpec((S, ATT_W + 2 * KV_W), lambda b, n: (b, 0)),
                   _full_spec((8, LANE)), _full_spec((8, LANE))],
        out_shape=[jax.ShapeDtypeStruct((T, ATT_W + 2 * KV_W), F32),
                   jax.ShapeDtypeStruct((8, LANE), F32), jax.ShapeDtypeStruct((8, LANE), F32)],
        scratch_shapes=[pltpu.VMEM((ATT_H, BQ, 2 * BQ), F32), pltpu.VMEM((ATT_H, BQ, 2 * BQ), F32)],
        compiler_params=_cp("arbitrary", "arbitrary"),
    )(sinks, rel_bias, table, u_att, u_att, u_att, u_att, u_att, dy)


def _head_of(i):
    return lax.shift_right_logical(i, 6)


def _head_mask(shape):
    return (_head_of(_iota(shape, 0)) == _head_of(_iota(shape, 1))).astype(F32)


def _dn_point(c, uba, alog, dtb):
    s = c * _sigmoid(c)
    qt, kt, vt = s[:, 0:256], s[:, 256:512], s[:, 512:768]
    ones_bd = _head_mask((DN_W, DN_W))
    q = qt * lax.rsqrt(_mmx(qt * qt, ones_bd) + EPS) * (HD ** -0.5)
    k = kt * lax.rsqrt(_mmx(kt * kt, ones_bd) + EPS)
    sel = _head_of(_iota((LANE, DN_W), 1))
    row = _iota((LANE, DN_W), 0)
    braw = _mmx(uba, (row == sel).astype(F32))
    araw = _mmx(uba, (row == sel + DN_H).astype(F32)) + dtb
    beta = _sigmoid(braw)
    g = -jnp.exp(alog) * (jnp.maximum(araw, 0.0) + jnp.log(1.0 + jnp.exp(-jnp.abs(araw))))
    return q, k, vt, g, beta


def dn_point_fwd(c, uba, alog, dtb, name):
    T = c.shape[0]
    tm = min(TOK_TILE, T)

    def body(c_ref, u_ref, al_ref, dt_ref, *outs):
        for ref, val in zip(outs, _dn_point(c_ref[...], u_ref[...], al_ref[...], dt_ref[...])):
            ref[...] = val

    return pl.pallas_call(
        body, name=name, grid=(T // tm,),
        in_specs=[_row_spec(tm, 768), _row_spec(tm, LANE), _full_spec((1, DN_W)), _full_spec((1, DN_W))],
        out_specs=[_row_spec(tm, DN_W)] * 5,
        out_shape=[jax.ShapeDtypeStruct((T, DN_W), F32)] * 5,
        compiler_params=_cp("parallel"),
    )(c, uba, alog, dtb)


def dn_point_bwd(c, uba, alog, dtb, douts, name):
    T = c.shape[0]
    tm = min(TOK_TILE, T)

    def body(c_ref, u_ref, al_ref, dt_ref, dq, dk, dv, dg, db, dc_ref, du_ref, dvec_ref):
        @pl.when(pl.program_id(0) == 0)
        def _():
            dvec_ref[...] = jnp.zeros_like(dvec_ref)

        _, vjp = jax.vjp(_dn_point, c_ref[...], u_ref[...], al_ref[...], dt_ref[...])
        dc, du, dal, ddt = vjp((dq[...], dk[...], dv[...], dg[...], db[...]))
        dc_ref[...] = dc
        du_ref[...] = du
        fold = (_iota((LANE, DN_W), 0) == _head_of(_iota((LANE, DN_W), 1))).astype(F32)
        both = jnp.concatenate([dal, ddt, jnp.zeros((6, DN_W), F32)], axis=0)
        dvec_ref[...] += _mmx_nt(both, fold)

    return pl.pallas_call(
        body, name=name, grid=(T // tm,),
        in_specs=[_row_spec(tm, 768), _row_spec(tm, LANE), _full_spec((1, DN_W)), _full_spec((1, DN_W))]
        + [_row_spec(tm, DN_W)] * 5,
        out_specs=[_row_spec(tm, 768), _row_spec(tm, LANE), _full_spec((8, LANE))],
        out_shape=[jax.ShapeDtypeStruct((T, 768), F32), jax.ShapeDtypeStruct((T, LANE), F32),
                   jax.ShapeDtypeStruct((8, LANE), F32)],
        compiler_params=_cp("arbitrary"),
    )(c, uba, alog, dtb, *douts)


def _unit_lower_inverses(lmats):
    eye = (_iota(lmats[0].shape, 0) == _iota(lmats[0].shape, 1)).astype(F32)
    tinvs = [eye - lm for lm in lmats]
    pws = list(lmats)
    for _ in range(5):
        pws = [_mm3(pw, pw) for pw in pws]
        tinvs = [t + _mm3(t, pw) for t, pw in zip(tinvs, pws)]
    return tuple(tinvs)


def _inverse_bwd(tinv, d):
    return -_mm3_nt(_mm3_tn(tinv, d), tinv)


@jax.custom_vjp
def _tri_invs(lmats):
    return _unit_lower_inverses(lmats)


def _tri_invs_fwd(lmats):
    tinvs = _unit_lower_inverses(lmats)
    return tinvs, tinvs


_tri_invs.defvjp(_tri_invs_fwd, lambda tinvs, ds: (tuple(_inverse_bwd(t, d) for t, d in zip(tinvs, ds)),))


@jax.custom_vjp
def _tri_inv_known(lmat, tinv):
    return tinv


_tri_inv_known.defvjp(lambda lmat, tinv: (tinv, tinv),
                      lambda tinv, d: (_inverse_bwd(tinv, d), jnp.zeros_like(tinv)))


DN_SUB = 4


def _dn_stack(x):
    return jnp.concatenate([x, x, x, x], axis=0) * _head_mask((DN_W, DN_W))


def _dn_pre_inverse(q, k, v, g, beta):
    hm = _head_mask((DN_W, DN_W))
    ri = _iota((DN_W, DN_W), 0) & (CHUNK - 1)
    ci = _iota((DN_W, DN_W), 1) & (CHUNK - 1)
    tri64 = (_iota((CHUNK, CHUNK), 0) >= _iota((CHUNK, CHUNK), 1)).astype(F32)
    gc = _mm3(tri64, g)
    ks = _dn_stack(k)
    gcol = jnp.sum(_dn_stack(gc), axis=1, keepdims=True) * (1.0 / HD)
    gmat = jnp.broadcast_to(gcol, (DN_W, DN_W))
    decay = jnp.exp(jnp.minimum(gmat - gmat.T, 0.0))
    lmat = _mm_nt(_dn_stack(k * beta), ks) * decay * (hm * (ri > ci).astype(F32))
    att = _mm_nt(_dn_stack(q), ks) * decay * (hm * (ri >= ci).astype(F32))
    return lmat, att, gc


def _dn_post_inverse(q, k, v, g, beta, tinv, att, gc):
    glast = jnp.sum(g, axis=0, keepdims=True)
    eg = jnp.exp(gc)
    u = _mm(tinv, _dn_stack(v * beta))
    w = _mm(tinv, _dn_stack(k * beta * eg))
    return u, w, att, _dn_stack(q * eg), _dn_stack(k * jnp.exp(glast - gc)), jnp.exp(glast), tinv


def _dn_apply(state, prep):
    u, w, att, qe, kd, eglast, _ = prep
    vn = u - _mm(w, state)
    o4 = _mm(qe, state) + _mm(att, vn)
    o = o4[0:64] + o4[64:128] + o4[128:192] + o4[192:256]
    return o, state * eglast + _mm_tn(kd, vn)


def _dn_chunks(state, q, k, v, g, beta, knowns=None):
    n = q.shape[0] // CHUNK
    chunks = [tuple(x[c * CHUNK:(c + 1) * CHUNK] for x in (q, k, v, g, beta)) for c in range(n)]
    pre = [_dn_pre_inverse(*ch) for ch in chunks]
    if knowns is None:
        tinvs = _tri_invs(tuple(p[0] for p in pre))
    else:
        tinvs = [_tri_inv_known(p[0], known) for p, known in zip(pre, knowns)]
    preps = [_dn_post_inverse(*ch, tinv, p[1], p[2]) for ch, tinv, p in zip(chunks, tinvs, pre)]
    outs = []
    for prep in preps:
        o, state = _dn_apply(state, prep)
        outs.append(o)
    return jnp.concatenate(outs, axis=0), state, [prep[-1] for prep in preps]


def dn_scan_fwd(q, k, v, g, beta, S, name):
    T = q.shape[0]
    rows = DN_SUB * CHUNK
    ns = S // rows

    def body(q_ref, k_ref, v_ref, g_ref, b_ref, o_ref, st_ref, ti_ref, s_s):
        @pl.when(pl.program_id(1) == 0)
        def _():
            s_s[...] = jnp.zeros_like(s_s)

        st = s_s[...]
        st_ref[0] = st
        o, new, tinvs = _dn_chunks(st, q_ref[...], k_ref[...], v_ref[...], g_ref[...], b_ref[...])
        o_ref[...] = o
        for c, tinv in enumerate(tinvs):
            ti_ref[c] = tinv
        s_s[...] = new

    spec = pl.BlockSpec((rows, DN_W), lambda b, t: (b * ns + t, 0))
    return pl.pallas_call(
        body, name=name, grid=(T // S, ns),
        in_specs=[spec] * 5,
        out_specs=[spec, pl.BlockSpec((1, DN_W, DN_W), lambda b, t: (b * ns + t, 0, 0)),
                   pl.BlockSpec((DN_SUB, DN_W, DN_W), lambda b, t: (b * ns + t, 0, 0))],
        out_shape=[jax.ShapeDtypeStruct((T, DN_W), F32),
                   jax.ShapeDtypeStruct((T // rows, DN_W, DN_W), F32),
                   jax.ShapeDtypeStruct((T // CHUNK, DN_W, DN_W), F32)],
        scratch_shapes=[pltpu.VMEM((DN_W, DN_W), F32)],
        compiler_params=_cp("parallel", "arbitrary"),
    )(q, k, v, g, beta)


def dn_scan_bwd(q, k, v, g, beta, states, tinvs, do, S, name):
    T = q.shape[0]
    rows = DN_SUB * CHUNK
    ns = S // rows

    def body(q_ref, k_ref, v_ref, g_ref, b_ref, st_ref, ti_ref, do_ref, dq, dk, dv, dg, db, ds_s):
        @pl.when(pl.program_id(1) == 0)
        def _():
            ds_s[...] = jnp.zeros_like(ds_s)

        knowns = [ti_ref[c] for c in range(DN_SUB)]
        _, vjp = jax.vjp(lambda *args: _dn_chunks(*args, knowns=knowns)[:2],
                         st_ref[0], q_ref[...], k_ref[...], v_ref[...], g_ref[...], b_ref[...])
        grads = vjp((do_ref[...], ds_s[...]))
        ds_s[...] = grads[0]
        for ref, val in zip((dq, dk, dv, dg, db), grads[1:]):
            ref[...] = val

    spec = pl.BlockSpec((rows, DN_W), lambda b, t: (b * ns + ns - 1 - t, 0))
    return pl.pallas_call(
        body, name=name, grid=(T // S, ns),
        in_specs=[spec] * 5 + [pl.BlockSpec((1, DN_W, DN_W), lambda b, t: (b * ns + ns - 1 - t, 0, 0)),
                               pl.BlockSpec((DN_SUB, DN_W, DN_W), lambda b, t: (b * ns + ns - 1 - t, 0, 0)),
                               spec],
        out_specs=[spec] * 5,
        out_shape=[jax.ShapeDtypeStruct((T, DN_W), F32)] * 5,
        scratch_shapes=[pltpu.VMEM((DN_W, DN_W), F32)],
        compiler_params=_cp("parallel", "arbitrary"),
    )(q, k, v, g, beta, states, tinvs, do)


def _dn_gate(o, z, nl):
    ms = _mmx(o * o, _head_mask((DN_W, DN_W))) * (1.0 / HD)
    return o * lax.rsqrt(ms + EPS) * nl * (z * _sigmoid(z))


def dn_gate_fwd(o, u_dn, nl, name):
    T = o.shape[0]
    tm = min(TOK_TILE, T)

    def body(o_ref, z_ref, n_ref, y_ref):
        y_ref[...] = _dn_gate(o_ref[...], z_ref[...], n_ref[...])

    return pl.pallas_call(
        body, name=name, grid=(T // tm,),
        in_specs=[_row_spec(tm, DN_W), pl.BlockSpec((tm, DN_W), lambda i: (i, 3)), _full_spec((1, DN_W))],
        out_specs=_row_spec(tm, DN_W),
        out_shape=jax.ShapeDtypeStruct((T, DN_W), F32),
        compiler_params=_cp("parallel"),
    )(o, u_dn, nl)


def dn_gate_bwd(o, u_dn, nl, dy, name):
    T = o.shape[0]
    tm = min(TOK_TILE, T)

    def body(o_ref, z_ref, n_ref, dy_ref, do_ref, dz_ref, dn_ref):
        @pl.when(pl.program_id(0) == 0)
        def _():
            dn_ref[...] = jnp.zeros_like(dn_ref)

        _, vjp = jax.vjp(_dn_gate, o_ref[...], z_ref[...], n_ref[...])
        do, dz, dn = vjp(dy_ref[...])
        do_ref[...] = do
        dz_ref[...] = dz
        fold = (_iota((LANE, DN_W), 0) == (_iota((LANE, DN_W), 1) & (HD - 1))).astype(F32)
        dn_ref[...] += _mmx_nt(jnp.concatenate([dn, jnp.zeros((7, DN_W), F32)], axis=0), fold)

    return pl.pallas_call(
        body, name=name, grid=(T // tm,),
        in_specs=[_row_spec(tm, DN_W), pl.BlockSpec((tm, DN_W), lambda i: (i, 3)), _full_spec((1, DN_W)),
                  _row_spec(tm, DN_W)],
        out_specs=[_row_spec(tm, DN_W), _row_spec(tm, DN_W), _full_spec((8, LANE))],
        out_shape=[jax.ShapeDtypeStruct((T, DN_W), F32), jax.ShapeDtypeStruct((T, DN_W), F32),
                   jax.ShapeDtypeStruct((8, LANE), F32)],
        compiler_params=_cp("arbitrary"),
    )(o, u_dn, nl, dy)


Y_SPLITS = (LRU_W, ATT_W, DN_W)
Y_OFFS = (0, LRU_W, LRU_W + ATT_W)


ROWS_DEV = D // N_DEV


def _dev_rows_spec(row0):
    return pl.BlockSpec((N_DEV, ROWS_DEV, D), lambda *_: (0, row0 // ROWS_DEV, 0))


def _dev_rows(w_ref, off, n):
    return w_ref[off // ROWS_DEV:(off + n) // ROWS_DEV].reshape(n, D)


def wout_fwd(h, ys, wb, name):
    T = h.shape[0]
    tm = min(TOK_TILE, T)

    def body(h_ref, y0, y1, y2, w_ref, o_ref, yc_ref):
        acc = h_ref[...]
        for ref, off, n in zip((y0, y1, y2), Y_OFFS, Y_SPLITS):
            y = ref[...].astype(BF16)
            yc_ref[:, off:off + n] = y
            acc += _mm(y, _dev_rows(w_ref, off, n))
        o_ref[...] = acc

    return pl.pallas_call(
        body, name=name, grid=(T // tm,),
        in_specs=[_row_spec(tm, D)] + [_row_spec(tm, n) for n in Y_SPLITS] + [_dev_rows_spec(B_WOUT)],
        out_specs=[_row_spec(tm, D), _row_spec(tm, D)],
        out_shape=[jax.ShapeDtypeStruct((T, D), F32), jax.ShapeDtypeStruct((T, D), BF16)],
        compiler_params=_cp("parallel"),
    )(h, *ys, wb)


def wout_bwd(dy, wb, name):
    T = dy.shape[0]
    tm = min(TOK_TILE, T)

    def body(dy_ref, w_ref, d0, d1, d2):
        dd = dy_ref[...].astype(BF16)
        for ref, off, n in zip((d0, d1, d2), Y_OFFS, Y_SPLITS):
            ref[...] = _mm_nt(dd, _dev_rows(w_ref, off, n))

    return pl.pallas_call(
        body, name=name, grid=(T // tm,),
        in_specs=[_row_spec(tm, D), _dev_rows_spec(B_WOUT)],
        out_specs=[_row_spec(tm, n) for n in Y_SPLITS],
        out_shape=[jax.ShapeDtypeStruct((T, n), F32) for n in Y_SPLITS],
        compiler_params=_cp("parallel"),
    )(dy, wb)


def ple_fwd(h, g, pe, wg, wp, name):
    T = h.shape[0]
    tm = min(TOK_TILE, T)

    def body(h_ref, g_ref, p_ref, wg_ref, wp_ref, o_ref):
        hh = h_ref[...]
        xn = _rms(hh, g_ref[...])[0]
        o_ref[...] = hh + _sigmoid(_mm(xn, _dev_rows(wg_ref, 0, D))) * _mm(p_ref[...], wp_ref[...])

    return pl.pallas_call(
        body, name=name, grid=(T // tm,),
        in_specs=[_row_spec(tm, D), _full_spec((1, D)), _row_spec(tm, PLE), _dev_rows_spec(B_PGATE),
                  _full_spec((PLE, D))],
        out_specs=_row_spec(tm, D),
        out_shape=jax.ShapeDtypeStruct((T, D), F32),
        compiler_params=_cp("parallel"),
    )(h, g, pe, wg, wp)


def ple_bwd(h, dy, g, pe, wg, wp, name):
    T = h.shape[0]
    tm = min(TOK_TILE, T)

    def body(h_ref, dy_ref, g_ref, p_ref, wg_ref, wp_ref, dh_ref, dz_ref, dpp_ref, xn_ref, dn_ref):
        @pl.when(pl.program_id(0) == 0)
        def _():
            dn_ref[...] = jnp.zeros_like(dn_ref)

        gg = g_ref[...]
        dy = dy_ref[...]
        xn, xhat, rstd = _rms(h_ref[...], gg)
        wg = _dev_rows(wg_ref, 0, D)
        gate = _sigmoid(_mm(xn, wg))
        pp = _mm(p_ref[...], wp_ref[...])
        dz = dy * pp * gate * (1.0 - gate)
        dz_ref[...] = dz.astype(BF16)
        dpp_ref[...] = (dy * gate).astype(BF16)
        xn_ref[...] = xn.astype(BF16)
        dh, dn = _rms_bwd(_mm_nt(dz, wg), xhat, rstd, gg)
        dh_ref[...] = dy + dh
        dn_ref[...] += dn

    return pl.pallas_call(
        body, name=name, grid=(T // tm,),
        in_specs=[_row_spec(tm, D), _row_spec(tm, D), _full_spec((1, D)), _row_spec(tm, PLE),
                  _dev_rows_spec(B_PGATE), _full_spec((PLE, D))],
        out_specs=[_row_spec(tm, D), _row_spec(tm, D), _row_spec(tm, D), _row_spec(tm, D), _full_spec((1, D))],
        out_shape=[jax.ShapeDtypeStruct((T, D), F32), jax.ShapeDtypeStruct((T, D), BF16),
                   jax.ShapeDtypeStruct((T, D), BF16), jax.ShapeDtypeStruct((T, D), BF16),
                   jax.ShapeDtypeStruct((1, D), F32)],
        compiler_params=_cp("arbitrary"),
    )(h, dy, g, pe, wg, wp)


def loss_head(h, g, target, name):
    T = h.shape[0]
    tm = min(TOK_TILE, T)

    def body(h_ref, g_ref, t_ref, loss_ref, dh_ref, dn_ref):
        @pl.when(pl.program_id(0) == 0)
        def _():
            dn_ref[...] = jnp.zeros_like(dn_ref)
            loss_ref[...] = jnp.zeros_like(loss_ref)

        gg = g_ref[...]
        y, xhat, rstd = _rms(h_ref[...], gg)
        err = y - t_ref[...]
        per_tok = jnp.mean(err * err, axis=-1, keepdims=True)
        loss_ref[...] += 0.5 * jnp.sum(per_tok, axis=0, keepdims=True)
        dh, dn = _rms_bwd(err * (1.0 / D), xhat, rstd, gg)
        dh_ref[...] = dh
        dn_ref[...] += dn

    return pl.pallas_call(
        body, name=name, grid=(T // tm,),
        in_specs=[_row_spec(tm, D), _full_spec((1, D)), _row_spec(tm, D)],
        out_specs=[_full_spec((8, LANE)), _row_spec(tm, D), _full_spec((1, D))],
        out_shape=[jax.ShapeDtypeStruct((8, LANE), F32), jax.ShapeDtypeStruct((T, D), F32),
                   jax.ShapeDtypeStruct((1, D), F32)],
        compiler_params=_cp("arbitrary"),
    )(h, g, target)


def _block_diag(w):
    return jnp.einsum('hij,hk->hikj', w, jnp.eye(4, dtype=w.dtype)).reshape(LRU_W, LRU_W)


def _layer_consts(W, l):
    row = lambda v: v.reshape(1, -1)
    zeros = jnp.zeros((5, LRU_W), F32)
    return dict(
        wa=_block_diag(W["lru_w_a"][l]), wx=_block_diag(W["lru_w_x"][l]),
        lru_vec=jnp.concatenate([row(W["lru_b_a"][l]), row(W["lru_b_x"][l]), row(W["lru_lambda"][l]), zeros], 0),
        lru_cb=row(W["lru_conv_b"][l]),
        sinks=W["attn_sinks"][l], rel=W["rel_bias"].reshape(-1),
        dn_cb=jnp.zeros((1, 3 * DN_W), F32),
        alog=row(jnp.repeat(W["dn_a_log"][l], HD)), dtb=row(jnp.repeat(W["dn_dt_bias"][l], HD)),
        dn_nl=row(jnp.tile(W["dn_norm"][l], DN_H)),
    )


def _layer_fwd(h0, pe, W, l, S):
    n = f"l{l}_"
    c_ = _layer_consts(W, l)
    row = lambda v: v.reshape(1, -1)
    wa, wb = W["wa"][l], W["wb"][l]
    h1, *ffn1_kept = ffn_fwd(h0, row(W["ffn1_norm"][l]), wa, wb, 0, n + "ffn1_fwd")
    u_lru, u_att, u_dn, u_ba, xn_mix = mixin_fwd(h1, row(W["mix_norm"][l]), W["w_in"][l], n + "mixin_fwd")
    xr = conv_fwd(u_lru, W["lru_conv_w"][l], c_["lru_cb"], S, 0, LRU_W, n + "lru_conv_fwd")
    y_lru = lru_fwd(xr, u_lru, c_["wa"], c_["wx"], c_["lru_vec"], S, n + "lru_fwd")
    y_att = attn_fwd(u_att, c_["sinks"], c_["rel"], S, n + "attn_fwd")
    cc = conv_fwd(u_dn, W["dn_conv_w"][l], c_["dn_cb"], S, 0, 3 * DN_W, n + "dn_conv_fwd")
    q, k, v, g, beta = dn_point_fwd(cc, u_ba, c_["alog"], c_["dtb"], n + "dn_point_fwd")
    o, states, tinvs = dn_scan_fwd(q, k, v, g, beta, S, n + "dn_scan_fwd")
    y_dn = dn_gate_fwd(o, u_dn, c_["dn_nl"], n + "dn_gate_fwd")
    h2, ycat = wout_fwd(h1, (y_lru, y_att, y_dn), wb, n + "wout_fwd")
    h3, *ffn2_kept = ffn_fwd(h2, row(W["ffn2_norm"][l]), wa, wb, 1, n + "ffn2_fwd")
    h4 = ple_fwd(h3, row(W["ple_norm"][l]), pe, wb, W["ple_w_proj"][l], n + "ple_fwd")
    saved = dict(ffn1=ffn1_kept, ffn2=ffn2_kept, h0=h0, h1=h1, h2=h2, h3=h3, u_lru=u_lru, u_att=u_att, u_dn=u_dn, u_ba=u_ba, xn_mix=xn_mix,
                 xr=xr, cc=cc, q=q, k=k, v=v, g=g, beta=beta, o=o, states=states, tinvs=tinvs, ycat=ycat)
    return h4, saved


GM_WOUT, GM_PGATE, GM_WIN, GM_PPROJ, GM_END, GM_ROWS = 0, 128, 256, 560, 592, 640


def _layer_bwd(dh4, sv, pe, W, l, S, token=None, on_piece=None):
    n = f"l{l}_"
    c_ = _layer_consts(W, l)
    row = lambda v: v.reshape(1, -1)
    behind = lambda v, tok: v if tok is None else v + tok
    on_piece = on_piece or (lambda *_: None)
    wa, wb = W["wa"][l], W["wb"][l]
    G = {"mix_rows": jnp.zeros((N_DEV, GM_ROWS, D), BF16)}
    dh3, dz, dpp, xn_p, dn = ple_bwd(sv["h3"], dh4, behind(row(W["ple_norm"][l]), token), pe, wb,
                                     W["ple_w_proj"][l], n + "ple_bwd")
    G["ple_norm"] = dn[0]
    G["mix_rows"] = grad_rows(xn_p, dz, G["mix_rows"], GM_PGATE, ROWS_DEV, n + "d_ple_w_gate")
    d_proj = matmul_tn(pe, dpp, n + "d_ple_w_proj")
    d_proj = d_proj.reshape(PLE, N_DEV, D // N_DEV).transpose(1, 0, 2).reshape(N_DEV, GM_END - GM_PPROJ, D)
    G["mix_rows"] = lax.dynamic_update_slice(G["mix_rows"], d_proj, (0, GM_PPROJ, 0))

    def ffn_back(which, fidx, h_in, dy, tok=None):
        gt, up, xn = sv[which]
        dh, dgt, dup, act, dn_ = ffn_bwd(h_in, dy, behind(row(W[which + "_norm"][l]), tok), gt, up, wa, wb, fidx,
                                         n + which + "_bwd")
        G[which + "_norm"] = dn_[0]
        cols = grad_cols(xn, dgt, jnp.zeros((2, D, FFP), BF16), 0, n + "d_" + which + "_w_gate")
        G[which + "_cols"] = grad_cols(xn, dup, cols, 1, n + "d_" + which + "_w_up")
        G[which + "_rows"] = grad_rows(act, dy, jnp.zeros((N_DEV, SHP, D), BF16), 0, SHP,
                                       n + "d_" + which + "_w_down", scale=0.5)
        return dh, on_piece(l, which, (G[which + "_cols"], G[which + "_rows"]))

    dh2, tok = ffn_back("ffn2", 1, sv["h2"], dh3)
    dy_lru, dy_att, dy_dn = wout_bwd(dh2, wb, n + "wout_bwd")
    G["mix_rows"] = grad_rows(sv["ycat"], dh2, G["mix_rows"], GM_WOUT, ROWS_DEV, n + "d_w_out")
    do, dz_dn, dnn = dn_gate_bwd(sv["o"], sv["u_dn"], behind(c_["dn_nl"], tok), dy_dn, n + "dn_gate_bwd")
    dqkvgb = dn_scan_bwd(sv["q"], sv["k"], sv["v"], sv["g"], sv["beta"], sv["states"], sv["tinvs"], do, S,
                         n + "dn_scan_bwd")
    dcc, du_ba, dvec_dn = dn_point_bwd(sv["cc"], sv["u_ba"], c_["alog"], c_["dtb"], dqkvgb, n + "dn_point_bwd")
    dqkv, dwb_dn = conv_bwd(sv["u_dn"], dcc, W["dn_conv_w"][l], S, 0, 3 * DN_W, n + "dn_conv_bwd")
    du_dn = jnp.concatenate([dqkv, dz_dn], axis=1)
    G["dn_norm"] = dnn[0, 0:HD]
    G["dn_a_log"] = dvec_dn[0, 0:DN_H]
    G["dn_dt_bias"] = dvec_dn[1, 0:DN_H]
    G["dn_conv_w"] = dwb_dn[0:4]
    du_att, drel, dsk = attn_bwd(sv["u_att"], dy_att, c_["sinks"], c_["rel"], S, n + "attn_bwd")
    G["attn_sinks"] = dsk[:, 0]
    G["rel_bias"] = drel[:, 0:REL_BUCKETS].T
    dxr, dgt_lru, dwa, dwx, dvec = lru_bwd(sv["xr"], sv["u_lru"], dy_lru, c_["wa"], c_["wx"], c_["lru_vec"], S,
                                           n + "lru_bwd")
    dx_lru, dwb_lru = conv_bwd(sv["u_lru"], dxr, W["lru_conv_w"][l], S, 0, LRU_W, n + "lru_conv_bwd")
    du_lru = jnp.concatenate([dx_lru, dgt_lru], axis=1)
    diag = lambda m: jnp.stack([m[c, HD * e:HD * (e + 1), HD * e:HD * (e + 1)] for c in range(2) for e in range(2)])
    G["lru_w_a"], G["lru_w_x"] = diag(dwa), diag(dwx)
    G["lru_b_a"], G["lru_b_x"], G["lru_lambda"] = dvec[0], dvec[1], dvec[2]
    G["lru_conv_w"], G["lru_conv_b"] = dwb_lru[0:4], dwb_lru[4]
    dh1, du_cat, dn = mixin_bwd(sv["h1"], dh2, row(W["mix_norm"][l]), W["w_in"][l],
                                (du_lru, du_att, du_dn, du_ba), n + "mixin_bwd")
    G["mix_norm"] = dn[0]
    d_in = matmul_tn(sv["xn_mix"], du_cat, n + "d_w_in")[:, :D_IN]
    d_in = d_in.reshape(D, N_DEV, D_IN // N_DEV).transpose(1, 0, 2).reshape(N_DEV, WIN_ROWS, D)
    d_in = jnp.pad(d_in, ((0, 0), (0, GM_PPROJ - GM_WIN - WIN_ROWS), (0, 0)))
    G["mix_rows"] = lax.dynamic_update_slice(G["mix_rows"], d_in, (0, GM_WIN, 0))
    tok = on_piece(l, "mix", (G["mix_rows"],))
    dh0, tok = ffn_back("ffn1", 0, sv["h0"], dh1, tok)
    return dh0, G, tok


def _core(x, pe, W, target, S, weights_at=None, on_piece=None):
    weights_at = weights_at or (lambda l, h: W)
    h = x
    saved = []
    for l in range(DEPTH):
        h, sv = _layer_fwd(h, pe[l], weights_at(l, h), l, S)
        saved.append(sv)
    loss_tile, dh, dfn = loss_head(h, W["final_norm"].reshape(1, -1), target, "loss_head")
    grads = [None] * DEPTH
    token = None
    for l in reversed(range(DEPTH)):
        dh, grads[l], token = _layer_bwd(dh, saved[l], pe[l], W, l, S, token, on_piece)
    return loss_tile[0, 0], dh, grads, dfn[0]


MESH_ID = pl.DeviceIdType.MESH
ANY_SPEC = pl.BlockSpec(memory_space=pl.ANY)
AXES = ("x", "y", "c")


def _my_pos():
    return lax.axis_index("x"), lax.axis_index("y"), lax.axis_index("c")


def _slot_of(px, py, pc):
    return 4 * px + 2 * py + pc


def all_gather(x, name):
    R, C = x.shape

    def body(x_ref, out_ref, send_sems, recv_sems, local_sem):
        mx, my, mc = _my_pos()
        me, sibling = (mx, my, mc), (mx, my, 1 - mc)
        chips = [(1 - mx, my), (mx, 1 - my), (1 - mx, 1 - my)]

        def copy(k, block, to, src=None):
            dst = out_ref.at[_slot_of(*block)]
            return pltpu.make_async_remote_copy(
                src_ref=dst if src is None else src, dst_ref=dst,
                send_sem=send_sems.at[k], recv_sem=recv_sems.at[k],
                device_id=to, device_id_type=MESH_ID)

        mine = pltpu.make_async_copy(x_ref, out_ref.at[_slot_of(*me)], local_sem)
        mine.start()
        first = [copy(0, me, sibling, src=x_ref)]
        first += [copy(1 + j, me, (*chip, mc), src=x_ref) for j, chip in enumerate(chips)]
        for cp in first:
            cp.start()
        passed = [copy(4 + j, (*chip, mc), sibling) for j, chip in enumerate(chips)]
        for j, chip in enumerate(chips):
            copy(1 + j, (*chip, mc), me).wait_recv()
            passed[j].start()
        copy(0, sibling, me).wait_recv()
        for j, chip in enumerate(chips):
            copy(4 + j, (*chip, 1 - mc), me).wait_recv()
        for cp in first + passed:
            cp.wait_send()
        mine.wait()

    return pl.pallas_call(
        body, name=name,
        out_shape=jax.ShapeDtypeStruct((N_DEV, R, C), x.dtype),
        in_specs=[ANY_SPEC], out_specs=ANY_SPEC,
        scratch_shapes=[pltpu.SemaphoreType.DMA((7,)), pltpu.SemaphoreType.DMA((7,)), pltpu.SemaphoreType.DMA],
    )(x)


def _col_window(ref, slot):
    return ref.at[:, pl.ds(pl.multiple_of(slot * SHP, LANE), SHP)]


def gather_layer(a_sh, b_sh, name):
    def body(a_ref, b_ref, ao_ref, bo_ref, send_sems, recv_sems, local_sems):
        mx, my, mc = _my_pos()
        me, sibling = (mx, my, mc), (mx, my, 1 - mc)
        chips = [(1 - mx, my), (mx, 1 - my), (1 - mx, 1 - my)]

        def copies(k, block, to, own=False):
            slot = _slot_of(*block)
            dsts = (_col_window(ao_ref, slot), bo_ref.at[slot])
            srcs = (a_ref, b_ref) if own else dsts
            return [pltpu.make_async_remote_copy(
                src_ref=s, dst_ref=d, send_sem=send_sems.at[2 * k + i], recv_sem=recv_sems.at[2 * k + i],
                device_id=to, device_id_type=MESH_ID) for i, (s, d) in enumerate(zip(srcs, dsts))]

        mine = [pltpu.make_async_copy(a_ref, _col_window(ao_ref, _slot_of(*me)), local_sems.at[0]),
                pltpu.make_async_copy(b_ref, bo_ref.at[_slot_of(*me)], local_sems.at[1])]
        for cp in mine:
            cp.start()
        first = copies(0, me, sibling, own=True)
        for j, chip in enumerate(chips):
            first += copies(1 + j, me, (*chip, mc), own=True)
        for cp in first:
            cp.start()
        passed = []
        for j, chip in enumerate(chips):
            for cp in copies(1 + j, (*chip, mc), me):
                cp.wait_recv()
            fwd = copies(4 + j, (*chip, mc), sibling)
            for cp in fwd:
                cp.start()
            passed += fwd
        for cp in copies(0, sibling, me):
            cp.wait_recv()
        for j, chip in enumerate(chips):
            for cp in copies(4 + j, (*chip, 1 - mc), me):
                cp.wait_recv()
        for cp in first + passed:
            cp.wait_send()
        for cp in mine:
            cp.wait()

    return pl.pallas_call(
        body, name=name,
        out_shape=[jax.ShapeDtypeStruct((a_sh.shape[0], FFP), a_sh.dtype),
                   jax.ShapeDtypeStruct((N_DEV,) + b_sh.shape, b_sh.dtype)],
        in_specs=[ANY_SPEC, ANY_SPEC], out_specs=[ANY_SPEC, ANY_SPEC],
        scratch_shapes=[pltpu.SemaphoreType.DMA((14,)), pltpu.SemaphoreType.DMA((14,)),
                        pltpu.SemaphoreType.DMA((2,))],
    )(a_sh, b_sh)


HBM_SPEC = pl.BlockSpec(memory_space=pltpu.HBM)
SEM_SPEC = pl.BlockSpec(memory_space=pltpu.SEMAPHORE)
SPLIT_EFFECT = pltpu.CompilerParams(has_side_effects=pltpu.SideEffectType.DATAFLOW_SIDE_EFFECTING)


def _split_ends(mode, src_ref, dst_ref, src_slot, dst_slot):
    cols = mode.endswith("cols")
    if mode.startswith("gather"):
        return src_ref, (_col_window(dst_ref, dst_slot) if cols else dst_ref.at[dst_slot])
    return (_col_window(src_ref, src_slot) if cols else src_ref.at[src_slot]), dst_ref.at[dst_slot]


def _split_peers():
    mx, my, mc = _my_pos()
    for r in range(1, N_DEV):
        peer = (1 - mx if r & 4 else mx, 1 - my if r & 2 else my, 1 - mc if r & 1 else mc)
        yield r - 1, peer, _slot_of(*peer)


def split_start(modes, srcs, dsts, name):
    n = len(modes)

    def body(*refs):
        send_sems, recv_sems, token = refs[2 * n], refs[2 * n + 1], refs[-1]
        mine = _slot_of(*_my_pos())
        for k, peer, ps in _split_peers():
            for i in range(n):
                src, dst = _split_ends(modes[i], refs[i], refs[n + i], ps, mine)
                pltpu.make_async_remote_copy(
                    src_ref=src, dst_ref=dst, send_sem=send_sems.at[n * k + i], recv_sem=recv_sems.at[n * k + i],
                    device_id=peer, device_id_type=MESH_ID).start()
        for i in range(n):
            src, dst = _split_ends(modes[i], refs[i], refs[n + i], mine, mine)
            pltpu.make_async_copy(src, dst, recv_sems.at[n * (N_DEV - 1) + i]).start()
        token[...] = jnp.zeros_like(token)

    bufs = tuple(srcs) + tuple(dsts)
    sems = pltpu.SemaphoreType.DMA((n * N_DEV,))
    res = pl.pallas_call(
        body, name=name,
        out_shape=(sems, sems) + tuple(pltpu.HBM(t.shape, t.dtype) for t in bufs)
        + (jax.ShapeDtypeStruct((8, LANE), F32),),
        in_specs=[HBM_SPEC] * (2 * n),
        out_specs=(SEM_SPEC, SEM_SPEC) + (HBM_SPEC,) * (2 * n) + (pl.BlockSpec(memory_space=pltpu.VMEM),),
        input_output_aliases={i: 2 + i for i in range(2 * n)},
        compiler_params=SPLIT_EFFECT,
    )(*(pltpu.with_memory_space_constraint(t, pltpu.HBM) for t in bufs))
    return list(res[:-1]), res[-1][0, 0]


def split_wait(modes, started, after, name):
    n = len(modes)
    send_sems, recv_sems, bufs = started[0], started[1], started[2:]

    def body(*refs):
        send_sems, recv_sems = refs[2 * n], refs[2 * n + 1]
        mine = _slot_of(*_my_pos())
        for k, peer, ps in _split_peers():
            for i in range(n):
                sent = _split_ends(modes[i], refs[i], refs[n + i], ps, mine)[0]
                landed = _split_ends(modes[i], refs[i], refs[n + i], mine, ps)[1]
                cp = pltpu.make_async_remote_copy(
                    src_ref=sent, dst_ref=landed, send_sem=send_sems.at[n * k + i],
                    recv_sem=recv_sems.at[n * k + i], device_id=peer, device_id_type=MESH_ID)
                cp.wait_send()
                cp.wait_recv()
        for i in range(n):
            src, dst = _split_ends(modes[i], refs[i], refs[n + i], mine, mine)
            pltpu.make_async_copy(src, dst, recv_sems.at[n * (N_DEV - 1) + i]).wait()

    res = pl.pallas_call(
        body, name=name,
        out_shape=tuple(pltpu.HBM(t.shape, t.dtype) for t in bufs),
        in_specs=[HBM_SPEC] * (2 * n) + [SEM_SPEC, SEM_SPEC, pl.BlockSpec(memory_space=pl.ANY)],
        out_specs=(HBM_SPEC,) * (2 * n),
        input_output_aliases={i: i for i in range(2 * n)},
        compiler_params=SPLIT_EFFECT,
    )(*bufs, send_sems, recv_sems, after)
    return list(res[n:])


def sum_parts(parts, name):
    _, R, C = parts.shape
    tr = _pick(R, (512, 336, 272, 256, 128, 64, 32, 16, 8))

    def body(p_ref, o_ref):
        acc = p_ref[0].astype(F32)
        for k in range(1, N_DEV):
            acc += p_ref[k].astype(F32)
        o_ref[...] = acc

    return pl.pallas_call(
        body, name=name, grid=(R // tr,),
        in_specs=[pl.BlockSpec((N_DEV, tr, C), lambda i: (0, i, 0))],
        out_specs=pl.BlockSpec((tr, C), lambda i: (i, 0)),
        out_shape=jax.ShapeDtypeStruct((R, C), F32),
        compiler_params=_cp("parallel"),
    )(parts)


def adamw(g, w, m, v, name):
    R, C = g.shape
    tr = _pick(R, (512, 352, 256, 128, 64, 32, 16, 8))
    c1 = 1.0 - ADAM_B1 ** ADAM_STEP
    c2 = 1.0 - ADAM_B2 ** ADAM_STEP

    def body(g_ref, w_ref, m_ref, v_ref, d_ref, nm_ref, nv_ref):
        gg = g_ref[...]
        mm = ADAM_B1 * m_ref[...] + (1.0 - ADAM_B1) * gg
        vv = ADAM_B2 * v_ref[...] + (1.0 - ADAM_B2) * (gg * gg)
        nm_ref[...] = mm
        nv_ref[...] = vv
        d_ref[...] = -ADAM_LR * ((mm / c1) / (jnp.sqrt(vv / c2) + ADAM_EPS) + ADAM_WD * w_ref[...])

    spec = pl.BlockSpec((tr, C), lambda i: (i, 0))
    return pl.pallas_call(
        body, name=name, grid=(R // tr,),
        in_specs=[spec] * 4, out_specs=[spec] * 3,
        out_shape=[jax.ShapeDtypeStruct((R, C), F32)] * 3,
        compiler_params=_cp("parallel"),
    )(g, w, m, v)


BIG = (("ffn1_w_gate", 1, D, FF), ("ffn1_w_up", 1, D, FF), ("ffn1_w_down", 0, FF, D),
       ("w_in", 1, D, D_IN), ("w_out", 0, D, D),
       ("ffn2_w_gate", 1, D, FF), ("ffn2_w_up", 1, D, FF), ("ffn2_w_down", 0, FF, D),
       ("ple_w_gate", 0, D, D), ("ple_w_proj", 1, PLE, D))
SMALL = (("ffn1_norm", (D,), None), ("mix_norm", (D,), None), ("lru_conv_w", (4, LRU_W), LRU_W // N_DEV),
         ("lru_conv_b", (LRU_W,), None), ("lru_w_a", (4, HD, HD), None), ("lru_b_a", (LRU_W,), None),
         ("lru_w_x", (4, HD, HD), None), ("lru_b_x", (LRU_W,), None), ("lru_lambda", (LRU_W,), None),
         ("attn_sinks", (ATT_H,), None), ("dn_conv_w", (4, 3 * DN_W), 3 * DN_W // N_DEV),
         ("dn_a_log", (DN_H,), None), ("dn_dt_bias", (DN_H,), None), ("dn_norm", (HD,), None),
         ("ffn2_norm", (D,), None), ("ple_norm", (D,), None))
SINGLE = (("rel_bias", (REL_BUCKETS, ATT_H)), ("final_norm", (D,)))


def _pack_rows(arrs, width, mult):
    flat = jnp.concatenate([a.reshape(-1) for a in arrs])
    rows = -(-flat.shape[0] // (width * mult)) * mult
    return jnp.pad(flat, (0, rows * width - flat.shape[0])).reshape(rows, width)


def _unpack_rows(packed, shapes):
    flat = packed.reshape(-1)
    out, off = [], 0
    for s in shapes:
        n = int(np.prod(s))
        out.append(flat[off:off + n].reshape(s))
        off += n
    return out


COL_NAMES = ("ffn1_w_gate", "ffn1_w_up", "ffn2_w_gate", "ffn2_w_up")


def _shard_cols(a, l):
    blk = jnp.concatenate([a[n][l] for n in COL_NAMES], axis=0)
    return jnp.pad(blk, ((0, 0), (0, SHP - SH))).astype(BF16)


def _shard_rows(a, l):
    to = lambda w, r: jnp.pad(w, ((0, r - w.shape[0]), (0, 0)))
    parts = [to(a["ffn1_w_down"][l], SHP), to(a["ffn2_w_down"][l], SHP), a["w_out"][l], a["ple_w_gate"][l],
             to(a["w_in"][l].reshape(WIN_ROWS, D), B_PPROJ - B_WIN), a["ple_w_proj"][l].reshape(-1, D)]
    return jnp.concatenate(parts, axis=0).astype(BF16)


def _full_w_in(wb):
    sh = wb[:, B_WIN:B_WIN + WIN_ROWS, :].reshape(N_DEV, D, D_IN // N_DEV)
    return jnp.pad(sh.transpose(1, 0, 2).reshape(D, D_IN), ((0, 0), (0, D_IN_PAD - D_IN)))


def _full_ple_proj(wb):
    sh = wb[:, B_PPROJ:B_ROWS, :].reshape(N_DEV, PLE, D // N_DEV)
    return sh.transpose(1, 0, 2).reshape(PLE, D)


PIECE_NAMES = {"ffn1": ("ffn1_w_gate", "ffn1_w_up", "ffn1_w_down"), "ffn2": ("ffn2_w_gate", "ffn2_w_up", "ffn2_w_down"),
               "mix": ("w_out", "ple_w_gate", "w_in", "ple_w_proj")}


def _shard_grads(piece, summed):
    if piece == "mix":
        rows, = summed
        return {"w_out": rows[GM_WOUT:GM_WOUT + ROWS_DEV], "ple_w_gate": rows[GM_PGATE:GM_PGATE + ROWS_DEV],
                "w_in": rows[GM_WIN:GM_WIN + WIN_ROWS].reshape(D, D_IN // N_DEV),
                "ple_w_proj": rows[GM_PPROJ:GM_END].reshape(PLE, D // N_DEV)}
    cols, rows = summed
    return {piece + "_w_gate": cols[:D, :SH], piece + "_w_up": cols[D:, :SH], piece + "_w_down": rows[:SH]}


def kernel(x, p, ffn1_norm, ffn1_w_gate, ffn1_w_up, ffn1_w_down, mix_norm, w_in, lru_conv_w, lru_conv_b, lru_w_a, lru_b_a, lru_w_x, lru_b_x, lru_lambda, attn_sinks, rel_bias, dn_conv_w, dn_a_log, dn_dt_bias, dn_norm, w_out, ffn2_norm, ffn2_w_gate, ffn2_w_up, ffn2_w_down, ple_norm, ple_w_gate, ple_w_proj, final_norm, loss_target, m_ffn1_norm, m_ffn1_w_gate, m_ffn1_w_up, m_ffn1_w_down, m_mix_norm, m_w_in, m_lru_conv_w, m_lru_conv_b, m_lru_w_a, m_lru_b_a, m_lru_w_x, m_lru_b_x, m_lru_lambda, m_attn_sinks, m_rel_bias, m_dn_conv_w, m_dn_a_log, m_dn_dt_bias, m_dn_norm, m_w_out, m_ffn2_norm, m_ffn2_w_gate, m_ffn2_w_up, m_ffn2_w_down, m_ple_norm, m_ple_w_gate, m_ple_w_proj, m_final_norm, v_ffn1_norm, v_ffn1_w_gate, v_ffn1_w_up, v_ffn1_w_down, v_mix_norm, v_w_in, v_lru_conv_w, v_lru_conv_b, v_lru_w_a, v_lru_b_a, v_lru_w_x, v_lru_b_x, v_lru_lambda, v_attn_sinks, v_rel_bias, v_dn_conv_w, v_dn_a_log, v_dn_dt_bias, v_dn_norm, v_w_out, v_ffn2_norm, v_ffn2_w_gate, v_ffn2_w_up, v_ffn2_w_down, v_ple_norm, v_ple_w_gate, v_ple_w_proj, v_final_norm):
    a = dict(locals())
    nb, S, _ = x.shape
    T = nb * S
    my_slot = _slot_of(*_my_pos())

    W = {"wa": [None] * DEPTH, "wb": [None] * DEPTH, "w_in": [None] * DEPTH, "ple_w_proj": [None] * DEPTH}

    def set_layer_weights(l, wa, wb):
        W["wa"][l], W["wb"][l] = wa, wb
        W["w_in"][l], W["ple_w_proj"][l] = _full_w_in(wb), _full_ple_proj(wb)

    def landing(mode, src):
        if mode == "gather_cols":
            return lax.empty((src.shape[0], FFP), src.dtype)
        if mode == "scatter_cols":
            return lax.empty((N_DEV, src.shape[0], SHP), src.dtype)
        return lax.empty((N_DEV,) + src.shape[mode == "scatter_block":], src.dtype)

    def start(modes, srcs, name):
        return split_start(modes, srcs, [landing(m, s) for m, s in zip(modes, srcs)], name)

    set_layer_weights(0, *gather_layer(_shard_cols(a, 0), _shard_rows(a, 0), "gather_weights_l0"))
    taps = all_gather(_pack_rows([lru_conv_w, dn_conv_w], LANE, 8), "gather_conv_taps")
    tap_shapes = [lru_conv_w.shape, dn_conv_w.shape]
    lcw, dcw = zip(*[_unpack_rows(taps[k], tap_shapes) for k in range(N_DEV)])
    W["lru_conv_w"] = jnp.concatenate(lcw, axis=-1)
    W["dn_conv_w"] = jnp.concatenate(dcw, axis=-1)
    for name, _, cols in SMALL:
        if cols is None:
            W[name] = a[name]
    W["rel_bias"], W["final_norm"] = rel_bias, final_norm

    GATHER, SCATTER = ("gather_cols", "gather_block"), ("scatter_cols", "scatter_block")
    cols1, rows1, _, _ = lax.optimization_barrier((_shard_cols(a, 1), _shard_rows(a, 1), W["wb"][0], taps))
    gather1, token = start(GATHER, (cols1, rows1), "gather_start_l1")
    W["ffn1_norm"] = ffn1_norm + token
    flight = {}

    def weights_at(l, h):
        if l == 1:
            set_layer_weights(1, *split_wait(GATHER, gather1, h, "gather_wait_l1"))
        return W

    def piece_modes(piece):
        return SCATTER[1:] if piece == "mix" else SCATTER

    def on_piece(l, piece, bufs):
        bufs = bufs if piece == "mix" else (bufs[0].reshape(2 * D, FFP), bufs[1])
        flight[l, piece], token = start(piece_modes(piece), bufs, f"exchange_start_l{l}_{piece}")
        return token

    loss_local, dx, grads, d_final = _core(x.reshape(T, D), p.reshape(DEPTH, T, PLE), W,
                                           loss_target.reshape(T, D), S, weights_at, on_piece)
    loss = lax.psum(loss_local, AXES)

    small_full = [jnp.stack([grads[l][name] for l in range(DEPTH)]) for name, _, _ in SMALL]
    small_full += [grads[0]["rel_bias"] + grads[1]["rel_bias"], d_final]
    small_flight, _ = start(("gather_block",), (_pack_rows(small_full, LANE, 8),), "gather_start_small_grads")

    out = {}

    shards = {}

    def land(l, piece, after):
        parts = split_wait(piece_modes(piece), flight[l, piece], after, f"exchange_wait_l{l}_{piece}")
        shards[l, piece] = _shard_grads(piece, [sum_parts(t, f"sum_grads_l{l}_{piece}_{i}")
                                                for i, t in enumerate(parts)])

    def update(piece):
        for name in PIECE_NAMES[piece]:
            g = jnp.stack([shards[l, piece][name] for l in range(DEPTH)])
            shape = a[name].shape
            two_d = lambda t: t.reshape(-1, shape[-1])
            res = adamw(two_d(g), two_d(a[name]), two_d(a["m_" + name]), two_d(a["v_" + name]), "adamw_" + name)
            out[name] = (g,) + tuple(r.reshape(shape) for r in res)

    for piece in ("ffn2", "mix", "ffn1"):
        land(1, piece, dx)
    summed1 = lax.optimization_barrier(tuple(shards[1, piece][PIECE_NAMES[piece][0]] for piece in PIECE_NAMES))
    land(0, "ffn2", summed1[0])
    land(0, "mix", summed1[1])
    update("ffn2")
    update("mix")
    done_early = lax.optimization_barrier(tuple(out[n][1] for n in PIECE_NAMES["ffn2"] + PIECE_NAMES["mix"]))
    small_parts, = split_wait(("gather_block",), small_flight, done_early[0], "gather_wait_small_grads")
    small_sum = sum_parts(small_parts, "sum_small_grads")
    g_small = dict(zip([n for n, _, _ in SMALL] + [n for n, _ in SINGLE],
                       _unpack_rows(small_sum, [s.shape for s in small_full])))
    for name, _, cols in SMALL:
        if cols is not None:
            g_small[name] = lax.dynamic_slice_in_dim(g_small[name], my_slot * cols, cols, axis=2)

    small_names = [n for n, _, _ in SMALL] + [n for n, _ in SINGLE]
    shapes = [a[n].shape for n in small_names]
    packed = [_pack_rows([a[pre + n] if pre is not None else g_small[n] for n in small_names], LANE, 8)
              for pre in (None, "", "m_", "v_")]
    res = adamw(*packed, "adamw_small")
    unpacked = [_unpack_rows(r, shapes) for r in res]
    for i, n in enumerate(small_names):
        out[n] = (g_small[n].reshape(shapes[i]),) + tuple(u[i] for u in unpacked)

    land(0, "ffn1", lax.optimization_barrier((res[0], done_early[1]))[0])
    update("ffn1")

    order = ['ffn1_norm', 'ffn1_w_gate', 'ffn1_w_up', 'ffn1_w_down', 'mix_norm', 'w_in', 'lru_conv_w', 'lru_conv_b',
             'lru_w_a', 'lru_b_a', 'lru_w_x', 'lru_b_x', 'lru_lambda', 'attn_sinks', 'rel_bias', 'dn_conv_w',
             'dn_a_log', 'dn_dt_bias', 'dn_norm', 'w_out', 'ffn2_norm', 'ffn2_w_gate', 'ffn2_w_up', 'ffn2_w_down',
             'ple_norm', 'ple_w_gate', 'ple_w_proj', 'final_norm']
    return (loss, dx.reshape(x.shape)) + tuple(out[n][k] for k in range(4) for n in order)
```

```python
import functools
import math

import numpy as np
import jax
import jax.numpy as jnp
from jax import lax
from jax.experimental import pallas as pl
from jax.experimental.pallas import tpu as pltpu

F32 = jnp.float32
BF16 = jnp.bfloat16
HI = lax.Precision.HIGHEST

D = 1024
DEPTH = 2
EPS = 1e-6
PLE = 256
FF = 2816
HD = 64
LRU_W = 256
LRU_C = 8.0
ATT_W = 512
ATT_H = 8
ATT_KV = 2
ATT_G = 4
KV_W = 128
WINDOW = 128
BQ = 128
REL_BUCKETS = 32
REL_MAX_DIST = 128
DN_W = 256
DN_H = 4
CHUNK = 64
D_IN = 2312
D_IN_PAD = 2432
N_DEV = 8

ADAM_LR = 0.001
ADAM_B1 = 0.9
ADAM_B2 = 0.999
ADAM_EPS = 1e-08
ADAM_WD = 0.01
ADAM_STEP = 10

LANE = 128
VMEM_LIMIT = 56 * 1024 * 1024
SH = FF // N_DEV
SHP = 384
FFP = N_DEV * SHP
FF_TILE = 2 * SHP
TOK_TILE = 512
R_DOWN2, R_WOUT, R_PGATE, R_PPROJ, R_ROWS = 0, 384, 512, 640, 672
WIN_ROWS = D * D_IN // N_DEV // 1024
IN_ROWS = 304
NEG = -1e30


def _cp(*sem):
    return pltpu.CompilerParams(dimension_semantics=tuple(sem), vmem_limit_bytes=VMEM_LIMIT)


def _dg(a, b, ca, cb, exact):
    dims = (((ca,), (cb,)), ((), ()))
    if exact == "f32":
        return lax.dot_general(a.astype(F32), b.astype(F32), dims, precision=HI, preferred_element_type=F32)
    if exact == "split":
        a_hi, b_hi = a.astype(BF16), b.astype(BF16)
        a_lo = (a - a_hi.astype(F32)).astype(BF16)
        b_lo = (b - b_hi.astype(F32)).astype(BF16)
        dot = lambda u, v: lax.dot_general(u, v, dims, preferred_element_type=F32)
        return dot(a_hi, b_hi) + (dot(a_hi, b_lo) + dot(a_lo, b_hi))
    return lax.dot_general(a.astype(BF16), b.astype(BF16), dims, preferred_element_type=F32)


def _make_mm(exact):
    @jax.custom_vjp
    def mm(a, b):
        return _dg(a, b, 1, 0, exact)

    @jax.custom_vjp
    def mm_nt(a, b):
        return _dg(a, b, 1, 1, exact)

    @jax.custom_vjp
    def mm_tn(a, b):
        return _dg(a, b, 0, 0, exact)

    mm.defvjp(lambda a, b: (mm(a, b), (a, b)),
              lambda r, d: (mm_nt(d, r[1]), mm_tn(r[0], d)))
    mm_nt.defvjp(lambda a, b: (mm_nt(a, b), (a, b)),
                 lambda r, d: (mm(d, r[1]), mm_tn(d, r[0])))
    mm_tn.defvjp(lambda a, b: (mm_tn(a, b), (a, b)),
                 lambda r, d: (mm_nt(r[1], d), mm(r[0], d)))
    return mm, mm_nt, mm_tn


_mm, _mm_nt, _mm_tn = _make_mm("bf16")
_mmx, _mmx_nt, _mmx_tn = _make_mm("f32")
_mm3, _mm3_nt, _mm3_tn = _make_mm("split")


def _iota(shape, dim):
    return lax.broadcasted_iota(jnp.int32, shape, dim)


def _sigmoid(x):
    return 1.0 / (1.0 + jnp.exp(-x))


def _rms(h, g):
    rstd = lax.rsqrt(jnp.mean(h * h, axis=-1, keepdims=True) + EPS)
    xhat = h * rstd
    return xhat * g, xhat, rstd


def _rms_bwd(dxn, xhat, rstd, g):
    dxhat = dxn * g
    dh = rstd * (dxhat - xhat * jnp.mean(dxhat * xhat, axis=-1, keepdims=True))
    dg = jnp.sum(dxn * xhat, axis=0, keepdims=True)
    return dh, dg


def _row_spec(tm, n):
    return pl.BlockSpec((tm, n), lambda i, *_: (i, 0))


def _full_spec(shape):
    nd = len(shape)
    return pl.BlockSpec(shape, lambda *_: (0,) * nd)


def _ffn_weight_specs():
    return [pl.BlockSpec((D, FF_TILE), lambda i, j: (0, j)),
            pl.BlockSpec((D, FF_TILE), lambda i, j: (1, j)),
            pl.BlockSpec((2, SHP, D), lambda i, j: (j, 0, 0))]


def ffn_fwd(h, g, wa, wb, name):
    T = h.shape[0]
    tm = min(TOK_TILE, T)
    nj = FFP // FF_TILE

    def body(h_ref, g_ref, wg_ref, wu_ref, wd_ref, o_ref, gt_ref, up_ref, xn_ref):
        j = pl.program_id(1)

        @pl.when(j == 0)
        def _():
            hh = h_ref[...]
            xn_ref[...] = _rms(hh, g_ref[...])[0].astype(BF16)
            o_ref[...] = hh

        xn = xn_ref[...]
        gt = _mm(xn, wg_ref[...])
        up = _mm(xn, wu_ref[...])
        gt_ref[...] = gt.astype(BF16)
        up_ref[...] = up.astype(BF16)
        act = gt * _sigmoid(gt) * up
        o_ref[...] += 0.5 * _mm(act, wd_ref[...].reshape(FF_TILE, D))

    tile = pl.BlockSpec((tm, FF_TILE), lambda i, j: (i, j))
    return pl.pallas_call(
        body, name=name, grid=(T // tm, nj),
        in_specs=[pl.BlockSpec((tm, D), lambda i, j: (i, 0)),
                  pl.BlockSpec((1, D), lambda i, j: (0, 0))] + _ffn_weight_specs(),
        out_specs=[pl.BlockSpec((tm, D), lambda i, j: (i, 0)), tile, tile,
                   pl.BlockSpec((tm, D), lambda i, j: (i, 0))],
        out_shape=[jax.ShapeDtypeStruct((T, D), F32), jax.ShapeDtypeStruct((T, FFP), BF16),
                   jax.ShapeDtypeStruct((T, FFP), BF16), jax.ShapeDtypeStruct((T, D), BF16)],
        compiler_params=_cp("parallel", "arbitrary"),
    )(h, g, wa, wa, wb)


def ffn_bwd(h, dy, g, gt_saved, up_saved, wa, wb, name):
    T = h.shape[0]
    tm = min(TOK_TILE, T)
    nj = FFP // FF_TILE

    def body(h_ref, dy_ref, g_ref, gt_ref, up_ref, wg_ref, wu_ref, wd_ref,
             dh_ref, dg_ref, du_ref, a_ref, dn_ref, dxn_s):
        i = pl.program_id(0)
        j = pl.program_id(1)

        @pl.when(j == 0)
        def _():
            dxn_s[...] = jnp.zeros_like(dxn_s)

        @pl.when((i == 0) & (j == 0))
        def _():
            dn_ref[...] = jnp.zeros_like(dn_ref)

        gt = gt_ref[...].astype(F32)
        up = up_ref[...].astype(F32)
        sg = _sigmoid(gt)
        si = gt * sg
        da = _mm_nt(0.5 * dy_ref[...], wd_ref[...].reshape(FF_TILE, D))
        dup = da * si
        dgt = da * up * (sg * (1.0 + gt * (1.0 - sg)))
        dg_ref[...] = dgt.astype(BF16)
        du_ref[...] = dup.astype(BF16)
        a_ref[...] = (si * up).astype(BF16)
        dxn_s[...] += _mm_nt(dgt, wg_ref[...]) + _mm_nt(dup, wu_ref[...])

        @pl.when(j == nj - 1)
        def _():
            gg = g_ref[...]
            _, xhat, rstd = _rms(h_ref[...], gg)
            dh, dn = _rms_bwd(dxn_s[...], xhat, rstd, gg)
            dh_ref[...] = dy_ref[...] + dh
            dn_ref[...] += dn

    tile = pl.BlockSpec((tm, FF_TILE), lambda i, j: (i, j))
    return pl.pallas_call(
        body, name=name, grid=(T // tm, nj),
        in_specs=[pl.BlockSpec((tm, D), lambda i, j: (i, 0)),
                  pl.BlockSpec((tm, D), lambda i, j: (i, 0)),
                  pl.BlockSpec((1, D), lambda i, j: (0, 0)), tile, tile] + _ffn_weight_specs(),
        out_specs=[pl.BlockSpec((tm, D), lambda i, j: (i, 0)), tile, tile, tile,
                   pl.BlockSpec((1, D), lambda i, j: (0, 0))],
        out_shape=[jax.ShapeDtypeStruct((T, D), F32)] + [jax.ShapeDtypeStruct((T, FFP), BF16)] * 3
        + [jax.ShapeDtypeStruct((1, D), F32)],
        scratch_shapes=[pltpu.VMEM((tm, D), F32)],
        compiler_params=_cp("arbitrary", "arbitrary"),
    )(h, dy, g, gt_saved, up_saved, wa, wa, wb)


def _pick(n, prefs):
    for t in prefs:
        if n % t == 0:
            return t
    return n


def _tn_body(nk, scale, out_dtype, squeeze):
    def body(a_ref, b_ref, *rest):
        o_ref, acc = rest[-2], rest[-1]
        k = pl.program_id(2)

        @pl.when(k == 0)
        def _():
            acc[...] = jnp.zeros_like(acc)

        acc[...] += _mm_tn(a_ref[...], b_ref[...])

        @pl.when(k == nk - 1)
        def _():
            res = (scale * acc[...]).astype(out_dtype)
            if squeeze:
                o_ref[0] = res
            else:
                o_ref[...] = res

    return body


def matmul_tn(a, b, name, scale=1.0, out_dtype=BF16):
    T, M = a.shape
    N = b.shape[1]
    tmm = _pick(M, (512, 256))
    tnn = _pick(N, (1024, 2432))
    tk = min(TOK_TILE, T)
    nk = T // tk
    return pl.pallas_call(
        _tn_body(nk, scale, out_dtype, False), name=name, grid=(M // tmm, N // tnn, nk),
        in_specs=[pl.BlockSpec((tk, tmm), lambda i, j, k: (k, i)),
                  pl.BlockSpec((tk, tnn), lambda i, j, k: (k, j))],
        out_specs=pl.BlockSpec((tmm, tnn), lambda i, j, k: (i, j)),
        out_shape=jax.ShapeDtypeStruct((M, N), out_dtype),
        scratch_shapes=[pltpu.VMEM((tmm, tnn), F32)],
        compiler_params=_cp("parallel", "parallel", "arbitrary"),
    )(a, b)


def grad_cols(a, b, dst, slot, name):
    T = a.shape[0]
    tmm, tnn = D, FFP // 2
    tk = min(TOK_TILE, T)
    nk = T // tk
    return pl.pallas_call(
        _tn_body(nk, 1.0, BF16, True), name=name, grid=(D // tmm, FFP // tnn, nk),
        in_specs=[pl.BlockSpec((tk, tmm), lambda i, j, k: (k, i)),
                  pl.BlockSpec((tk, tnn), lambda i, j, k: (k, j)),
                  pl.BlockSpec(memory_space=pl.ANY)],
        out_specs=pl.BlockSpec((1, tmm, tnn), lambda i, j, k: (slot, i, j)),
        out_shape=jax.ShapeDtypeStruct(dst.shape, dst.dtype),
        scratch_shapes=[pltpu.VMEM((tmm, tnn), F32)],
        input_output_aliases={2: 0},
        compiler_params=_cp("parallel", "parallel", "arbitrary"),
    )(a, b, dst)


def grad_rows(a, b, dst, row0, rows, name, scale=1.0):
    T = a.shape[0]
    tk = min(TOK_TILE, T)
    nk = T // tk
    blk = row0 // rows

    def body(a_ref, b_ref, dst_ref, o_ref, acc):
        k = pl.program_id(0)

        @pl.when(k == 0)
        def _():
            acc[...] = jnp.zeros_like(acc)

        acc[...] += _mm_tn(a_ref[...], b_ref[...])

        @pl.when(k == nk - 1)
        def _():
            o_ref[...] = (scale * acc[...]).astype(BF16).reshape(N_DEV, rows, D)

    return pl.pallas_call(
        body, name=name, grid=(nk,),
        in_specs=[pl.BlockSpec((tk, N_DEV * rows), lambda k: (k, 0)),
                  pl.BlockSpec((tk, D), lambda k: (k, 0)),
                  pl.BlockSpec(memory_space=pl.ANY)],
        out_specs=pl.BlockSpec((N_DEV, rows, D), lambda k: (0, blk, 0)),
        out_shape=jax.ShapeDtypeStruct(dst.shape, dst.dtype),
        scratch_shapes=[pltpu.VMEM((N_DEV * rows, D), F32)],
        input_output_aliases={2: 0},
        compiler_params=_cp("arbitrary"),
    )(a, b, dst)


U_SPLITS = (512, 768, 1024, 128)
U_OFFS = (0, 512, 1280, 2304)


def mixin_fwd(h, g, w_in, name):
    T = h.shape[0]
    tm = min(TOK_TILE, T)

    def body(h_ref, g_ref, w_ref, u0, u1, u2, u3, xn_ref):
        xn = _rms(h_ref[...], g_ref[...])[0].astype(BF16)
        xn_ref[...] = xn
        u = _mm(xn, w_ref[...])
        for ref, off, n in zip((u0, u1, u2, u3), U_OFFS, U_SPLITS):
            ref[...] = u[:, off:off + n]

    return pl.pallas_call(
        body, name=name, grid=(T // tm,),
        in_specs=[_row_spec(tm, D), _full_spec((1, D)), _full_spec((D, D_IN_PAD))],
        out_specs=[_row_spec(tm, n) for n in U_SPLITS] + [_row_spec(tm, D)],
        out_shape=[jax.ShapeDtypeStruct((T, n), F32) for n in U_SPLITS]
        + [jax.ShapeDtypeStruct((T, D), BF16)],
        compiler_params=_cp("parallel"),
    )(h, g, w_in)


def mixin_bwd(h, dh_in, g, w_in, dus, name):
    T = h.shape[0]
    tm = min(TOK_TILE, T)

    def body(h_ref, dhi_ref, g_ref, w_ref, d0, d1, d2, d3, dh_ref, du_ref, dn_ref):
        @pl.when(pl.program_id(0) == 0)
        def _():
            dn_ref[...] = jnp.zeros_like(dn_ref)

        dxn = jnp.zeros((tm, D), F32)
        for ref, off, n in zip((d0, d1, d2, d3), U_OFFS, U_SPLITS):
            du = ref[...]
            du_ref[:, off:off + n] = du.astype(BF16)
            dxn += _mm_nt(du, w_ref[:, off:off + n])
        gg = g_ref[...]
        _, xhat, rstd = _rms(h_ref[...], gg)
        dh, dn = _rms_bwd(dxn, xhat, rstd, gg)
        dh_ref[...] = dhi_ref[...] + dh
        dn_ref[...] += dn

    return pl.pallas_call(
        body, name=name, grid=(T // tm,),
        in_specs=[_row_spec(tm, D), _row_spec(tm, D), _full_spec((1, D)), _full_spec((D, D_IN_PAD))]
        + [_row_spec(tm, n) for n in U_SPLITS],
        out_specs=[_row_spec(tm, D), _row_spec(tm, D_IN_PAD), _full_spec((1, D))],
        out_shape=[jax.ShapeDtypeStruct((T, D), F32), jax.ShapeDtypeStruct((T, D_IN_PAD), BF16),
                   jax.ShapeDtypeStruct((1, D), F32)],
        compiler_params=_cp("arbitrary"),
    )(h, dh_in, g, w_in, *dus)


def _shift_down(x, s, row):
    if s == 0:
        return x
    return jnp.where(row >= s, pltpu.roll(x, s, 0), 0.0)


def _shift_up(x, s, row):
    if s == 0:
        return x
    n = x.shape[0]
    return jnp.where(row < n - s, pltpu.roll(x, n - s, 0), 0.0)


def conv_fwd(x, w, b, S, col0, C, name):
    T = x.shape[0]
    cb0 = col0 // LANE

    def body(x_ref, w_ref, b_ref, y_ref):
        xx = x_ref[...]
        row = _iota(xx.shape, 0)
        y = xx * w_ref[3:4, :] + b_ref[...]
        for k in range(3):
            y += _shift_down(xx, 3 - k, row) * w_ref[k:k + 1, :]
        y_ref[...] = y

    return pl.pallas_call(
        body, name=name, grid=(T // S, C // LANE),
        in_specs=[pl.BlockSpec((S, LANE), lambda s, c: (s, cb0 + c)),
                  pl.BlockSpec((4, LANE), lambda s, c: (0, c)),
                  pl.BlockSpec((1, LANE), lambda s, c: (0, c))],
        out_specs=pl.BlockSpec((S, LANE), lambda s, c: (s, c)),
        out_shape=jax.ShapeDtypeStruct((T, C), F32),
        compiler_params=_cp("parallel", "parallel"),
    )(x, w, b)


def conv_bwd(x, dy, w, S, col0, C, name):
    T = x.shape[0]
    cb0 = col0 // LANE

    def body(x_ref, dy_ref, w_ref, dx_ref, dwb_ref):
        @pl.when(pl.program_id(1) == 0)
        def _():
            dwb_ref[...] = jnp.zeros_like(dwb_ref)

        xx = x_ref[...]
        dd = dy_ref[...]
        row = _iota(xx.shape, 0)
        dx = dd * w_ref[3:4, :]
        for k in range(3):
            dx += _shift_up(dd, 3 - k, row) * w_ref[k:k + 1, :]
        dx_ref[...] = dx
        for k in range(4):
            dwb_ref[k:k + 1, :] += jnp.sum(dd * _shift_down(xx, 3 - k, row), axis=0, keepdims=True)
        dwb_ref[4:5, :] += jnp.sum(dd, axis=0, keepdims=True)

    return pl.pallas_call(
        body, name=name, grid=(C // LANE, T // S),
        in_specs=[pl.BlockSpec((S, LANE), lambda c, s: (s, cb0 + c)),
                  pl.BlockSpec((S, LANE), lambda c, s: (s, c)),
                  pl.BlockSpec((4, LANE), lambda c, s: (0, c))],
        out_specs=[pl.BlockSpec((S, LANE), lambda c, s: (s, c)),
                   pl.BlockSpec((8, LANE), lambda c, s: (0, c))],
        out_shape=[jax.ShapeDtypeStruct((T, C), F32), jax.ShapeDtypeStruct((8, C), F32)],
        compiler_params=_cp("parallel", "arbitrary"),
    )(x, dy, w)


def _scan(a, b, row):
    n = a.shape[0]
    d = 1
    while d < n:
        keep = row >= d
        b = a * jnp.where(keep, pltpu.roll(b, d, 0), 0.0) + b
        a = a * jnp.where(keep, pltpu.roll(a, d, 0), 1.0)
        d *= 2
    return b


def _rscan(a, b, row):
    n = a.shape[0]
    d = 1
    while d < n:
        keep = row < n - d
        b = a * jnp.where(keep, pltpu.roll(b, n - d, 0), 0.0) + b
        a = a * jnp.where(keep, pltpu.roll(a, n - d, 0), 1.0)
        d *= 2
    return b


GELU_C = math.sqrt(2.0 / math.pi)


def _gelu(x):
    t = jnp.tanh(GELU_C * (x + 0.044715 * (x * x * x)))
    return 0.5 * x * (1.0 + t), t


def _lru_gates(xr, wa, ba, wx, bx, lam):
    r = _sigmoid(_mm(xr, wa) + ba)
    i = _sigmoid(_mm(xr, wx) + bx)
    sp = jnp.maximum(-lam, 0.0) + jnp.log(1.0 + jnp.exp(-jnp.abs(lam)))
    la = -LRU_C * r * sp
    a = jnp.exp(la)
    e2 = a * a
    m = jnp.sqrt(-jnp.tanh(la) * (e2 + 1.0))
    return r, i, sp, a, e2, m


def lru_fwd(xr, u_lru, wa, wx, vec, S, name):
    T = xr.shape[0]

    def body(xr_ref, gt_ref, wa_ref, wx_ref, vec_ref, y_ref):
        x = xr_ref[...]
        row = _iota(x.shape, 0)
        r, i, sp, a, e2, m = _lru_gates(x, wa_ref[...], vec_ref[0:1, :], wx_ref[...], vec_ref[1:2, :],
                                        vec_ref[2:3, :])
        hh = _scan(a, m * (i * x), row)
        y_ref[...] = _gelu(gt_ref[...])[0] * hh

    return pl.pallas_call(
        body, name=name, grid=(T // S, LRU_W // LANE),
        in_specs=[pl.BlockSpec((S, LANE), lambda s, c: (s, c)),
                  pl.BlockSpec((S, LANE), lambda s, c: (s, 2 + c)),
                  pl.BlockSpec((LANE, LANE), lambda s, c: (c, c)),
                  pl.BlockSpec((LANE, LANE), lambda s, c: (c, c)),
                  pl.BlockSpec((8, LANE), lambda s, c: (0, c))],
        out_specs=pl.BlockSpec((S, LANE), lambda s, c: (s, c)),
        out_shape=jax.ShapeDtypeStruct((T, LRU_W), F32),
        compiler_params=_cp("parallel", "parallel"),
    )(xr, u_lru, wa, wx, vec)


def lru_bwd(xr, u_lru, dy, wa, wx, vec, S, name):
    T = xr.shape[0]

    def body(xr_ref, gt_ref, dy_ref, wa_ref, wx_ref, vec_ref,
             dxr_ref, dgt_ref, dwa_ref, dwx_ref, dvec_ref):
        @pl.when(pl.program_id(1) == 0)
        def _():
            dwa_ref[...] = jnp.zeros_like(dwa_ref)
            dwx_ref[...] = jnp.zeros_like(dwx_ref)
            dvec_ref[...] = jnp.zeros_like(dvec_ref)

        x = xr_ref[...]
        n = x.shape[0]
        row = _iota(x.shape, 0)
        lam = vec_ref[2:3, :]
        r, i, sp, a, e2, m = _lru_gates(x, wa_ref[...], vec_ref[0:1, :], wx_ref[...], vec_ref[1:2, :], lam)
        v = i * x
        hh = _scan(a, m * v, row)
        gt = gt_ref[...]
        dy = dy_ref[...]
        ge, t = _gelu(gt)
        dgt_ref[...] = dy * hh * (0.5 * (1.0 + t) + 0.5 * gt * (1.0 - t * t) * GELU_C
                                  * (1.0 + 3.0 * 0.044715 * gt * gt))
        a_next = jnp.where(row < n - 1, pltpu.roll(a, n - 1, 0), 0.0)
        G = _rscan(a_next, dy * ge, row)
        da = G * _shift_down(hh, 1, row)
        dv = G * m
        dla = da * a - (G * v) * e2 / m
        dr = dla * (-LRU_C * sp)
        dsp = jnp.sum(dla * (-LRU_C * r), axis=0, keepdims=True)
        dra = dr * r * (1.0 - r)
        dia = (dv * x) * i * (1.0 - i)
        dxr_ref[...] = dv * i + _mm_nt(dra, wa_ref[...]) + _mm_nt(dia, wx_ref[...])
        dwa_ref[0] += _mm_tn(x, dra)
        dwx_ref[0] += _mm_tn(x, dia)
        dvec_ref[0:1, :] += jnp.sum(dra, axis=0, keepdims=True)
        dvec_ref[1:2, :] += jnp.sum(dia, axis=0, keepdims=True)
        dvec_ref[2:3, :] += dsp * (-_sigmoid(-lam))

    return pl.pallas_call(
        body, name=name, grid=(LRU_W // LANE, T // S),
        in_specs=[pl.BlockSpec((S, LANE), lambda c, s: (s, c)),
                  pl.BlockSpec((S, LANE), lambda c, s: (s, 2 + c)),
                  pl.BlockSpec((S, LANE), lambda c, s: (s, c)),
                  pl.BlockSpec((LANE, LANE), lambda c, s: (c, c)),
                  pl.BlockSpec((LANE, LANE), lambda c, s: (c, c)),
                  pl.BlockSpec((8, LANE), lambda c, s: (0, c))],
        out_specs=[pl.BlockSpec((S, LANE), lambda c, s: (s, c)),
                   pl.BlockSpec((S, LANE), lambda c, s: (s, c)),
                   pl.BlockSpec((1, LANE, LANE), lambda c, s: (c, 0, 0)),
                   pl.BlockSpec((1, LANE, LANE), lambda c, s: (c, 0, 0)),
                   pl.BlockSpec((8, LANE), lambda c, s: (0, c))],
        out_shape=[jax.ShapeDtypeStruct((T, LRU_W), F32), jax.ShapeDtypeStruct((T, LRU_W), F32),
                   jax.ShapeDtypeStruct((2, LANE, LANE), F32), jax.ShapeDtypeStruct((2, LANE, LANE), F32),
                   jax.ShapeDtypeStruct((8, LRU_W), F32)],
        compiler_params=_cp("parallel", "arbitrary"),
    )(xr, u_lru, dy, wa, wx, vec)


def _bucket_table():
    qi = np.arange(BQ)[:, None]
    kj = np.arange(2 * BQ)[None, :]
    dist = BQ + qi - kj
    band = (dist >= 0) & (dist < WINDOW)
    dd = np.maximum(dist, 0)
    max_exact = REL_BUCKETS // 2
    large = max_exact + (np.log(np.maximum(dd, 1).astype(np.float32) / np.float32(max_exact))
                         / np.float32(math.log(REL_MAX_DIST / max_exact))
                         * np.float32(REL_BUCKETS - max_exact)).astype(np.int32)
    large = np.minimum(large, REL_BUCKETS - 1)
    bucket = np.where(dd < max_exact, dd, large)
    return np.where(band, bucket, -1).astype(np.int32)


def _att_specs(S):
    nb = S // BQ
    qc = ATT_W // LANE
    return [pl.BlockSpec((BQ, ATT_W), lambda b, n: (b * nb + n, 0)),
            pl.BlockSpec((BQ, KV_W), lambda b, n: (b * nb + jnp.maximum(n - 1, 0), qc)),
            pl.BlockSpec((BQ, KV_W), lambda b, n: (b * nb + n, qc)),
            pl.BlockSpec((BQ, KV_W), lambda b, n: (b * nb + jnp.maximum(n - 1, 0), qc + 1)),
            pl.BlockSpec((BQ, KV_W), lambda b, n: (b * nb + n, qc + 1))]


def _att_bias(bk, rb_ref, bias_s):
    for h in range(ATT_H):
        acc = jnp.zeros(bk.shape, F32)
        for bb in range(REL_BUCKETS):
            acc = jnp.where(bk == bb, rb_ref[bb * ATT_H + h], acc)
        bias_s[h] = acc


def _att_probs(qh, kg, bias, valid, sink):
    s = _mm_nt(qh, kg) * (HD ** -0.5) + bias
    s = jnp.where(valid, s, NEG)
    m = jnp.maximum(jnp.max(s, axis=-1, keepdims=True), sink)
    e = jnp.exp(s - m)
    es = jnp.exp(sink - m)
    den = jnp.sum(e, axis=-1, keepdims=True) + es
    return e / den, es / den


def attn_fwd(u_att, sinks, rel_bias, S, name):
    T = u_att.shape[0]
    nb = S // BQ
    table = jnp.asarray(_bucket_table())

    def body(sk_ref, rb_ref, bk_ref, q_ref, kp_ref, kc_ref, vp_ref, vc_ref, o_ref, bias_s):
        b = pl.program_id(0)
        n = pl.program_id(1)
        bk = bk_ref[...]

        @pl.when((b == 0) & (n == 0))
        def _():
            _att_bias(bk, rb_ref, bias_s)

        valid = (bk >= 0) & ((n > 0) | (_iota(bk.shape, 1) >= BQ))
        for h in range(ATT_H):
            gs = slice(HD * (h // ATT_G), HD * (h // ATT_G + 1))
            kg = jnp.concatenate([kp_ref[:, gs], kc_ref[:, gs]], axis=0)
            vg = jnp.concatenate([vp_ref[:, gs], vc_ref[:, gs]], axis=0)
            p, _ = _att_probs(q_ref[:, HD * h:HD * (h + 1)], kg, bias_s[h], valid, sk_ref[h])
            o_ref[:, HD * h:HD * (h + 1)] = _mm(p, vg)

    smem = pl.BlockSpec(memory_space=pltpu.SMEM)
    return pl.pallas_call(
        body, name=name, grid=(T // S, nb),
        in_specs=[smem, smem, _full_spec((BQ, 2 * BQ))] + _att_specs(S),
        out_specs=pl.BlockSpec((BQ, ATT_W), lambda b, n: (b * nb + n, 0)),
        out_shape=jax.ShapeDtypeStruct((T, ATT_W), F32),
        scratch_shapes=[pltpu.VMEM((ATT_H, BQ, 2 * BQ), F32)],
        compiler_params=_cp("arbitrary", "arbitrary"),
    )(sinks, rel_bias, table, u_att, u_att, u_att, u_att, u_att)


def attn_bwd(u_att, dy, sinks, rel_bias, S, name):
    T = u_att.shape[0]
    nb = S // BQ
    nB = T // S
    table = jnp.asarray(_bucket_table())
    scale = HD ** -0.5

    def body(sk_ref, rb_ref, bk_ref, q_ref, kp_ref, kc_ref, vp_ref, vc_ref, dy_ref,
             du_ref, drel_ref, dsk_ref, bias_s, dbias_s):
        b = pl.program_id(0)
        n = pl.program_id(1)
        bk = bk_ref[...]

        @pl.when((b == 0) & (n == 0))
        def _():
            _att_bias(bk, rb_ref, bias_s)
            dbias_s[...] = jnp.zeros_like(dbias_s)
            dsk_ref[...] = jnp.zeros_like(dsk_ref)
            drel_ref[...] = jnp.zeros_like(drel_ref)

        @pl.when(n == 0)
        def _():
            du_ref[...] = jnp.zeros_like(du_ref)

        valid = (bk >= 0) & ((n > 0) | (_iota(bk.shape, 1) >= BQ))
        r_cur = pl.multiple_of(n * BQ, BQ)
        r_prev = pl.multiple_of(jnp.maximum(n - 1, 0) * BQ, BQ)
        for g in range(ATT_KV):
            gs = slice(HD * g, HD * (g + 1))
            kg = jnp.concatenate([kp_ref[:, gs], kc_ref[:, gs]], axis=0)
            vg = jnp.concatenate([vp_ref[:, gs], vc_ref[:, gs]], axis=0)
            dk = jnp.zeros((2 * BQ, HD), F32)
            dv = jnp.zeros((2 * BQ, HD), F32)
            for e in range(ATT_G):
                h = g * ATT_G + e
                qh = q_ref[:, HD * h:HD * (h + 1)]
                do = dy_ref[:, HD * h:HD * (h + 1)]
                p, ps = _att_probs(qh, kg, bias_s[h], valid, sk_ref[h])
                dp = _mm_nt(do, vg)
                delta = jnp.sum(p * dp, axis=-1, keepdims=True)
                ds = p * (dp - delta)
                dbias_s[h] += ds
                dsk_ref[h:h + 1, :] += jnp.broadcast_to(
                    jnp.sum(-ps * delta, axis=0, keepdims=True), (1, LANE))
                dss = ds * scale
                du_ref[pl.ds(r_cur, BQ), HD * h:HD * (h + 1)] = _mm(dss, kg)
                dk += _mm_tn(dss, qh)
                dv += _mm_tn(p, do)
            ck = ATT_W + HD * g
            cv = ATT_W + KV_W + HD * g
            du_ref[pl.ds(r_prev, BQ), ck:ck + HD] += dk[0:BQ]
            du_ref[pl.ds(r_cur, BQ), ck:ck + HD] += dk[BQ:]
            du_ref[pl.ds(r_prev, BQ), cv:cv + HD] += dv[0:BQ]
            du_ref[pl.ds(r_cur, BQ), cv:cv + HD] += dv[BQ:]

        @pl.when((b == nB - 1) & (n == nb - 1))
        def _():
            lane = _iota((1, LANE), 1)
            for h in range(ATT_H):
                db = dbias_s[h]
                acc = jnp.zeros((1, LANE), F32)
                for bb in range(REL_BUCKETS):
                    val = jnp.sum(jnp.sum(jnp.where(bk == bb, db, 0.0), axis=1, keepdims=True),
                                  axis=0, keepdims=True)
                    acc = jnp.where(lane == bb, val, acc)
                drel_ref[h:h + 1, :] = acc

    smem = pl.BlockSpec(memory_space=pltpu.SMEM)
    return pl.pallas_call(
        body, name=name, grid=(nB, nb),
        in_specs=[smem, smem, _full_spec((BQ, 2 * BQ))] + _att_specs(S)
        + [pl.BlockSpec((BQ, ATT_W), lambda b, n: (b * nb + n, 0))],
        out_specs=[pl.BlockSpec((S, ATT_W + 2 * KV_W), lambda b, n: (b, 0)),
                   _full_spec((8, LANE)), _full_spec((8, LANE))],
        out_shape=[jax.ShapeDtypeStruct((T, ATT_W + 2 * KV_W), F32),
                   jax.ShapeDtypeStruct((8, LANE), F32), jax.ShapeDtypeStruct((8, LANE), F32)],
        scratch_shapes=[pltpu.VMEM((ATT_H, BQ, 2 * BQ), F32), pltpu.VMEM((ATT_H, BQ, 2 * BQ), F32)],
        compiler_params=_cp("arbitrary", "arbitrary"),
    )(sinks, rel_bias, table, u_att, u_att, u_att, u_att, u_att, dy)


def _head_of(i):
    return lax.shift_right_logical(i, 6)


def _head_mask(shape):
    return (_head_of(_iota(shape, 0)) == _head_of(_iota(shape, 1))).astype(F32)


def _dn_point(c, uba, alog, dtb):
    s = c * _sigmoid(c)
    qt, kt, vt = s[:, 0:256], s[:, 256:512], s[:, 512:768]
    ones_bd = _head_mask((DN_W, DN_W))
    q = qt * lax.rsqrt(_mmx(qt * qt, ones_bd) + EPS) * (HD ** -0.5)
    k = kt * lax.rsqrt(_mmx(kt * kt, ones_bd) + EPS)
    sel = _head_of(_iota((LANE, DN_W), 1))
    row = _iota((LANE, DN_W), 0)
    braw = _mmx(uba, (row == sel).astype(F32))
    araw = _mmx(uba, (row == sel + DN_H).astype(F32)) + dtb
    beta = _sigmoid(braw)
    g = -jnp.exp(alog) * (jnp.maximum(araw, 0.0) + jnp.log(1.0 + jnp.exp(-jnp.abs(araw))))
    return q, k, vt, g, beta


def dn_point_fwd(c, uba, alog, dtb, name):
    T = c.shape[0]
    tm = min(TOK_TILE, T)

    def body(c_ref, u_ref, al_ref, dt_ref, *outs):
        for ref, val in zip(outs, _dn_point(c_ref[...], u_ref[...], al_ref[...], dt_ref[...])):
            ref[...] = val

    return pl.pallas_call(
        body, name=name, grid=(T // tm,),
        in_specs=[_row_spec(tm, 768), _row_spec(tm, LANE), _full_spec((1, DN_W)), _full_spec((1, DN_W))],
        out_specs=[_row_spec(tm, DN_W)] * 5,
        out_shape=[jax.ShapeDtypeStruct((T, DN_W), F32)] * 5,
        compiler_params=_cp("parallel"),
    )(c, uba, alog, dtb)


def dn_point_bwd(c, uba, alog, dtb, douts, name):
    T = c.shape[0]
    tm = min(TOK_TILE, T)

    def body(c_ref, u_ref, al_ref, dt_ref, dq, dk, dv, dg, db, dc_ref, du_ref, dvec_ref):
        @pl.when(pl.program_id(0) == 0)
        def _():
            dvec_ref[...] = jnp.zeros_like(dvec_ref)

        _, vjp = jax.vjp(_dn_point, c_ref[...], u_ref[...], al_ref[...], dt_ref[...])
        dc, du, dal, ddt = vjp((dq[...], dk[...], dv[...], dg[...], db[...]))
        dc_ref[...] = dc
        du_ref[...] = du
        fold = (_iota((LANE, DN_W), 0) == _head_of(_iota((LANE, DN_W), 1))).astype(F32)
        both = jnp.concatenate([dal, ddt, jnp.zeros((6, DN_W), F32)], axis=0)
        dvec_ref[...] += _mmx_nt(both, fold)

    return pl.pallas_call(
        body, name=name, grid=(T // tm,),
        in_specs=[_row_spec(tm, 768), _row_spec(tm, LANE), _full_spec((1, DN_W)), _full_spec((1, DN_W))]
        + [_row_spec(tm, DN_W)] * 5,
        out_specs=[_row_spec(tm, 768), _row_spec(tm, LANE), _full_spec((8, LANE))],
        out_shape=[jax.ShapeDtypeStruct((T, 768), F32), jax.ShapeDtypeStruct((T, LANE), F32),
                   jax.ShapeDtypeStruct((8, LANE), F32)],
        compiler_params=_cp("arbitrary"),
    )(c, uba, alog, dtb, *douts)


def _unit_lower_inverses(lmats):
    eye = (_iota(lmats[0].shape, 0) == _iota(lmats[0].shape, 1)).astype(F32)
    tinvs = [eye - lm for lm in lmats]
    pws = list(lmats)
    for _ in range(5):
        pws = [_mm3(pw, pw) for pw in pws]
        tinvs = [t + _mm3(t, pw) for t, pw in zip(tinvs, pws)]
    return tuple(tinvs)


def _inverse_bwd(tinv, d):
    return -_mm3_nt(_mm3_tn(tinv, d), tinv)


@jax.custom_vjp
def _tri_invs(lmats):
    return _unit_lower_inverses(lmats)


def _tri_invs_fwd(lmats):
    tinvs = _unit_lower_inverses(lmats)
    return tinvs, tinvs


_tri_invs.defvjp(_tri_invs_fwd, lambda tinvs, ds: (tuple(_inverse_bwd(t, d) for t, d in zip(tinvs, ds)),))


@jax.custom_vjp
def _tri_inv_known(lmat, tinv):
    return tinv


_tri_inv_known.defvjp(lambda lmat, tinv: (tinv, tinv),
                      lambda tinv, d: (_inverse_bwd(tinv, d), jnp.zeros_like(tinv)))


DN_SUB = 4


def _dn_stack(x):
    return jnp.concatenate([x, x, x, x], axis=0) * _head_mask((DN_W, DN_W))


def _dn_pre_inverse(q, k, v, g, beta):
    hm = _head_mask((DN_W, DN_W))
    ri = _iota((DN_W, DN_W), 0) & (CHUNK - 1)
    ci = _iota((DN_W, DN_W), 1) & (CHUNK - 1)
    tri64 = (_iota((CHUNK, CHUNK), 0) >= _iota((CHUNK, CHUNK), 1)).astype(F32)
    gc = _mm3(tri64, g)
    ks = _dn_stack(k)
    gcol = jnp.sum(_dn_stack(gc), axis=1, keepdims=True) * (1.0 / HD)
    gmat = jnp.broadcast_to(gcol, (DN_W, DN_W))
    decay = jnp.exp(jnp.minimum(gmat - gmat.T, 0.0))
    lmat = _mm_nt(_dn_stack(k * beta), ks) * decay * (hm * (ri > ci).astype(F32))
    att = _mm_nt(_dn_stack(q), ks) * decay * (hm * (ri >= ci).astype(F32))
    return lmat, att, gc


def _dn_post_inverse(q, k, v, g, beta, tinv, att, gc):
    glast = jnp.sum(g, axis=0, keepdims=True)
    eg = jnp.exp(gc)
    u = _mm(tinv, _dn_stack(v * beta))
    w = _mm(tinv, _dn_stack(k * beta * eg))
    return u, w, att, _dn_stack(q * eg), _dn_stack(k * jnp.exp(glast - gc)), jnp.exp(glast), tinv


def _dn_apply(state, prep):
    u, w, att, qe, kd, eglast, _ = prep
    vn = u - _mm(w, state)
    o4 = _mm(qe, state) + _mm(att, vn)
    o = o4[0:64] + o4[64:128] + o4[128:192] + o4[192:256]
    return o, state * eglast + _mm_tn(kd, vn)


def _dn_chunks(state, q, k, v, g, beta, knowns=None):
    n = q.shape[0] // CHUNK
    chunks = [tuple(x[c * CHUNK:(c + 1) * CHUNK] for x in (q, k, v, g, beta)) for c in range(n)]
    pre = [_dn_pre_inverse(*ch) for ch in chunks]
    if knowns is None:
        tinvs = _tri_invs(tuple(p[0] for p in pre))
    else:
        tinvs = [_tri_inv_known(p[0], known) for p, known in zip(pre, knowns)]
    preps = [_dn_post_inverse(*ch, tinv, p[1], p[2]) for ch, tinv, p in zip(chunks, tinvs, pre)]
    outs = []
    for prep in preps:
        o, state = _dn_apply(state, prep)
        outs.append(o)
    return jnp.concatenate(outs, axis=0), state, [prep[-1] for prep in preps]


def dn_scan_fwd(q, k, v, g, beta, S, name):
    T = q.shape[0]
    rows = DN_SUB * CHUNK
    ns = S // rows

    def body(q_ref, k_ref, v_ref, g_ref, b_ref, o_ref, st_ref, ti_ref, s_s):
        @pl.when(pl.program_id(1) == 0)
        def _():
            s_s[...] = jnp.zeros_like(s_s)

        st = s_s[...]
        st_ref[0] = st
        o, new, tinvs = _dn_chunks(st, q_ref[...], k_ref[...], v_ref[...], g_ref[...], b_ref[...])
        o_ref[...] = o
        for c, tinv in enumerate(tinvs):
            ti_ref[c] = tinv
        s_s[...] = new

    spec = pl.BlockSpec((rows, DN_W), lambda b, t: (b * ns + t, 0))
    return pl.pallas_call(
        body, name=name, grid=(T // S, ns),
        in_specs=[spec] * 5,
        out_specs=[spec, pl.BlockSpec((1, DN_W, DN_W), lambda b, t: (b * ns + t, 0, 0)),
                   pl.BlockSpec((DN_SUB, DN_W, DN_W), lambda b, t: (b * ns + t, 0, 0))],
        out_shape=[jax.ShapeDtypeStruct((T, DN_W), F32),
                   jax.ShapeDtypeStruct((T // rows, DN_W, DN_W), F32),
                   jax.ShapeDtypeStruct((T // CHUNK, DN_W, DN_W), F32)],
        scratch_shapes=[pltpu.VMEM((DN_W, DN_W), F32)],
        compiler_params=_cp("parallel", "arbitrary"),
    )(q, k, v, g, beta)


def dn_scan_bwd(q, k, v, g, beta, states, tinvs, do, S, name):
    T = q.shape[0]
    rows = DN_SUB * CHUNK
    ns = S // rows

    def body(q_ref, k_ref, v_ref, g_ref, b_ref, st_ref, ti_ref, do_ref, dq, dk, dv, dg, db, ds_s):
        @pl.when(pl.program_id(1) == 0)
        def _():
            ds_s[...] = jnp.zeros_like(ds_s)

        knowns = [ti_ref[c] for c in range(DN_SUB)]
        _, vjp = jax.vjp(lambda *args: _dn_chunks(*args, knowns=knowns)[:2],
                         st_ref[0], q_ref[...], k_ref[...], v_ref[...], g_ref[...], b_ref[...])
        grads = vjp((do_ref[...], ds_s[...]))
        ds_s[...] = grads[0]
        for ref, val in zip((dq, dk, dv, dg, db), grads[1:]):
            ref[...] = val

    spec = pl.BlockSpec((rows, DN_W), lambda b, t: (b * ns + ns - 1 - t, 0))
    return pl.pallas_call(
        body, name=name, grid=(T // S, ns),
        in_specs=[spec] * 5 + [pl.BlockSpec((1, DN_W, DN_W), lambda b, t: (b * ns + ns - 1 - t, 0, 0)),
                               pl.BlockSpec((DN_SUB, DN_W, DN_W), lambda b, t: (b * ns + ns - 1 - t, 0, 0)),
                               spec],
        out_specs=[spec] * 5,
        out_shape=[jax.ShapeDtypeStruct((T, DN_W), F32)] * 5,
        scratch_shapes=[pltpu.VMEM((DN_W, DN_W), F32)],
        compiler_params=_cp("parallel", "arbitrary"),
    )(q, k, v, g, beta, states, tinvs, do)


def _dn_gate(o, z, nl):
    ms = _mmx(o * o, _head_mask((DN_W, DN_W))) * (1.0 / HD)
    return o * lax.rsqrt(ms + EPS) * nl * (z * _sigmoid(z))


def dn_gate_fwd(o, u_dn, nl, name):
    T = o.shape[0]
    tm = min(TOK_TILE, T)

    def body(o_ref, z_ref, n_ref, y_ref):
        y_ref[...] = _dn_gate(o_ref[...], z_ref[...], n_ref[...])

    return pl.pallas_call(
        body, name=name, grid=(T // tm,),
        in_specs=[_row_spec(tm, DN_W), pl.BlockSpec((tm, DN_W), lambda i: (i, 3)), _full_spec((1, DN_W))],
        out_specs=_row_spec(tm, DN_W),
        out_shape=jax.ShapeDtypeStruct((T, DN_W), F32),
        compiler_params=_cp("parallel"),
    )(o, u_dn, nl)


def dn_gate_bwd(o, u_dn, nl, dy, name):
    T = o.shape[0]
    tm = min(TOK_TILE, T)

    def body(o_ref, z_ref, n_ref, dy_ref, do_ref, dz_ref, dn_ref):
        @pl.when(pl.program_id(0) == 0)
        def _():
            dn_ref[...] = jnp.zeros_like(dn_ref)

        _, vjp = jax.vjp(_dn_gate, o_ref[...], z_ref[...], n_ref[...])
        do, dz, dn = vjp(dy_ref[...])
        do_ref[...] = do
        dz_ref[...] = dz
        fold = (_iota((LANE, DN_W), 0) == (_iota((LANE, DN_W), 1) & (HD - 1))).astype(F32)
        dn_ref[...] += _mmx_nt(jnp.concatenate([dn, jnp.zeros((7, DN_W), F32)], axis=0), fold)

    return pl.pallas_call(
        body, name=name, grid=(T // tm,),
        in_specs=[_row_spec(tm, DN_W), pl.BlockSpec((tm, DN_W), lambda i: (i, 3)), _full_spec((1, DN_W)),
                  _row_spec(tm, DN_W)],
        out_specs=[_row_spec(tm, DN_W), _row_spec(tm, DN_W), _full_spec((8, LANE))],
        out_shape=[jax.ShapeDtypeStruct((T, DN_W), F32), jax.ShapeDtypeStruct((T, DN_W), F32),
                   jax.ShapeDtypeStruct((8, LANE), F32)],
        compiler_params=_cp("arbitrary"),
    )(o, u_dn, nl, dy)


Y_SPLITS = (LRU_W, ATT_W, DN_W)
Y_OFFS = (0, LRU_W, LRU_W + ATT_W)


ROWS_DEV = D // N_DEV


def _dev_rows_spec(row0):
    return pl.BlockSpec((N_DEV, ROWS_DEV, D), lambda *_: (0, row0 // ROWS_DEV, 0))


def _dev_rows(w_ref, off, n):
    return w_ref[off // ROWS_DEV:(off + n) // ROWS_DEV].reshape(n, D)


def wout_fwd(h, ys, wb, name):
    T = h.shape[0]
    tm = min(TOK_TILE, T)

    def body(h_ref, y0, y1, y2, w_ref, o_ref, yc_ref):
        acc = h_ref[...]
        for ref, off, n in zip((y0, y1, y2), Y_OFFS, Y_SPLITS):
            y = ref[...].astype(BF16)
            yc_ref[:, off:off + n] = y
            acc += _mm(y, _dev_rows(w_ref, off, n))
        o_ref[...] = acc

    return pl.pallas_call(
        body, name=name, grid=(T // tm,),
        in_specs=[_row_spec(tm, D)] + [_row_spec(tm, n) for n in Y_SPLITS] + [_dev_rows_spec(R_WOUT)],
        out_specs=[_row_spec(tm, D), _row_spec(tm, D)],
        out_shape=[jax.ShapeDtypeStruct((T, D), F32), jax.ShapeDtypeStruct((T, D), BF16)],
        compiler_params=_cp("parallel"),
    )(h, *ys, wb)


def wout_bwd(dy, wb, name):
    T = dy.shape[0]
    tm = min(TOK_TILE, T)

    def body(dy_ref, w_ref, d0, d1, d2):
        dd = dy_ref[...].astype(BF16)
        for ref, off, n in zip((d0, d1, d2), Y_OFFS, Y_SPLITS):
            ref[...] = _mm_nt(dd, _dev_rows(w_ref, off, n))

    return pl.pallas_call(
        body, name=name, grid=(T // tm,),
        in_specs=[_row_spec(tm, D), _dev_rows_spec(R_WOUT)],
        out_specs=[_row_spec(tm, n) for n in Y_SPLITS],
        out_shape=[jax.ShapeDtypeStruct((T, n), F32) for n in Y_SPLITS],
        compiler_params=_cp("parallel"),
    )(dy, wb)


def ple_fwd(h, g, pe, wg, wp, name):
    T = h.shape[0]
    tm = min(TOK_TILE, T)

    def body(h_ref, g_ref, p_ref, wg_ref, wp_ref, o_ref):
        hh = h_ref[...]
        xn = _rms(hh, g_ref[...])[0]
        o_ref[...] = hh + _sigmoid(_mm(xn, _dev_rows(wg_ref, 0, D))) * _mm(p_ref[...], wp_ref[...])

    return pl.pallas_call(
        body, name=name, grid=(T // tm,),
        in_specs=[_row_spec(tm, D), _full_spec((1, D)), _row_spec(tm, PLE), _dev_rows_spec(R_PGATE),
                  _full_spec((PLE, D))],
        out_specs=_row_spec(tm, D),
        out_shape=jax.ShapeDtypeStruct((T, D), F32),
        compiler_params=_cp("parallel"),
    )(h, g, pe, wg, wp)


def ple_bwd(h, dy, g, pe, wg, wp, name):
    T = h.shape[0]
    tm = min(TOK_TILE, T)

    def body(h_ref, dy_ref, g_ref, p_ref, wg_ref, wp_ref, dh_ref, dz_ref, dpp_ref, xn_ref, dn_ref):
        @pl.when(pl.program_id(0) == 0)
        def _():
            dn_ref[...] = jnp.zeros_like(dn_ref)

        gg = g_ref[...]
        dy = dy_ref[...]
        xn, xhat, rstd = _rms(h_ref[...], gg)
        wg = _dev_rows(wg_ref, 0, D)
        gate = _sigmoid(_mm(xn, wg))
        pp = _mm(p_ref[...], wp_ref[...])
        dz = dy * pp * gate * (1.0 - gate)
        dz_ref[...] = dz.astype(BF16)
        dpp_ref[...] = (dy * gate).astype(BF16)
        xn_ref[...] = xn.astype(BF16)
        dh, dn = _rms_bwd(_mm_nt(dz, wg), xhat, rstd, gg)
        dh_ref[...] = dy + dh
        dn_ref[...] += dn

    return pl.pallas_call(
        body, name=name, grid=(T // tm,),
        in_specs=[_row_spec(tm, D), _row_spec(tm, D), _full_spec((1, D)), _row_spec(tm, PLE),
                  _dev_rows_spec(R_PGATE), _full_spec((PLE, D))],
        out_specs=[_row_spec(tm, D), _row_spec(tm, D), _row_spec(tm, D), _row_spec(tm, D), _full_spec((1, D))],
        out_shape=[jax.ShapeDtypeStruct((T, D), F32), jax.ShapeDtypeStruct((T, D), BF16),
                   jax.ShapeDtypeStruct((T, D), BF16), jax.ShapeDtypeStruct((T, D), BF16),
                   jax.ShapeDtypeStruct((1, D), F32)],
        compiler_params=_cp("arbitrary"),
    )(h, dy, g, pe, wg, wp)


def loss_head(h, g, target, name):
    T = h.shape[0]
    tm = min(TOK_TILE, T)

    def body(h_ref, g_ref, t_ref, loss_ref, dh_ref, dn_ref):
        @pl.when(pl.program_id(0) == 0)
        def _():
            dn_ref[...] = jnp.zeros_like(dn_ref)
            loss_ref[...] = jnp.zeros_like(loss_ref)

        gg = g_ref[...]
        y, xhat, rstd = _rms(h_ref[...], gg)
        err = y - t_ref[...]
        per_tok = jnp.mean(err * err, axis=-1, keepdims=True)
        loss_ref[...] += 0.5 * jnp.sum(per_tok, axis=0, keepdims=True)
        dh, dn = _rms_bwd(err * (1.0 / D), xhat, rstd, gg)
        dh_ref[...] = dh
        dn_ref[...] += dn

    return pl.pallas_call(
        body, name=name, grid=(T // tm,),
        in_specs=[_row_spec(tm, D), _full_spec((1, D)), _row_spec(tm, D)],
        out_specs=[_full_spec((8, LANE)), _row_spec(tm, D), _full_spec((1, D))],
        out_shape=[jax.ShapeDtypeStruct((8, LANE), F32), jax.ShapeDtypeStruct((T, D), F32),
                   jax.ShapeDtypeStruct((1, D), F32)],
        compiler_params=_cp("arbitrary"),
    )(h, g, target)


def _block_diag(w):
    return jnp.einsum('hij,hk->hikj', w, jnp.eye(4, dtype=w.dtype)).reshape(LRU_W, LRU_W)


def _layer_consts(W, l):
    row = lambda v: v.reshape(1, -1)
    zeros = jnp.zeros((5, LRU_W), F32)
    return dict(
        wa=_block_diag(W["lru_w_a"][l]), wx=_block_diag(W["lru_w_x"][l]),
        lru_vec=jnp.concatenate([row(W["lru_b_a"][l]), row(W["lru_b_x"][l]), row(W["lru_lambda"][l]), zeros], 0),
        lru_cb=row(W["lru_conv_b"][l]),
        sinks=W["attn_sinks"][l], rel=W["rel_bias"].reshape(-1),
        dn_cb=jnp.zeros((1, 3 * DN_W), F32),
        alog=row(jnp.repeat(W["dn_a_log"][l], HD)), dtb=row(jnp.repeat(W["dn_dt_bias"][l], HD)),
        dn_nl=row(jnp.tile(W["dn_norm"][l], DN_H)),
    )


def _layer_fwd(h0, pe, W, l, S, need=None):
    n = f"l{l}_"
    c_ = _layer_consts(W, l)
    row = lambda v: v.reshape(1, -1)
    need = need or (lambda *_: None)
    need(l, "f1", h0)
    h1, *ffn1_kept = ffn_fwd(h0, row(W["ffn1_norm"][l]), W["f1_cols"][l], W["f1_rows"][l], n + "ffn1_fwd")
    need(l, "in", h1)
    u_lru, u_att, u_dn, u_ba, xn_mix = mixin_fwd(h1, row(W["mix_norm"][l]), W["w_in"][l], n + "mixin_fwd")
    xr = conv_fwd(u_lru, W["lru_conv_w"][l], c_["lru_cb"], S, 0, LRU_W, n + "lru_conv_fwd")
    y_lru = lru_fwd(xr, u_lru, c_["wa"], c_["wx"], c_["lru_vec"], S, n + "lru_fwd")
    y_att = attn_fwd(u_att, c_["sinks"], c_["rel"], S, n + "attn_fwd")
    cc = conv_fwd(u_dn, W["dn_conv_w"][l], c_["dn_cb"], S, 0, 3 * DN_W, n + "dn_conv_fwd")
    q, k, v, g, beta = dn_point_fwd(cc, u_ba, c_["alog"], c_["dtb"], n + "dn_point_fwd")
    o, states, tinvs = dn_scan_fwd(q, k, v, g, beta, S, n + "dn_scan_fwd")
    y_dn = dn_gate_fwd(o, u_dn, c_["dn_nl"], n + "dn_gate_fwd")
    need(l, "rest", y_dn)
    h2, ycat = wout_fwd(h1, (y_lru, y_att, y_dn), W["r_rows"][l], n + "wout_fwd")
    h3, *ffn2_kept = ffn_fwd(h2, row(W["ffn2_norm"][l]), W["r_cols"][l], W["r_rows"][l], n + "ffn2_fwd")
    h4 = ple_fwd(h3, row(W["ple_norm"][l]), pe, W["r_rows"][l], W["ple_w_proj"][l], n + "ple_fwd")
    saved = dict(ffn1=ffn1_kept, ffn2=ffn2_kept, h0=h0, h1=h1, h2=h2, h3=h3, u_lru=u_lru, u_att=u_att, u_dn=u_dn,
                 u_ba=u_ba, xn_mix=xn_mix, xr=xr, cc=cc, q=q, k=k, v=v, g=g, beta=beta, o=o, states=states, tinvs=tinvs, ycat=ycat)
    return h4, saved


GM_WOUT, GM_PGATE, GM_WIN, GM_PPROJ, GM_END, GM_ROWS = 0, 128, 256, 560, 592, 640


def _layer_bwd(dh4, sv, pe, W, l, S, token=None, on_piece=None):
    n = f"l{l}_"
    c_ = _layer_consts(W, l)
    row = lambda v: v.reshape(1, -1)
    behind = lambda v, tok: v if tok is None else v + tok.astype(v.dtype)
    on_piece = on_piece or (lambda *_: None)
    G = {"mix_rows": jnp.zeros((N_DEV, GM_ROWS, D), BF16)}
    dh3, dz, dpp, xn_p, dn = ple_bwd(sv["h3"], dh4, behind(row(W["ple_norm"][l]), token), pe, W["r_rows"][l],
                                     W["ple_w_proj"][l], n + "ple_bwd")
    G["ple_norm"] = dn[0]
    G["mix_rows"] = grad_rows(xn_p, dz, G["mix_rows"], GM_PGATE, ROWS_DEV, n + "d_ple_w_gate")
    d_proj = matmul_tn(pe, dpp, n + "d_ple_w_proj")
    d_proj = d_proj.reshape(PLE, N_DEV, D // N_DEV).transpose(1, 0, 2).reshape(N_DEV, GM_END - GM_PPROJ, D)
    G["mix_rows"] = lax.dynamic_update_slice(G["mix_rows"], d_proj, (0, GM_PPROJ, 0))

    def ffn_back(which, cols_w, rows_w, h_in, dy, tok, one_by_one):
        gt, up, xn = sv[which]
        dh, dgt, dup, act, dn_ = ffn_bwd(h_in, dy, behind(row(W[which + "_norm"][l]), tok), gt, up, cols_w, rows_w,
                                         n + which + "_bwd")
        G[which + "_norm"] = dn_[0]
        zeros_rows = jnp.zeros((N_DEV, SHP, D), BF16)
        if one_by_one:
            G[which + "_gate"] = grad_cols(xn, dgt, jnp.zeros((1, D, FFP), BF16), 0, n + "d_" + which + "_w_gate")
            tok = on_piece(l, which + "_gate", (G[which + "_gate"],))
            G[which + "_up"] = grad_cols(xn, dup, behind(jnp.zeros((1, D, FFP), BF16), tok), 0,
                                         n + "d_" + which + "_w_up")
            tok = on_piece(l, which + "_up", (G[which + "_up"],))
            G[which + "_down"] = grad_rows(act, dy, behind(zeros_rows, tok), 0, SHP, n + "d_" + which + "_w_down",
                                           scale=0.5)
            return dh, on_piece(l, which + "_down", (G[which + "_down"],))
        cols = grad_cols(xn, dgt, jnp.zeros((2, D, FFP), BF16), 0, n + "d_" + which + "_w_gate")
        G[which + "_cols"] = grad_cols(xn, dup, cols, 1, n + "d_" + which + "_w_up")
        G[which + "_rows"] = grad_rows(act, dy, zeros_rows, 0, SHP, n + "d_" + which + "_w_down", scale=0.5)
        return dh, on_piece(l, which, (G[which + "_cols"], G[which + "_rows"]))

    dh2, tok = ffn_back("ffn2", W["r_cols"][l], W["r_rows"][l], sv["h2"], dh3, None, False)
    dy_lru, dy_att, dy_dn = wout_bwd(dh2, W["r_rows"][l], n + "wout_bwd")
    G["mix_rows"] = grad_rows(sv["ycat"], dh2, G["mix_rows"], GM_WOUT, ROWS_DEV, n + "d_w_out")
    do, dz_dn, dnn = dn_gate_bwd(sv["o"], sv["u_dn"], behind(c_["dn_nl"], tok), dy_dn, n + "dn_gate_bwd")
    dqkvgb = dn_scan_bwd(sv["q"], sv["k"], sv["v"], sv["g"], sv["beta"], sv["states"], sv["tinvs"], do, S,
                         n + "dn_scan_bwd")
    dcc, du_ba, dvec_dn = dn_point_bwd(sv["cc"], sv["u_ba"], c_["alog"], c_["dtb"], dqkvgb, n + "dn_point_bwd")
    dqkv, dwb_dn = conv_bwd(sv["u_dn"], dcc, W["dn_conv_w"][l], S, 0, 3 * DN_W, n + "dn_conv_bwd")
    du_dn = jnp.concatenate([dqkv, dz_dn], axis=1)
    G["dn_norm"] = dnn[0, 0:HD]
    G["dn_a_log"] = dvec_dn[0, 0:DN_H]
    G["dn_dt_bias"] = dvec_dn[1, 0:DN_H]
    G["dn_conv_w"] = dwb_dn[0:4]
    du_att, drel, dsk = attn_bwd(sv["u_att"], dy_att, c_["sinks"], c_["rel"], S, n + "attn_bwd")
    G["attn_sinks"] = dsk[:, 0]
    G["rel_bias"] = drel[:, 0:REL_BUCKETS].T
    dxr, dgt_lru, dwa, dwx, dvec = lru_bwd(sv["xr"], sv["u_lru"], dy_lru, c_["wa"], c_["wx"], c_["lru_vec"], S,
                                           n + "lru_bwd")
    dx_lru, dwb_lru = conv_bwd(sv["u_lru"], dxr, W["lru_conv_w"][l], S, 0, LRU_W, n + "lru_conv_bwd")
    du_lru = jnp.concatenate([dx_lru, dgt_lru], axis=1)
    diag = lambda m: jnp.stack([m[c, HD * e:HD * (e + 1), HD * e:HD * (e + 1)] for c in range(2) for e in range(2)])
    G["lru_w_a"], G["lru_w_x"] = diag(dwa), diag(dwx)
    G["lru_b_a"], G["lru_b_x"], G["lru_lambda"] = dvec[0], dvec[1], dvec[2]
    G["lru_conv_w"], G["lru_conv_b"] = dwb_lru[0:4], dwb_lru[4]
    dh1, du_cat, dn = mixin_bwd(sv["h1"], dh2, row(W["mix_norm"][l]), W["w_in"][l],
                                (du_lru, du_att, du_dn, du_ba), n + "mixin_bwd")
    G["mix_norm"] = dn[0]
    d_in = matmul_tn(sv["xn_mix"], du_cat, n + "d_w_in")[:, :D_IN]
    d_in = d_in.reshape(D, N_DEV, D_IN // N_DEV).transpose(1, 0, 2).reshape(N_DEV, WIN_ROWS, D)
    d_in = jnp.pad(d_in, ((0, 0), (0, GM_PPROJ - GM_WIN - WIN_ROWS), (0, 0)))
    G["mix_rows"] = lax.dynamic_update_slice(G["mix_rows"], d_in, (0, GM_WIN, 0))
    tok = on_piece(l, "mix", (G["mix_rows"],))
    dh0, tok = ffn_back("ffn1", W["f1_cols"][l], W["f1_rows"][l], sv["h0"], dh1, tok, l == 0)
    return dh0, G, tok


def _core(x, pe, W, target, S, need=None, on_piece=None):
    h = x
    saved = []
    for l in range(DEPTH):
        h, sv = _layer_fwd(h, pe[l], W, l, S, need)
        saved.append(sv)
    loss_tile, dh, dfn = loss_head(h, W["final_norm"].reshape(1, -1), target, "loss_head")
    grads = [None] * DEPTH
    token = None
    for l in reversed(range(DEPTH)):
        dh, grads[l], token = _layer_bwd(dh, saved[l], pe[l], W, l, S, token, on_piece)
    return loss_tile[0, 0], dh, grads, dfn[0]


MESH_ID = pl.DeviceIdType.MESH
ANY_SPEC = pl.BlockSpec(memory_space=pl.ANY)
AXES = ("x", "y", "c")


def _my_pos():
    return lax.axis_index("x"), lax.axis_index("y"), lax.axis_index("c")


def _slot_of(px, py, pc):
    return 4 * px + 2 * py + pc


def all_gather(x, name):
    R, C = x.shape

    def body(x_ref, out_ref, send_sems, recv_sems, local_sem):
        mx, my, mc = _my_pos()
        me, sibling = (mx, my, mc), (mx, my, 1 - mc)
        chips = [(1 - mx, my), (mx, 1 - my), (1 - mx, 1 - my)]

        def copy(k, block, to, src=None):
            dst = out_ref.at[_slot_of(*block)]
            return pltpu.make_async_remote_copy(
                src_ref=dst if src is None else src, dst_ref=dst,
                send_sem=send_sems.at[k], recv_sem=recv_sems.at[k],
                device_id=to, device_id_type=MESH_ID)

        mine = pltpu.make_async_copy(x_ref, out_ref.at[_slot_of(*me)], local_sem)
        mine.start()
        first = [copy(0, me, sibling, src=x_ref)]
        first += [copy(1 + j, me, (*chip, mc), src=x_ref) for j, chip in enumerate(chips)]
        for cp in first:
            cp.start()
        passed = [copy(4 + j, (*chip, mc), sibling) for j, chip in enumerate(chips)]
        for j, chip in enumerate(chips):
            copy(1 + j, (*chip, mc), me).wait_recv()
            passed[j].start()
        copy(0, sibling, me).wait_recv()
        for j, chip in enumerate(chips):
            copy(4 + j, (*chip, 1 - mc), me).wait_recv()
        for cp in first + passed:
            cp.wait_send()
        mine.wait()

    return pl.pallas_call(
        body, name=name,
        out_shape=jax.ShapeDtypeStruct((N_DEV, R, C), x.dtype),
        in_specs=[ANY_SPEC], out_specs=ANY_SPEC,
        scratch_shapes=[pltpu.SemaphoreType.DMA((7,)), pltpu.SemaphoreType.DMA((7,)), pltpu.SemaphoreType.DMA],
    )(x)


def _col_window(ref, slot):
    return ref.at[:, pl.ds(pl.multiple_of(slot * SHP, LANE), SHP)]


def gather_layer(a_sh, b_sh, name):
    def body(a_ref, b_ref, ao_ref, bo_ref, send_sems, recv_sems, local_sems):
        mx, my, mc = _my_pos()
        me, sibling = (mx, my, mc), (mx, my, 1 - mc)
        chips = [(1 - mx, my), (mx, 1 - my), (1 - mx, 1 - my)]

        def copies(k, block, to, own=False):
            slot = _slot_of(*block)
            dsts = (_col_window(ao_ref, slot), bo_ref.at[slot])
            srcs = (a_ref, b_ref) if own else dsts
            return [pltpu.make_async_remote_copy(
                src_ref=s, dst_ref=d, send_sem=send_sems.at[2 * k + i], recv_sem=recv_sems.at[2 * k + i],
                device_id=to, device_id_type=MESH_ID) for i, (s, d) in enumerate(zip(srcs, dsts))]

        mine = [pltpu.make_async_copy(a_ref, _col_window(ao_ref, _slot_of(*me)), local_sems.at[0]),
                pltpu.make_async_copy(b_ref, bo_ref.at[_slot_of(*me)], local_sems.at[1])]
        for cp in mine:
            cp.start()
        first = copies(0, me, sibling, own=True)
        for j, chip in enumerate(chips):
            first += copies(1 + j, me, (*chip, mc), own=True)
        for cp in first:
            cp.start()
        passed = []
        for j, chip in enumerate(chips):
            for cp in copies(1 + j, (*chip, mc), me):
                cp.wait_recv()
            fwd = copies(4 + j, (*chip, mc), sibling)
            for cp in fwd:
                cp.start()
            passed += fwd
        for cp in copies(0, sibling, me):
            cp.wait_recv()
        for j, chip in enumerate(chips):
            for cp in copies(4 + j, (*chip, 1 - mc), me):
                cp.wait_recv()
        for cp in first + passed:
            cp.wait_send()
        for cp in mine:
            cp.wait()

    return pl.pallas_call(
        body, name=name,
        out_shape=[jax.ShapeDtypeStruct((a_sh.shape[0], FFP), a_sh.dtype),
                   jax.ShapeDtypeStruct((N_DEV,) + b_sh.shape, b_sh.dtype)],
        in_specs=[ANY_SPEC, ANY_SPEC], out_specs=[ANY_SPEC, ANY_SPEC],
        scratch_shapes=[pltpu.SemaphoreType.DMA((14,)), pltpu.SemaphoreType.DMA((14,)),
                        pltpu.SemaphoreType.DMA((2,))],
    )(a_sh, b_sh)


HBM_SPEC = pl.BlockSpec(memory_space=pltpu.HBM)
SEM_SPEC = pl.BlockSpec(memory_space=pltpu.SEMAPHORE)
SPLIT_EFFECT = pltpu.CompilerParams(has_side_effects=pltpu.SideEffectType.DATAFLOW_SIDE_EFFECTING)


def _split_ends(mode, src_ref, dst_ref, src_slot, dst_slot):
    cols = mode.endswith("cols")
    if mode.startswith("gather"):
        return src_ref, (_col_window(dst_ref, dst_slot) if cols else dst_ref.at[dst_slot])
    return (_col_window(src_ref, src_slot) if cols else src_ref.at[src_slot]), dst_ref.at[dst_slot]


def _split_peers():
    mx, my, mc = _my_pos()
    for r in range(1, N_DEV):
        peer = (1 - mx if r & 4 else mx, 1 - my if r & 2 else my, 1 - mc if r & 1 else mc)
        yield r - 1, peer, _slot_of(*peer)


def split_start(modes, srcs, dsts, name):
    n = len(modes)

    def body(*refs):
        send_sems, recv_sems, token = refs[2 * n], refs[2 * n + 1], refs[-1]
        mine = _slot_of(*_my_pos())
        for k, peer, ps in _split_peers():
            for i in range(n):
                src, dst = _split_ends(modes[i], refs[i], refs[n + i], ps, mine)
                pltpu.make_async_remote_copy(
                    src_ref=src, dst_ref=dst, send_sem=send_sems.at[n * k + i], recv_sem=recv_sems.at[n * k + i],
                    device_id=peer, device_id_type=MESH_ID).start()
        for i in range(n):
            src, dst = _split_ends(modes[i], refs[i], refs[n + i], mine, mine)
            pltpu.make_async_copy(src, dst, recv_sems.at[n * (N_DEV - 1) + i]).start()
        token[...] = jnp.zeros_like(token)

    bufs = tuple(srcs) + tuple(dsts)
    sems = pltpu.SemaphoreType.DMA((n * N_DEV,))
    res = pl.pallas_call(
        body, name=name,
        out_shape=(sems, sems) + tuple(pltpu.HBM(t.shape, t.dtype) for t in bufs)
        + (jax.ShapeDtypeStruct((8, LANE), F32),),
        in_specs=[HBM_SPEC] * (2 * n),
        out_specs=(SEM_SPEC, SEM_SPEC) + (HBM_SPEC,) * (2 * n) + (pl.BlockSpec(memory_space=pltpu.VMEM),),
        input_output_aliases={i: 2 + i for i in range(2 * n)},
        compiler_params=SPLIT_EFFECT,
    )(*(pltpu.with_memory_space_constraint(t, pltpu.HBM) for t in bufs))
    return list(res[:-1]), res[-1][0, 0]


def split_wait(modes, started, after, name):
    n = len(modes)
    send_sems, recv_sems, bufs = started[0], started[1], started[2:]

    def body(*refs):
        send_sems, recv_sems = refs[2 * n], refs[2 * n + 1]
        mine = _slot_of(*_my_pos())
        for k, peer, ps in _split_peers():
            for i in range(n):
                sent = _split_ends(modes[i], refs[i], refs[n + i], ps, mine)[0]
                landed = _split_ends(modes[i], refs[i], refs[n + i], mine, ps)[1]
                cp = pltpu.make_async_remote_copy(
                    src_ref=sent, dst_ref=landed, send_sem=send_sems.at[n * k + i],
                    recv_sem=recv_sems.at[n * k + i], device_id=peer, device_id_type=MESH_ID)
                cp.wait_send()
                cp.wait_recv()
        for i in range(n):
            src, dst = _split_ends(modes[i], refs[i], refs[n + i], mine, mine)
            pltpu.make_async_copy(src, dst, recv_sems.at[n * (N_DEV - 1) + i]).wait()

    res = pl.pallas_call(
        body, name=name,
        out_shape=tuple(pltpu.HBM(t.shape, t.dtype) for t in bufs),
        in_specs=[HBM_SPEC] * (2 * n) + [SEM_SPEC, SEM_SPEC, pl.BlockSpec(memory_space=pl.ANY)],
        out_specs=(HBM_SPEC,) * (2 * n),
        input_output_aliases={i: i for i in range(2 * n)},
        compiler_params=SPLIT_EFFECT,
    )(*bufs, send_sems, recv_sems, after)
    return list(res[n:])


def sum_parts(parts, name):
    _, R, C = parts.shape
    tr = _pick(R, (512, 336, 272, 256, 128, 64, 32, 16, 8))

    def body(p_ref, o_ref):
        acc = p_ref[0].astype(F32)
        for k in range(1, N_DEV):
            acc += p_ref[k].astype(F32)
        o_ref[...] = acc

    return pl.pallas_call(
        body, name=name, grid=(R // tr,),
        in_specs=[pl.BlockSpec((N_DEV, tr, C), lambda i: (0, i, 0))],
        out_specs=pl.BlockSpec((tr, C), lambda i: (i, 0)),
        out_shape=jax.ShapeDtypeStruct((R, C), F32),
        compiler_params=_cp("parallel"),
    )(parts)


def adamw(g, w, m, v, name):
    R, C = g.shape
    tr = _pick(R, (512, 352, 256, 128, 64, 32, 16, 8))
    c1 = 1.0 - ADAM_B1 ** ADAM_STEP
    c2 = 1.0 - ADAM_B2 ** ADAM_STEP

    def body(g_ref, w_ref, m_ref, v_ref, d_ref, nm_ref, nv_ref):
        gg = g_ref[...]
        mm = ADAM_B1 * m_ref[...] + (1.0 - ADAM_B1) * gg
        vv = ADAM_B2 * v_ref[...] + (1.0 - ADAM_B2) * (gg * gg)
        nm_ref[...] = mm
        nv_ref[...] = vv
        d_ref[...] = -ADAM_LR * ((mm / c1) / (jnp.sqrt(vv / c2) + ADAM_EPS) + ADAM_WD * w_ref[...])

    spec = pl.BlockSpec((tr, C), lambda i: (i, 0))
    return pl.pallas_call(
        body, name=name, grid=(R // tr,),
        in_specs=[spec] * 4, out_specs=[spec] * 3,
        out_shape=[jax.ShapeDtypeStruct((R, C), F32)] * 3,
        compiler_params=_cp("parallel"),
    )(g, w, m, v)


BIG = (("ffn1_w_gate", 1, D, FF), ("ffn1_w_up", 1, D, FF), ("ffn1_w_down", 0, FF, D),
       ("w_in", 1, D, D_IN), ("w_out", 0, D, D),
       ("ffn2_w_gate", 1, D, FF), ("ffn2_w_up", 1, D, FF), ("ffn2_w_down", 0, FF, D),
       ("ple_w_gate", 0, D, D), ("ple_w_proj", 1, PLE, D))
SMALL = (("ffn1_norm", (D,), None), ("mix_norm", (D,), None), ("lru_conv_w", (4, LRU_W), LRU_W // N_DEV),
         ("lru_conv_b", (LRU_W,), None), ("lru_w_a", (4, HD, HD), None), ("lru_b_a", (LRU_W,), None),
         ("lru_w_x", (4, HD, HD), None), ("lru_b_x", (LRU_W,), None), ("lru_lambda", (LRU_W,), None),
         ("attn_sinks", (ATT_H,), None), ("dn_conv_w", (4, 3 * DN_W), 3 * DN_W // N_DEV),
         ("dn_a_log", (DN_H,), None), ("dn_dt_bias", (DN_H,), None), ("dn_norm", (HD,), None),
         ("ffn2_norm", (D,), None), ("ple_norm", (D,), None))
SINGLE = (("rel_bias", (REL_BUCKETS, ATT_H)), ("final_norm", (D,)))


def _pack_rows(arrs, width, mult):
    flat = jnp.concatenate([a.reshape(-1) for a in arrs])
    rows = -(-flat.shape[0] // (width * mult)) * mult
    return jnp.pad(flat, (0, rows * width - flat.shape[0])).reshape(rows, width)


def _unpack_rows(packed, shapes):
    flat = packed.reshape(-1)
    out, off = [], 0
    for s in shapes:
        n = int(np.prod(s))
        out.append(flat[off:off + n].reshape(s))
        off += n
    return out


def _pad_rows(w, r):
    return jnp.pad(w, ((0, r - w.shape[0]), (0, 0)))


def _shard_ffn(a, l, which, more=()):
    cols = jnp.concatenate([a[which + "_w_gate"][l], a[which + "_w_up"][l]], axis=0)
    rows = jnp.concatenate([_pad_rows(a[which + "_w_down"][l], SHP)] + list(more), axis=0)
    return jnp.pad(cols, ((0, 0), (0, SHP - SH))).astype(BF16), rows.astype(BF16)


def _shards(a, l):
    w_in_rows = _pad_rows(a["w_in"][l].reshape(WIN_ROWS, D), IN_ROWS).astype(BF16)
    rest = _shard_ffn(a, l, "ffn2", (a["w_out"][l], a["ple_w_gate"][l], a["ple_w_proj"][l].reshape(-1, D)))
    return _shard_ffn(a, l, "ffn1"), (w_in_rows,), rest


def _full_w_in(in_rows):
    sh = in_rows[:, :WIN_ROWS, :].reshape(N_DEV, D, D_IN // N_DEV)
    return jnp.pad(sh.transpose(1, 0, 2).reshape(D, D_IN), ((0, 0), (0, D_IN_PAD - D_IN)))


def _full_ple_proj(r_rows):
    sh = r_rows[:, R_PPROJ:R_ROWS, :].reshape(N_DEV, PLE, D // N_DEV)
    return sh.transpose(1, 0, 2).reshape(PLE, D)


PIECE_NAMES = {"ffn1": ("ffn1_w_gate", "ffn1_w_up", "ffn1_w_down"), "ffn2": ("ffn2_w_gate", "ffn2_w_up", "ffn2_w_down"),
               "mix": ("w_out", "ple_w_gate", "w_in", "ple_w_proj")}


def _shard_grads(piece, summed):
    if piece == "mix":
        rows, = summed
        return {"w_out": rows[GM_WOUT:GM_WOUT + ROWS_DEV], "ple_w_gate": rows[GM_PGATE:GM_PGATE + ROWS_DEV],
                "w_in": rows[GM_WIN:GM_WIN + WIN_ROWS].reshape(D, D_IN // N_DEV),
                "ple_w_proj": rows[GM_PPROJ:GM_END].reshape(PLE, D // N_DEV)}
    if piece in ("ffn1_gate", "ffn1_up"):
        return {piece.replace("_", "_w_"): summed[0][:, :SH]}
    if piece == "ffn1_down":
        return {"ffn1_w_down": summed[0][:SH]}
    cols, rows = summed
    return {piece + "_w_gate": cols[:D, :SH], piece + "_w_up": cols[D:, :SH], piece + "_w_down": rows[:SH]}


def kernel(x, p, ffn1_norm, ffn1_w_gate, ffn1_w_up, ffn1_w_down, mix_norm, w_in, lru_conv_w, lru_conv_b, lru_w_a, lru_b_a, lru_w_x, lru_b_x, lru_lambda, attn_sinks, rel_bias, dn_conv_w, dn_a_log, dn_dt_bias, dn_norm, w_out, ffn2_norm, ffn2_w_gate, ffn2_w_up, ffn2_w_down, ple_norm, ple_w_gate, ple_w_proj, final_norm, loss_target, m_ffn1_norm, m_ffn1_w_gate, m_ffn1_w_up, m_ffn1_w_down, m_mix_norm, m_w_in, m_lru_conv_w, m_lru_conv_b, m_lru_w_a, m_lru_b_a, m_lru_w_x, m_lru_b_x, m_lru_lambda, m_attn_sinks, m_rel_bias, m_dn_conv_w, m_dn_a_log, m_dn_dt_bias, m_dn_norm, m_w_out, m_ffn2_norm, m_ffn2_w_gate, m_ffn2_w_up, m_ffn2_w_down, m_ple_norm, m_ple_w_gate, m_ple_w_proj, m_final_norm, v_ffn1_norm, v_ffn1_w_gate, v_ffn1_w_up, v_ffn1_w_down, v_mix_norm, v_w_in, v_lru_conv_w, v_lru_conv_b, v_lru_w_a, v_lru_b_a, v_lru_w_x, v_lru_b_x, v_lru_lambda, v_attn_sinks, v_rel_bias, v_dn_conv_w, v_dn_a_log, v_dn_dt_bias, v_dn_norm, v_w_out, v_ffn2_norm, v_ffn2_w_gate, v_ffn2_w_up, v_ffn2_w_down, v_ple_norm, v_ple_w_gate, v_ple_w_proj, v_final_norm):
    a = dict(locals())
    nb, S, _ = x.shape
    T = nb * S
    my_slot = _slot_of(*_my_pos())

    W = {k: [None] * DEPTH for k in ("f1_cols", "f1_rows", "w_in", "r_cols", "r_rows", "ple_w_proj")}
    GATHER, SCATTER = ("gather_cols", "gather_block"), ("scatter_cols", "scatter_block")
    GROUP_MODES = {"f1": GATHER, "in": GATHER[1:], "rest": GATHER}

    def set_group(l, group, bufs):
        if group == "f1":
            W["f1_cols"][l], W["f1_rows"][l] = bufs
        elif group == "in":
            W["w_in"][l] = _full_w_in(bufs[0])
        else:
            W["r_cols"][l], W["r_rows"][l] = bufs
            W["ple_w_proj"][l] = _full_ple_proj(bufs[1])

    def landing(mode, src):
        if mode == "gather_cols":
            return lax.empty((src.shape[0], FFP), src.dtype)
        if mode == "scatter_cols":
            return lax.empty((N_DEV, src.shape[0], SHP), src.dtype)
        return lax.empty((N_DEV,) + src.shape[mode == "scatter_block":], src.dtype)

    def start(modes, srcs, name):
        return split_start(modes, srcs, [landing(m, s) for m, s in zip(modes, srcs)], name)

    shards0, shards1 = _shards(a, 0), _shards(a, 1)
    set_group(0, "f1", gather_layer(*shards0[0], "gather_weights_l0_ffn1"))
    taps = all_gather(_pack_rows([lru_conv_w, dn_conv_w], LANE, 8), "gather_conv_taps")
    tap_shapes = [lru_conv_w.shape, dn_conv_w.shape]
    lcw, dcw = zip(*[_unpack_rows(taps[k], tap_shapes) for k in range(N_DEV)])
    W["lru_conv_w"] = jnp.concatenate(lcw, axis=-1)
    W["dn_conv_w"] = jnp.concatenate(dcw, axis=-1)
    for name, _, cols in SMALL:
        if cols is None:
            W[name] = a[name]
    W["rel_bias"], W["final_norm"] = rel_bias, final_norm

    ALL_MODES = GROUP_MODES["f1"] + GROUP_MODES["in"] + GROUP_MODES["rest"]
    gathers, tied = {}, (W["f1_rows"][0], taps)
    for l, group, modes, srcs in ((0, "in", GROUP_MODES["in"], shards0[1]), (0, "rest", GROUP_MODES["rest"], shards0[2]),
                                  (1, "all", ALL_MODES, shards1[0] + shards1[1] + shards1[2])):
        srcs = lax.optimization_barrier(tuple(srcs) + tied)[:len(srcs)]
        gathers[l, group], token = start(modes, srcs, f"gather_start_l{l}_{group}")
        tied = (token,)
    W["ffn1_norm"] = ffn1_norm + token
    flight = {}

    def need(l, group, h):
        if l == 0 and group != "f1":
            set_group(0, group, split_wait(GROUP_MODES[group], gathers[0, group], h, f"gather_wait_l0_{group}"))
        if l == 1 and group == "f1":
            bufs = split_wait(ALL_MODES, gathers[1, "all"], h, "gather_wait_l1_all")
            set_group(1, "f1", bufs[0:2])
            set_group(1, "in", bufs[2:3])
            set_group(1, "rest", bufs[3:5])

    def piece_modes(piece):
        return {"mix": SCATTER[1:], "ffn1_gate": SCATTER[:1], "ffn1_up": SCATTER[:1], "ffn1_down": SCATTER[1:]}.get(
            piece, SCATTER)

    def on_piece(l, piece, bufs):
        bufs = [b.reshape(-1, FFP) if b.shape[-1] == FFP else b for b in bufs]
        flight[l, piece], token = start(piece_modes(piece), bufs, f"exchange_start_l{l}_{piece}")
        return token

    loss_local, dx, grads, d_final = _core(x.reshape(T, D), p.reshape(DEPTH, T, PLE), W,
                                           loss_target.reshape(T, D), S, need, on_piece)
    loss = lax.psum(loss_local, AXES)

    small_full = [jnp.stack([grads[l][name] for l in range(DEPTH)]) for name, _, _ in SMALL]
    small_full += [grads[0]["rel_bias"] + grads[1]["rel_bias"], d_final]
    small_flight, _ = start(("gather_block",), (_pack_rows(small_full, LANE, 8),), "gather_start_small_grads")

    out = {}

    shards = [{} for _ in range(DEPTH)]

    def land(l, piece, after):
        parts = split_wait(piece_modes(piece), flight[l, piece], after, f"exchange_wait_l{l}_{piece}")
        shards[l].update(_shard_grads(piece, [sum_parts(t, f"sum_grads_l{l}_{piece}_{i}")
                                              for i, t in enumerate(parts)]))

    def update(piece):
        for name in PIECE_NAMES[piece]:
            g = jnp.stack([shards[l][name] for l in range(DEPTH)])
            shape = a[name].shape
            two_d = lambda t: t.reshape(-1, shape[-1])
            res = adamw(two_d(g), two_d(a[name]), two_d(a["m_" + name]), two_d(a["v_" + name]), "adamw_" + name)
            out[name] = (g,) + tuple(r.reshape(shape) for r in res)

    for piece in ("ffn2", "mix", "ffn1"):
        land(1, piece, dx)
    summed1 = lax.optimization_barrier(tuple(shards[1][PIECE_NAMES[piece][0]] for piece in PIECE_NAMES))
    land(0, "ffn2", summed1[0])
    land(0, "mix", summed1[1])
    update("ffn2")
    update("mix")
    done_early = lax.optimization_barrier(tuple(out[n][1] for n in PIECE_NAMES["ffn2"] + PIECE_NAMES["mix"]))
    small_parts, = split_wait(("gather_block",), small_flight, done_early[0], "gather_wait_small_grads")
    small_sum = sum_parts(small_parts, "sum_small_grads")
    g_small = dict(zip([n for n, _, _ in SMALL] + [n for n, _ in SINGLE],
                       _unpack_rows(small_sum, [s.shape for s in small_full])))
    for name, _, cols in SMALL:
        if cols is not None:
            g_small[name] = lax.dynamic_slice_in_dim(g_small[name], my_slot * cols, cols, axis=2)

    small_names = [n for n, _, _ in SMALL] + [n for n, _ in SINGLE]
    shapes = [a[n].shape for n in small_names]
    packed = [_pack_rows([a[pre + n] if pre is not None else g_small[n] for n in small_names], LANE, 8)
              for pre in (None, "", "m_", "v_")]
    res = adamw(*packed, "adamw_small")
    unpacked = [_unpack_rows(r, shapes) for r in res]
    for i, n in enumerate(small_names):
        out[n] = (g_small[n].reshape(shapes[i]),) + tuple(u[i] for u in unpacked)

    after = lax.optimization_barrier((res[0], done_early[1]))[0]
    for piece in ("ffn1_gate", "ffn1_up", "ffn1_down"):
        land(0, piece, after)
    update("ffn1")

    order = ['ffn1_norm', 'ffn1_w_gate', 'ffn1_w_up', 'ffn1_w_down', 'mix_norm', 'w_in', 'lru_conv_w', 'lru_conv_b',
             'lru_w_a', 'lru_b_a', 'lru_w_x', 'lru_b_x', 'lru_lambda', 'attn_sinks', 'rel_bias', 'dn_conv_w',
             'dn_a_log', 'dn_dt_bias', 'dn_norm', 'w_out', 'ffn2_norm', 'ffn2_w_gate', 'ffn2_w_up', 'ffn2_w_down',
             'ple_norm', 'ple_w_gate', 'ple_w_proj', 'final_norm']
    return (loss, dx.reshape(x.shape)) + tuple(out[n][k] for k in range(4) for n in order)
```

```python
import functools
import math

import numpy as np
import jax
import jax.numpy as jnp
from jax import lax
from jax.experimental import pallas as pl
from jax.experimental.pallas import tpu as pltpu

F32 = jnp.float32
BF16 = jnp.bfloat16
HI = lax.Precision.HIGHEST

D = 1024
DEPTH = 2
EPS = 1e-6
PLE = 256
FF = 2816
HD = 64
LRU_W = 256
LRU_C = 8.0
ATT_W = 512
ATT_H = 8
ATT_KV = 2
ATT_G = 4
KV_W = 128
WINDOW = 128
BQ = 128
REL_BUCKETS = 32
REL_MAX_DIST = 128
DN_W = 256
DN_H = 4
CHUNK = 64
D_IN = 2312
D_IN_PAD = 2432
N_DEV = 8

ADAM_LR = 0.001
ADAM_B1 = 0.9
ADAM_B2 = 0.999
ADAM_EPS = 1e-08
ADAM_WD = 0.01
ADAM_STEP = 10

LANE = 128
VMEM_LIMIT = 56 * 1024 * 1024
SH = FF // N_DEV
SHP = 384
FFP = N_DEV * SHP
FF_TILE = 2 * SHP
TOK_TILE = 512
R_DOWN2, R_WOUT, R_PGATE, R_PPROJ, R_ROWS = 0, 384, 512, 640, 672
WIN_ROWS = D * D_IN // N_DEV // 1024
IN_ROWS = 304
NEG = -1e30


def _cp(*sem):
    return pltpu.CompilerParams(dimension_semantics=tuple(sem), vmem_limit_bytes=VMEM_LIMIT)


def _dg(a, b, ca, cb, exact):
    dims = (((ca,), (cb,)), ((), ()))
    if exact == "f32":
        return lax.dot_general(a.astype(F32), b.astype(F32), dims, precision=HI, preferred_element_type=F32)
    if exact == "split":
        a_hi, b_hi = a.astype(BF16), b.astype(BF16)
        a_lo = (a - a_hi.astype(F32)).astype(BF16)
        b_lo = (b - b_hi.astype(F32)).astype(BF16)
        dot = lambda u, v: lax.dot_general(u, v, dims, preferred_element_type=F32)
        return dot(a_hi, b_hi) + (dot(a_hi, b_lo) + dot(a_lo, b_hi))
    return lax.dot_general(a.astype(BF16), b.astype(BF16), dims, preferred_element_type=F32)


def _make_mm(exact):
    @jax.custom_vjp
    def mm(a, b):
        return _dg(a, b, 1, 0, exact)

    @jax.custom_vjp
    def mm_nt(a, b):
        return _dg(a, b, 1, 1, exact)

    @jax.custom_vjp
    def mm_tn(a, b):
        return _dg(a, b, 0, 0, exact)

    mm.defvjp(lambda a, b: (mm(a, b), (a, b)),
              lambda r, d: (mm_nt(d, r[1]), mm_tn(r[0], d)))
    mm_nt.defvjp(lambda a, b: (mm_nt(a, b), (a, b)),
                 lambda r, d: (mm(d, r[1]), mm_tn(d, r[0])))
    mm_tn.defvjp(lambda a, b: (mm_tn(a, b), (a, b)),
                 lambda r, d: (mm_nt(r[1], d), mm(r[0], d)))
    return mm, mm_nt, mm_tn


_mm, _mm_nt, _mm_tn = _make_mm("bf16")
_mmx, _mmx_nt, _mmx_tn = _make_mm("f32")
_mm3, _mm3_nt, _mm3_tn = _make_mm("split")


def _iota(shape, dim):
    return lax.broadcasted_iota(jnp.int32, shape, dim)


def _sigmoid(x):
    return 1.0 / (1.0 + jnp.exp(-x))


def _rms(h, g):
    rstd = lax.rsqrt(jnp.mean(h * h, axis=-1, keepdims=True) + EPS)
    xhat = h * rstd
    return xhat * g, xhat, rstd


def _rms_bwd(dxn, xhat, rstd, g):
    dxhat = dxn * g
    dh = rstd * (dxhat - xhat * jnp.mean(dxhat * xhat, axis=-1, keepdims=True))
    dg = jnp.sum(dxn * xhat, axis=0, keepdims=True)
    return dh, dg


def _row_spec(tm, n):
    return pl.BlockSpec((tm, n), lambda i, *_: (i, 0))


def _full_spec(shape):
    nd = len(shape)
    return pl.BlockSpec(shape, lambda *_: (0,) * nd)


def _ffn_weight_specs():
    return [pl.BlockSpec((D, FF_TILE), lambda i, j: (0, j)),
            pl.BlockSpec((D, FF_TILE), lambda i, j: (1, j)),
            pl.BlockSpec((2, SHP, D), lambda i, j: (j, 0, 0))]


def ffn_fwd(h, g, wa, wb, name):
    T = h.shape[0]
    tm = min(TOK_TILE, T)
    nj = FFP // FF_TILE

    def body(h_ref, g_ref, wg_ref, wu_ref, wd_ref, o_ref, gt_ref, up_ref, xn_ref):
        j = pl.program_id(1)

        @pl.when(j == 0)
        def _():
            hh = h_ref[...]
            xn_ref[...] = _rms(hh, g_ref[...])[0].astype(BF16)
            o_ref[...] = hh

        xn = xn_ref[...]
        gt = _mm(xn, wg_ref[...])
        up = _mm(xn, wu_ref[...])
        gt_ref[...] = gt.astype(BF16)
        up_ref[...] = up.astype(BF16)
        act = gt * _sigmoid(gt) * up
        o_ref[...] += 0.5 * _mm(act, wd_ref[...].reshape(FF_TILE, D))

    tile = pl.BlockSpec((tm, FF_TILE), lambda i, j: (i, j))
    return pl.pallas_call(
        body, name=name, grid=(T // tm, nj),
        in_specs=[pl.BlockSpec((tm, D), lambda i, j: (i, 0)),
                  pl.BlockSpec((1, D), lambda i, j: (0, 0))] + _ffn_weight_specs(),
        out_specs=[pl.BlockSpec((tm, D), lambda i, j: (i, 0)), tile, tile,
                   pl.BlockSpec((tm, D), lambda i, j: (i, 0))],
        out_shape=[jax.ShapeDtypeStruct((T, D), F32), jax.ShapeDtypeStruct((T, FFP), BF16),
                   jax.ShapeDtypeStruct((T, FFP), BF16), jax.ShapeDtypeStruct((T, D), BF16)],
        compiler_params=_cp("parallel", "arbitrary"),
    )(h, g, wa, wa, wb)


def ffn_bwd(h, dy, g, gt_saved, up_saved, wa, wb, name):
    T = h.shape[0]
    tm = min(TOK_TILE, T)
    nj = FFP // FF_TILE

    def body(h_ref, dy_ref, g_ref, gt_ref, up_ref, wg_ref, wu_ref, wd_ref,
             dh_ref, dg_ref, du_ref, a_ref, dn_ref, dxn_s):
        i = pl.program_id(0)
        j = pl.program_id(1)

        @pl.when(j == 0)
        def _():
            dxn_s[...] = jnp.zeros_like(dxn_s)

        @pl.when((i == 0) & (j == 0))
        def _():
            dn_ref[...] = jnp.zeros_like(dn_ref)

        gt = gt_ref[...].astype(F32)
        up = up_ref[...].astype(F32)
        sg = _sigmoid(gt)
        si = gt * sg
        da = _mm_nt(0.5 * dy_ref[...], wd_ref[...].reshape(FF_TILE, D))
        dup = da * si
        dgt = da * up * (sg * (1.0 + gt * (1.0 - sg)))
        dg_ref[...] = dgt.astype(BF16)
        du_ref[...] = dup.astype(BF16)
        a_ref[...] = (si * up).astype(BF16)
        dxn_s[...] += _mm_nt(dgt, wg_ref[...]) + _mm_nt(dup, wu_ref[...])

        @pl.when(j == nj - 1)
        def _():
            gg = g_ref[...]
            _, xhat, rstd = _rms(h_ref[...], gg)
            dh, dn = _rms_bwd(dxn_s[...], xhat, rstd, gg)
            dh_ref[...] = dy_ref[...] + dh
            dn_ref[...] += dn

    tile = pl.BlockSpec((tm, FF_TILE), lambda i, j: (i, j))
    return pl.pallas_call(
        body, name=name, grid=(T // tm, nj),
        in_specs=[pl.BlockSpec((tm, D), lambda i, j: (i, 0)),
                  pl.BlockSpec((tm, D), lambda i, j: (i, 0)),
                  pl.BlockSpec((1, D), lambda i, j: (0, 0)), tile, tile] + _ffn_weight_specs(),
        out_specs=[pl.BlockSpec((tm, D), lambda i, j: (i, 0)), tile, tile, tile,
                   pl.BlockSpec((1, D), lambda i, j: (0, 0))],
        out_shape=[jax.ShapeDtypeStruct((T, D), F32)] + [jax.ShapeDtypeStruct((T, FFP), BF16)] * 3
        + [jax.ShapeDtypeStruct((1, D), F32)],
        scratch_shapes=[pltpu.VMEM((tm, D), F32)],
        compiler_params=_cp("arbitrary", "arbitrary"),
    )(h, dy, g, gt_saved, up_saved, wa, wa, wb)


def _pick(n, prefs):
    for t in prefs:
        if n % t == 0:
            return t
    return n


def _tn_body(nk, scale, out_dtype, squeeze):
    def body(a_ref, b_ref, *rest):
        o_ref, acc = rest[-2], rest[-1]
        k = pl.program_id(2)

        @pl.when(k == 0)
        def _():
            acc[...] = jnp.zeros_like(acc)

        acc[...] += _mm_tn(a_ref[...], b_ref[...])

        @pl.when(k == nk - 1)
        def _():
            res = (scale * acc[...]).astype(out_dtype)
            if squeeze:
                o_ref[0] = res
            else:
                o_ref[...] = res

    return body


def matmul_tn(a, b, name, scale=1.0, out_dtype=BF16):
    T, M = a.shape
    N = b.shape[1]
    tmm = _pick(M, (512, 256))
    tnn = _pick(N, (1024, 2432))
    tk = min(TOK_TILE, T)
    nk = T // tk
    return pl.pallas_call(
        _tn_body(nk, scale, out_dtype, False), name=name, grid=(M // tmm, N // tnn, nk),
        in_specs=[pl.BlockSpec((tk, tmm), lambda i, j, k: (k, i)),
                  pl.BlockSpec((tk, tnn), lambda i, j, k: (k, j))],
        out_specs=pl.BlockSpec((tmm, tnn), lambda i, j, k: (i, j)),
        out_shape=jax.ShapeDtypeStruct((M, N), out_dtype),
        scratch_shapes=[pltpu.VMEM((tmm, tnn), F32)],
        compiler_params=_cp("parallel", "parallel", "arbitrary"),
    )(a, b)


def grad_cols(a, b, dst, slot, name):
    T = a.shape[0]
    tmm, tnn = D, FFP // 2
    tk = min(TOK_TILE, T)
    nk = T // tk
    return pl.pallas_call(
        _tn_body(nk, 1.0, BF16, True), name=name, grid=(D // tmm, FFP // tnn, nk),
        in_specs=[pl.BlockSpec((tk, tmm), lambda i, j, k: (k, i)),
                  pl.BlockSpec((tk, tnn), lambda i, j, k: (k, j)),
                  pl.BlockSpec(memory_space=pl.ANY)],
        out_specs=pl.BlockSpec((1, tmm, tnn), lambda i, j, k: (slot, i, j)),
        out_shape=jax.ShapeDtypeStruct(dst.shape, dst.dtype),
        scratch_shapes=[pltpu.VMEM((tmm, tnn), F32)],
        input_output_aliases={2: 0},
        compiler_params=_cp("parallel", "parallel", "arbitrary"),
    )(a, b, dst)


def grad_rows(a, b, dst, row0, rows, name, scale=1.0):
    T = a.shape[0]
    tk = min(TOK_TILE, T)
    nk = T // tk
    blk = row0 // rows

    def body(a_ref, b_ref, dst_ref, o_ref, acc):
        k = pl.program_id(0)

        @pl.when(k == 0)
        def _():
            acc[...] = jnp.zeros_like(acc)

        acc[...] += _mm_tn(a_ref[...], b_ref[...])

        @pl.when(k == nk - 1)
        def _():
            o_ref[...] = (scale * acc[...]).astype(BF16).reshape(N_DEV, rows, D)

    return pl.pallas_call(
        body, name=name, grid=(nk,),
        in_specs=[pl.BlockSpec((tk, N_DEV * rows), lambda k: (k, 0)),
                  pl.BlockSpec((tk, D), lambda k: (k, 0)),
                  pl.BlockSpec(memory_space=pl.ANY)],
        out_specs=pl.BlockSpec((N_DEV, rows, D), lambda k: (0, blk, 0)),
        out_shape=jax.ShapeDtypeStruct(dst.shape, dst.dtype),
        scratch_shapes=[pltpu.VMEM((N_DEV * rows, D), F32)],
        input_output_aliases={2: 0},
        compiler_params=_cp("arbitrary"),
    )(a, b, dst)


U_SPLITS = (512, 768, 1024, 128)
U_OFFS = (0, 512, 1280, 2304)


def mixin_fwd(h, g, w_in, name):
    T = h.shape[0]
    tm = min(TOK_TILE, T)

    def body(h_ref, g_ref, w_ref, u0, u1, u2, u3, xn_ref):
        xn = _rms(h_ref[...], g_ref[...])[0].astype(BF16)
        xn_ref[...] = xn
        u = _mm(xn, w_ref[...])
        for ref, off, n in zip((u0, u1, u2, u3), U_OFFS, U_SPLITS):
            ref[...] = u[:, off:off + n]

    return pl.pallas_call(
        body, name=name, grid=(T // tm,),
        in_specs=[_row_spec(tm, D), _full_spec((1, D)), _full_spec((D, D_IN_PAD))],
        out_specs=[_row_spec(tm, n) for n in U_SPLITS] + [_row_spec(tm, D)],
        out_shape=[jax.ShapeDtypeStruct((T, n), F32) for n in U_SPLITS]
        + [jax.ShapeDtypeStruct((T, D), BF16)],
        compiler_params=_cp("parallel"),
    )(h, g, w_in)


def mixin_bwd(h, dh_in, g, w_in, dus, name):
    T = h.shape[0]
    tm = min(TOK_TILE, T)

    def body(h_ref, dhi_ref, g_ref, w_ref, d0, d1, d2, d3, dh_ref, du_ref, dn_ref):
        @pl.when(pl.program_id(0) == 0)
        def _():
            dn_ref[...] = jnp.zeros_like(dn_ref)

        dxn = jnp.zeros((tm, D), F32)
        for ref, off, n in zip((d0, d1, d2, d3), U_OFFS, U_SPLITS):
            du = ref[...]
            du_ref[:, off:off + n] = du.astype(BF16)
            dxn += _mm_nt(du, w_ref[:, off:off + n])
        gg = g_ref[...]
        _, xhat, rstd = _rms(h_ref[...], gg)
        dh, dn = _rms_bwd(dxn, xhat, rstd, gg)
        dh_ref[...] = dhi_ref[...] + dh
        dn_ref[...] += dn

    return pl.pallas_call(
        body, name=name, grid=(T // tm,),
        in_specs=[_row_spec(tm, D), _row_spec(tm, D), _full_spec((1, D)), _full_spec((D, D_IN_PAD))]
        + [_row_spec(tm, n) for n in U_SPLITS],
        out_specs=[_row_spec(tm, D), _row_spec(tm, D_IN_PAD), _full_spec((1, D))],
        out_shape=[jax.ShapeDtypeStruct((T, D), F32), jax.ShapeDtypeStruct((T, D_IN_PAD), BF16),
                   jax.ShapeDtypeStruct((1, D), F32)],
        compiler_params=_cp("arbitrary"),
    )(h, dh_in, g, w_in, *dus)


def _shift_down(x, s, row):
    if s == 0:
        return x
    return jnp.where(row >= s, pltpu.roll(x, s, 0), 0.0)


def _shift_up(x, s, row):
    if s == 0:
        return x
    n = x.shape[0]
    return jnp.where(row < n - s, pltpu.roll(x, n - s, 0), 0.0)


def conv_fwd(x, w, b, S, col0, C, name):
    T = x.shape[0]
    cb0 = col0 // LANE

    def body(x_ref, w_ref, b_ref, y_ref):
        xx = x_ref[...]
        row = _iota(xx.shape, 0)
        y = xx * w_ref[3:4, :] + b_ref[...]
        for k in range(3):
            y += _shift_down(xx, 3 - k, row) * w_ref[k:k + 1, :]
        y_ref[...] = y

    return pl.pallas_call(
        body, name=name, grid=(T // S, C // LANE),
        in_specs=[pl.BlockSpec((S, LANE), lambda s, c: (s, cb0 + c)),
                  pl.BlockSpec((4, LANE), lambda s, c: (0, c)),
                  pl.BlockSpec((1, LANE), lambda s, c: (0, c))],
        out_specs=pl.BlockSpec((S, LANE), lambda s, c: (s, c)),
        out_shape=jax.ShapeDtypeStruct((T, C), F32),
        compiler_params=_cp("parallel", "parallel"),
    )(x, w, b)


def conv_bwd(x, dy, w, S, col0, C, name):
    T = x.shape[0]
    cb0 = col0 // LANE

    def body(x_ref, dy_ref, w_ref, dx_ref, dwb_ref):
        @pl.when(pl.program_id(1) == 0)
        def _():
            dwb_ref[...] = jnp.zeros_like(dwb_ref)

        xx = x_ref[...]
        dd = dy_ref[...]
        row = _iota(xx.shape, 0)
        dx = dd * w_ref[3:4, :]
        for k in range(3):
            dx += _shift_up(dd, 3 - k, row) * w_ref[k:k + 1, :]
        dx_ref[...] = dx
        for k in range(4):
            dwb_ref[k:k + 1, :] += jnp.sum(dd * _shift_down(xx, 3 - k, row), axis=0, keepdims=True)
        dwb_ref[4:5, :] += jnp.sum(dd, axis=0, keepdims=True)

    return pl.pallas_call(
        body, name=name, grid=(C // LANE, T // S),
        in_specs=[pl.BlockSpec((S, LANE), lambda c, s: (s, cb0 + c)),
                  pl.BlockSpec((S, LANE), lambda c, s: (s, c)),
                  pl.BlockSpec((4, LANE), lambda c, s: (0, c))],
        out_specs=[pl.BlockSpec((S, LANE), lambda c, s: (s, c)),
                   pl.BlockSpec((8, LANE), lambda c, s: (0, c))],
        out_shape=[jax.ShapeDtypeStruct((T, C), F32), jax.ShapeDtypeStruct((8, C), F32)],
        compiler_params=_cp("parallel", "arbitrary"),
    )(x, dy, w)


def _scan(a, b, row):
    n = a.shape[0]
    d = 1
    while d < n:
        keep = row >= d
        b = a * jnp.where(keep, pltpu.roll(b, d, 0), 0.0) + b
        a = a * jnp.where(keep, pltpu.roll(a, d, 0), 1.0)
        d *= 2
    return b


def _rscan(a, b, row):
    n = a.shape[0]
    d = 1
    while d < n:
        keep = row < n - d
        b = a * jnp.where(keep, pltpu.roll(b, n - d, 0), 0.0) + b
        a = a * jnp.where(keep, pltpu.roll(a, n - d, 0), 1.0)
        d *= 2
    return b


GELU_C = math.sqrt(2.0 / math.pi)


def _gelu(x):
    t = jnp.tanh(GELU_C * (x + 0.044715 * (x * x * x)))
    return 0.5 * x * (1.0 + t), t


def _lru_gates(xr, wa, ba, wx, bx, lam):
    r = _sigmoid(_mm(xr, wa) + ba)
    i = _sigmoid(_mm(xr, wx) + bx)
    sp = jnp.maximum(-lam, 0.0) + jnp.log(1.0 + jnp.exp(-jnp.abs(lam)))
    la = -LRU_C * r * sp
    a = jnp.exp(la)
    e2 = a * a
    m = jnp.sqrt(-jnp.tanh(la) * (e2 + 1.0))
    return r, i, sp, a, e2, m


def lru_fwd(xr, u_lru, wa, wx, vec, S, name):
    T = xr.shape[0]

    def body(xr_ref, gt_ref, wa_ref, wx_ref, vec_ref, y_ref):
        x = xr_ref[...]
        row = _iota(x.shape, 0)
        r, i, sp, a, e2, m = _lru_gates(x, wa_ref[...], vec_ref[0:1, :], wx_ref[...], vec_ref[1:2, :],
                                        vec_ref[2:3, :])
        hh = _scan(a, m * (i * x), row)
        y_ref[...] = _gelu(gt_ref[...])[0] * hh

    return pl.pallas_call(
        body, name=name, grid=(T // S, LRU_W // LANE),
        in_specs=[pl.BlockSpec((S, LANE), lambda s, c: (s, c)),
                  pl.BlockSpec((S, LANE), lambda s, c: (s, 2 + c)),
                  pl.BlockSpec((LANE, LANE), lambda s, c: (c, c)),
                  pl.BlockSpec((LANE, LANE), lambda s, c: (c, c)),
                  pl.BlockSpec((8, LANE), lambda s, c: (0, c))],
        out_specs=pl.BlockSpec((S, LANE), lambda s, c: (s, c)),
        out_shape=jax.ShapeDtypeStruct((T, LRU_W), F32),
        compiler_params=_cp("parallel", "parallel"),
    )(xr, u_lru, wa, wx, vec)


def lru_bwd(xr, u_lru, dy, wa, wx, vec, S, name):
    T = xr.shape[0]

    def body(xr_ref, gt_ref, dy_ref, wa_ref, wx_ref, vec_ref,
             dxr_ref, dgt_ref, dwa_ref, dwx_ref, dvec_ref):
        @pl.when(pl.program_id(1) == 0)
        def _():
            dwa_ref[...] = jnp.zeros_like(dwa_ref)
            dwx_ref[...] = jnp.zeros_like(dwx_ref)
            dvec_ref[...] = jnp.zeros_like(dvec_ref)

        x = xr_ref[...]
        n = x.shape[0]
        row = _iota(x.shape, 0)
        lam = vec_ref[2:3, :]
        r, i, sp, a, e2, m = _lru_gates(x, wa_ref[...], vec_ref[0:1, :], wx_ref[...], vec_ref[1:2, :], lam)
        v = i * x
        hh = _scan(a, m * v, row)
        gt = gt_ref[...]
        dy = dy_ref[...]
        ge, t = _gelu(gt)
        dgt_ref[...] = dy * hh * (0.5 * (1.0 + t) + 0.5 * gt * (1.0 - t * t) * GELU_C
                                  * (1.0 + 3.0 * 0.044715 * gt * gt))
        a_next = jnp.where(row < n - 1, pltpu.roll(a, n - 1, 0), 0.0)
        G = _rscan(a_next, dy * ge, row)
        da = G * _shift_down(hh, 1, row)
        dv = G * m
        dla = da * a - (G * v) * e2 / m
        dr = dla * (-LRU_C * sp)
        dsp = jnp.sum(dla * (-LRU_C * r), axis=0, keepdims=True)
        dra = dr * r * (1.0 - r)
        dia = (dv * x) * i * (1.0 - i)
        dxr_ref[...] = dv * i + _mm_nt(dra, wa_ref[...]) + _mm_nt(dia, wx_ref[...])
        dwa_ref[0] += _mm_tn(x, dra)
        dwx_ref[0] += _mm_tn(x, dia)
        dvec_ref[0:1, :] += jnp.sum(dra, axis=0, keepdims=True)
        dvec_ref[1:2, :] += jnp.sum(dia, axis=0, keepdims=True)
        dvec_ref[2:3, :] += dsp * (-_sigmoid(-lam))

    return pl.pallas_call(
        body, name=name, grid=(LRU_W // LANE, T // S),
        in_specs=[pl.BlockSpec((S, LANE), lambda c, s: (s, c)),
                  pl.BlockSpec((S, LANE), lambda c, s: (s, 2 + c)),
                  pl.BlockSpec((S, LANE), lambda c, s: (s, c)),
                  pl.BlockSpec((LANE, LANE), lambda c, s: (c, c)),
                  pl.BlockSpec((LANE, LANE), lambda c, s: (c, c)),
                  pl.BlockSpec((8, LANE), lambda c, s: (0, c))],
        out_specs=[pl.BlockSpec((S, LANE), lambda c, s: (s, c)),
                   pl.BlockSpec((S, LANE), lambda c, s: (s, c)),
                   pl.BlockSpec((1, LANE, LANE), lambda c, s: (c, 0, 0)),
                   pl.BlockSpec((1, LANE, LANE), lambda c, s: (c, 0, 0)),
                   pl.BlockSpec((8, LANE), lambda c, s: (0, c))],
        out_shape=[jax.ShapeDtypeStruct((T, LRU_W), F32), jax.ShapeDtypeStruct((T, LRU_W), F32),
                   jax.ShapeDtypeStruct((2, LANE, LANE), F32), jax.ShapeDtypeStruct((2, LANE, LANE), F32),
                   jax.ShapeDtypeStruct((8, LRU_W), F32)],
        compiler_params=_cp("parallel", "arbitrary"),
    )(xr, u_lru, dy, wa, wx, vec)


def _bucket_table():
    qi = np.arange(BQ)[:, None]
    kj = np.arange(2 * BQ)[None, :]
    dist = BQ + qi - kj
    band = (dist >= 0) & (dist < WINDOW)
    dd = np.maximum(dist, 0)
    max_exact = REL_BUCKETS // 2
    large = max_exact + (np.log(np.maximum(dd, 1).astype(np.float32) / np.float32(max_exact))
                         / np.float32(math.log(REL_MAX_DIST / max_exact))
                         * np.float32(REL_BUCKETS - max_exact)).astype(np.int32)
    large = np.minimum(large, REL_BUCKETS - 1)
    bucket = np.where(dd < max_exact, dd, large)
    return np.where(band, bucket, -1).astype(np.int32)


def _att_specs(S):
    nb = S // BQ
    qc = ATT_W // LANE
    return [pl.BlockSpec((BQ, ATT_W), lambda b, n: (b * nb + n, 0)),
            pl.BlockSpec((BQ, KV_W), lambda b, n: (b * nb + jnp.maximum(n - 1, 0), qc)),
            pl.BlockSpec((BQ, KV_W), lambda b, n: (b * nb + n, qc)),
            pl.BlockSpec((BQ, KV_W), lambda b, n: (b * nb + jnp.maximum(n - 1, 0), qc + 1)),
            pl.BlockSpec((BQ, KV_W), lambda b, n: (b * nb + n, qc + 1))]


def _att_bias(bk, rb_ref, bias_s):
    for h in range(ATT_H):
        acc = jnp.zeros(bk.shape, F32)
        for bb in range(REL_BUCKETS):
            acc = jnp.where(bk == bb, rb_ref[bb * ATT_H + h], acc)
        bias_s[h] = acc


def _att_probs(qh, kg, bias, valid, sink):
    s = _mm_nt(qh, kg) * (HD ** -0.5) + bias
    s = jnp.where(valid, s, NEG)
    m = jnp.maximum(jnp.max(s, axis=-1, keepdims=True), sink)
    e = jnp.exp(s - m)
    es = jnp.exp(sink - m)
    den = jnp.sum(e, axis=-1, keepdims=True) + es
    return e / den, es / den


def attn_fwd(u_att, sinks, rel_bias, S, name):
    T = u_att.shape[0]
    nb = S // BQ
    table = jnp.asarray(_bucket_table())

    def body(sk_ref, rb_ref, bk_ref, q_ref, kp_ref, kc_ref, vp_ref, vc_ref, o_ref, bias_s):
        b = pl.program_id(0)
        n = pl.program_id(1)
        bk = bk_ref[...]

        @pl.when((b == 0) & (n == 0))
        def _():
            _att_bias(bk, rb_ref, bias_s)

        valid = (bk >= 0) & ((n > 0) | (_iota(bk.shape, 1) >= BQ))
        for h in range(ATT_H):
            gs = slice(HD * (h // ATT_G), HD * (h // ATT_G + 1))
            kg = jnp.concatenate([kp_ref[:, gs], kc_ref[:, gs]], axis=0)
            vg = jnp.concatenate([vp_ref[:, gs], vc_ref[:, gs]], axis=0)
            p, _ = _att_probs(q_ref[:, HD * h:HD * (h + 1)], kg, bias_s[h], valid, sk_ref[h])
            o_ref[:, HD * h:HD * (h + 1)] = _mm(p, vg)

    smem = pl.BlockSpec(memory_space=pltpu.SMEM)
    return pl.pallas_call(
        body, name=name, grid=(T // S, nb),
        in_specs=[smem, smem, _full_spec((BQ, 2 * BQ))] + _att_specs(S),
        out_specs=pl.BlockSpec((BQ, ATT_W), lambda b, n: (b * nb + n, 0)),
        out_shape=jax.ShapeDtypeStruct((T, ATT_W), F32),
        scratch_shapes=[pltpu.VMEM((ATT_H, BQ, 2 * BQ), F32)],
        compiler_params=_cp("arbitrary", "arbitrary"),
    )(sinks, rel_bias, table, u_att, u_att, u_att, u_att, u_att)


def attn_bwd(u_att, dy, sinks, rel_bias, S, name):
    T = u_att.shape[0]
    nb = S // BQ
    nB = T // S
    table = jnp.asarray(_bucket_table())
    scale = HD ** -0.5

    def body(sk_ref, rb_ref, bk_ref, q_ref, kp_ref, kc_ref, vp_ref, vc_ref, dy_ref,
             du_ref, drel_ref, dsk_ref, bias_s, dbias_s):
        b = pl.program_id(0)
        n = pl.program_id(1)
        bk = bk_ref[...]

        @pl.when((b == 0) & (n == 0))
        def _():
            _att_bias(bk, rb_ref, bias_s)
            dbias_s[...] = jnp.zeros_like(dbias_s)
            dsk_ref[...] = jnp.zeros_like(dsk_ref)
            drel_ref[...] = jnp.zeros_like(drel_ref)

        @pl.when(n == 0)
        def _():
            du_ref[...] = jnp.zeros_like(du_ref)

        valid = (bk >= 0) & ((n > 0) | (_iota(bk.shape, 1) >= BQ))
        r_cur = pl.multiple_of(n * BQ, BQ)
        r_prev = pl.multiple_of(jnp.maximum(n - 1, 0) * BQ, BQ)
        for g in range(ATT_KV):
            gs = slice(HD * g, HD * (g + 1))
            kg = jnp.concatenate([kp_ref[:, gs], kc_ref[:, gs]], axis=0)
            vg = jnp.concatenate([vp_ref[:, gs], vc_ref[:, gs]], axis=0)
            dk = jnp.zeros((2 * BQ, HD), F32)
            dv = jnp.zeros((2 * BQ, HD), F32)
            for e in range(ATT_G):
                h = g * ATT_G + e
                qh = q_ref[:, HD * h:HD * (h + 1)]
                do = dy_ref[:, HD * h:HD * (h + 1)]
                p, ps = _att_probs(qh, kg, bias_s[h], valid, sk_ref[h])
                dp = _mm_nt(do, vg)
                delta = jnp.sum(p * dp, axis=-1, keepdims=True)
                ds = p * (dp - delta)
                dbias_s[h] += ds
                dsk_ref[h:h + 1, :] += jnp.broadcast_to(
                    jnp.sum(-ps * delta, axis=0, keepdims=True), (1, LANE))
                dss = ds * scale
                du_ref[pl.ds(r_cur, BQ), HD * h:HD * (h + 1)] = _mm(dss, kg)
                dk += _mm_tn(dss, qh)
                dv += _mm_tn(p, do)
            ck = ATT_W + HD * g
            cv = ATT_W + KV_W + HD * g
            du_ref[pl.ds(r_prev, BQ), ck:ck + HD] += dk[0:BQ]
            du_ref[pl.ds(r_cur, BQ), ck:ck + HD] += dk[BQ:]
            du_ref[pl.ds(r_prev, BQ), cv:cv + HD] += dv[0:BQ]
            du_ref[pl.ds(r_cur, BQ), cv:cv + HD] += dv[BQ:]

        @pl.when((b == nB - 1) & (n == nb - 1))
        def _():
            lane = _iota((1, LANE), 1)
            for h in range(ATT_H):
                db = dbias_s[h]
                acc = jnp.zeros((1, LANE), F32)
                for bb in range(REL_BUCKETS):
                    val = jnp.sum(jnp.sum(jnp.where(bk == bb, db, 0.0), axis=1, keepdims=True),
                                  axis=0, keepdims=True)
                    acc = jnp.where(lane == bb, val, acc)
                drel_ref[h:h + 1, :] = acc

    smem = pl.BlockSpec(memory_space=pltpu.SMEM)
    return pl.pallas_call(
        body, name=name, grid=(nB, nb),
        in_specs=[smem, smem, _full_spec((BQ, 2 * BQ))] + _att_specs(S)
        + [pl.BlockSpec((BQ, ATT_W), lambda b, n: (b * nb + n, 0))],
        out_specs=[pl.BlockSpec((S, ATT_W + 2 * KV_W), lambda b, n: (b, 0)),
                   _full_spec((8, LANE)), _full_spec((8, LANE))],
        out_shape=[jax.ShapeDtypeStruct((T, ATT_W + 2 * KV_W), F32),
                   jax.ShapeDtypeStruct((8, LANE), F32), jax.ShapeDtypeStruct((8, LANE), F32)],
        scratch_shapes=[pltpu.VMEM((ATT_H, BQ, 2 * BQ), F32), pltpu.VMEM((ATT_H, BQ, 2 * BQ), F32)],
        compiler_params=_cp("arbitrary", "arbitrary"),
    )(sinks, rel_bias, table, u_att, u_att, u_att, u_att, u_att, dy)


def _head_of(i):
    return lax.shift_right_logical(i, 6)


def _head_mask(shape):
    return (_head_of(_iota(shape, 0)) == _head_of(_iota(shape, 1))).astype(F32)


def _dn_point(c, uba, alog, dtb):
    s = c * _sigmoid(c)
    qt, kt, vt = s[:, 0:256], s[:, 256:512], s[:, 512:768]
    ones_bd = _head_mask((DN_W, DN_W))
    q = qt * lax.rsqrt(_mmx(qt * qt, ones_bd) + EPS) * (HD ** -0.5)
    k = kt * lax.rsqrt(_mmx(kt * kt, ones_bd) + EPS)
    sel = _head_of(_iota((LANE, DN_W), 1))
    row = _iota((LANE, DN_W), 0)
    braw = _mmx(uba, (row == sel).astype(F32))
    araw = _mmx(uba, (row == sel + DN_H).astype(F32)) + dtb
    beta = _sigmoid(braw)
    g = -jnp.exp(alog) * (jnp.maximum(araw, 0.0) + jnp.log(1.0 + jnp.exp(-jnp.abs(araw))))
    return q, k, vt, g, beta


def dn_point_fwd(c, uba, alog, dtb, name):
    T = c.shape[0]
    tm = min(TOK_TILE, T)

    def body(c_ref, u_ref, al_ref, dt_ref, *outs):
        for ref, val in zip(outs, _dn_point(c_ref[...], u_ref[...], al_ref[...], dt_ref[...])):
            ref[...] = val

    return pl.pallas_call(
        body, name=name, grid=(T // tm,),
        in_specs=[_row_spec(tm, 768), _row_spec(tm, LANE), _full_spec((1, DN_W)), _full_spec((1, DN_W))],
        out_specs=[_row_spec(tm, DN_W)] * 5,
        out_shape=[jax.ShapeDtypeStruct((T, DN_W), F32)] * 5,
        compiler_params=_cp("parallel"),
    )(c, uba, alog, dtb)


def dn_point_bwd(c, uba, alog, dtb, douts, name):
    T = c.shape[0]
    tm = min(TOK_TILE, T)

    def body(c_ref, u_ref, al_ref, dt_ref, dq, dk, dv, dg, db, dc_ref, du_ref, dvec_ref):
        @pl.when(pl.program_id(0) == 0)
        def _():
            dvec_ref[...] = jnp.zeros_like(dvec_ref)

        _, vjp = jax.vjp(_dn_point, c_ref[...], u_ref[...], al_ref[...], dt_ref[...])
        dc, du, dal, ddt = vjp((dq[...], dk[...], dv[...], dg[...], db[...]))
        dc_ref[...] = dc
        du_ref[...] = du
        fold = (_iota((LANE, DN_W), 0) == _head_of(_iota((LANE, DN_W), 1))).astype(F32)
        both = jnp.concatenate([dal, ddt, jnp.zeros((6, DN_W), F32)], axis=0)
        dvec_ref[...] += _mmx_nt(both, fold)

    return pl.pallas_call(
        body, name=name, grid=(T // tm,),
        in_specs=[_row_spec(tm, 768), _row_spec(tm, LANE), _full_spec((1, DN_W)), _full_spec((1, DN_W))]
        + [_row_spec(tm, DN_W)] * 5,
        out_specs=[_row_spec(tm, 768), _row_spec(tm, LANE), _full_spec((8, LANE))],
        out_shape=[jax.ShapeDtypeStruct((T, 768), F32), jax.ShapeDtypeStruct((T, LANE), F32),
                   jax.ShapeDtypeStruct((8, LANE), F32)],
        compiler_params=_cp("arbitrary"),
    )(c, uba, alog, dtb, *douts)


def _unit_lower_inverses(lmats):
    eye = (_iota(lmats[0].shape, 0) == _iota(lmats[0].shape, 1)).astype(F32)
    tinvs = [eye - lm for lm in lmats]
    pws = list(lmats)
    for _ in range(5):
        pws = [_mm3(pw, pw) for pw in pws]
        tinvs = [t + _mm3(t, pw) for t, pw in zip(tinvs, pws)]
    return tuple(tinvs)


def _inverse_bwd(tinv, d):
    return -_mm3_nt(_mm3_tn(tinv, d), tinv)


@jax.custom_vjp
def _tri_invs(lmats):
    return _unit_lower_inverses(lmats)


def _tri_invs_fwd(lmats):
    tinvs = _unit_lower_inverses(lmats)
    return tinvs, tinvs


_tri_invs.defvjp(_tri_invs_fwd, lambda tinvs, ds: (tuple(_inverse_bwd(t, d) for t, d in zip(tinvs, ds)),))


@jax.custom_vjp
def _tri_inv_known(lmat, tinv):
    return tinv


_tri_inv_known.defvjp(lambda lmat, tinv: (tinv, tinv),
                      lambda tinv, d: (_inverse_bwd(tinv, d), jnp.zeros_like(tinv)))


DN_SUB = 4


def _dn_stack(x):
    return jnp.concatenate([x, x, x, x], axis=0) * _head_mask((DN_W, DN_W))


def _dn_pre_inverse(q, k, v, g, beta):
    hm = _head_mask((DN_W, DN_W))
    ri = _iota((DN_W, DN_W), 0) & (CHUNK - 1)
    ci = _iota((DN_W, DN_W), 1) & (CHUNK - 1)
    tri64 = (_iota((CHUNK, CHUNK), 0) >= _iota((CHUNK, CHUNK), 1)).astype(F32)
    gc = _mm3(tri64, g)
    ks = _dn_stack(k)
    gcol = jnp.sum(_dn_stack(gc), axis=1, keepdims=True) * (1.0 / HD)
    gmat = jnp.broadcast_to(gcol, (DN_W, DN_W))
    decay = jnp.exp(jnp.minimum(gmat - gmat.T, 0.0))
    lmat = _mm_nt(_dn_stack(k * beta), ks) * decay * (hm * (ri > ci).astype(F32))
    att = _mm_nt(_dn_stack(q), ks) * decay * (hm * (ri >= ci).astype(F32))
    return lmat, att, gc


def _dn_post_inverse(q, k, v, g, beta, tinv, att, gc):
    glast = jnp.sum(g, axis=0, keepdims=True)
    eg = jnp.exp(gc)
    u = _mm(tinv, _dn_stack(v * beta))
    w = _mm(tinv, _dn_stack(k * beta * eg))
    return u, w, att, _dn_stack(q * eg), _dn_stack(k * jnp.exp(glast - gc)), jnp.exp(glast), tinv


def _dn_apply(state, prep):
    u, w, att, qe, kd, eglast, _ = prep
    vn = u - _mm(w, state)
    o4 = _mm(qe, state) + _mm(att, vn)
    o = o4[0:64] + o4[64:128] + o4[128:192] + o4[192:256]
    return o, state * eglast + _mm_tn(kd, vn)


def _dn_chunks(state, q, k, v, g, beta, knowns=None):
    n = q.shape[0] // CHUNK
    chunks = [tuple(x[c * CHUNK:(c + 1) * CHUNK] for x in (q, k, v, g, beta)) for c in range(n)]
    pre = [_dn_pre_inverse(*ch) for ch in chunks]
    if knowns is None:
        tinvs = _tri_invs(tuple(p[0] for p in pre))
    else:
        tinvs = [_tri_inv_known(p[0], known) for p, known in zip(pre, knowns)]
    preps = [_dn_post_inverse(*ch, tinv, p[1], p[2]) for ch, tinv, p in zip(chunks, tinvs, pre)]
    outs = []
    for prep in preps:
        o, state = _dn_apply(state, prep)
        outs.append(o)
    return jnp.concatenate(outs, axis=0), state, [prep[-1] for prep in preps]


def dn_scan_fwd(q, k, v, g, beta, S, name):
    T = q.shape[0]
    rows = DN_SUB * CHUNK
    ns = S // rows

    def body(q_ref, k_ref, v_ref, g_ref, b_ref, o_ref, st_ref, ti_ref, s_s):
        @pl.when(pl.program_id(1) == 0)
        def _():
            s_s[...] = jnp.zeros_like(s_s)

        st = s_s[...]
        st_ref[0] = st
        o, new, tinvs = _dn_chunks(st, q_ref[...], k_ref[...], v_ref[...], g_ref[...], b_ref[...])
        o_ref[...] = o
        for c, tinv in enumerate(tinvs):
            ti_ref[c] = tinv
        s_s[...] = new

    spec = pl.BlockSpec((rows, DN_W), lambda b, t: (b * ns + t, 0))
    return pl.pallas_call(
        body, name=name, grid=(T // S, ns),
        in_specs=[spec] * 5,
        out_specs=[spec, pl.BlockSpec((1, DN_W, DN_W), lambda b, t: (b * ns + t, 0, 0)),
                   pl.BlockSpec((DN_SUB, DN_W, DN_W), lambda b, t: (b * ns + t, 0, 0))],
        out_shape=[jax.ShapeDtypeStruct((T, DN_W), F32),
                   jax.ShapeDtypeStruct((T // rows, DN_W, DN_W), F32),
                   jax.ShapeDtypeStruct((T // CHUNK, DN_W, DN_W), F32)],
        scratch_shapes=[pltpu.VMEM((DN_W, DN_W), F32)],
        compiler_params=_cp("parallel", "arbitrary"),
    )(q, k, v, g, beta)


def dn_scan_bwd(q, k, v, g, beta, states, tinvs, do, S, name):
    T = q.shape[0]
    rows = DN_SUB * CHUNK
    ns = S // rows

    def body(q_ref, k_ref, v_ref, g_ref, b_ref, st_ref, ti_ref, do_ref, dq, dk, dv, dg, db, ds_s):
        @pl.when(pl.program_id(1) == 0)
        def _():
            ds_s[...] = jnp.zeros_like(ds_s)

        knowns = [ti_ref[c] for c in range(DN_SUB)]
        _, vjp = jax.vjp(lambda *args: _dn_chunks(*args, knowns=knowns)[:2],
                         st_ref[0], q_ref[...], k_ref[...], v_ref[...], g_ref[...], b_ref[...])
        grads = vjp((do_ref[...], ds_s[...]))
        ds_s[...] = grads[0]
        for ref, val in zip((dq, dk, dv, dg, db), grads[1:]):
            ref[...] = val

    spec = pl.BlockSpec((rows, DN_W), lambda b, t: (b * ns + ns - 1 - t, 0))
    return pl.pallas_call(
        body, name=name, grid=(T // S, ns),
        in_specs=[spec] * 5 + [pl.BlockSpec((1, DN_W, DN_W), lambda b, t: (b * ns + ns - 1 - t, 0, 0)),
                               pl.BlockSpec((DN_SUB, DN_W, DN_W), lambda b, t: (b * ns + ns - 1 - t, 0, 0)),
                               spec],
        out_specs=[spec] * 5,
        out_shape=[jax.ShapeDtypeStruct((T, DN_W), F32)] * 5,
        scratch_shapes=[pltpu.VMEM((DN_W, DN_W), F32)],
        compiler_params=_cp("parallel", "arbitrary"),
    )(q, k, v, g, beta, states, tinvs, do)


def _dn_gate(o, z, nl):
    ms = _mmx(o * o, _head_mask((DN_W, DN_W))) * (1.0 / HD)
    return o * lax.rsqrt(ms + EPS) * nl * (z * _sigmoid(z))


def dn_gate_fwd(o, u_dn, nl, name):
    T = o.shape[0]
    tm = min(TOK_TILE, T)

    def body(o_ref, z_ref, n_ref, y_ref):
        y_ref[...] = _dn_gate(o_ref[...], z_ref[...], n_ref[...])

    return pl.pallas_call(
        body, name=name, grid=(T // tm,),
        in_specs=[_row_spec(tm, DN_W), pl.BlockSpec((tm, DN_W), lambda i: (i, 3)), _full_spec((1, DN_W))],
        out_specs=_row_spec(tm, DN_W),
        out_shape=jax.ShapeDtypeStruct((T, DN_W), F32),
        compiler_params=_cp("parallel"),
    )(o, u_dn, nl)


def dn_gate_bwd(o, u_dn, nl, dy, name):
    T = o.shape[0]
    tm = min(TOK_TILE, T)

    def body(o_ref, z_ref, n_ref, dy_ref, do_ref, dz_ref, dn_ref):
        @pl.when(pl.program_id(0) == 0)
        def _():
            dn_ref[...] = jnp.zeros_like(dn_ref)

        _, vjp = jax.vjp(_dn_gate, o_ref[...], z_ref[...], n_ref[...])
        do, dz, dn = vjp(dy_ref[...])
        do_ref[...] = do
        dz_ref[...] = dz
        fold = (_iota((LANE, DN_W), 0) == (_iota((LANE, DN_W), 1) & (HD - 1))).astype(F32)
        dn_ref[...] += _mmx_nt(jnp.concatenate([dn, jnp.zeros((7, DN_W), F32)], axis=0), fold)

    return pl.pallas_call(
        body, name=name, grid=(T // tm,),
        in_specs=[_row_spec(tm, DN_W), pl.BlockSpec((tm, DN_W), lambda i: (i, 3)), _full_spec((1, DN_W)),
                  _row_spec(tm, DN_W)],
        out_specs=[_row_spec(tm, DN_W), _row_spec(tm, DN_W), _full_spec((8, LANE))],
        out_shape=[jax.ShapeDtypeStruct((T, DN_W), F32), jax.ShapeDtypeStruct((T, DN_W), F32),
                   jax.ShapeDtypeStruct((8, LANE), F32)],
        compiler_params=_cp("arbitrary"),
    )(o, u_dn, nl, dy)


Y_SPLITS = (LRU_W, ATT_W, DN_W)
Y_OFFS = (0, LRU_W, LRU_W + ATT_W)


ROWS_DEV = D // N_DEV


def _dev_rows_spec(row0):
    return pl.BlockSpec((N_DEV, ROWS_DEV, D), lambda *_: (0, row0 // ROWS_DEV, 0))


def _dev_rows(w_ref, off, n):
    return w_ref[off // ROWS_DEV:(off + n) // ROWS_DEV].reshape(n, D)


def wout_fwd(h, ys, wb, name):
    T = h.shape[0]
    tm = min(TOK_TILE, T)

    def body(h_ref, y0, y1, y2, w_ref, o_ref, yc_ref):
        acc = h_ref[...]
        for ref, off, n in zip((y0, y1, y2), Y_OFFS, Y_SPLITS):
            y = ref[...].astype(BF16)
            yc_ref[:, off:off + n] = y
            acc += _mm(y, _dev_rows(w_ref, off, n))
        o_ref[...] = acc

    return pl.pallas_call(
        body, name=name, grid=(T // tm,),
        in_specs=[_row_spec(tm, D)] + [_row_spec(tm, n) for n in Y_SPLITS] + [_dev_rows_spec(R_WOUT)],
        out_specs=[_row_spec(tm, D), _row_spec(tm, D)],
        out_shape=[jax.ShapeDtypeStruct((T, D), F32), jax.ShapeDtypeStruct((T, D), BF16)],
        compiler_params=_cp("parallel"),
    )(h, *ys, wb)


def wout_bwd(dy, wb, name):
    T = dy.shape[0]
    tm = min(TOK_TILE, T)

    def body(dy_ref, w_ref, d0, d1, d2):
        dd = dy_ref[...].astype(BF16)
        for ref, off, n in zip((d0, d1, d2), Y_OFFS, Y_SPLITS):
            ref[...] = _mm_nt(dd, _dev_rows(w_ref, off, n))

    return pl.pallas_call(
        body, name=name, grid=(T // tm,),
        in_specs=[_row_spec(tm, D), _dev_rows_spec(R_WOUT)],
        out_specs=[_row_spec(tm, n) for n in Y_SPLITS],
        out_shape=[jax.ShapeDtypeStruct((T, n), F32) for n in Y_SPLITS],
        compiler_params=_cp("parallel"),
    )(dy, wb)


def ple_fwd(h, g, pe, wg, wp, name):
    T = h.shape[0]
    tm = min(TOK_TILE, T)

    def body(h_ref, g_ref, p_ref, wg_ref, wp_ref, o_ref):
        hh = h_ref[...]
        xn = _rms(hh, g_ref[...])[0]
        o_ref[...] = hh + _sigmoid(_mm(xn, _dev_rows(wg_ref, 0, D))) * _mm(p_ref[...], wp_ref[...])

    return pl.pallas_call(
        body, name=name, grid=(T // tm,),
        in_specs=[_row_spec(tm, D), _full_spec((1, D)), _row_spec(tm, PLE), _dev_rows_spec(R_PGATE),
                  _full_spec((PLE, D))],
        out_specs=_row_spec(tm, D),
        out_shape=jax.ShapeDtypeStruct((T, D), F32),
        compiler_params=_cp("parallel"),
    )(h, g, pe, wg, wp)


def ple_bwd(h, dy, g, pe, wg, wp, name):
    T = h.shape[0]
    tm = min(TOK_TILE, T)

    def body(h_ref, dy_ref, g_ref, p_ref, wg_ref, wp_ref, dh_ref, dz_ref, dpp_ref, xn_ref, dn_ref):
        @pl.when(pl.program_id(0) == 0)
        def _():
            dn_ref[...] = jnp.zeros_like(dn_ref)

        gg = g_ref[...]
        dy = dy_ref[...]
        xn, xhat, rstd = _rms(h_ref[...], gg)
        wg = _dev_rows(wg_ref, 0, D)
        gate = _sigmoid(_mm(xn, wg))
        pp = _mm(p_ref[...], wp_ref[...])
        dz = dy * pp * gate * (1.0 - gate)
        dz_ref[...] = dz.astype(BF16)
        dpp_ref[...] = (dy * gate).astype(BF16)
        xn_ref[...] = xn.astype(BF16)
        dh, dn = _rms_bwd(_mm_nt(dz, wg), xhat, rstd, gg)
        dh_ref[...] = dy + dh
        dn_ref[...] += dn

    return pl.pallas_call(
        body, name=name, grid=(T // tm,),
        in_specs=[_row_spec(tm, D), _row_spec(tm, D), _full_spec((1, D)), _row_spec(tm, PLE),
                  _dev_rows_spec(R_PGATE), _full_spec((PLE, D))],
        out_specs=[_row_spec(tm, D), _row_spec(tm, D), _row_spec(tm, D), _row_spec(tm, D), _full_spec((1, D))],
        out_shape=[jax.ShapeDtypeStruct((T, D), F32), jax.ShapeDtypeStruct((T, D), BF16),
                   jax.ShapeDtypeStruct((T, D), BF16), jax.ShapeDtypeStruct((T, D), BF16),
                   jax.ShapeDtypeStruct((1, D), F32)],
        compiler_params=_cp("arbitrary"),
    )(h, dy, g, pe, wg, wp)


def loss_head(h, g, target, name):
    T = h.shape[0]
    tm = min(TOK_TILE, T)

    def body(h_ref, g_ref, t_ref, loss_ref, dh_ref, dn_ref):
        @pl.when(pl.program_id(0) == 0)
        def _():
            dn_ref[...] = jnp.zeros_like(dn_ref)
            loss_ref[...] = jnp.zeros_like(loss_ref)

        gg = g_ref[...]
        y, xhat, rstd = _rms(h_ref[...], gg)
        err = y - t_ref[...]
        per_tok = jnp.mean(err * err, axis=-1, keepdims=True)
        loss_ref[...] += 0.5 * jnp.sum(per_tok, axis=0, keepdims=True)
        dh, dn = _rms_bwd(err * (1.0 / D), xhat, rstd, gg)
        dh_ref[...] = dh
        dn_ref[...] += dn

    return pl.pallas_call(
        body, name=name, grid=(T // tm,),
        in_specs=[_row_spec(tm, D), _full_spec((1, D)), _row_spec(tm, D)],
        out_specs=[_full_spec((8, LANE)), _row_spec(tm, D), _full_spec((1, D))],
        out_shape=[jax.ShapeDtypeStruct((8, LANE), F32), jax.ShapeDtypeStruct((T, D), F32),
                   jax.ShapeDtypeStruct((1, D), F32)],
        compiler_params=_cp("arbitrary"),
    )(h, g, target)


def _block_diag(w):
    return jnp.einsum('hij,hk->hikj', w, jnp.eye(4, dtype=w.dtype)).reshape(LRU_W, LRU_W)


def _layer_consts(W, l):
    row = lambda v: v.reshape(1, -1)
    zeros = jnp.zeros((5, LRU_W), F32)
    return dict(
        wa=_block_diag(W["lru_w_a"][l]), wx=_block_diag(W["lru_w_x"][l]),
        lru_vec=jnp.concatenate([row(W["lru_b_a"][l]), row(W["lru_b_x"][l]), row(W["lru_lambda"][l]), zeros], 0),
        lru_cb=row(W["lru_conv_b"][l]),
        sinks=W["attn_sinks"][l], rel=W["rel_bias"].reshape(-1),
        dn_cb=jnp.zeros((1, 3 * DN_W), F32),
        alog=row(jnp.repeat(W["dn_a_log"][l], HD)), dtb=row(jnp.repeat(W["dn_dt_bias"][l], HD)),
        dn_nl=row(jnp.tile(W["dn_norm"][l], DN_H)),
    )


def _layer_fwd(h0, pe, W, l, S, need=None):
    n = f"l{l}_"
    c_ = _layer_consts(W, l)
    row = lambda v: v.reshape(1, -1)
    need = need or (lambda *_: None)
    need(l, "f1", h0)
    h1, *ffn1_kept = ffn_fwd(h0, row(W["ffn1_norm"][l]), W["f1_cols"][l], W["f1_rows"][l], n + "ffn1_fwd")
    need(l, "in", h1)
    u_lru, u_att, u_dn, u_ba, xn_mix = mixin_fwd(h1, row(W["mix_norm"][l]), W["w_in"][l], n + "mixin_fwd")
    xr = conv_fwd(u_lru, W["lru_conv_w"][l], c_["lru_cb"], S, 0, LRU_W, n + "lru_conv_fwd")
    y_lru = lru_fwd(xr, u_lru, c_["wa"], c_["wx"], c_["lru_vec"], S, n + "lru_fwd")
    y_att = attn_fwd(u_att, c_["sinks"], c_["rel"], S, n + "attn_fwd")
    cc = conv_fwd(u_dn, W["dn_conv_w"][l], c_["dn_cb"], S, 0, 3 * DN_W, n + "dn_conv_fwd")
    q, k, v, g, beta = dn_point_fwd(cc, u_ba, c_["alog"], c_["dtb"], n + "dn_point_fwd")
    o, states, tinvs = dn_scan_fwd(q, k, v, g, beta, S, n + "dn_scan_fwd")
    y_dn = dn_gate_fwd(o, u_dn, c_["dn_nl"], n + "dn_gate_fwd")
    need(l, "rest", y_dn)
    h2, ycat = wout_fwd(h1, (y_lru, y_att, y_dn), W["r_rows"][l], n + "wout_fwd")
    h3, *ffn2_kept = ffn_fwd(h2, row(W["ffn2_norm"][l]), W["r_cols"][l], W["r_rows"][l], n + "ffn2_fwd")
    h4 = ple_fwd(h3, row(W["ple_norm"][l]), pe, W["r_rows"][l], W["ple_w_proj"][l], n + "ple_fwd")
    saved = dict(ffn1=ffn1_kept, ffn2=ffn2_kept, h0=h0, h1=h1, h2=h2, h3=h3, u_lru=u_lru, u_att=u_att, u_dn=u_dn,
                 u_ba=u_ba, xn_mix=xn_mix, xr=xr, cc=cc, q=q, k=k, v=v, g=g, beta=beta, o=o, states=states, tinvs=tinvs, ycat=ycat)
    return h4, saved


GM_WOUT, GM_PGATE, GM_WIN, GM_PPROJ, GM_END, GM_ROWS = 0, 128, 256, 560, 592, 640


def _layer_bwd(dh4, sv, pe, W, l, S, token=None, on_piece=None):
    n = f"l{l}_"
    c_ = _layer_consts(W, l)
    row = lambda v: v.reshape(1, -1)
    behind = lambda v, tok: v if tok is None else v + tok.astype(v.dtype)
    on_piece = on_piece or (lambda *_: None)
    G = {"mix_rows": jnp.zeros((N_DEV, GM_ROWS, D), BF16)}
    dh3, dz, dpp, xn_p, dn = ple_bwd(sv["h3"], dh4, behind(row(W["ple_norm"][l]), token), pe, W["r_rows"][l],
                                     W["ple_w_proj"][l], n + "ple_bwd")
    G["ple_norm"] = dn[0]
    G["mix_rows"] = grad_rows(xn_p, dz, G["mix_rows"], GM_PGATE, ROWS_DEV, n + "d_ple_w_gate")
    d_proj = matmul_tn(pe, dpp, n + "d_ple_w_proj")
    d_proj = d_proj.reshape(PLE, N_DEV, D // N_DEV).transpose(1, 0, 2).reshape(N_DEV, GM_END - GM_PPROJ, D)
    G["mix_rows"] = lax.dynamic_update_slice(G["mix_rows"], d_proj, (0, GM_PPROJ, 0))

    def ffn_back(which, cols_w, rows_w, h_in, dy, tok, one_by_one):
        gt, up, xn = sv[which]
        dh, dgt, dup, act, dn_ = ffn_bwd(h_in, dy, behind(row(W[which + "_norm"][l]), tok), gt, up, cols_w, rows_w,
                                         n + which + "_bwd")
        G[which + "_norm"] = dn_[0]
        zeros_rows = jnp.zeros((N_DEV, SHP, D), BF16)
        if one_by_one:
            G[which + "_gate"] = grad_cols(xn, dgt, jnp.zeros((1, D, FFP), BF16), 0, n + "d_" + which + "_w_gate")
            tok = on_piece(l, which + "_gate", (G[which + "_gate"],))
            G[which + "_up"] = grad_cols(xn, dup, behind(jnp.zeros((1, D, FFP), BF16), tok), 0,
                                         n + "d_" + which + "_w_up")
            tok = on_piece(l, which + "_up", (G[which + "_up"],))
            G[which + "_down"] = grad_rows(act, dy, behind(zeros_rows, tok), 0, SHP, n + "d_" + which + "_w_down",
                                           scale=0.5)
            return dh, on_piece(l, which + "_down", (G[which + "_down"],))
        cols = grad_cols(xn, dgt, jnp.zeros((2, D, FFP), BF16), 0, n + "d_" + which + "_w_gate")
        G[which + "_cols"] = grad_cols(xn, dup, cols, 1, n + "d_" + which + "_w_up")
        G[which + "_rows"] = grad_rows(act, dy, zeros_rows, 0, SHP, n + "d_" + which + "_w_down", scale=0.5)
        return dh, on_piece(l, which, (G[which + "_cols"], G[which + "_rows"]))

    dh2, tok = ffn_back("ffn2", W["r_cols"][l], W["r_rows"][l], sv["h2"], dh3, None, False)
    dy_lru, dy_att, dy_dn = wout_bwd(dh2, W["r_rows"][l], n + "wout_bwd")
    G["mix_rows"] = grad_rows(sv["ycat"], dh2, G["mix_rows"], GM_WOUT, ROWS_DEV, n + "d_w_out")
    do, dz_dn, dnn = dn_gate_bwd(sv["o"], sv["u_dn"], behind(c_["dn_nl"], tok), dy_dn, n + "dn_gate_bwd")
    dqkvgb = dn_scan_bwd(sv["q"], sv["k"], sv["v"], sv["g"], sv["beta"], sv["states"], sv["tinvs"], do, S,
                         n + "dn_scan_bwd")
    dcc, du_ba, dvec_dn = dn_point_bwd(sv["cc"], sv["u_ba"], c_["alog"], c_["dtb"], dqkvgb, n + "dn_point_bwd")
    dqkv, dwb_dn = conv_bwd(sv["u_dn"], dcc, W["dn_conv_w"][l], S, 0, 3 * DN_W, n + "dn_conv_bwd")
    du_dn = jnp.concatenate([dqkv, dz_dn], axis=1)
    G["dn_norm"] = dnn[0, 0:HD]
    G["dn_a_log"] = dvec_dn[0, 0:DN_H]
    G["dn_dt_bias"] = dvec_dn[1, 0:DN_H]
    G["dn_conv_w"] = dwb_dn[0:4]
    du_att, drel, dsk = attn_bwd(sv["u_att"], dy_att, c_["sinks"], c_["rel"], S, n + "attn_bwd")
    G["attn_sinks"] = dsk[:, 0]
    G["rel_bias"] = drel[:, 0:REL_BUCKETS].T
    dxr, dgt_lru, dwa, dwx, dvec = lru_bwd(sv["xr"], sv["u_lru"], dy_lru, c_["wa"], c_["wx"], c_["lru_vec"], S,
                                           n + "lru_bwd")
    dx_lru, dwb_lru = conv_bwd(sv["u_lru"], dxr, W["lru_conv_w"][l], S, 0, LRU_W, n + "lru_conv_bwd")
    du_lru = jnp.concatenate([dx_lru, dgt_lru], axis=1)
    diag = lambda m: jnp.stack([m[c, HD * e:HD * (e + 1), HD * e:HD * (e + 1)] for c in range(2) for e in range(2)])
    G["lru_w_a"], G["lru_w_x"] = diag(dwa), diag(dwx)
    G["lru_b_a"], G["lru_b_x"], G["lru_lambda"] = dvec[0], dvec[1], dvec[2]
    G["lru_conv_w"], G["lru_conv_b"] = dwb_lru[0:4], dwb_lru[4]
    dh1, du_cat, dn = mixin_bwd(sv["h1"], dh2, row(W["mix_norm"][l]), W["w_in"][l],
                                (du_lru, du_att, du_dn, du_ba), n + "mixin_bwd")
    G["mix_norm"] = dn[0]
    d_in = matmul_tn(sv["xn_mix"], du_cat, n + "d_w_in")[:, :D_IN]
    d_in = d_in.reshape(D, N_DEV, D_IN // N_DEV).transpose(1, 0, 2).reshape(N_DEV, WIN_ROWS, D)
    d_in = jnp.pad(d_in, ((0, 0), (0, GM_PPROJ - GM_WIN - WIN_ROWS), (0, 0)))
    G["mix_rows"] = lax.dynamic_update_slice(G["mix_rows"], d_in, (0, GM_WIN, 0))
    tok = on_piece(l, "mix", (G["mix_rows"],))
    dh0, tok = ffn_back("ffn1", W["f1_cols"][l], W["f1_rows"][l], sv["h0"], dh1, tok, l == 0)
    return dh0, G, tok


def _core(x, pe, W, target, S, need=None, on_piece=None):
    h = x
    saved = []
    for l in range(DEPTH):
        h, sv = _layer_fwd(h, pe[l], W, l, S, need)
        saved.append(sv)
    loss_tile, dh, dfn = loss_head(h, W["final_norm"].reshape(1, -1), target, "loss_head")
    grads = [None] * DEPTH
    token = None
    for l in reversed(range(DEPTH)):
        dh, grads[l], token = _layer_bwd(dh, saved[l], pe[l], W, l, S, token, on_piece)
    return loss_tile[0, 0], dh, grads, dfn[0]


MESH_ID = pl.DeviceIdType.MESH
ANY_SPEC = pl.BlockSpec(memory_space=pl.ANY)
AXES = ("x", "y", "c")


def _my_pos():
    return lax.axis_index("x"), lax.axis_index("y"), lax.axis_index("c")


def _slot_of(px, py, pc):
    return 4 * px + 2 * py + pc


def all_gather(x, name):
    R, C = x.shape

    def body(x_ref, out_ref, send_sems, recv_sems, local_sem):
        mx, my, mc = _my_pos()
        me, sibling = (mx, my, mc), (mx, my, 1 - mc)
        chips = [(1 - mx, my), (mx, 1 - my), (1 - mx, 1 - my)]

        def copy(k, block, to, src=None):
            dst = out_ref.at[_slot_of(*block)]
            return pltpu.make_async_remote_copy(
                src_ref=dst if src is None else src, dst_ref=dst,
                send_sem=send_sems.at[k], recv_sem=recv_sems.at[k],
                device_id=to, device_id_type=MESH_ID)

        mine = pltpu.make_async_copy(x_ref, out_ref.at[_slot_of(*me)], local_sem)
        mine.start()
        first = [copy(0, me, sibling, src=x_ref)]
        first += [copy(1 + j, me, (*chip, mc), src=x_ref) for j, chip in enumerate(chips)]
        for cp in first:
            cp.start()
        passed = [copy(4 + j, (*chip, mc), sibling) for j, chip in enumerate(chips)]
        for j, chip in enumerate(chips):
            copy(1 + j, (*chip, mc), me).wait_recv()
            passed[j].start()
        copy(0, sibling, me).wait_recv()
        for j, chip in enumerate(chips):
            copy(4 + j, (*chip, 1 - mc), me).wait_recv()
        for cp in first + passed:
            cp.wait_send()
        mine.wait()

    return pl.pallas_call(
        body, name=name,
        out_shape=jax.ShapeDtypeStruct((N_DEV, R, C), x.dtype),
        in_specs=[ANY_SPEC], out_specs=ANY_SPEC,
        scratch_shapes=[pltpu.SemaphoreType.DMA((7,)), pltpu.SemaphoreType.DMA((7,)), pltpu.SemaphoreType.DMA],
    )(x)


def _col_window(ref, slot):
    return ref.at[:, pl.ds(pl.multiple_of(slot * SHP, LANE), SHP)]


def gather_layer(a_sh, b_sh, name):
    def body(a_ref, b_ref, ao_ref, bo_ref, send_sems, recv_sems, local_sems):
        mx, my, mc = _my_pos()
        me, sibling = (mx, my, mc), (mx, my, 1 - mc)
        chips = [(1 - mx, my), (mx, 1 - my), (1 - mx, 1 - my)]

        def copies(k, block, to, own=False):
            slot = _slot_of(*block)
            dsts = (_col_window(ao_ref, slot), bo_ref.at[slot])
            srcs = (a_ref, b_ref) if own else dsts
            return [pltpu.make_async_remote_copy(
                src_ref=s, dst_ref=d, send_sem=send_sems.at[2 * k + i], recv_sem=recv_sems.at[2 * k + i],
                device_id=to, device_id_type=MESH_ID) for i, (s, d) in enumerate(zip(srcs, dsts))]

        mine = [pltpu.make_async_copy(a_ref, _col_window(ao_ref, _slot_of(*me)), local_sems.at[0]),
                pltpu.make_async_copy(b_ref, bo_ref.at[_slot_of(*me)], local_sems.at[1])]
        for cp in mine:
            cp.start()
        first = copies(0, me, sibling, own=True)
        for j, chip in enumerate(chips):
            first += copies(1 + j, me, (*chip, mc), own=True)
        for cp in first:
            cp.start()
        passed = []
        for j, chip in enumerate(chips):
            for cp in copies(1 + j, (*chip, mc), me):
                cp.wait_recv()
            fwd = copies(4 + j, (*chip, mc), sibling)
            for cp in fwd:
                cp.start()
            passed += fwd
        for cp in copies(0, sibling, me):
            cp.wait_recv()
        for j, chip in enumerate(chips):
            for cp in copies(4 + j, (*chip, 1 - mc), me):
                cp.wait_recv()
        for cp in first + passed:
            cp.wait_send()
        for cp in mine:
            cp.wait()

    return pl.pallas_call(
        body, name=name,
        out_shape=[jax.ShapeDtypeStruct((a_sh.shape[0], FFP), a_sh.dtype),
                   jax.ShapeDtypeStruct((N_DEV,) + b_sh.shape, b_sh.dtype)],
        in_specs=[ANY_SPEC, ANY_SPEC], out_specs=[ANY_SPEC, ANY_SPEC],
        scratch_shapes=[pltpu.SemaphoreType.DMA((14,)), pltpu.SemaphoreType.DMA((14,)),
                        pltpu.SemaphoreType.DMA((2,))],
    )(a_sh, b_sh)


HBM_SPEC = pl.BlockSpec(memory_space=pltpu.HBM)
SEM_SPEC = pl.BlockSpec(memory_space=pltpu.SEMAPHORE)
SPLIT_EFFECT = pltpu.CompilerParams(has_side_effects=pltpu.SideEffectType.DATAFLOW_SIDE_EFFECTING)


def _split_ends(mode, src_ref, dst_ref, src_slot, dst_slot):
    cols = mode.endswith("cols")
    if mode.startswith("gather"):
        return src_ref, (_col_window(dst_ref, dst_slot) if cols else dst_ref.at[dst_slot])
    return (_col_window(src_ref, src_slot) if cols else src_ref.at[src_slot]), dst_ref.at[dst_slot]


def _split_peers():
    mx, my, mc = _my_pos()
    for r in range(1, N_DEV):
        peer = (1 - mx if r & 4 else mx, 1 - my if r & 2 else my, 1 - mc if r & 1 else mc)
        yield r - 1, peer, _slot_of(*peer)


def split_start(modes, srcs, dsts, name, after=()):
    n = len(modes)
    m = len(after)

    def body(*refs):
        send_sems, recv_sems, token = refs[2 * n + m], refs[2 * n + m + 1], refs[-1]
        mine = _slot_of(*_my_pos())
        for k, peer, ps in _split_peers():
            for i in range(n):
                src, dst = _split_ends(modes[i], refs[i], refs[n + i], ps, mine)
                pltpu.make_async_remote_copy(
                    src_ref=src, dst_ref=dst, send_sem=send_sems.at[n * k + i], recv_sem=recv_sems.at[n * k + i],
                    device_id=peer, device_id_type=MESH_ID).start()
        for i in range(n):
            src, dst = _split_ends(modes[i], refs[i], refs[n + i], mine, mine)
            pltpu.make_async_copy(src, dst, recv_sems.at[n * (N_DEV - 1) + i]).start()
        token[...] = jnp.zeros_like(token)

    bufs = tuple(srcs) + tuple(dsts)
    sems = pltpu.SemaphoreType.DMA((n * N_DEV,))
    res = pl.pallas_call(
        body, name=name,
        out_shape=(sems, sems) + tuple(pltpu.HBM(t.shape, t.dtype) for t in bufs)
        + (jax.ShapeDtypeStruct((8, LANE), F32),),
        in_specs=[HBM_SPEC] * (2 * n) + [ANY_SPEC] * m,
        out_specs=(SEM_SPEC, SEM_SPEC) + (HBM_SPEC,) * (2 * n) + (pl.BlockSpec(memory_space=pltpu.VMEM),),
        input_output_aliases={i: 2 + i for i in range(2 * n)},
        compiler_params=SPLIT_EFFECT,
    )(*(pltpu.with_memory_space_constraint(t, pltpu.HBM) for t in bufs), *after)
    return list(res[:-1]), res[-1]


def split_wait(modes, started, after, name):
    n = len(modes)
    after = tuple(after) if isinstance(after, (tuple, list)) else (after,)
    send_sems, recv_sems, bufs = started[0], started[1], started[2:]

    def body(*refs):
        send_sems, recv_sems = refs[2 * n], refs[2 * n + 1]
        mine = _slot_of(*_my_pos())
        for k, peer, ps in _split_peers():
            for i in range(n):
                sent = _split_ends(modes[i], refs[i], refs[n + i], ps, mine)[0]
                landed = _split_ends(modes[i], refs[i], refs[n + i], mine, ps)[1]
                cp = pltpu.make_async_remote_copy(
                    src_ref=sent, dst_ref=landed, send_sem=send_sems.at[n * k + i],
                    recv_sem=recv_sems.at[n * k + i], device_id=peer, device_id_type=MESH_ID)
                cp.wait_send()
                cp.wait_recv()
        for i in range(n):
            src, dst = _split_ends(modes[i], refs[i], refs[n + i], mine, mine)
            pltpu.make_async_copy(src, dst, recv_sems.at[n * (N_DEV - 1) + i]).wait()

    res = pl.pallas_call(
        body, name=name,
        out_shape=tuple(pltpu.HBM(t.shape, t.dtype) for t in bufs),
        in_specs=[HBM_SPEC] * (2 * n) + [SEM_SPEC, SEM_SPEC] + [ANY_SPEC] * len(after),
        out_specs=(HBM_SPEC,) * (2 * n),
        input_output_aliases={i: i for i in range(2 * n)},
        compiler_params=SPLIT_EFFECT,
    )(*bufs, send_sems, recv_sems, *after)
    return list(res[n:])


def sum_parts(parts, name):
    _, R, C = parts.shape
    tr = _pick(R, (512, 336, 272, 256, 128, 64, 32, 16, 8))

    def body(p_ref, o_ref):
        acc = p_ref[0].astype(F32)
        for k in range(1, N_DEV):
            acc += p_ref[k].astype(F32)
        o_ref[...] = acc

    return pl.pallas_call(
        body, name=name, grid=(R // tr,),
        in_specs=[pl.BlockSpec((N_DEV, tr, C), lambda i: (0, i, 0))],
        out_specs=pl.BlockSpec((tr, C), lambda i: (i, 0)),
        out_shape=jax.ShapeDtypeStruct((R, C), F32),
        compiler_params=_cp("parallel"),
    )(parts)


def adamw(g, w, m, v, name):
    R, C = g.shape
    tr = _pick(R, (512, 352, 256, 128, 64, 32, 16, 8))
    c1 = 1.0 - ADAM_B1 ** ADAM_STEP
    c2 = 1.0 - ADAM_B2 ** ADAM_STEP

    def body(g_ref, w_ref, m_ref, v_ref, d_ref, nm_ref, nv_ref):
        gg = g_ref[...]
        mm = ADAM_B1 * m_ref[...] + (1.0 - ADAM_B1) * gg
        vv = ADAM_B2 * v_ref[...] + (1.0 - ADAM_B2) * (gg * gg)
        nm_ref[...] = mm
        nv_ref[...] = vv
        d_ref[...] = -ADAM_LR * ((mm / c1) / (jnp.sqrt(vv / c2) + ADAM_EPS) + ADAM_WD * w_ref[...])

    spec = pl.BlockSpec((tr, C), lambda i: (i, 0))
    return pl.pallas_call(
        body, name=name, grid=(R // tr,),
        in_specs=[spec] * 4, out_specs=[spec] * 3,
        out_shape=[jax.ShapeDtypeStruct((R, C), F32)] * 3,
        compiler_params=_cp("parallel"),
    )(g, w, m, v)


BIG = (("ffn1_w_gate", 1, D, FF), ("ffn1_w_up", 1, D, FF), ("ffn1_w_down", 0, FF, D),
       ("w_in", 1, D, D_IN), ("w_out", 0, D, D),
       ("ffn2_w_gate", 1, D, FF), ("ffn2_w_up", 1, D, FF), ("ffn2_w_down", 0, FF, D),
       ("ple_w_gate", 0, D, D), ("ple_w_proj", 1, PLE, D))
SMALL = (("ffn1_norm", (D,), None), ("mix_norm", (D,), None), ("lru_conv_w", (4, LRU_W), LRU_W // N_DEV),
         ("lru_conv_b", (LRU_W,), None), ("lru_w_a", (4, HD, HD), None), ("lru_b_a", (LRU_W,), None),
         ("lru_w_x", (4, HD, HD), None), ("lru_b_x", (LRU_W,), None), ("lru_lambda", (LRU_W,), None),
         ("attn_sinks", (ATT_H,), None), ("dn_conv_w", (4, 3 * DN_W), 3 * DN_W // N_DEV),
         ("dn_a_log", (DN_H,), None), ("dn_dt_bias", (DN_H,), None), ("dn_norm", (HD,), None),
         ("ffn2_norm", (D,), None), ("ple_norm", (D,), None))
SINGLE = (("rel_bias", (REL_BUCKETS, ATT_H)), ("final_norm", (D,)))


def _pack_rows(arrs, width, mult):
    flat = jnp.concatenate([a.reshape(-1) for a in arrs])
    rows = -(-flat.shape[0] // (width * mult)) * mult
    return jnp.pad(flat, (0, rows * width - flat.shape[0])).reshape(rows, width)


def _unpack_rows(packed, shapes):
    flat = packed.reshape(-1)
    out, off = [], 0
    for s in shapes:
        n = int(np.prod(s))
        out.append(flat[off:off + n].reshape(s))
        off += n
    return out


def _pad_rows(w, r):
    return jnp.pad(w, ((0, r - w.shape[0]), (0, 0)))


def _shard_ffn(a, l, which, more=()):
    cols = jnp.concatenate([a[which + "_w_gate"][l], a[which + "_w_up"][l]], axis=0)
    rows = jnp.concatenate([_pad_rows(a[which + "_w_down"][l], SHP)] + list(more), axis=0)
    return jnp.pad(cols, ((0, 0), (0, SHP - SH))).astype(BF16), rows.astype(BF16)


def _shards(a, l):
    w_in_rows = _pad_rows(a["w_in"][l].reshape(WIN_ROWS, D), IN_ROWS).astype(BF16)
    rest = _shard_ffn(a, l, "ffn2", (a["w_out"][l], a["ple_w_gate"][l], a["ple_w_proj"][l].reshape(-1, D)))
    return _shard_ffn(a, l, "ffn1"), (w_in_rows,), rest


def _full_w_in(in_rows):
    sh = in_rows[:, :WIN_ROWS, :].reshape(N_DEV, D, D_IN // N_DEV)
    return jnp.pad(sh.transpose(1, 0, 2).reshape(D, D_IN), ((0, 0), (0, D_IN_PAD - D_IN)))


def _full_ple_proj(r_rows):
    sh = r_rows[:, R_PPROJ:R_ROWS, :].reshape(N_DEV, PLE, D // N_DEV)
    return sh.transpose(1, 0, 2).reshape(PLE, D)


PIECE_NAMES = {"ffn1": ("ffn1_w_gate", "ffn1_w_up", "ffn1_w_down"), "ffn2": ("ffn2_w_gate", "ffn2_w_up", "ffn2_w_down"),
               "mix": ("w_out", "ple_w_gate", "w_in", "ple_w_proj")}


def _shard_grads(piece, summed):
    if piece == "mix":
        rows, = summed
        return {"w_out": rows[GM_WOUT:GM_WOUT + ROWS_DEV], "ple_w_gate": rows[GM_PGATE:GM_PGATE + ROWS_DEV],
                "w_in": rows[GM_WIN:GM_WIN + WIN_ROWS].reshape(D, D_IN // N_DEV),
                "ple_w_proj": rows[GM_PPROJ:GM_END].reshape(PLE, D // N_DEV)}
    if piece in ("ffn1_gate", "ffn1_up"):
        return {piece.replace("_", "_w_"): summed[0][:, :SH]}
    if piece == "ffn1_down":
        return {"ffn1_w_down": summed[0][:SH]}
    cols, rows = summed
    return {piece + "_w_gate": cols[:D, :SH], piece + "_w_up": cols[D:, :SH], piece + "_w_down": rows[:SH]}


def kernel(x, p, ffn1_norm, ffn1_w_gate, ffn1_w_up, ffn1_w_down, mix_norm, w_in, lru_conv_w, lru_conv_b, lru_w_a, lru_b_a, lru_w_x, lru_b_x, lru_lambda, attn_sinks, rel_bias, dn_conv_w, dn_a_log, dn_dt_bias, dn_norm, w_out, ffn2_norm, ffn2_w_gate, ffn2_w_up, ffn2_w_down, ple_norm, ple_w_gate, ple_w_proj, final_norm, loss_target, m_ffn1_norm, m_ffn1_w_gate, m_ffn1_w_up, m_ffn1_w_down, m_mix_norm, m_w_in, m_lru_conv_w, m_lru_conv_b, m_lru_w_a, m_lru_b_a, m_lru_w_x, m_lru_b_x, m_lru_lambda, m_attn_sinks, m_rel_bias, m_dn_conv_w, m_dn_a_log, m_dn_dt_bias, m_dn_norm, m_w_out, m_ffn2_norm, m_ffn2_w_gate, m_ffn2_w_up, m_ffn2_w_down, m_ple_norm, m_ple_w_gate, m_ple_w_proj, m_final_norm, v_ffn1_norm, v_ffn1_w_gate, v_ffn1_w_up, v_ffn1_w_down, v_mix_norm, v_w_in, v_lru_conv_w, v_lru_conv_b, v_lru_w_a, v_lru_b_a, v_lru_w_x, v_lru_b_x, v_lru_lambda, v_attn_sinks, v_rel_bias, v_dn_conv_w, v_dn_a_log, v_dn_dt_bias, v_dn_norm, v_w_out, v_ffn2_norm, v_ffn2_w_gate, v_ffn2_w_up, v_ffn2_w_down, v_ple_norm, v_ple_w_gate, v_ple_w_proj, v_final_norm):
    a = dict(locals())
    nb, S, _ = x.shape
    T = nb * S
    my_slot = _slot_of(*_my_pos())

    W = {k: [None] * DEPTH for k in ("f1_cols", "f1_rows", "w_in", "r_cols", "r_rows", "ple_w_proj")}
    GATHER, SCATTER = ("gather_cols", "gather_block"), ("scatter_cols", "scatter_block")
    GROUP_MODES = {"f1": GATHER, "in": GATHER[1:], "rest": GATHER}

    def set_group(l, group, bufs):
        if group == "f1":
            W["f1_cols"][l], W["f1_rows"][l] = bufs
        elif group == "in":
            W["w_in"][l] = _full_w_in(bufs[0])
        else:
            W["r_cols"][l], W["r_rows"][l] = bufs
            W["ple_w_proj"][l] = _full_ple_proj(bufs[1])

    def landing(mode, src):
        if mode == "gather_cols":
            return lax.empty((src.shape[0], FFP), src.dtype)
        if mode == "scatter_cols":
            return lax.empty((N_DEV, src.shape[0], SHP), src.dtype)
        return lax.empty((N_DEV,) + src.shape[mode == "scatter_block":], src.dtype)

    def start(modes, srcs, name, after=()):
        return split_start(modes, srcs, [landing(m, s) for m, s in zip(modes, srcs)], name, after)

    shards0, shards1 = _shards(a, 0), _shards(a, 1)
    set_group(0, "f1", gather_layer(*shards0[0], "gather_weights_l0_ffn1"))
    taps = all_gather(_pack_rows([lru_conv_w, dn_conv_w], LANE, 8), "gather_conv_taps")
    tap_shapes = [lru_conv_w.shape, dn_conv_w.shape]
    lcw, dcw = zip(*[_unpack_rows(taps[k], tap_shapes) for k in range(N_DEV)])
    W["lru_conv_w"] = jnp.concatenate(lcw, axis=-1)
    W["dn_conv_w"] = jnp.concatenate(dcw, axis=-1)
    for name, _, cols in SMALL:
        if cols is None:
            W[name] = a[name]
    W["rel_bias"], W["final_norm"] = rel_bias, final_norm

    ALL_MODES = GROUP_MODES["f1"] + GROUP_MODES["in"] + GROUP_MODES["rest"]
    gathers, after = {}, (W["f1_rows"][0], taps)
    for l, group, modes, srcs in ((0, "in", GROUP_MODES["in"], shards0[1]), (0, "rest", GROUP_MODES["rest"], shards0[2]),
                                  (1, "all", ALL_MODES, shards1[0] + shards1[1] + shards1[2])):
        gathers[l, group], token = start(modes, srcs, f"gather_start_l{l}_{group}", after)
        after = (token,)
    W["ffn1_norm"] = ffn1_norm + token[0, 0]
    flight, tokens = {}, {}

    def need(l, group, h):
        if l == 0 and group != "f1":
            set_group(0, group, split_wait(GROUP_MODES[group], gathers[0, group], h, f"gather_wait_l0_{group}"))
        if l == 1 and group == "f1":
            bufs = split_wait(ALL_MODES, gathers[1, "all"], h, "gather_wait_l1_all")
            set_group(1, "f1", bufs[0:2])
            set_group(1, "in", bufs[2:3])
            set_group(1, "rest", bufs[3:5])

    def piece_modes(piece):
        return {"mix": SCATTER[1:], "ffn1_gate": SCATTER[:1], "ffn1_up": SCATTER[:1], "ffn1_down": SCATTER[1:]}.get(
            piece, SCATTER)

    def on_piece(l, piece, bufs):
        bufs = [b.reshape(-1, FFP) if b.shape[-1] == FFP else b for b in bufs]
        flight[l, piece], tokens[l, piece] = start(piece_modes(piece), bufs, f"exchange_start_l{l}_{piece}")
        return tokens[l, piece][0, 0]

    loss_local, dx, grads, d_final = _core(x.reshape(T, D), p.reshape(DEPTH, T, PLE), W,
                                           loss_target.reshape(T, D), S, need, on_piece)
    loss = lax.psum(loss_local, AXES)

    small_full = [jnp.stack([grads[l][name] for l in range(DEPTH)]) for name, _, _ in SMALL]
    small_full += [grads[0]["rel_bias"] + grads[1]["rel_bias"], d_final]
    small_flight, _ = start(("gather_block",), (_pack_rows(small_full, LANE, 8),), "gather_start_small_grads",
                            (tokens[0, "ffn1_down"],))

    out = {}

    shards = [{} for _ in range(DEPTH)]

    def land(l, piece, after):
        parts = split_wait(piece_modes(piece), flight[l, piece], after, f"exchange_wait_l{l}_{piece}")
        shards[l].update(_shard_grads(piece, [sum_parts(t, f"sum_grads_l{l}_{piece}_{i}")
                                              for i, t in enumerate(parts)]))

    def update(piece):
        for name in PIECE_NAMES[piece]:
            g = jnp.stack([shards[l][name] for l in range(DEPTH)])
            shape = a[name].shape
            two_d = lambda t: t.reshape(-1, shape[-1])
            res = adamw(two_d(g), two_d(a[name]), two_d(a["m_" + name]), two_d(a["v_" + name]), "adamw_" + name)
            out[name] = (g,) + tuple(r.reshape(shape) for r in res)

    for piece in ("ffn2", "mix", "ffn1"):
        land(1, piece, (dx, tokens[0, "ffn1_down"]))
    summed1 = tuple(shards[1][PIECE_NAMES[piece][0]] for piece in PIECE_NAMES)
    land(0, "ffn2", summed1)
    land(0, "mix", summed1)
    update("ffn2")
    update("mix")
    done_early = tuple(out[n][1] for n in PIECE_NAMES["ffn2"] + PIECE_NAMES["mix"])
    small_parts, = split_wait(("gather_block",), small_flight, done_early, "gather_wait_small_grads")
    small_sum = sum_parts(small_parts, "sum_small_grads")
    g_small = dict(zip([n for n, _, _ in SMALL] + [n for n, _ in SINGLE],
                       _unpack_rows(small_sum, [s.shape for s in small_full])))
    for name, _, cols in SMALL:
        if cols is not None:
            g_small[name] = lax.dynamic_slice_in_dim(g_small[name], my_slot * cols, cols, axis=2)

    small_names = [n for n, _, _ in SMALL] + [n for n, _ in SINGLE]
    shapes = [a[n].shape for n in small_names]
    packed = [_pack_rows([a[pre + n] if pre is not None else g_small[n] for n in small_names], LANE, 8)
              for pre in (None, "", "m_", "v_")]
    res = adamw(*packed, "adamw_small")
    unpacked = [_unpack_rows(r, shapes) for r in res]
    for i, n in enumerate(small_names):
        out[n] = (g_small[n].reshape(shapes[i]),) + tuple(u[i] for u in unpacked)

    for piece in ("ffn1_gate", "ffn1_up", "ffn1_down"):
        land(0, piece, (res[0],) + done_early)
    update("ffn1")

    order = ['ffn1_norm', 'ffn1_w_gate', 'ffn1_w_up', 'ffn1_w_down', 'mix_norm', 'w_in', 'lru_conv_w', 'lru_conv_b',
             'lru_w_a', 'lru_b_a', 'lru_w_x', 'lru_b_x', 'lru_lambda', 'attn_sinks', 'rel_bias', 'dn_conv_w',
             'dn_a_log', 'dn_dt_bias', 'dn_norm', 'w_out', 'ffn2_norm', 'ffn2_w_gate', 'ffn2_w_up', 'ffn2_w_down',
             'ple_norm', 'ple_w_gate', 'ple_w_proj', 'final_norm']
    return (loss, dx.reshape(x.shape)) + tuple(out[n][k] for k in range(4) for n in order)
```

```python
import functools
import math

import numpy as np
import jax
import jax.numpy as jnp
from jax import lax
from jax.experimental import pallas as pl
from jax.experimental.pallas import tpu as pltpu

F32 = jnp.float32
BF16 = jnp.bfloat16
HI = lax.Precision.HIGHEST

D = 1024
DEPTH = 2
EPS = 1e-6
PLE = 256
FF = 2816
HD = 64
LRU_W = 256
LRU_C = 8.0
ATT_W = 512
ATT_H = 8
ATT_KV = 2
ATT_G = 4
KV_W = 128
WINDOW = 128
BQ = 128
REL_BUCKETS = 32
REL_MAX_DIST = 128
DN_W = 256
DN_H = 4
CHUNK = 64
D_IN = 2312
D_IN_PAD = 2432
N_DEV = 8

ADAM_LR = 0.001
ADAM_B1 = 0.9
ADAM_B2 = 0.999
ADAM_EPS = 1e-08
ADAM_WD = 0.01
ADAM_STEP = 10

LANE = 128
VMEM_LIMIT = 56 * 1024 * 1024
SH = FF // N_DEV
SHP = 384
FFP = N_DEV * SHP
FF_TILE = 2 * SHP
TOK_TILE = 512
R_DOWN2, R_WOUT, R_PGATE, R_PPROJ, R_ROWS = 0, 384, 512, 640, 672
WIN_ROWS = D * D_IN // N_DEV // 1024
IN_ROWS = 304
NEG = -1e30


def _cp(*sem):
    return pltpu.CompilerParams(dimension_semantics=tuple(sem), vmem_limit_bytes=VMEM_LIMIT)


def _dg(a, b, ca, cb, exact):
    dims = (((ca,), (cb,)), ((), ()))
    if exact == "f32":
        return lax.dot_general(a.astype(F32), b.astype(F32), dims, precision=HI, preferred_element_type=F32)
    if exact == "split":
        a_hi, b_hi = a.astype(BF16), b.astype(BF16)
        a_lo = (a - a_hi.astype(F32)).astype(BF16)
        b_lo = (b - b_hi.astype(F32)).astype(BF16)
        dot = lambda u, v: lax.dot_general(u, v, dims, preferred_element_type=F32)
        return dot(a_hi, b_hi) + (dot(a_hi, b_lo) + dot(a_lo, b_hi))
    return lax.dot_general(a.astype(BF16), b.astype(BF16), dims, preferred_element_type=F32)


def _make_mm(exact):
    @jax.custom_vjp
    def mm(a, b):
        return _dg(a, b, 1, 0, exact)

    @jax.custom_vjp
    def mm_nt(a, b):
        return _dg(a, b, 1, 1, exact)

    @jax.custom_vjp
    def mm_tn(a, b):
        return _dg(a, b, 0, 0, exact)

    mm.defvjp(lambda a, b: (mm(a, b), (a, b)),
              lambda r, d: (mm_nt(d, r[1]), mm_tn(r[0], d)))
    mm_nt.defvjp(lambda a, b: (mm_nt(a, b), (a, b)),
                 lambda r, d: (mm(d, r[1]), mm_tn(d, r[0])))
    mm_tn.defvjp(lambda a, b: (mm_tn(a, b), (a, b)),
                 lambda r, d: (mm_nt(r[1], d), mm(r[0], d)))
    return mm, mm_nt, mm_tn


_mm, _mm_nt, _mm_tn = _make_mm("bf16")
_mmx, _mmx_nt, _mmx_tn = _make_mm("f32")
_mm3, _mm3_nt, _mm3_tn = _make_mm("split")


def _iota(shape, dim):
    return lax.broadcasted_iota(jnp.int32, shape, dim)


def _sigmoid(x):
    return 1.0 / (1.0 + jnp.exp(-x))


def _rms(h, g):
    rstd = lax.rsqrt(jnp.mean(h * h, axis=-1, keepdims=True) + EPS)
    xhat = h * rstd
    return xhat * g, xhat, rstd


def _rms_bwd(dxn, xhat, rstd, g):
    dxhat = dxn * g
    dh = rstd * (dxhat - xhat * jnp.mean(dxhat * xhat, axis=-1, keepdims=True))
    dg = jnp.sum(dxn * xhat, axis=0, keepdims=True)
    return dh, dg


def _row_spec(tm, n):
    return pl.BlockSpec((tm, n), lambda i, *_: (i, 0))


def _full_spec(shape):
    nd = len(shape)
    return pl.BlockSpec(shape, lambda *_: (0,) * nd)


def _ffn_weight_specs():
    return [pl.BlockSpec((D, FF_TILE), lambda i, j: (0, j)),
            pl.BlockSpec((D, FF_TILE), lambda i, j: (1, j)),
            pl.BlockSpec((2, SHP, D), lambda i, j: (j, 0, 0))]


def ffn_fwd(h, g, wa, wb, name):
    T = h.shape[0]
    tm = min(TOK_TILE, T)
    nj = FFP // FF_TILE

    def body(h_ref, g_ref, wg_ref, wu_ref, wd_ref, o_ref, gt_ref, up_ref, xn_ref):
        j = pl.program_id(1)

        @pl.when(j == 0)
        def _():
            hh = h_ref[...]
            xn_ref[...] = _rms(hh, g_ref[...])[0].astype(BF16)
            o_ref[...] = hh

        xn = xn_ref[...]
        gt = _mm(xn, wg_ref[...])
        up = _mm(xn, wu_ref[...])
        gt_ref[...] = gt.astype(BF16)
        up_ref[...] = up.astype(BF16)
        act = gt * _sigmoid(gt) * up
        o_ref[...] += 0.5 * _mm(act, wd_ref[...].reshape(FF_TILE, D))

    tile = pl.BlockSpec((tm, FF_TILE), lambda i, j: (i, j))
    return pl.pallas_call(
        body, name=name, grid=(T // tm, nj),
        in_specs=[pl.BlockSpec((tm, D), lambda i, j: (i, 0)),
                  pl.BlockSpec((1, D), lambda i, j: (0, 0))] + _ffn_weight_specs(),
        out_specs=[pl.BlockSpec((tm, D), lambda i, j: (i, 0)), tile, tile,
                   pl.BlockSpec((tm, D), lambda i, j: (i, 0))],
        out_shape=[jax.ShapeDtypeStruct((T, D), F32), jax.ShapeDtypeStruct((T, FFP), BF16),
                   jax.ShapeDtypeStruct((T, FFP), BF16), jax.ShapeDtypeStruct((T, D), BF16)],
        compiler_params=_cp("parallel", "arbitrary"),
    )(h, g, wa, wa, wb)


def ffn_bwd(h, dy, g, gt_saved, up_saved, wa, wb, name):
    T = h.shape[0]
    tm = min(TOK_TILE, T)
    nj = FFP // FF_TILE

    def body(h_ref, dy_ref, g_ref, gt_ref, up_ref, wg_ref, wu_ref, wd_ref,
             dh_ref, dg_ref, du_ref, a_ref, dn_ref, dxn_s):
        i = pl.program_id(0)
        j = pl.program_id(1)

        @pl.when(j == 0)
        def _():
            dxn_s[...] = jnp.zeros_like(dxn_s)

        @pl.when((i == 0) & (j == 0))
        def _():
            dn_ref[...] = jnp.zeros_like(dn_ref)

        gt = gt_ref[...].astype(F32)
        up = up_ref[...].astype(F32)
        sg = _sigmoid(gt)
        si = gt * sg
        da = _mm_nt(0.5 * dy_ref[...], wd_ref[...].reshape(FF_TILE, D))
        dup = da * si
        dgt = da * up * (sg * (1.0 + gt * (1.0 - sg)))
        dg_ref[...] = dgt.astype(BF16)
        du_ref[...] = dup.astype(BF16)
        a_ref[...] = (si * up).astype(BF16)
        dxn_s[...] += _mm_nt(dgt, wg_ref[...]) + _mm_nt(dup, wu_ref[...])

        @pl.when(j == nj - 1)
        def _():
            gg = g_ref[...]
            _, xhat, rstd = _rms(h_ref[...], gg)
            dh, dn = _rms_bwd(dxn_s[...], xhat, rstd, gg)
            dh_ref[...] = dy_ref[...] + dh
            dn_ref[...] += dn

    tile = pl.BlockSpec((tm, FF_TILE), lambda i, j: (i, j))
    return pl.pallas_call(
        body, name=name, grid=(T // tm, nj),
        in_specs=[pl.BlockSpec((tm, D), lambda i, j: (i, 0)),
                  pl.BlockSpec((tm, D), lambda i, j: (i, 0)),
                  pl.BlockSpec((1, D), lambda i, j: (0, 0)), tile, tile] + _ffn_weight_specs(),
        out_specs=[pl.BlockSpec((tm, D), lambda i, j: (i, 0)), tile, tile, tile,
                   pl.BlockSpec((1, D), lambda i, j: (0, 0))],
        out_shape=[jax.ShapeDtypeStruct((T, D), F32)] + [jax.ShapeDtypeStruct((T, FFP), BF16)] * 3
        + [jax.ShapeDtypeStruct((1, D), F32)],
        scratch_shapes=[pltpu.VMEM((tm, D), F32)],
        compiler_params=_cp("arbitrary", "arbitrary"),
    )(h, dy, g, gt_saved, up_saved, wa, wa, wb)


def _pick(n, prefs):
    for t in prefs:
        if n % t == 0:
            return t
    return n


def _tn_body(nk, scale, out_dtype, squeeze):
    def body(a_ref, b_ref, *rest):
        o_ref, acc = rest[-2], rest[-1]
        k = pl.program_id(2)

        @pl.when(k == 0)
        def _():
            acc[...] = jnp.zeros_like(acc)

        acc[...] += _mm_tn(a_ref[...], b_ref[...])

        @pl.when(k == nk - 1)
        def _():
            res = (scale * acc[...]).astype(out_dtype)
            if squeeze:
                o_ref[0] = res
            else:
                o_ref[...] = res

    return body


def matmul_tn(a, b, name, scale=1.0, out_dtype=BF16):
    T, M = a.shape
    N = b.shape[1]
    tmm = _pick(M, (512, 256))
    tnn = _pick(N, (1024, 2432))
    tk = min(TOK_TILE, T)
    nk = T // tk
    return pl.pallas_call(
        _tn_body(nk, scale, out_dtype, False), name=name, grid=(M // tmm, N // tnn, nk),
        in_specs=[pl.BlockSpec((tk, tmm), lambda i, j, k: (k, i)),
                  pl.BlockSpec((tk, tnn), lambda i, j, k: (k, j))],
        out_specs=pl.BlockSpec((tmm, tnn), lambda i, j, k: (i, j)),
        out_shape=jax.ShapeDtypeStruct((M, N), out_dtype),
        scratch_shapes=[pltpu.VMEM((tmm, tnn), F32)],
        compiler_params=_cp("parallel", "parallel", "arbitrary"),
    )(a, b)


def grad_cols(a, b, dst, slot, name):
    T = a.shape[0]
    tmm, tnn = D, FFP // 2
    tk = min(TOK_TILE, T)
    nk = T // tk
    return pl.pallas_call(
        _tn_body(nk, 1.0, BF16, True), name=name, grid=(D // tmm, FFP // tnn, nk),
        in_specs=[pl.BlockSpec((tk, tmm), lambda i, j, k: (k, i)),
                  pl.BlockSpec((tk, tnn), lambda i, j, k: (k, j)),
                  pl.BlockSpec(memory_space=pl.ANY)],
        out_specs=pl.BlockSpec((1, tmm, tnn), lambda i, j, k: (slot, i, j)),
        out_shape=jax.ShapeDtypeStruct(dst.shape, dst.dtype),
        scratch_shapes=[pltpu.VMEM((tmm, tnn), F32)],
        input_output_aliases={2: 0},
        compiler_params=_cp("parallel", "parallel", "arbitrary"),
    )(a, b, dst)


def grad_rows(a, b, dst, row0, rows, name, scale=1.0):
    T = a.shape[0]
    tk = min(TOK_TILE, T)
    nk = T // tk
    blk = row0 // rows

    def body(a_ref, b_ref, dst_ref, o_ref, acc):
        k = pl.program_id(0)

        @pl.when(k == 0)
        def _():
            acc[...] = jnp.zeros_like(acc)

        acc[...] += _mm_tn(a_ref[...], b_ref[...])

        @pl.when(k == nk - 1)
        def _():
            o_ref[...] = (scale * acc[...]).astype(BF16).reshape(N_DEV, rows, D)

    return pl.pallas_call(
        body, name=name, grid=(nk,),
        in_specs=[pl.BlockSpec((tk, N_DEV * rows), lambda k: (k, 0)),
                  pl.BlockSpec((tk, D), lambda k: (k, 0)),
                  pl.BlockSpec(memory_space=pl.ANY)],
        out_specs=pl.BlockSpec((N_DEV, rows, D), lambda k: (0, blk, 0)),
        out_shape=jax.ShapeDtypeStruct(dst.shape, dst.dtype),
        scratch_shapes=[pltpu.VMEM((N_DEV * rows, D), F32)],
        input_output_aliases={2: 0},
        compiler_params=_cp("arbitrary"),
    )(a, b, dst)


U_SPLITS = (512, 768, 1024, 128)
U_OFFS = (0, 512, 1280, 2304)


def mixin_fwd(h, g, w_in, name):
    T = h.shape[0]
    tm = min(TOK_TILE, T)

    def body(h_ref, g_ref, w_ref, u0, u1, u2, u3, xn_ref):
        xn = _rms(h_ref[...], g_ref[...])[0].astype(BF16)
        xn_ref[...] = xn
        u = _mm(xn, w_ref[...])
        for ref, off, n in zip((u0, u1, u2, u3), U_OFFS, U_SPLITS):
            ref[...] = u[:, off:off + n]

    return pl.pallas_call(
        body, name=name, grid=(T // tm,),
        in_specs=[_row_spec(tm, D), _full_spec((1, D)), _full_spec((D, D_IN_PAD))],
        out_specs=[_row_spec(tm, n) for n in U_SPLITS] + [_row_spec(tm, D)],
        out_shape=[jax.ShapeDtypeStruct((T, n), F32) for n in U_SPLITS]
        + [jax.ShapeDtypeStruct((T, D), BF16)],
        compiler_params=_cp("parallel"),
    )(h, g, w_in)


def mixin_bwd(h, dh_in, g, w_in, dus, name):
    T = h.shape[0]
    tm = min(TOK_TILE, T)

    def body(h_ref, dhi_ref, g_ref, w_ref, d0, d1, d2, d3, dh_ref, du_ref, dn_ref):
        @pl.when(pl.program_id(0) == 0)
        def _():
            dn_ref[...] = jnp.zeros_like(dn_ref)

        dxn = jnp.zeros((tm, D), F32)
        for ref, off, n in zip((d0, d1, d2, d3), U_OFFS, U_SPLITS):
            du = ref[...]
            du_ref[:, off:off + n] = du.astype(BF16)
            dxn += _mm_nt(du, w_ref[:, off:off + n])
        gg = g_ref[...]
        _, xhat, rstd = _rms(h_ref[...], gg)
        dh, dn = _rms_bwd(dxn, xhat, rstd, gg)
        dh_ref[...] = dhi_ref[...] + dh
        dn_ref[...] += dn

    return pl.pallas_call(
        body, name=name, grid=(T // tm,),
        in_specs=[_row_spec(tm, D), _row_spec(tm, D), _full_spec((1, D)), _full_spec((D, D_IN_PAD))]
        + [_row_spec(tm, n) for n in U_SPLITS],
        out_specs=[_row_spec(tm, D), _row_spec(tm, D_IN_PAD), _full_spec((1, D))],
        out_shape=[jax.ShapeDtypeStruct((T, D), F32), jax.ShapeDtypeStruct((T, D_IN_PAD), BF16),
                   jax.ShapeDtypeStruct((1, D), F32)],
        compiler_params=_cp("arbitrary"),
    )(h, dh_in, g, w_in, *dus)


def _shift_down(x, s, row):
    if s == 0:
        return x
    return jnp.where(row >= s, pltpu.roll(x, s, 0), 0.0)


def _shift_up(x, s, row):
    if s == 0:
        return x
    n = x.shape[0]
    return jnp.where(row < n - s, pltpu.roll(x, n - s, 0), 0.0)


def conv_fwd(x, w, b, S, col0, C, name):
    T = x.shape[0]
    cb0 = col0 // LANE

    def body(x_ref, w_ref, b_ref, y_ref):
        xx = x_ref[...]
        row = _iota(xx.shape, 0)
        y = xx * w_ref[3:4, :] + b_ref[...]
        for k in range(3):
            y += _shift_down(xx, 3 - k, row) * w_ref[k:k + 1, :]
        y_ref[...] = y

    return pl.pallas_call(
        body, name=name, grid=(T // S, C // LANE),
        in_specs=[pl.BlockSpec((S, LANE), lambda s, c: (s, cb0 + c)),
                  pl.BlockSpec((4, LANE), lambda s, c: (0, c)),
                  pl.BlockSpec((1, LANE), lambda s, c: (0, c))],
        out_specs=pl.BlockSpec((S, LANE), lambda s, c: (s, c)),
        out_shape=jax.ShapeDtypeStruct((T, C), F32),
        compiler_params=_cp("parallel", "parallel"),
    )(x, w, b)


def conv_bwd(x, dy, w, S, col0, C, name):
    T = x.shape[0]
    cb0 = col0 // LANE

    def body(x_ref, dy_ref, w_ref, dx_ref, dwb_ref):
        @pl.when(pl.program_id(1) == 0)
        def _():
            dwb_ref[...] = jnp.zeros_like(dwb_ref)

        xx = x_ref[...]
        dd = dy_ref[...]
        row = _iota(xx.shape, 0)
        dx = dd * w_ref[3:4, :]
        for k in range(3):
            dx += _shift_up(dd, 3 - k, row) * w_ref[k:k + 1, :]
        dx_ref[...] = dx
        for k in range(4):
            dwb_ref[k:k + 1, :] += jnp.sum(dd * _shift_down(xx, 3 - k, row), axis=0, keepdims=True)
        dwb_ref[4:5, :] += jnp.sum(dd, axis=0, keepdims=True)

    return pl.pallas_call(
        body, name=name, grid=(C // LANE, T // S),
        in_specs=[pl.BlockSpec((S, LANE), lambda c, s: (s, cb0 + c)),
                  pl.BlockSpec((S, LANE), lambda c, s: (s, c)),
                  pl.BlockSpec((4, LANE), lambda c, s: (0, c))],
        out_specs=[pl.BlockSpec((S, LANE), lambda c, s: (s, c)),
                   pl.BlockSpec((8, LANE), lambda c, s: (0, c))],
        out_shape=[jax.ShapeDtypeStruct((T, C), F32), jax.ShapeDtypeStruct((8, C), F32)],
        compiler_params=_cp("parallel", "arbitrary"),
    )(x, dy, w)


def _scan(a, b, row):
    n = a.shape[0]
    d = 1
    while d < n:
        keep = row >= d
        b = a * jnp.where(keep, pltpu.roll(b, d, 0), 0.0) + b
        a = a * jnp.where(keep, pltpu.roll(a, d, 0), 1.0)
        d *= 2
    return b


def _rscan(a, b, row):
    n = a.shape[0]
    d = 1
    while d < n:
        keep = row < n - d
        b = a * jnp.where(keep, pltpu.roll(b, n - d, 0), 0.0) + b
        a = a * jnp.where(keep, pltpu.roll(a, n - d, 0), 1.0)
        d *= 2
    return b


GELU_C = math.sqrt(2.0 / math.pi)


def _gelu(x):
    t = jnp.tanh(GELU_C * (x + 0.044715 * (x * x * x)))
    return 0.5 * x * (1.0 + t), t


def _lru_gates(xr, wa, ba, wx, bx, lam):
    r = _sigmoid(_mm(xr, wa) + ba)
    i = _sigmoid(_mm(xr, wx) + bx)
    sp = jnp.maximum(-lam, 0.0) + jnp.log(1.0 + jnp.exp(-jnp.abs(lam)))
    la = -LRU_C * r * sp
    a = jnp.exp(la)
    e2 = a * a
    m = jnp.sqrt(-jnp.tanh(la) * (e2 + 1.0))
    return r, i, sp, a, e2, m


def lru_fwd(xr, u_lru, wa, wx, vec, S, name):
    T = xr.shape[0]

    def body(xr_ref, gt_ref, wa_ref, wx_ref, vec_ref, y_ref):
        x = xr_ref[...]
        row = _iota(x.shape, 0)
        r, i, sp, a, e2, m = _lru_gates(x, wa_ref[...], vec_ref[0:1, :], wx_ref[...], vec_ref[1:2, :],
                                        vec_ref[2:3, :])
        hh = _scan(a, m * (i * x), row)
        y_ref[...] = _gelu(gt_ref[...])[0] * hh

    return pl.pallas_call(
        body, name=name, grid=(T // S, LRU_W // LANE),
        in_specs=[pl.BlockSpec((S, LANE), lambda s, c: (s, c)),
                  pl.BlockSpec((S, LANE), lambda s, c: (s, 2 + c)),
                  pl.BlockSpec((LANE, LANE), lambda s, c: (c, c)),
                  pl.BlockSpec((LANE, LANE), lambda s, c: (c, c)),
                  pl.BlockSpec((8, LANE), lambda s, c: (0, c))],
        out_specs=pl.BlockSpec((S, LANE), lambda s, c: (s, c)),
        out_shape=jax.ShapeDtypeStruct((T, LRU_W), F32),
        compiler_params=_cp("parallel", "parallel"),
    )(xr, u_lru, wa, wx, vec)


def lru_bwd(xr, u_lru, dy, wa, wx, vec, S, name):
    T = xr.shape[0]

    def body(xr_ref, gt_ref, dy_ref, wa_ref, wx_ref, vec_ref,
             dxr_ref, dgt_ref, dwa_ref, dwx_ref, dvec_ref):
        @pl.when(pl.program_id(1) == 0)
        def _():
            dwa_ref[...] = jnp.zeros_like(dwa_ref)
            dwx_ref[...] = jnp.zeros_like(dwx_ref)
            dvec_ref[...] = jnp.zeros_like(dvec_ref)

        x = xr_ref[...]
        n = x.shape[0]
        row = _iota(x.shape, 0)
        lam = vec_ref[2:3, :]
        r, i, sp, a, e2, m = _lru_gates(x, wa_ref[...], vec_ref[0:1, :], wx_ref[...], vec_ref[1:2, :], lam)
        v = i * x
        hh = _scan(a, m * v, row)
        gt = gt_ref[...]
        dy = dy_ref[...]
        ge, t = _gelu(gt)
        dgt_ref[...] = dy * hh * (0.5 * (1.0 + t) + 0.5 * gt * (1.0 - t * t) * GELU_C
                                  * (1.0 + 3.0 * 0.044715 * gt * gt))
        a_next = jnp.where(row < n - 1, pltpu.roll(a, n - 1, 0), 0.0)
        G = _rscan(a_next, dy * ge, row)
        da = G * _shift_down(hh, 1, row)
        dv = G * m
        dla = da * a - (G * v) * e2 / m
        dr = dla * (-LRU_C * sp)
        dsp = jnp.sum(dla * (-LRU_C * r), axis=0, keepdims=True)
        dra = dr * r * (1.0 - r)
        dia = (dv * x) * i * (1.0 - i)
        dxr_ref[...] = dv * i + _mm_nt(dra, wa_ref[...]) + _mm_nt(dia, wx_ref[...])
        dwa_ref[0] += _mm_tn(x, dra)
        dwx_ref[0] += _mm_tn(x, dia)
        dvec_ref[0:1, :] += jnp.sum(dra, axis=0, keepdims=True)
        dvec_ref[1:2, :] += jnp.sum(dia, axis=0, keepdims=True)
        dvec_ref[2:3, :] += dsp * (-_sigmoid(-lam))

    return pl.pallas_call(
        body, name=name, grid=(LRU_W // LANE, T // S),
        in_specs=[pl.BlockSpec((S, LANE), lambda c, s: (s, c)),
                  pl.BlockSpec((S, LANE), lambda c, s: (s, 2 + c)),
                  pl.BlockSpec((S, LANE), lambda c, s: (s, c)),
                  pl.BlockSpec((LANE, LANE), lambda c, s: (c, c)),
                  pl.BlockSpec((LANE, LANE), lambda c, s: (c, c)),
                  pl.BlockSpec((8, LANE), lambda c, s: (0, c))],
        out_specs=[pl.BlockSpec((S, LANE), lambda c, s: (s, c)),
                   pl.BlockSpec((S, LANE), lambda c, s: (s, c)),
                   pl.BlockSpec((1, LANE, LANE), lambda c, s: (c, 0, 0)),
                   pl.BlockSpec((1, LANE, LANE), lambda c, s: (c, 0, 0)),
                   pl.BlockSpec((8, LANE), lambda c, s: (0, c))],
        out_shape=[jax.ShapeDtypeStruct((T, LRU_W), F32), jax.ShapeDtypeStruct((T, LRU_W), F32),
                   jax.ShapeDtypeStruct((2, LANE, LANE), F32), jax.ShapeDtypeStruct((2, LANE, LANE), F32),
                   jax.ShapeDtypeStruct((8, LRU_W), F32)],
        compiler_params=_cp("parallel", "arbitrary"),
    )(xr, u_lru, dy, wa, wx, vec)


def _bucket_table():
    qi = np.arange(BQ)[:, None]
    kj = np.arange(2 * BQ)[None, :]
    dist = BQ + qi - kj
    band = (dist >= 0) & (dist < WINDOW)
    dd = np.maximum(dist, 0)
    max_exact = REL_BUCKETS // 2
    large = max_exact + (np.log(np.maximum(dd, 1).astype(np.float32) / np.float32(max_exact))
                         / np.float32(math.log(REL_MAX_DIST / max_exact))
                         * np.float32(REL_BUCKETS - max_exact)).astype(np.int32)
    large = np.minimum(large, REL_BUCKETS - 1)
    bucket = np.where(dd < max_exact, dd, large)
    return np.where(band, bucket, -1).astype(np.int32)


def _att_specs(S):
    nb = S // BQ
    qc = ATT_W // LANE
    return [pl.BlockSpec((BQ, ATT_W), lambda b, n: (b * nb + n, 0)),
            pl.BlockSpec((BQ, KV_W), lambda b, n: (b * nb + jnp.maximum(n - 1, 0), qc)),
            pl.BlockSpec((BQ, KV_W), lambda b, n: (b * nb + n, qc)),
            pl.BlockSpec((BQ, KV_W), lambda b, n: (b * nb + jnp.maximum(n - 1, 0), qc + 1)),
            pl.BlockSpec((BQ, KV_W), lambda b, n: (b * nb + n, qc + 1))]


def _att_bias(bk, rb_ref, bias_s):
    for h in range(ATT_H):
        acc = jnp.zeros(bk.shape, F32)
        for bb in range(REL_BUCKETS):
            acc = jnp.where(bk == bb, rb_ref[bb * ATT_H + h], acc)
        bias_s[h] = acc


def _att_probs(qs, kgs, bias_s, valid, sk_ref):
    heads = range(ATT_H)
    s = [_mm_nt(qs[h], kgs[h // ATT_G]) for h in heads]
    s = [jnp.where(valid, s[h] * (HD ** -0.5) + bias_s[h], NEG) for h in heads]
    m = [jnp.maximum(jnp.max(s[h], axis=-1, keepdims=True), sk_ref[h]) for h in heads]
    e = [jnp.exp(s[h] - m[h]) for h in heads]
    es = [jnp.exp(sk_ref[h] - m[h]) for h in heads]
    den = [jnp.sum(e[h], axis=-1, keepdims=True) + es[h] for h in heads]
    return [e[h] / den[h] for h in heads], [es[h] / den[h] for h in heads]


def _att_kv(kp_ref, kc_ref, vp_ref, vc_ref):
    cat = lambda a, b, g: jnp.concatenate([a[:, HD * g:HD * (g + 1)], b[:, HD * g:HD * (g + 1)]], axis=0)
    return ([cat(kp_ref, kc_ref, g) for g in range(ATT_KV)], [cat(vp_ref, vc_ref, g) for g in range(ATT_KV)])


def attn_fwd(u_att, sinks, rel_bias, S, name):
    T = u_att.shape[0]
    nb = S // BQ
    table = jnp.asarray(_bucket_table())

    def body(sk_ref, rb_ref, bk_ref, q_ref, kp_ref, kc_ref, vp_ref, vc_ref, o_ref, bias_s):
        b = pl.program_id(0)
        n = pl.program_id(1)
        bk = bk_ref[...]

        @pl.when((b == 0) & (n == 0))
        def _():
            _att_bias(bk, rb_ref, bias_s)

        valid = (bk >= 0) & ((n > 0) | (_iota(bk.shape, 1) >= BQ))
        kgs, vgs = _att_kv(kp_ref, kc_ref, vp_ref, vc_ref)
        p, _ = _att_probs([q_ref[:, HD * h:HD * (h + 1)] for h in range(ATT_H)], kgs, bias_s, valid, sk_ref)
        outs = [_mm(p[h], vgs[h // ATT_G]) for h in range(ATT_H)]
        for h in range(ATT_H):
            o_ref[:, HD * h:HD * (h + 1)] = outs[h]

    smem = pl.BlockSpec(memory_space=pltpu.SMEM)
    return pl.pallas_call(
        body, name=name, grid=(T // S, nb),
        in_specs=[smem, smem, _full_spec((BQ, 2 * BQ))] + _att_specs(S),
        out_specs=pl.BlockSpec((BQ, ATT_W), lambda b, n: (b * nb + n, 0)),
        out_shape=jax.ShapeDtypeStruct((T, ATT_W), F32),
        scratch_shapes=[pltpu.VMEM((ATT_H, BQ, 2 * BQ), F32)],
        compiler_params=_cp("arbitrary", "arbitrary"),
    )(sinks, rel_bias, table, u_att, u_att, u_att, u_att, u_att)


def attn_bwd(u_att, dy, sinks, rel_bias, S, name):
    T = u_att.shape[0]
    nb = S // BQ
    nB = T // S
    table = jnp.asarray(_bucket_table())
    scale = HD ** -0.5

    def body(sk_ref, rb_ref, bk_ref, q_ref, kp_ref, kc_ref, vp_ref, vc_ref, dy_ref,
             du_ref, drel_ref, dsk_ref, bias_s, dbias_s):
        b = pl.program_id(0)
        n = pl.program_id(1)
        bk = bk_ref[...]

        @pl.when((b == 0) & (n == 0))
        def _():
            _att_bias(bk, rb_ref, bias_s)
            dbias_s[...] = jnp.zeros_like(dbias_s)
            dsk_ref[...] = jnp.zeros_like(dsk_ref)
            drel_ref[...] = jnp.zeros_like(drel_ref)

        @pl.when(n == 0)
        def _():
            du_ref[...] = jnp.zeros_like(du_ref)

        valid = (bk >= 0) & ((n > 0) | (_iota(bk.shape, 1) >= BQ))
        r_cur = pl.multiple_of(n * BQ, BQ)
        r_prev = pl.multiple_of(jnp.maximum(n - 1, 0) * BQ, BQ)
        heads = range(ATT_H)
        kgs, vgs = _att_kv(kp_ref, kc_ref, vp_ref, vc_ref)
        qs = [q_ref[:, HD * h:HD * (h + 1)] for h in heads]
        dos = [dy_ref[:, HD * h:HD * (h + 1)] for h in heads]
        p, ps = _att_probs(qs, kgs, bias_s, valid, sk_ref)
        dp = [_mm_nt(dos[h], vgs[h // ATT_G]) for h in heads]
        delta = [jnp.sum(p[h] * dp[h], axis=-1, keepdims=True) for h in heads]
        ds = [p[h] * (dp[h] - delta[h]) for h in heads]
        dss = [ds[h] * scale for h in heads]
        dq = [_mm(dss[h], kgs[h // ATT_G]) for h in heads]
        dks = [_mm_tn(dss[h], qs[h]) for h in heads]
        dvs = [_mm_tn(p[h], dos[h]) for h in heads]
        for h in heads:
            dbias_s[h] += ds[h]
            dsk_ref[h:h + 1, :] += jnp.broadcast_to(jnp.sum(-ps[h] * delta[h], axis=0, keepdims=True), (1, LANE))
            du_ref[pl.ds(r_cur, BQ), HD * h:HD * (h + 1)] = dq[h]
        for g in range(ATT_KV):
            of_group = range(g * ATT_G, (g + 1) * ATT_G)
            dk = functools.reduce(lambda x, y: x + y, [dks[h] for h in of_group])
            dv = functools.reduce(lambda x, y: x + y, [dvs[h] for h in of_group])
            ck = ATT_W + HD * g
            cv = ATT_W + KV_W + HD * g
            du_ref[pl.ds(r_prev, BQ), ck:ck + HD] += dk[0:BQ]
            du_ref[pl.ds(r_cur, BQ), ck:ck + HD] += dk[BQ:]
            du_ref[pl.ds(r_prev, BQ), cv:cv + HD] += dv[0:BQ]
            du_ref[pl.ds(r_cur, BQ), cv:cv + HD] += dv[BQ:]

        @pl.when((b == nB - 1) & (n == nb - 1))
        def _():
            lane = _iota((1, LANE), 1)
            for h in range(ATT_H):
                db = dbias_s[h]
                acc = jnp.zeros((1, LANE), F32)
                for bb in range(REL_BUCKETS):
                    val = jnp.sum(jnp.sum(jnp.where(bk == bb, db, 0.0), axis=1, keepdims=True),
                                  axis=0, keepdims=True)
                    acc = jnp.where(lane == bb, val, acc)
                drel_ref[h:h + 1, :] = acc

    smem = pl.BlockSpec(memory_space=pltpu.SMEM)
    return pl.pallas_call(
        body, name=name, grid=(nB, nb),
        in_specs=[smem, smem, _full_spec((BQ, 2 * BQ))] + _att_specs(S)
        + [pl.BlockSpec((BQ, ATT_W), lambda b, n: (b * nb + n, 0))],
        out_specs=[pl.BlockSpec((S, ATT_W + 2 * KV_W), lambda b, n: (b, 0)),
                   _full_spec((8, LANE)), _full_spec((8, LANE))],
        out_shape=[jax.ShapeDtypeStruct((T, ATT_W + 2 * KV_W), F32),
                   jax.ShapeDtypeStruct((8, LANE), F32), jax.ShapeDtypeStruct((8, LANE), F32)],
        scratch_shapes=[pltpu.VMEM((ATT_H, BQ, 2 * BQ), F32), pltpu.VMEM((ATT_H, BQ, 2 * BQ), F32)],
        compiler_params=_cp("arbitrary", "arbitrary"),
    )(sinks, rel_bias, table, u_att, u_att, u_att, u_att, u_att, dy)


def _head_of(i):
    return lax.shift_right_logical(i, 6)


def _head_mask(shape):
    return (_head_of(_iota(shape, 0)) == _head_of(_iota(shape, 1))).astype(F32)


def _dn_point(c, uba, alog, dtb):
    s = c * _sigmoid(c)
    qt, kt, vt = s[:, 0:256], s[:, 256:512], s[:, 512:768]
    ones_bd = _head_mask((DN_W, DN_W))
    q = qt * lax.rsqrt(_mmx(qt * qt, ones_bd) + EPS) * (HD ** -0.5)
    k = kt * lax.rsqrt(_mmx(kt * kt, ones_bd) + EPS)
    sel = _head_of(_iota((LANE, DN_W), 1))
    row = _iota((LANE, DN_W), 0)
    braw = _mmx(uba, (row == sel).astype(F32))
    araw = _mmx(uba, (row == sel + DN_H).astype(F32)) + dtb
    beta = _sigmoid(braw)
    g = -jnp.exp(alog) * (jnp.maximum(araw, 0.0) + jnp.log(1.0 + jnp.exp(-jnp.abs(araw))))
    return q, k, vt, g, beta


def dn_point_fwd(c, uba, alog, dtb, name):
    T = c.shape[0]
    tm = min(TOK_TILE, T)

    def body(c_ref, u_ref, al_ref, dt_ref, *outs):
        for ref, val in zip(outs, _dn_point(c_ref[...], u_ref[...], al_ref[...], dt_ref[...])):
            ref[...] = val

    return pl.pallas_call(
        body, name=name, grid=(T // tm,),
        in_specs=[_row_spec(tm, 768), _row_spec(tm, LANE), _full_spec((1, DN_W)), _full_spec((1, DN_W))],
        out_specs=[_row_spec(tm, DN_W)] * 5,
        out_shape=[jax.ShapeDtypeStruct((T, DN_W), F32)] * 5,
        compiler_params=_cp("parallel"),
    )(c, uba, alog, dtb)


def dn_point_bwd(c, uba, alog, dtb, douts, name):
    T = c.shape[0]
    tm = min(TOK_TILE, T)

    def body(c_ref, u_ref, al_ref, dt_ref, dq, dk, dv, dg, db, dc_ref, du_ref, dvec_ref):
        @pl.when(pl.program_id(0) == 0)
        def _():
            dvec_ref[...] = jnp.zeros_like(dvec_ref)

        _, vjp = jax.vjp(_dn_point, c_ref[...], u_ref[...], al_ref[...], dt_ref[...])
        dc, du, dal, ddt = vjp((dq[...], dk[...], dv[...], dg[...], db[...]))
        dc_ref[...] = dc
        du_ref[...] = du
        fold = (_iota((LANE, DN_W), 0) == _head_of(_iota((LANE, DN_W), 1))).astype(F32)
        both = jnp.concatenate([dal, ddt, jnp.zeros((6, DN_W), F32)], axis=0)
        dvec_ref[...] += _mmx_nt(both, fold)

    return pl.pallas_call(
        body, name=name, grid=(T // tm,),
        in_specs=[_row_spec(tm, 768), _row_spec(tm, LANE), _full_spec((1, DN_W)), _full_spec((1, DN_W))]
        + [_row_spec(tm, DN_W)] * 5,
        out_specs=[_row_spec(tm, 768), _row_spec(tm, LANE), _full_spec((8, LANE))],
        out_shape=[jax.ShapeDtypeStruct((T, 768), F32), jax.ShapeDtypeStruct((T, LANE), F32),
                   jax.ShapeDtypeStruct((8, LANE), F32)],
        compiler_params=_cp("arbitrary"),
    )(c, uba, alog, dtb, *douts)


def _unit_lower_inverses(lmats):
    eye = (_iota(lmats[0].shape, 0) == _iota(lmats[0].shape, 1)).astype(F32)
    tinvs = [eye - lm for lm in lmats]
    pws = list(lmats)
    for _ in range(5):
        pws = [_mm3(pw, pw) for pw in pws]
        tinvs = [t + _mm3(t, pw) for t, pw in zip(tinvs, pws)]
    return tuple(tinvs)


def _inverse_bwd(tinv, d):
    return -_mm3_nt(_mm3_tn(tinv, d), tinv)


@jax.custom_vjp
def _tri_invs(lmats):
    return _unit_lower_inverses(lmats)


def _tri_invs_fwd(lmats):
    tinvs = _unit_lower_inverses(lmats)
    return tinvs, tinvs


_tri_invs.defvjp(_tri_invs_fwd, lambda tinvs, ds: (tuple(_inverse_bwd(t, d) for t, d in zip(tinvs, ds)),))


@jax.custom_vjp
def _tri_inv_known(lmat, tinv):
    return tinv


_tri_inv_known.defvjp(lambda lmat, tinv: (tinv, tinv),
                      lambda tinv, d: (_inverse_bwd(tinv, d), jnp.zeros_like(tinv)))


DN_SUB = 4


def _dn_stack(x):
    return jnp.concatenate([x, x, x, x], axis=0) * _head_mask((DN_W, DN_W))


def _dn_pre_inverse(q, k, v, g, beta):
    hm = _head_mask((DN_W, DN_W))
    ri = _iota((DN_W, DN_W), 0) & (CHUNK - 1)
    ci = _iota((DN_W, DN_W), 1) & (CHUNK - 1)
    tri64 = (_iota((CHUNK, CHUNK), 0) >= _iota((CHUNK, CHUNK), 1)).astype(F32)
    gc = _mm3(tri64, g)
    ks = _dn_stack(k)
    gcol = jnp.sum(_dn_stack(gc), axis=1, keepdims=True) * (1.0 / HD)
    gmat = jnp.broadcast_to(gcol, (DN_W, DN_W))
    decay = jnp.exp(jnp.minimum(gmat - gmat.T, 0.0))
    lmat = _mm_nt(_dn_stack(k * beta), ks) * decay * (hm * (ri > ci).astype(F32))
    att = _mm_nt(_dn_stack(q), ks) * decay * (hm * (ri >= ci).astype(F32))
    return lmat, att, gc


def _dn_post_inverse(q, k, v, g, beta, tinv, att, gc):
    glast = jnp.sum(g, axis=0, keepdims=True)
    eg = jnp.exp(gc)
    u = _mm(tinv, _dn_stack(v * beta))
    w = _mm(tinv, _dn_stack(k * beta * eg))
    return u, w, att, _dn_stack(q * eg), _dn_stack(k * jnp.exp(glast - gc)), jnp.exp(glast), tinv


def _dn_apply(state, prep):
    u, w, att, qe, kd, eglast, _ = prep
    vn = u - _mm(w, state)
    o4 = _mm(qe, state) + _mm(att, vn)
    o = o4[0:64] + o4[64:128] + o4[128:192] + o4[192:256]
    return o, state * eglast + _mm_tn(kd, vn)


def _dn_chunks(state, q, k, v, g, beta, knowns=None):
    n = q.shape[0] // CHUNK
    chunks = [tuple(x[c * CHUNK:(c + 1) * CHUNK] for x in (q, k, v, g, beta)) for c in range(n)]
    pre = [_dn_pre_inverse(*ch) for ch in chunks]
    if knowns is None:
        tinvs = _tri_invs(tuple(p[0] for p in pre))
    else:
        tinvs = [_tri_inv_known(p[0], known) for p, known in zip(pre, knowns)]
    preps = [_dn_post_inverse(*ch, tinv, p[1], p[2]) for ch, tinv, p in zip(chunks, tinvs, pre)]
    outs = []
    for prep in preps:
        o, state = _dn_apply(state, prep)
        outs.append(o)
    return jnp.concatenate(outs, axis=0), state, [prep[-1] for prep in preps]


def dn_scan_fwd(q, k, v, g, beta, S, name):
    T = q.shape[0]
    rows = DN_SUB * CHUNK
    ns = S // rows

    def body(q_ref, k_ref, v_ref, g_ref, b_ref, o_ref, st_ref, ti_ref, s_s):
        @pl.when(pl.program_id(1) == 0)
        def _():
            s_s[...] = jnp.zeros_like(s_s)

        st = s_s[...]
        st_ref[0] = st
        o, new, tinvs = _dn_chunks(st, q_ref[...], k_ref[...], v_ref[...], g_ref[...], b_ref[...])
        o_ref[...] = o
        for c, tinv in enumerate(tinvs):
            ti_ref[c] = tinv
        s_s[...] = new

    spec = pl.BlockSpec((rows, DN_W), lambda b, t: (b * ns + t, 0))
    return pl.pallas_call(
        body, name=name, grid=(T // S, ns),
        in_specs=[spec] * 5,
        out_specs=[spec, pl.BlockSpec((1, DN_W, DN_W), lambda b, t: (b * ns + t, 0, 0)),
                   pl.BlockSpec((DN_SUB, DN_W, DN_W), lambda b, t: (b * ns + t, 0, 0))],
        out_shape=[jax.ShapeDtypeStruct((T, DN_W), F32),
                   jax.ShapeDtypeStruct((T // rows, DN_W, DN_W), F32),
                   jax.ShapeDtypeStruct((T // CHUNK, DN_W, DN_W), F32)],
        scratch_shapes=[pltpu.VMEM((DN_W, DN_W), F32)],
        compiler_params=_cp("parallel", "arbitrary"),
    )(q, k, v, g, beta)


def dn_scan_bwd(q, k, v, g, beta, states, tinvs, do, S, name):
    T = q.shape[0]
    rows = DN_SUB * CHUNK
    ns = S // rows

    def body(q_ref, k_ref, v_ref, g_ref, b_ref, st_ref, ti_ref, do_ref, dq, dk, dv, dg, db, ds_s):
        @pl.when(pl.program_id(1) == 0)
        def _():
            ds_s[...] = jnp.zeros_like(ds_s)

        knowns = [ti_ref[c] for c in range(DN_SUB)]
        _, vjp = jax.vjp(lambda *args: _dn_chunks(*args, knowns=knowns)[:2],
                         st_ref[0], q_ref[...], k_ref[...], v_ref[...], g_ref[...], b_ref[...])
        grads = vjp((do_ref[...], ds_s[...]))
        ds_s[...] = grads[0]
        for ref, val in zip((dq, dk, dv, dg, db), grads[1:]):
            ref[...] = val

    spec = pl.BlockSpec((rows, DN_W), lambda b, t: (b * ns + ns - 1 - t, 0))
    return pl.pallas_call(
        body, name=name, grid=(T // S, ns),
        in_specs=[spec] * 5 + [pl.BlockSpec((1, DN_W, DN_W), lambda b, t: (b * ns + ns - 1 - t, 0, 0)),
                               pl.BlockSpec((DN_SUB, DN_W, DN_W), lambda b, t: (b * ns + ns - 1 - t, 0, 0)),
                               spec],
        out_specs=[spec] * 5,
        out_shape=[jax.ShapeDtypeStruct((T, DN_W), F32)] * 5,
        scratch_shapes=[pltpu.VMEM((DN_W, DN_W), F32)],
        compiler_params=_cp("parallel", "arbitrary"),
    )(q, k, v, g, beta, states, tinvs, do)


def _dn_gate(o, z, nl):
    ms = _mmx(o * o, _head_mask((DN_W, DN_W))) * (1.0 / HD)
    return o * lax.rsqrt(ms + EPS) * nl * (z * _sigmoid(z))


def dn_gate_fwd(o, u_dn, nl, name):
    T = o.shape[0]
    tm = min(TOK_TILE, T)

    def body(o_ref, z_ref, n_ref, y_ref):
        y_ref[...] = _dn_gate(o_ref[...], z_ref[...], n_ref[...])

    return pl.pallas_call(
        body, name=name, grid=(T // tm,),
        in_specs=[_row_spec(tm, DN_W), pl.BlockSpec((tm, DN_W), lambda i: (i, 3)), _full_spec((1, DN_W))],
        out_specs=_row_spec(tm, DN_W),
        out_shape=jax.ShapeDtypeStruct((T, DN_W), F32),
        compiler_params=_cp("parallel"),
    )(o, u_dn, nl)


def dn_gate_bwd(o, u_dn, nl, dy, name):
    T = o.shape[0]
    tm = min(TOK_TILE, T)

    def body(o_ref, z_ref, n_ref, dy_ref, do_ref, dz_ref, dn_ref):
        @pl.when(pl.program_id(0) == 0)
        def _():
            dn_ref[...] = jnp.zeros_like(dn_ref)

        _, vjp = jax.vjp(_dn_gate, o_ref[...], z_ref[...], n_ref[...])
        do, dz, dn = vjp(dy_ref[...])
        do_ref[...] = do
        dz_ref[...] = dz
        fold = (_iota((LANE, DN_W), 0) == (_iota((LANE, DN_W), 1) & (HD - 1))).astype(F32)
        dn_ref[...] += _mmx_nt(jnp.concatenate([dn, jnp.zeros((7, DN_W), F32)], axis=0), fold)

    return pl.pallas_call(
        body, name=name, grid=(T // tm,),
        in_specs=[_row_spec(tm, DN_W), pl.BlockSpec((tm, DN_W), lambda i: (i, 3)), _full_spec((1, DN_W)),
                  _row_spec(tm, DN_W)],
        out_specs=[_row_spec(tm, DN_W), _row_spec(tm, DN_W), _full_spec((8, LANE))],
        out_shape=[jax.ShapeDtypeStruct((T, DN_W), F32), jax.ShapeDtypeStruct((T, DN_W), F32),
                   jax.ShapeDtypeStruct((8, LANE), F32)],
        compiler_params=_cp("arbitrary"),
    )(o, u_dn, nl, dy)


Y_SPLITS = (LRU_W, ATT_W, DN_W)
Y_OFFS = (0, LRU_W, LRU_W + ATT_W)


ROWS_DEV = D // N_DEV


def _dev_rows_spec(row0):
    return pl.BlockSpec((N_DEV, ROWS_DEV, D), lambda *_: (0, row0 // ROWS_DEV, 0))


def _dev_rows(w_ref, off, n):
    return w_ref[off // ROWS_DEV:(off + n) // ROWS_DEV].reshape(n, D)


def wout_fwd(h, ys, wb, name):
    T = h.shape[0]
    tm = min(TOK_TILE, T)

    def body(h_ref, y0, y1, y2, w_ref, o_ref, yc_ref):
        acc = h_ref[...]
        for ref, off, n in zip((y0, y1, y2), Y_OFFS, Y_SPLITS):
            y = ref[...].astype(BF16)
            yc_ref[:, off:off + n] = y
            acc += _mm(y, _dev_rows(w_ref, off, n))
        o_ref[...] = acc

    return pl.pallas_call(
        body, name=name, grid=(T // tm,),
        in_specs=[_row_spec(tm, D)] + [_row_spec(tm, n) for n in Y_SPLITS] + [_dev_rows_spec(R_WOUT)],
        out_specs=[_row_spec(tm, D), _row_spec(tm, D)],
        out_shape=[jax.ShapeDtypeStruct((T, D), F32), jax.ShapeDtypeStruct((T, D), BF16)],
        compiler_params=_cp("parallel"),
    )(h, *ys, wb)


def wout_bwd(dy, wb, name):
    T = dy.shape[0]
    tm = min(TOK_TILE, T)

    def body(dy_ref, w_ref, d0, d1, d2):
        dd = dy_ref[...].astype(BF16)
        for ref, off, n in zip((d0, d1, d2), Y_OFFS, Y_SPLITS):
            ref[...] = _mm_nt(dd, _dev_rows(w_ref, off, n))

    return pl.pallas_call(
        body, name=name, grid=(T // tm,),
        in_specs=[_row_spec(tm, D), _dev_rows_spec(R_WOUT)],
        out_specs=[_row_spec(tm, n) for n in Y_SPLITS],
        out_shape=[jax.ShapeDtypeStruct((T, n), F32) for n in Y_SPLITS],
        compiler_params=_cp("parallel"),
    )(dy, wb)


def ple_fwd(h, g, pe, wg, wp, name):
    T = h.shape[0]
    tm = min(TOK_TILE, T)

    def body(h_ref, g_ref, p_ref, wg_ref, wp_ref, o_ref):
        hh = h_ref[...]
        xn = _rms(hh, g_ref[...])[0]
        o_ref[...] = hh + _sigmoid(_mm(xn, _dev_rows(wg_ref, 0, D))) * _mm(p_ref[...], wp_ref[...])

    return pl.pallas_call(
        body, name=name, grid=(T // tm,),
        in_specs=[_row_spec(tm, D), _full_spec((1, D)), _row_spec(tm, PLE), _dev_rows_spec(R_PGATE),
                  _full_spec((PLE, D))],
        out_specs=_row_spec(tm, D),
        out_shape=jax.ShapeDtypeStruct((T, D), F32),
        compiler_params=_cp("parallel"),
    )(h, g, pe, wg, wp)


def ple_bwd(h, dy, g, pe, wg, wp, name):
    T = h.shape[0]
    tm = min(TOK_TILE, T)

    def body(h_ref, dy_ref, g_ref, p_ref, wg_ref, wp_ref, dh_ref, dz_ref, dpp_ref, xn_ref, dn_ref):
        @pl.when(pl.program_id(0) == 0)
        def _():
            dn_ref[...] = jnp.zeros_like(dn_ref)

        gg = g_ref[...]
        dy = dy_ref[...]
        xn, xhat, rstd = _rms(h_ref[...], gg)
        wg = _dev_rows(wg_ref, 0, D)
        gate = _sigmoid(_mm(xn, wg))
        pp = _mm(p_ref[...], wp_ref[...])
        dz = dy * pp * gate * (1.0 - gate)
        dz_ref[...] = dz.astype(BF16)
        dpp_ref[...] = (dy * gate).astype(BF16)
        xn_ref[...] = xn.astype(BF16)
        dh, dn = _rms_bwd(_mm_nt(dz, wg), xhat, rstd, gg)
        dh_ref[...] = dy + dh
        dn_ref[...] += dn

    return pl.pallas_call(
        body, name=name, grid=(T // tm,),
        in_specs=[_row_spec(tm, D), _row_spec(tm, D), _full_spec((1, D)), _row_spec(tm, PLE),
                  _dev_rows_spec(R_PGATE), _full_spec((PLE, D))],
        out_specs=[_row_spec(tm, D), _row_spec(tm, D), _row_spec(tm, D), _row_spec(tm, D), _full_spec((1, D))],
        out_shape=[jax.ShapeDtypeStruct((T, D), F32), jax.ShapeDtypeStruct((T, D), BF16),
                   jax.ShapeDtypeStruct((T, D), BF16), jax.ShapeDtypeStruct((T, D), BF16),
                   jax.ShapeDtypeStruct((1, D), F32)],
        compiler_params=_cp("arbitrary"),
    )(h, dy, g, pe, wg, wp)


def loss_head(h, g, target, name):
    T = h.shape[0]
    tm = min(TOK_TILE, T)

    def body(h_ref, g_ref, t_ref, loss_ref, dh_ref, dn_ref):
        @pl.when(pl.program_id(0) == 0)
        def _():
            dn_ref[...] = jnp.zeros_like(dn_ref)
            loss_ref[...] = jnp.zeros_like(loss_ref)

        gg = g_ref[...]
        y, xhat, rstd = _rms(h_ref[...], gg)
        err = y - t_ref[...]
        per_tok = jnp.mean(err * err, axis=-1, keepdims=True)
        loss_ref[...] += 0.5 * jnp.sum(per_tok, axis=0, keepdims=True)
        dh, dn = _rms_bwd(err * (1.0 / D), xhat, rstd, gg)
        dh_ref[...] = dh
        dn_ref[...] += dn

    return pl.pallas_call(
        body, name=name, grid=(T // tm,),
        in_specs=[_row_spec(tm, D), _full_spec((1, D)), _row_spec(tm, D)],
        out_specs=[_full_spec((8, LANE)), _row_spec(tm, D), _full_spec((1, D))],
        out_shape=[jax.ShapeDtypeStruct((8, LANE), F32), jax.ShapeDtypeStruct((T, D), F32),
                   jax.ShapeDtypeStruct((1, D), F32)],
        compiler_params=_cp("arbitrary"),
    )(h, g, target)


def _block_diag(w):
    return jnp.einsum('hij,hk->hikj', w, jnp.eye(4, dtype=w.dtype)).reshape(LRU_W, LRU_W)


def _layer_consts(W, l):
    row = lambda v: v.reshape(1, -1)
    zeros = jnp.zeros((5, LRU_W), F32)
    return dict(
        wa=_block_diag(W["lru_w_a"][l]), wx=_block_diag(W["lru_w_x"][l]),
        lru_vec=jnp.concatenate([row(W["lru_b_a"][l]), row(W["lru_b_x"][l]), row(W["lru_lambda"][l]), zeros], 0),
        lru_cb=row(W["lru_conv_b"][l]),
        sinks=W["attn_sinks"][l], rel=W["rel_bias"].reshape(-1),
        dn_cb=jnp.zeros((1, 3 * DN_W), F32),
        alog=row(jnp.repeat(W["dn_a_log"][l], HD)), dtb=row(jnp.repeat(W["dn_dt_bias"][l], HD)),
        dn_nl=row(jnp.tile(W["dn_norm"][l], DN_H)),
    )


def _layer_fwd(h0, pe, W, l, S, need=None):
    n = f"l{l}_"
    c_ = _layer_consts(W, l)
    row = lambda v: v.reshape(1, -1)
    need = need or (lambda *_: None)
    need(l, "f1", h0)
    h1, *ffn1_kept = ffn_fwd(h0, row(W["ffn1_norm"][l]), W["f1_cols"][l], W["f1_rows"][l], n + "ffn1_fwd")
    need(l, "in", h1)
    u_lru, u_att, u_dn, u_ba, xn_mix = mixin_fwd(h1, row(W["mix_norm"][l]), W["w_in"][l], n + "mixin_fwd")
    xr = conv_fwd(u_lru, W["lru_conv_w"][l], c_["lru_cb"], S, 0, LRU_W, n + "lru_conv_fwd")
    y_lru = lru_fwd(xr, u_lru, c_["wa"], c_["wx"], c_["lru_vec"], S, n + "lru_fwd")
    y_att = attn_fwd(u_att, c_["sinks"], c_["rel"], S, n + "attn_fwd")
    cc = conv_fwd(u_dn, W["dn_conv_w"][l], c_["dn_cb"], S, 0, 3 * DN_W, n + "dn_conv_fwd")
    q, k, v, g, beta = dn_point_fwd(cc, u_ba, c_["alog"], c_["dtb"], n + "dn_point_fwd")
    o, states, tinvs = dn_scan_fwd(q, k, v, g, beta, S, n + "dn_scan_fwd")
    y_dn = dn_gate_fwd(o, u_dn, c_["dn_nl"], n + "dn_gate_fwd")
    need(l, "rest", y_dn)
    h2, ycat = wout_fwd(h1, (y_lru, y_att, y_dn), W["r_rows"][l], n + "wout_fwd")
    h3, *ffn2_kept = ffn_fwd(h2, row(W["ffn2_norm"][l]), W["r_cols"][l], W["r_rows"][l], n + "ffn2_fwd")
    h4 = ple_fwd(h3, row(W["ple_norm"][l]), pe, W["r_rows"][l], W["ple_w_proj"][l], n + "ple_fwd")
    saved = dict(ffn1=ffn1_kept, ffn2=ffn2_kept, h0=h0, h1=h1, h2=h2, h3=h3, u_lru=u_lru, u_att=u_att, u_dn=u_dn,
                 u_ba=u_ba, xn_mix=xn_mix, xr=xr, cc=cc, q=q, k=k, v=v, g=g, beta=beta, o=o, states=states, tinvs=tinvs, ycat=ycat)
    return h4, saved


GM_WOUT, GM_PGATE, GM_WIN, GM_PPROJ, GM_END, GM_ROWS = 0, 128, 256, 560, 592, 640


def _layer_bwd(dh4, sv, pe, W, l, S, token=None, on_piece=None):
    n = f"l{l}_"
    c_ = _layer_consts(W, l)
    row = lambda v: v.reshape(1, -1)
    behind = lambda v, tok: v if tok is None else v + tok.astype(v.dtype)
    on_piece = on_piece or (lambda *_: None)
    G = {"mix_rows": jnp.zeros((N_DEV, GM_ROWS, D), BF16)}
    dh3, dz, dpp, xn_p, dn = ple_bwd(sv["h3"], dh4, behind(row(W["ple_norm"][l]), token), pe, W["r_rows"][l],
                                     W["ple_w_proj"][l], n + "ple_bwd")
    G["ple_norm"] = dn[0]
    G["mix_rows"] = grad_rows(xn_p, dz, G["mix_rows"], GM_PGATE, ROWS_DEV, n + "d_ple_w_gate")
    d_proj = matmul_tn(pe, dpp, n + "d_ple_w_proj")
    d_proj = d_proj.reshape(PLE, N_DEV, D // N_DEV).transpose(1, 0, 2).reshape(N_DEV, GM_END - GM_PPROJ, D)
    G["mix_rows"] = lax.dynamic_update_slice(G["mix_rows"], d_proj, (0, GM_PPROJ, 0))

    def ffn_back(which, cols_w, rows_w, h_in, dy, tok, one_by_one):
        gt, up, xn = sv[which]
        dh, dgt, dup, act, dn_ = ffn_bwd(h_in, dy, behind(row(W[which + "_norm"][l]), tok), gt, up, cols_w, rows_w,
                                         n + which + "_bwd")
        G[which + "_norm"] = dn_[0]
        zeros_rows = jnp.zeros((N_DEV, SHP, D), BF16)
        if one_by_one:
            G[which + "_gate"] = grad_cols(xn, dgt, jnp.zeros((1, D, FFP), BF16), 0, n + "d_" + which + "_w_gate")
            tok = on_piece(l, which + "_gate", (G[which + "_gate"],))
            G[which + "_up"] = grad_cols(xn, dup, behind(jnp.zeros((1, D, FFP), BF16), tok), 0,
                                         n + "d_" + which + "_w_up")
            tok = on_piece(l, which + "_up", (G[which + "_up"],))
            G[which + "_down"] = grad_rows(act, dy, behind(zeros_rows, tok), 0, SHP, n + "d_" + which + "_w_down",
                                           scale=0.5)
            return dh, on_piece(l, which + "_down", (G[which + "_down"],))
        cols = grad_cols(xn, dgt, jnp.zeros((2, D, FFP), BF16), 0, n + "d_" + which + "_w_gate")
        G[which + "_cols"] = grad_cols(xn, dup, cols, 1, n + "d_" + which + "_w_up")
        G[which + "_rows"] = grad_rows(act, dy, zeros_rows, 0, SHP, n + "d_" + which + "_w_down", scale=0.5)
        return dh, on_piece(l, which, (G[which + "_cols"], G[which + "_rows"]))

    dh2, tok = ffn_back("ffn2", W["r_cols"][l], W["r_rows"][l], sv["h2"], dh3, None, False)
    dy_lru, dy_att, dy_dn = wout_bwd(dh2, W["r_rows"][l], n + "wout_bwd")
    G["mix_rows"] = grad_rows(sv["ycat"], dh2, G["mix_rows"], GM_WOUT, ROWS_DEV, n + "d_w_out")
    do, dz_dn, dnn = dn_gate_bwd(sv["o"], sv["u_dn"], behind(c_["dn_nl"], tok), dy_dn, n + "dn_gate_bwd")
    dqkvgb = dn_scan_bwd(sv["q"], sv["k"], sv["v"], sv["g"], sv["beta"], sv["states"], sv["tinvs"], do, S,
                         n + "dn_scan_bwd")
    dcc, du_ba, dvec_dn = dn_point_bwd(sv["cc"], sv["u_ba"], c_["alog"], c_["dtb"], dqkvgb, n + "dn_point_bwd")
    dqkv, dwb_dn = conv_bwd(sv["u_dn"], dcc, W["dn_conv_w"][l], S, 0, 3 * DN_W, n + "dn_conv_bwd")
    du_dn = jnp.concatenate([dqkv, dz_dn], axis=1)
    G["dn_norm"] = dnn[0, 0:HD]
    G["dn_a_log"] = dvec_dn[0, 0:DN_H]
    G["dn_dt_bias"] = dvec_dn[1, 0:DN_H]
    G["dn_conv_w"] = dwb_dn[0:4]
    du_att, drel, dsk = attn_bwd(sv["u_att"], dy_att, c_["sinks"], c_["rel"], S, n + "attn_bwd")
    G["attn_sinks"] = dsk[:, 0]
    G["rel_bias"] = drel[:, 0:REL_BUCKETS].T
    dxr, dgt_lru, dwa, dwx, dvec = lru_bwd(sv["xr"], sv["u_lru"], dy_lru, c_["wa"], c_["wx"], c_["lru_vec"], S,
                                           n + "lru_bwd")
    dx_lru, dwb_lru = conv_bwd(sv["u_lru"], dxr, W["lru_conv_w"][l], S, 0, LRU_W, n + "lru_conv_bwd")
    du_lru = jnp.concatenate([dx_lru, dgt_lru], axis=1)
    diag = lambda m: jnp.stack([m[c, HD * e:HD * (e + 1), HD * e:HD * (e + 1)] for c in range(2) for e in range(2)])
    G["lru_w_a"], G["lru_w_x"] = diag(dwa), diag(dwx)
    G["lru_b_a"], G["lru_b_x"], G["lru_lambda"] = dvec[0], dvec[1], dvec[2]
    G["lru_conv_w"], G["lru_conv_b"] = dwb_lru[0:4], dwb_lru[4]
    dh1, du_cat, dn = mixin_bwd(sv["h1"], dh2, row(W["mix_norm"][l]), W["w_in"][l],
                                (du_lru, du_att, du_dn, du_ba), n + "mixin_bwd")
    G["mix_norm"] = dn[0]
    d_in = matmul_tn(sv["xn_mix"], du_cat, n + "d_w_in")[:, :D_IN]
    d_in = d_in.reshape(D, N_DEV, D_IN // N_DEV).transpose(1, 0, 2).reshape(N_DEV, WIN_ROWS, D)
    d_in = jnp.pad(d_in, ((0, 0), (0, GM_PPROJ - GM_WIN - WIN_ROWS), (0, 0)))
    G["mix_rows"] = lax.dynamic_update_slice(G["mix_rows"], d_in, (0, GM_WIN, 0))
    tok = on_piece(l, "mix", (G["mix_rows"],))
    dh0, tok = ffn_back("ffn1", W["f1_cols"][l], W["f1_rows"][l], sv["h0"], dh1, tok, l == 0)
    return dh0, G, tok


def _core(x, pe, W, target, S, need=None, on_piece=None):
    h = x
    saved = []
    for l in range(DEPTH):
        h, sv = _layer_fwd(h, pe[l], W, l, S, need)
        saved.append(sv)
    loss_tile, dh, dfn = loss_head(h, W["final_norm"].reshape(1, -1), target, "loss_head")
    grads = [None] * DEPTH
    token = None
    for l in reversed(range(DEPTH)):
        dh, grads[l], token = _layer_bwd(dh, saved[l], pe[l], W, l, S, token, on_piece)
    return loss_tile[0, 0], dh, grads, dfn[0]


MESH_ID = pl.DeviceIdType.MESH
ANY_SPEC = pl.BlockSpec(memory_space=pl.ANY)
AXES = ("x", "y", "c")


def _my_pos():
    return lax.axis_index("x"), lax.axis_index("y"), lax.axis_index("c")


def _slot_of(px, py, pc):
    return 4 * px + 2 * py + pc


def all_gather(x, name):
    R, C = x.shape

    def body(x_ref, out_ref, send_sems, recv_sems, local_sem):
        mx, my, mc = _my_pos()
        me, sibling = (mx, my, mc), (mx, my, 1 - mc)
        chips = [(1 - mx, my), (mx, 1 - my), (1 - mx, 1 - my)]

        def copy(k, block, to, src=None):
            dst = out_ref.at[_slot_of(*block)]
            return pltpu.make_async_remote_copy(
                src_ref=dst if src is None else src, dst_ref=dst,
                send_sem=send_sems.at[k], recv_sem=recv_sems.at[k],
                device_id=to, device_id_type=MESH_ID)

        mine = pltpu.make_async_copy(x_ref, out_ref.at[_slot_of(*me)], local_sem)
        mine.start()
        first = [copy(0, me, sibling, src=x_ref)]
        first += [copy(1 + j, me, (*chip, mc), src=x_ref) for j, chip in enumerate(chips)]
        for cp in first:
            cp.start()
        passed = [copy(4 + j, (*chip, mc), sibling) for j, chip in enumerate(chips)]
        for j, chip in enumerate(chips):
            copy(1 + j, (*chip, mc), me).wait_recv()
            passed[j].start()
        copy(0, sibling, me).wait_recv()
        for j, chip in enumerate(chips):
            copy(4 + j, (*chip, 1 - mc), me).wait_recv()
        for cp in first + passed:
            cp.wait_send()
        mine.wait()

    return pl.pallas_call(
        body, name=name,
        out_shape=jax.ShapeDtypeStruct((N_DEV, R, C), x.dtype),
        in_specs=[ANY_SPEC], out_specs=ANY_SPEC,
        scratch_shapes=[pltpu.SemaphoreType.DMA((7,)), pltpu.SemaphoreType.DMA((7,)), pltpu.SemaphoreType.DMA],
    )(x)


def _col_window(ref, slot):
    return ref.at[:, pl.ds(pl.multiple_of(slot * SHP, LANE), SHP)]


def gather_layer(a_sh, b_sh, name):
    def body(a_ref, b_ref, ao_ref, bo_ref, send_sems, recv_sems, local_sems):
        mx, my, mc = _my_pos()
        me, sibling = (mx, my, mc), (mx, my, 1 - mc)
        chips = [(1 - mx, my), (mx, 1 - my), (1 - mx, 1 - my)]

        def copies(k, block, to, own=False):
            slot = _slot_of(*block)
            dsts = (_col_window(ao_ref, slot), bo_ref.at[slot])
            srcs = (a_ref, b_ref) if own else dsts
            return [pltpu.make_async_remote_copy(
                src_ref=s, dst_ref=d, send_sem=send_sems.at[2 * k + i], recv_sem=recv_sems.at[2 * k + i],
                device_id=to, device_id_type=MESH_ID) for i, (s, d) in enumerate(zip(srcs, dsts))]

        mine = [pltpu.make_async_copy(a_ref, _col_window(ao_ref, _slot_of(*me)), local_sems.at[0]),
                pltpu.make_async_copy(b_ref, bo_ref.at[_slot_of(*me)], local_sems.at[1])]
        for cp in mine:
            cp.start()
        first = copies(0, me, sibling, own=True)
        for j, chip in enumerate(chips):
            first += copies(1 + j, me, (*chip, mc), own=True)
        for cp in first:
            cp.start()
        passed = []
        for j, chip in enumerate(chips):
            for cp in copies(1 + j, (*chip, mc), me):
                cp.wait_recv()
            fwd = copies(4 + j, (*chip, mc), sibling)
            for cp in fwd:
                cp.start()
            passed += fwd
        for cp in copies(0, sibling, me):
            cp.wait_recv()
        for j, chip in enumerate(chips):
            for cp in copies(4 + j, (*chip, 1 - mc), me):
                cp.wait_recv()
        for cp in first + passed:
            cp.wait_send()
        for cp in mine:
            cp.wait()

    return pl.pallas_call(
        body, name=name,
        out_shape=[jax.ShapeDtypeStruct((a_sh.shape[0], FFP), a_sh.dtype),
                   jax.ShapeDtypeStruct((N_DEV,) + b_sh.shape, b_sh.dtype)],
        in_specs=[ANY_SPEC, ANY_SPEC], out_specs=[ANY_SPEC, ANY_SPEC],
        scratch_shapes=[pltpu.SemaphoreType.DMA((14,)), pltpu.SemaphoreType.DMA((14,)),
                        pltpu.SemaphoreType.DMA((2,))],
    )(a_sh, b_sh)


HBM_SPEC = pl.BlockSpec(memory_space=pltpu.HBM)
SEM_SPEC = pl.BlockSpec(memory_space=pltpu.SEMAPHORE)
SPLIT_EFFECT = pltpu.CompilerParams(has_side_effects=pltpu.SideEffectType.DATAFLOW_SIDE_EFFECTING)


def _split_ends(mode, src_ref, dst_ref, src_slot, dst_slot):
    cols = mode.endswith("cols")
    if mode.startswith("gather"):
        return src_ref, (_col_window(dst_ref, dst_slot) if cols else dst_ref.at[dst_slot])
    return (_col_window(src_ref, src_slot) if cols else src_ref.at[src_slot]), dst_ref.at[dst_slot]


def _split_peers():
    mx, my, mc = _my_pos()
    for r in range(1, N_DEV):
        peer = (1 - mx if r & 4 else mx, 1 - my if r & 2 else my, 1 - mc if r & 1 else mc)
        yield r - 1, peer, _slot_of(*peer)


def split_start(modes, srcs, dsts, name, after=()):
    n = len(modes)
    m = len(after)

    def body(*refs):
        send_sems, recv_sems, token = refs[2 * n + m], refs[2 * n + m + 1], refs[-1]
        mine = _slot_of(*_my_pos())
        for k, peer, ps in _split_peers():
            for i in range(n):
                src, dst = _split_ends(modes[i], refs[i], refs[n + i], ps, mine)
                pltpu.make_async_remote_copy(
                    src_ref=src, dst_ref=dst, send_sem=send_sems.at[n * k + i], recv_sem=recv_sems.at[n * k + i],
                    device_id=peer, device_id_type=MESH_ID).start()
        for i in range(n):
            src, dst = _split_ends(modes[i], refs[i], refs[n + i], mine, mine)
            pltpu.make_async_copy(src, dst, recv_sems.at[n * (N_DEV - 1) + i]).start()
        token[...] = jnp.zeros_like(token)

    bufs = tuple(srcs) + tuple(dsts)
    sems = pltpu.SemaphoreType.DMA((n * N_DEV,))
    res = pl.pallas_call(
        body, name=name,
        out_shape=(sems, sems) + tuple(pltpu.HBM(t.shape, t.dtype) for t in bufs)
        + (jax.ShapeDtypeStruct((8, LANE), F32),),
        in_specs=[HBM_SPEC] * (2 * n) + [ANY_SPEC] * m,
        out_specs=(SEM_SPEC, SEM_SPEC) + (HBM_SPEC,) * (2 * n) + (pl.BlockSpec(memory_space=pltpu.VMEM),),
        input_output_aliases={i: 2 + i for i in range(2 * n)},
        compiler_params=SPLIT_EFFECT,
    )(*(pltpu.with_memory_space_constraint(t, pltpu.HBM) for t in bufs), *after)
    return list(res[:-1]), res[-1]


def split_wait(modes, started, after, name):
    n = len(modes)
    after = tuple(after) if isinstance(after, (tuple, list)) else (after,)
    send_sems, recv_sems, bufs = started[0], started[1], started[2:]

    def body(*refs):
        send_sems, recv_sems = refs[2 * n], refs[2 * n + 1]
        mine = _slot_of(*_my_pos())
        for k, peer, ps in _split_peers():
            for i in range(n):
                sent = _split_ends(modes[i], refs[i], refs[n + i], ps, mine)[0]
                landed = _split_ends(modes[i], refs[i], refs[n + i], mine, ps)[1]
                cp = pltpu.make_async_remote_copy(
                    src_ref=sent, dst_ref=landed, send_sem=send_sems.at[n * k + i],
                    recv_sem=recv_sems.at[n * k + i], device_id=peer, device_id_type=MESH_ID)
                cp.wait_send()
                cp.wait_recv()
        for i in range(n):
            src, dst = _split_ends(modes[i], refs[i], refs[n + i], mine, mine)
            pltpu.make_async_copy(src, dst, recv_sems.at[n * (N_DEV - 1) + i]).wait()

    res = pl.pallas_call(
        body, name=name,
        out_shape=tuple(pltpu.HBM(t.shape, t.dtype) for t in bufs),
        in_specs=[HBM_SPEC] * (2 * n) + [SEM_SPEC, SEM_SPEC] + [ANY_SPEC] * len(after),
        out_specs=(HBM_SPEC,) * (2 * n),
        input_output_aliases={i: i for i in range(2 * n)},
        compiler_params=SPLIT_EFFECT,
    )(*bufs, send_sems, recv_sems, *after)
    return list(res[n:])


def sum_parts(parts, name):
    _, R, C = parts.shape
    tr = _pick(R, (512, 336, 272, 256, 128, 64, 32, 16, 8))

    def body(p_ref, o_ref):
        acc = p_ref[0].astype(F32)
        for k in range(1, N_DEV):
            acc += p_ref[k].astype(F32)
        o_ref[...] = acc

    return pl.pallas_call(
        body, name=name, grid=(R // tr,),
        in_specs=[pl.BlockSpec((N_DEV, tr, C), lambda i: (0, i, 0))],
        out_specs=pl.BlockSpec((tr, C), lambda i: (i, 0)),
        out_shape=jax.ShapeDtypeStruct((R, C), F32),
        compiler_params=_cp("parallel"),
    )(parts)


def adamw(g, w, m, v, name):
    R, C = g.shape
    tr = _pick(R, (512, 352, 256, 128, 64, 32, 16, 8))
    c1 = 1.0 - ADAM_B1 ** ADAM_STEP
    c2 = 1.0 - ADAM_B2 ** ADAM_STEP

    def body(g_ref, w_ref, m_ref, v_ref, d_ref, nm_ref, nv_ref):
        gg = g_ref[...]
        mm = ADAM_B1 * m_ref[...] + (1.0 - ADAM_B1) * gg
        vv = ADAM_B2 * v_ref[...] + (1.0 - ADAM_B2) * (gg * gg)
        nm_ref[...] = mm
        nv_ref[...] = vv
        d_ref[...] = -ADAM_LR * ((mm / c1) / (jnp.sqrt(vv / c2) + ADAM_EPS) + ADAM_WD * w_ref[...])

    spec = pl.BlockSpec((tr, C), lambda i: (i, 0))
    return pl.pallas_call(
        body, name=name, grid=(R // tr,),
        in_specs=[spec] * 4, out_specs=[spec] * 3,
        out_shape=[jax.ShapeDtypeStruct((R, C), F32)] * 3,
        compiler_params=_cp("parallel"),
    )(g, w, m, v)


BIG = (("ffn1_w_gate", 1, D, FF), ("ffn1_w_up", 1, D, FF), ("ffn1_w_down", 0, FF, D),
       ("w_in", 1, D, D_IN), ("w_out", 0, D, D),
       ("ffn2_w_gate", 1, D, FF), ("ffn2_w_up", 1, D, FF), ("ffn2_w_down", 0, FF, D),
       ("ple_w_gate", 0, D, D), ("ple_w_proj", 1, PLE, D))
SMALL = (("ffn1_norm", (D,), None), ("mix_norm", (D,), None), ("lru_conv_w", (4, LRU_W), LRU_W // N_DEV),
         ("lru_conv_b", (LRU_W,), None), ("lru_w_a", (4, HD, HD), None), ("lru_b_a", (LRU_W,), None),
         ("lru_w_x", (4, HD, HD), None), ("lru_b_x", (LRU_W,), None), ("lru_lambda", (LRU_W,), None),
         ("attn_sinks", (ATT_H,), None), ("dn_conv_w", (4, 3 * DN_W), 3 * DN_W // N_DEV),
         ("dn_a_log", (DN_H,), None), ("dn_dt_bias", (DN_H,), None), ("dn_norm", (HD,), None),
         ("ffn2_norm", (D,), None), ("ple_norm", (D,), None))
SINGLE = (("rel_bias", (REL_BUCKETS, ATT_H)), ("final_norm", (D,)))


def _pack_rows(arrs, width, mult):
    flat = jnp.concatenate([a.reshape(-1) for a in arrs])
    rows = -(-flat.shape[0] // (width * mult)) * mult
    return jnp.pad(flat, (0, rows * width - flat.shape[0])).reshape(rows, width)


def _unpack_rows(packed, shapes):
    flat = packed.reshape(-1)
    out, off = [], 0
    for s in shapes:
        n = int(np.prod(s))
        out.append(flat[off:off + n].reshape(s))
        off += n
    return out


def _pad_rows(w, r):
    return jnp.pad(w, ((0, r - w.shape[0]), (0, 0)))


def _shard_ffn(a, l, which, more=()):
    cols = jnp.concatenate([a[which + "_w_gate"][l], a[which + "_w_up"][l]], axis=0)
    rows = jnp.concatenate([_pad_rows(a[which + "_w_down"][l], SHP)] + list(more), axis=0)
    return jnp.pad(cols, ((0, 0), (0, SHP - SH))).astype(BF16), rows.astype(BF16)


def _shards(a, l):
    w_in_rows = _pad_rows(a["w_in"][l].reshape(WIN_ROWS, D), IN_ROWS).astype(BF16)
    rest = _shard_ffn(a, l, "ffn2", (a["w_out"][l], a["ple_w_gate"][l], a["ple_w_proj"][l].reshape(-1, D)))
    return _shard_ffn(a, l, "ffn1"), (w_in_rows,), rest


def _full_w_in(in_rows):
    sh = in_rows[:, :WIN_ROWS, :].reshape(N_DEV, D, D_IN // N_DEV)
    return jnp.pad(sh.transpose(1, 0, 2).reshape(D, D_IN), ((0, 0), (0, D_IN_PAD - D_IN)))


def _full_ple_proj(r_rows):
    sh = r_rows[:, R_PPROJ:R_ROWS, :].reshape(N_DEV, PLE, D // N_DEV)
    return sh.transpose(1, 0, 2).reshape(PLE, D)


PIECE_NAMES = {"ffn1": ("ffn1_w_gate", "ffn1_w_up", "ffn1_w_down"), "ffn2": ("ffn2_w_gate", "ffn2_w_up", "ffn2_w_down"),
               "mix": ("w_out", "ple_w_gate", "w_in", "ple_w_proj")}


def _shard_grads(piece, summed):
    if piece == "mix":
        rows, = summed
        return {"w_out": rows[GM_WOUT:GM_WOUT + ROWS_DEV], "ple_w_gate": rows[GM_PGATE:GM_PGATE + ROWS_DEV],
                "w_in": rows[GM_WIN:GM_WIN + WIN_ROWS].reshape(D, D_IN // N_DEV),
                "ple_w_proj": rows[GM_PPROJ:GM_END].reshape(PLE, D // N_DEV)}
    if piece in ("ffn1_gate", "ffn1_up"):
        return {piece.replace("_", "_w_"): summed[0][:, :SH]}
    if piece == "ffn1_down":
        return {"ffn1_w_down": summed[0][:SH]}
    cols, rows = summed
    return {piece + "_w_gate": cols[:D, :SH], piece + "_w_up": cols[D:, :SH], piece + "_w_down": rows[:SH]}


def kernel(x, p, ffn1_norm, ffn1_w_gate, ffn1_w_up, ffn1_w_down, mix_norm, w_in, lru_conv_w, lru_conv_b, lru_w_a, lru_b_a, lru_w_x, lru_b_x, lru_lambda, attn_sinks, rel_bias, dn_conv_w, dn_a_log, dn_dt_bias, dn_norm, w_out, ffn2_norm, ffn2_w_gate, ffn2_w_up, ffn2_w_down, ple_norm, ple_w_gate, ple_w_proj, final_norm, loss_target, m_ffn1_norm, m_ffn1_w_gate, m_ffn1_w_up, m_ffn1_w_down, m_mix_norm, m_w_in, m_lru_conv_w, m_lru_conv_b, m_lru_w_a, m_lru_b_a, m_lru_w_x, m_lru_b_x, m_lru_lambda, m_attn_sinks, m_rel_bias, m_dn_conv_w, m_dn_a_log, m_dn_dt_bias, m_dn_norm, m_w_out, m_ffn2_norm, m_ffn2_w_gate, m_ffn2_w_up, m_ffn2_w_down, m_ple_norm, m_ple_w_gate, m_ple_w_proj, m_final_norm, v_ffn1_norm, v_ffn1_w_gate, v_ffn1_w_up, v_ffn1_w_down, v_mix_norm, v_w_in, v_lru_conv_w, v_lru_conv_b, v_lru_w_a, v_lru_b_a, v_lru_w_x, v_lru_b_x, v_lru_lambda, v_attn_sinks, v_rel_bias, v_dn_conv_w, v_dn_a_log, v_dn_dt_bias, v_dn_norm, v_w_out, v_ffn2_norm, v_ffn2_w_gate, v_ffn2_w_up, v_ffn2_w_down, v_ple_norm, v_ple_w_gate, v_ple_w_proj, v_final_norm):
    a = dict(locals())
    nb, S, _ = x.shape
    T = nb * S
    my_slot = _slot_of(*_my_pos())

    W = {k: [None] * DEPTH for k in ("f1_cols", "f1_rows", "w_in", "r_cols", "r_rows", "ple_w_proj")}
    GATHER, SCATTER = ("gather_cols", "gather_block"), ("scatter_cols", "scatter_block")
    GROUP_MODES = {"f1": GATHER, "in": GATHER[1:], "rest": GATHER}

    def set_group(l, group, bufs):
        if group == "f1":
            W["f1_cols"][l], W["f1_rows"][l] = bufs
        elif group == "in":
            W["w_in"][l] = _full_w_in(bufs[0])
        else:
            W["r_cols"][l], W["r_rows"][l] = bufs
            W["ple_w_proj"][l] = _full_ple_proj(bufs[1])

    def landing(mode, src):
        if mode == "gather_cols":
            return lax.empty((src.shape[0], FFP), src.dtype)
        if mode == "scatter_cols":
            return lax.empty((N_DEV, src.shape[0], SHP), src.dtype)
        return lax.empty((N_DEV,) + src.shape[mode == "scatter_block":], src.dtype)

    def start(modes, srcs, name, after=()):
        return split_start(modes, srcs, [landing(m, s) for m, s in zip(modes, srcs)], name, after)

    shards0, shards1 = _shards(a, 0), _shards(a, 1)
    set_group(0, "f1", gather_layer(*shards0[0], "gather_weights_l0_ffn1"))
    taps = all_gather(_pack_rows([lru_conv_w, dn_conv_w], LANE, 8), "gather_conv_taps")
    tap_shapes = [lru_conv_w.shape, dn_conv_w.shape]
    lcw, dcw = zip(*[_unpack_rows(taps[k], tap_shapes) for k in range(N_DEV)])
    W["lru_conv_w"] = jnp.concatenate(lcw, axis=-1)
    W["dn_conv_w"] = jnp.concatenate(dcw, axis=-1)
    for name, _, cols in SMALL:
        if cols is None:
            W[name] = a[name]
    W["rel_bias"], W["final_norm"] = rel_bias, final_norm

    gathers, after = {}, (W["f1_rows"][0], taps)
    for l, group, srcs in ((0, "in", shards0[1]), (0, "rest", shards0[2]),
                           (1, "f1", shards1[0]), (1, "in", shards1[1]), (1, "rest", shards1[2])):
        gathers[l, group], token = start(GROUP_MODES[group], srcs, f"gather_start_l{l}_{group}", after)
        after = (token,)
    W["ffn1_norm"] = ffn1_norm + token[0, 0]
    flight, tokens = {}, {}

    def need(l, group, h):
        if (l, group) in gathers:
            set_group(l, group, split_wait(GROUP_MODES[group], gathers[l, group], h, f"gather_wait_l{l}_{group}"))

    def piece_modes(piece):
        return {"mix": SCATTER[1:], "ffn1_gate": SCATTER[:1], "ffn1_up": SCATTER[:1], "ffn1_down": SCATTER[1:]}.get(
            piece, SCATTER)

    def on_piece(l, piece, bufs):
        bufs = [b.reshape(-1, FFP) if b.shape[-1] == FFP else b for b in bufs]
        flight[l, piece], tokens[l, piece] = start(piece_modes(piece), bufs, f"exchange_start_l{l}_{piece}")
        return tokens[l, piece][0, 0]

    loss_local, dx, grads, d_final = _core(x.reshape(T, D), p.reshape(DEPTH, T, PLE), W,
                                           loss_target.reshape(T, D), S, need, on_piece)
    loss = lax.psum(loss_local, AXES)

    small_full = [jnp.stack([grads[l][name] for l in range(DEPTH)]) for name, _, _ in SMALL]
    small_full += [grads[0]["rel_bias"] + grads[1]["rel_bias"], d_final]
    small_flight, _ = start(("gather_block",), (_pack_rows(small_full, LANE, 8),), "gather_start_small_grads",
                            (tokens[0, "ffn1_down"],))

    out = {}

    shards = [{} for _ in range(DEPTH)]

    def land(l, piece, after):
        parts = split_wait(piece_modes(piece), flight[l, piece], after, f"exchange_wait_l{l}_{piece}")
        shards[l].update(_shard_grads(piece, [sum_parts(t, f"sum_grads_l{l}_{piece}_{i}")
                                              for i, t in enumerate(parts)]))

    def update(piece):
        for name in PIECE_NAMES[piece]:
            g = jnp.stack([shards[l][name] for l in range(DEPTH)])
            shape = a[name].shape
            two_d = lambda t: t.reshape(-1, shape[-1])
            res = adamw(two_d(g), two_d(a[name]), two_d(a["m_" + name]), two_d(a["v_" + name]), "adamw_" + name)
            out[name] = (g,) + tuple(r.reshape(shape) for r in res)

    for piece in ("ffn2", "mix", "ffn1"):
        land(1, piece, (dx, tokens[0, "ffn1_down"]))
    summed1 = tuple(shards[1][PIECE_NAMES[piece][0]] for piece in PIECE_NAMES)
    land(0, "ffn2", summed1)
    land(0, "mix", summed1)
    update("ffn2")
    update("mix")
    done_early = tuple(out[n][1] for n in PIECE_NAMES["ffn2"] + PIECE_NAMES["mix"])
    small_parts, = split_wait(("gather_block",), small_flight, done_early, "gather_wait_small_grads")
    small_sum = sum_parts(small_parts, "sum_small_grads")
    g_small = dict(zip([n for n, _, _ in SMALL] + [n for n, _ in SINGLE],
                       _unpack_rows(small_sum, [s.shape for s in small_full])))
    for name, _, cols in SMALL:
        if cols is not None:
            g_small[name] = lax.dynamic_slice_in_dim(g_small[name], my_slot * cols, cols, axis=2)

    small_names = [n for n, _, _ in SMALL] + [n for n, _ in SINGLE]
    shapes = [a[n].shape for n in small_names]
    packed = [_pack_rows([a[pre + n] if pre is not None else g_small[n] for n in small_names], LANE, 8)
              for pre in (None, "", "m_", "v_")]
    res = adamw(*packed, "adamw_small")
    unpacked = [_unpack_rows(r, shapes) for r in res]
    for i, n in enumerate(small_names):
        out[n] = (g_small[n].reshape(shapes[i]),) + tuple(u[i] for u in unpacked)

    for piece in ("ffn1_gate", "ffn1_up", "ffn1_down"):
        land(0, piece, (res[0],) + done_early)
    update("ffn1")

    order = ['ffn1_norm', 'ffn1_w_gate', 'ffn1_w_up', 'ffn1_w_down', 'mix_norm', 'w_in', 'lru_conv_w', 'lru_conv_b',
             'lru_w_a', 'lru_b_a', 'lru_w_x', 'lru_b_x', 'lru_lambda', 'attn_sinks', 'rel_bias', 'dn_conv_w',
             'dn_a_log', 'dn_dt_bias', 'dn_norm', 'w_out', 'ffn2_norm', 'ffn2_w_gate', 'ffn2_w_up', 'ffn2_w_down',
             'ple_norm', 'ple_w_gate', 'ple_w_proj', 'final_norm']
    return (loss, dx.reshape(x.shape)) + tuple(out[n][k] for k in range(4) for n in order)
```

```python
import functools
import math

import numpy as np
import jax
import jax.numpy as jnp
from jax import lax
from jax.experimental import pallas as pl
from jax.experimental.pallas import tpu as pltpu

F32 = jnp.float32
BF16 = jnp.bfloat16
HI = lax.Precision.HIGHEST

D = 1024
DEPTH = 2
EPS = 1e-6
PLE = 256
FF = 2816
HD = 64
LRU_W = 256
LRU_C = 8.0
ATT_W = 512
ATT_H = 8
ATT_KV = 2
ATT_G = 4
KV_W = 128
WINDOW = 128
BQ = 128
REL_BUCKETS = 32
REL_MAX_DIST = 128
DN_W = 256
DN_H = 4
CHUNK = 64
D_IN = 2312
D_IN_PAD = 2432
N_DEV = 8

ADAM_LR = 0.001
ADAM_B1 = 0.9
ADAM_B2 = 0.999
ADAM_EPS = 1e-08
ADAM_WD = 0.01
ADAM_STEP = 10

LANE = 128
VMEM_LIMIT = 56 * 1024 * 1024
SH = FF // N_DEV
SHP = 384
FFP = N_DEV * SHP
FF_TILE = 2 * SHP
TOK_TILE = 512
R_DOWN2, R_WOUT, R_PGATE, R_PPROJ, R_ROWS = 0, 384, 512, 640, 672
WIN_ROWS = D * D_IN // N_DEV // 1024
IN_ROWS = 304
NEG = -1e30


def _cp(*sem):
    return pltpu.CompilerParams(dimension_semantics=tuple(sem), vmem_limit_bytes=VMEM_LIMIT)


def _dg(a, b, ca, cb, exact):
    dims = (((ca,), (cb,)), ((), ()))
    if exact == "f32":
        return lax.dot_general(a.astype(F32), b.astype(F32), dims, precision=HI, preferred_element_type=F32)
    if exact == "split":
        a_hi, b_hi = a.astype(BF16), b.astype(BF16)
        a_lo = (a - a_hi.astype(F32)).astype(BF16)
        b_lo = (b - b_hi.astype(F32)).astype(BF16)
        dot = lambda u, v: lax.dot_general(u, v, dims, preferred_element_type=F32)
        return dot(a_hi, b_hi) + (dot(a_hi, b_lo) + dot(a_lo, b_hi))
    return lax.dot_general(a.astype(BF16), b.astype(BF16), dims, preferred_element_type=F32)


def _make_mm(exact):
    @jax.custom_vjp
    def mm(a, b):
        return _dg(a, b, 1, 0, exact)

    @jax.custom_vjp
    def mm_nt(a, b):
        return _dg(a, b, 1, 1, exact)

    @jax.custom_vjp
    def mm_tn(a, b):
        return _dg(a, b, 0, 0, exact)

    mm.defvjp(lambda a, b: (mm(a, b), (a, b)),
              lambda r, d: (mm_nt(d, r[1]), mm_tn(r[0], d)))
    mm_nt.defvjp(lambda a, b: (mm_nt(a, b), (a, b)),
                 lambda r, d: (mm(d, r[1]), mm_tn(d, r[0])))
    mm_tn.defvjp(lambda a, b: (mm_tn(a, b), (a, b)),
                 lambda r, d: (mm_nt(r[1], d), mm(r[0], d)))
    return mm, mm_nt, mm_tn


_mm, _mm_nt, _mm_tn = _make_mm("bf16")
_mmx, _mmx_nt, _mmx_tn = _make_mm("f32")
_mm3, _mm3_nt, _mm3_tn = _make_mm("split")


def _iota(shape, dim):
    return lax.broadcasted_iota(jnp.int32, shape, dim)


def _sigmoid(x):
    return 1.0 / (1.0 + jnp.exp(-x))


def _rms(h, g):
    rstd = lax.rsqrt(jnp.mean(h * h, axis=-1, keepdims=True) + EPS)
    xhat = h * rstd
    return xhat * g, xhat, rstd


def _rms_bwd(dxn, xhat, rstd, g):
    dxhat = dxn * g
    dh = rstd * (dxhat - xhat * jnp.mean(dxhat * xhat, axis=-1, keepdims=True))
    dg = jnp.sum(dxn * xhat, axis=0, keepdims=True)
    return dh, dg


def _row_spec(tm, n):
    return pl.BlockSpec((tm, n), lambda i, *_: (i, 0))


def _full_spec(shape):
    nd = len(shape)
    return pl.BlockSpec(shape, lambda *_: (0,) * nd)


def _ffn_weight_specs():
    return [pl.BlockSpec((D, FF_TILE), lambda i, j: (0, j)),
            pl.BlockSpec((D, FF_TILE), lambda i, j: (1, j)),
            pl.BlockSpec((2, SHP, D), lambda i, j: (j, 0, 0))]


def ffn_fwd(h, g, wa, wb, name):
    T = h.shape[0]
    tm = min(TOK_TILE, T)
    nj = FFP // FF_TILE

    def body(h_ref, g_ref, wg_ref, wu_ref, wd_ref, o_ref, gt_ref, up_ref, xn_ref):
        j = pl.program_id(1)

        @pl.when(j == 0)
        def _():
            hh = h_ref[...]
            xn_ref[...] = _rms(hh, g_ref[...])[0].astype(BF16)
            o_ref[...] = hh

        xn = xn_ref[...]
        gt = _mm(xn, wg_ref[...])
        up = _mm(xn, wu_ref[...])
        gt_ref[...] = gt.astype(BF16)
        up_ref[...] = up.astype(BF16)
        act = gt * _sigmoid(gt) * up
        o_ref[...] += 0.5 * _mm(act, wd_ref[...].reshape(FF_TILE, D))

    tile = pl.BlockSpec((tm, FF_TILE), lambda i, j: (i, j))
    return pl.pallas_call(
        body, name=name, grid=(T // tm, nj),
        in_specs=[pl.BlockSpec((tm, D), lambda i, j: (i, 0)),
                  pl.BlockSpec((1, D), lambda i, j: (0, 0))] + _ffn_weight_specs(),
        out_specs=[pl.BlockSpec((tm, D), lambda i, j: (i, 0)), tile, tile,
                   pl.BlockSpec((tm, D), lambda i, j: (i, 0))],
        out_shape=[jax.ShapeDtypeStruct((T, D), F32), jax.ShapeDtypeStruct((T, FFP), BF16),
                   jax.ShapeDtypeStruct((T, FFP), BF16), jax.ShapeDtypeStruct((T, D), BF16)],
        compiler_params=_cp("parallel", "arbitrary"),
    )(h, g, wa, wa, wb)


def ffn_bwd(h, dy, g, gt_saved, up_saved, wa, wb, name):
    T = h.shape[0]
    tm = min(TOK_TILE, T)
    nj = FFP // FF_TILE

    def body(h_ref, dy_ref, g_ref, gt_ref, up_ref, wg_ref, wu_ref, wd_ref,
             dh_ref, dg_ref, du_ref, a_ref, dn_ref, dxn_s):
        i = pl.program_id(0)
        j = pl.program_id(1)

        @pl.when(j == 0)
        def _():
            dxn_s[...] = jnp.zeros_like(dxn_s)

        @pl.when((i == 0) & (j == 0))
        def _():
            dn_ref[...] = jnp.zeros_like(dn_ref)

        gt = gt_ref[...].astype(F32)
        up = up_ref[...].astype(F32)
        sg = _sigmoid(gt)
        si = gt * sg
        da = _mm_nt(0.5 * dy_ref[...], wd_ref[...].reshape(FF_TILE, D))
        dup = da * si
        dgt = da * up * (sg * (1.0 + gt * (1.0 - sg)))
        dg_ref[...] = dgt.astype(BF16)
        du_ref[...] = dup.astype(BF16)
        a_ref[...] = (si * up).astype(BF16)
        dxn_s[...] += _mm_nt(dgt, wg_ref[...]) + _mm_nt(dup, wu_ref[...])

        @pl.when(j == nj - 1)
        def _():
            gg = g_ref[...]
            _, xhat, rstd = _rms(h_ref[...], gg)
            dh, dn = _rms_bwd(dxn_s[...], xhat, rstd, gg)
            dh_ref[...] = dy_ref[...] + dh
            dn_ref[...] += dn

    tile = pl.BlockSpec((tm, FF_TILE), lambda i, j: (i, j))
    return pl.pallas_call(
        body, name=name, grid=(T // tm, nj),
        in_specs=[pl.BlockSpec((tm, D), lambda i, j: (i, 0)),
                  pl.BlockSpec((tm, D), lambda i, j: (i, 0)),
                  pl.BlockSpec((1, D), lambda i, j: (0, 0)), tile, tile] + _ffn_weight_specs(),
        out_specs=[pl.BlockSpec((tm, D), lambda i, j: (i, 0)), tile, tile, tile,
                   pl.BlockSpec((1, D), lambda i, j: (0, 0))],
        out_shape=[jax.ShapeDtypeStruct((T, D), F32)] + [jax.ShapeDtypeStruct((T, FFP), BF16)] * 3
        + [jax.ShapeDtypeStruct((1, D), F32)],
        scratch_shapes=[pltpu.VMEM((tm, D), F32)],
        compiler_params=_cp("arbitrary", "arbitrary"),
    )(h, dy, g, gt_saved, up_saved, wa, wa, wb)


def _pick(n, prefs):
    for t in prefs:
        if n % t == 0:
            return t
    return n


def _tn_body(nk, scale, out_dtype, squeeze):
    def body(a_ref, b_ref, *rest):
        o_ref, acc = rest[-2], rest[-1]
        k = pl.program_id(2)

        @pl.when(k == 0)
        def _():
            acc[...] = jnp.zeros_like(acc)

        acc[...] += _mm_tn(a_ref[...], b_ref[...])

        @pl.when(k == nk - 1)
        def _():
            res = (scale * acc[...]).astype(out_dtype)
            if squeeze:
                o_ref[0] = res
            else:
                o_ref[...] = res

    return body


def matmul_tn(a, b, name, scale=1.0, out_dtype=BF16):
    T, M = a.shape
    N = b.shape[1]
    tmm = _pick(M, (512, 256))
    tnn = _pick(N, (1024, 2432))
    tk = min(TOK_TILE, T)
    nk = T // tk
    return pl.pallas_call(
        _tn_body(nk, scale, out_dtype, False), name=name, grid=(M // tmm, N // tnn, nk),
        in_specs=[pl.BlockSpec((tk, tmm), lambda i, j, k: (k, i)),
                  pl.BlockSpec((tk, tnn), lambda i, j, k: (k, j))],
        out_specs=pl.BlockSpec((tmm, tnn), lambda i, j, k: (i, j)),
        out_shape=jax.ShapeDtypeStruct((M, N), out_dtype),
        scratch_shapes=[pltpu.VMEM((tmm, tnn), F32)],
        compiler_params=_cp("parallel", "parallel", "arbitrary"),
    )(a, b)


def grad_cols(a, b, dst, slot, name):
    T = a.shape[0]
    tmm, tnn = D, FFP // 2
    tk = min(TOK_TILE, T)
    nk = T // tk
    return pl.pallas_call(
        _tn_body(nk, 1.0, BF16, True), name=name, grid=(D // tmm, FFP // tnn, nk),
        in_specs=[pl.BlockSpec((tk, tmm), lambda i, j, k: (k, i)),
                  pl.BlockSpec((tk, tnn), lambda i, j, k: (k, j)),
                  pl.BlockSpec(memory_space=pl.ANY)],
        out_specs=pl.BlockSpec((1, tmm, tnn), lambda i, j, k: (slot, i, j)),
        out_shape=jax.ShapeDtypeStruct(dst.shape, dst.dtype),
        scratch_shapes=[pltpu.VMEM((tmm, tnn), F32)],
        input_output_aliases={2: 0},
        compiler_params=_cp("parallel", "parallel", "arbitrary"),
    )(a, b, dst)


def grad_rows(a, b, dst, row0, rows, name, scale=1.0):
    T = a.shape[0]
    tk = min(TOK_TILE, T)
    nk = T // tk
    blk = row0 // rows

    def body(a_ref, b_ref, dst_ref, o_ref, acc):
        k = pl.program_id(0)

        @pl.when(k == 0)
        def _():
            acc[...] = jnp.zeros_like(acc)

        acc[...] += _mm_tn(a_ref[...], b_ref[...])

        @pl.when(k == nk - 1)
        def _():
            o_ref[...] = (scale * acc[...]).astype(BF16).reshape(N_DEV, rows, D)

    return pl.pallas_call(
        body, name=name, grid=(nk,),
        in_specs=[pl.BlockSpec((tk, N_DEV * rows), lambda k: (k, 0)),
                  pl.BlockSpec((tk, D), lambda k: (k, 0)),
                  pl.BlockSpec(memory_space=pl.ANY)],
        out_specs=pl.BlockSpec((N_DEV, rows, D), lambda k: (0, blk, 0)),
        out_shape=jax.ShapeDtypeStruct(dst.shape, dst.dtype),
        scratch_shapes=[pltpu.VMEM((N_DEV * rows, D), F32)],
        input_output_aliases={2: 0},
        compiler_params=_cp("arbitrary"),
    )(a, b, dst)


U_SPLITS = (512, 768, 1024, 128)
U_OFFS = (0, 512, 1280, 2304)


def mixin_fwd(h, g, w_in, name):
    T = h.shape[0]
    tm = min(TOK_TILE, T)

    def body(h_ref, g_ref, w_ref, u0, u1, u2, u3, xn_ref):
        xn = _rms(h_ref[...], g_ref[...])[0].astype(BF16)
        xn_ref[...] = xn
        u = _mm(xn, w_ref[...])
        for ref, off, n in zip((u0, u1, u2, u3), U_OFFS, U_SPLITS):
            ref[...] = u[:, off:off + n]

    return pl.pallas_call(
        body, name=name, grid=(T // tm,),
        in_specs=[_row_spec(tm, D), _full_spec((1, D)), _full_spec((D, D_IN_PAD))],
        out_specs=[_row_spec(tm, n) for n in U_SPLITS] + [_row_spec(tm, D)],
        out_shape=[jax.ShapeDtypeStruct((T, n), F32) for n in U_SPLITS]
        + [jax.ShapeDtypeStruct((T, D), BF16)],
        compiler_params=_cp("parallel"),
    )(h, g, w_in)


def mixin_bwd(h, dh_in, g, w_in, dus, name):
    T = h.shape[0]
    tm = min(TOK_TILE, T)

    def body(h_ref, dhi_ref, g_ref, w_ref, d0, d1, d2, d3, dh_ref, du_ref, dn_ref):
        @pl.when(pl.program_id(0) == 0)
        def _():
            dn_ref[...] = jnp.zeros_like(dn_ref)

        dxn = jnp.zeros((tm, D), F32)
        for ref, off, n in zip((d0, d1, d2, d3), U_OFFS, U_SPLITS):
            du = ref[...]
            du_ref[:, off:off + n] = du.astype(BF16)
            dxn += _mm_nt(du, w_ref[:, off:off + n])
        gg = g_ref[...]
        _, xhat, rstd = _rms(h_ref[...], gg)
        dh, dn = _rms_bwd(dxn, xhat, rstd, gg)
        dh_ref[...] = dhi_ref[...] + dh
        dn_ref[...] += dn

    return pl.pallas_call(
        body, name=name, grid=(T // tm,),
        in_specs=[_row_spec(tm, D), _row_spec(tm, D), _full_spec((1, D)), _full_spec((D, D_IN_PAD))]
        + [_row_spec(tm, n) for n in U_SPLITS],
        out_specs=[_row_spec(tm, D), _row_spec(tm, D_IN_PAD), _full_spec((1, D))],
        out_shape=[jax.ShapeDtypeStruct((T, D), F32), jax.ShapeDtypeStruct((T, D_IN_PAD), BF16),
                   jax.ShapeDtypeStruct((1, D), F32)],
        compiler_params=_cp("arbitrary"),
    )(h, dh_in, g, w_in, *dus)


def _shift_down(x, s, row):
    if s == 0:
        return x
    return jnp.where(row >= s, pltpu.roll(x, s, 0), 0.0)


def _shift_up(x, s, row):
    if s == 0:
        return x
    n = x.shape[0]
    return jnp.where(row < n - s, pltpu.roll(x, n - s, 0), 0.0)


def conv_fwd(x, w, b, S, col0, C, name):
    T = x.shape[0]
    cb0 = col0 // LANE

    def body(x_ref, w_ref, b_ref, y_ref):
        xx = x_ref[...]
        row = _iota(xx.shape, 0)
        y = xx * w_ref[3:4, :] + b_ref[...]
        for k in range(3):
            y += _shift_down(xx, 3 - k, row) * w_ref[k:k + 1, :]
        y_ref[...] = y

    return pl.pallas_call(
        body, name=name, grid=(T // S, C // LANE),
        in_specs=[pl.BlockSpec((S, LANE), lambda s, c: (s, cb0 + c)),
                  pl.BlockSpec((4, LANE), lambda s, c: (0, c)),
                  pl.BlockSpec((1, LANE), lambda s, c: (0, c))],
        out_specs=pl.BlockSpec((S, LANE), lambda s, c: (s, c)),
        out_shape=jax.ShapeDtypeStruct((T, C), F32),
        compiler_params=_cp("parallel", "parallel"),
    )(x, w, b)


def conv_bwd(x, dy, w, S, col0, C, name):
    T = x.shape[0]
    cb0 = col0 // LANE

    def body(x_ref, dy_ref, w_ref, dx_ref, dwb_ref):
        @pl.when(pl.program_id(1) == 0)
        def _():
            dwb_ref[...] = jnp.zeros_like(dwb_ref)

        xx = x_ref[...]
        dd = dy_ref[...]
        row = _iota(xx.shape, 0)
        dx = dd * w_ref[3:4, :]
        for k in range(3):
            dx += _shift_up(dd, 3 - k, row) * w_ref[k:k + 1, :]
        dx_ref[...] = dx
        for k in range(4):
            dwb_ref[k:k + 1, :] += jnp.sum(dd * _shift_down(xx, 3 - k, row), axis=0, keepdims=True)
        dwb_ref[4:5, :] += jnp.sum(dd, axis=0, keepdims=True)

    return pl.pallas_call(
        body, name=name, grid=(C // LANE, T // S),
        in_specs=[pl.BlockSpec((S, LANE), lambda c, s: (s, cb0 + c)),
                  pl.BlockSpec((S, LANE), lambda c, s: (s, c)),
                  pl.BlockSpec((4, LANE), lambda c, s: (0, c))],
        out_specs=[pl.BlockSpec((S, LANE), lambda c, s: (s, c)),
                   pl.BlockSpec((8, LANE), lambda c, s: (0, c))],
        out_shape=[jax.ShapeDtypeStruct((T, C), F32), jax.ShapeDtypeStruct((8, C), F32)],
        compiler_params=_cp("parallel", "arbitrary"),
    )(x, dy, w)


def _scan(a, b, row):
    n = a.shape[0]
    d = 1
    while d < n:
        keep = row >= d
        b = a * jnp.where(keep, pltpu.roll(b, d, 0), 0.0) + b
        a = a * jnp.where(keep, pltpu.roll(a, d, 0), 1.0)
        d *= 2
    return b


def _rscan(a, b, row):
    n = a.shape[0]
    d = 1
    while d < n:
        keep = row < n - d
        b = a * jnp.where(keep, pltpu.roll(b, n - d, 0), 0.0) + b
        a = a * jnp.where(keep, pltpu.roll(a, n - d, 0), 1.0)
        d *= 2
    return b


GELU_C = math.sqrt(2.0 / math.pi)


def _gelu(x):
    t = jnp.tanh(GELU_C * (x + 0.044715 * (x * x * x)))
    return 0.5 * x * (1.0 + t), t


def _lru_gates(xr, wa, ba, wx, bx, lam):
    r = _sigmoid(_mm(xr, wa) + ba)
    i = _sigmoid(_mm(xr, wx) + bx)
    sp = jnp.maximum(-lam, 0.0) + jnp.log(1.0 + jnp.exp(-jnp.abs(lam)))
    la = -LRU_C * r * sp
    a = jnp.exp(la)
    e2 = a * a
    m = jnp.sqrt(-jnp.tanh(la) * (e2 + 1.0))
    return r, i, sp, a, e2, m


def lru_fwd(xr, u_lru, wa, wx, vec, S, name):
    T = xr.shape[0]

    def body(xr_ref, gt_ref, wa_ref, wx_ref, vec_ref, y_ref):
        x = xr_ref[...]
        row = _iota(x.shape, 0)
        r, i, sp, a, e2, m = _lru_gates(x, wa_ref[...], vec_ref[0:1, :], wx_ref[...], vec_ref[1:2, :],
                                        vec_ref[2:3, :])
        hh = _scan(a, m * (i * x), row)
        y_ref[...] = _gelu(gt_ref[...])[0] * hh

    return pl.pallas_call(
        body, name=name, grid=(T // S, LRU_W // LANE),
        in_specs=[pl.BlockSpec((S, LANE), lambda s, c: (s, c)),
                  pl.BlockSpec((S, LANE), lambda s, c: (s, 2 + c)),
                  pl.BlockSpec((LANE, LANE), lambda s, c: (c, c)),
                  pl.BlockSpec((LANE, LANE), lambda s, c: (c, c)),
                  pl.BlockSpec((8, LANE), lambda s, c: (0, c))],
        out_specs=pl.BlockSpec((S, LANE), lambda s, c: (s, c)),
        out_shape=jax.ShapeDtypeStruct((T, LRU_W), F32),
        compiler_params=_cp("parallel", "parallel"),
    )(xr, u_lru, wa, wx, vec)


def lru_bwd(xr, u_lru, dy, wa, wx, vec, S, name):
    T = xr.shape[0]

    def body(xr_ref, gt_ref, dy_ref, wa_ref, wx_ref, vec_ref,
             dxr_ref, dgt_ref, dwa_ref, dwx_ref, dvec_ref):
        @pl.when(pl.program_id(1) == 0)
        def _():
            dwa_ref[...] = jnp.zeros_like(dwa_ref)
            dwx_ref[...] = jnp.zeros_like(dwx_ref)
            dvec_ref[...] = jnp.zeros_like(dvec_ref)

        x = xr_ref[...]
        n = x.shape[0]
        row = _iota(x.shape, 0)
        lam = vec_ref[2:3, :]
        r, i, sp, a, e2, m = _lru_gates(x, wa_ref[...], vec_ref[0:1, :], wx_ref[...], vec_ref[1:2, :], lam)
        v = i * x
        hh = _scan(a, m * v, row)
        gt = gt_ref[...]
        dy = dy_ref[...]
        ge, t = _gelu(gt)
        dgt_ref[...] = dy * hh * (0.5 * (1.0 + t) + 0.5 * gt * (1.0 - t * t) * GELU_C
                                  * (1.0 + 3.0 * 0.044715 * gt * gt))
        a_next = jnp.where(row < n - 1, pltpu.roll(a, n - 1, 0), 0.0)
        G = _rscan(a_next, dy * ge, row)
        da = G * _shift_down(hh, 1, row)
        dv = G * m
        dla = da * a - (G * v) * e2 / m
        dr = dla * (-LRU_C * sp)
        dsp = jnp.sum(dla * (-LRU_C * r), axis=0, keepdims=True)
        dra = dr * r * (1.0 - r)
        dia = (dv * x) * i * (1.0 - i)
        dxr_ref[...] = dv * i + _mm_nt(dra, wa_ref[...]) + _mm_nt(dia, wx_ref[...])
        dwa_ref[0] += _mm_tn(x, dra)
        dwx_ref[0] += _mm_tn(x, dia)
        dvec_ref[0:1, :] += jnp.sum(dra, axis=0, keepdims=True)
        dvec_ref[1:2, :] += jnp.sum(dia, axis=0, keepdims=True)
        dvec_ref[2:3, :] += dsp * (-_sigmoid(-lam))

    return pl.pallas_call(
        body, name=name, grid=(LRU_W // LANE, T // S),
        in_specs=[pl.BlockSpec((S, LANE), lambda c, s: (s, c)),
                  pl.BlockSpec((S, LANE), lambda c, s: (s, 2 + c)),
                  pl.BlockSpec((S, LANE), lambda c, s: (s, c)),
                  pl.BlockSpec((LANE, LANE), lambda c, s: (c, c)),
                  pl.BlockSpec((LANE, LANE), lambda c, s: (c, c)),
                  pl.BlockSpec((8, LANE), lambda c, s: (0, c))],
        out_specs=[pl.BlockSpec((S, LANE), lambda c, s: (s, c)),
                   pl.BlockSpec((S, LANE), lambda c, s: (s, c)),
                   pl.BlockSpec((1, LANE, LANE), lambda c, s: (c, 0, 0)),
                   pl.BlockSpec((1, LANE, LANE), lambda c, s: (c, 0, 0)),
                   pl.BlockSpec((8, LANE), lambda c, s: (0, c))],
        out_shape=[jax.ShapeDtypeStruct((T, LRU_W), F32), jax.ShapeDtypeStruct((T, LRU_W), F32),
                   jax.ShapeDtypeStruct((2, LANE, LANE), F32), jax.ShapeDtypeStruct((2, LANE, LANE), F32),
                   jax.ShapeDtypeStruct((8, LRU_W), F32)],
        compiler_params=_cp("parallel", "arbitrary"),
    )(xr, u_lru, dy, wa, wx, vec)


def _bucket_table():
    qi = np.arange(BQ)[:, None]
    kj = np.arange(2 * BQ)[None, :]
    dist = BQ + qi - kj
    band = (dist >= 0) & (dist < WINDOW)
    dd = np.maximum(dist, 0)
    max_exact = REL_BUCKETS // 2
    large = max_exact + (np.log(np.maximum(dd, 1).astype(np.float32) / np.float32(max_exact))
                         / np.float32(math.log(REL_MAX_DIST / max_exact))
                         * np.float32(REL_BUCKETS - max_exact)).astype(np.int32)
    large = np.minimum(large, REL_BUCKETS - 1)
    bucket = np.where(dd < max_exact, dd, large)
    return np.where(band, bucket, -1).astype(np.int32)


def _att_specs(S):
    nb = S // BQ
    qc = ATT_W // LANE
    return [pl.BlockSpec((BQ, ATT_W), lambda b, n: (b * nb + n, 0)),
            pl.BlockSpec((BQ, KV_W), lambda b, n: (b * nb + jnp.maximum(n - 1, 0), qc)),
            pl.BlockSpec((BQ, KV_W), lambda b, n: (b * nb + n, qc)),
            pl.BlockSpec((BQ, KV_W), lambda b, n: (b * nb + jnp.maximum(n - 1, 0), qc + 1)),
            pl.BlockSpec((BQ, KV_W), lambda b, n: (b * nb + n, qc + 1))]


def _att_bias(bk, rb_ref, bias_s):
    for h in range(ATT_H):
        acc = jnp.zeros(bk.shape, F32)
        for bb in range(REL_BUCKETS):
            acc = jnp.where(bk == bb, rb_ref[bb * ATT_H + h], acc)
        bias_s[h] = acc


def _att_probs(qs, kgs, bias_s, valid, sk_ref):
    heads = range(ATT_H)
    s = [_mm_nt(qs[h], kgs[h // ATT_G]) for h in heads]
    s = [jnp.where(valid, s[h] * (HD ** -0.5) + bias_s[h], NEG) for h in heads]
    m = [jnp.maximum(jnp.max(s[h], axis=-1, keepdims=True), sk_ref[h]) for h in heads]
    e = [jnp.exp(s[h] - m[h]) for h in heads]
    es = [jnp.exp(sk_ref[h] - m[h]) for h in heads]
    den = [jnp.sum(e[h], axis=-1, keepdims=True) + es[h] for h in heads]
    return [e[h] / den[h] for h in heads], [es[h] / den[h] for h in heads]


def _att_kv(kp_ref, kc_ref, vp_ref, vc_ref):
    cat = lambda a, b, g: jnp.concatenate([a[:, HD * g:HD * (g + 1)], b[:, HD * g:HD * (g + 1)]], axis=0)
    return ([cat(kp_ref, kc_ref, g) for g in range(ATT_KV)], [cat(vp_ref, vc_ref, g) for g in range(ATT_KV)])


def attn_fwd(u_att, sinks, rel_bias, S, name):
    T = u_att.shape[0]
    nb = S // BQ
    table = jnp.asarray(_bucket_table())

    def body(sk_ref, rb_ref, bk_ref, q_ref, kp_ref, kc_ref, vp_ref, vc_ref, o_ref, bias_s):
        b = pl.program_id(0)
        n = pl.program_id(1)
        bk = bk_ref[...]

        @pl.when((b == 0) & (n == 0))
        def _():
            _att_bias(bk, rb_ref, bias_s)

        valid = (bk >= 0) & ((n > 0) | (_iota(bk.shape, 1) >= BQ))
        kgs, vgs = _att_kv(kp_ref, kc_ref, vp_ref, vc_ref)
        p, _ = _att_probs([q_ref[:, HD * h:HD * (h + 1)] for h in range(ATT_H)], kgs, bias_s, valid, sk_ref)
        outs = [_mm(p[h], vgs[h // ATT_G]) for h in range(ATT_H)]
        for h in range(ATT_H):
            o_ref[:, HD * h:HD * (h + 1)] = outs[h]

    smem = pl.BlockSpec(memory_space=pltpu.SMEM)
    return pl.pallas_call(
        body, name=name, grid=(T // S, nb),
        in_specs=[smem, smem, _full_spec((BQ, 2 * BQ))] + _att_specs(S),
        out_specs=pl.BlockSpec((BQ, ATT_W), lambda b, n: (b * nb + n, 0)),
        out_shape=jax.ShapeDtypeStruct((T, ATT_W), F32),
        scratch_shapes=[pltpu.VMEM((ATT_H, BQ, 2 * BQ), F32)],
        compiler_params=_cp("arbitrary", "arbitrary"),
    )(sinks, rel_bias, table, u_att, u_att, u_att, u_att, u_att)


def attn_bwd(u_att, dy, sinks, rel_bias, S, name):
    T = u_att.shape[0]
    nb = S // BQ
    nB = T // S
    table = jnp.asarray(_bucket_table())
    scale = HD ** -0.5

    def body(sk_ref, rb_ref, bk_ref, q_ref, kp_ref, kc_ref, vp_ref, vc_ref, dy_ref,
             du_ref, drel_ref, dsk_ref, bias_s, dbias_s):
        b = pl.program_id(0)
        n = pl.program_id(1)
        bk = bk_ref[...]

        @pl.when((b == 0) & (n == 0))
        def _():
            _att_bias(bk, rb_ref, bias_s)
            dbias_s[...] = jnp.zeros_like(dbias_s)
            dsk_ref[...] = jnp.zeros_like(dsk_ref)
            drel_ref[...] = jnp.zeros_like(drel_ref)

        @pl.when(n == 0)
        def _():
            du_ref[...] = jnp.zeros_like(du_ref)

        valid = (bk >= 0) & ((n > 0) | (_iota(bk.shape, 1) >= BQ))
        r_cur = pl.multiple_of(n * BQ, BQ)
        r_prev = pl.multiple_of(jnp.maximum(n - 1, 0) * BQ, BQ)
        heads = range(ATT_H)
        kgs, vgs = _att_kv(kp_ref, kc_ref, vp_ref, vc_ref)
        qs = [q_ref[:, HD * h:HD * (h + 1)] for h in heads]
        dos = [dy_ref[:, HD * h:HD * (h + 1)] for h in heads]
        p, ps = _att_probs(qs, kgs, bias_s, valid, sk_ref)
        dp = [_mm_nt(dos[h], vgs[h // ATT_G]) for h in heads]
        delta = [jnp.sum(p[h] * dp[h], axis=-1, keepdims=True) for h in heads]
        ds = [p[h] * (dp[h] - delta[h]) for h in heads]
        dss = [ds[h] * scale for h in heads]
        dq = [_mm(dss[h], kgs[h // ATT_G]) for h in heads]
        dks = [_mm_tn(dss[h], qs[h]) for h in heads]
        dvs = [_mm_tn(p[h], dos[h]) for h in heads]
        for h in heads:
            dbias_s[h] += ds[h]
            dsk_ref[h:h + 1, :] += jnp.broadcast_to(jnp.sum(-ps[h] * delta[h], axis=0, keepdims=True), (1, LANE))
            du_ref[pl.ds(r_cur, BQ), HD * h:HD * (h + 1)] = dq[h]
        for g in range(ATT_KV):
            of_group = range(g * ATT_G, (g + 1) * ATT_G)
            dk = functools.reduce(lambda x, y: x + y, [dks[h] for h in of_group])
            dv = functools.reduce(lambda x, y: x + y, [dvs[h] for h in of_group])
            ck = ATT_W + HD * g
            cv = ATT_W + KV_W + HD * g
            du_ref[pl.ds(r_prev, BQ), ck:ck + HD] += dk[0:BQ]
            du_ref[pl.ds(r_cur, BQ), ck:ck + HD] += dk[BQ:]
            du_ref[pl.ds(r_prev, BQ), cv:cv + HD] += dv[0:BQ]
            du_ref[pl.ds(r_cur, BQ), cv:cv + HD] += dv[BQ:]

        @pl.when((b == nB - 1) & (n == nb - 1))
        def _():
            lane = _iota((1, LANE), 1)
            for h in range(ATT_H):
                db = dbias_s[h]
                acc = jnp.zeros((1, LANE), F32)
                for bb in range(REL_BUCKETS):
                    val = jnp.sum(jnp.sum(jnp.where(bk == bb, db, 0.0), axis=1, keepdims=True),
                                  axis=0, keepdims=True)
                    acc = jnp.where(lane == bb, val, acc)
                drel_ref[h:h + 1, :] = acc

    smem = pl.BlockSpec(memory_space=pltpu.SMEM)
    return pl.pallas_call(
        body, name=name, grid=(nB, nb),
        in_specs=[smem, smem, _full_spec((BQ, 2 * BQ))] + _att_specs(S)
        + [pl.BlockSpec((BQ, ATT_W), lambda b, n: (b * nb + n, 0))],
        out_specs=[pl.BlockSpec((S, ATT_W + 2 * KV_W), lambda b, n: (b, 0)),
                   _full_spec((8, LANE)), _full_spec((8, LANE))],
        out_shape=[jax.ShapeDtypeStruct((T, ATT_W + 2 * KV_W), F32),
                   jax.ShapeDtypeStruct((8, LANE), F32), jax.ShapeDtypeStruct((8, LANE), F32)],
        scratch_shapes=[pltpu.VMEM((ATT_H, BQ, 2 * BQ), F32), pltpu.VMEM((ATT_H, BQ, 2 * BQ), F32)],
        compiler_params=_cp("arbitrary", "arbitrary"),
    )(sinks, rel_bias, table, u_att, u_att, u_att, u_att, u_att, dy)


def _head_of(i):
    return lax.shift_right_logical(i, 6)


def _head_mask(shape):
    return (_head_of(_iota(shape, 0)) == _head_of(_iota(shape, 1))).astype(F32)


def _dn_point(c, uba, alog, dtb):
    s = c * _sigmoid(c)
    qt, kt, vt = s[:, 0:256], s[:, 256:512], s[:, 512:768]
    ones_bd = _head_mask((DN_W, DN_W))
    q = qt * lax.rsqrt(_mmx(qt * qt, ones_bd) + EPS) * (HD ** -0.5)
    k = kt * lax.rsqrt(_mmx(kt * kt, ones_bd) + EPS)
    sel = _head_of(_iota((LANE, DN_W), 1))
    row = _iota((LANE, DN_W), 0)
    braw = _mmx(uba, (row == sel).astype(F32))
    araw = _mmx(uba, (row == sel + DN_H).astype(F32)) + dtb
    beta = _sigmoid(braw)
    g = -jnp.exp(alog) * (jnp.maximum(araw, 0.0) + jnp.log(1.0 + jnp.exp(-jnp.abs(araw))))
    return q, k, vt, g, beta


def dn_point_fwd(c, uba, alog, dtb, name):
    T = c.shape[0]
    tm = min(TOK_TILE, T)

    def body(c_ref, u_ref, al_ref, dt_ref, *outs):
        for ref, val in zip(outs, _dn_point(c_ref[...], u_ref[...], al_ref[...], dt_ref[...])):
            ref[...] = val

    return pl.pallas_call(
        body, name=name, grid=(T // tm,),
        in_specs=[_row_spec(tm, 768), _row_spec(tm, LANE), _full_spec((1, DN_W)), _full_spec((1, DN_W))],
        out_specs=[_row_spec(tm, DN_W)] * 5,
        out_shape=[jax.ShapeDtypeStruct((T, DN_W), F32)] * 5,
        compiler_params=_cp("parallel"),
    )(c, uba, alog, dtb)


def dn_point_bwd(c, uba, alog, dtb, douts, name):
    T = c.shape[0]
    tm = min(TOK_TILE, T)

    def body(c_ref, u_ref, al_ref, dt_ref, dq, dk, dv, dg, db, dc_ref, du_ref, dvec_ref):
        @pl.when(pl.program_id(0) == 0)
        def _():
            dvec_ref[...] = jnp.zeros_like(dvec_ref)

        _, vjp = jax.vjp(_dn_point, c_ref[...], u_ref[...], al_ref[...], dt_ref[...])
        dc, du, dal, ddt = vjp((dq[...], dk[...], dv[...], dg[...], db[...]))
        dc_ref[...] = dc
        du_ref[...] = du
        fold = (_iota((LANE, DN_W), 0) == _head_of(_iota((LANE, DN_W), 1))).astype(F32)
        both = jnp.concatenate([dal, ddt, jnp.zeros((6, DN_W), F32)], axis=0)
        dvec_ref[...] += _mmx_nt(both, fold)

    return pl.pallas_call(
        body, name=name, grid=(T // tm,),
        in_specs=[_row_spec(tm, 768), _row_spec(tm, LANE), _full_spec((1, DN_W)), _full_spec((1, DN_W))]
        + [_row_spec(tm, DN_W)] * 5,
        out_specs=[_row_spec(tm, 768), _row_spec(tm, LANE), _full_spec((8, LANE))],
        out_shape=[jax.ShapeDtypeStruct((T, 768), F32), jax.ShapeDtypeStruct((T, LANE), F32),
                   jax.ShapeDtypeStruct((8, LANE), F32)],
        compiler_params=_cp("arbitrary"),
    )(c, uba, alog, dtb, *douts)


def _unit_lower_inverses(lmats):
    eye = (_iota(lmats[0].shape, 0) == _iota(lmats[0].shape, 1)).astype(F32)
    tinvs = [eye - lm for lm in lmats]
    pws = list(lmats)
    for _ in range(5):
        pws = [_mm3(pw, pw) for pw in pws]
        tinvs = [t + _mm3(t, pw) for t, pw in zip(tinvs, pws)]
    return tuple(tinvs)


def _inverse_bwd(tinv, d):
    return -_mm3_nt(_mm3_tn(tinv, d), tinv)


@jax.custom_vjp
def _tri_invs(lmats):
    return _unit_lower_inverses(lmats)


def _tri_invs_fwd(lmats):
    tinvs = _unit_lower_inverses(lmats)
    return tinvs, tinvs


_tri_invs.defvjp(_tri_invs_fwd, lambda tinvs, ds: (tuple(_inverse_bwd(t, d) for t, d in zip(tinvs, ds)),))


@jax.custom_vjp
def _tri_inv_known(lmat, tinv):
    return tinv


_tri_inv_known.defvjp(lambda lmat, tinv: (tinv, tinv),
                      lambda tinv, d: (_inverse_bwd(tinv, d), jnp.zeros_like(tinv)))


DN_SUB = 4


def _dn_stack(x):
    return jnp.concatenate([x, x, x, x], axis=0) * _head_mask((DN_W, DN_W))


def _dn_pre_inverse(q, k, v, g, beta):
    hm = _head_mask((DN_W, DN_W))
    ri = _iota((DN_W, DN_W), 0) & (CHUNK - 1)
    ci = _iota((DN_W, DN_W), 1) & (CHUNK - 1)
    tri64 = (_iota((CHUNK, CHUNK), 0) >= _iota((CHUNK, CHUNK), 1)).astype(F32)
    gc = _mm3(tri64, g)
    ks = _dn_stack(k)
    gcol = jnp.sum(_dn_stack(gc), axis=1, keepdims=True) * (1.0 / HD)
    gmat = jnp.broadcast_to(gcol, (DN_W, DN_W))
    decay = jnp.exp(jnp.minimum(gmat - gmat.T, 0.0))
    lmat = _mm_nt(_dn_stack(k * beta), ks) * decay * (hm * (ri > ci).astype(F32))
    att = _mm_nt(_dn_stack(q), ks) * decay * (hm * (ri >= ci).astype(F32))
    return lmat, att, gc


def _dn_post_inverse(q, k, v, g, beta, tinv, att, gc):
    glast = jnp.sum(g, axis=0, keepdims=True)
    eg = jnp.exp(gc)
    u = _mm(tinv, _dn_stack(v * beta))
    w = _mm(tinv, _dn_stack(k * beta * eg))
    return u, w, att, _dn_stack(q * eg), _dn_stack(k * jnp.exp(glast - gc)), jnp.exp(glast), tinv


def _dn_apply(state, prep):
    u, w, att, qe, kd, eglast, _ = prep
    vn = u - _mm(w, state)
    o4 = _mm(qe, state) + _mm(att, vn)
    o = o4[0:64] + o4[64:128] + o4[128:192] + o4[192:256]
    return o, state * eglast + _mm_tn(kd, vn)


def _dn_chunks(state, q, k, v, g, beta, knowns=None):
    n = q.shape[0] // CHUNK
    chunks = [tuple(x[c * CHUNK:(c + 1) * CHUNK] for x in (q, k, v, g, beta)) for c in range(n)]
    pre = [_dn_pre_inverse(*ch) for ch in chunks]
    if knowns is None:
        tinvs = _tri_invs(tuple(p[0] for p in pre))
    else:
        tinvs = [_tri_inv_known(p[0], known) for p, known in zip(pre, knowns)]
    preps = [_dn_post_inverse(*ch, tinv, p[1], p[2]) for ch, tinv, p in zip(chunks, tinvs, pre)]
    outs = []
    for prep in preps:
        o, state = _dn_apply(state, prep)
        outs.append(o)
    return jnp.concatenate(outs, axis=0), state, [prep[-1] for prep in preps]


def dn_scan_fwd(q, k, v, g, beta, S, name):
    T = q.shape[0]
    rows = DN_SUB * CHUNK
    ns = S // rows

    def body(q_ref, k_ref, v_ref, g_ref, b_ref, o_ref, st_ref, ti_ref, s_s):
        @pl.when(pl.program_id(1) == 0)
        def _():
            s_s[...] = jnp.zeros_like(s_s)

        st = s_s[...]
        st_ref[0] = st
        o, new, tinvs = _dn_chunks(st, q_ref[...], k_ref[...], v_ref[...], g_ref[...], b_ref[...])
        o_ref[...] = o
        for c, tinv in enumerate(tinvs):
            ti_ref[c] = tinv
        s_s[...] = new

    spec = pl.BlockSpec((rows, DN_W), lambda b, t: (b * ns + t, 0))
    return pl.pallas_call(
        body, name=name, grid=(T // S, ns),
        in_specs=[spec] * 5,
        out_specs=[spec, pl.BlockSpec((1, DN_W, DN_W), lambda b, t: (b * ns + t, 0, 0)),
                   pl.BlockSpec((DN_SUB, DN_W, DN_W), lambda b, t: (b * ns + t, 0, 0))],
        out_shape=[jax.ShapeDtypeStruct((T, DN_W), F32),
                   jax.ShapeDtypeStruct((T // rows, DN_W, DN_W), F32),
                   jax.ShapeDtypeStruct((T // CHUNK, DN_W, DN_W), F32)],
        scratch_shapes=[pltpu.VMEM((DN_W, DN_W), F32)],
        compiler_params=_cp("parallel", "arbitrary"),
    )(q, k, v, g, beta)


def dn_scan_bwd(q, k, v, g, beta, states, tinvs, do, S, name):
    T = q.shape[0]
    rows = DN_SUB * CHUNK
    ns = S // rows

    def body(q_ref, k_ref, v_ref, g_ref, b_ref, st_ref, ti_ref, do_ref, dq, dk, dv, dg, db, ds_s):
        @pl.when(pl.program_id(1) == 0)
        def _():
            ds_s[...] = jnp.zeros_like(ds_s)

        knowns = [ti_ref[c] for c in range(DN_SUB)]
        _, vjp = jax.vjp(lambda *args: _dn_chunks(*args, knowns=knowns)[:2],
                         st_ref[0], q_ref[...], k_ref[...], v_ref[...], g_ref[...], b_ref[...])
        grads = vjp((do_ref[...], ds_s[...]))
        ds_s[...] = grads[0]
        for ref, val in zip((dq, dk, dv, dg, db), grads[1:]):
            ref[...] = val

    spec = pl.BlockSpec((rows, DN_W), lambda b, t: (b * ns + ns - 1 - t, 0))
    return pl.pallas_call(
        body, name=name, grid=(T // S, ns),
        in_specs=[spec] * 5 + [pl.BlockSpec((1, DN_W, DN_W), lambda b, t: (b * ns + ns - 1 - t, 0, 0)),
                               pl.BlockSpec((DN_SUB, DN_W, DN_W), lambda b, t: (b * ns + ns - 1 - t, 0, 0)),
                               spec],
        out_specs=[spec] * 5,
        out_shape=[jax.ShapeDtypeStruct((T, DN_W), F32)] * 5,
        scratch_shapes=[pltpu.VMEM((DN_W, DN_W), F32)],
        compiler_params=_cp("parallel", "arbitrary"),
    )(q, k, v, g, beta, states, tinvs, do)


def _dn_gate(o, z, nl):
    ms = _mmx(o * o, _head_mask((DN_W, DN_W))) * (1.0 / HD)
    return o * lax.rsqrt(ms + EPS) * nl * (z * _sigmoid(z))


def dn_gate_fwd(o, u_dn, nl, name):
    T = o.shape[0]
    tm = min(TOK_TILE, T)

    def body(o_ref, z_ref, n_ref, y_ref):
        y_ref[...] = _dn_gate(o_ref[...], z_ref[...], n_ref[...])

    return pl.pallas_call(
        body, name=name, grid=(T // tm,),
        in_specs=[_row_spec(tm, DN_W), pl.BlockSpec((tm, DN_W), lambda i: (i, 3)), _full_spec((1, DN_W))],
        out_specs=_row_spec(tm, DN_W),
        out_shape=jax.ShapeDtypeStruct((T, DN_W), F32),
        compiler_params=_cp("parallel"),
    )(o, u_dn, nl)


def dn_gate_bwd(o, u_dn, nl, dy, name):
    T = o.shape[0]
    tm = min(TOK_TILE, T)

    def body(o_ref, z_ref, n_ref, dy_ref, do_ref, dz_ref, dn_ref):
        @pl.when(pl.program_id(0) == 0)
        def _():
            dn_ref[...] = jnp.zeros_like(dn_ref)

        _, vjp = jax.vjp(_dn_gate, o_ref[...], z_ref[...], n_ref[...])
        do, dz, dn = vjp(dy_ref[...])
        do_ref[...] = do
        dz_ref[...] = dz
        fold = (_iota((LANE, DN_W), 0) == (_iota((LANE, DN_W), 1) & (HD - 1))).astype(F32)
        dn_ref[...] += _mmx_nt(jnp.concatenate([dn, jnp.zeros((7, DN_W), F32)], axis=0), fold)

    return pl.pallas_call(
        body, name=name, grid=(T // tm,),
        in_specs=[_row_spec(tm, DN_W), pl.BlockSpec((tm, DN_W), lambda i: (i, 3)), _full_spec((1, DN_W)),
                  _row_spec(tm, DN_W)],
        out_specs=[_row_spec(tm, DN_W), _row_spec(tm, DN_W), _full_spec((8, LANE))],
        out_shape=[jax.ShapeDtypeStruct((T, DN_W), F32), jax.ShapeDtypeStruct((T, DN_W), F32),
                   jax.ShapeDtypeStruct((8, LANE), F32)],
        compiler_params=_cp("arbitrary"),
    )(o, u_dn, nl, dy)


Y_SPLITS = (LRU_W, ATT_W, DN_W)
Y_OFFS = (0, LRU_W, LRU_W + ATT_W)


ROWS_DEV = D // N_DEV


def _dev_rows_spec(row0):
    return pl.BlockSpec((N_DEV, ROWS_DEV, D), lambda *_: (0, row0 // ROWS_DEV, 0))


def _dev_rows(w_ref, off, n):
    return w_ref[off // ROWS_DEV:(off + n) // ROWS_DEV].reshape(n, D)


def wout_fwd(h, ys, wb, name):
    T = h.shape[0]
    tm = min(TOK_TILE, T)

    def body(h_ref, y0, y1, y2, w_ref, o_ref, yc_ref):
        acc = h_ref[...]
        for ref, off, n in zip((y0, y1, y2), Y_OFFS, Y_SPLITS):
            y = ref[...].astype(BF16)
            yc_ref[:, off:off + n] = y
            acc += _mm(y, _dev_rows(w_ref, off, n))
        o_ref[...] = acc

    return pl.pallas_call(
        body, name=name, grid=(T // tm,),
        in_specs=[_row_spec(tm, D)] + [_row_spec(tm, n) for n in Y_SPLITS] + [_dev_rows_spec(R_WOUT)],
        out_specs=[_row_spec(tm, D), _row_spec(tm, D)],
        out_shape=[jax.ShapeDtypeStruct((T, D), F32), jax.ShapeDtypeStruct((T, D), BF16)],
        compiler_params=_cp("parallel"),
    )(h, *ys, wb)


def wout_bwd(dy, wb, name):
    T = dy.shape[0]
    tm = min(TOK_TILE, T)

    def body(dy_ref, w_ref, d0, d1, d2):
        dd = dy_ref[...].astype(BF16)
        for ref, off, n in zip((d0, d1, d2), Y_OFFS, Y_SPLITS):
            ref[...] = _mm_nt(dd, _dev_rows(w_ref, off, n))

    return pl.pallas_call(
        body, name=name, grid=(T // tm,),
        in_specs=[_row_spec(tm, D), _dev_rows_spec(R_WOUT)],
        out_specs=[_row_spec(tm, n) for n in Y_SPLITS],
        out_shape=[jax.ShapeDtypeStruct((T, n), F32) for n in Y_SPLITS],
        compiler_params=_cp("parallel"),
    )(dy, wb)


def ple_fwd(h, g, pe, wg, wp, name):
    T = h.shape[0]
    tm = min(TOK_TILE, T)

    def body(h_ref, g_ref, p_ref, wg_ref, wp_ref, o_ref):
        hh = h_ref[...]
        xn = _rms(hh, g_ref[...])[0]
        o_ref[...] = hh + _sigmoid(_mm(xn, _dev_rows(wg_ref, 0, D))) * _mm(p_ref[...], wp_ref[...])

    return pl.pallas_call(
        body, name=name, grid=(T // tm,),
        in_specs=[_row_spec(tm, D), _full_spec((1, D)), _row_spec(tm, PLE), _dev_rows_spec(R_PGATE),
                  _full_spec((PLE, D))],
        out_specs=_row_spec(tm, D),
        out_shape=jax.ShapeDtypeStruct((T, D), F32),
        compiler_params=_cp("parallel"),
    )(h, g, pe, wg, wp)


def ple_bwd(h, dy, g, pe, wg, wp, name):
    T = h.shape[0]
    tm = min(TOK_TILE, T)

    def body(h_ref, dy_ref, g_ref, p_ref, wg_ref, wp_ref, dh_ref, dz_ref, dpp_ref, xn_ref, dn_ref):
        @pl.when(pl.program_id(0) == 0)
        def _():
            dn_ref[...] = jnp.zeros_like(dn_ref)

        gg = g_ref[...]
        dy = dy_ref[...]
        xn, xhat, rstd = _rms(h_ref[...], gg)
        wg = _dev_rows(wg_ref, 0, D)
        gate = _sigmoid(_mm(xn, wg))
        pp = _mm(p_ref[...], wp_ref[...])
        dz = dy * pp * gate * (1.0 - gate)
        dz_ref[...] = dz.astype(BF16)
        dpp_ref[...] = (dy * gate).astype(BF16)
        xn_ref[...] = xn.astype(BF16)
        dh, dn = _rms_bwd(_mm_nt(dz, wg), xhat, rstd, gg)
        dh_ref[...] = dy + dh
        dn_ref[...] += dn

    return pl.pallas_call(
        body, name=name, grid=(T // tm,),
        in_specs=[_row_spec(tm, D), _row_spec(tm, D), _full_spec((1, D)), _row_spec(tm, PLE),
                  _dev_rows_spec(R_PGATE), _full_spec((PLE, D))],
        out_specs=[_row_spec(tm, D), _row_spec(tm, D), _row_spec(tm, D), _row_spec(tm, D), _full_spec((1, D))],
        out_shape=[jax.ShapeDtypeStruct((T, D), F32), jax.ShapeDtypeStruct((T, D), BF16),
                   jax.ShapeDtypeStruct((T, D), BF16), jax.ShapeDtypeStruct((T, D), BF16),
                   jax.ShapeDtypeStruct((1, D), F32)],
        compiler_params=_cp("arbitrary"),
    )(h, dy, g, pe, wg, wp)


def loss_head(h, g, target, name):
    T = h.shape[0]
    tm = min(TOK_TILE, T)

    def body(h_ref, g_ref, t_ref, loss_ref, dh_ref, dn_ref):
        @pl.when(pl.program_id(0) == 0)
        def _():
            dn_ref[...] = jnp.zeros_like(dn_ref)
            loss_ref[...] = jnp.zeros_like(loss_ref)

        gg = g_ref[...]
        y, xhat, rstd = _rms(h_ref[...], gg)
        err = y - t_ref[...]
        per_tok = jnp.mean(err * err, axis=-1, keepdims=True)
        loss_ref[...] += 0.5 * jnp.sum(per_tok, axis=0, keepdims=True)
        dh, dn = _rms_bwd(err * (1.0 / D), xhat, rstd, gg)
        dh_ref[...] = dh
        dn_ref[...] += dn

    return pl.pallas_call(
        body, name=name, grid=(T // tm,),
        in_specs=[_row_spec(tm, D), _full_spec((1, D)), _row_spec(tm, D)],
        out_specs=[_full_spec((8, LANE)), _row_spec(tm, D), _full_spec((1, D))],
        out_shape=[jax.ShapeDtypeStruct((8, LANE), F32), jax.ShapeDtypeStruct((T, D), F32),
                   jax.ShapeDtypeStruct((1, D), F32)],
        compiler_params=_cp("arbitrary"),
    )(h, g, target)


def _block_diag(w):
    return jnp.einsum('hij,hk->hikj', w, jnp.eye(4, dtype=w.dtype)).reshape(LRU_W, LRU_W)


def _layer_consts(W, l):
    row = lambda v: v.reshape(1, -1)
    zeros = jnp.zeros((5, LRU_W), F32)
    return dict(
        wa=_block_diag(W["lru_w_a"][l]), wx=_block_diag(W["lru_w_x"][l]),
        lru_vec=jnp.concatenate([row(W["lru_b_a"][l]), row(W["lru_b_x"][l]), row(W["lru_lambda"][l]), zeros], 0),
        lru_cb=row(W["lru_conv_b"][l]),
        sinks=W["attn_sinks"][l], rel=W["rel_bias"].reshape(-1),
        dn_cb=jnp.zeros((1, 3 * DN_W), F32),
        alog=row(jnp.repeat(W["dn_a_log"][l], HD)), dtb=row(jnp.repeat(W["dn_dt_bias"][l], HD)),
        dn_nl=row(jnp.tile(W["dn_norm"][l], DN_H)),
    )


def _layer_fwd(h0, pe, W, l, S, need=None):
    n = f"l{l}_"
    c_ = _layer_consts(W, l)
    row = lambda v: v.reshape(1, -1)
    need = need or (lambda *_: None)
    need(l, "f1", h0)
    h1, *ffn1_kept = ffn_fwd(h0, row(W["ffn1_norm"][l]), W["f1_cols"][l], W["f1_rows"][l], n + "ffn1_fwd")
    need(l, "in", h1)
    u_lru, u_att, u_dn, u_ba, xn_mix = mixin_fwd(h1, row(W["mix_norm"][l]), W["w_in"][l], n + "mixin_fwd")
    xr = conv_fwd(u_lru, W["lru_conv_w"][l], c_["lru_cb"], S, 0, LRU_W, n + "lru_conv_fwd")
    y_lru = lru_fwd(xr, u_lru, c_["wa"], c_["wx"], c_["lru_vec"], S, n + "lru_fwd")
    y_att = attn_fwd(u_att, c_["sinks"], c_["rel"], S, n + "attn_fwd")
    cc = conv_fwd(u_dn, W["dn_conv_w"][l], c_["dn_cb"], S, 0, 3 * DN_W, n + "dn_conv_fwd")
    q, k, v, g, beta = dn_point_fwd(cc, u_ba, c_["alog"], c_["dtb"], n + "dn_point_fwd")
    o, states, tinvs = dn_scan_fwd(q, k, v, g, beta, S, n + "dn_scan_fwd")
    y_dn = dn_gate_fwd(o, u_dn, c_["dn_nl"], n + "dn_gate_fwd")
    need(l, "rest", y_dn)
    h2, ycat = wout_fwd(h1, (y_lru, y_att, y_dn), W["r_rows"][l], n + "wout_fwd")
    h3, *ffn2_kept = ffn_fwd(h2, row(W["ffn2_norm"][l]), W["r_cols"][l], W["r_rows"][l], n + "ffn2_fwd")
    h4 = ple_fwd(h3, row(W["ple_norm"][l]), pe, W["r_rows"][l], W["ple_w_proj"][l], n + "ple_fwd")
    saved = dict(ffn1=ffn1_kept, ffn2=ffn2_kept, h0=h0, h1=h1, h2=h2, h3=h3, u_lru=u_lru, u_att=u_att, u_dn=u_dn,
                 u_ba=u_ba, xn_mix=xn_mix, xr=xr, cc=cc, q=q, k=k, v=v, g=g, beta=beta, o=o, states=states, tinvs=tinvs, ycat=ycat)
    return h4, saved


GM_WOUT, GM_PGATE, GM_WIN, GM_PPROJ, GM_END, GM_ROWS = 0, 128, 256, 560, 592, 640


def _layer_bwd(dh4, sv, pe, W, l, S, token=None, on_piece=None):
    n = f"l{l}_"
    c_ = _layer_consts(W, l)
    row = lambda v: v.reshape(1, -1)
    behind = lambda v, tok: v if tok is None else v + tok.astype(v.dtype)
    on_piece = on_piece or (lambda *_: None)
    G = {"mix_rows": jnp.zeros((N_DEV, GM_ROWS, D), BF16)}
    dh3, dz, dpp, xn_p, dn = ple_bwd(sv["h3"], dh4, behind(row(W["ple_norm"][l]), token), pe, W["r_rows"][l],
                                     W["ple_w_proj"][l], n + "ple_bwd")
    G["ple_norm"] = dn[0]
    G["mix_rows"] = grad_rows(xn_p, dz, G["mix_rows"], GM_PGATE, ROWS_DEV, n + "d_ple_w_gate")
    d_proj = matmul_tn(pe, dpp, n + "d_ple_w_proj")
    d_proj = d_proj.reshape(PLE, N_DEV, D // N_DEV).transpose(1, 0, 2).reshape(N_DEV, GM_END - GM_PPROJ, D)
    G["mix_rows"] = lax.dynamic_update_slice(G["mix_rows"], d_proj, (0, GM_PPROJ, 0))

    def ffn_back(which, cols_w, rows_w, h_in, dy, tok, one_by_one):
        gt, up, xn = sv[which]
        dh, dgt, dup, act, dn_ = ffn_bwd(h_in, dy, behind(row(W[which + "_norm"][l]), tok), gt, up, cols_w, rows_w,
                                         n + which + "_bwd")
        G[which + "_norm"] = dn_[0]
        zeros_rows = jnp.zeros((N_DEV, SHP, D), BF16)
        if one_by_one:
            G[which + "_gate"] = grad_cols(xn, dgt, lax.empty((1, D, FFP), BF16), 0, n + "d_" + which + "_w_gate")
            tok = on_piece(l, which + "_gate", (G[which + "_gate"],))
            G[which + "_up"] = grad_cols(xn, dup, behind(jnp.zeros((1, D, FFP), BF16), tok), 0,
                                         n + "d_" + which + "_w_up")
            tok = on_piece(l, which + "_up", (G[which + "_up"],))
            G[which + "_down"] = grad_rows(act, dy, behind(zeros_rows, tok), 0, SHP, n + "d_" + which + "_w_down",
                                           scale=0.5)
            return dh, on_piece(l, which + "_down", (G[which + "_down"],))
        cols = grad_cols(xn, dgt, lax.empty((2, D, FFP), BF16), 0, n + "d_" + which + "_w_gate")
        G[which + "_cols"] = grad_cols(xn, dup, cols, 1, n + "d_" + which + "_w_up")
        G[which + "_rows"] = grad_rows(act, dy, lax.empty((N_DEV, SHP, D), BF16), 0, SHP,
                                       n + "d_" + which + "_w_down", scale=0.5)
        return dh, on_piece(l, which, (G[which + "_cols"], G[which + "_rows"]))

    dh2, tok = ffn_back("ffn2", W["r_cols"][l], W["r_rows"][l], sv["h2"], dh3, None, False)
    dy_lru, dy_att, dy_dn = wout_bwd(dh2, W["r_rows"][l], n + "wout_bwd")
    G["mix_rows"] = grad_rows(sv["ycat"], dh2, G["mix_rows"], GM_WOUT, ROWS_DEV, n + "d_w_out")
    do, dz_dn, dnn = dn_gate_bwd(sv["o"], sv["u_dn"], behind(c_["dn_nl"], tok), dy_dn, n + "dn_gate_bwd")
    dqkvgb = dn_scan_bwd(sv["q"], sv["k"], sv["v"], sv["g"], sv["beta"], sv["states"], sv["tinvs"], do, S,
                         n + "dn_scan_bwd")
    dcc, du_ba, dvec_dn = dn_point_bwd(sv["cc"], sv["u_ba"], c_["alog"], c_["dtb"], dqkvgb, n + "dn_point_bwd")
    dqkv, dwb_dn = conv_bwd(sv["u_dn"], dcc, W["dn_conv_w"][l], S, 0, 3 * DN_W, n + "dn_conv_bwd")
    du_dn = jnp.concatenate([dqkv, dz_dn], axis=1)
    G["dn_norm"] = dnn[0, 0:HD]
    G["dn_a_log"] = dvec_dn[0, 0:DN_H]
    G["dn_dt_bias"] = dvec_dn[1, 0:DN_H]
    G["dn_conv_w"] = dwb_dn[0:4]
    du_att, drel, dsk = attn_bwd(sv["u_att"], dy_att, c_["sinks"], c_["rel"], S, n + "attn_bwd")
    G["attn_sinks"] = dsk[:, 0]
    G["rel_bias"] = drel[:, 0:REL_BUCKETS].T
    dxr, dgt_lru, dwa, dwx, dvec = lru_bwd(sv["xr"], sv["u_lru"], dy_lru, c_["wa"], c_["wx"], c_["lru_vec"], S,
                                           n + "lru_bwd")
    dx_lru, dwb_lru = conv_bwd(sv["u_lru"], dxr, W["lru_conv_w"][l], S, 0, LRU_W, n + "lru_conv_bwd")
    du_lru = jnp.concatenate([dx_lru, dgt_lru], axis=1)
    diag = lambda m: jnp.stack([m[c, HD * e:HD * (e + 1), HD * e:HD * (e + 1)] for c in range(2) for e in range(2)])
    G["lru_w_a"], G["lru_w_x"] = diag(dwa), diag(dwx)
    G["lru_b_a"], G["lru_b_x"], G["lru_lambda"] = dvec[0], dvec[1], dvec[2]
    G["lru_conv_w"], G["lru_conv_b"] = dwb_lru[0:4], dwb_lru[4]
    dh1, du_cat, dn = mixin_bwd(sv["h1"], dh2, row(W["mix_norm"][l]), W["w_in"][l],
                                (du_lru, du_att, du_dn, du_ba), n + "mixin_bwd")
    G["mix_norm"] = dn[0]
    d_in = matmul_tn(sv["xn_mix"], du_cat, n + "d_w_in")[:, :D_IN]
    d_in = d_in.reshape(D, N_DEV, D_IN // N_DEV).transpose(1, 0, 2).reshape(N_DEV, WIN_ROWS, D)
    d_in = jnp.pad(d_in, ((0, 0), (0, GM_PPROJ - GM_WIN - WIN_ROWS), (0, 0)))
    G["mix_rows"] = lax.dynamic_update_slice(G["mix_rows"], d_in, (0, GM_WIN, 0))
    tok = on_piece(l, "mix", (G["mix_rows"],))
    dh0, tok = ffn_back("ffn1", W["f1_cols"][l], W["f1_rows"][l], sv["h0"], dh1, tok, l == 0)
    return dh0, G, tok


def _core(x, pe, W, target, S, need=None, on_piece=None):
    h = x
    saved = []
    for l in range(DEPTH):
        h, sv = _layer_fwd(h, pe[l], W, l, S, need)
        saved.append(sv)
    loss_tile, dh, dfn = loss_head(h, W["final_norm"].reshape(1, -1), target, "loss_head")
    grads = [None] * DEPTH
    token = None
    for l in reversed(range(DEPTH)):
        dh, grads[l], token = _layer_bwd(dh, saved[l], pe[l], W, l, S, token, on_piece)
    return loss_tile[0, 0], dh, grads, dfn[0]


MESH_ID = pl.DeviceIdType.MESH
ANY_SPEC = pl.BlockSpec(memory_space=pl.ANY)
AXES = ("x", "y", "c")


def _my_pos():
    return lax.axis_index("x"), lax.axis_index("y"), lax.axis_index("c")


def _slot_of(px, py, pc):
    return 4 * px + 2 * py + pc


def all_gather(x, name):
    R, C = x.shape

    def body(x_ref, out_ref, send_sems, recv_sems, local_sem):
        mx, my, mc = _my_pos()
        me, sibling = (mx, my, mc), (mx, my, 1 - mc)
        chips = [(1 - mx, my), (mx, 1 - my), (1 - mx, 1 - my)]

        def copy(k, block, to, src=None):
            dst = out_ref.at[_slot_of(*block)]
            return pltpu.make_async_remote_copy(
                src_ref=dst if src is None else src, dst_ref=dst,
                send_sem=send_sems.at[k], recv_sem=recv_sems.at[k],
                device_id=to, device_id_type=MESH_ID)

        mine = pltpu.make_async_copy(x_ref, out_ref.at[_slot_of(*me)], local_sem)
        mine.start()
        first = [copy(0, me, sibling, src=x_ref)]
        first += [copy(1 + j, me, (*chip, mc), src=x_ref) for j, chip in enumerate(chips)]
        for cp in first:
            cp.start()
        passed = [copy(4 + j, (*chip, mc), sibling) for j, chip in enumerate(chips)]
        for j, chip in enumerate(chips):
            copy(1 + j, (*chip, mc), me).wait_recv()
            passed[j].start()
        copy(0, sibling, me).wait_recv()
        for j, chip in enumerate(chips):
            copy(4 + j, (*chip, 1 - mc), me).wait_recv()
        for cp in first + passed:
            cp.wait_send()
        mine.wait()

    return pl.pallas_call(
        body, name=name,
        out_shape=jax.ShapeDtypeStruct((N_DEV, R, C), x.dtype),
        in_specs=[ANY_SPEC], out_specs=ANY_SPEC,
        scratch_shapes=[pltpu.SemaphoreType.DMA((7,)), pltpu.SemaphoreType.DMA((7,)), pltpu.SemaphoreType.DMA],
    )(x)


def _col_window(ref, slot):
    return ref.at[:, pl.ds(pl.multiple_of(slot * SHP, LANE), SHP)]


def gather_layer(a_sh, b_sh, name):
    def body(a_ref, b_ref, ao_ref, bo_ref, send_sems, recv_sems, local_sems):
        mx, my, mc = _my_pos()
        me, sibling = (mx, my, mc), (mx, my, 1 - mc)
        chips = [(1 - mx, my), (mx, 1 - my), (1 - mx, 1 - my)]

        def copies(k, block, to, own=False):
            slot = _slot_of(*block)
            dsts = (_col_window(ao_ref, slot), bo_ref.at[slot])
            srcs = (a_ref, b_ref) if own else dsts
            return [pltpu.make_async_remote_copy(
                src_ref=s, dst_ref=d, send_sem=send_sems.at[2 * k + i], recv_sem=recv_sems.at[2 * k + i],
                device_id=to, device_id_type=MESH_ID) for i, (s, d) in enumerate(zip(srcs, dsts))]

        mine = [pltpu.make_async_copy(a_ref, _col_window(ao_ref, _slot_of(*me)), local_sems.at[0]),
                pltpu.make_async_copy(b_ref, bo_ref.at[_slot_of(*me)], local_sems.at[1])]
        for cp in mine:
            cp.start()
        first = copies(0, me, sibling, own=True)
        for j, chip in enumerate(chips):
            first += copies(1 + j, me, (*chip, mc), own=True)
        for cp in first:
            cp.start()
        passed = []
        for j, chip in enumerate(chips):
            for cp in copies(1 + j, (*chip, mc), me):
                cp.wait_recv()
            fwd = copies(4 + j, (*chip, mc), sibling)
            for cp in fwd:
                cp.start()
            passed += fwd
        for cp in copies(0, sibling, me):
            cp.wait_recv()
        for j, chip in enumerate(chips):
            for cp in copies(4 + j, (*chip, 1 - mc), me):
                cp.wait_recv()
        for cp in first + passed:
            cp.wait_send()
        for cp in mine:
            cp.wait()

    return pl.pallas_call(
        body, name=name,
        out_shape=[jax.ShapeDtypeStruct((a_sh.shape[0], FFP), a_sh.dtype),
                   jax.ShapeDtypeStruct((N_DEV,) + b_sh.shape, b_sh.dtype)],
        in_specs=[ANY_SPEC, ANY_SPEC], out_specs=[ANY_SPEC, ANY_SPEC],
        scratch_shapes=[pltpu.SemaphoreType.DMA((14,)), pltpu.SemaphoreType.DMA((14,)),
                        pltpu.SemaphoreType.DMA((2,))],
    )(a_sh, b_sh)


HBM_SPEC = pl.BlockSpec(memory_space=pltpu.HBM)
SEM_SPEC = pl.BlockSpec(memory_space=pltpu.SEMAPHORE)
SPLIT_EFFECT = pltpu.CompilerParams(has_side_effects=pltpu.SideEffectType.DATAFLOW_SIDE_EFFECTING)


def _split_ends(mode, src_ref, dst_ref, src_slot, dst_slot):
    cols = mode.endswith("cols")
    if mode.startswith("gather"):
        return src_ref, (_col_window(dst_ref, dst_slot) if cols else dst_ref.at[dst_slot])
    return (_col_window(src_ref, src_slot) if cols else src_ref.at[src_slot]), dst_ref.at[dst_slot]


def _split_peers():
    mx, my, mc = _my_pos()
    for r in range(1, N_DEV):
        peer = (1 - mx if r & 4 else mx, 1 - my if r & 2 else my, 1 - mc if r & 1 else mc)
        yield r - 1, peer, _slot_of(*peer)


def split_start(modes, srcs, dsts, name, after=()):
    n = len(modes)
    m = len(after)

    def body(*refs):
        send_sems, recv_sems, token = refs[2 * n + m], refs[2 * n + m + 1], refs[-1]
        mine = _slot_of(*_my_pos())
        for k, peer, ps in _split_peers():
            for i in range(n):
                src, dst = _split_ends(modes[i], refs[i], refs[n + i], ps, mine)
                pltpu.make_async_remote_copy(
                    src_ref=src, dst_ref=dst, send_sem=send_sems.at[n * k + i], recv_sem=recv_sems.at[n * k + i],
                    device_id=peer, device_id_type=MESH_ID).start()
        for i in range(n):
            src, dst = _split_ends(modes[i], refs[i], refs[n + i], mine, mine)
            pltpu.make_async_copy(src, dst, recv_sems.at[n * (N_DEV - 1) + i]).start()
        token[...] = jnp.zeros_like(token)

    bufs = tuple(srcs) + tuple(dsts)
    sems = pltpu.SemaphoreType.DMA((n * N_DEV,))
    res = pl.pallas_call(
        body, name=name,
        out_shape=(sems, sems) + tuple(pltpu.HBM(t.shape, t.dtype) for t in bufs)
        + (jax.ShapeDtypeStruct((8, LANE), F32),),
        in_specs=[HBM_SPEC] * (2 * n) + [ANY_SPEC] * m,
        out_specs=(SEM_SPEC, SEM_SPEC) + (HBM_SPEC,) * (2 * n) + (pl.BlockSpec(memory_space=pltpu.VMEM),),
        input_output_aliases={i: 2 + i for i in range(2 * n)},
        compiler_params=SPLIT_EFFECT,
    )(*(pltpu.with_memory_space_constraint(t, pltpu.HBM) for t in bufs), *after)
    return list(res[:-1]), res[-1]


def split_wait(modes, started, after, name):
    n = len(modes)
    after = tuple(after) if isinstance(after, (tuple, list)) else (after,)
    send_sems, recv_sems, bufs = started[0], started[1], started[2:]

    def body(*refs):
        send_sems, recv_sems = refs[2 * n], refs[2 * n + 1]
        mine = _slot_of(*_my_pos())
        for k, peer, ps in _split_peers():
            for i in range(n):
                sent = _split_ends(modes[i], refs[i], refs[n + i], ps, mine)[0]
                landed = _split_ends(modes[i], refs[i], refs[n + i], mine, ps)[1]
                cp = pltpu.make_async_remote_copy(
                    src_ref=sent, dst_ref=landed, send_sem=send_sems.at[n * k + i],
                    recv_sem=recv_sems.at[n * k + i], device_id=peer, device_id_type=MESH_ID)
                cp.wait_send()
                cp.wait_recv()
        for i in range(n):
            src, dst = _split_ends(modes[i], refs[i], refs[n + i], mine, mine)
            pltpu.make_async_copy(src, dst, recv_sems.at[n * (N_DEV - 1) + i]).wait()

    res = pl.pallas_call(
        body, name=name,
        out_shape=tuple(pltpu.HBM(t.shape, t.dtype) for t in bufs),
        in_specs=[HBM_SPEC] * (2 * n) + [SEM_SPEC, SEM_SPEC] + [ANY_SPEC] * len(after),
        out_specs=(HBM_SPEC,) * (2 * n),
        input_output_aliases={i: i for i in range(2 * n)},
        compiler_params=SPLIT_EFFECT,
    )(*bufs, send_sems, recv_sems, *after)
    return list(res[n:])


def sum_parts(parts, name):
    _, R, C = parts.shape
    tr = _pick(R, (512, 336, 272, 256, 128, 64, 32, 16, 8))

    def body(p_ref, o_ref):
        acc = p_ref[0].astype(F32)
        for k in range(1, N_DEV):
            acc += p_ref[k].astype(F32)
        o_ref[...] = acc

    return pl.pallas_call(
        body, name=name, grid=(R // tr,),
        in_specs=[pl.BlockSpec((N_DEV, tr, C), lambda i: (0, i, 0))],
        out_specs=pl.BlockSpec((tr, C), lambda i: (i, 0)),
        out_shape=jax.ShapeDtypeStruct((R, C), F32),
        compiler_params=_cp("parallel"),
    )(parts)


def adamw(g, w, m, v, name):
    lead, (R, C) = g.shape[:-2], g.shape[-2:]
    tr = _pick(R, (512, 352, 256, 128, 64, 32, 16, 8))
    c1 = 1.0 - ADAM_B1 ** ADAM_STEP
    c2 = 1.0 - ADAM_B2 ** ADAM_STEP

    def body(g_ref, w_ref, m_ref, v_ref, d_ref, nm_ref, nv_ref):
        gg = g_ref[...]
        mm = ADAM_B1 * m_ref[...] + (1.0 - ADAM_B1) * gg
        vv = ADAM_B2 * v_ref[...] + (1.0 - ADAM_B2) * (gg * gg)
        nm_ref[...] = mm
        nv_ref[...] = vv
        d_ref[...] = -ADAM_LR * ((mm / c1) / (jnp.sqrt(vv / c2) + ADAM_EPS) + ADAM_WD * w_ref[...])

    if lead:
        spec = pl.BlockSpec((1, tr, C), lambda l, i: (l, i, 0))
    else:
        spec = pl.BlockSpec((tr, C), lambda l, i: (i, 0))
    return pl.pallas_call(
        body, name=name, grid=(lead[0] if lead else 1, R // tr),
        in_specs=[spec] * 4, out_specs=[spec] * 3,
        out_shape=[jax.ShapeDtypeStruct(g.shape, F32)] * 3,
        compiler_params=_cp("parallel", "parallel"),
    )(g, w, m, v)


BIG = (("ffn1_w_gate", 1, D, FF), ("ffn1_w_up", 1, D, FF), ("ffn1_w_down", 0, FF, D),
       ("w_in", 1, D, D_IN), ("w_out", 0, D, D),
       ("ffn2_w_gate", 1, D, FF), ("ffn2_w_up", 1, D, FF), ("ffn2_w_down", 0, FF, D),
       ("ple_w_gate", 0, D, D), ("ple_w_proj", 1, PLE, D))
SMALL = (("ffn1_norm", (D,), None), ("mix_norm", (D,), None), ("lru_conv_w", (4, LRU_W), LRU_W // N_DEV),
         ("lru_conv_b", (LRU_W,), None), ("lru_w_a", (4, HD, HD), None), ("lru_b_a", (LRU_W,), None),
         ("lru_w_x", (4, HD, HD), None), ("lru_b_x", (LRU_W,), None), ("lru_lambda", (LRU_W,), None),
         ("attn_sinks", (ATT_H,), None), ("dn_conv_w", (4, 3 * DN_W), 3 * DN_W // N_DEV),
         ("dn_a_log", (DN_H,), None), ("dn_dt_bias", (DN_H,), None), ("dn_norm", (HD,), None),
         ("ffn2_norm", (D,), None), ("ple_norm", (D,), None))
SINGLE = (("rel_bias", (REL_BUCKETS, ATT_H)), ("final_norm", (D,)))


def _pack_rows(arrs, width, mult):
    flat = jnp.concatenate([a.reshape(-1) for a in arrs])
    rows = -(-flat.shape[0] // (width * mult)) * mult
    return jnp.pad(flat, (0, rows * width - flat.shape[0])).reshape(rows, width)


def _unpack_rows(packed, shapes):
    flat = packed.reshape(-1)
    out, off = [], 0
    for s in shapes:
        n = int(np.prod(s))
        out.append(flat[off:off + n].reshape(s))
        off += n
    return out


def _pad_rows(w, r):
    return jnp.pad(w, ((0, r - w.shape[0]), (0, 0)))


def _shard_ffn(a, l, which, more=()):
    cols = jnp.concatenate([a[which + "_w_gate"][l], a[which + "_w_up"][l]], axis=0)
    rows = jnp.concatenate([_pad_rows(a[which + "_w_down"][l], SHP)] + list(more), axis=0)
    return jnp.pad(cols, ((0, 0), (0, SHP - SH))).astype(BF16), rows.astype(BF16)


def _shards(a, l):
    w_in_rows = _pad_rows(a["w_in"][l].reshape(WIN_ROWS, D), IN_ROWS).astype(BF16)
    rest = _shard_ffn(a, l, "ffn2", (a["w_out"][l], a["ple_w_gate"][l], a["ple_w_proj"][l].reshape(-1, D)))
    return _shard_ffn(a, l, "ffn1"), (w_in_rows,), rest


def _full_w_in(in_rows):
    sh = in_rows[:, :WIN_ROWS, :].reshape(N_DEV, D, D_IN // N_DEV)
    return jnp.pad(sh.transpose(1, 0, 2).reshape(D, D_IN), ((0, 0), (0, D_IN_PAD - D_IN)))


def _full_ple_proj(r_rows):
    sh = r_rows[:, R_PPROJ:R_ROWS, :].reshape(N_DEV, PLE, D // N_DEV)
    return sh.transpose(1, 0, 2).reshape(PLE, D)


PIECE_NAMES = {"ffn1": ("ffn1_w_gate", "ffn1_w_up", "ffn1_w_down"), "ffn2": ("ffn2_w_gate", "ffn2_w_up", "ffn2_w_down"),
               "mix": ("w_out", "ple_w_gate", "w_in", "ple_w_proj")}


def _shard_grads(piece, summed):
    if piece == "mix":
        rows, = summed
        return {"w_out": rows[GM_WOUT:GM_WOUT + ROWS_DEV], "ple_w_gate": rows[GM_PGATE:GM_PGATE + ROWS_DEV],
                "w_in": rows[GM_WIN:GM_WIN + WIN_ROWS].reshape(D, D_IN // N_DEV),
                "ple_w_proj": rows[GM_PPROJ:GM_END].reshape(PLE, D // N_DEV)}
    if piece in ("ffn1_gate", "ffn1_up"):
        return {piece.replace("_", "_w_"): summed[0][:, :SH]}
    if piece == "ffn1_down":
        return {"ffn1_w_down": summed[0][:SH]}
    cols, rows = summed
    return {piece + "_w_gate": cols[:D, :SH], piece + "_w_up": cols[D:, :SH], piece + "_w_down": rows[:SH]}


def kernel(x, p, ffn1_norm, ffn1_w_gate, ffn1_w_up, ffn1_w_down, mix_norm, w_in, lru_conv_w, lru_conv_b, lru_w_a, lru_b_a, lru_w_x, lru_b_x, lru_lambda, attn_sinks, rel_bias, dn_conv_w, dn_a_log, dn_dt_bias, dn_norm, w_out, ffn2_norm, ffn2_w_gate, ffn2_w_up, ffn2_w_down, ple_norm, ple_w_gate, ple_w_proj, final_norm, loss_target, m_ffn1_norm, m_ffn1_w_gate, m_ffn1_w_up, m_ffn1_w_down, m_mix_norm, m_w_in, m_lru_conv_w, m_lru_conv_b, m_lru_w_a, m_lru_b_a, m_lru_w_x, m_lru_b_x, m_lru_lambda, m_attn_sinks, m_rel_bias, m_dn_conv_w, m_dn_a_log, m_dn_dt_bias, m_dn_norm, m_w_out, m_ffn2_norm, m_ffn2_w_gate, m_ffn2_w_up, m_ffn2_w_down, m_ple_norm, m_ple_w_gate, m_ple_w_proj, m_final_norm, v_ffn1_norm, v_ffn1_w_gate, v_ffn1_w_up, v_ffn1_w_down, v_mix_norm, v_w_in, v_lru_conv_w, v_lru_conv_b, v_lru_w_a, v_lru_b_a, v_lru_w_x, v_lru_b_x, v_lru_lambda, v_attn_sinks, v_rel_bias, v_dn_conv_w, v_dn_a_log, v_dn_dt_bias, v_dn_norm, v_w_out, v_ffn2_norm, v_ffn2_w_gate, v_ffn2_w_up, v_ffn2_w_down, v_ple_norm, v_ple_w_gate, v_ple_w_proj, v_final_norm):
    a = dict(locals())
    nb, S, _ = x.shape
    T = nb * S
    my_slot = _slot_of(*_my_pos())

    W = {k: [None] * DEPTH for k in ("f1_cols", "f1_rows", "w_in", "r_cols", "r_rows", "ple_w_proj")}
    GATHER, SCATTER = ("gather_cols", "gather_block"), ("scatter_cols", "scatter_block")
    GROUP_MODES = {"f1": GATHER, "in": GATHER[1:], "rest": GATHER}

    def set_group(l, group, bufs):
        if group == "f1":
            W["f1_cols"][l], W["f1_rows"][l] = bufs
        elif group == "in":
            W["w_in"][l] = _full_w_in(bufs[0])
        else:
            W["r_cols"][l], W["r_rows"][l] = bufs
            W["ple_w_proj"][l] = _full_ple_proj(bufs[1])

    def landing(mode, src):
        if mode == "gather_cols":
            return lax.empty((src.shape[0], FFP), src.dtype)
        if mode == "scatter_cols":
            return lax.empty((N_DEV, src.shape[0], SHP), src.dtype)
        return lax.empty((N_DEV,) + src.shape[mode == "scatter_block":], src.dtype)

    def start(modes, srcs, name, after=()):
        return split_start(modes, srcs, [landing(m, s) for m, s in zip(modes, srcs)], name, after)

    shards0, shards1 = _shards(a, 0), _shards(a, 1)
    set_group(0, "f1", gather_layer(*shards0[0], "gather_weights_l0_ffn1"))
    taps = all_gather(_pack_rows([lru_conv_w, dn_conv_w], LANE, 8), "gather_conv_taps")
    flat_taps = taps.reshape(N_DEV, -1)
    for name, first, tap in (("lru_conv_w", 0, lru_conv_w), ("dn_conv_w", lru_conv_w.size, dn_conv_w)):
        per_dev = flat_taps[:, first:first + tap.size].reshape((N_DEV,) + tap.shape)
        W[name] = jnp.moveaxis(per_dev, 0, -2).reshape(tap.shape[:-1] + (N_DEV * tap.shape[-1],))
    for name, _, cols in SMALL:
        if cols is None:
            W[name] = a[name]
    W["rel_bias"], W["final_norm"] = rel_bias, final_norm

    gathers, after = {}, (W["f1_rows"][0], taps)
    for l, group, srcs in ((0, "in", shards0[1]), (0, "rest", shards0[2]),
                           (1, "f1", shards1[0]), (1, "in", shards1[1]), (1, "rest", shards1[2])):
        gathers[l, group], token = start(GROUP_MODES[group], srcs, f"gather_start_l{l}_{group}", after)
        after = (token,)
    W["ffn1_norm"] = ffn1_norm + token[0, 0]
    flight, tokens = {}, {}

    def need(l, group, h):
        if (l, group) in gathers:
            set_group(l, group, split_wait(GROUP_MODES[group], gathers[l, group], h, f"gather_wait_l{l}_{group}"))

    def piece_modes(piece):
        return {"mix": SCATTER[1:], "ffn1_gate": SCATTER[:1], "ffn1_up": SCATTER[:1], "ffn1_down": SCATTER[1:]}.get(
            piece, SCATTER)

    def on_piece(l, piece, bufs):
        bufs = [b.reshape(-1, FFP) if b.shape[-1] == FFP else b for b in bufs]
        flight[l, piece], tokens[l, piece] = start(piece_modes(piece), bufs, f"exchange_start_l{l}_{piece}")
        return tokens[l, piece][0, 0]

    loss_local, dx, grads, d_final = _core(x.reshape(T, D), p.reshape(DEPTH, T, PLE), W,
                                           loss_target.reshape(T, D), S, need, on_piece)
    loss = lax.psum(loss_local, AXES)

    small_full = [jnp.stack([grads[l][name] for l in range(DEPTH)]) for name, _, _ in SMALL]
    small_full += [grads[0]["rel_bias"] + grads[1]["rel_bias"], d_final]
    small_flight, _ = start(("gather_block",), (_pack_rows(small_full, LANE, 8),), "gather_start_small_grads",
                            (tokens[0, "ffn1_down"],))

    out = {}

    shards = [{} for _ in range(DEPTH)]

    def land(l, piece, after):
        parts = split_wait(piece_modes(piece), flight[l, piece], after, f"exchange_wait_l{l}_{piece}")
        shards[l].update(_shard_grads(piece, [sum_parts(t, f"sum_grads_l{l}_{piece}_{i}")
                                              for i, t in enumerate(parts)]))

    def update(piece):
        for name in PIECE_NAMES[piece]:
            g = jnp.stack([shards[l][name] for l in range(DEPTH)])
            out[name] = (g,) + tuple(adamw(g, a[name], a["m_" + name], a["v_" + name], "adamw_" + name))

    for piece in ("ffn2", "mix", "ffn1"):
        land(1, piece, (dx, tokens[0, "ffn1_down"]))
    summed1 = tuple(shards[1][PIECE_NAMES[piece][0]] for piece in PIECE_NAMES)
    land(0, "ffn2", summed1)
    land(0, "mix", summed1)
    update("ffn2")
    update("mix")
    done_early = tuple(out[n][1] for n in PIECE_NAMES["ffn2"] + PIECE_NAMES["mix"])
    small_parts, = split_wait(("gather_block",), small_flight, done_early, "gather_wait_small_grads")
    small_sum = sum_parts(small_parts, "sum_small_grads")
    g_small = dict(zip([n for n, _, _ in SMALL] + [n for n, _ in SINGLE],
                       _unpack_rows(small_sum, [s.shape for s in small_full])))
    for name, _, cols in SMALL:
        if cols is not None:
            g_small[name] = lax.dynamic_slice_in_dim(g_small[name], my_slot * cols, cols, axis=2)

    for n in [n for n, _, _ in SMALL] + [n for n, _ in SINGLE]:
        shape = a[n].shape
        flat = lambda t: t.reshape((-1, shape[-1]) if len(shape) > 1 else (1, -1))
        res = adamw(flat(g_small[n]), flat(a[n]), flat(a["m_" + n]), flat(a["v_" + n]), "adamw_" + n)
        out[n] = (g_small[n].reshape(shape),) + tuple(r.reshape(shape) for r in res)

    for piece in ("ffn1_gate", "ffn1_up", "ffn1_down"):
        land(0, piece, (out["final_norm"][1],) + done_early)
    update("ffn1")

    order = ['ffn1_norm', 'ffn1_w_gate', 'ffn1_w_up', 'ffn1_w_down', 'mix_norm', 'w_in', 'lru_conv_w', 'lru_conv_b',
             'lru_w_a', 'lru_b_a', 'lru_w_x', 'lru_b_x', 'lru_lambda', 'attn_sinks', 'rel_bias', 'dn_conv_w',
             'dn_a_log', 'dn_dt_bias', 'dn_norm', 'w_out', 'ffn2_norm', 'ffn2_w_gate', 'ffn2_w_up', 'ffn2_w_down',
             'ple_norm', 'ple_w_gate', 'ple_w_proj', 'final_norm']
    return (loss, dx.reshape(x.shape)) + tuple(out[n][k] for k in range(4) for n in order)
```

```python
import functools
import math

import numpy as np
import jax
import jax.numpy as jnp
from jax import lax
from jax.experimental import pallas as pl
from jax.experimental.pallas import tpu as pltpu

F32 = jnp.float32
BF16 = jnp.bfloat16
HI = lax.Precision.HIGHEST

D = 1024
DEPTH = 2
EPS = 1e-6
PLE = 256
FF = 2816
HD = 64
LRU_W = 256
LRU_C = 8.0
ATT_W = 512
ATT_H = 8
ATT_KV = 2
ATT_G = 4
KV_W = 128
WINDOW = 128
BQ = 128
REL_BUCKETS = 32
REL_MAX_DIST = 128
DN_W = 256
DN_H = 4
CHUNK = 64
D_IN = 2312
D_IN_PAD = 2432
N_DEV = 8

ADAM_LR = 0.001
ADAM_B1 = 0.9
ADAM_B2 = 0.999
ADAM_EPS = 1e-08
ADAM_WD = 0.01
ADAM_STEP = 10

LANE = 128
VMEM_LIMIT = 56 * 1024 * 1024
SH = FF // N_DEV
SHP = 384
FFP = N_DEV * SHP
FF_TILE = 2 * SHP
TOK_TILE = 512
R_DOWN2, R_WOUT, R_PGATE, R_PPROJ, R_ROWS = 0, 384, 512, 640, 672
WIN_ROWS = D * D_IN // N_DEV // 1024
IN_ROWS = 304
NEG = -1e30


def _cp(*sem):
    return pltpu.CompilerParams(dimension_semantics=tuple(sem), vmem_limit_bytes=VMEM_LIMIT)


def _dg(a, b, ca, cb, exact):
    dims = (((ca,), (cb,)), ((), ()))
    if exact == "f32":
        return lax.dot_general(a.astype(F32), b.astype(F32), dims, precision=HI, preferred_element_type=F32)
    if exact == "split":
        a_hi, b_hi = a.astype(BF16), b.astype(BF16)
        a_lo = (a - a_hi.astype(F32)).astype(BF16)
        b_lo = (b - b_hi.astype(F32)).astype(BF16)
        dot = lambda u, v: lax.dot_general(u, v, dims, preferred_element_type=F32)
        return dot(a_hi, b_hi) + (dot(a_hi, b_lo) + dot(a_lo, b_hi))
    return lax.dot_general(a.astype(BF16), b.astype(BF16), dims, preferred_element_type=F32)


def _make_mm(exact):
    @jax.custom_vjp
    def mm(a, b):
        return _dg(a, b, 1, 0, exact)

    @jax.custom_vjp
    def mm_nt(a, b):
        return _dg(a, b, 1, 1, exact)

    @jax.custom_vjp
    def mm_tn(a, b):
        return _dg(a, b, 0, 0, exact)

    mm.defvjp(lambda a, b: (mm(a, b), (a, b)),
              lambda r, d: (mm_nt(d, r[1]), mm_tn(r[0], d)))
    mm_nt.defvjp(lambda a, b: (mm_nt(a, b), (a, b)),
                 lambda r, d: (mm(d, r[1]), mm_tn(d, r[0])))
    mm_tn.defvjp(lambda a, b: (mm_tn(a, b), (a, b)),
                 lambda r, d: (mm_nt(r[1], d), mm(r[0], d)))
    return mm, mm_nt, mm_tn


_mm, _mm_nt, _mm_tn = _make_mm("bf16")
_mmx, _mmx_nt, _mmx_tn = _make_mm("f32")
_mm3, _mm3_nt, _mm3_tn = _make_mm("split")


def _iota(shape, dim):
    return lax.broadcasted_iota(jnp.int32, shape, dim)


def _sigmoid(x):
    return 1.0 / (1.0 + jnp.exp(-x))


def _rms(h, g):
    rstd = lax.rsqrt(jnp.mean(h * h, axis=-1, keepdims=True) + EPS)
    xhat = h * rstd
    return xhat * g, xhat, rstd


def _rms_bwd(dxn, xhat, rstd, g):
    dxhat = dxn * g
    dh = rstd * (dxhat - xhat * jnp.mean(dxhat * xhat, axis=-1, keepdims=True))
    dg = jnp.sum(dxn * xhat, axis=0, keepdims=True)
    return dh, dg


def _row_spec(tm, n):
    return pl.BlockSpec((tm, n), lambda i, *_: (i, 0))


def _full_spec(shape):
    nd = len(shape)
    return pl.BlockSpec(shape, lambda *_: (0,) * nd)


def _ffn_weight_specs():
    return [pl.BlockSpec((D, FF_TILE), lambda i, j: (0, j)),
            pl.BlockSpec((D, FF_TILE), lambda i, j: (1, j)),
            pl.BlockSpec((2, SHP, D), lambda i, j: (j, 0, 0))]


def ffn_fwd(h, g, wa, wb, name):
    T = h.shape[0]
    tm = min(TOK_TILE, T)
    nj = FFP // FF_TILE

    def body(h_ref, g_ref, wg_ref, wu_ref, wd_ref, o_ref, gt_ref, up_ref, xn_ref):
        j = pl.program_id(1)

        @pl.when(j == 0)
        def _():
            hh = h_ref[...]
            xn_ref[...] = _rms(hh, g_ref[...])[0].astype(BF16)
            o_ref[...] = hh

        xn = xn_ref[...]
        gt = _mm(xn, wg_ref[...])
        up = _mm(xn, wu_ref[...])
        gt_ref[...] = gt.astype(BF16)
        up_ref[...] = up.astype(BF16)
        act = gt * _sigmoid(gt) * up
        o_ref[...] += 0.5 * _mm(act, wd_ref[...].reshape(FF_TILE, D))

    tile = pl.BlockSpec((tm, FF_TILE), lambda i, j: (i, j))
    return pl.pallas_call(
        body, name=name, grid=(T // tm, nj),
        in_specs=[pl.BlockSpec((tm, D), lambda i, j: (i, 0)),
                  pl.BlockSpec((1, D), lambda i, j: (0, 0))] + _ffn_weight_specs(),
        out_specs=[pl.BlockSpec((tm, D), lambda i, j: (i, 0)), tile, tile,
                   pl.BlockSpec((tm, D), lambda i, j: (i, 0))],
        out_shape=[jax.ShapeDtypeStruct((T, D), F32), jax.ShapeDtypeStruct((T, FFP), BF16),
                   jax.ShapeDtypeStruct((T, FFP), BF16), jax.ShapeDtypeStruct((T, D), BF16)],
        compiler_params=_cp("parallel", "arbitrary"),
    )(h, g, wa, wa, wb)


def ffn_bwd(h, dy, g, gt_saved, up_saved, wa, wb, name):
    T = h.shape[0]
    tm = min(TOK_TILE, T)
    nj = FFP // FF_TILE

    def body(h_ref, dy_ref, g_ref, gt_ref, up_ref, wg_ref, wu_ref, wd_ref,
             dh_ref, dg_ref, du_ref, a_ref, dn_ref, dxn_s, dyh_s):
        i = pl.program_id(0)
        j = pl.program_id(1)

        @pl.when(j == 0)
        def _():
            dxn_s[...] = jnp.zeros_like(dxn_s)
            dyh_s[...] = (0.5 * dy_ref[...]).astype(BF16)

        @pl.when((i == 0) & (j == 0))
        def _():
            dn_ref[...] = jnp.zeros_like(dn_ref)

        gt = gt_ref[...].astype(F32)
        up = up_ref[...].astype(F32)
        sg = _sigmoid(gt)
        si = gt * sg
        da = _mm_nt(dyh_s[...], wd_ref[...].reshape(FF_TILE, D))
        dup = da * si
        dgt = da * up * (sg * (1.0 + gt * (1.0 - sg)))
        dg_ref[...] = dgt.astype(BF16)
        du_ref[...] = dup.astype(BF16)
        a_ref[...] = (si * up).astype(BF16)
        dxn_s[...] += _mm_nt(dgt, wg_ref[...]) + _mm_nt(dup, wu_ref[...])

        @pl.when(j == nj - 1)
        def _():
            gg = g_ref[...]
            _, xhat, rstd = _rms(h_ref[...], gg)
            dh, dn = _rms_bwd(dxn_s[...], xhat, rstd, gg)
            dh_ref[...] = dy_ref[...] + dh
            dn_ref[...] += dn

    tile = pl.BlockSpec((tm, FF_TILE), lambda i, j: (i, j))
    return pl.pallas_call(
        body, name=name, grid=(T // tm, nj),
        in_specs=[pl.BlockSpec((tm, D), lambda i, j: (i, 0)),
                  pl.BlockSpec((tm, D), lambda i, j: (i, 0)),
                  pl.BlockSpec((1, D), lambda i, j: (0, 0)), tile, tile] + _ffn_weight_specs(),
        out_specs=[pl.BlockSpec((tm, D), lambda i, j: (i, 0)), tile, tile, tile,
                   pl.BlockSpec((1, D), lambda i, j: (0, 0))],
        out_shape=[jax.ShapeDtypeStruct((T, D), F32)] + [jax.ShapeDtypeStruct((T, FFP), BF16)] * 3
        + [jax.ShapeDtypeStruct((1, D), F32)],
        scratch_shapes=[pltpu.VMEM((tm, D), F32), pltpu.VMEM((tm, D), BF16)],
        compiler_params=_cp("arbitrary", "arbitrary"),
    )(h, dy, g, gt_saved, up_saved, wa, wa, wb)


def _pick(n, prefs):
    for t in prefs:
        if n % t == 0:
            return t
    return n


def _tn_body(nk, scale, out_dtype, squeeze):
    def body(a_ref, b_ref, *rest):
        o_ref, acc = rest[-2], rest[-1]
        k = pl.program_id(2)

        @pl.when(k == 0)
        def _():
            acc[...] = jnp.zeros_like(acc)

        acc[...] += _mm_tn(a_ref[...], b_ref[...])

        @pl.when(k == nk - 1)
        def _():
            res = (scale * acc[...]).astype(out_dtype)
            if squeeze:
                o_ref[0] = res
            else:
                o_ref[...] = res

    return body


def matmul_tn(a, b, name, scale=1.0, out_dtype=BF16):
    T, M = a.shape
    N = b.shape[1]
    tmm = _pick(M, (512, 256))
    tnn = _pick(N, (1024, 2432))
    tk = min(TOK_TILE, T)
    nk = T // tk
    return pl.pallas_call(
        _tn_body(nk, scale, out_dtype, False), name=name, grid=(M // tmm, N // tnn, nk),
        in_specs=[pl.BlockSpec((tk, tmm), lambda i, j, k: (k, i)),
                  pl.BlockSpec((tk, tnn), lambda i, j, k: (k, j))],
        out_specs=pl.BlockSpec((tmm, tnn), lambda i, j, k: (i, j)),
        out_shape=jax.ShapeDtypeStruct((M, N), out_dtype),
        scratch_shapes=[pltpu.VMEM((tmm, tnn), F32)],
        compiler_params=_cp("parallel", "parallel", "arbitrary"),
    )(a, b)


def grad_cols(a, b, dst, slot, name):
    T = a.shape[0]
    tmm, tnn = D, FFP // 2
    tk = min(TOK_TILE, T)
    nk = T // tk
    return pl.pallas_call(
        _tn_body(nk, 1.0, BF16, True), name=name, grid=(D // tmm, FFP // tnn, nk),
        in_specs=[pl.BlockSpec((tk, tmm), lambda i, j, k: (k, i)),
                  pl.BlockSpec((tk, tnn), lambda i, j, k: (k, j)),
                  pl.BlockSpec(memory_space=pl.ANY)],
        out_specs=pl.BlockSpec((1, tmm, tnn), lambda i, j, k: (slot, i, j)),
        out_shape=jax.ShapeDtypeStruct(dst.shape, dst.dtype),
        scratch_shapes=[pltpu.VMEM((tmm, tnn), F32)],
        input_output_aliases={2: 0},
        compiler_params=_cp("parallel", "parallel", "arbitrary"),
    )(a, b, dst)


def grad_rows(a, b, dst, row0, rows, name, scale=1.0):
    T = a.shape[0]
    tk = min(TOK_TILE, T)
    nk = T // tk
    blk = row0 // rows

    def body(a_ref, b_ref, dst_ref, o_ref, acc):
        k = pl.program_id(0)

        @pl.when(k == 0)
        def _():
            acc[...] = jnp.zeros_like(acc)

        acc[...] += _mm_tn(a_ref[...], b_ref[...])

        @pl.when(k == nk - 1)
        def _():
            o_ref[...] = (scale * acc[...]).astype(BF16).reshape(N_DEV, rows, D)

    return pl.pallas_call(
        body, name=name, grid=(nk,),
        in_specs=[pl.BlockSpec((tk, N_DEV * rows), lambda k: (k, 0)),
                  pl.BlockSpec((tk, D), lambda k: (k, 0)),
                  pl.BlockSpec(memory_space=pl.ANY)],
        out_specs=pl.BlockSpec((N_DEV, rows, D), lambda k: (0, blk, 0)),
        out_shape=jax.ShapeDtypeStruct(dst.shape, dst.dtype),
        scratch_shapes=[pltpu.VMEM((N_DEV * rows, D), F32)],
        input_output_aliases={2: 0},
        compiler_params=_cp("arbitrary"),
    )(a, b, dst)


U_SPLITS = (512, 768, 1024, 128)
U_OFFS = (0, 512, 1280, 2304)


def mixin_fwd(h, g, w_in, name):
    T = h.shape[0]
    tm = min(TOK_TILE, T)

    def body(h_ref, g_ref, w_ref, u0, u1, u2, u3, xn_ref):
        xn = _rms(h_ref[...], g_ref[...])[0].astype(BF16)
        xn_ref[...] = xn
        u = _mm(xn, w_ref[...])
        for ref, off, n in zip((u0, u1, u2, u3), U_OFFS, U_SPLITS):
            ref[...] = u[:, off:off + n]

    return pl.pallas_call(
        body, name=name, grid=(T // tm,),
        in_specs=[_row_spec(tm, D), _full_spec((1, D)), _full_spec((D, D_IN_PAD))],
        out_specs=[_row_spec(tm, n) for n in U_SPLITS] + [_row_spec(tm, D)],
        out_shape=[jax.ShapeDtypeStruct((T, n), F32) for n in U_SPLITS]
        + [jax.ShapeDtypeStruct((T, D), BF16)],
        compiler_params=_cp("parallel"),
    )(h, g, w_in)


DU_SPLITS = (256, 256, 768, 768, 256, 128)
DU_OFFS = (0, 256, 512, 1280, 2048, 2304)


def mixin_bwd(h, dh_in, g, w_in, dus, name):
    T = h.shape[0]
    tm = min(TOK_TILE, T)

    def body(h_ref, dhi_ref, g_ref, w_ref, *refs):
        dh_ref, du_ref, dn_ref = refs[-3:]

        @pl.when(pl.program_id(0) == 0)
        def _():
            dn_ref[...] = jnp.zeros_like(dn_ref)

        dxn = jnp.zeros((tm, D), F32)
        for ref, off, n in zip(refs[:-3], DU_OFFS, DU_SPLITS):
            du = ref[...]
            du_ref[:, off:off + n] = du.astype(BF16)
            dxn += _mm_nt(du, w_ref[:, off:off + n])
        gg = g_ref[...]
        _, xhat, rstd = _rms(h_ref[...], gg)
        dh, dn = _rms_bwd(dxn, xhat, rstd, gg)
        dh_ref[...] = dhi_ref[...] + dh
        dn_ref[...] += dn

    return pl.pallas_call(
        body, name=name, grid=(T // tm,),
        in_specs=[_row_spec(tm, D), _row_spec(tm, D), _full_spec((1, D)), _full_spec((D, D_IN_PAD))]
        + [_row_spec(tm, n) for n in DU_SPLITS],
        out_specs=[_row_spec(tm, D), _row_spec(tm, D_IN_PAD), _full_spec((1, D))],
        out_shape=[jax.ShapeDtypeStruct((T, D), F32), jax.ShapeDtypeStruct((T, D_IN_PAD), BF16),
                   jax.ShapeDtypeStruct((1, D), F32)],
        compiler_params=_cp("arbitrary"),
    )(h, dh_in, g, w_in, *dus)


def _shift_down(x, s, row):
    if s == 0:
        return x
    return jnp.where(row >= s, pltpu.roll(x, s, 0), 0.0)


def _shift_up(x, s, row):
    if s == 0:
        return x
    n = x.shape[0]
    return jnp.where(row < n - s, pltpu.roll(x, n - s, 0), 0.0)


def conv_fwd(x, w, b, S, col0, C, name):
    T = x.shape[0]
    cb0 = col0 // LANE

    def body(x_ref, w_ref, b_ref, y_ref):
        xx = x_ref[...]
        row = _iota(xx.shape, 0)
        y = xx * w_ref[3:4, :] + b_ref[...]
        for k in range(3):
            y += _shift_down(xx, 3 - k, row) * w_ref[k:k + 1, :]
        y_ref[...] = y

    return pl.pallas_call(
        body, name=name, grid=(T // S, C // LANE),
        in_specs=[pl.BlockSpec((S, LANE), lambda s, c: (s, cb0 + c)),
                  pl.BlockSpec((4, LANE), lambda s, c: (0, c)),
                  pl.BlockSpec((1, LANE), lambda s, c: (0, c))],
        out_specs=pl.BlockSpec((S, LANE), lambda s, c: (s, c)),
        out_shape=jax.ShapeDtypeStruct((T, C), F32),
        compiler_params=_cp("parallel", "parallel"),
    )(x, w, b)


def conv_bwd(x, dy, w, S, col0, C, name):
    T = x.shape[0]
    cb0 = col0 // LANE

    def body(x_ref, dy_ref, w_ref, dx_ref, dwb_ref):
        @pl.when(pl.program_id(1) == 0)
        def _():
            dwb_ref[...] = jnp.zeros_like(dwb_ref)

        xx = x_ref[...]
        dd = dy_ref[...]
        row = _iota(xx.shape, 0)
        dx = dd * w_ref[3:4, :]
        for k in range(3):
            dx += _shift_up(dd, 3 - k, row) * w_ref[k:k + 1, :]
        dx_ref[...] = dx
        for k in range(4):
            dwb_ref[k:k + 1, :] += jnp.sum(dd * _shift_down(xx, 3 - k, row), axis=0, keepdims=True)
        dwb_ref[4:5, :] += jnp.sum(dd, axis=0, keepdims=True)

    return pl.pallas_call(
        body, name=name, grid=(C // LANE, T // S),
        in_specs=[pl.BlockSpec((S, LANE), lambda c, s: (s, cb0 + c)),
                  pl.BlockSpec((S, LANE), lambda c, s: (s, c)),
                  pl.BlockSpec((4, LANE), lambda c, s: (0, c))],
        out_specs=[pl.BlockSpec((S, LANE), lambda c, s: (s, c)),
                   pl.BlockSpec((8, LANE), lambda c, s: (0, c))],
        out_shape=[jax.ShapeDtypeStruct((T, C), F32), jax.ShapeDtypeStruct((8, C), F32)],
        compiler_params=_cp("parallel", "arbitrary"),
    )(x, dy, w)


def _scan(a, b, row):
    n = a.shape[0]
    d = 1
    while d < n:
        keep = row >= d
        b = a * jnp.where(keep, pltpu.roll(b, d, 0), 0.0) + b
        a = a * jnp.where(keep, pltpu.roll(a, d, 0), 1.0)
        d *= 2
    return b


def _rscan(a, b, row):
    n = a.shape[0]
    d = 1
    while d < n:
        keep = row < n - d
        b = a * jnp.where(keep, pltpu.roll(b, n - d, 0), 0.0) + b
        a = a * jnp.where(keep, pltpu.roll(a, n - d, 0), 1.0)
        d *= 2
    return b


GELU_C = math.sqrt(2.0 / math.pi)


def _gelu(x):
    t = jnp.tanh(GELU_C * (x + 0.044715 * (x * x * x)))
    return 0.5 * x * (1.0 + t), t


def _lru_gates(xr, wa, ba, wx, bx, lam):
    r = _sigmoid(_mm(xr, wa) + ba)
    i = _sigmoid(_mm(xr, wx) + bx)
    sp = jnp.maximum(-lam, 0.0) + jnp.log(1.0 + jnp.exp(-jnp.abs(lam)))
    la = -LRU_C * r * sp
    a = jnp.exp(la)
    e2 = a * a
    m = jnp.sqrt(-jnp.tanh(la) * (e2 + 1.0))
    return r, i, sp, a, e2, m


def lru_fwd(xr, u_lru, wa, wx, vec, S, name):
    T = xr.shape[0]

    def body(xr_ref, gt_ref, wa_ref, wx_ref, vec_ref, y_ref):
        x = xr_ref[...]
        row = _iota(x.shape, 0)
        r, i, sp, a, e2, m = _lru_gates(x, wa_ref[...], vec_ref[0:1, :], wx_ref[...], vec_ref[1:2, :],
                                        vec_ref[2:3, :])
        hh = _scan(a, m * (i * x), row)
        y_ref[...] = _gelu(gt_ref[...])[0] * hh

    return pl.pallas_call(
        body, name=name, grid=(T // S, LRU_W // LANE),
        in_specs=[pl.BlockSpec((S, LANE), lambda s, c: (s, c)),
                  pl.BlockSpec((S, LANE), lambda s, c: (s, 2 + c)),
                  pl.BlockSpec((LANE, LANE), lambda s, c: (c, c)),
                  pl.BlockSpec((LANE, LANE), lambda s, c: (c, c)),
                  pl.BlockSpec((8, LANE), lambda s, c: (0, c))],
        out_specs=pl.BlockSpec((S, LANE), lambda s, c: (s, c)),
        out_shape=jax.ShapeDtypeStruct((T, LRU_W), F32),
        compiler_params=_cp("parallel", "parallel"),
    )(xr, u_lru, wa, wx, vec)


def lru_bwd(xr, u_lru, dy, wa, wx, vec, S, name):
    T = xr.shape[0]

    def body(xr_ref, gt_ref, dy_ref, wa_ref, wx_ref, vec_ref,
             dxr_ref, dgt_ref, dwa_ref, dwx_ref, dvec_ref):
        @pl.when(pl.program_id(1) == 0)
        def _():
            dwa_ref[...] = jnp.zeros_like(dwa_ref)
            dwx_ref[...] = jnp.zeros_like(dwx_ref)
            dvec_ref[...] = jnp.zeros_like(dvec_ref)

        x = xr_ref[...]
        n = x.shape[0]
        row = _iota(x.shape, 0)
        lam = vec_ref[2:3, :]
        r, i, sp, a, e2, m = _lru_gates(x, wa_ref[...], vec_ref[0:1, :], wx_ref[...], vec_ref[1:2, :], lam)
        v = i * x
        hh = _scan(a, m * v, row)
        gt = gt_ref[...]
        dy = dy_ref[...]
        ge, t = _gelu(gt)
        dgt_ref[...] = dy * hh * (0.5 * (1.0 + t) + 0.5 * gt * (1.0 - t * t) * GELU_C
                                  * (1.0 + 3.0 * 0.044715 * gt * gt))
        a_next = jnp.where(row < n - 1, pltpu.roll(a, n - 1, 0), 0.0)
        G = _rscan(a_next, dy * ge, row)
        da = G * _shift_down(hh, 1, row)
        dv = G * m
        dla = da * a - (G * v) * e2 / m
        dr = dla * (-LRU_C * sp)
        dsp = jnp.sum(dla * (-LRU_C * r), axis=0, keepdims=True)
        dra = dr * r * (1.0 - r)
        dia = (dv * x) * i * (1.0 - i)
        dxr_ref[...] = dv * i + _mm_nt(dra, wa_ref[...]) + _mm_nt(dia, wx_ref[...])
        dwa_ref[0] += _mm_tn(x, dra)
        dwx_ref[0] += _mm_tn(x, dia)
        dvec_ref[0:1, :] += jnp.sum(dra, axis=0, keepdims=True)
        dvec_ref[1:2, :] += jnp.sum(dia, axis=0, keepdims=True)
        dvec_ref[2:3, :] += dsp * (-_sigmoid(-lam))

    return pl.pallas_call(
        body, name=name, grid=(LRU_W // LANE, T // S),
        in_specs=[pl.BlockSpec((S, LANE), lambda c, s: (s, c)),
                  pl.BlockSpec((S, LANE), lambda c, s: (s, 2 + c)),
                  pl.BlockSpec((S, LANE), lambda c, s: (s, c)),
                  pl.BlockSpec((LANE, LANE), lambda c, s: (c, c)),
                  pl.BlockSpec((LANE, LANE), lambda c, s: (c, c)),
                  pl.BlockSpec((8, LANE), lambda c, s: (0, c))],
        out_specs=[pl.BlockSpec((S, LANE), lambda c, s: (s, c)),
                   pl.BlockSpec((S, LANE), lambda c, s: (s, c)),
                   pl.BlockSpec((1, LANE, LANE), lambda c, s: (c, 0, 0)),
                   pl.BlockSpec((1, LANE, LANE), lambda c, s: (c, 0, 0)),
                   pl.BlockSpec((8, LANE), lambda c, s: (0, c))],
        out_shape=[jax.ShapeDtypeStruct((T, LRU_W), F32), jax.ShapeDtypeStruct((T, LRU_W), F32),
                   jax.ShapeDtypeStruct((2, LANE, LANE), F32), jax.ShapeDtypeStruct((2, LANE, LANE), F32),
                   jax.ShapeDtypeStruct((8, LRU_W), F32)],
        compiler_params=_cp("parallel", "arbitrary"),
    )(xr, u_lru, dy, wa, wx, vec)


def _bucket_table():
    qi = np.arange(BQ)[:, None]
    kj = np.arange(2 * BQ)[None, :]
    dist = BQ + qi - kj
    band = (dist >= 0) & (dist < WINDOW)
    dd = np.maximum(dist, 0)
    max_exact = REL_BUCKETS // 2
    large = max_exact + (np.log(np.maximum(dd, 1).astype(np.float32) / np.float32(max_exact))
                         / np.float32(math.log(REL_MAX_DIST / max_exact))
                         * np.float32(REL_BUCKETS - max_exact)).astype(np.int32)
    large = np.minimum(large, REL_BUCKETS - 1)
    bucket = np.where(dd < max_exact, dd, large)
    return np.where(band, bucket, -1).astype(np.int32)


def _att_specs(S):
    nb = S // BQ
    qc = ATT_W // LANE
    return [pl.BlockSpec((BQ, ATT_W), lambda b, n: (b * nb + n, 0)),
            pl.BlockSpec((BQ, KV_W), lambda b, n: (b * nb + jnp.maximum(n - 1, 0), qc)),
            pl.BlockSpec((BQ, KV_W), lambda b, n: (b * nb + n, qc)),
            pl.BlockSpec((BQ, KV_W), lambda b, n: (b * nb + jnp.maximum(n - 1, 0), qc + 1)),
            pl.BlockSpec((BQ, KV_W), lambda b, n: (b * nb + n, qc + 1))]


def _att_bias(bk, rb_ref, bias_s):
    for h in range(ATT_H):
        acc = jnp.zeros(bk.shape, F32)
        for bb in range(REL_BUCKETS):
            acc = jnp.where(bk == bb, rb_ref[bb * ATT_H + h], acc)
        bias_s[h] = acc


def _att_probs(qs, kgs, bias_s, valid, sk_ref):
    heads = range(ATT_H)
    s = [_mm_nt(qs[h], kgs[h // ATT_G]) for h in heads]
    s = [jnp.where(valid, s[h] * (HD ** -0.5) + bias_s[h], NEG) for h in heads]
    m = [jnp.maximum(jnp.max(s[h], axis=-1, keepdims=True), sk_ref[h]) for h in heads]
    e = [jnp.exp(s[h] - m[h]) for h in heads]
    es = [jnp.exp(sk_ref[h] - m[h]) for h in heads]
    den = [jnp.sum(e[h], axis=-1, keepdims=True) + es[h] for h in heads]
    return [e[h] / den[h] for h in heads], [es[h] / den[h] for h in heads]


def _att_kv(kp_ref, kc_ref, vp_ref, vc_ref):
    cat = lambda a, b, g: jnp.concatenate([a[:, HD * g:HD * (g + 1)], b[:, HD * g:HD * (g + 1)]], axis=0)
    return ([cat(kp_ref, kc_ref, g) for g in range(ATT_KV)], [cat(vp_ref, vc_ref, g) for g in range(ATT_KV)])


def attn_fwd(u_att, sinks, rel_bias, S, name):
    T = u_att.shape[0]
    nb = S // BQ
    table = jnp.asarray(_bucket_table())

    def body(sk_ref, rb_ref, bk_ref, q_ref, kp_ref, kc_ref, vp_ref, vc_ref, o_ref, bias_s):
        b = pl.program_id(0)
        n = pl.program_id(1)
        bk = bk_ref[...]

        @pl.when((b == 0) & (n == 0))
        def _():
            _att_bias(bk, rb_ref, bias_s)

        valid = (bk >= 0) & ((n > 0) | (_iota(bk.shape, 1) >= BQ))
        kgs, vgs = _att_kv(kp_ref, kc_ref, vp_ref, vc_ref)
        p, _ = _att_probs([q_ref[:, HD * h:HD * (h + 1)] for h in range(ATT_H)], kgs, bias_s, valid, sk_ref)
        outs = [_mm(p[h], vgs[h // ATT_G]) for h in range(ATT_H)]
        for h in range(ATT_H):
            o_ref[:, HD * h:HD * (h + 1)] = outs[h]

    smem = pl.BlockSpec(memory_space=pltpu.SMEM)
    return pl.pallas_call(
        body, name=name, grid=(T // S, nb),
        in_specs=[smem, smem, _full_spec((BQ, 2 * BQ))] + _att_specs(S),
        out_specs=pl.BlockSpec((BQ, ATT_W), lambda b, n: (b * nb + n, 0)),
        out_shape=jax.ShapeDtypeStruct((T, ATT_W), F32),
        scratch_shapes=[pltpu.VMEM((ATT_H, BQ, 2 * BQ), F32)],
        compiler_params=_cp("arbitrary", "arbitrary"),
    )(sinks, rel_bias, table, u_att, u_att, u_att, u_att, u_att)


def attn_bwd(u_att, dy, sinks, rel_bias, S, name):
    T = u_att.shape[0]
    nb = S // BQ
    nB = T // S
    table = jnp.asarray(_bucket_table())
    scale = HD ** -0.5

    def body(sk_ref, rb_ref, bk_ref, q_ref, kp_ref, kc_ref, vp_ref, vc_ref, dy_ref,
             du_ref, drel_ref, dsk_ref, bias_s, dbias_s):
        b = pl.program_id(0)
        n = pl.program_id(1)
        bk = bk_ref[...]

        @pl.when((b == 0) & (n == 0))
        def _():
            _att_bias(bk, rb_ref, bias_s)
            dbias_s[...] = jnp.zeros_like(dbias_s)
            dsk_ref[...] = jnp.zeros_like(dsk_ref)
            drel_ref[...] = jnp.zeros_like(drel_ref)

        @pl.when(n == 0)
        def _():
            du_ref[...] = jnp.zeros_like(du_ref)

        valid = (bk >= 0) & ((n > 0) | (_iota(bk.shape, 1) >= BQ))
        r_cur = pl.multiple_of(n * BQ, BQ)
        r_prev = pl.multiple_of(jnp.maximum(n - 1, 0) * BQ, BQ)
        heads = range(ATT_H)
        kgs, vgs = _att_kv(kp_ref, kc_ref, vp_ref, vc_ref)
        qs = [q_ref[:, HD * h:HD * (h + 1)] for h in heads]
        dos = [dy_ref[:, HD * h:HD * (h + 1)] for h in heads]
        p, ps = _att_probs(qs, kgs, bias_s, valid, sk_ref)
        dp = [_mm_nt(dos[h], vgs[h // ATT_G]) for h in heads]
        delta = [jnp.sum(p[h] * dp[h], axis=-1, keepdims=True) for h in heads]
        ds = [p[h] * (dp[h] - delta[h]) for h in heads]
        dss = [ds[h] * scale for h in heads]
        dq = [_mm(dss[h], kgs[h // ATT_G]) for h in heads]
        dks = [_mm_tn(dss[h], qs[h]) for h in heads]
        dvs = [_mm_tn(p[h], dos[h]) for h in heads]
        for h in heads:
            dbias_s[h] += ds[h]
            dsk_ref[h:h + 1, :] += jnp.broadcast_to(jnp.sum(-ps[h] * delta[h], axis=0, keepdims=True), (1, LANE))
            du_ref[pl.ds(r_cur, BQ), HD * h:HD * (h + 1)] = dq[h]
        for g in range(ATT_KV):
            of_group = range(g * ATT_G, (g + 1) * ATT_G)
            dk = functools.reduce(lambda x, y: x + y, [dks[h] for h in of_group])
            dv = functools.reduce(lambda x, y: x + y, [dvs[h] for h in of_group])
            ck = ATT_W + HD * g
            cv = ATT_W + KV_W + HD * g
            du_ref[pl.ds(r_prev, BQ), ck:ck + HD] += dk[0:BQ]
            du_ref[pl.ds(r_cur, BQ), ck:ck + HD] += dk[BQ:]
            du_ref[pl.ds(r_prev, BQ), cv:cv + HD] += dv[0:BQ]
            du_ref[pl.ds(r_cur, BQ), cv:cv + HD] += dv[BQ:]

        @pl.when((b == nB - 1) & (n == nb - 1))
        def _():
            lane = _iota((1, LANE), 1)
            for h in range(ATT_H):
                db = dbias_s[h]
                acc = jnp.zeros((1, LANE), F32)
                for bb in range(REL_BUCKETS):
                    val = jnp.sum(jnp.sum(jnp.where(bk == bb, db, 0.0), axis=1, keepdims=True),
                                  axis=0, keepdims=True)
                    acc = jnp.where(lane == bb, val, acc)
                drel_ref[h:h + 1, :] = acc

    smem = pl.BlockSpec(memory_space=pltpu.SMEM)
    return pl.pallas_call(
        body, name=name, grid=(nB, nb),
        in_specs=[smem, smem, _full_spec((BQ, 2 * BQ))] + _att_specs(S)
        + [pl.BlockSpec((BQ, ATT_W), lambda b, n: (b * nb + n, 0))],
        out_specs=[pl.BlockSpec((S, ATT_W + 2 * KV_W), lambda b, n: (b, 0)),
                   _full_spec((8, LANE)), _full_spec((8, LANE))],
        out_shape=[jax.ShapeDtypeStruct((T, ATT_W + 2 * KV_W), F32),
                   jax.ShapeDtypeStruct((8, LANE), F32), jax.ShapeDtypeStruct((8, LANE), F32)],
        scratch_shapes=[pltpu.VMEM((ATT_H, BQ, 2 * BQ), F32), pltpu.VMEM((ATT_H, BQ, 2 * BQ), F32)],
        compiler_params=_cp("arbitrary", "arbitrary"),
    )(sinks, rel_bias, table, u_att, u_att, u_att, u_att, u_att, dy)


def _head_of(i):
    return lax.shift_right_logical(i, 6)


def _head_mask(shape):
    return (_head_of(_iota(shape, 0)) == _head_of(_iota(shape, 1))).astype(F32)


def _dn_point(c, uba, alog, dtb):
    s = c * _sigmoid(c)
    qt, kt, vt = s[:, 0:256], s[:, 256:512], s[:, 512:768]
    ones_bd = _head_mask((DN_W, DN_W))
    q = qt * lax.rsqrt(_mm3(qt * qt, ones_bd) + EPS) * (HD ** -0.5)
    k = kt * lax.rsqrt(_mm3(kt * kt, ones_bd) + EPS)
    sel = _head_of(_iota((LANE, DN_W), 1))
    row = _iota((LANE, DN_W), 0)
    braw = _mm3(uba, (row == sel).astype(F32))
    araw = _mm3(uba, (row == sel + DN_H).astype(F32)) + dtb
    beta = _sigmoid(braw)
    g = -jnp.exp(alog) * (jnp.maximum(araw, 0.0) + jnp.log(1.0 + jnp.exp(-jnp.abs(araw))))
    return q, k, vt, g, beta


def dn_point_fwd(c, uba, alog, dtb, name):
    T = c.shape[0]
    tm = min(TOK_TILE, T)

    def body(c_ref, u_ref, al_ref, dt_ref, *outs):
        for ref, val in zip(outs, _dn_point(c_ref[...], u_ref[...], al_ref[...], dt_ref[...])):
            ref[...] = val

    return pl.pallas_call(
        body, name=name, grid=(T // tm,),
        in_specs=[_row_spec(tm, 768), _row_spec(tm, LANE), _full_spec((1, DN_W)), _full_spec((1, DN_W))],
        out_specs=[_row_spec(tm, DN_W)] * 5,
        out_shape=[jax.ShapeDtypeStruct((T, DN_W), F32)] * 5,
        compiler_params=_cp("parallel"),
    )(c, uba, alog, dtb)


def dn_point_bwd(c, uba, alog, dtb, douts, name):
    T = c.shape[0]
    tm = min(TOK_TILE, T)

    def body(c_ref, u_ref, al_ref, dt_ref, dq, dk, dv, dg, db, dc_ref, du_ref, dvec_ref):
        @pl.when(pl.program_id(0) == 0)
        def _():
            dvec_ref[...] = jnp.zeros_like(dvec_ref)

        _, vjp = jax.vjp(_dn_point, c_ref[...], u_ref[...], al_ref[...], dt_ref[...])
        dc, du, dal, ddt = vjp((dq[...], dk[...], dv[...], dg[...], db[...]))
        dc_ref[...] = dc
        du_ref[...] = du
        fold = (_iota((LANE, DN_W), 0) == _head_of(_iota((LANE, DN_W), 1))).astype(F32)
        both = jnp.concatenate([dal, ddt, jnp.zeros((6, DN_W), F32)], axis=0)
        dvec_ref[...] += _mmx_nt(both, fold)

    return pl.pallas_call(
        body, name=name, grid=(T // tm,),
        in_specs=[_row_spec(tm, 768), _row_spec(tm, LANE), _full_spec((1, DN_W)), _full_spec((1, DN_W))]
        + [_row_spec(tm, DN_W)] * 5,
        out_specs=[_row_spec(tm, 768), _row_spec(tm, LANE), _full_spec((8, LANE))],
        out_shape=[jax.ShapeDtypeStruct((T, 768), F32), jax.ShapeDtypeStruct((T, LANE), F32),
                   jax.ShapeDtypeStruct((8, LANE), F32)],
        compiler_params=_cp("arbitrary"),
    )(c, uba, alog, dtb, *douts)


def _unit_lower_inverses(lmats):
    eye = (_iota(lmats[0].shape, 0) == _iota(lmats[0].shape, 1)).astype(F32)
    tinvs = [eye - lm for lm in lmats]
    pws = list(lmats)
    for _ in range(5):
        pws = [_mm3(pw, pw) for pw in pws]
        tinvs = [t + _mm3(t, pw) for t, pw in zip(tinvs, pws)]
    return tuple(tinvs)


def _inverse_bwd(tinv, d):
    return -_mm3_nt(_mm3_tn(tinv, d), tinv)


@jax.custom_vjp
def _tri_invs(lmats):
    return _unit_lower_inverses(lmats)


def _tri_invs_fwd(lmats):
    tinvs = _unit_lower_inverses(lmats)
    return tinvs, tinvs


_tri_invs.defvjp(_tri_invs_fwd, lambda tinvs, ds: (tuple(_inverse_bwd(t, d) for t, d in zip(tinvs, ds)),))


@jax.custom_vjp
def _tri_inv_known(lmat, tinv):
    return tinv


_tri_inv_known.defvjp(lambda lmat, tinv: (tinv, tinv),
                      lambda tinv, d: (_inverse_bwd(tinv, d), jnp.zeros_like(tinv)))


DN_SUB = 4


def _dn_stack(x):
    return jnp.concatenate([x, x, x, x], axis=0) * _head_mask((DN_W, DN_W))


def _dn_pre_inverse(q, k, v, g, beta):
    hm = _head_mask((DN_W, DN_W))
    ri = _iota((DN_W, DN_W), 0) & (CHUNK - 1)
    ci = _iota((DN_W, DN_W), 1) & (CHUNK - 1)
    tri64 = (_iota((CHUNK, CHUNK), 0) >= _iota((CHUNK, CHUNK), 1)).astype(F32)
    gc = _mm3(tri64, g)
    ks = _dn_stack(k)
    gcol = jnp.sum(_dn_stack(gc), axis=1, keepdims=True) * (1.0 / HD)
    gmat = jnp.broadcast_to(gcol, (DN_W, DN_W))
    decay = jnp.exp(jnp.minimum(gmat - gmat.T, 0.0))
    lmat = _mm_nt(_dn_stack(k * beta), ks) * decay * (hm * (ri > ci).astype(F32))
    att = _mm_nt(_dn_stack(q), ks) * decay * (hm * (ri >= ci).astype(F32))
    return lmat, att, gc


def _dn_post_inverse(q, k, v, g, beta, tinv, att, gc):
    glast = jnp.sum(g, axis=0, keepdims=True)
    eg = jnp.exp(gc)
    u = _mm(tinv, _dn_stack(v * beta))
    w = _mm(tinv, _dn_stack(k * beta * eg))
    return u, w, att, _dn_stack(q * eg), _dn_stack(k * jnp.exp(glast - gc)), jnp.exp(glast), tinv


def _dn_apply(state, prep):
    u, w, att, qe, kd, eglast, _ = prep
    vn = u - _mm(w, state)
    o4 = _mm(qe, state) + _mm(att, vn)
    o = o4[0:64] + o4[64:128] + o4[128:192] + o4[192:256]
    return o, state * eglast + _mm_tn(kd, vn)


def _dn_chunks(state, q, k, v, g, beta, knowns=None):
    n = q.shape[0] // CHUNK
    chunks = [tuple(x[c * CHUNK:(c + 1) * CHUNK] for x in (q, k, v, g, beta)) for c in range(n)]
    pre = [_dn_pre_inverse(*ch) for ch in chunks]
    if knowns is None:
        tinvs = _tri_invs(tuple(p[0] for p in pre))
    else:
        tinvs = [_tri_inv_known(p[0], known) for p, known in zip(pre, knowns)]
    preps = [_dn_post_inverse(*ch, tinv, p[1], p[2]) for ch, tinv, p in zip(chunks, tinvs, pre)]
    outs = []
    for prep in preps:
        o, state = _dn_apply(state, prep)
        outs.append(o)
    return jnp.concatenate(outs, axis=0), state, [prep[-1] for prep in preps]


def dn_scan_fwd(q, k, v, g, beta, S, name):
    T = q.shape[0]
    rows = DN_SUB * CHUNK
    ns = S // rows

    def body(q_ref, k_ref, v_ref, g_ref, b_ref, o_ref, st_ref, ti_ref, s_s):
        @pl.when(pl.program_id(1) == 0)
        def _():
            s_s[...] = jnp.zeros_like(s_s)

        st = s_s[...]
        st_ref[0] = st
        o, new, tinvs = _dn_chunks(st, q_ref[...], k_ref[...], v_ref[...], g_ref[...], b_ref[...])
        o_ref[...] = o
        for c, tinv in enumerate(tinvs):
            ti_ref[c] = tinv
        s_s[...] = new

    spec = pl.BlockSpec((rows, DN_W), lambda b, t: (b * ns + t, 0))
    return pl.pallas_call(
        body, name=name, grid=(T // S, ns),
        in_specs=[spec] * 5,
        out_specs=[spec, pl.BlockSpec((1, DN_W, DN_W), lambda b, t: (b * ns + t, 0, 0)),
                   pl.BlockSpec((DN_SUB, DN_W, DN_W), lambda b, t: (b * ns + t, 0, 0))],
        out_shape=[jax.ShapeDtypeStruct((T, DN_W), F32),
                   jax.ShapeDtypeStruct((T // rows, DN_W, DN_W), F32),
                   jax.ShapeDtypeStruct((T // CHUNK, DN_W, DN_W), F32)],
        scratch_shapes=[pltpu.VMEM((DN_W, DN_W), F32)],
        compiler_params=_cp("parallel", "arbitrary"),
    )(q, k, v, g, beta)


def dn_scan_bwd(q, k, v, g, beta, states, tinvs, do, S, name):
    T = q.shape[0]
    rows = DN_SUB * CHUNK
    ns = S // rows

    def body(q_ref, k_ref, v_ref, g_ref, b_ref, st_ref, ti_ref, do_ref, dq, dk, dv, dg, db, ds_s):
        @pl.when(pl.program_id(1) == 0)
        def _():
            ds_s[...] = jnp.zeros_like(ds_s)

        knowns = [ti_ref[c] for c in range(DN_SUB)]
        _, vjp = jax.vjp(lambda *args: _dn_chunks(*args, knowns=knowns)[:2],
                         st_ref[0], q_ref[...], k_ref[...], v_ref[...], g_ref[...], b_ref[...])
        grads = vjp((do_ref[...], ds_s[...]))
        ds_s[...] = grads[0]
        for ref, val in zip((dq, dk, dv, dg, db), grads[1:]):
            ref[...] = val

    spec = pl.BlockSpec((rows, DN_W), lambda b, t: (b * ns + ns - 1 - t, 0))
    return pl.pallas_call(
        body, name=name, grid=(T // S, ns),
        in_specs=[spec] * 5 + [pl.BlockSpec((1, DN_W, DN_W), lambda b, t: (b * ns + ns - 1 - t, 0, 0)),
                               pl.BlockSpec((DN_SUB, DN_W, DN_W), lambda b, t: (b * ns + ns - 1 - t, 0, 0)),
                               spec],
        out_specs=[spec] * 5,
        out_shape=[jax.ShapeDtypeStruct((T, DN_W), F32)] * 5,
        scratch_shapes=[pltpu.VMEM((DN_W, DN_W), F32)],
        compiler_params=_cp("parallel", "arbitrary"),
    )(q, k, v, g, beta, states, tinvs, do)


def _dn_gate(o, z, nl):
    ms = _mm3(o * o, _head_mask((DN_W, DN_W))) * (1.0 / HD)
    return o * lax.rsqrt(ms + EPS) * nl * (z * _sigmoid(z))


def dn_gate_fwd(o, u_dn, nl, name):
    T = o.shape[0]
    tm = min(TOK_TILE, T)

    def body(o_ref, z_ref, n_ref, y_ref):
        y_ref[...] = _dn_gate(o_ref[...], z_ref[...], n_ref[...])

    return pl.pallas_call(
        body, name=name, grid=(T // tm,),
        in_specs=[_row_spec(tm, DN_W), pl.BlockSpec((tm, DN_W), lambda i: (i, 3)), _full_spec((1, DN_W))],
        out_specs=_row_spec(tm, DN_W),
        out_shape=jax.ShapeDtypeStruct((T, DN_W), F32),
        compiler_params=_cp("parallel"),
    )(o, u_dn, nl)


def dn_gate_bwd(o, u_dn, nl, dy, name):
    T = o.shape[0]
    tm = min(TOK_TILE, T)

    def body(o_ref, z_ref, n_ref, dy_ref, do_ref, dz_ref, dn_ref):
        @pl.when(pl.program_id(0) == 0)
        def _():
            dn_ref[...] = jnp.zeros_like(dn_ref)

        _, vjp = jax.vjp(_dn_gate, o_ref[...], z_ref[...], n_ref[...])
        do, dz, dn = vjp(dy_ref[...])
        do_ref[...] = do
        dz_ref[...] = dz
        fold = (_iota((LANE, DN_W), 0) == (_iota((LANE, DN_W), 1) & (HD - 1))).astype(F32)
        dn_ref[...] += _mmx_nt(jnp.concatenate([dn, jnp.zeros((7, DN_W), F32)], axis=0), fold)

    return pl.pallas_call(
        body, name=name, grid=(T // tm,),
        in_specs=[_row_spec(tm, DN_W), pl.BlockSpec((tm, DN_W), lambda i: (i, 3)), _full_spec((1, DN_W)),
                  _row_spec(tm, DN_W)],
        out_specs=[_row_spec(tm, DN_W), _row_spec(tm, DN_W), _full_spec((8, LANE))],
        out_shape=[jax.ShapeDtypeStruct((T, DN_W), F32), jax.ShapeDtypeStruct((T, DN_W), F32),
                   jax.ShapeDtypeStruct((8, LANE), F32)],
        compiler_params=_cp("arbitrary"),
    )(o, u_dn, nl, dy)


Y_SPLITS = (LRU_W, ATT_W, DN_W)
Y_OFFS = (0, LRU_W, LRU_W + ATT_W)


ROWS_DEV = D // N_DEV


def _dev_rows_spec(row0):
    return pl.BlockSpec((N_DEV, ROWS_DEV, D), lambda *_: (0, row0 // ROWS_DEV, 0))


def _dev_rows(w_ref, off, n):
    return w_ref[off // ROWS_DEV:(off + n) // ROWS_DEV].reshape(n, D)


def wout_fwd(h, ys, wb, name):
    T = h.shape[0]
    tm = min(TOK_TILE, T)

    def body(h_ref, y0, y1, y2, w_ref, o_ref, yc_ref):
        acc = h_ref[...]
        for ref, off, n in zip((y0, y1, y2), Y_OFFS, Y_SPLITS):
            y = ref[...].astype(BF16)
            yc_ref[:, off:off + n] = y
            acc += _mm(y, _dev_rows(w_ref, off, n))
        o_ref[...] = acc

    return pl.pallas_call(
        body, name=name, grid=(T // tm,),
        in_specs=[_row_spec(tm, D)] + [_row_spec(tm, n) for n in Y_SPLITS] + [_dev_rows_spec(R_WOUT)],
        out_specs=[_row_spec(tm, D), _row_spec(tm, D)],
        out_shape=[jax.ShapeDtypeStruct((T, D), F32), jax.ShapeDtypeStruct((T, D), BF16)],
        compiler_params=_cp("parallel"),
    )(h, *ys, wb)


def wout_bwd(dy, wb, name):
    T = dy.shape[0]
    tm = min(TOK_TILE, T)

    def body(dy_ref, w_ref, d0, d1, d2):
        dd = dy_ref[...].astype(BF16)
        for ref, off, n in zip((d0, d1, d2), Y_OFFS, Y_SPLITS):
            ref[...] = _mm_nt(dd, _dev_rows(w_ref, off, n))

    return pl.pallas_call(
        body, name=name, grid=(T // tm,),
        in_specs=[_row_spec(tm, D), _dev_rows_spec(R_WOUT)],
        out_specs=[_row_spec(tm, n) for n in Y_SPLITS],
        out_shape=[jax.ShapeDtypeStruct((T, n), F32) for n in Y_SPLITS],
        compiler_params=_cp("parallel"),
    )(dy, wb)


def ple_fwd(h, g, pe, wg, wp, name):
    T = h.shape[0]
    tm = min(TOK_TILE, T)

    def body(h_ref, g_ref, p_ref, wg_ref, wp_ref, o_ref):
        hh = h_ref[...]
        xn = _rms(hh, g_ref[...])[0]
        o_ref[...] = hh + _sigmoid(_mm(xn, _dev_rows(wg_ref, 0, D))) * _mm(p_ref[...], wp_ref[...])

    return pl.pallas_call(
        body, name=name, grid=(T // tm,),
        in_specs=[_row_spec(tm, D), _full_spec((1, D)), _row_spec(tm, PLE), _dev_rows_spec(R_PGATE),
                  _full_spec((PLE, D))],
        out_specs=_row_spec(tm, D),
        out_shape=jax.ShapeDtypeStruct((T, D), F32),
        compiler_params=_cp("parallel"),
    )(h, g, pe, wg, wp)


def ple_bwd(h, dy, g, pe, wg, wp, name):
    T = h.shape[0]
    tm = min(TOK_TILE, T)

    def body(h_ref, dy_ref, g_ref, p_ref, wg_ref, wp_ref, dh_ref, dz_ref, dpp_ref, xn_ref, dn_ref):
        @pl.when(pl.program_id(0) == 0)
        def _():
            dn_ref[...] = jnp.zeros_like(dn_ref)

        gg = g_ref[...]
        dy = dy_ref[...]
        xn, xhat, rstd = _rms(h_ref[...], gg)
        wg = _dev_rows(wg_ref, 0, D)
        gate = _sigmoid(_mm(xn, wg))
        pp = _mm(p_ref[...], wp_ref[...])
        dz = dy * pp * gate * (1.0 - gate)
        dz_ref[...] = dz.astype(BF16)
        dpp_ref[...] = (dy * gate).astype(BF16)
        xn_ref[...] = xn.astype(BF16)
        dh, dn = _rms_bwd(_mm_nt(dz, wg), xhat, rstd, gg)
        dh_ref[...] = dy + dh
        dn_ref[...] += dn

    return pl.pallas_call(
        body, name=name, grid=(T // tm,),
        in_specs=[_row_spec(tm, D), _row_spec(tm, D), _full_spec((1, D)), _row_spec(tm, PLE),
                  _dev_rows_spec(R_PGATE), _full_spec((PLE, D))],
        out_specs=[_row_spec(tm, D), _row_spec(tm, D), _row_spec(tm, D), _row_spec(tm, D), _full_spec((1, D))],
        out_shape=[jax.ShapeDtypeStruct((T, D), F32), jax.ShapeDtypeStruct((T, D), BF16),
                   jax.ShapeDtypeStruct((T, D), BF16), jax.ShapeDtypeStruct((T, D), BF16),
                   jax.ShapeDtypeStruct((1, D), F32)],
        compiler_params=_cp("arbitrary"),
    )(h, dy, g, pe, wg, wp)


def loss_head(h, g, target, name):
    T = h.shape[0]
    tm = min(TOK_TILE, T)

    def body(h_ref, g_ref, t_ref, loss_ref, dh_ref, dn_ref):
        @pl.when(pl.program_id(0) == 0)
        def _():
            dn_ref[...] = jnp.zeros_like(dn_ref)
            loss_ref[...] = jnp.zeros_like(loss_ref)

        gg = g_ref[...]
        y, xhat, rstd = _rms(h_ref[...], gg)
        err = y - t_ref[...]
        per_tok = jnp.mean(err * err, axis=-1, keepdims=True)
        loss_ref[...] += 0.5 * jnp.sum(per_tok, axis=0, keepdims=True)
        dh, dn = _rms_bwd(err * (1.0 / D), xhat, rstd, gg)
        dh_ref[...] = dh
        dn_ref[...] += dn

    return pl.pallas_call(
        body, name=name, grid=(T // tm,),
        in_specs=[_row_spec(tm, D), _full_spec((1, D)), _row_spec(tm, D)],
        out_specs=[_full_spec((8, LANE)), _row_spec(tm, D), _full_spec((1, D))],
        out_shape=[jax.ShapeDtypeStruct((8, LANE), F32), jax.ShapeDtypeStruct((T, D), F32),
                   jax.ShapeDtypeStruct((1, D), F32)],
        compiler_params=_cp("arbitrary"),
    )(h, g, target)


def _block_diag(w):
    return jnp.einsum('hij,hk->hikj', w, jnp.eye(4, dtype=w.dtype)).reshape(LRU_W, LRU_W)


def _layer_consts(W, l):
    row = lambda v: v.reshape(1, -1)
    zeros = jnp.zeros((5, LRU_W), F32)
    return dict(
        wa=_block_diag(W["lru_w_a"][l]), wx=_block_diag(W["lru_w_x"][l]),
        lru_vec=jnp.concatenate([row(W["lru_b_a"][l]), row(W["lru_b_x"][l]), row(W["lru_lambda"][l]), zeros], 0),
        lru_cb=row(W["lru_conv_b"][l]),
        sinks=W["attn_sinks"][l], rel=W["rel_bias"].reshape(-1),
        dn_cb=jnp.zeros((1, 3 * DN_W), F32),
        alog=row(jnp.repeat(W["dn_a_log"][l], HD)), dtb=row(jnp.repeat(W["dn_dt_bias"][l], HD)),
        dn_nl=row(jnp.tile(W["dn_norm"][l], DN_H)),
    )


def _layer_fwd(h0, pe, W, l, S, need=None):
    n = f"l{l}_"
    c_ = _layer_consts(W, l)
    row = lambda v: v.reshape(1, -1)
    need = need or (lambda *_: None)
    need(l, "f1", h0)
    h1, *ffn1_kept = ffn_fwd(h0, row(W["ffn1_norm"][l]), W["f1_cols"][l], W["f1_rows"][l], n + "ffn1_fwd")
    need(l, "in", h1)
    u_lru, u_att, u_dn, u_ba, xn_mix = mixin_fwd(h1, row(W["mix_norm"][l]), W["w_in"][l], n + "mixin_fwd")
    xr = conv_fwd(u_lru, W["lru_conv_w"][l], c_["lru_cb"], S, 0, LRU_W, n + "lru_conv_fwd")
    y_lru = lru_fwd(xr, u_lru, c_["wa"], c_["wx"], c_["lru_vec"], S, n + "lru_fwd")
    y_att = attn_fwd(u_att, c_["sinks"], c_["rel"], S, n + "attn_fwd")
    cc = conv_fwd(u_dn, W["dn_conv_w"][l], c_["dn_cb"], S, 0, 3 * DN_W, n + "dn_conv_fwd")
    q, k, v, g, beta = dn_point_fwd(cc, u_ba, c_["alog"], c_["dtb"], n + "dn_point_fwd")
    o, states, tinvs = dn_scan_fwd(q, k, v, g, beta, S, n + "dn_scan_fwd")
    y_dn = dn_gate_fwd(o, u_dn, c_["dn_nl"], n + "dn_gate_fwd")
    need(l, "rest", y_dn)
    h2, ycat = wout_fwd(h1, (y_lru, y_att, y_dn), W["r_rows"][l], n + "wout_fwd")
    h3, *ffn2_kept = ffn_fwd(h2, row(W["ffn2_norm"][l]), W["r_cols"][l], W["r_rows"][l], n + "ffn2_fwd")
    h4 = ple_fwd(h3, row(W["ple_norm"][l]), pe, W["r_rows"][l], W["ple_w_proj"][l], n + "ple_fwd")
    saved = dict(ffn1=ffn1_kept, ffn2=ffn2_kept, h0=h0, h1=h1, h2=h2, h3=h3, u_lru=u_lru, u_att=u_att, u_dn=u_dn,
                 u_ba=u_ba, xn_mix=xn_mix, xr=xr, cc=cc, q=q, k=k, v=v, g=g, beta=beta, o=o, states=states, tinvs=tinvs, ycat=ycat)
    return h4, saved


GM_WOUT, GM_PGATE, GM_WIN, GM_PPROJ, GM_END, GM_ROWS = 0, 128, 256, 560, 592, 640


def _layer_bwd(dh4, sv, pe, W, l, S, token=None, on_piece=None):
    n = f"l{l}_"
    c_ = _layer_consts(W, l)
    row = lambda v: v.reshape(1, -1)
    behind = lambda v, tok: v if tok is None else v + tok.astype(v.dtype)
    on_piece = on_piece or (lambda *_: None)
    G = {"mix_rows": jnp.zeros((N_DEV, GM_ROWS, D), BF16)}
    dh3, dz, dpp, xn_p, dn = ple_bwd(sv["h3"], dh4, behind(row(W["ple_norm"][l]), token), pe, W["r_rows"][l],
                                     W["ple_w_proj"][l], n + "ple_bwd")
    G["ple_norm"] = dn[0]
    G["mix_rows"] = grad_rows(xn_p, dz, G["mix_rows"], GM_PGATE, ROWS_DEV, n + "d_ple_w_gate")
    d_proj = matmul_tn(pe, dpp, n + "d_ple_w_proj")
    d_proj = d_proj.reshape(PLE, N_DEV, D // N_DEV).transpose(1, 0, 2).reshape(N_DEV, GM_END - GM_PPROJ, D)
    G["mix_rows"] = lax.dynamic_update_slice(G["mix_rows"], d_proj, (0, GM_PPROJ, 0))

    def ffn_back(which, cols_w, rows_w, h_in, dy, tok, one_by_one):
        gt, up, xn = sv[which]
        dh, dgt, dup, act, dn_ = ffn_bwd(h_in, dy, behind(row(W[which + "_norm"][l]), tok), gt, up, cols_w, rows_w,
                                         n + which + "_bwd")
        G[which + "_norm"] = dn_[0]
        zeros_rows = jnp.zeros((N_DEV, SHP, D), BF16)
        if one_by_one:
            G[which + "_gate"] = grad_cols(xn, dgt, lax.empty((1, D, FFP), BF16), 0, n + "d_" + which + "_w_gate")
            tok = on_piece(l, which + "_gate", (G[which + "_gate"],))
            G[which + "_up"] = grad_cols(xn, dup, behind(jnp.zeros((1, D, FFP), BF16), tok), 0,
                                         n + "d_" + which + "_w_up")
            tok = on_piece(l, which + "_up", (G[which + "_up"],))
            G[which + "_down"] = grad_rows(act, dy, behind(zeros_rows, tok), 0, SHP, n + "d_" + which + "_w_down",
                                           scale=0.5)
            return dh, on_piece(l, which + "_down", (G[which + "_down"],))
        cols = grad_cols(xn, dgt, lax.empty((2, D, FFP), BF16), 0, n + "d_" + which + "_w_gate")
        G[which + "_cols"] = grad_cols(xn, dup, cols, 1, n + "d_" + which + "_w_up")
        G[which + "_rows"] = grad_rows(act, dy, lax.empty((N_DEV, SHP, D), BF16), 0, SHP,
                                       n + "d_" + which + "_w_down", scale=0.5)
        return dh, on_piece(l, which, (G[which + "_cols"], G[which + "_rows"]))

    dh2, tok = ffn_back("ffn2", W["r_cols"][l], W["r_rows"][l], sv["h2"], dh3, None, False)
    dy_lru, dy_att, dy_dn = wout_bwd(dh2, W["r_rows"][l], n + "wout_bwd")
    G["mix_rows"] = grad_rows(sv["ycat"], dh2, G["mix_rows"], GM_WOUT, ROWS_DEV, n + "d_w_out")
    do, dz_dn, dnn = dn_gate_bwd(sv["o"], sv["u_dn"], behind(c_["dn_nl"], tok), dy_dn, n + "dn_gate_bwd")
    dqkvgb = dn_scan_bwd(sv["q"], sv["k"], sv["v"], sv["g"], sv["beta"], sv["states"], sv["tinvs"], do, S,
                         n + "dn_scan_bwd")
    dcc, du_ba, dvec_dn = dn_point_bwd(sv["cc"], sv["u_ba"], c_["alog"], c_["dtb"], dqkvgb, n + "dn_point_bwd")
    dqkv, dwb_dn = conv_bwd(sv["u_dn"], dcc, W["dn_conv_w"][l], S, 0, 3 * DN_W, n + "dn_conv_bwd")
    G["dn_norm"] = dnn[0, 0:HD]
    G["dn_a_log"] = dvec_dn[0, 0:DN_H]
    G["dn_dt_bias"] = dvec_dn[1, 0:DN_H]
    G["dn_conv_w"] = dwb_dn[0:4]
    du_att, drel, dsk = attn_bwd(sv["u_att"], dy_att, c_["sinks"], c_["rel"], S, n + "attn_bwd")
    G["attn_sinks"] = dsk[:, 0]
    G["rel_bias"] = drel[:, 0:REL_BUCKETS].T
    dxr, dgt_lru, dwa, dwx, dvec = lru_bwd(sv["xr"], sv["u_lru"], dy_lru, c_["wa"], c_["wx"], c_["lru_vec"], S,
                                           n + "lru_bwd")
    dx_lru, dwb_lru = conv_bwd(sv["u_lru"], dxr, W["lru_conv_w"][l], S, 0, LRU_W, n + "lru_conv_bwd")
    diag = lambda m: jnp.stack([m[c, HD * e:HD * (e + 1), HD * e:HD * (e + 1)] for c in range(2) for e in range(2)])
    G["lru_w_a"], G["lru_w_x"] = diag(dwa), diag(dwx)
    G["lru_b_a"], G["lru_b_x"], G["lru_lambda"] = dvec[0], dvec[1], dvec[2]
    G["lru_conv_w"], G["lru_conv_b"] = dwb_lru[0:4], dwb_lru[4]
    dh1, du_cat, dn = mixin_bwd(sv["h1"], dh2, row(W["mix_norm"][l]), W["w_in"][l],
                                (dx_lru, dgt_lru, du_att, dqkv, dz_dn, du_ba), n + "mixin_bwd")
    G["mix_norm"] = dn[0]
    d_in = matmul_tn(sv["xn_mix"], du_cat, n + "d_w_in")[:, :D_IN]
    d_in = d_in.reshape(D, N_DEV, D_IN // N_DEV).transpose(1, 0, 2).reshape(N_DEV, WIN_ROWS, D)
    d_in = jnp.pad(d_in, ((0, 0), (0, GM_PPROJ - GM_WIN - WIN_ROWS), (0, 0)))
    G["mix_rows"] = lax.dynamic_update_slice(G["mix_rows"], d_in, (0, GM_WIN, 0))
    tok = on_piece(l, "mix", (G["mix_rows"],))
    dh0, tok = ffn_back("ffn1", W["f1_cols"][l], W["f1_rows"][l], sv["h0"], dh1, tok, l == 0)
    return dh0, G, tok


def _core(x, pe, W, target, S, need=None, on_piece=None):
    h = x
    saved = []
    for l in range(DEPTH):
        h, sv = _layer_fwd(h, pe[l], W, l, S, need)
        saved.append(sv)
    loss_tile, dh, dfn = loss_head(h, W["final_norm"].reshape(1, -1), target, "loss_head")
    grads = [None] * DEPTH
    token = None
    for l in reversed(range(DEPTH)):
        dh, grads[l], token = _layer_bwd(dh, saved[l], pe[l], W, l, S, token, on_piece)
    return loss_tile[0, 0], dh, grads, dfn[0]


MESH_ID = pl.DeviceIdType.MESH
ANY_SPEC = pl.BlockSpec(memory_space=pl.ANY)
AXES = ("x", "y", "c")


def _my_pos():
    return lax.axis_index("x"), lax.axis_index("y"), lax.axis_index("c")


def _slot_of(px, py, pc):
    return 4 * px + 2 * py + pc


def all_gather(x, name):
    R, C = x.shape

    def body(x_ref, out_ref, send_sems, recv_sems, local_sem):
        mx, my, mc = _my_pos()
        me, sibling = (mx, my, mc), (mx, my, 1 - mc)
        chips = [(1 - mx, my), (mx, 1 - my), (1 - mx, 1 - my)]

        def copy(k, block, to, src=None):
            dst = out_ref.at[_slot_of(*block)]
            return pltpu.make_async_remote_copy(
                src_ref=dst if src is None else src, dst_ref=dst,
                send_sem=send_sems.at[k], recv_sem=recv_sems.at[k],
                device_id=to, device_id_type=MESH_ID)

        mine = pltpu.make_async_copy(x_ref, out_ref.at[_slot_of(*me)], local_sem)
        mine.start()
        first = [copy(0, me, sibling, src=x_ref)]
        first += [copy(1 + j, me, (*chip, mc), src=x_ref) for j, chip in enumerate(chips)]
        for cp in first:
            cp.start()
        passed = [copy(4 + j, (*chip, mc), sibling) for j, chip in enumerate(chips)]
        for j, chip in enumerate(chips):
            copy(1 + j, (*chip, mc), me).wait_recv()
            passed[j].start()
        copy(0, sibling, me).wait_recv()
        for j, chip in enumerate(chips):
            copy(4 + j, (*chip, 1 - mc), me).wait_recv()
        for cp in first + passed:
            cp.wait_send()
        mine.wait()

    return pl.pallas_call(
        body, name=name,
        out_shape=jax.ShapeDtypeStruct((N_DEV, R, C), x.dtype),
        in_specs=[ANY_SPEC], out_specs=ANY_SPEC,
        scratch_shapes=[pltpu.SemaphoreType.DMA((7,)), pltpu.SemaphoreType.DMA((7,)), pltpu.SemaphoreType.DMA],
    )(x)


def _col_window(ref, slot):
    return ref.at[:, pl.ds(pl.multiple_of(slot * SHP, LANE), SHP)]


def gather_layer(a_sh, b_sh, name):
    def body(a_ref, b_ref, ao_ref, bo_ref, send_sems, recv_sems, local_sems):
        mx, my, mc = _my_pos()
        me, sibling = (mx, my, mc), (mx, my, 1 - mc)
        chips = [(1 - mx, my), (mx, 1 - my), (1 - mx, 1 - my)]

        def copies(k, block, to, own=False):
            slot = _slot_of(*block)
            dsts = (_col_window(ao_ref, slot), bo_ref.at[slot])
            srcs = (a_ref, b_ref) if own else dsts
            return [pltpu.make_async_remote_copy(
                src_ref=s, dst_ref=d, send_sem=send_sems.at[2 * k + i], recv_sem=recv_sems.at[2 * k + i],
                device_id=to, device_id_type=MESH_ID) for i, (s, d) in enumerate(zip(srcs, dsts))]

        mine = [pltpu.make_async_copy(a_ref, _col_window(ao_ref, _slot_of(*me)), local_sems.at[0]),
                pltpu.make_async_copy(b_ref, bo_ref.at[_slot_of(*me)], local_sems.at[1])]
        for cp in mine:
            cp.start()
        first = copies(0, me, sibling, own=True)
        for j, chip in enumerate(chips):
            first += copies(1 + j, me, (*chip, mc), own=True)
        for cp in first:
            cp.start()
        passed = []
        for j, chip in enumerate(chips):
            for cp in copies(1 + j, (*chip, mc), me):
                cp.wait_recv()
            fwd = copies(4 + j, (*chip, mc), sibling)
            for cp in fwd:
                cp.start()
            passed += fwd
        for cp in copies(0, sibling, me):
            cp.wait_recv()
        for j, chip in enumerate(chips):
            for cp in copies(4 + j, (*chip, 1 - mc), me):
                cp.wait_recv()
        for cp in first + passed:
            cp.wait_send()
        for cp in mine:
            cp.wait()

    return pl.pallas_call(
        body, name=name,
        out_shape=[jax.ShapeDtypeStruct((a_sh.shape[0], FFP), a_sh.dtype),
                   jax.ShapeDtypeStruct((N_DEV,) + b_sh.shape, b_sh.dtype)],
        in_specs=[ANY_SPEC, ANY_SPEC], out_specs=[ANY_SPEC, ANY_SPEC],
        scratch_shapes=[pltpu.SemaphoreType.DMA((14,)), pltpu.SemaphoreType.DMA((14,)),
                        pltpu.SemaphoreType.DMA((2,))],
    )(a_sh, b_sh)


HBM_SPEC = pl.BlockSpec(memory_space=pltpu.HBM)
SEM_SPEC = pl.BlockSpec(memory_space=pltpu.SEMAPHORE)
SPLIT_EFFECT = pltpu.CompilerParams(has_side_effects=pltpu.SideEffectType.DATAFLOW_SIDE_EFFECTING)


def _split_ends(mode, src_ref, dst_ref, src_slot, dst_slot):
    cols = mode.endswith("cols")
    if mode.startswith("gather"):
        return src_ref, (_col_window(dst_ref, dst_slot) if cols else dst_ref.at[dst_slot])
    return (_col_window(src_ref, src_slot) if cols else src_ref.at[src_slot]), dst_ref.at[dst_slot]


def _split_peers():
    mx, my, mc = _my_pos()
    for r in range(1, N_DEV):
        peer = (1 - mx if r & 4 else mx, 1 - my if r & 2 else my, 1 - mc if r & 1 else mc)
        yield r - 1, peer, _slot_of(*peer)


def split_start(modes, srcs, dsts, name, after=()):
    n = len(modes)
    m = len(after)

    def body(*refs):
        send_sems, recv_sems, token = refs[2 * n + m], refs[2 * n + m + 1], refs[-1]
        mine = _slot_of(*_my_pos())
        for k, peer, ps in _split_peers():
            for i in range(n):
                src, dst = _split_ends(modes[i], refs[i], refs[n + i], ps, mine)
                pltpu.make_async_remote_copy(
                    src_ref=src, dst_ref=dst, send_sem=send_sems.at[n * k + i], recv_sem=recv_sems.at[n * k + i],
                    device_id=peer, device_id_type=MESH_ID).start()
        for i in range(n):
            src, dst = _split_ends(modes[i], refs[i], refs[n + i], mine, mine)
            pltpu.make_async_copy(src, dst, recv_sems.at[n * (N_DEV - 1) + i]).start()
        token[...] = jnp.zeros_like(token)

    bufs = tuple(srcs) + tuple(dsts)
    sems = pltpu.SemaphoreType.DMA((n * N_DEV,))
    res = pl.pallas_call(
        body, name=name,
        out_shape=(sems, sems) + tuple(pltpu.HBM(t.shape, t.dtype) for t in bufs)
        + (jax.ShapeDtypeStruct((8, LANE), F32),),
        in_specs=[HBM_SPEC] * (2 * n) + [ANY_SPEC] * m,
        out_specs=(SEM_SPEC, SEM_SPEC) + (HBM_SPEC,) * (2 * n) + (pl.BlockSpec(memory_space=pltpu.VMEM),),
        input_output_aliases={i: 2 + i for i in range(2 * n)},
        compiler_params=SPLIT_EFFECT,
    )(*(pltpu.with_memory_space_constraint(t, pltpu.HBM) for t in bufs), *after)
    return list(res[:-1]), res[-1]


def split_wait(modes, started, after, name):
    n = len(modes)
    after = tuple(after) if isinstance(after, (tuple, list)) else (after,)
    send_sems, recv_sems, bufs = started[0], started[1], started[2:]

    def body(*refs):
        send_sems, recv_sems = refs[2 * n], refs[2 * n + 1]
        mine = _slot_of(*_my_pos())
        for k, peer, ps in _split_peers():
            for i in range(n):
                sent = _split_ends(modes[i], refs[i], refs[n + i], ps, mine)[0]
                landed = _split_ends(modes[i], refs[i], refs[n + i], mine, ps)[1]
                cp = pltpu.make_async_remote_copy(
                    src_ref=sent, dst_ref=landed, send_sem=send_sems.at[n * k + i],
                    recv_sem=recv_sems.at[n * k + i], device_id=peer, device_id_type=MESH_ID)
                cp.wait_send()
                cp.wait_recv()
        for i in range(n):
            src, dst = _split_ends(modes[i], refs[i], refs[n + i], mine, mine)
            pltpu.make_async_copy(src, dst, recv_sems.at[n * (N_DEV - 1) + i]).wait()

    res = pl.pallas_call(
        body, name=name,
        out_shape=tuple(pltpu.HBM(t.shape, t.dtype) for t in bufs),
        in_specs=[HBM_SPEC] * (2 * n) + [SEM_SPEC, SEM_SPEC] + [ANY_SPEC] * len(after),
        out_specs=(HBM_SPEC,) * (2 * n),
        input_output_aliases={i: i for i in range(2 * n)},
        compiler_params=SPLIT_EFFECT,
    )(*bufs, send_sems, recv_sems, *after)
    return list(res[n:])


def sum_parts(parts, name):
    _, R, C = parts.shape
    tr = _pick(R, (512, 336, 272, 256, 128, 64, 32, 16, 8))

    def body(p_ref, o_ref):
        acc = p_ref[0].astype(F32)
        for k in range(1, N_DEV):
            acc += p_ref[k].astype(F32)
        o_ref[...] = acc

    return pl.pallas_call(
        body, name=name, grid=(R // tr,),
        in_specs=[pl.BlockSpec((N_DEV, tr, C), lambda i: (0, i, 0))],
        out_specs=pl.BlockSpec((tr, C), lambda i: (i, 0)),
        out_shape=jax.ShapeDtypeStruct((R, C), F32),
        compiler_params=_cp("parallel"),
    )(parts)


def adamw(g, w, m, v, name):
    lead, (R, C) = g.shape[:-2], g.shape[-2:]
    tr = _pick(R, (512, 352, 256, 128, 64, 32, 16, 8))
    c1 = 1.0 - ADAM_B1 ** ADAM_STEP
    c2 = 1.0 - ADAM_B2 ** ADAM_STEP

    def body(g_ref, w_ref, m_ref, v_ref, d_ref, nm_ref, nv_ref):
        gg = g_ref[...]
        mm = ADAM_B1 * m_ref[...] + (1.0 - ADAM_B1) * gg
        vv = ADAM_B2 * v_ref[...] + (1.0 - ADAM_B2) * (gg * gg)
        nm_ref[...] = mm
        nv_ref[...] = vv
        d_ref[...] = -ADAM_LR * ((mm / c1) / (jnp.sqrt(vv / c2) + ADAM_EPS) + ADAM_WD * w_ref[...])

    if lead:
        spec = pl.BlockSpec((1, tr, C), lambda l, i: (l, i, 0))
    else:
        spec = pl.BlockSpec((tr, C), lambda l, i: (i, 0))
    return pl.pallas_call(
        body, name=name, grid=(lead[0] if lead else 1, R // tr),
        in_specs=[spec] * 4, out_specs=[spec] * 3,
        out_shape=[jax.ShapeDtypeStruct(g.shape, F32)] * 3,
        compiler_params=_cp("parallel", "parallel"),
    )(g, w, m, v)


BIG = (("ffn1_w_gate", 1, D, FF), ("ffn1_w_up", 1, D, FF), ("ffn1_w_down", 0, FF, D),
       ("w_in", 1, D, D_IN), ("w_out", 0, D, D),
       ("ffn2_w_gate", 1, D, FF), ("ffn2_w_up", 1, D, FF), ("ffn2_w_down", 0, FF, D),
       ("ple_w_gate", 0, D, D), ("ple_w_proj", 1, PLE, D))
SMALL = (("ffn1_norm", (D,), None), ("mix_norm", (D,), None), ("lru_conv_w", (4, LRU_W), LRU_W // N_DEV),
         ("lru_conv_b", (LRU_W,), None), ("lru_w_a", (4, HD, HD), None), ("lru_b_a", (LRU_W,), None),
         ("lru_w_x", (4, HD, HD), None), ("lru_b_x", (LRU_W,), None), ("lru_lambda", (LRU_W,), None),
         ("attn_sinks", (ATT_H,), None), ("dn_conv_w", (4, 3 * DN_W), 3 * DN_W // N_DEV),
         ("dn_a_log", (DN_H,), None), ("dn_dt_bias", (DN_H,), None), ("dn_norm", (HD,), None),
         ("ffn2_norm", (D,), None), ("ple_norm", (D,), None))
SINGLE = (("rel_bias", (REL_BUCKETS, ATT_H)), ("final_norm", (D,)))


def _pack_rows(arrs, width, mult):
    flat = jnp.concatenate([a.reshape(-1) for a in arrs])
    rows = -(-flat.shape[0] // (width * mult)) * mult
    return jnp.pad(flat, (0, rows * width - flat.shape[0])).reshape(rows, width)


def _unpack_rows(packed, shapes):
    flat = packed.reshape(-1)
    out, off = [], 0
    for s in shapes:
        n = int(np.prod(s))
        out.append(flat[off:off + n].reshape(s))
        off += n
    return out


def _pad_rows(w, r):
    return jnp.pad(w, ((0, r - w.shape[0]), (0, 0)))


def _shard_ffn(a, l, which, more=()):
    cols = jnp.concatenate([a[which + "_w_gate"][l], a[which + "_w_up"][l]], axis=0)
    rows = jnp.concatenate([_pad_rows(a[which + "_w_down"][l], SHP)] + list(more), axis=0)
    return jnp.pad(cols, ((0, 0), (0, SHP - SH))).astype(BF16), rows.astype(BF16)


def _shards(a, l):
    w_in_rows = _pad_rows(a["w_in"][l].reshape(WIN_ROWS, D), IN_ROWS).astype(BF16)
    rest = _shard_ffn(a, l, "ffn2", (a["w_out"][l], a["ple_w_gate"][l], a["ple_w_proj"][l].reshape(-1, D)))
    return _shard_ffn(a, l, "ffn1"), (w_in_rows,), rest


def _full_w_in(in_rows):
    sh = in_rows[:, :WIN_ROWS, :].reshape(N_DEV, D, D_IN // N_DEV)
    return jnp.pad(sh.transpose(1, 0, 2).reshape(D, D_IN), ((0, 0), (0, D_IN_PAD - D_IN)))


def _full_ple_proj(r_rows):
    sh = r_rows[:, R_PPROJ:R_ROWS, :].reshape(N_DEV, PLE, D // N_DEV)
    return sh.transpose(1, 0, 2).reshape(PLE, D)


PIECE_NAMES = {"ffn1": ("ffn1_w_gate", "ffn1_w_up", "ffn1_w_down"), "ffn2": ("ffn2_w_gate", "ffn2_w_up", "ffn2_w_down"),
               "mix": ("w_out", "ple_w_gate", "w_in", "ple_w_proj")}


def _shard_grads(piece, summed):
    if piece == "mix":
        rows, = summed
        return {"w_out": rows[GM_WOUT:GM_WOUT + ROWS_DEV], "ple_w_gate": rows[GM_PGATE:GM_PGATE + ROWS_DEV],
                "w_in": rows[GM_WIN:GM_WIN + WIN_ROWS].reshape(D, D_IN // N_DEV),
                "ple_w_proj": rows[GM_PPROJ:GM_END].reshape(PLE, D // N_DEV)}
    if piece in ("ffn1_gate", "ffn1_up"):
        return {piece.replace("_", "_w_"): summed[0][:, :SH]}
    if piece == "ffn1_down":
        return {"ffn1_w_down": summed[0][:SH]}
    cols, rows = summed
    return {piece + "_w_gate": cols[:D, :SH], piece + "_w_up": cols[D:, :SH], piece + "_w_down": rows[:SH]}


def kernel(x, p, ffn1_norm, ffn1_w_gate, ffn1_w_up, ffn1_w_down, mix_norm, w_in, lru_conv_w, lru_conv_b, lru_w_a, lru_b_a, lru_w_x, lru_b_x, lru_lambda, attn_sinks, rel_bias, dn_conv_w, dn_a_log, dn_dt_bias, dn_norm, w_out, ffn2_norm, ffn2_w_gate, ffn2_w_up, ffn2_w_down, ple_norm, ple_w_gate, ple_w_proj, final_norm, loss_target, m_ffn1_norm, m_ffn1_w_gate, m_ffn1_w_up, m_ffn1_w_down, m_mix_norm, m_w_in, m_lru_conv_w, m_lru_conv_b, m_lru_w_a, m_lru_b_a, m_lru_w_x, m_lru_b_x, m_lru_lambda, m_attn_sinks, m_rel_bias, m_dn_conv_w, m_dn_a_log, m_dn_dt_bias, m_dn_norm, m_w_out, m_ffn2_norm, m_ffn2_w_gate, m_ffn2_w_up, m_ffn2_w_down, m_ple_norm, m_ple_w_gate, m_ple_w_proj, m_final_norm, v_ffn1_norm, v_ffn1_w_gate, v_ffn1_w_up, v_ffn1_w_down, v_mix_norm, v_w_in, v_lru_conv_w, v_lru_conv_b, v_lru_w_a, v_lru_b_a, v_lru_w_x, v_lru_b_x, v_lru_lambda, v_attn_sinks, v_rel_bias, v_dn_conv_w, v_dn_a_log, v_dn_dt_bias, v_dn_norm, v_w_out, v_ffn2_norm, v_ffn2_w_gate, v_ffn2_w_up, v_ffn2_w_down, v_ple_norm, v_ple_w_gate, v_ple_w_proj, v_final_norm):
    a = dict(locals())
    nb, S, _ = x.shape
    T = nb * S
    my_slot = _slot_of(*_my_pos())

    W = {k: [None] * DEPTH for k in ("f1_cols", "f1_rows", "w_in", "r_cols", "r_rows", "ple_w_proj")}
    GATHER, SCATTER = ("gather_cols", "gather_block"), ("scatter_cols", "scatter_block")
    GROUP_MODES = {"f1": GATHER, "in": GATHER[1:], "rest": GATHER}

    def set_group(l, group, bufs):
        if group == "f1":
            W["f1_cols"][l], W["f1_rows"][l] = bufs
        elif group == "in":
            W["w_in"][l] = _full_w_in(bufs[0])
        else:
            W["r_cols"][l], W["r_rows"][l] = bufs
            W["ple_w_proj"][l] = _full_ple_proj(bufs[1])

    def landing(mode, src):
        if mode == "gather_cols":
            return lax.empty((src.shape[0], FFP), src.dtype)
        if mode == "scatter_cols":
            return lax.empty((N_DEV, src.shape[0], SHP), src.dtype)
        return lax.empty((N_DEV,) + src.shape[mode == "scatter_block":], src.dtype)

    def start(modes, srcs, name, after=()):
        return split_start(modes, srcs, [landing(m, s) for m, s in zip(modes, srcs)], name, after)

    shards0, shards1 = _shards(a, 0), _shards(a, 1)
    set_group(0, "f1", gather_layer(*shards0[0], "gather_weights_l0_ffn1"))
    taps = all_gather(_pack_rows([lru_conv_w, dn_conv_w], LANE, 8), "gather_conv_taps")
    flat_taps = taps.reshape(N_DEV, -1)
    for name, first, tap in (("lru_conv_w", 0, lru_conv_w), ("dn_conv_w", lru_conv_w.size, dn_conv_w)):
        per_dev = flat_taps[:, first:first + tap.size].reshape((N_DEV,) + tap.shape)
        W[name] = jnp.moveaxis(per_dev, 0, -2).reshape(tap.shape[:-1] + (N_DEV * tap.shape[-1],))
    for name, _, cols in SMALL:
        if cols is None:
            W[name] = a[name]
    W["rel_bias"], W["final_norm"] = rel_bias, final_norm

    gathers, after = {}, (W["f1_rows"][0], taps)
    for l, group, srcs in ((0, "in", shards0[1]), (0, "rest", shards0[2]),
                           (1, "f1", shards1[0]), (1, "in", shards1[1]), (1, "rest", shards1[2])):
        gathers[l, group], token = start(GROUP_MODES[group], srcs, f"gather_start_l{l}_{group}", after)
        after = (token,)
    W["ffn1_norm"] = ffn1_norm + token[0, 0]
    flight, tokens = {}, {}

    def need(l, group, h):
        if (l, group) in gathers:
            set_group(l, group, split_wait(GROUP_MODES[group], gathers[l, group], h, f"gather_wait_l{l}_{group}"))

    def piece_modes(piece):
        return {"mix": SCATTER[1:], "ffn1_gate": SCATTER[:1], "ffn1_up": SCATTER[:1], "ffn1_down": SCATTER[1:]}.get(
            piece, SCATTER)

    def on_piece(l, piece, bufs):
        bufs = [b.reshape(-1, FFP) if b.shape[-1] == FFP else b for b in bufs]
        flight[l, piece], tokens[l, piece] = start(piece_modes(piece), bufs, f"exchange_start_l{l}_{piece}")
        return tokens[l, piece][0, 0]

    loss_local, dx, grads, d_final = _core(x.reshape(T, D), p.reshape(DEPTH, T, PLE), W,
                                           loss_target.reshape(T, D), S, need, on_piece)
    loss = lax.psum(loss_local, AXES)

    small_full = [jnp.stack([grads[l][name] for l in range(DEPTH)]) for name, _, _ in SMALL]
    small_full += [grads[0]["rel_bias"] + grads[1]["rel_bias"], d_final]
    small_flight, _ = start(("gather_block",), (_pack_rows(small_full, LANE, 8),), "gather_start_small_grads",
                            (tokens[0, "ffn1_down"],))

    out = {}

    shards = [{} for _ in range(DEPTH)]

    def land(l, piece, after):
        parts = split_wait(piece_modes(piece), flight[l, piece], after, f"exchange_wait_l{l}_{piece}")
        shards[l].update(_shard_grads(piece, [sum_parts(t, f"sum_grads_l{l}_{piece}_{i}")
                                              for i, t in enumerate(parts)]))

    def update(piece):
        for name in PIECE_NAMES[piece]:
            g = jnp.stack([shards[l][name] for l in range(DEPTH)])
            out[name] = (g,) + tuple(adamw(g, a[name], a["m_" + name], a["v_" + name], "adamw_" + name))

    for piece in ("ffn2", "mix", "ffn1"):
        land(1, piece, (dx, tokens[0, "ffn1_down"]))
    summed1 = tuple(shards[1][PIECE_NAMES[piece][0]] for piece in PIECE_NAMES)
    land(0, "ffn2", summed1)
    land(0, "mix", summed1)
    update("ffn2")
    update("mix")
    done_early = tuple(out[n][1] for n in PIECE_NAMES["ffn2"] + PIECE_NAMES["mix"])
    small_parts, = split_wait(("gather_block",), small_flight, done_early, "gather_wait_small_grads")
    small_sum = sum_parts(small_parts, "sum_small_grads")
    g_small = dict(zip([n for n, _, _ in SMALL] + [n for n, _ in SINGLE],
                       _unpack_rows(small_sum, [s.shape for s in small_full])))
    for name, _, cols in SMALL:
        if cols is not None:
            g_small[name] = lax.dynamic_slice_in_dim(g_small[name], my_slot * cols, cols, axis=2)

    for n in [n for n, _, _ in SMALL] + [n for n, _ in SINGLE]:
        shape = a[n].shape
        flat = lambda t: t.reshape((-1, shape[-1]) if len(shape) > 1 else (1, -1))
        res = adamw(flat(g_small[n]), flat(a[n]), flat(a["m_" + n]), flat(a["v_" + n]), "adamw_" + n)
        out[n] = (g_small[n].reshape(shape),) + tuple(r.reshape(shape) for r in res)

    for piece in ("ffn1_gate", "ffn1_up", "ffn1_down"):
        land(0, piece, (out["final_norm"][1],) + done_early)
    update("ffn1")

    order = ['ffn1_norm', 'ffn1_w_gate', 'ffn1_w_up', 'ffn1_w_down', 'mix_norm', 'w_in', 'lru_conv_w', 'lru_conv_b',
             'lru_w_a', 'lru_b_a', 'lru_w_x', 'lru_b_x', 'lru_lambda', 'attn_sinks', 'rel_bias', 'dn_conv_w',
             'dn_a_log', 'dn_dt_bias', 'dn_norm', 'w_out', 'ffn2_norm', 'ffn2_w_gate', 'ffn2_w_up', 'ffn2_w_down',
             'ple_norm', 'ple_w_gate', 'ple_w_proj', 'final_norm']
    return (loss, dx.reshape(x.shape)) + tuple(out[n][k] for k in range(4) for n in order)
```

```python
import functools
import math

import numpy as np
import jax
import jax.numpy as jnp
from jax import lax
from jax.experimental import pallas as pl
from jax.experimental.pallas import tpu as pltpu

F32 = jnp.float32
BF16 = jnp.bfloat16
HI = lax.Precision.HIGHEST

D = 1024
DEPTH = 2
EPS = 1e-6
PLE = 256
FF = 2816
HD = 64
LRU_W = 256
LRU_C = 8.0
ATT_W = 512
ATT_H = 8
ATT_KV = 2
ATT_G = 4
KV_W = 128
WINDOW = 128
BQ = 128
REL_BUCKETS = 32
REL_MAX_DIST = 128
DN_W = 256
DN_H = 4
CHUNK = 64
D_IN = 2312
D_IN_PAD = 2432
N_DEV = 8

ADAM_LR = 0.001
ADAM_B1 = 0.9
ADAM_B2 = 0.999
ADAM_EPS = 1e-08
ADAM_WD = 0.01
ADAM_STEP = 10

LANE = 128
VMEM_LIMIT = 56 * 1024 * 1024
SH = FF // N_DEV
SHP = 384
FFP = N_DEV * SHP
FF_TILE = 2 * SHP
FF_SUB = 256
TOK_TILE = 512
R_DOWN2, R_WOUT, R_PGATE, R_PPROJ, R_ROWS = 0, 384, 512, 640, 672
WIN_ROWS = D * D_IN // N_DEV // 1024
IN_ROWS = 304
NEG = -1e30


def _cp(*sem):
    return pltpu.CompilerParams(dimension_semantics=tuple(sem), vmem_limit_bytes=VMEM_LIMIT)


def _dg(a, b, ca, cb, exact):
    dims = (((ca,), (cb,)), ((), ()))
    if exact == "f32":
        return lax.dot_general(a.astype(F32), b.astype(F32), dims, precision=HI, preferred_element_type=F32)
    if exact == "split":
        a_hi, b_hi = a.astype(BF16), b.astype(BF16)
        a_lo = (a - a_hi.astype(F32)).astype(BF16)
        b_lo = (b - b_hi.astype(F32)).astype(BF16)
        dot = lambda u, v: lax.dot_general(u, v, dims, preferred_element_type=F32)
        return dot(a_hi, b_hi) + (dot(a_hi, b_lo) + dot(a_lo, b_hi))
    return lax.dot_general(a.astype(BF16), b.astype(BF16), dims, preferred_element_type=F32)


def _make_mm(exact):
    @jax.custom_vjp
    def mm(a, b):
        return _dg(a, b, 1, 0, exact)

    @jax.custom_vjp
    def mm_nt(a, b):
        return _dg(a, b, 1, 1, exact)

    @jax.custom_vjp
    def mm_tn(a, b):
        return _dg(a, b, 0, 0, exact)

    mm.defvjp(lambda a, b: (mm(a, b), (a, b)),
              lambda r, d: (mm_nt(d, r[1]), mm_tn(r[0], d)))
    mm_nt.defvjp(lambda a, b: (mm_nt(a, b), (a, b)),
                 lambda r, d: (mm(d, r[1]), mm_tn(d, r[0])))
    mm_tn.defvjp(lambda a, b: (mm_tn(a, b), (a, b)),
                 lambda r, d: (mm_nt(r[1], d), mm(r[0], d)))
    return mm, mm_nt, mm_tn


_mm, _mm_nt, _mm_tn = _make_mm("bf16")
_mmx, _mmx_nt, _mmx_tn = _make_mm("f32")
_mm3, _mm3_nt, _mm3_tn = _make_mm("split")


def _iota(shape, dim):
    return lax.broadcasted_iota(jnp.int32, shape, dim)


def _sigmoid(x):
    return 0.5 * jnp.tanh(0.5 * x) + 0.5


def _rms(h, g):
    rstd = lax.rsqrt(jnp.mean(h * h, axis=-1, keepdims=True) + EPS)
    xhat = h * rstd
    return xhat * g, xhat, rstd


def _rms_bwd(dxn, xhat, rstd, g):
    dxhat = dxn * g
    dh = rstd * (dxhat - xhat * jnp.mean(dxhat * xhat, axis=-1, keepdims=True))
    dg = jnp.sum(dxn * xhat, axis=0, keepdims=True)
    return dh, dg


def _row_spec(tm, n):
    return pl.BlockSpec((tm, n), lambda i, *_: (i, 0))


def _full_spec(shape):
    nd = len(shape)
    return pl.BlockSpec(shape, lambda *_: (0,) * nd)


def _ffn_weight_specs():
    return [pl.BlockSpec((D, FF_TILE), lambda i, j: (0, j)),
            pl.BlockSpec((D, FF_TILE), lambda i, j: (1, j)),
            pl.BlockSpec((2, SHP, D), lambda i, j: (j, 0, 0))]


def ffn_fwd(h, g, wa, wb, name):
    T = h.shape[0]
    tm = min(TOK_TILE, T)
    nj = FFP // FF_TILE

    def body(h_ref, g_ref, wg_ref, wu_ref, wd_ref, o_ref, gt_ref, up_ref, xn_ref):
        j = pl.program_id(1)

        @pl.when(j == 0)
        def _():
            hh = h_ref[...]
            xn_ref[...] = _rms(hh, g_ref[...])[0].astype(BF16)
            o_ref[...] = hh

        blocks = [slice(c, c + FF_SUB) for c in range(0, FF_TILE, FF_SUB)]
        xn = xn_ref[...]
        wd = wd_ref[...].reshape(FF_TILE, D)
        gt = [_mm(xn, wg_ref[:, c]) for c in blocks]
        up = [_mm(xn, wu_ref[:, c]) for c in blocks]
        act = [t * _sigmoid(t) * u for t, u in zip(gt, up)]
        down = [_mm(act[k], wd[c]) for k, c in enumerate(blocks)]
        for k, c in enumerate(blocks):
            gt_ref[:, c] = gt[k].astype(BF16)
            up_ref[:, c] = up[k].astype(BF16)
        o_ref[...] += 0.5 * functools.reduce(lambda x, y: x + y, down)

    tile = pl.BlockSpec((tm, FF_TILE), lambda i, j: (i, j))
    return pl.pallas_call(
        body, name=name, grid=(T // tm, nj),
        in_specs=[pl.BlockSpec((tm, D), lambda i, j: (i, 0)),
                  pl.BlockSpec((1, D), lambda i, j: (0, 0))] + _ffn_weight_specs(),
        out_specs=[pl.BlockSpec((tm, D), lambda i, j: (i, 0)), tile, tile,
                   pl.BlockSpec((tm, D), lambda i, j: (i, 0))],
        out_shape=[jax.ShapeDtypeStruct((T, D), F32), jax.ShapeDtypeStruct((T, FFP), BF16),
                   jax.ShapeDtypeStruct((T, FFP), BF16), jax.ShapeDtypeStruct((T, D), BF16)],
        compiler_params=_cp("parallel", "arbitrary"),
    )(h, g, wa, wa, wb)


def ffn_bwd(h, dy, g, gt_saved, up_saved, wa, wb, name):
    T = h.shape[0]
    tm = min(TOK_TILE, T)
    nj = FFP // FF_TILE

    def body(h_ref, dy_ref, g_ref, gt_ref, up_ref, wg_ref, wu_ref, wd_ref,
             dh_ref, dg_ref, du_ref, a_ref, dn_ref, dxn_s, dyh_s):
        i = pl.program_id(0)
        j = pl.program_id(1)

        @pl.when(j == 0)
        def _():
            dxn_s[...] = jnp.zeros_like(dxn_s)
            dyh_s[...] = (0.5 * dy_ref[...]).astype(BF16)

        @pl.when((i == 0) & (j == 0))
        def _():
            dn_ref[...] = jnp.zeros_like(dn_ref)

        blocks = [slice(c, c + FF_SUB) for c in range(0, FF_TILE, FF_SUB)]
        wd = wd_ref[...].reshape(FF_TILE, D)
        dyh = dyh_s[...]
        gt = [gt_ref[:, c].astype(F32) for c in blocks]
        up = [up_ref[:, c].astype(F32) for c in blocks]
        da = [_mm_nt(dyh, wd[c]) for c in blocks]
        sg = [_sigmoid(t) for t in gt]
        si = [t * s for t, s in zip(gt, sg)]
        dup = [d * s for d, s in zip(da, si)]
        dgt = [d * u * (s * (1.0 + t * (1.0 - s))) for d, u, s, t in zip(da, up, sg, gt)]
        dxn = [_mm_nt(dgt[k], wg_ref[:, c]) + _mm_nt(dup[k], wu_ref[:, c]) for k, c in enumerate(blocks)]
        for k, c in enumerate(blocks):
            dg_ref[:, c] = dgt[k].astype(BF16)
            du_ref[:, c] = dup[k].astype(BF16)
            a_ref[:, c] = (si[k] * up[k]).astype(BF16)
        dxn_s[...] += functools.reduce(lambda x, y: x + y, dxn)

        @pl.when(j == nj - 1)
        def _():
            gg = g_ref[...]
            _, xhat, rstd = _rms(h_ref[...], gg)
            dh, dn = _rms_bwd(dxn_s[...], xhat, rstd, gg)
            dh_ref[...] = dy_ref[...] + dh
            dn_ref[...] += dn

    tile = pl.BlockSpec((tm, FF_TILE), lambda i, j: (i, j))
    return pl.pallas_call(
        body, name=name, grid=(T // tm, nj),
        in_specs=[pl.BlockSpec((tm, D), lambda i, j: (i, 0)),
                  pl.BlockSpec((tm, D), lambda i, j: (i, 0)),
                  pl.BlockSpec((1, D), lambda i, j: (0, 0)), tile, tile] + _ffn_weight_specs(),
        out_specs=[pl.BlockSpec((tm, D), lambda i, j: (i, 0)), tile, tile, tile,
                   pl.BlockSpec((1, D), lambda i, j: (0, 0))],
        out_shape=[jax.ShapeDtypeStruct((T, D), F32)] + [jax.ShapeDtypeStruct((T, FFP), BF16)] * 3
        + [jax.ShapeDtypeStruct((1, D), F32)],
        scratch_shapes=[pltpu.VMEM((tm, D), F32), pltpu.VMEM((tm, D), BF16)],
        compiler_params=_cp("arbitrary", "arbitrary"),
    )(h, dy, g, gt_saved, up_saved, wa, wa, wb)


def _pick(n, prefs):
    for t in prefs:
        if n % t == 0:
            return t
    return n


def _tn_body(nk, scale, out_dtype, squeeze):
    def body(a_ref, b_ref, *rest):
        o_ref, acc = rest[-2], rest[-1]
        k = pl.program_id(2)

        @pl.when(k == 0)
        def _():
            acc[...] = jnp.zeros_like(acc)

        acc[...] += _mm_tn(a_ref[...], b_ref[...])

        @pl.when(k == nk - 1)
        def _():
            res = (scale * acc[...]).astype(out_dtype)
            if squeeze:
                o_ref[0] = res
            else:
                o_ref[...] = res

    return body


def matmul_tn(a, b, name, scale=1.0, out_dtype=BF16):
    T, M = a.shape
    N = b.shape[1]
    tmm = _pick(M, (512, 256))
    tnn = _pick(N, (1024, 2432))
    tk = min(TOK_TILE, T)
    nk = T // tk
    return pl.pallas_call(
        _tn_body(nk, scale, out_dtype, False), name=name, grid=(M // tmm, N // tnn, nk),
        in_specs=[pl.BlockSpec((tk, tmm), lambda i, j, k: (k, i)),
                  pl.BlockSpec((tk, tnn), lambda i, j, k: (k, j))],
        out_specs=pl.BlockSpec((tmm, tnn), lambda i, j, k: (i, j)),
        out_shape=jax.ShapeDtypeStruct((M, N), out_dtype),
        scratch_shapes=[pltpu.VMEM((tmm, tnn), F32)],
        compiler_params=_cp("parallel", "parallel", "arbitrary"),
    )(a, b)


def grad_cols(a, b, dst, slot, name):
    T = a.shape[0]
    tmm, tnn = D, FFP // 2
    tk = min(TOK_TILE, T)
    nk = T // tk
    return pl.pallas_call(
        _tn_body(nk, 1.0, BF16, True), name=name, grid=(D // tmm, FFP // tnn, nk),
        in_specs=[pl.BlockSpec((tk, tmm), lambda i, j, k: (k, i)),
                  pl.BlockSpec((tk, tnn), lambda i, j, k: (k, j)),
                  pl.BlockSpec(memory_space=pl.ANY)],
        out_specs=pl.BlockSpec((1, tmm, tnn), lambda i, j, k: (slot, i, j)),
        out_shape=jax.ShapeDtypeStruct(dst.shape, dst.dtype),
        scratch_shapes=[pltpu.VMEM((tmm, tnn), F32)],
        input_output_aliases={2: 0},
        compiler_params=_cp("parallel", "parallel", "arbitrary"),
    )(a, b, dst)


def grad_rows(a, b, dst, row0, rows, name, scale=1.0):
    T = a.shape[0]
    tk = min(TOK_TILE, T)
    nk = T // tk
    blk = row0 // rows

    def body(a_ref, b_ref, dst_ref, o_ref, acc):
        k = pl.program_id(0)

        @pl.when(k == 0)
        def _():
            acc[...] = jnp.zeros_like(acc)

        acc[...] += _mm_tn(a_ref[...], b_ref[...])

        @pl.when(k == nk - 1)
        def _():
            o_ref[...] = (scale * acc[...]).astype(BF16).reshape(N_DEV, rows, D)

    return pl.pallas_call(
        body, name=name, grid=(nk,),
        in_specs=[pl.BlockSpec((tk, N_DEV * rows), lambda k: (k, 0)),
                  pl.BlockSpec((tk, D), lambda k: (k, 0)),
                  pl.BlockSpec(memory_space=pl.ANY)],
        out_specs=pl.BlockSpec((N_DEV, rows, D), lambda k: (0, blk, 0)),
        out_shape=jax.ShapeDtypeStruct(dst.shape, dst.dtype),
        scratch_shapes=[pltpu.VMEM((N_DEV * rows, D), F32)],
        input_output_aliases={2: 0},
        compiler_params=_cp("arbitrary"),
    )(a, b, dst)


U_SPLITS = (512, 768, 1024, 128)
U_OFFS = (0, 512, 1280, 2304)


def mixin_fwd(h, g, w_in, name):
    T = h.shape[0]
    tm = min(TOK_TILE, T)

    def body(h_ref, g_ref, w_ref, u0, u1, u2, u3, xn_ref):
        xn = _rms(h_ref[...], g_ref[...])[0].astype(BF16)
        xn_ref[...] = xn
        u = _mm(xn, w_ref[...])
        for ref, off, n in zip((u0, u1, u2, u3), U_OFFS, U_SPLITS):
            ref[...] = u[:, off:off + n]

    return pl.pallas_call(
        body, name=name, grid=(T // tm,),
        in_specs=[_row_spec(tm, D), _full_spec((1, D)), _full_spec((D, D_IN_PAD))],
        out_specs=[_row_spec(tm, n) for n in U_SPLITS] + [_row_spec(tm, D)],
        out_shape=[jax.ShapeDtypeStruct((T, n), F32) for n in U_SPLITS]
        + [jax.ShapeDtypeStruct((T, D), BF16)],
        compiler_params=_cp("parallel"),
    )(h, g, w_in)


DU_SPLITS = (256, 256, 768, 768, 256, 128)
DU_OFFS = (0, 256, 512, 1280, 2048, 2304)


def mixin_bwd(h, dh_in, g, w_in, dus, name):
    T = h.shape[0]
    tm = min(TOK_TILE, T)

    def body(h_ref, dhi_ref, g_ref, w_ref, *refs):
        dh_ref, du_ref, dn_ref = refs[-3:]

        @pl.when(pl.program_id(0) == 0)
        def _():
            dn_ref[...] = jnp.zeros_like(dn_ref)

        dxn = jnp.zeros((tm, D), F32)
        for ref, off, n in zip(refs[:-3], DU_OFFS, DU_SPLITS):
            du = ref[...]
            du_ref[:, off:off + n] = du.astype(BF16)
            dxn += _mm_nt(du, w_ref[:, off:off + n])
        gg = g_ref[...]
        _, xhat, rstd = _rms(h_ref[...], gg)
        dh, dn = _rms_bwd(dxn, xhat, rstd, gg)
        dh_ref[...] = dhi_ref[...] + dh
        dn_ref[...] += dn

    return pl.pallas_call(
        body, name=name, grid=(T // tm,),
        in_specs=[_row_spec(tm, D), _row_spec(tm, D), _full_spec((1, D)), _full_spec((D, D_IN_PAD))]
        + [_row_spec(tm, n) for n in DU_SPLITS],
        out_specs=[_row_spec(tm, D), _row_spec(tm, D_IN_PAD), _full_spec((1, D))],
        out_shape=[jax.ShapeDtypeStruct((T, D), F32), jax.ShapeDtypeStruct((T, D_IN_PAD), BF16),
                   jax.ShapeDtypeStruct((1, D), F32)],
        compiler_params=_cp("arbitrary"),
    )(h, dh_in, g, w_in, *dus)


def _shift_down(x, s, row):
    if s == 0:
        return x
    return jnp.where(row >= s, pltpu.roll(x, s, 0), 0.0)


def _shift_up(x, s, row):
    if s == 0:
        return x
    n = x.shape[0]
    return jnp.where(row < n - s, pltpu.roll(x, n - s, 0), 0.0)


def conv_fwd(x, w, b, S, col0, C, name):
    T = x.shape[0]
    cb0 = col0 // LANE

    def body(x_ref, w_ref, b_ref, y_ref):
        xx = x_ref[...]
        row = _iota(xx.shape, 0)
        y = xx * w_ref[3:4, :] + b_ref[...]
        for k in range(3):
            y += _shift_down(xx, 3 - k, row) * w_ref[k:k + 1, :]
        y_ref[...] = y

    return pl.pallas_call(
        body, name=name, grid=(T // S, C // LANE),
        in_specs=[pl.BlockSpec((S, LANE), lambda s, c: (s, cb0 + c)),
                  pl.BlockSpec((4, LANE), lambda s, c: (0, c)),
                  pl.BlockSpec((1, LANE), lambda s, c: (0, c))],
        out_specs=pl.BlockSpec((S, LANE), lambda s, c: (s, c)),
        out_shape=jax.ShapeDtypeStruct((T, C), F32),
        compiler_params=_cp("parallel", "parallel"),
    )(x, w, b)


def conv_bwd(x, dy, w, S, col0, C, name):
    T = x.shape[0]
    cb0 = col0 // LANE

    def body(x_ref, dy_ref, w_ref, dx_ref, dwb_ref):
        @pl.when(pl.program_id(1) == 0)
        def _():
            dwb_ref[...] = jnp.zeros_like(dwb_ref)

        xx = x_ref[...]
        dd = dy_ref[...]
        row = _iota(xx.shape, 0)
        dx = dd * w_ref[3:4, :]
        for k in range(3):
            dx += _shift_up(dd, 3 - k, row) * w_ref[k:k + 1, :]
        dx_ref[...] = dx
        for k in range(4):
            dwb_ref[k:k + 1, :] += jnp.sum(dd * _shift_down(xx, 3 - k, row), axis=0, keepdims=True)
        dwb_ref[4:5, :] += jnp.sum(dd, axis=0, keepdims=True)

    return pl.pallas_call(
        body, name=name, grid=(C // LANE, T // S),
        in_specs=[pl.BlockSpec((S, LANE), lambda c, s: (s, cb0 + c)),
                  pl.BlockSpec((S, LANE), lambda c, s: (s, c)),
                  pl.BlockSpec((4, LANE), lambda c, s: (0, c))],
        out_specs=[pl.BlockSpec((S, LANE), lambda c, s: (s, c)),
                   pl.BlockSpec((8, LANE), lambda c, s: (0, c))],
        out_shape=[jax.ShapeDtypeStruct((T, C), F32), jax.ShapeDtypeStruct((8, C), F32)],
        compiler_params=_cp("parallel", "arbitrary"),
    )(x, dy, w)


def _scan(a, b, row):
    n = a.shape[0]
    d = 1
    while d < n:
        keep = row >= d
        b = a * jnp.where(keep, pltpu.roll(b, d, 0), 0.0) + b
        a = a * jnp.where(keep, pltpu.roll(a, d, 0), 1.0)
        d *= 2
    return b


def _rscan(a, b, row):
    n = a.shape[0]
    d = 1
    while d < n:
        keep = row < n - d
        b = a * jnp.where(keep, pltpu.roll(b, n - d, 0), 0.0) + b
        a = a * jnp.where(keep, pltpu.roll(a, n - d, 0), 1.0)
        d *= 2
    return b


GELU_C = math.sqrt(2.0 / math.pi)


def _gelu(x):
    t = jnp.tanh(GELU_C * (x + 0.044715 * (x * x * x)))
    return 0.5 * x * (1.0 + t), t


def _lru_gates(xr, wa, ba, wx, bx, lam):
    r = _sigmoid(_mm(xr, wa) + ba)
    i = _sigmoid(_mm(xr, wx) + bx)
    sp = jnp.maximum(-lam, 0.0) + jnp.log(1.0 + jnp.exp(-jnp.abs(lam)))
    la = -LRU_C * r * sp
    a = jnp.exp(la)
    e2 = a * a
    m = jnp.sqrt(-jnp.tanh(la) * (e2 + 1.0))
    return r, i, sp, a, e2, m


def lru_fwd(xr, u_lru, wa, wx, vec, S, name):
    T = xr.shape[0]

    def body(xr_ref, gt_ref, wa_ref, wx_ref, vec_ref, y_ref):
        x = xr_ref[...]
        row = _iota(x.shape, 0)
        r, i, sp, a, e2, m = _lru_gates(x, wa_ref[...], vec_ref[0:1, :], wx_ref[...], vec_ref[1:2, :],
                                        vec_ref[2:3, :])
        hh = _scan(a, m * (i * x), row)
        y_ref[...] = _gelu(gt_ref[...])[0] * hh

    return pl.pallas_call(
        body, name=name, grid=(T // S, LRU_W // LANE),
        in_specs=[pl.BlockSpec((S, LANE), lambda s, c: (s, c)),
                  pl.BlockSpec((S, LANE), lambda s, c: (s, 2 + c)),
                  pl.BlockSpec((LANE, LANE), lambda s, c: (c, c)),
                  pl.BlockSpec((LANE, LANE), lambda s, c: (c, c)),
                  pl.BlockSpec((8, LANE), lambda s, c: (0, c))],
        out_specs=pl.BlockSpec((S, LANE), lambda s, c: (s, c)),
        out_shape=jax.ShapeDtypeStruct((T, LRU_W), F32),
        compiler_params=_cp("parallel", "parallel"),
    )(xr, u_lru, wa, wx, vec)


def lru_bwd(xr, u_lru, dy, wa, wx, vec, S, name):
    T = xr.shape[0]

    def body(xr_ref, gt_ref, dy_ref, wa_ref, wx_ref, vec_ref,
             dxr_ref, dgt_ref, dwa_ref, dwx_ref, dvec_ref):
        @pl.when(pl.program_id(1) == 0)
        def _():
            dwa_ref[...] = jnp.zeros_like(dwa_ref)
            dwx_ref[...] = jnp.zeros_like(dwx_ref)
            dvec_ref[...] = jnp.zeros_like(dvec_ref)

        x = xr_ref[...]
        n = x.shape[0]
        row = _iota(x.shape, 0)
        lam = vec_ref[2:3, :]
        r, i, sp, a, e2, m = _lru_gates(x, wa_ref[...], vec_ref[0:1, :], wx_ref[...], vec_ref[1:2, :], lam)
        v = i * x
        hh = _scan(a, m * v, row)
        gt = gt_ref[...]
        dy = dy_ref[...]
        ge, t = _gelu(gt)
        dgt_ref[...] = dy * hh * (0.5 * (1.0 + t) + 0.5 * gt * (1.0 - t * t) * GELU_C
                                  * (1.0 + 3.0 * 0.044715 * gt * gt))
        a_next = jnp.where(row < n - 1, pltpu.roll(a, n - 1, 0), 0.0)
        G = _rscan(a_next, dy * ge, row)
        da = G * _shift_down(hh, 1, row)
        dv = G * m
        dla = da * a - (G * v) * e2 / m
        dr = dla * (-LRU_C * sp)
        dsp = jnp.sum(dla * (-LRU_C * r), axis=0, keepdims=True)
        dra = dr * r * (1.0 - r)
        dia = (dv * x) * i * (1.0 - i)
        dxr_ref[...] = dv * i + _mm_nt(dra, wa_ref[...]) + _mm_nt(dia, wx_ref[...])
        dwa_ref[0] += _mm_tn(x, dra)
        dwx_ref[0] += _mm_tn(x, dia)
        dvec_ref[0:1, :] += jnp.sum(dra, axis=0, keepdims=True)
        dvec_ref[1:2, :] += jnp.sum(dia, axis=0, keepdims=True)
        dvec_ref[2:3, :] += dsp * (-_sigmoid(-lam))

    return pl.pallas_call(
        body, name=name, grid=(LRU_W // LANE, T // S),
        in_specs=[pl.BlockSpec((S, LANE), lambda c, s: (s, c)),
                  pl.BlockSpec((S, LANE), lambda c, s: (s, 2 + c)),
                  pl.BlockSpec((S, LANE), lambda c, s: (s, c)),
                  pl.BlockSpec((LANE, LANE), lambda c, s: (c, c)),
                  pl.BlockSpec((LANE, LANE), lambda c, s: (c, c)),
                  pl.BlockSpec((8, LANE), lambda c, s: (0, c))],
        out_specs=[pl.BlockSpec((S, LANE), lambda c, s: (s, c)),
                   pl.BlockSpec((S, LANE), lambda c, s: (s, c)),
                   pl.BlockSpec((1, LANE, LANE), lambda c, s: (c, 0, 0)),
                   pl.BlockSpec((1, LANE, LANE), lambda c, s: (c, 0, 0)),
                   pl.BlockSpec((8, LANE), lambda c, s: (0, c))],
        out_shape=[jax.ShapeDtypeStruct((T, LRU_W), F32), jax.ShapeDtypeStruct((T, LRU_W), F32),
                   jax.ShapeDtypeStruct((2, LANE, LANE), F32), jax.ShapeDtypeStruct((2, LANE, LANE), F32),
                   jax.ShapeDtypeStruct((8, LRU_W), F32)],
        compiler_params=_cp("parallel", "arbitrary"),
    )(xr, u_lru, dy, wa, wx, vec)


def _bucket_table():
    qi = np.arange(BQ)[:, None]
    kj = np.arange(2 * BQ)[None, :]
    dist = BQ + qi - kj
    band = (dist >= 0) & (dist < WINDOW)
    dd = np.maximum(dist, 0)
    max_exact = REL_BUCKETS // 2
    large = max_exact + (np.log(np.maximum(dd, 1).astype(np.float32) / np.float32(max_exact))
                         / np.float32(math.log(REL_MAX_DIST / max_exact))
                         * np.float32(REL_BUCKETS - max_exact)).astype(np.int32)
    large = np.minimum(large, REL_BUCKETS - 1)
    bucket = np.where(dd < max_exact, dd, large)
    return np.where(band, bucket, -1).astype(np.int32)


def _att_specs(S):
    nb = S // BQ
    qc = ATT_W // LANE
    return [pl.BlockSpec((BQ, ATT_W), lambda b, n: (b * nb + n, 0)),
            pl.BlockSpec((BQ, KV_W), lambda b, n: (b * nb + jnp.maximum(n - 1, 0), qc)),
            pl.BlockSpec((BQ, KV_W), lambda b, n: (b * nb + n, qc)),
            pl.BlockSpec((BQ, KV_W), lambda b, n: (b * nb + jnp.maximum(n - 1, 0), qc + 1)),
            pl.BlockSpec((BQ, KV_W), lambda b, n: (b * nb + n, qc + 1))]


def _att_bias(bk, rb_ref, bias_s):
    for h in range(ATT_H):
        acc = jnp.zeros(bk.shape, F32)
        for bb in range(REL_BUCKETS):
            acc = jnp.where(bk == bb, rb_ref[bb * ATT_H + h], acc)
        bias_s[h] = acc


def _att_probs(qs, kgs, bias_s, valid, sk_ref):
    heads = range(ATT_H)
    s = [_mm_nt(qs[h], kgs[h // ATT_G]) for h in heads]
    s = [jnp.where(valid, s[h] * (HD ** -0.5) + bias_s[h], NEG) for h in heads]
    m = [jnp.maximum(jnp.max(s[h], axis=-1, keepdims=True), sk_ref[h]) for h in heads]
    e = [jnp.exp(s[h] - m[h]) for h in heads]
    es = [jnp.exp(sk_ref[h] - m[h]) for h in heads]
    den = [jnp.sum(e[h], axis=-1, keepdims=True) + es[h] for h in heads]
    return [e[h] / den[h] for h in heads], [es[h] / den[h] for h in heads]


def _att_kv(kp_ref, kc_ref, vp_ref, vc_ref):
    cat = lambda a, b, g: jnp.concatenate([a[:, HD * g:HD * (g + 1)], b[:, HD * g:HD * (g + 1)]], axis=0)
    return ([cat(kp_ref, kc_ref, g) for g in range(ATT_KV)], [cat(vp_ref, vc_ref, g) for g in range(ATT_KV)])


def attn_fwd(u_att, sinks, rel_bias, S, name):
    T = u_att.shape[0]
    nb = S // BQ
    table = jnp.asarray(_bucket_table())

    def body(sk_ref, rb_ref, bk_ref, q_ref, kp_ref, kc_ref, vp_ref, vc_ref, o_ref, bias_s):
        b = pl.program_id(0)
        n = pl.program_id(1)
        bk = bk_ref[...]

        @pl.when((b == 0) & (n == 0))
        def _():
            _att_bias(bk, rb_ref, bias_s)

        valid = (bk >= 0) & ((n > 0) | (_iota(bk.shape, 1) >= BQ))
        kgs, vgs = _att_kv(kp_ref, kc_ref, vp_ref, vc_ref)
        p, _ = _att_probs([q_ref[:, HD * h:HD * (h + 1)] for h in range(ATT_H)], kgs, bias_s, valid, sk_ref)
        outs = [_mm(p[h], vgs[h // ATT_G]) for h in range(ATT_H)]
        for h in range(ATT_H):
            o_ref[:, HD * h:HD * (h + 1)] = outs[h]

    smem = pl.BlockSpec(memory_space=pltpu.SMEM)
    return pl.pallas_call(
        body, name=name, grid=(T // S, nb),
        in_specs=[smem, smem, _full_spec((BQ, 2 * BQ))] + _att_specs(S),
        out_specs=pl.BlockSpec((BQ, ATT_W), lambda b, n: (b * nb + n, 0)),
        out_shape=jax.ShapeDtypeStruct((T, ATT_W), F32),
        scratch_shapes=[pltpu.VMEM((ATT_H, BQ, 2 * BQ), F32)],
        compiler_params=_cp("arbitrary", "arbitrary"),
    )(sinks, rel_bias, table, u_att, u_att, u_att, u_att, u_att)


def attn_bwd(u_att, dy, sinks, rel_bias, S, name):
    T = u_att.shape[0]
    nb = S // BQ
    nB = T // S
    table = jnp.asarray(_bucket_table())
    scale = HD ** -0.5

    def body(sk_ref, rb_ref, bk_ref, q_ref, kp_ref, kc_ref, vp_ref, vc_ref, dy_ref,
             du_ref, drel_ref, dsk_ref, bias_s, dbias_s):
        b = pl.program_id(0)
        n = pl.program_id(1)
        bk = bk_ref[...]

        @pl.when((b == 0) & (n == 0))
        def _():
            _att_bias(bk, rb_ref, bias_s)
            dbias_s[...] = jnp.zeros_like(dbias_s)
            dsk_ref[...] = jnp.zeros_like(dsk_ref)
            drel_ref[...] = jnp.zeros_like(drel_ref)

        @pl.when(n == 0)
        def _():
            du_ref[...] = jnp.zeros_like(du_ref)

        valid = (bk >= 0) & ((n > 0) | (_iota(bk.shape, 1) >= BQ))
        r_cur = pl.multiple_of(n * BQ, BQ)
        r_prev = pl.multiple_of(jnp.maximum(n - 1, 0) * BQ, BQ)
        heads = range(ATT_H)
        kgs, vgs = _att_kv(kp_ref, kc_ref, vp_ref, vc_ref)
        qs = [q_ref[:, HD * h:HD * (h + 1)] for h in heads]
        dos = [dy_ref[:, HD * h:HD * (h + 1)] for h in heads]
        p, ps = _att_probs(qs, kgs, bias_s, valid, sk_ref)
        dp = [_mm_nt(dos[h], vgs[h // ATT_G]) for h in heads]
        delta = [jnp.sum(p[h] * dp[h], axis=-1, keepdims=True) for h in heads]
        ds = [p[h] * (dp[h] - delta[h]) for h in heads]
        dss = [ds[h] * scale for h in heads]
        dq = [_mm(dss[h], kgs[h // ATT_G]) for h in heads]
        dks = [_mm_tn(dss[h], qs[h]) for h in heads]
        dvs = [_mm_tn(p[h], dos[h]) for h in heads]
        for h in heads:
            dbias_s[h] += ds[h]
            dsk_ref[h:h + 1, :] += jnp.broadcast_to(jnp.sum(-ps[h] * delta[h], axis=0, keepdims=True), (1, LANE))
            du_ref[pl.ds(r_cur, BQ), HD * h:HD * (h + 1)] = dq[h]
        for g in range(ATT_KV):
            of_group = range(g * ATT_G, (g + 1) * ATT_G)
            dk = functools.reduce(lambda x, y: x + y, [dks[h] for h in of_group])
            dv = functools.reduce(lambda x, y: x + y, [dvs[h] for h in of_group])
            ck = ATT_W + HD * g
            cv = ATT_W + KV_W + HD * g
            du_ref[pl.ds(r_prev, BQ), ck:ck + HD] += dk[0:BQ]
            du_ref[pl.ds(r_cur, BQ), ck:ck + HD] += dk[BQ:]
            du_ref[pl.ds(r_prev, BQ), cv:cv + HD] += dv[0:BQ]
            du_ref[pl.ds(r_cur, BQ), cv:cv + HD] += dv[BQ:]

        @pl.when((b == nB - 1) & (n == nb - 1))
        def _():
            lane = _iota((1, LANE), 1)
            for h in range(ATT_H):
                db = dbias_s[h]
                acc = jnp.zeros((1, LANE), F32)
                for bb in range(REL_BUCKETS):
                    val = jnp.sum(jnp.sum(jnp.where(bk == bb, db, 0.0), axis=1, keepdims=True),
                                  axis=0, keepdims=True)
                    acc = jnp.where(lane == bb, val, acc)
                drel_ref[h:h + 1, :] = acc

    smem = pl.BlockSpec(memory_space=pltpu.SMEM)
    return pl.pallas_call(
        body, name=name, grid=(nB, nb),
        in_specs=[smem, smem, _full_spec((BQ, 2 * BQ))] + _att_specs(S)
        + [pl.BlockSpec((BQ, ATT_W), lambda b, n: (b * nb + n, 0))],
        out_specs=[pl.BlockSpec((S, ATT_W + 2 * KV_W), lambda b, n: (b, 0)),
                   _full_spec((8, LANE)), _full_spec((8, LANE))],
        out_shape=[jax.ShapeDtypeStruct((T, ATT_W + 2 * KV_W), F32),
                   jax.ShapeDtypeStruct((8, LANE), F32), jax.ShapeDtypeStruct((8, LANE), F32)],
        scratch_shapes=[pltpu.VMEM((ATT_H, BQ, 2 * BQ), F32), pltpu.VMEM((ATT_H, BQ, 2 * BQ), F32)],
        compiler_params=_cp("arbitrary", "arbitrary"),
    )(sinks, rel_bias, table, u_att, u_att, u_att, u_att, u_att, dy)


def _head_of(i):
    return lax.shift_right_logical(i, 6)


def _head_mask(shape):
    return (_head_of(_iota(shape, 0)) == _head_of(_iota(shape, 1))).astype(F32)


def _dn_point(c, uba, alog, dtb):
    s = c * _sigmoid(c)
    qt, kt, vt = s[:, 0:256], s[:, 256:512], s[:, 512:768]
    ones_bd = _head_mask((DN_W, DN_W))
    q = qt * lax.rsqrt(_mm3(qt * qt, ones_bd) + EPS) * (HD ** -0.5)
    k = kt * lax.rsqrt(_mm3(kt * kt, ones_bd) + EPS)
    sel = _head_of(_iota((LANE, DN_W), 1))
    row = _iota((LANE, DN_W), 0)
    braw = _mm3(uba, (row == sel).astype(F32))
    araw = _mm3(uba, (row == sel + DN_H).astype(F32)) + dtb
    beta = _sigmoid(braw)
    g = -jnp.exp(alog) * (jnp.maximum(araw, 0.0) + jnp.log(1.0 + jnp.exp(-jnp.abs(araw))))
    return q, k, vt, g, beta


def dn_point_fwd(c, uba, alog, dtb, name):
    T = c.shape[0]
    tm = min(TOK_TILE, T)

    def body(c_ref, u_ref, al_ref, dt_ref, *outs):
        for ref, val in zip(outs, _dn_point(c_ref[...], u_ref[...], al_ref[...], dt_ref[...])):
            ref[...] = val

    return pl.pallas_call(
        body, name=name, grid=(T // tm,),
        in_specs=[_row_spec(tm, 768), _row_spec(tm, LANE), _full_spec((1, DN_W)), _full_spec((1, DN_W))],
        out_specs=[_row_spec(tm, DN_W)] * 5,
        out_shape=[jax.ShapeDtypeStruct((T, DN_W), F32)] * 5,
        compiler_params=_cp("parallel"),
    )(c, uba, alog, dtb)


def dn_point_bwd(c, uba, alog, dtb, douts, name):
    T = c.shape[0]
    tm = min(TOK_TILE, T)

    def body(c_ref, u_ref, al_ref, dt_ref, dq, dk, dv, dg, db, dc_ref, du_ref, dvec_ref):
        @pl.when(pl.program_id(0) == 0)
        def _():
            dvec_ref[...] = jnp.zeros_like(dvec_ref)

        _, vjp = jax.vjp(_dn_point, c_ref[...], u_ref[...], al_ref[...], dt_ref[...])
        dc, du, dal, ddt = vjp((dq[...], dk[...], dv[...], dg[...], db[...]))
        dc_ref[...] = dc
        du_ref[...] = du
        fold = (_iota((LANE, DN_W), 0) == _head_of(_iota((LANE, DN_W), 1))).astype(F32)
        both = jnp.concatenate([dal, ddt, jnp.zeros((6, DN_W), F32)], axis=0)
        dvec_ref[...] += _mmx_nt(both, fold)

    return pl.pallas_call(
        body, name=name, grid=(T // tm,),
        in_specs=[_row_spec(tm, 768), _row_spec(tm, LANE), _full_spec((1, DN_W)), _full_spec((1, DN_W))]
        + [_row_spec(tm, DN_W)] * 5,
        out_specs=[_row_spec(tm, 768), _row_spec(tm, LANE), _full_spec((8, LANE))],
        out_shape=[jax.ShapeDtypeStruct((T, 768), F32), jax.ShapeDtypeStruct((T, LANE), F32),
                   jax.ShapeDtypeStruct((8, LANE), F32)],
        compiler_params=_cp("arbitrary"),
    )(c, uba, alog, dtb, *douts)


def _unit_lower_inverses(lmats):
    eye = (_iota(lmats[0].shape, 0) == _iota(lmats[0].shape, 1)).astype(F32)
    tinvs = [eye - lm for lm in lmats]
    pws = list(lmats)
    for _ in range(5):
        pws = [_mm3(pw, pw) for pw in pws]
        tinvs = [t + _mm3(t, pw) for t, pw in zip(tinvs, pws)]
    return tuple(tinvs)


def _inverse_bwd(tinv, d):
    return -_mm3_nt(_mm3_tn(tinv, d), tinv)


@jax.custom_vjp
def _tri_invs(lmats):
    return _unit_lower_inverses(lmats)


def _tri_invs_fwd(lmats):
    tinvs = _unit_lower_inverses(lmats)
    return tinvs, tinvs


_tri_invs.defvjp(_tri_invs_fwd, lambda tinvs, ds: (tuple(_inverse_bwd(t, d) for t, d in zip(tinvs, ds)),))


@jax.custom_vjp
def _tri_inv_known(lmat, tinv):
    return tinv


_tri_inv_known.defvjp(lambda lmat, tinv: (tinv, tinv),
                      lambda tinv, d: (_inverse_bwd(tinv, d), jnp.zeros_like(tinv)))


DN_SUB = 4


def _dn_stack(x):
    return jnp.concatenate([x, x, x, x], axis=0) * _head_mask((DN_W, DN_W))


def _dn_pre_inverse(q, k, v, g, beta):
    hm = _head_mask((DN_W, DN_W))
    ri = _iota((DN_W, DN_W), 0) & (CHUNK - 1)
    ci = _iota((DN_W, DN_W), 1) & (CHUNK - 1)
    tri64 = (_iota((CHUNK, CHUNK), 0) >= _iota((CHUNK, CHUNK), 1)).astype(F32)
    gc = _mm3(tri64, g)
    ks = _dn_stack(k)
    gcol = jnp.sum(_dn_stack(gc), axis=1, keepdims=True) * (1.0 / HD)
    gmat = jnp.broadcast_to(gcol, (DN_W, DN_W))
    decay = jnp.exp(jnp.minimum(gmat - gmat.T, 0.0))
    lmat = _mm_nt(_dn_stack(k * beta), ks) * decay * (hm * (ri > ci).astype(F32))
    att = _mm_nt(_dn_stack(q), ks) * decay * (hm * (ri >= ci).astype(F32))
    return lmat, att, gc


def _dn_post_inverse(q, k, v, g, beta, tinv, att, gc):
    glast = jnp.sum(g, axis=0, keepdims=True)
    eg = jnp.exp(gc)
    u = _mm(tinv, _dn_stack(v * beta))
    w = _mm(tinv, _dn_stack(k * beta * eg))
    return u, w, att, _dn_stack(q * eg), _dn_stack(k * jnp.exp(glast - gc)), jnp.exp(glast), tinv


def _dn_apply(state, prep):
    u, w, att, qe, kd, eglast, _ = prep
    vn = u - _mm(w, state)
    o4 = _mm(qe, state) + _mm(att, vn)
    o = o4[0:64] + o4[64:128] + o4[128:192] + o4[192:256]
    return o, state * eglast + _mm_tn(kd, vn)


def _dn_chunks(state, q, k, v, g, beta, knowns=None):
    n = q.shape[0] // CHUNK
    chunks = [tuple(x[c * CHUNK:(c + 1) * CHUNK] for x in (q, k, v, g, beta)) for c in range(n)]
    pre = [_dn_pre_inverse(*ch) for ch in chunks]
    if knowns is None:
        tinvs = _tri_invs(tuple(p[0] for p in pre))
    else:
        tinvs = [_tri_inv_known(p[0], known) for p, known in zip(pre, knowns)]
    preps = [_dn_post_inverse(*ch, tinv, p[1], p[2]) for ch, tinv, p in zip(chunks, tinvs, pre)]
    outs = []
    for prep in preps:
        o, state = _dn_apply(state, prep)
        outs.append(o)
    return jnp.concatenate(outs, axis=0), state, [prep[-1] for prep in preps]


def dn_scan_fwd(q, k, v, g, beta, S, name):
    T = q.shape[0]
    rows = DN_SUB * CHUNK
    ns = S // rows

    def body(q_ref, k_ref, v_ref, g_ref, b_ref, o_ref, st_ref, ti_ref, s_s):
        @pl.when(pl.program_id(1) == 0)
        def _():
            s_s[...] = jnp.zeros_like(s_s)

        st = s_s[...]
        st_ref[0] = st
        o, new, tinvs = _dn_chunks(st, q_ref[...], k_ref[...], v_ref[...], g_ref[...], b_ref[...])
        o_ref[...] = o
        for c, tinv in enumerate(tinvs):
            ti_ref[c] = tinv
        s_s[...] = new

    spec = pl.BlockSpec((rows, DN_W), lambda b, t: (b * ns + t, 0))
    return pl.pallas_call(
        body, name=name, grid=(T // S, ns),
        in_specs=[spec] * 5,
        out_specs=[spec, pl.BlockSpec((1, DN_W, DN_W), lambda b, t: (b * ns + t, 0, 0)),
                   pl.BlockSpec((DN_SUB, DN_W, DN_W), lambda b, t: (b * ns + t, 0, 0))],
        out_shape=[jax.ShapeDtypeStruct((T, DN_W), F32),
                   jax.ShapeDtypeStruct((T // rows, DN_W, DN_W), F32),
                   jax.ShapeDtypeStruct((T // CHUNK, DN_W, DN_W), F32)],
        scratch_shapes=[pltpu.VMEM((DN_W, DN_W), F32)],
        compiler_params=_cp("parallel", "arbitrary"),
    )(q, k, v, g, beta)


def dn_scan_bwd(q, k, v, g, beta, states, tinvs, do, S, name):
    T = q.shape[0]
    rows = DN_SUB * CHUNK
    ns = S // rows

    def body(q_ref, k_ref, v_ref, g_ref, b_ref, st_ref, ti_ref, do_ref, dq, dk, dv, dg, db, ds_s):
        @pl.when(pl.program_id(1) == 0)
        def _():
            ds_s[...] = jnp.zeros_like(ds_s)

        knowns = [ti_ref[c] for c in range(DN_SUB)]
        _, vjp = jax.vjp(lambda *args: _dn_chunks(*args, knowns=knowns)[:2],
                         st_ref[0], q_ref[...], k_ref[...], v_ref[...], g_ref[...], b_ref[...])
        grads = vjp((do_ref[...], ds_s[...]))
        ds_s[...] = grads[0]
        for ref, val in zip((dq, dk, dv, dg, db), grads[1:]):
            ref[...] = val

    spec = pl.BlockSpec((rows, DN_W), lambda b, t: (b * ns + ns - 1 - t, 0))
    return pl.pallas_call(
        body, name=name, grid=(T // S, ns),
        in_specs=[spec] * 5 + [pl.BlockSpec((1, DN_W, DN_W), lambda b, t: (b * ns + ns - 1 - t, 0, 0)),
                               pl.BlockSpec((DN_SUB, DN_W, DN_W), lambda b, t: (b * ns + ns - 1 - t, 0, 0)),
                               spec],
        out_specs=[spec] * 5,
        out_shape=[jax.ShapeDtypeStruct((T, DN_W), F32)] * 5,
        scratch_shapes=[pltpu.VMEM((DN_W, DN_W), F32)],
        compiler_params=_cp("parallel", "arbitrary"),
    )(q, k, v, g, beta, states, tinvs, do)


def _dn_gate(o, z, nl):
    ms = _mm3(o * o, _head_mask((DN_W, DN_W))) * (1.0 / HD)
    return o * lax.rsqrt(ms + EPS) * nl * (z * _sigmoid(z))


def dn_gate_fwd(o, u_dn, nl, name):
    T = o.shape[0]
    tm = min(TOK_TILE, T)

    def body(o_ref, z_ref, n_ref, y_ref):
        y_ref[...] = _dn_gate(o_ref[...], z_ref[...], n_ref[...])

    return pl.pallas_call(
        body, name=name, grid=(T // tm,),
        in_specs=[_row_spec(tm, DN_W), pl.BlockSpec((tm, DN_W), lambda i: (i, 3)), _full_spec((1, DN_W))],
        out_specs=_row_spec(tm, DN_W),
        out_shape=jax.ShapeDtypeStruct((T, DN_W), F32),
        compiler_params=_cp("parallel"),
    )(o, u_dn, nl)


def dn_gate_bwd(o, u_dn, nl, dy, name):
    T = o.shape[0]
    tm = min(TOK_TILE, T)

    def body(o_ref, z_ref, n_ref, dy_ref, do_ref, dz_ref, dn_ref):
        @pl.when(pl.program_id(0) == 0)
        def _():
            dn_ref[...] = jnp.zeros_like(dn_ref)

        _, vjp = jax.vjp(_dn_gate, o_ref[...], z_ref[...], n_ref[...])
        do, dz, dn = vjp(dy_ref[...])
        do_ref[...] = do
        dz_ref[...] = dz
        fold = (_iota((LANE, DN_W), 0) == (_iota((LANE, DN_W), 1) & (HD - 1))).astype(F32)
        dn_ref[...] += _mmx_nt(jnp.concatenate([dn, jnp.zeros((7, DN_W), F32)], axis=0), fold)

    return pl.pallas_call(
        body, name=name, grid=(T // tm,),
        in_specs=[_row_spec(tm, DN_W), pl.BlockSpec((tm, DN_W), lambda i: (i, 3)), _full_spec((1, DN_W)),
                  _row_spec(tm, DN_W)],
        out_specs=[_row_spec(tm, DN_W), _row_spec(tm, DN_W), _full_spec((8, LANE))],
        out_shape=[jax.ShapeDtypeStruct((T, DN_W), F32), jax.ShapeDtypeStruct((T, DN_W), F32),
                   jax.ShapeDtypeStruct((8, LANE), F32)],
        compiler_params=_cp("arbitrary"),
    )(o, u_dn, nl, dy)


Y_SPLITS = (LRU_W, ATT_W, DN_W)
Y_OFFS = (0, LRU_W, LRU_W + ATT_W)


ROWS_DEV = D // N_DEV


def _dev_rows_spec(row0):
    return pl.BlockSpec((N_DEV, ROWS_DEV, D), lambda *_: (0, row0 // ROWS_DEV, 0))


def _dev_rows(w_ref, off, n):
    return w_ref[off // ROWS_DEV:(off + n) // ROWS_DEV].reshape(n, D)


def wout_fwd(h, ys, wb, name):
    T = h.shape[0]
    tm = min(TOK_TILE, T)

    def body(h_ref, y0, y1, y2, w_ref, o_ref, yc_ref):
        acc = h_ref[...]
        for ref, off, n in zip((y0, y1, y2), Y_OFFS, Y_SPLITS):
            y = ref[...].astype(BF16)
            yc_ref[:, off:off + n] = y
            acc += _mm(y, _dev_rows(w_ref, off, n))
        o_ref[...] = acc

    return pl.pallas_call(
        body, name=name, grid=(T // tm,),
        in_specs=[_row_spec(tm, D)] + [_row_spec(tm, n) for n in Y_SPLITS] + [_dev_rows_spec(R_WOUT)],
        out_specs=[_row_spec(tm, D), _row_spec(tm, D)],
        out_shape=[jax.ShapeDtypeStruct((T, D), F32), jax.ShapeDtypeStruct((T, D), BF16)],
        compiler_params=_cp("parallel"),
    )(h, *ys, wb)


def wout_bwd(dy, wb, name):
    T = dy.shape[0]
    tm = min(TOK_TILE, T)

    def body(dy_ref, w_ref, d0, d1, d2):
        dd = dy_ref[...].astype(BF16)
        for ref, off, n in zip((d0, d1, d2), Y_OFFS, Y_SPLITS):
            ref[...] = _mm_nt(dd, _dev_rows(w_ref, off, n))

    return pl.pallas_call(
        body, name=name, grid=(T // tm,),
        in_specs=[_row_spec(tm, D), _dev_rows_spec(R_WOUT)],
        out_specs=[_row_spec(tm, n) for n in Y_SPLITS],
        out_shape=[jax.ShapeDtypeStruct((T, n), F32) for n in Y_SPLITS],
        compiler_params=_cp("parallel"),
    )(dy, wb)


def ple_fwd(h, g, pe, wg, wp, name):
    T = h.shape[0]
    tm = min(TOK_TILE, T)

    def body(h_ref, g_ref, p_ref, wg_ref, wp_ref, o_ref):
        hh = h_ref[...]
        xn = _rms(hh, g_ref[...])[0]
        o_ref[...] = hh + _sigmoid(_mm(xn, _dev_rows(wg_ref, 0, D))) * _mm(p_ref[...], wp_ref[...])

    return pl.pallas_call(
        body, name=name, grid=(T // tm,),
        in_specs=[_row_spec(tm, D), _full_spec((1, D)), _row_spec(tm, PLE), _dev_rows_spec(R_PGATE),
                  _full_spec((PLE, D))],
        out_specs=_row_spec(tm, D),
        out_shape=jax.ShapeDtypeStruct((T, D), F32),
        compiler_params=_cp("parallel"),
    )(h, g, pe, wg, wp)


def ple_bwd(h, dy, g, pe, wg, wp, name):
    T = h.shape[0]
    tm = min(TOK_TILE, T)

    def body(h_ref, dy_ref, g_ref, p_ref, wg_ref, wp_ref, dh_ref, dz_ref, dpp_ref, xn_ref, dn_ref):
        @pl.when(pl.program_id(0) == 0)
        def _():
            dn_ref[...] = jnp.zeros_like(dn_ref)

        gg = g_ref[...]
        dy = dy_ref[...]
        xn, xhat, rstd = _rms(h_ref[...], gg)
        wg = _dev_rows(wg_ref, 0, D)
        gate = _sigmoid(_mm(xn, wg))
        pp = _mm(p_ref[...], wp_ref[...])
        dz = dy * pp * gate * (1.0 - gate)
        dz_ref[...] = dz.astype(BF16)
        dpp_ref[...] = (dy * gate).astype(BF16)
        xn_ref[...] = xn.astype(BF16)
        dh, dn = _rms_bwd(_mm_nt(dz, wg), xhat, rstd, gg)
        dh_ref[...] = dy + dh
        dn_ref[...] += dn

    return pl.pallas_call(
        body, name=name, grid=(T // tm,),
        in_specs=[_row_spec(tm, D), _row_spec(tm, D), _full_spec((1, D)), _row_spec(tm, PLE),
                  _dev_rows_spec(R_PGATE), _full_spec((PLE, D))],
        out_specs=[_row_spec(tm, D), _row_spec(tm, D), _row_spec(tm, D), _row_spec(tm, D), _full_spec((1, D))],
        out_shape=[jax.ShapeDtypeStruct((T, D), F32), jax.ShapeDtypeStruct((T, D), BF16),
                   jax.ShapeDtypeStruct((T, D), BF16), jax.ShapeDtypeStruct((T, D), BF16),
                   jax.ShapeDtypeStruct((1, D), F32)],
        compiler_params=_cp("arbitrary"),
    )(h, dy, g, pe, wg, wp)


def loss_head(h, g, target, name):
    T = h.shape[0]
    tm = min(TOK_TILE, T)

    def body(h_ref, g_ref, t_ref, loss_ref, dh_ref, dn_ref):
        @pl.when(pl.program_id(0) == 0)
        def _():
            dn_ref[...] = jnp.zeros_like(dn_ref)
            loss_ref[...] = jnp.zeros_like(loss_ref)

        gg = g_ref[...]
        y, xhat, rstd = _rms(h_ref[...], gg)
        err = y - t_ref[...]
        per_tok = jnp.mean(err * err, axis=-1, keepdims=True)
        loss_ref[...] += 0.5 * jnp.sum(per_tok, axis=0, keepdims=True)
        dh, dn = _rms_bwd(err * (1.0 / D), xhat, rstd, gg)
        dh_ref[...] = dh
        dn_ref[...] += dn

    return pl.pallas_call(
        body, name=name, grid=(T // tm,),
        in_specs=[_row_spec(tm, D), _full_spec((1, D)), _row_spec(tm, D)],
        out_specs=[_full_spec((8, LANE)), _row_spec(tm, D), _full_spec((1, D))],
        out_shape=[jax.ShapeDtypeStruct((8, LANE), F32), jax.ShapeDtypeStruct((T, D), F32),
                   jax.ShapeDtypeStruct((1, D), F32)],
        compiler_params=_cp("arbitrary"),
    )(h, g, target)


def _block_diag(w):
    return jnp.einsum('hij,hk->hikj', w, jnp.eye(4, dtype=w.dtype)).reshape(LRU_W, LRU_W)


def _layer_consts(W, l):
    row = lambda v: v.reshape(1, -1)
    zeros = jnp.zeros((5, LRU_W), F32)
    return dict(
        wa=_block_diag(W["lru_w_a"][l]), wx=_block_diag(W["lru_w_x"][l]),
        lru_vec=jnp.concatenate([row(W["lru_b_a"][l]), row(W["lru_b_x"][l]), row(W["lru_lambda"][l]), zeros], 0),
        lru_cb=row(W["lru_conv_b"][l]),
        sinks=W["attn_sinks"][l], rel=W["rel_bias"].reshape(-1),
        dn_cb=jnp.zeros((1, 3 * DN_W), F32),
        alog=row(jnp.repeat(W["dn_a_log"][l], HD)), dtb=row(jnp.repeat(W["dn_dt_bias"][l], HD)),
        dn_nl=row(jnp.tile(W["dn_norm"][l], DN_H)),
    )


def _layer_fwd(h0, pe, W, l, S, need=None):
    n = f"l{l}_"
    c_ = _layer_consts(W, l)
    row = lambda v: v.reshape(1, -1)
    need = need or (lambda *_: None)
    need(l, "f1", h0)
    h1, *ffn1_kept = ffn_fwd(h0, row(W["ffn1_norm"][l]), W["f1_cols"][l], W["f1_rows"][l], n + "ffn1_fwd")
    need(l, "in", h1)
    u_lru, u_att, u_dn, u_ba, xn_mix = mixin_fwd(h1, row(W["mix_norm"][l]), W["w_in"][l], n + "mixin_fwd")
    xr = conv_fwd(u_lru, W["lru_conv_w"][l], c_["lru_cb"], S, 0, LRU_W, n + "lru_conv_fwd")
    y_lru = lru_fwd(xr, u_lru, c_["wa"], c_["wx"], c_["lru_vec"], S, n + "lru_fwd")
    y_att = attn_fwd(u_att, c_["sinks"], c_["rel"], S, n + "attn_fwd")
    cc = conv_fwd(u_dn, W["dn_conv_w"][l], c_["dn_cb"], S, 0, 3 * DN_W, n + "dn_conv_fwd")
    q, k, v, g, beta = dn_point_fwd(cc, u_ba, c_["alog"], c_["dtb"], n + "dn_point_fwd")
    o, states, tinvs = dn_scan_fwd(q, k, v, g, beta, S, n + "dn_scan_fwd")
    y_dn = dn_gate_fwd(o, u_dn, c_["dn_nl"], n + "dn_gate_fwd")
    need(l, "rest", y_dn)
    h2, ycat = wout_fwd(h1, (y_lru, y_att, y_dn), W["r_rows"][l], n + "wout_fwd")
    h3, *ffn2_kept = ffn_fwd(h2, row(W["ffn2_norm"][l]), W["r_cols"][l], W["r_rows"][l], n + "ffn2_fwd")
    h4 = ple_fwd(h3, row(W["ple_norm"][l]), pe, W["r_rows"][l], W["ple_w_proj"][l], n + "ple_fwd")
    saved = dict(ffn1=ffn1_kept, ffn2=ffn2_kept, h0=h0, h1=h1, h2=h2, h3=h3, u_lru=u_lru, u_att=u_att, u_dn=u_dn,
                 u_ba=u_ba, xn_mix=xn_mix, xr=xr, cc=cc, q=q, k=k, v=v, g=g, beta=beta, o=o, states=states, tinvs=tinvs, ycat=ycat)
    return h4, saved


GM_WOUT, GM_PGATE, GM_WIN, GM_PPROJ, GM_END, GM_ROWS = 0, 128, 256, 560, 592, 640


def _layer_bwd(dh4, sv, pe, W, l, S, token=None, on_piece=None):
    n = f"l{l}_"
    c_ = _layer_consts(W, l)
    row = lambda v: v.reshape(1, -1)
    behind = lambda v, tok: v if tok is None else v + tok.astype(v.dtype)
    on_piece = on_piece or (lambda *_: None)
    G = {"mix_rows": jnp.zeros((N_DEV, GM_ROWS, D), BF16)}
    dh3, dz, dpp, xn_p, dn = ple_bwd(sv["h3"], dh4, behind(row(W["ple_norm"][l]), token), pe, W["r_rows"][l],
                                     W["ple_w_proj"][l], n + "ple_bwd")
    G["ple_norm"] = dn[0]
    G["mix_rows"] = grad_rows(xn_p, dz, G["mix_rows"], GM_PGATE, ROWS_DEV, n + "d_ple_w_gate")
    d_proj = matmul_tn(pe, dpp, n + "d_ple_w_proj")
    d_proj = d_proj.reshape(PLE, N_DEV, D // N_DEV).transpose(1, 0, 2).reshape(N_DEV, GM_END - GM_PPROJ, D)
    G["mix_rows"] = lax.dynamic_update_slice(G["mix_rows"], d_proj, (0, GM_PPROJ, 0))

    def ffn_back(which, cols_w, rows_w, h_in, dy, tok, one_by_one):
        gt, up, xn = sv[which]
        dh, dgt, dup, act, dn_ = ffn_bwd(h_in, dy, behind(row(W[which + "_norm"][l]), tok), gt, up, cols_w, rows_w,
                                         n + which + "_bwd")
        G[which + "_norm"] = dn_[0]
        zeros_rows = jnp.zeros((N_DEV, SHP, D), BF16)
        if one_by_one:
            G[which + "_gate"] = grad_cols(xn, dgt, lax.empty((1, D, FFP), BF16), 0, n + "d_" + which + "_w_gate")
            tok = on_piece(l, which + "_gate", (G[which + "_gate"],))
            G[which + "_up"] = grad_cols(xn, dup, behind(jnp.zeros((1, D, FFP), BF16), tok), 0,
                                         n + "d_" + which + "_w_up")
            tok = on_piece(l, which + "_up", (G[which + "_up"],))
            G[which + "_down"] = grad_rows(act, dy, behind(zeros_rows, tok), 0, SHP, n + "d_" + which + "_w_down",
                                           scale=0.5)
            return dh, on_piece(l, which + "_down", (G[which + "_down"],))
        cols = grad_cols(xn, dgt, lax.empty((2, D, FFP), BF16), 0, n + "d_" + which + "_w_gate")
        G[which + "_cols"] = grad_cols(xn, dup, cols, 1, n + "d_" + which + "_w_up")
        G[which + "_rows"] = grad_rows(act, dy, lax.empty((N_DEV, SHP, D), BF16), 0, SHP,
                                       n + "d_" + which + "_w_down", scale=0.5)
        return dh, on_piece(l, which, (G[which + "_cols"], G[which + "_rows"]))

    dh2, tok = ffn_back("ffn2", W["r_cols"][l], W["r_rows"][l], sv["h2"], dh3, None, False)
    dy_lru, dy_att, dy_dn = wout_bwd(dh2, W["r_rows"][l], n + "wout_bwd")
    G["mix_rows"] = grad_rows(sv["ycat"], dh2, G["mix_rows"], GM_WOUT, ROWS_DEV, n + "d_w_out")
    do, dz_dn, dnn = dn_gate_bwd(sv["o"], sv["u_dn"], behind(c_["dn_nl"], tok), dy_dn, n + "dn_gate_bwd")
    dqkvgb = dn_scan_bwd(sv["q"], sv["k"], sv["v"], sv["g"], sv["beta"], sv["states"], sv["tinvs"], do, S,
                         n + "dn_scan_bwd")
    dcc, du_ba, dvec_dn = dn_point_bwd(sv["cc"], sv["u_ba"], c_["alog"], c_["dtb"], dqkvgb, n + "dn_point_bwd")
    dqkv, dwb_dn = conv_bwd(sv["u_dn"], dcc, W["dn_conv_w"][l], S, 0, 3 * DN_W, n + "dn_conv_bwd")
    G["dn_norm"] = dnn[0, 0:HD]
    G["dn_a_log"] = dvec_dn[0, 0:DN_H]
    G["dn_dt_bias"] = dvec_dn[1, 0:DN_H]
    G["dn_conv_w"] = dwb_dn[0:4]
    du_att, drel, dsk = attn_bwd(sv["u_att"], dy_att, c_["sinks"], c_["rel"], S, n + "attn_bwd")
    G["attn_sinks"] = dsk[:, 0]
    G["rel_bias"] = drel[:, 0:REL_BUCKETS].T
    dxr, dgt_lru, dwa, dwx, dvec = lru_bwd(sv["xr"], sv["u_lru"], dy_lru, c_["wa"], c_["wx"], c_["lru_vec"], S,
                                           n + "lru_bwd")
    dx_lru, dwb_lru = conv_bwd(sv["u_lru"], dxr, W["lru_conv_w"][l], S, 0, LRU_W, n + "lru_conv_bwd")
    diag = lambda m: jnp.stack([m[c, HD * e:HD * (e + 1), HD * e:HD * (e + 1)] for c in range(2) for e in range(2)])
    G["lru_w_a"], G["lru_w_x"] = diag(dwa), diag(dwx)
    G["lru_b_a"], G["lru_b_x"], G["lru_lambda"] = dvec[0], dvec[1], dvec[2]
    G["lru_conv_w"], G["lru_conv_b"] = dwb_lru[0:4], dwb_lru[4]
    dh1, du_cat, dn = mixin_bwd(sv["h1"], dh2, row(W["mix_norm"][l]), W["w_in"][l],
                                (dx_lru, dgt_lru, du_att, dqkv, dz_dn, du_ba), n + "mixin_bwd")
    G["mix_norm"] = dn[0]
    d_in = matmul_tn(sv["xn_mix"], du_cat, n + "d_w_in")[:, :D_IN]
    d_in = d_in.reshape(D, N_DEV, D_IN // N_DEV).transpose(1, 0, 2).reshape(N_DEV, WIN_ROWS, D)
    d_in = jnp.pad(d_in, ((0, 0), (0, GM_PPROJ - GM_WIN - WIN_ROWS), (0, 0)))
    G["mix_rows"] = lax.dynamic_update_slice(G["mix_rows"], d_in, (0, GM_WIN, 0))
    tok = on_piece(l, "mix", (G["mix_rows"],))
    dh0, tok = ffn_back("ffn1", W["f1_cols"][l], W["f1_rows"][l], sv["h0"], dh1, tok, l == 0)
    return dh0, G, tok


def _core(x, pe, W, target, S, need=None, on_piece=None):
    h = x
    saved = []
    for l in range(DEPTH):
        h, sv = _layer_fwd(h, pe[l], W, l, S, need)
        saved.append(sv)
    loss_tile, dh, dfn = loss_head(h, W["final_norm"].reshape(1, -1), target, "loss_head")
    grads = [None] * DEPTH
    token = None
    for l in reversed(range(DEPTH)):
        dh, grads[l], token = _layer_bwd(dh, saved[l], pe[l], W, l, S, token, on_piece)
    return loss_tile[0, 0], dh, grads, dfn[0]


MESH_ID = pl.DeviceIdType.MESH
ANY_SPEC = pl.BlockSpec(memory_space=pl.ANY)
AXES = ("x", "y", "c")


def _my_pos():
    return lax.axis_index("x"), lax.axis_index("y"), lax.axis_index("c")


def _slot_of(px, py, pc):
    return 4 * px + 2 * py + pc


def all_gather(x, name):
    R, C = x.shape

    def body(x_ref, out_ref, send_sems, recv_sems, local_sem):
        mx, my, mc = _my_pos()
        me, sibling = (mx, my, mc), (mx, my, 1 - mc)
        chips = [(1 - mx, my), (mx, 1 - my), (1 - mx, 1 - my)]

        def copy(k, block, to, src=None):
            dst = out_ref.at[_slot_of(*block)]
            return pltpu.make_async_remote_copy(
                src_ref=dst if src is None else src, dst_ref=dst,
                send_sem=send_sems.at[k], recv_sem=recv_sems.at[k],
                device_id=to, device_id_type=MESH_ID)

        mine = pltpu.make_async_copy(x_ref, out_ref.at[_slot_of(*me)], local_sem)
        mine.start()
        first = [copy(0, me, sibling, src=x_ref)]
        first += [copy(1 + j, me, (*chip, mc), src=x_ref) for j, chip in enumerate(chips)]
        for cp in first:
            cp.start()
        passed = [copy(4 + j, (*chip, mc), sibling) for j, chip in enumerate(chips)]
        for j, chip in enumerate(chips):
            copy(1 + j, (*chip, mc), me).wait_recv()
            passed[j].start()
        copy(0, sibling, me).wait_recv()
        for j, chip in enumerate(chips):
            copy(4 + j, (*chip, 1 - mc), me).wait_recv()
        for cp in first + passed:
            cp.wait_send()
        mine.wait()

    return pl.pallas_call(
        body, name=name,
        out_shape=jax.ShapeDtypeStruct((N_DEV, R, C), x.dtype),
        in_specs=[ANY_SPEC], out_specs=ANY_SPEC,
        scratch_shapes=[pltpu.SemaphoreType.DMA((7,)), pltpu.SemaphoreType.DMA((7,)), pltpu.SemaphoreType.DMA],
    )(x)


def _col_window(ref, slot):
    return ref.at[:, pl.ds(pl.multiple_of(slot * SHP, LANE), SHP)]


def gather_layer(a_sh, b_sh, name):
    def body(a_ref, b_ref, ao_ref, bo_ref, send_sems, recv_sems, local_sems):
        mx, my, mc = _my_pos()
        me, sibling = (mx, my, mc), (mx, my, 1 - mc)
        chips = [(1 - mx, my), (mx, 1 - my), (1 - mx, 1 - my)]

        def copies(k, block, to, own=False):
            slot = _slot_of(*block)
            dsts = (_col_window(ao_ref, slot), bo_ref.at[slot])
            srcs = (a_ref, b_ref) if own else dsts
            return [pltpu.make_async_remote_copy(
                src_ref=s, dst_ref=d, send_sem=send_sems.at[2 * k + i], recv_sem=recv_sems.at[2 * k + i],
                device_id=to, device_id_type=MESH_ID) for i, (s, d) in enumerate(zip(srcs, dsts))]

        mine = [pltpu.make_async_copy(a_ref, _col_window(ao_ref, _slot_of(*me)), local_sems.at[0]),
                pltpu.make_async_copy(b_ref, bo_ref.at[_slot_of(*me)], local_sems.at[1])]
        for cp in mine:
            cp.start()
        first = copies(0, me, sibling, own=True)
        for j, chip in enumerate(chips):
            first += copies(1 + j, me, (*chip, mc), own=True)
        for cp in first:
            cp.start()
        passed = []
        for j, chip in enumerate(chips):
            for cp in copies(1 + j, (*chip, mc), me):
                cp.wait_recv()
            fwd = copies(4 + j, (*chip, mc), sibling)
            for cp in fwd:
                cp.start()
            passed += fwd
        for cp in copies(0, sibling, me):
            cp.wait_recv()
        for j, chip in enumerate(chips):
            for cp in copies(4 + j, (*chip, 1 - mc), me):
                cp.wait_recv()
        for cp in first + passed:
            cp.wait_send()
        for cp in mine:
            cp.wait()

    return pl.pallas_call(
        body, name=name,
        out_shape=[jax.ShapeDtypeStruct((a_sh.shape[0], FFP), a_sh.dtype),
                   jax.ShapeDtypeStruct((N_DEV,) + b_sh.shape, b_sh.dtype)],
        in_specs=[ANY_SPEC, ANY_SPEC], out_specs=[ANY_SPEC, ANY_SPEC],
        scratch_shapes=[pltpu.SemaphoreType.DMA((14,)), pltpu.SemaphoreType.DMA((14,)),
                        pltpu.SemaphoreType.DMA((2,))],
    )(a_sh, b_sh)


HBM_SPEC = pl.BlockSpec(memory_space=pltpu.HBM)
SEM_SPEC = pl.BlockSpec(memory_space=pltpu.SEMAPHORE)
SPLIT_EFFECT = pltpu.CompilerParams(has_side_effects=pltpu.SideEffectType.DATAFLOW_SIDE_EFFECTING)


def _split_ends(mode, src_ref, dst_ref, src_slot, dst_slot):
    cols = mode.endswith("cols")
    if mode.startswith("gather"):
        return src_ref, (_col_window(dst_ref, dst_slot) if cols else dst_ref.at[dst_slot])
    return (_col_window(src_ref, src_slot) if cols else src_ref.at[src_slot]), dst_ref.at[dst_slot]


def _split_peers():
    mx, my, mc = _my_pos()
    for r in range(1, N_DEV):
        peer = (1 - mx if r & 4 else mx, 1 - my if r & 2 else my, 1 - mc if r & 1 else mc)
        yield r - 1, peer, _slot_of(*peer)


def split_start(modes, srcs, dsts, name, after=()):
    n = len(modes)
    m = len(after)

    def body(*refs):
        send_sems, recv_sems, token = refs[2 * n + m], refs[2 * n + m + 1], refs[-1]
        mine = _slot_of(*_my_pos())
        for k, peer, ps in _split_peers():
            for i in range(n):
                src, dst = _split_ends(modes[i], refs[i], refs[n + i], ps, mine)
                pltpu.make_async_remote_copy(
                    src_ref=src, dst_ref=dst, send_sem=send_sems.at[n * k + i], recv_sem=recv_sems.at[n * k + i],
                    device_id=peer, device_id_type=MESH_ID).start()
        for i in range(n):
            src, dst = _split_ends(modes[i], refs[i], refs[n + i], mine, mine)
            pltpu.make_async_copy(src, dst, recv_sems.at[n * (N_DEV - 1) + i]).start()
        token[...] = jnp.zeros_like(token)

    bufs = tuple(srcs) + tuple(dsts)
    sems = pltpu.SemaphoreType.DMA((n * N_DEV,))
    res = pl.pallas_call(
        body, name=name,
        out_shape=(sems, sems) + tuple(pltpu.HBM(t.shape, t.dtype) for t in bufs)
        + (jax.ShapeDtypeStruct((8, LANE), F32),),
        in_specs=[HBM_SPEC] * (2 * n) + [ANY_SPEC] * m,
        out_specs=(SEM_SPEC, SEM_SPEC) + (HBM_SPEC,) * (2 * n) + (pl.BlockSpec(memory_space=pltpu.VMEM),),
        input_output_aliases={i: 2 + i for i in range(2 * n)},
        compiler_params=SPLIT_EFFECT,
    )(*(pltpu.with_memory_space_constraint(t, pltpu.HBM) for t in bufs), *after)
    return list(res[:-1]), res[-1]


def split_wait(modes, started, after, name):
    n = len(modes)
    after = tuple(after) if isinstance(after, (tuple, list)) else (after,)
    send_sems, recv_sems, bufs = started[0], started[1], started[2:]

    def body(*refs):
        send_sems, recv_sems = refs[2 * n], refs[2 * n + 1]
        mine = _slot_of(*_my_pos())
        for k, peer, ps in _split_peers():
            for i in range(n):
                sent = _split_ends(modes[i], refs[i], refs[n + i], ps, mine)[0]
                landed = _split_ends(modes[i], refs[i], refs[n + i], mine, ps)[1]
                cp = pltpu.make_async_remote_copy(
                    src_ref=sent, dst_ref=landed, send_sem=send_sems.at[n * k + i],
                    recv_sem=recv_sems.at[n * k + i], device_id=peer, device_id_type=MESH_ID)
                cp.wait_send()
                cp.wait_recv()
        for i in range(n):
            src, dst = _split_ends(modes[i], refs[i], refs[n + i], mine, mine)
            pltpu.make_async_copy(src, dst, recv_sems.at[n * (N_DEV - 1) + i]).wait()

    res = pl.pallas_call(
        body, name=name,
        out_shape=tuple(pltpu.HBM(t.shape, t.dtype) for t in bufs),
        in_specs=[HBM_SPEC] * (2 * n) + [SEM_SPEC, SEM_SPEC] + [ANY_SPEC] * len(after),
        out_specs=(HBM_SPEC,) * (2 * n),
        input_output_aliases={i: i for i in range(2 * n)},
        compiler_params=SPLIT_EFFECT,
    )(*bufs, send_sems, recv_sems, *after)
    return list(res[n:])


def sum_parts(parts, name):
    _, R, C = parts.shape
    tr = _pick(R, (512, 336, 272, 256, 128, 64, 32, 16, 8))

    def body(p_ref, o_ref):
        acc = p_ref[0].astype(F32)
        for k in range(1, N_DEV):
            acc += p_ref[k].astype(F32)
        o_ref[...] = acc

    return pl.pallas_call(
        body, name=name, grid=(R // tr,),
        in_specs=[pl.BlockSpec((N_DEV, tr, C), lambda i: (0, i, 0))],
        out_specs=pl.BlockSpec((tr, C), lambda i: (i, 0)),
        out_shape=jax.ShapeDtypeStruct((R, C), F32),
        compiler_params=_cp("parallel"),
    )(parts)


def adamw(g, w, m, v, name):
    lead, (R, C) = g.shape[:-2], g.shape[-2:]
    tr = _pick(R, (512, 352, 256, 128, 64, 32, 16, 8))
    c1 = 1.0 - ADAM_B1 ** ADAM_STEP
    c2 = 1.0 - ADAM_B2 ** ADAM_STEP

    def body(g_ref, w_ref, m_ref, v_ref, d_ref, nm_ref, nv_ref):
        gg = g_ref[...]
        mm = ADAM_B1 * m_ref[...] + (1.0 - ADAM_B1) * gg
        vv = ADAM_B2 * v_ref[...] + (1.0 - ADAM_B2) * (gg * gg)
        nm_ref[...] = mm
        nv_ref[...] = vv
        d_ref[...] = -ADAM_LR * ((mm / c1) / (jnp.sqrt(vv / c2) + ADAM_EPS) + ADAM_WD * w_ref[...])

    if lead:
        spec = pl.BlockSpec((1, tr, C), lambda l, i: (l, i, 0))
    else:
        spec = pl.BlockSpec((tr, C), lambda l, i: (i, 0))
    return pl.pallas_call(
        body, name=name, grid=(lead[0] if lead else 1, R // tr),
        in_specs=[spec] * 4, out_specs=[spec] * 3,
        out_shape=[jax.ShapeDtypeStruct(g.shape, F32)] * 3,
        compiler_params=_cp("parallel", "parallel"),
    )(g, w, m, v)


BIG = (("ffn1_w_gate", 1, D, FF), ("ffn1_w_up", 1, D, FF), ("ffn1_w_down", 0, FF, D),
       ("w_in", 1, D, D_IN), ("w_out", 0, D, D),
       ("ffn2_w_gate", 1, D, FF), ("ffn2_w_up", 1, D, FF), ("ffn2_w_down", 0, FF, D),
       ("ple_w_gate", 0, D, D), ("ple_w_proj", 1, PLE, D))
SMALL = (("ffn1_norm", (D,), None), ("mix_norm", (D,), None), ("lru_conv_w", (4, LRU_W), LRU_W // N_DEV),
         ("lru_conv_b", (LRU_W,), None), ("lru_w_a", (4, HD, HD), None), ("lru_b_a", (LRU_W,), None),
         ("lru_w_x", (4, HD, HD), None), ("lru_b_x", (LRU_W,), None), ("lru_lambda", (LRU_W,), None),
         ("attn_sinks", (ATT_H,), None), ("dn_conv_w", (4, 3 * DN_W), 3 * DN_W // N_DEV),
         ("dn_a_log", (DN_H,), None), ("dn_dt_bias", (DN_H,), None), ("dn_norm", (HD,), None),
         ("ffn2_norm", (D,), None), ("ple_norm", (D,), None))
SINGLE = (("rel_bias", (REL_BUCKETS, ATT_H)), ("final_norm", (D,)))


def _pack_rows(arrs, width, mult):
    flat = jnp.concatenate([a.reshape(-1) for a in arrs])
    rows = -(-flat.shape[0] // (width * mult)) * mult
    return jnp.pad(flat, (0, rows * width - flat.shape[0])).reshape(rows, width)


def _unpack_rows(packed, shapes):
    flat = packed.reshape(-1)
    out, off = [], 0
    for s in shapes:
        n = int(np.prod(s))
        out.append(flat[off:off + n].reshape(s))
        off += n
    return out


def _pad_rows(w, r):
    return jnp.pad(w, ((0, r - w.shape[0]), (0, 0)))


def _shard_ffn(a, l, which, more=()):
    cols = jnp.concatenate([a[which + "_w_gate"][l], a[which + "_w_up"][l]], axis=0)
    rows = jnp.concatenate([_pad_rows(a[which + "_w_down"][l], SHP)] + list(more), axis=0)
    return jnp.pad(cols, ((0, 0), (0, SHP - SH))).astype(BF16), rows.astype(BF16)


def _shards(a, l):
    w_in_rows = _pad_rows(a["w_in"][l].reshape(WIN_ROWS, D), IN_ROWS).astype(BF16)
    rest = _shard_ffn(a, l, "ffn2", (a["w_out"][l], a["ple_w_gate"][l], a["ple_w_proj"][l].reshape(-1, D)))
    return _shard_ffn(a, l, "ffn1"), (w_in_rows,), rest


def _full_w_in(in_rows):
    sh = in_rows[:, :WIN_ROWS, :].reshape(N_DEV, D, D_IN // N_DEV)
    return jnp.pad(sh.transpose(1, 0, 2).reshape(D, D_IN), ((0, 0), (0, D_IN_PAD - D_IN)))


def _full_ple_proj(r_rows):
    sh = r_rows[:, R_PPROJ:R_ROWS, :].reshape(N_DEV, PLE, D // N_DEV)
    return sh.transpose(1, 0, 2).reshape(PLE, D)


PIECE_NAMES = {"ffn1": ("ffn1_w_gate", "ffn1_w_up", "ffn1_w_down"), "ffn2": ("ffn2_w_gate", "ffn2_w_up", "ffn2_w_down"),
               "mix": ("w_out", "ple_w_gate", "w_in", "ple_w_proj")}


def _shard_grads(piece, summed):
    if piece == "mix":
        rows, = summed
        return {"w_out": rows[GM_WOUT:GM_WOUT + ROWS_DEV], "ple_w_gate": rows[GM_PGATE:GM_PGATE + ROWS_DEV],
                "w_in": rows[GM_WIN:GM_WIN + WIN_ROWS].reshape(D, D_IN // N_DEV),
                "ple_w_proj": rows[GM_PPROJ:GM_END].reshape(PLE, D // N_DEV)}
    if piece in ("ffn1_gate", "ffn1_up"):
        return {piece.replace("_", "_w_"): summed[0][:, :SH]}
    if piece == "ffn1_down":
        return {"ffn1_w_down": summed[0][:SH]}
    cols, rows = summed
    return {piece + "_w_gate": cols[:D, :SH], piece + "_w_up": cols[D:, :SH], piece + "_w_down": rows[:SH]}


def kernel(x, p, ffn1_norm, ffn1_w_gate, ffn1_w_up, ffn1_w_down, mix_norm, w_in, lru_conv_w, lru_conv_b, lru_w_a, lru_b_a, lru_w_x, lru_b_x, lru_lambda, attn_sinks, rel_bias, dn_conv_w, dn_a_log, dn_dt_bias, dn_norm, w_out, ffn2_norm, ffn2_w_gate, ffn2_w_up, ffn2_w_down, ple_norm, ple_w_gate, ple_w_proj, final_norm, loss_target, m_ffn1_norm, m_ffn1_w_gate, m_ffn1_w_up, m_ffn1_w_down, m_mix_norm, m_w_in, m_lru_conv_w, m_lru_conv_b, m_lru_w_a, m_lru_b_a, m_lru_w_x, m_lru_b_x, m_lru_lambda, m_attn_sinks, m_rel_bias, m_dn_conv_w, m_dn_a_log, m_dn_dt_bias, m_dn_norm, m_w_out, m_ffn2_norm, m_ffn2_w_gate, m_ffn2_w_up, m_ffn2_w_down, m_ple_norm, m_ple_w_gate, m_ple_w_proj, m_final_norm, v_ffn1_norm, v_ffn1_w_gate, v_ffn1_w_up, v_ffn1_w_down, v_mix_norm, v_w_in, v_lru_conv_w, v_lru_conv_b, v_lru_w_a, v_lru_b_a, v_lru_w_x, v_lru_b_x, v_lru_lambda, v_attn_sinks, v_rel_bias, v_dn_conv_w, v_dn_a_log, v_dn_dt_bias, v_dn_norm, v_w_out, v_ffn2_norm, v_ffn2_w_gate, v_ffn2_w_up, v_ffn2_w_down, v_ple_norm, v_ple_w_gate, v_ple_w_proj, v_final_norm):
    a = dict(locals())
    nb, S, _ = x.shape
    T = nb * S
    my_slot = _slot_of(*_my_pos())

    W = {k: [None] * DEPTH for k in ("f1_cols", "f1_rows", "w_in", "r_cols", "r_rows", "ple_w_proj")}
    GATHER, SCATTER = ("gather_cols", "gather_block"), ("scatter_cols", "scatter_block")
    GROUP_MODES = {"f1": GATHER, "in": GATHER[1:], "rest": GATHER}

    def set_group(l, group, bufs):
        if group == "f1":
            W["f1_cols"][l], W["f1_rows"][l] = bufs
        elif group == "in":
            W["w_in"][l] = _full_w_in(bufs[0])
        else:
            W["r_cols"][l], W["r_rows"][l] = bufs
            W["ple_w_proj"][l] = _full_ple_proj(bufs[1])

    def landing(mode, src):
        if mode == "gather_cols":
            return lax.empty((src.shape[0], FFP), src.dtype)
        if mode == "scatter_cols":
            return lax.empty((N_DEV, src.shape[0], SHP), src.dtype)
        return lax.empty((N_DEV,) + src.shape[mode == "scatter_block":], src.dtype)

    def start(modes, srcs, name, after=()):
        return split_start(modes, srcs, [landing(m, s) for m, s in zip(modes, srcs)], name, after)

    shards0, shards1 = _shards(a, 0), _shards(a, 1)
    set_group(0, "f1", gather_layer(*shards0[0], "gather_weights_l0_ffn1"))
    taps = all_gather(_pack_rows([lru_conv_w, dn_conv_w], LANE, 8), "gather_conv_taps")
    flat_taps = taps.reshape(N_DEV, -1)
    for name, first, tap in (("lru_conv_w", 0, lru_conv_w), ("dn_conv_w", lru_conv_w.size, dn_conv_w)):
        per_dev = flat_taps[:, first:first + tap.size].reshape((N_DEV,) + tap.shape)
        W[name] = jnp.moveaxis(per_dev, 0, -2).reshape(tap.shape[:-1] + (N_DEV * tap.shape[-1],))
    for name, _, cols in SMALL:
        if cols is None:
            W[name] = a[name]
    W["rel_bias"], W["final_norm"] = rel_bias, final_norm

    gathers, after = {}, (W["f1_rows"][0], taps)
    for l, group, srcs in ((0, "in", shards0[1]), (0, "rest", shards0[2]),
                           (1, "f1", shards1[0]), (1, "in", shards1[1]), (1, "rest", shards1[2])):
        gathers[l, group], token = start(GROUP_MODES[group], srcs, f"gather_start_l{l}_{group}", after)
        after = (token,)
    W["ffn1_norm"] = ffn1_norm + token[0, 0]
    flight, tokens = {}, {}

    def need(l, group, h):
        if (l, group) in gathers:
            set_group(l, group, split_wait(GROUP_MODES[group], gathers[l, group], h, f"gather_wait_l{l}_{group}"))

    def piece_modes(piece):
        return {"mix": SCATTER[1:], "ffn1_gate": SCATTER[:1], "ffn1_up": SCATTER[:1], "ffn1_down": SCATTER[1:]}.get(
            piece, SCATTER)

    def on_piece(l, piece, bufs):
        bufs = [b.reshape(-1, FFP) if b.shape[-1] == FFP else b for b in bufs]
        flight[l, piece], tokens[l, piece] = start(piece_modes(piece), bufs, f"exchange_start_l{l}_{piece}")
        return tokens[l, piece][0, 0]

    loss_local, dx, grads, d_final = _core(x.reshape(T, D), p.reshape(DEPTH, T, PLE), W,
                                           loss_target.reshape(T, D), S, need, on_piece)
    loss = lax.psum(loss_local, AXES)

    small_full = [jnp.stack([grads[l][name] for l in range(DEPTH)]) for name, _, _ in SMALL]
    small_full += [grads[0]["rel_bias"] + grads[1]["rel_bias"], d_final]
    small_flight, _ = start(("gather_block",), (_pack_rows(small_full, LANE, 8),), "gather_start_small_grads",
                            (tokens[0, "ffn1_down"],))

    out = {}

    shards = [{} for _ in range(DEPTH)]

    def land(l, piece, after):
        parts = split_wait(piece_modes(piece), flight[l, piece], after, f"exchange_wait_l{l}_{piece}")
        shards[l].update(_shard_grads(piece, [sum_parts(t, f"sum_grads_l{l}_{piece}_{i}")
                                              for i, t in enumerate(parts)]))

    def update(piece):
        for name in PIECE_NAMES[piece]:
            g = jnp.stack([shards[l][name] for l in range(DEPTH)])
            out[name] = (g,) + tuple(adamw(g, a[name], a["m_" + name], a["v_" + name], "adamw_" + name))

    for piece in ("ffn2", "mix", "ffn1"):
        land(1, piece, (dx, tokens[0, "ffn1_down"]))
    summed1 = tuple(shards[1][PIECE_NAMES[piece][0]] for piece in PIECE_NAMES)
    land(0, "ffn2", summed1)
    land(0, "mix", summed1)
    update("ffn2")
    update("mix")
    done_early = tuple(out[n][1] for n in PIECE_NAMES["ffn2"] + PIECE_NAMES["mix"])
    small_parts, = split_wait(("gather_block",), small_flight, done_early, "gather_wait_small_grads")
    small_sum = sum_parts(small_parts, "sum_small_grads")
    g_small = dict(zip([n for n, _, _ in SMALL] + [n for n, _ in SINGLE],
                       _unpack_rows(small_sum, [s.shape for s in small_full])))
    for name, _, cols in SMALL:
        if cols is not None:
            g_small[name] = lax.dynamic_slice_in_dim(g_small[name], my_slot * cols, cols, axis=2)

    for n in [n for n, _, _ in SMALL] + [n for n, _ in SINGLE]:
        shape = a[n].shape
        flat = lambda t: t.reshape((-1, shape[-1]) if len(shape) > 1 else (1, -1))
        res = adamw(flat(g_small[n]), flat(a[n]), flat(a["m_" + n]), flat(a["v_" + n]), "adamw_" + n)
        out[n] = (g_small[n].reshape(shape),) + tuple(r.reshape(shape) for r in res)

    for piece in ("ffn1_gate", "ffn1_up", "ffn1_down"):
        land(0, piece, (out["final_norm"][1],) + done_early)
    update("ffn1")

    order = ['ffn1_norm', 'ffn1_w_gate', 'ffn1_w_up', 'ffn1_w_down', 'mix_norm', 'w_in', 'lru_conv_w', 'lru_conv_b',
             'lru_w_a', 'lru_b_a', 'lru_w_x', 'lru_b_x', 'lru_lambda', 'attn_sinks', 'rel_bias', 'dn_conv_w',
             'dn_a_log', 'dn_dt_bias', 'dn_norm', 'w_out', 'ffn2_norm', 'ffn2_w_gate', 'ffn2_w_up', 'ffn2_w_down',
             'ple_norm', 'ple_w_gate', 'ple_w_proj', 'final_norm']
    return (loss, dx.reshape(x.shape)) + tuple(out[n][k] for k in range(4) for n in order)
```

```python
import functools
import math

import numpy as np
import jax
import jax.numpy as jnp
from jax import lax
from jax.experimental import pallas as pl
from jax.experimental.pallas import tpu as pltpu

F32 = jnp.float32
BF16 = jnp.bfloat16
HI = lax.Precision.HIGHEST

D = 1024
DEPTH = 2
EPS = 1e-6
PLE = 256
FF = 2816
HD = 64
LRU_W = 256
LRU_C = 8.0
ATT_W = 512
ATT_H = 8
ATT_KV = 2
ATT_G = 4
KV_W = 128
WINDOW = 128
BQ = 128
REL_BUCKETS = 32
REL_MAX_DIST = 128
DN_W = 256
DN_H = 4
CHUNK = 64
D_IN = 2312
D_IN_PAD = 2432
N_DEV = 8

ADAM_LR = 0.001
ADAM_B1 = 0.9
ADAM_B2 = 0.999
ADAM_EPS = 1e-08
ADAM_WD = 0.01
ADAM_STEP = 10

LANE = 128
VMEM_LIMIT = 56 * 1024 * 1024
SH = FF // N_DEV
SHP = 384
FFP = N_DEV * SHP
FF_TILE = 2 * SHP
FF_SUB = 256
TOK_TILE = 512
R_DOWN2, R_WOUT, R_PGATE, R_PPROJ, R_ROWS = 0, 384, 512, 640, 672
WIN_ROWS = D * D_IN // N_DEV // 1024
IN_ROWS = 304
NEG = -1e30


def _cp(*sem):
    return pltpu.CompilerParams(dimension_semantics=tuple(sem), vmem_limit_bytes=VMEM_LIMIT)


def _dg(a, b, ca, cb, exact):
    dims = (((ca,), (cb,)), ((), ()))
    if exact == "f32":
        return lax.dot_general(a.astype(F32), b.astype(F32), dims, precision=HI, preferred_element_type=F32)
    if exact == "split":
        a_hi, b_hi = a.astype(BF16), b.astype(BF16)
        a_lo = (a - a_hi.astype(F32)).astype(BF16)
        b_lo = (b - b_hi.astype(F32)).astype(BF16)
        dot = lambda u, v: lax.dot_general(u, v, dims, preferred_element_type=F32)
        return dot(a_hi, b_hi) + (dot(a_hi, b_lo) + dot(a_lo, b_hi))
    return lax.dot_general(a.astype(BF16), b.astype(BF16), dims, preferred_element_type=F32)


def _make_mm(exact):
    @jax.custom_vjp
    def mm(a, b):
        return _dg(a, b, 1, 0, exact)

    @jax.custom_vjp
    def mm_nt(a, b):
        return _dg(a, b, 1, 1, exact)

    @jax.custom_vjp
    def mm_tn(a, b):
        return _dg(a, b, 0, 0, exact)

    mm.defvjp(lambda a, b: (mm(a, b), (a, b)),
              lambda r, d: (mm_nt(d, r[1]), mm_tn(r[0], d)))
    mm_nt.defvjp(lambda a, b: (mm_nt(a, b), (a, b)),
                 lambda r, d: (mm(d, r[1]), mm_tn(d, r[0])))
    mm_tn.defvjp(lambda a, b: (mm_tn(a, b), (a, b)),
                 lambda r, d: (mm_nt(r[1], d), mm(r[0], d)))
    return mm, mm_nt, mm_tn


_mm, _mm_nt, _mm_tn = _make_mm("bf16")
_mmx, _mmx_nt, _mmx_tn = _make_mm("f32")
_mm3, _mm3_nt, _mm3_tn = _make_mm("split")


def _iota(shape, dim):
    return lax.broadcasted_iota(jnp.int32, shape, dim)


def _sigmoid(x):
    return 0.5 * jnp.tanh(0.5 * x) + 0.5


def _rms(h, g):
    rstd = lax.rsqrt(jnp.mean(h * h, axis=-1, keepdims=True) + EPS)
    xhat = h * rstd
    return xhat * g, xhat, rstd


def _rms_bwd(dxn, xhat, rstd, g):
    dxhat = dxn * g
    dh = rstd * (dxhat - xhat * jnp.mean(dxhat * xhat, axis=-1, keepdims=True))
    dg = jnp.sum(dxn * xhat, axis=0, keepdims=True)
    return dh, dg


def _row_spec(tm, n):
    return pl.BlockSpec((tm, n), lambda i, *_: (i, 0))


def _full_spec(shape):
    nd = len(shape)
    return pl.BlockSpec(shape, lambda *_: (0,) * nd)


def _ffn_weight_specs():
    return [pl.BlockSpec((D, FF_TILE), lambda i, j: (0, j)),
            pl.BlockSpec((D, FF_TILE), lambda i, j: (1, j)),
            pl.BlockSpec((2, SHP, D), lambda i, j: (j, 0, 0))]


def ffn_fwd(h, g, wa, wb, name):
    T = h.shape[0]
    tm = min(TOK_TILE, T)
    nj = FFP // FF_TILE

    def body(h_ref, g_ref, wg_ref, wu_ref, wd_ref, o_ref, gt_ref, up_ref, xn_ref):
        j = pl.program_id(1)

        @pl.when(j == 0)
        def _():
            hh = h_ref[...]
            xn_ref[...] = _rms(hh, g_ref[...])[0].astype(BF16)
            o_ref[...] = hh

        blocks = [slice(c, c + FF_SUB) for c in range(0, FF_TILE, FF_SUB)]
        xn = xn_ref[...]
        wd = wd_ref[...].reshape(FF_TILE, D)
        gt = [_mm(xn, wg_ref[:, c]) for c in blocks]
        up = [_mm(xn, wu_ref[:, c]) for c in blocks]
        act = [t * _sigmoid(t) * u for t, u in zip(gt, up)]
        down = [_mm(act[k], wd[c]) for k, c in enumerate(blocks)]
        for k, c in enumerate(blocks):
            gt_ref[:, c] = gt[k].astype(BF16)
            up_ref[:, c] = up[k].astype(BF16)
        o_ref[...] += 0.5 * functools.reduce(lambda x, y: x + y, down)

    tile = pl.BlockSpec((tm, FF_TILE), lambda i, j: (i, j))
    return pl.pallas_call(
        body, name=name, grid=(T // tm, nj),
        in_specs=[pl.BlockSpec((tm, D), lambda i, j: (i, 0)),
                  pl.BlockSpec((1, D), lambda i, j: (0, 0))] + _ffn_weight_specs(),
        out_specs=[pl.BlockSpec((tm, D), lambda i, j: (i, 0)), tile, tile,
                   pl.BlockSpec((tm, D), lambda i, j: (i, 0))],
        out_shape=[jax.ShapeDtypeStruct((T, D), F32), jax.ShapeDtypeStruct((T, FFP), BF16),
                   jax.ShapeDtypeStruct((T, FFP), BF16), jax.ShapeDtypeStruct((T, D), BF16)],
        compiler_params=_cp("parallel", "arbitrary"),
    )(h, g, wa, wa, wb)


def ffn_bwd(h, dy, g, gt_saved, up_saved, wa, wb, name):
    T = h.shape[0]
    tm = min(TOK_TILE, T)
    nj = FFP // FF_TILE

    def body(h_ref, dy_ref, g_ref, gt_ref, up_ref, wg_ref, wu_ref, wd_ref,
             dh_ref, dg_ref, du_ref, a_ref, dn_ref, dxn_s, dyh_s):
        i = pl.program_id(0)
        j = pl.program_id(1)

        @pl.when(j == 0)
        def _():
            dxn_s[...] = jnp.zeros_like(dxn_s)
            dyh_s[...] = (0.5 * dy_ref[...]).astype(BF16)

        @pl.when((i == 0) & (j == 0))
        def _():
            dn_ref[...] = jnp.zeros_like(dn_ref)

        blocks = [slice(c, c + FF_SUB) for c in range(0, FF_TILE, FF_SUB)]
        wd = wd_ref[...].reshape(FF_TILE, D)
        dyh = dyh_s[...]
        gt = [gt_ref[:, c].astype(F32) for c in blocks]
        up = [up_ref[:, c].astype(F32) for c in blocks]
        da = [_mm_nt(dyh, wd[c]) for c in blocks]
        sg = [_sigmoid(t) for t in gt]
        si = [t * s for t, s in zip(gt, sg)]
        dup = [d * s for d, s in zip(da, si)]
        dgt = [d * u * (s * (1.0 + t * (1.0 - s))) for d, u, s, t in zip(da, up, sg, gt)]
        dxn = [_mm_nt(dgt[k], wg_ref[:, c]) + _mm_nt(dup[k], wu_ref[:, c]) for k, c in enumerate(blocks)]
        for k, c in enumerate(blocks):
            dg_ref[:, c] = dgt[k].astype(BF16)
            du_ref[:, c] = dup[k].astype(BF16)
            a_ref[:, c] = (si[k] * up[k]).astype(BF16)
        dxn_s[...] += functools.reduce(lambda x, y: x + y, dxn)

        @pl.when(j == nj - 1)
        def _():
            gg = g_ref[...]
            _, xhat, rstd = _rms(h_ref[...], gg)
            dh, dn = _rms_bwd(dxn_s[...], xhat, rstd, gg)
            dh_ref[...] = dy_ref[...] + dh
            dn_ref[...] += dn

    tile = pl.BlockSpec((tm, FF_TILE), lambda i, j: (i, j))
    return pl.pallas_call(
        body, name=name, grid=(T // tm, nj),
        in_specs=[pl.BlockSpec((tm, D), lambda i, j: (i, 0)),
                  pl.BlockSpec((tm, D), lambda i, j: (i, 0)),
                  pl.BlockSpec((1, D), lambda i, j: (0, 0)), tile, tile] + _ffn_weight_specs(),
        out_specs=[pl.BlockSpec((tm, D), lambda i, j: (i, 0)), tile, tile, tile,
                   pl.BlockSpec((1, D), lambda i, j: (0, 0))],
        out_shape=[jax.ShapeDtypeStruct((T, D), F32)] + [jax.ShapeDtypeStruct((T, FFP), BF16)] * 3
        + [jax.ShapeDtypeStruct((1, D), F32)],
        scratch_shapes=[pltpu.VMEM((tm, D), F32), pltpu.VMEM((tm, D), BF16)],
        compiler_params=_cp("arbitrary", "arbitrary"),
    )(h, dy, g, gt_saved, up_saved, wa, wa, wb)


def _pick(n, prefs):
    for t in prefs:
        if n % t == 0:
            return t
    return n


def _tn_body(nk, scale, out_dtype, squeeze):
    def body(a_ref, b_ref, *rest):
        o_ref, acc = rest[-2], rest[-1]
        k = pl.program_id(2)

        @pl.when(k == 0)
        def _():
            acc[...] = jnp.zeros_like(acc)

        acc[...] += _mm_tn(a_ref[...], b_ref[...])

        @pl.when(k == nk - 1)
        def _():
            res = (scale * acc[...]).astype(out_dtype)
            if squeeze:
                o_ref[0] = res
            else:
                o_ref[...] = res

    return body


def matmul_tn(a, b, name, scale=1.0, out_dtype=BF16):
    T, M = a.shape
    N = b.shape[1]
    tmm = _pick(M, (512, 256))
    tnn = _pick(N, (1024, 2432))
    tk = min(TOK_TILE, T)
    nk = T // tk
    return pl.pallas_call(
        _tn_body(nk, scale, out_dtype, False), name=name, grid=(M // tmm, N // tnn, nk),
        in_specs=[pl.BlockSpec((tk, tmm), lambda i, j, k: (k, i)),
                  pl.BlockSpec((tk, tnn), lambda i, j, k: (k, j))],
        out_specs=pl.BlockSpec((tmm, tnn), lambda i, j, k: (i, j)),
        out_shape=jax.ShapeDtypeStruct((M, N), out_dtype),
        scratch_shapes=[pltpu.VMEM((tmm, tnn), F32)],
        compiler_params=_cp("parallel", "parallel", "arbitrary"),
    )(a, b)


def grad_cols(a, b, dst, slot, name):
    T = a.shape[0]
    tmm, tnn = D, FFP // 2
    tk = min(TOK_TILE, T)
    nk = T // tk
    return pl.pallas_call(
        _tn_body(nk, 1.0, BF16, True), name=name, grid=(D // tmm, FFP // tnn, nk),
        in_specs=[pl.BlockSpec((tk, tmm), lambda i, j, k: (k, i)),
                  pl.BlockSpec((tk, tnn), lambda i, j, k: (k, j)),
                  pl.BlockSpec(memory_space=pl.ANY)],
        out_specs=pl.BlockSpec((1, tmm, tnn), lambda i, j, k: (slot, i, j)),
        out_shape=jax.ShapeDtypeStruct(dst.shape, dst.dtype),
        scratch_shapes=[pltpu.VMEM((tmm, tnn), F32)],
        input_output_aliases={2: 0},
        compiler_params=_cp("parallel", "parallel", "arbitrary"),
    )(a, b, dst)


def grad_rows(a, b, dst, row0, rows, name, scale=1.0):
    T = a.shape[0]
    tk = min(TOK_TILE, T)
    nk = T // tk
    blk = row0 // rows

    def body(a_ref, b_ref, dst_ref, o_ref, acc):
        k = pl.program_id(0)

        @pl.when(k == 0)
        def _():
            acc[...] = jnp.zeros_like(acc)

        acc[...] += _mm_tn(a_ref[...], b_ref[...])

        @pl.when(k == nk - 1)
        def _():
            o_ref[...] = (scale * acc[...]).astype(BF16).reshape(N_DEV, rows, D)

    return pl.pallas_call(
        body, name=name, grid=(nk,),
        in_specs=[pl.BlockSpec((tk, N_DEV * rows), lambda k: (k, 0)),
                  pl.BlockSpec((tk, D), lambda k: (k, 0)),
                  pl.BlockSpec(memory_space=pl.ANY)],
        out_specs=pl.BlockSpec((N_DEV, rows, D), lambda k: (0, blk, 0)),
        out_shape=jax.ShapeDtypeStruct(dst.shape, dst.dtype),
        scratch_shapes=[pltpu.VMEM((N_DEV * rows, D), F32)],
        input_output_aliases={2: 0},
        compiler_params=_cp("arbitrary"),
    )(a, b, dst)


U_SPLITS = (512, 768, 1024, 128)
U_OFFS = (0, 512, 1280, 2304)


def mixin_fwd(h, g, w_in, name):
    T = h.shape[0]
    tm = min(TOK_TILE, T)

    def body(h_ref, g_ref, w_ref, u0, u1, u2, u3, xn_ref):
        xn = _rms(h_ref[...], g_ref[...])[0].astype(BF16)
        xn_ref[...] = xn
        u = _mm(xn, w_ref[...])
        for ref, off, n in zip((u0, u1, u2, u3), U_OFFS, U_SPLITS):
            ref[...] = u[:, off:off + n]

    return pl.pallas_call(
        body, name=name, grid=(T // tm,),
        in_specs=[_row_spec(tm, D), _full_spec((1, D)), _full_spec((D, D_IN_PAD))],
        out_specs=[_row_spec(tm, n) for n in U_SPLITS] + [_row_spec(tm, D)],
        out_shape=[jax.ShapeDtypeStruct((T, n), F32) for n in U_SPLITS]
        + [jax.ShapeDtypeStruct((T, D), BF16)],
        compiler_params=_cp("parallel"),
    )(h, g, w_in)


DU_SPLITS = (256, 256, 768, 768, 256, 128)
DU_OFFS = (0, 256, 512, 1280, 2048, 2304)


def mixin_bwd(h, dh_in, g, w_in, dus, name):
    T = h.shape[0]
    tm = min(TOK_TILE, T)

    def body(h_ref, dhi_ref, g_ref, w_ref, *refs):
        dh_ref, du_ref, dn_ref = refs[-3:]

        @pl.when(pl.program_id(0) == 0)
        def _():
            dn_ref[...] = jnp.zeros_like(dn_ref)

        dxn = jnp.zeros((tm, D), F32)
        for ref, off, n in zip(refs[:-3], DU_OFFS, DU_SPLITS):
            du = ref[...]
            du_ref[:, off:off + n] = du.astype(BF16)
            dxn += _mm_nt(du, w_ref[:, off:off + n])
        gg = g_ref[...]
        _, xhat, rstd = _rms(h_ref[...], gg)
        dh, dn = _rms_bwd(dxn, xhat, rstd, gg)
        dh_ref[...] = dhi_ref[...] + dh
        dn_ref[...] += dn

    return pl.pallas_call(
        body, name=name, grid=(T // tm,),
        in_specs=[_row_spec(tm, D), _row_spec(tm, D), _full_spec((1, D)), _full_spec((D, D_IN_PAD))]
        + [_row_spec(tm, n) for n in DU_SPLITS],
        out_specs=[_row_spec(tm, D), _row_spec(tm, D_IN_PAD), _full_spec((1, D))],
        out_shape=[jax.ShapeDtypeStruct((T, D), F32), jax.ShapeDtypeStruct((T, D_IN_PAD), BF16),
                   jax.ShapeDtypeStruct((1, D), F32)],
        compiler_params=_cp("arbitrary"),
    )(h, dh_in, g, w_in, *dus)


def _shift_down(x, s, row):
    if s == 0:
        return x
    return jnp.where(row >= s, pltpu.roll(x, s, 0), 0.0)


def _shift_up(x, s, row):
    if s == 0:
        return x
    n = x.shape[0]
    return jnp.where(row < n - s, pltpu.roll(x, n - s, 0), 0.0)


def conv_fwd(x, w, b, S, col0, C, name):
    T = x.shape[0]
    cb0 = col0 // LANE

    def body(x_ref, w_ref, b_ref, y_ref):
        xx = x_ref[...]
        row = _iota(xx.shape, 0)
        y = xx * w_ref[3:4, :] + b_ref[...]
        for k in range(3):
            y += _shift_down(xx, 3 - k, row) * w_ref[k:k + 1, :]
        y_ref[...] = y

    return pl.pallas_call(
        body, name=name, grid=(T // S, C // LANE),
        in_specs=[pl.BlockSpec((S, LANE), lambda s, c: (s, cb0 + c)),
                  pl.BlockSpec((4, LANE), lambda s, c: (0, c)),
                  pl.BlockSpec((1, LANE), lambda s, c: (0, c))],
        out_specs=pl.BlockSpec((S, LANE), lambda s, c: (s, c)),
        out_shape=jax.ShapeDtypeStruct((T, C), F32),
        compiler_params=_cp("parallel", "parallel"),
    )(x, w, b)


def conv_bwd(x, dy, w, S, col0, C, name):
    T = x.shape[0]
    cb0 = col0 // LANE

    def body(x_ref, dy_ref, w_ref, dx_ref, dwb_ref):
        @pl.when(pl.program_id(1) == 0)
        def _():
            dwb_ref[...] = jnp.zeros_like(dwb_ref)

        xx = x_ref[...]
        dd = dy_ref[...]
        row = _iota(xx.shape, 0)
        dx = dd * w_ref[3:4, :]
        for k in range(3):
            dx += _shift_up(dd, 3 - k, row) * w_ref[k:k + 1, :]
        dx_ref[...] = dx
        for k in range(4):
            dwb_ref[k:k + 1, :] += jnp.sum(dd * _shift_down(xx, 3 - k, row), axis=0, keepdims=True)
        dwb_ref[4:5, :] += jnp.sum(dd, axis=0, keepdims=True)

    return pl.pallas_call(
        body, name=name, grid=(C // LANE, T // S),
        in_specs=[pl.BlockSpec((S, LANE), lambda c, s: (s, cb0 + c)),
                  pl.BlockSpec((S, LANE), lambda c, s: (s, c)),
                  pl.BlockSpec((4, LANE), lambda c, s: (0, c))],
        out_specs=[pl.BlockSpec((S, LANE), lambda c, s: (s, c)),
                   pl.BlockSpec((8, LANE), lambda c, s: (0, c))],
        out_shape=[jax.ShapeDtypeStruct((T, C), F32), jax.ShapeDtypeStruct((8, C), F32)],
        compiler_params=_cp("parallel", "arbitrary"),
    )(x, dy, w)


def _scan(a, b, row):
    n = a.shape[0]
    d = 1
    while d < n:
        keep = row >= d
        b = a * jnp.where(keep, pltpu.roll(b, d, 0), 0.0) + b
        a = a * jnp.where(keep, pltpu.roll(a, d, 0), 1.0)
        d *= 2
    return b


def _rscan(a, b, row):
    n = a.shape[0]
    d = 1
    while d < n:
        keep = row < n - d
        b = a * jnp.where(keep, pltpu.roll(b, n - d, 0), 0.0) + b
        a = a * jnp.where(keep, pltpu.roll(a, n - d, 0), 1.0)
        d *= 2
    return b


GELU_C = math.sqrt(2.0 / math.pi)


def _gelu(x):
    t = jnp.tanh(GELU_C * (x + 0.044715 * (x * x * x)))
    return 0.5 * x * (1.0 + t), t


def _lru_gates(xr, wa, ba, wx, bx, lam):
    r = _sigmoid(_mm(xr, wa) + ba)
    i = _sigmoid(_mm(xr, wx) + bx)
    sp = jnp.maximum(-lam, 0.0) + jnp.log(1.0 + jnp.exp(-jnp.abs(lam)))
    la = -LRU_C * r * sp
    a = jnp.exp(la)
    e2 = a * a
    m = jnp.sqrt(-jnp.tanh(la) * (e2 + 1.0))
    return r, i, sp, a, e2, m


def lru_fwd(xr, u_lru, wa, wx, vec, S, name):
    T = xr.shape[0]

    def body(xr_ref, gt_ref, wa_ref, wx_ref, vec_ref, y_ref):
        x = xr_ref[...]
        row = _iota(x.shape, 0)
        r, i, sp, a, e2, m = _lru_gates(x, wa_ref[...], vec_ref[0:1, :], wx_ref[...], vec_ref[1:2, :],
                                        vec_ref[2:3, :])
        hh = _scan(a, m * (i * x), row)
        y_ref[...] = _gelu(gt_ref[...])[0] * hh

    return pl.pallas_call(
        body, name=name, grid=(T // S, LRU_W // LANE),
        in_specs=[pl.BlockSpec((S, LANE), lambda s, c: (s, c)),
                  pl.BlockSpec((S, LANE), lambda s, c: (s, 2 + c)),
                  pl.BlockSpec((LANE, LANE), lambda s, c: (c, c)),
                  pl.BlockSpec((LANE, LANE), lambda s, c: (c, c)),
                  pl.BlockSpec((8, LANE), lambda s, c: (0, c))],
        out_specs=pl.BlockSpec((S, LANE), lambda s, c: (s, c)),
        out_shape=jax.ShapeDtypeStruct((T, LRU_W), F32),
        compiler_params=_cp("parallel", "parallel"),
    )(xr, u_lru, wa, wx, vec)


def lru_bwd(xr, u_lru, dy, wa, wx, vec, S, name):
    T = xr.shape[0]

    def body(xr_ref, gt_ref, dy_ref, wa_ref, wx_ref, vec_ref,
             dxr_ref, dgt_ref, dwa_ref, dwx_ref, dvec_ref):
        @pl.when(pl.program_id(1) == 0)
        def _():
            dwa_ref[...] = jnp.zeros_like(dwa_ref)
            dwx_ref[...] = jnp.zeros_like(dwx_ref)
            dvec_ref[...] = jnp.zeros_like(dvec_ref)

        x = xr_ref[...]
        n = x.shape[0]
        row = _iota(x.shape, 0)
        lam = vec_ref[2:3, :]
        r, i, sp, a, e2, m = _lru_gates(x, wa_ref[...], vec_ref[0:1, :], wx_ref[...], vec_ref[1:2, :], lam)
        v = i * x
        hh = _scan(a, m * v, row)
        gt = gt_ref[...]
        dy = dy_ref[...]
        ge, t = _gelu(gt)
        dgt_ref[...] = dy * hh * (0.5 * (1.0 + t) + 0.5 * gt * (1.0 - t * t) * GELU_C
                                  * (1.0 + 3.0 * 0.044715 * gt * gt))
        a_next = jnp.where(row < n - 1, pltpu.roll(a, n - 1, 0), 0.0)
        G = _rscan(a_next, dy * ge, row)
        da = G * _shift_down(hh, 1, row)
        dv = G * m
        dla = da * a - (G * v) * e2 / m
        dr = dla * (-LRU_C * sp)
        dsp = jnp.sum(dla * (-LRU_C * r), axis=0, keepdims=True)
        dra = dr * r * (1.0 - r)
        dia = (dv * x) * i * (1.0 - i)
        dxr_ref[...] = dv * i + _mm_nt(dra, wa_ref[...]) + _mm_nt(dia, wx_ref[...])
        dwa_ref[0] += _mm_tn(x, dra)
        dwx_ref[0] += _mm_tn(x, dia)
        dvec_ref[0:1, :] += jnp.sum(dra, axis=0, keepdims=True)
        dvec_ref[1:2, :] += jnp.sum(dia, axis=0, keepdims=True)
        dvec_ref[2:3, :] += dsp * (-_sigmoid(-lam))

    return pl.pallas_call(
        body, name=name, grid=(LRU_W // LANE, T // S),
        in_specs=[pl.BlockSpec((S, LANE), lambda c, s: (s, c)),
                  pl.BlockSpec((S, LANE), lambda c, s: (s, 2 + c)),
                  pl.BlockSpec((S, LANE), lambda c, s: (s, c)),
                  pl.BlockSpec((LANE, LANE), lambda c, s: (c, c)),
                  pl.BlockSpec((LANE, LANE), lambda c, s: (c, c)),
                  pl.BlockSpec((8, LANE), lambda c, s: (0, c))],
        out_specs=[pl.BlockSpec((S, LANE), lambda c, s: (s, c)),
                   pl.BlockSpec((S, LANE), lambda c, s: (s, c)),
                   pl.BlockSpec((1, LANE, LANE), lambda c, s: (c, 0, 0)),
                   pl.BlockSpec((1, LANE, LANE), lambda c, s: (c, 0, 0)),
                   pl.BlockSpec((8, LANE), lambda c, s: (0, c))],
        out_shape=[jax.ShapeDtypeStruct((T, LRU_W), F32), jax.ShapeDtypeStruct((T, LRU_W), F32),
                   jax.ShapeDtypeStruct((2, LANE, LANE), F32), jax.ShapeDtypeStruct((2, LANE, LANE), F32),
                   jax.ShapeDtypeStruct((8, LRU_W), F32)],
        compiler_params=_cp("parallel", "arbitrary"),
    )(xr, u_lru, dy, wa, wx, vec)


def _bucket_table():
    qi = np.arange(BQ)[:, None]
    kj = np.arange(2 * BQ)[None, :]
    dist = BQ + qi - kj
    band = (dist >= 0) & (dist < WINDOW)
    dd = np.maximum(dist, 0)
    max_exact = REL_BUCKETS // 2
    large = max_exact + (np.log(np.maximum(dd, 1).astype(np.float32) / np.float32(max_exact))
                         / np.float32(math.log(REL_MAX_DIST / max_exact))
                         * np.float32(REL_BUCKETS - max_exact)).astype(np.int32)
    large = np.minimum(large, REL_BUCKETS - 1)
    bucket = np.where(dd < max_exact, dd, large)
    return np.where(band, bucket, -1).astype(np.int32)


def _att_specs(S):
    nb = S // BQ
    qc = ATT_W // LANE
    return [pl.BlockSpec((BQ, ATT_W), lambda b, n: (b * nb + n, 0)),
            pl.BlockSpec((BQ, KV_W), lambda b, n: (b * nb + jnp.maximum(n - 1, 0), qc)),
            pl.BlockSpec((BQ, KV_W), lambda b, n: (b * nb + n, qc)),
            pl.BlockSpec((BQ, KV_W), lambda b, n: (b * nb + jnp.maximum(n - 1, 0), qc + 1)),
            pl.BlockSpec((BQ, KV_W), lambda b, n: (b * nb + n, qc + 1))]


def _att_bias(bk, rb_ref, bias_s):
    for h in range(ATT_H):
        acc = jnp.zeros(bk.shape, F32)
        for bb in range(REL_BUCKETS):
            acc = jnp.where(bk == bb, rb_ref[bb * ATT_H + h], acc)
        bias_s[h] = acc


def _att_probs(qs, kgs, bias_s, valid, sk_ref):
    heads = range(ATT_H)
    s = [_mm_nt(qs[h], kgs[h // ATT_G]) for h in heads]
    s = [jnp.where(valid, s[h] * (HD ** -0.5) + bias_s[h], NEG) for h in heads]
    m = [jnp.maximum(jnp.max(s[h], axis=-1, keepdims=True), sk_ref[h]) for h in heads]
    e = [jnp.exp(s[h] - m[h]) for h in heads]
    es = [jnp.exp(sk_ref[h] - m[h]) for h in heads]
    den = [jnp.sum(e[h], axis=-1, keepdims=True) + es[h] for h in heads]
    return [e[h] / den[h] for h in heads], [es[h] / den[h] for h in heads]


def _att_kv(kp_ref, kc_ref, vp_ref, vc_ref):
    cat = lambda a, b, g: jnp.concatenate([a[:, HD * g:HD * (g + 1)], b[:, HD * g:HD * (g + 1)]], axis=0)
    return ([cat(kp_ref, kc_ref, g) for g in range(ATT_KV)], [cat(vp_ref, vc_ref, g) for g in range(ATT_KV)])


def attn_fwd(u_att, sinks, rel_bias, S, name):
    T = u_att.shape[0]
    nb = S // BQ
    table = jnp.asarray(_bucket_table())

    def body(sk_ref, rb_ref, bk_ref, q_ref, kp_ref, kc_ref, vp_ref, vc_ref, o_ref, bias_s):
        b = pl.program_id(0)
        n = pl.program_id(1)
        bk = bk_ref[...]

        @pl.when((b == 0) & (n == 0))
        def _():
            _att_bias(bk, rb_ref, bias_s)

        valid = (bk >= 0) & ((n > 0) | (_iota(bk.shape, 1) >= BQ))
        kgs, vgs = _att_kv(kp_ref, kc_ref, vp_ref, vc_ref)
        p, _ = _att_probs([q_ref[:, HD * h:HD * (h + 1)] for h in range(ATT_H)], kgs, bias_s, valid, sk_ref)
        outs = [_mm(p[h], vgs[h // ATT_G]) for h in range(ATT_H)]
        for h in range(ATT_H):
            o_ref[:, HD * h:HD * (h + 1)] = outs[h]

    smem = pl.BlockSpec(memory_space=pltpu.SMEM)
    return pl.pallas_call(
        body, name=name, grid=(T // S, nb),
        in_specs=[smem, smem, _full_spec((BQ, 2 * BQ))] + _att_specs(S),
        out_specs=pl.BlockSpec((BQ, ATT_W), lambda b, n: (b * nb + n, 0)),
        out_shape=jax.ShapeDtypeStruct((T, ATT_W), F32),
        scratch_shapes=[pltpu.VMEM((ATT_H, BQ, 2 * BQ), F32)],
        compiler_params=_cp("arbitrary", "arbitrary"),
    )(sinks, rel_bias, table, u_att, u_att, u_att, u_att, u_att)


def attn_bwd(u_att, dy, sinks, rel_bias, S, name):
    T = u_att.shape[0]
    nb = S // BQ
    nB = T // S
    table = jnp.asarray(_bucket_table())
    scale = HD ** -0.5

    def body(sk_ref, rb_ref, bk_ref, q_ref, kp_ref, kc_ref, vp_ref, vc_ref, dy_ref,
             du_ref, drel_ref, dsk_ref, bias_s, dbias_s):
        b = pl.program_id(0)
        n = pl.program_id(1)
        bk = bk_ref[...]

        @pl.when((b == 0) & (n == 0))
        def _():
            _att_bias(bk, rb_ref, bias_s)
            dbias_s[...] = jnp.zeros_like(dbias_s)
            dsk_ref[...] = jnp.zeros_like(dsk_ref)
            drel_ref[...] = jnp.zeros_like(drel_ref)

        @pl.when(n == 0)
        def _():
            du_ref[...] = jnp.zeros_like(du_ref)

        valid = (bk >= 0) & ((n > 0) | (_iota(bk.shape, 1) >= BQ))
        r_cur = pl.multiple_of(n * BQ, BQ)
        r_prev = pl.multiple_of(jnp.maximum(n - 1, 0) * BQ, BQ)
        heads = range(ATT_H)
        kgs, vgs = _att_kv(kp_ref, kc_ref, vp_ref, vc_ref)
        qs = [q_ref[:, HD * h:HD * (h + 1)] for h in heads]
        dos = [dy_ref[:, HD * h:HD * (h + 1)] for h in heads]
        p, ps = _att_probs(qs, kgs, bias_s, valid, sk_ref)
        dp = [_mm_nt(dos[h], vgs[h // ATT_G]) for h in heads]
        delta = [jnp.sum(p[h] * dp[h], axis=-1, keepdims=True) for h in heads]
        ds = [p[h] * (dp[h] - delta[h]) for h in heads]
        dss = [ds[h] * scale for h in heads]
        dq = [_mm(dss[h], kgs[h // ATT_G]) for h in heads]
        dks = [_mm_tn(dss[h], qs[h]) for h in heads]
        dvs = [_mm_tn(p[h], dos[h]) for h in heads]
        for h in heads:
            dbias_s[h] += ds[h]
            dsk_ref[h:h + 1, :] += jnp.broadcast_to(jnp.sum(-ps[h] * delta[h], axis=0, keepdims=True), (1, LANE))
            du_ref[pl.ds(r_cur, BQ), HD * h:HD * (h + 1)] = dq[h]
        for g in range(ATT_KV):
            of_group = range(g * ATT_G, (g + 1) * ATT_G)
            dk = functools.reduce(lambda x, y: x + y, [dks[h] for h in of_group])
            dv = functools.reduce(lambda x, y: x + y, [dvs[h] for h in of_group])
            ck = ATT_W + HD * g
            cv = ATT_W + KV_W + HD * g
            du_ref[pl.ds(r_prev, BQ), ck:ck + HD] += dk[0:BQ]
            du_ref[pl.ds(r_cur, BQ), ck:ck + HD] += dk[BQ:]
            du_ref[pl.ds(r_prev, BQ), cv:cv + HD] += dv[0:BQ]
            du_ref[pl.ds(r_cur, BQ), cv:cv + HD] += dv[BQ:]

        @pl.when((b == nB - 1) & (n == nb - 1))
        def _():
            lane = _iota((1, LANE), 1)
            for h in range(ATT_H):
                db = dbias_s[h]
                acc = jnp.zeros((1, LANE), F32)
                for bb in range(REL_BUCKETS):
                    val = jnp.sum(jnp.sum(jnp.where(bk == bb, db, 0.0), axis=1, keepdims=True),
                                  axis=0, keepdims=True)
                    acc = jnp.where(lane == bb, val, acc)
                drel_ref[h:h + 1, :] = acc

    smem = pl.BlockSpec(memory_space=pltpu.SMEM)
    return pl.pallas_call(
        body, name=name, grid=(nB, nb),
        in_specs=[smem, smem, _full_spec((BQ, 2 * BQ))] + _att_specs(S)
        + [pl.BlockSpec((BQ, ATT_W), lambda b, n: (b * nb + n, 0))],
        out_specs=[pl.BlockSpec((S, ATT_W + 2 * KV_W), lambda b, n: (b, 0)),
                   _full_spec((8, LANE)), _full_spec((8, LANE))],
        out_shape=[jax.ShapeDtypeStruct((T, ATT_W + 2 * KV_W), F32),
                   jax.ShapeDtypeStruct((8, LANE), F32), jax.ShapeDtypeStruct((8, LANE), F32)],
        scratch_shapes=[pltpu.VMEM((ATT_H, BQ, 2 * BQ), F32), pltpu.VMEM((ATT_H, BQ, 2 * BQ), F32)],
        compiler_params=_cp("arbitrary", "arbitrary"),
    )(sinks, rel_bias, table, u_att, u_att, u_att, u_att, u_att, dy)


def _head_of(i):
    return lax.shift_right_logical(i, 6)


def _head_mask(shape):
    return (_head_of(_iota(shape, 0)) == _head_of(_iota(shape, 1))).astype(F32)


def _dn_point(c, uba, alog, dtb):
    s = c * _sigmoid(c)
    qt, kt, vt = s[:, 0:256], s[:, 256:512], s[:, 512:768]
    ones_bd = _head_mask((DN_W, DN_W))
    q = qt * lax.rsqrt(_mm3(qt * qt, ones_bd) + EPS) * (HD ** -0.5)
    k = kt * lax.rsqrt(_mm3(kt * kt, ones_bd) + EPS)
    sel = _head_of(_iota((LANE, DN_W), 1))
    row = _iota((LANE, DN_W), 0)
    braw = _mm3(uba, (row == sel).astype(F32))
    araw = _mm3(uba, (row == sel + DN_H).astype(F32)) + dtb
    beta = _sigmoid(braw)
    g = -jnp.exp(alog) * (jnp.maximum(araw, 0.0) + jnp.log(1.0 + jnp.exp(-jnp.abs(araw))))
    return q, k, vt, g, beta


def dn_point_fwd(c, uba, alog, dtb, name):
    T = c.shape[0]
    tm = min(TOK_TILE, T)

    def body(c_ref, u_ref, al_ref, dt_ref, *outs):
        for ref, val in zip(outs, _dn_point(c_ref[...], u_ref[...], al_ref[...], dt_ref[...])):
            ref[...] = val

    return pl.pallas_call(
        body, name=name, grid=(T // tm,),
        in_specs=[_row_spec(tm, 768), _row_spec(tm, LANE), _full_spec((1, DN_W)), _full_spec((1, DN_W))],
        out_specs=[_row_spec(tm, DN_W)] * 5,
        out_shape=[jax.ShapeDtypeStruct((T, DN_W), F32)] * 5,
        compiler_params=_cp("parallel"),
    )(c, uba, alog, dtb)


def dn_point_bwd(c, uba, alog, dtb, douts, name):
    T = c.shape[0]
    tm = min(TOK_TILE, T)

    def body(c_ref, u_ref, al_ref, dt_ref, dq, dk, dv, dg, db, dc_ref, du_ref, dvec_ref):
        @pl.when(pl.program_id(0) == 0)
        def _():
            dvec_ref[...] = jnp.zeros_like(dvec_ref)

        _, vjp = jax.vjp(_dn_point, c_ref[...], u_ref[...], al_ref[...], dt_ref[...])
        dc, du, dal, ddt = vjp((dq[...], dk[...], dv[...], dg[...], db[...]))
        dc_ref[...] = dc
        du_ref[...] = du
        fold = (_iota((LANE, DN_W), 0) == _head_of(_iota((LANE, DN_W), 1))).astype(F32)
        both = jnp.concatenate([dal, ddt, jnp.zeros((6, DN_W), F32)], axis=0)
        dvec_ref[...] += _mmx_nt(both, fold)

    return pl.pallas_call(
        body, name=name, grid=(T // tm,),
        in_specs=[_row_spec(tm, 768), _row_spec(tm, LANE), _full_spec((1, DN_W)), _full_spec((1, DN_W))]
        + [_row_spec(tm, DN_W)] * 5,
        out_specs=[_row_spec(tm, 768), _row_spec(tm, LANE), _full_spec((8, LANE))],
        out_shape=[jax.ShapeDtypeStruct((T, 768), F32), jax.ShapeDtypeStruct((T, LANE), F32),
                   jax.ShapeDtypeStruct((8, LANE), F32)],
        compiler_params=_cp("arbitrary"),
    )(c, uba, alog, dtb, *douts)


def _unit_lower_inverses(lmats):
    eye = (_iota(lmats[0].shape, 0) == _iota(lmats[0].shape, 1)).astype(F32)
    tinvs = [eye - lm for lm in lmats]
    pws = list(lmats)
    for _ in range(5):
        pws = [_mm3(pw, pw) for pw in pws]
        tinvs = [t + _mm3(t, pw) for t, pw in zip(tinvs, pws)]
    return tuple(tinvs)


def _inverse_bwd(tinv, d):
    return -_mm3_nt(_mm3_tn(tinv, d), tinv)


@jax.custom_vjp
def _tri_invs(lmats):
    return _unit_lower_inverses(lmats)


def _tri_invs_fwd(lmats):
    tinvs = _unit_lower_inverses(lmats)
    return tinvs, tinvs


_tri_invs.defvjp(_tri_invs_fwd, lambda tinvs, ds: (tuple(_inverse_bwd(t, d) for t, d in zip(tinvs, ds)),))


@jax.custom_vjp
def _tri_inv_known(lmat, tinv):
    return tinv


_tri_inv_known.defvjp(lambda lmat, tinv: (tinv, tinv),
                      lambda tinv, d: (_inverse_bwd(tinv, d), jnp.zeros_like(tinv)))


DN_SUB = 4


def _dn_stack(x):
    return jnp.concatenate([x, x, x, x], axis=0) * _head_mask((DN_W, DN_W))


def _dn_pre_inverse(q, k, v, g, beta):
    hm = _head_mask((DN_W, DN_W))
    ri = _iota((DN_W, DN_W), 0) & (CHUNK - 1)
    ci = _iota((DN_W, DN_W), 1) & (CHUNK - 1)
    tri64 = (_iota((CHUNK, CHUNK), 0) >= _iota((CHUNK, CHUNK), 1)).astype(F32)
    gc = _mm3(tri64, g)
    ks = _dn_stack(k)
    gcol = jnp.sum(_dn_stack(gc), axis=1, keepdims=True) * (1.0 / HD)
    gmat = jnp.broadcast_to(gcol, (DN_W, DN_W))
    decay = jnp.exp(jnp.minimum(gmat - gmat.T, 0.0))
    lmat = _mm_nt(_dn_stack(k * beta), ks) * decay * (hm * (ri > ci).astype(F32))
    att = _mm_nt(_dn_stack(q), ks) * decay * (hm * (ri >= ci).astype(F32))
    return lmat, att, gc


def _dn_post_inverse(q, k, v, g, beta, tinv, att, gc):
    glast = jnp.sum(g, axis=0, keepdims=True)
    eg = jnp.exp(gc)
    u = _mm(tinv, _dn_stack(v * beta))
    w = _mm(tinv, _dn_stack(k * beta * eg))
    return u, w, att, _dn_stack(q * eg), _dn_stack(k * jnp.exp(glast - gc)), jnp.exp(glast), tinv


def _dn_apply(state, prep):
    u, w, att, qe, kd, eglast, _ = prep
    vn = u - _mm(w, state)
    o4 = _mm(qe, state) + _mm(att, vn)
    o = o4[0:64] + o4[64:128] + o4[128:192] + o4[192:256]
    return o, state * eglast + _mm_tn(kd, vn)


def _dn_chunks(state, q, k, v, g, beta, knowns=None):
    n = q.shape[0] // CHUNK
    chunks = [tuple(x[c * CHUNK:(c + 1) * CHUNK] for x in (q, k, v, g, beta)) for c in range(n)]
    pre = [_dn_pre_inverse(*ch) for ch in chunks]
    if knowns is None:
        tinvs = _tri_invs(tuple(p[0] for p in pre))
    else:
        tinvs = [_tri_inv_known(p[0], known) for p, known in zip(pre, knowns)]
    preps = [_dn_post_inverse(*ch, tinv, p[1], p[2]) for ch, tinv, p in zip(chunks, tinvs, pre)]
    outs = []
    for prep in preps:
        o, state = _dn_apply(state, prep)
        outs.append(o)
    return jnp.concatenate(outs, axis=0), state, [prep[-1] for prep in preps]


def dn_scan_fwd(q, k, v, g, beta, S, name):
    T = q.shape[0]
    rows = DN_SUB * CHUNK
    ns = S // rows

    def body(q_ref, k_ref, v_ref, g_ref, b_ref, o_ref, st_ref, ti_ref, s_s):
        @pl.when(pl.program_id(1) == 0)
        def _():
            s_s[...] = jnp.zeros_like(s_s)

        st = s_s[...]
        st_ref[0] = st
        o, new, tinvs = _dn_chunks(st, q_ref[...], k_ref[...], v_ref[...], g_ref[...], b_ref[...])
        o_ref[...] = o
        for c, tinv in enumerate(tinvs):
            ti_ref[c] = tinv
        s_s[...] = new

    spec = pl.BlockSpec((rows, DN_W), lambda b, t: (b * ns + t, 0))
    return pl.pallas_call(
        body, name=name, grid=(T // S, ns),
        in_specs=[spec] * 5,
        out_specs=[spec, pl.BlockSpec((1, DN_W, DN_W), lambda b, t: (b * ns + t, 0, 0)),
                   pl.BlockSpec((DN_SUB, DN_W, DN_W), lambda b, t: (b * ns + t, 0, 0))],
        out_shape=[jax.ShapeDtypeStruct((T, DN_W), F32),
                   jax.ShapeDtypeStruct((T // rows, DN_W, DN_W), F32),
                   jax.ShapeDtypeStruct((T // CHUNK, DN_W, DN_W), F32)],
        scratch_shapes=[pltpu.VMEM((DN_W, DN_W), F32)],
        compiler_params=_cp("parallel", "arbitrary"),
    )(q, k, v, g, beta)


def dn_scan_bwd(q, k, v, g, beta, states, tinvs, do, S, name):
    T = q.shape[0]
    rows = DN_SUB * CHUNK
    ns = S // rows

    def body(q_ref, k_ref, v_ref, g_ref, b_ref, st_ref, ti_ref, do_ref, dq, dk, dv, dg, db, ds_s):
        @pl.when(pl.program_id(1) == 0)
        def _():
            ds_s[...] = jnp.zeros_like(ds_s)

        knowns = [ti_ref[c] for c in range(DN_SUB)]
        _, vjp = jax.vjp(lambda *args: _dn_chunks(*args, knowns=knowns)[:2],
                         st_ref[0], q_ref[...], k_ref[...], v_ref[...], g_ref[...], b_ref[...])
        grads = vjp((do_ref[...], ds_s[...]))
        ds_s[...] = grads[0]
        for ref, val in zip((dq, dk, dv, dg, db), grads[1:]):
            ref[...] = val

    spec = pl.BlockSpec((rows, DN_W), lambda b, t: (b * ns + ns - 1 - t, 0))
    return pl.pallas_call(
        body, name=name, grid=(T // S, ns),
        in_specs=[spec] * 5 + [pl.BlockSpec((1, DN_W, DN_W), lambda b, t: (b * ns + ns - 1 - t, 0, 0)),
                               pl.BlockSpec((DN_SUB, DN_W, DN_W), lambda b, t: (b * ns + ns - 1 - t, 0, 0)),
                               spec],
        out_specs=[spec] * 5,
        out_shape=[jax.ShapeDtypeStruct((T, DN_W), F32)] * 5,
        scratch_shapes=[pltpu.VMEM((DN_W, DN_W), F32)],
        compiler_params=_cp("parallel", "arbitrary"),
    )(q, k, v, g, beta, states, tinvs, do)


def _dn_gate(o, z, nl):
    ms = _mm3(o * o, _head_mask((DN_W, DN_W))) * (1.0 / HD)
    return o * lax.rsqrt(ms + EPS) * nl * (z * _sigmoid(z))


def dn_gate_fwd(o, u_dn, nl, name):
    T = o.shape[0]
    tm = min(TOK_TILE, T)

    def body(o_ref, z_ref, n_ref, y_ref):
        y_ref[...] = _dn_gate(o_ref[...], z_ref[...], n_ref[...])

    return pl.pallas_call(
        body, name=name, grid=(T // tm,),
        in_specs=[_row_spec(tm, DN_W), pl.BlockSpec((tm, DN_W), lambda i: (i, 3)), _full_spec((1, DN_W))],
        out_specs=_row_spec(tm, DN_W),
        out_shape=jax.ShapeDtypeStruct((T, DN_W), F32),
        compiler_params=_cp("parallel"),
    )(o, u_dn, nl)


def dn_gate_bwd(o, u_dn, nl, dy, name):
    T = o.shape[0]
    tm = min(TOK_TILE, T)

    def body(o_ref, z_ref, n_ref, dy_ref, do_ref, dz_ref, dn_ref):
        @pl.when(pl.program_id(0) == 0)
        def _():
            dn_ref[...] = jnp.zeros_like(dn_ref)

        _, vjp = jax.vjp(_dn_gate, o_ref[...], z_ref[...], n_ref[...])
        do, dz, dn = vjp(dy_ref[...])
        do_ref[...] = do
        dz_ref[...] = dz
        fold = (_iota((LANE, DN_W), 0) == (_iota((LANE, DN_W), 1) & (HD - 1))).astype(F32)
        dn_ref[...] += _mmx_nt(jnp.concatenate([dn, jnp.zeros((7, DN_W), F32)], axis=0), fold)

    return pl.pallas_call(
        body, name=name, grid=(T // tm,),
        in_specs=[_row_spec(tm, DN_W), pl.BlockSpec((tm, DN_W), lambda i: (i, 3)), _full_spec((1, DN_W)),
                  _row_spec(tm, DN_W)],
        out_specs=[_row_spec(tm, DN_W), _row_spec(tm, DN_W), _full_spec((8, LANE))],
        out_shape=[jax.ShapeDtypeStruct((T, DN_W), F32), jax.ShapeDtypeStruct((T, DN_W), F32),
                   jax.ShapeDtypeStruct((8, LANE), F32)],
        compiler_params=_cp("arbitrary"),
    )(o, u_dn, nl, dy)


Y_SPLITS = (LRU_W, ATT_W, DN_W)
Y_OFFS = (0, LRU_W, LRU_W + ATT_W)


ROWS_DEV = D // N_DEV


def _dev_rows_spec(row0):
    return pl.BlockSpec((N_DEV, ROWS_DEV, D), lambda *_: (0, row0 // ROWS_DEV, 0))


def _dev_rows(w_ref, off, n):
    return w_ref[off // ROWS_DEV:(off + n) // ROWS_DEV].reshape(n, D)


def wout_fwd(h, ys, wb, name):
    T = h.shape[0]
    tm = min(TOK_TILE, T)

    def body(h_ref, y0, y1, y2, w_ref, o_ref, yc_ref):
        acc = h_ref[...]
        for ref, off, n in zip((y0, y1, y2), Y_OFFS, Y_SPLITS):
            y = ref[...].astype(BF16)
            yc_ref[:, off:off + n] = y
            acc += _mm(y, _dev_rows(w_ref, off, n))
        o_ref[...] = acc

    return pl.pallas_call(
        body, name=name, grid=(T // tm,),
        in_specs=[_row_spec(tm, D)] + [_row_spec(tm, n) for n in Y_SPLITS] + [_dev_rows_spec(R_WOUT)],
        out_specs=[_row_spec(tm, D), _row_spec(tm, D)],
        out_shape=[jax.ShapeDtypeStruct((T, D), F32), jax.ShapeDtypeStruct((T, D), BF16)],
        compiler_params=_cp("parallel"),
    )(h, *ys, wb)


def wout_bwd(dy, wb, name):
    T = dy.shape[0]
    tm = min(TOK_TILE, T)

    def body(dy_ref, w_ref, d0, d1, d2):
        dd = dy_ref[...].astype(BF16)
        for ref, off, n in zip((d0, d1, d2), Y_OFFS, Y_SPLITS):
            ref[...] = _mm_nt(dd, _dev_rows(w_ref, off, n))

    return pl.pallas_call(
        body, name=name, grid=(T // tm,),
        in_specs=[_row_spec(tm, D), _dev_rows_spec(R_WOUT)],
        out_specs=[_row_spec(tm, n) for n in Y_SPLITS],
        out_shape=[jax.ShapeDtypeStruct((T, n), F32) for n in Y_SPLITS],
        compiler_params=_cp("parallel"),
    )(dy, wb)


def ple_fwd(h, g, pe, wg, wp, name):
    T = h.shape[0]
    tm = min(TOK_TILE, T)

    def body(h_ref, g_ref, p_ref, wg_ref, wp_ref, o_ref):
        hh = h_ref[...]
        xn = _rms(hh, g_ref[...])[0]
        o_ref[...] = hh + _sigmoid(_mm(xn, _dev_rows(wg_ref, 0, D))) * _mm(p_ref[...], wp_ref[...])

    return pl.pallas_call(
        body, name=name, grid=(T // tm,),
        in_specs=[_row_spec(tm, D), _full_spec((1, D)), _row_spec(tm, PLE), _dev_rows_spec(R_PGATE),
                  _full_spec((PLE, D))],
        out_specs=_row_spec(tm, D),
        out_shape=jax.ShapeDtypeStruct((T, D), F32),
        compiler_params=_cp("parallel"),
    )(h, g, pe, wg, wp)


def ple_bwd(h, dy, g, pe, wg, wp, name):
    T = h.shape[0]
    tm = min(TOK_TILE, T)

    def body(h_ref, dy_ref, g_ref, p_ref, wg_ref, wp_ref, dh_ref, dz_ref, dpp_ref, xn_ref, dn_ref):
        @pl.when(pl.program_id(0) == 0)
        def _():
            dn_ref[...] = jnp.zeros_like(dn_ref)

        gg = g_ref[...]
        dy = dy_ref[...]
        xn, xhat, rstd = _rms(h_ref[...], gg)
        wg = _dev_rows(wg_ref, 0, D)
        gate = _sigmoid(_mm(xn, wg))
        pp = _mm(p_ref[...], wp_ref[...])
        dz = dy * pp * gate * (1.0 - gate)
        dz_ref[...] = dz.astype(BF16)
        dpp_ref[...] = (dy * gate).astype(BF16)
        xn_ref[...] = xn.astype(BF16)
        dh, dn = _rms_bwd(_mm_nt(dz, wg), xhat, rstd, gg)
        dh_ref[...] = dy + dh
        dn_ref[...] += dn

    return pl.pallas_call(
        body, name=name, grid=(T // tm,),
        in_specs=[_row_spec(tm, D), _row_spec(tm, D), _full_spec((1, D)), _row_spec(tm, PLE),
                  _dev_rows_spec(R_PGATE), _full_spec((PLE, D))],
        out_specs=[_row_spec(tm, D), _row_spec(tm, D), _row_spec(tm, D), _row_spec(tm, D), _full_spec((1, D))],
        out_shape=[jax.ShapeDtypeStruct((T, D), F32), jax.ShapeDtypeStruct((T, D), BF16),
                   jax.ShapeDtypeStruct((T, D), BF16), jax.ShapeDtypeStruct((T, D), BF16),
                   jax.ShapeDtypeStruct((1, D), F32)],
        compiler_params=_cp("arbitrary"),
    )(h, dy, g, pe, wg, wp)


def loss_head(h, g, target, name):
    T = h.shape[0]
    tm = min(TOK_TILE, T)

    def body(h_ref, g_ref, t_ref, loss_ref, dh_ref, dn_ref):
        @pl.when(pl.program_id(0) == 0)
        def _():
            dn_ref[...] = jnp.zeros_like(dn_ref)
            loss_ref[...] = jnp.zeros_like(loss_ref)

        gg = g_ref[...]
        y, xhat, rstd = _rms(h_ref[...], gg)
        err = y - t_ref[...]
        per_tok = jnp.mean(err * err, axis=-1, keepdims=True)
        loss_ref[...] += 0.5 * jnp.sum(per_tok, axis=0, keepdims=True)
        dh, dn = _rms_bwd(err * (1.0 / D), xhat, rstd, gg)
        dh_ref[...] = dh
        dn_ref[...] += dn

    return pl.pallas_call(
        body, name=name, grid=(T // tm,),
        in_specs=[_row_spec(tm, D), _full_spec((1, D)), _row_spec(tm, D)],
        out_specs=[_full_spec((8, LANE)), _row_spec(tm, D), _full_spec((1, D))],
        out_shape=[jax.ShapeDtypeStruct((8, LANE), F32), jax.ShapeDtypeStruct((T, D), F32),
                   jax.ShapeDtypeStruct((1, D), F32)],
        compiler_params=_cp("arbitrary"),
    )(h, g, target)


def _block_diag(w):
    return jnp.einsum('hij,hk->hikj', w, jnp.eye(4, dtype=w.dtype)).reshape(LRU_W, LRU_W)


def _layer_consts(W, l):
    row = lambda v: v.reshape(1, -1)
    zeros = jnp.zeros((5, LRU_W), F32)
    return dict(
        wa=_block_diag(W["lru_w_a"][l]), wx=_block_diag(W["lru_w_x"][l]),
        lru_vec=jnp.concatenate([row(W["lru_b_a"][l]), row(W["lru_b_x"][l]), row(W["lru_lambda"][l]), zeros], 0),
        lru_cb=row(W["lru_conv_b"][l]),
        sinks=W["attn_sinks"][l], rel=W["rel_bias"].reshape(-1),
        dn_cb=jnp.zeros((1, 3 * DN_W), F32),
        alog=row(jnp.repeat(W["dn_a_log"][l], HD)), dtb=row(jnp.repeat(W["dn_dt_bias"][l], HD)),
        dn_nl=row(jnp.tile(W["dn_norm"][l], DN_H)),
    )


def _layer_fwd(h0, pe, W, l, S, need=None):
    n = f"l{l}_"
    c_ = _layer_consts(W, l)
    row = lambda v: v.reshape(1, -1)
    need = need or (lambda *_: None)
    need(l, "f1", h0)
    h1, *ffn1_kept = ffn_fwd(h0, row(W["ffn1_norm"][l]), W["f1_cols"][l], W["f1_rows"][l], n + "ffn1_fwd")
    need(l, "in", h1)
    u_lru, u_att, u_dn, u_ba, xn_mix = mixin_fwd(h1, row(W["mix_norm"][l]), W["w_in"][l], n + "mixin_fwd")
    xr = conv_fwd(u_lru, W["lru_conv_w"][l], c_["lru_cb"], S, 0, LRU_W, n + "lru_conv_fwd")
    y_lru = lru_fwd(xr, u_lru, c_["wa"], c_["wx"], c_["lru_vec"], S, n + "lru_fwd")
    y_att = attn_fwd(u_att, c_["sinks"], c_["rel"], S, n + "attn_fwd")
    cc = conv_fwd(u_dn, W["dn_conv_w"][l], c_["dn_cb"], S, 0, 3 * DN_W, n + "dn_conv_fwd")
    q, k, v, g, beta = dn_point_fwd(cc, u_ba, c_["alog"], c_["dtb"], n + "dn_point_fwd")
    o, states, tinvs = dn_scan_fwd(q, k, v, g, beta, S, n + "dn_scan_fwd")
    y_dn = dn_gate_fwd(o, u_dn, c_["dn_nl"], n + "dn_gate_fwd")
    need(l, "rest", y_dn)
    h2, ycat = wout_fwd(h1, (y_lru, y_att, y_dn), W["r_rows"][l], n + "wout_fwd")
    h3, *ffn2_kept = ffn_fwd(h2, row(W["ffn2_norm"][l]), W["r_cols"][l], W["r_rows"][l], n + "ffn2_fwd")
    h4 = ple_fwd(h3, row(W["ple_norm"][l]), pe, W["r_rows"][l], W["ple_w_proj"][l], n + "ple_fwd")
    saved = dict(ffn1=ffn1_kept, ffn2=ffn2_kept, h0=h0, h1=h1, h2=h2, h3=h3, u_lru=u_lru, u_att=u_att, u_dn=u_dn,
                 u_ba=u_ba, xn_mix=xn_mix, xr=xr, cc=cc, q=q, k=k, v=v, g=g, beta=beta, o=o, states=states, tinvs=tinvs, ycat=ycat)
    return h4, saved


GM_WOUT, GM_PGATE, GM_WIN, GM_PPROJ, GM_END, GM_ROWS = 0, 128, 256, 560, 592, 640


def _layer_bwd(dh4, sv, pe, W, l, S, token=None, on_piece=None):
    n = f"l{l}_"
    c_ = _layer_consts(W, l)
    row = lambda v: v.reshape(1, -1)
    behind = lambda v, tok: v if tok is None else v + tok.astype(v.dtype)
    on_piece = on_piece or (lambda *_: None)
    G = {"mix_rows": jnp.zeros((N_DEV, GM_ROWS, D), BF16)}
    dh3, dz, dpp, xn_p, dn = ple_bwd(sv["h3"], dh4, behind(row(W["ple_norm"][l]), token), pe, W["r_rows"][l],
                                     W["ple_w_proj"][l], n + "ple_bwd")
    G["ple_norm"] = dn[0]
    G["mix_rows"] = grad_rows(xn_p, dz, G["mix_rows"], GM_PGATE, ROWS_DEV, n + "d_ple_w_gate")
    d_proj = matmul_tn(pe, dpp, n + "d_ple_w_proj")
    d_proj = d_proj.reshape(PLE, N_DEV, D // N_DEV).transpose(1, 0, 2).reshape(N_DEV, GM_END - GM_PPROJ, D)
    G["mix_rows"] = lax.dynamic_update_slice(G["mix_rows"], d_proj, (0, GM_PPROJ, 0))

    def ffn_back(which, cols_w, rows_w, h_in, dy, tok, one_by_one):
        gt, up, xn = sv[which]
        dh, dgt, dup, act, dn_ = ffn_bwd(h_in, dy, behind(row(W[which + "_norm"][l]), tok), gt, up, cols_w, rows_w,
                                         n + which + "_bwd")
        G[which + "_norm"] = dn_[0]
        zeros_rows = jnp.zeros((N_DEV, SHP, D), BF16)
        if one_by_one:
            G[which + "_gate"] = grad_cols(xn, dgt, lax.empty((1, D, FFP), BF16), 0, n + "d_" + which + "_w_gate")
            tok = on_piece(l, which + "_gate", (G[which + "_gate"],))
            G[which + "_up"] = grad_cols(xn, dup, behind(jnp.zeros((1, D, FFP), BF16), tok), 0,
                                         n + "d_" + which + "_w_up")
            tok = on_piece(l, which + "_up", (G[which + "_up"],))
            G[which + "_down"] = grad_rows(act, dy, behind(zeros_rows, tok), 0, SHP, n + "d_" + which + "_w_down",
                                           scale=0.5)
            return dh, on_piece(l, which + "_down", (G[which + "_down"],))
        cols = grad_cols(xn, dgt, lax.empty((2, D, FFP), BF16), 0, n + "d_" + which + "_w_gate")
        G[which + "_cols"] = grad_cols(xn, dup, cols, 1, n + "d_" + which + "_w_up")
        G[which + "_rows"] = grad_rows(act, dy, lax.empty((N_DEV, SHP, D), BF16), 0, SHP,
                                       n + "d_" + which + "_w_down", scale=0.5)
        return dh, on_piece(l, which, (G[which + "_cols"], G[which + "_rows"]))

    dh2, tok = ffn_back("ffn2", W["r_cols"][l], W["r_rows"][l], sv["h2"], dh3, None, False)
    dy_lru, dy_att, dy_dn = wout_bwd(dh2, W["r_rows"][l], n + "wout_bwd")
    G["mix_rows"] = grad_rows(sv["ycat"], dh2, G["mix_rows"], GM_WOUT, ROWS_DEV, n + "d_w_out")
    do, dz_dn, dnn = dn_gate_bwd(sv["o"], sv["u_dn"], behind(c_["dn_nl"], tok), dy_dn, n + "dn_gate_bwd")
    dqkvgb = dn_scan_bwd(sv["q"], sv["k"], sv["v"], sv["g"], sv["beta"], sv["states"], sv["tinvs"], do, S,
                         n + "dn_scan_bwd")
    dcc, du_ba, dvec_dn = dn_point_bwd(sv["cc"], sv["u_ba"], c_["alog"], c_["dtb"], dqkvgb, n + "dn_point_bwd")
    dqkv, dwb_dn = conv_bwd(sv["u_dn"], dcc, W["dn_conv_w"][l], S, 0, 3 * DN_W, n + "dn_conv_bwd")
    G["dn_norm"] = dnn[0, 0:HD]
    G["dn_a_log"] = dvec_dn[0, 0:DN_H]
    G["dn_dt_bias"] = dvec_dn[1, 0:DN_H]
    G["dn_conv_w"] = dwb_dn[0:4]
    du_att, drel, dsk = attn_bwd(sv["u_att"], dy_att, c_["sinks"], c_["rel"], S, n + "attn_bwd")
    G["attn_sinks"] = dsk[:, 0]
    G["rel_bias"] = drel[:, 0:REL_BUCKETS].T
    dxr, dgt_lru, dwa, dwx, dvec = lru_bwd(sv["xr"], sv["u_lru"], dy_lru, c_["wa"], c_["wx"], c_["lru_vec"], S,
                                           n + "lru_bwd")
    dx_lru, dwb_lru = conv_bwd(sv["u_lru"], dxr, W["lru_conv_w"][l], S, 0, LRU_W, n + "lru_conv_bwd")
    diag = lambda m: jnp.stack([m[c, HD * e:HD * (e + 1), HD * e:HD * (e + 1)] for c in range(2) for e in range(2)])
    G["lru_w_a"], G["lru_w_x"] = diag(dwa), diag(dwx)
    G["lru_b_a"], G["lru_b_x"], G["lru_lambda"] = dvec[0], dvec[1], dvec[2]
    G["lru_conv_w"], G["lru_conv_b"] = dwb_lru[0:4], dwb_lru[4]
    dh1, du_cat, dn = mixin_bwd(sv["h1"], dh2, row(W["mix_norm"][l]), W["w_in"][l],
                                (dx_lru, dgt_lru, du_att, dqkv, dz_dn, du_ba), n + "mixin_bwd")
    G["mix_norm"] = dn[0]
    d_in = matmul_tn(sv["xn_mix"], du_cat, n + "d_w_in")[:, :D_IN]
    d_in = d_in.reshape(D, N_DEV, D_IN // N_DEV).transpose(1, 0, 2).reshape(N_DEV, WIN_ROWS, D)
    d_in = jnp.pad(d_in, ((0, 0), (0, GM_PPROJ - GM_WIN - WIN_ROWS), (0, 0)))
    G["mix_rows"] = lax.dynamic_update_slice(G["mix_rows"], d_in, (0, GM_WIN, 0))
    tok = on_piece(l, "mix", (G["mix_rows"],))
    dh0, tok = ffn_back("ffn1", W["f1_cols"][l], W["f1_rows"][l], sv["h0"], dh1, tok, l == 0)
    return dh0, G, tok


def _core(x, pe, W, target, S, need=None, on_piece=None):
    h = x
    saved = []
    for l in range(DEPTH):
        h, sv = _layer_fwd(h, pe[l], W, l, S, need)
        saved.append(sv)
    loss_tile, dh, dfn = loss_head(h, W["final_norm"].reshape(1, -1), target, "loss_head")
    grads = [None] * DEPTH
    token = None
    for l in reversed(range(DEPTH)):
        dh, grads[l], token = _layer_bwd(dh, saved[l], pe[l], W, l, S, token, on_piece)
    return loss_tile[0, 0], dh, grads, dfn[0]


MESH_ID = pl.DeviceIdType.MESH
ANY_SPEC = pl.BlockSpec(memory_space=pl.ANY)
AXES = ("x", "y", "c")


def _my_pos():
    return lax.axis_index("x"), lax.axis_index("y"), lax.axis_index("c")


def _slot_of(px, py, pc):
    return 4 * px + 2 * py + pc


def all_gather(x, name):
    R, C = x.shape

    def body(x_ref, out_ref, send_sems, recv_sems, local_sem):
        mx, my, mc = _my_pos()
        me, sibling = (mx, my, mc), (mx, my, 1 - mc)
        chips = [(1 - mx, my), (mx, 1 - my), (1 - mx, 1 - my)]

        def copy(k, block, to, src=None):
            dst = out_ref.at[_slot_of(*block)]
            return pltpu.make_async_remote_copy(
                src_ref=dst if src is None else src, dst_ref=dst,
                send_sem=send_sems.at[k], recv_sem=recv_sems.at[k],
                device_id=to, device_id_type=MESH_ID)

        mine = pltpu.make_async_copy(x_ref, out_ref.at[_slot_of(*me)], local_sem)
        mine.start()
        first = [copy(0, me, sibling, src=x_ref)]
        first += [copy(1 + j, me, (*chip, mc), src=x_ref) for j, chip in enumerate(chips)]
        for cp in first:
            cp.start()
        passed = [copy(4 + j, (*chip, mc), sibling) for j, chip in enumerate(chips)]
        for j, chip in enumerate(chips):
            copy(1 + j, (*chip, mc), me).wait_recv()
            passed[j].start()
        copy(0, sibling, me).wait_recv()
        for j, chip in enumerate(chips):
            copy(4 + j, (*chip, 1 - mc), me).wait_recv()
        for cp in first + passed:
            cp.wait_send()
        mine.wait()

    return pl.pallas_call(
        body, name=name,
        out_shape=jax.ShapeDtypeStruct((N_DEV, R, C), x.dtype),
        in_specs=[ANY_SPEC], out_specs=ANY_SPEC,
        scratch_shapes=[pltpu.SemaphoreType.DMA((7,)), pltpu.SemaphoreType.DMA((7,)), pltpu.SemaphoreType.DMA],
    )(x)


def _col_window(ref, slot):
    return ref.at[:, pl.ds(pl.multiple_of(slot * SHP, LANE), SHP)]


def gather_layer(a_sh, b_sh, name):
    def body(a_ref, b_ref, ao_ref, bo_ref, send_sems, recv_sems, local_sems):
        mx, my, mc = _my_pos()
        me, sibling = (mx, my, mc), (mx, my, 1 - mc)
        chips = [(1 - mx, my), (mx, 1 - my), (1 - mx, 1 - my)]

        def copies(k, block, to, own=False):
            slot = _slot_of(*block)
            dsts = (_col_window(ao_ref, slot), bo_ref.at[slot])
            srcs = (a_ref, b_ref) if own else dsts
            return [pltpu.make_async_remote_copy(
                src_ref=s, dst_ref=d, send_sem=send_sems.at[2 * k + i], recv_sem=recv_sems.at[2 * k + i],
                device_id=to, device_id_type=MESH_ID) for i, (s, d) in enumerate(zip(srcs, dsts))]

        mine = [pltpu.make_async_copy(a_ref, _col_window(ao_ref, _slot_of(*me)), local_sems.at[0]),
                pltpu.make_async_copy(b_ref, bo_ref.at[_slot_of(*me)], local_sems.at[1])]
        for cp in mine:
            cp.start()
        first = copies(0, me, sibling, own=True)
        for j, chip in enumerate(chips):
            first += copies(1 + j, me, (*chip, mc), own=True)
        for cp in first:
            cp.start()
        passed = []
        for j, chip in enumerate(chips):
            for cp in copies(1 + j, (*chip, mc), me):
                cp.wait_recv()
            fwd = copies(4 + j, (*chip, mc), sibling)
            for cp in fwd:
                cp.start()
            passed += fwd
        for cp in copies(0, sibling, me):
            cp.wait_recv()
        for j, chip in enumerate(chips):
            for cp in copies(4 + j, (*chip, 1 - mc), me):
                cp.wait_recv()
        for cp in first + passed:
            cp.wait_send()
        for cp in mine:
            cp.wait()

    return pl.pallas_call(
        body, name=name,
        out_shape=[jax.ShapeDtypeStruct((a_sh.shape[0], FFP), a_sh.dtype),
                   jax.ShapeDtypeStruct((N_DEV,) + b_sh.shape, b_sh.dtype)],
        in_specs=[ANY_SPEC, ANY_SPEC], out_specs=[ANY_SPEC, ANY_SPEC],
        scratch_shapes=[pltpu.SemaphoreType.DMA((14,)), pltpu.SemaphoreType.DMA((14,)),
                        pltpu.SemaphoreType.DMA((2,))],
    )(a_sh, b_sh)


HBM_SPEC = pl.BlockSpec(memory_space=pltpu.HBM)
SEM_SPEC = pl.BlockSpec(memory_space=pltpu.SEMAPHORE)
SPLIT_EFFECT = pltpu.CompilerParams(has_side_effects=pltpu.SideEffectType.DATAFLOW_SIDE_EFFECTING)


def _split_ends(mode, src_ref, dst_ref, src_slot, dst_slot):
    cols = mode.endswith("cols")
    if mode.startswith("gather"):
        return src_ref, (_col_window(dst_ref, dst_slot) if cols else dst_ref.at[dst_slot])
    return (_col_window(src_ref, src_slot) if cols else src_ref.at[src_slot]), dst_ref.at[dst_slot]


def _split_peers():
    mx, my, mc = _my_pos()
    for r in range(1, N_DEV):
        peer = (1 - mx if r & 4 else mx, 1 - my if r & 2 else my, 1 - mc if r & 1 else mc)
        yield r - 1, peer, _slot_of(*peer)


def split_start(modes, srcs, dsts, name, after=()):
    n = len(modes)
    m = len(after)

    def body(*refs):
        send_sems, recv_sems, token = refs[2 * n + m], refs[2 * n + m + 1], refs[-1]
        mine = _slot_of(*_my_pos())
        for k, peer, ps in _split_peers():
            for i in range(n):
                src, dst = _split_ends(modes[i], refs[i], refs[n + i], ps, mine)
                pltpu.make_async_remote_copy(
                    src_ref=src, dst_ref=dst, send_sem=send_sems.at[n * k + i], recv_sem=recv_sems.at[n * k + i],
                    device_id=peer, device_id_type=MESH_ID).start()
        for i in range(n):
            src, dst = _split_ends(modes[i], refs[i], refs[n + i], mine, mine)
            pltpu.make_async_copy(src, dst, recv_sems.at[n * (N_DEV - 1) + i]).start()
        token[...] = jnp.zeros_like(token)

    bufs = tuple(srcs) + tuple(dsts)
    sems = pltpu.SemaphoreType.DMA((n * N_DEV,))
    res = pl.pallas_call(
        body, name=name,
        out_shape=(sems, sems) + tuple(pltpu.HBM(t.shape, t.dtype) for t in bufs)
        + (jax.ShapeDtypeStruct((8, LANE), F32),),
        in_specs=[HBM_SPEC] * (2 * n) + [ANY_SPEC] * m,
        out_specs=(SEM_SPEC, SEM_SPEC) + (HBM_SPEC,) * (2 * n) + (pl.BlockSpec(memory_space=pltpu.VMEM),),
        input_output_aliases={i: 2 + i for i in range(2 * n)},
        compiler_params=SPLIT_EFFECT,
    )(*(pltpu.with_memory_space_constraint(t, pltpu.HBM) for t in bufs), *after)
    return list(res[:-1]), res[-1]


def split_wait(modes, started, after, name):
    n = len(modes)
    after = tuple(after) if isinstance(after, (tuple, list)) else (after,)
    send_sems, recv_sems, bufs = started[0], started[1], started[2:]

    def body(*refs):
        send_sems, recv_sems = refs[2 * n], refs[2 * n + 1]
        mine = _slot_of(*_my_pos())
        for k, peer, ps in _split_peers():
            for i in range(n):
                sent = _split_ends(modes[i], refs[i], refs[n + i], ps, mine)[0]
                landed = _split_ends(modes[i], refs[i], refs[n + i], mine, ps)[1]
                cp = pltpu.make_async_remote_copy(
                    src_ref=sent, dst_ref=landed, send_sem=send_sems.at[n * k + i],
                    recv_sem=recv_sems.at[n * k + i], device_id=peer, device_id_type=MESH_ID)
                cp.wait_send()
                cp.wait_recv()
        for i in range(n):
            src, dst = _split_ends(modes[i], refs[i], refs[n + i], mine, mine)
            pltpu.make_async_copy(src, dst, recv_sems.at[n * (N_DEV - 1) + i]).wait()

    res = pl.pallas_call(
        body, name=name,
        out_shape=tuple(pltpu.HBM(t.shape, t.dtype) for t in bufs),
        in_specs=[HBM_SPEC] * (2 * n) + [SEM_SPEC, SEM_SPEC] + [ANY_SPEC] * len(after),
        out_specs=(HBM_SPEC,) * (2 * n),
        input_output_aliases={i: i for i in range(2 * n)},
        compiler_params=SPLIT_EFFECT,
    )(*bufs, send_sems, recv_sems, *after)
    return list(res[n:])


def sum_parts(parts, name):
    _, R, C = parts.shape
    tr = _pick(R, (512, 336, 272, 256, 128, 64, 32, 16, 8))

    def body(p_ref, o_ref):
        acc = p_ref[0].astype(F32)
        for k in range(1, N_DEV):
            acc += p_ref[k].astype(F32)
        o_ref[...] = acc

    return pl.pallas_call(
        body, name=name, grid=(R // tr,),
        in_specs=[pl.BlockSpec((N_DEV, tr, C), lambda i: (0, i, 0))],
        out_specs=pl.BlockSpec((tr, C), lambda i: (i, 0)),
        out_shape=jax.ShapeDtypeStruct((R, C), F32),
        compiler_params=_cp("parallel"),
    )(parts)


def sum_into(parts, row0, rows, cols, layer, dst, name):
    tr = _pick(rows, (512, 352, 128))
    blk0 = row0 // tr

    def body(p_ref, *rest):
        o_ref = rest[-1]
        acc = p_ref[0, :, 0:cols].astype(F32)
        for k in range(1, N_DEV):
            acc += p_ref[k, :, 0:cols].astype(F32)
        o_ref[0] = acc

    aliased = dst is not None
    return pl.pallas_call(
        body, name=name, grid=(rows // tr,),
        in_specs=[pl.BlockSpec((N_DEV, tr, parts.shape[2]), lambda i: (0, blk0 + i, 0))]
        + [pl.BlockSpec(memory_space=pl.ANY)] * aliased,
        out_specs=pl.BlockSpec((1, tr, cols), lambda i: (layer, i, 0)),
        out_shape=jax.ShapeDtypeStruct((DEPTH, rows, cols), F32),
        input_output_aliases={1: 0} if aliased else {},
        compiler_params=_cp("parallel"),
    )(*((parts, dst) if aliased else (parts,)))


def adamw(g, w, m, v, name):
    lead, (R, C) = g.shape[:-2], g.shape[-2:]
    tr = _pick(R, (512, 352, 256, 128, 64, 32, 16, 8))
    c1 = 1.0 - ADAM_B1 ** ADAM_STEP
    c2 = 1.0 - ADAM_B2 ** ADAM_STEP

    def body(g_ref, w_ref, m_ref, v_ref, d_ref, nm_ref, nv_ref):
        gg = g_ref[...]
        mm = ADAM_B1 * m_ref[...] + (1.0 - ADAM_B1) * gg
        vv = ADAM_B2 * v_ref[...] + (1.0 - ADAM_B2) * (gg * gg)
        nm_ref[...] = mm
        nv_ref[...] = vv
        d_ref[...] = -ADAM_LR * ((mm / c1) / (jnp.sqrt(vv / c2) + ADAM_EPS) + ADAM_WD * w_ref[...])

    if lead:
        spec = pl.BlockSpec((1, tr, C), lambda l, i: (l, i, 0))
    else:
        spec = pl.BlockSpec((tr, C), lambda l, i: (i, 0))
    return pl.pallas_call(
        body, name=name, grid=(lead[0] if lead else 1, R // tr),
        in_specs=[spec] * 4, out_specs=[spec] * 3,
        out_shape=[jax.ShapeDtypeStruct(g.shape, F32)] * 3,
        compiler_params=_cp("parallel", "parallel"),
    )(g, w, m, v)


BIG = (("ffn1_w_gate", 1, D, FF), ("ffn1_w_up", 1, D, FF), ("ffn1_w_down", 0, FF, D),
       ("w_in", 1, D, D_IN), ("w_out", 0, D, D),
       ("ffn2_w_gate", 1, D, FF), ("ffn2_w_up", 1, D, FF), ("ffn2_w_down", 0, FF, D),
       ("ple_w_gate", 0, D, D), ("ple_w_proj", 1, PLE, D))
SMALL = (("ffn1_norm", (D,), None), ("mix_norm", (D,), None), ("lru_conv_w", (4, LRU_W), LRU_W // N_DEV),
         ("lru_conv_b", (LRU_W,), None), ("lru_w_a", (4, HD, HD), None), ("lru_b_a", (LRU_W,), None),
         ("lru_w_x", (4, HD, HD), None), ("lru_b_x", (LRU_W,), None), ("lru_lambda", (LRU_W,), None),
         ("attn_sinks", (ATT_H,), None), ("dn_conv_w", (4, 3 * DN_W), 3 * DN_W // N_DEV),
         ("dn_a_log", (DN_H,), None), ("dn_dt_bias", (DN_H,), None), ("dn_norm", (HD,), None),
         ("ffn2_norm", (D,), None), ("ple_norm", (D,), None))
SINGLE = (("rel_bias", (REL_BUCKETS, ATT_H)), ("final_norm", (D,)))


def _pack_rows(arrs, width, mult):
    flat = jnp.concatenate([a.reshape(-1) for a in arrs])
    rows = -(-flat.shape[0] // (width * mult)) * mult
    return jnp.pad(flat, (0, rows * width - flat.shape[0])).reshape(rows, width)


def _unpack_rows(packed, shapes):
    flat = packed.reshape(-1)
    out, off = [], 0
    for s in shapes:
        n = int(np.prod(s))
        out.append(flat[off:off + n].reshape(s))
        off += n
    return out


def _pad_rows(w, r):
    return jnp.pad(w, ((0, r - w.shape[0]), (0, 0)))


def _shard_ffn(a, l, which, more=()):
    cols = jnp.concatenate([a[which + "_w_gate"][l], a[which + "_w_up"][l]], axis=0)
    rows = jnp.concatenate([_pad_rows(a[which + "_w_down"][l], SHP)] + list(more), axis=0)
    return jnp.pad(cols, ((0, 0), (0, SHP - SH))).astype(BF16), rows.astype(BF16)


def _shards(a, l):
    w_in_rows = _pad_rows(a["w_in"][l].reshape(WIN_ROWS, D), IN_ROWS).astype(BF16)
    rest = _shard_ffn(a, l, "ffn2", (a["w_out"][l], a["ple_w_gate"][l], a["ple_w_proj"][l].reshape(-1, D)))
    return _shard_ffn(a, l, "ffn1"), (w_in_rows,), rest


def _full_w_in(in_rows):
    sh = in_rows[:, :WIN_ROWS, :].reshape(N_DEV, D, D_IN // N_DEV)
    return jnp.pad(sh.transpose(1, 0, 2).reshape(D, D_IN), ((0, 0), (0, D_IN_PAD - D_IN)))


def _full_ple_proj(r_rows):
    sh = r_rows[:, R_PPROJ:R_ROWS, :].reshape(N_DEV, PLE, D // N_DEV)
    return sh.transpose(1, 0, 2).reshape(PLE, D)


PIECE_NAMES = {"ffn1": ("ffn1_w_gate", "ffn1_w_up", "ffn1_w_down"), "ffn2": ("ffn2_w_gate", "ffn2_w_up", "ffn2_w_down"),
               "mix": ("w_out", "ple_w_gate", "w_in", "ple_w_proj")}


def _shard_grads(piece, summed):
    if piece == "mix":
        rows, = summed
        return {"w_out": rows[GM_WOUT:GM_WOUT + ROWS_DEV], "ple_w_gate": rows[GM_PGATE:GM_PGATE + ROWS_DEV],
                "w_in": rows[GM_WIN:GM_WIN + WIN_ROWS].reshape(D, D_IN // N_DEV),
                "ple_w_proj": rows[GM_PPROJ:GM_END].reshape(PLE, D // N_DEV)}
    if piece in ("ffn1_gate", "ffn1_up"):
        return {piece.replace("_", "_w_"): summed[0][:, :SH]}
    if piece == "ffn1_down":
        return {"ffn1_w_down": summed[0][:SH]}
    cols, rows = summed
    return {piece + "_w_gate": cols[:D, :SH], piece + "_w_up": cols[D:, :SH], piece + "_w_down": rows[:SH]}


def kernel(x, p, ffn1_norm, ffn1_w_gate, ffn1_w_up, ffn1_w_down, mix_norm, w_in, lru_conv_w, lru_conv_b, lru_w_a, lru_b_a, lru_w_x, lru_b_x, lru_lambda, attn_sinks, rel_bias, dn_conv_w, dn_a_log, dn_dt_bias, dn_norm, w_out, ffn2_norm, ffn2_w_gate, ffn2_w_up, ffn2_w_down, ple_norm, ple_w_gate, ple_w_proj, final_norm, loss_target, m_ffn1_norm, m_ffn1_w_gate, m_ffn1_w_up, m_ffn1_w_down, m_mix_norm, m_w_in, m_lru_conv_w, m_lru_conv_b, m_lru_w_a, m_lru_b_a, m_lru_w_x, m_lru_b_x, m_lru_lambda, m_attn_sinks, m_rel_bias, m_dn_conv_w, m_dn_a_log, m_dn_dt_bias, m_dn_norm, m_w_out, m_ffn2_norm, m_ffn2_w_gate, m_ffn2_w_up, m_ffn2_w_down, m_ple_norm, m_ple_w_gate, m_ple_w_proj, m_final_norm, v_ffn1_norm, v_ffn1_w_gate, v_ffn1_w_up, v_ffn1_w_down, v_mix_norm, v_w_in, v_lru_conv_w, v_lru_conv_b, v_lru_w_a, v_lru_b_a, v_lru_w_x, v_lru_b_x, v_lru_lambda, v_attn_sinks, v_rel_bias, v_dn_conv_w, v_dn_a_log, v_dn_dt_bias, v_dn_norm, v_w_out, v_ffn2_norm, v_ffn2_w_gate, v_ffn2_w_up, v_ffn2_w_down, v_ple_norm, v_ple_w_gate, v_ple_w_proj, v_final_norm):
    a = dict(locals())
    nb, S, _ = x.shape
    T = nb * S
    my_slot = _slot_of(*_my_pos())

    W = {k: [None] * DEPTH for k in ("f1_cols", "f1_rows", "w_in", "r_cols", "r_rows", "ple_w_proj")}
    GATHER, SCATTER = ("gather_cols", "gather_block"), ("scatter_cols", "scatter_block")
    GROUP_MODES = {"f1": GATHER, "in": GATHER[1:], "rest": GATHER}

    def set_group(l, group, bufs):
        if group == "f1":
            W["f1_cols"][l], W["f1_rows"][l] = bufs
        elif group == "in":
            W["w_in"][l] = _full_w_in(bufs[0])
        else:
            W["r_cols"][l], W["r_rows"][l] = bufs
            W["ple_w_proj"][l] = _full_ple_proj(bufs[1])

    def landing(mode, src):
        if mode == "gather_cols":
            return lax.empty((src.shape[0], FFP), src.dtype)
        if mode == "scatter_cols":
            return lax.empty((N_DEV, src.shape[0], SHP), src.dtype)
        return lax.empty((N_DEV,) + src.shape[mode == "scatter_block":], src.dtype)

    def start(modes, srcs, name, after=()):
        return split_start(modes, srcs, [landing(m, s) for m, s in zip(modes, srcs)], name, after)

    shards0, shards1 = _shards(a, 0), _shards(a, 1)
    set_group(0, "f1", gather_layer(*shards0[0], "gather_weights_l0_ffn1"))
    taps = all_gather(_pack_rows([lru_conv_w, dn_conv_w], LANE, 8), "gather_conv_taps")
    flat_taps = taps.reshape(N_DEV, -1)
    for name, first, tap in (("lru_conv_w", 0, lru_conv_w), ("dn_conv_w", lru_conv_w.size, dn_conv_w)):
        per_dev = flat_taps[:, first:first + tap.size].reshape((N_DEV,) + tap.shape)
        W[name] = jnp.moveaxis(per_dev, 0, -2).reshape(tap.shape[:-1] + (N_DEV * tap.shape[-1],))
    for name, _, cols in SMALL:
        if cols is None:
            W[name] = a[name]
    W["rel_bias"], W["final_norm"] = rel_bias, final_norm

    gathers, after = {}, (W["f1_rows"][0], taps)
    for l, group, srcs in ((0, "in", shards0[1]), (0, "rest", shards0[2]),
                           (1, "f1", shards1[0]), (1, "in", shards1[1]), (1, "rest", shards1[2])):
        gathers[l, group], token = start(GROUP_MODES[group], srcs, f"gather_start_l{l}_{group}", after)
        after = (token,)
    W["ffn1_norm"] = ffn1_norm + token[0, 0]
    flight, tokens = {}, {}

    def need(l, group, h):
        if (l, group) in gathers:
            set_group(l, group, split_wait(GROUP_MODES[group], gathers[l, group], h, f"gather_wait_l{l}_{group}"))

    def piece_modes(piece):
        return {"mix": SCATTER[1:], "ffn1_gate": SCATTER[:1], "ffn1_up": SCATTER[:1], "ffn1_down": SCATTER[1:]}.get(
            piece, SCATTER)

    def on_piece(l, piece, bufs):
        bufs = [b.reshape(-1, FFP) if b.shape[-1] == FFP else b for b in bufs]
        flight[l, piece], tokens[l, piece] = start(piece_modes(piece), bufs, f"exchange_start_l{l}_{piece}")
        return tokens[l, piece][0, 0]

    loss_local, dx, grads, d_final = _core(x.reshape(T, D), p.reshape(DEPTH, T, PLE), W,
                                           loss_target.reshape(T, D), S, need, on_piece)
    loss = lax.psum(loss_local, AXES)

    small_full = [jnp.stack([grads[l][name] for l in range(DEPTH)]) for name, _, _ in SMALL]
    small_full += [grads[0]["rel_bias"] + grads[1]["rel_bias"], d_final]
    small_flight, _ = start(("gather_block",), (_pack_rows(small_full, LANE, 8),), "gather_start_small_grads",
                            (tokens[0, "ffn1_down"],))

    out = {}

    landed = {}

    def land(l, piece, after):
        landed[l, piece] = split_wait(piece_modes(piece), flight[l, piece], after, f"exchange_wait_l{l}_{piece}")

    def where(name, l):
        if name in ("w_out", "ple_w_gate"):
            return "mix", 0, (GM_WOUT if name == "w_out" else GM_PGATE), ROWS_DEV, D
        ffn, kind = name[:4], name[7:]
        one_by_one = (l, ffn) == (0, "ffn1")
        if kind == "down":
            return (ffn + "_down", 0, 0, SH, D) if one_by_one else (ffn, 1, 0, SH, D)
        if one_by_one:
            return f"{ffn}_{kind}", 0, 0, D, SH
        return ffn, 0, (0 if kind == "gate" else D), D, SH

    def update(piece):
        for name in PIECE_NAMES[piece]:
            if name in ("w_in", "ple_w_proj"):
                g = jnp.stack([_shard_grads("mix", [mix_sums[l]])[name] for l in range(DEPTH)])
            else:
                g = None
                for l in reversed(range(DEPTH)):
                    piece_l, idx, row0, rows, cols = where(name, l)
                    g = sum_into(landed[l, piece_l][idx], row0, rows, cols, l, g, f"sum_{name}_l{l}")
            out[name] = (g,) + tuple(adamw(g, a[name], a["m_" + name], a["v_" + name], "adamw_" + name))

    for l, piece in ((1, "ffn2"), (1, "mix"), (1, "ffn1"), (0, "ffn2"), (0, "mix")):
        land(l, piece, (dx, tokens[0, "ffn1_down"]))
    mix_sums = [sum_parts(landed[l, "mix"][0], f"sum_mix_grads_l{l}") for l in range(DEPTH)]
    update("ffn2")
    update("mix")
    done_early = tuple(out[n][1] for n in PIECE_NAMES["ffn2"] + PIECE_NAMES["mix"])
    small_parts, = split_wait(("gather_block",), small_flight, done_early, "gather_wait_small_grads")
    small_sum = sum_parts(small_parts, "sum_small_grads")
    g_small = dict(zip([n for n, _, _ in SMALL] + [n for n, _ in SINGLE],
                       _unpack_rows(small_sum, [s.shape for s in small_full])))
    for name, _, cols in SMALL:
        if cols is not None:
            g_small[name] = lax.dynamic_slice_in_dim(g_small[name], my_slot * cols, cols, axis=2)

    for n in [n for n, _, _ in SMALL] + [n for n, _ in SINGLE]:
        shape = a[n].shape
        flat = lambda t: t.reshape((-1, shape[-1]) if len(shape) > 1 else (1, -1))
        res = adamw(flat(g_small[n]), flat(a[n]), flat(a["m_" + n]), flat(a["v_" + n]), "adamw_" + n)
        out[n] = (g_small[n].reshape(shape),) + tuple(r.reshape(shape) for r in res)

    for piece in ("ffn1_gate", "ffn1_up", "ffn1_down"):
        land(0, piece, (out["final_norm"][1],) + done_early)
    update("ffn1")

    order = ['ffn1_norm', 'ffn1_w_gate', 'ffn1_w_up', 'ffn1_w_down', 'mix_norm', 'w_in', 'lru_conv_w', 'lru_conv_b',
             'lru_w_a', 'lru_b_a', 'lru_w_x', 'lru_b_x', 'lru_lambda', 'attn_sinks', 'rel_bias', 'dn_conv_w',
             'dn_a_log', 'dn_dt_bias', 'dn_norm', 'w_out', 'ffn2_norm', 'ffn2_w_gate', 'ffn2_w_up', 'ffn2_w_down',
             'ple_norm', 'ple_w_gate', 'ple_w_proj', 'final_norm']
    return (loss, dx.reshape(x.shape)) + tuple(out[n][k] for k in range(4) for n in order)
```

```python
import functools
import math

import numpy as np
import jax
import jax.numpy as jnp
from jax import lax
from jax.experimental import pallas as pl
from jax.experimental.pallas import tpu as pltpu

F32 = jnp.float32
BF16 = jnp.bfloat16
HI = lax.Precision.HIGHEST

D = 1024
DEPTH = 2
EPS = 1e-6
PLE = 256
FF = 2816
HD = 64
LRU_W = 256
LRU_C = 8.0
ATT_W = 512
ATT_H = 8
ATT_KV = 2
ATT_G = 4
KV_W = 128
WINDOW = 128
BQ = 128
REL_BUCKETS = 32
REL_MAX_DIST = 128
DN_W = 256
DN_H = 4
CHUNK = 64
D_IN = 2312
D_IN_PAD = 2432
N_DEV = 8

ADAM_LR = 0.001
ADAM_B1 = 0.9
ADAM_B2 = 0.999
ADAM_EPS = 1e-08
ADAM_WD = 0.01
ADAM_STEP = 10

LANE = 128
VMEM_LIMIT = 56 * 1024 * 1024
SH = FF // N_DEV
SHP = 384
FFP = N_DEV * SHP
FF_TILE = 2 * SHP
FF_SUB = 256
TOK_TILE = 512
R_DOWN2, R_WOUT, R_PGATE, R_PPROJ, R_ROWS = 0, 384, 512, 640, 672
WIN_ROWS = D * D_IN // N_DEV // 1024
IN_ROWS = 304
NEG = -1e30


def _cp(*sem):
    return pltpu.CompilerParams(dimension_semantics=tuple(sem), vmem_limit_bytes=VMEM_LIMIT)


def _dg(a, b, ca, cb, exact):
    dims = (((ca,), (cb,)), ((), ()))
    if exact == "f32":
        return lax.dot_general(a.astype(F32), b.astype(F32), dims, precision=HI, preferred_element_type=F32)
    if exact == "split":
        a_hi, b_hi = a.astype(BF16), b.astype(BF16)
        a_lo = (a - a_hi.astype(F32)).astype(BF16)
        b_lo = (b - b_hi.astype(F32)).astype(BF16)
        dot = lambda u, v: lax.dot_general(u, v, dims, preferred_element_type=F32)
        return dot(a_hi, b_hi) + (dot(a_hi, b_lo) + dot(a_lo, b_hi))
    return lax.dot_general(a.astype(BF16), b.astype(BF16), dims, preferred_element_type=F32)


def _make_mm(exact):
    @jax.custom_vjp
    def mm(a, b):
        return _dg(a, b, 1, 0, exact)

    @jax.custom_vjp
    def mm_nt(a, b):
        return _dg(a, b, 1, 1, exact)

    @jax.custom_vjp
    def mm_tn(a, b):
        return _dg(a, b, 0, 0, exact)

    mm.defvjp(lambda a, b: (mm(a, b), (a, b)),
              lambda r, d: (mm_nt(d, r[1]), mm_tn(r[0], d)))
    mm_nt.defvjp(lambda a, b: (mm_nt(a, b), (a, b)),
                 lambda r, d: (mm(d, r[1]), mm_tn(d, r[0])))
    mm_tn.defvjp(lambda a, b: (mm_tn(a, b), (a, b)),
                 lambda r, d: (mm_nt(r[1], d), mm(r[0], d)))
    return mm, mm_nt, mm_tn


_mm, _mm_nt, _mm_tn = _make_mm("bf16")
_mmx, _mmx_nt, _mmx_tn = _make_mm("f32")
_mm3, _mm3_nt, _mm3_tn = _make_mm("split")


def _iota(shape, dim):
    return lax.broadcasted_iota(jnp.int32, shape, dim)


def _sigmoid(x):
    return 0.5 * jnp.tanh(0.5 * x) + 0.5


def _rms(h, g):
    rstd = lax.rsqrt(jnp.mean(h * h, axis=-1, keepdims=True) + EPS)
    xhat = h * rstd
    return xhat * g, xhat, rstd


def _rms_bwd(dxn, xhat, rstd, g):
    dxhat = dxn * g
    dh = rstd * (dxhat - xhat * jnp.mean(dxhat * xhat, axis=-1, keepdims=True))
    dg = jnp.sum(dxn * xhat, axis=0, keepdims=True)
    return dh, dg


def _row_spec(tm, n):
    return pl.BlockSpec((tm, n), lambda i, *_: (i, 0))


def _full_spec(shape):
    nd = len(shape)
    return pl.BlockSpec(shape, lambda *_: (0,) * nd)


def _ffn_weight_specs():
    return [pl.BlockSpec((D, FF_TILE), lambda i, j: (0, j)),
            pl.BlockSpec((D, FF_TILE), lambda i, j: (1, j)),
            pl.BlockSpec((2, SHP, D), lambda i, j: (j, 0, 0))]


def ffn_fwd(h, g, wa, wb, name):
    T = h.shape[0]
    tm = min(TOK_TILE, T)
    nj = FFP // FF_TILE

    def body(h_ref, g_ref, wg_ref, wu_ref, wd_ref, o_ref, gt_ref, up_ref, xn_ref):
        j = pl.program_id(1)

        @pl.when(j == 0)
        def _():
            hh = h_ref[...]
            xn_ref[...] = _rms(hh, g_ref[...])[0].astype(BF16)
            o_ref[...] = hh

        blocks = [slice(c, c + FF_SUB) for c in range(0, FF_TILE, FF_SUB)]
        xn = xn_ref[...]
        wd = wd_ref[...].reshape(FF_TILE, D)
        gt = [_mm(xn, wg_ref[:, c]) for c in blocks]
        up = [_mm(xn, wu_ref[:, c]) for c in blocks]
        act = [t * _sigmoid(t) * u for t, u in zip(gt, up)]
        down = [_mm(act[k], wd[c]) for k, c in enumerate(blocks)]
        for k, c in enumerate(blocks):
            gt_ref[:, c] = gt[k].astype(BF16)
            up_ref[:, c] = up[k].astype(BF16)
        o_ref[...] += 0.5 * functools.reduce(lambda x, y: x + y, down)

    tile = pl.BlockSpec((tm, FF_TILE), lambda i, j: (i, j))
    return pl.pallas_call(
        body, name=name, grid=(T // tm, nj),
        in_specs=[pl.BlockSpec((tm, D), lambda i, j: (i, 0)),
                  pl.BlockSpec((1, D), lambda i, j: (0, 0))] + _ffn_weight_specs(),
        out_specs=[pl.BlockSpec((tm, D), lambda i, j: (i, 0)), tile, tile,
                   pl.BlockSpec((tm, D), lambda i, j: (i, 0))],
        out_shape=[jax.ShapeDtypeStruct((T, D), F32), jax.ShapeDtypeStruct((T, FFP), BF16),
                   jax.ShapeDtypeStruct((T, FFP), BF16), jax.ShapeDtypeStruct((T, D), BF16)],
        compiler_params=_cp("parallel", "arbitrary"),
    )(h, g, wa, wa, wb)


def ffn_bwd(h, dy, g, gt_saved, up_saved, wa, wb, name):
    T = h.shape[0]
    tm = min(TOK_TILE, T)
    nj = FFP // FF_TILE

    def body(h_ref, dy_ref, g_ref, gt_ref, up_ref, wg_ref, wu_ref, wd_ref,
             dh_ref, dg_ref, du_ref, a_ref, dn_ref, dxn_s, dyh_s):
        i = pl.program_id(0)
        j = pl.program_id(1)

        @pl.when(j == 0)
        def _():
            dxn_s[...] = jnp.zeros_like(dxn_s)
            dyh_s[...] = (0.5 * dy_ref[...]).astype(BF16)

        @pl.when((i == 0) & (j == 0))
        def _():
            dn_ref[...] = jnp.zeros_like(dn_ref)

        blocks = [slice(c, c + FF_SUB) for c in range(0, FF_TILE, FF_SUB)]
        wd = wd_ref[...].reshape(FF_TILE, D)
        dyh = dyh_s[...]
        gt = [gt_ref[:, c].astype(F32) for c in blocks]
        up = [up_ref[:, c].astype(F32) for c in blocks]
        da = [_mm_nt(dyh, wd[c]) for c in blocks]
        sg = [_sigmoid(t) for t in gt]
        si = [t * s for t, s in zip(gt, sg)]
        dup = [d * s for d, s in zip(da, si)]
        dgt = [d * u * (s * (1.0 + t * (1.0 - s))) for d, u, s, t in zip(da, up, sg, gt)]
        dxn = [_mm_nt(dgt[k], wg_ref[:, c]) + _mm_nt(dup[k], wu_ref[:, c]) for k, c in enumerate(blocks)]
        for k, c in enumerate(blocks):
            dg_ref[:, c] = dgt[k].astype(BF16)
            du_ref[:, c] = dup[k].astype(BF16)
            a_ref[:, c] = (si[k] * up[k]).astype(BF16)
        dxn_s[...] += functools.reduce(lambda x, y: x + y, dxn)

        @pl.when(j == nj - 1)
        def _():
            gg = g_ref[...]
            _, xhat, rstd = _rms(h_ref[...], gg)
            dh, dn = _rms_bwd(dxn_s[...], xhat, rstd, gg)
            dh_ref[...] = dy_ref[...] + dh
            dn_ref[...] += dn

    tile = pl.BlockSpec((tm, FF_TILE), lambda i, j: (i, j))
    return pl.pallas_call(
        body, name=name, grid=(T // tm, nj),
        in_specs=[pl.BlockSpec((tm, D), lambda i, j: (i, 0)),
                  pl.BlockSpec((tm, D), lambda i, j: (i, 0)),
                  pl.BlockSpec((1, D), lambda i, j: (0, 0)), tile, tile] + _ffn_weight_specs(),
        out_specs=[pl.BlockSpec((tm, D), lambda i, j: (i, 0)), tile, tile, tile,
                   pl.BlockSpec((1, D), lambda i, j: (0, 0))],
        out_shape=[jax.ShapeDtypeStruct((T, D), F32)] + [jax.ShapeDtypeStruct((T, FFP), BF16)] * 3
        + [jax.ShapeDtypeStruct((1, D), F32)],
        scratch_shapes=[pltpu.VMEM((tm, D), F32), pltpu.VMEM((tm, D), BF16)],
        compiler_params=_cp("arbitrary", "arbitrary"),
    )(h, dy, g, gt_saved, up_saved, wa, wa, wb)


def _pick(n, prefs):
    for t in prefs:
        if n % t == 0:
            return t
    return n


def _tn_body(nk, scale, out_dtype, squeeze):
    def body(a_ref, b_ref, *rest):
        o_ref, acc = rest[-2], rest[-1]
        k = pl.program_id(2)

        @pl.when(k == 0)
        def _():
            acc[...] = jnp.zeros_like(acc)

        acc[...] += _mm_tn(a_ref[...], b_ref[...])

        @pl.when(k == nk - 1)
        def _():
            res = (scale * acc[...]).astype(out_dtype)
            if squeeze:
                o_ref[0] = res
            else:
                o_ref[...] = res

    return body


def matmul_tn(a, b, name, scale=1.0, out_dtype=BF16):
    T, M = a.shape
    N = b.shape[1]
    tmm = _pick(M, (512, 256))
    tnn = _pick(N, (1024, 2432))
    tk = min(2 * TOK_TILE, T)
    nk = T // tk
    return pl.pallas_call(
        _tn_body(nk, scale, out_dtype, False), name=name, grid=(M // tmm, N // tnn, nk),
        in_specs=[pl.BlockSpec((tk, tmm), lambda i, j, k: (k, i)),
                  pl.BlockSpec((tk, tnn), lambda i, j, k: (k, j))],
        out_specs=pl.BlockSpec((tmm, tnn), lambda i, j, k: (i, j)),
        out_shape=jax.ShapeDtypeStruct((M, N), out_dtype),
        scratch_shapes=[pltpu.VMEM((tmm, tnn), F32)],
        compiler_params=_cp("parallel", "parallel", "arbitrary"),
    )(a, b)


def grad_cols(a, b, dst, slot, name):
    T = a.shape[0]
    tmm, tnn = D, FFP // 2
    tk = min(2 * TOK_TILE, T)
    nk = T // tk
    return pl.pallas_call(
        _tn_body(nk, 1.0, BF16, True), name=name, grid=(D // tmm, FFP // tnn, nk),
        in_specs=[pl.BlockSpec((tk, tmm), lambda i, j, k: (k, i)),
                  pl.BlockSpec((tk, tnn), lambda i, j, k: (k, j)),
                  pl.BlockSpec(memory_space=pl.ANY)],
        out_specs=pl.BlockSpec((1, tmm, tnn), lambda i, j, k: (slot, i, j)),
        out_shape=jax.ShapeDtypeStruct(dst.shape, dst.dtype),
        scratch_shapes=[pltpu.VMEM((tmm, tnn), F32)],
        input_output_aliases={2: 0},
        compiler_params=_cp("parallel", "parallel", "arbitrary"),
    )(a, b, dst)


def grad_rows(a, b, dst, row0, rows, name, scale=1.0):
    T = a.shape[0]
    tk = min(2 * TOK_TILE, T)
    nk = T // tk
    blk = row0 // rows

    def body(a_ref, b_ref, dst_ref, o_ref, acc):
        k = pl.program_id(0)

        @pl.when(k == 0)
        def _():
            acc[...] = jnp.zeros_like(acc)

        acc[...] += _mm_tn(a_ref[...], b_ref[...])

        @pl.when(k == nk - 1)
        def _():
            o_ref[...] = (scale * acc[...]).astype(BF16).reshape(N_DEV, rows, D)

    return pl.pallas_call(
        body, name=name, grid=(nk,),
        in_specs=[pl.BlockSpec((tk, N_DEV * rows), lambda k: (k, 0)),
                  pl.BlockSpec((tk, D), lambda k: (k, 0)),
                  pl.BlockSpec(memory_space=pl.ANY)],
        out_specs=pl.BlockSpec((N_DEV, rows, D), lambda k: (0, blk, 0)),
        out_shape=jax.ShapeDtypeStruct(dst.shape, dst.dtype),
        scratch_shapes=[pltpu.VMEM((N_DEV * rows, D), F32)],
        input_output_aliases={2: 0},
        compiler_params=_cp("arbitrary"),
    )(a, b, dst)


U_SPLITS = (512, 768, 1024, 128)
U_OFFS = (0, 512, 1280, 2304)


def mixin_fwd(h, g, w_in, name):
    T = h.shape[0]
    tm = min(TOK_TILE, T)

    def body(h_ref, g_ref, w_ref, u0, u1, u2, u3, xn_ref):
        xn = _rms(h_ref[...], g_ref[...])[0].astype(BF16)
        xn_ref[...] = xn
        u = _mm(xn, w_ref[...])
        for ref, off, n in zip((u0, u1, u2, u3), U_OFFS, U_SPLITS):
            ref[...] = u[:, off:off + n]

    return pl.pallas_call(
        body, name=name, grid=(T // tm,),
        in_specs=[_row_spec(tm, D), _full_spec((1, D)), _full_spec((D, D_IN_PAD))],
        out_specs=[_row_spec(tm, n) for n in U_SPLITS] + [_row_spec(tm, D)],
        out_shape=[jax.ShapeDtypeStruct((T, n), F32) for n in U_SPLITS]
        + [jax.ShapeDtypeStruct((T, D), BF16)],
        compiler_params=_cp("parallel"),
    )(h, g, w_in)


DU_SPLITS = (256, 256, 768, 768, 256, 128)
DU_OFFS = (0, 256, 512, 1280, 2048, 2304)


def mixin_bwd(h, dh_in, g, w_in, dus, name):
    T = h.shape[0]
    tm = min(TOK_TILE, T)

    def body(h_ref, dhi_ref, g_ref, w_ref, *refs):
        dh_ref, du_ref, dn_ref = refs[-3:]

        @pl.when(pl.program_id(0) == 0)
        def _():
            dn_ref[...] = jnp.zeros_like(dn_ref)

        for ref, off, n in zip(refs[:-3], DU_OFFS, DU_SPLITS):
            du_ref[:, off:off + n] = ref[...].astype(BF16)
        dxn = _mm_nt(du_ref[...], w_ref[...])
        gg = g_ref[...]
        _, xhat, rstd = _rms(h_ref[...], gg)
        dh, dn = _rms_bwd(dxn, xhat, rstd, gg)
        dh_ref[...] = dhi_ref[...] + dh
        dn_ref[...] += dn

    return pl.pallas_call(
        body, name=name, grid=(T // tm,),
        in_specs=[_row_spec(tm, D), _row_spec(tm, D), _full_spec((1, D)), _full_spec((D, D_IN_PAD))]
        + [_row_spec(tm, n) for n in DU_SPLITS],
        out_specs=[_row_spec(tm, D), _row_spec(tm, D_IN_PAD), _full_spec((1, D))],
        out_shape=[jax.ShapeDtypeStruct((T, D), F32), jax.ShapeDtypeStruct((T, D_IN_PAD), BF16),
                   jax.ShapeDtypeStruct((1, D), F32)],
        compiler_params=_cp("arbitrary"),
    )(h, dh_in, g, w_in, *dus)


def _shift_down(x, s, row):
    if s == 0:
        return x
    return jnp.where(row >= s, pltpu.roll(x, s, 0), 0.0)


def _shift_up(x, s, row):
    if s == 0:
        return x
    n = x.shape[0]
    return jnp.where(row < n - s, pltpu.roll(x, n - s, 0), 0.0)


def conv_fwd(x, w, b, S, col0, C, name):
    T = x.shape[0]
    cb0 = col0 // LANE

    def body(x_ref, w_ref, b_ref, y_ref):
        xx = x_ref[...]
        row = _iota(xx.shape, 0)
        y = xx * w_ref[3:4, :] + b_ref[...]
        for k in range(3):
            y += _shift_down(xx, 3 - k, row) * w_ref[k:k + 1, :]
        y_ref[...] = y

    return pl.pallas_call(
        body, name=name, grid=(T // S, C // LANE),
        in_specs=[pl.BlockSpec((S, LANE), lambda s, c: (s, cb0 + c)),
                  pl.BlockSpec((4, LANE), lambda s, c: (0, c)),
                  pl.BlockSpec((1, LANE), lambda s, c: (0, c))],
        out_specs=pl.BlockSpec((S, LANE), lambda s, c: (s, c)),
        out_shape=jax.ShapeDtypeStruct((T, C), F32),
        compiler_params=_cp("parallel", "parallel"),
    )(x, w, b)


def conv_bwd(x, dy, w, S, col0, C, name):
    T = x.shape[0]
    cb0 = col0 // LANE

    def body(x_ref, dy_ref, w_ref, dx_ref, dwb_ref):
        @pl.when(pl.program_id(1) == 0)
        def _():
            dwb_ref[...] = jnp.zeros_like(dwb_ref)

        xx = x_ref[...]
        dd = dy_ref[...]
        row = _iota(xx.shape, 0)
        dx = dd * w_ref[3:4, :]
        for k in range(3):
            dx += _shift_up(dd, 3 - k, row) * w_ref[k:k + 1, :]
        dx_ref[...] = dx
        for k in range(4):
            dwb_ref[k:k + 1, :] += jnp.sum(dd * _shift_down(xx, 3 - k, row), axis=0, keepdims=True)
        dwb_ref[4:5, :] += jnp.sum(dd, axis=0, keepdims=True)

    return pl.pallas_call(
        body, name=name, grid=(C // LANE, T // S),
        in_specs=[pl.BlockSpec((S, LANE), lambda c, s: (s, cb0 + c)),
                  pl.BlockSpec((S, LANE), lambda c, s: (s, c)),
                  pl.BlockSpec((4, LANE), lambda c, s: (0, c))],
        out_specs=[pl.BlockSpec((S, LANE), lambda c, s: (s, c)),
                   pl.BlockSpec((8, LANE), lambda c, s: (0, c))],
        out_shape=[jax.ShapeDtypeStruct((T, C), F32), jax.ShapeDtypeStruct((8, C), F32)],
        compiler_params=_cp("parallel", "arbitrary"),
    )(x, dy, w)


def _scan(a, b, row):
    n = a.shape[0]
    d = 1
    while d < n:
        keep = row >= d
        b = a * jnp.where(keep, pltpu.roll(b, d, 0), 0.0) + b
        a = a * jnp.where(keep, pltpu.roll(a, d, 0), 1.0)
        d *= 2
    return b


def _rscan(a, b, row):
    n = a.shape[0]
    d = 1
    while d < n:
        keep = row < n - d
        b = a * jnp.where(keep, pltpu.roll(b, n - d, 0), 0.0) + b
        a = a * jnp.where(keep, pltpu.roll(a, n - d, 0), 1.0)
        d *= 2
    return b


GELU_C = math.sqrt(2.0 / math.pi)


def _gelu(x):
    t = jnp.tanh(GELU_C * (x + 0.044715 * (x * x * x)))
    return 0.5 * x * (1.0 + t), t


def _lru_gates(xr, wa, ba, wx, bx, lam):
    r = _sigmoid(_mm(xr, wa) + ba)
    i = _sigmoid(_mm(xr, wx) + bx)
    sp = jnp.maximum(-lam, 0.0) + jnp.log(1.0 + jnp.exp(-jnp.abs(lam)))
    la = -LRU_C * r * sp
    a = jnp.exp(la)
    e2 = a * a
    m = jnp.sqrt(-jnp.tanh(la) * (e2 + 1.0))
    return r, i, sp, a, e2, m


def lru_fwd(xr, u_lru, wa, wx, vec, S, name):
    T = xr.shape[0]

    def body(xr_ref, gt_ref, wa_ref, wx_ref, vec_ref, y_ref):
        x = xr_ref[...]
        row = _iota(x.shape, 0)
        r, i, sp, a, e2, m = _lru_gates(x, wa_ref[...], vec_ref[0:1, :], wx_ref[...], vec_ref[1:2, :],
                                        vec_ref[2:3, :])
        hh = _scan(a, m * (i * x), row)
        y_ref[...] = _gelu(gt_ref[...])[0] * hh

    return pl.pallas_call(
        body, name=name, grid=(T // S, LRU_W // LANE),
        in_specs=[pl.BlockSpec((S, LANE), lambda s, c: (s, c)),
                  pl.BlockSpec((S, LANE), lambda s, c: (s, 2 + c)),
                  pl.BlockSpec((LANE, LANE), lambda s, c: (c, c)),
                  pl.BlockSpec((LANE, LANE), lambda s, c: (c, c)),
                  pl.BlockSpec((8, LANE), lambda s, c: (0, c))],
        out_specs=pl.BlockSpec((S, LANE), lambda s, c: (s, c)),
        out_shape=jax.ShapeDtypeStruct((T, LRU_W), F32),
        compiler_params=_cp("parallel", "parallel"),
    )(xr, u_lru, wa, wx, vec)


def lru_bwd(xr, u_lru, dy, wa, wx, vec, S, name):
    T = xr.shape[0]

    def body(xr_ref, gt_ref, dy_ref, wa_ref, wx_ref, vec_ref,
             dxr_ref, dgt_ref, dwa_ref, dwx_ref, dvec_ref):
        @pl.when(pl.program_id(1) == 0)
        def _():
            dwa_ref[...] = jnp.zeros_like(dwa_ref)
            dwx_ref[...] = jnp.zeros_like(dwx_ref)
            dvec_ref[...] = jnp.zeros_like(dvec_ref)

        x = xr_ref[...]
        n = x.shape[0]
        row = _iota(x.shape, 0)
        lam = vec_ref[2:3, :]
        r, i, sp, a, e2, m = _lru_gates(x, wa_ref[...], vec_ref[0:1, :], wx_ref[...], vec_ref[1:2, :], lam)
        v = i * x
        hh = _scan(a, m * v, row)
        gt = gt_ref[...]
        dy = dy_ref[...]
        ge, t = _gelu(gt)
        dgt_ref[...] = dy * hh * (0.5 * (1.0 + t) + 0.5 * gt * (1.0 - t * t) * GELU_C
                                  * (1.0 + 3.0 * 0.044715 * gt * gt))
        a_next = jnp.where(row < n - 1, pltpu.roll(a, n - 1, 0), 0.0)
        G = _rscan(a_next, dy * ge, row)
        da = G * _shift_down(hh, 1, row)
        dv = G * m
        dla = da * a - (G * v) * e2 / m
        dr = dla * (-LRU_C * sp)
        dsp = jnp.sum(dla * (-LRU_C * r), axis=0, keepdims=True)
        dra = dr * r * (1.0 - r)
        dia = (dv * x) * i * (1.0 - i)
        dxr_ref[...] = dv * i + _mm_nt(dra, wa_ref[...]) + _mm_nt(dia, wx_ref[...])
        dwa_ref[0] += _mm_tn(x, dra)
        dwx_ref[0] += _mm_tn(x, dia)
        dvec_ref[0:1, :] += jnp.sum(dra, axis=0, keepdims=True)
        dvec_ref[1:2, :] += jnp.sum(dia, axis=0, keepdims=True)
        dvec_ref[2:3, :] += dsp * (-_sigmoid(-lam))

    return pl.pallas_call(
        body, name=name, grid=(LRU_W // LANE, T // S),
        in_specs=[pl.BlockSpec((S, LANE), lambda c, s: (s, c)),
                  pl.BlockSpec((S, LANE), lambda c, s: (s, 2 + c)),
                  pl.BlockSpec((S, LANE), lambda c, s: (s, c)),
                  pl.BlockSpec((LANE, LANE), lambda c, s: (c, c)),
                  pl.BlockSpec((LANE, LANE), lambda c, s: (c, c)),
                  pl.BlockSpec((8, LANE), lambda c, s: (0, c))],
        out_specs=[pl.BlockSpec((S, LANE), lambda c, s: (s, c)),
                   pl.BlockSpec((S, LANE), lambda c, s: (s, c)),
                   pl.BlockSpec((1, LANE, LANE), lambda c, s: (c, 0, 0)),
                   pl.BlockSpec((1, LANE, LANE), lambda c, s: (c, 0, 0)),
                   pl.BlockSpec((8, LANE), lambda c, s: (0, c))],
        out_shape=[jax.ShapeDtypeStruct((T, LRU_W), F32), jax.ShapeDtypeStruct((T, LRU_W), F32),
                   jax.ShapeDtypeStruct((2, LANE, LANE), F32), jax.ShapeDtypeStruct((2, LANE, LANE), F32),
                   jax.ShapeDtypeStruct((8, LRU_W), F32)],
        compiler_params=_cp("parallel", "arbitrary"),
    )(xr, u_lru, dy, wa, wx, vec)


def _bucket_table():
    qi = np.arange(BQ)[:, None]
    kj = np.arange(2 * BQ)[None, :]
    dist = BQ + qi - kj
    band = (dist >= 0) & (dist < WINDOW)
    dd = np.maximum(dist, 0)
    max_exact = REL_BUCKETS // 2
    large = max_exact + (np.log(np.maximum(dd, 1).astype(np.float32) / np.float32(max_exact))
                         / np.float32(math.log(REL_MAX_DIST / max_exact))
                         * np.float32(REL_BUCKETS - max_exact)).astype(np.int32)
    large = np.minimum(large, REL_BUCKETS - 1)
    bucket = np.where(dd < max_exact, dd, large)
    return np.where(band, bucket, -1).astype(np.int32)


def _att_specs(S):
    nb = S // BQ
    qc = ATT_W // LANE
    return [pl.BlockSpec((BQ, ATT_W), lambda b, n: (b * nb + n, 0)),
            pl.BlockSpec((BQ, KV_W), lambda b, n: (b * nb + jnp.maximum(n - 1, 0), qc)),
            pl.BlockSpec((BQ, KV_W), lambda b, n: (b * nb + n, qc)),
            pl.BlockSpec((BQ, KV_W), lambda b, n: (b * nb + jnp.maximum(n - 1, 0), qc + 1)),
            pl.BlockSpec((BQ, KV_W), lambda b, n: (b * nb + n, qc + 1))]


def _att_bias(bk, rb_ref, bias_s):
    for h in range(ATT_H):
        acc = jnp.zeros(bk.shape, F32)
        for bb in range(REL_BUCKETS):
            acc = jnp.where(bk == bb, rb_ref[bb * ATT_H + h], acc)
        bias_s[h] = acc


def _att_probs(qs, kgs, bias_s, valid, sk_ref):
    heads = range(ATT_H)
    s = [_mm_nt(qs[h], kgs[h // ATT_G]) for h in heads]
    s = [jnp.where(valid, s[h] * (HD ** -0.5) + bias_s[h], NEG) for h in heads]
    m = [jnp.maximum(jnp.max(s[h], axis=-1, keepdims=True), sk_ref[h]) for h in heads]
    e = [jnp.exp(s[h] - m[h]) for h in heads]
    es = [jnp.exp(sk_ref[h] - m[h]) for h in heads]
    den = [jnp.sum(e[h], axis=-1, keepdims=True) + es[h] for h in heads]
    return [e[h] / den[h] for h in heads], [es[h] / den[h] for h in heads]


def _att_kv(kp_ref, kc_ref, vp_ref, vc_ref):
    cat = lambda a, b, g: jnp.concatenate([a[:, HD * g:HD * (g + 1)], b[:, HD * g:HD * (g + 1)]], axis=0)
    return ([cat(kp_ref, kc_ref, g) for g in range(ATT_KV)], [cat(vp_ref, vc_ref, g) for g in range(ATT_KV)])


def attn_fwd(u_att, sinks, rel_bias, S, name):
    T = u_att.shape[0]
    nb = S // BQ
    table = jnp.asarray(_bucket_table())

    def body(sk_ref, rb_ref, bk_ref, q_ref, kp_ref, kc_ref, vp_ref, vc_ref, o_ref, bias_s):
        b = pl.program_id(0)
        n = pl.program_id(1)
        bk = bk_ref[...]

        @pl.when((b == 0) & (n == 0))
        def _():
            _att_bias(bk, rb_ref, bias_s)

        valid = (bk >= 0) & ((n > 0) | (_iota(bk.shape, 1) >= BQ))
        kgs, vgs = _att_kv(kp_ref, kc_ref, vp_ref, vc_ref)
        p, _ = _att_probs([q_ref[:, HD * h:HD * (h + 1)] for h in range(ATT_H)], kgs, bias_s, valid, sk_ref)
        outs = [_mm(p[h], vgs[h // ATT_G]) for h in range(ATT_H)]
        for h in range(ATT_H):
            o_ref[:, HD * h:HD * (h + 1)] = outs[h]

    smem = pl.BlockSpec(memory_space=pltpu.SMEM)
    return pl.pallas_call(
        body, name=name, grid=(T // S, nb),
        in_specs=[smem, smem, _full_spec((BQ, 2 * BQ))] + _att_specs(S),
        out_specs=pl.BlockSpec((BQ, ATT_W), lambda b, n: (b * nb + n, 0)),
        out_shape=jax.ShapeDtypeStruct((T, ATT_W), F32),
        scratch_shapes=[pltpu.VMEM((ATT_H, BQ, 2 * BQ), F32)],
        compiler_params=_cp("arbitrary", "arbitrary"),
    )(sinks, rel_bias, table, u_att, u_att, u_att, u_att, u_att)


def attn_bwd(u_att, dy, sinks, rel_bias, S, name):
    T = u_att.shape[0]
    nb = S // BQ
    nB = T // S
    table = jnp.asarray(_bucket_table())
    scale = HD ** -0.5

    def body(sk_ref, rb_ref, bk_ref, q_ref, kp_ref, kc_ref, vp_ref, vc_ref, dy_ref,
             du_ref, drel_ref, dsk_ref, bias_s, dbias_s):
        b = pl.program_id(0)
        n = pl.program_id(1)
        bk = bk_ref[...]

        @pl.when((b == 0) & (n == 0))
        def _():
            _att_bias(bk, rb_ref, bias_s)
            dbias_s[...] = jnp.zeros_like(dbias_s)
            dsk_ref[...] = jnp.zeros_like(dsk_ref)
            drel_ref[...] = jnp.zeros_like(drel_ref)

        @pl.when(n == 0)
        def _():
            du_ref[...] = jnp.zeros_like(du_ref)

        valid = (bk >= 0) & ((n > 0) | (_iota(bk.shape, 1) >= BQ))
        r_cur = pl.multiple_of(n * BQ, BQ)
        r_prev = pl.multiple_of(jnp.maximum(n - 1, 0) * BQ, BQ)
        heads = range(ATT_H)
        kgs, vgs = _att_kv(kp_ref, kc_ref, vp_ref, vc_ref)
        qs = [q_ref[:, HD * h:HD * (h + 1)] for h in heads]
        dos = [dy_ref[:, HD * h:HD * (h + 1)] for h in heads]
        p, ps = _att_probs(qs, kgs, bias_s, valid, sk_ref)
        dp = [_mm_nt(dos[h], vgs[h // ATT_G]) for h in heads]
        delta = [jnp.sum(p[h] * dp[h], axis=-1, keepdims=True) for h in heads]
        ds = [p[h] * (dp[h] - delta[h]) for h in heads]
        dss = [ds[h] * scale for h in heads]
        dq = [_mm(dss[h], kgs[h // ATT_G]) for h in heads]
        dks = [_mm_tn(dss[h], qs[h]) for h in heads]
        dvs = [_mm_tn(p[h], dos[h]) for h in heads]
        for h in heads:
            dbias_s[h] += ds[h]
            dsk_ref[h:h + 1, :] += jnp.broadcast_to(jnp.sum(-ps[h] * delta[h], axis=0, keepdims=True), (1, LANE))
            du_ref[pl.ds(r_cur, BQ), HD * h:HD * (h + 1)] = dq[h]
        for g in range(ATT_KV):
            of_group = range(g * ATT_G, (g + 1) * ATT_G)
            dk = functools.reduce(lambda x, y: x + y, [dks[h] for h in of_group])
            dv = functools.reduce(lambda x, y: x + y, [dvs[h] for h in of_group])
            ck = ATT_W + HD * g
            cv = ATT_W + KV_W + HD * g
            du_ref[pl.ds(r_prev, BQ), ck:ck + HD] += dk[0:BQ]
            du_ref[pl.ds(r_cur, BQ), ck:ck + HD] += dk[BQ:]
            du_ref[pl.ds(r_prev, BQ), cv:cv + HD] += dv[0:BQ]
            du_ref[pl.ds(r_cur, BQ), cv:cv + HD] += dv[BQ:]

        @pl.when((b == nB - 1) & (n == nb - 1))
        def _():
            lane = _iota((1, LANE), 1)
            for h in range(ATT_H):
                db = dbias_s[h]
                acc = jnp.zeros((1, LANE), F32)
                for bb in range(REL_BUCKETS):
                    val = jnp.sum(jnp.sum(jnp.where(bk == bb, db, 0.0), axis=1, keepdims=True),
                                  axis=0, keepdims=True)
                    acc = jnp.where(lane == bb, val, acc)
                drel_ref[h:h + 1, :] = acc

    smem = pl.BlockSpec(memory_space=pltpu.SMEM)
    return pl.pallas_call(
        body, name=name, grid=(nB, nb),
        in_specs=[smem, smem, _full_spec((BQ, 2 * BQ))] + _att_specs(S)
        + [pl.BlockSpec((BQ, ATT_W), lambda b, n: (b * nb + n, 0))],
        out_specs=[pl.BlockSpec((S, ATT_W + 2 * KV_W), lambda b, n: (b, 0)),
                   _full_spec((8, LANE)), _full_spec((8, LANE))],
        out_shape=[jax.ShapeDtypeStruct((T, ATT_W + 2 * KV_W), F32),
                   jax.ShapeDtypeStruct((8, LANE), F32), jax.ShapeDtypeStruct((8, LANE), F32)],
        scratch_shapes=[pltpu.VMEM((ATT_H, BQ, 2 * BQ), F32), pltpu.VMEM((ATT_H, BQ, 2 * BQ), F32)],
        compiler_params=_cp("arbitrary", "arbitrary"),
    )(sinks, rel_bias, table, u_att, u_att, u_att, u_att, u_att, dy)


def _head_of(i):
    return lax.shift_right_logical(i, 6)


def _head_mask(shape):
    return (_head_of(_iota(shape, 0)) == _head_of(_iota(shape, 1))).astype(F32)


def _dn_point(c, uba, alog, dtb):
    s = c * _sigmoid(c)
    qt, kt, vt = s[:, 0:256], s[:, 256:512], s[:, 512:768]
    ones_bd = _head_mask((DN_W, DN_W))
    q = qt * lax.rsqrt(_mm3(qt * qt, ones_bd) + EPS) * (HD ** -0.5)
    k = kt * lax.rsqrt(_mm3(kt * kt, ones_bd) + EPS)
    sel = _head_of(_iota((LANE, DN_W), 1))
    row = _iota((LANE, DN_W), 0)
    braw = _mm3(uba, (row == sel).astype(F32))
    araw = _mm3(uba, (row == sel + DN_H).astype(F32)) + dtb
    beta = _sigmoid(braw)
    g = -jnp.exp(alog) * (jnp.maximum(araw, 0.0) + jnp.log(1.0 + jnp.exp(-jnp.abs(araw))))
    return q, k, vt, g, beta


def dn_point_fwd(c, uba, alog, dtb, name):
    T = c.shape[0]
    tm = min(TOK_TILE, T)

    def body(c_ref, u_ref, al_ref, dt_ref, *outs):
        for ref, val in zip(outs, _dn_point(c_ref[...], u_ref[...], al_ref[...], dt_ref[...])):
            ref[...] = val

    return pl.pallas_call(
        body, name=name, grid=(T // tm,),
        in_specs=[_row_spec(tm, 768), _row_spec(tm, LANE), _full_spec((1, DN_W)), _full_spec((1, DN_W))],
        out_specs=[_row_spec(tm, DN_W)] * 5,
        out_shape=[jax.ShapeDtypeStruct((T, DN_W), F32)] * 5,
        compiler_params=_cp("parallel"),
    )(c, uba, alog, dtb)


def dn_point_bwd(c, uba, alog, dtb, douts, name):
    T = c.shape[0]
    tm = min(TOK_TILE, T)

    def body(c_ref, u_ref, al_ref, dt_ref, dq, dk, dv, dg, db, dc_ref, du_ref, dvec_ref):
        @pl.when(pl.program_id(0) == 0)
        def _():
            dvec_ref[...] = jnp.zeros_like(dvec_ref)

        _, vjp = jax.vjp(_dn_point, c_ref[...], u_ref[...], al_ref[...], dt_ref[...])
        dc, du, dal, ddt = vjp((dq[...], dk[...], dv[...], dg[...], db[...]))
        dc_ref[...] = dc
        du_ref[...] = du
        fold = (_iota((LANE, DN_W), 0) == _head_of(_iota((LANE, DN_W), 1))).astype(F32)
        both = jnp.concatenate([dal, ddt, jnp.zeros((6, DN_W), F32)], axis=0)
        dvec_ref[...] += _mmx_nt(both, fold)

    return pl.pallas_call(
        body, name=name, grid=(T // tm,),
        in_specs=[_row_spec(tm, 768), _row_spec(tm, LANE), _full_spec((1, DN_W)), _full_spec((1, DN_W))]
        + [_row_spec(tm, DN_W)] * 5,
        out_specs=[_row_spec(tm, 768), _row_spec(tm, LANE), _full_spec((8, LANE))],
        out_shape=[jax.ShapeDtypeStruct((T, 768), F32), jax.ShapeDtypeStruct((T, LANE), F32),
                   jax.ShapeDtypeStruct((8, LANE), F32)],
        compiler_params=_cp("arbitrary"),
    )(c, uba, alog, dtb, *douts)


def _unit_lower_inverses(lmats):
    eye = (_iota(lmats[0].shape, 0) == _iota(lmats[0].shape, 1)).astype(F32)
    tinvs = [eye - lm for lm in lmats]
    pws = list(lmats)
    for _ in range(5):
        pws = [_mm3(pw, pw) for pw in pws]
        tinvs = [t + _mm3(t, pw) for t, pw in zip(tinvs, pws)]
    return tuple(tinvs)


def _inverse_bwd(tinv, d):
    return -_mm3_nt(_mm3_tn(tinv, d), tinv)


@jax.custom_vjp
def _tri_invs(lmats):
    return _unit_lower_inverses(lmats)


def _tri_invs_fwd(lmats):
    tinvs = _unit_lower_inverses(lmats)
    return tinvs, tinvs


_tri_invs.defvjp(_tri_invs_fwd, lambda tinvs, ds: (tuple(_inverse_bwd(t, d) for t, d in zip(tinvs, ds)),))


@jax.custom_vjp
def _tri_inv_known(lmat, tinv):
    return tinv


_tri_inv_known.defvjp(lambda lmat, tinv: (tinv, tinv),
                      lambda tinv, d: (_inverse_bwd(tinv, d), jnp.zeros_like(tinv)))


DN_SUB = 4


def _dn_stack(x):
    return jnp.concatenate([x, x, x, x], axis=0) * _head_mask((DN_W, DN_W))


def _dn_pre_inverse(q, k, v, g, beta):
    hm = _head_mask((DN_W, DN_W))
    ri = _iota((DN_W, DN_W), 0) & (CHUNK - 1)
    ci = _iota((DN_W, DN_W), 1) & (CHUNK - 1)
    tri64 = (_iota((CHUNK, CHUNK), 0) >= _iota((CHUNK, CHUNK), 1)).astype(F32)
    gc = _mm3(tri64, g)
    ks = _dn_stack(k)
    gcol = jnp.sum(_dn_stack(gc), axis=1, keepdims=True) * (1.0 / HD)
    gmat = jnp.broadcast_to(gcol, (DN_W, DN_W))
    decay = jnp.exp(jnp.minimum(gmat - gmat.T, 0.0))
    lmat = _mm_nt(_dn_stack(k * beta), ks) * decay * (hm * (ri > ci).astype(F32))
    att = _mm_nt(_dn_stack(q), ks) * decay * (hm * (ri >= ci).astype(F32))
    return lmat, att, gc


def _dn_post_inverse(q, k, v, g, beta, tinv, att, gc):
    glast = jnp.sum(g, axis=0, keepdims=True)
    eg = jnp.exp(gc)
    u = _mm(tinv, _dn_stack(v * beta))
    w = _mm(tinv, _dn_stack(k * beta * eg))
    return u, w, att, _dn_stack(q * eg), _dn_stack(k * jnp.exp(glast - gc)), jnp.exp(glast), tinv


def _dn_apply(state, prep):
    u, w, att, qe, kd, eglast, _ = prep
    vn = u - _mm(w, state)
    o4 = _mm(qe, state) + _mm(att, vn)
    o = o4[0:64] + o4[64:128] + o4[128:192] + o4[192:256]
    return o, state * eglast + _mm_tn(kd, vn)


def _dn_chunks(state, q, k, v, g, beta, knowns=None):
    n = q.shape[0] // CHUNK
    chunks = [tuple(x[c * CHUNK:(c + 1) * CHUNK] for x in (q, k, v, g, beta)) for c in range(n)]
    pre = [_dn_pre_inverse(*ch) for ch in chunks]
    if knowns is None:
        tinvs = _tri_invs(tuple(p[0] for p in pre))
    else:
        tinvs = [_tri_inv_known(p[0], known) for p, known in zip(pre, knowns)]
    preps = [_dn_post_inverse(*ch, tinv, p[1], p[2]) for ch, tinv, p in zip(chunks, tinvs, pre)]
    outs = []
    for prep in preps:
        o, state = _dn_apply(state, prep)
        outs.append(o)
    return jnp.concatenate(outs, axis=0), state, [prep[-1] for prep in preps]


def dn_scan_fwd(q, k, v, g, beta, S, name):
    T = q.shape[0]
    rows = DN_SUB * CHUNK
    ns = S // rows

    def body(q_ref, k_ref, v_ref, g_ref, b_ref, o_ref, st_ref, ti_ref, s_s):
        @pl.when(pl.program_id(1) == 0)
        def _():
            s_s[...] = jnp.zeros_like(s_s)

        st = s_s[...]
        st_ref[0] = st
        o, new, tinvs = _dn_chunks(st, q_ref[...], k_ref[...], v_ref[...], g_ref[...], b_ref[...])
        o_ref[...] = o
        for c, tinv in enumerate(tinvs):
            ti_ref[c] = tinv
        s_s[...] = new

    spec = pl.BlockSpec((rows, DN_W), lambda b, t: (b * ns + t, 0))
    return pl.pallas_call(
        body, name=name, grid=(T // S, ns),
        in_specs=[spec] * 5,
        out_specs=[spec, pl.BlockSpec((1, DN_W, DN_W), lambda b, t: (b * ns + t, 0, 0)),
                   pl.BlockSpec((DN_SUB, DN_W, DN_W), lambda b, t: (b * ns + t, 0, 0))],
        out_shape=[jax.ShapeDtypeStruct((T, DN_W), F32),
                   jax.ShapeDtypeStruct((T // rows, DN_W, DN_W), F32),
                   jax.ShapeDtypeStruct((T // CHUNK, DN_W, DN_W), F32)],
        scratch_shapes=[pltpu.VMEM((DN_W, DN_W), F32)],
        compiler_params=_cp("parallel", "arbitrary"),
    )(q, k, v, g, beta)


def dn_scan_bwd(q, k, v, g, beta, states, tinvs, do, S, name):
    T = q.shape[0]
    rows = DN_SUB * CHUNK
    ns = S // rows

    def body(q_ref, k_ref, v_ref, g_ref, b_ref, st_ref, ti_ref, do_ref, dq, dk, dv, dg, db, ds_s):
        @pl.when(pl.program_id(1) == 0)
        def _():
            ds_s[...] = jnp.zeros_like(ds_s)

        knowns = [ti_ref[c] for c in range(DN_SUB)]
        _, vjp = jax.vjp(lambda *args: _dn_chunks(*args, knowns=knowns)[:2],
                         st_ref[0], q_ref[...], k_ref[...], v_ref[...], g_ref[...], b_ref[...])
        grads = vjp((do_ref[...], ds_s[...]))
        ds_s[...] = grads[0]
        for ref, val in zip((dq, dk, dv, dg, db), grads[1:]):
            ref[...] = val

    spec = pl.BlockSpec((rows, DN_W), lambda b, t: (b * ns + ns - 1 - t, 0))
    return pl.pallas_call(
        body, name=name, grid=(T // S, ns),
        in_specs=[spec] * 5 + [pl.BlockSpec((1, DN_W, DN_W), lambda b, t: (b * ns + ns - 1 - t, 0, 0)),
                               pl.BlockSpec((DN_SUB, DN_W, DN_W), lambda b, t: (b * ns + ns - 1 - t, 0, 0)),
                               spec],
        out_specs=[spec] * 5,
        out_shape=[jax.ShapeDtypeStruct((T, DN_W), F32)] * 5,
        scratch_shapes=[pltpu.VMEM((DN_W, DN_W), F32)],
        compiler_params=_cp("parallel", "arbitrary"),
    )(q, k, v, g, beta, states, tinvs, do)


def _dn_gate(o, z, nl):
    ms = _mm3(o * o, _head_mask((DN_W, DN_W))) * (1.0 / HD)
    return o * lax.rsqrt(ms + EPS) * nl * (z * _sigmoid(z))


def dn_gate_fwd(o, u_dn, nl, name):
    T = o.shape[0]
    tm = min(TOK_TILE, T)

    def body(o_ref, z_ref, n_ref, y_ref):
        y_ref[...] = _dn_gate(o_ref[...], z_ref[...], n_ref[...])

    return pl.pallas_call(
        body, name=name, grid=(T // tm,),
        in_specs=[_row_spec(tm, DN_W), pl.BlockSpec((tm, DN_W), lambda i: (i, 3)), _full_spec((1, DN_W))],
        out_specs=_row_spec(tm, DN_W),
        out_shape=jax.ShapeDtypeStruct((T, DN_W), F32),
        compiler_params=_cp("parallel"),
    )(o, u_dn, nl)


def dn_gate_bwd(o, u_dn, nl, dy, name):
    T = o.shape[0]
    tm = min(TOK_TILE, T)

    def body(o_ref, z_ref, n_ref, dy_ref, do_ref, dz_ref, dn_ref):
        @pl.when(pl.program_id(0) == 0)
        def _():
            dn_ref[...] = jnp.zeros_like(dn_ref)

        _, vjp = jax.vjp(_dn_gate, o_ref[...], z_ref[...], n_ref[...])
        do, dz, dn = vjp(dy_ref[...])
        do_ref[...] = do
        dz_ref[...] = dz
        fold = (_iota((LANE, DN_W), 0) == (_iota((LANE, DN_W), 1) & (HD - 1))).astype(F32)
        dn_ref[...] += _mmx_nt(jnp.concatenate([dn, jnp.zeros((7, DN_W), F32)], axis=0), fold)

    return pl.pallas_call(
        body, name=name, grid=(T // tm,),
        in_specs=[_row_spec(tm, DN_W), pl.BlockSpec((tm, DN_W), lambda i: (i, 3)), _full_spec((1, DN_W)),
                  _row_spec(tm, DN_W)],
        out_specs=[_row_spec(tm, DN_W), _row_spec(tm, DN_W), _full_spec((8, LANE))],
        out_shape=[jax.ShapeDtypeStruct((T, DN_W), F32), jax.ShapeDtypeStruct((T, DN_W), F32),
                   jax.ShapeDtypeStruct((8, LANE), F32)],
        compiler_params=_cp("arbitrary"),
    )(o, u_dn, nl, dy)


Y_SPLITS = (LRU_W, ATT_W, DN_W)
Y_OFFS = (0, LRU_W, LRU_W + ATT_W)


ROWS_DEV = D // N_DEV


def _dev_rows_spec(row0):
    return pl.BlockSpec((N_DEV, ROWS_DEV, D), lambda *_: (0, row0 // ROWS_DEV, 0))


def _dev_rows(w_ref, off, n):
    return w_ref[off // ROWS_DEV:(off + n) // ROWS_DEV].reshape(n, D)


def wout_fwd(h, ys, wb, name):
    T = h.shape[0]
    tm = min(TOK_TILE, T)

    def body(h_ref, y0, y1, y2, w_ref, o_ref, yc_ref):
        for ref, off, n in zip((y0, y1, y2), Y_OFFS, Y_SPLITS):
            yc_ref[:, off:off + n] = ref[...].astype(BF16)
        o_ref[...] = h_ref[...] + _mm(yc_ref[...], _dev_rows(w_ref, 0, D))

    return pl.pallas_call(
        body, name=name, grid=(T // tm,),
        in_specs=[_row_spec(tm, D)] + [_row_spec(tm, n) for n in Y_SPLITS] + [_dev_rows_spec(R_WOUT)],
        out_specs=[_row_spec(tm, D), _row_spec(tm, D)],
        out_shape=[jax.ShapeDtypeStruct((T, D), F32), jax.ShapeDtypeStruct((T, D), BF16)],
        compiler_params=_cp("parallel"),
    )(h, *ys, wb)


def wout_bwd(dy, wb, name):
    T = dy.shape[0]
    tm = min(TOK_TILE, T)

    def body(dy_ref, w_ref, d0, d1, d2):
        dys = _mm_nt(dy_ref[...], _dev_rows(w_ref, 0, D))
        for ref, off, n in zip((d0, d1, d2), Y_OFFS, Y_SPLITS):
            ref[...] = dys[:, off:off + n]

    return pl.pallas_call(
        body, name=name, grid=(T // tm,),
        in_specs=[_row_spec(tm, D), _dev_rows_spec(R_WOUT)],
        out_specs=[_row_spec(tm, n) for n in Y_SPLITS],
        out_shape=[jax.ShapeDtypeStruct((T, n), F32) for n in Y_SPLITS],
        compiler_params=_cp("parallel"),
    )(dy, wb)


def ple_fwd(h, g, pe, wg, wp, name):
    T = h.shape[0]
    tm = min(TOK_TILE, T)

    def body(h_ref, g_ref, p_ref, wg_ref, wp_ref, o_ref):
        hh = h_ref[...]
        xn = _rms(hh, g_ref[...])[0]
        o_ref[...] = hh + _sigmoid(_mm(xn, _dev_rows(wg_ref, 0, D))) * _mm(p_ref[...], wp_ref[...])

    return pl.pallas_call(
        body, name=name, grid=(T // tm,),
        in_specs=[_row_spec(tm, D), _full_spec((1, D)), _row_spec(tm, PLE), _dev_rows_spec(R_PGATE),
                  _full_spec((PLE, D))],
        out_specs=_row_spec(tm, D),
        out_shape=jax.ShapeDtypeStruct((T, D), F32),
        compiler_params=_cp("parallel"),
    )(h, g, pe, wg, wp)


def ple_bwd(h, dy, g, pe, wg, wp, name):
    T = h.shape[0]
    tm = min(TOK_TILE, T)

    def body(h_ref, dy_ref, g_ref, p_ref, wg_ref, wp_ref, dh_ref, dz_ref, dpp_ref, xn_ref, dn_ref):
        @pl.when(pl.program_id(0) == 0)
        def _():
            dn_ref[...] = jnp.zeros_like(dn_ref)

        gg = g_ref[...]
        dy = dy_ref[...]
        xn, xhat, rstd = _rms(h_ref[...], gg)
        wg = _dev_rows(wg_ref, 0, D)
        gate = _sigmoid(_mm(xn, wg))
        pp = _mm(p_ref[...], wp_ref[...])
        dz = dy * pp * gate * (1.0 - gate)
        dz_ref[...] = dz.astype(BF16)
        dpp_ref[...] = (dy * gate).astype(BF16)
        xn_ref[...] = xn.astype(BF16)
        dh, dn = _rms_bwd(_mm_nt(dz, wg), xhat, rstd, gg)
        dh_ref[...] = dy + dh
        dn_ref[...] += dn

    return pl.pallas_call(
        body, name=name, grid=(T // tm,),
        in_specs=[_row_spec(tm, D), _row_spec(tm, D), _full_spec((1, D)), _row_spec(tm, PLE),
                  _dev_rows_spec(R_PGATE), _full_spec((PLE, D))],
        out_specs=[_row_spec(tm, D), _row_spec(tm, D), _row_spec(tm, D), _row_spec(tm, D), _full_spec((1, D))],
        out_shape=[jax.ShapeDtypeStruct((T, D), F32), jax.ShapeDtypeStruct((T, D), BF16),
                   jax.ShapeDtypeStruct((T, D), BF16), jax.ShapeDtypeStruct((T, D), BF16),
                   jax.ShapeDtypeStruct((1, D), F32)],
        compiler_params=_cp("arbitrary"),
    )(h, dy, g, pe, wg, wp)


def loss_head(h, g, target, name):
    T = h.shape[0]
    tm = min(TOK_TILE, T)

    def body(h_ref, g_ref, t_ref, loss_ref, dh_ref, dn_ref):
        @pl.when(pl.program_id(0) == 0)
        def _():
            dn_ref[...] = jnp.zeros_like(dn_ref)
            loss_ref[...] = jnp.zeros_like(loss_ref)

        gg = g_ref[...]
        y, xhat, rstd = _rms(h_ref[...], gg)
        err = y - t_ref[...]
        per_tok = jnp.mean(err * err, axis=-1, keepdims=True)
        loss_ref[...] += 0.5 * jnp.sum(per_tok, axis=0, keepdims=True)
        dh, dn = _rms_bwd(err * (1.0 / D), xhat, rstd, gg)
        dh_ref[...] = dh
        dn_ref[...] += dn

    return pl.pallas_call(
        body, name=name, grid=(T // tm,),
        in_specs=[_row_spec(tm, D), _full_spec((1, D)), _row_spec(tm, D)],
        out_specs=[_full_spec((8, LANE)), _row_spec(tm, D), _full_spec((1, D))],
        out_shape=[jax.ShapeDtypeStruct((8, LANE), F32), jax.ShapeDtypeStruct((T, D), F32),
                   jax.ShapeDtypeStruct((1, D), F32)],
        compiler_params=_cp("arbitrary"),
    )(h, g, target)


def _block_diag(w):
    return jnp.einsum('hij,hk->hikj', w, jnp.eye(4, dtype=w.dtype)).reshape(LRU_W, LRU_W)


def _layer_consts(W, l):
    row = lambda v: v.reshape(1, -1)
    zeros = jnp.zeros((5, LRU_W), F32)
    return dict(
        wa=_block_diag(W["lru_w_a"][l]), wx=_block_diag(W["lru_w_x"][l]),
        lru_vec=jnp.concatenate([row(W["lru_b_a"][l]), row(W["lru_b_x"][l]), row(W["lru_lambda"][l]), zeros], 0),
        lru_cb=row(W["lru_conv_b"][l]),
        sinks=W["attn_sinks"][l], rel=W["rel_bias"].reshape(-1),
        dn_cb=jnp.zeros((1, 3 * DN_W), F32),
        alog=row(jnp.repeat(W["dn_a_log"][l], HD)), dtb=row(jnp.repeat(W["dn_dt_bias"][l], HD)),
        dn_nl=row(jnp.tile(W["dn_norm"][l], DN_H)),
    )


def _layer_fwd(h0, pe, W, l, S, need=None):
    n = f"l{l}_"
    c_ = _layer_consts(W, l)
    row = lambda v: v.reshape(1, -1)
    need = need or (lambda *_: None)
    need(l, "f1", h0)
    h1, *ffn1_kept = ffn_fwd(h0, row(W["ffn1_norm"][l]), W["f1_cols"][l], W["f1_rows"][l], n + "ffn1_fwd")
    need(l, "in", h1)
    u_lru, u_att, u_dn, u_ba, xn_mix = mixin_fwd(h1, row(W["mix_norm"][l]), W["w_in"][l], n + "mixin_fwd")
    xr = conv_fwd(u_lru, W["lru_conv_w"][l], c_["lru_cb"], S, 0, LRU_W, n + "lru_conv_fwd")
    y_lru = lru_fwd(xr, u_lru, c_["wa"], c_["wx"], c_["lru_vec"], S, n + "lru_fwd")
    y_att = attn_fwd(u_att, c_["sinks"], c_["rel"], S, n + "attn_fwd")
    cc = conv_fwd(u_dn, W["dn_conv_w"][l], c_["dn_cb"], S, 0, 3 * DN_W, n + "dn_conv_fwd")
    q, k, v, g, beta = dn_point_fwd(cc, u_ba, c_["alog"], c_["dtb"], n + "dn_point_fwd")
    o, states, tinvs = dn_scan_fwd(q, k, v, g, beta, S, n + "dn_scan_fwd")
    y_dn = dn_gate_fwd(o, u_dn, c_["dn_nl"], n + "dn_gate_fwd")
    need(l, "rest", y_dn)
    h2, ycat = wout_fwd(h1, (y_lru, y_att, y_dn), W["r_rows"][l], n + "wout_fwd")
    h3, *ffn2_kept = ffn_fwd(h2, row(W["ffn2_norm"][l]), W["r_cols"][l], W["r_rows"][l], n + "ffn2_fwd")
    h4 = ple_fwd(h3, row(W["ple_norm"][l]), pe, W["r_rows"][l], W["ple_w_proj"][l], n + "ple_fwd")
    saved = dict(ffn1=ffn1_kept, ffn2=ffn2_kept, h0=h0, h1=h1, h2=h2, h3=h3, u_lru=u_lru, u_att=u_att, u_dn=u_dn,
                 u_ba=u_ba, xn_mix=xn_mix, xr=xr, cc=cc, q=q, k=k, v=v, g=g, beta=beta, o=o, states=states, tinvs=tinvs, ycat=ycat)
    return h4, saved


GM_WOUT, GM_PGATE, GM_WIN, GM_PPROJ, GM_END, GM_ROWS = 0, 128, 256, 560, 592, 640


def _layer_bwd(dh4, sv, pe, W, l, S, token=None, on_piece=None):
    n = f"l{l}_"
    c_ = _layer_consts(W, l)
    row = lambda v: v.reshape(1, -1)
    behind = lambda v, tok: v if tok is None else v + tok.astype(v.dtype)
    on_piece = on_piece or (lambda *_: None)
    G = {"mix_rows": jnp.zeros((N_DEV, GM_ROWS, D), BF16)}
    dh3, dz, dpp, xn_p, dn = ple_bwd(sv["h3"], dh4, behind(row(W["ple_norm"][l]), token), pe, W["r_rows"][l],
                                     W["ple_w_proj"][l], n + "ple_bwd")
    G["ple_norm"] = dn[0]
    G["mix_rows"] = grad_rows(xn_p, dz, G["mix_rows"], GM_PGATE, ROWS_DEV, n + "d_ple_w_gate")
    d_proj = matmul_tn(pe, dpp, n + "d_ple_w_proj")
    d_proj = d_proj.reshape(PLE, N_DEV, D // N_DEV).transpose(1, 0, 2).reshape(N_DEV, GM_END - GM_PPROJ, D)
    G["mix_rows"] = lax.dynamic_update_slice(G["mix_rows"], d_proj, (0, GM_PPROJ, 0))

    def ffn_back(which, cols_w, rows_w, h_in, dy, tok, one_by_one):
        gt, up, xn = sv[which]
        dh, dgt, dup, act, dn_ = ffn_bwd(h_in, dy, behind(row(W[which + "_norm"][l]), tok), gt, up, cols_w, rows_w,
                                         n + which + "_bwd")
        G[which + "_norm"] = dn_[0]
        zeros_rows = jnp.zeros((N_DEV, SHP, D), BF16)
        if one_by_one:
            G[which + "_gate"] = grad_cols(xn, dgt, lax.empty((1, D, FFP), BF16), 0, n + "d_" + which + "_w_gate")
            tok = on_piece(l, which + "_gate", (G[which + "_gate"],))
            G[which + "_up"] = grad_cols(xn, dup, behind(jnp.zeros((1, D, FFP), BF16), tok), 0,
                                         n + "d_" + which + "_w_up")
            tok = on_piece(l, which + "_up", (G[which + "_up"],))
            G[which + "_down"] = grad_rows(act, dy, behind(zeros_rows, tok), 0, SHP, n + "d_" + which + "_w_down",
                                           scale=0.5)
            return dh, on_piece(l, which + "_down", (G[which + "_down"],))
        cols = grad_cols(xn, dgt, lax.empty((2, D, FFP), BF16), 0, n + "d_" + which + "_w_gate")
        G[which + "_cols"] = grad_cols(xn, dup, cols, 1, n + "d_" + which + "_w_up")
        G[which + "_rows"] = grad_rows(act, dy, lax.empty((N_DEV, SHP, D), BF16), 0, SHP,
                                       n + "d_" + which + "_w_down", scale=0.5)
        return dh, on_piece(l, which, (G[which + "_cols"], G[which + "_rows"]))

    dh2, tok = ffn_back("ffn2", W["r_cols"][l], W["r_rows"][l], sv["h2"], dh3, None, False)
    dy_lru, dy_att, dy_dn = wout_bwd(dh2, W["r_rows"][l], n + "wout_bwd")
    G["mix_rows"] = grad_rows(sv["ycat"], dh2, G["mix_rows"], GM_WOUT, ROWS_DEV, n + "d_w_out")
    do, dz_dn, dnn = dn_gate_bwd(sv["o"], sv["u_dn"], behind(c_["dn_nl"], tok), dy_dn, n + "dn_gate_bwd")
    dqkvgb = dn_scan_bwd(sv["q"], sv["k"], sv["v"], sv["g"], sv["beta"], sv["states"], sv["tinvs"], do, S,
                         n + "dn_scan_bwd")
    dcc, du_ba, dvec_dn = dn_point_bwd(sv["cc"], sv["u_ba"], c_["alog"], c_["dtb"], dqkvgb, n + "dn_point_bwd")
    dqkv, dwb_dn = conv_bwd(sv["u_dn"], dcc, W["dn_conv_w"][l], S, 0, 3 * DN_W, n + "dn_conv_bwd")
    G["dn_norm"] = dnn[0, 0:HD]
    G["dn_a_log"] = dvec_dn[0, 0:DN_H]
    G["dn_dt_bias"] = dvec_dn[1, 0:DN_H]
    G["dn_conv_w"] = dwb_dn[0:4]
    du_att, drel, dsk = attn_bwd(sv["u_att"], dy_att, c_["sinks"], c_["rel"], S, n + "attn_bwd")
    G["attn_sinks"] = dsk[:, 0]
    G["rel_bias"] = drel[:, 0:REL_BUCKETS].T
    dxr, dgt_lru, dwa, dwx, dvec = lru_bwd(sv["xr"], sv["u_lru"], dy_lru, c_["wa"], c_["wx"], c_["lru_vec"], S,
                                           n + "lru_bwd")
    dx_lru, dwb_lru = conv_bwd(sv["u_lru"], dxr, W["lru_conv_w"][l], S, 0, LRU_W, n + "lru_conv_bwd")
    diag = lambda m: jnp.stack([m[c, HD * e:HD * (e + 1), HD * e:HD * (e + 1)] for c in range(2) for e in range(2)])
    G["lru_w_a"], G["lru_w_x"] = diag(dwa), diag(dwx)
    G["lru_b_a"], G["lru_b_x"], G["lru_lambda"] = dvec[0], dvec[1], dvec[2]
    G["lru_conv_w"], G["lru_conv_b"] = dwb_lru[0:4], dwb_lru[4]
    dh1, du_cat, dn = mixin_bwd(sv["h1"], dh2, row(W["mix_norm"][l]), W["w_in"][l],
                                (dx_lru, dgt_lru, du_att, dqkv, dz_dn, du_ba), n + "mixin_bwd")
    G["mix_norm"] = dn[0]
    d_in = matmul_tn(sv["xn_mix"], du_cat, n + "d_w_in")[:, :D_IN]
    d_in = d_in.reshape(D, N_DEV, D_IN // N_DEV).transpose(1, 0, 2).reshape(N_DEV, WIN_ROWS, D)
    d_in = jnp.pad(d_in, ((0, 0), (0, GM_PPROJ - GM_WIN - WIN_ROWS), (0, 0)))
    G["mix_rows"] = lax.dynamic_update_slice(G["mix_rows"], d_in, (0, GM_WIN, 0))
    tok = on_piece(l, "mix", (G["mix_rows"],))
    dh0, tok = ffn_back("ffn1", W["f1_cols"][l], W["f1_rows"][l], sv["h0"], dh1, tok, l == 0)
    return dh0, G, tok


def _core(x, pe, W, target, S, need=None, on_piece=None):
    h = x
    saved = []
    for l in range(DEPTH):
        h, sv = _layer_fwd(h, pe[l], W, l, S, need)
        saved.append(sv)
    loss_tile, dh, dfn = loss_head(h, W["final_norm"].reshape(1, -1), target, "loss_head")
    grads = [None] * DEPTH
    token = None
    for l in reversed(range(DEPTH)):
        dh, grads[l], token = _layer_bwd(dh, saved[l], pe[l], W, l, S, token, on_piece)
    return loss_tile[0, 0], dh, grads, dfn[0]


MESH_ID = pl.DeviceIdType.MESH
ANY_SPEC = pl.BlockSpec(memory_space=pl.ANY)
AXES = ("x", "y", "c")


def _my_pos():
    return lax.axis_index("x"), lax.axis_index("y"), lax.axis_index("c")


def _slot_of(px, py, pc):
    return 4 * px + 2 * py + pc


def all_gather(x, name):
    R, C = x.shape

    def body(x_ref, out_ref, send_sems, recv_sems, local_sem):
        mx, my, mc = _my_pos()
        me, sibling = (mx, my, mc), (mx, my, 1 - mc)
        chips = [(1 - mx, my), (mx, 1 - my), (1 - mx, 1 - my)]

        def copy(k, block, to, src=None):
            dst = out_ref.at[_slot_of(*block)]
            return pltpu.make_async_remote_copy(
                src_ref=dst if src is None else src, dst_ref=dst,
                send_sem=send_sems.at[k], recv_sem=recv_sems.at[k],
                device_id=to, device_id_type=MESH_ID)

        mine = pltpu.make_async_copy(x_ref, out_ref.at[_slot_of(*me)], local_sem)
        mine.start()
        first = [copy(0, me, sibling, src=x_ref)]
        first += [copy(1 + j, me, (*chip, mc), src=x_ref) for j, chip in enumerate(chips)]
        for cp in first:
            cp.start()
        passed = [copy(4 + j, (*chip, mc), sibling) for j, chip in enumerate(chips)]
        for j, chip in enumerate(chips):
            copy(1 + j, (*chip, mc), me).wait_recv()
            passed[j].start()
        copy(0, sibling, me).wait_recv()
        for j, chip in enumerate(chips):
            copy(4 + j, (*chip, 1 - mc), me).wait_recv()
        for cp in first + passed:
            cp.wait_send()
        mine.wait()

    return pl.pallas_call(
        body, name=name,
        out_shape=jax.ShapeDtypeStruct((N_DEV, R, C), x.dtype),
        in_specs=[ANY_SPEC], out_specs=ANY_SPEC,
        scratch_shapes=[pltpu.SemaphoreType.DMA((7,)), pltpu.SemaphoreType.DMA((7,)), pltpu.SemaphoreType.DMA],
    )(x)


def _col_window(ref, slot):
    return ref.at[:, pl.ds(pl.multiple_of(slot * SHP, LANE), SHP)]


def gather_layer(a_sh, b_sh, name):
    def body(a_ref, b_ref, ao_ref, bo_ref, send_sems, recv_sems, local_sems):
        mx, my, mc = _my_pos()
        me, sibling = (mx, my, mc), (mx, my, 1 - mc)
        chips = [(1 - mx, my), (mx, 1 - my), (1 - mx, 1 - my)]

        def copies(k, block, to, own=False):
            slot = _slot_of(*block)
            dsts = (_col_window(ao_ref, slot), bo_ref.at[slot])
            srcs = (a_ref, b_ref) if own else dsts
            return [pltpu.make_async_remote_copy(
                src_ref=s, dst_ref=d, send_sem=send_sems.at[2 * k + i], recv_sem=recv_sems.at[2 * k + i],
                device_id=to, device_id_type=MESH_ID) for i, (s, d) in enumerate(zip(srcs, dsts))]

        mine = [pltpu.make_async_copy(a_ref, _col_window(ao_ref, _slot_of(*me)), local_sems.at[0]),
                pltpu.make_async_copy(b_ref, bo_ref.at[_slot_of(*me)], local_sems.at[1])]
        for cp in mine:
            cp.start()
        first = copies(0, me, sibling, own=True)
        for j, chip in enumerate(chips):
            first += copies(1 + j, me, (*chip, mc), own=True)
        for cp in first:
            cp.start()
        passed = []
        for j, chip in enumerate(chips):
            for cp in copies(1 + j, (*chip, mc), me):
                cp.wait_recv()
            fwd = copies(4 + j, (*chip, mc), sibling)
            for cp in fwd:
                cp.start()
            passed += fwd
        for cp in copies(0, sibling, me):
            cp.wait_recv()
        for j, chip in enumerate(chips):
            for cp in copies(4 + j, (*chip, 1 - mc), me):
                cp.wait_recv()
        for cp in first + passed:
            cp.wait_send()
        for cp in mine:
            cp.wait()

    return pl.pallas_call(
        body, name=name,
        out_shape=[jax.ShapeDtypeStruct((a_sh.shape[0], FFP), a_sh.dtype),
                   jax.ShapeDtypeStruct((N_DEV,) + b_sh.shape, b_sh.dtype)],
        in_specs=[ANY_SPEC, ANY_SPEC], out_specs=[ANY_SPEC, ANY_SPEC],
        scratch_shapes=[pltpu.SemaphoreType.DMA((14,)), pltpu.SemaphoreType.DMA((14,)),
                        pltpu.SemaphoreType.DMA((2,))],
    )(a_sh, b_sh)


HBM_SPEC = pl.BlockSpec(memory_space=pltpu.HBM)
SEM_SPEC = pl.BlockSpec(memory_space=pltpu.SEMAPHORE)
SPLIT_EFFECT = pltpu.CompilerParams(has_side_effects=pltpu.SideEffectType.DATAFLOW_SIDE_EFFECTING)


def _split_ends(mode, src_ref, dst_ref, src_slot, dst_slot):
    cols = mode.endswith("cols")
    if mode.startswith("gather"):
        return src_ref, (_col_window(dst_ref, dst_slot) if cols else dst_ref.at[dst_slot])
    return (_col_window(src_ref, src_slot) if cols else src_ref.at[src_slot]), dst_ref.at[dst_slot]


def _split_peers():
    mx, my, mc = _my_pos()
    for r in range(1, N_DEV):
        peer = (1 - mx if r & 4 else mx, 1 - my if r & 2 else my, 1 - mc if r & 1 else mc)
        yield r - 1, peer, _slot_of(*peer)


def split_start(modes, srcs, dsts, name, after=()):
    n = len(modes)
    m = len(after)

    def body(*refs):
        send_sems, recv_sems, token = refs[2 * n + m], refs[2 * n + m + 1], refs[-1]
        mine = _slot_of(*_my_pos())
        for k, peer, ps in _split_peers():
            for i in range(n):
                src, dst = _split_ends(modes[i], refs[i], refs[n + i], ps, mine)
                pltpu.make_async_remote_copy(
                    src_ref=src, dst_ref=dst, send_sem=send_sems.at[n * k + i], recv_sem=recv_sems.at[n * k + i],
                    device_id=peer, device_id_type=MESH_ID).start()
        for i in range(n):
            src, dst = _split_ends(modes[i], refs[i], refs[n + i], mine, mine)
            pltpu.make_async_copy(src, dst, recv_sems.at[n * (N_DEV - 1) + i]).start()
        token[...] = jnp.zeros_like(token)

    bufs = tuple(srcs) + tuple(dsts)
    sems = pltpu.SemaphoreType.DMA((n * N_DEV,))
    res = pl.pallas_call(
        body, name=name,
        out_shape=(sems, sems) + tuple(pltpu.HBM(t.shape, t.dtype) for t in bufs)
        + (jax.ShapeDtypeStruct((8, LANE), F32),),
        in_specs=[HBM_SPEC] * (2 * n) + [ANY_SPEC] * m,
        out_specs=(SEM_SPEC, SEM_SPEC) + (HBM_SPEC,) * (2 * n) + (pl.BlockSpec(memory_space=pltpu.VMEM),),
        input_output_aliases={i: 2 + i for i in range(2 * n)},
        compiler_params=SPLIT_EFFECT,
    )(*(pltpu.with_memory_space_constraint(t, pltpu.HBM) for t in bufs), *after)
    return list(res[:-1]), res[-1]


def split_wait(modes, started, after, name):
    n = len(modes)
    after = tuple(after) if isinstance(after, (tuple, list)) else (after,)
    send_sems, recv_sems, bufs = started[0], started[1], started[2:]

    def body(*refs):
        send_sems, recv_sems = refs[2 * n], refs[2 * n + 1]
        mine = _slot_of(*_my_pos())
        for k, peer, ps in _split_peers():
            for i in range(n):
                sent = _split_ends(modes[i], refs[i], refs[n + i], ps, mine)[0]
                landed = _split_ends(modes[i], refs[i], refs[n + i], mine, ps)[1]
                cp = pltpu.make_async_remote_copy(
                    src_ref=sent, dst_ref=landed, send_sem=send_sems.at[n * k + i],
                    recv_sem=recv_sems.at[n * k + i], device_id=peer, device_id_type=MESH_ID)
                cp.wait_send()
                cp.wait_recv()
        for i in range(n):
            src, dst = _split_ends(modes[i], refs[i], refs[n + i], mine, mine)
            pltpu.make_async_copy(src, dst, recv_sems.at[n * (N_DEV - 1) + i]).wait()

    res = pl.pallas_call(
        body, name=name,
        out_shape=tuple(pltpu.HBM(t.shape, t.dtype) for t in bufs),
        in_specs=[HBM_SPEC] * (2 * n) + [SEM_SPEC, SEM_SPEC] + [ANY_SPEC] * len(after),
        out_specs=(HBM_SPEC,) * (2 * n),
        input_output_aliases={i: i for i in range(2 * n)},
        compiler_params=SPLIT_EFFECT,
    )(*bufs, send_sems, recv_sems, *after)
    return list(res[n:])


def sum_parts(parts, name):
    _, R, C = parts.shape
    tr = _pick(R, (512, 336, 272, 256, 128, 64, 32, 16, 8))

    def body(p_ref, o_ref):
        acc = p_ref[0].astype(F32)
        for k in range(1, N_DEV):
            acc += p_ref[k].astype(F32)
        o_ref[...] = acc

    return pl.pallas_call(
        body, name=name, grid=(R // tr,),
        in_specs=[pl.BlockSpec((N_DEV, tr, C), lambda i: (0, i, 0))],
        out_specs=pl.BlockSpec((tr, C), lambda i: (i, 0)),
        out_shape=jax.ShapeDtypeStruct((R, C), F32),
        compiler_params=_cp("parallel"),
    )(parts)


def sum_into(parts, row0, rows, cols, layer, dst, name):
    tr = _pick(rows, (512, 352, 128))
    blk0 = row0 // tr

    def body(p_ref, *rest):
        o_ref = rest[-1]
        acc = p_ref[0, :, 0:cols].astype(F32)
        for k in range(1, N_DEV):
            acc += p_ref[k, :, 0:cols].astype(F32)
        o_ref[0] = acc

    aliased = dst is not None
    return pl.pallas_call(
        body, name=name, grid=(rows // tr,),
        in_specs=[pl.BlockSpec((N_DEV, tr, parts.shape[2]), lambda i: (0, blk0 + i, 0))]
        + [pl.BlockSpec(memory_space=pl.ANY)] * aliased,
        out_specs=pl.BlockSpec((1, tr, cols), lambda i: (layer, i, 0)),
        out_shape=jax.ShapeDtypeStruct((DEPTH, rows, cols), F32),
        input_output_aliases={1: 0} if aliased else {},
        compiler_params=_cp("parallel"),
    )(*((parts, dst) if aliased else (parts,)))


def adamw(g, w, m, v, name):
    lead, (R, C) = g.shape[:-2], g.shape[-2:]
    tr = _pick(R, (512, 352, 256, 128, 64, 32, 16, 8))
    c1 = 1.0 - ADAM_B1 ** ADAM_STEP
    c2 = 1.0 - ADAM_B2 ** ADAM_STEP

    def body(g_ref, w_ref, m_ref, v_ref, d_ref, nm_ref, nv_ref):
        gg = g_ref[...]
        mm = ADAM_B1 * m_ref[...] + (1.0 - ADAM_B1) * gg
        vv = ADAM_B2 * v_ref[...] + (1.0 - ADAM_B2) * (gg * gg)
        nm_ref[...] = mm
        nv_ref[...] = vv
        d_ref[...] = -ADAM_LR * ((mm / c1) / (jnp.sqrt(vv / c2) + ADAM_EPS) + ADAM_WD * w_ref[...])

    if lead:
        spec = pl.BlockSpec((1, tr, C), lambda l, i: (l, i, 0))
    else:
        spec = pl.BlockSpec((tr, C), lambda l, i: (i, 0))
    return pl.pallas_call(
        body, name=name, grid=(lead[0] if lead else 1, R // tr),
        in_specs=[spec] * 4, out_specs=[spec] * 3,
        out_shape=[jax.ShapeDtypeStruct(g.shape, F32)] * 3,
        compiler_params=_cp("parallel", "parallel"),
    )(g, w, m, v)


BIG = (("ffn1_w_gate", 1, D, FF), ("ffn1_w_up", 1, D, FF), ("ffn1_w_down", 0, FF, D),
       ("w_in", 1, D, D_IN), ("w_out", 0, D, D),
       ("ffn2_w_gate", 1, D, FF), ("ffn2_w_up", 1, D, FF), ("ffn2_w_down", 0, FF, D),
       ("ple_w_gate", 0, D, D), ("ple_w_proj", 1, PLE, D))
SMALL = (("ffn1_norm", (D,), None), ("mix_norm", (D,), None), ("lru_conv_w", (4, LRU_W), LRU_W // N_DEV),
         ("lru_conv_b", (LRU_W,), None), ("lru_w_a", (4, HD, HD), None), ("lru_b_a", (LRU_W,), None),
         ("lru_w_x", (4, HD, HD), None), ("lru_b_x", (LRU_W,), None), ("lru_lambda", (LRU_W,), None),
         ("attn_sinks", (ATT_H,), None), ("dn_conv_w", (4, 3 * DN_W), 3 * DN_W // N_DEV),
         ("dn_a_log", (DN_H,), None), ("dn_dt_bias", (DN_H,), None), ("dn_norm", (HD,), None),
         ("ffn2_norm", (D,), None), ("ple_norm", (D,), None))
SINGLE = (("rel_bias", (REL_BUCKETS, ATT_H)), ("final_norm", (D,)))


def _pack_rows(arrs, width, mult):
    flat = jnp.concatenate([a.reshape(-1) for a in arrs])
    rows = -(-flat.shape[0] // (width * mult)) * mult
    return jnp.pad(flat, (0, rows * width - flat.shape[0])).reshape(rows, width)


def _unpack_rows(packed, shapes):
    flat = packed.reshape(-1)
    out, off = [], 0
    for s in shapes:
        n = int(np.prod(s))
        out.append(flat[off:off + n].reshape(s))
        off += n
    return out


def _pad_rows(w, r):
    return jnp.pad(w, ((0, r - w.shape[0]), (0, 0)))


def _shard_ffn(a, l, which, more=()):
    cols = jnp.concatenate([a[which + "_w_gate"][l], a[which + "_w_up"][l]], axis=0)
    rows = jnp.concatenate([_pad_rows(a[which + "_w_down"][l], SHP)] + list(more), axis=0)
    return jnp.pad(cols, ((0, 0), (0, SHP - SH))).astype(BF16), rows.astype(BF16)


def _shards(a, l):
    w_in_rows = _pad_rows(a["w_in"][l].reshape(WIN_ROWS, D), IN_ROWS).astype(BF16)
    rest = _shard_ffn(a, l, "ffn2", (a["w_out"][l], a["ple_w_gate"][l], a["ple_w_proj"][l].reshape(-1, D)))
    return _shard_ffn(a, l, "ffn1"), (w_in_rows,), rest


def _full_w_in(in_rows):
    sh = in_rows[:, :WIN_ROWS, :].reshape(N_DEV, D, D_IN // N_DEV)
    return jnp.pad(sh.transpose(1, 0, 2).reshape(D, D_IN), ((0, 0), (0, D_IN_PAD - D_IN)))


def _full_ple_proj(r_rows):
    sh = r_rows[:, R_PPROJ:R_ROWS, :].reshape(N_DEV, PLE, D // N_DEV)
    return sh.transpose(1, 0, 2).reshape(PLE, D)


PIECE_NAMES = {"ffn1": ("ffn1_w_gate", "ffn1_w_up", "ffn1_w_down"), "ffn2": ("ffn2_w_gate", "ffn2_w_up", "ffn2_w_down"),
               "mix": ("w_out", "ple_w_gate", "w_in", "ple_w_proj")}


def _shard_grads(piece, summed):
    if piece == "mix":
        rows, = summed
        return {"w_out": rows[GM_WOUT:GM_WOUT + ROWS_DEV], "ple_w_gate": rows[GM_PGATE:GM_PGATE + ROWS_DEV],
                "w_in": rows[GM_WIN:GM_WIN + WIN_ROWS].reshape(D, D_IN // N_DEV),
                "ple_w_proj": rows[GM_PPROJ:GM_END].reshape(PLE, D // N_DEV)}
    if piece in ("ffn1_gate", "ffn1_up"):
        return {piece.replace("_", "_w_"): summed[0][:, :SH]}
    if piece == "ffn1_down":
        return {"ffn1_w_down": summed[0][:SH]}
    cols, rows = summed
    return {piece + "_w_gate": cols[:D, :SH], piece + "_w_up": cols[D:, :SH], piece + "_w_down": rows[:SH]}


def kernel(x, p, ffn1_norm, ffn1_w_gate, ffn1_w_up, ffn1_w_down, mix_norm, w_in, lru_conv_w, lru_conv_b, lru_w_a, lru_b_a, lru_w_x, lru_b_x, lru_lambda, attn_sinks, rel_bias, dn_conv_w, dn_a_log, dn_dt_bias, dn_norm, w_out, ffn2_norm, ffn2_w_gate, ffn2_w_up, ffn2_w_down, ple_norm, ple_w_gate, ple_w_proj, final_norm, loss_target, m_ffn1_norm, m_ffn1_w_gate, m_ffn1_w_up, m_ffn1_w_down, m_mix_norm, m_w_in, m_lru_conv_w, m_lru_conv_b, m_lru_w_a, m_lru_b_a, m_lru_w_x, m_lru_b_x, m_lru_lambda, m_attn_sinks, m_rel_bias, m_dn_conv_w, m_dn_a_log, m_dn_dt_bias, m_dn_norm, m_w_out, m_ffn2_norm, m_ffn2_w_gate, m_ffn2_w_up, m_ffn2_w_down, m_ple_norm, m_ple_w_gate, m_ple_w_proj, m_final_norm, v_ffn1_norm, v_ffn1_w_gate, v_ffn1_w_up, v_ffn1_w_down, v_mix_norm, v_w_in, v_lru_conv_w, v_lru_conv_b, v_lru_w_a, v_lru_b_a, v_lru_w_x, v_lru_b_x, v_lru_lambda, v_attn_sinks, v_rel_bias, v_dn_conv_w, v_dn_a_log, v_dn_dt_bias, v_dn_norm, v_w_out, v_ffn2_norm, v_ffn2_w_gate, v_ffn2_w_up, v_ffn2_w_down, v_ple_norm, v_ple_w_gate, v_ple_w_proj, v_final_norm):
    a = dict(locals())
    nb, S, _ = x.shape
    T = nb * S
    my_slot = _slot_of(*_my_pos())

    W = {k: [None] * DEPTH for k in ("f1_cols", "f1_rows", "w_in", "r_cols", "r_rows", "ple_w_proj")}
    GATHER, SCATTER = ("gather_cols", "gather_block"), ("scatter_cols", "scatter_block")
    GROUP_MODES = {"f1": GATHER, "in": GATHER[1:], "rest": GATHER}

    def set_group(l, group, bufs):
        if group == "f1":
            W["f1_cols"][l], W["f1_rows"][l] = bufs
        elif group == "in":
            W["w_in"][l] = _full_w_in(bufs[0])
        else:
            W["r_cols"][l], W["r_rows"][l] = bufs
            W["ple_w_proj"][l] = _full_ple_proj(bufs[1])

    def landing(mode, src):
        if mode == "gather_cols":
            return lax.empty((src.shape[0], FFP), src.dtype)
        if mode == "scatter_cols":
            return lax.empty((N_DEV, src.shape[0], SHP), src.dtype)
        return lax.empty((N_DEV,) + src.shape[mode == "scatter_block":], src.dtype)

    def start(modes, srcs, name, after=()):
        return split_start(modes, srcs, [landing(m, s) for m, s in zip(modes, srcs)], name, after)

    shards0, shards1 = _shards(a, 0), _shards(a, 1)
    set_group(0, "f1", gather_layer(*shards0[0], "gather_weights_l0_ffn1"))
    taps = all_gather(_pack_rows([lru_conv_w, dn_conv_w], LANE, 8), "gather_conv_taps")
    flat_taps = taps.reshape(N_DEV, -1)
    for name, first, tap in (("lru_conv_w", 0, lru_conv_w), ("dn_conv_w", lru_conv_w.size, dn_conv_w)):
        per_dev = flat_taps[:, first:first + tap.size].reshape((N_DEV,) + tap.shape)
        W[name] = jnp.moveaxis(per_dev, 0, -2).reshape(tap.shape[:-1] + (N_DEV * tap.shape[-1],))
    for name, _, cols in SMALL:
        if cols is None:
            W[name] = a[name]
    W["rel_bias"], W["final_norm"] = rel_bias, final_norm

    gathers, after = {}, (W["f1_rows"][0], taps)
    for l, group, srcs in ((0, "in", shards0[1]), (0, "rest", shards0[2]),
                           (1, "f1", shards1[0]), (1, "in", shards1[1]), (1, "rest", shards1[2])):
        gathers[l, group], token = start(GROUP_MODES[group], srcs, f"gather_start_l{l}_{group}", after)
        after = (token,)
    W["ffn1_norm"] = ffn1_norm + token[0, 0]
    flight, tokens = {}, {}

    def need(l, group, h):
        if (l, group) in gathers:
            set_group(l, group, split_wait(GROUP_MODES[group], gathers[l, group], h, f"gather_wait_l{l}_{group}"))

    def piece_modes(piece):
        return {"mix": SCATTER[1:], "ffn1_gate": SCATTER[:1], "ffn1_up": SCATTER[:1], "ffn1_down": SCATTER[1:]}.get(
            piece, SCATTER)

    def on_piece(l, piece, bufs):
        bufs = [b.reshape(-1, FFP) if b.shape[-1] == FFP else b for b in bufs]
        flight[l, piece], tokens[l, piece] = start(piece_modes(piece), bufs, f"exchange_start_l{l}_{piece}")
        return tokens[l, piece][0, 0]

    loss_local, dx, grads, d_final = _core(x.reshape(T, D), p.reshape(DEPTH, T, PLE), W,
                                           loss_target.reshape(T, D), S, need, on_piece)
    loss = lax.psum(loss_local, AXES)

    small_full = [jnp.stack([grads[l][name] for l in range(DEPTH)]) for name, _, _ in SMALL]
    small_full += [grads[0]["rel_bias"] + grads[1]["rel_bias"], d_final]
    small_flight, _ = start(("gather_block",), (_pack_rows(small_full, LANE, 8),), "gather_start_small_grads",
                            (tokens[0, "ffn1_down"],))

    out = {}

    landed = {}

    def land(l, piece, after):
        landed[l, piece] = split_wait(piece_modes(piece), flight[l, piece], after, f"exchange_wait_l{l}_{piece}")

    def where(name, l):
        if name in ("w_out", "ple_w_gate"):
            return "mix", 0, (GM_WOUT if name == "w_out" else GM_PGATE), ROWS_DEV, D
        ffn, kind = name[:4], name[7:]
        one_by_one = (l, ffn) == (0, "ffn1")
        if kind == "down":
            return (ffn + "_down", 0, 0, SH, D) if one_by_one else (ffn, 1, 0, SH, D)
        if one_by_one:
            return f"{ffn}_{kind}", 0, 0, D, SH
        return ffn, 0, (0 if kind == "gate" else D), D, SH

    def update(piece):
        for name in PIECE_NAMES[piece]:
            if name in ("w_in", "ple_w_proj"):
                g = jnp.stack([_shard_grads("mix", [mix_sums[l]])[name] for l in range(DEPTH)])
            else:
                g = None
                for l in reversed(range(DEPTH)):
                    piece_l, idx, row0, rows, cols = where(name, l)
                    g = sum_into(landed[l, piece_l][idx], row0, rows, cols, l, g, f"sum_{name}_l{l}")
            out[name] = (g,) + tuple(adamw(g, a[name], a["m_" + name], a["v_" + name], "adamw_" + name))

    for l, piece in ((1, "ffn2"), (1, "mix"), (1, "ffn1"), (0, "ffn2"), (0, "mix")):
        land(l, piece, (dx, tokens[0, "ffn1_down"]))
    mix_sums = [sum_parts(landed[l, "mix"][0], f"sum_mix_grads_l{l}") for l in range(DEPTH)]
    update("ffn2")
    update("mix")
    done_early = tuple(out[n][1] for n in PIECE_NAMES["ffn2"] + PIECE_NAMES["mix"])
    small_parts, = split_wait(("gather_block",), small_flight, done_early, "gather_wait_small_grads")
    small_sum = sum_parts(small_parts, "sum_small_grads")
    g_small = dict(zip([n for n, _, _ in SMALL] + [n for n, _ in SINGLE],
                       _unpack_rows(small_sum, [s.shape for s in small_full])))
    for name, _, cols in SMALL:
        if cols is not None:
            g_small[name] = lax.dynamic_slice_in_dim(g_small[name], my_slot * cols, cols, axis=2)

    for n in [n for n, _, _ in SMALL] + [n for n, _ in SINGLE]:
        shape = a[n].shape
        flat = lambda t: t.reshape((-1, shape[-1]) if len(shape) > 1 else (1, -1))
        res = adamw(flat(g_small[n]), flat(a[n]), flat(a["m_" + n]), flat(a["v_" + n]), "adamw_" + n)
        out[n] = (g_small[n].reshape(shape),) + tuple(r.reshape(shape) for r in res)

    for piece in ("ffn1_gate", "ffn1_up", "ffn1_down"):
        land(0, piece, (out["final_norm"][1],) + done_early)
    update("ffn1")

    order = ['ffn1_norm', 'ffn1_w_gate', 'ffn1_w_up', 'ffn1_w_down', 'mix_norm', 'w_in', 'lru_conv_w', 'lru_conv_b',
             'lru_w_a', 'lru_b_a', 'lru_w_x', 'lru_b_x', 'lru_lambda', 'attn_sinks', 'rel_bias', 'dn_conv_w',
             'dn_a_log', 'dn_dt_bias', 'dn_norm', 'w_out', 'ffn2_norm', 'ffn2_w_gate', 'ffn2_w_up', 'ffn2_w_down',
             'ple_norm', 'ple_w_gate', 'ple_w_proj', 'final_norm']
    return (loss, dx.reshape(x.shape)) + tuple(out[n][k] for k in range(4) for n in order)
```

```python
import functools
import math

import numpy as np
import jax
import jax.numpy as jnp
from jax import lax
from jax.experimental import pallas as pl
from jax.experimental.pallas import tpu as pltpu

F32 = jnp.float32
BF16 = jnp.bfloat16
HI = lax.Precision.HIGHEST

D = 1024
DEPTH = 2
EPS = 1e-6
PLE = 256
FF = 2816
HD = 64
LRU_W = 256
LRU_C = 8.0
ATT_W = 512
ATT_H = 8
ATT_KV = 2
ATT_G = 4
KV_W = 128
WINDOW = 128
BQ = 128
REL_BUCKETS = 32
REL_MAX_DIST = 128
DN_W = 256
DN_H = 4
CHUNK = 64
D_IN = 2312
D_IN_PAD = 2432
N_DEV = 8

ADAM_LR = 0.001
ADAM_B1 = 0.9
ADAM_B2 = 0.999
ADAM_EPS = 1e-08
ADAM_WD = 0.01
ADAM_STEP = 10

LANE = 128
VMEM_LIMIT = 56 * 1024 * 1024
SH = FF // N_DEV
SHP = 384
FFP = N_DEV * SHP
FF_TILE = 2 * SHP
FF_SUB = 256
TOK_TILE = 512
R_DOWN2, R_WOUT, R_PGATE, R_PPROJ, R_ROWS = 0, 384, 512, 640, 672
WIN_ROWS = D * D_IN // N_DEV // 1024
IN_ROWS = 304
NEG = -1e30


def _cp(*sem):
    return pltpu.CompilerParams(dimension_semantics=tuple(sem), vmem_limit_bytes=VMEM_LIMIT)


def _dg(a, b, ca, cb, exact):
    dims = (((ca,), (cb,)), ((), ()))
    if exact == "f32":
        return lax.dot_general(a.astype(F32), b.astype(F32), dims, precision=HI, preferred_element_type=F32)
    if exact == "split":
        a_hi, b_hi = a.astype(BF16), b.astype(BF16)
        a_lo = (a - a_hi.astype(F32)).astype(BF16)
        b_lo = (b - b_hi.astype(F32)).astype(BF16)
        dot = lambda u, v: lax.dot_general(u, v, dims, preferred_element_type=F32)
        return dot(a_hi, b_hi) + (dot(a_hi, b_lo) + dot(a_lo, b_hi))
    return lax.dot_general(a.astype(BF16), b.astype(BF16), dims, preferred_element_type=F32)


def _make_mm(exact):
    @jax.custom_vjp
    def mm(a, b):
        return _dg(a, b, 1, 0, exact)

    @jax.custom_vjp
    def mm_nt(a, b):
        return _dg(a, b, 1, 1, exact)

    @jax.custom_vjp
    def mm_tn(a, b):
        return _dg(a, b, 0, 0, exact)

    mm.defvjp(lambda a, b: (mm(a, b), (a, b)),
              lambda r, d: (mm_nt(d, r[1]), mm_tn(r[0], d)))
    mm_nt.defvjp(lambda a, b: (mm_nt(a, b), (a, b)),
                 lambda r, d: (mm(d, r[1]), mm_tn(d, r[0])))
    mm_tn.defvjp(lambda a, b: (mm_tn(a, b), (a, b)),
                 lambda r, d: (mm_nt(r[1], d), mm(r[0], d)))
    return mm, mm_nt, mm_tn


_mm, _mm_nt, _mm_tn = _make_mm("bf16")
_mmx, _mmx_nt, _mmx_tn = _make_mm("f32")
_mm3, _mm3_nt, _mm3_tn = _make_mm("split")


def _iota(shape, dim):
    return lax.broadcasted_iota(jnp.int32, shape, dim)


def _sigmoid(x):
    return 0.5 * jnp.tanh(0.5 * x) + 0.5


def _rms(h, g):
    rstd = lax.rsqrt(jnp.mean(h * h, axis=-1, keepdims=True) + EPS)
    xhat = h * rstd
    return xhat * g, xhat, rstd


def _rms_bwd(dxn, xhat, rstd, g):
    dxhat = dxn * g
    dh = rstd * (dxhat - xhat * jnp.mean(dxhat * xhat, axis=-1, keepdims=True))
    dg = jnp.sum(dxn * xhat, axis=0, keepdims=True)
    return dh, dg


def _row_spec(tm, n):
    return pl.BlockSpec((tm, n), lambda i, *_: (i, 0))


def _full_spec(shape):
    nd = len(shape)
    return pl.BlockSpec(shape, lambda *_: (0,) * nd)


def _ffn_weight_specs():
    return [pl.BlockSpec((D, FF_TILE), lambda i, j: (0, j)),
            pl.BlockSpec((D, FF_TILE), lambda i, j: (1, j)),
            pl.BlockSpec((2, SHP, D), lambda i, j: (j, 0, 0))]


def ffn_fwd(h, g, wa, wb, name):
    T = h.shape[0]
    tm = min(2 * TOK_TILE, T)
    nj = FFP // FF_TILE

    def body(h_ref, g_ref, wg_ref, wu_ref, wd_ref, o_ref, gt_ref, up_ref, xn_ref):
        j = pl.program_id(1)

        @pl.when(j == 0)
        def _():
            hh = h_ref[...]
            xn_ref[...] = _rms(hh, g_ref[...])[0].astype(BF16)
            o_ref[...] = hh

        blocks = [slice(c, c + FF_SUB) for c in range(0, FF_TILE, FF_SUB)]
        xn = xn_ref[...]
        wd = wd_ref[...].reshape(FF_TILE, D)
        gt = [_mm(xn, wg_ref[:, c]) for c in blocks]
        up = [_mm(xn, wu_ref[:, c]) for c in blocks]
        act = [t * _sigmoid(t) * u for t, u in zip(gt, up)]
        down = [_mm(act[k], wd[c]) for k, c in enumerate(blocks)]
        for k, c in enumerate(blocks):
            gt_ref[:, c] = gt[k].astype(BF16)
            up_ref[:, c] = up[k].astype(BF16)
        o_ref[...] += 0.5 * functools.reduce(lambda x, y: x + y, down)

    tile = pl.BlockSpec((tm, FF_TILE), lambda i, j: (i, j))
    return pl.pallas_call(
        body, name=name, grid=(T // tm, nj),
        in_specs=[pl.BlockSpec((tm, D), lambda i, j: (i, 0)),
                  pl.BlockSpec((1, D), lambda i, j: (0, 0))] + _ffn_weight_specs(),
        out_specs=[pl.BlockSpec((tm, D), lambda i, j: (i, 0)), tile, tile,
                   pl.BlockSpec((tm, D), lambda i, j: (i, 0))],
        out_shape=[jax.ShapeDtypeStruct((T, D), F32), jax.ShapeDtypeStruct((T, FFP), BF16),
                   jax.ShapeDtypeStruct((T, FFP), BF16), jax.ShapeDtypeStruct((T, D), BF16)],
        compiler_params=_cp("parallel", "arbitrary"),
    )(h, g, wa, wa, wb)


def ffn_bwd(h, dy, g, gt_saved, up_saved, wa, wb, name):
    T = h.shape[0]
    tm = min(TOK_TILE, T)
    nj = FFP // FF_TILE

    def body(h_ref, dy_ref, g_ref, gt_ref, up_ref, wg_ref, wu_ref, wd_ref,
             dh_ref, dg_ref, du_ref, a_ref, dn_ref, dxn_s, dyh_s):
        i = pl.program_id(0)
        j = pl.program_id(1)

        @pl.when(j == 0)
        def _():
            dxn_s[...] = jnp.zeros_like(dxn_s)
            dyh_s[...] = (0.5 * dy_ref[...]).astype(BF16)

        @pl.when((i == 0) & (j == 0))
        def _():
            dn_ref[...] = jnp.zeros_like(dn_ref)

        blocks = [slice(c, c + FF_SUB) for c in range(0, FF_TILE, FF_SUB)]
        wd = wd_ref[...].reshape(FF_TILE, D)
        dyh = dyh_s[...]
        gt = [gt_ref[:, c].astype(F32) for c in blocks]
        up = [up_ref[:, c].astype(F32) for c in blocks]
        da = [_mm_nt(dyh, wd[c]) for c in blocks]
        sg = [_sigmoid(t) for t in gt]
        si = [t * s for t, s in zip(gt, sg)]
        dup = [d * s for d, s in zip(da, si)]
        dgt = [d * u * (s * (1.0 + t * (1.0 - s))) for d, u, s, t in zip(da, up, sg, gt)]
        dxn = [_mm_nt(dgt[k], wg_ref[:, c]) + _mm_nt(dup[k], wu_ref[:, c]) for k, c in enumerate(blocks)]
        for k, c in enumerate(blocks):
            dg_ref[:, c] = dgt[k].astype(BF16)
            du_ref[:, c] = dup[k].astype(BF16)
            a_ref[:, c] = (si[k] * up[k]).astype(BF16)
        dxn_s[...] += functools.reduce(lambda x, y: x + y, dxn)

        @pl.when(j == nj - 1)
        def _():
            gg = g_ref[...]
            _, xhat, rstd = _rms(h_ref[...], gg)
            dh, dn = _rms_bwd(dxn_s[...], xhat, rstd, gg)
            dh_ref[...] = dy_ref[...] + dh
            dn_ref[...] += dn

    tile = pl.BlockSpec((tm, FF_TILE), lambda i, j: (i, j))
    return pl.pallas_call(
        body, name=name, grid=(T // tm, nj),
        in_specs=[pl.BlockSpec((tm, D), lambda i, j: (i, 0)),
                  pl.BlockSpec((tm, D), lambda i, j: (i, 0)),
                  pl.BlockSpec((1, D), lambda i, j: (0, 0)), tile, tile] + _ffn_weight_specs(),
        out_specs=[pl.BlockSpec((tm, D), lambda i, j: (i, 0)), tile, tile, tile,
                   pl.BlockSpec((1, D), lambda i, j: (0, 0))],
        out_shape=[jax.ShapeDtypeStruct((T, D), F32)] + [jax.ShapeDtypeStruct((T, FFP), BF16)] * 3
        + [jax.ShapeDtypeStruct((1, D), F32)],
        scratch_shapes=[pltpu.VMEM((tm, D), F32), pltpu.VMEM((tm, D), BF16)],
        compiler_params=_cp("arbitrary", "arbitrary"),
    )(h, dy, g, gt_saved, up_saved, wa, wa, wb)


def _pick(n, prefs):
    for t in prefs:
        if n % t == 0:
            return t
    return n


def _tn_body(nk, scale, out_dtype, squeeze):
    def body(a_ref, b_ref, *rest):
        o_ref, acc = rest[-2], rest[-1]
        k = pl.program_id(2)

        @pl.when(k == 0)
        def _():
            acc[...] = jnp.zeros_like(acc)

        acc[...] += _mm_tn(a_ref[...], b_ref[...])

        @pl.when(k == nk - 1)
        def _():
            res = (scale * acc[...]).astype(out_dtype)
            if squeeze:
                o_ref[0] = res
            else:
                o_ref[...] = res

    return body


def matmul_tn(a, b, name, scale=1.0, out_dtype=BF16):
    T, M = a.shape
    N = b.shape[1]
    tmm = _pick(M, (512, 256))
    tnn = _pick(N, (1024, 2432))
    tk = min(2 * TOK_TILE, T)
    nk = T // tk
    return pl.pallas_call(
        _tn_body(nk, scale, out_dtype, False), name=name, grid=(M // tmm, N // tnn, nk),
        in_specs=[pl.BlockSpec((tk, tmm), lambda i, j, k: (k, i)),
                  pl.BlockSpec((tk, tnn), lambda i, j, k: (k, j))],
        out_specs=pl.BlockSpec((tmm, tnn), lambda i, j, k: (i, j)),
        out_shape=jax.ShapeDtypeStruct((M, N), out_dtype),
        scratch_shapes=[pltpu.VMEM((tmm, tnn), F32)],
        compiler_params=_cp("parallel", "parallel", "arbitrary"),
    )(a, b)


def grad_cols(a, b, dst, slot, name):
    T = a.shape[0]
    tmm, tnn = D, FFP // 2
    tk = min(2 * TOK_TILE, T)
    nk = T // tk
    return pl.pallas_call(
        _tn_body(nk, 1.0, BF16, True), name=name, grid=(D // tmm, FFP // tnn, nk),
        in_specs=[pl.BlockSpec((tk, tmm), lambda i, j, k: (k, i)),
                  pl.BlockSpec((tk, tnn), lambda i, j, k: (k, j)),
                  pl.BlockSpec(memory_space=pl.ANY)],
        out_specs=pl.BlockSpec((1, tmm, tnn), lambda i, j, k: (slot, i, j)),
        out_shape=jax.ShapeDtypeStruct(dst.shape, dst.dtype),
        scratch_shapes=[pltpu.VMEM((tmm, tnn), F32)],
        input_output_aliases={2: 0},
        compiler_params=_cp("parallel", "parallel", "arbitrary"),
    )(a, b, dst)


def grad_rows(a, b, dst, row0, rows, name, scale=1.0):
    T = a.shape[0]
    tk = min(2 * TOK_TILE, T)
    nk = T // tk
    blk = row0 // rows

    def body(a_ref, b_ref, dst_ref, o_ref, acc):
        k = pl.program_id(0)

        @pl.when(k == 0)
        def _():
            acc[...] = jnp.zeros_like(acc)

        acc[...] += _mm_tn(a_ref[...], b_ref[...])

        @pl.when(k == nk - 1)
        def _():
            o_ref[...] = (scale * acc[...]).astype(BF16).reshape(N_DEV, rows, D)

    return pl.pallas_call(
        body, name=name, grid=(nk,),
        in_specs=[pl.BlockSpec((tk, N_DEV * rows), lambda k: (k, 0)),
                  pl.BlockSpec((tk, D), lambda k: (k, 0)),
                  pl.BlockSpec(memory_space=pl.ANY)],
        out_specs=pl.BlockSpec((N_DEV, rows, D), lambda k: (0, blk, 0)),
        out_shape=jax.ShapeDtypeStruct(dst.shape, dst.dtype),
        scratch_shapes=[pltpu.VMEM((N_DEV * rows, D), F32)],
        input_output_aliases={2: 0},
        compiler_params=_cp("arbitrary"),
    )(a, b, dst)


U_SPLITS = (512, 768, 1024, 128)
U_OFFS = (0, 512, 1280, 2304)


def mixin_fwd(h, g, w_in, name):
    T = h.shape[0]
    tm = min(TOK_TILE, T)

    def body(h_ref, g_ref, w_ref, u0, u1, u2, u3, xn_ref):
        xn = _rms(h_ref[...], g_ref[...])[0].astype(BF16)
        xn_ref[...] = xn
        u = _mm(xn, w_ref[...])
        for ref, off, n in zip((u0, u1, u2, u3), U_OFFS, U_SPLITS):
            ref[...] = u[:, off:off + n]

    return pl.pallas_call(
        body, name=name, grid=(T // tm,),
        in_specs=[_row_spec(tm, D), _full_spec((1, D)), _full_spec((D, D_IN_PAD))],
        out_specs=[_row_spec(tm, n) for n in U_SPLITS] + [_row_spec(tm, D)],
        out_shape=[jax.ShapeDtypeStruct((T, n), F32) for n in U_SPLITS]
        + [jax.ShapeDtypeStruct((T, D), BF16)],
        compiler_params=_cp("parallel"),
    )(h, g, w_in)


DU_SPLITS = (256, 256, 768, 768, 256, 128)
DU_OFFS = (0, 256, 512, 1280, 2048, 2304)


def mixin_bwd(h, dh_in, g, w_in, dus, name):
    T = h.shape[0]
    tm = min(TOK_TILE, T)

    def body(h_ref, dhi_ref, g_ref, w_ref, *refs):
        dh_ref, du_ref, dn_ref = refs[-3:]

        @pl.when(pl.program_id(0) == 0)
        def _():
            dn_ref[...] = jnp.zeros_like(dn_ref)

        for ref, off, n in zip(refs[:-3], DU_OFFS, DU_SPLITS):
            du_ref[:, off:off + n] = ref[...].astype(BF16)
        dxn = _mm_nt(du_ref[...], w_ref[...])
        gg = g_ref[...]
        _, xhat, rstd = _rms(h_ref[...], gg)
        dh, dn = _rms_bwd(dxn, xhat, rstd, gg)
        dh_ref[...] = dhi_ref[...] + dh
        dn_ref[...] += dn

    return pl.pallas_call(
        body, name=name, grid=(T // tm,),
        in_specs=[_row_spec(tm, D), _row_spec(tm, D), _full_spec((1, D)), _full_spec((D, D_IN_PAD))]
        + [_row_spec(tm, n) for n in DU_SPLITS],
        out_specs=[_row_spec(tm, D), _row_spec(tm, D_IN_PAD), _full_spec((1, D))],
        out_shape=[jax.ShapeDtypeStruct((T, D), F32), jax.ShapeDtypeStruct((T, D_IN_PAD), BF16),
                   jax.ShapeDtypeStruct((1, D), F32)],
        compiler_params=_cp("arbitrary"),
    )(h, dh_in, g, w_in, *dus)


def _shift_down(x, s, row):
    if s == 0:
        return x
    return jnp.where(row >= s, pltpu.roll(x, s, 0), 0.0)


def _shift_up(x, s, row):
    if s == 0:
        return x
    n = x.shape[0]
    return jnp.where(row < n - s, pltpu.roll(x, n - s, 0), 0.0)


def conv_fwd(x, w, b, S, col0, C, name):
    T = x.shape[0]
    cb0 = col0 // LANE

    def body(x_ref, w_ref, b_ref, y_ref):
        xx = x_ref[...]
        row = _iota(xx.shape, 0)
        y = xx * w_ref[3:4, :] + b_ref[...]
        for k in range(3):
            y += _shift_down(xx, 3 - k, row) * w_ref[k:k + 1, :]
        y_ref[...] = y

    return pl.pallas_call(
        body, name=name, grid=(T // S, C // LANE),
        in_specs=[pl.BlockSpec((S, LANE), lambda s, c: (s, cb0 + c)),
                  pl.BlockSpec((4, LANE), lambda s, c: (0, c)),
                  pl.BlockSpec((1, LANE), lambda s, c: (0, c))],
        out_specs=pl.BlockSpec((S, LANE), lambda s, c: (s, c)),
        out_shape=jax.ShapeDtypeStruct((T, C), F32),
        compiler_params=_cp("parallel", "parallel"),
    )(x, w, b)


def conv_bwd(x, dy, w, S, col0, C, name):
    T = x.shape[0]
    cb0 = col0 // LANE

    def body(x_ref, dy_ref, w_ref, dx_ref, dwb_ref):
        @pl.when(pl.program_id(1) == 0)
        def _():
            dwb_ref[...] = jnp.zeros_like(dwb_ref)

        xx = x_ref[...]
        dd = dy_ref[...]
        row = _iota(xx.shape, 0)
        dx = dd * w_ref[3:4, :]
        for k in range(3):
            dx += _shift_up(dd, 3 - k, row) * w_ref[k:k + 1, :]
        dx_ref[...] = dx
        for k in range(4):
            dwb_ref[k:k + 1, :] += jnp.sum(dd * _shift_down(xx, 3 - k, row), axis=0, keepdims=True)
        dwb_ref[4:5, :] += jnp.sum(dd, axis=0, keepdims=True)

    return pl.pallas_call(
        body, name=name, grid=(C // LANE, T // S),
        in_specs=[pl.BlockSpec((S, LANE), lambda c, s: (s, cb0 + c)),
                  pl.BlockSpec((S, LANE), lambda c, s: (s, c)),
                  pl.BlockSpec((4, LANE), lambda c, s: (0, c))],
        out_specs=[pl.BlockSpec((S, LANE), lambda c, s: (s, c)),
                   pl.BlockSpec((8, LANE), lambda c, s: (0, c))],
        out_shape=[jax.ShapeDtypeStruct((T, C), F32), jax.ShapeDtypeStruct((8, C), F32)],
        compiler_params=_cp("parallel", "arbitrary"),
    )(x, dy, w)


def _scan(a, b, row):
    n = a.shape[0]
    d = 1
    while d < n:
        keep = row >= d
        b = a * jnp.where(keep, pltpu.roll(b, d, 0), 0.0) + b
        a = a * jnp.where(keep, pltpu.roll(a, d, 0), 1.0)
        d *= 2
    return b


def _rscan(a, b, row):
    n = a.shape[0]
    d = 1
    while d < n:
        keep = row < n - d
        b = a * jnp.where(keep, pltpu.roll(b, n - d, 0), 0.0) + b
        a = a * jnp.where(keep, pltpu.roll(a, n - d, 0), 1.0)
        d *= 2
    return b


GELU_C = math.sqrt(2.0 / math.pi)


def _gelu(x):
    t = jnp.tanh(GELU_C * (x + 0.044715 * (x * x * x)))
    return 0.5 * x * (1.0 + t), t


def _lru_gates(xr, wa, ba, wx, bx, lam):
    r = _sigmoid(_mm(xr, wa) + ba)
    i = _sigmoid(_mm(xr, wx) + bx)
    sp = jnp.maximum(-lam, 0.0) + jnp.log(1.0 + jnp.exp(-jnp.abs(lam)))
    la = -LRU_C * r * sp
    a = jnp.exp(la)
    e2 = a * a
    m = jnp.sqrt(-jnp.tanh(la) * (e2 + 1.0))
    return r, i, sp, a, e2, m


def lru_fwd(xr, u_lru, wa, wx, vec, S, name):
    T = xr.shape[0]

    def body(xr_ref, gt_ref, wa_ref, wx_ref, vec_ref, y_ref):
        x = xr_ref[...]
        row = _iota(x.shape, 0)
        r, i, sp, a, e2, m = _lru_gates(x, wa_ref[...], vec_ref[0:1, :], wx_ref[...], vec_ref[1:2, :],
                                        vec_ref[2:3, :])
        hh = _scan(a, m * (i * x), row)
        y_ref[...] = _gelu(gt_ref[...])[0] * hh

    return pl.pallas_call(
        body, name=name, grid=(T // S, LRU_W // LANE),
        in_specs=[pl.BlockSpec((S, LANE), lambda s, c: (s, c)),
                  pl.BlockSpec((S, LANE), lambda s, c: (s, 2 + c)),
                  pl.BlockSpec((LANE, LANE), lambda s, c: (c, c)),
                  pl.BlockSpec((LANE, LANE), lambda s, c: (c, c)),
                  pl.BlockSpec((8, LANE), lambda s, c: (0, c))],
        out_specs=pl.BlockSpec((S, LANE), lambda s, c: (s, c)),
        out_shape=jax.ShapeDtypeStruct((T, LRU_W), F32),
        compiler_params=_cp("parallel", "parallel"),
    )(xr, u_lru, wa, wx, vec)


def lru_bwd(xr, u_lru, dy, wa, wx, vec, S, name):
    T = xr.shape[0]

    def body(xr_ref, gt_ref, dy_ref, wa_ref, wx_ref, vec_ref,
             dxr_ref, dgt_ref, dwa_ref, dwx_ref, dvec_ref):
        @pl.when(pl.program_id(1) == 0)
        def _():
            dwa_ref[...] = jnp.zeros_like(dwa_ref)
            dwx_ref[...] = jnp.zeros_like(dwx_ref)
            dvec_ref[...] = jnp.zeros_like(dvec_ref)

        x = xr_ref[...]
        n = x.shape[0]
        row = _iota(x.shape, 0)
        lam = vec_ref[2:3, :]
        r, i, sp, a, e2, m = _lru_gates(x, wa_ref[...], vec_ref[0:1, :], wx_ref[...], vec_ref[1:2, :], lam)
        v = i * x
        hh = _scan(a, m * v, row)
        gt = gt_ref[...]
        dy = dy_ref[...]
        ge, t = _gelu(gt)
        dgt_ref[...] = dy * hh * (0.5 * (1.0 + t) + 0.5 * gt * (1.0 - t * t) * GELU_C
                                  * (1.0 + 3.0 * 0.044715 * gt * gt))
        a_next = jnp.where(row < n - 1, pltpu.roll(a, n - 1, 0), 0.0)
        G = _rscan(a_next, dy * ge, row)
        da = G * _shift_down(hh, 1, row)
        dv = G * m
        dla = da * a - (G * v) * e2 / m
        dr = dla * (-LRU_C * sp)
        dsp = jnp.sum(dla * (-LRU_C * r), axis=0, keepdims=True)
        dra = dr * r * (1.0 - r)
        dia = (dv * x) * i * (1.0 - i)
        dxr_ref[...] = dv * i + _mm_nt(dra, wa_ref[...]) + _mm_nt(dia, wx_ref[...])
        dwa_ref[0] += _mm_tn(x, dra)
        dwx_ref[0] += _mm_tn(x, dia)
        dvec_ref[0:1, :] += jnp.sum(dra, axis=0, keepdims=True)
        dvec_ref[1:2, :] += jnp.sum(dia, axis=0, keepdims=True)
        dvec_ref[2:3, :] += dsp * (-_sigmoid(-lam))

    return pl.pallas_call(
        body, name=name, grid=(LRU_W // LANE, T // S),
        in_specs=[pl.BlockSpec((S, LANE), lambda c, s: (s, c)),
                  pl.BlockSpec((S, LANE), lambda c, s: (s, 2 + c)),
                  pl.BlockSpec((S, LANE), lambda c, s: (s, c)),
                  pl.BlockSpec((LANE, LANE), lambda c, s: (c, c)),
                  pl.BlockSpec((LANE, LANE), lambda c, s: (c, c)),
                  pl.BlockSpec((8, LANE), lambda c, s: (0, c))],
        out_specs=[pl.BlockSpec((S, LANE), lambda c, s: (s, c)),
                   pl.BlockSpec((S, LANE), lambda c, s: (s, c)),
                   pl.BlockSpec((1, LANE, LANE), lambda c, s: (c, 0, 0)),
                   pl.BlockSpec((1, LANE, LANE), lambda c, s: (c, 0, 0)),
                   pl.BlockSpec((8, LANE), lambda c, s: (0, c))],
        out_shape=[jax.ShapeDtypeStruct((T, LRU_W), F32), jax.ShapeDtypeStruct((T, LRU_W), F32),
                   jax.ShapeDtypeStruct((2, LANE, LANE), F32), jax.ShapeDtypeStruct((2, LANE, LANE), F32),
                   jax.ShapeDtypeStruct((8, LRU_W), F32)],
        compiler_params=_cp("parallel", "arbitrary"),
    )(xr, u_lru, dy, wa, wx, vec)


def _bucket_table():
    qi = np.arange(BQ)[:, None]
    kj = np.arange(2 * BQ)[None, :]
    dist = BQ + qi - kj
    band = (dist >= 0) & (dist < WINDOW)
    dd = np.maximum(dist, 0)
    max_exact = REL_BUCKETS // 2
    large = max_exact + (np.log(np.maximum(dd, 1).astype(np.float32) / np.float32(max_exact))
                         / np.float32(math.log(REL_MAX_DIST / max_exact))
                         * np.float32(REL_BUCKETS - max_exact)).astype(np.int32)
    large = np.minimum(large, REL_BUCKETS - 1)
    bucket = np.where(dd < max_exact, dd, large)
    return np.where(band, bucket, -1).astype(np.int32)


def _att_specs(S):
    nb = S // BQ
    qc = ATT_W // LANE
    return [pl.BlockSpec((BQ, ATT_W), lambda b, n: (b * nb + n, 0)),
            pl.BlockSpec((BQ, KV_W), lambda b, n: (b * nb + jnp.maximum(n - 1, 0), qc)),
            pl.BlockSpec((BQ, KV_W), lambda b, n: (b * nb + n, qc)),
            pl.BlockSpec((BQ, KV_W), lambda b, n: (b * nb + jnp.maximum(n - 1, 0), qc + 1)),
            pl.BlockSpec((BQ, KV_W), lambda b, n: (b * nb + n, qc + 1))]


def _att_bias(bk, rb_ref, bias_s):
    for h in range(ATT_H):
        acc = jnp.zeros(bk.shape, F32)
        for bb in range(REL_BUCKETS):
            acc = jnp.where(bk == bb, rb_ref[bb * ATT_H + h], acc)
        bias_s[h] = acc


def _att_probs(qs, kgs, bias_s, valid, sk_ref):
    heads = range(ATT_H)
    s = [_mm_nt(qs[h], kgs[h // ATT_G]) for h in heads]
    s = [jnp.where(valid, s[h] * (HD ** -0.5) + bias_s[h], NEG) for h in heads]
    m = [jnp.maximum(jnp.max(s[h], axis=-1, keepdims=True), sk_ref[h]) for h in heads]
    e = [jnp.exp(s[h] - m[h]) for h in heads]
    es = [jnp.exp(sk_ref[h] - m[h]) for h in heads]
    den = [jnp.sum(e[h], axis=-1, keepdims=True) + es[h] for h in heads]
    return [e[h] / den[h] for h in heads], [es[h] / den[h] for h in heads]


def _att_kv(kp_ref, kc_ref, vp_ref, vc_ref):
    cat = lambda a, b, g: jnp.concatenate([a[:, HD * g:HD * (g + 1)], b[:, HD * g:HD * (g + 1)]], axis=0)
    return ([cat(kp_ref, kc_ref, g) for g in range(ATT_KV)], [cat(vp_ref, vc_ref, g) for g in range(ATT_KV)])


def attn_fwd(u_att, sinks, rel_bias, S, name):
    T = u_att.shape[0]
    nb = S // BQ
    table = jnp.asarray(_bucket_table())

    def body(sk_ref, rb_ref, bk_ref, q_ref, kp_ref, kc_ref, vp_ref, vc_ref, o_ref, bias_s):
        b = pl.program_id(0)
        n = pl.program_id(1)
        bk = bk_ref[...]

        @pl.when((b == 0) & (n == 0))
        def _():
            _att_bias(bk, rb_ref, bias_s)

        valid = (bk >= 0) & ((n > 0) | (_iota(bk.shape, 1) >= BQ))
        kgs, vgs = _att_kv(kp_ref, kc_ref, vp_ref, vc_ref)
        p, _ = _att_probs([q_ref[:, HD * h:HD * (h + 1)] for h in range(ATT_H)], kgs, bias_s, valid, sk_ref)
        outs = [_mm(p[h], vgs[h // ATT_G]) for h in range(ATT_H)]
        for h in range(ATT_H):
            o_ref[:, HD * h:HD * (h + 1)] = outs[h]

    smem = pl.BlockSpec(memory_space=pltpu.SMEM)
    return pl.pallas_call(
        body, name=name, grid=(T // S, nb),
        in_specs=[smem, smem, _full_spec((BQ, 2 * BQ))] + _att_specs(S),
        out_specs=pl.BlockSpec((BQ, ATT_W), lambda b, n: (b * nb + n, 0)),
        out_shape=jax.ShapeDtypeStruct((T, ATT_W), F32),
        scratch_shapes=[pltpu.VMEM((ATT_H, BQ, 2 * BQ), F32)],
        compiler_params=_cp("arbitrary", "arbitrary"),
    )(sinks, rel_bias, table, u_att, u_att, u_att, u_att, u_att)


def attn_bwd(u_att, dy, sinks, rel_bias, S, name):
    T = u_att.shape[0]
    nb = S // BQ
    nB = T // S
    table = jnp.asarray(_bucket_table())
    scale = HD ** -0.5

    def body(sk_ref, rb_ref, bk_ref, q_ref, kp_ref, kc_ref, vp_ref, vc_ref, dy_ref,
             du_ref, drel_ref, dsk_ref, bias_s, dbias_s):
        b = pl.program_id(0)
        n = pl.program_id(1)
        bk = bk_ref[...]

        @pl.when((b == 0) & (n == 0))
        def _():
            _att_bias(bk, rb_ref, bias_s)
            dbias_s[...] = jnp.zeros_like(dbias_s)
            dsk_ref[...] = jnp.zeros_like(dsk_ref)
            drel_ref[...] = jnp.zeros_like(drel_ref)

        @pl.when(n == 0)
        def _():
            du_ref[...] = jnp.zeros_like(du_ref)

        valid = (bk >= 0) & ((n > 0) | (_iota(bk.shape, 1) >= BQ))
        r_cur = pl.multiple_of(n * BQ, BQ)
        r_prev = pl.multiple_of(jnp.maximum(n - 1, 0) * BQ, BQ)
        heads = range(ATT_H)
        kgs, vgs = _att_kv(kp_ref, kc_ref, vp_ref, vc_ref)
        qs = [q_ref[:, HD * h:HD * (h + 1)] for h in heads]
        dos = [dy_ref[:, HD * h:HD * (h + 1)] for h in heads]
        p, ps = _att_probs(qs, kgs, bias_s, valid, sk_ref)
        dp = [_mm_nt(dos[h], vgs[h // ATT_G]) for h in heads]
        delta = [jnp.sum(p[h] * dp[h], axis=-1, keepdims=True) for h in heads]
        ds = [p[h] * (dp[h] - delta[h]) for h in heads]
        dss = [ds[h] * scale for h in heads]
        dq = [_mm(dss[h], kgs[h // ATT_G]) for h in heads]
        dks = [_mm_tn(dss[h], qs[h]) for h in heads]
        dvs = [_mm_tn(p[h], dos[h]) for h in heads]
        for h in heads:
            dbias_s[h] += ds[h]
            dsk_ref[h:h + 1, :] += jnp.broadcast_to(jnp.sum(-ps[h] * delta[h], axis=0, keepdims=True), (1, LANE))
            du_ref[pl.ds(r_cur, BQ), HD * h:HD * (h + 1)] = dq[h]
        for g in range(ATT_KV):
            of_group = range(g * ATT_G, (g + 1) * ATT_G)
            dk = functools.reduce(lambda x, y: x + y, [dks[h] for h in of_group])
            dv = functools.reduce(lambda x, y: x + y, [dvs[h] for h in of_group])
            ck = ATT_W + HD * g
            cv = ATT_W + KV_W + HD * g
            du_ref[pl.ds(r_prev, BQ), ck:ck + HD] += dk[0:BQ]
            du_ref[pl.ds(r_cur, BQ), ck:ck + HD] += dk[BQ:]
            du_ref[pl.ds(r_prev, BQ), cv:cv + HD] += dv[0:BQ]
            du_ref[pl.ds(r_cur, BQ), cv:cv + HD] += dv[BQ:]

        @pl.when((b == nB - 1) & (n == nb - 1))
        def _():
            lane = _iota((1, LANE), 1)
            for h in range(ATT_H):
                db = dbias_s[h]
                acc = jnp.zeros((1, LANE), F32)
                for bb in range(REL_BUCKETS):
                    val = jnp.sum(jnp.sum(jnp.where(bk == bb, db, 0.0), axis=1, keepdims=True),
                                  axis=0, keepdims=True)
                    acc = jnp.where(lane == bb, val, acc)
                drel_ref[h:h + 1, :] = acc

    smem = pl.BlockSpec(memory_space=pltpu.SMEM)
    return pl.pallas_call(
        body, name=name, grid=(nB, nb),
        in_specs=[smem, smem, _full_spec((BQ, 2 * BQ))] + _att_specs(S)
        + [pl.BlockSpec((BQ, ATT_W), lambda b, n: (b * nb + n, 0))],
        out_specs=[pl.BlockSpec((S, ATT_W + 2 * KV_W), lambda b, n: (b, 0)),
                   _full_spec((8, LANE)), _full_spec((8, LANE))],
        out_shape=[jax.ShapeDtypeStruct((T, ATT_W + 2 * KV_W), F32),
                   jax.ShapeDtypeStruct((8, LANE), F32), jax.ShapeDtypeStruct((8, LANE), F32)],
        scratch_shapes=[pltpu.VMEM((ATT_H, BQ, 2 * BQ), F32), pltpu.VMEM((ATT_H, BQ, 2 * BQ), F32)],
        compiler_params=_cp("arbitrary", "arbitrary"),
    )(sinks, rel_bias, table, u_att, u_att, u_att, u_att, u_att, dy)


def _head_of(i):
    return lax.shift_right_logical(i, 6)


def _head_mask(shape):
    return (_head_of(_iota(shape, 0)) == _head_of(_iota(shape, 1))).astype(F32)


def _dn_point(c, uba, alog, dtb):
    s = c * _sigmoid(c)
    qt, kt, vt = s[:, 0:256], s[:, 256:512], s[:, 512:768]
    ones_bd = _head_mask((DN_W, DN_W))
    q = qt * lax.rsqrt(_mm3(qt * qt, ones_bd) + EPS) * (HD ** -0.5)
    k = kt * lax.rsqrt(_mm3(kt * kt, ones_bd) + EPS)
    sel = _head_of(_iota((LANE, DN_W), 1))
    row = _iota((LANE, DN_W), 0)
    braw = _mm3(uba, (row == sel).astype(F32))
    araw = _mm3(uba, (row == sel + DN_H).astype(F32)) + dtb
    beta = _sigmoid(braw)
    g = -jnp.exp(alog) * (jnp.maximum(araw, 0.0) + jnp.log(1.0 + jnp.exp(-jnp.abs(araw))))
    return q, k, vt, g, beta


def dn_point_fwd(c, uba, alog, dtb, name):
    T = c.shape[0]
    tm = min(TOK_TILE, T)

    def body(c_ref, u_ref, al_ref, dt_ref, *outs):
        for ref, val in zip(outs, _dn_point(c_ref[...], u_ref[...], al_ref[...], dt_ref[...])):
            ref[...] = val

    return pl.pallas_call(
        body, name=name, grid=(T // tm,),
        in_specs=[_row_spec(tm, 768), _row_spec(tm, LANE), _full_spec((1, DN_W)), _full_spec((1, DN_W))],
        out_specs=[_row_spec(tm, DN_W)] * 5,
        out_shape=[jax.ShapeDtypeStruct((T, DN_W), F32)] * 5,
        compiler_params=_cp("parallel"),
    )(c, uba, alog, dtb)


def dn_point_bwd(c, uba, alog, dtb, douts, name):
    T = c.shape[0]
    tm = min(TOK_TILE, T)

    def body(c_ref, u_ref, al_ref, dt_ref, dq, dk, dv, dg, db, dc_ref, du_ref, dvec_ref):
        @pl.when(pl.program_id(0) == 0)
        def _():
            dvec_ref[...] = jnp.zeros_like(dvec_ref)

        _, vjp = jax.vjp(_dn_point, c_ref[...], u_ref[...], al_ref[...], dt_ref[...])
        dc, du, dal, ddt = vjp((dq[...], dk[...], dv[...], dg[...], db[...]))
        dc_ref[...] = dc
        du_ref[...] = du
        fold = (_iota((LANE, DN_W), 0) == _head_of(_iota((LANE, DN_W), 1))).astype(F32)
        both = jnp.concatenate([dal, ddt, jnp.zeros((6, DN_W), F32)], axis=0)
        dvec_ref[...] += _mmx_nt(both, fold)

    return pl.pallas_call(
        body, name=name, grid=(T // tm,),
        in_specs=[_row_spec(tm, 768), _row_spec(tm, LANE), _full_spec((1, DN_W)), _full_spec((1, DN_W))]
        + [_row_spec(tm, DN_W)] * 5,
        out_specs=[_row_spec(tm, 768), _row_spec(tm, LANE), _full_spec((8, LANE))],
        out_shape=[jax.ShapeDtypeStruct((T, 768), F32), jax.ShapeDtypeStruct((T, LANE), F32),
                   jax.ShapeDtypeStruct((8, LANE), F32)],
        compiler_params=_cp("arbitrary"),
    )(c, uba, alog, dtb, *douts)


def _unit_lower_inverses(lmats):
    eye = (_iota(lmats[0].shape, 0) == _iota(lmats[0].shape, 1)).astype(F32)
    tinvs = [eye - lm for lm in lmats]
    pws = list(lmats)
    for _ in range(5):
        pws = [_mm3(pw, pw) for pw in pws]
        tinvs = [t + _mm3(t, pw) for t, pw in zip(tinvs, pws)]
    return tuple(tinvs)


def _inverse_bwd(tinv, d):
    return -_mm3_nt(_mm3_tn(tinv, d), tinv)


@jax.custom_vjp
def _tri_invs(lmats):
    return _unit_lower_inverses(lmats)


def _tri_invs_fwd(lmats):
    tinvs = _unit_lower_inverses(lmats)
    return tinvs, tinvs


_tri_invs.defvjp(_tri_invs_fwd, lambda tinvs, ds: (tuple(_inverse_bwd(t, d) for t, d in zip(tinvs, ds)),))


@jax.custom_vjp
def _tri_inv_known(lmat, tinv):
    return tinv


_tri_inv_known.defvjp(lambda lmat, tinv: (tinv, tinv),
                      lambda tinv, d: (_inverse_bwd(tinv, d), jnp.zeros_like(tinv)))


DN_SUB = 4


def _dn_stack(x):
    return jnp.concatenate([x, x, x, x], axis=0) * _head_mask((DN_W, DN_W))


def _dn_pre_inverse(q, k, v, g, beta):
    hm = _head_mask((DN_W, DN_W))
    ri = _iota((DN_W, DN_W), 0) & (CHUNK - 1)
    ci = _iota((DN_W, DN_W), 1) & (CHUNK - 1)
    tri64 = (_iota((CHUNK, CHUNK), 0) >= _iota((CHUNK, CHUNK), 1)).astype(F32)
    gc = _mm3(tri64, g)
    ks = _dn_stack(k)
    gcol = jnp.sum(_dn_stack(gc), axis=1, keepdims=True) * (1.0 / HD)
    gmat = jnp.broadcast_to(gcol, (DN_W, DN_W))
    decay = jnp.exp(jnp.minimum(gmat - gmat.T, 0.0))
    lmat = _mm_nt(_dn_stack(k * beta), ks) * decay * (hm * (ri > ci).astype(F32))
    att = _mm_nt(_dn_stack(q), ks) * decay * (hm * (ri >= ci).astype(F32))
    return lmat, att, gc


def _dn_post_inverse(q, k, v, g, beta, tinv, att, gc):
    glast = jnp.sum(g, axis=0, keepdims=True)
    eg = jnp.exp(gc)
    u = _mm(tinv, _dn_stack(v * beta))
    w = _mm(tinv, _dn_stack(k * beta * eg))
    return u, w, att, _dn_stack(q * eg), _dn_stack(k * jnp.exp(glast - gc)), jnp.exp(glast), tinv


def _dn_apply(state, prep):
    u, w, att, qe, kd, eglast, _ = prep
    vn = u - _mm(w, state)
    o4 = _mm(qe, state) + _mm(att, vn)
    o = o4[0:64] + o4[64:128] + o4[128:192] + o4[192:256]
    return o, state * eglast + _mm_tn(kd, vn)


def _dn_chunks(state, q, k, v, g, beta, knowns=None):
    n = q.shape[0] // CHUNK
    chunks = [tuple(x[c * CHUNK:(c + 1) * CHUNK] for x in (q, k, v, g, beta)) for c in range(n)]
    pre = [_dn_pre_inverse(*ch) for ch in chunks]
    if knowns is None:
        tinvs = _tri_invs(tuple(p[0] for p in pre))
    else:
        tinvs = [_tri_inv_known(p[0], known) for p, known in zip(pre, knowns)]
    preps = [_dn_post_inverse(*ch, tinv, p[1], p[2]) for ch, tinv, p in zip(chunks, tinvs, pre)]
    outs = []
    for prep in preps:
        o, state = _dn_apply(state, prep)
        outs.append(o)
    return jnp.concatenate(outs, axis=0), state, [prep[-1] for prep in preps]


def dn_scan_fwd(q, k, v, g, beta, S, name):
    T = q.shape[0]
    rows = DN_SUB * CHUNK
    ns = S // rows

    def body(q_ref, k_ref, v_ref, g_ref, b_ref, o_ref, st_ref, ti_ref, s_s):
        @pl.when(pl.program_id(1) == 0)
        def _():
            s_s[...] = jnp.zeros_like(s_s)

        st = s_s[...]
        st_ref[0] = st
        o, new, tinvs = _dn_chunks(st, q_ref[...], k_ref[...], v_ref[...], g_ref[...], b_ref[...])
        o_ref[...] = o
        for c, tinv in enumerate(tinvs):
            ti_ref[c] = tinv
        s_s[...] = new

    spec = pl.BlockSpec((rows, DN_W), lambda b, t: (b * ns + t, 0))
    return pl.pallas_call(
        body, name=name, grid=(T // S, ns),
        in_specs=[spec] * 5,
        out_specs=[spec, pl.BlockSpec((1, DN_W, DN_W), lambda b, t: (b * ns + t, 0, 0)),
                   pl.BlockSpec((DN_SUB, DN_W, DN_W), lambda b, t: (b * ns + t, 0, 0))],
        out_shape=[jax.ShapeDtypeStruct((T, DN_W), F32),
                   jax.ShapeDtypeStruct((T // rows, DN_W, DN_W), F32),
                   jax.ShapeDtypeStruct((T // CHUNK, DN_W, DN_W), F32)],
        scratch_shapes=[pltpu.VMEM((DN_W, DN_W), F32)],
        compiler_params=_cp("parallel", "arbitrary"),
    )(q, k, v, g, beta)


def dn_scan_bwd(q, k, v, g, beta, states, tinvs, do, S, name):
    T = q.shape[0]
    rows = DN_SUB * CHUNK
    ns = S // rows

    def body(q_ref, k_ref, v_ref, g_ref, b_ref, st_ref, ti_ref, do_ref, dq, dk, dv, dg, db, ds_s):
        @pl.when(pl.program_id(1) == 0)
        def _():
            ds_s[...] = jnp.zeros_like(ds_s)

        knowns = [ti_ref[c] for c in range(DN_SUB)]
        _, vjp = jax.vjp(lambda *args: _dn_chunks(*args, knowns=knowns)[:2],
                         st_ref[0], q_ref[...], k_ref[...], v_ref[...], g_ref[...], b_ref[...])
        grads = vjp((do_ref[...], ds_s[...]))
        ds_s[...] = grads[0]
        for ref, val in zip((dq, dk, dv, dg, db), grads[1:]):
            ref[...] = val

    spec = pl.BlockSpec((rows, DN_W), lambda b, t: (b * ns + ns - 1 - t, 0))
    return pl.pallas_call(
        body, name=name, grid=(T // S, ns),
        in_specs=[spec] * 5 + [pl.BlockSpec((1, DN_W, DN_W), lambda b, t: (b * ns + ns - 1 - t, 0, 0)),
                               pl.BlockSpec((DN_SUB, DN_W, DN_W), lambda b, t: (b * ns + ns - 1 - t, 0, 0)),
                               spec],
        out_specs=[spec] * 5,
        out_shape=[jax.ShapeDtypeStruct((T, DN_W), F32)] * 5,
        scratch_shapes=[pltpu.VMEM((DN_W, DN_W), F32)],
        compiler_params=_cp("parallel", "arbitrary"),
    )(q, k, v, g, beta, states, tinvs, do)


def _dn_gate(o, z, nl):
    ms = _mm3(o * o, _head_mask((DN_W, DN_W))) * (1.0 / HD)
    return o * lax.rsqrt(ms + EPS) * nl * (z * _sigmoid(z))


def dn_gate_fwd(o, u_dn, nl, name):
    T = o.shape[0]
    tm = min(TOK_TILE, T)

    def body(o_ref, z_ref, n_ref, y_ref):
        y_ref[...] = _dn_gate(o_ref[...], z_ref[...], n_ref[...])

    return pl.pallas_call(
        body, name=name, grid=(T // tm,),
        in_specs=[_row_spec(tm, DN_W), pl.BlockSpec((tm, DN_W), lambda i: (i, 3)), _full_spec((1, DN_W))],
        out_specs=_row_spec(tm, DN_W),
        out_shape=jax.ShapeDtypeStruct((T, DN_W), F32),
        compiler_params=_cp("parallel"),
    )(o, u_dn, nl)


def dn_gate_bwd(o, u_dn, nl, dy, name):
    T = o.shape[0]
    tm = min(TOK_TILE, T)

    def body(o_ref, z_ref, n_ref, dy_ref, do_ref, dz_ref, dn_ref):
        @pl.when(pl.program_id(0) == 0)
        def _():
            dn_ref[...] = jnp.zeros_like(dn_ref)

        _, vjp = jax.vjp(_dn_gate, o_ref[...], z_ref[...], n_ref[...])
        do, dz, dn = vjp(dy_ref[...])
        do_ref[...] = do
        dz_ref[...] = dz
        fold = (_iota((LANE, DN_W), 0) == (_iota((LANE, DN_W), 1) & (HD - 1))).astype(F32)
        dn_ref[...] += _mmx_nt(jnp.concatenate([dn, jnp.zeros((7, DN_W), F32)], axis=0), fold)

    return pl.pallas_call(
        body, name=name, grid=(T // tm,),
        in_specs=[_row_spec(tm, DN_W), pl.BlockSpec((tm, DN_W), lambda i: (i, 3)), _full_spec((1, DN_W)),
                  _row_spec(tm, DN_W)],
        out_specs=[_row_spec(tm, DN_W), _row_spec(tm, DN_W), _full_spec((8, LANE))],
        out_shape=[jax.ShapeDtypeStruct((T, DN_W), F32), jax.ShapeDtypeStruct((T, DN_W), F32),
                   jax.ShapeDtypeStruct((8, LANE), F32)],
        compiler_params=_cp("arbitrary"),
    )(o, u_dn, nl, dy)


Y_SPLITS = (LRU_W, ATT_W, DN_W)
Y_OFFS = (0, LRU_W, LRU_W + ATT_W)


ROWS_DEV = D // N_DEV


def _dev_rows_spec(row0):
    return pl.BlockSpec((N_DEV, ROWS_DEV, D), lambda *_: (0, row0 // ROWS_DEV, 0))


def _dev_rows(w_ref, off, n):
    return w_ref[off // ROWS_DEV:(off + n) // ROWS_DEV].reshape(n, D)


def wout_fwd(h, ys, wb, name):
    T = h.shape[0]
    tm = min(TOK_TILE, T)

    def body(h_ref, y0, y1, y2, w_ref, o_ref, yc_ref):
        for ref, off, n in zip((y0, y1, y2), Y_OFFS, Y_SPLITS):
            yc_ref[:, off:off + n] = ref[...].astype(BF16)
        o_ref[...] = h_ref[...] + _mm(yc_ref[...], _dev_rows(w_ref, 0, D))

    return pl.pallas_call(
        body, name=name, grid=(T // tm,),
        in_specs=[_row_spec(tm, D)] + [_row_spec(tm, n) for n in Y_SPLITS] + [_dev_rows_spec(R_WOUT)],
        out_specs=[_row_spec(tm, D), _row_spec(tm, D)],
        out_shape=[jax.ShapeDtypeStruct((T, D), F32), jax.ShapeDtypeStruct((T, D), BF16)],
        compiler_params=_cp("parallel"),
    )(h, *ys, wb)


def wout_bwd(dy, wb, name):
    T = dy.shape[0]
    tm = min(TOK_TILE, T)

    def body(dy_ref, w_ref, d0, d1, d2):
        dys = _mm_nt(dy_ref[...], _dev_rows(w_ref, 0, D))
        for ref, off, n in zip((d0, d1, d2), Y_OFFS, Y_SPLITS):
            ref[...] = dys[:, off:off + n]

    return pl.pallas_call(
        body, name=name, grid=(T // tm,),
        in_specs=[_row_spec(tm, D), _dev_rows_spec(R_WOUT)],
        out_specs=[_row_spec(tm, n) for n in Y_SPLITS],
        out_shape=[jax.ShapeDtypeStruct((T, n), F32) for n in Y_SPLITS],
        compiler_params=_cp("parallel"),
    )(dy, wb)


def ple_fwd(h, g, pe, wg, wp, name):
    T = h.shape[0]
    tm = min(TOK_TILE, T)

    def body(h_ref, g_ref, p_ref, wg_ref, wp_ref, o_ref):
        hh = h_ref[...]
        xn = _rms(hh, g_ref[...])[0]
        o_ref[...] = hh + _sigmoid(_mm(xn, _dev_rows(wg_ref, 0, D))) * _mm(p_ref[...], wp_ref[...])

    return pl.pallas_call(
        body, name=name, grid=(T // tm,),
        in_specs=[_row_spec(tm, D), _full_spec((1, D)), _row_spec(tm, PLE), _dev_rows_spec(R_PGATE),
                  _full_spec((PLE, D))],
        out_specs=_row_spec(tm, D),
        out_shape=jax.ShapeDtypeStruct((T, D), F32),
        compiler_params=_cp("parallel"),
    )(h, g, pe, wg, wp)


def ple_bwd(h, dy, g, pe, wg, wp, name):
    T = h.shape[0]
    tm = min(TOK_TILE, T)

    def body(h_ref, dy_ref, g_ref, p_ref, wg_ref, wp_ref, dh_ref, dz_ref, dpp_ref, xn_ref, dn_ref):
        @pl.when(pl.program_id(0) == 0)
        def _():
            dn_ref[...] = jnp.zeros_like(dn_ref)

        gg = g_ref[...]
        dy = dy_ref[...]
        xn, xhat, rstd = _rms(h_ref[...], gg)
        wg = _dev_rows(wg_ref, 0, D)
        gate = _sigmoid(_mm(xn, wg))
        pp = _mm(p_ref[...], wp_ref[...])
        dz = dy * pp * gate * (1.0 - gate)
        dz_ref[...] = dz.astype(BF16)
        dpp_ref[...] = (dy * gate).astype(BF16)
        xn_ref[...] = xn.astype(BF16)
        dh, dn = _rms_bwd(_mm_nt(dz, wg), xhat, rstd, gg)
        dh_ref[...] = dy + dh
        dn_ref[...] += dn

    return pl.pallas_call(
        body, name=name, grid=(T // tm,),
        in_specs=[_row_spec(tm, D), _row_spec(tm, D), _full_spec((1, D)), _row_spec(tm, PLE),
                  _dev_rows_spec(R_PGATE), _full_spec((PLE, D))],
        out_specs=[_row_spec(tm, D), _row_spec(tm, D), _row_spec(tm, D), _row_spec(tm, D), _full_spec((1, D))],
        out_shape=[jax.ShapeDtypeStruct((T, D), F32), jax.ShapeDtypeStruct((T, D), BF16),
                   jax.ShapeDtypeStruct((T, D), BF16), jax.ShapeDtypeStruct((T, D), BF16),
                   jax.ShapeDtypeStruct((1, D), F32)],
        compiler_params=_cp("arbitrary"),
    )(h, dy, g, pe, wg, wp)


def loss_head(h, g, target, name):
    T = h.shape[0]
    tm = min(TOK_TILE, T)

    def body(h_ref, g_ref, t_ref, loss_ref, dh_ref, dn_ref):
        @pl.when(pl.program_id(0) == 0)
        def _():
            dn_ref[...] = jnp.zeros_like(dn_ref)
            loss_ref[...] = jnp.zeros_like(loss_ref)

        gg = g_ref[...]
        y, xhat, rstd = _rms(h_ref[...], gg)
        err = y - t_ref[...]
        per_tok = jnp.mean(err * err, axis=-1, keepdims=True)
        loss_ref[...] += 0.5 * jnp.sum(per_tok, axis=0, keepdims=True)
        dh, dn = _rms_bwd(err * (1.0 / D), xhat, rstd, gg)
        dh_ref[...] = dh
        dn_ref[...] += dn

    return pl.pallas_call(
        body, name=name, grid=(T // tm,),
        in_specs=[_row_spec(tm, D), _full_spec((1, D)), _row_spec(tm, D)],
        out_specs=[_full_spec((8, LANE)), _row_spec(tm, D), _full_spec((1, D))],
        out_shape=[jax.ShapeDtypeStruct((8, LANE), F32), jax.ShapeDtypeStruct((T, D), F32),
                   jax.ShapeDtypeStruct((1, D), F32)],
        compiler_params=_cp("arbitrary"),
    )(h, g, target)


def _block_diag(w):
    return jnp.einsum('hij,hk->hikj', w, jnp.eye(4, dtype=w.dtype)).reshape(LRU_W, LRU_W)


def _layer_consts(W, l):
    row = lambda v: v.reshape(1, -1)
    zeros = jnp.zeros((5, LRU_W), F32)
    return dict(
        wa=_block_diag(W["lru_w_a"][l]), wx=_block_diag(W["lru_w_x"][l]),
        lru_vec=jnp.concatenate([row(W["lru_b_a"][l]), row(W["lru_b_x"][l]), row(W["lru_lambda"][l]), zeros], 0),
        lru_cb=row(W["lru_conv_b"][l]),
        sinks=W["attn_sinks"][l], rel=W["rel_bias"].reshape(-1),
        dn_cb=jnp.zeros((1, 3 * DN_W), F32),
        alog=row(jnp.repeat(W["dn_a_log"][l], HD)), dtb=row(jnp.repeat(W["dn_dt_bias"][l], HD)),
        dn_nl=row(jnp.tile(W["dn_norm"][l], DN_H)),
    )


def _layer_fwd(h0, pe, W, l, S, need=None):
    n = f"l{l}_"
    c_ = _layer_consts(W, l)
    row = lambda v: v.reshape(1, -1)
    need = need or (lambda *_: None)
    need(l, "f1", h0)
    h1, *ffn1_kept = ffn_fwd(h0, row(W["ffn1_norm"][l]), W["f1_cols"][l], W["f1_rows"][l], n + "ffn1_fwd")
    need(l, "in", h1)
    u_lru, u_att, u_dn, u_ba, xn_mix = mixin_fwd(h1, row(W["mix_norm"][l]), W["w_in"][l], n + "mixin_fwd")
    xr = conv_fwd(u_lru, W["lru_conv_w"][l], c_["lru_cb"], S, 0, LRU_W, n + "lru_conv_fwd")
    y_lru = lru_fwd(xr, u_lru, c_["wa"], c_["wx"], c_["lru_vec"], S, n + "lru_fwd")
    y_att = attn_fwd(u_att, c_["sinks"], c_["rel"], S, n + "attn_fwd")
    cc = conv_fwd(u_dn, W["dn_conv_w"][l], c_["dn_cb"], S, 0, 3 * DN_W, n + "dn_conv_fwd")
    q, k, v, g, beta = dn_point_fwd(cc, u_ba, c_["alog"], c_["dtb"], n + "dn_point_fwd")
    o, states, tinvs = dn_scan_fwd(q, k, v, g, beta, S, n + "dn_scan_fwd")
    y_dn = dn_gate_fwd(o, u_dn, c_["dn_nl"], n + "dn_gate_fwd")
    need(l, "rest", y_dn)
    h2, ycat = wout_fwd(h1, (y_lru, y_att, y_dn), W["r_rows"][l], n + "wout_fwd")
    h3, *ffn2_kept = ffn_fwd(h2, row(W["ffn2_norm"][l]), W["r_cols"][l], W["r_rows"][l], n + "ffn2_fwd")
    h4 = ple_fwd(h3, row(W["ple_norm"][l]), pe, W["r_rows"][l], W["ple_w_proj"][l], n + "ple_fwd")
    saved = dict(ffn1=ffn1_kept, ffn2=ffn2_kept, h0=h0, h1=h1, h2=h2, h3=h3, u_lru=u_lru, u_att=u_att, u_dn=u_dn,
                 u_ba=u_ba, xn_mix=xn_mix, xr=xr, cc=cc, q=q, k=k, v=v, g=g, beta=beta, o=o, states=states, tinvs=tinvs, ycat=ycat)
    return h4, saved


GM_WOUT, GM_PGATE, GM_WIN, GM_PPROJ, GM_END, GM_ROWS = 0, 128, 256, 560, 592, 640


def _layer_bwd(dh4, sv, pe, W, l, S, token=None, on_piece=None):
    n = f"l{l}_"
    c_ = _layer_consts(W, l)
    row = lambda v: v.reshape(1, -1)
    behind = lambda v, tok: v if tok is None else v + tok.astype(v.dtype)
    on_piece = on_piece or (lambda *_: None)
    G = {"mix_rows": jnp.zeros((N_DEV, GM_ROWS, D), BF16)}
    dh3, dz, dpp, xn_p, dn = ple_bwd(sv["h3"], dh4, behind(row(W["ple_norm"][l]), token), pe, W["r_rows"][l],
                                     W["ple_w_proj"][l], n + "ple_bwd")
    G["ple_norm"] = dn[0]
    G["mix_rows"] = grad_rows(xn_p, dz, G["mix_rows"], GM_PGATE, ROWS_DEV, n + "d_ple_w_gate")
    d_proj = matmul_tn(pe, dpp, n + "d_ple_w_proj")
    d_proj = d_proj.reshape(PLE, N_DEV, D // N_DEV).transpose(1, 0, 2).reshape(N_DEV, GM_END - GM_PPROJ, D)
    G["mix_rows"] = lax.dynamic_update_slice(G["mix_rows"], d_proj, (0, GM_PPROJ, 0))

    def ffn_back(which, cols_w, rows_w, h_in, dy, tok, one_by_one):
        gt, up, xn = sv[which]
        dh, dgt, dup, act, dn_ = ffn_bwd(h_in, dy, behind(row(W[which + "_norm"][l]), tok), gt, up, cols_w, rows_w,
                                         n + which + "_bwd")
        G[which + "_norm"] = dn_[0]
        zeros_rows = jnp.zeros((N_DEV, SHP, D), BF16)
        if one_by_one:
            G[which + "_gate"] = grad_cols(xn, dgt, lax.empty((1, D, FFP), BF16), 0, n + "d_" + which + "_w_gate")
            tok = on_piece(l, which + "_gate", (G[which + "_gate"],))
            G[which + "_up"] = grad_cols(xn, dup, behind(jnp.zeros((1, D, FFP), BF16), tok), 0,
                                         n + "d_" + which + "_w_up")
            tok = on_piece(l, which + "_up", (G[which + "_up"],))
            G[which + "_down"] = grad_rows(act, dy, behind(zeros_rows, tok), 0, SHP, n + "d_" + which + "_w_down",
                                           scale=0.5)
            return dh, on_piece(l, which + "_down", (G[which + "_down"],))
        cols = grad_cols(xn, dgt, lax.empty((2, D, FFP), BF16), 0, n + "d_" + which + "_w_gate")
        G[which + "_cols"] = grad_cols(xn, dup, cols, 1, n + "d_" + which + "_w_up")
        G[which + "_rows"] = grad_rows(act, dy, lax.empty((N_DEV, SHP, D), BF16), 0, SHP,
                                       n + "d_" + which + "_w_down", scale=0.5)
        return dh, on_piece(l, which, (G[which + "_cols"], G[which + "_rows"]))

    dh2, tok = ffn_back("ffn2", W["r_cols"][l], W["r_rows"][l], sv["h2"], dh3, None, False)
    dy_lru, dy_att, dy_dn = wout_bwd(dh2, W["r_rows"][l], n + "wout_bwd")
    G["mix_rows"] = grad_rows(sv["ycat"], dh2, G["mix_rows"], GM_WOUT, ROWS_DEV, n + "d_w_out")
    do, dz_dn, dnn = dn_gate_bwd(sv["o"], sv["u_dn"], behind(c_["dn_nl"], tok), dy_dn, n + "dn_gate_bwd")
    dqkvgb = dn_scan_bwd(sv["q"], sv["k"], sv["v"], sv["g"], sv["beta"], sv["states"], sv["tinvs"], do, S,
                         n + "dn_scan_bwd")
    dcc, du_ba, dvec_dn = dn_point_bwd(sv["cc"], sv["u_ba"], c_["alog"], c_["dtb"], dqkvgb, n + "dn_point_bwd")
    dqkv, dwb_dn = conv_bwd(sv["u_dn"], dcc, W["dn_conv_w"][l], S, 0, 3 * DN_W, n + "dn_conv_bwd")
    G["dn_norm"] = dnn[0, 0:HD]
    G["dn_a_log"] = dvec_dn[0, 0:DN_H]
    G["dn_dt_bias"] = dvec_dn[1, 0:DN_H]
    G["dn_conv_w"] = dwb_dn[0:4]
    du_att, drel, dsk = attn_bwd(sv["u_att"], dy_att, c_["sinks"], c_["rel"], S, n + "attn_bwd")
    G["attn_sinks"] = dsk[:, 0]
    G["rel_bias"] = drel[:, 0:REL_BUCKETS].T
    dxr, dgt_lru, dwa, dwx, dvec = lru_bwd(sv["xr"], sv["u_lru"], dy_lru, c_["wa"], c_["wx"], c_["lru_vec"], S,
                                           n + "lru_bwd")
    dx_lru, dwb_lru = conv_bwd(sv["u_lru"], dxr, W["lru_conv_w"][l], S, 0, LRU_W, n + "lru_conv_bwd")
    diag = lambda m: jnp.stack([m[c, HD * e:HD * (e + 1), HD * e:HD * (e + 1)] for c in range(2) for e in range(2)])
    G["lru_w_a"], G["lru_w_x"] = diag(dwa), diag(dwx)
    G["lru_b_a"], G["lru_b_x"], G["lru_lambda"] = dvec[0], dvec[1], dvec[2]
    G["lru_conv_w"], G["lru_conv_b"] = dwb_lru[0:4], dwb_lru[4]
    dh1, du_cat, dn = mixin_bwd(sv["h1"], dh2, row(W["mix_norm"][l]), W["w_in"][l],
                                (dx_lru, dgt_lru, du_att, dqkv, dz_dn, du_ba), n + "mixin_bwd")
    G["mix_norm"] = dn[0]
    d_in = matmul_tn(sv["xn_mix"], du_cat, n + "d_w_in")[:, :D_IN]
    d_in = d_in.reshape(D, N_DEV, D_IN // N_DEV).transpose(1, 0, 2).reshape(N_DEV, WIN_ROWS, D)
    d_in = jnp.pad(d_in, ((0, 0), (0, GM_PPROJ - GM_WIN - WIN_ROWS), (0, 0)))
    G["mix_rows"] = lax.dynamic_update_slice(G["mix_rows"], d_in, (0, GM_WIN, 0))
    tok = on_piece(l, "mix", (G["mix_rows"],))
    dh0, tok = ffn_back("ffn1", W["f1_cols"][l], W["f1_rows"][l], sv["h0"], dh1, tok, l == 0)
    return dh0, G, tok


def _core(x, pe, W, target, S, need=None, on_piece=None):
    h = x
    saved = []
    for l in range(DEPTH):
        h, sv = _layer_fwd(h, pe[l], W, l, S, need)
        saved.append(sv)
    loss_tile, dh, dfn = loss_head(h, W["final_norm"].reshape(1, -1), target, "loss_head")
    grads = [None] * DEPTH
    token = None
    for l in reversed(range(DEPTH)):
        dh, grads[l], token = _layer_bwd(dh, saved[l], pe[l], W, l, S, token, on_piece)
    return loss_tile[0, 0], dh, grads, dfn[0]


MESH_ID = pl.DeviceIdType.MESH
ANY_SPEC = pl.BlockSpec(memory_space=pl.ANY)
AXES = ("x", "y", "c")


def _my_pos():
    return lax.axis_index("x"), lax.axis_index("y"), lax.axis_index("c")


def _slot_of(px, py, pc):
    return 4 * px + 2 * py + pc


def all_gather(x, name):
    R, C = x.shape

    def body(x_ref, out_ref, send_sems, recv_sems, local_sem):
        mx, my, mc = _my_pos()
        me, sibling = (mx, my, mc), (mx, my, 1 - mc)
        chips = [(1 - mx, my), (mx, 1 - my), (1 - mx, 1 - my)]

        def copy(k, block, to, src=None):
            dst = out_ref.at[_slot_of(*block)]
            return pltpu.make_async_remote_copy(
                src_ref=dst if src is None else src, dst_ref=dst,
                send_sem=send_sems.at[k], recv_sem=recv_sems.at[k],
                device_id=to, device_id_type=MESH_ID)

        mine = pltpu.make_async_copy(x_ref, out_ref.at[_slot_of(*me)], local_sem)
        mine.start()
        first = [copy(0, me, sibling, src=x_ref)]
        first += [copy(1 + j, me, (*chip, mc), src=x_ref) for j, chip in enumerate(chips)]
        for cp in first:
            cp.start()
        passed = [copy(4 + j, (*chip, mc), sibling) for j, chip in enumerate(chips)]
        for j, chip in enumerate(chips):
            copy(1 + j, (*chip, mc), me).wait_recv()
            passed[j].start()
        copy(0, sibling, me).wait_recv()
        for j, chip in enumerate(chips):
            copy(4 + j, (*chip, 1 - mc), me).wait_recv()
        for cp in first + passed:
            cp.wait_send()
        mine.wait()

    return pl.pallas_call(
        body, name=name,
        out_shape=jax.ShapeDtypeStruct((N_DEV, R, C), x.dtype),
        in_specs=[ANY_SPEC], out_specs=ANY_SPEC,
        scratch_shapes=[pltpu.SemaphoreType.DMA((7,)), pltpu.SemaphoreType.DMA((7,)), pltpu.SemaphoreType.DMA],
    )(x)


def _col_window(ref, slot):
    return ref.at[:, pl.ds(pl.multiple_of(slot * SHP, LANE), SHP)]


def gather_layer(a_sh, b_sh, name):
    def body(a_ref, b_ref, ao_ref, bo_ref, send_sems, recv_sems, local_sems):
        mx, my, mc = _my_pos()
        me, sibling = (mx, my, mc), (mx, my, 1 - mc)
        chips = [(1 - mx, my), (mx, 1 - my), (1 - mx, 1 - my)]

        def copies(k, block, to, own=False):
            slot = _slot_of(*block)
            dsts = (_col_window(ao_ref, slot), bo_ref.at[slot])
            srcs = (a_ref, b_ref) if own else dsts
            return [pltpu.make_async_remote_copy(
                src_ref=s, dst_ref=d, send_sem=send_sems.at[2 * k + i], recv_sem=recv_sems.at[2 * k + i],
                device_id=to, device_id_type=MESH_ID) for i, (s, d) in enumerate(zip(srcs, dsts))]

        mine = [pltpu.make_async_copy(a_ref, _col_window(ao_ref, _slot_of(*me)), local_sems.at[0]),
                pltpu.make_async_copy(b_ref, bo_ref.at[_slot_of(*me)], local_sems.at[1])]
        for cp in mine:
            cp.start()
        first = copies(0, me, sibling, own=True)
        for j, chip in enumerate(chips):
            first += copies(1 + j, me, (*chip, mc), own=True)
        for cp in first:
            cp.start()
        passed = []
        for j, chip in enumerate(chips):
            for cp in copies(1 + j, (*chip, mc), me):
                cp.wait_recv()
            fwd = copies(4 + j, (*chip, mc), sibling)
            for cp in fwd:
                cp.start()
            passed += fwd
        for cp in copies(0, sibling, me):
            cp.wait_recv()
        for j, chip in enumerate(chips):
            for cp in copies(4 + j, (*chip, 1 - mc), me):
                cp.wait_recv()
        for cp in first + passed:
            cp.wait_send()
        for cp in mine:
            cp.wait()

    return pl.pallas_call(
        body, name=name,
        out_shape=[jax.ShapeDtypeStruct((a_sh.shape[0], FFP), a_sh.dtype),
                   jax.ShapeDtypeStruct((N_DEV,) + b_sh.shape, b_sh.dtype)],
        in_specs=[ANY_SPEC, ANY_SPEC], out_specs=[ANY_SPEC, ANY_SPEC],
        scratch_shapes=[pltpu.SemaphoreType.DMA((14,)), pltpu.SemaphoreType.DMA((14,)),
                        pltpu.SemaphoreType.DMA((2,))],
    )(a_sh, b_sh)


HBM_SPEC = pl.BlockSpec(memory_space=pltpu.HBM)
SEM_SPEC = pl.BlockSpec(memory_space=pltpu.SEMAPHORE)
SPLIT_EFFECT = pltpu.CompilerParams(has_side_effects=pltpu.SideEffectType.DATAFLOW_SIDE_EFFECTING)


def _split_ends(mode, src_ref, dst_ref, src_slot, dst_slot):
    cols = mode.endswith("cols")
    if mode.startswith("gather"):
        return src_ref, (_col_window(dst_ref, dst_slot) if cols else dst_ref.at[dst_slot])
    return (_col_window(src_ref, src_slot) if cols else src_ref.at[src_slot]), dst_ref.at[dst_slot]


def _split_peers():
    mx, my, mc = _my_pos()
    for r in range(1, N_DEV):
        peer = (1 - mx if r & 4 else mx, 1 - my if r & 2 else my, 1 - mc if r & 1 else mc)
        yield r - 1, peer, _slot_of(*peer)


def split_start(modes, srcs, dsts, name, after=()):
    n = len(modes)
    m = len(after)

    def body(*refs):
        send_sems, recv_sems, token = refs[2 * n + m], refs[2 * n + m + 1], refs[-1]
        mine = _slot_of(*_my_pos())
        for k, peer, ps in _split_peers():
            for i in range(n):
                src, dst = _split_ends(modes[i], refs[i], refs[n + i], ps, mine)
                pltpu.make_async_remote_copy(
                    src_ref=src, dst_ref=dst, send_sem=send_sems.at[n * k + i], recv_sem=recv_sems.at[n * k + i],
                    device_id=peer, device_id_type=MESH_ID).start()
        for i in range(n):
            src, dst = _split_ends(modes[i], refs[i], refs[n + i], mine, mine)
            pltpu.make_async_copy(src, dst, recv_sems.at[n * (N_DEV - 1) + i]).start()
        token[...] = jnp.zeros_like(token)

    bufs = tuple(srcs) + tuple(dsts)
    sems = pltpu.SemaphoreType.DMA((n * N_DEV,))
    res = pl.pallas_call(
        body, name=name,
        out_shape=(sems, sems) + tuple(pltpu.HBM(t.shape, t.dtype) for t in bufs)
        + (jax.ShapeDtypeStruct((8, LANE), F32),),
        in_specs=[HBM_SPEC] * (2 * n) + [ANY_SPEC] * m,
        out_specs=(SEM_SPEC, SEM_SPEC) + (HBM_SPEC,) * (2 * n) + (pl.BlockSpec(memory_space=pltpu.VMEM),),
        input_output_aliases={i: 2 + i for i in range(2 * n)},
        compiler_params=SPLIT_EFFECT,
    )(*(pltpu.with_memory_space_constraint(t, pltpu.HBM) for t in bufs), *after)
    return list(res[:-1]), res[-1]


def split_wait(modes, started, after, name):
    n = len(modes)
    after = tuple(after) if isinstance(after, (tuple, list)) else (after,)
    send_sems, recv_sems, bufs = started[0], started[1], started[2:]

    def body(*refs):
        send_sems, recv_sems = refs[2 * n], refs[2 * n + 1]
        mine = _slot_of(*_my_pos())
        for k, peer, ps in _split_peers():
            for i in range(n):
                sent = _split_ends(modes[i], refs[i], refs[n + i], ps, mine)[0]
                landed = _split_ends(modes[i], refs[i], refs[n + i], mine, ps)[1]
                cp = pltpu.make_async_remote_copy(
                    src_ref=sent, dst_ref=landed, send_sem=send_sems.at[n * k + i],
                    recv_sem=recv_sems.at[n * k + i], device_id=peer, device_id_type=MESH_ID)
                cp.wait_send()
                cp.wait_recv()
        for i in range(n):
            src, dst = _split_ends(modes[i], refs[i], refs[n + i], mine, mine)
            pltpu.make_async_copy(src, dst, recv_sems.at[n * (N_DEV - 1) + i]).wait()

    res = pl.pallas_call(
        body, name=name,
        out_shape=tuple(pltpu.HBM(t.shape, t.dtype) for t in bufs),
        in_specs=[HBM_SPEC] * (2 * n) + [SEM_SPEC, SEM_SPEC] + [ANY_SPEC] * len(after),
        out_specs=(HBM_SPEC,) * (2 * n),
        input_output_aliases={i: i for i in range(2 * n)},
        compiler_params=SPLIT_EFFECT,
    )(*bufs, send_sems, recv_sems, *after)
    return list(res[n:])


def sum_parts(parts, name):
    _, R, C = parts.shape
    tr = _pick(R, (512, 336, 272, 256, 128, 64, 32, 16, 8))

    def body(p_ref, o_ref):
        acc = p_ref[0].astype(F32)
        for k in range(1, N_DEV):
            acc += p_ref[k].astype(F32)
        o_ref[...] = acc

    return pl.pallas_call(
        body, name=name, grid=(R // tr,),
        in_specs=[pl.BlockSpec((N_DEV, tr, C), lambda i: (0, i, 0))],
        out_specs=pl.BlockSpec((tr, C), lambda i: (i, 0)),
        out_shape=jax.ShapeDtypeStruct((R, C), F32),
        compiler_params=_cp("parallel"),
    )(parts)


def sum_into(parts, row0, rows, cols, layer, dst, name):
    tr = _pick(rows, (512, 352, 128))
    blk0 = row0 // tr

    def body(p_ref, *rest):
        o_ref = rest[-1]
        acc = p_ref[0, :, 0:cols].astype(F32)
        for k in range(1, N_DEV):
            acc += p_ref[k, :, 0:cols].astype(F32)
        o_ref[0] = acc

    aliased = dst is not None
    return pl.pallas_call(
        body, name=name, grid=(rows // tr,),
        in_specs=[pl.BlockSpec((N_DEV, tr, parts.shape[2]), lambda i: (0, blk0 + i, 0))]
        + [pl.BlockSpec(memory_space=pl.ANY)] * aliased,
        out_specs=pl.BlockSpec((1, tr, cols), lambda i: (layer, i, 0)),
        out_shape=jax.ShapeDtypeStruct((DEPTH, rows, cols), F32),
        input_output_aliases={1: 0} if aliased else {},
        compiler_params=_cp("parallel"),
    )(*((parts, dst) if aliased else (parts,)))


def adamw(g, w, m, v, name):
    lead, (R, C) = g.shape[:-2], g.shape[-2:]
    tr = _pick(R, (512, 352, 256, 128, 64, 32, 16, 8))
    c1 = 1.0 - ADAM_B1 ** ADAM_STEP
    c2 = 1.0 - ADAM_B2 ** ADAM_STEP

    def body(g_ref, w_ref, m_ref, v_ref, d_ref, nm_ref, nv_ref):
        gg = g_ref[...]
        mm = ADAM_B1 * m_ref[...] + (1.0 - ADAM_B1) * gg
        vv = ADAM_B2 * v_ref[...] + (1.0 - ADAM_B2) * (gg * gg)
        nm_ref[...] = mm
        nv_ref[...] = vv
        d_ref[...] = -ADAM_LR * ((mm / c1) / (jnp.sqrt(vv / c2) + ADAM_EPS) + ADAM_WD * w_ref[...])

    if lead:
        spec = pl.BlockSpec((1, tr, C), lambda l, i: (l, i, 0))
    else:
        spec = pl.BlockSpec((tr, C), lambda l, i: (i, 0))
    return pl.pallas_call(
        body, name=name, grid=(lead[0] if lead else 1, R // tr),
        in_specs=[spec] * 4, out_specs=[spec] * 3,
        out_shape=[jax.ShapeDtypeStruct(g.shape, F32)] * 3,
        compiler_params=_cp("parallel", "parallel"),
    )(g, w, m, v)


BIG = (("ffn1_w_gate", 1, D, FF), ("ffn1_w_up", 1, D, FF), ("ffn1_w_down", 0, FF, D),
       ("w_in", 1, D, D_IN), ("w_out", 0, D, D),
       ("ffn2_w_gate", 1, D, FF), ("ffn2_w_up", 1, D, FF), ("ffn2_w_down", 0, FF, D),
       ("ple_w_gate", 0, D, D), ("ple_w_proj", 1, PLE, D))
SMALL = (("ffn1_norm", (D,), None), ("mix_norm", (D,), None), ("lru_conv_w", (4, LRU_W), LRU_W // N_DEV),
         ("lru_conv_b", (LRU_W,), None), ("lru_w_a", (4, HD, HD), None), ("lru_b_a", (LRU_W,), None),
         ("lru_w_x", (4, HD, HD), None), ("lru_b_x", (LRU_W,), None), ("lru_lambda", (LRU_W,), None),
         ("attn_sinks", (ATT_H,), None), ("dn_conv_w", (4, 3 * DN_W), 3 * DN_W // N_DEV),
         ("dn_a_log", (DN_H,), None), ("dn_dt_bias", (DN_H,), None), ("dn_norm", (HD,), None),
         ("ffn2_norm", (D,), None), ("ple_norm", (D,), None))
SINGLE = (("rel_bias", (REL_BUCKETS, ATT_H)), ("final_norm", (D,)))


def _pack_rows(arrs, width, mult):
    flat = jnp.concatenate([a.reshape(-1) for a in arrs])
    rows = -(-flat.shape[0] // (width * mult)) * mult
    return jnp.pad(flat, (0, rows * width - flat.shape[0])).reshape(rows, width)


def _unpack_rows(packed, shapes):
    flat = packed.reshape(-1)
    out, off = [], 0
    for s in shapes:
        n = int(np.prod(s))
        out.append(flat[off:off + n].reshape(s))
        off += n
    return out


def _pad_rows(w, r):
    return jnp.pad(w, ((0, r - w.shape[0]), (0, 0)))


def _shard_ffn(a, l, which, more=()):
    cols = jnp.concatenate([a[which + "_w_gate"][l], a[which + "_w_up"][l]], axis=0)
    rows = jnp.concatenate([_pad_rows(a[which + "_w_down"][l], SHP)] + list(more), axis=0)
    return jnp.pad(cols, ((0, 0), (0, SHP - SH))).astype(BF16), rows.astype(BF16)


def _shards(a, l):
    w_in_rows = _pad_rows(a["w_in"][l].reshape(WIN_ROWS, D), IN_ROWS).astype(BF16)
    rest = _shard_ffn(a, l, "ffn2", (a["w_out"][l], a["ple_w_gate"][l], a["ple_w_proj"][l].reshape(-1, D)))
    return _shard_ffn(a, l, "ffn1"), (w_in_rows,), rest


def _full_w_in(in_rows):
    sh = in_rows[:, :WIN_ROWS, :].reshape(N_DEV, D, D_IN // N_DEV)
    return jnp.pad(sh.transpose(1, 0, 2).reshape(D, D_IN), ((0, 0), (0, D_IN_PAD - D_IN)))


def _full_ple_proj(r_rows):
    sh = r_rows[:, R_PPROJ:R_ROWS, :].reshape(N_DEV, PLE, D // N_DEV)
    return sh.transpose(1, 0, 2).reshape(PLE, D)


PIECE_NAMES = {"ffn1": ("ffn1_w_gate", "ffn1_w_up", "ffn1_w_down"), "ffn2": ("ffn2_w_gate", "ffn2_w_up", "ffn2_w_down"),
               "mix": ("w_out", "ple_w_gate", "w_in", "ple_w_proj")}


def _shard_grads(piece, summed):
    if piece == "mix":
        rows, = summed
        return {"w_out": rows[GM_WOUT:GM_WOUT + ROWS_DEV], "ple_w_gate": rows[GM_PGATE:GM_PGATE + ROWS_DEV],
                "w_in": rows[GM_WIN:GM_WIN + WIN_ROWS].reshape(D, D_IN // N_DEV),
                "ple_w_proj": rows[GM_PPROJ:GM_END].reshape(PLE, D // N_DEV)}
    if piece in ("ffn1_gate", "ffn1_up"):
        return {piece.replace("_", "_w_"): summed[0][:, :SH]}
    if piece == "ffn1_down":
        return {"ffn1_w_down": summed[0][:SH]}
    cols, rows = summed
    return {piece + "_w_gate": cols[:D, :SH], piece + "_w_up": cols[D:, :SH], piece + "_w_down": rows[:SH]}


def kernel(x, p, ffn1_norm, ffn1_w_gate, ffn1_w_up, ffn1_w_down, mix_norm, w_in, lru_conv_w, lru_conv_b, lru_w_a, lru_b_a, lru_w_x, lru_b_x, lru_lambda, attn_sinks, rel_bias, dn_conv_w, dn_a_log, dn_dt_bias, dn_norm, w_out, ffn2_norm, ffn2_w_gate, ffn2_w_up, ffn2_w_down, ple_norm, ple_w_gate, ple_w_proj, final_norm, loss_target, m_ffn1_norm, m_ffn1_w_gate, m_ffn1_w_up, m_ffn1_w_down, m_mix_norm, m_w_in, m_lru_conv_w, m_lru_conv_b, m_lru_w_a, m_lru_b_a, m_lru_w_x, m_lru_b_x, m_lru_lambda, m_attn_sinks, m_rel_bias, m_dn_conv_w, m_dn_a_log, m_dn_dt_bias, m_dn_norm, m_w_out, m_ffn2_norm, m_ffn2_w_gate, m_ffn2_w_up, m_ffn2_w_down, m_ple_norm, m_ple_w_gate, m_ple_w_proj, m_final_norm, v_ffn1_norm, v_ffn1_w_gate, v_ffn1_w_up, v_ffn1_w_down, v_mix_norm, v_w_in, v_lru_conv_w, v_lru_conv_b, v_lru_w_a, v_lru_b_a, v_lru_w_x, v_lru_b_x, v_lru_lambda, v_attn_sinks, v_rel_bias, v_dn_conv_w, v_dn_a_log, v_dn_dt_bias, v_dn_norm, v_w_out, v_ffn2_norm, v_ffn2_w_gate, v_ffn2_w_up, v_ffn2_w_down, v_ple_norm, v_ple_w_gate, v_ple_w_proj, v_final_norm):
    a = dict(locals())
    nb, S, _ = x.shape
    T = nb * S
    my_slot = _slot_of(*_my_pos())

    W = {k: [None] * DEPTH for k in ("f1_cols", "f1_rows", "w_in", "r_cols", "r_rows", "ple_w_proj")}
    GATHER, SCATTER = ("gather_cols", "gather_block"), ("scatter_cols", "scatter_block")
    GROUP_MODES = {"f1": GATHER, "in": GATHER[1:], "rest": GATHER}

    def set_group(l, group, bufs):
        if group == "f1":
            W["f1_cols"][l], W["f1_rows"][l] = bufs
        elif group == "in":
            W["w_in"][l] = _full_w_in(bufs[0])
        else:
            W["r_cols"][l], W["r_rows"][l] = bufs
            W["ple_w_proj"][l] = _full_ple_proj(bufs[1])

    def landing(mode, src):
        if mode == "gather_cols":
            return lax.empty((src.shape[0], FFP), src.dtype)
        if mode == "scatter_cols":
            return lax.empty((N_DEV, src.shape[0], SHP), src.dtype)
        return lax.empty((N_DEV,) + src.shape[mode == "scatter_block":], src.dtype)

    def start(modes, srcs, name, after=()):
        return split_start(modes, srcs, [landing(m, s) for m, s in zip(modes, srcs)], name, after)

    shards0, shards1 = _shards(a, 0), _shards(a, 1)
    set_group(0, "f1", gather_layer(*shards0[0], "gather_weights_l0_ffn1"))
    taps = all_gather(_pack_rows([lru_conv_w, dn_conv_w], LANE, 8), "gather_conv_taps")
    flat_taps = taps.reshape(N_DEV, -1)
    for name, first, tap in (("lru_conv_w", 0, lru_conv_w), ("dn_conv_w", lru_conv_w.size, dn_conv_w)):
        per_dev = flat_taps[:, first:first + tap.size].reshape((N_DEV,) + tap.shape)
        W[name] = jnp.moveaxis(per_dev, 0, -2).reshape(tap.shape[:-1] + (N_DEV * tap.shape[-1],))
    for name, _, cols in SMALL:
        if cols is None:
            W[name] = a[name]
    W["rel_bias"], W["final_norm"] = rel_bias, final_norm

    gathers, after = {}, (W["f1_rows"][0], taps)
    for l, group, srcs in ((0, "in", shards0[1]), (0, "rest", shards0[2]),
                           (1, "f1", shards1[0]), (1, "in", shards1[1]), (1, "rest", shards1[2])):
        gathers[l, group], token = start(GROUP_MODES[group], srcs, f"gather_start_l{l}_{group}", after)
        after = (token,)
    W["ffn1_norm"] = ffn1_norm + token[0, 0]
    flight, tokens = {}, {}

    def need(l, group, h):
        if (l, group) in gathers:
            set_group(l, group, split_wait(GROUP_MODES[group], gathers[l, group], h, f"gather_wait_l{l}_{group}"))

    def piece_modes(piece):
        return {"mix": SCATTER[1:], "ffn1_gate": SCATTER[:1], "ffn1_up": SCATTER[:1], "ffn1_down": SCATTER[1:]}.get(
            piece, SCATTER)

    def on_piece(l, piece, bufs):
        bufs = [b.reshape(-1, FFP) if b.shape[-1] == FFP else b for b in bufs]
        flight[l, piece], tokens[l, piece] = start(piece_modes(piece), bufs, f"exchange_start_l{l}_{piece}")
        return tokens[l, piece][0, 0]

    loss_local, dx, grads, d_final = _core(x.reshape(T, D), p.reshape(DEPTH, T, PLE), W,
                                           loss_target.reshape(T, D), S, need, on_piece)
    loss = lax.psum(loss_local, AXES)

    small_full = [jnp.stack([grads[l][name] for l in range(DEPTH)]) for name, _, _ in SMALL]
    small_full += [grads[0]["rel_bias"] + grads[1]["rel_bias"], d_final]
    small_flight, _ = start(("gather_block",), (_pack_rows(small_full, LANE, 8),), "gather_start_small_grads",
                            (tokens[0, "ffn1_down"],))

    out = {}

    landed = {}

    def land(l, piece, after):
        landed[l, piece] = split_wait(piece_modes(piece), flight[l, piece], after, f"exchange_wait_l{l}_{piece}")

    def where(name, l):
        if name in ("w_out", "ple_w_gate"):
            return "mix", 0, (GM_WOUT if name == "w_out" else GM_PGATE), ROWS_DEV, D
        ffn, kind = name[:4], name[7:]
        one_by_one = (l, ffn) == (0, "ffn1")
        if kind == "down":
            return (ffn + "_down", 0, 0, SH, D) if one_by_one else (ffn, 1, 0, SH, D)
        if one_by_one:
            return f"{ffn}_{kind}", 0, 0, D, SH
        return ffn, 0, (0 if kind == "gate" else D), D, SH

    def update(piece):
        for name in PIECE_NAMES[piece]:
            if name in ("w_in", "ple_w_proj"):
                g = jnp.stack([_shard_grads("mix", [mix_sums[l]])[name] for l in range(DEPTH)])
            else:
                g = None
                for l in reversed(range(DEPTH)):
                    piece_l, idx, row0, rows, cols = where(name, l)
                    g = sum_into(landed[l, piece_l][idx], row0, rows, cols, l, g, f"sum_{name}_l{l}")
            out[name] = (g,) + tuple(adamw(g, a[name], a["m_" + name], a["v_" + name], "adamw_" + name))

    for l, piece in ((1, "ffn2"), (1, "mix"), (1, "ffn1"), (0, "ffn2"), (0, "mix")):
        land(l, piece, (dx, tokens[0, "ffn1_down"]))
    mix_sums = [sum_parts(landed[l, "mix"][0], f"sum_mix_grads_l{l}") for l in range(DEPTH)]
    update("ffn2")
    update("mix")
    done_early = tuple(out[n][1] for n in PIECE_NAMES["ffn2"] + PIECE_NAMES["mix"])
    small_parts, = split_wait(("gather_block",), small_flight, done_early, "gather_wait_small_grads")
    small_sum = sum_parts(small_parts, "sum_small_grads")
    g_small = dict(zip([n for n, _, _ in SMALL] + [n for n, _ in SINGLE],
                       _unpack_rows(small_sum, [s.shape for s in small_full])))
    for name, _, cols in SMALL:
        if cols is not None:
            g_small[name] = lax.dynamic_slice_in_dim(g_small[name], my_slot * cols, cols, axis=2)

    for n in [n for n, _, _ in SMALL] + [n for n, _ in SINGLE]:
        shape = a[n].shape
        flat = lambda t: t.reshape((-1, shape[-1]) if len(shape) > 1 else (1, -1))
        res = adamw(flat(g_small[n]), flat(a[n]), flat(a["m_" + n]), flat(a["v_" + n]), "adamw_" + n)
        out[n] = (g_small[n].reshape(shape),) + tuple(r.reshape(shape) for r in res)

    for piece in ("ffn1_gate", "ffn1_up", "ffn1_down"):
        land(0, piece, (out["final_norm"][1],) + done_early)
    update("ffn1")

    order = ['ffn1_norm', 'ffn1_w_gate', 'ffn1_w_up', 'ffn1_w_down', 'mix_norm', 'w_in', 'lru_conv_w', 'lru_conv_b',
             'lru_w_a', 'lru_b_a', 'lru_w_x', 'lru_b_x', 'lru_lambda', 'attn_sinks', 'rel_bias', 'dn_conv_w',
             'dn_a_log', 'dn_dt_bias', 'dn_norm', 'w_out', 'ffn2_norm', 'ffn2_w_gate', 'ffn2_w_up', 'ffn2_w_down',
             'ple_norm', 'ple_w_gate', 'ple_w_proj', 'final_norm']
    return (loss, dx.reshape(x.shape)) + tuple(out[n][k] for k in range(4) for n in order)
```

```python
import functools
import math

import numpy as np
import jax
import jax.numpy as jnp
from jax import lax
from jax.experimental import pallas as pl
from jax.experimental.pallas import tpu as pltpu

F32 = jnp.float32
BF16 = jnp.bfloat16
HI = lax.Precision.HIGHEST

D = 1024
DEPTH = 2
EPS = 1e-6
PLE = 256
FF = 2816
HD = 64
LRU_W = 256
LRU_C = 8.0
ATT_W = 512
ATT_H = 8
ATT_KV = 2
ATT_G = 4
KV_W = 128
WINDOW = 128
BQ = 128
REL_BUCKETS = 32
REL_MAX_DIST = 128
DN_W = 256
DN_H = 4
CHUNK = 64
D_IN = 2312
D_IN_PAD = 2432
N_DEV = 8

ADAM_LR = 0.001
ADAM_B1 = 0.9
ADAM_B2 = 0.999
ADAM_EPS = 1e-08
ADAM_WD = 0.01
ADAM_STEP = 10

LANE = 128
VMEM_LIMIT = 56 * 1024 * 1024
SH = FF // N_DEV
SHP = 384
FFP = N_DEV * SHP
FF_TILE = 2 * SHP
FF_SUB = 256
TOK_TILE = 512
R_DOWN2, R_WOUT, R_PGATE, R_PPROJ, R_ROWS = 0, 384, 512, 640, 672
WIN_ROWS = D * D_IN // N_DEV // 1024
IN_ROWS = 304
NEG = -1e30


def _cp(*sem):
    return pltpu.CompilerParams(dimension_semantics=tuple(sem), vmem_limit_bytes=VMEM_LIMIT)


def _dg(a, b, ca, cb, exact):
    dims = (((ca,), (cb,)), ((), ()))
    if exact == "f32":
        return lax.dot_general(a.astype(F32), b.astype(F32), dims, precision=HI, preferred_element_type=F32)
    if exact == "split":
        a_hi, b_hi = a.astype(BF16), b.astype(BF16)
        a_lo = (a - a_hi.astype(F32)).astype(BF16)
        b_lo = (b - b_hi.astype(F32)).astype(BF16)
        dot = lambda u, v: lax.dot_general(u, v, dims, preferred_element_type=F32)
        return dot(a_hi, b_hi) + (dot(a_hi, b_lo) + dot(a_lo, b_hi))
    return lax.dot_general(a.astype(BF16), b.astype(BF16), dims, preferred_element_type=F32)


def _make_mm(exact):
    @jax.custom_vjp
    def mm(a, b):
        return _dg(a, b, 1, 0, exact)

    @jax.custom_vjp
    def mm_nt(a, b):
        return _dg(a, b, 1, 1, exact)

    @jax.custom_vjp
    def mm_tn(a, b):
        return _dg(a, b, 0, 0, exact)

    mm.defvjp(lambda a, b: (mm(a, b), (a, b)),
              lambda r, d: (mm_nt(d, r[1]), mm_tn(r[0], d)))
    mm_nt.defvjp(lambda a, b: (mm_nt(a, b), (a, b)),
                 lambda r, d: (mm(d, r[1]), mm_tn(d, r[0])))
    mm_tn.defvjp(lambda a, b: (mm_tn(a, b), (a, b)),
                 lambda r, d: (mm_nt(r[1], d), mm(r[0], d)))
    return mm, mm_nt, mm_tn


_mm, _mm_nt, _mm_tn = _make_mm("bf16")
_mmx, _mmx_nt, _mmx_tn = _make_mm("f32")
_mm3, _mm3_nt, _mm3_tn = _make_mm("split")


def _iota(shape, dim):
    return lax.broadcasted_iota(jnp.int32, shape, dim)


def _sigmoid(x):
    return 0.5 * jnp.tanh(0.5 * x) + 0.5


def _rms(h, g):
    rstd = lax.rsqrt(jnp.mean(h * h, axis=-1, keepdims=True) + EPS)
    xhat = h * rstd
    return xhat * g, xhat, rstd


def _rms_bwd(dxn, xhat, rstd, g):
    dxhat = dxn * g
    dh = rstd * (dxhat - xhat * jnp.mean(dxhat * xhat, axis=-1, keepdims=True))
    dg = jnp.sum(dxn * xhat, axis=0, keepdims=True)
    return dh, dg


def _row_spec(tm, n):
    return pl.BlockSpec((tm, n), lambda i, *_: (i, 0))


def _full_spec(shape):
    nd = len(shape)
    return pl.BlockSpec(shape, lambda *_: (0,) * nd)


def _ffn_weight_specs():
    return [pl.BlockSpec((D, FF_TILE), lambda i, j: (0, j)),
            pl.BlockSpec((D, FF_TILE), lambda i, j: (1, j)),
            pl.BlockSpec((2, SHP, D), lambda i, j: (j, 0, 0))]


def ffn_fwd(h, g, wa, wb, name):
    T = h.shape[0]
    tm = min(2 * TOK_TILE, T)
    nj = FFP // FF_TILE

    def body(h_ref, g_ref, wg_ref, wu_ref, wd_ref, o_ref, gt_ref, up_ref, xn_ref):
        j = pl.program_id(1)

        @pl.when(j == 0)
        def _():
            hh = h_ref[...]
            xn_ref[...] = _rms(hh, g_ref[...])[0].astype(BF16)
            o_ref[...] = hh

        blocks = [slice(c, c + FF_SUB) for c in range(0, FF_TILE, FF_SUB)]
        xn = xn_ref[...]
        wd = wd_ref[...].reshape(FF_TILE, D)
        gt = [_mm(xn, wg_ref[:, c]) for c in blocks]
        up = [_mm(xn, wu_ref[:, c]) for c in blocks]
        act = [t * _sigmoid(t) * u for t, u in zip(gt, up)]
        down = [_mm(act[k], wd[c]) for k, c in enumerate(blocks)]
        for k, c in enumerate(blocks):
            gt_ref[:, c] = gt[k].astype(BF16)
            up_ref[:, c] = up[k].astype(BF16)
        o_ref[...] += 0.5 * functools.reduce(lambda x, y: x + y, down)

    tile = pl.BlockSpec((tm, FF_TILE), lambda i, j: (i, j))
    return pl.pallas_call(
        body, name=name, grid=(T // tm, nj),
        in_specs=[pl.BlockSpec((tm, D), lambda i, j: (i, 0)),
                  pl.BlockSpec((1, D), lambda i, j: (0, 0))] + _ffn_weight_specs(),
        out_specs=[pl.BlockSpec((tm, D), lambda i, j: (i, 0)), tile, tile,
                   pl.BlockSpec((tm, D), lambda i, j: (i, 0))],
        out_shape=[jax.ShapeDtypeStruct((T, D), F32), jax.ShapeDtypeStruct((T, FFP), BF16),
                   jax.ShapeDtypeStruct((T, FFP), BF16), jax.ShapeDtypeStruct((T, D), BF16)],
        compiler_params=_cp("parallel", "arbitrary"),
    )(h, g, wa, wa, wb)


def ffn_bwd(h, dy, g, gt_saved, up_saved, wa, wb, name):
    T = h.shape[0]
    tm = min(TOK_TILE, T)
    nj = FFP // FF_TILE

    def body(h_ref, dy_ref, g_ref, gt_ref, up_ref, wg_ref, wu_ref, wd_ref,
             dh_ref, dg_ref, du_ref, a_ref, dn_ref, dxn_s, dyh_s):
        i = pl.program_id(0)
        j = pl.program_id(1)

        @pl.when(j == 0)
        def _():
            dxn_s[...] = jnp.zeros_like(dxn_s)
            dyh_s[...] = (0.5 * dy_ref[...]).astype(BF16)

        @pl.when((i == 0) & (j == 0))
        def _():
            dn_ref[...] = jnp.zeros_like(dn_ref)

        blocks = [slice(c, c + FF_SUB) for c in range(0, FF_TILE, FF_SUB)]
        wd = wd_ref[...].reshape(FF_TILE, D)
        dyh = dyh_s[...]
        gt = [gt_ref[:, c].astype(F32) for c in blocks]
        up = [up_ref[:, c].astype(F32) for c in blocks]
        da = [_mm_nt(dyh, wd[c]) for c in blocks]
        sg = [_sigmoid(t) for t in gt]
        si = [t * s for t, s in zip(gt, sg)]
        dup = [d * s for d, s in zip(da, si)]
        dgt = [d * u * (s * (1.0 + t * (1.0 - s))) for d, u, s, t in zip(da, up, sg, gt)]
        dxn = [_mm_nt(dgt[k], wg_ref[:, c]) + _mm_nt(dup[k], wu_ref[:, c]) for k, c in enumerate(blocks)]
        for k, c in enumerate(blocks):
            dg_ref[:, c] = dgt[k].astype(BF16)
            du_ref[:, c] = dup[k].astype(BF16)
            a_ref[:, c] = (si[k] * up[k]).astype(BF16)
        dxn_s[...] += functools.reduce(lambda x, y: x + y, dxn)

        @pl.when(j == nj - 1)
        def _():
            gg = g_ref[...]
            _, xhat, rstd = _rms(h_ref[...], gg)
            dh, dn = _rms_bwd(dxn_s[...], xhat, rstd, gg)
            dh_ref[...] = dy_ref[...] + dh
            dn_ref[...] += dn

    tile = pl.BlockSpec((tm, FF_TILE), lambda i, j: (i, j))
    return pl.pallas_call(
        body, name=name, grid=(T // tm, nj),
        in_specs=[pl.BlockSpec((tm, D), lambda i, j: (i, 0)),
                  pl.BlockSpec((tm, D), lambda i, j: (i, 0)),
                  pl.BlockSpec((1, D), lambda i, j: (0, 0)), tile, tile] + _ffn_weight_specs(),
        out_specs=[pl.BlockSpec((tm, D), lambda i, j: (i, 0)), tile, tile, tile,
                   pl.BlockSpec((1, D), lambda i, j: (0, 0))],
        out_shape=[jax.ShapeDtypeStruct((T, D), F32)] + [jax.ShapeDtypeStruct((T, FFP), BF16)] * 3
        + [jax.ShapeDtypeStruct((1, D), F32)],
        scratch_shapes=[pltpu.VMEM((tm, D), F32), pltpu.VMEM((tm, D), BF16)],
        compiler_params=_cp("arbitrary", "arbitrary"),
    )(h, dy, g, gt_saved, up_saved, wa, wa, wb)


def _pick(n, prefs):
    for t in prefs:
        if n % t == 0:
            return t
    return n


def _tn_body(nk, scale, out_dtype, squeeze):
    def body(a_ref, b_ref, *rest):
        o_ref, acc = rest[-2], rest[-1]
        k = pl.program_id(2)

        @pl.when(k == 0)
        def _():
            acc[...] = jnp.zeros_like(acc)

        acc[...] += _mm_tn(a_ref[...], b_ref[...])

        @pl.when(k == nk - 1)
        def _():
            res = (scale * acc[...]).astype(out_dtype)
            if squeeze:
                o_ref[0] = res
            else:
                o_ref[...] = res

    return body


def matmul_tn(a, b, name, scale=1.0, out_dtype=BF16):
    T, M = a.shape
    N = b.shape[1]
    tmm = _pick(M, (512, 256))
    tnn = _pick(N, (1024, 2432))
    tk = min(2 * TOK_TILE, T)
    nk = T // tk
    return pl.pallas_call(
        _tn_body(nk, scale, out_dtype, False), name=name, grid=(M // tmm, N // tnn, nk),
        in_specs=[pl.BlockSpec((tk, tmm), lambda i, j, k: (k, i)),
                  pl.BlockSpec((tk, tnn), lambda i, j, k: (k, j))],
        out_specs=pl.BlockSpec((tmm, tnn), lambda i, j, k: (i, j)),
        out_shape=jax.ShapeDtypeStruct((M, N), out_dtype),
        scratch_shapes=[pltpu.VMEM((tmm, tnn), F32)],
        compiler_params=_cp("parallel", "parallel", "arbitrary"),
    )(a, b)


def grad_cols(a, b, dst, slot, name):
    T = a.shape[0]
    tmm, tnn = D, FFP // 2
    tk = min(2 * TOK_TILE, T)
    nk = T // tk
    return pl.pallas_call(
        _tn_body(nk, 1.0, BF16, True), name=name, grid=(D // tmm, FFP // tnn, nk),
        in_specs=[pl.BlockSpec((tk, tmm), lambda i, j, k: (k, i)),
                  pl.BlockSpec((tk, tnn), lambda i, j, k: (k, j)),
                  pl.BlockSpec(memory_space=pl.ANY)],
        out_specs=pl.BlockSpec((1, tmm, tnn), lambda i, j, k: (slot, i, j)),
        out_shape=jax.ShapeDtypeStruct(dst.shape, dst.dtype),
        scratch_shapes=[pltpu.VMEM((tmm, tnn), F32)],
        input_output_aliases={2: 0},
        compiler_params=_cp("parallel", "parallel", "arbitrary"),
    )(a, b, dst)


def grad_rows(a, b, dst, row0, rows, name, scale=1.0):
    T = a.shape[0]
    tk = min(2 * TOK_TILE, T)
    nk = T // tk
    blk = row0 // rows

    def body(a_ref, b_ref, dst_ref, o_ref, acc):
        k = pl.program_id(0)

        @pl.when(k == 0)
        def _():
            acc[...] = jnp.zeros_like(acc)

        acc[...] += _mm_tn(a_ref[...], b_ref[...])

        @pl.when(k == nk - 1)
        def _():
            o_ref[...] = (scale * acc[...]).astype(BF16).reshape(N_DEV, rows, D)

    return pl.pallas_call(
        body, name=name, grid=(nk,),
        in_specs=[pl.BlockSpec((tk, N_DEV * rows), lambda k: (k, 0)),
                  pl.BlockSpec((tk, D), lambda k: (k, 0)),
                  pl.BlockSpec(memory_space=pl.ANY)],
        out_specs=pl.BlockSpec((N_DEV, rows, D), lambda k: (0, blk, 0)),
        out_shape=jax.ShapeDtypeStruct(dst.shape, dst.dtype),
        scratch_shapes=[pltpu.VMEM((N_DEV * rows, D), F32)],
        input_output_aliases={2: 0},
        compiler_params=_cp("arbitrary"),
    )(a, b, dst)


U_SPLITS = (512, 768, 1024, 128)
U_OFFS = (0, 512, 1280, 2304)


def mixin_fwd(h, g, w_in, name):
    T = h.shape[0]
    tm = min(TOK_TILE, T)

    def body(h_ref, g_ref, w_ref, u0, u1, u2, u3, xn_ref):
        xn = _rms(h_ref[...], g_ref[...])[0].astype(BF16)
        xn_ref[...] = xn
        u = _mm(xn, w_ref[...])
        for ref, off, n in zip((u0, u1, u2, u3), U_OFFS, U_SPLITS):
            ref[...] = u[:, off:off + n]

    return pl.pallas_call(
        body, name=name, grid=(T // tm,),
        in_specs=[_row_spec(tm, D), _full_spec((1, D)), _full_spec((D, D_IN_PAD))],
        out_specs=[_row_spec(tm, n) for n in U_SPLITS] + [_row_spec(tm, D)],
        out_shape=[jax.ShapeDtypeStruct((T, n), F32) for n in U_SPLITS]
        + [jax.ShapeDtypeStruct((T, D), BF16)],
        compiler_params=_cp("parallel"),
    )(h, g, w_in)


DU_SPLITS = (256, 256, 768, 768, 256, 128)
DU_OFFS = (0, 256, 512, 1280, 2048, 2304)


def mixin_bwd(h, dh_in, g, w_in, dus, name):
    T = h.shape[0]
    tm = min(TOK_TILE, T)

    def body(h_ref, dhi_ref, g_ref, w_ref, *refs):
        dh_ref, du_ref, dn_ref = refs[-3:]

        @pl.when(pl.program_id(0) == 0)
        def _():
            dn_ref[...] = jnp.zeros_like(dn_ref)

        for ref, off, n in zip(refs[:-3], DU_OFFS, DU_SPLITS):
            du_ref[:, off:off + n] = ref[...].astype(BF16)
        dxn = _mm_nt(du_ref[...], w_ref[...])
        gg = g_ref[...]
        _, xhat, rstd = _rms(h_ref[...], gg)
        dh, dn = _rms_bwd(dxn, xhat, rstd, gg)
        dh_ref[...] = dhi_ref[...] + dh
        dn_ref[...] += dn

    return pl.pallas_call(
        body, name=name, grid=(T // tm,),
        in_specs=[_row_spec(tm, D), _row_spec(tm, D), _full_spec((1, D)), _full_spec((D, D_IN_PAD))]
        + [_row_spec(tm, n) for n in DU_SPLITS],
        out_specs=[_row_spec(tm, D), _row_spec(tm, D_IN_PAD), _full_spec((1, D))],
        out_shape=[jax.ShapeDtypeStruct((T, D), F32), jax.ShapeDtypeStruct((T, D_IN_PAD), BF16),
                   jax.ShapeDtypeStruct((1, D), F32)],
        compiler_params=_cp("arbitrary"),
    )(h, dh_in, g, w_in, *dus)


def _shift_down(x, s, row):
    if s == 0:
        return x
    return jnp.where(row >= s, pltpu.roll(x, s, 0), 0.0)


def _shift_up(x, s, row):
    if s == 0:
        return x
    n = x.shape[0]
    return jnp.where(row < n - s, pltpu.roll(x, n - s, 0), 0.0)


def conv_fwd(x, w, b, S, col0, C, name):
    T = x.shape[0]
    cb0 = col0 // LANE

    def body(x_ref, w_ref, b_ref, y_ref):
        xx = x_ref[...]
        row = _iota(xx.shape, 0)
        y = xx * w_ref[3:4, :] + b_ref[...]
        for k in range(3):
            y += _shift_down(xx, 3 - k, row) * w_ref[k:k + 1, :]
        y_ref[...] = y

    return pl.pallas_call(
        body, name=name, grid=(T // S, C // LANE),
        in_specs=[pl.BlockSpec((S, LANE), lambda s, c: (s, cb0 + c)),
                  pl.BlockSpec((4, LANE), lambda s, c: (0, c)),
                  pl.BlockSpec((1, LANE), lambda s, c: (0, c))],
        out_specs=pl.BlockSpec((S, LANE), lambda s, c: (s, c)),
        out_shape=jax.ShapeDtypeStruct((T, C), F32),
        compiler_params=_cp("parallel", "parallel"),
    )(x, w, b)


def conv_bwd(x, dy, w, S, col0, C, name):
    T = x.shape[0]
    cb0 = col0 // LANE

    def body(x_ref, dy_ref, w_ref, dx_ref, dwb_ref):
        @pl.when(pl.program_id(1) == 0)
        def _():
            dwb_ref[...] = jnp.zeros_like(dwb_ref)

        xx = x_ref[...]
        dd = dy_ref[...]
        row = _iota(xx.shape, 0)
        dx = dd * w_ref[3:4, :]
        for k in range(3):
            dx += _shift_up(dd, 3 - k, row) * w_ref[k:k + 1, :]
        dx_ref[...] = dx
        for k in range(4):
            dwb_ref[k:k + 1, :] += jnp.sum(dd * _shift_down(xx, 3 - k, row), axis=0, keepdims=True)
        dwb_ref[4:5, :] += jnp.sum(dd, axis=0, keepdims=True)

    return pl.pallas_call(
        body, name=name, grid=(C // LANE, T // S),
        in_specs=[pl.BlockSpec((S, LANE), lambda c, s: (s, cb0 + c)),
                  pl.BlockSpec((S, LANE), lambda c, s: (s, c)),
                  pl.BlockSpec((4, LANE), lambda c, s: (0, c))],
        out_specs=[pl.BlockSpec((S, LANE), lambda c, s: (s, c)),
                   pl.BlockSpec((8, LANE), lambda c, s: (0, c))],
        out_shape=[jax.ShapeDtypeStruct((T, C), F32), jax.ShapeDtypeStruct((8, C), F32)],
        compiler_params=_cp("parallel", "arbitrary"),
    )(x, dy, w)


def _scan(a, b, row):
    n = a.shape[0]
    d = 1
    while d < n:
        keep = row >= d
        b = a * jnp.where(keep, pltpu.roll(b, d, 0), 0.0) + b
        a = a * jnp.where(keep, pltpu.roll(a, d, 0), 1.0)
        d *= 2
    return b


def _rscan(a, b, row):
    n = a.shape[0]
    d = 1
    while d < n:
        keep = row < n - d
        b = a * jnp.where(keep, pltpu.roll(b, n - d, 0), 0.0) + b
        a = a * jnp.where(keep, pltpu.roll(a, n - d, 0), 1.0)
        d *= 2
    return b


GELU_C = math.sqrt(2.0 / math.pi)


def _gelu(x):
    t = jnp.tanh(GELU_C * (x + 0.044715 * (x * x * x)))
    return 0.5 * x * (1.0 + t), t


def _lru_gates(xr, wa, ba, wx, bx, lam):
    r = _sigmoid(_mm(xr, wa) + ba)
    i = _sigmoid(_mm(xr, wx) + bx)
    sp = jnp.maximum(-lam, 0.0) + jnp.log(1.0 + jnp.exp(-jnp.abs(lam)))
    la = -LRU_C * r * sp
    a = jnp.exp(la)
    e2 = a * a
    m = jnp.sqrt(-jnp.tanh(la) * (e2 + 1.0))
    return r, i, sp, a, e2, m


def lru_fwd(xr, u_lru, wa, wx, vec, S, name):
    T = xr.shape[0]

    def body(xr_ref, gt_ref, wa_ref, wx_ref, vec_ref, y_ref):
        x = xr_ref[...]
        row = _iota(x.shape, 0)
        r, i, sp, a, e2, m = _lru_gates(x, wa_ref[...], vec_ref[0:1, :], wx_ref[...], vec_ref[1:2, :],
                                        vec_ref[2:3, :])
        hh = _scan(a, m * (i * x), row)
        y_ref[...] = _gelu(gt_ref[...])[0] * hh

    return pl.pallas_call(
        body, name=name, grid=(T // S, LRU_W // LANE),
        in_specs=[pl.BlockSpec((S, LANE), lambda s, c: (s, c)),
                  pl.BlockSpec((S, LANE), lambda s, c: (s, 2 + c)),
                  pl.BlockSpec((LANE, LANE), lambda s, c: (c, c)),
                  pl.BlockSpec((LANE, LANE), lambda s, c: (c, c)),
                  pl.BlockSpec((8, LANE), lambda s, c: (0, c))],
        out_specs=pl.BlockSpec((S, LANE), lambda s, c: (s, c)),
        out_shape=jax.ShapeDtypeStruct((T, LRU_W), F32),
        compiler_params=_cp("parallel", "parallel"),
    )(xr, u_lru, wa, wx, vec)


def lru_bwd(xr, u_lru, dy, wa, wx, vec, S, name):
    T = xr.shape[0]

    def body(xr_ref, gt_ref, dy_ref, wa_ref, wx_ref, vec_ref,
             dxr_ref, dgt_ref, dwa_ref, dwx_ref, dvec_ref):
        @pl.when(pl.program_id(1) == 0)
        def _():
            dwa_ref[...] = jnp.zeros_like(dwa_ref)
            dwx_ref[...] = jnp.zeros_like(dwx_ref)
            dvec_ref[...] = jnp.zeros_like(dvec_ref)

        x = xr_ref[...]
        n = x.shape[0]
        row = _iota(x.shape, 0)
        lam = vec_ref[2:3, :]
        r, i, sp, a, e2, m = _lru_gates(x, wa_ref[...], vec_ref[0:1, :], wx_ref[...], vec_ref[1:2, :], lam)
        v = i * x
        hh = _scan(a, m * v, row)
        gt = gt_ref[...]
        dy = dy_ref[...]
        ge, t = _gelu(gt)
        dgt_ref[...] = dy * hh * (0.5 * (1.0 + t) + 0.5 * gt * (1.0 - t * t) * GELU_C
                                  * (1.0 + 3.0 * 0.044715 * gt * gt))
        a_next = jnp.where(row < n - 1, pltpu.roll(a, n - 1, 0), 0.0)
        G = _rscan(a_next, dy * ge, row)
        da = G * _shift_down(hh, 1, row)
        dv = G * m
        dla = da * a - (G * v) * e2 / m
        dr = dla * (-LRU_C * sp)
        dsp = jnp.sum(dla * (-LRU_C * r), axis=0, keepdims=True)
        dra = dr * r * (1.0 - r)
        dia = (dv * x) * i * (1.0 - i)
        dxr_ref[...] = dv * i + _mm_nt(dra, wa_ref[...]) + _mm_nt(dia, wx_ref[...])
        dwa_ref[0] += _mm_tn(x, dra)
        dwx_ref[0] += _mm_tn(x, dia)
        dvec_ref[0:1, :] += jnp.sum(dra, axis=0, keepdims=True)
        dvec_ref[1:2, :] += jnp.sum(dia, axis=0, keepdims=True)
        dvec_ref[2:3, :] += dsp * (-_sigmoid(-lam))

    return pl.pallas_call(
        body, name=name, grid=(LRU_W // LANE, T // S),
        in_specs=[pl.BlockSpec((S, LANE), lambda c, s: (s, c)),
                  pl.BlockSpec((S, LANE), lambda c, s: (s, 2 + c)),
                  pl.BlockSpec((S, LANE), lambda c, s: (s, c)),
                  pl.BlockSpec((LANE, LANE), lambda c, s: (c, c)),
                  pl.BlockSpec((LANE, LANE), lambda c, s: (c, c)),
                  pl.BlockSpec((8, LANE), lambda c, s: (0, c))],
        out_specs=[pl.BlockSpec((S, LANE), lambda c, s: (s, c)),
                   pl.BlockSpec((S, LANE), lambda c, s: (s, c)),
                   pl.BlockSpec((1, LANE, LANE), lambda c, s: (c, 0, 0)),
                   pl.BlockSpec((1, LANE, LANE), lambda c, s: (c, 0, 0)),
                   pl.BlockSpec((8, LANE), lambda c, s: (0, c))],
        out_shape=[jax.ShapeDtypeStruct((T, LRU_W), F32), jax.ShapeDtypeStruct((T, LRU_W), F32),
                   jax.ShapeDtypeStruct((2, LANE, LANE), F32), jax.ShapeDtypeStruct((2, LANE, LANE), F32),
                   jax.ShapeDtypeStruct((8, LRU_W), F32)],
        compiler_params=_cp("parallel", "arbitrary"),
    )(xr, u_lru, dy, wa, wx, vec)


def _bucket_table():
    qi = np.arange(BQ)[:, None]
    kj = np.arange(2 * BQ)[None, :]
    dist = BQ + qi - kj
    band = (dist >= 0) & (dist < WINDOW)
    dd = np.maximum(dist, 0)
    max_exact = REL_BUCKETS // 2
    large = max_exact + (np.log(np.maximum(dd, 1).astype(np.float32) / np.float32(max_exact))
                         / np.float32(math.log(REL_MAX_DIST / max_exact))
                         * np.float32(REL_BUCKETS - max_exact)).astype(np.int32)
    large = np.minimum(large, REL_BUCKETS - 1)
    bucket = np.where(dd < max_exact, dd, large)
    return np.where(band, bucket, -1).astype(np.int32)


def _att_specs(S):
    nb = S // BQ
    qc = ATT_W // LANE
    return [pl.BlockSpec((BQ, ATT_W), lambda b, n: (b * nb + n, 0)),
            pl.BlockSpec((BQ, KV_W), lambda b, n: (b * nb + jnp.maximum(n - 1, 0), qc)),
            pl.BlockSpec((BQ, KV_W), lambda b, n: (b * nb + n, qc)),
            pl.BlockSpec((BQ, KV_W), lambda b, n: (b * nb + jnp.maximum(n - 1, 0), qc + 1)),
            pl.BlockSpec((BQ, KV_W), lambda b, n: (b * nb + n, qc + 1))]


def _att_bias(bk, rb_ref, bias_s):
    for h in range(ATT_H):
        acc = jnp.zeros(bk.shape, F32)
        for bb in range(REL_BUCKETS):
            acc = jnp.where(bk == bb, rb_ref[bb * ATT_H + h], acc)
        bias_s[h] = acc


def _att_probs(qs, kgs, bias_s, valid, sk_ref):
    heads = range(ATT_H)
    s = [_mm_nt(qs[h], kgs[h // ATT_G]) for h in heads]
    s = [jnp.where(valid, s[h] * (HD ** -0.5) + bias_s[h], NEG) for h in heads]
    m = [jnp.maximum(jnp.max(s[h], axis=-1, keepdims=True), sk_ref[h]) for h in heads]
    e = [jnp.exp(s[h] - m[h]) for h in heads]
    es = [jnp.exp(sk_ref[h] - m[h]) for h in heads]
    den = [jnp.sum(e[h], axis=-1, keepdims=True) + es[h] for h in heads]
    return [e[h] / den[h] for h in heads], [es[h] / den[h] for h in heads]


def _att_kv(kp_ref, kc_ref, vp_ref, vc_ref):
    cat = lambda a, b, g: jnp.concatenate([a[:, HD * g:HD * (g + 1)], b[:, HD * g:HD * (g + 1)]], axis=0)
    return ([cat(kp_ref, kc_ref, g) for g in range(ATT_KV)], [cat(vp_ref, vc_ref, g) for g in range(ATT_KV)])


def attn_fwd(u_att, sinks, rel_bias, S, name):
    T = u_att.shape[0]
    nb = S // BQ
    table = jnp.asarray(_bucket_table())

    def body(sk_ref, rb_ref, bk_ref, q_ref, kp_ref, kc_ref, vp_ref, vc_ref, o_ref, bias_s):
        b = pl.program_id(0)
        n = pl.program_id(1)
        bk = bk_ref[...]

        @pl.when((b == 0) & (n == 0))
        def _():
            _att_bias(bk, rb_ref, bias_s)

        valid = (bk >= 0) & ((n > 0) | (_iota(bk.shape, 1) >= BQ))
        kgs, vgs = _att_kv(kp_ref, kc_ref, vp_ref, vc_ref)
        p, _ = _att_probs([q_ref[:, HD * h:HD * (h + 1)] for h in range(ATT_H)], kgs, bias_s, valid, sk_ref)
        outs = [_mm(p[h], vgs[h // ATT_G]) for h in range(ATT_H)]
        for h in range(ATT_H):
            o_ref[:, HD * h:HD * (h + 1)] = outs[h]

    smem = pl.BlockSpec(memory_space=pltpu.SMEM)
    return pl.pallas_call(
        body, name=name, grid=(T // S, nb),
        in_specs=[smem, smem, _full_spec((BQ, 2 * BQ))] + _att_specs(S),
        out_specs=pl.BlockSpec((BQ, ATT_W), lambda b, n: (b * nb + n, 0)),
        out_shape=jax.ShapeDtypeStruct((T, ATT_W), F32),
        scratch_shapes=[pltpu.VMEM((ATT_H, BQ, 2 * BQ), F32)],
        compiler_params=_cp("arbitrary", "arbitrary"),
    )(sinks, rel_bias, table, u_att, u_att, u_att, u_att, u_att)


def attn_bwd(u_att, dy, sinks, rel_bias, S, name):
    T = u_att.shape[0]
    nb = S // BQ
    nB = T // S
    table = jnp.asarray(_bucket_table())
    scale = HD ** -0.5

    def body(sk_ref, rb_ref, bk_ref, q_ref, kp_ref, kc_ref, vp_ref, vc_ref, dy_ref,
             du_ref, drel_ref, dsk_ref, bias_s, dbias_s):
        b = pl.program_id(0)
        n = pl.program_id(1)
        bk = bk_ref[...]

        @pl.when((b == 0) & (n == 0))
        def _():
            _att_bias(bk, rb_ref, bias_s)
            dbias_s[...] = jnp.zeros_like(dbias_s)
            dsk_ref[...] = jnp.zeros_like(dsk_ref)
            drel_ref[...] = jnp.zeros_like(drel_ref)

        @pl.when(n == 0)
        def _():
            du_ref[...] = jnp.zeros_like(du_ref)

        valid = (bk >= 0) & ((n > 0) | (_iota(bk.shape, 1) >= BQ))
        r_cur = pl.multiple_of(n * BQ, BQ)
        r_prev = pl.multiple_of(jnp.maximum(n - 1, 0) * BQ, BQ)
        heads = range(ATT_H)
        kgs, vgs = _att_kv(kp_ref, kc_ref, vp_ref, vc_ref)
        qs = [q_ref[:, HD * h:HD * (h + 1)] for h in heads]
        dos = [dy_ref[:, HD * h:HD * (h + 1)] for h in heads]
        p, ps = _att_probs(qs, kgs, bias_s, valid, sk_ref)
        dp = [_mm_nt(dos[h], vgs[h // ATT_G]) for h in heads]
        delta = [jnp.sum(p[h] * dp[h], axis=-1, keepdims=True) for h in heads]
        ds = [p[h] * (dp[h] - delta[h]) for h in heads]
        dss = [ds[h] * scale for h in heads]
        dq = [_mm(dss[h], kgs[h // ATT_G]) for h in heads]
        dks = [_mm_tn(dss[h], qs[h]) for h in heads]
        dvs = [_mm_tn(p[h], dos[h]) for h in heads]
        for h in heads:
            dbias_s[h] += ds[h]
            dsk_ref[h:h + 1, :] += jnp.broadcast_to(jnp.sum(-ps[h] * delta[h], axis=0, keepdims=True), (1, LANE))
            du_ref[pl.ds(r_cur, BQ), HD * h:HD * (h + 1)] = dq[h]
        for g in range(ATT_KV):
            of_group = range(g * ATT_G, (g + 1) * ATT_G)
            dk = functools.reduce(lambda x, y: x + y, [dks[h] for h in of_group])
            dv = functools.reduce(lambda x, y: x + y, [dvs[h] for h in of_group])
            ck = ATT_W + HD * g
            cv = ATT_W + KV_W + HD * g
            du_ref[pl.ds(r_prev, BQ), ck:ck + HD] += dk[0:BQ]
            du_ref[pl.ds(r_cur, BQ), ck:ck + HD] += dk[BQ:]
            du_ref[pl.ds(r_prev, BQ), cv:cv + HD] += dv[0:BQ]
            du_ref[pl.ds(r_cur, BQ), cv:cv + HD] += dv[BQ:]

        @pl.when((b == nB - 1) & (n == nb - 1))
        def _():
            lane = _iota((1, LANE), 1)
            for h in range(ATT_H):
                db = dbias_s[h]
                acc = jnp.zeros((1, LANE), F32)
                for bb in range(REL_BUCKETS):
                    val = jnp.sum(jnp.sum(jnp.where(bk == bb, db, 0.0), axis=1, keepdims=True),
                                  axis=0, keepdims=True)
                    acc = jnp.where(lane == bb, val, acc)
                drel_ref[h:h + 1, :] = acc

    smem = pl.BlockSpec(memory_space=pltpu.SMEM)
    return pl.pallas_call(
        body, name=name, grid=(nB, nb),
        in_specs=[smem, smem, _full_spec((BQ, 2 * BQ))] + _att_specs(S)
        + [pl.BlockSpec((BQ, ATT_W), lambda b, n: (b * nb + n, 0))],
        out_specs=[pl.BlockSpec((S, ATT_W + 2 * KV_W), lambda b, n: (b, 0)),
                   _full_spec((8, LANE)), _full_spec((8, LANE))],
        out_shape=[jax.ShapeDtypeStruct((T, ATT_W + 2 * KV_W), F32),
                   jax.ShapeDtypeStruct((8, LANE), F32), jax.ShapeDtypeStruct((8, LANE), F32)],
        scratch_shapes=[pltpu.VMEM((ATT_H, BQ, 2 * BQ), F32), pltpu.VMEM((ATT_H, BQ, 2 * BQ), F32)],
        compiler_params=_cp("arbitrary", "arbitrary"),
    )(sinks, rel_bias, table, u_att, u_att, u_att, u_att, u_att, dy)


def _head_of(i):
    return lax.shift_right_logical(i, 6)


def _head_mask(shape):
    return (_head_of(_iota(shape, 0)) == _head_of(_iota(shape, 1))).astype(F32)


def _dn_point(c, uba, alog, dtb):
    s = c * _sigmoid(c)
    qt, kt, vt = s[:, 0:256], s[:, 256:512], s[:, 512:768]
    ones_bd = _head_mask((DN_W, DN_W))
    q = qt * lax.rsqrt(_mm3(qt * qt, ones_bd) + EPS) * (HD ** -0.5)
    k = kt * lax.rsqrt(_mm3(kt * kt, ones_bd) + EPS)
    sel = _head_of(_iota((LANE, DN_W), 1))
    row = _iota((LANE, DN_W), 0)
    braw = _mm3(uba, (row == sel).astype(F32))
    araw = _mm3(uba, (row == sel + DN_H).astype(F32)) + dtb
    beta = _sigmoid(braw)
    g = -jnp.exp(alog) * (jnp.maximum(araw, 0.0) + jnp.log(1.0 + jnp.exp(-jnp.abs(araw))))
    return q, k, vt, g, beta


def dn_point_fwd(c, uba, alog, dtb, name):
    T = c.shape[0]
    tm = min(TOK_TILE, T)

    def body(c_ref, u_ref, al_ref, dt_ref, *outs):
        for ref, val in zip(outs, _dn_point(c_ref[...], u_ref[...], al_ref[...], dt_ref[...])):
            ref[...] = val

    return pl.pallas_call(
        body, name=name, grid=(T // tm,),
        in_specs=[_row_spec(tm, 768), _row_spec(tm, LANE), _full_spec((1, DN_W)), _full_spec((1, DN_W))],
        out_specs=[_row_spec(tm, DN_W)] * 5,
        out_shape=[jax.ShapeDtypeStruct((T, DN_W), F32)] * 5,
        compiler_params=_cp("parallel"),
    )(c, uba, alog, dtb)


def dn_point_bwd(c, uba, alog, dtb, douts, name):
    T = c.shape[0]
    tm = min(TOK_TILE, T)

    def body(c_ref, u_ref, al_ref, dt_ref, dq, dk, dv, dg, db, dc_ref, du_ref, dvec_ref):
        @pl.when(pl.program_id(0) == 0)
        def _():
            dvec_ref[...] = jnp.zeros_like(dvec_ref)

        _, vjp = jax.vjp(_dn_point, c_ref[...], u_ref[...], al_ref[...], dt_ref[...])
        dc, du, dal, ddt = vjp((dq[...], dk[...], dv[...], dg[...], db[...]))
        dc_ref[...] = dc
        du_ref[...] = du
        fold = (_iota((LANE, DN_W), 0) == _head_of(_iota((LANE, DN_W), 1))).astype(F32)
        both = jnp.concatenate([dal, ddt, jnp.zeros((6, DN_W), F32)], axis=0)
        dvec_ref[...] += _mmx_nt(both, fold)

    return pl.pallas_call(
        body, name=name, grid=(T // tm,),
        in_specs=[_row_spec(tm, 768), _row_spec(tm, LANE), _full_spec((1, DN_W)), _full_spec((1, DN_W))]
        + [_row_spec(tm, DN_W)] * 5,
        out_specs=[_row_spec(tm, 768), _row_spec(tm, LANE), _full_spec((8, LANE))],
        out_shape=[jax.ShapeDtypeStruct((T, 768), F32), jax.ShapeDtypeStruct((T, LANE), F32),
                   jax.ShapeDtypeStruct((8, LANE), F32)],
        compiler_params=_cp("arbitrary"),
    )(c, uba, alog, dtb, *douts)


def _unit_lower_inverses(lmats):
    eye = (_iota(lmats[0].shape, 0) == _iota(lmats[0].shape, 1)).astype(F32)
    tinvs = [eye - lm for lm in lmats]
    pws = list(lmats)
    for _ in range(5):
        pws = [_mm3(pw, pw) for pw in pws]
        tinvs = [t + _mm3(t, pw) for t, pw in zip(tinvs, pws)]
    return tuple(tinvs)


def _inverse_bwd(tinv, d):
    return -_mm3_nt(_mm3_tn(tinv, d), tinv)


@jax.custom_vjp
def _tri_invs(lmats):
    return _unit_lower_inverses(lmats)


def _tri_invs_fwd(lmats):
    tinvs = _unit_lower_inverses(lmats)
    return tinvs, tinvs


_tri_invs.defvjp(_tri_invs_fwd, lambda tinvs, ds: (tuple(_inverse_bwd(t, d) for t, d in zip(tinvs, ds)),))


@jax.custom_vjp
def _tri_inv_known(lmat, tinv):
    return tinv


_tri_inv_known.defvjp(lambda lmat, tinv: (tinv, tinv),
                      lambda tinv, d: (_inverse_bwd(tinv, d), jnp.zeros_like(tinv)))


DN_SUB = 4


def _dn_stack(x):
    return jnp.concatenate([x, x, x, x], axis=0) * _head_mask((DN_W, DN_W))


def _dn_pre_inverse(q, k, v, g, beta):
    hm = _head_mask((DN_W, DN_W))
    ri = _iota((DN_W, DN_W), 0) & (CHUNK - 1)
    ci = _iota((DN_W, DN_W), 1) & (CHUNK - 1)
    tri64 = (_iota((CHUNK, CHUNK), 0) >= _iota((CHUNK, CHUNK), 1)).astype(F32)
    gc = _mm3(tri64, g)
    ks = _dn_stack(k)
    gcol = jnp.sum(_dn_stack(gc), axis=1, keepdims=True) * (1.0 / HD)
    gmat = jnp.broadcast_to(gcol, (DN_W, DN_W))
    decay = jnp.exp(jnp.minimum(gmat - gmat.T, 0.0))
    lmat = _mm_nt(_dn_stack(k * beta), ks) * decay * (hm * (ri > ci).astype(F32))
    att = _mm_nt(_dn_stack(q), ks) * decay * (hm * (ri >= ci).astype(F32))
    return lmat, att, gc


def _dn_post_inverse(q, k, v, g, beta, tinv, att, gc):
    glast = jnp.sum(g, axis=0, keepdims=True)
    eg = jnp.exp(gc)
    u = _mm(tinv, _dn_stack(v * beta))
    w = _mm(tinv, _dn_stack(k * beta * eg))
    return u, w, att, _dn_stack(q * eg), _dn_stack(k * jnp.exp(glast - gc)), jnp.exp(glast), tinv


def _dn_apply(state, prep):
    u, w, att, qe, kd, eglast, _ = prep
    vn = u - _mm(w, state)
    o4 = _mm(qe, state) + _mm(att, vn)
    o = o4[0:64] + o4[64:128] + o4[128:192] + o4[192:256]
    return o, state * eglast + _mm_tn(kd, vn)


def _dn_chunks(states, q, k, v, g, beta, knowns=None):
    nb = len(q)
    n = q[0].shape[0] // CHUNK
    chunks = [[tuple(x[b][c * CHUNK:(c + 1) * CHUNK] for x in (q, k, v, g, beta)) for c in range(n)]
              for b in range(nb)]
    pre = [[_dn_pre_inverse(*ch) for ch in seq] for seq in chunks]
    lmats = [p[0] for seq in pre for p in seq]
    if knowns is None:
        flat = _tri_invs(tuple(lmats))
    else:
        flat = [_tri_inv_known(lm, kn) for lm, kn in zip(lmats, [kn for seq in knowns for kn in seq])]
    tinvs = [flat[b * n:(b + 1) * n] for b in range(nb)]
    preps = [[_dn_post_inverse(*chunks[b][c], tinvs[b][c], pre[b][c][1], pre[b][c][2]) for c in range(n)]
             for b in range(nb)]
    states = list(states)
    outs = [[] for _ in range(nb)]
    for c in range(n):
        for b in range(nb):
            o, states[b] = _dn_apply(states[b], preps[b][c])
            outs[b].append(o)
    return tuple(jnp.concatenate(o, axis=0) for o in outs), tuple(states), tinvs


def _dn_scan_specs(nb, S, reverse):
    rows = DN_SUB * CHUNK
    ns = S // rows
    at = (lambda t: ns - 1 - t) if reverse else (lambda t: t)
    return (pl.BlockSpec((nb, rows, DN_W), lambda t: (0, at(t), 0)),
            pl.BlockSpec((nb, 1, DN_W, DN_W), lambda t: (0, at(t), 0, 0)),
            pl.BlockSpec((nb, DN_SUB, DN_W, DN_W), lambda t: (0, at(t), 0, 0)))


def dn_scan_fwd(q, k, v, g, beta, S, name):
    T = q.shape[0]
    nb = T // S
    ns = S // (DN_SUB * CHUNK)
    seqs = range(nb)

    def body(q_ref, k_ref, v_ref, g_ref, b_ref, o_ref, st_ref, ti_ref, s_s):
        @pl.when(pl.program_id(0) == 0)
        def _():
            s_s[...] = jnp.zeros_like(s_s)

        per = lambda ref: tuple(ref[b] for b in seqs)
        sts = per(s_s)
        for b in seqs:
            st_ref[b, 0] = sts[b]
        outs, news, tinvs = _dn_chunks(sts, per(q_ref), per(k_ref), per(v_ref), per(g_ref), per(b_ref))
        for b in seqs:
            o_ref[b] = outs[b]
            s_s[b] = news[b]
            for c, tinv in enumerate(tinvs[b]):
                ti_ref[b, c] = tinv

    spec, st_spec, ti_spec = _dn_scan_specs(nb, S, False)
    o, states, tinvs = pl.pallas_call(
        body, name=name, grid=(ns,),
        in_specs=[spec] * 5,
        out_specs=[spec, st_spec, ti_spec],
        out_shape=[jax.ShapeDtypeStruct((nb, S, DN_W), F32),
                   jax.ShapeDtypeStruct((nb, ns, DN_W, DN_W), F32),
                   jax.ShapeDtypeStruct((nb, S // CHUNK, DN_W, DN_W), F32)],
        scratch_shapes=[pltpu.VMEM((nb, DN_W, DN_W), F32)],
        compiler_params=_cp("arbitrary"),
    )(*(t.reshape(nb, S, DN_W) for t in (q, k, v, g, beta)))
    return o.reshape(T, DN_W), states, tinvs


def dn_scan_bwd(q, k, v, g, beta, states, tinvs, do, S, name):
    T = q.shape[0]
    nb = T // S
    ns = S // (DN_SUB * CHUNK)
    seqs = range(nb)

    def body(q_ref, k_ref, v_ref, g_ref, b_ref, st_ref, ti_ref, do_ref, dq, dk, dv, dg, db, ds_s):
        @pl.when(pl.program_id(0) == 0)
        def _():
            ds_s[...] = jnp.zeros_like(ds_s)

        per = lambda ref: tuple(ref[b] for b in seqs)
        knowns = [[ti_ref[b, c] for c in range(DN_SUB)] for b in seqs]
        _, vjp = jax.vjp(lambda *args: _dn_chunks(*args, knowns=knowns)[:2],
                         tuple(st_ref[b, 0] for b in seqs), per(q_ref), per(k_ref), per(v_ref), per(g_ref),
                         per(b_ref))
        grads = vjp((per(do_ref), per(ds_s)))
        for b in seqs:
            ds_s[b] = grads[0][b]
            for ref, val in zip((dq, dk, dv, dg, db), grads[1:]):
                ref[b] = val[b]

    spec, st_spec, ti_spec = _dn_scan_specs(nb, S, True)
    res = pl.pallas_call(
        body, name=name, grid=(ns,),
        in_specs=[spec] * 5 + [st_spec, ti_spec, spec],
        out_specs=[spec] * 5,
        out_shape=[jax.ShapeDtypeStruct((nb, S, DN_W), F32)] * 5,
        scratch_shapes=[pltpu.VMEM((nb, DN_W, DN_W), F32)],
        compiler_params=_cp("arbitrary"),
    )(*(t.reshape(nb, S, DN_W) for t in (q, k, v, g, beta)), states, tinvs, do.reshape(nb, S, DN_W))
    return [r.reshape(T, DN_W) for r in res]


def _dn_gate(o, z, nl):
    ms = _mm3(o * o, _head_mask((DN_W, DN_W))) * (1.0 / HD)
    return o * lax.rsqrt(ms + EPS) * nl * (z * _sigmoid(z))


def dn_gate_fwd(o, u_dn, nl, name):
    T = o.shape[0]
    tm = min(TOK_TILE, T)

    def body(o_ref, z_ref, n_ref, y_ref):
        y_ref[...] = _dn_gate(o_ref[...], z_ref[...], n_ref[...])

    return pl.pallas_call(
        body, name=name, grid=(T // tm,),
        in_specs=[_row_spec(tm, DN_W), pl.BlockSpec((tm, DN_W), lambda i: (i, 3)), _full_spec((1, DN_W))],
        out_specs=_row_spec(tm, DN_W),
        out_shape=jax.ShapeDtypeStruct((T, DN_W), F32),
        compiler_params=_cp("parallel"),
    )(o, u_dn, nl)


def dn_gate_bwd(o, u_dn, nl, dy, name):
    T = o.shape[0]
    tm = min(TOK_TILE, T)

    def body(o_ref, z_ref, n_ref, dy_ref, do_ref, dz_ref, dn_ref):
        @pl.when(pl.program_id(0) == 0)
        def _():
            dn_ref[...] = jnp.zeros_like(dn_ref)

        _, vjp = jax.vjp(_dn_gate, o_ref[...], z_ref[...], n_ref[...])
        do, dz, dn = vjp(dy_ref[...])
        do_ref[...] = do
        dz_ref[...] = dz
        fold = (_iota((LANE, DN_W), 0) == (_iota((LANE, DN_W), 1) & (HD - 1))).astype(F32)
        dn_ref[...] += _mmx_nt(jnp.concatenate([dn, jnp.zeros((7, DN_W), F32)], axis=0), fold)

    return pl.pallas_call(
        body, name=name, grid=(T // tm,),
        in_specs=[_row_spec(tm, DN_W), pl.BlockSpec((tm, DN_W), lambda i: (i, 3)), _full_spec((1, DN_W)),
                  _row_spec(tm, DN_W)],
        out_specs=[_row_spec(tm, DN_W), _row_spec(tm, DN_W), _full_spec((8, LANE))],
        out_shape=[jax.ShapeDtypeStruct((T, DN_W), F32), jax.ShapeDtypeStruct((T, DN_W), F32),
                   jax.ShapeDtypeStruct((8, LANE), F32)],
        compiler_params=_cp("arbitrary"),
    )(o, u_dn, nl, dy)


Y_SPLITS = (LRU_W, ATT_W, DN_W)
Y_OFFS = (0, LRU_W, LRU_W + ATT_W)


ROWS_DEV = D // N_DEV


def _dev_rows_spec(row0):
    return pl.BlockSpec((N_DEV, ROWS_DEV, D), lambda *_: (0, row0 // ROWS_DEV, 0))


def _dev_rows(w_ref, off, n):
    return w_ref[off // ROWS_DEV:(off + n) // ROWS_DEV].reshape(n, D)


def wout_fwd(h, ys, wb, name):
    T = h.shape[0]
    tm = min(TOK_TILE, T)

    def body(h_ref, y0, y1, y2, w_ref, o_ref, yc_ref):
        for ref, off, n in zip((y0, y1, y2), Y_OFFS, Y_SPLITS):
            yc_ref[:, off:off + n] = ref[...].astype(BF16)
        o_ref[...] = h_ref[...] + _mm(yc_ref[...], _dev_rows(w_ref, 0, D))

    return pl.pallas_call(
        body, name=name, grid=(T // tm,),
        in_specs=[_row_spec(tm, D)] + [_row_spec(tm, n) for n in Y_SPLITS] + [_dev_rows_spec(R_WOUT)],
        out_specs=[_row_spec(tm, D), _row_spec(tm, D)],
        out_shape=[jax.ShapeDtypeStruct((T, D), F32), jax.ShapeDtypeStruct((T, D), BF16)],
        compiler_params=_cp("parallel"),
    )(h, *ys, wb)


def wout_bwd(dy, wb, name):
    T = dy.shape[0]
    tm = min(TOK_TILE, T)

    def body(dy_ref, w_ref, d0, d1, d2):
        dys = _mm_nt(dy_ref[...], _dev_rows(w_ref, 0, D))
        for ref, off, n in zip((d0, d1, d2), Y_OFFS, Y_SPLITS):
            ref[...] = dys[:, off:off + n]

    return pl.pallas_call(
        body, name=name, grid=(T // tm,),
        in_specs=[_row_spec(tm, D), _dev_rows_spec(R_WOUT)],
        out_specs=[_row_spec(tm, n) for n in Y_SPLITS],
        out_shape=[jax.ShapeDtypeStruct((T, n), F32) for n in Y_SPLITS],
        compiler_params=_cp("parallel"),
    )(dy, wb)


def ple_fwd(h, g, pe, wg, wp, name):
    T = h.shape[0]
    tm = min(TOK_TILE, T)

    def body(h_ref, g_ref, p_ref, wg_ref, wp_ref, o_ref):
        hh = h_ref[...]
        xn = _rms(hh, g_ref[...])[0]
        o_ref[...] = hh + _sigmoid(_mm(xn, _dev_rows(wg_ref, 0, D))) * _mm(p_ref[...], wp_ref[...])

    return pl.pallas_call(
        body, name=name, grid=(T // tm,),
        in_specs=[_row_spec(tm, D), _full_spec((1, D)), _row_spec(tm, PLE), _dev_rows_spec(R_PGATE),
                  _full_spec((PLE, D))],
        out_specs=_row_spec(tm, D),
        out_shape=jax.ShapeDtypeStruct((T, D), F32),
        compiler_params=_cp("parallel"),
    )(h, g, pe, wg, wp)


def ple_bwd(h, dy, g, pe, wg, wp, name):
    T = h.shape[0]
    tm = min(TOK_TILE, T)

    def body(h_ref, dy_ref, g_ref, p_ref, wg_ref, wp_ref, dh_ref, dz_ref, dpp_ref, xn_ref, dn_ref):
        @pl.when(pl.program_id(0) == 0)
        def _():
            dn_ref[...] = jnp.zeros_like(dn_ref)

        gg = g_ref[...]
        dy = dy_ref[...]
        xn, xhat, rstd = _rms(h_ref[...], gg)
        wg = _dev_rows(wg_ref, 0, D)
        gate = _sigmoid(_mm(xn, wg))
        pp = _mm(p_ref[...], wp_ref[...])
        dz = dy * pp * gate * (1.0 - gate)
        dz_ref[...] = dz.astype(BF16)
        dpp_ref[...] = (dy * gate).astype(BF16)
        xn_ref[...] = xn.astype(BF16)
        dh, dn = _rms_bwd(_mm_nt(dz, wg), xhat, rstd, gg)
        dh_ref[...] = dy + dh
        dn_ref[...] += dn

    return pl.pallas_call(
        body, name=name, grid=(T // tm,),
        in_specs=[_row_spec(tm, D), _row_spec(tm, D), _full_spec((1, D)), _row_spec(tm, PLE),
                  _dev_rows_spec(R_PGATE), _full_spec((PLE, D))],
        out_specs=[_row_spec(tm, D), _row_spec(tm, D), _row_spec(tm, D), _row_spec(tm, D), _full_spec((1, D))],
        out_shape=[jax.ShapeDtypeStruct((T, D), F32), jax.ShapeDtypeStruct((T, D), BF16),
                   jax.ShapeDtypeStruct((T, D), BF16), jax.ShapeDtypeStruct((T, D), BF16),
                   jax.ShapeDtypeStruct((1, D), F32)],
        compiler_params=_cp("arbitrary"),
    )(h, dy, g, pe, wg, wp)


def loss_head(h, g, target, name):
    T = h.shape[0]
    tm = min(TOK_TILE, T)

    def body(h_ref, g_ref, t_ref, loss_ref, dh_ref, dn_ref):
        @pl.when(pl.program_id(0) == 0)
        def _():
            dn_ref[...] = jnp.zeros_like(dn_ref)
            loss_ref[...] = jnp.zeros_like(loss_ref)

        gg = g_ref[...]
        y, xhat, rstd = _rms(h_ref[...], gg)
        err = y - t_ref[...]
        per_tok = jnp.mean(err * err, axis=-1, keepdims=True)
        loss_ref[...] += 0.5 * jnp.sum(per_tok, axis=0, keepdims=True)
        dh, dn = _rms_bwd(err * (1.0 / D), xhat, rstd, gg)
        dh_ref[...] = dh
        dn_ref[...] += dn

    return pl.pallas_call(
        body, name=name, grid=(T // tm,),
        in_specs=[_row_spec(tm, D), _full_spec((1, D)), _row_spec(tm, D)],
        out_specs=[_full_spec((8, LANE)), _row_spec(tm, D), _full_spec((1, D))],
        out_shape=[jax.ShapeDtypeStruct((8, LANE), F32), jax.ShapeDtypeStruct((T, D), F32),
                   jax.ShapeDtypeStruct((1, D), F32)],
        compiler_params=_cp("arbitrary"),
    )(h, g, target)


def _block_diag(w):
    return jnp.einsum('hij,hk->hikj', w, jnp.eye(4, dtype=w.dtype)).reshape(LRU_W, LRU_W)


def _layer_consts(W, l):
    row = lambda v: v.reshape(1, -1)
    zeros = jnp.zeros((5, LRU_W), F32)
    return dict(
        wa=_block_diag(W["lru_w_a"][l]), wx=_block_diag(W["lru_w_x"][l]),
        lru_vec=jnp.concatenate([row(W["lru_b_a"][l]), row(W["lru_b_x"][l]), row(W["lru_lambda"][l]), zeros], 0),
        lru_cb=row(W["lru_conv_b"][l]),
        sinks=W["attn_sinks"][l], rel=W["rel_bias"].reshape(-1),
        dn_cb=jnp.zeros((1, 3 * DN_W), F32),
        alog=row(jnp.repeat(W["dn_a_log"][l], HD)), dtb=row(jnp.repeat(W["dn_dt_bias"][l], HD)),
        dn_nl=row(jnp.tile(W["dn_norm"][l], DN_H)),
    )


def _layer_fwd(h0, pe, W, l, S, need=None):
    n = f"l{l}_"
    c_ = _layer_consts(W, l)
    row = lambda v: v.reshape(1, -1)
    need = need or (lambda *_: None)
    need(l, "f1", h0)
    h1, *ffn1_kept = ffn_fwd(h0, row(W["ffn1_norm"][l]), W["f1_cols"][l], W["f1_rows"][l], n + "ffn1_fwd")
    need(l, "in", h1)
    u_lru, u_att, u_dn, u_ba, xn_mix = mixin_fwd(h1, row(W["mix_norm"][l]), W["w_in"][l], n + "mixin_fwd")
    xr = conv_fwd(u_lru, W["lru_conv_w"][l], c_["lru_cb"], S, 0, LRU_W, n + "lru_conv_fwd")
    y_lru = lru_fwd(xr, u_lru, c_["wa"], c_["wx"], c_["lru_vec"], S, n + "lru_fwd")
    y_att = attn_fwd(u_att, c_["sinks"], c_["rel"], S, n + "attn_fwd")
    cc = conv_fwd(u_dn, W["dn_conv_w"][l], c_["dn_cb"], S, 0, 3 * DN_W, n + "dn_conv_fwd")
    q, k, v, g, beta = dn_point_fwd(cc, u_ba, c_["alog"], c_["dtb"], n + "dn_point_fwd")
    o, states, tinvs = dn_scan_fwd(q, k, v, g, beta, S, n + "dn_scan_fwd")
    y_dn = dn_gate_fwd(o, u_dn, c_["dn_nl"], n + "dn_gate_fwd")
    need(l, "rest", y_dn)
    h2, ycat = wout_fwd(h1, (y_lru, y_att, y_dn), W["r_rows"][l], n + "wout_fwd")
    h3, *ffn2_kept = ffn_fwd(h2, row(W["ffn2_norm"][l]), W["r_cols"][l], W["r_rows"][l], n + "ffn2_fwd")
    h4 = ple_fwd(h3, row(W["ple_norm"][l]), pe, W["r_rows"][l], W["ple_w_proj"][l], n + "ple_fwd")
    saved = dict(ffn1=ffn1_kept, ffn2=ffn2_kept, h0=h0, h1=h1, h2=h2, h3=h3, u_lru=u_lru, u_att=u_att, u_dn=u_dn,
                 u_ba=u_ba, xn_mix=xn_mix, xr=xr, cc=cc, q=q, k=k, v=v, g=g, beta=beta, o=o, states=states, tinvs=tinvs, ycat=ycat)
    return h4, saved


GM_WOUT, GM_PGATE, GM_WIN, GM_PPROJ, GM_END, GM_ROWS = 0, 128, 256, 560, 592, 640


def _layer_bwd(dh4, sv, pe, W, l, S, token=None, on_piece=None):
    n = f"l{l}_"
    c_ = _layer_consts(W, l)
    row = lambda v: v.reshape(1, -1)
    behind = lambda v, tok: v if tok is None else v + tok.astype(v.dtype)
    on_piece = on_piece or (lambda *_: None)
    G = {"mix_rows": jnp.zeros((N_DEV, GM_ROWS, D), BF16)}
    dh3, dz, dpp, xn_p, dn = ple_bwd(sv["h3"], dh4, behind(row(W["ple_norm"][l]), token), pe, W["r_rows"][l],
                                     W["ple_w_proj"][l], n + "ple_bwd")
    G["ple_norm"] = dn[0]
    G["mix_rows"] = grad_rows(xn_p, dz, G["mix_rows"], GM_PGATE, ROWS_DEV, n + "d_ple_w_gate")
    d_proj = matmul_tn(pe, dpp, n + "d_ple_w_proj")
    d_proj = d_proj.reshape(PLE, N_DEV, D // N_DEV).transpose(1, 0, 2).reshape(N_DEV, GM_END - GM_PPROJ, D)
    G["mix_rows"] = lax.dynamic_update_slice(G["mix_rows"], d_proj, (0, GM_PPROJ, 0))

    def ffn_back(which, cols_w, rows_w, h_in, dy, tok, one_by_one):
        gt, up, xn = sv[which]
        dh, dgt, dup, act, dn_ = ffn_bwd(h_in, dy, behind(row(W[which + "_norm"][l]), tok), gt, up, cols_w, rows_w,
                                         n + which + "_bwd")
        G[which + "_norm"] = dn_[0]
        zeros_rows = jnp.zeros((N_DEV, SHP, D), BF16)
        if one_by_one:
            G[which + "_gate"] = grad_cols(xn, dgt, lax.empty((1, D, FFP), BF16), 0, n + "d_" + which + "_w_gate")
            tok = on_piece(l, which + "_gate", (G[which + "_gate"],))
            G[which + "_up"] = grad_cols(xn, dup, behind(jnp.zeros((1, D, FFP), BF16), tok), 0,
                                         n + "d_" + which + "_w_up")
            tok = on_piece(l, which + "_up", (G[which + "_up"],))
            G[which + "_down"] = grad_rows(act, dy, behind(zeros_rows, tok), 0, SHP, n + "d_" + which + "_w_down",
                                           scale=0.5)
            return dh, on_piece(l, which + "_down", (G[which + "_down"],))
        cols = grad_cols(xn, dgt, lax.empty((2, D, FFP), BF16), 0, n + "d_" + which + "_w_gate")
        G[which + "_cols"] = grad_cols(xn, dup, cols, 1, n + "d_" + which + "_w_up")
        G[which + "_rows"] = grad_rows(act, dy, lax.empty((N_DEV, SHP, D), BF16), 0, SHP,
                                       n + "d_" + which + "_w_down", scale=0.5)
        return dh, on_piece(l, which, (G[which + "_cols"], G[which + "_rows"]))

    dh2, tok = ffn_back("ffn2", W["r_cols"][l], W["r_rows"][l], sv["h2"], dh3, None, False)
    dy_lru, dy_att, dy_dn = wout_bwd(dh2, W["r_rows"][l], n + "wout_bwd")
    G["mix_rows"] = grad_rows(sv["ycat"], dh2, G["mix_rows"], GM_WOUT, ROWS_DEV, n + "d_w_out")
    do, dz_dn, dnn = dn_gate_bwd(sv["o"], sv["u_dn"], behind(c_["dn_nl"], tok), dy_dn, n + "dn_gate_bwd")
    dqkvgb = dn_scan_bwd(sv["q"], sv["k"], sv["v"], sv["g"], sv["beta"], sv["states"], sv["tinvs"], do, S,
                         n + "dn_scan_bwd")
    dcc, du_ba, dvec_dn = dn_point_bwd(sv["cc"], sv["u_ba"], c_["alog"], c_["dtb"], dqkvgb, n + "dn_point_bwd")
    dqkv, dwb_dn = conv_bwd(sv["u_dn"], dcc, W["dn_conv_w"][l], S, 0, 3 * DN_W, n + "dn_conv_bwd")
    G["dn_norm"] = dnn[0, 0:HD]
    G["dn_a_log"] = dvec_dn[0, 0:DN_H]
    G["dn_dt_bias"] = dvec_dn[1, 0:DN_H]
    G["dn_conv_w"] = dwb_dn[0:4]
    du_att, drel, dsk = attn_bwd(sv["u_att"], dy_att, c_["sinks"], c_["rel"], S, n + "attn_bwd")
    G["attn_sinks"] = dsk[:, 0]
    G["rel_bias"] = drel[:, 0:REL_BUCKETS].T
    dxr, dgt_lru, dwa, dwx, dvec = lru_bwd(sv["xr"], sv["u_lru"], dy_lru, c_["wa"], c_["wx"], c_["lru_vec"], S,
                                           n + "lru_bwd")
    dx_lru, dwb_lru = conv_bwd(sv["u_lru"], dxr, W["lru_conv_w"][l], S, 0, LRU_W, n + "lru_conv_bwd")
    diag = lambda m: jnp.stack([m[c, HD * e:HD * (e + 1), HD * e:HD * (e + 1)] for c in range(2) for e in range(2)])
    G["lru_w_a"], G["lru_w_x"] = diag(dwa), diag(dwx)
    G["lru_b_a"], G["lru_b_x"], G["lru_lambda"] = dvec[0], dvec[1], dvec[2]
    G["lru_conv_w"], G["lru_conv_b"] = dwb_lru[0:4], dwb_lru[4]
    dh1, du_cat, dn = mixin_bwd(sv["h1"], dh2, row(W["mix_norm"][l]), W["w_in"][l],
                                (dx_lru, dgt_lru, du_att, dqkv, dz_dn, du_ba), n + "mixin_bwd")
    G["mix_norm"] = dn[0]
    d_in = matmul_tn(sv["xn_mix"], du_cat, n + "d_w_in")[:, :D_IN]
    d_in = d_in.reshape(D, N_DEV, D_IN // N_DEV).transpose(1, 0, 2).reshape(N_DEV, WIN_ROWS, D)
    d_in = jnp.pad(d_in, ((0, 0), (0, GM_PPROJ - GM_WIN - WIN_ROWS), (0, 0)))
    G["mix_rows"] = lax.dynamic_update_slice(G["mix_rows"], d_in, (0, GM_WIN, 0))
    tok = on_piece(l, "mix", (G["mix_rows"],))
    dh0, tok = ffn_back("ffn1", W["f1_cols"][l], W["f1_rows"][l], sv["h0"], dh1, tok, l == 0)
    return dh0, G, tok


def _core(x, pe, W, target, S, need=None, on_piece=None):
    h = x
    saved = []
    for l in range(DEPTH):
        h, sv = _layer_fwd(h, pe[l], W, l, S, need)
        saved.append(sv)
    loss_tile, dh, dfn = loss_head(h, W["final_norm"].reshape(1, -1), target, "loss_head")
    grads = [None] * DEPTH
    token = None
    for l in reversed(range(DEPTH)):
        dh, grads[l], token = _layer_bwd(dh, saved[l], pe[l], W, l, S, token, on_piece)
    return loss_tile[0, 0], dh, grads, dfn[0]


MESH_ID = pl.DeviceIdType.MESH
ANY_SPEC = pl.BlockSpec(memory_space=pl.ANY)
AXES = ("x", "y", "c")


def _my_pos():
    return lax.axis_index("x"), lax.axis_index("y"), lax.axis_index("c")


def _slot_of(px, py, pc):
    return 4 * px + 2 * py + pc


def all_gather(x, name):
    R, C = x.shape

    def body(x_ref, out_ref, send_sems, recv_sems, local_sem):
        mx, my, mc = _my_pos()
        me, sibling = (mx, my, mc), (mx, my, 1 - mc)
        chips = [(1 - mx, my), (mx, 1 - my), (1 - mx, 1 - my)]

        def copy(k, block, to, src=None):
            dst = out_ref.at[_slot_of(*block)]
            return pltpu.make_async_remote_copy(
                src_ref=dst if src is None else src, dst_ref=dst,
                send_sem=send_sems.at[k], recv_sem=recv_sems.at[k],
                device_id=to, device_id_type=MESH_ID)

        mine = pltpu.make_async_copy(x_ref, out_ref.at[_slot_of(*me)], local_sem)
        mine.start()
        first = [copy(0, me, sibling, src=x_ref)]
        first += [copy(1 + j, me, (*chip, mc), src=x_ref) for j, chip in enumerate(chips)]
        for cp in first:
            cp.start()
        passed = [copy(4 + j, (*chip, mc), sibling) for j, chip in enumerate(chips)]
        for j, chip in enumerate(chips):
            copy(1 + j, (*chip, mc), me).wait_recv()
            passed[j].start()
        copy(0, sibling, me).wait_recv()
        for j, chip in enumerate(chips):
            copy(4 + j, (*chip, 1 - mc), me).wait_recv()
        for cp in first + passed:
            cp.wait_send()
        mine.wait()

    return pl.pallas_call(
        body, name=name,
        out_shape=jax.ShapeDtypeStruct((N_DEV, R, C), x.dtype),
        in_specs=[ANY_SPEC], out_specs=ANY_SPEC,
        scratch_shapes=[pltpu.SemaphoreType.DMA((7,)), pltpu.SemaphoreType.DMA((7,)), pltpu.SemaphoreType.DMA],
    )(x)


def _col_window(ref, slot):
    return ref.at[:, pl.ds(pl.multiple_of(slot * SHP, LANE), SHP)]


def gather_layer(a_sh, b_sh, name):
    def body(a_ref, b_ref, ao_ref, bo_ref, send_sems, recv_sems, local_sems):
        mx, my, mc = _my_pos()
        me, sibling = (mx, my, mc), (mx, my, 1 - mc)
        chips = [(1 - mx, my), (mx, 1 - my), (1 - mx, 1 - my)]

        def copies(k, block, to, own=False):
            slot = _slot_of(*block)
            dsts = (_col_window(ao_ref, slot), bo_ref.at[slot])
            srcs = (a_ref, b_ref) if own else dsts
            return [pltpu.make_async_remote_copy(
                src_ref=s, dst_ref=d, send_sem=send_sems.at[2 * k + i], recv_sem=recv_sems.at[2 * k + i],
                device_id=to, device_id_type=MESH_ID) for i, (s, d) in enumerate(zip(srcs, dsts))]

        mine = [pltpu.make_async_copy(a_ref, _col_window(ao_ref, _slot_of(*me)), local_sems.at[0]),
                pltpu.make_async_copy(b_ref, bo_ref.at[_slot_of(*me)], local_sems.at[1])]
        for cp in mine:
            cp.start()
        first = copies(0, me, sibling, own=True)
        for j, chip in enumerate(chips):
            first += copies(1 + j, me, (*chip, mc), own=True)
        for cp in first:
            cp.start()
        passed = []
        for j, chip in enumerate(chips):
            for cp in copies(1 + j, (*chip, mc), me):
                cp.wait_recv()
            fwd = copies(4 + j, (*chip, mc), sibling)
            for cp in fwd:
                cp.start()
            passed += fwd
        for cp in copies(0, sibling, me):
            cp.wait_recv()
        for j, chip in enumerate(chips):
            for cp in copies(4 + j, (*chip, 1 - mc), me):
                cp.wait_recv()
        for cp in first + passed:
            cp.wait_send()
        for cp in mine:
            cp.wait()

    return pl.pallas_call(
        body, name=name,
        out_shape=[jax.ShapeDtypeStruct((a_sh.shape[0], FFP), a_sh.dtype),
                   jax.ShapeDtypeStruct((N_DEV,) + b_sh.shape, b_sh.dtype)],
        in_specs=[ANY_SPEC, ANY_SPEC], out_specs=[ANY_SPEC, ANY_SPEC],
        scratch_shapes=[pltpu.SemaphoreType.DMA((14,)), pltpu.SemaphoreType.DMA((14,)),
                        pltpu.SemaphoreType.DMA((2,))],
    )(a_sh, b_sh)


HBM_SPEC = pl.BlockSpec(memory_space=pltpu.HBM)
SEM_SPEC = pl.BlockSpec(memory_space=pltpu.SEMAPHORE)
SPLIT_EFFECT = pltpu.CompilerParams(has_side_effects=pltpu.SideEffectType.DATAFLOW_SIDE_EFFECTING)


def _split_ends(mode, src_ref, dst_ref, src_slot, dst_slot):
    cols = mode.endswith("cols")
    if mode.startswith("gather"):
        return src_ref, (_col_window(dst_ref, dst_slot) if cols else dst_ref.at[dst_slot])
    return (_col_window(src_ref, src_slot) if cols else src_ref.at[src_slot]), dst_ref.at[dst_slot]


def _split_peers():
    mx, my, mc = _my_pos()
    for r in range(1, N_DEV):
        peer = (1 - mx if r & 4 else mx, 1 - my if r & 2 else my, 1 - mc if r & 1 else mc)
        yield r - 1, peer, _slot_of(*peer)


def split_start(modes, srcs, dsts, name, after=()):
    n = len(modes)
    m = len(after)

    def body(*refs):
        send_sems, recv_sems, token = refs[2 * n + m], refs[2 * n + m + 1], refs[-1]
        mine = _slot_of(*_my_pos())
        for k, peer, ps in _split_peers():
            for i in range(n):
                src, dst = _split_ends(modes[i], refs[i], refs[n + i], ps, mine)
                pltpu.make_async_remote_copy(
                    src_ref=src, dst_ref=dst, send_sem=send_sems.at[n * k + i], recv_sem=recv_sems.at[n * k + i],
                    device_id=peer, device_id_type=MESH_ID).start()
        for i in range(n):
            src, dst = _split_ends(modes[i], refs[i], refs[n + i], mine, mine)
            pltpu.make_async_copy(src, dst, recv_sems.at[n * (N_DEV - 1) + i]).start()
        token[...] = jnp.zeros_like(token)

    bufs = tuple(srcs) + tuple(dsts)
    sems = pltpu.SemaphoreType.DMA((n * N_DEV,))
    res = pl.pallas_call(
        body, name=name,
        out_shape=(sems, sems) + tuple(pltpu.HBM(t.shape, t.dtype) for t in bufs)
        + (jax.ShapeDtypeStruct((8, LANE), F32),),
        in_specs=[HBM_SPEC] * (2 * n) + [ANY_SPEC] * m,
        out_specs=(SEM_SPEC, SEM_SPEC) + (HBM_SPEC,) * (2 * n) + (pl.BlockSpec(memory_space=pltpu.VMEM),),
        input_output_aliases={i: 2 + i for i in range(2 * n)},
        compiler_params=SPLIT_EFFECT,
    )(*(pltpu.with_memory_space_constraint(t, pltpu.HBM) for t in bufs), *after)
    return list(res[:-1]), res[-1]


def split_wait(modes, started, after, name):
    n = len(modes)
    after = tuple(after) if isinstance(after, (tuple, list)) else (after,)
    send_sems, recv_sems, bufs = started[0], started[1], started[2:]

    def body(*refs):
        send_sems, recv_sems = refs[2 * n], refs[2 * n + 1]
        mine = _slot_of(*_my_pos())
        for k, peer, ps in _split_peers():
            for i in range(n):
                sent = _split_ends(modes[i], refs[i], refs[n + i], ps, mine)[0]
                landed = _split_ends(modes[i], refs[i], refs[n + i], mine, ps)[1]
                cp = pltpu.make_async_remote_copy(
                    src_ref=sent, dst_ref=landed, send_sem=send_sems.at[n * k + i],
                    recv_sem=recv_sems.at[n * k + i], device_id=peer, device_id_type=MESH_ID)
                cp.wait_send()
                cp.wait_recv()
        for i in range(n):
            src, dst = _split_ends(modes[i], refs[i], refs[n + i], mine, mine)
            pltpu.make_async_copy(src, dst, recv_sems.at[n * (N_DEV - 1) + i]).wait()

    res = pl.pallas_call(
        body, name=name,
        out_shape=tuple(pltpu.HBM(t.shape, t.dtype) for t in bufs),
        in_specs=[HBM_SPEC] * (2 * n) + [SEM_SPEC, SEM_SPEC] + [ANY_SPEC] * len(after),
        out_specs=(HBM_SPEC,) * (2 * n),
        input_output_aliases={i: i for i in range(2 * n)},
        compiler_params=SPLIT_EFFECT,
    )(*bufs, send_sems, recv_sems, *after)
    return list(res[n:])


def sum_parts(parts, name):
    _, R, C = parts.shape
    tr = _pick(R, (512, 336, 272, 256, 128, 64, 32, 16, 8))

    def body(p_ref, o_ref):
        acc = p_ref[0].astype(F32)
        for k in range(1, N_DEV):
            acc += p_ref[k].astype(F32)
        o_ref[...] = acc

    return pl.pallas_call(
        body, name=name, grid=(R // tr,),
        in_specs=[pl.BlockSpec((N_DEV, tr, C), lambda i: (0, i, 0))],
        out_specs=pl.BlockSpec((tr, C), lambda i: (i, 0)),
        out_shape=jax.ShapeDtypeStruct((R, C), F32),
        compiler_params=_cp("parallel"),
    )(parts)


def sum_into(parts, row0, rows, cols, layer, dst, name):
    tr = _pick(rows, (512, 352, 128))
    blk0 = row0 // tr

    def body(p_ref, *rest):
        o_ref = rest[-1]
        acc = p_ref[0, :, 0:cols].astype(F32)
        for k in range(1, N_DEV):
            acc += p_ref[k, :, 0:cols].astype(F32)
        o_ref[0] = acc

    aliased = dst is not None
    return pl.pallas_call(
        body, name=name, grid=(rows // tr,),
        in_specs=[pl.BlockSpec((N_DEV, tr, parts.shape[2]), lambda i: (0, blk0 + i, 0))]
        + [pl.BlockSpec(memory_space=pl.ANY)] * aliased,
        out_specs=pl.BlockSpec((1, tr, cols), lambda i: (layer, i, 0)),
        out_shape=jax.ShapeDtypeStruct((DEPTH, rows, cols), F32),
        input_output_aliases={1: 0} if aliased else {},
        compiler_params=_cp("parallel"),
    )(*((parts, dst) if aliased else (parts,)))


def adamw(g, w, m, v, name):
    lead, (R, C) = g.shape[:-2], g.shape[-2:]
    tr = _pick(R, (512, 352, 256, 128, 64, 32, 16, 8))
    c1 = 1.0 - ADAM_B1 ** ADAM_STEP
    c2 = 1.0 - ADAM_B2 ** ADAM_STEP

    def body(g_ref, w_ref, m_ref, v_ref, d_ref, nm_ref, nv_ref):
        gg = g_ref[...]
        mm = ADAM_B1 * m_ref[...] + (1.0 - ADAM_B1) * gg
        vv = ADAM_B2 * v_ref[...] + (1.0 - ADAM_B2) * (gg * gg)
        nm_ref[...] = mm
        nv_ref[...] = vv
        d_ref[...] = -ADAM_LR * ((mm / c1) / (jnp.sqrt(vv / c2) + ADAM_EPS) + ADAM_WD * w_ref[...])

    if lead:
        spec = pl.BlockSpec((1, tr, C), lambda l, i: (l, i, 0))
    else:
        spec = pl.BlockSpec((tr, C), lambda l, i: (i, 0))
    return pl.pallas_call(
        body, name=name, grid=(lead[0] if lead else 1, R // tr),
        in_specs=[spec] * 4, out_specs=[spec] * 3,
        out_shape=[jax.ShapeDtypeStruct(g.shape, F32)] * 3,
        compiler_params=_cp("parallel", "parallel"),
    )(g, w, m, v)


BIG = (("ffn1_w_gate", 1, D, FF), ("ffn1_w_up", 1, D, FF), ("ffn1_w_down", 0, FF, D),
       ("w_in", 1, D, D_IN), ("w_out", 0, D, D),
       ("ffn2_w_gate", 1, D, FF), ("ffn2_w_up", 1, D, FF), ("ffn2_w_down", 0, FF, D),
       ("ple_w_gate", 0, D, D), ("ple_w_proj", 1, PLE, D))
SMALL = (("ffn1_norm", (D,), None), ("mix_norm", (D,), None), ("lru_conv_w", (4, LRU_W), LRU_W // N_DEV),
         ("lru_conv_b", (LRU_W,), None), ("lru_w_a", (4, HD, HD), None), ("lru_b_a", (LRU_W,), None),
         ("lru_w_x", (4, HD, HD), None), ("lru_b_x", (LRU_W,), None), ("lru_lambda", (LRU_W,), None),
         ("attn_sinks", (ATT_H,), None), ("dn_conv_w", (4, 3 * DN_W), 3 * DN_W // N_DEV),
         ("dn_a_log", (DN_H,), None), ("dn_dt_bias", (DN_H,), None), ("dn_norm", (HD,), None),
         ("ffn2_norm", (D,), None), ("ple_norm", (D,), None))
SINGLE = (("rel_bias", (REL_BUCKETS, ATT_H)), ("final_norm", (D,)))


def _pack_rows(arrs, width, mult):
    flat = jnp.concatenate([a.reshape(-1) for a in arrs])
    rows = -(-flat.shape[0] // (width * mult)) * mult
    return jnp.pad(flat, (0, rows * width - flat.shape[0])).reshape(rows, width)


def _unpack_rows(packed, shapes):
    flat = packed.reshape(-1)
    out, off = [], 0
    for s in shapes:
        n = int(np.prod(s))
        out.append(flat[off:off + n].reshape(s))
        off += n
    return out


def _pad_rows(w, r):
    return jnp.pad(w, ((0, r - w.shape[0]), (0, 0)))


def _shard_ffn(a, l, which, more=()):
    cols = jnp.concatenate([a[which + "_w_gate"][l], a[which + "_w_up"][l]], axis=0)
    rows = jnp.concatenate([_pad_rows(a[which + "_w_down"][l], SHP)] + list(more), axis=0)
    return jnp.pad(cols, ((0, 0), (0, SHP - SH))).astype(BF16), rows.astype(BF16)


def _shards(a, l):
    w_in_rows = _pad_rows(a["w_in"][l].reshape(WIN_ROWS, D), IN_ROWS).astype(BF16)
    rest = _shard_ffn(a, l, "ffn2", (a["w_out"][l], a["ple_w_gate"][l], a["ple_w_proj"][l].reshape(-1, D)))
    return _shard_ffn(a, l, "ffn1"), (w_in_rows,), rest


def _full_w_in(in_rows):
    sh = in_rows[:, :WIN_ROWS, :].reshape(N_DEV, D, D_IN // N_DEV)
    return jnp.pad(sh.transpose(1, 0, 2).reshape(D, D_IN), ((0, 0), (0, D_IN_PAD - D_IN)))


def _full_ple_proj(r_rows):
    sh = r_rows[:, R_PPROJ:R_ROWS, :].reshape(N_DEV, PLE, D // N_DEV)
    return sh.transpose(1, 0, 2).reshape(PLE, D)


PIECE_NAMES = {"ffn1": ("ffn1_w_gate", "ffn1_w_up", "ffn1_w_down"), "ffn2": ("ffn2_w_gate", "ffn2_w_up", "ffn2_w_down"),
               "mix": ("w_out", "ple_w_gate", "w_in", "ple_w_proj")}


def _shard_grads(piece, summed):
    if piece == "mix":
        rows, = summed
        return {"w_out": rows[GM_WOUT:GM_WOUT + ROWS_DEV], "ple_w_gate": rows[GM_PGATE:GM_PGATE + ROWS_DEV],
                "w_in": rows[GM_WIN:GM_WIN + WIN_ROWS].reshape(D, D_IN // N_DEV),
                "ple_w_proj": rows[GM_PPROJ:GM_END].reshape(PLE, D // N_DEV)}
    if piece in ("ffn1_gate", "ffn1_up"):
        return {piece.replace("_", "_w_"): summed[0][:, :SH]}
    if piece == "ffn1_down":
        return {"ffn1_w_down": summed[0][:SH]}
    cols, rows = summed
    return {piece + "_w_gate": cols[:D, :SH], piece + "_w_up": cols[D:, :SH], piece + "_w_down": rows[:SH]}


def kernel(x, p, ffn1_norm, ffn1_w_gate, ffn1_w_up, ffn1_w_down, mix_norm, w_in, lru_conv_w, lru_conv_b, lru_w_a, lru_b_a, lru_w_x, lru_b_x, lru_lambda, attn_sinks, rel_bias, dn_conv_w, dn_a_log, dn_dt_bias, dn_norm, w_out, ffn2_norm, ffn2_w_gate, ffn2_w_up, ffn2_w_down, ple_norm, ple_w_gate, ple_w_proj, final_norm, loss_target, m_ffn1_norm, m_ffn1_w_gate, m_ffn1_w_up, m_ffn1_w_down, m_mix_norm, m_w_in, m_lru_conv_w, m_lru_conv_b, m_lru_w_a, m_lru_b_a, m_lru_w_x, m_lru_b_x, m_lru_lambda, m_attn_sinks, m_rel_bias, m_dn_conv_w, m_dn_a_log, m_dn_dt_bias, m_dn_norm, m_w_out, m_ffn2_norm, m_ffn2_w_gate, m_ffn2_w_up, m_ffn2_w_down, m_ple_norm, m_ple_w_gate, m_ple_w_proj, m_final_norm, v_ffn1_norm, v_ffn1_w_gate, v_ffn1_w_up, v_ffn1_w_down, v_mix_norm, v_w_in, v_lru_conv_w, v_lru_conv_b, v_lru_w_a, v_lru_b_a, v_lru_w_x, v_lru_b_x, v_lru_lambda, v_attn_sinks, v_rel_bias, v_dn_conv_w, v_dn_a_log, v_dn_dt_bias, v_dn_norm, v_w_out, v_ffn2_norm, v_ffn2_w_gate, v_ffn2_w_up, v_ffn2_w_down, v_ple_norm, v_ple_w_gate, v_ple_w_proj, v_final_norm):
    a = dict(locals())
    nb, S, _ = x.shape
    T = nb * S
    my_slot = _slot_of(*_my_pos())

    W = {k: [None] * DEPTH for k in ("f1_cols", "f1_rows", "w_in", "r_cols", "r_rows", "ple_w_proj")}
    GATHER, SCATTER = ("gather_cols", "gather_block"), ("scatter_cols", "scatter_block")
    GROUP_MODES = {"f1": GATHER, "in": GATHER[1:], "rest": GATHER}

    def set_group(l, group, bufs):
        if group == "f1":
            W["f1_cols"][l], W["f1_rows"][l] = bufs
        elif group == "in":
            W["w_in"][l] = _full_w_in(bufs[0])
        else:
            W["r_cols"][l], W["r_rows"][l] = bufs
            W["ple_w_proj"][l] = _full_ple_proj(bufs[1])

    def landing(mode, src):
        if mode == "gather_cols":
            return lax.empty((src.shape[0], FFP), src.dtype)
        if mode == "scatter_cols":
            return lax.empty((N_DEV, src.shape[0], SHP), src.dtype)
        return lax.empty((N_DEV,) + src.shape[mode == "scatter_block":], src.dtype)

    def start(modes, srcs, name, after=()):
        return split_start(modes, srcs, [landing(m, s) for m, s in zip(modes, srcs)], name, after)

    shards0, shards1 = _shards(a, 0), _shards(a, 1)
    set_group(0, "f1", gather_layer(*shards0[0], "gather_weights_l0_ffn1"))
    taps = all_gather(_pack_rows([lru_conv_w, dn_conv_w], LANE, 8), "gather_conv_taps")
    flat_taps = taps.reshape(N_DEV, -1)
    for name, first, tap in (("lru_conv_w", 0, lru_conv_w), ("dn_conv_w", lru_conv_w.size, dn_conv_w)):
        per_dev = flat_taps[:, first:first + tap.size].reshape((N_DEV,) + tap.shape)
        W[name] = jnp.moveaxis(per_dev, 0, -2).reshape(tap.shape[:-1] + (N_DEV * tap.shape[-1],))
    for name, _, cols in SMALL:
        if cols is None:
            W[name] = a[name]
    W["rel_bias"], W["final_norm"] = rel_bias, final_norm

    gathers, after = {}, (W["f1_rows"][0], taps)
    for l, group, srcs in ((0, "in", shards0[1]), (0, "rest", shards0[2]),
                           (1, "f1", shards1[0]), (1, "in", shards1[1]), (1, "rest", shards1[2])):
        gathers[l, group], token = start(GROUP_MODES[group], srcs, f"gather_start_l{l}_{group}", after)
        after = (token,)
    W["ffn1_norm"] = ffn1_norm + token[0, 0]
    flight, tokens = {}, {}

    def need(l, group, h):
        if (l, group) in gathers:
            set_group(l, group, split_wait(GROUP_MODES[group], gathers[l, group], h, f"gather_wait_l{l}_{group}"))

    def piece_modes(piece):
        return {"mix": SCATTER[1:], "ffn1_gate": SCATTER[:1], "ffn1_up": SCATTER[:1], "ffn1_down": SCATTER[1:]}.get(
            piece, SCATTER)

    def on_piece(l, piece, bufs):
        bufs = [b.reshape(-1, FFP) if b.shape[-1] == FFP else b for b in bufs]
        flight[l, piece], tokens[l, piece] = start(piece_modes(piece), bufs, f"exchange_start_l{l}_{piece}")
        return tokens[l, piece][0, 0]

    loss_local, dx, grads, d_final = _core(x.reshape(T, D), p.reshape(DEPTH, T, PLE), W,
                                           loss_target.reshape(T, D), S, need, on_piece)
    loss = lax.psum(loss_local, AXES)

    small_full = [jnp.stack([grads[l][name] for l in range(DEPTH)]) for name, _, _ in SMALL]
    small_full += [grads[0]["rel_bias"] + grads[1]["rel_bias"], d_final]
    small_flight, _ = start(("gather_block",), (_pack_rows(small_full, LANE, 8),), "gather_start_small_grads",
                            (tokens[0, "ffn1_down"],))

    out = {}

    landed = {}

    def land(l, piece, after):
        landed[l, piece] = split_wait(piece_modes(piece), flight[l, piece], after, f"exchange_wait_l{l}_{piece}")

    def where(name, l):
        if name in ("w_out", "ple_w_gate"):
            return "mix", 0, (GM_WOUT if name == "w_out" else GM_PGATE), ROWS_DEV, D
        ffn, kind = name[:4], name[7:]
        one_by_one = (l, ffn) == (0, "ffn1")
        if kind == "down":
            return (ffn + "_down", 0, 0, SH, D) if one_by_one else (ffn, 1, 0, SH, D)
        if one_by_one:
            return f"{ffn}_{kind}", 0, 0, D, SH
        return ffn, 0, (0 if kind == "gate" else D), D, SH

    def update(piece):
        for name in PIECE_NAMES[piece]:
            if name in ("w_in", "ple_w_proj"):
                g = jnp.stack([_shard_grads("mix", [mix_sums[l]])[name] for l in range(DEPTH)])
            else:
                g = None
                for l in reversed(range(DEPTH)):
                    piece_l, idx, row0, rows, cols = where(name, l)
                    g = sum_into(landed[l, piece_l][idx], row0, rows, cols, l, g, f"sum_{name}_l{l}")
            out[name] = (g,) + tuple(adamw(g, a[name], a["m_" + name], a["v_" + name], "adamw_" + name))

    for l, piece in ((1, "ffn2"), (1, "mix"), (1, "ffn1"), (0, "ffn2"), (0, "mix")):
        land(l, piece, (dx, tokens[0, "ffn1_down"]))
    mix_sums = [sum_parts(landed[l, "mix"][0], f"sum_mix_grads_l{l}") for l in range(DEPTH)]
    update("ffn2")
    update("mix")
    done_early = tuple(out[n][1] for n in PIECE_NAMES["ffn2"] + PIECE_NAMES["mix"])
    small_parts, = split_wait(("gather_block",), small_flight, done_early, "gather_wait_small_grads")
    small_sum = sum_parts(small_parts, "sum_small_grads")
    g_small = dict(zip([n for n, _, _ in SMALL] + [n for n, _ in SINGLE],
                       _unpack_rows(small_sum, [s.shape for s in small_full])))
    for name, _, cols in SMALL:
        if cols is not None:
            g_small[name] = lax.dynamic_slice_in_dim(g_small[name], my_slot * cols, cols, axis=2)

    for n in [n for n, _, _ in SMALL] + [n for n, _ in SINGLE]:
        shape = a[n].shape
        flat = lambda t: t.reshape((-1, shape[-1]) if len(shape) > 1 else (1, -1))
        res = adamw(flat(g_small[n]), flat(a[n]), flat(a["m_" + n]), flat(a["v_" + n]), "adamw_" + n)
        out[n] = (g_small[n].reshape(shape),) + tuple(r.reshape(shape) for r in res)

    for piece in ("ffn1_gate", "ffn1_up", "ffn1_down"):
        land(0, piece, (out["final_norm"][1],) + done_early)
    update("ffn1")

    order = ['ffn1_norm', 'ffn1_w_gate', 'ffn1_w_up', 'ffn1_w_down', 'mix_norm', 'w_in', 'lru_conv_w', 'lru_conv_b',
             'lru_w_a', 'lru_b_a', 'lru_w_x', 'lru_b_x', 'lru_lambda', 'attn_sinks', 'rel_bias', 'dn_conv_w',
             'dn_a_log', 'dn_dt_bias', 'dn_norm', 'w_out', 'ffn2_norm', 'ffn2_w_gate', 'ffn2_w_up', 'ffn2_w_down',
             'ple_norm', 'ple_w_gate', 'ple_w_proj', 'final_norm']
    return (loss, dx.reshape(x.shape)) + tuple(out[n][k] for k in range(4) for n in order)
```

```python
import functools
import math

import numpy as np
import jax
import jax.numpy as jnp
from jax import lax
from jax.experimental import pallas as pl
from jax.experimental.pallas import tpu as pltpu

F32 = jnp.float32
BF16 = jnp.bfloat16
HI = lax.Precision.HIGHEST

D = 1024
DEPTH = 2
EPS = 1e-6
PLE = 256
FF = 2816
HD = 64
LRU_W = 256
LRU_C = 8.0
ATT_W = 512
ATT_H = 8
ATT_KV = 2
ATT_G = 4
KV_W = 128
WINDOW = 128
BQ = 128
REL_BUCKETS = 32
REL_MAX_DIST = 128
DN_W = 256
DN_H = 4
CHUNK = 64
D_IN = 2312
D_IN_PAD = 2432
N_DEV = 8

ADAM_LR = 0.001
ADAM_B1 = 0.9
ADAM_B2 = 0.999
ADAM_EPS = 1e-08
ADAM_WD = 0.01
ADAM_STEP = 10

LANE = 128
VMEM_LIMIT = 56 * 1024 * 1024
SH = FF // N_DEV
SHP = 384
FFP = N_DEV * SHP
FF_TILE = 2 * SHP
FF_SUB = 256
TOK_TILE = 512
LIGHT_TILE = 1024
R_DOWN2, R_WOUT, R_PGATE, R_PPROJ, R_ROWS = 0, 384, 512, 640, 672
WIN_ROWS = D * D_IN // N_DEV // 1024
IN_ROWS = 304
NEG = -1e30


def _cp(*sem):
    return pltpu.CompilerParams(dimension_semantics=tuple(sem), vmem_limit_bytes=VMEM_LIMIT)


def _dg(a, b, ca, cb, exact):
    dims = (((ca,), (cb,)), ((), ()))
    if exact == "f32":
        return lax.dot_general(a.astype(F32), b.astype(F32), dims, precision=HI, preferred_element_type=F32)
    if exact == "split":
        a_hi, b_hi = a.astype(BF16), b.astype(BF16)
        a_lo = (a - a_hi.astype(F32)).astype(BF16)
        b_lo = (b - b_hi.astype(F32)).astype(BF16)
        dot = lambda u, v: lax.dot_general(u, v, dims, preferred_element_type=F32)
        return dot(a_hi, b_hi) + (dot(a_hi, b_lo) + dot(a_lo, b_hi))
    return lax.dot_general(a.astype(BF16), b.astype(BF16), dims, preferred_element_type=F32)


def _make_mm(exact):
    @jax.custom_vjp
    def mm(a, b):
        return _dg(a, b, 1, 0, exact)

    @jax.custom_vjp
    def mm_nt(a, b):
        return _dg(a, b, 1, 1, exact)

    @jax.custom_vjp
    def mm_tn(a, b):
        return _dg(a, b, 0, 0, exact)

    mm.defvjp(lambda a, b: (mm(a, b), (a, b)),
              lambda r, d: (mm_nt(d, r[1]), mm_tn(r[0], d)))
    mm_nt.defvjp(lambda a, b: (mm_nt(a, b), (a, b)),
                 lambda r, d: (mm(d, r[1]), mm_tn(d, r[0])))
    mm_tn.defvjp(lambda a, b: (mm_tn(a, b), (a, b)),
                 lambda r, d: (mm_nt(r[1], d), mm(r[0], d)))
    return mm, mm_nt, mm_tn


_mm, _mm_nt, _mm_tn = _make_mm("bf16")
_mmx, _mmx_nt, _mmx_tn = _make_mm("f32")
_mm3, _mm3_nt, _mm3_tn = _make_mm("split")


def _iota(shape, dim):
    return lax.broadcasted_iota(jnp.int32, shape, dim)


def _sigmoid(x):
    return 0.5 * jnp.tanh(0.5 * x) + 0.5


def _rms(h, g):
    rstd = lax.rsqrt(jnp.mean(h * h, axis=-1, keepdims=True) + EPS)
    xhat = h * rstd
    return xhat * g, xhat, rstd


def _rms_bwd(dxn, xhat, rstd, g):
    dxhat = dxn * g
    dh = rstd * (dxhat - xhat * jnp.mean(dxhat * xhat, axis=-1, keepdims=True))
    dg = jnp.sum(dxn * xhat, axis=0, keepdims=True)
    return dh, dg


def _row_spec(tm, n):
    return pl.BlockSpec((tm, n), lambda i, *_: (i, 0))


def _full_spec(shape):
    nd = len(shape)
    return pl.BlockSpec(shape, lambda *_: (0,) * nd)


def _ffn_weight_specs():
    return [pl.BlockSpec((D, FF_TILE), lambda i, j: (0, j)),
            pl.BlockSpec((D, FF_TILE), lambda i, j: (1, j)),
            pl.BlockSpec((2, SHP, D), lambda i, j: (j, 0, 0))]


def ffn_fwd(h, g, wa, wb, name):
    T = h.shape[0]
    tm = min(2 * TOK_TILE, T)
    nj = FFP // FF_TILE

    def body(h_ref, g_ref, wg_ref, wu_ref, wd_ref, o_ref, gt_ref, up_ref, xn_ref):
        j = pl.program_id(1)

        @pl.when(j == 0)
        def _():
            hh = h_ref[...]
            xn_ref[...] = _rms(hh, g_ref[...])[0].astype(BF16)
            o_ref[...] = hh

        blocks = [slice(c, c + FF_SUB) for c in range(0, FF_TILE, FF_SUB)]
        xn = xn_ref[...]
        wd = wd_ref[...].reshape(FF_TILE, D)
        gt = [_mm(xn, wg_ref[:, c]) for c in blocks]
        up = [_mm(xn, wu_ref[:, c]) for c in blocks]
        act = [t * _sigmoid(t) * u for t, u in zip(gt, up)]
        down = [_mm(act[k], wd[c]) for k, c in enumerate(blocks)]
        for k, c in enumerate(blocks):
            gt_ref[:, c] = gt[k].astype(BF16)
            up_ref[:, c] = up[k].astype(BF16)
        o_ref[...] += 0.5 * functools.reduce(lambda x, y: x + y, down)

    tile = pl.BlockSpec((tm, FF_TILE), lambda i, j: (i, j))
    return pl.pallas_call(
        body, name=name, grid=(T // tm, nj),
        in_specs=[pl.BlockSpec((tm, D), lambda i, j: (i, 0)),
                  pl.BlockSpec((1, D), lambda i, j: (0, 0))] + _ffn_weight_specs(),
        out_specs=[pl.BlockSpec((tm, D), lambda i, j: (i, 0)), tile, tile,
                   pl.BlockSpec((tm, D), lambda i, j: (i, 0))],
        out_shape=[jax.ShapeDtypeStruct((T, D), F32), jax.ShapeDtypeStruct((T, FFP), BF16),
                   jax.ShapeDtypeStruct((T, FFP), BF16), jax.ShapeDtypeStruct((T, D), BF16)],
        compiler_params=_cp("parallel", "arbitrary"),
    )(h, g, wa, wa, wb)


def ffn_bwd(h, dy, g, gt_saved, up_saved, wa, wb, name):
    T = h.shape[0]
    tm = min(TOK_TILE, T)
    nj = FFP // FF_TILE

    def body(h_ref, dy_ref, g_ref, gt_ref, up_ref, wg_ref, wu_ref, wd_ref,
             dh_ref, dg_ref, du_ref, a_ref, dn_ref, dxn_s, dyh_s):
        i = pl.program_id(0)
        j = pl.program_id(1)

        @pl.when(j == 0)
        def _():
            dxn_s[...] = jnp.zeros_like(dxn_s)
            dyh_s[...] = (0.5 * dy_ref[...]).astype(BF16)

        @pl.when((i == 0) & (j == 0))
        def _():
            dn_ref[...] = jnp.zeros_like(dn_ref)

        blocks = [slice(c, c + FF_SUB) for c in range(0, FF_TILE, FF_SUB)]
        wd = wd_ref[...].reshape(FF_TILE, D)
        dyh = dyh_s[...]
        gt = [gt_ref[:, c].astype(F32) for c in blocks]
        up = [up_ref[:, c].astype(F32) for c in blocks]
        da = [_mm_nt(dyh, wd[c]) for c in blocks]
        sg = [_sigmoid(t) for t in gt]
        si = [t * s for t, s in zip(gt, sg)]
        dup = [d * s for d, s in zip(da, si)]
        dgt = [d * u * (s * (1.0 + t * (1.0 - s))) for d, u, s, t in zip(da, up, sg, gt)]
        dxn = [_mm_nt(dgt[k], wg_ref[:, c]) + _mm_nt(dup[k], wu_ref[:, c]) for k, c in enumerate(blocks)]
        for k, c in enumerate(blocks):
            dg_ref[:, c] = dgt[k].astype(BF16)
            du_ref[:, c] = dup[k].astype(BF16)
            a_ref[:, c] = (si[k] * up[k]).astype(BF16)
        dxn_s[...] += functools.reduce(lambda x, y: x + y, dxn)

        @pl.when(j == nj - 1)
        def _():
            gg = g_ref[...]
            _, xhat, rstd = _rms(h_ref[...], gg)
            dh, dn = _rms_bwd(dxn_s[...], xhat, rstd, gg)
            dh_ref[...] = dy_ref[...] + dh
            dn_ref[...] += dn

    tile = pl.BlockSpec((tm, FF_TILE), lambda i, j: (i, j))
    return pl.pallas_call(
        body, name=name, grid=(T // tm, nj),
        in_specs=[pl.BlockSpec((tm, D), lambda i, j: (i, 0)),
                  pl.BlockSpec((tm, D), lambda i, j: (i, 0)),
                  pl.BlockSpec((1, D), lambda i, j: (0, 0)), tile, tile] + _ffn_weight_specs(),
        out_specs=[pl.BlockSpec((tm, D), lambda i, j: (i, 0)), tile, tile, tile,
                   pl.BlockSpec((1, D), lambda i, j: (0, 0))],
        out_shape=[jax.ShapeDtypeStruct((T, D), F32)] + [jax.ShapeDtypeStruct((T, FFP), BF16)] * 3
        + [jax.ShapeDtypeStruct((1, D), F32)],
        scratch_shapes=[pltpu.VMEM((tm, D), F32), pltpu.VMEM((tm, D), BF16)],
        compiler_params=_cp("arbitrary", "arbitrary"),
    )(h, dy, g, gt_saved, up_saved, wa, wa, wb)


def _pick(n, prefs):
    for t in prefs:
        if n % t == 0:
            return t
    return n


def _tn_body(nk, scale, out_dtype, squeeze):
    def body(a_ref, b_ref, *rest):
        o_ref, acc = rest[-2], rest[-1]
        k = pl.program_id(2)

        @pl.when(k == 0)
        def _():
            acc[...] = jnp.zeros_like(acc)

        acc[...] += _mm_tn(a_ref[...], b_ref[...])

        @pl.when(k == nk - 1)
        def _():
            res = (scale * acc[...]).astype(out_dtype)
            if squeeze:
                o_ref[0] = res
            else:
                o_ref[...] = res

    return body


def matmul_tn(a, b, name, scale=1.0, out_dtype=BF16):
    T, M = a.shape
    N = b.shape[1]
    tmm = _pick(M, (512, 256))
    tnn = _pick(N, (1024, 2432))
    tk = min(2 * TOK_TILE, T)
    nk = T // tk
    return pl.pallas_call(
        _tn_body(nk, scale, out_dtype, False), name=name, grid=(M // tmm, N // tnn, nk),
        in_specs=[pl.BlockSpec((tk, tmm), lambda i, j, k: (k, i)),
                  pl.BlockSpec((tk, tnn), lambda i, j, k: (k, j))],
        out_specs=pl.BlockSpec((tmm, tnn), lambda i, j, k: (i, j)),
        out_shape=jax.ShapeDtypeStruct((M, N), out_dtype),
        scratch_shapes=[pltpu.VMEM((tmm, tnn), F32)],
        compiler_params=_cp("parallel", "parallel", "arbitrary"),
    )(a, b)


def grad_cols(a, b, dst, slot, name):
    T = a.shape[0]
    tmm, tnn = D, FFP // 2
    tk = min(2 * TOK_TILE, T)
    nk = T // tk
    return pl.pallas_call(
        _tn_body(nk, 1.0, BF16, True), name=name, grid=(D // tmm, FFP // tnn, nk),
        in_specs=[pl.BlockSpec((tk, tmm), lambda i, j, k: (k, i)),
                  pl.BlockSpec((tk, tnn), lambda i, j, k: (k, j)),
                  pl.BlockSpec(memory_space=pl.ANY)],
        out_specs=pl.BlockSpec((1, tmm, tnn), lambda i, j, k: (slot, i, j)),
        out_shape=jax.ShapeDtypeStruct(dst.shape, dst.dtype),
        scratch_shapes=[pltpu.VMEM((tmm, tnn), F32)],
        input_output_aliases={2: 0},
        compiler_params=_cp("parallel", "parallel", "arbitrary"),
    )(a, b, dst)


def grad_rows(a, b, dst, row0, rows, name, scale=1.0):
    T = a.shape[0]
    tk = min(2 * TOK_TILE, T)
    nk = T // tk
    blk = row0 // rows

    def body(a_ref, b_ref, dst_ref, o_ref, acc):
        k = pl.program_id(0)

        @pl.when(k == 0)
        def _():
            acc[...] = jnp.zeros_like(acc)

        acc[...] += _mm_tn(a_ref[...], b_ref[...])

        @pl.when(k == nk - 1)
        def _():
            o_ref[...] = (scale * acc[...]).astype(BF16).reshape(N_DEV, rows, D)

    return pl.pallas_call(
        body, name=name, grid=(nk,),
        in_specs=[pl.BlockSpec((tk, N_DEV * rows), lambda k: (k, 0)),
                  pl.BlockSpec((tk, D), lambda k: (k, 0)),
                  pl.BlockSpec(memory_space=pl.ANY)],
        out_specs=pl.BlockSpec((N_DEV, rows, D), lambda k: (0, blk, 0)),
        out_shape=jax.ShapeDtypeStruct(dst.shape, dst.dtype),
        scratch_shapes=[pltpu.VMEM((N_DEV * rows, D), F32)],
        input_output_aliases={2: 0},
        compiler_params=_cp("arbitrary"),
    )(a, b, dst)


U_SPLITS = (512, 768, 1024, 128)
U_OFFS = (0, 512, 1280, 2304)


def mixin_fwd(h, g, w_in, name):
    T = h.shape[0]
    tm = min(TOK_TILE, T)

    def body(h_ref, g_ref, w_ref, u0, u1, u2, u3, xn_ref):
        xn = _rms(h_ref[...], g_ref[...])[0].astype(BF16)
        xn_ref[...] = xn
        u = _mm(xn, w_ref[...])
        for ref, off, n in zip((u0, u1, u2, u3), U_OFFS, U_SPLITS):
            ref[...] = u[:, off:off + n]

    return pl.pallas_call(
        body, name=name, grid=(T // tm,),
        in_specs=[_row_spec(tm, D), _full_spec((1, D)), _full_spec((D, D_IN_PAD))],
        out_specs=[_row_spec(tm, n) for n in U_SPLITS] + [_row_spec(tm, D)],
        out_shape=[jax.ShapeDtypeStruct((T, n), F32) for n in U_SPLITS]
        + [jax.ShapeDtypeStruct((T, D), BF16)],
        compiler_params=_cp("parallel"),
    )(h, g, w_in)


DU_SPLITS = (256, 256, 768, 768, 256, 128)
DU_OFFS = (0, 256, 512, 1280, 2048, 2304)


def mixin_bwd(h, dh_in, g, w_in, dus, name):
    T = h.shape[0]
    tm = min(TOK_TILE, T)

    def body(h_ref, dhi_ref, g_ref, w_ref, *refs):
        dh_ref, du_ref, dn_ref = refs[-3:]

        @pl.when(pl.program_id(0) == 0)
        def _():
            dn_ref[...] = jnp.zeros_like(dn_ref)

        for ref, off, n in zip(refs[:-3], DU_OFFS, DU_SPLITS):
            du_ref[:, off:off + n] = ref[...].astype(BF16)
        dxn = _mm_nt(du_ref[...], w_ref[...])
        gg = g_ref[...]
        _, xhat, rstd = _rms(h_ref[...], gg)
        dh, dn = _rms_bwd(dxn, xhat, rstd, gg)
        dh_ref[...] = dhi_ref[...] + dh
        dn_ref[...] += dn

    return pl.pallas_call(
        body, name=name, grid=(T // tm,),
        in_specs=[_row_spec(tm, D), _row_spec(tm, D), _full_spec((1, D)), _full_spec((D, D_IN_PAD))]
        + [_row_spec(tm, n) for n in DU_SPLITS],
        out_specs=[_row_spec(tm, D), _row_spec(tm, D_IN_PAD), _full_spec((1, D))],
        out_shape=[jax.ShapeDtypeStruct((T, D), F32), jax.ShapeDtypeStruct((T, D_IN_PAD), BF16),
                   jax.ShapeDtypeStruct((1, D), F32)],
        compiler_params=_cp("arbitrary"),
    )(h, dh_in, g, w_in, *dus)


def _shift_down(x, s, row):
    if s == 0:
        return x
    return jnp.where(row >= s, pltpu.roll(x, s, 0), 0.0)


def _shift_up(x, s, row):
    if s == 0:
        return x
    n = x.shape[0]
    return jnp.where(row < n - s, pltpu.roll(x, n - s, 0), 0.0)


def conv_fwd(x, w, b, S, col0, C, name):
    T = x.shape[0]
    cb0 = col0 // LANE

    def body(x_ref, w_ref, b_ref, y_ref):
        xx = x_ref[...]
        row = _iota(xx.shape, 0)
        y = xx * w_ref[3:4, :] + b_ref[...]
        for k in range(3):
            y += _shift_down(xx, 3 - k, row) * w_ref[k:k + 1, :]
        y_ref[...] = y

    return pl.pallas_call(
        body, name=name, grid=(T // S, C // LANE),
        in_specs=[pl.BlockSpec((S, LANE), lambda s, c: (s, cb0 + c)),
                  pl.BlockSpec((4, LANE), lambda s, c: (0, c)),
                  pl.BlockSpec((1, LANE), lambda s, c: (0, c))],
        out_specs=pl.BlockSpec((S, LANE), lambda s, c: (s, c)),
        out_shape=jax.ShapeDtypeStruct((T, C), F32),
        compiler_params=_cp("parallel", "parallel"),
    )(x, w, b)


def conv_bwd(x, dy, w, S, col0, C, name):
    T = x.shape[0]
    cb0 = col0 // LANE

    def body(x_ref, dy_ref, w_ref, dx_ref, dwb_ref):
        @pl.when(pl.program_id(1) == 0)
        def _():
            dwb_ref[...] = jnp.zeros_like(dwb_ref)

        xx = x_ref[...]
        dd = dy_ref[...]
        row = _iota(xx.shape, 0)
        dx = dd * w_ref[3:4, :]
        for k in range(3):
            dx += _shift_up(dd, 3 - k, row) * w_ref[k:k + 1, :]
        dx_ref[...] = dx
        for k in range(4):
            dwb_ref[k:k + 1, :] += jnp.sum(dd * _shift_down(xx, 3 - k, row), axis=0, keepdims=True)
        dwb_ref[4:5, :] += jnp.sum(dd, axis=0, keepdims=True)

    return pl.pallas_call(
        body, name=name, grid=(C // LANE, T // S),
        in_specs=[pl.BlockSpec((S, LANE), lambda c, s: (s, cb0 + c)),
                  pl.BlockSpec((S, LANE), lambda c, s: (s, c)),
                  pl.BlockSpec((4, LANE), lambda c, s: (0, c))],
        out_specs=[pl.BlockSpec((S, LANE), lambda c, s: (s, c)),
                   pl.BlockSpec((8, LANE), lambda c, s: (0, c))],
        out_shape=[jax.ShapeDtypeStruct((T, C), F32), jax.ShapeDtypeStruct((8, C), F32)],
        compiler_params=_cp("parallel", "arbitrary"),
    )(x, dy, w)


def _scan(a, b, row):
    n = a.shape[0]
    d = 1
    while d < n:
        keep = row >= d
        b = a * jnp.where(keep, pltpu.roll(b, d, 0), 0.0) + b
        a = a * jnp.where(keep, pltpu.roll(a, d, 0), 1.0)
        d *= 2
    return b


def _rscan(a, b, row):
    n = a.shape[0]
    d = 1
    while d < n:
        keep = row < n - d
        b = a * jnp.where(keep, pltpu.roll(b, n - d, 0), 0.0) + b
        a = a * jnp.where(keep, pltpu.roll(a, n - d, 0), 1.0)
        d *= 2
    return b


GELU_C = math.sqrt(2.0 / math.pi)


def _gelu(x):
    t = jnp.tanh(GELU_C * (x + 0.044715 * (x * x * x)))
    return 0.5 * x * (1.0 + t), t


def _lru_gates(xr, wa, ba, wx, bx, lam):
    r = _sigmoid(_mm(xr, wa) + ba)
    i = _sigmoid(_mm(xr, wx) + bx)
    sp = jnp.maximum(-lam, 0.0) + jnp.log(1.0 + jnp.exp(-jnp.abs(lam)))
    la = -LRU_C * r * sp
    a = jnp.exp(la)
    e2 = a * a
    m = jnp.sqrt(-jnp.tanh(la) * (e2 + 1.0))
    return r, i, sp, a, e2, m


def lru_fwd(xr, u_lru, wa, wx, vec, S, name):
    T = xr.shape[0]

    def body(xr_ref, gt_ref, wa_ref, wx_ref, vec_ref, y_ref):
        x = xr_ref[...]
        row = _iota(x.shape, 0)
        r, i, sp, a, e2, m = _lru_gates(x, wa_ref[...], vec_ref[0:1, :], wx_ref[...], vec_ref[1:2, :],
                                        vec_ref[2:3, :])
        hh = _scan(a, m * (i * x), row)
        y_ref[...] = _gelu(gt_ref[...])[0] * hh

    return pl.pallas_call(
        body, name=name, grid=(T // S, LRU_W // LANE),
        in_specs=[pl.BlockSpec((S, LANE), lambda s, c: (s, c)),
                  pl.BlockSpec((S, LANE), lambda s, c: (s, 2 + c)),
                  pl.BlockSpec((LANE, LANE), lambda s, c: (c, c)),
                  pl.BlockSpec((LANE, LANE), lambda s, c: (c, c)),
                  pl.BlockSpec((8, LANE), lambda s, c: (0, c))],
        out_specs=pl.BlockSpec((S, LANE), lambda s, c: (s, c)),
        out_shape=jax.ShapeDtypeStruct((T, LRU_W), F32),
        compiler_params=_cp("parallel", "parallel"),
    )(xr, u_lru, wa, wx, vec)


def lru_bwd(xr, u_lru, dy, wa, wx, vec, S, name):
    T = xr.shape[0]

    def body(xr_ref, gt_ref, dy_ref, wa_ref, wx_ref, vec_ref,
             dxr_ref, dgt_ref, dwa_ref, dwx_ref, dvec_ref):
        @pl.when(pl.program_id(1) == 0)
        def _():
            dwa_ref[...] = jnp.zeros_like(dwa_ref)
            dwx_ref[...] = jnp.zeros_like(dwx_ref)
            dvec_ref[...] = jnp.zeros_like(dvec_ref)

        x = xr_ref[...]
        n = x.shape[0]
        row = _iota(x.shape, 0)
        lam = vec_ref[2:3, :]
        r, i, sp, a, e2, m = _lru_gates(x, wa_ref[...], vec_ref[0:1, :], wx_ref[...], vec_ref[1:2, :], lam)
        v = i * x
        hh = _scan(a, m * v, row)
        gt = gt_ref[...]
        dy = dy_ref[...]
        ge, t = _gelu(gt)
        dgt_ref[...] = dy * hh * (0.5 * (1.0 + t) + 0.5 * gt * (1.0 - t * t) * GELU_C
                                  * (1.0 + 3.0 * 0.044715 * gt * gt))
        a_next = jnp.where(row < n - 1, pltpu.roll(a, n - 1, 0), 0.0)
        G = _rscan(a_next, dy * ge, row)
        da = G * _shift_down(hh, 1, row)
        dv = G * m
        dla = da * a - (G * v) * e2 / m
        dr = dla * (-LRU_C * sp)
        dsp = jnp.sum(dla * (-LRU_C * r), axis=0, keepdims=True)
        dra = dr * r * (1.0 - r)
        dia = (dv * x) * i * (1.0 - i)
        dxr_ref[...] = dv * i + _mm_nt(dra, wa_ref[...]) + _mm_nt(dia, wx_ref[...])
        dwa_ref[0] += _mm_tn(x, dra)
        dwx_ref[0] += _mm_tn(x, dia)
        dvec_ref[0:1, :] += jnp.sum(dra, axis=0, keepdims=True)
        dvec_ref[1:2, :] += jnp.sum(dia, axis=0, keepdims=True)
        dvec_ref[2:3, :] += dsp * (-_sigmoid(-lam))

    return pl.pallas_call(
        body, name=name, grid=(LRU_W // LANE, T // S),
        in_specs=[pl.BlockSpec((S, LANE), lambda c, s: (s, c)),
                  pl.BlockSpec((S, LANE), lambda c, s: (s, 2 + c)),
                  pl.BlockSpec((S, LANE), lambda c, s: (s, c)),
                  pl.BlockSpec((LANE, LANE), lambda c, s: (c, c)),
                  pl.BlockSpec((LANE, LANE), lambda c, s: (c, c)),
                  pl.BlockSpec((8, LANE), lambda c, s: (0, c))],
        out_specs=[pl.BlockSpec((S, LANE), lambda c, s: (s, c)),
                   pl.BlockSpec((S, LANE), lambda c, s: (s, c)),
                   pl.BlockSpec((1, LANE, LANE), lambda c, s: (c, 0, 0)),
                   pl.BlockSpec((1, LANE, LANE), lambda c, s: (c, 0, 0)),
                   pl.BlockSpec((8, LANE), lambda c, s: (0, c))],
        out_shape=[jax.ShapeDtypeStruct((T, LRU_W), F32), jax.ShapeDtypeStruct((T, LRU_W), F32),
                   jax.ShapeDtypeStruct((2, LANE, LANE), F32), jax.ShapeDtypeStruct((2, LANE, LANE), F32),
                   jax.ShapeDtypeStruct((8, LRU_W), F32)],
        compiler_params=_cp("parallel", "arbitrary"),
    )(xr, u_lru, dy, wa, wx, vec)


def _bucket_table():
    qi = np.arange(BQ)[:, None]
    kj = np.arange(2 * BQ)[None, :]
    dist = BQ + qi - kj
    band = (dist >= 0) & (dist < WINDOW)
    dd = np.maximum(dist, 0)
    max_exact = REL_BUCKETS // 2
    large = max_exact + (np.log(np.maximum(dd, 1).astype(np.float32) / np.float32(max_exact))
                         / np.float32(math.log(REL_MAX_DIST / max_exact))
                         * np.float32(REL_BUCKETS - max_exact)).astype(np.int32)
    large = np.minimum(large, REL_BUCKETS - 1)
    bucket = np.where(dd < max_exact, dd, large)
    return np.where(band, bucket, -1).astype(np.int32)


def _att_specs(S):
    nb = S // BQ
    qc = ATT_W // LANE
    return [pl.BlockSpec((BQ, ATT_W), lambda b, n: (b * nb + n, 0)),
            pl.BlockSpec((BQ, KV_W), lambda b, n: (b * nb + jnp.maximum(n - 1, 0), qc)),
            pl.BlockSpec((BQ, KV_W), lambda b, n: (b * nb + n, qc)),
            pl.BlockSpec((BQ, KV_W), lambda b, n: (b * nb + jnp.maximum(n - 1, 0), qc + 1)),
            pl.BlockSpec((BQ, KV_W), lambda b, n: (b * nb + n, qc + 1))]


def _att_bias(bk, rb_ref, bias_s):
    for h in range(ATT_H):
        acc = jnp.zeros(bk.shape, F32)
        for bb in range(REL_BUCKETS):
            acc = jnp.where(bk == bb, rb_ref[bb * ATT_H + h], acc)
        bias_s[h] = acc


def _att_probs(qs, kgs, bias_s, valid, sk_ref):
    heads = range(ATT_H)
    s = [_mm_nt(qs[h], kgs[h // ATT_G]) for h in heads]
    s = [jnp.where(valid, s[h] * (HD ** -0.5) + bias_s[h], NEG) for h in heads]
    m = [jnp.maximum(jnp.max(s[h], axis=-1, keepdims=True), sk_ref[h]) for h in heads]
    e = [jnp.exp(s[h] - m[h]) for h in heads]
    es = [jnp.exp(sk_ref[h] - m[h]) for h in heads]
    den = [jnp.sum(e[h], axis=-1, keepdims=True) + es[h] for h in heads]
    return [e[h] / den[h] for h in heads], [es[h] / den[h] for h in heads]


def _att_kv(kp_ref, kc_ref, vp_ref, vc_ref):
    cat = lambda a, b, g: jnp.concatenate([a[:, HD * g:HD * (g + 1)], b[:, HD * g:HD * (g + 1)]], axis=0)
    return ([cat(kp_ref, kc_ref, g) for g in range(ATT_KV)], [cat(vp_ref, vc_ref, g) for g in range(ATT_KV)])


def attn_fwd(u_att, sinks, rel_bias, S, name):
    T = u_att.shape[0]
    nb = S // BQ
    table = jnp.asarray(_bucket_table())

    def body(sk_ref, rb_ref, bk_ref, q_ref, kp_ref, kc_ref, vp_ref, vc_ref, o_ref, bias_s):
        b = pl.program_id(0)
        n = pl.program_id(1)
        bk = bk_ref[...]

        @pl.when((b == 0) & (n == 0))
        def _():
            _att_bias(bk, rb_ref, bias_s)

        valid = (bk >= 0) & ((n > 0) | (_iota(bk.shape, 1) >= BQ))
        kgs, vgs = _att_kv(kp_ref, kc_ref, vp_ref, vc_ref)
        p, _ = _att_probs([q_ref[:, HD * h:HD * (h + 1)] for h in range(ATT_H)], kgs, bias_s, valid, sk_ref)
        outs = [_mm(p[h], vgs[h // ATT_G]) for h in range(ATT_H)]
        for h in range(ATT_H):
            o_ref[:, HD * h:HD * (h + 1)] = outs[h]

    smem = pl.BlockSpec(memory_space=pltpu.SMEM)
    return pl.pallas_call(
        body, name=name, grid=(T // S, nb),
        in_specs=[smem, smem, _full_spec((BQ, 2 * BQ))] + _att_specs(S),
        out_specs=pl.BlockSpec((BQ, ATT_W), lambda b, n: (b * nb + n, 0)),
        out_shape=jax.ShapeDtypeStruct((T, ATT_W), F32),
        scratch_shapes=[pltpu.VMEM((ATT_H, BQ, 2 * BQ), F32)],
        compiler_params=_cp("arbitrary", "arbitrary"),
    )(sinks, rel_bias, table, u_att, u_att, u_att, u_att, u_att)


def attn_bwd(u_att, dy, sinks, rel_bias, S, name):
    T = u_att.shape[0]
    nb = S // BQ
    nB = T // S
    table = jnp.asarray(_bucket_table())
    scale = HD ** -0.5

    def body(sk_ref, rb_ref, bk_ref, q_ref, kp_ref, kc_ref, vp_ref, vc_ref, dy_ref,
             du_ref, drel_ref, dsk_ref, bias_s, dbias_s):
        b = pl.program_id(0)
        n = pl.program_id(1)
        bk = bk_ref[...]

        @pl.when((b == 0) & (n == 0))
        def _():
            _att_bias(bk, rb_ref, bias_s)
            dbias_s[...] = jnp.zeros_like(dbias_s)
            dsk_ref[...] = jnp.zeros_like(dsk_ref)
            drel_ref[...] = jnp.zeros_like(drel_ref)

        @pl.when(n == 0)
        def _():
            du_ref[...] = jnp.zeros_like(du_ref)

        valid = (bk >= 0) & ((n > 0) | (_iota(bk.shape, 1) >= BQ))
        r_cur = pl.multiple_of(n * BQ, BQ)
        r_prev = pl.multiple_of(jnp.maximum(n - 1, 0) * BQ, BQ)
        heads = range(ATT_H)
        kgs, vgs = _att_kv(kp_ref, kc_ref, vp_ref, vc_ref)
        qs = [q_ref[:, HD * h:HD * (h + 1)] for h in heads]
        dos = [dy_ref[:, HD * h:HD * (h + 1)] for h in heads]
        p, ps = _att_probs(qs, kgs, bias_s, valid, sk_ref)
        dp = [_mm_nt(dos[h], vgs[h // ATT_G]) for h in heads]
        delta = [jnp.sum(p[h] * dp[h], axis=-1, keepdims=True) for h in heads]
        ds = [p[h] * (dp[h] - delta[h]) for h in heads]
        dss = [ds[h] * scale for h in heads]
        dq = [_mm(dss[h], kgs[h // ATT_G]) for h in heads]
        dks = [_mm_tn(dss[h], qs[h]) for h in heads]
        dvs = [_mm_tn(p[h], dos[h]) for h in heads]
        for h in heads:
            dbias_s[h] += ds[h]
            dsk_ref[h:h + 1, :] += jnp.broadcast_to(jnp.sum(-ps[h] * delta[h], axis=0, keepdims=True), (1, LANE))
            du_ref[pl.ds(r_cur, BQ), HD * h:HD * (h + 1)] = dq[h]
        for g in range(ATT_KV):
            of_group = range(g * ATT_G, (g + 1) * ATT_G)
            dk = functools.reduce(lambda x, y: x + y, [dks[h] for h in of_group])
            dv = functools.reduce(lambda x, y: x + y, [dvs[h] for h in of_group])
            ck = ATT_W + HD * g
            cv = ATT_W + KV_W + HD * g
            du_ref[pl.ds(r_prev, BQ), ck:ck + HD] += dk[0:BQ]
            du_ref[pl.ds(r_cur, BQ), ck:ck + HD] += dk[BQ:]
            du_ref[pl.ds(r_prev, BQ), cv:cv + HD] += dv[0:BQ]
            du_ref[pl.ds(r_cur, BQ), cv:cv + HD] += dv[BQ:]

        @pl.when((b == nB - 1) & (n == nb - 1))
        def _():
            lane = _iota((1, LANE), 1)
            for h in range(ATT_H):
                db = dbias_s[h]
                acc = jnp.zeros((1, LANE), F32)
                for bb in range(REL_BUCKETS):
                    val = jnp.sum(jnp.sum(jnp.where(bk == bb, db, 0.0), axis=1, keepdims=True),
                                  axis=0, keepdims=True)
                    acc = jnp.where(lane == bb, val, acc)
                drel_ref[h:h + 1, :] = acc

    smem = pl.BlockSpec(memory_space=pltpu.SMEM)
    return pl.pallas_call(
        body, name=name, grid=(nB, nb),
        in_specs=[smem, smem, _full_spec((BQ, 2 * BQ))] + _att_specs(S)
        + [pl.BlockSpec((BQ, ATT_W), lambda b, n: (b * nb + n, 0))],
        out_specs=[pl.BlockSpec((S, ATT_W + 2 * KV_W), lambda b, n: (b, 0)),
                   _full_spec((8, LANE)), _full_spec((8, LANE))],
        out_shape=[jax.ShapeDtypeStruct((T, ATT_W + 2 * KV_W), F32),
                   jax.ShapeDtypeStruct((8, LANE), F32), jax.ShapeDtypeStruct((8, LANE), F32)],
        scratch_shapes=[pltpu.VMEM((ATT_H, BQ, 2 * BQ), F32), pltpu.VMEM((ATT_H, BQ, 2 * BQ), F32)],
        compiler_params=_cp("arbitrary", "arbitrary"),
    )(sinks, rel_bias, table, u_att, u_att, u_att, u_att, u_att, dy)


def _head_of(i):
    return lax.shift_right_logical(i, 6)


def _head_mask(shape):
    return (_head_of(_iota(shape, 0)) == _head_of(_iota(shape, 1))).astype(F32)


def _dn_point(c, uba, alog, dtb):
    s = c * _sigmoid(c)
    qt, kt, vt = s[:, 0:256], s[:, 256:512], s[:, 512:768]
    ones_bd = _head_mask((DN_W, DN_W))
    q = qt * lax.rsqrt(_mm3(qt * qt, ones_bd) + EPS) * (HD ** -0.5)
    k = kt * lax.rsqrt(_mm3(kt * kt, ones_bd) + EPS)
    sel = _head_of(_iota((LANE, DN_W), 1))
    row = _iota((LANE, DN_W), 0)
    braw = _mm3(uba, (row == sel).astype(F32))
    araw = _mm3(uba, (row == sel + DN_H).astype(F32)) + dtb
    beta = _sigmoid(braw)
    g = -jnp.exp(alog) * (jnp.maximum(araw, 0.0) + jnp.log(1.0 + jnp.exp(-jnp.abs(araw))))
    return q, k, vt, g, beta


def dn_point_fwd(c, uba, alog, dtb, name):
    T = c.shape[0]
    tm = min(TOK_TILE, T)

    def body(c_ref, u_ref, al_ref, dt_ref, *outs):
        for ref, val in zip(outs, _dn_point(c_ref[...], u_ref[...], al_ref[...], dt_ref[...])):
            ref[...] = val

    return pl.pallas_call(
        body, name=name, grid=(T // tm,),
        in_specs=[_row_spec(tm, 768), _row_spec(tm, LANE), _full_spec((1, DN_W)), _full_spec((1, DN_W))],
        out_specs=[_row_spec(tm, DN_W)] * 5,
        out_shape=[jax.ShapeDtypeStruct((T, DN_W), F32)] * 5,
        compiler_params=_cp("parallel"),
    )(c, uba, alog, dtb)


def dn_point_bwd(c, uba, alog, dtb, douts, name):
    T = c.shape[0]
    tm = min(TOK_TILE, T)

    def body(c_ref, u_ref, al_ref, dt_ref, dq, dk, dv, dg, db, dc_ref, du_ref, dvec_ref):
        @pl.when(pl.program_id(0) == 0)
        def _():
            dvec_ref[...] = jnp.zeros_like(dvec_ref)

        _, vjp = jax.vjp(_dn_point, c_ref[...], u_ref[...], al_ref[...], dt_ref[...])
        dc, du, dal, ddt = vjp((dq[...], dk[...], dv[...], dg[...], db[...]))
        dc_ref[...] = dc
        du_ref[...] = du
        fold = (_iota((LANE, DN_W), 0) == _head_of(_iota((LANE, DN_W), 1))).astype(F32)
        both = jnp.concatenate([dal, ddt, jnp.zeros((6, DN_W), F32)], axis=0)
        dvec_ref[...] += _mmx_nt(both, fold)

    return pl.pallas_call(
        body, name=name, grid=(T // tm,),
        in_specs=[_row_spec(tm, 768), _row_spec(tm, LANE), _full_spec((1, DN_W)), _full_spec((1, DN_W))]
        + [_row_spec(tm, DN_W)] * 5,
        out_specs=[_row_spec(tm, 768), _row_spec(tm, LANE), _full_spec((8, LANE))],
        out_shape=[jax.ShapeDtypeStruct((T, 768), F32), jax.ShapeDtypeStruct((T, LANE), F32),
                   jax.ShapeDtypeStruct((8, LANE), F32)],
        compiler_params=_cp("arbitrary"),
    )(c, uba, alog, dtb, *douts)


def _unit_lower_inverses(lmats):
    eye = (_iota(lmats[0].shape, 0) == _iota(lmats[0].shape, 1)).astype(F32)
    tinvs = [eye - lm for lm in lmats]
    pws = list(lmats)
    for _ in range(5):
        pws = [_mm3(pw, pw) for pw in pws]
        tinvs = [t + _mm3(t, pw) for t, pw in zip(tinvs, pws)]
    return tuple(tinvs)


def _inverse_bwd(tinv, d):
    return -_mm3_nt(_mm3_tn(tinv, d), tinv)


@jax.custom_vjp
def _tri_invs(lmats):
    return _unit_lower_inverses(lmats)


def _tri_invs_fwd(lmats):
    tinvs = _unit_lower_inverses(lmats)
    return tinvs, tinvs


_tri_invs.defvjp(_tri_invs_fwd, lambda tinvs, ds: (tuple(_inverse_bwd(t, d) for t, d in zip(tinvs, ds)),))


@jax.custom_vjp
def _tri_inv_known(lmat, tinv):
    return tinv


_tri_inv_known.defvjp(lambda lmat, tinv: (tinv, tinv),
                      lambda tinv, d: (_inverse_bwd(tinv, d), jnp.zeros_like(tinv)))


DN_SUB = 4


def _dn_stack(x):
    return jnp.concatenate([x, x, x, x], axis=0) * _head_mask((DN_W, DN_W))


def _dn_pre_inverse(q, k, v, g, beta):
    hm = _head_mask((DN_W, DN_W))
    ri = _iota((DN_W, DN_W), 0) & (CHUNK - 1)
    ci = _iota((DN_W, DN_W), 1) & (CHUNK - 1)
    tri64 = (_iota((CHUNK, CHUNK), 0) >= _iota((CHUNK, CHUNK), 1)).astype(F32)
    gc = _mm3(tri64, g)
    ks = _dn_stack(k)
    gcol = jnp.sum(_dn_stack(gc), axis=1, keepdims=True) * (1.0 / HD)
    gmat = jnp.broadcast_to(gcol, (DN_W, DN_W))
    decay = jnp.exp(jnp.minimum(gmat - gmat.T, 0.0))
    lmat = _mm_nt(_dn_stack(k * beta), ks) * decay * (hm * (ri > ci).astype(F32))
    att = _mm_nt(_dn_stack(q), ks) * decay * (hm * (ri >= ci).astype(F32))
    return lmat, att, gc


def _dn_post_inverse(q, k, v, g, beta, tinv, att, gc):
    glast = jnp.sum(g, axis=0, keepdims=True)
    eg = jnp.exp(gc)
    u = _mm(tinv, _dn_stack(v * beta))
    w = _mm(tinv, _dn_stack(k * beta * eg))
    return u, w, att, _dn_stack(q * eg), _dn_stack(k * jnp.exp(glast - gc)), jnp.exp(glast), tinv


def _dn_apply(state, prep):
    u, w, att, qe, kd, eglast, _ = prep
    vn = u - _mm(w, state)
    o4 = _mm(qe, state) + _mm(att, vn)
    o = o4[0:64] + o4[64:128] + o4[128:192] + o4[192:256]
    return o, state * eglast + _mm_tn(kd, vn)


def _dn_chunks(states, q, k, v, g, beta, knowns=None):
    nb = len(q)
    n = q[0].shape[0] // CHUNK
    chunks = [[tuple(x[b][c * CHUNK:(c + 1) * CHUNK] for x in (q, k, v, g, beta)) for c in range(n)]
              for b in range(nb)]
    pre = [[_dn_pre_inverse(*ch) for ch in seq] for seq in chunks]
    lmats = [p[0] for seq in pre for p in seq]
    if knowns is None:
        flat = _tri_invs(tuple(lmats))
    else:
        flat = [_tri_inv_known(lm, kn) for lm, kn in zip(lmats, [kn for seq in knowns for kn in seq])]
    tinvs = [flat[b * n:(b + 1) * n] for b in range(nb)]
    preps = [[_dn_post_inverse(*chunks[b][c], tinvs[b][c], pre[b][c][1], pre[b][c][2]) for c in range(n)]
             for b in range(nb)]
    states = list(states)
    outs = [[] for _ in range(nb)]
    for c in range(n):
        for b in range(nb):
            o, states[b] = _dn_apply(states[b], preps[b][c])
            outs[b].append(o)
    return tuple(jnp.concatenate(o, axis=0) for o in outs), tuple(states), tinvs


def _dn_scan_specs(nb, S, reverse):
    rows = DN_SUB * CHUNK
    ns = S // rows
    at = (lambda t: ns - 1 - t) if reverse else (lambda t: t)
    return (pl.BlockSpec((nb, rows, DN_W), lambda t: (0, at(t), 0)),
            pl.BlockSpec((nb, 1, DN_W, DN_W), lambda t: (0, at(t), 0, 0)),
            pl.BlockSpec((nb, DN_SUB, DN_W, DN_W), lambda t: (0, at(t), 0, 0)))


def dn_scan_fwd(q, k, v, g, beta, S, name):
    T = q.shape[0]
    nb = T // S
    ns = S // (DN_SUB * CHUNK)
    seqs = range(nb)

    def body(q_ref, k_ref, v_ref, g_ref, b_ref, o_ref, st_ref, ti_ref, s_s):
        @pl.when(pl.program_id(0) == 0)
        def _():
            s_s[...] = jnp.zeros_like(s_s)

        per = lambda ref: tuple(ref[b] for b in seqs)
        sts = per(s_s)
        for b in seqs:
            st_ref[b, 0] = sts[b]
        outs, news, tinvs = _dn_chunks(sts, per(q_ref), per(k_ref), per(v_ref), per(g_ref), per(b_ref))
        for b in seqs:
            o_ref[b] = outs[b]
            s_s[b] = news[b]
            for c, tinv in enumerate(tinvs[b]):
                ti_ref[b, c] = tinv

    spec, st_spec, ti_spec = _dn_scan_specs(nb, S, False)
    o, states, tinvs = pl.pallas_call(
        body, name=name, grid=(ns,),
        in_specs=[spec] * 5,
        out_specs=[spec, st_spec, ti_spec],
        out_shape=[jax.ShapeDtypeStruct((nb, S, DN_W), F32),
                   jax.ShapeDtypeStruct((nb, ns, DN_W, DN_W), F32),
                   jax.ShapeDtypeStruct((nb, S // CHUNK, DN_W, DN_W), F32)],
        scratch_shapes=[pltpu.VMEM((nb, DN_W, DN_W), F32)],
        compiler_params=_cp("arbitrary"),
    )(*(t.reshape(nb, S, DN_W) for t in (q, k, v, g, beta)))
    return o.reshape(T, DN_W), states, tinvs


def dn_scan_bwd(q, k, v, g, beta, states, tinvs, do, S, name):
    T = q.shape[0]
    nb = T // S
    ns = S // (DN_SUB * CHUNK)
    seqs = range(nb)

    def body(q_ref, k_ref, v_ref, g_ref, b_ref, st_ref, ti_ref, do_ref, dq, dk, dv, dg, db, ds_s):
        @pl.when(pl.program_id(0) == 0)
        def _():
            ds_s[...] = jnp.zeros_like(ds_s)

        per = lambda ref: tuple(ref[b] for b in seqs)
        knowns = [[ti_ref[b, c] for c in range(DN_SUB)] for b in seqs]
        _, vjp = jax.vjp(lambda *args: _dn_chunks(*args, knowns=knowns)[:2],
                         tuple(st_ref[b, 0] for b in seqs), per(q_ref), per(k_ref), per(v_ref), per(g_ref),
                         per(b_ref))
        grads = vjp((per(do_ref), per(ds_s)))
        for b in seqs:
            ds_s[b] = grads[0][b]
            for ref, val in zip((dq, dk, dv, dg, db), grads[1:]):
                ref[b] = val[b]

    spec, st_spec, ti_spec = _dn_scan_specs(nb, S, True)
    res = pl.pallas_call(
        body, name=name, grid=(ns,),
        in_specs=[spec] * 5 + [st_spec, ti_spec, spec],
        out_specs=[spec] * 5,
        out_shape=[jax.ShapeDtypeStruct((nb, S, DN_W), F32)] * 5,
        scratch_shapes=[pltpu.VMEM((nb, DN_W, DN_W), F32)],
        compiler_params=_cp("arbitrary"),
    )(*(t.reshape(nb, S, DN_W) for t in (q, k, v, g, beta)), states, tinvs, do.reshape(nb, S, DN_W))
    return [r.reshape(T, DN_W) for r in res]


def _dn_gate(o, z, nl):
    ms = _mm3(o * o, _head_mask((DN_W, DN_W))) * (1.0 / HD)
    return o * lax.rsqrt(ms + EPS) * nl * (z * _sigmoid(z))


def dn_gate_fwd(o, u_dn, nl, name):
    T = o.shape[0]
    tm = min(LIGHT_TILE, T)

    def body(o_ref, z_ref, n_ref, y_ref):
        y_ref[...] = _dn_gate(o_ref[...], z_ref[...], n_ref[...])

    return pl.pallas_call(
        body, name=name, grid=(T // tm,),
        in_specs=[_row_spec(tm, DN_W), pl.BlockSpec((tm, DN_W), lambda i: (i, 3)), _full_spec((1, DN_W))],
        out_specs=_row_spec(tm, DN_W),
        out_shape=jax.ShapeDtypeStruct((T, DN_W), F32),
        compiler_params=_cp("parallel"),
    )(o, u_dn, nl)


def dn_gate_bwd(o, u_dn, nl, dy, name):
    T = o.shape[0]
    tm = min(LIGHT_TILE, T)

    def body(o_ref, z_ref, n_ref, dy_ref, do_ref, dz_ref, dn_ref):
        @pl.when(pl.program_id(0) == 0)
        def _():
            dn_ref[...] = jnp.zeros_like(dn_ref)

        _, vjp = jax.vjp(_dn_gate, o_ref[...], z_ref[...], n_ref[...])
        do, dz, dn = vjp(dy_ref[...])
        do_ref[...] = do
        dz_ref[...] = dz
        fold = (_iota((LANE, DN_W), 0) == (_iota((LANE, DN_W), 1) & (HD - 1))).astype(F32)
        dn_ref[...] += _mmx_nt(jnp.concatenate([dn, jnp.zeros((7, DN_W), F32)], axis=0), fold)

    return pl.pallas_call(
        body, name=name, grid=(T // tm,),
        in_specs=[_row_spec(tm, DN_W), pl.BlockSpec((tm, DN_W), lambda i: (i, 3)), _full_spec((1, DN_W)),
                  _row_spec(tm, DN_W)],
        out_specs=[_row_spec(tm, DN_W), _row_spec(tm, DN_W), _full_spec((8, LANE))],
        out_shape=[jax.ShapeDtypeStruct((T, DN_W), F32), jax.ShapeDtypeStruct((T, DN_W), F32),
                   jax.ShapeDtypeStruct((8, LANE), F32)],
        compiler_params=_cp("arbitrary"),
    )(o, u_dn, nl, dy)


Y_SPLITS = (LRU_W, ATT_W, DN_W)
Y_OFFS = (0, LRU_W, LRU_W + ATT_W)


ROWS_DEV = D // N_DEV


def _dev_rows_spec(row0):
    return pl.BlockSpec((N_DEV, ROWS_DEV, D), lambda *_: (0, row0 // ROWS_DEV, 0))


def _dev_rows(w_ref, off, n):
    return w_ref[off // ROWS_DEV:(off + n) // ROWS_DEV].reshape(n, D)


def wout_fwd(h, ys, wb, name):
    T = h.shape[0]
    tm = min(LIGHT_TILE, T)

    def body(h_ref, y0, y1, y2, w_ref, o_ref, yc_ref):
        for ref, off, n in zip((y0, y1, y2), Y_OFFS, Y_SPLITS):
            yc_ref[:, off:off + n] = ref[...].astype(BF16)
        o_ref[...] = h_ref[...] + _mm(yc_ref[...], _dev_rows(w_ref, 0, D))

    return pl.pallas_call(
        body, name=name, grid=(T // tm,),
        in_specs=[_row_spec(tm, D)] + [_row_spec(tm, n) for n in Y_SPLITS] + [_dev_rows_spec(R_WOUT)],
        out_specs=[_row_spec(tm, D), _row_spec(tm, D)],
        out_shape=[jax.ShapeDtypeStruct((T, D), F32), jax.ShapeDtypeStruct((T, D), BF16)],
        compiler_params=_cp("parallel"),
    )(h, *ys, wb)


def wout_bwd(dy, wb, name):
    T = dy.shape[0]
    tm = min(LIGHT_TILE, T)

    def body(dy_ref, w_ref, d0, d1, d2):
        dys = _mm_nt(dy_ref[...], _dev_rows(w_ref, 0, D))
        for ref, off, n in zip((d0, d1, d2), Y_OFFS, Y_SPLITS):
            ref[...] = dys[:, off:off + n]

    return pl.pallas_call(
        body, name=name, grid=(T // tm,),
        in_specs=[_row_spec(tm, D), _dev_rows_spec(R_WOUT)],
        out_specs=[_row_spec(tm, n) for n in Y_SPLITS],
        out_shape=[jax.ShapeDtypeStruct((T, n), F32) for n in Y_SPLITS],
        compiler_params=_cp("parallel"),
    )(dy, wb)


def ple_fwd(h, g, pe, wg, wp, name):
    T = h.shape[0]
    tm = min(LIGHT_TILE, T)

    def body(h_ref, g_ref, p_ref, wg_ref, wp_ref, o_ref):
        hh = h_ref[...]
        xn = _rms(hh, g_ref[...])[0]
        o_ref[...] = hh + _sigmoid(_mm(xn, _dev_rows(wg_ref, 0, D))) * _mm(p_ref[...], wp_ref[...])

    return pl.pallas_call(
        body, name=name, grid=(T // tm,),
        in_specs=[_row_spec(tm, D), _full_spec((1, D)), _row_spec(tm, PLE), _dev_rows_spec(R_PGATE),
                  _full_spec((PLE, D))],
        out_specs=_row_spec(tm, D),
        out_shape=jax.ShapeDtypeStruct((T, D), F32),
        compiler_params=_cp("parallel"),
    )(h, g, pe, wg, wp)


def ple_bwd(h, dy, g, pe, wg, wp, name):
    T = h.shape[0]
    tm = min(TOK_TILE, T)

    def body(h_ref, dy_ref, g_ref, p_ref, wg_ref, wp_ref, dh_ref, dz_ref, dpp_ref, xn_ref, dn_ref):
        @pl.when(pl.program_id(0) == 0)
        def _():
            dn_ref[...] = jnp.zeros_like(dn_ref)

        gg = g_ref[...]
        dy = dy_ref[...]
        xn, xhat, rstd = _rms(h_ref[...], gg)
        wg = _dev_rows(wg_ref, 0, D)
        gate = _sigmoid(_mm(xn, wg))
        pp = _mm(p_ref[...], wp_ref[...])
        dz = dy * pp * gate * (1.0 - gate)
        dz_ref[...] = dz.astype(BF16)
        dpp_ref[...] = (dy * gate).astype(BF16)
        xn_ref[...] = xn.astype(BF16)
        dh, dn = _rms_bwd(_mm_nt(dz, wg), xhat, rstd, gg)
        dh_ref[...] = dy + dh
        dn_ref[...] += dn

    return pl.pallas_call(
        body, name=name, grid=(T // tm,),
        in_specs=[_row_spec(tm, D), _row_spec(tm, D), _full_spec((1, D)), _row_spec(tm, PLE),
                  _dev_rows_spec(R_PGATE), _full_spec((PLE, D))],
        out_specs=[_row_spec(tm, D), _row_spec(tm, D), _row_spec(tm, D), _row_spec(tm, D), _full_spec((1, D))],
        out_shape=[jax.ShapeDtypeStruct((T, D), F32), jax.ShapeDtypeStruct((T, D), BF16),
                   jax.ShapeDtypeStruct((T, D), BF16), jax.ShapeDtypeStruct((T, D), BF16),
                   jax.ShapeDtypeStruct((1, D), F32)],
        compiler_params=_cp("arbitrary"),
    )(h, dy, g, pe, wg, wp)


def loss_head(h, g, target, name):
    T = h.shape[0]
    tm = min(LIGHT_TILE, T)

    def body(h_ref, g_ref, t_ref, loss_ref, dh_ref, dn_ref):
        @pl.when(pl.program_id(0) == 0)
        def _():
            dn_ref[...] = jnp.zeros_like(dn_ref)
            loss_ref[...] = jnp.zeros_like(loss_ref)

        gg = g_ref[...]
        y, xhat, rstd = _rms(h_ref[...], gg)
        err = y - t_ref[...]
        per_tok = jnp.mean(err * err, axis=-1, keepdims=True)
        loss_ref[...] += 0.5 * jnp.sum(per_tok, axis=0, keepdims=True)
        dh, dn = _rms_bwd(err * (1.0 / D), xhat, rstd, gg)
        dh_ref[...] = dh
        dn_ref[...] += dn

    return pl.pallas_call(
        body, name=name, grid=(T // tm,),
        in_specs=[_row_spec(tm, D), _full_spec((1, D)), _row_spec(tm, D)],
        out_specs=[_full_spec((8, LANE)), _row_spec(tm, D), _full_spec((1, D))],
        out_shape=[jax.ShapeDtypeStruct((8, LANE), F32), jax.ShapeDtypeStruct((T, D), F32),
                   jax.ShapeDtypeStruct((1, D), F32)],
        compiler_params=_cp("arbitrary"),
    )(h, g, target)


def _block_diag(w):
    return jnp.einsum('hij,hk->hikj', w, jnp.eye(4, dtype=w.dtype)).reshape(LRU_W, LRU_W)


def _layer_consts(W, l):
    row = lambda v: v.reshape(1, -1)
    zeros = jnp.zeros((5, LRU_W), F32)
    return dict(
        wa=_block_diag(W["lru_w_a"][l]), wx=_block_diag(W["lru_w_x"][l]),
        lru_vec=jnp.concatenate([row(W["lru_b_a"][l]), row(W["lru_b_x"][l]), row(W["lru_lambda"][l]), zeros], 0),
        lru_cb=row(W["lru_conv_b"][l]),
        sinks=W["attn_sinks"][l], rel=W["rel_bias"].reshape(-1),
        dn_cb=jnp.zeros((1, 3 * DN_W), F32),
        alog=row(jnp.repeat(W["dn_a_log"][l], HD)), dtb=row(jnp.repeat(W["dn_dt_bias"][l], HD)),
        dn_nl=row(jnp.tile(W["dn_norm"][l], DN_H)),
    )


def _layer_fwd(h0, pe, W, l, S, need=None):
    n = f"l{l}_"
    c_ = _layer_consts(W, l)
    row = lambda v: v.reshape(1, -1)
    need = need or (lambda *_: None)
    need(l, "f1", h0)
    h1, *ffn1_kept = ffn_fwd(h0, row(W["ffn1_norm"][l]), W["f1_cols"][l], W["f1_rows"][l], n + "ffn1_fwd")
    need(l, "in", h1)
    u_lru, u_att, u_dn, u_ba, xn_mix = mixin_fwd(h1, row(W["mix_norm"][l]), W["w_in"][l], n + "mixin_fwd")
    xr = conv_fwd(u_lru, W["lru_conv_w"][l], c_["lru_cb"], S, 0, LRU_W, n + "lru_conv_fwd")
    y_lru = lru_fwd(xr, u_lru, c_["wa"], c_["wx"], c_["lru_vec"], S, n + "lru_fwd")
    y_att = attn_fwd(u_att, c_["sinks"], c_["rel"], S, n + "attn_fwd")
    cc = conv_fwd(u_dn, W["dn_conv_w"][l], c_["dn_cb"], S, 0, 3 * DN_W, n + "dn_conv_fwd")
    q, k, v, g, beta = dn_point_fwd(cc, u_ba, c_["alog"], c_["dtb"], n + "dn_point_fwd")
    o, states, tinvs = dn_scan_fwd(q, k, v, g, beta, S, n + "dn_scan_fwd")
    y_dn = dn_gate_fwd(o, u_dn, c_["dn_nl"], n + "dn_gate_fwd")
    need(l, "rest", y_dn)
    h2, ycat = wout_fwd(h1, (y_lru, y_att, y_dn), W["r_rows"][l], n + "wout_fwd")
    h3, *ffn2_kept = ffn_fwd(h2, row(W["ffn2_norm"][l]), W["r_cols"][l], W["r_rows"][l], n + "ffn2_fwd")
    h4 = ple_fwd(h3, row(W["ple_norm"][l]), pe, W["r_rows"][l], W["ple_w_proj"][l], n + "ple_fwd")
    saved = dict(ffn1=ffn1_kept, ffn2=ffn2_kept, h0=h0, h1=h1, h2=h2, h3=h3, u_lru=u_lru, u_att=u_att, u_dn=u_dn,
                 u_ba=u_ba, xn_mix=xn_mix, xr=xr, cc=cc, q=q, k=k, v=v, g=g, beta=beta, o=o, states=states, tinvs=tinvs, ycat=ycat)
    return h4, saved


GM_WOUT, GM_PGATE, GM_WIN, GM_PPROJ, GM_END, GM_ROWS = 0, 128, 256, 560, 592, 640


def _layer_bwd(dh4, sv, pe, W, l, S, token=None, on_piece=None):
    n = f"l{l}_"
    c_ = _layer_consts(W, l)
    row = lambda v: v.reshape(1, -1)
    behind = lambda v, tok: v if tok is None else v + tok.astype(v.dtype)
    on_piece = on_piece or (lambda *_: None)
    G = {"mix_rows": jnp.zeros((N_DEV, GM_ROWS, D), BF16)}
    dh3, dz, dpp, xn_p, dn = ple_bwd(sv["h3"], dh4, behind(row(W["ple_norm"][l]), token), pe, W["r_rows"][l],
                                     W["ple_w_proj"][l], n + "ple_bwd")
    G["ple_norm"] = dn[0]
    G["mix_rows"] = grad_rows(xn_p, dz, G["mix_rows"], GM_PGATE, ROWS_DEV, n + "d_ple_w_gate")
    d_proj = matmul_tn(pe, dpp, n + "d_ple_w_proj")
    d_proj = d_proj.reshape(PLE, N_DEV, D // N_DEV).transpose(1, 0, 2).reshape(N_DEV, GM_END - GM_PPROJ, D)
    G["mix_rows"] = lax.dynamic_update_slice(G["mix_rows"], d_proj, (0, GM_PPROJ, 0))

    def ffn_back(which, cols_w, rows_w, h_in, dy, tok, one_by_one):
        gt, up, xn = sv[which]
        dh, dgt, dup, act, dn_ = ffn_bwd(h_in, dy, behind(row(W[which + "_norm"][l]), tok), gt, up, cols_w, rows_w,
                                         n + which + "_bwd")
        G[which + "_norm"] = dn_[0]
        zeros_rows = jnp.zeros((N_DEV, SHP, D), BF16)
        if one_by_one:
            G[which + "_gate"] = grad_cols(xn, dgt, lax.empty((1, D, FFP), BF16), 0, n + "d_" + which + "_w_gate")
            tok = on_piece(l, which + "_gate", (G[which + "_gate"],))
            G[which + "_up"] = grad_cols(xn, dup, behind(jnp.zeros((1, D, FFP), BF16), tok), 0,
                                         n + "d_" + which + "_w_up")
            tok = on_piece(l, which + "_up", (G[which + "_up"],))
            G[which + "_down"] = grad_rows(act, dy, behind(zeros_rows, tok), 0, SHP, n + "d_" + which + "_w_down",
                                           scale=0.5)
            return dh, on_piece(l, which + "_down", (G[which + "_down"],))
        cols = grad_cols(xn, dgt, lax.empty((2, D, FFP), BF16), 0, n + "d_" + which + "_w_gate")
        G[which + "_cols"] = grad_cols(xn, dup, cols, 1, n + "d_" + which + "_w_up")
        G[which + "_rows"] = grad_rows(act, dy, lax.empty((N_DEV, SHP, D), BF16), 0, SHP,
                                       n + "d_" + which + "_w_down", scale=0.5)
        return dh, on_piece(l, which, (G[which + "_cols"], G[which + "_rows"]))

    dh2, tok = ffn_back("ffn2", W["r_cols"][l], W["r_rows"][l], sv["h2"], dh3, None, False)
    dy_lru, dy_att, dy_dn = wout_bwd(dh2, W["r_rows"][l], n + "wout_bwd")
    G["mix_rows"] = grad_rows(sv["ycat"], dh2, G["mix_rows"], GM_WOUT, ROWS_DEV, n + "d_w_out")
    do, dz_dn, dnn = dn_gate_bwd(sv["o"], sv["u_dn"], behind(c_["dn_nl"], tok), dy_dn, n + "dn_gate_bwd")
    dqkvgb = dn_scan_bwd(sv["q"], sv["k"], sv["v"], sv["g"], sv["beta"], sv["states"], sv["tinvs"], do, S,
                         n + "dn_scan_bwd")
    dcc, du_ba, dvec_dn = dn_point_bwd(sv["cc"], sv["u_ba"], c_["alog"], c_["dtb"], dqkvgb, n + "dn_point_bwd")
    dqkv, dwb_dn = conv_bwd(sv["u_dn"], dcc, W["dn_conv_w"][l], S, 0, 3 * DN_W, n + "dn_conv_bwd")
    G["dn_norm"] = dnn[0, 0:HD]
    G["dn_a_log"] = dvec_dn[0, 0:DN_H]
    G["dn_dt_bias"] = dvec_dn[1, 0:DN_H]
    G["dn_conv_w"] = dwb_dn[0:4]
    du_att, drel, dsk = attn_bwd(sv["u_att"], dy_att, c_["sinks"], c_["rel"], S, n + "attn_bwd")
    G["attn_sinks"] = dsk[:, 0]
    G["rel_bias"] = drel[:, 0:REL_BUCKETS].T
    dxr, dgt_lru, dwa, dwx, dvec = lru_bwd(sv["xr"], sv["u_lru"], dy_lru, c_["wa"], c_["wx"], c_["lru_vec"], S,
                                           n + "lru_bwd")
    dx_lru, dwb_lru = conv_bwd(sv["u_lru"], dxr, W["lru_conv_w"][l], S, 0, LRU_W, n + "lru_conv_bwd")
    diag = lambda m: jnp.stack([m[c, HD * e:HD * (e + 1), HD * e:HD * (e + 1)] for c in range(2) for e in range(2)])
    G["lru_w_a"], G["lru_w_x"] = diag(dwa), diag(dwx)
    G["lru_b_a"], G["lru_b_x"], G["lru_lambda"] = dvec[0], dvec[1], dvec[2]
    G["lru_conv_w"], G["lru_conv_b"] = dwb_lru[0:4], dwb_lru[4]
    dh1, du_cat, dn = mixin_bwd(sv["h1"], dh2, row(W["mix_norm"][l]), W["w_in"][l],
                                (dx_lru, dgt_lru, du_att, dqkv, dz_dn, du_ba), n + "mixin_bwd")
    G["mix_norm"] = dn[0]
    d_in = matmul_tn(sv["xn_mix"], du_cat, n + "d_w_in")[:, :D_IN]
    d_in = d_in.reshape(D, N_DEV, D_IN // N_DEV).transpose(1, 0, 2).reshape(N_DEV, WIN_ROWS, D)
    d_in = jnp.pad(d_in, ((0, 0), (0, GM_PPROJ - GM_WIN - WIN_ROWS), (0, 0)))
    G["mix_rows"] = lax.dynamic_update_slice(G["mix_rows"], d_in, (0, GM_WIN, 0))
    tok = on_piece(l, "mix", (G["mix_rows"],))
    dh0, tok = ffn_back("ffn1", W["f1_cols"][l], W["f1_rows"][l], sv["h0"], dh1, tok, l == 0)
    return dh0, G, tok


def _core(x, pe, W, target, S, need=None, on_piece=None):
    h = x
    saved = []
    for l in range(DEPTH):
        h, sv = _layer_fwd(h, pe[l], W, l, S, need)
        saved.append(sv)
    loss_tile, dh, dfn = loss_head(h, W["final_norm"].reshape(1, -1), target, "loss_head")
    grads = [None] * DEPTH
    token = None
    for l in reversed(range(DEPTH)):
        dh, grads[l], token = _layer_bwd(dh, saved[l], pe[l], W, l, S, token, on_piece)
    return loss_tile[0, 0], dh, grads, dfn[0]


MESH_ID = pl.DeviceIdType.MESH
ANY_SPEC = pl.BlockSpec(memory_space=pl.ANY)
AXES = ("x", "y", "c")


def _my_pos():
    return lax.axis_index("x"), lax.axis_index("y"), lax.axis_index("c")


def _slot_of(px, py, pc):
    return 4 * px + 2 * py + pc


def all_gather(x, name):
    R, C = x.shape

    def body(x_ref, out_ref, send_sems, recv_sems, local_sem):
        mx, my, mc = _my_pos()
        me, sibling = (mx, my, mc), (mx, my, 1 - mc)
        chips = [(1 - mx, my), (mx, 1 - my), (1 - mx, 1 - my)]

        def copy(k, block, to, src=None):
            dst = out_ref.at[_slot_of(*block)]
            return pltpu.make_async_remote_copy(
                src_ref=dst if src is None else src, dst_ref=dst,
                send_sem=send_sems.at[k], recv_sem=recv_sems.at[k],
                device_id=to, device_id_type=MESH_ID)

        mine = pltpu.make_async_copy(x_ref, out_ref.at[_slot_of(*me)], local_sem)
        mine.start()
        first = [copy(0, me, sibling, src=x_ref)]
        first += [copy(1 + j, me, (*chip, mc), src=x_ref) for j, chip in enumerate(chips)]
        for cp in first:
            cp.start()
        passed = [copy(4 + j, (*chip, mc), sibling) for j, chip in enumerate(chips)]
        for j, chip in enumerate(chips):
            copy(1 + j, (*chip, mc), me).wait_recv()
            passed[j].start()
        copy(0, sibling, me).wait_recv()
        for j, chip in enumerate(chips):
            copy(4 + j, (*chip, 1 - mc), me).wait_recv()
        for cp in first + passed:
            cp.wait_send()
        mine.wait()

    return pl.pallas_call(
        body, name=name,
        out_shape=jax.ShapeDtypeStruct((N_DEV, R, C), x.dtype),
        in_specs=[ANY_SPEC], out_specs=ANY_SPEC,
        scratch_shapes=[pltpu.SemaphoreType.DMA((7,)), pltpu.SemaphoreType.DMA((7,)), pltpu.SemaphoreType.DMA],
    )(x)


def _col_window(ref, slot):
    return ref.at[:, pl.ds(pl.multiple_of(slot * SHP, LANE), SHP)]


def gather_layer(a_sh, b_sh, name):
    def body(a_ref, b_ref, ao_ref, bo_ref, send_sems, recv_sems, local_sems):
        mx, my, mc = _my_pos()
        me, sibling = (mx, my, mc), (mx, my, 1 - mc)
        chips = [(1 - mx, my), (mx, 1 - my), (1 - mx, 1 - my)]

        def copies(k, block, to, own=False):
            slot = _slot_of(*block)
            dsts = (_col_window(ao_ref, slot), bo_ref.at[slot])
            srcs = (a_ref, b_ref) if own else dsts
            return [pltpu.make_async_remote_copy(
                src_ref=s, dst_ref=d, send_sem=send_sems.at[2 * k + i], recv_sem=recv_sems.at[2 * k + i],
                device_id=to, device_id_type=MESH_ID) for i, (s, d) in enumerate(zip(srcs, dsts))]

        mine = [pltpu.make_async_copy(a_ref, _col_window(ao_ref, _slot_of(*me)), local_sems.at[0]),
                pltpu.make_async_copy(b_ref, bo_ref.at[_slot_of(*me)], local_sems.at[1])]
        for cp in mine:
            cp.start()
        first = copies(0, me, sibling, own=True)
        for j, chip in enumerate(chips):
            first += copies(1 + j, me, (*chip, mc), own=True)
        for cp in first:
            cp.start()
        passed = []
        for j, chip in enumerate(chips):
            for cp in copies(1 + j, (*chip, mc), me):
                cp.wait_recv()
            fwd = copies(4 + j, (*chip, mc), sibling)
            for cp in fwd:
                cp.start()
            passed += fwd
        for cp in copies(0, sibling, me):
            cp.wait_recv()
        for j, chip in enumerate(chips):
            for cp in copies(4 + j, (*chip, 1 - mc), me):
                cp.wait_recv()
        for cp in first + passed:
            cp.wait_send()
        for cp in mine:
            cp.wait()

    return pl.pallas_call(
        body, name=name,
        out_shape=[jax.ShapeDtypeStruct((a_sh.shape[0], FFP), a_sh.dtype),
                   jax.ShapeDtypeStruct((N_DEV,) + b_sh.shape, b_sh.dtype)],
        in_specs=[ANY_SPEC, ANY_SPEC], out_specs=[ANY_SPEC, ANY_SPEC],
        scratch_shapes=[pltpu.SemaphoreType.DMA((14,)), pltpu.SemaphoreType.DMA((14,)),
                        pltpu.SemaphoreType.DMA((2,))],
    )(a_sh, b_sh)


HBM_SPEC = pl.BlockSpec(memory_space=pltpu.HBM)
SEM_SPEC = pl.BlockSpec(memory_space=pltpu.SEMAPHORE)
SPLIT_EFFECT = pltpu.CompilerParams(has_side_effects=pltpu.SideEffectType.DATAFLOW_SIDE_EFFECTING)


def _split_ends(mode, src_ref, dst_ref, src_slot, dst_slot):
    cols = mode.endswith("cols")
    if mode.startswith("gather"):
        return src_ref, (_col_window(dst_ref, dst_slot) if cols else dst_ref.at[dst_slot])
    return (_col_window(src_ref, src_slot) if cols else src_ref.at[src_slot]), dst_ref.at[dst_slot]


def _split_peers():
    mx, my, mc = _my_pos()
    for r in range(1, N_DEV):
        peer = (1 - mx if r & 4 else mx, 1 - my if r & 2 else my, 1 - mc if r & 1 else mc)
        yield r - 1, peer, _slot_of(*peer)


def split_start(modes, srcs, dsts, name, after=()):
    n = len(modes)
    m = len(after)

    def body(*refs):
        send_sems, recv_sems, token = refs[2 * n + m], refs[2 * n + m + 1], refs[-1]
        mine = _slot_of(*_my_pos())
        for k, peer, ps in _split_peers():
            for i in range(n):
                src, dst = _split_ends(modes[i], refs[i], refs[n + i], ps, mine)
                pltpu.make_async_remote_copy(
                    src_ref=src, dst_ref=dst, send_sem=send_sems.at[n * k + i], recv_sem=recv_sems.at[n * k + i],
                    device_id=peer, device_id_type=MESH_ID).start()
        for i in range(n):
            src, dst = _split_ends(modes[i], refs[i], refs[n + i], mine, mine)
            pltpu.make_async_copy(src, dst, recv_sems.at[n * (N_DEV - 1) + i]).start()
        token[...] = jnp.zeros_like(token)

    bufs = tuple(srcs) + tuple(dsts)
    sems = pltpu.SemaphoreType.DMA((n * N_DEV,))
    res = pl.pallas_call(
        body, name=name,
        out_shape=(sems, sems) + tuple(pltpu.HBM(t.shape, t.dtype) for t in bufs)
        + (jax.ShapeDtypeStruct((8, LANE), F32),),
        in_specs=[HBM_SPEC] * (2 * n) + [ANY_SPEC] * m,
        out_specs=(SEM_SPEC, SEM_SPEC) + (HBM_SPEC,) * (2 * n) + (pl.BlockSpec(memory_space=pltpu.VMEM),),
        input_output_aliases={i: 2 + i for i in range(2 * n)},
        compiler_params=SPLIT_EFFECT,
    )(*(pltpu.with_memory_space_constraint(t, pltpu.HBM) for t in bufs), *after)
    return list(res[:-1]), res[-1]


def split_wait(modes, started, after, name):
    n = len(modes)
    after = tuple(after) if isinstance(after, (tuple, list)) else (after,)
    send_sems, recv_sems, bufs = started[0], started[1], started[2:]

    def body(*refs):
        send_sems, recv_sems = refs[2 * n], refs[2 * n + 1]
        mine = _slot_of(*_my_pos())
        for k, peer, ps in _split_peers():
            for i in range(n):
                sent = _split_ends(modes[i], refs[i], refs[n + i], ps, mine)[0]
                landed = _split_ends(modes[i], refs[i], refs[n + i], mine, ps)[1]
                cp = pltpu.make_async_remote_copy(
                    src_ref=sent, dst_ref=landed, send_sem=send_sems.at[n * k + i],
                    recv_sem=recv_sems.at[n * k + i], device_id=peer, device_id_type=MESH_ID)
                cp.wait_send()
                cp.wait_recv()
        for i in range(n):
            src, dst = _split_ends(modes[i], refs[i], refs[n + i], mine, mine)
            pltpu.make_async_copy(src, dst, recv_sems.at[n * (N_DEV - 1) + i]).wait()

    res = pl.pallas_call(
        body, name=name,
        out_shape=tuple(pltpu.HBM(t.shape, t.dtype) for t in bufs),
        in_specs=[HBM_SPEC] * (2 * n) + [SEM_SPEC, SEM_SPEC] + [ANY_SPEC] * len(after),
        out_specs=(HBM_SPEC,) * (2 * n),
        input_output_aliases={i: i for i in range(2 * n)},
        compiler_params=SPLIT_EFFECT,
    )(*bufs, send_sems, recv_sems, *after)
    return list(res[n:])


def sum_parts(parts, name):
    _, R, C = parts.shape
    tr = _pick(R, (512, 336, 272, 256, 128, 64, 32, 16, 8))

    def body(p_ref, o_ref):
        acc = p_ref[0].astype(F32)
        for k in range(1, N_DEV):
            acc += p_ref[k].astype(F32)
        o_ref[...] = acc

    return pl.pallas_call(
        body, name=name, grid=(R // tr,),
        in_specs=[pl.BlockSpec((N_DEV, tr, C), lambda i: (0, i, 0))],
        out_specs=pl.BlockSpec((tr, C), lambda i: (i, 0)),
        out_shape=jax.ShapeDtypeStruct((R, C), F32),
        compiler_params=_cp("parallel"),
    )(parts)


def sum_into(parts, row0, rows, cols, layer, dst, name):
    tr = _pick(rows, (512, 352, 128))
    blk0 = row0 // tr

    def body(p_ref, *rest):
        o_ref = rest[-1]
        acc = p_ref[0, :, 0:cols].astype(F32)
        for k in range(1, N_DEV):
            acc += p_ref[k, :, 0:cols].astype(F32)
        o_ref[0] = acc

    aliased = dst is not None
    return pl.pallas_call(
        body, name=name, grid=(rows // tr,),
        in_specs=[pl.BlockSpec((N_DEV, tr, parts.shape[2]), lambda i: (0, blk0 + i, 0))]
        + [pl.BlockSpec(memory_space=pl.ANY)] * aliased,
        out_specs=pl.BlockSpec((1, tr, cols), lambda i: (layer, i, 0)),
        out_shape=jax.ShapeDtypeStruct((DEPTH, rows, cols), F32),
        input_output_aliases={1: 0} if aliased else {},
        compiler_params=_cp("parallel"),
    )(*((parts, dst) if aliased else (parts,)))


def adamw(g, w, m, v, name):
    lead, (R, C) = g.shape[:-2], g.shape[-2:]
    tr = _pick(R, (512, 352, 256, 128, 64, 32, 16, 8))
    c1 = 1.0 - ADAM_B1 ** ADAM_STEP
    c2 = 1.0 - ADAM_B2 ** ADAM_STEP

    def body(g_ref, w_ref, m_ref, v_ref, d_ref, nm_ref, nv_ref):
        gg = g_ref[...]
        mm = ADAM_B1 * m_ref[...] + (1.0 - ADAM_B1) * gg
        vv = ADAM_B2 * v_ref[...] + (1.0 - ADAM_B2) * (gg * gg)
        nm_ref[...] = mm
        nv_ref[...] = vv
        d_ref[...] = -ADAM_LR * ((mm / c1) / (jnp.sqrt(vv / c2) + ADAM_EPS) + ADAM_WD * w_ref[...])

    if lead:
        spec = pl.BlockSpec((1, tr, C), lambda l, i: (l, i, 0))
    else:
        spec = pl.BlockSpec((tr, C), lambda l, i: (i, 0))
    return pl.pallas_call(
        body, name=name, grid=(lead[0] if lead else 1, R // tr),
        in_specs=[spec] * 4, out_specs=[spec] * 3,
        out_shape=[jax.ShapeDtypeStruct(g.shape, F32)] * 3,
        compiler_params=_cp("parallel", "parallel"),
    )(g, w, m, v)


BIG = (("ffn1_w_gate", 1, D, FF), ("ffn1_w_up", 1, D, FF), ("ffn1_w_down", 0, FF, D),
       ("w_in", 1, D, D_IN), ("w_out", 0, D, D),
       ("ffn2_w_gate", 1, D, FF), ("ffn2_w_up", 1, D, FF), ("ffn2_w_down", 0, FF, D),
       ("ple_w_gate", 0, D, D), ("ple_w_proj", 1, PLE, D))
SMALL = (("ffn1_norm", (D,), None), ("mix_norm", (D,), None), ("lru_conv_w", (4, LRU_W), LRU_W // N_DEV),
         ("lru_conv_b", (LRU_W,), None), ("lru_w_a", (4, HD, HD), None), ("lru_b_a", (LRU_W,), None),
         ("lru_w_x", (4, HD, HD), None), ("lru_b_x", (LRU_W,), None), ("lru_lambda", (LRU_W,), None),
         ("attn_sinks", (ATT_H,), None), ("dn_conv_w", (4, 3 * DN_W), 3 * DN_W // N_DEV),
         ("dn_a_log", (DN_H,), None), ("dn_dt_bias", (DN_H,), None), ("dn_norm", (HD,), None),
         ("ffn2_norm", (D,), None), ("ple_norm", (D,), None))
SINGLE = (("rel_bias", (REL_BUCKETS, ATT_H)), ("final_norm", (D,)))


def _pack_rows(arrs, width, mult):
    flat = jnp.concatenate([a.reshape(-1) for a in arrs])
    rows = -(-flat.shape[0] // (width * mult)) * mult
    return jnp.pad(flat, (0, rows * width - flat.shape[0])).reshape(rows, width)


def _unpack_rows(packed, shapes):
    flat = packed.reshape(-1)
    out, off = [], 0
    for s in shapes:
        n = int(np.prod(s))
        out.append(flat[off:off + n].reshape(s))
        off += n
    return out


def _pad_rows(w, r):
    return jnp.pad(w, ((0, r - w.shape[0]), (0, 0)))


def _shard_ffn(a, l, which, more=()):
    cols = jnp.concatenate([a[which + "_w_gate"][l], a[which + "_w_up"][l]], axis=0)
    rows = jnp.concatenate([_pad_rows(a[which + "_w_down"][l], SHP)] + list(more), axis=0)
    return jnp.pad(cols, ((0, 0), (0, SHP - SH))).astype(BF16), rows.astype(BF16)


def _shards(a, l):
    w_in_rows = _pad_rows(a["w_in"][l].reshape(WIN_ROWS, D), IN_ROWS).astype(BF16)
    rest = _shard_ffn(a, l, "ffn2", (a["w_out"][l], a["ple_w_gate"][l], a["ple_w_proj"][l].reshape(-1, D)))
    return _shard_ffn(a, l, "ffn1"), (w_in_rows,), rest


def _full_w_in(in_rows):
    sh = in_rows[:, :WIN_ROWS, :].reshape(N_DEV, D, D_IN // N_DEV)
    return jnp.pad(sh.transpose(1, 0, 2).reshape(D, D_IN), ((0, 0), (0, D_IN_PAD - D_IN)))


def _full_ple_proj(r_rows):
    sh = r_rows[:, R_PPROJ:R_ROWS, :].reshape(N_DEV, PLE, D // N_DEV)
    return sh.transpose(1, 0, 2).reshape(PLE, D)


PIECE_NAMES = {"ffn1": ("ffn1_w_gate", "ffn1_w_up", "ffn1_w_down"), "ffn2": ("ffn2_w_gate", "ffn2_w_up", "ffn2_w_down"),
               "mix": ("w_out", "ple_w_gate", "w_in", "ple_w_proj")}


def _shard_grads(piece, summed):
    if piece == "mix":
        rows, = summed
        return {"w_out": rows[GM_WOUT:GM_WOUT + ROWS_DEV], "ple_w_gate": rows[GM_PGATE:GM_PGATE + ROWS_DEV],
                "w_in": rows[GM_WIN:GM_WIN + WIN_ROWS].reshape(D, D_IN // N_DEV),
                "ple_w_proj": rows[GM_PPROJ:GM_END].reshape(PLE, D // N_DEV)}
    if piece in ("ffn1_gate", "ffn1_up"):
        return {piece.replace("_", "_w_"): summed[0][:, :SH]}
    if piece == "ffn1_down":
        return {"ffn1_w_down": summed[0][:SH]}
    cols, rows = summed
    return {piece + "_w_gate": cols[:D, :SH], piece + "_w_up": cols[D:, :SH], piece + "_w_down": rows[:SH]}


def kernel(x, p, ffn1_norm, ffn1_w_gate, ffn1_w_up, ffn1_w_down, mix_norm, w_in, lru_conv_w, lru_conv_b, lru_w_a, lru_b_a, lru_w_x, lru_b_x, lru_lambda, attn_sinks, rel_bias, dn_conv_w, dn_a_log, dn_dt_bias, dn_norm, w_out, ffn2_norm, ffn2_w_gate, ffn2_w_up, ffn2_w_down, ple_norm, ple_w_gate, ple_w_proj, final_norm, loss_target, m_ffn1_norm, m_ffn1_w_gate, m_ffn1_w_up, m_ffn1_w_down, m_mix_norm, m_w_in, m_lru_conv_w, m_lru_conv_b, m_lru_w_a, m_lru_b_a, m_lru_w_x, m_lru_b_x, m_lru_lambda, m_attn_sinks, m_rel_bias, m_dn_conv_w, m_dn_a_log, m_dn_dt_bias, m_dn_norm, m_w_out, m_ffn2_norm, m_ffn2_w_gate, m_ffn2_w_up, m_ffn2_w_down, m_ple_norm, m_ple_w_gate, m_ple_w_proj, m_final_norm, v_ffn1_norm, v_ffn1_w_gate, v_ffn1_w_up, v_ffn1_w_down, v_mix_norm, v_w_in, v_lru_conv_w, v_lru_conv_b, v_lru_w_a, v_lru_b_a, v_lru_w_x, v_lru_b_x, v_lru_lambda, v_attn_sinks, v_rel_bias, v_dn_conv_w, v_dn_a_log, v_dn_dt_bias, v_dn_norm, v_w_out, v_ffn2_norm, v_ffn2_w_gate, v_ffn2_w_up, v_ffn2_w_down, v_ple_norm, v_ple_w_gate, v_ple_w_proj, v_final_norm):
    a = dict(locals())
    nb, S, _ = x.shape
    T = nb * S
    my_slot = _slot_of(*_my_pos())

    W = {k: [None] * DEPTH for k in ("f1_cols", "f1_rows", "w_in", "r_cols", "r_rows", "ple_w_proj")}
    GATHER, SCATTER = ("gather_cols", "gather_block"), ("scatter_cols", "scatter_block")
    GROUP_MODES = {"f1": GATHER, "in": GATHER[1:], "rest": GATHER}

    def set_group(l, group, bufs):
        if group == "f1":
            W["f1_cols"][l], W["f1_rows"][l] = bufs
        elif group == "in":
            W["w_in"][l] = _full_w_in(bufs[0])
        else:
            W["r_cols"][l], W["r_rows"][l] = bufs
            W["ple_w_proj"][l] = _full_ple_proj(bufs[1])

    def landing(mode, src):
        if mode == "gather_cols":
            return lax.empty((src.shape[0], FFP), src.dtype)
        if mode == "scatter_cols":
            return lax.empty((N_DEV, src.shape[0], SHP), src.dtype)
        return lax.empty((N_DEV,) + src.shape[mode == "scatter_block":], src.dtype)

    def start(modes, srcs, name, after=()):
        return split_start(modes, srcs, [landing(m, s) for m, s in zip(modes, srcs)], name, after)

    shards0, shards1 = _shards(a, 0), _shards(a, 1)
    set_group(0, "f1", gather_layer(*shards0[0], "gather_weights_l0_ffn1"))
    taps = all_gather(_pack_rows([lru_conv_w, dn_conv_w], LANE, 8), "gather_conv_taps")
    flat_taps = taps.reshape(N_DEV, -1)
    for name, first, tap in (("lru_conv_w", 0, lru_conv_w), ("dn_conv_w", lru_conv_w.size, dn_conv_w)):
        per_dev = flat_taps[:, first:first + tap.size].reshape((N_DEV,) + tap.shape)
        W[name] = jnp.moveaxis(per_dev, 0, -2).reshape(tap.shape[:-1] + (N_DEV * tap.shape[-1],))
    for name, _, cols in SMALL:
        if cols is None:
            W[name] = a[name]
    W["rel_bias"], W["final_norm"] = rel_bias, final_norm

    gathers, after = {}, (W["f1_rows"][0], taps)
    for l, group, srcs in ((0, "in", shards0[1]), (0, "rest", shards0[2]),
                           (1, "f1", shards1[0]), (1, "in", shards1[1]), (1, "rest", shards1[2])):
        gathers[l, group], token = start(GROUP_MODES[group], srcs, f"gather_start_l{l}_{group}", after)
        after = (token,)
    W["ffn1_norm"] = ffn1_norm + token[0, 0]
    flight, tokens = {}, {}

    def need(l, group, h):
        if (l, group) in gathers:
            set_group(l, group, split_wait(GROUP_MODES[group], gathers[l, group], h, f"gather_wait_l{l}_{group}"))

    def piece_modes(piece):
        return {"mix": SCATTER[1:], "ffn1_gate": SCATTER[:1], "ffn1_up": SCATTER[:1], "ffn1_down": SCATTER[1:]}.get(
            piece, SCATTER)

    def on_piece(l, piece, bufs):
        bufs = [b.reshape(-1, FFP) if b.shape[-1] == FFP else b for b in bufs]
        flight[l, piece], tokens[l, piece] = start(piece_modes(piece), bufs, f"exchange_start_l{l}_{piece}")
        return tokens[l, piece][0, 0]

    loss_local, dx, grads, d_final = _core(x.reshape(T, D), p.reshape(DEPTH, T, PLE), W,
                                           loss_target.reshape(T, D), S, need, on_piece)
    loss = lax.psum(loss_local, AXES)

    small_full = [jnp.stack([grads[l][name] for l in range(DEPTH)]) for name, _, _ in SMALL]
    small_full += [grads[0]["rel_bias"] + grads[1]["rel_bias"], d_final]
    small_flight, _ = start(("gather_block",), (_pack_rows(small_full, LANE, 8),), "gather_start_small_grads",
                            (tokens[0, "ffn1_down"],))

    out = {}

    landed = {}

    def land(l, piece, after):
        landed[l, piece] = split_wait(piece_modes(piece), flight[l, piece], after, f"exchange_wait_l{l}_{piece}")

    def where(name, l):
        if name in ("w_out", "ple_w_gate"):
            return "mix", 0, (GM_WOUT if name == "w_out" else GM_PGATE), ROWS_DEV, D
        ffn, kind = name[:4], name[7:]
        one_by_one = (l, ffn) == (0, "ffn1")
        if kind == "down":
            return (ffn + "_down", 0, 0, SH, D) if one_by_one else (ffn, 1, 0, SH, D)
        if one_by_one:
            return f"{ffn}_{kind}", 0, 0, D, SH
        return ffn, 0, (0 if kind == "gate" else D), D, SH

    def update(piece):
        for name in PIECE_NAMES[piece]:
            if name in ("w_in", "ple_w_proj"):
                g = jnp.stack([_shard_grads("mix", [mix_sums[l]])[name] for l in range(DEPTH)])
            else:
                g = None
                for l in reversed(range(DEPTH)):
                    piece_l, idx, row0, rows, cols = where(name, l)
                    g = sum_into(landed[l, piece_l][idx], row0, rows, cols, l, g, f"sum_{name}_l{l}")
            out[name] = (g,) + tuple(adamw(g, a[name], a["m_" + name], a["v_" + name], "adamw_" + name))

    for l, piece in ((1, "ffn2"), (1, "mix"), (1, "ffn1"), (0, "ffn2"), (0, "mix")):
        land(l, piece, (dx, tokens[0, "ffn1_down"]))
    mix_sums = [sum_parts(landed[l, "mix"][0], f"sum_mix_grads_l{l}") for l in range(DEPTH)]
    update("ffn2")
    update("mix")
    done_early = tuple(out[n][1] for n in PIECE_NAMES["ffn2"] + PIECE_NAMES["mix"])
    small_parts, = split_wait(("gather_block",), small_flight, done_early, "gather_wait_small_grads")
    small_sum = sum_parts(small_parts, "sum_small_grads")
    g_small = dict(zip([n for n, _, _ in SMALL] + [n for n, _ in SINGLE],
                       _unpack_rows(small_sum, [s.shape for s in small_full])))
    for name, _, cols in SMALL:
        if cols is not None:
            g_small[name] = lax.dynamic_slice_in_dim(g_small[name], my_slot * cols, cols, axis=2)

    for n in [n for n, _, _ in SMALL] + [n for n, _ in SINGLE]:
        shape = a[n].shape
        flat = lambda t: t.reshape((-1, shape[-1]) if len(shape) > 1 else (1, -1))
        res = adamw(flat(g_small[n]), flat(a[n]), flat(a["m_" + n]), flat(a["v_" + n]), "adamw_" + n)
        out[n] = (g_small[n].reshape(shape),) + tuple(r.reshape(shape) for r in res)

    for piece in ("ffn1_gate", "ffn1_up", "ffn1_down"):
        land(0, piece, (out["final_norm"][1],) + done_early)
    update("ffn1")

    order = ['ffn1_norm', 'ffn1_w_gate', 'ffn1_w_up', 'ffn1_w_down', 'mix_norm', 'w_in', 'lru_conv_w', 'lru_conv_b',
             'lru_w_a', 'lru_b_a', 'lru_w_x', 'lru_b_x', 'lru_lambda', 'attn_sinks', 'rel_bias', 'dn_conv_w',
             'dn_a_log', 'dn_dt_bias', 'dn_norm', 'w_out', 'ffn2_norm', 'ffn2_w_gate', 'ffn2_w_up', 'ffn2_w_down',
             'ple_norm', 'ple_w_gate', 'ple_w_proj', 'final_norm']
    return (loss, dx.reshape(x.shape)) + tuple(out[n][k] for k in range(4) for n in order)
```

```python
import functools
import math

import numpy as np
import jax
import jax.numpy as jnp
from jax import lax
from jax.experimental import pallas as pl
from jax.experimental.pallas import tpu as pltpu

F32 = jnp.float32
BF16 = jnp.bfloat16
HI = lax.Precision.HIGHEST

D = 1024
DEPTH = 2
EPS = 1e-6
PLE = 256
FF = 2816
HD = 64
LRU_W = 256
LRU_C = 8.0
ATT_W = 512
ATT_H = 8
ATT_KV = 2
ATT_G = 4
KV_W = 128
WINDOW = 128
BQ = 128
REL_BUCKETS = 32
REL_MAX_DIST = 128
DN_W = 256
DN_H = 4
CHUNK = 64
D_IN = 2312
D_IN_PAD = 2432
N_DEV = 8

ADAM_LR = 0.001
ADAM_B1 = 0.9
ADAM_B2 = 0.999
ADAM_EPS = 1e-08
ADAM_WD = 0.01
ADAM_STEP = 10

LANE = 128
VMEM_LIMIT = 56 * 1024 * 1024
SH = FF // N_DEV
SHP = 384
FFP = N_DEV * SHP
FF_TILE = 2 * SHP
FF_SUB = 256
TOK_TILE = 512
LIGHT_TILE = 1024
R_DOWN2, R_WOUT, R_PGATE, R_PPROJ, R_ROWS = 0, 384, 512, 640, 672
WIN_ROWS = D * D_IN // N_DEV // 1024
IN_ROWS = 304
NEG = -1e30


def _cp(*sem):
    return pltpu.CompilerParams(dimension_semantics=tuple(sem), vmem_limit_bytes=VMEM_LIMIT)


def _dg(a, b, ca, cb, exact):
    dims = (((ca,), (cb,)), ((), ()))
    if exact == "f32":
        return lax.dot_general(a.astype(F32), b.astype(F32), dims, precision=HI, preferred_element_type=F32)
    if exact == "split":
        a_hi, b_hi = a.astype(BF16), b.astype(BF16)
        a_lo = (a - a_hi.astype(F32)).astype(BF16)
        b_lo = (b - b_hi.astype(F32)).astype(BF16)
        dot = lambda u, v: lax.dot_general(u, v, dims, preferred_element_type=F32)
        return dot(a_hi, b_hi) + (dot(a_hi, b_lo) + dot(a_lo, b_hi))
    return lax.dot_general(a.astype(BF16), b.astype(BF16), dims, preferred_element_type=F32)


def _make_mm(exact):
    @jax.custom_vjp
    def mm(a, b):
        return _dg(a, b, 1, 0, exact)

    @jax.custom_vjp
    def mm_nt(a, b):
        return _dg(a, b, 1, 1, exact)

    @jax.custom_vjp
    def mm_tn(a, b):
        return _dg(a, b, 0, 0, exact)

    mm.defvjp(lambda a, b: (mm(a, b), (a, b)),
              lambda r, d: (mm_nt(d, r[1]), mm_tn(r[0], d)))
    mm_nt.defvjp(lambda a, b: (mm_nt(a, b), (a, b)),
                 lambda r, d: (mm(d, r[1]), mm_tn(d, r[0])))
    mm_tn.defvjp(lambda a, b: (mm_tn(a, b), (a, b)),
                 lambda r, d: (mm_nt(r[1], d), mm(r[0], d)))
    return mm, mm_nt, mm_tn


_mm, _mm_nt, _mm_tn = _make_mm("bf16")
_mmx, _mmx_nt, _mmx_tn = _make_mm("f32")
_mm3, _mm3_nt, _mm3_tn = _make_mm("split")


def _iota(shape, dim):
    return lax.broadcasted_iota(jnp.int32, shape, dim)


def _sigmoid(x):
    return 0.5 * jnp.tanh(0.5 * x) + 0.5


def _rms(h, g):
    rstd = lax.rsqrt(jnp.mean(h * h, axis=-1, keepdims=True) + EPS)
    xhat = h * rstd
    return xhat * g, xhat, rstd


def _rms_bwd(dxn, xhat, rstd, g):
    dxhat = dxn * g
    dh = rstd * (dxhat - xhat * jnp.mean(dxhat * xhat, axis=-1, keepdims=True))
    dg = jnp.sum(dxn * xhat, axis=0, keepdims=True)
    return dh, dg


def _row_spec(tm, n):
    return pl.BlockSpec((tm, n), lambda i, *_: (i, 0))


def _full_spec(shape):
    nd = len(shape)
    return pl.BlockSpec(shape, lambda *_: (0,) * nd)


def _ffn_weight_specs():
    return [pl.BlockSpec((D, FF_TILE), lambda i, j: (0, j)),
            pl.BlockSpec((D, FF_TILE), lambda i, j: (1, j)),
            pl.BlockSpec((2, SHP, D), lambda i, j: (j, 0, 0))]


def ffn_fwd(h, g, wa, wb, name):
    T = h.shape[0]
    tm = min(2 * TOK_TILE, T)
    nj = FFP // FF_TILE

    def body(h_ref, g_ref, wg_ref, wu_ref, wd_ref, o_ref, gt_ref, up_ref, xn_ref):
        j = pl.program_id(1)

        @pl.when(j == 0)
        def _():
            hh = h_ref[...]
            xn_ref[...] = _rms(hh, g_ref[...])[0].astype(BF16)
            o_ref[...] = hh

        blocks = [slice(c, c + FF_SUB) for c in range(0, FF_TILE, FF_SUB)]
        xn = xn_ref[...]
        wd = wd_ref[...].reshape(FF_TILE, D)
        gt = [_mm(xn, wg_ref[:, c]) for c in blocks]
        up = [_mm(xn, wu_ref[:, c]) for c in blocks]
        act = [t * _sigmoid(t) * u for t, u in zip(gt, up)]
        down = [_mm(act[k], wd[c]) for k, c in enumerate(blocks)]
        for k, c in enumerate(blocks):
            gt_ref[:, c] = gt[k].astype(BF16)
            up_ref[:, c] = up[k].astype(BF16)
        o_ref[...] += 0.5 * functools.reduce(lambda x, y: x + y, down)

    tile = pl.BlockSpec((tm, FF_TILE), lambda i, j: (i, j))
    return pl.pallas_call(
        body, name=name, grid=(T // tm, nj),
        in_specs=[pl.BlockSpec((tm, D), lambda i, j: (i, 0)),
                  pl.BlockSpec((1, D), lambda i, j: (0, 0))] + _ffn_weight_specs(),
        out_specs=[pl.BlockSpec((tm, D), lambda i, j: (i, 0)), tile, tile,
                   pl.BlockSpec((tm, D), lambda i, j: (i, 0))],
        out_shape=[jax.ShapeDtypeStruct((T, D), F32), jax.ShapeDtypeStruct((T, FFP), BF16),
                   jax.ShapeDtypeStruct((T, FFP), BF16), jax.ShapeDtypeStruct((T, D), BF16)],
        compiler_params=_cp("parallel", "arbitrary"),
    )(h, g, wa, wa, wb)


def ffn_bwd(h, dy, g, gt_saved, up_saved, wa, wb, name):
    T = h.shape[0]
    tm = min(TOK_TILE, T)
    nj = FFP // FF_TILE

    def body(h_ref, dy_ref, g_ref, gt_ref, up_ref, wg_ref, wu_ref, wd_ref,
             dh_ref, dg_ref, du_ref, a_ref, dn_ref, dxn_s, dyh_s):
        i = pl.program_id(0)
        j = pl.program_id(1)

        @pl.when(j == 0)
        def _():
            dxn_s[...] = jnp.zeros_like(dxn_s)
            dyh_s[...] = (0.5 * dy_ref[...]).astype(BF16)

        @pl.when((i == 0) & (j == 0))
        def _():
            dn_ref[...] = jnp.zeros_like(dn_ref)

        blocks = [slice(c, c + FF_SUB) for c in range(0, FF_TILE, FF_SUB)]
        wd = wd_ref[...].reshape(FF_TILE, D)
        dyh = dyh_s[...]
        gt = [gt_ref[:, c].astype(F32) for c in blocks]
        up = [up_ref[:, c].astype(F32) for c in blocks]
        da = [_mm_nt(dyh, wd[c]) for c in blocks]
        sg = [_sigmoid(t) for t in gt]
        si = [t * s for t, s in zip(gt, sg)]
        dup = [d * s for d, s in zip(da, si)]
        dgt = [d * u * (s * (1.0 + t * (1.0 - s))) for d, u, s, t in zip(da, up, sg, gt)]
        dxn = [_mm_nt(dgt[k], wg_ref[:, c]) + _mm_nt(dup[k], wu_ref[:, c]) for k, c in enumerate(blocks)]
        for k, c in enumerate(blocks):
            dg_ref[:, c] = dgt[k].astype(BF16)
            du_ref[:, c] = dup[k].astype(BF16)
            a_ref[:, c] = (si[k] * up[k]).astype(BF16)
        dxn_s[...] += functools.reduce(lambda x, y: x + y, dxn)

        @pl.when(j == nj - 1)
        def _():
            gg = g_ref[...]
            _, xhat, rstd = _rms(h_ref[...], gg)
            dh, dn = _rms_bwd(dxn_s[...], xhat, rstd, gg)
            dh_ref[...] = dy_ref[...] + dh
            dn_ref[...] += dn

    tile = pl.BlockSpec((tm, FF_TILE), lambda i, j: (i, j))
    return pl.pallas_call(
        body, name=name, grid=(T // tm, nj),
        in_specs=[pl.BlockSpec((tm, D), lambda i, j: (i, 0)),
                  pl.BlockSpec((tm, D), lambda i, j: (i, 0)),
                  pl.BlockSpec((1, D), lambda i, j: (0, 0)), tile, tile] + _ffn_weight_specs(),
        out_specs=[pl.BlockSpec((tm, D), lambda i, j: (i, 0)), tile, tile, tile,
                   pl.BlockSpec((1, D), lambda i, j: (0, 0))],
        out_shape=[jax.ShapeDtypeStruct((T, D), F32)] + [jax.ShapeDtypeStruct((T, FFP), BF16)] * 3
        + [jax.ShapeDtypeStruct((1, D), F32)],
        scratch_shapes=[pltpu.VMEM((tm, D), F32), pltpu.VMEM((tm, D), BF16)],
        compiler_params=_cp("arbitrary", "arbitrary"),
    )(h, dy, g, gt_saved, up_saved, wa, wa, wb)


def _pick(n, prefs):
    for t in prefs:
        if n % t == 0:
            return t
    return n


def _tn_body(nk, scale, out_dtype, squeeze):
    def body(a_ref, b_ref, *rest):
        o_ref, acc = rest[-2], rest[-1]
        k = pl.program_id(2)

        @pl.when(k == 0)
        def _():
            acc[...] = jnp.zeros_like(acc)

        acc[...] += _mm_tn(a_ref[...], b_ref[...])

        @pl.when(k == nk - 1)
        def _():
            res = (scale * acc[...]).astype(out_dtype)
            if squeeze:
                o_ref[0] = res
            else:
                o_ref[...] = res

    return body


def matmul_tn(a, b, name, scale=1.0, out_dtype=BF16):
    T, M = a.shape
    N = b.shape[1]
    tmm = _pick(M, (512, 256))
    tnn = _pick(N, (1024, 2432))
    tk = min(2 * TOK_TILE, T)
    nk = T // tk
    return pl.pallas_call(
        _tn_body(nk, scale, out_dtype, False), name=name, grid=(M // tmm, N // tnn, nk),
        in_specs=[pl.BlockSpec((tk, tmm), lambda i, j, k: (k, i)),
                  pl.BlockSpec((tk, tnn), lambda i, j, k: (k, j))],
        out_specs=pl.BlockSpec((tmm, tnn), lambda i, j, k: (i, j)),
        out_shape=jax.ShapeDtypeStruct((M, N), out_dtype),
        scratch_shapes=[pltpu.VMEM((tmm, tnn), F32)],
        compiler_params=_cp("parallel", "parallel", "arbitrary"),
    )(a, b)


def grad_cols(a, b, dst, slot, name):
    T = a.shape[0]
    tmm, tnn = D, FFP // 2
    tk = min(2 * TOK_TILE, T)
    nk = T // tk
    return pl.pallas_call(
        _tn_body(nk, 1.0, BF16, True), name=name, grid=(D // tmm, FFP // tnn, nk),
        in_specs=[pl.BlockSpec((tk, tmm), lambda i, j, k: (k, i)),
                  pl.BlockSpec((tk, tnn), lambda i, j, k: (k, j)),
                  pl.BlockSpec(memory_space=pl.ANY)],
        out_specs=pl.BlockSpec((1, tmm, tnn), lambda i, j, k: (slot, i, j)),
        out_shape=jax.ShapeDtypeStruct(dst.shape, dst.dtype),
        scratch_shapes=[pltpu.VMEM((tmm, tnn), F32)],
        input_output_aliases={2: 0},
        compiler_params=_cp("parallel", "parallel", "arbitrary"),
    )(a, b, dst)


def grad_cols_pair(a, b0, b1, name):
    T = a.shape[0]
    tnn = FF_TILE
    tk = min(2 * TOK_TILE, T)
    nk = T // tk

    def body(a_ref, b0_ref, b1_ref, o_ref, acc):
        k = pl.program_id(1)

        @pl.when(k == 0)
        def _():
            acc[...] = jnp.zeros_like(acc)

        acc[...] += _mm_tn(a_ref[...], jnp.concatenate([b0_ref[...], b1_ref[...]], axis=1))

        @pl.when(k == nk - 1)
        def _():
            o_ref[0] = acc[:, :tnn].astype(BF16)
            o_ref[1] = acc[:, tnn:].astype(BF16)

    b_spec = pl.BlockSpec((tk, tnn), lambda j, k: (k, j))
    return pl.pallas_call(
        body, name=name, grid=(FFP // tnn, nk),
        in_specs=[pl.BlockSpec((tk, D), lambda j, k: (k, 0)), b_spec, b_spec],
        out_specs=pl.BlockSpec((2, D, tnn), lambda j, k: (0, 0, j)),
        out_shape=jax.ShapeDtypeStruct((2, D, FFP), BF16),
        scratch_shapes=[pltpu.VMEM((D, 2 * tnn), F32)],
        compiler_params=_cp("parallel", "arbitrary"),
    )(a, b0, b1)


def grad_rows(a, b, dst, row0, rows, name, scale=1.0):
    T = a.shape[0]
    tk = min(2 * TOK_TILE, T)
    nk = T // tk
    blk = row0 // rows

    def body(a_ref, b_ref, dst_ref, o_ref, acc):
        k = pl.program_id(0)

        @pl.when(k == 0)
        def _():
            acc[...] = jnp.zeros_like(acc)

        acc[...] += _mm_tn(a_ref[...], b_ref[...])

        @pl.when(k == nk - 1)
        def _():
            o_ref[...] = (scale * acc[...]).astype(BF16).reshape(N_DEV, rows, D)

    return pl.pallas_call(
        body, name=name, grid=(nk,),
        in_specs=[pl.BlockSpec((tk, N_DEV * rows), lambda k: (k, 0)),
                  pl.BlockSpec((tk, D), lambda k: (k, 0)),
                  pl.BlockSpec(memory_space=pl.ANY)],
        out_specs=pl.BlockSpec((N_DEV, rows, D), lambda k: (0, blk, 0)),
        out_shape=jax.ShapeDtypeStruct(dst.shape, dst.dtype),
        scratch_shapes=[pltpu.VMEM((N_DEV * rows, D), F32)],
        input_output_aliases={2: 0},
        compiler_params=_cp("arbitrary"),
    )(a, b, dst)


U_SPLITS = (512, 768, 1024, 128)
U_OFFS = (0, 512, 1280, 2304)


def mixin_fwd(h, g, w_in, name):
    T = h.shape[0]
    tm = min(TOK_TILE, T)

    def body(h_ref, g_ref, w_ref, u0, u1, u2, u3, xn_ref):
        xn = _rms(h_ref[...], g_ref[...])[0].astype(BF16)
        xn_ref[...] = xn
        u = _mm(xn, w_ref[...])
        for ref, off, n in zip((u0, u1, u2, u3), U_OFFS, U_SPLITS):
            ref[...] = u[:, off:off + n]

    return pl.pallas_call(
        body, name=name, grid=(T // tm,),
        in_specs=[_row_spec(tm, D), _full_spec((1, D)), _full_spec((D, D_IN_PAD))],
        out_specs=[_row_spec(tm, n) for n in U_SPLITS] + [_row_spec(tm, D)],
        out_shape=[jax.ShapeDtypeStruct((T, n), F32) for n in U_SPLITS]
        + [jax.ShapeDtypeStruct((T, D), BF16)],
        compiler_params=_cp("parallel"),
    )(h, g, w_in)


DU_SPLITS = (256, 256, 768, 768, 256, 128)
DU_OFFS = (0, 256, 512, 1280, 2048, 2304)


def mixin_bwd(h, dh_in, g, w_in, dus, name):
    T = h.shape[0]
    tm = min(TOK_TILE, T)

    def body(h_ref, dhi_ref, g_ref, w_ref, *refs):
        dh_ref, du_ref, dn_ref = refs[-3:]

        @pl.when(pl.program_id(0) == 0)
        def _():
            dn_ref[...] = jnp.zeros_like(dn_ref)

        for ref, off, n in zip(refs[:-3], DU_OFFS, DU_SPLITS):
            du_ref[:, off:off + n] = ref[...].astype(BF16)
        dxn = _mm_nt(du_ref[...], w_ref[...])
        gg = g_ref[...]
        _, xhat, rstd = _rms(h_ref[...], gg)
        dh, dn = _rms_bwd(dxn, xhat, rstd, gg)
        dh_ref[...] = dhi_ref[...] + dh
        dn_ref[...] += dn

    return pl.pallas_call(
        body, name=name, grid=(T // tm,),
        in_specs=[_row_spec(tm, D), _row_spec(tm, D), _full_spec((1, D)), _full_spec((D, D_IN_PAD))]
        + [_row_spec(tm, n) for n in DU_SPLITS],
        out_specs=[_row_spec(tm, D), _row_spec(tm, D_IN_PAD), _full_spec((1, D))],
        out_shape=[jax.ShapeDtypeStruct((T, D), F32), jax.ShapeDtypeStruct((T, D_IN_PAD), BF16),
                   jax.ShapeDtypeStruct((1, D), F32)],
        compiler_params=_cp("arbitrary"),
    )(h, dh_in, g, w_in, *dus)


def _shift_down(x, s, row):
    if s == 0:
        return x
    return jnp.where(row >= s, pltpu.roll(x, s, 0), 0.0)


def _shift_up(x, s, row):
    if s == 0:
        return x
    n = x.shape[0]
    return jnp.where(row < n - s, pltpu.roll(x, n - s, 0), 0.0)


def conv_fwd(x, w, b, S, col0, C, name):
    T = x.shape[0]
    cb0 = col0 // LANE

    def body(x_ref, w_ref, b_ref, y_ref):
        xx = x_ref[...]
        row = _iota(xx.shape, 0)
        y = xx * w_ref[3:4, :] + b_ref[...]
        for k in range(3):
            y += _shift_down(xx, 3 - k, row) * w_ref[k:k + 1, :]
        y_ref[...] = y

    return pl.pallas_call(
        body, name=name, grid=(T // S, C // LANE),
        in_specs=[pl.BlockSpec((S, LANE), lambda s, c: (s, cb0 + c)),
                  pl.BlockSpec((4, LANE), lambda s, c: (0, c)),
                  pl.BlockSpec((1, LANE), lambda s, c: (0, c))],
        out_specs=pl.BlockSpec((S, LANE), lambda s, c: (s, c)),
        out_shape=jax.ShapeDtypeStruct((T, C), F32),
        compiler_params=_cp("parallel", "parallel"),
    )(x, w, b)


def conv_bwd(x, dy, w, S, col0, C, name):
    T = x.shape[0]
    cb0 = col0 // LANE

    def body(x_ref, dy_ref, w_ref, dx_ref, dwb_ref):
        @pl.when(pl.program_id(1) == 0)
        def _():
            dwb_ref[...] = jnp.zeros_like(dwb_ref)

        xx = x_ref[...]
        dd = dy_ref[...]
        row = _iota(xx.shape, 0)
        dx = dd * w_ref[3:4, :]
        for k in range(3):
            dx += _shift_up(dd, 3 - k, row) * w_ref[k:k + 1, :]
        dx_ref[...] = dx
        for k in range(4):
            dwb_ref[k:k + 1, :] += jnp.sum(dd * _shift_down(xx, 3 - k, row), axis=0, keepdims=True)
        dwb_ref[4:5, :] += jnp.sum(dd, axis=0, keepdims=True)

    return pl.pallas_call(
        body, name=name, grid=(C // LANE, T // S),
        in_specs=[pl.BlockSpec((S, LANE), lambda c, s: (s, cb0 + c)),
                  pl.BlockSpec((S, LANE), lambda c, s: (s, c)),
                  pl.BlockSpec((4, LANE), lambda c, s: (0, c))],
        out_specs=[pl.BlockSpec((S, LANE), lambda c, s: (s, c)),
                   pl.BlockSpec((8, LANE), lambda c, s: (0, c))],
        out_shape=[jax.ShapeDtypeStruct((T, C), F32), jax.ShapeDtypeStruct((8, C), F32)],
        compiler_params=_cp("parallel", "arbitrary"),
    )(x, dy, w)


def _scan(a, b, row):
    n = a.shape[0]
    d = 1
    while d < n:
        keep = row >= d
        b = a * jnp.where(keep, pltpu.roll(b, d, 0), 0.0) + b
        a = a * jnp.where(keep, pltpu.roll(a, d, 0), 1.0)
        d *= 2
    return b


def _rscan(a, b, row):
    n = a.shape[0]
    d = 1
    while d < n:
        keep = row < n - d
        b = a * jnp.where(keep, pltpu.roll(b, n - d, 0), 0.0) + b
        a = a * jnp.where(keep, pltpu.roll(a, n - d, 0), 1.0)
        d *= 2
    return b


GELU_C = math.sqrt(2.0 / math.pi)


def _gelu(x):
    t = jnp.tanh(GELU_C * (x + 0.044715 * (x * x * x)))
    return 0.5 * x * (1.0 + t), t


def _lru_gates(xr, wa, ba, wx, bx, lam):
    r = _sigmoid(_mm(xr, wa) + ba)
    i = _sigmoid(_mm(xr, wx) + bx)
    sp = jnp.maximum(-lam, 0.0) + jnp.log(1.0 + jnp.exp(-jnp.abs(lam)))
    la = -LRU_C * r * sp
    a = jnp.exp(la)
    e2 = a * a
    m = jnp.sqrt(-jnp.tanh(la) * (e2 + 1.0))
    return r, i, sp, a, e2, m


def lru_fwd(xr, u_lru, wa, wx, vec, S, name):
    T = xr.shape[0]

    def body(xr_ref, gt_ref, wa_ref, wx_ref, vec_ref, y_ref):
        x = xr_ref[...]
        row = _iota(x.shape, 0)
        r, i, sp, a, e2, m = _lru_gates(x, wa_ref[...], vec_ref[0:1, :], wx_ref[...], vec_ref[1:2, :],
                                        vec_ref[2:3, :])
        hh = _scan(a, m * (i * x), row)
        y_ref[...] = _gelu(gt_ref[...])[0] * hh

    return pl.pallas_call(
        body, name=name, grid=(T // S, LRU_W // LANE),
        in_specs=[pl.BlockSpec((S, LANE), lambda s, c: (s, c)),
                  pl.BlockSpec((S, LANE), lambda s, c: (s, 2 + c)),
                  pl.BlockSpec((LANE, LANE), lambda s, c: (c, c)),
                  pl.BlockSpec((LANE, LANE), lambda s, c: (c, c)),
                  pl.BlockSpec((8, LANE), lambda s, c: (0, c))],
        out_specs=pl.BlockSpec((S, LANE), lambda s, c: (s, c)),
        out_shape=jax.ShapeDtypeStruct((T, LRU_W), F32),
        compiler_params=_cp("parallel", "parallel"),
    )(xr, u_lru, wa, wx, vec)


def lru_bwd(xr, u_lru, dy, wa, wx, vec, S, name):
    T = xr.shape[0]

    def body(xr_ref, gt_ref, dy_ref, wa_ref, wx_ref, vec_ref,
             dxr_ref, dgt_ref, dwa_ref, dwx_ref, dvec_ref):
        @pl.when(pl.program_id(1) == 0)
        def _():
            dwa_ref[...] = jnp.zeros_like(dwa_ref)
            dwx_ref[...] = jnp.zeros_like(dwx_ref)
            dvec_ref[...] = jnp.zeros_like(dvec_ref)

        x = xr_ref[...]
        n = x.shape[0]
        row = _iota(x.shape, 0)
        lam = vec_ref[2:3, :]
        r, i, sp, a, e2, m = _lru_gates(x, wa_ref[...], vec_ref[0:1, :], wx_ref[...], vec_ref[1:2, :], lam)
        v = i * x
        hh = _scan(a, m * v, row)
        gt = gt_ref[...]
        dy = dy_ref[...]
        ge, t = _gelu(gt)
        dgt_ref[...] = dy * hh * (0.5 * (1.0 + t) + 0.5 * gt * (1.0 - t * t) * GELU_C
                                  * (1.0 + 3.0 * 0.044715 * gt * gt))
        a_next = jnp.where(row < n - 1, pltpu.roll(a, n - 1, 0), 0.0)
        G = _rscan(a_next, dy * ge, row)
        da = G * _shift_down(hh, 1, row)
        dv = G * m
        dla = da * a - (G * v) * e2 / m
        dr = dla * (-LRU_C * sp)
        dsp = jnp.sum(dla * (-LRU_C * r), axis=0, keepdims=True)
        dra = dr * r * (1.0 - r)
        dia = (dv * x) * i * (1.0 - i)
        dxr_ref[...] = dv * i + _mm_nt(dra, wa_ref[...]) + _mm_nt(dia, wx_ref[...])
        dwa_ref[0] += _mm_tn(x, dra)
        dwx_ref[0] += _mm_tn(x, dia)
        dvec_ref[0:1, :] += jnp.sum(dra, axis=0, keepdims=True)
        dvec_ref[1:2, :] += jnp.sum(dia, axis=0, keepdims=True)
        dvec_ref[2:3, :] += dsp * (-_sigmoid(-lam))

    return pl.pallas_call(
        body, name=name, grid=(LRU_W // LANE, T // S),
        in_specs=[pl.BlockSpec((S, LANE), lambda c, s: (s, c)),
                  pl.BlockSpec((S, LANE), lambda c, s: (s, 2 + c)),
                  pl.BlockSpec((S, LANE), lambda c, s: (s, c)),
                  pl.BlockSpec((LANE, LANE), lambda c, s: (c, c)),
                  pl.BlockSpec((LANE, LANE), lambda c, s: (c, c)),
                  pl.BlockSpec((8, LANE), lambda c, s: (0, c))],
        out_specs=[pl.BlockSpec((S, LANE), lambda c, s: (s, c)),
                   pl.BlockSpec((S, LANE), lambda c, s: (s, c)),
                   pl.BlockSpec((1, LANE, LANE), lambda c, s: (c, 0, 0)),
                   pl.BlockSpec((1, LANE, LANE), lambda c, s: (c, 0, 0)),
                   pl.BlockSpec((8, LANE), lambda c, s: (0, c))],
        out_shape=[jax.ShapeDtypeStruct((T, LRU_W), F32), jax.ShapeDtypeStruct((T, LRU_W), F32),
                   jax.ShapeDtypeStruct((2, LANE, LANE), F32), jax.ShapeDtypeStruct((2, LANE, LANE), F32),
                   jax.ShapeDtypeStruct((8, LRU_W), F32)],
        compiler_params=_cp("parallel", "arbitrary"),
    )(xr, u_lru, dy, wa, wx, vec)


def _bucket_table():
    qi = np.arange(BQ)[:, None]
    kj = np.arange(2 * BQ)[None, :]
    dist = BQ + qi - kj
    band = (dist >= 0) & (dist < WINDOW)
    dd = np.maximum(dist, 0)
    max_exact = REL_BUCKETS // 2
    large = max_exact + (np.log(np.maximum(dd, 1).astype(np.float32) / np.float32(max_exact))
                         / np.float32(math.log(REL_MAX_DIST / max_exact))
                         * np.float32(REL_BUCKETS - max_exact)).astype(np.int32)
    large = np.minimum(large, REL_BUCKETS - 1)
    bucket = np.where(dd < max_exact, dd, large)
    return np.where(band, bucket, -1).astype(np.int32)


def _att_specs(S):
    nb = S // BQ
    qc = ATT_W // LANE
    return [pl.BlockSpec((BQ, ATT_W), lambda b, n: (b * nb + n, 0)),
            pl.BlockSpec((BQ, KV_W), lambda b, n: (b * nb + jnp.maximum(n - 1, 0), qc)),
            pl.BlockSpec((BQ, KV_W), lambda b, n: (b * nb + n, qc)),
            pl.BlockSpec((BQ, KV_W), lambda b, n: (b * nb + jnp.maximum(n - 1, 0), qc + 1)),
            pl.BlockSpec((BQ, KV_W), lambda b, n: (b * nb + n, qc + 1))]


def _att_bias(bk, rb_ref, bias_s):
    for h in range(ATT_H):
        acc = jnp.zeros(bk.shape, F32)
        for bb in range(REL_BUCKETS):
            acc = jnp.where(bk == bb, rb_ref[bb * ATT_H + h], acc)
        bias_s[h] = acc


def _att_probs(qs, kgs, bias_s, valid, sk_ref):
    heads = range(ATT_H)
    s = [_mm_nt(qs[h], kgs[h // ATT_G]) for h in heads]
    s = [jnp.where(valid, s[h] * (HD ** -0.5) + bias_s[h], NEG) for h in heads]
    m = [jnp.maximum(jnp.max(s[h], axis=-1, keepdims=True), sk_ref[h]) for h in heads]
    e = [jnp.exp(s[h] - m[h]) for h in heads]
    es = [jnp.exp(sk_ref[h] - m[h]) for h in heads]
    den = [jnp.sum(e[h], axis=-1, keepdims=True) + es[h] for h in heads]
    return [e[h] / den[h] for h in heads], [es[h] / den[h] for h in heads]


def _att_kv(kp_ref, kc_ref, vp_ref, vc_ref):
    cat = lambda a, b, g: jnp.concatenate([a[:, HD * g:HD * (g + 1)], b[:, HD * g:HD * (g + 1)]], axis=0)
    return ([cat(kp_ref, kc_ref, g) for g in range(ATT_KV)], [cat(vp_ref, vc_ref, g) for g in range(ATT_KV)])


def attn_fwd(u_att, sinks, rel_bias, S, name):
    T = u_att.shape[0]
    nb = S // BQ
    table = jnp.asarray(_bucket_table())

    def body(sk_ref, rb_ref, bk_ref, q_ref, kp_ref, kc_ref, vp_ref, vc_ref, o_ref, bias_s):
        b = pl.program_id(0)
        n = pl.program_id(1)
        bk = bk_ref[...]

        @pl.when((b == 0) & (n == 0))
        def _():
            _att_bias(bk, rb_ref, bias_s)

        valid = (bk >= 0) & ((n > 0) | (_iota(bk.shape, 1) >= BQ))
        kgs, vgs = _att_kv(kp_ref, kc_ref, vp_ref, vc_ref)
        p, _ = _att_probs([q_ref[:, HD * h:HD * (h + 1)] for h in range(ATT_H)], kgs, bias_s, valid, sk_ref)
        outs = [_mm(p[h], vgs[h // ATT_G]) for h in range(ATT_H)]
        for h in range(ATT_H):
            o_ref[:, HD * h:HD * (h + 1)] = outs[h]

    smem = pl.BlockSpec(memory_space=pltpu.SMEM)
    return pl.pallas_call(
        body, name=name, grid=(T // S, nb),
        in_specs=[smem, smem, _full_spec((BQ, 2 * BQ))] + _att_specs(S),
        out_specs=pl.BlockSpec((BQ, ATT_W), lambda b, n: (b * nb + n, 0)),
        out_shape=jax.ShapeDtypeStruct((T, ATT_W), F32),
        scratch_shapes=[pltpu.VMEM((ATT_H, BQ, 2 * BQ), F32)],
        compiler_params=_cp("arbitrary", "arbitrary"),
    )(sinks, rel_bias, table, u_att, u_att, u_att, u_att, u_att)


def attn_bwd(u_att, dy, sinks, rel_bias, S, name):
    T = u_att.shape[0]
    nb = S // BQ
    nB = T // S
    table = jnp.asarray(_bucket_table())
    scale = HD ** -0.5

    def body(sk_ref, rb_ref, bk_ref, q_ref, kp_ref, kc_ref, vp_ref, vc_ref, dy_ref,
             du_ref, drel_ref, dsk_ref, bias_s, dbias_s):
        b = pl.program_id(0)
        n = pl.program_id(1)
        bk = bk_ref[...]

        @pl.when((b == 0) & (n == 0))
        def _():
            _att_bias(bk, rb_ref, bias_s)
            dbias_s[...] = jnp.zeros_like(dbias_s)
            dsk_ref[...] = jnp.zeros_like(dsk_ref)
            drel_ref[...] = jnp.zeros_like(drel_ref)

        @pl.when(n == 0)
        def _():
            du_ref[...] = jnp.zeros_like(du_ref)

        valid = (bk >= 0) & ((n > 0) | (_iota(bk.shape, 1) >= BQ))
        r_cur = pl.multiple_of(n * BQ, BQ)
        r_prev = pl.multiple_of(jnp.maximum(n - 1, 0) * BQ, BQ)
        heads = range(ATT_H)
        kgs, vgs = _att_kv(kp_ref, kc_ref, vp_ref, vc_ref)
        qs = [q_ref[:, HD * h:HD * (h + 1)] for h in heads]
        dos = [dy_ref[:, HD * h:HD * (h + 1)] for h in heads]
        p, ps = _att_probs(qs, kgs, bias_s, valid, sk_ref)
        dp = [_mm_nt(dos[h], vgs[h // ATT_G]) for h in heads]
        delta = [jnp.sum(p[h] * dp[h], axis=-1, keepdims=True) for h in heads]
        ds = [p[h] * (dp[h] - delta[h]) for h in heads]
        dss = [ds[h] * scale for h in heads]
        dq = [_mm(dss[h], kgs[h // ATT_G]) for h in heads]
        dks = [_mm_tn(dss[h], qs[h]) for h in heads]
        dvs = [_mm_tn(p[h], dos[h]) for h in heads]
        for h in heads:
            dbias_s[h] += ds[h]
            dsk_ref[h:h + 1, :] += jnp.broadcast_to(jnp.sum(-ps[h] * delta[h], axis=0, keepdims=True), (1, LANE))
            du_ref[pl.ds(r_cur, BQ), HD * h:HD * (h + 1)] = dq[h]
        for g in range(ATT_KV):
            of_group = range(g * ATT_G, (g + 1) * ATT_G)
            dk = functools.reduce(lambda x, y: x + y, [dks[h] for h in of_group])
            dv = functools.reduce(lambda x, y: x + y, [dvs[h] for h in of_group])
            ck = ATT_W + HD * g
            cv = ATT_W + KV_W + HD * g
            du_ref[pl.ds(r_prev, BQ), ck:ck + HD] += dk[0:BQ]
            du_ref[pl.ds(r_cur, BQ), ck:ck + HD] += dk[BQ:]
            du_ref[pl.ds(r_prev, BQ), cv:cv + HD] += dv[0:BQ]
            du_ref[pl.ds(r_cur, BQ), cv:cv + HD] += dv[BQ:]

        @pl.when((b == nB - 1) & (n == nb - 1))
        def _():
            lane = _iota((1, LANE), 1)
            for h in range(ATT_H):
                db = dbias_s[h]
                acc = jnp.zeros((1, LANE), F32)
                for bb in range(REL_BUCKETS):
                    val = jnp.sum(jnp.sum(jnp.where(bk == bb, db, 0.0), axis=1, keepdims=True),
                                  axis=0, keepdims=True)
                    acc = jnp.where(lane == bb, val, acc)
                drel_ref[h:h + 1, :] = acc

    smem = pl.BlockSpec(memory_space=pltpu.SMEM)
    return pl.pallas_call(
        body, name=name, grid=(nB, nb),
        in_specs=[smem, smem, _full_spec((BQ, 2 * BQ))] + _att_specs(S)
        + [pl.BlockSpec((BQ, ATT_W), lambda b, n: (b * nb + n, 0))],
        out_specs=[pl.BlockSpec((S, ATT_W + 2 * KV_W), lambda b, n: (b, 0)),
                   _full_spec((8, LANE)), _full_spec((8, LANE))],
        out_shape=[jax.ShapeDtypeStruct((T, ATT_W + 2 * KV_W), F32),
                   jax.ShapeDtypeStruct((8, LANE), F32), jax.ShapeDtypeStruct((8, LANE), F32)],
        scratch_shapes=[pltpu.VMEM((ATT_H, BQ, 2 * BQ), F32), pltpu.VMEM((ATT_H, BQ, 2 * BQ), F32)],
        compiler_params=_cp("arbitrary", "arbitrary"),
    )(sinks, rel_bias, table, u_att, u_att, u_att, u_att, u_att, dy)


def _head_of(i):
    return lax.shift_right_logical(i, 6)


def _head_mask(shape):
    return (_head_of(_iota(shape, 0)) == _head_of(_iota(shape, 1))).astype(F32)


def _dn_point(c, uba, alog, dtb):
    s = c * _sigmoid(c)
    qt, kt, vt = s[:, 0:256], s[:, 256:512], s[:, 512:768]
    ones_bd = _head_mask((DN_W, DN_W))
    q = qt * lax.rsqrt(_mm3(qt * qt, ones_bd) + EPS) * (HD ** -0.5)
    k = kt * lax.rsqrt(_mm3(kt * kt, ones_bd) + EPS)
    sel = _head_of(_iota((LANE, DN_W), 1))
    row = _iota((LANE, DN_W), 0)
    braw = _mm3(uba, (row == sel).astype(F32))
    araw = _mm3(uba, (row == sel + DN_H).astype(F32)) + dtb
    beta = _sigmoid(braw)
    g = -jnp.exp(alog) * (jnp.maximum(araw, 0.0) + jnp.log(1.0 + jnp.exp(-jnp.abs(araw))))
    return q, k, vt, g, beta


def dn_point_fwd(c, uba, alog, dtb, name):
    T = c.shape[0]
    tm = min(TOK_TILE, T)

    def body(c_ref, u_ref, al_ref, dt_ref, *outs):
        for ref, val in zip(outs, _dn_point(c_ref[...], u_ref[...], al_ref[...], dt_ref[...])):
            ref[...] = val

    return pl.pallas_call(
        body, name=name, grid=(T // tm,),
        in_specs=[_row_spec(tm, 768), _row_spec(tm, LANE), _full_spec((1, DN_W)), _full_spec((1, DN_W))],
        out_specs=[_row_spec(tm, DN_W)] * 5,
        out_shape=[jax.ShapeDtypeStruct((T, DN_W), F32)] * 5,
        compiler_params=_cp("parallel"),
    )(c, uba, alog, dtb)


def dn_point_bwd(c, uba, alog, dtb, douts, name):
    T = c.shape[0]
    tm = min(TOK_TILE, T)

    def body(c_ref, u_ref, al_ref, dt_ref, dq, dk, dv, dg, db, dc_ref, du_ref, dvec_ref):
        @pl.when(pl.program_id(0) == 0)
        def _():
            dvec_ref[...] = jnp.zeros_like(dvec_ref)

        _, vjp = jax.vjp(_dn_point, c_ref[...], u_ref[...], al_ref[...], dt_ref[...])
        dc, du, dal, ddt = vjp((dq[...], dk[...], dv[...], dg[...], db[...]))
        dc_ref[...] = dc
        du_ref[...] = du
        fold = (_iota((LANE, DN_W), 0) == _head_of(_iota((LANE, DN_W), 1))).astype(F32)
        both = jnp.concatenate([dal, ddt, jnp.zeros((6, DN_W), F32)], axis=0)
        dvec_ref[...] += _mmx_nt(both, fold)

    return pl.pallas_call(
        body, name=name, grid=(T // tm,),
        in_specs=[_row_spec(tm, 768), _row_spec(tm, LANE), _full_spec((1, DN_W)), _full_spec((1, DN_W))]
        + [_row_spec(tm, DN_W)] * 5,
        out_specs=[_row_spec(tm, 768), _row_spec(tm, LANE), _full_spec((8, LANE))],
        out_shape=[jax.ShapeDtypeStruct((T, 768), F32), jax.ShapeDtypeStruct((T, LANE), F32),
                   jax.ShapeDtypeStruct((8, LANE), F32)],
        compiler_params=_cp("arbitrary"),
    )(c, uba, alog, dtb, *douts)


def _unit_lower_inverses(lmats):
    eye = (_iota(lmats[0].shape, 0) == _iota(lmats[0].shape, 1)).astype(F32)
    tinvs = [eye - lm for lm in lmats]
    pws = list(lmats)
    for _ in range(5):
        pws = [_mm3(pw, pw) for pw in pws]
        tinvs = [t + _mm3(t, pw) for t, pw in zip(tinvs, pws)]
    return tuple(tinvs)


def _inverse_bwd(tinv, d):
    return -_mm3_nt(_mm3_tn(tinv, d), tinv)


@jax.custom_vjp
def _tri_invs(lmats):
    return _unit_lower_inverses(lmats)


def _tri_invs_fwd(lmats):
    tinvs = _unit_lower_inverses(lmats)
    return tinvs, tinvs


_tri_invs.defvjp(_tri_invs_fwd, lambda tinvs, ds: (tuple(_inverse_bwd(t, d) for t, d in zip(tinvs, ds)),))


@jax.custom_vjp
def _tri_inv_known(lmat, tinv):
    return tinv


_tri_inv_known.defvjp(lambda lmat, tinv: (tinv, tinv),
                      lambda tinv, d: (_inverse_bwd(tinv, d), jnp.zeros_like(tinv)))


DN_SUB = 4


def _dn_stack(x):
    return jnp.concatenate([x, x, x, x], axis=0) * _head_mask((DN_W, DN_W))


def _dn_pre_inverse(q, k, v, g, beta):
    hm = _head_mask((DN_W, DN_W))
    ri = _iota((DN_W, DN_W), 0) & (CHUNK - 1)
    ci = _iota((DN_W, DN_W), 1) & (CHUNK - 1)
    tri64 = (_iota((CHUNK, CHUNK), 0) >= _iota((CHUNK, CHUNK), 1)).astype(F32)
    gc = _mm3(tri64, g)
    ks = _dn_stack(k)
    gcol = jnp.sum(_dn_stack(gc), axis=1, keepdims=True) * (1.0 / HD)
    gmat = jnp.broadcast_to(gcol, (DN_W, DN_W))
    decay = jnp.exp(jnp.minimum(gmat - gmat.T, 0.0))
    lmat = _mm_nt(_dn_stack(k * beta), ks) * decay * (hm * (ri > ci).astype(F32))
    att = _mm_nt(_dn_stack(q), ks) * decay * (hm * (ri >= ci).astype(F32))
    return lmat, att, gc


def _dn_post_inverse(q, k, v, g, beta, tinv, att, gc):
    glast = jnp.sum(g, axis=0, keepdims=True)
    eg = jnp.exp(gc)
    u = _mm(tinv, _dn_stack(v * beta))
    w = _mm(tinv, _dn_stack(k * beta * eg))
    return u, w, att, _dn_stack(q * eg), _dn_stack(k * jnp.exp(glast - gc)), jnp.exp(glast), tinv


def _dn_apply(state, prep):
    u, w, att, qe, kd, eglast, _ = prep
    vn = u - _mm(w, state)
    o4 = _mm(qe, state) + _mm(att, vn)
    o = o4[0:64] + o4[64:128] + o4[128:192] + o4[192:256]
    return o, state * eglast + _mm_tn(kd, vn)


def _dn_chunks(states, q, k, v, g, beta, knowns=None):
    nb = len(q)
    n = q[0].shape[0] // CHUNK
    chunks = [[tuple(x[b][c * CHUNK:(c + 1) * CHUNK] for x in (q, k, v, g, beta)) for c in range(n)]
              for b in range(nb)]
    pre = [[_dn_pre_inverse(*ch) for ch in seq] for seq in chunks]
    lmats = [p[0] for seq in pre for p in seq]
    if knowns is None:
        flat = _tri_invs(tuple(lmats))
    else:
        flat = [_tri_inv_known(lm, kn) for lm, kn in zip(lmats, [kn for seq in knowns for kn in seq])]
    tinvs = [flat[b * n:(b + 1) * n] for b in range(nb)]
    preps = [[_dn_post_inverse(*chunks[b][c], tinvs[b][c], pre[b][c][1], pre[b][c][2]) for c in range(n)]
             for b in range(nb)]
    states = list(states)
    outs = [[] for _ in range(nb)]
    for c in range(n):
        for b in range(nb):
            o, states[b] = _dn_apply(states[b], preps[b][c])
            outs[b].append(o)
    return tuple(jnp.concatenate(o, axis=0) for o in outs), tuple(states), tinvs


def _dn_scan_specs(nb, S, reverse):
    rows = DN_SUB * CHUNK
    ns = S // rows
    at = (lambda t: ns - 1 - t) if reverse else (lambda t: t)
    return (pl.BlockSpec((nb, rows, DN_W), lambda t: (0, at(t), 0)),
            pl.BlockSpec((nb, 1, DN_W, DN_W), lambda t: (0, at(t), 0, 0)),
            pl.BlockSpec((nb, DN_SUB, DN_W, DN_W), lambda t: (0, at(t), 0, 0)))


def dn_scan_fwd(q, k, v, g, beta, S, name):
    T = q.shape[0]
    nb = T // S
    ns = S // (DN_SUB * CHUNK)
    seqs = range(nb)

    def body(q_ref, k_ref, v_ref, g_ref, b_ref, o_ref, st_ref, ti_ref, s_s):
        @pl.when(pl.program_id(0) == 0)
        def _():
            s_s[...] = jnp.zeros_like(s_s)

        per = lambda ref: tuple(ref[b] for b in seqs)
        sts = per(s_s)
        for b in seqs:
            st_ref[b, 0] = sts[b]
        outs, news, tinvs = _dn_chunks(sts, per(q_ref), per(k_ref), per(v_ref), per(g_ref), per(b_ref))
        for b in seqs:
            o_ref[b] = outs[b]
            s_s[b] = news[b]
            for c, tinv in enumerate(tinvs[b]):
                ti_ref[b, c] = tinv

    spec, st_spec, ti_spec = _dn_scan_specs(nb, S, False)
    o, states, tinvs = pl.pallas_call(
        body, name=name, grid=(ns,),
        in_specs=[spec] * 5,
        out_specs=[spec, st_spec, ti_spec],
        out_shape=[jax.ShapeDtypeStruct((nb, S, DN_W), F32),
                   jax.ShapeDtypeStruct((nb, ns, DN_W, DN_W), F32),
                   jax.ShapeDtypeStruct((nb, S // CHUNK, DN_W, DN_W), F32)],
        scratch_shapes=[pltpu.VMEM((nb, DN_W, DN_W), F32)],
        compiler_params=_cp("arbitrary"),
    )(*(t.reshape(nb, S, DN_W) for t in (q, k, v, g, beta)))
    return o.reshape(T, DN_W), states, tinvs


def dn_scan_bwd(q, k, v, g, beta, states, tinvs, do, S, name):
    T = q.shape[0]
    nb = T // S
    ns = S // (DN_SUB * CHUNK)
    seqs = range(nb)

    def body(q_ref, k_ref, v_ref, g_ref, b_ref, st_ref, ti_ref, do_ref, dq, dk, dv, dg, db, ds_s):
        @pl.when(pl.program_id(0) == 0)
        def _():
            ds_s[...] = jnp.zeros_like(ds_s)

        per = lambda ref: tuple(ref[b] for b in seqs)
        knowns = [[ti_ref[b, c] for c in range(DN_SUB)] for b in seqs]
        _, vjp = jax.vjp(lambda *args: _dn_chunks(*args, knowns=knowns)[:2],
                         tuple(st_ref[b, 0] for b in seqs), per(q_ref), per(k_ref), per(v_ref), per(g_ref),
                         per(b_ref))
        grads = vjp((per(do_ref), per(ds_s)))
        for b in seqs:
            ds_s[b] = grads[0][b]
            for ref, val in zip((dq, dk, dv, dg, db), grads[1:]):
                ref[b] = val[b]

    spec, st_spec, ti_spec = _dn_scan_specs(nb, S, True)
    res = pl.pallas_call(
        body, name=name, grid=(ns,),
        in_specs=[spec] * 5 + [st_spec, ti_spec, spec],
        out_specs=[spec] * 5,
        out_shape=[jax.ShapeDtypeStruct((nb, S, DN_W), F32)] * 5,
        scratch_shapes=[pltpu.VMEM((nb, DN_W, DN_W), F32)],
        compiler_params=_cp("arbitrary"),
    )(*(t.reshape(nb, S, DN_W) for t in (q, k, v, g, beta)), states, tinvs, do.reshape(nb, S, DN_W))
    return [r.reshape(T, DN_W) for r in res]


def _dn_gate(o, z, nl):
    ms = _mm3(o * o, _head_mask((DN_W, DN_W))) * (1.0 / HD)
    return o * lax.rsqrt(ms + EPS) * nl * (z * _sigmoid(z))


def dn_gate_fwd(o, u_dn, nl, name):
    T = o.shape[0]
    tm = min(LIGHT_TILE, T)

    def body(o_ref, z_ref, n_ref, y_ref):
        y_ref[...] = _dn_gate(o_ref[...], z_ref[...], n_ref[...])

    return pl.pallas_call(
        body, name=name, grid=(T // tm,),
        in_specs=[_row_spec(tm, DN_W), pl.BlockSpec((tm, DN_W), lambda i: (i, 3)), _full_spec((1, DN_W))],
        out_specs=_row_spec(tm, DN_W),
        out_shape=jax.ShapeDtypeStruct((T, DN_W), F32),
        compiler_params=_cp("parallel"),
    )(o, u_dn, nl)


def dn_gate_bwd(o, u_dn, nl, dy, name):
    T = o.shape[0]
    tm = min(LIGHT_TILE, T)

    def body(o_ref, z_ref, n_ref, dy_ref, do_ref, dz_ref, dn_ref):
        @pl.when(pl.program_id(0) == 0)
        def _():
            dn_ref[...] = jnp.zeros_like(dn_ref)

        _, vjp = jax.vjp(_dn_gate, o_ref[...], z_ref[...], n_ref[...])
        do, dz, dn = vjp(dy_ref[...])
        do_ref[...] = do
        dz_ref[...] = dz
        fold = (_iota((LANE, DN_W), 0) == (_iota((LANE, DN_W), 1) & (HD - 1))).astype(F32)
        dn_ref[...] += _mmx_nt(jnp.concatenate([dn, jnp.zeros((7, DN_W), F32)], axis=0), fold)

    return pl.pallas_call(
        body, name=name, grid=(T // tm,),
        in_specs=[_row_spec(tm, DN_W), pl.BlockSpec((tm, DN_W), lambda i: (i, 3)), _full_spec((1, DN_W)),
                  _row_spec(tm, DN_W)],
        out_specs=[_row_spec(tm, DN_W), _row_spec(tm, DN_W), _full_spec((8, LANE))],
        out_shape=[jax.ShapeDtypeStruct((T, DN_W), F32), jax.ShapeDtypeStruct((T, DN_W), F32),
                   jax.ShapeDtypeStruct((8, LANE), F32)],
        compiler_params=_cp("arbitrary"),
    )(o, u_dn, nl, dy)


Y_SPLITS = (LRU_W, ATT_W, DN_W)
Y_OFFS = (0, LRU_W, LRU_W + ATT_W)


ROWS_DEV = D // N_DEV


def _dev_rows_spec(row0):
    return pl.BlockSpec((N_DEV, ROWS_DEV, D), lambda *_: (0, row0 // ROWS_DEV, 0))


def _dev_rows(w_ref, off, n):
    return w_ref[off // ROWS_DEV:(off + n) // ROWS_DEV].reshape(n, D)


def wout_fwd(h, ys, wb, name):
    T = h.shape[0]
    tm = min(LIGHT_TILE, T)

    def body(h_ref, y0, y1, y2, w_ref, o_ref, yc_ref):
        for ref, off, n in zip((y0, y1, y2), Y_OFFS, Y_SPLITS):
            yc_ref[:, off:off + n] = ref[...].astype(BF16)
        o_ref[...] = h_ref[...] + _mm(yc_ref[...], _dev_rows(w_ref, 0, D))

    return pl.pallas_call(
        body, name=name, grid=(T // tm,),
        in_specs=[_row_spec(tm, D)] + [_row_spec(tm, n) for n in Y_SPLITS] + [_dev_rows_spec(R_WOUT)],
        out_specs=[_row_spec(tm, D), _row_spec(tm, D)],
        out_shape=[jax.ShapeDtypeStruct((T, D), F32), jax.ShapeDtypeStruct((T, D), BF16)],
        compiler_params=_cp("parallel"),
    )(h, *ys, wb)


def wout_bwd(dy, wb, name):
    T = dy.shape[0]
    tm = min(LIGHT_TILE, T)

    def body(dy_ref, w_ref, d0, d1, d2):
        dys = _mm_nt(dy_ref[...], _dev_rows(w_ref, 0, D))
        for ref, off, n in zip((d0, d1, d2), Y_OFFS, Y_SPLITS):
            ref[...] = dys[:, off:off + n]

    return pl.pallas_call(
        body, name=name, grid=(T // tm,),
        in_specs=[_row_spec(tm, D), _dev_rows_spec(R_WOUT)],
        out_specs=[_row_spec(tm, n) for n in Y_SPLITS],
        out_shape=[jax.ShapeDtypeStruct((T, n), F32) for n in Y_SPLITS],
        compiler_params=_cp("parallel"),
    )(dy, wb)


def ple_fwd(h, g, pe, wg, wp, name):
    T = h.shape[0]
    tm = min(LIGHT_TILE, T)

    def body(h_ref, g_ref, p_ref, wg_ref, wp_ref, o_ref):
        hh = h_ref[...]
        xn = _rms(hh, g_ref[...])[0]
        o_ref[...] = hh + _sigmoid(_mm(xn, _dev_rows(wg_ref, 0, D))) * _mm(p_ref[...], wp_ref[...])

    return pl.pallas_call(
        body, name=name, grid=(T // tm,),
        in_specs=[_row_spec(tm, D), _full_spec((1, D)), _row_spec(tm, PLE), _dev_rows_spec(R_PGATE),
                  _full_spec((PLE, D))],
        out_specs=_row_spec(tm, D),
        out_shape=jax.ShapeDtypeStruct((T, D), F32),
        compiler_params=_cp("parallel"),
    )(h, g, pe, wg, wp)


def ple_bwd(h, dy, g, pe, wg, wp, name):
    T = h.shape[0]
    tm = min(TOK_TILE, T)

    def body(h_ref, dy_ref, g_ref, p_ref, wg_ref, wp_ref, dh_ref, dz_ref, dpp_ref, xn_ref, dn_ref):
        @pl.when(pl.program_id(0) == 0)
        def _():
            dn_ref[...] = jnp.zeros_like(dn_ref)

        gg = g_ref[...]
        dy = dy_ref[...]
        xn, xhat, rstd = _rms(h_ref[...], gg)
        wg = _dev_rows(wg_ref, 0, D)
        gate = _sigmoid(_mm(xn, wg))
        pp = _mm(p_ref[...], wp_ref[...])
        dz = dy * pp * gate * (1.0 - gate)
        dz_ref[...] = dz.astype(BF16)
        dpp_ref[...] = (dy * gate).astype(BF16)
        xn_ref[...] = xn.astype(BF16)
        dh, dn = _rms_bwd(_mm_nt(dz, wg), xhat, rstd, gg)
        dh_ref[...] = dy + dh
        dn_ref[...] += dn

    return pl.pallas_call(
        body, name=name, grid=(T // tm,),
        in_specs=[_row_spec(tm, D), _row_spec(tm, D), _full_spec((1, D)), _row_spec(tm, PLE),
                  _dev_rows_spec(R_PGATE), _full_spec((PLE, D))],
        out_specs=[_row_spec(tm, D), _row_spec(tm, D), _row_spec(tm, D), _row_spec(tm, D), _full_spec((1, D))],
        out_shape=[jax.ShapeDtypeStruct((T, D), F32), jax.ShapeDtypeStruct((T, D), BF16),
                   jax.ShapeDtypeStruct((T, D), BF16), jax.ShapeDtypeStruct((T, D), BF16),
                   jax.ShapeDtypeStruct((1, D), F32)],
        compiler_params=_cp("arbitrary"),
    )(h, dy, g, pe, wg, wp)


def loss_head(h, g, target, name):
    T = h.shape[0]
    tm = min(LIGHT_TILE, T)

    def body(h_ref, g_ref, t_ref, loss_ref, dh_ref, dn_ref):
        @pl.when(pl.program_id(0) == 0)
        def _():
            dn_ref[...] = jnp.zeros_like(dn_ref)
            loss_ref[...] = jnp.zeros_like(loss_ref)

        gg = g_ref[...]
        y, xhat, rstd = _rms(h_ref[...], gg)
        err = y - t_ref[...]
        per_tok = jnp.mean(err * err, axis=-1, keepdims=True)
        loss_ref[...] += 0.5 * jnp.sum(per_tok, axis=0, keepdims=True)
        dh, dn = _rms_bwd(err * (1.0 / D), xhat, rstd, gg)
        dh_ref[...] = dh
        dn_ref[...] += dn

    return pl.pallas_call(
        body, name=name, grid=(T // tm,),
        in_specs=[_row_spec(tm, D), _full_spec((1, D)), _row_spec(tm, D)],
        out_specs=[_full_spec((8, LANE)), _row_spec(tm, D), _full_spec((1, D))],
        out_shape=[jax.ShapeDtypeStruct((8, LANE), F32), jax.ShapeDtypeStruct((T, D), F32),
                   jax.ShapeDtypeStruct((1, D), F32)],
        compiler_params=_cp("arbitrary"),
    )(h, g, target)


def _block_diag(w):
    return jnp.einsum('hij,hk->hikj', w, jnp.eye(4, dtype=w.dtype)).reshape(LRU_W, LRU_W)


def _layer_consts(W, l):
    row = lambda v: v.reshape(1, -1)
    zeros = jnp.zeros((5, LRU_W), F32)
    return dict(
        wa=_block_diag(W["lru_w_a"][l]), wx=_block_diag(W["lru_w_x"][l]),
        lru_vec=jnp.concatenate([row(W["lru_b_a"][l]), row(W["lru_b_x"][l]), row(W["lru_lambda"][l]), zeros], 0),
        lru_cb=row(W["lru_conv_b"][l]),
        sinks=W["attn_sinks"][l], rel=W["rel_bias"].reshape(-1),
        dn_cb=jnp.zeros((1, 3 * DN_W), F32),
        alog=row(jnp.repeat(W["dn_a_log"][l], HD)), dtb=row(jnp.repeat(W["dn_dt_bias"][l], HD)),
        dn_nl=row(jnp.tile(W["dn_norm"][l], DN_H)),
    )


def _layer_fwd(h0, pe, W, l, S, need=None):
    n = f"l{l}_"
    c_ = _layer_consts(W, l)
    row = lambda v: v.reshape(1, -1)
    need = need or (lambda *_: None)
    need(l, "f1", h0)
    h1, *ffn1_kept = ffn_fwd(h0, row(W["ffn1_norm"][l]), W["f1_cols"][l], W["f1_rows"][l], n + "ffn1_fwd")
    need(l, "in", h1)
    u_lru, u_att, u_dn, u_ba, xn_mix = mixin_fwd(h1, row(W["mix_norm"][l]), W["w_in"][l], n + "mixin_fwd")
    xr = conv_fwd(u_lru, W["lru_conv_w"][l], c_["lru_cb"], S, 0, LRU_W, n + "lru_conv_fwd")
    y_lru = lru_fwd(xr, u_lru, c_["wa"], c_["wx"], c_["lru_vec"], S, n + "lru_fwd")
    y_att = attn_fwd(u_att, c_["sinks"], c_["rel"], S, n + "attn_fwd")
    cc = conv_fwd(u_dn, W["dn_conv_w"][l], c_["dn_cb"], S, 0, 3 * DN_W, n + "dn_conv_fwd")
    q, k, v, g, beta = dn_point_fwd(cc, u_ba, c_["alog"], c_["dtb"], n + "dn_point_fwd")
    o, states, tinvs = dn_scan_fwd(q, k, v, g, beta, S, n + "dn_scan_fwd")
    y_dn = dn_gate_fwd(o, u_dn, c_["dn_nl"], n + "dn_gate_fwd")
    need(l, "rest", y_dn)
    h2, ycat = wout_fwd(h1, (y_lru, y_att, y_dn), W["r_rows"][l], n + "wout_fwd")
    h3, *ffn2_kept = ffn_fwd(h2, row(W["ffn2_norm"][l]), W["r_cols"][l], W["r_rows"][l], n + "ffn2_fwd")
    h4 = ple_fwd(h3, row(W["ple_norm"][l]), pe, W["r_rows"][l], W["ple_w_proj"][l], n + "ple_fwd")
    saved = dict(ffn1=ffn1_kept, ffn2=ffn2_kept, h0=h0, h1=h1, h2=h2, h3=h3, u_lru=u_lru, u_att=u_att, u_dn=u_dn,
                 u_ba=u_ba, xn_mix=xn_mix, xr=xr, cc=cc, q=q, k=k, v=v, g=g, beta=beta, o=o, states=states, tinvs=tinvs, ycat=ycat)
    return h4, saved


GM_WOUT, GM_PGATE, GM_WIN, GM_PPROJ, GM_END, GM_ROWS = 0, 128, 256, 560, 592, 640


def _layer_bwd(dh4, sv, pe, W, l, S, token=None, on_piece=None):
    n = f"l{l}_"
    c_ = _layer_consts(W, l)
    row = lambda v: v.reshape(1, -1)
    behind = lambda v, tok: v if tok is None else v + tok.astype(v.dtype)
    on_piece = on_piece or (lambda *_: None)
    G = {"mix_rows": jnp.zeros((N_DEV, GM_ROWS, D), BF16)}
    dh3, dz, dpp, xn_p, dn = ple_bwd(sv["h3"], dh4, behind(row(W["ple_norm"][l]), token), pe, W["r_rows"][l],
                                     W["ple_w_proj"][l], n + "ple_bwd")
    G["ple_norm"] = dn[0]
    G["mix_rows"] = grad_rows(xn_p, dz, G["mix_rows"], GM_PGATE, ROWS_DEV, n + "d_ple_w_gate")
    d_proj = matmul_tn(pe, dpp, n + "d_ple_w_proj")
    d_proj = d_proj.reshape(PLE, N_DEV, D // N_DEV).transpose(1, 0, 2).reshape(N_DEV, GM_END - GM_PPROJ, D)
    G["mix_rows"] = lax.dynamic_update_slice(G["mix_rows"], d_proj, (0, GM_PPROJ, 0))

    def ffn_back(which, cols_w, rows_w, h_in, dy, tok, one_by_one):
        gt, up, xn = sv[which]
        dh, dgt, dup, act, dn_ = ffn_bwd(h_in, dy, behind(row(W[which + "_norm"][l]), tok), gt, up, cols_w, rows_w,
                                         n + which + "_bwd")
        G[which + "_norm"] = dn_[0]
        zeros_rows = jnp.zeros((N_DEV, SHP, D), BF16)
        if one_by_one:
            G[which + "_gate"] = grad_cols(xn, dgt, lax.empty((1, D, FFP), BF16), 0, n + "d_" + which + "_w_gate")
            tok = on_piece(l, which + "_gate", (G[which + "_gate"],))
            G[which + "_up"] = grad_cols(xn, dup, behind(jnp.zeros((1, D, FFP), BF16), tok), 0,
                                         n + "d_" + which + "_w_up")
            tok = on_piece(l, which + "_up", (G[which + "_up"],))
            G[which + "_down"] = grad_rows(act, dy, behind(zeros_rows, tok), 0, SHP, n + "d_" + which + "_w_down",
                                           scale=0.5)
            return dh, on_piece(l, which + "_down", (G[which + "_down"],))
        G[which + "_cols"] = grad_cols_pair(xn, dgt, dup, n + "d_" + which + "_w_gate_up")
        G[which + "_rows"] = grad_rows(act, dy, lax.empty((N_DEV, SHP, D), BF16), 0, SHP,
                                       n + "d_" + which + "_w_down", scale=0.5)
        return dh, on_piece(l, which, (G[which + "_cols"], G[which + "_rows"]))

    dh2, tok = ffn_back("ffn2", W["r_cols"][l], W["r_rows"][l], sv["h2"], dh3, None, False)
    dy_lru, dy_att, dy_dn = wout_bwd(dh2, W["r_rows"][l], n + "wout_bwd")
    G["mix_rows"] = grad_rows(sv["ycat"], dh2, G["mix_rows"], GM_WOUT, ROWS_DEV, n + "d_w_out")
    do, dz_dn, dnn = dn_gate_bwd(sv["o"], sv["u_dn"], behind(c_["dn_nl"], tok), dy_dn, n + "dn_gate_bwd")
    dqkvgb = dn_scan_bwd(sv["q"], sv["k"], sv["v"], sv["g"], sv["beta"], sv["states"], sv["tinvs"], do, S,
                         n + "dn_scan_bwd")
    dcc, du_ba, dvec_dn = dn_point_bwd(sv["cc"], sv["u_ba"], c_["alog"], c_["dtb"], dqkvgb, n + "dn_point_bwd")
    dqkv, dwb_dn = conv_bwd(sv["u_dn"], dcc, W["dn_conv_w"][l], S, 0, 3 * DN_W, n + "dn_conv_bwd")
    G["dn_norm"] = dnn[0, 0:HD]
    G["dn_a_log"] = dvec_dn[0, 0:DN_H]
    G["dn_dt_bias"] = dvec_dn[1, 0:DN_H]
    G["dn_conv_w"] = dwb_dn[0:4]
    du_att, drel, dsk = attn_bwd(sv["u_att"], dy_att, c_["sinks"], c_["rel"], S, n + "attn_bwd")
    G["attn_sinks"] = dsk[:, 0]
    G["rel_bias"] = drel[:, 0:REL_BUCKETS].T
    dxr, dgt_lru, dwa, dwx, dvec = lru_bwd(sv["xr"], sv["u_lru"], dy_lru, c_["wa"], c_["wx"], c_["lru_vec"], S,
                                           n + "lru_bwd")
    dx_lru, dwb_lru = conv_bwd(sv["u_lru"], dxr, W["lru_conv_w"][l], S, 0, LRU_W, n + "lru_conv_bwd")
    diag = lambda m: jnp.stack([m[c, HD * e:HD * (e + 1), HD * e:HD * (e + 1)] for c in range(2) for e in range(2)])
    G["lru_w_a"], G["lru_w_x"] = diag(dwa), diag(dwx)
    G["lru_b_a"], G["lru_b_x"], G["lru_lambda"] = dvec[0], dvec[1], dvec[2]
    G["lru_conv_w"], G["lru_conv_b"] = dwb_lru[0:4], dwb_lru[4]
    dh1, du_cat, dn = mixin_bwd(sv["h1"], dh2, row(W["mix_norm"][l]), W["w_in"][l],
                                (dx_lru, dgt_lru, du_att, dqkv, dz_dn, du_ba), n + "mixin_bwd")
    G["mix_norm"] = dn[0]
    d_in = matmul_tn(sv["xn_mix"], du_cat, n + "d_w_in")[:, :D_IN]
    d_in = d_in.reshape(D, N_DEV, D_IN // N_DEV).transpose(1, 0, 2).reshape(N_DEV, WIN_ROWS, D)
    d_in = jnp.pad(d_in, ((0, 0), (0, GM_PPROJ - GM_WIN - WIN_ROWS), (0, 0)))
    G["mix_rows"] = lax.dynamic_update_slice(G["mix_rows"], d_in, (0, GM_WIN, 0))
    tok = on_piece(l, "mix", (G["mix_rows"],))
    dh0, tok = ffn_back("ffn1", W["f1_cols"][l], W["f1_rows"][l], sv["h0"], dh1, tok, l == 0)
    return dh0, G, tok


def _core(x, pe, W, target, S, need=None, on_piece=None):
    h = x
    saved = []
    for l in range(DEPTH):
        h, sv = _layer_fwd(h, pe[l], W, l, S, need)
        saved.append(sv)
    loss_tile, dh, dfn = loss_head(h, W["final_norm"].reshape(1, -1), target, "loss_head")
    grads = [None] * DEPTH
    token = None
    for l in reversed(range(DEPTH)):
        dh, grads[l], token = _layer_bwd(dh, saved[l], pe[l], W, l, S, token, on_piece)
    return loss_tile[0, 0], dh, grads, dfn[0]


MESH_ID = pl.DeviceIdType.MESH
ANY_SPEC = pl.BlockSpec(memory_space=pl.ANY)
AXES = ("x", "y", "c")


def _my_pos():
    return lax.axis_index("x"), lax.axis_index("y"), lax.axis_index("c")


def _slot_of(px, py, pc):
    return 4 * px + 2 * py + pc


def all_gather(x, name):
    R, C = x.shape

    def body(x_ref, out_ref, send_sems, recv_sems, local_sem):
        mx, my, mc = _my_pos()
        me, sibling = (mx, my, mc), (mx, my, 1 - mc)
        chips = [(1 - mx, my), (mx, 1 - my), (1 - mx, 1 - my)]

        def copy(k, block, to, src=None):
            dst = out_ref.at[_slot_of(*block)]
            return pltpu.make_async_remote_copy(
                src_ref=dst if src is None else src, dst_ref=dst,
                send_sem=send_sems.at[k], recv_sem=recv_sems.at[k],
                device_id=to, device_id_type=MESH_ID)

        mine = pltpu.make_async_copy(x_ref, out_ref.at[_slot_of(*me)], local_sem)
        mine.start()
        first = [copy(0, me, sibling, src=x_ref)]
        first += [copy(1 + j, me, (*chip, mc), src=x_ref) for j, chip in enumerate(chips)]
        for cp in first:
            cp.start()
        passed = [copy(4 + j, (*chip, mc), sibling) for j, chip in enumerate(chips)]
        for j, chip in enumerate(chips):
            copy(1 + j, (*chip, mc), me).wait_recv()
            passed[j].start()
        copy(0, sibling, me).wait_recv()
        for j, chip in enumerate(chips):
            copy(4 + j, (*chip, 1 - mc), me).wait_recv()
        for cp in first + passed:
            cp.wait_send()
        mine.wait()

    return pl.pallas_call(
        body, name=name,
        out_shape=jax.ShapeDtypeStruct((N_DEV, R, C), x.dtype),
        in_specs=[ANY_SPEC], out_specs=ANY_SPEC,
        scratch_shapes=[pltpu.SemaphoreType.DMA((7,)), pltpu.SemaphoreType.DMA((7,)), pltpu.SemaphoreType.DMA],
    )(x)


def _col_window(ref, slot):
    return ref.at[:, pl.ds(pl.multiple_of(slot * SHP, LANE), SHP)]


def gather_layer(a_sh, b_sh, name):
    def body(a_ref, b_ref, ao_ref, bo_ref, send_sems, recv_sems, local_sems):
        mx, my, mc = _my_pos()
        me, sibling = (mx, my, mc), (mx, my, 1 - mc)
        chips = [(1 - mx, my), (mx, 1 - my), (1 - mx, 1 - my)]

        def copies(k, block, to, own=False):
            slot = _slot_of(*block)
            dsts = (_col_window(ao_ref, slot), bo_ref.at[slot])
            srcs = (a_ref, b_ref) if own else dsts
            return [pltpu.make_async_remote_copy(
                src_ref=s, dst_ref=d, send_sem=send_sems.at[2 * k + i], recv_sem=recv_sems.at[2 * k + i],
                device_id=to, device_id_type=MESH_ID) for i, (s, d) in enumerate(zip(srcs, dsts))]

        mine = [pltpu.make_async_copy(a_ref, _col_window(ao_ref, _slot_of(*me)), local_sems.at[0]),
                pltpu.make_async_copy(b_ref, bo_ref.at[_slot_of(*me)], local_sems.at[1])]
        for cp in mine:
            cp.start()
        first = copies(0, me, sibling, own=True)
        for j, chip in enumerate(chips):
            first += copies(1 + j, me, (*chip, mc), own=True)
        for cp in first:
            cp.start()
        passed = []
        for j, chip in enumerate(chips):
            for cp in copies(1 + j, (*chip, mc), me):
                cp.wait_recv()
            fwd = copies(4 + j, (*chip, mc), sibling)
            for cp in fwd:
                cp.start()
            passed += fwd
        for cp in copies(0, sibling, me):
            cp.wait_recv()
        for j, chip in enumerate(chips):
            for cp in copies(4 + j, (*chip, 1 - mc), me):
                cp.wait_recv()
        for cp in first + passed:
            cp.wait_send()
        for cp in mine:
            cp.wait()

    return pl.pallas_call(
        body, name=name,
        out_shape=[jax.ShapeDtypeStruct((a_sh.shape[0], FFP), a_sh.dtype),
                   jax.ShapeDtypeStruct((N_DEV,) + b_sh.shape, b_sh.dtype)],
        in_specs=[ANY_SPEC, ANY_SPEC], out_specs=[ANY_SPEC, ANY_SPEC],
        scratch_shapes=[pltpu.SemaphoreType.DMA((14,)), pltpu.SemaphoreType.DMA((14,)),
                        pltpu.SemaphoreType.DMA((2,))],
    )(a_sh, b_sh)


HBM_SPEC = pl.BlockSpec(memory_space=pltpu.HBM)
SEM_SPEC = pl.BlockSpec(memory_space=pltpu.SEMAPHORE)
SPLIT_EFFECT = pltpu.CompilerParams(has_side_effects=pltpu.SideEffectType.DATAFLOW_SIDE_EFFECTING)


def _split_ends(mode, src_ref, dst_ref, src_slot, dst_slot):
    cols = mode.endswith("cols")
    if mode.startswith("gather"):
        return src_ref, (_col_window(dst_ref, dst_slot) if cols else dst_ref.at[dst_slot])
    return (_col_window(src_ref, src_slot) if cols else src_ref.at[src_slot]), dst_ref.at[dst_slot]


def _split_peers():
    mx, my, mc = _my_pos()
    for r in range(1, N_DEV):
        peer = (1 - mx if r & 4 else mx, 1 - my if r & 2 else my, 1 - mc if r & 1 else mc)
        yield r - 1, peer, _slot_of(*peer)


def split_start(modes, srcs, dsts, name, after=()):
    n = len(modes)
    m = len(after)

    def body(*refs):
        send_sems, recv_sems, token = refs[2 * n + m], refs[2 * n + m + 1], refs[-1]
        mine = _slot_of(*_my_pos())
        for k, peer, ps in _split_peers():
            for i in range(n):
                src, dst = _split_ends(modes[i], refs[i], refs[n + i], ps, mine)
                pltpu.make_async_remote_copy(
                    src_ref=src, dst_ref=dst, send_sem=send_sems.at[n * k + i], recv_sem=recv_sems.at[n * k + i],
                    device_id=peer, device_id_type=MESH_ID).start()
        for i in range(n):
            src, dst = _split_ends(modes[i], refs[i], refs[n + i], mine, mine)
            pltpu.make_async_copy(src, dst, recv_sems.at[n * (N_DEV - 1) + i]).start()
        token[...] = jnp.zeros_like(token)

    bufs = tuple(srcs) + tuple(dsts)
    sems = pltpu.SemaphoreType.DMA((n * N_DEV,))
    res = pl.pallas_call(
        body, name=name,
        out_shape=(sems, sems) + tuple(pltpu.HBM(t.shape, t.dtype) for t in bufs)
        + (jax.ShapeDtypeStruct((8, LANE), F32),),
        in_specs=[HBM_SPEC] * (2 * n) + [ANY_SPEC] * m,
        out_specs=(SEM_SPEC, SEM_SPEC) + (HBM_SPEC,) * (2 * n) + (pl.BlockSpec(memory_space=pltpu.VMEM),),
        input_output_aliases={i: 2 + i for i in range(2 * n)},
        compiler_params=SPLIT_EFFECT,
    )(*(pltpu.with_memory_space_constraint(t, pltpu.HBM) for t in bufs), *after)
    return list(res[:-1]), res[-1]


def split_wait(modes, started, after, name):
    n = len(modes)
    after = tuple(after) if isinstance(after, (tuple, list)) else (after,)
    send_sems, recv_sems, bufs = started[0], started[1], started[2:]

    def body(*refs):
        send_sems, recv_sems = refs[2 * n], refs[2 * n + 1]
        mine = _slot_of(*_my_pos())
        for k, peer, ps in _split_peers():
            for i in range(n):
                sent = _split_ends(modes[i], refs[i], refs[n + i], ps, mine)[0]
                landed = _split_ends(modes[i], refs[i], refs[n + i], mine, ps)[1]
                cp = pltpu.make_async_remote_copy(
                    src_ref=sent, dst_ref=landed, send_sem=send_sems.at[n * k + i],
                    recv_sem=recv_sems.at[n * k + i], device_id=peer, device_id_type=MESH_ID)
                cp.wait_send()
                cp.wait_recv()
        for i in range(n):
            src, dst = _split_ends(modes[i], refs[i], refs[n + i], mine, mine)
            pltpu.make_async_copy(src, dst, recv_sems.at[n * (N_DEV - 1) + i]).wait()

    res = pl.pallas_call(
        body, name=name,
        out_shape=tuple(pltpu.HBM(t.shape, t.dtype) for t in bufs),
        in_specs=[HBM_SPEC] * (2 * n) + [SEM_SPEC, SEM_SPEC] + [ANY_SPEC] * len(after),
        out_specs=(HBM_SPEC,) * (2 * n),
        input_output_aliases={i: i for i in range(2 * n)},
        compiler_params=SPLIT_EFFECT,
    )(*bufs, send_sems, recv_sems, *after)
    return list(res[n:])


def sum_parts(parts, name):
    _, R, C = parts.shape
    tr = _pick(R, (512, 336, 272, 256, 128, 64, 32, 16, 8))

    def body(p_ref, o_ref):
        acc = p_ref[0].astype(F32)
        for k in range(1, N_DEV):
            acc += p_ref[k].astype(F32)
        o_ref[...] = acc

    return pl.pallas_call(
        body, name=name, grid=(R // tr,),
        in_specs=[pl.BlockSpec((N_DEV, tr, C), lambda i: (0, i, 0))],
        out_specs=pl.BlockSpec((tr, C), lambda i: (i, 0)),
        out_shape=jax.ShapeDtypeStruct((R, C), F32),
        compiler_params=_cp("parallel"),
    )(parts)


def sum_into(parts, row0, rows, cols, layer, dst, name):
    tr = _pick(rows, (512, 352, 128))
    blk0 = row0 // tr

    def body(p_ref, *rest):
        o_ref = rest[-1]
        acc = p_ref[0, :, 0:cols].astype(F32)
        for k in range(1, N_DEV):
            acc += p_ref[k, :, 0:cols].astype(F32)
        o_ref[0] = acc

    aliased = dst is not None
    return pl.pallas_call(
        body, name=name, grid=(rows // tr,),
        in_specs=[pl.BlockSpec((N_DEV, tr, parts.shape[2]), lambda i: (0, blk0 + i, 0))]
        + [pl.BlockSpec(memory_space=pl.ANY)] * aliased,
        out_specs=pl.BlockSpec((1, tr, cols), lambda i: (layer, i, 0)),
        out_shape=jax.ShapeDtypeStruct((DEPTH, rows, cols), F32),
        input_output_aliases={1: 0} if aliased else {},
        compiler_params=_cp("parallel"),
    )(*((parts, dst) if aliased else (parts,)))


def adamw(g, w, m, v, name):
    lead, (R, C) = g.shape[:-2], g.shape[-2:]
    tr = _pick(R, (512, 352, 256, 128, 64, 32, 16, 8))
    c1 = 1.0 - ADAM_B1 ** ADAM_STEP
    c2 = 1.0 - ADAM_B2 ** ADAM_STEP

    def body(g_ref, w_ref, m_ref, v_ref, d_ref, nm_ref, nv_ref):
        gg = g_ref[...]
        mm = ADAM_B1 * m_ref[...] + (1.0 - ADAM_B1) * gg
        vv = ADAM_B2 * v_ref[...] + (1.0 - ADAM_B2) * (gg * gg)
        nm_ref[...] = mm
        nv_ref[...] = vv
        d_ref[...] = -ADAM_LR * ((mm / c1) / (jnp.sqrt(vv / c2) + ADAM_EPS) + ADAM_WD * w_ref[...])

    if lead:
        spec = pl.BlockSpec((1, tr, C), lambda l, i: (l, i, 0))
    else:
        spec = pl.BlockSpec((tr, C), lambda l, i: (i, 0))
    return pl.pallas_call(
        body, name=name, grid=(lead[0] if lead else 1, R // tr),
        in_specs=[spec] * 4, out_specs=[spec] * 3,
        out_shape=[jax.ShapeDtypeStruct(g.shape, F32)] * 3,
        compiler_params=_cp("parallel", "parallel"),
    )(g, w, m, v)


BIG = (("ffn1_w_gate", 1, D, FF), ("ffn1_w_up", 1, D, FF), ("ffn1_w_down", 0, FF, D),
       ("w_in", 1, D, D_IN), ("w_out", 0, D, D),
       ("ffn2_w_gate", 1, D, FF), ("ffn2_w_up", 1, D, FF), ("ffn2_w_down", 0, FF, D),
       ("ple_w_gate", 0, D, D), ("ple_w_proj", 1, PLE, D))
SMALL = (("ffn1_norm", (D,), None), ("mix_norm", (D,), None), ("lru_conv_w", (4, LRU_W), LRU_W // N_DEV),
         ("lru_conv_b", (LRU_W,), None), ("lru_w_a", (4, HD, HD), None), ("lru_b_a", (LRU_W,), None),
         ("lru_w_x", (4, HD, HD), None), ("lru_b_x", (LRU_W,), None), ("lru_lambda", (LRU_W,), None),
         ("attn_sinks", (ATT_H,), None), ("dn_conv_w", (4, 3 * DN_W), 3 * DN_W // N_DEV),
         ("dn_a_log", (DN_H,), None), ("dn_dt_bias", (DN_H,), None), ("dn_norm", (HD,), None),
         ("ffn2_norm", (D,), None), ("ple_norm", (D,), None))
SINGLE = (("rel_bias", (REL_BUCKETS, ATT_H)), ("final_norm", (D,)))


def _pack_rows(arrs, width, mult):
    flat = jnp.concatenate([a.reshape(-1) for a in arrs])
    rows = -(-flat.shape[0] // (width * mult)) * mult
    return jnp.pad(flat, (0, rows * width - flat.shape[0])).reshape(rows, width)


def _unpack_rows(packed, shapes):
    flat = packed.reshape(-1)
    out, off = [], 0
    for s in shapes:
        n = int(np.prod(s))
        out.append(flat[off:off + n].reshape(s))
        off += n
    return out


def _pad_rows(w, r):
    return jnp.pad(w, ((0, r - w.shape[0]), (0, 0)))


def _shard_ffn(a, l, which, more=()):
    cols = jnp.concatenate([a[which + "_w_gate"][l], a[which + "_w_up"][l]], axis=0)
    rows = jnp.concatenate([_pad_rows(a[which + "_w_down"][l], SHP)] + list(more), axis=0)
    return jnp.pad(cols, ((0, 0), (0, SHP - SH))).astype(BF16), rows.astype(BF16)


def _shards(a, l):
    w_in_rows = _pad_rows(a["w_in"][l].reshape(WIN_ROWS, D), IN_ROWS).astype(BF16)
    rest = _shard_ffn(a, l, "ffn2", (a["w_out"][l], a["ple_w_gate"][l], a["ple_w_proj"][l].reshape(-1, D)))
    return _shard_ffn(a, l, "ffn1"), (w_in_rows,), rest


def _full_w_in(in_rows):
    sh = in_rows[:, :WIN_ROWS, :].reshape(N_DEV, D, D_IN // N_DEV)
    return jnp.pad(sh.transpose(1, 0, 2).reshape(D, D_IN), ((0, 0), (0, D_IN_PAD - D_IN)))


def _full_ple_proj(r_rows):
    sh = r_rows[:, R_PPROJ:R_ROWS, :].reshape(N_DEV, PLE, D // N_DEV)
    return sh.transpose(1, 0, 2).reshape(PLE, D)


PIECE_NAMES = {"ffn1": ("ffn1_w_gate", "ffn1_w_up", "ffn1_w_down"), "ffn2": ("ffn2_w_gate", "ffn2_w_up", "ffn2_w_down"),
               "mix": ("w_out", "ple_w_gate", "w_in", "ple_w_proj")}


def _shard_grads(piece, summed):
    if piece == "mix":
        rows, = summed
        return {"w_out": rows[GM_WOUT:GM_WOUT + ROWS_DEV], "ple_w_gate": rows[GM_PGATE:GM_PGATE + ROWS_DEV],
                "w_in": rows[GM_WIN:GM_WIN + WIN_ROWS].reshape(D, D_IN // N_DEV),
                "ple_w_proj": rows[GM_PPROJ:GM_END].reshape(PLE, D // N_DEV)}
    if piece in ("ffn1_gate", "ffn1_up"):
        return {piece.replace("_", "_w_"): summed[0][:, :SH]}
    if piece == "ffn1_down":
        return {"ffn1_w_down": summed[0][:SH]}
    cols, rows = summed
    return {piece + "_w_gate": cols[:D, :SH], piece + "_w_up": cols[D:, :SH], piece + "_w_down": rows[:SH]}


def kernel(x, p, ffn1_norm, ffn1_w_gate, ffn1_w_up, ffn1_w_down, mix_norm, w_in, lru_conv_w, lru_conv_b, lru_w_a, lru_b_a, lru_w_x, lru_b_x, lru_lambda, attn_sinks, rel_bias, dn_conv_w, dn_a_log, dn_dt_bias, dn_norm, w_out, ffn2_norm, ffn2_w_gate, ffn2_w_up, ffn2_w_down, ple_norm, ple_w_gate, ple_w_proj, final_norm, loss_target, m_ffn1_norm, m_ffn1_w_gate, m_ffn1_w_up, m_ffn1_w_down, m_mix_norm, m_w_in, m_lru_conv_w, m_lru_conv_b, m_lru_w_a, m_lru_b_a, m_lru_w_x, m_lru_b_x, m_lru_lambda, m_attn_sinks, m_rel_bias, m_dn_conv_w, m_dn_a_log, m_dn_dt_bias, m_dn_norm, m_w_out, m_ffn2_norm, m_ffn2_w_gate, m_ffn2_w_up, m_ffn2_w_down, m_ple_norm, m_ple_w_gate, m_ple_w_proj, m_final_norm, v_ffn1_norm, v_ffn1_w_gate, v_ffn1_w_up, v_ffn1_w_down, v_mix_norm, v_w_in, v_lru_conv_w, v_lru_conv_b, v_lru_w_a, v_lru_b_a, v_lru_w_x, v_lru_b_x, v_lru_lambda, v_attn_sinks, v_rel_bias, v_dn_conv_w, v_dn_a_log, v_dn_dt_bias, v_dn_norm, v_w_out, v_ffn2_norm, v_ffn2_w_gate, v_ffn2_w_up, v_ffn2_w_down, v_ple_norm, v_ple_w_gate, v_ple_w_proj, v_final_norm):
    a = dict(locals())
    nb, S, _ = x.shape
    T = nb * S
    my_slot = _slot_of(*_my_pos())

    W = {k: [None] * DEPTH for k in ("f1_cols", "f1_rows", "w_in", "r_cols", "r_rows", "ple_w_proj")}
    GATHER, SCATTER = ("gather_cols", "gather_block"), ("scatter_cols", "scatter_block")
    GROUP_MODES = {"f1": GATHER, "in": GATHER[1:], "rest": GATHER}

    def set_group(l, group, bufs):
        if group == "f1":
            W["f1_cols"][l], W["f1_rows"][l] = bufs
        elif group == "in":
            W["w_in"][l] = _full_w_in(bufs[0])
        else:
            W["r_cols"][l], W["r_rows"][l] = bufs
            W["ple_w_proj"][l] = _full_ple_proj(bufs[1])

    def landing(mode, src):
        if mode == "gather_cols":
            return lax.empty((src.shape[0], FFP), src.dtype)
        if mode == "scatter_cols":
            return lax.empty((N_DEV, src.shape[0], SHP), src.dtype)
        return lax.empty((N_DEV,) + src.shape[mode == "scatter_block":], src.dtype)

    def start(modes, srcs, name, after=()):
        return split_start(modes, srcs, [landing(m, s) for m, s in zip(modes, srcs)], name, after)

    shards0, shards1 = _shards(a, 0), _shards(a, 1)
    set_group(0, "f1", gather_layer(*shards0[0], "gather_weights_l0_ffn1"))
    taps = all_gather(_pack_rows([lru_conv_w, dn_conv_w], LANE, 8), "gather_conv_taps")
    flat_taps = taps.reshape(N_DEV, -1)
    for name, first, tap in (("lru_conv_w", 0, lru_conv_w), ("dn_conv_w", lru_conv_w.size, dn_conv_w)):
        per_dev = flat_taps[:, first:first + tap.size].reshape((N_DEV,) + tap.shape)
        W[name] = jnp.moveaxis(per_dev, 0, -2).reshape(tap.shape[:-1] + (N_DEV * tap.shape[-1],))
    for name, _, cols in SMALL:
        if cols is None:
            W[name] = a[name]
    W["rel_bias"], W["final_norm"] = rel_bias, final_norm

    gathers, after = {}, (W["f1_rows"][0], taps)
    for l, group, srcs in ((0, "in", shards0[1]), (0, "rest", shards0[2]),
                           (1, "f1", shards1[0]), (1, "in", shards1[1]), (1, "rest", shards1[2])):
        gathers[l, group], token = start(GROUP_MODES[group], srcs, f"gather_start_l{l}_{group}", after)
        after = (token,)
    W["ffn1_norm"] = ffn1_norm + token[0, 0]
    flight, tokens = {}, {}

    def need(l, group, h):
        if (l, group) in gathers:
            set_group(l, group, split_wait(GROUP_MODES[group], gathers[l, group], h, f"gather_wait_l{l}_{group}"))

    def piece_modes(piece):
        return {"mix": SCATTER[1:], "ffn1_gate": SCATTER[:1], "ffn1_up": SCATTER[:1], "ffn1_down": SCATTER[1:]}.get(
            piece, SCATTER)

    def on_piece(l, piece, bufs):
        bufs = [b.reshape(-1, FFP) if b.shape[-1] == FFP else b for b in bufs]
        flight[l, piece], tokens[l, piece] = start(piece_modes(piece), bufs, f"exchange_start_l{l}_{piece}")
        return tokens[l, piece][0, 0]

    loss_local, dx, grads, d_final = _core(x.reshape(T, D), p.reshape(DEPTH, T, PLE), W,
                                           loss_target.reshape(T, D), S, need, on_piece)
    loss = lax.psum(loss_local, AXES)

    small_full = [jnp.stack([grads[l][name] for l in range(DEPTH)]) for name, _, _ in SMALL]
    small_full += [grads[0]["rel_bias"] + grads[1]["rel_bias"], d_final]
    small_flight, _ = start(("gather_block",), (_pack_rows(small_full, LANE, 8),), "gather_start_small_grads",
                            (tokens[0, "ffn1_down"],))

    out = {}

    landed = {}

    def land(l, piece, after):
        landed[l, piece] = split_wait(piece_modes(piece), flight[l, piece], after, f"exchange_wait_l{l}_{piece}")

    def where(name, l):
        if name in ("w_out", "ple_w_gate"):
            return "mix", 0, (GM_WOUT if name == "w_out" else GM_PGATE), ROWS_DEV, D
        ffn, kind = name[:4], name[7:]
        one_by_one = (l, ffn) == (0, "ffn1")
        if kind == "down":
            return (ffn + "_down", 0, 0, SH, D) if one_by_one else (ffn, 1, 0, SH, D)
        if one_by_one:
            return f"{ffn}_{kind}", 0, 0, D, SH
        return ffn, 0, (0 if kind == "gate" else D), D, SH

    def update(piece):
        for name in PIECE_NAMES[piece]:
            if name in ("w_in", "ple_w_proj"):
                g = jnp.stack([_shard_grads("mix", [mix_sums[l]])[name] for l in range(DEPTH)])
            else:
                g = None
                for l in reversed(range(DEPTH)):
                    piece_l, idx, row0, rows, cols = where(name, l)
                    g = sum_into(landed[l, piece_l][idx], row0, rows, cols, l, g, f"sum_{name}_l{l}")
            out[name] = (g,) + tuple(adamw(g, a[name], a["m_" + name], a["v_" + name], "adamw_" + name))

    for l, piece in ((1, "ffn2"), (1, "mix"), (1, "ffn1"), (0, "ffn2"), (0, "mix")):
        land(l, piece, (dx, tokens[0, "ffn1_down"]))
    mix_sums = [sum_parts(landed[l, "mix"][0], f"sum_mix_grads_l{l}") for l in range(DEPTH)]
    update("ffn2")
    update("mix")
    done_early = tuple(out[n][1] for n in PIECE_NAMES["ffn2"] + PIECE_NAMES["mix"])
    small_parts, = split_wait(("gather_block",), small_flight, done_early, "gather_wait_small_grads")
    small_sum = sum_parts(small_parts, "sum_small_grads")
    g_small = dict(zip([n for n, _, _ in SMALL] + [n for n, _ in SINGLE],
                       _unpack_rows(small_sum, [s.shape for s in small_full])))
    for name, _, cols in SMALL:
        if cols is not None:
            g_small[name] = lax.dynamic_slice_in_dim(g_small[name], my_slot * cols, cols, axis=2)

    for n in [n for n, _, _ in SMALL] + [n for n, _ in SINGLE]:
        shape = a[n].shape
        flat = lambda t: t.reshape((-1, shape[-1]) if len(shape) > 1 else (1, -1))
        res = adamw(flat(g_small[n]), flat(a[n]), flat(a["m_" + n]), flat(a["v_" + n]), "adamw_" + n)
        out[n] = (g_small[n].reshape(shape),) + tuple(r.reshape(shape) for r in res)

    for piece in ("ffn1_gate", "ffn1_up", "ffn1_down"):
        land(0, piece, (out["final_norm"][1],) + done_early)
    update("ffn1")

    order = ['ffn1_norm', 'ffn1_w_gate', 'ffn1_w_up', 'ffn1_w_down', 'mix_norm', 'w_in', 'lru_conv_w', 'lru_conv_b',
             'lru_w_a', 'lru_b_a', 'lru_w_x', 'lru_b_x', 'lru_lambda', 'attn_sinks', 'rel_bias', 'dn_conv_w',
             'dn_a_log', 'dn_dt_bias', 'dn_norm', 'w_out', 'ffn2_norm', 'ffn2_w_gate', 'ffn2_w_up', 'ffn2_w_down',
             'ple_norm', 'ple_w_gate', 'ple_w_proj', 'final_norm']
    return (loss, dx.reshape(x.shape)) + tuple(out[n][k] for k in range(4) for n in order)
```

```python
import functools
import math

import numpy as np
import jax
import jax.numpy as jnp
from jax import lax
from jax.experimental import pallas as pl
from jax.experimental.pallas import tpu as pltpu

F32 = jnp.float32
BF16 = jnp.bfloat16
HI = lax.Precision.HIGHEST

D = 1024
DEPTH = 2
EPS = 1e-6
PLE = 256
FF = 2816
HD = 64
LRU_W = 256
LRU_C = 8.0
ATT_W = 512
ATT_H = 8
ATT_KV = 2
ATT_G = 4
KV_W = 128
WINDOW = 128
BQ = 128
REL_BUCKETS = 32
REL_MAX_DIST = 128
DN_W = 256
DN_H = 4
CHUNK = 64
D_IN = 2312
D_IN_PAD = 2432
N_DEV = 8

ADAM_LR = 0.001
ADAM_B1 = 0.9
ADAM_B2 = 0.999
ADAM_EPS = 1e-08
ADAM_WD = 0.01
ADAM_STEP = 10

LANE = 128
VMEM_LIMIT = 56 * 1024 * 1024
SH = FF // N_DEV
SHP = 384
FFP = N_DEV * SHP
FF_TILE = 2 * SHP
FF_SUB = 256
TOK_TILE = 512
LIGHT_TILE = 1024
R_DOWN2, R_WOUT, R_PGATE, R_PPROJ, R_ROWS = 0, 384, 512, 640, 672
WIN_ROWS = D * D_IN // N_DEV // 1024
IN_ROWS = 304
NEG = -1e30


def _cp(*sem):
    return pltpu.CompilerParams(dimension_semantics=tuple(sem), vmem_limit_bytes=VMEM_LIMIT)


def _dg(a, b, ca, cb, exact):
    dims = (((ca,), (cb,)), ((), ()))
    if exact == "f32":
        return lax.dot_general(a.astype(F32), b.astype(F32), dims, precision=HI, preferred_element_type=F32)
    if exact == "split":
        a_hi, b_hi = a.astype(BF16), b.astype(BF16)
        a_lo = (a - a_hi.astype(F32)).astype(BF16)
        b_lo = (b - b_hi.astype(F32)).astype(BF16)
        dot = lambda u, v: lax.dot_general(u, v, dims, preferred_element_type=F32)
        return dot(a_hi, b_hi) + (dot(a_hi, b_lo) + dot(a_lo, b_hi))
    return lax.dot_general(a.astype(BF16), b.astype(BF16), dims, preferred_element_type=F32)


def _make_mm(exact):
    @jax.custom_vjp
    def mm(a, b):
        return _dg(a, b, 1, 0, exact)

    @jax.custom_vjp
    def mm_nt(a, b):
        return _dg(a, b, 1, 1, exact)

    @jax.custom_vjp
    def mm_tn(a, b):
        return _dg(a, b, 0, 0, exact)

    mm.defvjp(lambda a, b: (mm(a, b), (a, b)),
              lambda r, d: (mm_nt(d, r[1]), mm_tn(r[0], d)))
    mm_nt.defvjp(lambda a, b: (mm_nt(a, b), (a, b)),
                 lambda r, d: (mm(d, r[1]), mm_tn(d, r[0])))
    mm_tn.defvjp(lambda a, b: (mm_tn(a, b), (a, b)),
                 lambda r, d: (mm_nt(r[1], d), mm(r[0], d)))
    return mm, mm_nt, mm_tn


_mm, _mm_nt, _mm_tn = _make_mm("bf16")
_mmx, _mmx_nt, _mmx_tn = _make_mm("f32")
_mm3, _mm3_nt, _mm3_tn = _make_mm("split")


def _iota(shape, dim):
    return lax.broadcasted_iota(jnp.int32, shape, dim)


def _sigmoid(x):
    return 0.5 * jnp.tanh(0.5 * x) + 0.5


def _rms(h, g):
    rstd = lax.rsqrt(jnp.mean(h * h, axis=-1, keepdims=True) + EPS)
    xhat = h * rstd
    return xhat * g, xhat, rstd


def _rms_bwd(dxn, xhat, rstd, g):
    dxhat = dxn * g
    dh = rstd * (dxhat - xhat * jnp.mean(dxhat * xhat, axis=-1, keepdims=True))
    dg = jnp.sum(dxn * xhat, axis=0, keepdims=True)
    return dh, dg


def _row_spec(tm, n):
    return pl.BlockSpec((tm, n), lambda i, *_: (i, 0))


def _full_spec(shape):
    nd = len(shape)
    return pl.BlockSpec(shape, lambda *_: (0,) * nd)


def _ffn_weight_specs():
    return [pl.BlockSpec((D, FF_TILE), lambda i, j: (0, j)),
            pl.BlockSpec((D, FF_TILE), lambda i, j: (1, j)),
            pl.BlockSpec((2, SHP, D), lambda i, j: (j, 0, 0))]


def ffn_fwd(h, g, wa, wb, name):
    T = h.shape[0]
    tm = min(2 * TOK_TILE, T)
    nj = FFP // FF_TILE

    def body(h_ref, g_ref, wg_ref, wu_ref, wd_ref, o_ref, gt_ref, up_ref, xn_ref):
        j = pl.program_id(1)

        @pl.when(j == 0)
        def _():
            hh = h_ref[...]
            xn_ref[...] = _rms(hh, g_ref[...])[0].astype(BF16)
            o_ref[...] = hh

        blocks = [slice(c, c + FF_SUB) for c in range(0, FF_TILE, FF_SUB)]
        xn = xn_ref[...]
        wd = wd_ref[...].reshape(FF_TILE, D)
        gt = [_mm(xn, wg_ref[:, c]) for c in blocks]
        up = [_mm(xn, wu_ref[:, c]) for c in blocks]
        act = [t * _sigmoid(t) * u for t, u in zip(gt, up)]
        down = [_mm(act[k], wd[c]) for k, c in enumerate(blocks)]
        for k, c in enumerate(blocks):
            gt_ref[:, c] = gt[k].astype(BF16)
            up_ref[:, c] = up[k].astype(BF16)
        o_ref[...] += 0.5 * functools.reduce(lambda x, y: x + y, down)

    tile = pl.BlockSpec((tm, FF_TILE), lambda i, j: (i, j))
    return pl.pallas_call(
        body, name=name, grid=(T // tm, nj),
        in_specs=[pl.BlockSpec((tm, D), lambda i, j: (i, 0)),
                  pl.BlockSpec((1, D), lambda i, j: (0, 0))] + _ffn_weight_specs(),
        out_specs=[pl.BlockSpec((tm, D), lambda i, j: (i, 0)), tile, tile,
                   pl.BlockSpec((tm, D), lambda i, j: (i, 0))],
        out_shape=[jax.ShapeDtypeStruct((T, D), F32), jax.ShapeDtypeStruct((T, FFP), BF16),
                   jax.ShapeDtypeStruct((T, FFP), BF16), jax.ShapeDtypeStruct((T, D), BF16)],
        compiler_params=_cp("parallel", "arbitrary"),
    )(h, g, wa, wa, wb)


def ffn_bwd(h, dy, g, gt_saved, up_saved, wa, wb, name):
    T = h.shape[0]
    tm = min(TOK_TILE, T)
    nj = FFP // FF_TILE

    def body(h_ref, dy_ref, g_ref, gt_ref, up_ref, wg_ref, wu_ref, wd_ref,
             dh_ref, dg_ref, du_ref, a_ref, dn_ref, dxn_s, dyh_s):
        i = pl.program_id(0)
        j = pl.program_id(1)

        @pl.when(j == 0)
        def _():
            dxn_s[...] = jnp.zeros_like(dxn_s)
            dyh_s[...] = (0.5 * dy_ref[...]).astype(BF16)

        @pl.when((i == 0) & (j == 0))
        def _():
            dn_ref[...] = jnp.zeros_like(dn_ref)

        blocks = [slice(c, c + FF_SUB) for c in range(0, FF_TILE, FF_SUB)]
        wd = wd_ref[...].reshape(FF_TILE, D)
        dyh = dyh_s[...]
        gt = [gt_ref[:, c].astype(F32) for c in blocks]
        up = [up_ref[:, c].astype(F32) for c in blocks]
        da = [_mm_nt(dyh, wd[c]) for c in blocks]
        sg = [_sigmoid(t) for t in gt]
        si = [t * s for t, s in zip(gt, sg)]
        dup = [d * s for d, s in zip(da, si)]
        dgt = [d * u * (s * (1.0 + t * (1.0 - s))) for d, u, s, t in zip(da, up, sg, gt)]
        dxn = [_mm_nt(jnp.concatenate([dgt[k].astype(BF16), dup[k].astype(BF16)], axis=1),
                      jnp.concatenate([wg_ref[:, c], wu_ref[:, c]], axis=1)) for k, c in enumerate(blocks)]
        for k, c in enumerate(blocks):
            dg_ref[:, c] = dgt[k].astype(BF16)
            du_ref[:, c] = dup[k].astype(BF16)
            a_ref[:, c] = (si[k] * up[k]).astype(BF16)
        dxn_s[...] += functools.reduce(lambda x, y: x + y, dxn)

        @pl.when(j == nj - 1)
        def _():
            gg = g_ref[...]
            _, xhat, rstd = _rms(h_ref[...], gg)
            dh, dn = _rms_bwd(dxn_s[...], xhat, rstd, gg)
            dh_ref[...] = dy_ref[...] + dh
            dn_ref[...] += dn

    tile = pl.BlockSpec((tm, FF_TILE), lambda i, j: (i, j))
    return pl.pallas_call(
        body, name=name, grid=(T // tm, nj),
        in_specs=[pl.BlockSpec((tm, D), lambda i, j: (i, 0)),
                  pl.BlockSpec((tm, D), lambda i, j: (i, 0)),
                  pl.BlockSpec((1, D), lambda i, j: (0, 0)), tile, tile] + _ffn_weight_specs(),
        out_specs=[pl.BlockSpec((tm, D), lambda i, j: (i, 0)), tile, tile, tile,
                   pl.BlockSpec((1, D), lambda i, j: (0, 0))],
        out_shape=[jax.ShapeDtypeStruct((T, D), F32)] + [jax.ShapeDtypeStruct((T, FFP), BF16)] * 3
        + [jax.ShapeDtypeStruct((1, D), F32)],
        scratch_shapes=[pltpu.VMEM((tm, D), F32), pltpu.VMEM((tm, D), BF16)],
        compiler_params=_cp("arbitrary", "arbitrary"),
    )(h, dy, g, gt_saved, up_saved, wa, wa, wb)


def _pick(n, prefs):
    for t in prefs:
        if n % t == 0:
            return t
    return n


def _tn_body(nk, scale, out_dtype, squeeze):
    def body(a_ref, b_ref, *rest):
        o_ref, acc = rest[-2], rest[-1]
        k = pl.program_id(2)

        @pl.when(k == 0)
        def _():
            acc[...] = jnp.zeros_like(acc)

        acc[...] += _mm_tn(a_ref[...], b_ref[...])

        @pl.when(k == nk - 1)
        def _():
            res = (scale * acc[...]).astype(out_dtype)
            if squeeze:
                o_ref[0] = res
            else:
                o_ref[...] = res

    return body


def matmul_tn(a, b, name, scale=1.0, out_dtype=BF16):
    T, M = a.shape
    N = b.shape[1]
    tmm = _pick(M, (512, 256))
    tnn = _pick(N, (1024, 2432))
    tk = min(2 * TOK_TILE, T)
    nk = T // tk
    return pl.pallas_call(
        _tn_body(nk, scale, out_dtype, False), name=name, grid=(M // tmm, N // tnn, nk),
        in_specs=[pl.BlockSpec((tk, tmm), lambda i, j, k: (k, i)),
                  pl.BlockSpec((tk, tnn), lambda i, j, k: (k, j))],
        out_specs=pl.BlockSpec((tmm, tnn), lambda i, j, k: (i, j)),
        out_shape=jax.ShapeDtypeStruct((M, N), out_dtype),
        scratch_shapes=[pltpu.VMEM((tmm, tnn), F32)],
        compiler_params=_cp("parallel", "parallel", "arbitrary"),
    )(a, b)


def grad_cols(a, b, dst, slot, name):
    T = a.shape[0]
    tmm, tnn = D, FFP // 2
    tk = min(2 * TOK_TILE, T)
    nk = T // tk
    return pl.pallas_call(
        _tn_body(nk, 1.0, BF16, True), name=name, grid=(D // tmm, FFP // tnn, nk),
        in_specs=[pl.BlockSpec((tk, tmm), lambda i, j, k: (k, i)),
                  pl.BlockSpec((tk, tnn), lambda i, j, k: (k, j)),
                  pl.BlockSpec(memory_space=pl.ANY)],
        out_specs=pl.BlockSpec((1, tmm, tnn), lambda i, j, k: (slot, i, j)),
        out_shape=jax.ShapeDtypeStruct(dst.shape, dst.dtype),
        scratch_shapes=[pltpu.VMEM((tmm, tnn), F32)],
        input_output_aliases={2: 0},
        compiler_params=_cp("parallel", "parallel", "arbitrary"),
    )(a, b, dst)


def grad_cols_pair(a, b0, b1, name):
    T = a.shape[0]
    tnn = FF_TILE
    tk = min(2 * TOK_TILE, T)
    nk = T // tk

    def body(a_ref, b0_ref, b1_ref, o_ref, acc):
        k = pl.program_id(1)

        @pl.when(k == 0)
        def _():
            acc[...] = jnp.zeros_like(acc)

        acc[...] += _mm_tn(a_ref[...], jnp.concatenate([b0_ref[...], b1_ref[...]], axis=1))

        @pl.when(k == nk - 1)
        def _():
            o_ref[0] = acc[:, :tnn].astype(BF16)
            o_ref[1] = acc[:, tnn:].astype(BF16)

    b_spec = pl.BlockSpec((tk, tnn), lambda j, k: (k, j))
    return pl.pallas_call(
        body, name=name, grid=(FFP // tnn, nk),
        in_specs=[pl.BlockSpec((tk, D), lambda j, k: (k, 0)), b_spec, b_spec],
        out_specs=pl.BlockSpec((2, D, tnn), lambda j, k: (0, 0, j)),
        out_shape=jax.ShapeDtypeStruct((2, D, FFP), BF16),
        scratch_shapes=[pltpu.VMEM((D, 2 * tnn), F32)],
        compiler_params=_cp("parallel", "arbitrary"),
    )(a, b0, b1)


def grad_rows(a, b, dst, row0, rows, name, scale=1.0):
    T = a.shape[0]
    tk = min(2 * TOK_TILE, T)
    nk = T // tk
    blk = row0 // rows

    def body(a_ref, b_ref, dst_ref, o_ref, acc):
        k = pl.program_id(0)

        @pl.when(k == 0)
        def _():
            acc[...] = jnp.zeros_like(acc)

        acc[...] += _mm_tn(a_ref[...], b_ref[...])

        @pl.when(k == nk - 1)
        def _():
            o_ref[...] = (scale * acc[...]).astype(BF16).reshape(N_DEV, rows, D)

    return pl.pallas_call(
        body, name=name, grid=(nk,),
        in_specs=[pl.BlockSpec((tk, N_DEV * rows), lambda k: (k, 0)),
                  pl.BlockSpec((tk, D), lambda k: (k, 0)),
                  pl.BlockSpec(memory_space=pl.ANY)],
        out_specs=pl.BlockSpec((N_DEV, rows, D), lambda k: (0, blk, 0)),
        out_shape=jax.ShapeDtypeStruct(dst.shape, dst.dtype),
        scratch_shapes=[pltpu.VMEM((N_DEV * rows, D), F32)],
        input_output_aliases={2: 0},
        compiler_params=_cp("arbitrary"),
    )(a, b, dst)


U_SPLITS = (512, 768, 1024, 128)
U_OFFS = (0, 512, 1280, 2304)


def mixin_fwd(h, g, w_in, name):
    T = h.shape[0]
    tm = min(TOK_TILE, T)

    def body(h_ref, g_ref, w_ref, u0, u1, u2, u3, xn_ref):
        xn = _rms(h_ref[...], g_ref[...])[0].astype(BF16)
        xn_ref[...] = xn
        u = _mm(xn, w_ref[...])
        for ref, off, n in zip((u0, u1, u2, u3), U_OFFS, U_SPLITS):
            ref[...] = u[:, off:off + n]

    return pl.pallas_call(
        body, name=name, grid=(T // tm,),
        in_specs=[_row_spec(tm, D), _full_spec((1, D)), _full_spec((D, D_IN_PAD))],
        out_specs=[_row_spec(tm, n) for n in U_SPLITS] + [_row_spec(tm, D)],
        out_shape=[jax.ShapeDtypeStruct((T, n), F32) for n in U_SPLITS]
        + [jax.ShapeDtypeStruct((T, D), BF16)],
        compiler_params=_cp("parallel"),
    )(h, g, w_in)


DU_SPLITS = (256, 256, 768, 768, 256, 128)
DU_OFFS = (0, 256, 512, 1280, 2048, 2304)


def mixin_bwd(h, dh_in, g, w_in, dus, name):
    T = h.shape[0]
    tm = min(TOK_TILE, T)

    def body(h_ref, dhi_ref, g_ref, w_ref, *refs):
        dh_ref, du_ref, dn_ref = refs[-3:]

        @pl.when(pl.program_id(0) == 0)
        def _():
            dn_ref[...] = jnp.zeros_like(dn_ref)

        for ref, off, n in zip(refs[:-3], DU_OFFS, DU_SPLITS):
            du_ref[:, off:off + n] = ref[...].astype(BF16)
        dxn = _mm_nt(du_ref[...], w_ref[...])
        gg = g_ref[...]
        _, xhat, rstd = _rms(h_ref[...], gg)
        dh, dn = _rms_bwd(dxn, xhat, rstd, gg)
        dh_ref[...] = dhi_ref[...] + dh
        dn_ref[...] += dn

    return pl.pallas_call(
        body, name=name, grid=(T // tm,),
        in_specs=[_row_spec(tm, D), _row_spec(tm, D), _full_spec((1, D)), _full_spec((D, D_IN_PAD))]
        + [_row_spec(tm, n) for n in DU_SPLITS],
        out_specs=[_row_spec(tm, D), _row_spec(tm, D_IN_PAD), _full_spec((1, D))],
        out_shape=[jax.ShapeDtypeStruct((T, D), F32), jax.ShapeDtypeStruct((T, D_IN_PAD), BF16),
                   jax.ShapeDtypeStruct((1, D), F32)],
        compiler_params=_cp("arbitrary"),
    )(h, dh_in, g, w_in, *dus)


def _shift_down(x, s, row):
    if s == 0:
        return x
    return jnp.where(row >= s, pltpu.roll(x, s, 0), 0.0)


def _shift_up(x, s, row):
    if s == 0:
        return x
    n = x.shape[0]
    return jnp.where(row < n - s, pltpu.roll(x, n - s, 0), 0.0)


def conv_fwd(x, w, b, S, col0, C, name):
    T = x.shape[0]
    cb0 = col0 // LANE

    def body(x_ref, w_ref, b_ref, y_ref):
        xx = x_ref[...]
        row = _iota(xx.shape, 0)
        y = xx * w_ref[3:4, :] + b_ref[...]
        for k in range(3):
            y += _shift_down(xx, 3 - k, row) * w_ref[k:k + 1, :]
        y_ref[...] = y

    return pl.pallas_call(
        body, name=name, grid=(T // S, C // LANE),
        in_specs=[pl.BlockSpec((S, LANE), lambda s, c: (s, cb0 + c)),
                  pl.BlockSpec((4, LANE), lambda s, c: (0, c)),
                  pl.BlockSpec((1, LANE), lambda s, c: (0, c))],
        out_specs=pl.BlockSpec((S, LANE), lambda s, c: (s, c)),
        out_shape=jax.ShapeDtypeStruct((T, C), F32),
        compiler_params=_cp("parallel", "parallel"),
    )(x, w, b)


def conv_bwd(x, dy, w, S, col0, C, name):
    T = x.shape[0]
    cb0 = col0 // LANE

    def body(x_ref, dy_ref, w_ref, dx_ref, dwb_ref):
        @pl.when(pl.program_id(1) == 0)
        def _():
            dwb_ref[...] = jnp.zeros_like(dwb_ref)

        xx = x_ref[...]
        dd = dy_ref[...]
        row = _iota(xx.shape, 0)
        dx = dd * w_ref[3:4, :]
        for k in range(3):
            dx += _shift_up(dd, 3 - k, row) * w_ref[k:k + 1, :]
        dx_ref[...] = dx
        for k in range(4):
            dwb_ref[k:k + 1, :] += jnp.sum(dd * _shift_down(xx, 3 - k, row), axis=0, keepdims=True)
        dwb_ref[4:5, :] += jnp.sum(dd, axis=0, keepdims=True)

    return pl.pallas_call(
        body, name=name, grid=(C // LANE, T // S),
        in_specs=[pl.BlockSpec((S, LANE), lambda c, s: (s, cb0 + c)),
                  pl.BlockSpec((S, LANE), lambda c, s: (s, c)),
                  pl.BlockSpec((4, LANE), lambda c, s: (0, c))],
        out_specs=[pl.BlockSpec((S, LANE), lambda c, s: (s, c)),
                   pl.BlockSpec((8, LANE), lambda c, s: (0, c))],
        out_shape=[jax.ShapeDtypeStruct((T, C), F32), jax.ShapeDtypeStruct((8, C), F32)],
        compiler_params=_cp("parallel", "arbitrary"),
    )(x, dy, w)


def _scan(a, b, row):
    n = a.shape[0]
    d = 1
    while d < n:
        keep = row >= d
        b = a * jnp.where(keep, pltpu.roll(b, d, 0), 0.0) + b
        a = a * jnp.where(keep, pltpu.roll(a, d, 0), 1.0)
        d *= 2
    return b


def _rscan(a, b, row):
    n = a.shape[0]
    d = 1
    while d < n:
        keep = row < n - d
        b = a * jnp.where(keep, pltpu.roll(b, n - d, 0), 0.0) + b
        a = a * jnp.where(keep, pltpu.roll(a, n - d, 0), 1.0)
        d *= 2
    return b


GELU_C = math.sqrt(2.0 / math.pi)


def _gelu(x):
    t = jnp.tanh(GELU_C * (x + 0.044715 * (x * x * x)))
    return 0.5 * x * (1.0 + t), t


def _lru_gates(xr, wa, ba, wx, bx, lam):
    r = _sigmoid(_mm(xr, wa) + ba)
    i = _sigmoid(_mm(xr, wx) + bx)
    sp = jnp.maximum(-lam, 0.0) + jnp.log(1.0 + jnp.exp(-jnp.abs(lam)))
    la = -LRU_C * r * sp
    a = jnp.exp(la)
    e2 = a * a
    m = jnp.sqrt(-jnp.tanh(la) * (e2 + 1.0))
    return r, i, sp, a, e2, m


def lru_fwd(xr, u_lru, wa, wx, vec, S, name):
    T = xr.shape[0]

    def body(xr_ref, gt_ref, wa_ref, wx_ref, vec_ref, y_ref):
        x = xr_ref[...]
        row = _iota(x.shape, 0)
        r, i, sp, a, e2, m = _lru_gates(x, wa_ref[...], vec_ref[0:1, :], wx_ref[...], vec_ref[1:2, :],
                                        vec_ref[2:3, :])
        hh = _scan(a, m * (i * x), row)
        y_ref[...] = _gelu(gt_ref[...])[0] * hh

    return pl.pallas_call(
        body, name=name, grid=(T // S, LRU_W // LANE),
        in_specs=[pl.BlockSpec((S, LANE), lambda s, c: (s, c)),
                  pl.BlockSpec((S, LANE), lambda s, c: (s, 2 + c)),
                  pl.BlockSpec((LANE, LANE), lambda s, c: (c, c)),
                  pl.BlockSpec((LANE, LANE), lambda s, c: (c, c)),
                  pl.BlockSpec((8, LANE), lambda s, c: (0, c))],
        out_specs=pl.BlockSpec((S, LANE), lambda s, c: (s, c)),
        out_shape=jax.ShapeDtypeStruct((T, LRU_W), F32),
        compiler_params=_cp("parallel", "parallel"),
    )(xr, u_lru, wa, wx, vec)


def lru_bwd(xr, u_lru, dy, wa, wx, vec, S, name):
    T = xr.shape[0]

    def body(xr_ref, gt_ref, dy_ref, wa_ref, wx_ref, vec_ref,
             dxr_ref, dgt_ref, dwa_ref, dwx_ref, dvec_ref):
        @pl.when(pl.program_id(1) == 0)
        def _():
            dwa_ref[...] = jnp.zeros_like(dwa_ref)
            dwx_ref[...] = jnp.zeros_like(dwx_ref)
            dvec_ref[...] = jnp.zeros_like(dvec_ref)

        x = xr_ref[...]
        n = x.shape[0]
        row = _iota(x.shape, 0)
        lam = vec_ref[2:3, :]
        r, i, sp, a, e2, m = _lru_gates(x, wa_ref[...], vec_ref[0:1, :], wx_ref[...], vec_ref[1:2, :], lam)
        v = i * x
        hh = _scan(a, m * v, row)
        gt = gt_ref[...]
        dy = dy_ref[...]
        ge, t = _gelu(gt)
        dgt_ref[...] = dy * hh * (0.5 * (1.0 + t) + 0.5 * gt * (1.0 - t * t) * GELU_C
                                  * (1.0 + 3.0 * 0.044715 * gt * gt))
        a_next = jnp.where(row < n - 1, pltpu.roll(a, n - 1, 0), 0.0)
        G = _rscan(a_next, dy * ge, row)
        da = G * _shift_down(hh, 1, row)
        dv = G * m
        dla = da * a - (G * v) * e2 / m
        dr = dla * (-LRU_C * sp)
        dsp = jnp.sum(dla * (-LRU_C * r), axis=0, keepdims=True)
        dra = dr * r * (1.0 - r)
        dia = (dv * x) * i * (1.0 - i)
        dxr_ref[...] = dv * i + _mm_nt(dra, wa_ref[...]) + _mm_nt(dia, wx_ref[...])
        dwa_ref[0] += _mm_tn(x, dra)
        dwx_ref[0] += _mm_tn(x, dia)
        dvec_ref[0:1, :] += jnp.sum(dra, axis=0, keepdims=True)
        dvec_ref[1:2, :] += jnp.sum(dia, axis=0, keepdims=True)
        dvec_ref[2:3, :] += dsp * (-_sigmoid(-lam))

    return pl.pallas_call(
        body, name=name, grid=(LRU_W // LANE, T // S),
        in_specs=[pl.BlockSpec((S, LANE), lambda c, s: (s, c)),
                  pl.BlockSpec((S, LANE), lambda c, s: (s, 2 + c)),
                  pl.BlockSpec((S, LANE), lambda c, s: (s, c)),
                  pl.BlockSpec((LANE, LANE), lambda c, s: (c, c)),
                  pl.BlockSpec((LANE, LANE), lambda c, s: (c, c)),
                  pl.BlockSpec((8, LANE), lambda c, s: (0, c))],
        out_specs=[pl.BlockSpec((S, LANE), lambda c, s: (s, c)),
                   pl.BlockSpec((S, LANE), lambda c, s: (s, c)),
                   pl.BlockSpec((1, LANE, LANE), lambda c, s: (c, 0, 0)),
                   pl.BlockSpec((1, LANE, LANE), lambda c, s: (c, 0, 0)),
                   pl.BlockSpec((8, LANE), lambda c, s: (0, c))],
        out_shape=[jax.ShapeDtypeStruct((T, LRU_W), F32), jax.ShapeDtypeStruct((T, LRU_W), F32),
                   jax.ShapeDtypeStruct((2, LANE, LANE), F32), jax.ShapeDtypeStruct((2, LANE, LANE), F32),
                   jax.ShapeDtypeStruct((8, LRU_W), F32)],
        compiler_params=_cp("parallel", "arbitrary"),
    )(xr, u_lru, dy, wa, wx, vec)


def _bucket_table():
    qi = np.arange(BQ)[:, None]
    kj = np.arange(2 * BQ)[None, :]
    dist = BQ + qi - kj
    band = (dist >= 0) & (dist < WINDOW)
    dd = np.maximum(dist, 0)
    max_exact = REL_BUCKETS // 2
    large = max_exact + (np.log(np.maximum(dd, 1).astype(np.float32) / np.float32(max_exact))
                         / np.float32(math.log(REL_MAX_DIST / max_exact))
                         * np.float32(REL_BUCKETS - max_exact)).astype(np.int32)
    large = np.minimum(large, REL_BUCKETS - 1)
    bucket = np.where(dd < max_exact, dd, large)
    return np.where(band, bucket, -1).astype(np.int32)


def _att_specs(S):
    nb = S // BQ
    qc = ATT_W // LANE
    return [pl.BlockSpec((BQ, ATT_W), lambda b, n: (b * nb + n, 0)),
            pl.BlockSpec((BQ, KV_W), lambda b, n: (b * nb + jnp.maximum(n - 1, 0), qc)),
            pl.BlockSpec((BQ, KV_W), lambda b, n: (b * nb + n, qc)),
            pl.BlockSpec((BQ, KV_W), lambda b, n: (b * nb + jnp.maximum(n - 1, 0), qc + 1)),
            pl.BlockSpec((BQ, KV_W), lambda b, n: (b * nb + n, qc + 1))]


def _att_bias(bk, rb_ref, bias_s):
    for h in range(ATT_H):
        acc = jnp.zeros(bk.shape, F32)
        for bb in range(REL_BUCKETS):
            acc = jnp.where(bk == bb, rb_ref[bb * ATT_H + h], acc)
        bias_s[h] = acc


def _att_probs(qs, kgs, bias_s, valid, sk_ref):
    heads = range(ATT_H)
    s = [_mm_nt(qs[h], kgs[h // ATT_G]) for h in heads]
    s = [jnp.where(valid, s[h] * (HD ** -0.5) + bias_s[h], NEG) for h in heads]
    m = [jnp.maximum(jnp.max(s[h], axis=-1, keepdims=True), sk_ref[h]) for h in heads]
    e = [jnp.exp(s[h] - m[h]) for h in heads]
    es = [jnp.exp(sk_ref[h] - m[h]) for h in heads]
    den = [jnp.sum(e[h], axis=-1, keepdims=True) + es[h] for h in heads]
    return [e[h] / den[h] for h in heads], [es[h] / den[h] for h in heads]


def _att_kv(kp_ref, kc_ref, vp_ref, vc_ref):
    cat = lambda a, b, g: jnp.concatenate([a[:, HD * g:HD * (g + 1)], b[:, HD * g:HD * (g + 1)]], axis=0)
    return ([cat(kp_ref, kc_ref, g) for g in range(ATT_KV)], [cat(vp_ref, vc_ref, g) for g in range(ATT_KV)])


def attn_fwd(u_att, sinks, rel_bias, S, name):
    T = u_att.shape[0]
    nb = S // BQ
    table = jnp.asarray(_bucket_table())

    def body(sk_ref, rb_ref, bk_ref, q_ref, kp_ref, kc_ref, vp_ref, vc_ref, o_ref, bias_s):
        b = pl.program_id(0)
        n = pl.program_id(1)
        bk = bk_ref[...]

        @pl.when((b == 0) & (n == 0))
        def _():
            _att_bias(bk, rb_ref, bias_s)

        valid = (bk >= 0) & ((n > 0) | (_iota(bk.shape, 1) >= BQ))
        kgs, vgs = _att_kv(kp_ref, kc_ref, vp_ref, vc_ref)
        p, _ = _att_probs([q_ref[:, HD * h:HD * (h + 1)] for h in range(ATT_H)], kgs, bias_s, valid, sk_ref)
        outs = [_mm(p[h], vgs[h // ATT_G]) for h in range(ATT_H)]
        for h in range(ATT_H):
            o_ref[:, HD * h:HD * (h + 1)] = outs[h]

    smem = pl.BlockSpec(memory_space=pltpu.SMEM)
    return pl.pallas_call(
        body, name=name, grid=(T // S, nb),
        in_specs=[smem, smem, _full_spec((BQ, 2 * BQ))] + _att_specs(S),
        out_specs=pl.BlockSpec((BQ, ATT_W), lambda b, n: (b * nb + n, 0)),
        out_shape=jax.ShapeDtypeStruct((T, ATT_W), F32),
        scratch_shapes=[pltpu.VMEM((ATT_H, BQ, 2 * BQ), F32)],
        compiler_params=_cp("arbitrary", "arbitrary"),
    )(sinks, rel_bias, table, u_att, u_att, u_att, u_att, u_att)


def attn_bwd(u_att, dy, sinks, rel_bias, S, name):
    T = u_att.shape[0]
    nb = S // BQ
    nB = T // S
    table = jnp.asarray(_bucket_table())
    scale = HD ** -0.5

    def body(sk_ref, rb_ref, bk_ref, q_ref, kp_ref, kc_ref, vp_ref, vc_ref, dy_ref,
             du_ref, drel_ref, dsk_ref, bias_s, dbias_s):
        b = pl.program_id(0)
        n = pl.program_id(1)
        bk = bk_ref[...]

        @pl.when((b == 0) & (n == 0))
        def _():
            _att_bias(bk, rb_ref, bias_s)
            dbias_s[...] = jnp.zeros_like(dbias_s)
            dsk_ref[...] = jnp.zeros_like(dsk_ref)
            drel_ref[...] = jnp.zeros_like(drel_ref)

        @pl.when(n == 0)
        def _():
            du_ref[...] = jnp.zeros_like(du_ref)

        valid = (bk >= 0) & ((n > 0) | (_iota(bk.shape, 1) >= BQ))
        r_cur = pl.multiple_of(n * BQ, BQ)
        r_prev = pl.multiple_of(jnp.maximum(n - 1, 0) * BQ, BQ)
        heads = range(ATT_H)
        kgs, vgs = _att_kv(kp_ref, kc_ref, vp_ref, vc_ref)
        qs = [q_ref[:, HD * h:HD * (h + 1)] for h in heads]
        dos = [dy_ref[:, HD * h:HD * (h + 1)] for h in heads]
        p, ps = _att_probs(qs, kgs, bias_s, valid, sk_ref)
        dp = [_mm_nt(dos[h], vgs[h // ATT_G]) for h in heads]
        delta = [jnp.sum(p[h] * dp[h], axis=-1, keepdims=True) for h in heads]
        ds = [p[h] * (dp[h] - delta[h]) for h in heads]
        dss = [ds[h] * scale for h in heads]
        dq = [_mm(dss[h], kgs[h // ATT_G]) for h in heads]
        dks = [_mm_tn(dss[h], qs[h]) for h in heads]
        dvs = [_mm_tn(p[h], dos[h]) for h in heads]
        for h in heads:
            dbias_s[h] += ds[h]
            dsk_ref[h:h + 1, :] += jnp.broadcast_to(jnp.sum(-ps[h] * delta[h], axis=0, keepdims=True), (1, LANE))
            du_ref[pl.ds(r_cur, BQ), HD * h:HD * (h + 1)] = dq[h]
        for g in range(ATT_KV):
            of_group = range(g * ATT_G, (g + 1) * ATT_G)
            dk = functools.reduce(lambda x, y: x + y, [dks[h] for h in of_group])
            dv = functools.reduce(lambda x, y: x + y, [dvs[h] for h in of_group])
            ck = ATT_W + HD * g
            cv = ATT_W + KV_W + HD * g
            du_ref[pl.ds(r_prev, BQ), ck:ck + HD] += dk[0:BQ]
            du_ref[pl.ds(r_cur, BQ), ck:ck + HD] += dk[BQ:]
            du_ref[pl.ds(r_prev, BQ), cv:cv + HD] += dv[0:BQ]
            du_ref[pl.ds(r_cur, BQ), cv:cv + HD] += dv[BQ:]

        @pl.when((b == nB - 1) & (n == nb - 1))
        def _():
            lane = _iota((1, LANE), 1)
            for h in range(ATT_H):
                db = dbias_s[h]
                acc = jnp.zeros((1, LANE), F32)
                for bb in range(REL_BUCKETS):
                    val = jnp.sum(jnp.sum(jnp.where(bk == bb, db, 0.0), axis=1, keepdims=True),
                                  axis=0, keepdims=True)
                    acc = jnp.where(lane == bb, val, acc)
                drel_ref[h:h + 1, :] = acc

    smem = pl.BlockSpec(memory_space=pltpu.SMEM)
    return pl.pallas_call(
        body, name=name, grid=(nB, nb),
        in_specs=[smem, smem, _full_spec((BQ, 2 * BQ))] + _att_specs(S)
        + [pl.BlockSpec((BQ, ATT_W), lambda b, n: (b * nb + n, 0))],
        out_specs=[pl.BlockSpec((S, ATT_W + 2 * KV_W), lambda b, n: (b, 0)),
                   _full_spec((8, LANE)), _full_spec((8, LANE))],
        out_shape=[jax.ShapeDtypeStruct((T, ATT_W + 2 * KV_W), F32),
                   jax.ShapeDtypeStruct((8, LANE), F32), jax.ShapeDtypeStruct((8, LANE), F32)],
        scratch_shapes=[pltpu.VMEM((ATT_H, BQ, 2 * BQ), F32), pltpu.VMEM((ATT_H, BQ, 2 * BQ), F32)],
        compiler_params=_cp("arbitrary", "arbitrary"),
    )(sinks, rel_bias, table, u_att, u_att, u_att, u_att, u_att, dy)


def _head_of(i):
    return lax.shift_right_logical(i, 6)


def _head_mask(shape):
    return (_head_of(_iota(shape, 0)) == _head_of(_iota(shape, 1))).astype(F32)


def _dn_point(c, uba, alog, dtb):
    s = c * _sigmoid(c)
    qt, kt, vt = s[:, 0:256], s[:, 256:512], s[:, 512:768]
    ones_bd = _head_mask((DN_W, DN_W))
    q = qt * lax.rsqrt(_mm3(qt * qt, ones_bd) + EPS) * (HD ** -0.5)
    k = kt * lax.rsqrt(_mm3(kt * kt, ones_bd) + EPS)
    sel = _head_of(_iota((LANE, DN_W), 1))
    row = _iota((LANE, DN_W), 0)
    braw = _mm3(uba, (row == sel).astype(F32))
    araw = _mm3(uba, (row == sel + DN_H).astype(F32)) + dtb
    beta = _sigmoid(braw)
    g = -jnp.exp(alog) * (jnp.maximum(araw, 0.0) + jnp.log(1.0 + jnp.exp(-jnp.abs(araw))))
    return q, k, vt, g, beta


def dn_point_fwd(c, uba, alog, dtb, name):
    T = c.shape[0]
    tm = min(TOK_TILE, T)

    def body(c_ref, u_ref, al_ref, dt_ref, *outs):
        for ref, val in zip(outs, _dn_point(c_ref[...], u_ref[...], al_ref[...], dt_ref[...])):
            ref[...] = val

    return pl.pallas_call(
        body, name=name, grid=(T // tm,),
        in_specs=[_row_spec(tm, 768), _row_spec(tm, LANE), _full_spec((1, DN_W)), _full_spec((1, DN_W))],
        out_specs=[_row_spec(tm, DN_W)] * 5,
        out_shape=[jax.ShapeDtypeStruct((T, DN_W), F32)] * 5,
        compiler_params=_cp("parallel"),
    )(c, uba, alog, dtb)


def dn_point_bwd(c, uba, alog, dtb, douts, name):
    T = c.shape[0]
    tm = min(TOK_TILE, T)

    def body(c_ref, u_ref, al_ref, dt_ref, dq, dk, dv, dg, db, dc_ref, du_ref, dvec_ref):
        @pl.when(pl.program_id(0) == 0)
        def _():
            dvec_ref[...] = jnp.zeros_like(dvec_ref)

        _, vjp = jax.vjp(_dn_point, c_ref[...], u_ref[...], al_ref[...], dt_ref[...])
        dc, du, dal, ddt = vjp((dq[...], dk[...], dv[...], dg[...], db[...]))
        dc_ref[...] = dc
        du_ref[...] = du
        fold = (_iota((LANE, DN_W), 0) == _head_of(_iota((LANE, DN_W), 1))).astype(F32)
        both = jnp.concatenate([dal, ddt, jnp.zeros((6, DN_W), F32)], axis=0)
        dvec_ref[...] += _mmx_nt(both, fold)

    return pl.pallas_call(
        body, name=name, grid=(T // tm,),
        in_specs=[_row_spec(tm, 768), _row_spec(tm, LANE), _full_spec((1, DN_W)), _full_spec((1, DN_W))]
        + [_row_spec(tm, DN_W)] * 5,
        out_specs=[_row_spec(tm, 768), _row_spec(tm, LANE), _full_spec((8, LANE))],
        out_shape=[jax.ShapeDtypeStruct((T, 768), F32), jax.ShapeDtypeStruct((T, LANE), F32),
                   jax.ShapeDtypeStruct((8, LANE), F32)],
        compiler_params=_cp("arbitrary"),
    )(c, uba, alog, dtb, *douts)


def _unit_lower_inverses(lmats):
    eye = (_iota(lmats[0].shape, 0) == _iota(lmats[0].shape, 1)).astype(F32)
    tinvs = [eye - lm for lm in lmats]
    pws = list(lmats)
    for _ in range(5):
        pws = [_mm3(pw, pw) for pw in pws]
        tinvs = [t + _mm3(t, pw) for t, pw in zip(tinvs, pws)]
    return tuple(tinvs)


def _inverse_bwd(tinv, d):
    return -_mm3_nt(_mm3_tn(tinv, d), tinv)


@jax.custom_vjp
def _tri_invs(lmats):
    return _unit_lower_inverses(lmats)


def _tri_invs_fwd(lmats):
    tinvs = _unit_lower_inverses(lmats)
    return tinvs, tinvs


_tri_invs.defvjp(_tri_invs_fwd, lambda tinvs, ds: (tuple(_inverse_bwd(t, d) for t, d in zip(tinvs, ds)),))


@jax.custom_vjp
def _tri_inv_known(lmat, tinv):
    return tinv


_tri_inv_known.defvjp(lambda lmat, tinv: (tinv, tinv),
                      lambda tinv, d: (_inverse_bwd(tinv, d), jnp.zeros_like(tinv)))


DN_SUB = 4


def _dn_stack(x):
    return jnp.concatenate([x, x, x, x], axis=0) * _head_mask((DN_W, DN_W))


def _dn_pre_inverse(q, k, v, g, beta):
    hm = _head_mask((DN_W, DN_W))
    ri = _iota((DN_W, DN_W), 0) & (CHUNK - 1)
    ci = _iota((DN_W, DN_W), 1) & (CHUNK - 1)
    tri64 = (_iota((CHUNK, CHUNK), 0) >= _iota((CHUNK, CHUNK), 1)).astype(F32)
    gc = _mm3(tri64, g)
    ks = _dn_stack(k)
    gcol = jnp.sum(_dn_stack(gc), axis=1, keepdims=True) * (1.0 / HD)
    gmat = jnp.broadcast_to(gcol, (DN_W, DN_W))
    decay = jnp.exp(jnp.minimum(gmat - gmat.T, 0.0))
    lmat = _mm_nt(_dn_stack(k * beta), ks) * decay * (hm * (ri > ci).astype(F32))
    att = _mm_nt(_dn_stack(q), ks) * decay * (hm * (ri >= ci).astype(F32))
    return lmat, att, gc


def _dn_post_inverse(q, k, v, g, beta, tinv, att, gc):
    glast = jnp.sum(g, axis=0, keepdims=True)
    eg = jnp.exp(gc)
    u = _mm(tinv, _dn_stack(v * beta))
    w = _mm(tinv, _dn_stack(k * beta * eg))
    return u, w, att, _dn_stack(q * eg), _dn_stack(k * jnp.exp(glast - gc)), jnp.exp(glast), tinv


def _dn_apply(state, prep):
    u, w, att, qe, kd, eglast, _ = prep
    vn = u - _mm(w, state)
    o4 = _mm(qe, state) + _mm(att, vn)
    o = o4[0:64] + o4[64:128] + o4[128:192] + o4[192:256]
    return o, state * eglast + _mm_tn(kd, vn)


def _dn_chunks(states, q, k, v, g, beta, knowns=None):
    nb = len(q)
    n = q[0].shape[0] // CHUNK
    chunks = [[tuple(x[b][c * CHUNK:(c + 1) * CHUNK] for x in (q, k, v, g, beta)) for c in range(n)]
              for b in range(nb)]
    pre = [[_dn_pre_inverse(*ch) for ch in seq] for seq in chunks]
    lmats = [p[0] for seq in pre for p in seq]
    if knowns is None:
        flat = _tri_invs(tuple(lmats))
    else:
        flat = [_tri_inv_known(lm, kn) for lm, kn in zip(lmats, [kn for seq in knowns for kn in seq])]
    tinvs = [flat[b * n:(b + 1) * n] for b in range(nb)]
    preps = [[_dn_post_inverse(*chunks[b][c], tinvs[b][c], pre[b][c][1], pre[b][c][2]) for c in range(n)]
             for b in range(nb)]
    states = list(states)
    outs = [[] for _ in range(nb)]
    for c in range(n):
        for b in range(nb):
            o, states[b] = _dn_apply(states[b], preps[b][c])
            outs[b].append(o)
    return tuple(jnp.concatenate(o, axis=0) for o in outs), tuple(states), tinvs


def _dn_scan_specs(nb, S, reverse):
    rows = DN_SUB * CHUNK
    ns = S // rows
    at = (lambda t: ns - 1 - t) if reverse else (lambda t: t)
    return (pl.BlockSpec((nb, rows, DN_W), lambda t: (0, at(t), 0)),
            pl.BlockSpec((nb, 1, DN_W, DN_W), lambda t: (0, at(t), 0, 0)),
            pl.BlockSpec((nb, DN_SUB, DN_W, DN_W), lambda t: (0, at(t), 0, 0)))


def dn_scan_fwd(q, k, v, g, beta, S, name):
    T = q.shape[0]
    nb = T // S
    ns = S // (DN_SUB * CHUNK)
    seqs = range(nb)

    def body(q_ref, k_ref, v_ref, g_ref, b_ref, o_ref, st_ref, ti_ref, s_s):
        @pl.when(pl.program_id(0) == 0)
        def _():
            s_s[...] = jnp.zeros_like(s_s)

        per = lambda ref: tuple(ref[b] for b in seqs)
        sts = per(s_s)
        for b in seqs:
            st_ref[b, 0] = sts[b]
        outs, news, tinvs = _dn_chunks(sts, per(q_ref), per(k_ref), per(v_ref), per(g_ref), per(b_ref))
        for b in seqs:
            o_ref[b] = outs[b]
            s_s[b] = news[b]
            for c, tinv in enumerate(tinvs[b]):
                ti_ref[b, c] = tinv

    spec, st_spec, ti_spec = _dn_scan_specs(nb, S, False)
    o, states, tinvs = pl.pallas_call(
        body, name=name, grid=(ns,),
        in_specs=[spec] * 5,
        out_specs=[spec, st_spec, ti_spec],
        out_shape=[jax.ShapeDtypeStruct((nb, S, DN_W), F32),
                   jax.ShapeDtypeStruct((nb, ns, DN_W, DN_W), F32),
                   jax.ShapeDtypeStruct((nb, S // CHUNK, DN_W, DN_W), F32)],
        scratch_shapes=[pltpu.VMEM((nb, DN_W, DN_W), F32)],
        compiler_params=_cp("arbitrary"),
    )(*(t.reshape(nb, S, DN_W) for t in (q, k, v, g, beta)))
    return o.reshape(T, DN_W), states, tinvs


def dn_scan_bwd(q, k, v, g, beta, states, tinvs, do, S, name):
    T = q.shape[0]
    nb = T // S
    ns = S // (DN_SUB * CHUNK)
    seqs = range(nb)

    def body(q_ref, k_ref, v_ref, g_ref, b_ref, st_ref, ti_ref, do_ref, dq, dk, dv, dg, db, ds_s):
        @pl.when(pl.program_id(0) == 0)
        def _():
            ds_s[...] = jnp.zeros_like(ds_s)

        per = lambda ref: tuple(ref[b] for b in seqs)
        knowns = [[ti_ref[b, c] for c in range(DN_SUB)] for b in seqs]
        _, vjp = jax.vjp(lambda *args: _dn_chunks(*args, knowns=knowns)[:2],
                         tuple(st_ref[b, 0] for b in seqs), per(q_ref), per(k_ref), per(v_ref), per(g_ref),
                         per(b_ref))
        grads = vjp((per(do_ref), per(ds_s)))
        for b in seqs:
            ds_s[b] = grads[0][b]
            for ref, val in zip((dq, dk, dv, dg, db), grads[1:]):
                ref[b] = val[b]

    spec, st_spec, ti_spec = _dn_scan_specs(nb, S, True)
    res = pl.pallas_call(
        body, name=name, grid=(ns,),
        in_specs=[spec] * 5 + [st_spec, ti_spec, spec],
        out_specs=[spec] * 5,
        out_shape=[jax.ShapeDtypeStruct((nb, S, DN_W), F32)] * 5,
        scratch_shapes=[pltpu.VMEM((nb, DN_W, DN_W), F32)],
        compiler_params=_cp("arbitrary"),
    )(*(t.reshape(nb, S, DN_W) for t in (q, k, v, g, beta)), states, tinvs, do.reshape(nb, S, DN_W))
    return [r.reshape(T, DN_W) for r in res]


def _dn_gate(o, z, nl):
    ms = _mm3(o * o, _head_mask((DN_W, DN_W))) * (1.0 / HD)
    return o * lax.rsqrt(ms + EPS) * nl * (z * _sigmoid(z))


def dn_gate_fwd(o, u_dn, nl, name):
    T = o.shape[0]
    tm = min(LIGHT_TILE, T)

    def body(o_ref, z_ref, n_ref, y_ref):
        y_ref[...] = _dn_gate(o_ref[...], z_ref[...], n_ref[...])

    return pl.pallas_call(
        body, name=name, grid=(T // tm,),
        in_specs=[_row_spec(tm, DN_W), pl.BlockSpec((tm, DN_W), lambda i: (i, 3)), _full_spec((1, DN_W))],
        out_specs=_row_spec(tm, DN_W),
        out_shape=jax.ShapeDtypeStruct((T, DN_W), F32),
        compiler_params=_cp("parallel"),
    )(o, u_dn, nl)


def dn_gate_bwd(o, u_dn, nl, dy, name):
    T = o.shape[0]
    tm = min(LIGHT_TILE, T)

    def body(o_ref, z_ref, n_ref, dy_ref, do_ref, dz_ref, dn_ref):
        @pl.when(pl.program_id(0) == 0)
        def _():
            dn_ref[...] = jnp.zeros_like(dn_ref)

        _, vjp = jax.vjp(_dn_gate, o_ref[...], z_ref[...], n_ref[...])
        do, dz, dn = vjp(dy_ref[...])
        do_ref[...] = do
        dz_ref[...] = dz
        fold = (_iota((LANE, DN_W), 0) == (_iota((LANE, DN_W), 1) & (HD - 1))).astype(F32)
        dn_ref[...] += _mmx_nt(jnp.concatenate([dn, jnp.zeros((7, DN_W), F32)], axis=0), fold)

    return pl.pallas_call(
        body, name=name, grid=(T // tm,),
        in_specs=[_row_spec(tm, DN_W), pl.BlockSpec((tm, DN_W), lambda i: (i, 3)), _full_spec((1, DN_W)),
                  _row_spec(tm, DN_W)],
        out_specs=[_row_spec(tm, DN_W), _row_spec(tm, DN_W), _full_spec((8, LANE))],
        out_shape=[jax.ShapeDtypeStruct((T, DN_W), F32), jax.ShapeDtypeStruct((T, DN_W), F32),
                   jax.ShapeDtypeStruct((8, LANE), F32)],
        compiler_params=_cp("arbitrary"),
    )(o, u_dn, nl, dy)


Y_SPLITS = (LRU_W, ATT_W, DN_W)
Y_OFFS = (0, LRU_W, LRU_W + ATT_W)


ROWS_DEV = D // N_DEV


def _dev_rows_spec(row0):
    return pl.BlockSpec((N_DEV, ROWS_DEV, D), lambda *_: (0, row0 // ROWS_DEV, 0))


def _dev_rows(w_ref, off, n):
    return w_ref[off // ROWS_DEV:(off + n) // ROWS_DEV].reshape(n, D)


def wout_fwd(h, ys, wb, name):
    T = h.shape[0]
    tm = min(LIGHT_TILE, T)

    def body(h_ref, y0, y1, y2, w_ref, o_ref, yc_ref):
        for ref, off, n in zip((y0, y1, y2), Y_OFFS, Y_SPLITS):
            yc_ref[:, off:off + n] = ref[...].astype(BF16)
        o_ref[...] = h_ref[...] + _mm(yc_ref[...], _dev_rows(w_ref, 0, D))

    return pl.pallas_call(
        body, name=name, grid=(T // tm,),
        in_specs=[_row_spec(tm, D)] + [_row_spec(tm, n) for n in Y_SPLITS] + [_dev_rows_spec(R_WOUT)],
        out_specs=[_row_spec(tm, D), _row_spec(tm, D)],
        out_shape=[jax.ShapeDtypeStruct((T, D), F32), jax.ShapeDtypeStruct((T, D), BF16)],
        compiler_params=_cp("parallel"),
    )(h, *ys, wb)


def wout_bwd(dy, wb, name):
    T = dy.shape[0]
    tm = min(LIGHT_TILE, T)

    def body(dy_ref, w_ref, d0, d1, d2):
        dys = _mm_nt(dy_ref[...], _dev_rows(w_ref, 0, D))
        for ref, off, n in zip((d0, d1, d2), Y_OFFS, Y_SPLITS):
            ref[...] = dys[:, off:off + n]

    return pl.pallas_call(
        body, name=name, grid=(T // tm,),
        in_specs=[_row_spec(tm, D), _dev_rows_spec(R_WOUT)],
        out_specs=[_row_spec(tm, n) for n in Y_SPLITS],
        out_shape=[jax.ShapeDtypeStruct((T, n), F32) for n in Y_SPLITS],
        compiler_params=_cp("parallel"),
    )(dy, wb)


def ple_fwd(h, g, pe, wg, wp, name):
    T = h.shape[0]
    tm = min(LIGHT_TILE, T)

    def body(h_ref, g_ref, p_ref, wg_ref, wp_ref, o_ref):
        hh = h_ref[...]
        xn = _rms(hh, g_ref[...])[0]
        o_ref[...] = hh + _sigmoid(_mm(xn, _dev_rows(wg_ref, 0, D))) * _mm(p_ref[...], wp_ref[...])

    return pl.pallas_call(
        body, name=name, grid=(T // tm,),
        in_specs=[_row_spec(tm, D), _full_spec((1, D)), _row_spec(tm, PLE), _dev_rows_spec(R_PGATE),
                  _full_spec((PLE, D))],
        out_specs=_row_spec(tm, D),
        out_shape=jax.ShapeDtypeStruct((T, D), F32),
        compiler_params=_cp("parallel"),
    )(h, g, pe, wg, wp)


def ple_bwd(h, dy, g, pe, wg, wp, name):
    T = h.shape[0]
    tm = min(TOK_TILE, T)

    def body(h_ref, dy_ref, g_ref, p_ref, wg_ref, wp_ref, dh_ref, dz_ref, dpp_ref, xn_ref, dn_ref):
        @pl.when(pl.program_id(0) == 0)
        def _():
            dn_ref[...] = jnp.zeros_like(dn_ref)

        gg = g_ref[...]
        dy = dy_ref[...]
        xn, xhat, rstd = _rms(h_ref[...], gg)
        wg = _dev_rows(wg_ref, 0, D)
        gate = _sigmoid(_mm(xn, wg))
        pp = _mm(p_ref[...], wp_ref[...])
        dz = dy * pp * gate * (1.0 - gate)
        dz_ref[...] = dz.astype(BF16)
        dpp_ref[...] = (dy * gate).astype(BF16)
        xn_ref[...] = xn.astype(BF16)
        dh, dn = _rms_bwd(_mm_nt(dz, wg), xhat, rstd, gg)
        dh_ref[...] = dy + dh
        dn_ref[...] += dn

    return pl.pallas_call(
        body, name=name, grid=(T // tm,),
        in_specs=[_row_spec(tm, D), _row_spec(tm, D), _full_spec((1, D)), _row_spec(tm, PLE),
                  _dev_rows_spec(R_PGATE), _full_spec((PLE, D))],
        out_specs=[_row_spec(tm, D), _row_spec(tm, D), _row_spec(tm, D), _row_spec(tm, D), _full_spec((1, D))],
        out_shape=[jax.ShapeDtypeStruct((T, D), F32), jax.ShapeDtypeStruct((T, D), BF16),
                   jax.ShapeDtypeStruct((T, D), BF16), jax.ShapeDtypeStruct((T, D), BF16),
                   jax.ShapeDtypeStruct((1, D), F32)],
        compiler_params=_cp("arbitrary"),
    )(h, dy, g, pe, wg, wp)


def loss_head(h, g, target, name):
    T = h.shape[0]
    tm = min(LIGHT_TILE, T)

    def body(h_ref, g_ref, t_ref, loss_ref, dh_ref, dn_ref):
        @pl.when(pl.program_id(0) == 0)
        def _():
            dn_ref[...] = jnp.zeros_like(dn_ref)
            loss_ref[...] = jnp.zeros_like(loss_ref)

        gg = g_ref[...]
        y, xhat, rstd = _rms(h_ref[...], gg)
        err = y - t_ref[...]
        per_tok = jnp.mean(err * err, axis=-1, keepdims=True)
        loss_ref[...] += 0.5 * jnp.sum(per_tok, axis=0, keepdims=True)
        dh, dn = _rms_bwd(err * (1.0 / D), xhat, rstd, gg)
        dh_ref[...] = dh
        dn_ref[...] += dn

    return pl.pallas_call(
        body, name=name, grid=(T // tm,),
        in_specs=[_row_spec(tm, D), _full_spec((1, D)), _row_spec(tm, D)],
        out_specs=[_full_spec((8, LANE)), _row_spec(tm, D), _full_spec((1, D))],
        out_shape=[jax.ShapeDtypeStruct((8, LANE), F32), jax.ShapeDtypeStruct((T, D), F32),
                   jax.ShapeDtypeStruct((1, D), F32)],
        compiler_params=_cp("arbitrary"),
    )(h, g, target)


def _block_diag(w):
    return jnp.einsum('hij,hk->hikj', w, jnp.eye(4, dtype=w.dtype)).reshape(LRU_W, LRU_W)


def _layer_consts(W, l):
    row = lambda v: v.reshape(1, -1)
    zeros = jnp.zeros((5, LRU_W), F32)
    return dict(
        wa=_block_diag(W["lru_w_a"][l]), wx=_block_diag(W["lru_w_x"][l]),
        lru_vec=jnp.concatenate([row(W["lru_b_a"][l]), row(W["lru_b_x"][l]), row(W["lru_lambda"][l]), zeros], 0),
        lru_cb=row(W["lru_conv_b"][l]),
        sinks=W["attn_sinks"][l], rel=W["rel_bias"].reshape(-1),
        dn_cb=jnp.zeros((1, 3 * DN_W), F32),
        alog=row(jnp.repeat(W["dn_a_log"][l], HD)), dtb=row(jnp.repeat(W["dn_dt_bias"][l], HD)),
        dn_nl=row(jnp.tile(W["dn_norm"][l], DN_H)),
    )


def _layer_fwd(h0, pe, W, l, S, need=None):
    n = f"l{l}_"
    c_ = _layer_consts(W, l)
    row = lambda v: v.reshape(1, -1)
    need = need or (lambda *_: None)
    need(l, "f1", h0)
    h1, *ffn1_kept = ffn_fwd(h0, row(W["ffn1_norm"][l]), W["f1_cols"][l], W["f1_rows"][l], n + "ffn1_fwd")
    need(l, "in", h1)
    u_lru, u_att, u_dn, u_ba, xn_mix = mixin_fwd(h1, row(W["mix_norm"][l]), W["w_in"][l], n + "mixin_fwd")
    xr = conv_fwd(u_lru, W["lru_conv_w"][l], c_["lru_cb"], S, 0, LRU_W, n + "lru_conv_fwd")
    y_lru = lru_fwd(xr, u_lru, c_["wa"], c_["wx"], c_["lru_vec"], S, n + "lru_fwd")
    y_att = attn_fwd(u_att, c_["sinks"], c_["rel"], S, n + "attn_fwd")
    cc = conv_fwd(u_dn, W["dn_conv_w"][l], c_["dn_cb"], S, 0, 3 * DN_W, n + "dn_conv_fwd")
    q, k, v, g, beta = dn_point_fwd(cc, u_ba, c_["alog"], c_["dtb"], n + "dn_point_fwd")
    o, states, tinvs = dn_scan_fwd(q, k, v, g, beta, S, n + "dn_scan_fwd")
    y_dn = dn_gate_fwd(o, u_dn, c_["dn_nl"], n + "dn_gate_fwd")
    need(l, "rest", y_dn)
    h2, ycat = wout_fwd(h1, (y_lru, y_att, y_dn), W["r_rows"][l], n + "wout_fwd")
    h3, *ffn2_kept = ffn_fwd(h2, row(W["ffn2_norm"][l]), W["r_cols"][l], W["r_rows"][l], n + "ffn2_fwd")
    h4 = ple_fwd(h3, row(W["ple_norm"][l]), pe, W["r_rows"][l], W["ple_w_proj"][l], n + "ple_fwd")
    saved = dict(ffn1=ffn1_kept, ffn2=ffn2_kept, h0=h0, h1=h1, h2=h2, h3=h3, u_lru=u_lru, u_att=u_att, u_dn=u_dn,
                 u_ba=u_ba, xn_mix=xn_mix, xr=xr, cc=cc, q=q, k=k, v=v, g=g, beta=beta, o=o, states=states, tinvs=tinvs, ycat=ycat)
    return h4, saved


GM_WOUT, GM_PGATE, GM_WIN, GM_PPROJ, GM_END, GM_ROWS = 0, 128, 256, 560, 592, 640


def _layer_bwd(dh4, sv, pe, W, l, S, token=None, on_piece=None):
    n = f"l{l}_"
    c_ = _layer_consts(W, l)
    row = lambda v: v.reshape(1, -1)
    behind = lambda v, tok: v if tok is None else v + tok.astype(v.dtype)
    on_piece = on_piece or (lambda *_: None)
    G = {"mix_rows": jnp.zeros((N_DEV, GM_ROWS, D), BF16)}
    dh3, dz, dpp, xn_p, dn = ple_bwd(sv["h3"], dh4, behind(row(W["ple_norm"][l]), token), pe, W["r_rows"][l],
                                     W["ple_w_proj"][l], n + "ple_bwd")
    G["ple_norm"] = dn[0]
    G["mix_rows"] = grad_rows(xn_p, dz, G["mix_rows"], GM_PGATE, ROWS_DEV, n + "d_ple_w_gate")
    d_proj = matmul_tn(pe, dpp, n + "d_ple_w_proj")
    d_proj = d_proj.reshape(PLE, N_DEV, D // N_DEV).transpose(1, 0, 2).reshape(N_DEV, GM_END - GM_PPROJ, D)
    G["mix_rows"] = lax.dynamic_update_slice(G["mix_rows"], d_proj, (0, GM_PPROJ, 0))

    def ffn_back(which, cols_w, rows_w, h_in, dy, tok, one_by_one):
        gt, up, xn = sv[which]
        dh, dgt, dup, act, dn_ = ffn_bwd(h_in, dy, behind(row(W[which + "_norm"][l]), tok), gt, up, cols_w, rows_w,
                                         n + which + "_bwd")
        G[which + "_norm"] = dn_[0]
        zeros_rows = jnp.zeros((N_DEV, SHP, D), BF16)
        if one_by_one:
            G[which + "_gate"] = grad_cols(xn, dgt, lax.empty((1, D, FFP), BF16), 0, n + "d_" + which + "_w_gate")
            tok = on_piece(l, which + "_gate", (G[which + "_gate"],))
            G[which + "_up"] = grad_cols(xn, dup, behind(jnp.zeros((1, D, FFP), BF16), tok), 0,
                                         n + "d_" + which + "_w_up")
            tok = on_piece(l, which + "_up", (G[which + "_up"],))
            G[which + "_down"] = grad_rows(act, dy, behind(zeros_rows, tok), 0, SHP, n + "d_" + which + "_w_down",
                                           scale=0.5)
            return dh, on_piece(l, which + "_down", (G[which + "_down"],))
        G[which + "_cols"] = grad_cols_pair(xn, dgt, dup, n + "d_" + which + "_w_gate_up")
        G[which + "_rows"] = grad_rows(act, dy, lax.empty((N_DEV, SHP, D), BF16), 0, SHP,
                                       n + "d_" + which + "_w_down", scale=0.5)
        return dh, on_piece(l, which, (G[which + "_cols"], G[which + "_rows"]))

    dh2, tok = ffn_back("ffn2", W["r_cols"][l], W["r_rows"][l], sv["h2"], dh3, None, False)
    dy_lru, dy_att, dy_dn = wout_bwd(dh2, W["r_rows"][l], n + "wout_bwd")
    G["mix_rows"] = grad_rows(sv["ycat"], dh2, G["mix_rows"], GM_WOUT, ROWS_DEV, n + "d_w_out")
    do, dz_dn, dnn = dn_gate_bwd(sv["o"], sv["u_dn"], behind(c_["dn_nl"], tok), dy_dn, n + "dn_gate_bwd")
    dqkvgb = dn_scan_bwd(sv["q"], sv["k"], sv["v"], sv["g"], sv["beta"], sv["states"], sv["tinvs"], do, S,
                         n + "dn_scan_bwd")
    dcc, du_ba, dvec_dn = dn_point_bwd(sv["cc"], sv["u_ba"], c_["alog"], c_["dtb"], dqkvgb, n + "dn_point_bwd")
    dqkv, dwb_dn = conv_bwd(sv["u_dn"], dcc, W["dn_conv_w"][l], S, 0, 3 * DN_W, n + "dn_conv_bwd")
    G["dn_norm"] = dnn[0, 0:HD]
    G["dn_a_log"] = dvec_dn[0, 0:DN_H]
    G["dn_dt_bias"] = dvec_dn[1, 0:DN_H]
    G["dn_conv_w"] = dwb_dn[0:4]
    du_att, drel, dsk = attn_bwd(sv["u_att"], dy_att, c_["sinks"], c_["rel"], S, n + "attn_bwd")
    G["attn_sinks"] = dsk[:, 0]
    G["rel_bias"] = drel[:, 0:REL_BUCKETS].T
    dxr, dgt_lru, dwa, dwx, dvec = lru_bwd(sv["xr"], sv["u_lru"], dy_lru, c_["wa"], c_["wx"], c_["lru_vec"], S,
                                           n + "lru_bwd")
    dx_lru, dwb_lru = conv_bwd(sv["u_lru"], dxr, W["lru_conv_w"][l], S, 0, LRU_W, n + "lru_conv_bwd")
    diag = lambda m: jnp.stack([m[c, HD * e:HD * (e + 1), HD * e:HD * (e + 1)] for c in range(2) for e in range(2)])
    G["lru_w_a"], G["lru_w_x"] = diag(dwa), diag(dwx)
    G["lru_b_a"], G["lru_b_x"], G["lru_lambda"] = dvec[0], dvec[1], dvec[2]
    G["lru_conv_w"], G["lru_conv_b"] = dwb_lru[0:4], dwb_lru[4]
    dh1, du_cat, dn = mixin_bwd(sv["h1"], dh2, row(W["mix_norm"][l]), W["w_in"][l],
                                (dx_lru, dgt_lru, du_att, dqkv, dz_dn, du_ba), n + "mixin_bwd")
    G["mix_norm"] = dn[0]
    d_in = matmul_tn(sv["xn_mix"], du_cat, n + "d_w_in")[:, :D_IN]
    d_in = d_in.reshape(D, N_DEV, D_IN // N_DEV).transpose(1, 0, 2).reshape(N_DEV, WIN_ROWS, D)
    d_in = jnp.pad(d_in, ((0, 0), (0, GM_PPROJ - GM_WIN - WIN_ROWS), (0, 0)))
    G["mix_rows"] = lax.dynamic_update_slice(G["mix_rows"], d_in, (0, GM_WIN, 0))
    tok = on_piece(l, "mix", (G["mix_rows"],))
    dh0, tok = ffn_back("ffn1", W["f1_cols"][l], W["f1_rows"][l], sv["h0"], dh1, tok, l == 0)
    return dh0, G, tok


def _core(x, pe, W, target, S, need=None, on_piece=None):
    h = x
    saved = []
    for l in range(DEPTH):
        h, sv = _layer_fwd(h, pe[l], W, l, S, need)
        saved.append(sv)
    loss_tile, dh, dfn = loss_head(h, W["final_norm"].reshape(1, -1), target, "loss_head")
    grads = [None] * DEPTH
    token = None
    for l in reversed(range(DEPTH)):
        dh, grads[l], token = _layer_bwd(dh, saved[l], pe[l], W, l, S, token, on_piece)
    return loss_tile[0, 0], dh, grads, dfn[0]


MESH_ID = pl.DeviceIdType.MESH
ANY_SPEC = pl.BlockSpec(memory_space=pl.ANY)
AXES = ("x", "y", "c")


def _my_pos():
    return lax.axis_index("x"), lax.axis_index("y"), lax.axis_index("c")


def _slot_of(px, py, pc):
    return 4 * px + 2 * py + pc


def all_gather(x, name):
    R, C = x.shape

    def body(x_ref, out_ref, send_sems, recv_sems, local_sem):
        mx, my, mc = _my_pos()
        me, sibling = (mx, my, mc), (mx, my, 1 - mc)
        chips = [(1 - mx, my), (mx, 1 - my), (1 - mx, 1 - my)]

        def copy(k, block, to, src=None):
            dst = out_ref.at[_slot_of(*block)]
            return pltpu.make_async_remote_copy(
                src_ref=dst if src is None else src, dst_ref=dst,
                send_sem=send_sems.at[k], recv_sem=recv_sems.at[k],
                device_id=to, device_id_type=MESH_ID)

        mine = pltpu.make_async_copy(x_ref, out_ref.at[_slot_of(*me)], local_sem)
        mine.start()
        first = [copy(0, me, sibling, src=x_ref)]
        first += [copy(1 + j, me, (*chip, mc), src=x_ref) for j, chip in enumerate(chips)]
        for cp in first:
            cp.start()
        passed = [copy(4 + j, (*chip, mc), sibling) for j, chip in enumerate(chips)]
        for j, chip in enumerate(chips):
            copy(1 + j, (*chip, mc), me).wait_recv()
            passed[j].start()
        copy(0, sibling, me).wait_recv()
        for j, chip in enumerate(chips):
            copy(4 + j, (*chip, 1 - mc), me).wait_recv()
        for cp in first + passed:
            cp.wait_send()
        mine.wait()

    return pl.pallas_call(
        body, name=name,
        out_shape=jax.ShapeDtypeStruct((N_DEV, R, C), x.dtype),
        in_specs=[ANY_SPEC], out_specs=ANY_SPEC,
        scratch_shapes=[pltpu.SemaphoreType.DMA((7,)), pltpu.SemaphoreType.DMA((7,)), pltpu.SemaphoreType.DMA],
    )(x)


def _col_window(ref, slot):
    return ref.at[:, pl.ds(pl.multiple_of(slot * SHP, LANE), SHP)]


def gather_layer(a_sh, b_sh, name):
    def body(a_ref, b_ref, ao_ref, bo_ref, send_sems, recv_sems, local_sems):
        mx, my, mc = _my_pos()
        me, sibling = (mx, my, mc), (mx, my, 1 - mc)
        chips = [(1 - mx, my), (mx, 1 - my), (1 - mx, 1 - my)]

        def copies(k, block, to, own=False):
            slot = _slot_of(*block)
            dsts = (_col_window(ao_ref, slot), bo_ref.at[slot])
            srcs = (a_ref, b_ref) if own else dsts
            return [pltpu.make_async_remote_copy(
                src_ref=s, dst_ref=d, send_sem=send_sems.at[2 * k + i], recv_sem=recv_sems.at[2 * k + i],
                device_id=to, device_id_type=MESH_ID) for i, (s, d) in enumerate(zip(srcs, dsts))]

        mine = [pltpu.make_async_copy(a_ref, _col_window(ao_ref, _slot_of(*me)), local_sems.at[0]),
                pltpu.make_async_copy(b_ref, bo_ref.at[_slot_of(*me)], local_sems.at[1])]
        for cp in mine:
            cp.start()
        first = copies(0, me, sibling, own=True)
        for j, chip in enumerate(chips):
            first += copies(1 + j, me, (*chip, mc), own=True)
        for cp in first:
            cp.start()
        passed = []
        for j, chip in enumerate(chips):
            for cp in copies(1 + j, (*chip, mc), me):
                cp.wait_recv()
            fwd = copies(4 + j, (*chip, mc), sibling)
            for cp in fwd:
                cp.start()
            passed += fwd
        for cp in copies(0, sibling, me):
            cp.wait_recv()
        for j, chip in enumerate(chips):
            for cp in copies(4 + j, (*chip, 1 - mc), me):
                cp.wait_recv()
        for cp in first + passed:
            cp.wait_send()
        for cp in mine:
            cp.wait()

    return pl.pallas_call(
        body, name=name,
        out_shape=[jax.ShapeDtypeStruct((a_sh.shape[0], FFP), a_sh.dtype),
                   jax.ShapeDtypeStruct((N_DEV,) + b_sh.shape, b_sh.dtype)],
        in_specs=[ANY_SPEC, ANY_SPEC], out_specs=[ANY_SPEC, ANY_SPEC],
        scratch_shapes=[pltpu.SemaphoreType.DMA((14,)), pltpu.SemaphoreType.DMA((14,)),
                        pltpu.SemaphoreType.DMA((2,))],
    )(a_sh, b_sh)


HBM_SPEC = pl.BlockSpec(memory_space=pltpu.HBM)
SEM_SPEC = pl.BlockSpec(memory_space=pltpu.SEMAPHORE)
SPLIT_EFFECT = pltpu.CompilerParams(has_side_effects=pltpu.SideEffectType.DATAFLOW_SIDE_EFFECTING)


def _split_ends(mode, src_ref, dst_ref, src_slot, dst_slot):
    cols = mode.endswith("cols")
    if mode.startswith("gather"):
        return src_ref, (_col_window(dst_ref, dst_slot) if cols else dst_ref.at[dst_slot])
    return (_col_window(src_ref, src_slot) if cols else src_ref.at[src_slot]), dst_ref.at[dst_slot]


def _split_peers():
    mx, my, mc = _my_pos()
    for r in range(1, N_DEV):
        peer = (1 - mx if r & 4 else mx, 1 - my if r & 2 else my, 1 - mc if r & 1 else mc)
        yield r - 1, peer, _slot_of(*peer)


def split_start(modes, srcs, dsts, name, after=()):
    n = len(modes)
    m = len(after)

    def body(*refs):
        send_sems, recv_sems, token = refs[2 * n + m], refs[2 * n + m + 1], refs[-1]
        mine = _slot_of(*_my_pos())
        for k, peer, ps in _split_peers():
            for i in range(n):
                src, dst = _split_ends(modes[i], refs[i], refs[n + i], ps, mine)
                pltpu.make_async_remote_copy(
                    src_ref=src, dst_ref=dst, send_sem=send_sems.at[n * k + i], recv_sem=recv_sems.at[n * k + i],
                    device_id=peer, device_id_type=MESH_ID).start()
        for i in range(n):
            src, dst = _split_ends(modes[i], refs[i], refs[n + i], mine, mine)
            pltpu.make_async_copy(src, dst, recv_sems.at[n * (N_DEV - 1) + i]).start()
        token[...] = jnp.zeros_like(token)

    bufs = tuple(srcs) + tuple(dsts)
    sems = pltpu.SemaphoreType.DMA((n * N_DEV,))
    res = pl.pallas_call(
        body, name=name,
        out_shape=(sems, sems) + tuple(pltpu.HBM(t.shape, t.dtype) for t in bufs)
        + (jax.ShapeDtypeStruct((8, LANE), F32),),
        in_specs=[HBM_SPEC] * (2 * n) + [ANY_SPEC] * m,
        out_specs=(SEM_SPEC, SEM_SPEC) + (HBM_SPEC,) * (2 * n) + (pl.BlockSpec(memory_space=pltpu.VMEM),),
        input_output_aliases={i: 2 + i for i in range(2 * n)},
        compiler_params=SPLIT_EFFECT,
    )(*(pltpu.with_memory_space_constraint(t, pltpu.HBM) for t in bufs), *after)
    return list(res[:-1]), res[-1]


def split_wait(modes, started, after, name):
    n = len(modes)
    after = tuple(after) if isinstance(after, (tuple, list)) else (after,)
    send_sems, recv_sems, bufs = started[0], started[1], started[2:]

    def body(*refs):
        send_sems, recv_sems = refs[2 * n], refs[2 * n + 1]
        mine = _slot_of(*_my_pos())
        for k, peer, ps in _split_peers():
            for i in range(n):
                sent = _split_ends(modes[i], refs[i], refs[n + i], ps, mine)[0]
                landed = _split_ends(modes[i], refs[i], refs[n + i], mine, ps)[1]
                cp = pltpu.make_async_remote_copy(
                    src_ref=sent, dst_ref=landed, send_sem=send_sems.at[n * k + i],
                    recv_sem=recv_sems.at[n * k + i], device_id=peer, device_id_type=MESH_ID)
                cp.wait_send()
                cp.wait_recv()
        for i in range(n):
            src, dst = _split_ends(modes[i], refs[i], refs[n + i], mine, mine)
            pltpu.make_async_copy(src, dst, recv_sems.at[n * (N_DEV - 1) + i]).wait()

    res = pl.pallas_call(
        body, name=name,
        out_shape=tuple(pltpu.HBM(t.shape, t.dtype) for t in bufs),
        in_specs=[HBM_SPEC] * (2 * n) + [SEM_SPEC, SEM_SPEC] + [ANY_SPEC] * len(after),
        out_specs=(HBM_SPEC,) * (2 * n),
        input_output_aliases={i: i for i in range(2 * n)},
        compiler_params=SPLIT_EFFECT,
    )(*bufs, send_sems, recv_sems, *after)
    return list(res[n:])


def sum_parts(parts, name):
    _, R, C = parts.shape
    tr = _pick(R, (512, 336, 272, 256, 128, 64, 32, 16, 8))

    def body(p_ref, o_ref):
        acc = p_ref[0].astype(F32)
        for k in range(1, N_DEV):
            acc += p_ref[k].astype(F32)
        o_ref[...] = acc

    return pl.pallas_call(
        body, name=name, grid=(R // tr,),
        in_specs=[pl.BlockSpec((N_DEV, tr, C), lambda i: (0, i, 0))],
        out_specs=pl.BlockSpec((tr, C), lambda i: (i, 0)),
        out_shape=jax.ShapeDtypeStruct((R, C), F32),
        compiler_params=_cp("parallel"),
    )(parts)


def sum_into(parts, row0, rows, cols, layer, dst, name):
    tr = _pick(rows, (512, 352, 128))
    blk0 = row0 // tr

    def body(p_ref, *rest):
        o_ref = rest[-1]
        acc = p_ref[0, :, 0:cols].astype(F32)
        for k in range(1, N_DEV):
            acc += p_ref[k, :, 0:cols].astype(F32)
        o_ref[0] = acc

    aliased = dst is not None
    return pl.pallas_call(
        body, name=name, grid=(rows // tr,),
        in_specs=[pl.BlockSpec((N_DEV, tr, parts.shape[2]), lambda i: (0, blk0 + i, 0))]
        + [pl.BlockSpec(memory_space=pl.ANY)] * aliased,
        out_specs=pl.BlockSpec((1, tr, cols), lambda i: (layer, i, 0)),
        out_shape=jax.ShapeDtypeStruct((DEPTH, rows, cols), F32),
        input_output_aliases={1: 0} if aliased else {},
        compiler_params=_cp("parallel"),
    )(*((parts, dst) if aliased else (parts,)))


def adamw(g, w, m, v, name):
    lead, (R, C) = g.shape[:-2], g.shape[-2:]
    tr = _pick(R, (512, 352, 256, 128, 64, 32, 16, 8))
    c1 = 1.0 - ADAM_B1 ** ADAM_STEP
    c2 = 1.0 - ADAM_B2 ** ADAM_STEP

    def body(g_ref, w_ref, m_ref, v_ref, d_ref, nm_ref, nv_ref):
        gg = g_ref[...]
        mm = ADAM_B1 * m_ref[...] + (1.0 - ADAM_B1) * gg
        vv = ADAM_B2 * v_ref[...] + (1.0 - ADAM_B2) * (gg * gg)
        nm_ref[...] = mm
        nv_ref[...] = vv
        d_ref[...] = -ADAM_LR * ((mm / c1) / (jnp.sqrt(vv / c2) + ADAM_EPS) + ADAM_WD * w_ref[...])

    if lead:
        spec = pl.BlockSpec((1, tr, C), lambda l, i: (l, i, 0))
    else:
        spec = pl.BlockSpec((tr, C), lambda l, i: (i, 0))
    return pl.pallas_call(
        body, name=name, grid=(lead[0] if lead else 1, R // tr),
        in_specs=[spec] * 4, out_specs=[spec] * 3,
        out_shape=[jax.ShapeDtypeStruct(g.shape, F32)] * 3,
        compiler_params=_cp("parallel", "parallel"),
    )(g, w, m, v)


BIG = (("ffn1_w_gate", 1, D, FF), ("ffn1_w_up", 1, D, FF), ("ffn1_w_down", 0, FF, D),
       ("w_in", 1, D, D_IN), ("w_out", 0, D, D),
       ("ffn2_w_gate", 1, D, FF), ("ffn2_w_up", 1, D, FF), ("ffn2_w_down", 0, FF, D),
       ("ple_w_gate", 0, D, D), ("ple_w_proj", 1, PLE, D))
SMALL = (("ffn1_norm", (D,), None), ("mix_norm", (D,), None), ("lru_conv_w", (4, LRU_W), LRU_W // N_DEV),
         ("lru_conv_b", (LRU_W,), None), ("lru_w_a", (4, HD, HD), None), ("lru_b_a", (LRU_W,), None),
         ("lru_w_x", (4, HD, HD), None), ("lru_b_x", (LRU_W,), None), ("lru_lambda", (LRU_W,), None),
         ("attn_sinks", (ATT_H,), None), ("dn_conv_w", (4, 3 * DN_W), 3 * DN_W // N_DEV),
         ("dn_a_log", (DN_H,), None), ("dn_dt_bias", (DN_H,), None), ("dn_norm", (HD,), None),
         ("ffn2_norm", (D,), None), ("ple_norm", (D,), None))
SINGLE = (("rel_bias", (REL_BUCKETS, ATT_H)), ("final_norm", (D,)))


def _pack_rows(arrs, width, mult):
    flat = jnp.concatenate([a.reshape(-1) for a in arrs])
    rows = -(-flat.shape[0] // (width * mult)) * mult
    return jnp.pad(flat, (0, rows * width - flat.shape[0])).reshape(rows, width)


def _unpack_rows(packed, shapes):
    flat = packed.reshape(-1)
    out, off = [], 0
    for s in shapes:
        n = int(np.prod(s))
        out.append(flat[off:off + n].reshape(s))
        off += n
    return out


def _pad_rows(w, r):
    return jnp.pad(w, ((0, r - w.shape[0]), (0, 0)))


def _shard_ffn(a, l, which, more=()):
    cols = jnp.concatenate([a[which + "_w_gate"][l], a[which + "_w_up"][l]], axis=0)
    rows = jnp.concatenate([_pad_rows(a[which + "_w_down"][l], SHP)] + list(more), axis=0)
    return jnp.pad(cols, ((0, 0), (0, SHP - SH))).astype(BF16), rows.astype(BF16)


def _shards(a, l):
    w_in_rows = _pad_rows(a["w_in"][l].reshape(WIN_ROWS, D), IN_ROWS).astype(BF16)
    rest = _shard_ffn(a, l, "ffn2", (a["w_out"][l], a["ple_w_gate"][l], a["ple_w_proj"][l].reshape(-1, D)))
    return _shard_ffn(a, l, "ffn1"), (w_in_rows,), rest


def _full_w_in(in_rows):
    sh = in_rows[:, :WIN_ROWS, :].reshape(N_DEV, D, D_IN // N_DEV)
    return jnp.pad(sh.transpose(1, 0, 2).reshape(D, D_IN), ((0, 0), (0, D_IN_PAD - D_IN)))


def _full_ple_proj(r_rows):
    sh = r_rows[:, R_PPROJ:R_ROWS, :].reshape(N_DEV, PLE, D // N_DEV)
    return sh.transpose(1, 0, 2).reshape(PLE, D)


PIECE_NAMES = {"ffn1": ("ffn1_w_gate", "ffn1_w_up", "ffn1_w_down"), "ffn2": ("ffn2_w_gate", "ffn2_w_up", "ffn2_w_down"),
               "mix": ("w_out", "ple_w_gate", "w_in", "ple_w_proj")}


def _shard_grads(piece, summed):
    if piece == "mix":
        rows, = summed
        return {"w_out": rows[GM_WOUT:GM_WOUT + ROWS_DEV], "ple_w_gate": rows[GM_PGATE:GM_PGATE + ROWS_DEV],
                "w_in": rows[GM_WIN:GM_WIN + WIN_ROWS].reshape(D, D_IN // N_DEV),
                "ple_w_proj": rows[GM_PPROJ:GM_END].reshape(PLE, D // N_DEV)}
    if piece in ("ffn1_gate", "ffn1_up"):
        return {piece.replace("_", "_w_"): summed[0][:, :SH]}
    if piece == "ffn1_down":
        return {"ffn1_w_down": summed[0][:SH]}
    cols, rows = summed
    return {piece + "_w_gate": cols[:D, :SH], piece + "_w_up": cols[D:, :SH], piece + "_w_down": rows[:SH]}


def kernel(x, p, ffn1_norm, ffn1_w_gate, ffn1_w_up, ffn1_w_down, mix_norm, w_in, lru_conv_w, lru_conv_b, lru_w_a, lru_b_a, lru_w_x, lru_b_x, lru_lambda, attn_sinks, rel_bias, dn_conv_w, dn_a_log, dn_dt_bias, dn_norm, w_out, ffn2_norm, ffn2_w_gate, ffn2_w_up, ffn2_w_down, ple_norm, ple_w_gate, ple_w_proj, final_norm, loss_target, m_ffn1_norm, m_ffn1_w_gate, m_ffn1_w_up, m_ffn1_w_down, m_mix_norm, m_w_in, m_lru_conv_w, m_lru_conv_b, m_lru_w_a, m_lru_b_a, m_lru_w_x, m_lru_b_x, m_lru_lambda, m_attn_sinks, m_rel_bias, m_dn_conv_w, m_dn_a_log, m_dn_dt_bias, m_dn_norm, m_w_out, m_ffn2_norm, m_ffn2_w_gate, m_ffn2_w_up, m_ffn2_w_down, m_ple_norm, m_ple_w_gate, m_ple_w_proj, m_final_norm, v_ffn1_norm, v_ffn1_w_gate, v_ffn1_w_up, v_ffn1_w_down, v_mix_norm, v_w_in, v_lru_conv_w, v_lru_conv_b, v_lru_w_a, v_lru_b_a, v_lru_w_x, v_lru_b_x, v_lru_lambda, v_attn_sinks, v_rel_bias, v_dn_conv_w, v_dn_a_log, v_dn_dt_bias, v_dn_norm, v_w_out, v_ffn2_norm, v_ffn2_w_gate, v_ffn2_w_up, v_ffn2_w_down, v_ple_norm, v_ple_w_gate, v_ple_w_proj, v_final_norm):
    a = dict(locals())
    nb, S, _ = x.shape
    T = nb * S
    my_slot = _slot_of(*_my_pos())

    W = {k: [None] * DEPTH for k in ("f1_cols", "f1_rows", "w_in", "r_cols", "r_rows", "ple_w_proj")}
    GATHER, SCATTER = ("gather_cols", "gather_block"), ("scatter_cols", "scatter_block")
    GROUP_MODES = {"f1": GATHER, "in": GATHER[1:], "rest": GATHER}

    def set_group(l, group, bufs):
        if group == "f1":
            W["f1_cols"][l], W["f1_rows"][l] = bufs
        elif group == "in":
            W["w_in"][l] = _full_w_in(bufs[0])
        else:
            W["r_cols"][l], W["r_rows"][l] = bufs
            W["ple_w_proj"][l] = _full_ple_proj(bufs[1])

    def landing(mode, src):
        if mode == "gather_cols":
            return lax.empty((src.shape[0], FFP), src.dtype)
        if mode == "scatter_cols":
            return lax.empty((N_DEV, src.shape[0], SHP), src.dtype)
        return lax.empty((N_DEV,) + src.shape[mode == "scatter_block":], src.dtype)

    def start(modes, srcs, name, after=()):
        return split_start(modes, srcs, [landing(m, s) for m, s in zip(modes, srcs)], name, after)

    shards0, shards1 = _shards(a, 0), _shards(a, 1)
    set_group(0, "f1", gather_layer(*shards0[0], "gather_weights_l0_ffn1"))
    taps = all_gather(_pack_rows([lru_conv_w, dn_conv_w], LANE, 8), "gather_conv_taps")
    flat_taps = taps.reshape(N_DEV, -1)
    for name, first, tap in (("lru_conv_w", 0, lru_conv_w), ("dn_conv_w", lru_conv_w.size, dn_conv_w)):
        per_dev = flat_taps[:, first:first + tap.size].reshape((N_DEV,) + tap.shape)
        W[name] = jnp.moveaxis(per_dev, 0, -2).reshape(tap.shape[:-1] + (N_DEV * tap.shape[-1],))
    for name, _, cols in SMALL:
        if cols is None:
            W[name] = a[name]
    W["rel_bias"], W["final_norm"] = rel_bias, final_norm

    gathers, after = {}, (W["f1_rows"][0], taps)
    for l, group, srcs in ((0, "in", shards0[1]), (0, "rest", shards0[2]),
                           (1, "f1", shards1[0]), (1, "in", shards1[1]), (1, "rest", shards1[2])):
        gathers[l, group], token = start(GROUP_MODES[group], srcs, f"gather_start_l{l}_{group}", after)
        after = (token,)
    W["ffn1_norm"] = ffn1_norm + token[0, 0]
    flight, tokens = {}, {}

    def need(l, group, h):
        if (l, group) in gathers:
            set_group(l, group, split_wait(GROUP_MODES[group], gathers[l, group], h, f"gather_wait_l{l}_{group}"))

    def piece_modes(piece):
        return {"mix": SCATTER[1:], "ffn1_gate": SCATTER[:1], "ffn1_up": SCATTER[:1], "ffn1_down": SCATTER[1:]}.get(
            piece, SCATTER)

    def on_piece(l, piece, bufs):
        bufs = [b.reshape(-1, FFP) if b.shape[-1] == FFP else b for b in bufs]
        flight[l, piece], tokens[l, piece] = start(piece_modes(piece), bufs, f"exchange_start_l{l}_{piece}")
        return tokens[l, piece][0, 0]

    loss_local, dx, grads, d_final = _core(x.reshape(T, D), p.reshape(DEPTH, T, PLE), W,
                                           loss_target.reshape(T, D), S, need, on_piece)
    loss = lax.psum(loss_local, AXES)

    small_full = [jnp.stack([grads[l][name] for l in range(DEPTH)]) for name, _, _ in SMALL]
    small_full += [grads[0]["rel_bias"] + grads[1]["rel_bias"], d_final]
    small_flight, _ = start(("gather_block",), (_pack_rows(small_full, LANE, 8),), "gather_start_small_grads",
                            (tokens[0, "ffn1_down"],))

    out = {}

    landed = {}

    def land(l, piece, after):
        landed[l, piece] = split_wait(piece_modes(piece), flight[l, piece], after, f"exchange_wait_l{l}_{piece}")

    def where(name, l):
        if name in ("w_out", "ple_w_gate"):
            return "mix", 0, (GM_WOUT if name == "w_out" else GM_PGATE), ROWS_DEV, D
        ffn, kind = name[:4], name[7:]
        one_by_one = (l, ffn) == (0, "ffn1")
        if kind == "down":
            return (ffn + "_down", 0, 0, SH, D) if one_by_one else (ffn, 1, 0, SH, D)
        if one_by_one:
            return f"{ffn}_{kind}", 0, 0, D, SH
        return ffn, 0, (0 if kind == "gate" else D), D, SH

    def update(piece):
        for name in PIECE_NAMES[piece]:
            if name in ("w_in", "ple_w_proj"):
                g = jnp.stack([_shard_grads("mix", [mix_sums[l]])[name] for l in range(DEPTH)])
            else:
                g = None
                for l in reversed(range(DEPTH)):
                    piece_l, idx, row0, rows, cols = where(name, l)
                    g = sum_into(landed[l, piece_l][idx], row0, rows, cols, l, g, f"sum_{name}_l{l}")
            out[name] = (g,) + tuple(adamw(g, a[name], a["m_" + name], a["v_" + name], "adamw_" + name))

    for l, piece in ((1, "ffn2"), (1, "mix"), (1, "ffn1"), (0, "ffn2"), (0, "mix")):
        land(l, piece, (dx, tokens[0, "ffn1_down"]))
    mix_sums = [sum_parts(landed[l, "mix"][0], f"sum_mix_grads_l{l}") for l in range(DEPTH)]
    update("ffn2")
    update("mix")
    done_early = tuple(out[n][1] for n in PIECE_NAMES["ffn2"] + PIECE_NAMES["mix"])
    small_parts, = split_wait(("gather_block",), small_flight, done_early, "gather_wait_small_grads")
    small_sum = sum_parts(small_parts, "sum_small_grads")
    g_small = dict(zip([n for n, _, _ in SMALL] + [n for n, _ in SINGLE],
                       _unpack_rows(small_sum, [s.shape for s in small_full])))
    for name, _, cols in SMALL:
        if cols is not None:
            g_small[name] = lax.dynamic_slice_in_dim(g_small[name], my_slot * cols, cols, axis=2)

    for n in [n for n, _, _ in SMALL] + [n for n, _ in SINGLE]:
        shape = a[n].shape
        flat = lambda t: t.reshape((-1, shape[-1]) if len(shape) > 1 else (1, -1))
        res = adamw(flat(g_small[n]), flat(a[n]), flat(a["m_" + n]), flat(a["v_" + n]), "adamw_" + n)
        out[n] = (g_small[n].reshape(shape),) + tuple(r.reshape(shape) for r in res)

    for piece in ("ffn1_gate", "ffn1_up", "ffn1_down"):
        land(0, piece, (out["final_norm"][1],) + done_early)
    update("ffn1")

    order = ['ffn1_norm', 'ffn1_w_gate', 'ffn1_w_up', 'ffn1_w_down', 'mix_norm', 'w_in', 'lru_conv_w', 'lru_conv_b',
             'lru_w_a', 'lru_b_a', 'lru_w_x', 'lru_b_x', 'lru_lambda', 'attn_sinks', 'rel_bias', 'dn_conv_w',
             'dn_a_log', 'dn_dt_bias', 'dn_norm', 'w_out', 'ffn2_norm', 'ffn2_w_gate', 'ffn2_w_up', 'ffn2_w_down',
             'ple_norm', 'ple_w_gate', 'ple_w_proj', 'final_norm']
    return (loss, dx.reshape(x.shape)) + tuple(out[n][k] for k in range(4) for n in order)
```
